```python
import jax, jax.numpy as jnp
from jax import lax
import numpy as np

D_MODEL = 1024
BATCH = 8
SEQ = 2048
DEPTH = 2

CHUNK = 64
N_EVEN = (DEPTH + 1) // 2
N_ODD = DEPTH // 2
GROUP_WIDTH = D_MODEL // 2
D_FF = 4 * D_MODEL
NORM_EPS = 1e-6

GLA_HEADS = 4
GLA_DV = GROUP_WIDTH // GLA_HEADS
GLA_DK = GLA_DV // 2
GLA_KW = GLA_HEADS * GLA_DK
GLA_RANK = 16
GLA_GATE_TAU = 16.0

FOX_HEAD_DIM = 64
FOX_HEADS = GROUP_WIDTH // FOX_HEAD_DIM
FOX_BLOCK = 128

CA_HEAD_DIM = 64
CA_HEADS = GROUP_WIDTH // CA_HEAD_DIM
CA_LEFT_CHUNKS = 8
CA_BAND = (CA_LEFT_CHUNKS + 1) * CHUNK
REL_CLIP = 128

LRU_WIDTH = GROUP_WIDTH
LRU_BLOCKS = 8
LRU_BLOCK_DIM = LRU_WIDTH // LRU_BLOCKS
CONV_WIDTH = 4
LRU_C = 8.0

EVEN_SIZES = (GLA_KW, GLA_KW, GROUP_WIDTH, GROUP_WIDTH, GLA_RANK,
              GROUP_WIDTH, GROUP_WIDTH, GROUP_WIDTH, FOX_HEADS)
EVEN_IN = sum(EVEN_SIZES)
ODD_SIZES = (GROUP_WIDTH, GROUP_WIDTH, GROUP_WIDTH, LRU_WIDTH, LRU_WIDTH)
ODD_IN = sum(ODD_SIZES)

kernel_name = 'hybrid_gla_fox_chunkattn_rglru_trunk'


def _split(a, sizes):
    return jnp.split(a, [int(s) for s in np.cumsum(sizes)[:-1]], axis=-1)


def rmsnorm(x, w):
    xf = x.astype(jnp.float32)
    y = xf * lax.rsqrt(jnp.mean(xf * xf, axis=-1, keepdims=True) + NORM_EPS)
    return (y * w.astype(jnp.float32)).astype(x.dtype)


def gla_mixer(q, k, v, r, a_low, w_a_up, b_a, norm_w):
    B, T, _ = q.shape
    nc = T // CHUNK
    f32 = jnp.float32
    qc = q.reshape(B, nc, CHUNK, GLA_HEADS, GLA_DK).astype(f32) * (GLA_DK ** -0.5)
    kc = k.reshape(B, nc, CHUNK, GLA_HEADS, GLA_DK).astype(f32)
    vc = v.reshape(B, nc, CHUNK, GLA_HEADS, GLA_DV).astype(f32)
    log_a = jax.nn.log_sigmoid((a_low @ w_a_up + b_a).astype(f32)) / GLA_GATE_TAU
    log_a = log_a.reshape(B, nc, CHUNK, GLA_HEADS, GLA_DK)
    cum = jnp.cumsum(log_a, axis=2)
    total = cum[:, :, -1]
    k_dec = kc * jnp.exp(total[:, :, None] - cum)
    inc = jnp.einsum('bcshk,bcshv->bchkv', k_dec, vc)

    def step(state, inp):
        dec, add = inp
        state = dec[..., None] * state + add
        return state, state

    init = jnp.zeros((B, GLA_HEADS, GLA_DK, GLA_DV), f32)
    _, states = lax.scan(step, init, (jnp.moveaxis(jnp.exp(total), 1, 0), jnp.moveaxis(inc, 1, 0)))
    states = jnp.moveaxis(states, 0, 1)
    o = jnp.einsum('bcthk,bchkv->bcthv', qc, states).reshape(B, T, GLA_HEADS, GLA_DV)
    o = o * lax.rsqrt(jnp.mean(o * o, axis=-1, keepdims=True) + NORM_EPS)
    o = o.reshape(B, T, GROUP_WIDTH) * norm_w.astype(f32)
    return (o * jax.nn.silu(r.astype(f32))).astype(q.dtype)


def fox_mixer(q, k, v, f_logit):
    B, T, _ = q.shape
    f32 = jnp.float32
    qh = q.reshape(B, T, FOX_HEADS, FOX_HEAD_DIM)
    kh = k.reshape(B, T, FOX_HEADS, FOX_HEAD_DIM)
    vh = v.reshape(B, T, FOX_HEADS, FOX_HEAD_DIM)
    cum = jnp.cumsum(jax.nn.log_sigmoid(f_logit.astype(f32)), axis=1).transpose(0, 2, 1)
    scale = FOX_HEAD_DIM ** -0.5
    neg = jnp.finfo(f32).min
    outs = []
    for blk in range(T // FOX_BLOCK):
        q0 = blk * FOX_BLOCK
        q1 = q0 + FOX_BLOCK
        s = jnp.einsum('bqhd,bkhd->bhqk', qh[:, q0:q1], kh[:, :q1]).astype(f32) * scale
        s = s + (cum[:, :, q0:q1, None] - cum[:, :, None, :q1])
        mask = (q0 + jnp.arange(FOX_BLOCK))[:, None] >= jnp.arange(q1)[None, :]
        p = jax.nn.softmax(jnp.where(mask, s, neg), axis=-1)
        outs.append(jnp.einsum('bhqk,bkhd->bqhd', p.astype(v.dtype), vh[:, :q1]))
    return jnp.concatenate(outs, axis=1).reshape(B, T, GROUP_WIDTH)


def chunk_rel_attention(q, k, v, rel_bias):
    B, T, _ = q.shape
    nc = T // CHUNK
    f32 = jnp.float32
    pad = CA_LEFT_CHUNKS * CHUNK
    qc = q.reshape(B, nc, CHUNK, CA_HEADS, CA_HEAD_DIM)
    kp = jnp.pad(k, ((0, 0), (pad, 0), (0, 0))).reshape(B, nc + CA_LEFT_CHUNKS, CHUNK, CA_HEADS, CA_HEAD_DIM)
    vp = jnp.pad(v, ((0, 0), (pad, 0), (0, 0))).reshape(B, nc + CA_LEFT_CHUNKS, CHUNK, CA_HEADS, CA_HEAD_DIM)
    k_band = jnp.concatenate([kp[:, j:j + nc] for j in range(CA_LEFT_CHUNKS + 1)], axis=2)
    v_band = jnp.concatenate([vp[:, j:j + nc] for j in range(CA_LEFT_CHUNKS + 1)], axis=2)
    s = jnp.einsum('bcqhd,bckhd->bchqk', qc, k_band).astype(f32) * (CA_HEAD_DIM ** -0.5)
    qi = jnp.arange(CHUNK)
    kj = jnp.arange(CA_BAND)
    rel = jnp.clip(pad + qi[:, None] - kj[None, :], -REL_CLIP, REL_CLIP) + REL_CLIP
    bias = rel_bias.astype(f32)[:, rel]
    key_pos = jnp.arange(nc)[:, None] * CHUNK - pad + kj[None, :]
    valid = (key_pos >= 0)[None, :, None, None, :]
    s = jnp.where(valid, s + bias[None, None], jnp.finfo(f32).min)
    p = jax.nn.softmax(s, axis=-1)
    o = jnp.einsum('bchqk,bckhd->bcqhd', p.astype(v.dtype), v_band)
    return o.reshape(B, T, GROUP_WIDTH)


def rglru_mixer(gate_in, x_in, conv_w, conv_b, w_a, b_a, w_x, b_x, lam):
    B, T, W = x_in.shape
    f32 = jnp.float32
    xc = lax.conv_general_dilated(x_in, conv_w[:, None, :], window_strides=(1,),
                                  padding=[(CONV_WIDTH - 1, 0)],
                                  dimension_numbers=('NWC', 'WIO', 'NWC'),
                                  feature_group_count=W) + conv_b
    xb = xc.reshape(B, T, LRU_BLOCKS, LRU_BLOCK_DIM)
    r = jax.nn.sigmoid((jnp.einsum('btnd,nde->btne', xb, w_a).reshape(B, T, W) + b_a).astype(f32))
    i = jax.nn.sigmoid((jnp.einsum('btnd,nde->btne', xb, w_x).reshape(B, T, W) + b_x).astype(f32))
    log_a = LRU_C * r * jax.nn.log_sigmoid(lam.astype(f32))
    a = jnp.exp(log_a)
    b = jnp.sqrt(-jnp.expm1(2.0 * log_a)) * (i * xc.astype(f32))

    def combine(left, right):
        a_l, b_l = left
        a_r, b_r = right
        return a_l * a_r, a_r * b_l + b_r

    _, h = lax.associative_scan(combine, (a, b), axis=1)
    return (h * jax.nn.gelu(gate_in.astype(f32))).astype(x_in.dtype)


def sq_relu_mlp(h, w_up, w_down):
    return jnp.square(jax.nn.relu(h @ w_up)) @ w_down


def _fwd_setup_inputs(seed: int = 0) -> dict:
    key = jax.random.key(seed)
    ks = jax.random.split(key, 24)
    nrm = jax.random.normal
    f32 = jnp.float32
    x = nrm(ks[0], (BATCH, SEQ, D_MODEL), f32)
    norm_w = 1.0 + 0.05 * nrm(ks[1], (DEPTH, 4, D_MODEL), f32)
    w_in_even = nrm(ks[2], (N_EVEN, D_MODEL, EVEN_IN), f32) * D_MODEL ** -0.5
    gla_w_a_up = nrm(ks[3], (N_EVEN, GLA_RANK, GLA_KW), f32) * GLA_RANK ** -0.5
    gla_b_a = jax.random.uniform(ks[4], (N_EVEN, GLA_KW), f32, 0.0, 2.0)
    gla_norm_w = 1.0 + 0.05 * nrm(ks[5], (N_EVEN, GROUP_WIDTH), f32)
    fox_b_f = jax.random.uniform(ks[6], (N_EVEN, FOX_HEADS), f32, 1.0, 4.0)
    w_out_even = nrm(ks[7], (N_EVEN, 2 * GROUP_WIDTH, D_MODEL), f32) * (2 * GROUP_WIDTH) ** -0.5
    w_in_odd = nrm(ks[8], (N_ODD, D_MODEL, ODD_IN), f32) * D_MODEL ** -0.5
    rel_bias = 0.5 * nrm(ks[9], (N_ODD, CA_HEADS, 2 * REL_CLIP + 1), f32)
    conv_w = nrm(ks[10], (N_ODD, CONV_WIDTH, LRU_WIDTH), f32) * CONV_WIDTH ** -0.5
    conv_b = 0.01 * nrm(ks[11], (N_ODD, LRU_WIDTH), f32)
    lru_w_a = nrm(ks[12], (N_ODD, LRU_BLOCKS, LRU_BLOCK_DIM, LRU_BLOCK_DIM), f32) * LRU_BLOCK_DIM ** -0.5
    lru_b_a = 0.1 * nrm(ks[13], (N_ODD, LRU_WIDTH), f32)
    lru_w_x = nrm(ks[14], (N_ODD, LRU_BLOCKS, LRU_BLOCK_DIM, LRU_BLOCK_DIM), f32) * LRU_BLOCK_DIM ** -0.5
    lru_b_x = 0.1 * nrm(ks[15], (N_ODD, LRU_WIDTH), f32)
    a_c = jax.random.uniform(ks[16], (N_ODD, LRU_WIDTH), f32, 0.9, 0.999)
    a_base = a_c ** (1.0 / LRU_C)
    lru_lambda = jnp.log(a_base) - jnp.log1p(-a_base)
    w_out_odd = nrm(ks[17], (N_ODD, 2 * GROUP_WIDTH, D_MODEL), f32) * (2 * GROUP_WIDTH) ** -0.5
    w_mlp_up = nrm(ks[18], (DEPTH, D_MODEL, D_FF), f32) * D_MODEL ** -0.5
    w_mlp_down = nrm(ks[19], (DEPTH, D_FF, D_MODEL), f32) * D_FF ** -0.5
    return {'x': x, 'norm_w': norm_w, 'w_in_even': w_in_even, 'gla_w_a_up': gla_w_a_up,
            'gla_b_a': gla_b_a, 'gla_norm_w': gla_norm_w, 'fox_b_f': fox_b_f,
            'w_out_even': w_out_even, 'w_in_odd': w_in_odd, 'rel_bias': rel_bias,
            'conv_w': conv_w, 'conv_b': conv_b, 'lru_w_a': lru_w_a, 'lru_b_a': lru_b_a,
            'lru_w_x': lru_w_x, 'lru_b_x': lru_b_x, 'lru_lambda': lru_lambda,
            'w_out_odd': w_out_odd, 'w_mlp_up': w_mlp_up, 'w_mlp_down': w_mlp_down}


def _fwd_reference(x, norm_w, w_in_even, gla_w_a_up, gla_b_a, gla_norm_w, fox_b_f, w_out_even,
              w_in_odd, rel_bias, conv_w, conv_b, lru_w_a, lru_b_a, lru_w_x, lru_b_x,
              lru_lambda, w_out_odd, w_mlp_up, w_mlp_down):
    for layer in range(DEPTH):
        j = layer // 2
        h = rmsnorm(x, norm_w[layer, 0])
        if layer % 2 == 0:
            proj = h @ w_in_even[j]
            g_q, g_k, g_v, g_r, g_a, f_q, f_k, f_v, f_f = _split(proj, EVEN_SIZES)
            out_a = gla_mixer(g_q, g_k, g_v, g_r, g_a, gla_w_a_up[j], gla_b_a[j], gla_norm_w[j])
            out_b = fox_mixer(f_q, f_k, f_v, f_f + fox_b_f[j])
            mix = jnp.concatenate([out_a, out_b], axis=-1) @ w_out_even[j]
        else:
            proj = h @ w_in_odd[j]
            c_q, c_k, c_v, d_gate, d_in = _split(proj, ODD_SIZES)
            out_c = chunk_rel_attention(c_q, c_k, c_v, rel_bias[j])
            out_d = rglru_mixer(d_gate, d_in, conv_w[j], conv_b[j], lru_w_a[j], lru_b_a[j],
                                lru_w_x[j], lru_b_x[j], lru_lambda[j])
            mix = jnp.concatenate([out_c, out_d], axis=-1) @ w_out_odd[j]
        x = x + rmsnorm(mix, norm_w[layer, 1])
        h = rmsnorm(x, norm_w[layer, 2])
        x = x + rmsnorm(sq_relu_mlp(h, w_mlp_up[layer], w_mlp_down[layer]), norm_w[layer, 3])
    return x


import jax as _jax
import jax.numpy as _jnp

TWIN_FORMAT = 'train_step'
FWD_PARAMS = ['x', 'norm_w', 'w_in_even', 'gla_w_a_up', 'gla_b_a', 'gla_norm_w', 'fox_b_f', 'w_out_even', 'w_in_odd', 'rel_bias', 'conv_w', 'conv_b', 'lru_w_a', 'lru_b_a', 'lru_w_x', 'lru_b_x', 'lru_lambda', 'w_out_odd', 'w_mlp_up', 'w_mlp_down']
TWIN_WEIGHTS = ['norm_w', 'w_in_even', 'gla_w_a_up', 'gla_b_a', 'gla_norm_w', 'fox_b_f', 'w_out_even', 'w_in_odd', 'rel_bias', 'conv_w', 'conv_b', 'lru_w_a', 'lru_b_a', 'lru_w_x', 'lru_b_x', 'lru_lambda', 'w_out_odd', 'w_mlp_up', 'w_mlp_down']
TWIN_DIFF_INPUT = 'x'
TWIN_INPUTS = ['x', 'norm_w', 'w_in_even', 'gla_w_a_up', 'gla_b_a', 'gla_norm_w', 'fox_b_f', 'w_out_even', 'w_in_odd', 'rel_bias', 'conv_w', 'conv_b', 'lru_w_a', 'lru_b_a', 'lru_w_x', 'lru_b_x', 'lru_lambda', 'w_out_odd', 'w_mlp_up', 'w_mlp_down', 'loss_target', 'm_norm_w', 'm_w_in_even', 'm_gla_w_a_up', 'm_gla_b_a', 'm_gla_norm_w', 'm_fox_b_f', 'm_w_out_even', 'm_w_in_odd', 'm_rel_bias', 'm_conv_w', 'm_conv_b', 'm_lru_w_a', 'm_lru_b_a', 'm_lru_w_x', 'm_lru_b_x', 'm_lru_lambda', 'm_w_out_odd', 'm_w_mlp_up', 'm_w_mlp_down', 'v_norm_w', 'v_w_in_even', 'v_gla_w_a_up', 'v_gla_b_a', 'v_gla_norm_w', 'v_fox_b_f', 'v_w_out_even', 'v_w_in_odd', 'v_rel_bias', 'v_conv_w', 'v_conv_b', 'v_lru_w_a', 'v_lru_b_a', 'v_lru_w_x', 'v_lru_b_x', 'v_lru_lambda', 'v_w_out_odd', 'v_w_mlp_up', 'v_w_mlp_down']
TWIN_OUTPUTS = ['loss', 'grad_x', 'grad_norm_w', 'grad_w_in_even', 'grad_gla_w_a_up', 'grad_gla_b_a', 'grad_gla_norm_w', 'grad_fox_b_f', 'grad_w_out_even', 'grad_w_in_odd', 'grad_rel_bias', 'grad_conv_w', 'grad_conv_b', 'grad_lru_w_a', 'grad_lru_b_a', 'grad_lru_w_x', 'grad_lru_b_x', 'grad_lru_lambda', 'grad_w_out_odd', 'grad_w_mlp_up', 'grad_w_mlp_down', 'delta_norm_w', 'delta_w_in_even', 'delta_gla_w_a_up', 'delta_gla_b_a', 'delta_gla_norm_w', 'delta_fox_b_f', 'delta_w_out_even', 'delta_w_in_odd', 'delta_rel_bias', 'delta_conv_w', 'delta_conv_b', 'delta_lru_w_a', 'delta_lru_b_a', 'delta_lru_w_x', 'delta_lru_b_x', 'delta_lru_lambda', 'delta_w_out_odd', 'delta_w_mlp_up', 'delta_w_mlp_down', 'new_m_norm_w', 'new_m_w_in_even', 'new_m_gla_w_a_up', 'new_m_gla_b_a', 'new_m_gla_norm_w', 'new_m_fox_b_f', 'new_m_w_out_even', 'new_m_w_in_odd', 'new_m_rel_bias', 'new_m_conv_w', 'new_m_conv_b', 'new_m_lru_w_a', 'new_m_lru_b_a', 'new_m_lru_w_x', 'new_m_lru_b_x', 'new_m_lru_lambda', 'new_m_w_out_odd', 'new_m_w_mlp_up', 'new_m_w_mlp_down', 'new_v_norm_w', 'new_v_w_in_even', 'new_v_gla_w_a_up', 'new_v_gla_b_a', 'new_v_gla_norm_w', 'new_v_fox_b_f', 'new_v_w_out_even', 'new_v_w_in_odd', 'new_v_rel_bias', 'new_v_conv_w', 'new_v_conv_b', 'new_v_lru_w_a', 'new_v_lru_b_a', 'new_v_lru_w_x', 'new_v_lru_b_x', 'new_v_lru_lambda', 'new_v_w_out_odd', 'new_v_w_mlp_up', 'new_v_w_mlp_down']
TWIN_LEAF_KINDS = {'loss': 'loss', 'grad_x': 'grad_x', 'grad_norm_w': 'grad_w', 'grad_w_in_even': 'grad_w', 'grad_gla_w_a_up': 'grad_w', 'grad_gla_b_a': 'grad_w', 'grad_gla_norm_w': 'grad_w', 'grad_fox_b_f': 'grad_w', 'grad_w_out_even': 'grad_w', 'grad_w_in_odd': 'grad_w', 'grad_rel_bias': 'grad_w', 'grad_conv_w': 'grad_w', 'grad_conv_b': 'grad_w', 'grad_lru_w_a': 'grad_w', 'grad_lru_b_a': 'grad_w', 'grad_lru_w_x': 'grad_w', 'grad_lru_b_x': 'grad_w', 'grad_lru_lambda': 'grad_w', 'grad_w_out_odd': 'grad_w', 'grad_w_mlp_up': 'grad_w', 'grad_w_mlp_down': 'grad_w', 'delta_norm_w': 'delta_w', 'delta_w_in_even': 'delta_w', 'delta_gla_w_a_up': 'delta_w', 'delta_gla_b_a': 'delta_w', 'delta_gla_norm_w': 'delta_w', 'delta_fox_b_f': 'delta_w', 'delta_w_out_even': 'delta_w', 'delta_w_in_odd': 'delta_w', 'delta_rel_bias': 'delta_w', 'delta_conv_w': 'delta_w', 'delta_conv_b': 'delta_w', 'delta_lru_w_a': 'delta_w', 'delta_lru_b_a': 'delta_w', 'delta_lru_w_x': 'delta_w', 'delta_lru_b_x': 'delta_w', 'delta_lru_lambda': 'delta_w', 'delta_w_out_odd': 'delta_w', 'delta_w_mlp_up': 'delta_w', 'delta_w_mlp_down': 'delta_w', 'new_m_norm_w': 'new_m', 'new_m_w_in_even': 'new_m', 'new_m_gla_w_a_up': 'new_m', 'new_m_gla_b_a': 'new_m', 'new_m_gla_norm_w': 'new_m', 'new_m_fox_b_f': 'new_m', 'new_m_w_out_even': 'new_m', 'new_m_w_in_odd': 'new_m', 'new_m_rel_bias': 'new_m', 'new_m_conv_w': 'new_m', 'new_m_conv_b': 'new_m', 'new_m_lru_w_a': 'new_m', 'new_m_lru_b_a': 'new_m', 'new_m_lru_w_x': 'new_m', 'new_m_lru_b_x': 'new_m', 'new_m_lru_lambda': 'new_m', 'new_m_w_out_odd': 'new_m', 'new_m_w_mlp_up': 'new_m', 'new_m_w_mlp_down': 'new_m', 'new_v_norm_w': 'new_v', 'new_v_w_in_even': 'new_v', 'new_v_gla_w_a_up': 'new_v', 'new_v_gla_b_a': 'new_v', 'new_v_gla_norm_w': 'new_v', 'new_v_fox_b_f': 'new_v', 'new_v_w_out_even': 'new_v', 'new_v_w_in_odd': 'new_v', 'new_v_rel_bias': 'new_v', 'new_v_conv_w': 'new_v', 'new_v_conv_b': 'new_v', 'new_v_lru_w_a': 'new_v', 'new_v_lru_b_a': 'new_v', 'new_v_lru_w_x': 'new_v', 'new_v_lru_b_x': 'new_v', 'new_v_lru_lambda': 'new_v', 'new_v_w_out_odd': 'new_v', 'new_v_w_mlp_up': 'new_v', 'new_v_w_mlp_down': 'new_v'}


def _forward(args):
    return _fwd_reference(*[args[k] for k in FWD_PARAMS])


def _output_shape():
    out = _jax.eval_shape(lambda: _forward(_fwd_setup_inputs(0)))
    return out.shape, out.dtype

N_MICROBATCH = 1
ADAM_LR = 0.001
ADAM_B1 = 0.9
ADAM_B2 = 0.999
ADAM_EPS = 1e-08
ADAM_WD = 0.01
ADAM_STEP = 10
PER_EXAMPLE_BATCH_AXIS = {'x': 0, 'loss_target': 0}
SHARED_INPUTS = []
_WEIGHT_DTYPES = {'norm_w': _jnp.float32, 'w_in_even': _jnp.float32, 'gla_w_a_up': _jnp.float32, 'gla_b_a': _jnp.float32, 'gla_norm_w': _jnp.float32, 'fox_b_f': _jnp.float32, 'w_out_even': _jnp.float32, 'w_in_odd': _jnp.float32, 'rel_bias': _jnp.float32, 'conv_w': _jnp.float32, 'conv_b': _jnp.float32, 'lru_w_a': _jnp.float32, 'lru_b_a': _jnp.float32, 'lru_w_x': _jnp.float32, 'lru_b_x': _jnp.float32, 'lru_lambda': _jnp.float32, 'w_out_odd': _jnp.float32, 'w_mlp_up': _jnp.float32, 'w_mlp_down': _jnp.float32}
MOMENT_SCALE = {'norm_w': 1.210168e+01, 'w_in_even': 6.243704e-01, 'gla_w_a_up': 1.356324e-01, 'gla_b_a': 6.926683e-01, 'gla_norm_w': 1.285305e+00, 'fox_b_f': 2.247212e+00, 'w_out_even': 7.793762e-01, 'w_in_odd': 3.517211e+00, 'rel_bias': 5.233232e-02, 'conv_w': 6.975562e+00, 'conv_b': 2.534843e+01, 'lru_w_a': 8.503149e-01, 'lru_b_a': 6.411991e-01, 'lru_w_x': 1.639309e+00, 'lru_b_x': 2.245190e+00, 'lru_lambda': 1.268988e+00, 'w_out_odd': 4.793870e+00, 'w_mlp_up': 7.815453e-01, 'w_mlp_down': 5.940915e+00}


def _to_microbatches(a, axis):
    t = _jnp.moveaxis(a, axis, 0)
    t = t.reshape((N_MICROBATCH, t.shape[0] // N_MICROBATCH) + t.shape[1:])
    return _jnp.moveaxis(t, 1, axis + 1)


def setup_inputs(seed: int = 0) -> dict:
    inp = _fwd_setup_inputs(seed)
    key = _jax.random.fold_in(_jax.random.key(seed), 7919)
    shape, _ = _output_shape()
    out = dict(inp)
    out["loss_target"] = _jax.random.normal(_jax.random.fold_in(key, 0), shape, _jnp.float32)
    for i, name in enumerate(TWIN_WEIGHTS):
        w = inp[name].astype(_jnp.float32)
        if MOMENT_SCALE is None:
            s = _jnp.sqrt(_jnp.mean(_jnp.square(w)) + 1e-30)
        else:
            s = MOMENT_SCALE[name]
        km, kv = _jax.random.split(_jax.random.fold_in(key, i + 1))
        out[name] = w
        out["m_" + name] = s * _jax.random.normal(km, w.shape, _jnp.float32)
        out["v_" + name] = (s * s) * _jax.random.uniform(kv, w.shape, _jnp.float32, 0.5, 1.5)
    if N_MICROBATCH > 1:
        for name, axis in PER_EXAMPLE_BATCH_AXIS.items():
            out[name] = _to_microbatches(out[name], axis)
    return {'x': out['x'], 'norm_w': out['norm_w'], 'w_in_even': out['w_in_even'], 'gla_w_a_up': out['gla_w_a_up'], 'gla_b_a': out['gla_b_a'], 'gla_norm_w': out['gla_norm_w'], 'fox_b_f': out['fox_b_f'], 'w_out_even': out['w_out_even'], 'w_in_odd': out['w_in_odd'], 'rel_bias': out['rel_bias'], 'conv_w': out['conv_w'], 'conv_b': out['conv_b'], 'lru_w_a': out['lru_w_a'], 'lru_b_a': out['lru_b_a'], 'lru_w_x': out['lru_w_x'], 'lru_b_x': out['lru_b_x'], 'lru_lambda': out['lru_lambda'], 'w_out_odd': out['w_out_odd'], 'w_mlp_up': out['w_mlp_up'], 'w_mlp_down': out['w_mlp_down'], 'loss_target': out['loss_target'], 'm_norm_w': out['m_norm_w'], 'm_w_in_even': out['m_w_in_even'], 'm_gla_w_a_up': out['m_gla_w_a_up'], 'm_gla_b_a': out['m_gla_b_a'], 'm_gla_norm_w': out['m_gla_norm_w'], 'm_fox_b_f': out['m_fox_b_f'], 'm_w_out_even': out['m_w_out_even'], 'm_w_in_odd': out['m_w_in_odd'], 'm_rel_bias': out['m_rel_bias'], 'm_conv_w': out['m_conv_w'], 'm_conv_b': out['m_conv_b'], 'm_lru_w_a': out['m_lru_w_a'], 'm_lru_b_a': out['m_lru_b_a'], 'm_lru_w_x': out['m_lru_w_x'], 'm_lru_b_x': out['m_lru_b_x'], 'm_lru_lambda': out['m_lru_lambda'], 'm_w_out_odd': out['m_w_out_odd'], 'm_w_mlp_up': out['m_w_mlp_up'], 'm_w_mlp_down': out['m_w_mlp_down'], 'v_norm_w': out['v_norm_w'], 'v_w_in_even': out['v_w_in_even'], 'v_gla_w_a_up': out['v_gla_w_a_up'], 'v_gla_b_a': out['v_gla_b_a'], 'v_gla_norm_w': out['v_gla_norm_w'], 'v_fox_b_f': out['v_fox_b_f'], 'v_w_out_even': out['v_w_out_even'], 'v_w_in_odd': out['v_w_in_odd'], 'v_rel_bias': out['v_rel_bias'], 'v_conv_w': out['v_conv_w'], 'v_conv_b': out['v_conv_b'], 'v_lru_w_a': out['v_lru_w_a'], 'v_lru_b_a': out['v_lru_b_a'], 'v_lru_w_x': out['v_lru_w_x'], 'v_lru_b_x': out['v_lru_b_x'], 'v_lru_lambda': out['v_lru_lambda'], 'v_w_out_odd': out['v_w_out_odd'], 'v_w_mlp_up': out['v_w_mlp_up'], 'v_w_mlp_down': out['v_w_mlp_down']}


def _loss(weights, diff, rest, loss_target):
    with _jax.named_scope("forward"):
        args = {**rest, TWIN_DIFF_INPUT: diff, **{k: w.astype(_WEIGHT_DTYPES[k]) for k, w in weights.items()}}
        y = _forward(args)
    with _jax.named_scope("loss_head"):
        err = _jnp.square(y.astype(_jnp.float32) - loss_target)
        return 0.5 * _jnp.sum(_jnp.mean(err, axis=-1)) if err.ndim else 0.5 * err


def _adamw(w, g, m, v):
    m = ADAM_B1 * m + (1.0 - ADAM_B1) * g
    v = ADAM_B2 * v + (1.0 - ADAM_B2) * _jnp.square(g)
    m_hat = m / (1.0 - ADAM_B1 ** ADAM_STEP)
    v_hat = v / (1.0 - ADAM_B2 ** ADAM_STEP)
    delta = -ADAM_LR * (m_hat / (_jnp.sqrt(v_hat) + ADAM_EPS) + ADAM_WD * w)
    return delta, m, v


def reference(x, norm_w, w_in_even, gla_w_a_up, gla_b_a, gla_norm_w, fox_b_f, w_out_even, w_in_odd, rel_bias, conv_w, conv_b, lru_w_a, lru_b_a, lru_w_x, lru_b_x, lru_lambda, w_out_odd, w_mlp_up, w_mlp_down, loss_target, m_norm_w, m_w_in_even, m_gla_w_a_up, m_gla_b_a, m_gla_norm_w, m_fox_b_f, m_w_out_even, m_w_in_odd, m_rel_bias, m_conv_w, m_conv_b, m_lru_w_a, m_lru_b_a, m_lru_w_x, m_lru_b_x, m_lru_lambda, m_w_out_odd, m_w_mlp_up, m_w_mlp_down, v_norm_w, v_w_in_even, v_gla_w_a_up, v_gla_b_a, v_gla_norm_w, v_fox_b_f, v_w_out_even, v_w_in_odd, v_rel_bias, v_conv_w, v_conv_b, v_lru_w_a, v_lru_b_a, v_lru_w_x, v_lru_b_x, v_lru_lambda, v_w_out_odd, v_w_mlp_up, v_w_mlp_down):
    given = dict(x=x, norm_w=norm_w, w_in_even=w_in_even, gla_w_a_up=gla_w_a_up, gla_b_a=gla_b_a, gla_norm_w=gla_norm_w, fox_b_f=fox_b_f, w_out_even=w_out_even, w_in_odd=w_in_odd, rel_bias=rel_bias, conv_w=conv_w, conv_b=conv_b, lru_w_a=lru_w_a, lru_b_a=lru_b_a, lru_w_x=lru_w_x, lru_b_x=lru_b_x, lru_lambda=lru_lambda, w_out_odd=w_out_odd, w_mlp_up=w_mlp_up, w_mlp_down=w_mlp_down, loss_target=loss_target, m_norm_w=m_norm_w, m_w_in_even=m_w_in_even, m_gla_w_a_up=m_gla_w_a_up, m_gla_b_a=m_gla_b_a, m_gla_norm_w=m_gla_norm_w, m_fox_b_f=m_fox_b_f, m_w_out_even=m_w_out_even, m_w_in_odd=m_w_in_odd, m_rel_bias=m_rel_bias, m_conv_w=m_conv_w, m_conv_b=m_conv_b, m_lru_w_a=m_lru_w_a, m_lru_b_a=m_lru_b_a, m_lru_w_x=m_lru_w_x, m_lru_b_x=m_lru_b_x, m_lru_lambda=m_lru_lambda, m_w_out_odd=m_w_out_odd, m_w_mlp_up=m_w_mlp_up, m_w_mlp_down=m_w_mlp_down, v_norm_w=v_norm_w, v_w_in_even=v_w_in_even, v_gla_w_a_up=v_gla_w_a_up, v_gla_b_a=v_gla_b_a, v_gla_norm_w=v_gla_norm_w, v_fox_b_f=v_fox_b_f, v_w_out_even=v_w_out_even, v_w_in_odd=v_w_in_odd, v_rel_bias=v_rel_bias, v_conv_w=v_conv_w, v_conv_b=v_conv_b, v_lru_w_a=v_lru_w_a, v_lru_b_a=v_lru_b_a, v_lru_w_x=v_lru_w_x, v_lru_b_x=v_lru_b_x, v_lru_lambda=v_lru_lambda, v_w_out_odd=v_w_out_odd, v_w_mlp_up=v_w_mlp_up, v_w_mlp_down=v_w_mlp_down)
    weights = {n: given[n] for n in TWIN_WEIGHTS}
    shared = {n: given[n] for n in SHARED_INPUTS}
    per_example = {n: given[n] for n in ['x']}
    grad_fn = _jax.value_and_grad(_loss, argnums=(0, 1))

    def one_microbatch(ex, loss_target):
        ex = dict(ex)
        diff = ex.pop(TWIN_DIFF_INPUT)
        return grad_fn(weights, diff, {**shared, **ex}, loss_target)

    if N_MICROBATCH == 1:
        loss, (grad_w, grad_x) = one_microbatch(per_example, given["loss_target"])
    else:
        def body(carry, xs):
            loss_sum, grad_sum = carry
            l_k, (gw_k, gx_k) = one_microbatch(xs[0], xs[1])
            with _jax.named_scope("update"):
                return (loss_sum + l_k, _jax.tree.map(_jnp.add, grad_sum, gw_k)), gx_k

        init = (_jnp.zeros((), _jnp.float32), _jax.tree.map(_jnp.zeros_like, weights))
        (loss, grad_w), grad_x = _jax.lax.scan(body, init, (per_example, given["loss_target"]))
    with _jax.named_scope("update"):
        delta_w, new_m, new_v = {}, {}, {}
        for n in TWIN_WEIGHTS:
            delta_w[n], new_m[n], new_v[n] = _adamw(weights[n], grad_w[n], given["m_" + n], given["v_" + n])
    return (loss, grad_x, *[grad_w[n] for n in TWIN_WEIGHTS], *[delta_w[n] for n in TWIN_WEIGHTS],
            *[new_m[n] for n in TWIN_WEIGHTS], *[new_v[n] for n in TWIN_WEIGHTS])
```

```python
import functools

import jax
import jax.numpy as jnp
from jax import lax
from jax.experimental import pallas as pl
from jax.experimental.pallas import tpu as pltpu

F32 = jnp.float32
BF16 = jnp.bfloat16
MESH = pl.DeviceIdType.MESH

T = 2048
D = 1024
DFF = 4096
EPS = 1e-6
CHUNK = 64
NCHUNK = T // CHUNK
PE = 3200
PO = 2560
AUX_BLK = 3072 // 128
FOX_LANE0 = 16
GLA_SCALE = 64 ** -0.5
ATT_SCALE = 64 ** -0.5
NEG = float(jnp.finfo(jnp.float32).min)
CA_BAND = 576
CA_PAD = 512
REL_PAD = 384

VMEM_LIMIT = 48 * 1024 * 1024

ADAM_LR, ADAM_B1, ADAM_B2, ADAM_EPS, ADAM_WD, ADAM_STEP = 0.001, 0.9, 0.999, 1e-08, 0.01, 10

COMM_C = 1024
COMM_R = 6144
COMM_RH = COMM_R // 2

_DIMS = {"nn": (((1,), (0,)), ((), ())), "nt": (((1,), (1,)), ((), ())), "tn": (((0,), (0,)), ((), ()))}


def _cp(sem, **kw):
    return pltpu.CompilerParams(dimension_semantics=sem, vmem_limit_bytes=VMEM_LIMIT, **kw)


def _dot(a, b, mode):
    return lax.dot_general(a.astype(BF16), b.astype(BF16), _DIMS[mode], preferred_element_type=F32)


@functools.partial(jax.custom_vjp, nondiff_argnums=(2,))
def bdot(a, b, mode):
    return _dot(a, b, mode)


def _bdot_fwd(a, b, mode):
    return _dot(a, b, mode), (a, b)


def _bdot_bwd(mode, res, g):
    a, b = res
    if mode == "nn":
        da, db = _dot(g, b, "nt"), _dot(a, g, "tn")
    elif mode == "nt":
        da, db = _dot(g, b, "nn"), _dot(g, a, "tn")
    else:
        da, db = _dot(b, g, "nt"), _dot(a, g, "nn")
    return da.astype(a.dtype), db.astype(b.dtype)


bdot.defvjp(_bdot_fwd, _bdot_bwd)


def _hdot_raw(a, b, mode):
    return lax.dot_general(a, b, _DIMS[mode], precision=lax.Precision.HIGHEST, preferred_element_type=F32)


@functools.partial(jax.custom_vjp, nondiff_argnums=(2,))
def hdot(a, b, mode):
    return _hdot_raw(a, b, mode)


def _hdot_fwd(a, b, mode):
    return _hdot_raw(a, b, mode), (a, b)


def _hdot_bwd(mode, res, g):
    a, b = res
    if mode == "nn":
        return _hdot_raw(g, b, "nt"), _hdot_raw(a, g, "tn")
    if mode == "nt":
        return _hdot_raw(g, b, "nn"), _hdot_raw(g, a, "tn")
    return _hdot_raw(b, g, "nt"), _hdot_raw(a, g, "nn")


hdot.defvjp(_hdot_fwd, _hdot_bwd)


def _log_sigmoid(x):
    return jnp.minimum(x, 0.0) - jnp.log(1.0 + jnp.exp(-jnp.abs(x)))


def _sigmoid(x):
    return 1.0 / (1.0 + jnp.exp(-x))


def _expm1(x):
    series = x * (1.0 + x * 0.5 * (1.0 + x * (1.0 / 3.0) * (1.0 + x * 0.25)))
    return jnp.where(jnp.abs(x) < 0.03, series, jnp.exp(x) - 1.0)


def _gelu_tanh(x):
    return 0.5 * x * (1.0 + jnp.tanh(0.7978845608028654 * (x + 0.044715 * x * x * x)))


def _softmax_rows(s):
    m = jnp.max(s, axis=-1, keepdims=True)
    p = jnp.exp(s - m)
    return p / jnp.sum(p, axis=-1, keepdims=True)


def _iota(shape, dim):
    return lax.broadcasted_iota(jnp.int32, shape, dim)


def _mm(a, b, mode, *, tm, tn, tk=None, out_dtype=F32, name):
    if mode == "nn":
        (m, k), n = a.shape, b.shape[1]
    elif mode == "nt":
        (m, k), n = a.shape, b.shape[0]
    else:
        (k, m), n = a.shape, b.shape[1]
    tk = k if tk is None else tk
    assert m % tm == 0 and n % tn == 0 and k % tk == 0, (name, a.shape, b.shape)
    nk = k // tk
    a_spec = {"nn": pl.BlockSpec((tm, tk), lambda i, j, kk: (i, kk)),
              "nt": pl.BlockSpec((tm, tk), lambda i, j, kk: (i, kk)),
              "tn": pl.BlockSpec((tk, tm), lambda i, j, kk: (kk, i))}[mode]
    b_spec = {"nn": pl.BlockSpec((tk, tn), lambda i, j, kk: (kk, j)),
              "nt": pl.BlockSpec((tn, tk), lambda i, j, kk: (j, kk)),
              "tn": pl.BlockSpec((tk, tn), lambda i, j, kk: (kk, j))}[mode]

    if nk == 1:
        def body(a_ref, b_ref, o_ref):
            o_ref[...] = _dot(a_ref[...], b_ref[...], mode).astype(out_dtype)
        scratch = []
    else:
        def body(a_ref, b_ref, o_ref, acc_ref):
            kk = pl.program_id(2)

            @pl.when(kk == 0)
            def _():
                acc_ref[...] = jnp.zeros_like(acc_ref)

            acc_ref[...] += _dot(a_ref[...], b_ref[...], mode)

            @pl.when(kk == nk - 1)
            def _():
                o_ref[...] = acc_ref[...].astype(out_dtype)
        scratch = [pltpu.VMEM((tm, tn), F32)]

    return pl.pallas_call(
        body, name=name, grid=(m // tm, n // tn, nk),
        in_specs=[a_spec, b_spec],
        out_specs=pl.BlockSpec((tm, tn), lambda i, j, kk: (i, j)),
        out_shape=jax.ShapeDtypeStruct((m, n), out_dtype),
        scratch_shapes=scratch,
        compiler_params=_cp(("parallel", "parallel", "arbitrary")),
    )(a, b)


ROWS = 256


def _prenorm(x, w, name):
    def body(x_ref, w_ref, o_ref):
        xv = x_ref[...]
        r = lax.rsqrt(jnp.mean(xv * xv, axis=-1, keepdims=True) + EPS)
        o_ref[...] = (xv * r * w_ref[...]).astype(BF16)

    return pl.pallas_call(
        body, name=name, grid=(T // ROWS,),
        in_specs=[pl.BlockSpec((ROWS, D), lambda i: (i, 0)), pl.BlockSpec((1, D), lambda i: (0, 0))],
        out_specs=pl.BlockSpec((ROWS, D), lambda i: (i, 0)),
        out_shape=jax.ShapeDtypeStruct((T, D), BF16),
        compiler_params=_cp(("parallel",)),
    )(x, w)


def _postnorm(x, z, w, name):
    def body(x_ref, z_ref, w_ref, o_ref):
        zv = z_ref[...]
        r = lax.rsqrt(jnp.mean(zv * zv, axis=-1, keepdims=True) + EPS)
        o_ref[...] = x_ref[...] + zv * r * w_ref[...]

    return pl.pallas_call(
        body, name=name, grid=(T // ROWS,),
        in_specs=[pl.BlockSpec((ROWS, D), lambda i: (i, 0)), pl.BlockSpec((ROWS, D), lambda i: (i, 0)),
                  pl.BlockSpec((1, D), lambda i: (0, 0))],
        out_specs=pl.BlockSpec((ROWS, D), lambda i: (i, 0)),
        out_shape=jax.ShapeDtypeStruct((T, D), F32),
        compiler_params=_cp(("parallel",)),
    )(x, z, w)


def _norm_bwd(z, w, dy, add, name):
    has_add = add is not None

    def body(*refs):
        if has_add:
            z_ref, w_ref, dy_ref, add_ref, dz_ref, dw_ref = refs
        else:
            z_ref, w_ref, dy_ref, dz_ref, dw_ref = refs
        i = pl.program_id(0)

        @pl.when(i == 0)
        def _():
            dw_ref[...] = jnp.zeros_like(dw_ref)

        zv = z_ref[...].astype(F32)
        dyv = dy_ref[...]
        r = lax.rsqrt(jnp.mean(zv * zv, axis=-1, keepdims=True) + EPS)
        wdy = dyv * w_ref[...]
        dz = r * wdy - zv * (r * r * r) * jnp.mean(zv * wdy, axis=-1, keepdims=True)
        if has_add:
            dz = dz + add_ref[...]
        dz_ref[...] = dz
        dw_ref[...] += jnp.sum(dyv * zv * r, axis=0, keepdims=True)

    row = pl.BlockSpec((ROWS, D), lambda i: (i, 0))
    vec = pl.BlockSpec((1, D), lambda i: (0, 0))
    ins = [z, w, dy] + ([add] if has_add else [])
    return pl.pallas_call(
        body, name=name, grid=(T // ROWS,),
        in_specs=[row, vec, row] + ([row] if has_add else []),
        out_specs=(row, vec),
        out_shape=(jax.ShapeDtypeStruct((T, D), F32), jax.ShapeDtypeStruct((1, D), F32)),
        compiler_params=_cp(("arbitrary",)),
    )(*ins)


def _relu2(u, name):
    def body(u_ref, o_ref):
        r = jnp.maximum(u_ref[...], 0.0)
        o_ref[...] = (r * r).astype(BF16)

    blk = pl.BlockSpec((ROWS, DFF), lambda i: (i, 0))
    return pl.pallas_call(body, name=name, grid=(T // ROWS,), in_specs=[blk], out_specs=blk,
                          out_shape=jax.ShapeDtypeStruct((T, DFF), BF16), compiler_params=_cp(("parallel",)))(u)


def _relu2_bwd(u, da, name):
    def body(u_ref, da_ref, o_ref):
        o_ref[...] = da_ref[...] * (2.0 * jnp.maximum(u_ref[...], 0.0))

    blk = pl.BlockSpec((ROWS, DFF), lambda i: (i, 0))
    return pl.pallas_call(body, name=name, grid=(T // ROWS,), in_specs=[blk, blk], out_specs=blk,
                          out_shape=jax.ShapeDtypeStruct((T, DFF), F32), compiler_params=_cp(("parallel",)))(u, da)


def _loss_and_grad(y, tgt):
    def body(y_ref, t_ref, g_ref, l_ref):
        i = pl.program_id(0)

        @pl.when(i == 0)
        def _():
            l_ref[...] = jnp.zeros_like(l_ref)

        e = y_ref[...] - t_ref[...]
        g_ref[...] = e * (1.0 / D)
        l_ref[...] += jnp.sum(e * e) * (0.5 / D)

    row = pl.BlockSpec((ROWS, D), lambda i: (i, 0))
    return pl.pallas_call(
        body, name="loss_head", grid=(T // ROWS,), in_specs=[row, row],
        out_specs=(row, pl.BlockSpec((1, 128), lambda i: (0, 0))),
        out_shape=(jax.ShapeDtypeStruct((T, D), F32), jax.ShapeDtypeStruct((1, 128), F32)),
        compiler_params=_cp(("arbitrary",)),
    )(y, tgt)


def _adamw(w, g, m, v, name):
    rows, cols = w.shape
    tr = rows if rows <= 512 else 256
    assert rows % tr == 0, (name, w.shape)
    c1 = 1.0 - ADAM_B1 ** ADAM_STEP
    c2 = 1.0 - ADAM_B2 ** ADAM_STEP

    def body(w_ref, g_ref, m_ref, v_ref, d_ref, mo_ref, vo_ref):
        gv = g_ref[...]
        mn = ADAM_B1 * m_ref[...] + (1.0 - ADAM_B1) * gv
        vn = ADAM_B2 * v_ref[...] + (1.0 - ADAM_B2) * (gv * gv)
        m_hat = mn / c1
        v_hat = vn / c2
        d_ref[...] = -ADAM_LR * (m_hat / (jnp.sqrt(v_hat) + ADAM_EPS) + ADAM_WD * w_ref[...])
        mo_ref[...] = mn
        vo_ref[...] = vn

    blk = pl.BlockSpec((tr, cols), lambda i: (i, 0))
    sds = jax.ShapeDtypeStruct((rows, cols), F32)
    return pl.pallas_call(body, name=name, grid=(rows // tr,), in_specs=[blk] * 4, out_specs=(blk,) * 3,
                          out_shape=(sds,) * 3, compiler_params=_cp(("parallel",)))(w, g, m, v)


def _gla_consts():
    ltri = (_iota((CHUNK, CHUNK), 0) >= _iota((CHUNK, CHUNK), 1)).astype(F32)
    ones_c = jnp.ones((CHUNK, 128), F32)
    mask = (_iota((256, 512), 0) // 64 == _iota((256, 512), 1) // 128).astype(F32)
    return ltri, ones_c, mask


def _gla_chunk(consts, q, k, v, r, aux, s_prev, wa, ba, nw):
    ltri, ones_c, mask = consts
    la = _log_sigmoid(bdot(aux, wa, "nn") + ba) * (1.0 / 16.0)
    cum = hdot(ltri, la, "nn")
    total = jnp.sum(la, axis=0, keepdims=True)
    k_dec = k * jnp.exp(total - cum)
    inc = bdot(k_dec, v, "tn") * mask
    dec = jnp.exp(hdot(la, ones_c, "tn"))
    dec = jnp.concatenate([dec, dec, dec, dec], axis=1)
    s_new = dec * s_prev + inc
    o = bdot(q * GLA_SCALE, s_new, "nn")
    parts = []
    for h in range(4):
        oh = o[:, h * 128:(h + 1) * 128]
        parts.append(oh * lax.rsqrt(jnp.mean(oh * oh, axis=-1, keepdims=True) + EPS))
    on = jnp.concatenate(parts, axis=1)
    return s_new, on * nw * (r * _sigmoid(r))


def _gla_specs(cmap):
    return [pl.BlockSpec((CHUNK, 256), lambda c: (cmap(c), 0)),
            pl.BlockSpec((CHUNK, 256), lambda c: (cmap(c), 1)),
            pl.BlockSpec((CHUNK, 512), lambda c: (cmap(c), 1)),
            pl.BlockSpec((CHUNK, 512), lambda c: (cmap(c), 2)),
            pl.BlockSpec((CHUNK, 128), lambda c: (cmap(c), AUX_BLK))]


def _gla_fwd(proj, wa, ba, nw):
    def body(q_ref, k_ref, v_ref, r_ref, aux_ref, wa_ref, ba_ref, nw_ref, o_ref, sp_ref, s_ref):
        c = pl.program_id(0)

        @pl.when(c == 0)
        def _():
            s_ref[...] = jnp.zeros_like(s_ref)

        s_prev = s_ref[...]
        sp_ref[...] = s_prev
        s_new, out = _gla_chunk(_gla_consts(), q_ref[...], k_ref[...], v_ref[...], r_ref[...], aux_ref[...],
                                s_prev, wa_ref[...], ba_ref[...], nw_ref[...])
        s_ref[...] = s_new
        o_ref[...] = out

    full = lambda shape: pl.BlockSpec(shape, lambda c: (0,) * len(shape))
    return pl.pallas_call(
        body, name="gla_fwd", grid=(NCHUNK,),
        in_specs=_gla_specs(lambda c: c) + [full((128, 256)), full((1, 256)), full((1, 512))],
        out_specs=(pl.BlockSpec((CHUNK, 512), lambda c: (c, 0)), pl.BlockSpec((None, 256, 512), lambda c: (c, 0, 0))),
        out_shape=(jax.ShapeDtypeStruct((T, 512), F32), jax.ShapeDtypeStruct((NCHUNK, 256, 512), F32)),
        scratch_shapes=[pltpu.VMEM((256, 512), F32)],
        compiler_params=_cp(("arbitrary",)),
    )(proj, proj, proj, proj, proj, wa, ba, nw)


def _gla_bwd(proj, s_prev_all, wa, ba, nw, dcat):
    rev = lambda c: NCHUNK - 1 - c

    def body(q_ref, k_ref, v_ref, r_ref, aux_ref, sp_ref, wa_ref, ba_ref, nw_ref, do_ref,
             dq_ref, dk_ref, dv_ref, dr_ref, daux_ref, dwa_ref, dba_ref, dnw_ref, ds_ref):
        c = pl.program_id(0)

        @pl.when(c == 0)
        def _():
            ds_ref[...] = jnp.zeros_like(ds_ref)
            dwa_ref[...] = jnp.zeros_like(dwa_ref)
            dba_ref[...] = jnp.zeros_like(dba_ref)
            dnw_ref[...] = jnp.zeros_like(dnw_ref)

        fn = functools.partial(_gla_chunk, _gla_consts())
        _, vjp = jax.vjp(fn, q_ref[...], k_ref[...], v_ref[...], r_ref[...], aux_ref[...], sp_ref[...],
                         wa_ref[...], ba_ref[...], nw_ref[...])
        dq, dk, dv, dr, daux, dsp, dwa, dba, dnw = vjp((ds_ref[...], do_ref[...]))
        dq_ref[...] = dq
        dk_ref[...] = dk
        dv_ref[...] = dv
        dr_ref[...] = dr
        daux_ref[...] = daux
        ds_ref[...] = dsp
        dwa_ref[...] += dwa
        dba_ref[...] += dba
        dnw_ref[...] += dnw

    full = lambda shape: pl.BlockSpec(shape, lambda c: (0,) * len(shape))
    blk = lambda w: pl.BlockSpec((CHUNK, w), lambda c: (rev(c), 0))
    sds = lambda *s: jax.ShapeDtypeStruct(s, F32)
    return pl.pallas_call(
        body, name="gla_bwd", grid=(NCHUNK,),
        in_specs=_gla_specs(rev) + [pl.BlockSpec((None, 256, 512), lambda c: (rev(c), 0, 0)),
                                    full((128, 256)), full((1, 256)), full((1, 512)), blk(512)],
        out_specs=(blk(256), blk(256), blk(512), blk(512), blk(128), full((128, 256)), full((1, 256)), full((1, 512))),
        out_shape=(sds(T, 256), sds(T, 256), sds(T, 512), sds(T, 512), sds(T, 128),
                   sds(128, 256), sds(1, 256), sds(1, 512)),
        scratch_shapes=[pltpu.VMEM((256, 512), F32)],
        compiler_params=_cp(("arbitrary",)),
    )(proj, proj, proj, proj, proj, s_prev_all, wa, ba, nw, dcat)


GATE_ROWS = 128


def _fox_gate_block(ltri, aux, bpad, carry):
    lf = _log_sigmoid(aux + bpad)
    cum = hdot(ltri, lf, "nn") + carry
    return cum, carry + jnp.sum(lf, axis=0, keepdims=True)


def _gate_ltri():
    return (_iota((GATE_ROWS, GATE_ROWS), 0) >= _iota((GATE_ROWS, GATE_ROWS), 1)).astype(F32)


def _fox_gate_fwd(proj, bpad):
    def body(aux_ref, b_ref, cum_ref, carry_ref):
        i = pl.program_id(0)

        @pl.when(i == 0)
        def _():
            carry_ref[...] = jnp.zeros_like(carry_ref)

        cum, carry = _fox_gate_block(_gate_ltri(), aux_ref[...], b_ref[...], carry_ref[...])
        cum_ref[...] = cum
        carry_ref[...] = carry

    return pl.pallas_call(
        body, name="fox_gate_fwd", grid=(T // GATE_ROWS,),
        in_specs=[pl.BlockSpec((GATE_ROWS, 128), lambda i: (i, AUX_BLK)), pl.BlockSpec((1, 128), lambda i: (0, 0))],
        out_specs=pl.BlockSpec((GATE_ROWS, 128), lambda i: (i, 0)),
        out_shape=jax.ShapeDtypeStruct((T, 128), F32),
        scratch_shapes=[pltpu.VMEM((1, 128), F32)],
        compiler_params=_cp(("arbitrary",)),
    )(proj, bpad)


def _fox_gate_bwd(proj, bpad, dcrow, dccol_t, daux_gla):
    nb = T // GATE_ROWS
    rev = lambda i: nb - 1 - i

    def body(aux_ref, b_ref, dr_ref, dc_ref, dg_ref, daux_ref, db_ref, dcarry_ref):
        i = pl.program_id(0)

        @pl.when(i == 0)
        def _():
            dcarry_ref[...] = jnp.zeros_like(dcarry_ref)
            db_ref[...] = jnp.zeros_like(db_ref)

        dcum = dr_ref[0] + dr_ref[1] + dr_ref[2] + dr_ref[3] + dc_ref[...]
        fn = functools.partial(_fox_gate_block, _gate_ltri())
        _, vjp = jax.vjp(fn, aux_ref[...], b_ref[...], jnp.zeros((1, 128), F32))
        daux, db, dcarry = vjp((dcum, dcarry_ref[...]))
        daux_ref[...] = daux + dg_ref[...]
        db_ref[...] += db
        dcarry_ref[...] = dcarry

    blk = pl.BlockSpec((GATE_ROWS, 128), lambda i: (rev(i), 0))
    vec = pl.BlockSpec((1, 128), lambda i: (0, 0))
    return pl.pallas_call(
        body, name="fox_gate_bwd", grid=(nb,),
        in_specs=[pl.BlockSpec((GATE_ROWS, 128), lambda i: (rev(i), AUX_BLK)), vec,
                  pl.BlockSpec((4, GATE_ROWS, 128), lambda i: (0, rev(i), 0)), blk, blk],
        out_specs=(blk, vec),
        out_shape=(jax.ShapeDtypeStruct((T, 128), F32), jax.ShapeDtypeStruct((1, 128), F32)),
        scratch_shapes=[pltpu.VMEM((1, 128), F32)],
        compiler_params=_cp(("arbitrary",)),
    )(proj, bpad, dcrow, dccol_t, daux_gla)


FOX_Q = 128


def _fox_block(hp, qb, q, k, v, crow, ccol):
    lane = _iota((FOX_Q, 128), 1)
    causal = (qb * FOX_Q + _iota((FOX_Q, T), 0)) >= _iota((FOX_Q, T), 1)
    sub = _iota((8, T), 0)
    outs = []
    for e in range(2):
        h = 2 * hp + e
        qm = jnp.where((lane >= 64 * e) & (lane < 64 * (e + 1)), q, 0.0)
        s = bdot(qm, k, "nt") * ATT_SCALE
        ct = jnp.sum(jnp.where(lane == FOX_LANE0 + h, crow, 0.0), axis=1, keepdims=True)
        cs = jnp.sum(jnp.where(sub == h, ccol, 0.0), axis=0, keepdims=True)
        s = jnp.where(causal, s + (ct - cs), NEG)
        outs.append(bdot(_softmax_rows(s), v, "nn"))
    return jnp.where(lane < 64, outs[0], outs[1])


def _fox_in_specs():
    return [pl.BlockSpec((FOX_Q, 128), lambda hp, qb: (qb, 12 + hp)),
            pl.BlockSpec((T, 128), lambda hp, qb: (0, 16 + hp)),
            pl.BlockSpec((T, 128), lambda hp, qb: (0, 20 + hp)),
            pl.BlockSpec((FOX_Q, 128), lambda hp, qb: (qb, 0)),
            pl.BlockSpec((8, T), lambda hp, qb: (0, 0))]


def _fox_fwd(proj, cum_r, cum_c):
    def body(q_ref, k_ref, v_ref, cr_ref, cc_ref, o_ref):
        o_ref[...] = _fox_block(pl.program_id(0), pl.program_id(1), q_ref[...], k_ref[...], v_ref[...],
                                cr_ref[...], cc_ref[...])

    return pl.pallas_call(
        body, name="fox_fwd", grid=(4, T // FOX_Q), in_specs=_fox_in_specs(),
        out_specs=pl.BlockSpec((FOX_Q, 128), lambda hp, qb: (qb, hp)),
        out_shape=jax.ShapeDtypeStruct((T, 512), F32),
        compiler_params=_cp(("parallel", "parallel")),
    )(proj, proj, proj, cum_r, cum_c)


def _fox_bwd(proj, cum_r, cum_c, dcat):
    def body(q_ref, k_ref, v_ref, cr_ref, cc_ref, do_ref, dq_ref, dk_ref, dv_ref, dcr_ref, dcc_ref):
        qb = pl.program_id(1)

        @pl.when(qb == 0)
        def _():
            dk_ref[...] = jnp.zeros_like(dk_ref)
            dv_ref[...] = jnp.zeros_like(dv_ref)
            dcc_ref[...] = jnp.zeros_like(dcc_ref)

        fn = functools.partial(_fox_block, pl.program_id(0), qb)
        _, vjp = jax.vjp(fn, q_ref[...], k_ref[...], v_ref[...], cr_ref[...], cc_ref[...])
        dq, dk, dv, dcr, dcc = vjp(do_ref[...])
        dq_ref[...] = dq
        dk_ref[...] += dk
        dv_ref[...] += dv
        dcr_ref[...] = dcr
        dcc_ref[...] += dcc

    sds = lambda *s: jax.ShapeDtypeStruct(s, F32)
    return pl.pallas_call(
        body, name="fox_bwd", grid=(4, T // FOX_Q),
        in_specs=_fox_in_specs() + [pl.BlockSpec((FOX_Q, 128), lambda hp, qb: (qb, 4 + hp))],
        out_specs=(pl.BlockSpec((FOX_Q, 128), lambda hp, qb: (qb, hp)),
                   pl.BlockSpec((T, 128), lambda hp, qb: (0, hp)),
                   pl.BlockSpec((T, 128), lambda hp, qb: (0, hp)),
                   pl.BlockSpec((None, FOX_Q, 128), lambda hp, qb: (hp, qb, 0)),
                   pl.BlockSpec((None, 8, T), lambda hp, qb: (hp, 0, 0))),
        out_shape=(sds(T, 512), sds(T, 512), sds(T, 512), sds(4, T, 128), sds(4, 8, T)),
        compiler_params=_cp(("parallel", "arbitrary")),
    )(proj, proj, proj, cum_r, cum_c, dcat)


def _rel_onehot(q):
    kj = _iota((REL_PAD, CA_BAND), 1)
    rel = jnp.clip(CA_PAD + q - kj, -128, 128) + 128
    return (_iota((REL_PAD, CA_BAND), 0) == rel).astype(F32)


def _bias_build(rbp):
    def body(rb_ref, o_ref):
        o_ref[...] = _hdot_raw(rb_ref[...], _rel_onehot(pl.program_id(0)), "nn")

    return pl.pallas_call(
        body, name="ca_bias_build", grid=(CHUNK,),
        in_specs=[pl.BlockSpec((8, REL_PAD), lambda q: (0, 0))],
        out_specs=pl.BlockSpec((None, 8, CA_BAND), lambda q: (q, 0, 0)),
        out_shape=jax.ShapeDtypeStruct((CHUNK, 8, CA_BAND), F32),
        compiler_params=_cp(("parallel",)),
    )(rbp)


def _bias_grad(dbias_q):
    def body(db_ref, o_ref):
        q = pl.program_id(0)

        @pl.when(q == 0)
        def _():
            o_ref[...] = jnp.zeros_like(o_ref)

        o_ref[...] += _hdot_raw(db_ref[...], _rel_onehot(q), "nt")

    return pl.pallas_call(
        body, name="ca_bias_grad", grid=(CHUNK,),
        in_specs=[pl.BlockSpec((None, 8, CA_BAND), lambda q: (q, 0, 0))],
        out_specs=pl.BlockSpec((8, REL_PAD), lambda q: (0, 0)),
        out_shape=jax.ShapeDtypeStruct((8, REL_PAD), F32),
        compiler_params=_cp(("arbitrary",)),
    )(dbias_q)


def _ca_block(c, q, kb, vb, bias2):
    lane = _iota((CHUNK, 128), 1)
    valid = (c * CHUNK - CA_PAD + _iota((CHUNK, CA_BAND), 1)) >= 0
    outs = []
    for e in range(2):
        qm = jnp.where((lane >= 64 * e) & (lane < 64 * (e + 1)), q, 0.0)
        s = bdot(qm, kb, "nt") * ATT_SCALE
        s = jnp.where(valid, s + bias2[e], NEG)
        outs.append(bdot(_softmax_rows(s), vb, "nn"))
    return jnp.where(lane < 64, outs[0], outs[1])


def _ca_fwd(proj, kvpad, bias):
    def body(q_ref, k_ref, v_ref, b_ref, o_ref):
        c = pl.program_id(1)
        band = pl.ds(pl.multiple_of(c * CHUNK, CHUNK), CA_BAND)
        o_ref[...] = _ca_block(c, q_ref[...], k_ref[band, :], v_ref[band, :], b_ref[...])

    return pl.pallas_call(
        body, name="ca_fwd", grid=(4, NCHUNK),
        in_specs=[pl.BlockSpec((CHUNK, 128), lambda hp, c: (c, hp)),
                  pl.BlockSpec((T + CA_PAD, 128), lambda hp, c: (0, hp)),
                  pl.BlockSpec((T + CA_PAD, 128), lambda hp, c: (0, 4 + hp)),
                  pl.BlockSpec((2, CHUNK, CA_BAND), lambda hp, c: (hp, 0, 0))],
        out_specs=pl.BlockSpec((CHUNK, 128), lambda hp, c: (c, hp)),
        out_shape=jax.ShapeDtypeStruct((T, 512), F32),
        compiler_params=_cp(("parallel", "parallel")),
    )(proj, kvpad, kvpad, bias)


def _ca_bwd(proj, kvpad, bias, dcat):
    def body(q_ref, k_ref, v_ref, b_ref, do_ref, dq_ref, dk_ref, dv_ref, db_ref):
        c = pl.program_id(1)

        @pl.when(c == 0)
        def _():
            dk_ref[...] = jnp.zeros_like(dk_ref)
            dv_ref[...] = jnp.zeros_like(dv_ref)
            db_ref[...] = jnp.zeros_like(db_ref)

        band = pl.ds(pl.multiple_of(c * CHUNK, CHUNK), CA_BAND)
        fn = functools.partial(_ca_block, c)
        _, vjp = jax.vjp(fn, q_ref[...], k_ref[band, :], v_ref[band, :], b_ref[...])
        dq, dkb, dvb, db = vjp(do_ref[...])
        dq_ref[...] = dq
        dk_ref[band, :] += dkb
        dv_ref[band, :] += dvb
        db_ref[...] += db

    sds = lambda *s: jax.ShapeDtypeStruct(s, F32)
    padded = lambda: pl.BlockSpec((T + CA_PAD, 128), lambda hp, c: (0, hp))
    return pl.pallas_call(
        body, name="ca_bwd", grid=(4, NCHUNK),
        in_specs=[pl.BlockSpec((CHUNK, 128), lambda hp, c: (c, hp)),
                  pl.BlockSpec((T + CA_PAD, 128), lambda hp, c: (0, hp)),
                  pl.BlockSpec((T + CA_PAD, 128), lambda hp, c: (0, 4 + hp)),
                  pl.BlockSpec((2, CHUNK, CA_BAND), lambda hp, c: (hp, 0, 0)),
                  pl.BlockSpec((CHUNK, 128), lambda hp, c: (c, hp))],
        out_specs=(pl.BlockSpec((CHUNK, 128), lambda hp, c: (c, hp)), padded(), padded(),
                   pl.BlockSpec((2, CHUNK, CA_BAND), lambda hp, c: (hp, 0, 0))),
        out_shape=(sds(T, 512), sds(T + CA_PAD, 512), sds(T + CA_PAD, 512), sds(8, CHUNK, CA_BAND)),
        compiler_params=_cp(("parallel", "arbitrary")),
    )(proj, kvpad, kvpad, bias, dcat)


def _lru_pre(xs, cw, cb, wa, ba, wx, bx, lam):
    xc = cb + xs[0] * cw[0:1, :] + xs[1] * cw[1:2, :] + xs[2] * cw[2:3, :] + xs[3] * cw[3:4, :]
    ra = _sigmoid(bdot(xc, wa, "nn") + ba)
    ii = _sigmoid(bdot(xc, wx, "nn") + bx)
    la = 8.0 * ra * _log_sigmoid(lam)
    return jnp.exp(la), jnp.sqrt(-_expm1(2.0 * la)) * (ii * xc)


def _lru_pre_specs():
    full = lambda shape: pl.BlockSpec(shape, lambda i: (0,) * len(shape))
    return [pl.BlockSpec((4, ROWS, 512), lambda i: (0, i, 0)), full((4, 512)), full((1, 512)),
            full((512, 512)), full((1, 512)), full((512, 512)), full((1, 512)), full((1, 512))]


def _lru_pre_fwd(xs, cw, cb, wa, ba, wx, bx, lam):
    def body(xs_ref, cw_ref, cb_ref, wa_ref, ba_ref, wx_ref, bx_ref, lam_ref, a_ref, b_ref):
        a, b = _lru_pre(xs_ref[...], cw_ref[...], cb_ref[...], wa_ref[...], ba_ref[...], wx_ref[...], bx_ref[...],
                        lam_ref[...])
        a_ref[...] = a
        b_ref[...] = b

    row = pl.BlockSpec((ROWS, 512), lambda i: (i, 0))
    sds = jax.ShapeDtypeStruct((T, 512), F32)
    return pl.pallas_call(body, name="lru_pre_fwd", grid=(T // ROWS,), in_specs=_lru_pre_specs(),
                          out_specs=(row, row), out_shape=(sds, sds), compiler_params=_cp(("parallel",)),
                          )(xs, cw, cb, wa, ba, wx, bx, lam)


def _lru_pre_bwd(xs, cw, cb, wa, ba, wx, bx, lam, da, db):
    def body(xs_ref, cw_ref, cb_ref, wa_ref, ba_ref, wx_ref, bx_ref, lam_ref, da_ref, db_ref,
             dxs_ref, dcw_ref, dcb_ref, dwa_ref, dba_ref, dwx_ref, dbx_ref, dlam_ref):
        acc = (dcw_ref, dcb_ref, dwa_ref, dba_ref, dwx_ref, dbx_ref, dlam_ref)

        @pl.when(pl.program_id(0) == 0)
        def _():
            for r in acc:
                r[...] = jnp.zeros_like(r)

        _, vjp = jax.vjp(_lru_pre, xs_ref[...], cw_ref[...], cb_ref[...], wa_ref[...], ba_ref[...], wx_ref[...],
                         bx_ref[...], lam_ref[...])
        grads = vjp((da_ref[...], db_ref[...]))
        dxs_ref[...] = grads[0]
        for r, g in zip(acc, grads[1:]):
            r[...] += g

    row = pl.BlockSpec((ROWS, 512), lambda i: (i, 0))
    specs = _lru_pre_specs()
    sds = lambda *s: jax.ShapeDtypeStruct(s, F32)
    return pl.pallas_call(
        body, name="lru_pre_bwd", grid=(T // ROWS,), in_specs=specs + [row, row], out_specs=tuple(specs),
        out_shape=(sds(4, T, 512), sds(4, 512), sds(1, 512), sds(512, 512), sds(1, 512), sds(512, 512), sds(1, 512),
                   sds(1, 512)),
        compiler_params=_cp(("arbitrary",)),
    )(xs, cw, cb, wa, ba, wx, bx, lam, da, db)


def _lru_scan_fwd(a, b):
    def body(a_ref, b_ref, h_ref):
        def step(t, h):
            h = a_ref[pl.ds(t, 1), :] * h + b_ref[pl.ds(t, 1), :]
            h_ref[pl.ds(t, 1), :] = h
            return h

        lax.fori_loop(0, T, step, jnp.zeros((1, 512), F32))

    return pl.pallas_call(body, name="lru_scan_fwd", out_shape=jax.ShapeDtypeStruct((T, 512), F32),
                          compiler_params=pltpu.CompilerParams(vmem_limit_bytes=VMEM_LIMIT))(a, b)


def _lru_scan_bwd(a, h, dh):
    def body(a_ref, h_ref, dh_ref, da_ref, db_ref):
        def step(i, carry):
            t = T - 1 - i
            g = dh_ref[pl.ds(t, 1), :] + carry
            db_ref[pl.ds(t, 1), :] = g
            da_ref[pl.ds(t, 1), :] = g * h_ref[pl.ds(t - 1, 1), :]
            return a_ref[pl.ds(t, 1), :] * g

        carry = lax.fori_loop(0, T - 1, step, jnp.zeros((1, 512), F32))
        db_ref[pl.ds(0, 1), :] = dh_ref[pl.ds(0, 1), :] + carry
        da_ref[pl.ds(0, 1), :] = jnp.zeros((1, 512), F32)

    sds = jax.ShapeDtypeStruct((T, 512), F32)
    return pl.pallas_call(body, name="lru_scan_bwd", out_shape=(sds, sds),
                          compiler_params=pltpu.CompilerParams(vmem_limit_bytes=VMEM_LIMIT))(a, h, dh)


def _lru_post(h, gate):
    return h * _gelu_tanh(gate)


def _lru_post_fwd(h, proj):
    def body(h_ref, g_ref, o_ref):
        o_ref[...] = _lru_post(h_ref[...], g_ref[...])

    row = pl.BlockSpec((ROWS, 512), lambda i: (i, 0))
    return pl.pallas_call(body, name="lru_post_fwd", grid=(T // ROWS,),
                          in_specs=[row, pl.BlockSpec((ROWS, 512), lambda i: (i, 3))], out_specs=row,
                          out_shape=jax.ShapeDtypeStruct((T, 512), F32), compiler_params=_cp(("parallel",)))(h, proj)


def _lru_post_bwd(h, proj, dcat):
    def body(h_ref, g_ref, do_ref, dh_ref, dg_ref):
        _, vjp = jax.vjp(_lru_post, h_ref[...], g_ref[...])
        dh, dg = vjp(do_ref[...])
        dh_ref[...] = dh
        dg_ref[...] = dg

    row = pl.BlockSpec((ROWS, 512), lambda i: (i, 0))
    sds = jax.ShapeDtypeStruct((T, 512), F32)
    return pl.pallas_call(body, name="lru_post_bwd", grid=(T // ROWS,),
                          in_specs=[row, pl.BlockSpec((ROWS, 512), lambda i: (i, 3)),
                                    pl.BlockSpec((ROWS, 512), lambda i: (i, 1))],
                          out_specs=(row, row), out_shape=(sds, sds), compiler_params=_cp(("parallel",)))(h, proj, dcat)


def _conv_dx(dxs_shift):
    def body(d_ref, o_ref):
        o_ref[...] = d_ref[0] + d_ref[1] + d_ref[2] + d_ref[3]

    row = pl.BlockSpec((ROWS, 512), lambda i: (i, 0))
    return pl.pallas_call(body, name="lru_conv_dx", grid=(T // ROWS,),
                          in_specs=[pl.BlockSpec((4, ROWS, 512), lambda i: (0, i, 0))], out_specs=row,
                          out_shape=jax.ShapeDtypeStruct((T, 512), F32), compiler_params=_cp(("parallel",)))(dxs_shift)


def _position():
    return lax.axis_index("x"), lax.axis_index("y"), lax.axis_index("c")


def _other_chips(x, y):
    return [(1 - x, y), (x, 1 - y), (1 - x, 1 - y)]


def _allgather_shards(src):
    rows, cols = src.shape
    half = rows // 2

    def body(src_ref, out_ref, send_sems, recv_sems, local_sem):
        x, y, c = _position()
        sibling = (x, y, 1 - c)
        chips = _other_chips(x, y)
        me = 2 * x + y

        def part(chip, hc):
            return out_ref.at[2 * chip[0] + chip[1], pl.ds(hc * half, half), :]

        def copy(k, chip, hc, to, src=None):
            return pltpu.make_async_remote_copy(
                src_ref=part(chip, hc) if src is None else src, dst_ref=part(chip, hc),
                send_sem=send_sems.at[k], recv_sem=recv_sems.at[k], device_id=to, device_id_type=MESH)

        mine = pltpu.make_async_copy(src_ref, out_ref.at[me], local_sem)
        mine.start()
        my_half = src_ref.at[pl.ds(c * half, half), :]
        first = [copy(j, (x, y), c, (*chip, c), src=my_half) for j, chip in enumerate(chips)]
        for cp in first:
            cp.start()
        passed = [copy(3 + j, chip, c, sibling) for j, chip in enumerate(chips)]
        for j, chip in enumerate(chips):
            copy(j, chip, c, (x, y, c)).wait_recv()
            passed[j].start()
        for j, chip in enumerate(chips):
            copy(3 + j, chip, 1 - c, (x, y, c)).wait_recv()
        for cp in first + passed:
            cp.wait_send()
        mine.wait()

    return pl.pallas_call(
        body, name="allgather_weights",
        in_specs=[pl.BlockSpec(memory_space=pl.ANY)], out_specs=pl.BlockSpec(memory_space=pl.ANY),
        out_shape=jax.ShapeDtypeStruct((4, rows, cols), src.dtype),
        scratch_shapes=[pltpu.SemaphoreType.DMA((6,)), pltpu.SemaphoreType.DMA((6,)), pltpu.SemaphoreType.DMA],
    )(src)


def _pair_swap_half(g4):
    _, _, rh, cols = g4.shape

    def body(g_ref, out_ref, send_sem, recv_sem):
        x, y, c = _position()
        cp = pltpu.make_async_remote_copy(src_ref=g_ref.at[:, 1 - c], dst_ref=out_ref, send_sem=send_sem,
                                          recv_sem=recv_sem, device_id=(x, y, 1 - c), device_id_type=MESH)
        cp.start()
        cp.wait()

    return pl.pallas_call(
        body, name="grad_pair_swap",
        in_specs=[pl.BlockSpec(memory_space=pl.ANY)], out_specs=pl.BlockSpec(memory_space=pl.ANY),
        out_shape=jax.ShapeDtypeStruct((4, rh, cols), g4.dtype),
        scratch_shapes=[pltpu.SemaphoreType.DMA, pltpu.SemaphoreType.DMA],
    )(g4)


def _pair_swap(a, name):
    def body(a_ref, out_ref, send_sem, recv_sem):
        x, y, c = _position()
        cp = pltpu.make_async_remote_copy(src_ref=a_ref, dst_ref=out_ref, send_sem=send_sem, recv_sem=recv_sem,
                                          device_id=(x, y, 1 - c), device_id_type=MESH)
        cp.start()
        cp.wait()

    return pl.pallas_call(
        body, name=name,
        in_specs=[pl.BlockSpec(memory_space=pl.ANY)], out_specs=pl.BlockSpec(memory_space=pl.ANY),
        out_shape=jax.ShapeDtypeStruct(a.shape, a.dtype),
        scratch_shapes=[pltpu.SemaphoreType.DMA, pltpu.SemaphoreType.DMA],
    )(a)


def _chip_alltoall(p):
    def body(p_ref, q_ref, send_sems, recv_sems, local_sem):
        x, y, c = _position()
        me = 2 * x + y
        chips = _other_chips(x, y)
        mine = pltpu.make_async_copy(p_ref.at[me], q_ref.at[me], local_sem)
        mine.start()
        sends = [pltpu.make_async_remote_copy(
            src_ref=p_ref.at[2 * chip[0] + chip[1]], dst_ref=q_ref.at[me], send_sem=send_sems.at[j],
            recv_sem=recv_sems.at[j], device_id=(*chip, c), device_id_type=MESH) for j, chip in enumerate(chips)]
        for cp in sends:
            cp.start()
        for j, chip in enumerate(chips):
            slot = q_ref.at[2 * chip[0] + chip[1]]
            pltpu.make_async_remote_copy(src_ref=slot, dst_ref=slot, send_sem=send_sems.at[j], recv_sem=recv_sems.at[j],
                                         device_id=(x, y, c), device_id_type=MESH).wait_recv()
        for cp in sends:
            cp.wait_send()
        mine.wait()

    return pl.pallas_call(
        body, name="grad_alltoall",
        in_specs=[pl.BlockSpec(memory_space=pl.ANY)], out_specs=pl.BlockSpec(memory_space=pl.ANY),
        out_shape=jax.ShapeDtypeStruct(p.shape, p.dtype),
        scratch_shapes=[pltpu.SemaphoreType.DMA((3,)), pltpu.SemaphoreType.DMA((3,)), pltpu.SemaphoreType.DMA],
    )(p)


COMM_ROWS = 512


def _pair_add(g4, recv, c_idx):
    _, _, rh, cols = g4.shape

    def body(c_ref, g_ref, r_ref, o_ref):
        o_ref[...] = g_ref[...] + r_ref[...]

    return pl.pallas_call(
        body, name="grad_pair_add",
        grid_spec=pltpu.PrefetchScalarGridSpec(
            num_scalar_prefetch=1, grid=(4, rh // COMM_ROWS),
            in_specs=[pl.BlockSpec((None, None, COMM_ROWS, cols), lambda s, j, c_ref: (s, c_ref[0], j, 0)),
                      pl.BlockSpec((None, COMM_ROWS, cols), lambda s, j, c_ref: (s, j, 0))],
            out_specs=pl.BlockSpec((None, COMM_ROWS, cols), lambda s, j, c_ref: (s, j, 0))),
        out_shape=jax.ShapeDtypeStruct((4, rh, cols), F32),
        compiler_params=_cp(("parallel", "parallel")),
    )(c_idx, g4, recv)


def _sum_chips(q):
    _, rh, cols = q.shape

    def body(q_ref, o_ref):
        o_ref[...] = ((q_ref[0] + q_ref[1]) + q_ref[2]) + q_ref[3]

    return pl.pallas_call(
        body, name="grad_sum_chips", grid=(rh // COMM_ROWS,),
        in_specs=[pl.BlockSpec((4, COMM_ROWS, cols), lambda j: (0, j, 0))],
        out_specs=pl.BlockSpec((COMM_ROWS, cols), lambda j: (j, 0)),
        out_shape=jax.ShapeDtypeStruct((rh, cols), F32),
        compiler_params=_cp(("parallel",)),
    )(q)


def _shard_major(g, axis):
    shape = g.shape
    g = g.reshape(shape[:axis] + (4, shape[axis] // 4) + shape[axis + 1:])
    return jnp.moveaxis(g, axis, 0).reshape(4, -1)


def _unshard(g4, shape, axis):
    n = shape[axis] // 4
    g = g4.reshape((4,) + shape[:axis] + (n,) + shape[axis + 1:])
    return jnp.moveaxis(g, 0, axis).reshape(shape)


def _pack_rows(flat4, dtype):
    pad = COMM_R * COMM_C - flat4.shape[1]
    assert pad >= 0
    return jnp.pad(flat4.astype(dtype), ((0, 0), (0, pad))).reshape(4, COMM_R, COMM_C)


def _split(flat, shapes):
    out, off = [], 0
    for shp in shapes:
        n = 1
        for d in shp:
            n *= d
        out.append(flat[..., off:off + n].reshape(flat.shape[:-1] + tuple(shp)))
        off += n
    return out


def _even_cols_to_kernel(w):
    return jnp.concatenate([w[:, :1536], w[:, 1552:3088], w[:, 1536:1552], w[:, 3088:3096],
                            jnp.zeros((w.shape[0], PE - 3096), w.dtype)], axis=1)


def _even_cols_from_kernel(g):
    return jnp.concatenate([g[:, :1536], g[:, 3072:3088], g[:, 1536:3072], g[:, 3088:3096]], axis=1)


def _block_diag(w):
    out = jnp.zeros((512, 512), w.dtype)
    for n in range(8):
        out = lax.dynamic_update_slice(out, w[n], (64 * n, 64 * n))
    return out


def _diag_blocks(g):
    return jnp.stack([g[64 * n:64 * (n + 1), 64 * n:64 * (n + 1)] for n in range(8)])


def _shift_down(a, s):
    return a if s == 0 else jnp.pad(a, ((s, 0), (0, 0)))[:a.shape[0]]


def _shift_up(a, s):
    return a if s == 0 else jnp.pad(a, ((0, s), (0, 0)))[s:]


BIG_SHAPES = [(1024, 774), (256, 1024), (1024, 640), (256, 1024), (2, 1024, 1024), (2, 1024, 1024)]
SMALL_SHARDED_SHAPES = [(2, 4, 256), (16, 64), (4, 128), (128,), (128,), (128,), (128,)]
REPL_SHAPES = [(256,), (512,), (8,), (8, 257), (8, 64, 64), (8, 64, 64)]


def kernel(x, norm_w, w_in_even, gla_w_a_up, gla_b_a, gla_norm_w, fox_b_f, w_out_even, w_in_odd, rel_bias, conv_w, conv_b, lru_w_a, lru_b_a, lru_w_x, lru_b_x, lru_lambda, w_out_odd, w_mlp_up, w_mlp_down, loss_target, m_norm_w, m_w_in_even, m_gla_w_a_up, m_gla_b_a, m_gla_norm_w, m_fox_b_f, m_w_out_even, m_w_in_odd, m_rel_bias, m_conv_w, m_conv_b, m_lru_w_a, m_lru_b_a, m_lru_w_x, m_lru_b_x, m_lru_lambda, m_w_out_odd, m_w_mlp_up, m_w_mlp_down, v_norm_w, v_w_in_even, v_gla_w_a_up, v_gla_b_a, v_gla_norm_w, v_fox_b_f, v_w_out_even, v_w_in_odd, v_rel_bias, v_conv_w, v_conv_b, v_lru_w_a, v_lru_b_a, v_lru_w_x, v_lru_b_x, v_lru_lambda, v_w_out_odd, v_w_mlp_up, v_w_mlp_down):
    c_idx = lax.axis_index("c")

    big_local = [w_in_even[0], w_out_even[0], w_in_odd[0], w_out_odd[0], w_mlp_up, w_mlp_down]
    small_local = [norm_w, gla_w_a_up[0], conv_w[0], conv_b[0], lru_b_a[0], lru_b_x[0], lru_lambda[0]]
    small_flat = jnp.concatenate([a.reshape(-1) for a in small_local])
    small_bits = lax.bitcast_convert_type(small_flat, BF16).reshape(-1)
    flat = jnp.concatenate([a.astype(BF16).reshape(-1) for a in big_local] + [small_bits])
    n_small = small_flat.shape[0]
    src = jnp.pad(flat, (0, COMM_R * COMM_C - flat.shape[0])).reshape(COMM_R, COMM_C)
    gathered = _allgather_shards(src).reshape(4, COMM_R * COMM_C)

    g_big = _split(gathered, BIG_SHAPES)
    n_big = sum(a.size for a in big_local)
    g_small_bits = gathered[:, n_big:n_big + 2 * n_small].reshape(4, n_small, 2)
    g_small = _split(lax.bitcast_convert_type(g_small_bits, F32), SMALL_SHARDED_SHAPES)

    w_in_e = _even_cols_to_kernel(_unshard(g_big[0], (1024, 3096), 1))
    w_out_e = _unshard(g_big[1], (1024, 1024), 0)
    w_in_o = _unshard(g_big[2], (1024, 2560), 1)
    w_out_o = _unshard(g_big[3], (1024, 1024), 0)
    w_up = _unshard(g_big[4], (2, 1024, 4096), 2)
    w_dn = _unshard(g_big[5], (2, 4096, 1024), 1)
    nw_full = _unshard(g_small[0], (2, 4, 1024), 2)
    wa_up = _unshard(g_small[1], (16, 256), 1)
    cw = _unshard(g_small[2], (4, 512), 1)
    cb, lba, lbx, lam = [_unshard(g, (512,), 0).reshape(1, 512) for g in g_small[3:]]
    nw = lambda layer, i: nw_full[layer, i].reshape(1, D)

    wa_pad = jnp.pad(wa_up, ((0, 128 - 16), (0, 0)))
    gla_ba = gla_b_a.reshape(1, 256)
    gla_nw = gla_norm_w.reshape(1, 512)
    fox_bpad = jnp.pad(fox_b_f.reshape(1, 8), ((0, 0), (FOX_LANE0, 128 - FOX_LANE0 - 8)))
    rbp = jnp.pad(rel_bias[0], ((0, 0), (0, REL_PAD - 257)))
    wa_bd = _block_diag(lru_w_a[0])
    wx_bd = _block_diag(lru_w_x[0])

    x0 = x[0]
    tgt = loss_target[0]

    h0 = _prenorm(x0, nw(0, 0), "prenorm_l0_mix")
    proj_e = _mm(h0, w_in_e, "nn", tm=1024, tn=640, name="mm_in_even")
    out_a, s_prev = _gla_fwd(proj_e, wa_pad, gla_ba, gla_nw)
    cum_r = _fox_gate_fwd(proj_e, fox_bpad)
    cum_c = cum_r[:, FOX_LANE0:FOX_LANE0 + 8].T
    out_b = _fox_fwd(proj_e, cum_r, cum_c)
    cat0 = jnp.concatenate([out_a, out_b], axis=1)
    mix0 = _mm(cat0, w_out_e, "nn", tm=1024, tn=512, name="mm_out_even")
    x1 = _postnorm(x0, mix0, nw(0, 1), "postnorm_l0_mix")
    h1 = _prenorm(x1, nw(0, 2), "prenorm_l0_mlp")
    u0 = _mm(h1, w_up[0], "nn", tm=1024, tn=1024, name="mm_up_l0")
    a0 = _relu2(u0, "relu2_l0")
    d0 = _mm(a0, w_dn[0], "nn", tm=1024, tn=512, tk=2048, name="mm_down_l0")
    x2 = _postnorm(x1, d0, nw(0, 3), "postnorm_l0_mlp")

    h2 = _prenorm(x2, nw(1, 0), "prenorm_l1_mix")
    proj_o = _mm(h2, w_in_o, "nn", tm=1024, tn=640, name="mm_in_odd")
    bias_q = _bias_build(rbp)
    bias = bias_q.transpose(1, 0, 2)
    kvpad = jnp.pad(proj_o[:, 512:1536], ((CA_PAD, 0), (0, 0)))
    out_c = _ca_fwd(proj_o, kvpad, bias)
    x_in = proj_o[:, 2048:2560]
    xs = jnp.stack([_shift_down(x_in, 3 - j) for j in range(4)])
    lru_a, lru_b = _lru_pre_fwd(xs, cw, cb, wa_bd, lba, wx_bd, lbx, lam)
    hh = _lru_scan_fwd(lru_a, lru_b)
    out_d = _lru_post_fwd(hh, proj_o)
    cat1 = jnp.concatenate([out_c, out_d], axis=1)
    mix1 = _mm(cat1, w_out_o, "nn", tm=1024, tn=512, name="mm_out_odd")
    x3 = _postnorm(x2, mix1, nw(1, 1), "postnorm_l1_mix")
    h3 = _prenorm(x3, nw(1, 2), "prenorm_l1_mlp")
    u1 = _mm(h3, w_up[1], "nn", tm=1024, tn=1024, name="mm_up_l1")
    a1 = _relu2(u1, "relu2_l1")
    d1 = _mm(a1, w_dn[1], "nn", tm=1024, tn=512, tk=2048, name="mm_down_l1")
    x4 = _postnorm(x3, d1, nw(1, 3), "postnorm_l1_mlp")

    g4, loss_part = _loss_and_grad(x4, tgt)
    loss = lax.psum(loss_part[0, 0], ("x", "y", "c"))

    dd1, dnw13 = _norm_bwd(d1, nw(1, 3), g4, None, "postnorm_l1_mlp_bwd")
    g_dn1 = _mm(a1, dd1, "tn", tm=512, tn=1024, name="mm_down_l1_dw")
    da1 = _mm(dd1, w_dn[1], "nt", tm=1024, tn=1024, name="mm_down_l1_dx")
    du1 = _relu2_bwd(u1, da1, "relu2_l1_bwd")
    g_up1 = _mm(h3, du1, "tn", tm=512, tn=1024, name="mm_up_l1_dw")
    dh3 = _mm(du1, w_up[1], "nt", tm=1024, tn=512, tk=2048, name="mm_up_l1_dx")
    g3, dnw12 = _norm_bwd(x3, nw(1, 2), dh3, g4, "prenorm_l1_mlp_bwd")
    dmix1, dnw11 = _norm_bwd(mix1, nw(1, 1), g3, None, "postnorm_l1_mix_bwd")
    g_out_o = _mm(cat1, dmix1, "tn", tm=512, tn=1024, name="mm_out_odd_dw")
    dcat1 = _mm(dmix1, w_out_o, "nt", tm=1024, tn=512, name="mm_out_odd_dx")

    dq_c, dkpad, dvpad, dbias = _ca_bwd(proj_o, kvpad, bias, dcat1)
    g_rel = _bias_grad(dbias.transpose(1, 0, 2))[:, :257]
    dhh, dgate = _lru_post_bwd(hh, proj_o, dcat1)
    da_l, db_l = _lru_scan_bwd(lru_a, hh, dhh)
    dxs, g_cw, g_cb, g_wa_bd, g_lba, g_wx_bd, g_lbx, g_lam = _lru_pre_bwd(xs, cw, cb, wa_bd, lba, wx_bd, lbx, lam, da_l, db_l)
    dx_in = _conv_dx(jnp.stack([_shift_up(dxs[j], 3 - j) for j in range(4)]))
    dproj_o = jnp.concatenate([dq_c, dkpad[CA_PAD:], dvpad[CA_PAD:], dgate, dx_in], axis=1)
    g_in_o = _mm(h2, dproj_o, "tn", tm=512, tn=640, name="mm_in_odd_dw")
    dh2 = _mm(dproj_o, w_in_o, "nt", tm=1024, tn=512, tk=1280, name="mm_in_odd_dx")
    g2, dnw10 = _norm_bwd(x2, nw(1, 0), dh2, g3, "prenorm_l1_mix_bwd")

    dd0, dnw03 = _norm_bwd(d0, nw(0, 3), g2, None, "postnorm_l0_mlp_bwd")
    g_dn0 = _mm(a0, dd0, "tn", tm=512, tn=1024, name="mm_down_l0_dw")
    da0 = _mm(dd0, w_dn[0], "nt", tm=1024, tn=1024, name="mm_down_l0_dx")
    du0 = _relu2_bwd(u0, da0, "relu2_l0_bwd")
    g_up0 = _mm(h1, du0, "tn", tm=512, tn=1024, name="mm_up_l0_dw")
    dh1 = _mm(du0, w_up[0], "nt", tm=1024, tn=512, tk=2048, name="mm_up_l0_dx")
    g1, dnw02 = _norm_bwd(x1, nw(0, 2), dh1, g2, "prenorm_l0_mlp_bwd")
    dmix0, dnw01 = _norm_bwd(mix0, nw(0, 1), g1, None, "postnorm_l0_mix_bwd")
    g_out_e = _mm(cat0, dmix0, "tn", tm=512, tn=1024, name="mm_out_even_dw")
    dcat0 = _mm(dmix0, w_out_e, "nt", tm=1024, tn=512, name="mm_out_even_dx")

    dq_g, dk_g, dv_g, dr_g, daux_g, g_wa_pad, g_gla_ba, g_gla_nw = _gla_bwd(proj_e, s_prev, wa_pad, gla_ba, gla_nw, dcat0)
    dq_f, dk_f, dv_f, dcrow, dccol = _fox_bwd(proj_e, cum_r, cum_c, dcat0)
    dccol_t = jnp.pad(dccol.sum(axis=0).T, ((0, 0), (FOX_LANE0, 128 - FOX_LANE0 - 8)))
    daux, g_fox_bpad = _fox_gate_bwd(proj_e, fox_bpad, dcrow, dccol_t, daux_g)
    dproj_e = jnp.concatenate([dq_g, dk_g, dv_g, dr_g, dq_f, dk_f, dv_f, daux], axis=1)
    g_in_e = _mm(h0, dproj_e, "tn", tm=512, tn=640, name="mm_in_even_dw")
    dh0 = _mm(dproj_e, w_in_e, "nt", tm=1024, tn=512, tk=640, name="mm_in_even_dx")
    grad_x, dnw00 = _norm_bwd(x0, nw(0, 0), dh0, g1, "prenorm_l0_mix_bwd")

    g_norm = jnp.stack([jnp.concatenate([dnw00, dnw01, dnw02, dnw03]), jnp.concatenate([dnw10, dnw11, dnw12, dnw13])])
    sharded = [(_even_cols_from_kernel(g_in_e), 1), (g_out_e, 0), (g_in_o, 1), (g_out_o, 0),
               (jnp.stack([g_up0, g_up1]), 2), (jnp.stack([g_dn0, g_dn1]), 1),
               (g_norm, 2), (g_wa_pad[:16], 1), (g_cw, 1), (g_cb[0], 0), (g_lba[0], 0), (g_lbx[0], 0), (g_lam[0], 0)]
    replicated = [g_gla_ba[0], g_gla_nw[0], g_fox_bpad[0, FOX_LANE0:FOX_LANE0 + 8], g_rel, _diag_blocks(g_wa_bd),
                  _diag_blocks(g_wx_bd)]
    flat4 = jnp.concatenate([_shard_major(g, ax) for g, ax in sharded]
                            + [jnp.broadcast_to(g.reshape(1, -1), (4, g.size)) for g in replicated], axis=1)
    g4buf = _pack_rows(flat4, F32).reshape(4, 2, COMM_RH, COMM_C)

    recv_half = _pair_swap_half(g4buf)
    pair_sum = _pair_add(g4buf, recv_half, c_idx.reshape(1).astype(jnp.int32))
    from_chips = _chip_alltoall(pair_sum)
    my_half = _sum_chips(from_chips)
    other_half = _pair_swap(my_half, "grad_half_swap")
    lo = jnp.where(c_idx == 0, my_half, other_half)
    hi = jnp.where(c_idx == 0, other_half, my_half)
    reduced = jnp.concatenate([lo, hi], axis=0).reshape(-1)
    g_shards = _split(reduced, BIG_SHAPES + SMALL_SHARDED_SHAPES + REPL_SHAPES)

    names = ["norm_w", "w_in_even", "gla_w_a_up", "gla_b_a", "gla_norm_w", "fox_b_f", "w_out_even", "w_in_odd", "rel_bias",
             "conv_w", "conv_b", "lru_w_a", "lru_b_a", "lru_w_x", "lru_b_x", "lru_lambda", "w_out_odd", "w_mlp_up",
             "w_mlp_down"]
    order = ["w_in_even", "w_out_even", "w_in_odd", "w_out_odd", "w_mlp_up", "w_mlp_down", "norm_w", "gla_w_a_up", "conv_w",
             "conv_b", "lru_b_a", "lru_b_x", "lru_lambda", "gla_b_a", "gla_norm_w", "fox_b_f", "rel_bias", "lru_w_a",
             "lru_w_x"]
    g_of = dict(zip(order, g_shards))
    w_of = dict(norm_w=norm_w, w_in_even=w_in_even, gla_w_a_up=gla_w_a_up, gla_b_a=gla_b_a, gla_norm_w=gla_norm_w,
                fox_b_f=fox_b_f, w_out_even=w_out_even, w_in_odd=w_in_odd, rel_bias=rel_bias, conv_w=conv_w, conv_b=conv_b,
                lru_w_a=lru_w_a, lru_b_a=lru_b_a, lru_w_x=lru_w_x, lru_b_x=lru_b_x, lru_lambda=lru_lambda,
                w_out_odd=w_out_odd, w_mlp_up=w_mlp_up, w_mlp_down=w_mlp_down)
    m_of = dict(norm_w=m_norm_w, w_in_even=m_w_in_even, gla_w_a_up=m_gla_w_a_up, gla_b_a=m_gla_b_a,
                gla_norm_w=m_gla_norm_w, fox_b_f=m_fox_b_f, w_out_even=m_w_out_even, w_in_odd=m_w_in_odd,
                rel_bias=m_rel_bias, conv_w=m_conv_w, conv_b=m_conv_b, lru_w_a=m_lru_w_a, lru_b_a=m_lru_b_a,
                lru_w_x=m_lru_w_x, lru_b_x=m_lru_b_x, lru_lambda=m_lru_lambda, w_out_odd=m_w_out_odd,
                w_mlp_up=m_w_mlp_up, w_mlp_down=m_w_mlp_down)
    v_of = dict(norm_w=v_norm_w, w_in_even=v_w_in_even, gla_w_a_up=v_gla_w_a_up, gla_b_a=v_gla_b_a,
                gla_norm_w=v_gla_norm_w, fox_b_f=v_fox_b_f, w_out_even=v_w_out_even, w_in_odd=v_w_in_odd,
                rel_bias=v_rel_bias, conv_w=v_conv_w, conv_b=v_conv_b, lru_w_a=v_lru_w_a, lru_b_a=v_lru_b_a,
                lru_w_x=v_lru_w_x, lru_b_x=v_lru_b_x, lru_lambda=v_lru_lambda, w_out_odd=v_w_out_odd,
                w_mlp_up=v_w_mlp_up, w_mlp_down=v_w_mlp_down)
    grads, deltas, new_ms, new_vs = [], [], [], []
    for n in names:
        w = w_of[n]
        two_d = (-1, w.shape[-1])
        g = g_of[n].reshape(w.shape)
        d, mn, vn = _adamw(w.reshape(two_d), g.reshape(two_d), m_of[n].reshape(two_d), v_of[n].reshape(two_d),
                           "adamw_" + n)
        grads.append(g)
        deltas.append(d.reshape(w.shape))
        new_ms.append(mn.reshape(w.shape))
        new_vs.append(vn.reshape(w.shape))

    return (loss, grad_x.reshape(1, T, D), *grads, *deltas, *new_ms, *new_vs)
```

```python
import functools

import jax
import jax.numpy as jnp
from jax import lax
from jax.experimental import pallas as pl
from jax.experimental.pallas import tpu as pltpu

F32 = jnp.float32
BF16 = jnp.bfloat16
MESH = pl.DeviceIdType.MESH

T = 2048
D = 1024
DFF = 4096
EPS = 1e-6
CHUNK = 64
NCHUNK = T // CHUNK
PE = 3200
PO = 2560
AUX_BLK = 3072 // 128
FOX_LANE0 = 16
GLA_SCALE = 64 ** -0.5
ATT_SCALE = 64 ** -0.5
NEG = float(jnp.finfo(jnp.float32).min)
CA_BAND = 576
CA_PAD = 512
REL_PAD = 384

VMEM_LIMIT = 48 * 1024 * 1024

ADAM_LR, ADAM_B1, ADAM_B2, ADAM_EPS, ADAM_WD, ADAM_STEP = 0.001, 0.9, 0.999, 1e-08, 0.01, 10

GB_ROWS = 6144
GB_UP = (0, 1024)
GB_DN = (2048, 3072)
GB_IN_O = 4096
GB_OUT_E = 4736
GB_OUT_O = 4992
GB_TAIL = 5248

_DIMS = {"nn": (((1,), (0,)), ((), ())), "nt": (((1,), (1,)), ((), ())), "tn": (((0,), (0,)), ((), ()))}


def _cp(sem, **kw):
    return pltpu.CompilerParams(dimension_semantics=sem, vmem_limit_bytes=VMEM_LIMIT, **kw)


def _dot(a, b, mode):
    return lax.dot_general(a.astype(BF16), b.astype(BF16), _DIMS[mode], preferred_element_type=F32)


@functools.partial(jax.custom_vjp, nondiff_argnums=(2,))
def bdot(a, b, mode):
    return _dot(a, b, mode)


def _bdot_fwd(a, b, mode):
    return _dot(a, b, mode), (a, b)


def _bdot_bwd(mode, res, g):
    a, b = res
    if mode == "nn":
        da, db = _dot(g, b, "nt"), _dot(a, g, "tn")
    elif mode == "nt":
        da, db = _dot(g, b, "nn"), _dot(g, a, "tn")
    else:
        da, db = _dot(b, g, "nt"), _dot(a, g, "nn")
    return da.astype(a.dtype), db.astype(b.dtype)


bdot.defvjp(_bdot_fwd, _bdot_bwd)


def _hdot_raw(a, b, mode):
    return lax.dot_general(a, b, _DIMS[mode], precision=lax.Precision.HIGHEST, preferred_element_type=F32)


@functools.partial(jax.custom_vjp, nondiff_argnums=(2,))
def hdot(a, b, mode):
    return _hdot_raw(a, b, mode)


def _hdot_fwd(a, b, mode):
    return _hdot_raw(a, b, mode), (a, b)


def _hdot_bwd(mode, res, g):
    a, b = res
    if mode == "nn":
        return _hdot_raw(g, b, "nt"), _hdot_raw(a, g, "tn")
    if mode == "nt":
        return _hdot_raw(g, b, "nn"), _hdot_raw(g, a, "tn")
    return _hdot_raw(b, g, "nt"), _hdot_raw(a, g, "nn")


hdot.defvjp(_hdot_fwd, _hdot_bwd)


def _log_sigmoid(x):
    return jnp.minimum(x, 0.0) - jnp.log(1.0 + jnp.exp(-jnp.abs(x)))


def _sigmoid(x):
    return 1.0 / (1.0 + jnp.exp(-x))


def _expm1(x):
    series = x * (1.0 + x * 0.5 * (1.0 + x * (1.0 / 3.0) * (1.0 + x * 0.25)))
    return jnp.where(jnp.abs(x) < 0.03, series, jnp.exp(x) - 1.0)


def _gelu_tanh(x):
    return 0.5 * x * (1.0 + jnp.tanh(0.7978845608028654 * (x + 0.044715 * x * x * x)))


def _softmax_rows(s):
    m = jnp.max(s, axis=-1, keepdims=True)
    p = jnp.exp(s - m)
    return p / jnp.sum(p, axis=-1, keepdims=True)


def _iota(shape, dim):
    return lax.broadcasted_iota(jnp.int32, shape, dim)


def _mm(a, b, mode, *, tm, tn, tk=None, out_dtype=F32, name, b_layer=None, into=None):
    b2 = b.shape[-2:]
    if mode == "nn":
        (m, k), n = a.shape, b2[1]
    elif mode == "nt":
        (m, k), n = a.shape, b2[0]
    else:
        (k, m), n = a.shape, b2[1]
    tk = k if tk is None else tk
    assert m % tm == 0 and n % tn == 0 and k % tk == 0, (name, a.shape, b.shape)
    nk = k // tk
    a_spec = {"nn": pl.BlockSpec((tm, tk), lambda i, j, kk: (i, kk)),
              "nt": pl.BlockSpec((tm, tk), lambda i, j, kk: (i, kk)),
              "tn": pl.BlockSpec((tk, tm), lambda i, j, kk: (kk, i))}[mode]
    b_blk = {"nn": (tk, tn), "nt": (tn, tk), "tn": (tk, tn)}[mode]
    b_idx = {"nn": lambda i, j, kk: (kk, j), "nt": lambda i, j, kk: (j, kk), "tn": lambda i, j, kk: (kk, j)}[mode]
    if b_layer is None:
        b_spec = pl.BlockSpec(b_blk, b_idx)
    else:
        b_spec = pl.BlockSpec((None,) + b_blk, lambda i, j, kk: (b_layer,) + b_idx(i, j, kk))

    if into is None:
        out_spec = pl.BlockSpec((tm, tn), lambda i, j, kk: (i, j))
        out_shape = jax.ShapeDtypeStruct((m, n), out_dtype)
        extra_in, extra_specs, aliases = [], [], {}
    else:
        buf, per_slot, row_off = into
        assert m == 4 * per_slot and per_slot % tm == 0 and row_off % tm == 0 and buf.shape[2] == n, (name, buf.shape)
        bps = per_slot // tm
        out_dtype = buf.dtype
        out_spec = pl.BlockSpec((None, tm, tn), lambda i, j, kk: (i // bps, row_off // tm + i % bps, j))
        out_shape = jax.ShapeDtypeStruct(buf.shape, buf.dtype)
        extra_in, extra_specs, aliases = [buf], [pl.BlockSpec(memory_space=pl.ANY)], {2: 0}

    def body(*refs):
        a_ref, b_ref = refs[0], refs[1]
        o_ref = refs[2 + len(extra_in)]
        if nk == 1:
            o_ref[...] = _dot(a_ref[...], b_ref[...], mode).astype(out_dtype)
            return
        acc_ref = refs[-1]
        kk = pl.program_id(2)

        @pl.when(kk == 0)
        def _():
            acc_ref[...] = jnp.zeros_like(acc_ref)

        acc_ref[...] += _dot(a_ref[...], b_ref[...], mode)

        @pl.when(kk == nk - 1)
        def _():
            o_ref[...] = acc_ref[...].astype(out_dtype)

    return pl.pallas_call(
        body, name=name, grid=(m // tm, n // tn, nk),
        in_specs=[a_spec, b_spec] + extra_specs,
        out_specs=out_spec, out_shape=out_shape,
        scratch_shapes=[pltpu.VMEM((tm, tn), F32)] if nk > 1 else [],
        input_output_aliases=aliases,
        compiler_params=_cp(("parallel", "parallel", "arbitrary")),
    )(a, b, *extra_in)


ROWS = 256


def _prenorm(x, w, name):
    def body(x_ref, w_ref, o_ref):
        xv = x_ref[...]
        r = lax.rsqrt(jnp.mean(xv * xv, axis=-1, keepdims=True) + EPS)
        o_ref[...] = (xv * r * w_ref[...]).astype(BF16)

    return pl.pallas_call(
        body, name=name, grid=(T // ROWS,),
        in_specs=[pl.BlockSpec((ROWS, D), lambda i: (i, 0)), pl.BlockSpec((1, D), lambda i: (0, 0))],
        out_specs=pl.BlockSpec((ROWS, D), lambda i: (i, 0)),
        out_shape=jax.ShapeDtypeStruct((T, D), BF16),
        compiler_params=_cp(("parallel",)),
    )(x, w)


def _postnorm(x, z, w, name):
    def body(x_ref, z_ref, w_ref, o_ref):
        zv = z_ref[...]
        r = lax.rsqrt(jnp.mean(zv * zv, axis=-1, keepdims=True) + EPS)
        o_ref[...] = x_ref[...] + zv * r * w_ref[...]

    return pl.pallas_call(
        body, name=name, grid=(T // ROWS,),
        in_specs=[pl.BlockSpec((ROWS, D), lambda i: (i, 0)), pl.BlockSpec((ROWS, D), lambda i: (i, 0)),
                  pl.BlockSpec((1, D), lambda i: (0, 0))],
        out_specs=pl.BlockSpec((ROWS, D), lambda i: (i, 0)),
        out_shape=jax.ShapeDtypeStruct((T, D), F32),
        compiler_params=_cp(("parallel",)),
    )(x, z, w)


def _norm_bwd(z, w, dy, add, name):
    has_add = add is not None

    def body(*refs):
        if has_add:
            z_ref, w_ref, dy_ref, add_ref, dz_ref, dw_ref = refs
        else:
            z_ref, w_ref, dy_ref, dz_ref, dw_ref = refs
        i = pl.program_id(0)

        @pl.when(i == 0)
        def _():
            dw_ref[...] = jnp.zeros_like(dw_ref)

        zv = z_ref[...].astype(F32)
        dyv = dy_ref[...]
        r = lax.rsqrt(jnp.mean(zv * zv, axis=-1, keepdims=True) + EPS)
        wdy = dyv * w_ref[...]
        dz = r * wdy - zv * (r * r * r) * jnp.mean(zv * wdy, axis=-1, keepdims=True)
        if has_add:
            dz = dz + add_ref[...]
        dz_ref[...] = dz
        dw_ref[...] += jnp.sum(dyv * zv * r, axis=0, keepdims=True)

    row = pl.BlockSpec((ROWS, D), lambda i: (i, 0))
    vec = pl.BlockSpec((1, D), lambda i: (0, 0))
    ins = [z, w, dy] + ([add] if has_add else [])
    return pl.pallas_call(
        body, name=name, grid=(T // ROWS,),
        in_specs=[row, vec, row] + ([row] if has_add else []),
        out_specs=(row, vec),
        out_shape=(jax.ShapeDtypeStruct((T, D), F32), jax.ShapeDtypeStruct((1, D), F32)),
        compiler_params=_cp(("arbitrary",)),
    )(*ins)


def _relu2(u, name):
    def body(u_ref, o_ref):
        r = jnp.maximum(u_ref[...], 0.0)
        o_ref[...] = (r * r).astype(BF16)

    blk = pl.BlockSpec((ROWS, DFF), lambda i: (i, 0))
    return pl.pallas_call(body, name=name, grid=(T // ROWS,), in_specs=[blk], out_specs=blk,
                          out_shape=jax.ShapeDtypeStruct((T, DFF), BF16), compiler_params=_cp(("parallel",)))(u)


def _relu2_bwd(u, da, name):
    def body(u_ref, da_ref, o_ref):
        o_ref[...] = da_ref[...] * (2.0 * jnp.maximum(u_ref[...], 0.0))

    blk = pl.BlockSpec((ROWS, DFF), lambda i: (i, 0))
    return pl.pallas_call(body, name=name, grid=(T // ROWS,), in_specs=[blk, blk], out_specs=blk,
                          out_shape=jax.ShapeDtypeStruct((T, DFF), F32), compiler_params=_cp(("parallel",)))(u, da)


def _loss_and_grad(y, tgt):
    def body(y_ref, t_ref, g_ref, l_ref):
        i = pl.program_id(0)

        @pl.when(i == 0)
        def _():
            l_ref[...] = jnp.zeros_like(l_ref)

        e = y_ref[...] - t_ref[...]
        g_ref[...] = e * (1.0 / D)
        l_ref[...] += jnp.sum(e * e) * (0.5 / D)

    row = pl.BlockSpec((ROWS, D), lambda i: (i, 0))
    return pl.pallas_call(
        body, name="loss_head", grid=(T // ROWS,), in_specs=[row, row],
        out_specs=(row, pl.BlockSpec((1, 128), lambda i: (0, 0))),
        out_shape=(jax.ShapeDtypeStruct((T, D), F32), jax.ShapeDtypeStruct((1, 128), F32)),
        compiler_params=_cp(("arbitrary",)),
    )(y, tgt)


def _adamw(w, g, m, v, name):
    lead = w.shape[:-2]
    assert len(lead) <= 1 and g.shape == w.shape, (name, w.shape, g.shape)
    rows, cols = w.shape[-2:]
    tr = rows if rows <= 512 else 256
    assert rows % tr == 0, (name, w.shape)
    c1 = 1.0 - ADAM_B1 ** ADAM_STEP
    c2 = 1.0 - ADAM_B2 ** ADAM_STEP

    def body(w_ref, g_ref, m_ref, v_ref, d_ref, mo_ref, vo_ref):
        gv = g_ref[...]
        mn = ADAM_B1 * m_ref[...] + (1.0 - ADAM_B1) * gv
        vn = ADAM_B2 * v_ref[...] + (1.0 - ADAM_B2) * (gv * gv)
        m_hat = mn / c1
        v_hat = vn / c2
        d_ref[...] = -ADAM_LR * (m_hat / (jnp.sqrt(v_hat) + ADAM_EPS) + ADAM_WD * w_ref[...])
        mo_ref[...] = mn
        vo_ref[...] = vn

    if lead:
        grid = (lead[0], rows // tr)
        blk = pl.BlockSpec((None, tr, cols), lambda l, i: (l, i, 0))
    else:
        grid = (rows // tr,)
        blk = pl.BlockSpec((tr, cols), lambda i: (i, 0))
    sds = jax.ShapeDtypeStruct(w.shape, F32)
    return pl.pallas_call(body, name=name, grid=grid, in_specs=[blk] * 4, out_specs=(blk,) * 3,
                          out_shape=(sds,) * 3, compiler_params=_cp(("parallel",) * len(grid)))(w, g, m, v)


def _gla_consts():
    ltri = (_iota((CHUNK, CHUNK), 0) >= _iota((CHUNK, CHUNK), 1)).astype(F32)
    ones_c = jnp.ones((CHUNK, 128), F32)
    mask = (_iota((256, 512), 0) // 64 == _iota((256, 512), 1) // 128).astype(F32)
    return ltri, ones_c, mask


def _gla_chunk(consts, q, k, v, r, aux, s_prev, wa, ba, nw):
    ltri, ones_c, mask = consts
    la = _log_sigmoid(bdot(aux, wa, "nn") + ba) * (1.0 / 16.0)
    cum = hdot(ltri, la, "nn")
    total = jnp.sum(la, axis=0, keepdims=True)
    k_dec = k * jnp.exp(total - cum)
    inc = bdot(k_dec, v, "tn") * mask
    dec = jnp.exp(hdot(la, ones_c, "tn"))
    dec = jnp.concatenate([dec, dec, dec, dec], axis=1)
    s_new = dec * s_prev + inc
    o = bdot(q * GLA_SCALE, s_new, "nn")
    parts = []
    for h in range(4):
        oh = o[:, h * 128:(h + 1) * 128]
        parts.append(oh * lax.rsqrt(jnp.mean(oh * oh, axis=-1, keepdims=True) + EPS))
    on = jnp.concatenate(parts, axis=1)
    return s_new, on * nw * (r * _sigmoid(r))


def _gla_specs(cmap):
    return [pl.BlockSpec((CHUNK, 256), lambda c: (cmap(c), 0)),
            pl.BlockSpec((CHUNK, 256), lambda c: (cmap(c), 1)),
            pl.BlockSpec((CHUNK, 512), lambda c: (cmap(c), 1)),
            pl.BlockSpec((CHUNK, 512), lambda c: (cmap(c), 2)),
            pl.BlockSpec((CHUNK, 128), lambda c: (cmap(c), AUX_BLK))]


def _gla_fwd(proj, wa, ba, nw):
    def body(q_ref, k_ref, v_ref, r_ref, aux_ref, wa_ref, ba_ref, nw_ref, o_ref, sp_ref, s_ref):
        c = pl.program_id(0)

        @pl.when(c == 0)
        def _():
            s_ref[...] = jnp.zeros_like(s_ref)

        s_prev = s_ref[...]
        sp_ref[...] = s_prev
        s_new, out = _gla_chunk(_gla_consts(), q_ref[...], k_ref[...], v_ref[...], r_ref[...], aux_ref[...],
                                s_prev, wa_ref[...], ba_ref[...], nw_ref[...])
        s_ref[...] = s_new
        o_ref[...] = out

    full = lambda shape: pl.BlockSpec(shape, lambda c: (0,) * len(shape))
    return pl.pallas_call(
        body, name="gla_fwd", grid=(NCHUNK,),
        in_specs=_gla_specs(lambda c: c) + [full((128, 256)), full((1, 256)), full((1, 512))],
        out_specs=(pl.BlockSpec((CHUNK, 512), lambda c: (c, 0)), pl.BlockSpec((None, 256, 512), lambda c: (c, 0, 0))),
        out_shape=(jax.ShapeDtypeStruct((T, D), F32), jax.ShapeDtypeStruct((NCHUNK, 256, 512), F32)),
        scratch_shapes=[pltpu.VMEM((256, 512), F32)],
        compiler_params=_cp(("arbitrary",)),
    )(proj, proj, proj, proj, proj, wa, ba, nw)


def _gla_bwd(proj, s_prev_all, wa, ba, nw, dcat):
    rev = lambda c: NCHUNK - 1 - c

    def body(q_ref, k_ref, v_ref, r_ref, aux_ref, sp_ref, wa_ref, ba_ref, nw_ref, do_ref,
             dq_ref, dk_ref, dv_ref, dr_ref, daux_ref, dwa_ref, dba_ref, dnw_ref, ds_ref):
        c = pl.program_id(0)

        @pl.when(c == 0)
        def _():
            ds_ref[...] = jnp.zeros_like(ds_ref)
            dwa_ref[...] = jnp.zeros_like(dwa_ref)
            dba_ref[...] = jnp.zeros_like(dba_ref)
            dnw_ref[...] = jnp.zeros_like(dnw_ref)

        fn = functools.partial(_gla_chunk, _gla_consts())
        _, vjp = jax.vjp(fn, q_ref[...], k_ref[...], v_ref[...], r_ref[...], aux_ref[...], sp_ref[...],
                         wa_ref[...], ba_ref[...], nw_ref[...])
        dq, dk, dv, dr, daux, dsp, dwa, dba, dnw = vjp((ds_ref[...], do_ref[...]))
        dq_ref[...] = dq
        dk_ref[...] = dk
        dv_ref[...] = dv
        dr_ref[...] = dr
        daux_ref[...] = daux
        ds_ref[...] = dsp
        dwa_ref[...] += dwa
        dba_ref[...] += dba
        dnw_ref[...] += dnw

    full = lambda shape: pl.BlockSpec(shape, lambda c: (0,) * len(shape))
    blk = lambda w: pl.BlockSpec((CHUNK, w), lambda c: (rev(c), 0))
    sds = lambda *s: jax.ShapeDtypeStruct(s, F32)
    return pl.pallas_call(
        body, name="gla_bwd", grid=(NCHUNK,),
        in_specs=_gla_specs(rev) + [pl.BlockSpec((None, 256, 512), lambda c: (rev(c), 0, 0)),
                                    full((128, 256)), full((1, 256)), full((1, 512)), blk(512)],
        out_specs=(blk(256), blk(256), blk(512), blk(512), blk(128), full((128, 256)), full((1, 256)), full((1, 512))),
        out_shape=(sds(T, 256), sds(T, 256), sds(T, 512), sds(T, 512), sds(T, 128),
                   sds(128, 256), sds(1, 256), sds(1, 512)),
        scratch_shapes=[pltpu.VMEM((256, 512), F32)],
        compiler_params=_cp(("arbitrary",)),
    )(proj, proj, proj, proj, proj, s_prev_all, wa, ba, nw, dcat)


GATE_ROWS = 128


def _fox_gate_block(ltri, aux, bpad, carry):
    lf = _log_sigmoid(aux + bpad)
    cum = hdot(ltri, lf, "nn") + carry
    return cum, carry + jnp.sum(lf, axis=0, keepdims=True)


def _gate_ltri():
    return (_iota((GATE_ROWS, GATE_ROWS), 0) >= _iota((GATE_ROWS, GATE_ROWS), 1)).astype(F32)


def _fox_gate_fwd(proj, bpad):
    def body(aux_ref, b_ref, cum_ref, carry_ref):
        i = pl.program_id(0)

        @pl.when(i == 0)
        def _():
            carry_ref[...] = jnp.zeros_like(carry_ref)

        cum, carry = _fox_gate_block(_gate_ltri(), aux_ref[...], b_ref[...], carry_ref[...])
        cum_ref[...] = cum
        carry_ref[...] = carry

    return pl.pallas_call(
        body, name="fox_gate_fwd", grid=(T // GATE_ROWS,),
        in_specs=[pl.BlockSpec((GATE_ROWS, 128), lambda i: (i, AUX_BLK)), pl.BlockSpec((1, 128), lambda i: (0, 0))],
        out_specs=pl.BlockSpec((GATE_ROWS, 128), lambda i: (i, 0)),
        out_shape=jax.ShapeDtypeStruct((T, 128), F32),
        scratch_shapes=[pltpu.VMEM((1, 128), F32)],
        compiler_params=_cp(("arbitrary",)),
    )(proj, bpad)


def _fox_gate_bwd(proj, bpad, dcrow, dccol_t, daux_gla):
    nb = T // GATE_ROWS
    rev = lambda i: nb - 1 - i

    def body(aux_ref, b_ref, dr_ref, dc_ref, dg_ref, daux_ref, db_ref, dcarry_ref):
        i = pl.program_id(0)

        @pl.when(i == 0)
        def _():
            dcarry_ref[...] = jnp.zeros_like(dcarry_ref)
            db_ref[...] = jnp.zeros_like(db_ref)

        dcum = dr_ref[0] + dr_ref[1] + dr_ref[2] + dr_ref[3] + dc_ref[...]
        fn = functools.partial(_fox_gate_block, _gate_ltri())
        _, vjp = jax.vjp(fn, aux_ref[...], b_ref[...], jnp.zeros((1, 128), F32))
        daux, db, dcarry = vjp((dcum, dcarry_ref[...]))
        daux_ref[...] = daux + dg_ref[...]
        db_ref[...] += db
        dcarry_ref[...] = dcarry

    blk = pl.BlockSpec((GATE_ROWS, 128), lambda i: (rev(i), 0))
    vec = pl.BlockSpec((1, 128), lambda i: (0, 0))
    return pl.pallas_call(
        body, name="fox_gate_bwd", grid=(nb,),
        in_specs=[pl.BlockSpec((GATE_ROWS, 128), lambda i: (rev(i), AUX_BLK)), vec,
                  pl.BlockSpec((4, GATE_ROWS, 128), lambda i: (0, rev(i), 0)), blk, blk],
        out_specs=(blk, vec),
        out_shape=(jax.ShapeDtypeStruct((T, 128), F32), jax.ShapeDtypeStruct((1, 128), F32)),
        scratch_shapes=[pltpu.VMEM((1, 128), F32)],
        compiler_params=_cp(("arbitrary",)),
    )(proj, bpad, dcrow, dccol_t, daux_gla)


FOX_Q = 128


def _fox_block(hp, qb, q, k, v, crow, ccol):
    lane = _iota((FOX_Q, 128), 1)
    causal = (qb * FOX_Q + _iota((FOX_Q, T), 0)) >= _iota((FOX_Q, T), 1)
    sub = _iota((8, T), 0)
    outs = []
    for e in range(2):
        h = 2 * hp + e
        qm = jnp.where((lane >= 64 * e) & (lane < 64 * (e + 1)), q, 0.0)
        s = bdot(qm, k, "nt") * ATT_SCALE
        ct = jnp.sum(jnp.where(lane == FOX_LANE0 + h, crow, 0.0), axis=1, keepdims=True)
        cs = jnp.sum(jnp.where(sub == h, ccol, 0.0), axis=0, keepdims=True)
        s = jnp.where(causal, s + (ct - cs), NEG)
        outs.append(bdot(_softmax_rows(s), v, "nn"))
    return jnp.where(lane < 64, outs[0], outs[1])


def _fox_in_specs():
    return [pl.BlockSpec((FOX_Q, 128), lambda hp, qb: (qb, 12 + hp)),
            pl.BlockSpec((T, 128), lambda hp, qb: (0, 16 + hp)),
            pl.BlockSpec((T, 128), lambda hp, qb: (0, 20 + hp)),
            pl.BlockSpec((FOX_Q, 128), lambda hp, qb: (qb, 0)),
            pl.BlockSpec((8, T), lambda hp, qb: (0, 0))]


def _fox_fwd(proj, cum_r, cum_c, cat):
    def body(q_ref, k_ref, v_ref, cr_ref, cc_ref, cat_ref, o_ref):
        o_ref[...] = _fox_block(pl.program_id(0), pl.program_id(1), q_ref[...], k_ref[...], v_ref[...],
                                cr_ref[...], cc_ref[...])

    return pl.pallas_call(
        body, name="fox_fwd", grid=(4, T // FOX_Q), in_specs=_fox_in_specs() + [pl.BlockSpec(memory_space=pl.ANY)],
        out_specs=pl.BlockSpec((FOX_Q, 128), lambda hp, qb: (qb, 4 + hp)),
        out_shape=jax.ShapeDtypeStruct((T, D), F32), input_output_aliases={5: 0},
        compiler_params=_cp(("parallel", "parallel")),
    )(proj, proj, proj, cum_r, cum_c, cat)


def _fox_bwd(proj, cum_r, cum_c, dcat):
    def body(q_ref, k_ref, v_ref, cr_ref, cc_ref, do_ref, dq_ref, dk_ref, dv_ref, dcr_ref, dcc_ref):
        qb = pl.program_id(1)

        @pl.when(qb == 0)
        def _():
            dk_ref[...] = jnp.zeros_like(dk_ref)
            dv_ref[...] = jnp.zeros_like(dv_ref)
            dcc_ref[...] = jnp.zeros_like(dcc_ref)

        fn = functools.partial(_fox_block, pl.program_id(0), qb)
        _, vjp = jax.vjp(fn, q_ref[...], k_ref[...], v_ref[...], cr_ref[...], cc_ref[...])
        dq, dk, dv, dcr, dcc = vjp(do_ref[...])
        dq_ref[...] = dq
        dk_ref[...] += dk
        dv_ref[...] += dv
        dcr_ref[...] = dcr
        dcc_ref[...] += dcc

    sds = lambda *s: jax.ShapeDtypeStruct(s, F32)
    return pl.pallas_call(
        body, name="fox_bwd", grid=(4, T // FOX_Q),
        in_specs=_fox_in_specs() + [pl.BlockSpec((FOX_Q, 128), lambda hp, qb: (qb, 4 + hp))],
        out_specs=(pl.BlockSpec((FOX_Q, 128), lambda hp, qb: (qb, hp)),
                   pl.BlockSpec((T, 128), lambda hp, qb: (0, hp)),
                   pl.BlockSpec((T, 128), lambda hp, qb: (0, hp)),
                   pl.BlockSpec((None, FOX_Q, 128), lambda hp, qb: (hp, qb, 0)),
                   pl.BlockSpec((None, 8, T), lambda hp, qb: (hp, 0, 0))),
        out_shape=(sds(T, 512), sds(T, 512), sds(T, 512), sds(4, T, 128), sds(4, 8, T)),
        compiler_params=_cp(("parallel", "arbitrary")),
    )(proj, proj, proj, cum_r, cum_c, dcat)


def _rel_onehot(q):
    kj = _iota((REL_PAD, CA_BAND), 1)
    rel = jnp.clip(CA_PAD + q - kj, -128, 128) + 128
    return (_iota((REL_PAD, CA_BAND), 0) == rel).astype(F32)


def _bias_build(rbp):
    def body(rb_ref, o_ref):
        o_ref[...] = _hdot_raw(rb_ref[...], _rel_onehot(pl.program_id(0)), "nn")

    return pl.pallas_call(
        body, name="ca_bias_build", grid=(CHUNK,),
        in_specs=[pl.BlockSpec((8, REL_PAD), lambda q: (0, 0))],
        out_specs=pl.BlockSpec((None, 8, CA_BAND), lambda q: (q, 0, 0)),
        out_shape=jax.ShapeDtypeStruct((CHUNK, 8, CA_BAND), F32),
        compiler_params=_cp(("parallel",)),
    )(rbp)


def _bias_grad(dbias_q):
    def body(db_ref, o_ref):
        q = pl.program_id(0)

        @pl.when(q == 0)
        def _():
            o_ref[...] = jnp.zeros_like(o_ref)

        o_ref[...] += _hdot_raw(db_ref[...], _rel_onehot(q), "nt")

    return pl.pallas_call(
        body, name="ca_bias_grad", grid=(CHUNK,),
        in_specs=[pl.BlockSpec((None, 8, CA_BAND), lambda q: (q, 0, 0))],
        out_specs=pl.BlockSpec((8, REL_PAD), lambda q: (0, 0)),
        out_shape=jax.ShapeDtypeStruct((8, REL_PAD), F32),
        compiler_params=_cp(("arbitrary",)),
    )(dbias_q)


def _ca_block(c, q, kb, vb, bias2):
    lane = _iota((CHUNK, 128), 1)
    valid = (c * CHUNK - CA_PAD + _iota((CHUNK, CA_BAND), 1)) >= 0
    outs = []
    for e in range(2):
        qm = jnp.where((lane >= 64 * e) & (lane < 64 * (e + 1)), q, 0.0)
        s = bdot(qm, kb, "nt") * ATT_SCALE
        s = jnp.where(valid, s + bias2[e], NEG)
        outs.append(bdot(_softmax_rows(s), vb, "nn"))
    return jnp.where(lane < 64, outs[0], outs[1])


def _ca_fwd(proj, kvpad, bias):
    def body(q_ref, k_ref, v_ref, b_ref, o_ref):
        c = pl.program_id(1)
        band = pl.ds(pl.multiple_of(c * CHUNK, CHUNK), CA_BAND)
        o_ref[...] = _ca_block(c, q_ref[...], k_ref[band, :], v_ref[band, :], b_ref[...])

    return pl.pallas_call(
        body, name="ca_fwd", grid=(4, NCHUNK),
        in_specs=[pl.BlockSpec((CHUNK, 128), lambda hp, c: (c, hp)),
                  pl.BlockSpec((T + CA_PAD, 128), lambda hp, c: (0, hp)),
                  pl.BlockSpec((T + CA_PAD, 128), lambda hp, c: (0, 4 + hp)),
                  pl.BlockSpec((2, CHUNK, CA_BAND), lambda hp, c: (hp, 0, 0))],
        out_specs=pl.BlockSpec((CHUNK, 128), lambda hp, c: (c, hp)),
        out_shape=jax.ShapeDtypeStruct((T, D), F32),
        compiler_params=_cp(("parallel", "parallel")),
    )(proj, kvpad, kvpad, bias)


def _ca_bwd(proj, kvpad, bias, dcat):
    def body(q_ref, k_ref, v_ref, b_ref, do_ref, dq_ref, dk_ref, dv_ref, db_ref):
        c = pl.program_id(1)

        @pl.when(c == 0)
        def _():
            dk_ref[...] = jnp.zeros_like(dk_ref)
            dv_ref[...] = jnp.zeros_like(dv_ref)
            db_ref[...] = jnp.zeros_like(db_ref)

        band = pl.ds(pl.multiple_of(c * CHUNK, CHUNK), CA_BAND)
        fn = functools.partial(_ca_block, c)
        _, vjp = jax.vjp(fn, q_ref[...], k_ref[band, :], v_ref[band, :], b_ref[...])
        dq, dkb, dvb, db = vjp(do_ref[...])
        dq_ref[...] = dq
        dk_ref[band, :] += dkb
        dv_ref[band, :] += dvb
        db_ref[...] += db

    sds = lambda *s: jax.ShapeDtypeStruct(s, F32)
    padded = lambda: pl.BlockSpec((T + CA_PAD, 128), lambda hp, c: (0, hp))
    return pl.pallas_call(
        body, name="ca_bwd", grid=(4, NCHUNK),
        in_specs=[pl.BlockSpec((CHUNK, 128), lambda hp, c: (c, hp)),
                  pl.BlockSpec((T + CA_PAD, 128), lambda hp, c: (0, hp)),
                  pl.BlockSpec((T + CA_PAD, 128), lambda hp, c: (0, 4 + hp)),
                  pl.BlockSpec((2, CHUNK, CA_BAND), lambda hp, c: (hp, 0, 0)),
                  pl.BlockSpec((CHUNK, 128), lambda hp, c: (c, hp))],
        out_specs=(pl.BlockSpec((CHUNK, 128), lambda hp, c: (c, hp)), padded(), padded(),
                   pl.BlockSpec((2, CHUNK, CA_BAND), lambda hp, c: (hp, 0, 0))),
        out_shape=(sds(T, 512), sds(T + CA_PAD, 512), sds(T + CA_PAD, 512), sds(8, CHUNK, CA_BAND)),
        compiler_params=_cp(("parallel", "arbitrary")),
    )(proj, kvpad, kvpad, bias, dcat)


def _lru_pre(xs, cw, cb, wa, ba, wx, bx, lam):
    xc = cb + xs[0] * cw[0:1, :] + xs[1] * cw[1:2, :] + xs[2] * cw[2:3, :] + xs[3] * cw[3:4, :]
    ra = _sigmoid(bdot(xc, wa, "nn") + ba)
    ii = _sigmoid(bdot(xc, wx, "nn") + bx)
    la = 8.0 * ra * _log_sigmoid(lam)
    return jnp.exp(la), jnp.sqrt(-_expm1(2.0 * la)) * (ii * xc)


def _lru_pre_specs():
    full = lambda shape: pl.BlockSpec(shape, lambda i: (0,) * len(shape))
    return [pl.BlockSpec((4, ROWS, 512), lambda i: (0, i, 0)), full((4, 512)), full((1, 512)),
            full((512, 512)), full((1, 512)), full((512, 512)), full((1, 512)), full((1, 512))]


def _lru_pre_fwd(xs, cw, cb, wa, ba, wx, bx, lam):
    def body(xs_ref, cw_ref, cb_ref, wa_ref, ba_ref, wx_ref, bx_ref, lam_ref, a_ref, b_ref):
        a, b = _lru_pre(xs_ref[...], cw_ref[...], cb_ref[...], wa_ref[...], ba_ref[...], wx_ref[...], bx_ref[...],
                        lam_ref[...])
        a_ref[...] = a
        b_ref[...] = b

    row = pl.BlockSpec((ROWS, 512), lambda i: (i, 0))
    sds = jax.ShapeDtypeStruct((T, 512), F32)
    return pl.pallas_call(body, name="lru_pre_fwd", grid=(T // ROWS,), in_specs=_lru_pre_specs(),
                          out_specs=(row, row), out_shape=(sds, sds), compiler_params=_cp(("parallel",)),
                          )(xs, cw, cb, wa, ba, wx, bx, lam)


def _lru_pre_bwd(xs, cw, cb, wa, ba, wx, bx, lam, da, db):
    def body(xs_ref, cw_ref, cb_ref, wa_ref, ba_ref, wx_ref, bx_ref, lam_ref, da_ref, db_ref,
             dxs_ref, dcw_ref, dcb_ref, dwa_ref, dba_ref, dwx_ref, dbx_ref, dlam_ref):
        acc = (dcw_ref, dcb_ref, dwa_ref, dba_ref, dwx_ref, dbx_ref, dlam_ref)

        @pl.when(pl.program_id(0) == 0)
        def _():
            for r in acc:
                r[...] = jnp.zeros_like(r)

        _, vjp = jax.vjp(_lru_pre, xs_ref[...], cw_ref[...], cb_ref[...], wa_ref[...], ba_ref[...], wx_ref[...],
                         bx_ref[...], lam_ref[...])
        grads = vjp((da_ref[...], db_ref[...]))
        dxs_ref[...] = grads[0]
        for r, g in zip(acc, grads[1:]):
            r[...] += g

    row = pl.BlockSpec((ROWS, 512), lambda i: (i, 0))
    specs = _lru_pre_specs()
    sds = lambda *s: jax.ShapeDtypeStruct(s, F32)
    return pl.pallas_call(
        body, name="lru_pre_bwd", grid=(T // ROWS,), in_specs=specs + [row, row], out_specs=tuple(specs),
        out_shape=(sds(4, T, 512), sds(4, 512), sds(1, 512), sds(512, 512), sds(1, 512), sds(512, 512), sds(1, 512),
                   sds(1, 512)),
        compiler_params=_cp(("arbitrary",)),
    )(xs, cw, cb, wa, ba, wx, bx, lam, da, db)


def _lru_scan_fwd(a, b):
    def body(a_ref, b_ref, h_ref):
        def step(t, h):
            h = a_ref[pl.ds(t, 1), :] * h + b_ref[pl.ds(t, 1), :]
            h_ref[pl.ds(t, 1), :] = h
            return h

        lax.fori_loop(0, T, step, jnp.zeros((1, 512), F32))

    return pl.pallas_call(body, name="lru_scan_fwd", out_shape=jax.ShapeDtypeStruct((T, 512), F32),
                          compiler_params=pltpu.CompilerParams(vmem_limit_bytes=VMEM_LIMIT))(a, b)


def _lru_scan_bwd(a, h, dh):
    def body(a_ref, h_ref, dh_ref, da_ref, db_ref):
        def step(i, carry):
            t = T - 1 - i
            g = dh_ref[pl.ds(t, 1), :] + carry
            db_ref[pl.ds(t, 1), :] = g
            da_ref[pl.ds(t, 1), :] = g * h_ref[pl.ds(t - 1, 1), :]
            return a_ref[pl.ds(t, 1), :] * g

        carry = lax.fori_loop(0, T - 1, step, jnp.zeros((1, 512), F32))
        db_ref[pl.ds(0, 1), :] = dh_ref[pl.ds(0, 1), :] + carry
        da_ref[pl.ds(0, 1), :] = jnp.zeros((1, 512), F32)

    sds = jax.ShapeDtypeStruct((T, 512), F32)
    return pl.pallas_call(body, name="lru_scan_bwd", out_shape=(sds, sds),
                          compiler_params=pltpu.CompilerParams(vmem_limit_bytes=VMEM_LIMIT))(a, h, dh)


def _lru_post(h, gate):
    return h * _gelu_tanh(gate)


def _lru_post_fwd(h, proj, cat):
    def body(h_ref, g_ref, cat_ref, o_ref):
        o_ref[...] = _lru_post(h_ref[...], g_ref[...])

    row = pl.BlockSpec((ROWS, 512), lambda i: (i, 0))
    return pl.pallas_call(body, name="lru_post_fwd", grid=(T // ROWS,),
                          in_specs=[row, pl.BlockSpec((ROWS, 512), lambda i: (i, 3)), pl.BlockSpec(memory_space=pl.ANY)],
                          out_specs=pl.BlockSpec((ROWS, 512), lambda i: (i, 1)),
                          out_shape=jax.ShapeDtypeStruct((T, D), F32), input_output_aliases={2: 0},
                          compiler_params=_cp(("parallel",)))(h, proj, cat)


def _lru_post_bwd(h, proj, dcat):
    def body(h_ref, g_ref, do_ref, dh_ref, dg_ref):
        _, vjp = jax.vjp(_lru_post, h_ref[...], g_ref[...])
        dh, dg = vjp(do_ref[...])
        dh_ref[...] = dh
        dg_ref[...] = dg

    row = pl.BlockSpec((ROWS, 512), lambda i: (i, 0))
    sds = jax.ShapeDtypeStruct((T, 512), F32)
    return pl.pallas_call(body, name="lru_post_bwd", grid=(T // ROWS,),
                          in_specs=[row, pl.BlockSpec((ROWS, 512), lambda i: (i, 3)),
                                    pl.BlockSpec((ROWS, 512), lambda i: (i, 1))],
                          out_specs=(row, row), out_shape=(sds, sds), compiler_params=_cp(("parallel",)))(h, proj, dcat)


def _conv_dx(dxs_shift):
    def body(d_ref, o_ref):
        o_ref[...] = d_ref[0] + d_ref[1] + d_ref[2] + d_ref[3]

    row = pl.BlockSpec((ROWS, 512), lambda i: (i, 0))
    return pl.pallas_call(body, name="lru_conv_dx", grid=(T // ROWS,),
                          in_specs=[pl.BlockSpec((4, ROWS, 512), lambda i: (0, i, 0))], out_specs=row,
                          out_shape=jax.ShapeDtypeStruct((T, 512), F32), compiler_params=_cp(("parallel",)))(dxs_shift)


def _position():
    return lax.axis_index("x"), lax.axis_index("y"), lax.axis_index("c")


def _other_chips(x, y):
    return [(1 - x, y), (x, 1 - y), (1 - x, 1 - y)]


def _al(v, n):
    return v * n if isinstance(v, int) else pl.multiple_of(v * n, n)


_AG_ITEMS = [
    ((4, 32, 128), lambda o, s, h: o.at[s, pl.ds(_al(h, 16), 16), :], lambda r, h: r.at[pl.ds(_al(h, 16), 16), :]),
    ((4, 1024, 774), lambda o, s, h: o.at[s, pl.ds(_al(h, 512), 512), :], lambda r, h: r.at[pl.ds(_al(h, 512), 512), :]),
    ((1024, 1024), lambda o, s, h: o.at[pl.ds(_al(2 * s + h, 128), 128), :], lambda r, h: r.at[pl.ds(_al(h, 128), 128), :]),
    ((2, 1024, 4096), lambda o, s, h: o.at[h, :, pl.ds(_al(s, 1024), 1024)], lambda r, h: r.at[h]),
    ((2, 4096, 1024), lambda o, s, h: o.at[h, pl.ds(_al(s, 1024), 1024), :], lambda r, h: r.at[h]),
    ((1024, 2560), lambda o, s, h: o.at[pl.ds(_al(h, 512), 512), pl.ds(_al(s, 640), 640)],
     lambda r, h: r.at[pl.ds(_al(h, 512), 512), :]),
    ((1024, 1024), lambda o, s, h: o.at[pl.ds(_al(2 * s + h, 128), 128), :], lambda r, h: r.at[pl.ds(_al(h, 128), 128), :]),
]


def _allgather_weights(shards):
    n = len(_AG_ITEMS)

    def body(*refs):
        srcs, outs = refs[:n], refs[n:2 * n]
        send_sems, recv_sems, local_sems = refs[2 * n:]
        x, y, c = _position()
        sibling = (x, y, 1 - c)
        chips = _other_chips(x, y)
        me = 2 * x + y
        waits = []
        for i, (_, dst, half) in enumerate(_AG_ITEMS):
            out_ref, src_ref = outs[i], srcs[i]

            def copy(k, slot, hc, to, src=None, dst=dst, out_ref=out_ref, i=i):
                there = dst(out_ref, slot, hc)
                return pltpu.make_async_remote_copy(
                    src_ref=there if src is None else src, dst_ref=there, send_sem=send_sems.at[6 * i + k],
                    recv_sem=recv_sems.at[6 * i + k], device_id=to, device_id_type=MESH)

            for hc in range(2):
                mine = pltpu.make_async_copy(half(src_ref, hc), dst(out_ref, me, hc), local_sems.at[2 * i + hc])
                mine.start()
                waits.append(mine.wait)
            first = [copy(j, me, c, (*chip, c), src=half(src_ref, c)) for j, chip in enumerate(chips)]
            for cp in first:
                cp.start()
                waits.append(cp.wait_send)
        for i in range(n):
            dst, out_ref = _AG_ITEMS[i][1], outs[i]

            def copy(k, slot, hc, to, dst=dst, out_ref=out_ref, i=i):
                there = dst(out_ref, slot, hc)
                return pltpu.make_async_remote_copy(
                    src_ref=there, dst_ref=there, send_sem=send_sems.at[6 * i + k], recv_sem=recv_sems.at[6 * i + k],
                    device_id=to, device_id_type=MESH)

            for j, chip in enumerate(chips):
                slot = 2 * chip[0] + chip[1]
                copy(j, slot, c, (x, y, c)).wait_recv()
                passed = copy(3 + j, slot, c, sibling)
                passed.start()
                waits.append(passed.wait_send)
        for i in range(n):
            dst, out_ref = _AG_ITEMS[i][1], outs[i]
            for j, chip in enumerate(chips):
                there = dst(out_ref, 2 * chip[0] + chip[1], 1 - c)
                pltpu.make_async_remote_copy(
                    src_ref=there, dst_ref=there, send_sem=send_sems.at[6 * i + 3 + j],
                    recv_sem=recv_sems.at[6 * i + 3 + j], device_id=(x, y, c), device_id_type=MESH).wait_recv()
        for w in waits:
            w()

    any_spec = pl.BlockSpec(memory_space=pl.ANY)
    return pl.pallas_call(
        body, name="allgather_weights",
        in_specs=[any_spec] * n, out_specs=(any_spec,) * n,
        out_shape=tuple(jax.ShapeDtypeStruct(shape, s.dtype) for (shape, _, _), s in zip(_AG_ITEMS, shards)),
        scratch_shapes=[pltpu.SemaphoreType.DMA((6 * n,)), pltpu.SemaphoreType.DMA((6 * n,)),
                        pltpu.SemaphoreType.DMA((2 * n,))],
    )(*shards)


def _pair_swap_cols(gb):
    _, rows, cols = gb.shape
    hc = cols // 2

    def body(g_ref, out_ref, send_sem, recv_sem):
        x, y, c = _position()
        cp = pltpu.make_async_remote_copy(src_ref=g_ref.at[:, :, pl.ds(_al(1 - c, hc), hc)], dst_ref=out_ref,
                                          send_sem=send_sem, recv_sem=recv_sem, device_id=(x, y, 1 - c),
                                          device_id_type=MESH)
        cp.start()
        cp.wait()

    return pl.pallas_call(
        body, name="grad_pair_swap",
        in_specs=[pl.BlockSpec(memory_space=pl.ANY)], out_specs=pl.BlockSpec(memory_space=pl.ANY),
        out_shape=jax.ShapeDtypeStruct((4, rows, hc), gb.dtype),
        scratch_shapes=[pltpu.SemaphoreType.DMA, pltpu.SemaphoreType.DMA],
    )(gb)


def _handover(half):
    rows, hc = half.shape

    def body(h_ref, out_ref, send_sem, recv_sem, local_sem):
        x, y, c = _position()
        mine = pltpu.make_async_copy(h_ref, out_ref.at[:, pl.ds(_al(c, hc), hc)], local_sem)
        mine.start()
        cp = pltpu.make_async_remote_copy(src_ref=h_ref, dst_ref=out_ref.at[:, pl.ds(_al(c, hc), hc)],
                                          send_sem=send_sem, recv_sem=recv_sem, device_id=(x, y, 1 - c),
                                          device_id_type=MESH)
        cp.start()
        theirs = out_ref.at[:, pl.ds(_al(1 - c, hc), hc)]
        pltpu.make_async_remote_copy(src_ref=theirs, dst_ref=theirs, send_sem=send_sem, recv_sem=recv_sem,
                                     device_id=(x, y, c), device_id_type=MESH).wait_recv()
        cp.wait_send()
        mine.wait()

    return pl.pallas_call(
        body, name="grad_handover",
        in_specs=[pl.BlockSpec(memory_space=pl.ANY)], out_specs=pl.BlockSpec(memory_space=pl.ANY),
        out_shape=jax.ShapeDtypeStruct((rows, 2 * hc), half.dtype),
        scratch_shapes=[pltpu.SemaphoreType.DMA, pltpu.SemaphoreType.DMA, pltpu.SemaphoreType.DMA],
    )(half)


def _chip_alltoall(p):
    def body(p_ref, q_ref, send_sems, recv_sems, local_sem):
        x, y, c = _position()
        me = 2 * x + y
        chips = _other_chips(x, y)
        mine = pltpu.make_async_copy(p_ref.at[me], q_ref.at[me], local_sem)
        mine.start()
        sends = [pltpu.make_async_remote_copy(
            src_ref=p_ref.at[2 * chip[0] + chip[1]], dst_ref=q_ref.at[me], send_sem=send_sems.at[j],
            recv_sem=recv_sems.at[j], device_id=(*chip, c), device_id_type=MESH) for j, chip in enumerate(chips)]
        for cp in sends:
            cp.start()
        for j, chip in enumerate(chips):
            slot = q_ref.at[2 * chip[0] + chip[1]]
            pltpu.make_async_remote_copy(src_ref=slot, dst_ref=slot, send_sem=send_sems.at[j], recv_sem=recv_sems.at[j],
                                         device_id=(x, y, c), device_id_type=MESH).wait_recv()
        for cp in sends:
            cp.wait_send()
        mine.wait()

    return pl.pallas_call(
        body, name="grad_alltoall",
        in_specs=[pl.BlockSpec(memory_space=pl.ANY)], out_specs=pl.BlockSpec(memory_space=pl.ANY),
        out_shape=jax.ShapeDtypeStruct(p.shape, p.dtype),
        scratch_shapes=[pltpu.SemaphoreType.DMA((3,)), pltpu.SemaphoreType.DMA((3,)), pltpu.SemaphoreType.DMA],
    )(p)


COMM_ROWS = 512


def _pair_add(gb, recv, c_idx):
    _, rows, cols = gb.shape
    hc = cols // 2

    def body(c_ref, g_ref, r_ref, o_ref):
        o_ref[...] = (g_ref[...].astype(F32) + r_ref[...].astype(F32)).astype(o_ref.dtype)

    return pl.pallas_call(
        body, name="grad_pair_add",
        grid_spec=pltpu.PrefetchScalarGridSpec(
            num_scalar_prefetch=1, grid=(4, rows // COMM_ROWS),
            in_specs=[pl.BlockSpec((None, COMM_ROWS, hc), lambda s, j, c_ref: (s, j, c_ref[0])),
                      pl.BlockSpec((None, COMM_ROWS, hc), lambda s, j, c_ref: (s, j, 0))],
            out_specs=pl.BlockSpec((None, COMM_ROWS, hc), lambda s, j, c_ref: (s, j, 0))),
        out_shape=jax.ShapeDtypeStruct((4, rows, hc), gb.dtype),
        compiler_params=_cp(("parallel", "parallel")),
    )(c_idx, gb, recv)


def _sum_chips(q):
    _, rows, hc = q.shape

    def body(q_ref, o_ref):
        o_ref[...] = ((q_ref[0].astype(F32) + q_ref[1].astype(F32)) + q_ref[2].astype(F32)) + q_ref[3].astype(F32)

    return pl.pallas_call(
        body, name="grad_sum_chips", grid=(rows // COMM_ROWS,),
        in_specs=[pl.BlockSpec((4, COMM_ROWS, hc), lambda j: (0, j, 0))],
        out_specs=pl.BlockSpec((COMM_ROWS, hc), lambda j: (j, 0)),
        out_shape=jax.ShapeDtypeStruct((rows, hc), F32),
        compiler_params=_cp(("parallel",)),
    )(q)


def _shard_major(g, axis):
    shape = g.shape
    g = g.reshape(shape[:axis] + (4, shape[axis] // 4) + shape[axis + 1:])
    return jnp.moveaxis(g, axis, 0).reshape(4, -1)


def _unshard(g4, shape, axis):
    n = shape[axis] // 4
    g = g4.reshape((4,) + shape[:axis] + (n,) + shape[axis + 1:])
    return jnp.moveaxis(g, 0, axis).reshape(shape)


def _split(flat, shapes):
    out, off = [], 0
    for shp in shapes:
        n = 1
        for d in shp:
            n *= d
        out.append(flat[..., off:off + n].reshape(flat.shape[:-1] + tuple(shp)))
        off += n
    return out


def _even_cols_to_kernel(w):
    return jnp.concatenate([w[:, :1536], w[:, 1552:3088], w[:, 1536:1552], w[:, 3088:3096],
                            jnp.zeros((w.shape[0], PE - 3096), w.dtype)], axis=1)


def _block_diag(w):
    out = jnp.zeros((512, 512), w.dtype)
    for n in range(8):
        out = lax.dynamic_update_slice(out, w[n], (64 * n, 64 * n))
    return out


def _diag_blocks(g):
    return jnp.stack([g[64 * n:64 * (n + 1), 64 * n:64 * (n + 1)] for n in range(8)])


def _shift_down(a, s):
    return a if s == 0 else jnp.pad(a, ((s, 0), (0, 0)))[:a.shape[0]]


def _shift_up(a, s):
    return a if s == 0 else jnp.pad(a, ((0, s), (0, 0)))[s:]


SMALL_SHARDED_SHAPES = [(2, 4, 256), (16, 64), (4, 128), (128,), (128,), (128,), (128,)]
REPL_SHAPES = [(256,), (512,), (8,), (8, 257), (8, 64, 64), (8, 64, 64)]


def kernel(x, norm_w, w_in_even, gla_w_a_up, gla_b_a, gla_norm_w, fox_b_f, w_out_even, w_in_odd, rel_bias, conv_w, conv_b, lru_w_a, lru_b_a, lru_w_x, lru_b_x, lru_lambda, w_out_odd, w_mlp_up, w_mlp_down, loss_target, m_norm_w, m_w_in_even, m_gla_w_a_up, m_gla_b_a, m_gla_norm_w, m_fox_b_f, m_w_out_even, m_w_in_odd, m_rel_bias, m_conv_w, m_conv_b, m_lru_w_a, m_lru_b_a, m_lru_w_x, m_lru_b_x, m_lru_lambda, m_w_out_odd, m_w_mlp_up, m_w_mlp_down, v_norm_w, v_w_in_even, v_gla_w_a_up, v_gla_b_a, v_gla_norm_w, v_fox_b_f, v_w_out_even, v_w_in_odd, v_rel_bias, v_conv_w, v_conv_b, v_lru_w_a, v_lru_b_a, v_lru_w_x, v_lru_b_x, v_lru_lambda, v_w_out_odd, v_w_mlp_up, v_w_mlp_down):
    c_idx = lax.axis_index("c")

    small_local = [norm_w, gla_w_a_up[0], conv_w[0], conv_b[0], lru_b_a[0], lru_b_x[0], lru_lambda[0]]
    small_src = jnp.concatenate([a.reshape(-1) for a in small_local]).reshape(32, 128)
    small4, w_in_e4, w_out_e, w_up, w_dn, w_in_o, w_out_o = _allgather_weights(
        [small_src, w_in_even[0].astype(BF16), w_out_even[0].astype(BF16), w_mlp_up.astype(BF16),
         w_mlp_down.astype(BF16), w_in_odd[0].astype(BF16), w_out_odd[0].astype(BF16)])

    w_in_e = _even_cols_to_kernel(_unshard(w_in_e4, (1024, 3096), 1))
    g_small = _split(small4.reshape(4, 32 * 128), SMALL_SHARDED_SHAPES)
    nw_full = _unshard(g_small[0], (2, 4, 1024), 2)
    wa_up = _unshard(g_small[1], (16, 256), 1)
    cw = _unshard(g_small[2], (4, 512), 1)
    cb, lba, lbx, lam = [_unshard(g, (512,), 0).reshape(1, 512) for g in g_small[3:]]
    nw = lambda layer, i: nw_full[layer, i].reshape(1, D)

    wa_pad = jnp.pad(wa_up, ((0, 128 - 16), (0, 0)))
    gla_ba = gla_b_a.reshape(1, 256)
    gla_nw = gla_norm_w.reshape(1, 512)
    fox_bpad = jnp.pad(fox_b_f.reshape(1, 8), ((0, 0), (FOX_LANE0, 128 - FOX_LANE0 - 8)))
    rbp = jnp.pad(rel_bias[0], ((0, 0), (0, REL_PAD - 257)))
    wa_bd = _block_diag(lru_w_a[0])
    wx_bd = _block_diag(lru_w_x[0])

    x0 = x[0]
    tgt = loss_target[0]

    h0 = _prenorm(x0, nw(0, 0), "prenorm_l0_mix")
    proj_e = _mm(h0, w_in_e, "nn", tm=1024, tn=640, name="mm_in_even")
    cat0, s_prev = _gla_fwd(proj_e, wa_pad, gla_ba, gla_nw)
    cum_r = _fox_gate_fwd(proj_e, fox_bpad)
    cum_c = cum_r[:, FOX_LANE0:FOX_LANE0 + 8].T
    cat0 = _fox_fwd(proj_e, cum_r, cum_c, cat0)
    mix0 = _mm(cat0, w_out_e, "nn", tm=1024, tn=512, name="mm_out_even")
    x1 = _postnorm(x0, mix0, nw(0, 1), "postnorm_l0_mix")
    h1 = _prenorm(x1, nw(0, 2), "prenorm_l0_mlp")
    u0 = _mm(h1, w_up, "nn", tm=1024, tn=1024, b_layer=0, name="mm_up_l0")
    a0 = _relu2(u0, "relu2_l0")
    d0 = _mm(a0, w_dn, "nn", tm=1024, tn=512, tk=2048, b_layer=0, name="mm_down_l0")
    x2 = _postnorm(x1, d0, nw(0, 3), "postnorm_l0_mlp")

    h2 = _prenorm(x2, nw(1, 0), "prenorm_l1_mix")
    proj_o = _mm(h2, w_in_o, "nn", tm=1024, tn=640, name="mm_in_odd")
    bias_q = _bias_build(rbp)
    bias = bias_q.transpose(1, 0, 2)
    kvpad = jnp.pad(proj_o[:, 512:1536], ((CA_PAD, 0), (0, 0)))
    cat1 = _ca_fwd(proj_o, kvpad, bias)
    x_in = proj_o[:, 2048:2560]
    xs = jnp.stack([_shift_down(x_in, 3 - j) for j in range(4)])
    lru_a, lru_b = _lru_pre_fwd(xs, cw, cb, wa_bd, lba, wx_bd, lbx, lam)
    hh = _lru_scan_fwd(lru_a, lru_b)
    cat1 = _lru_post_fwd(hh, proj_o, cat1)
    mix1 = _mm(cat1, w_out_o, "nn", tm=1024, tn=512, name="mm_out_odd")
    x3 = _postnorm(x2, mix1, nw(1, 1), "postnorm_l1_mix")
    h3 = _prenorm(x3, nw(1, 2), "prenorm_l1_mlp")
    u1 = _mm(h3, w_up, "nn", tm=1024, tn=1024, b_layer=1, name="mm_up_l1")
    a1 = _relu2(u1, "relu2_l1")
    d1 = _mm(a1, w_dn, "nn", tm=1024, tn=512, tk=2048, b_layer=1, name="mm_down_l1")
    x4 = _postnorm(x3, d1, nw(1, 3), "postnorm_l1_mlp")

    g4, loss_part = _loss_and_grad(x4, tgt)
    loss = lax.psum(loss_part[0, 0], ("x", "y", "c"))

    gb = lax.empty((4, GB_ROWS, D), BF16)
    dd1, dnw13 = _norm_bwd(d1, nw(1, 3), g4, None, "postnorm_l1_mlp_bwd")
    gb = _mm(a1, dd1, "tn", tm=512, tn=1024, into=(gb, 1024, GB_DN[1]), name="mm_down_l1_dw")
    da1 = _mm(dd1, w_dn, "nt", tm=1024, tn=1024, b_layer=1, name="mm_down_l1_dx")
    du1 = _relu2_bwd(u1, da1, "relu2_l1_bwd")
    gb = _mm(du1, h3, "tn", tm=512, tn=1024, into=(gb, 1024, GB_UP[1]), name="mm_up_l1_dw")
    dh3 = _mm(du1, w_up, "nt", tm=1024, tn=512, tk=2048, b_layer=1, name="mm_up_l1_dx")
    g3, dnw12 = _norm_bwd(x3, nw(1, 2), dh3, g4, "prenorm_l1_mlp_bwd")
    dmix1, dnw11 = _norm_bwd(mix1, nw(1, 1), g3, None, "postnorm_l1_mix_bwd")
    gb = _mm(cat1, dmix1, "tn", tm=128, tn=1024, into=(gb, 256, GB_OUT_O), name="mm_out_odd_dw")
    dcat1 = _mm(dmix1, w_out_o, "nt", tm=1024, tn=512, name="mm_out_odd_dx")

    dq_c, dkpad, dvpad, dbias = _ca_bwd(proj_o, kvpad, bias, dcat1)
    g_rel = _bias_grad(dbias.transpose(1, 0, 2))[:, :257]
    dhh, dgate = _lru_post_bwd(hh, proj_o, dcat1)
    da_l, db_l = _lru_scan_bwd(lru_a, hh, dhh)
    dxs, g_cw, g_cb, g_wa_bd, g_lba, g_wx_bd, g_lbx, g_lam = _lru_pre_bwd(xs, cw, cb, wa_bd, lba, wx_bd, lbx, lam, da_l, db_l)
    dx_in = _conv_dx(jnp.stack([_shift_up(dxs[j], 3 - j) for j in range(4)]))
    dproj_o = jnp.concatenate([dq_c, dkpad[CA_PAD:], dvpad[CA_PAD:], dgate, dx_in], axis=1)
    gb = _mm(dproj_o, h2, "tn", tm=128, tn=1024, into=(gb, 640, GB_IN_O), name="mm_in_odd_dw")
    dh2 = _mm(dproj_o, w_in_o, "nt", tm=1024, tn=512, tk=1280, name="mm_in_odd_dx")
    g2, dnw10 = _norm_bwd(x2, nw(1, 0), dh2, g3, "prenorm_l1_mix_bwd")

    dd0, dnw03 = _norm_bwd(d0, nw(0, 3), g2, None, "postnorm_l0_mlp_bwd")
    gb = _mm(a0, dd0, "tn", tm=512, tn=1024, into=(gb, 1024, GB_DN[0]), name="mm_down_l0_dw")
    da0 = _mm(dd0, w_dn, "nt", tm=1024, tn=1024, b_layer=0, name="mm_down_l0_dx")
    du0 = _relu2_bwd(u0, da0, "relu2_l0_bwd")
    gb = _mm(du0, h1, "tn", tm=512, tn=1024, into=(gb, 1024, GB_UP[0]), name="mm_up_l0_dw")
    dh1 = _mm(du0, w_up, "nt", tm=1024, tn=512, tk=2048, b_layer=0, name="mm_up_l0_dx")
    g1, dnw02 = _norm_bwd(x1, nw(0, 2), dh1, g2, "prenorm_l0_mlp_bwd")
    dmix0, dnw01 = _norm_bwd(mix0, nw(0, 1), g1, None, "postnorm_l0_mix_bwd")
    gb = _mm(cat0, dmix0, "tn", tm=128, tn=1024, into=(gb, 256, GB_OUT_E), name="mm_out_even_dw")
    dcat0 = _mm(dmix0, w_out_e, "nt", tm=1024, tn=512, name="mm_out_even_dx")

    dq_g, dk_g, dv_g, dr_g, daux_g, g_wa_pad, g_gla_ba, g_gla_nw = _gla_bwd(proj_e, s_prev, wa_pad, gla_ba, gla_nw, dcat0)
    dq_f, dk_f, dv_f, dcrow, dccol = _fox_bwd(proj_e, cum_r, cum_c, dcat0)
    dccol_t = jnp.pad(dccol.sum(axis=0).T, ((0, 0), (FOX_LANE0, 128 - FOX_LANE0 - 8)))
    daux, g_fox_bpad = _fox_gate_bwd(proj_e, fox_bpad, dcrow, dccol_t, daux_g)
    dproj_e = jnp.concatenate([dq_g, dk_g, dv_g, dr_g, dq_f, dk_f, dv_f, daux], axis=1)
    gt_in_e = _mm(dproj_e, h0, "tn", tm=640, tn=1024, name="mm_in_even_dw")
    dh0 = _mm(dproj_e, w_in_e, "nt", tm=1024, tn=512, tk=640, name="mm_in_even_dx")
    grad_x, dnw00 = _norm_bwd(x0, nw(0, 0), dh0, g1, "prenorm_l0_mix_bwd")

    g_norm = jnp.stack([jnp.concatenate([dnw00, dnw01, dnw02, dnw03]), jnp.concatenate([dnw10, dnw11, dnw12, dnw13])])
    sharded = [(g_norm, 2), (g_wa_pad[:16], 1), (g_cw, 1), (g_cb[0], 0), (g_lba[0], 0), (g_lbx[0], 0), (g_lam[0], 0)]
    replicated = [g_gla_ba[0], g_gla_nw[0], g_fox_bpad[0, FOX_LANE0:FOX_LANE0 + 8], g_rel, _diag_blocks(g_wa_bd),
                  _diag_blocks(g_wx_bd)]
    small4 = jnp.concatenate([_shard_major(g, ax) for g, ax in sharded]
                             + [jnp.broadcast_to(g.reshape(1, -1), (4, g.size)) for g in replicated], axis=1)
    n_small = small4.shape[1]
    small_rows = GB_ROWS - GB_TAIL - 774
    small4 = jnp.pad(small4, ((0, 0), (0, small_rows * D - n_small))).reshape(4, small_rows, D)
    gt_rows = jnp.concatenate([gt_in_e[:1536], gt_in_e[3072:3088], gt_in_e[1536:3072], gt_in_e[3088:3096]], axis=0)
    tail = jnp.concatenate([gt_rows.reshape(4, 774, D), small4], axis=1).astype(BF16)
    gb = lax.dynamic_update_slice(gb, tail, (0, GB_TAIL, 0))

    recv_half = _pair_swap_cols(gb)
    pair_sum = _pair_add(gb, recv_half, c_idx.reshape(1).astype(jnp.int32))
    from_chips = _chip_alltoall(pair_sum)
    reduced = _handover(_sum_chips(from_chips))

    g_up = reduced[GB_UP[0]:GB_UP[0] + 2048].reshape(2, 1024, 1024).transpose(0, 2, 1)
    g_dn = reduced[GB_DN[0]:GB_DN[0] + 2048].reshape(2, 1024, 1024)
    g_small = _split(reduced[GB_TAIL + 774:].reshape(-1)[:n_small], SMALL_SHARDED_SHAPES + REPL_SHAPES)
    g_of = dict(zip(["norm_w", "gla_w_a_up", "conv_w", "conv_b", "lru_b_a", "lru_b_x", "lru_lambda", "gla_b_a",
                     "gla_norm_w", "fox_b_f", "rel_bias", "lru_w_a", "lru_w_x"], g_small))
    g_of.update(w_mlp_up=g_up, w_mlp_down=g_dn, w_in_odd=reduced[GB_IN_O:GB_IN_O + 640].T,
                w_out_even=reduced[GB_OUT_E:GB_OUT_E + 256], w_out_odd=reduced[GB_OUT_O:GB_OUT_O + 256],
                w_in_even=reduced[GB_TAIL:GB_TAIL + 774].T)

    names = ["norm_w", "w_in_even", "gla_w_a_up", "gla_b_a", "gla_norm_w", "fox_b_f", "w_out_even", "w_in_odd", "rel_bias",
             "conv_w", "conv_b", "lru_w_a", "lru_b_a", "lru_w_x", "lru_b_x", "lru_lambda", "w_out_odd", "w_mlp_up",
             "w_mlp_down"]
    w_of = dict(norm_w=norm_w, w_in_even=w_in_even, gla_w_a_up=gla_w_a_up, gla_b_a=gla_b_a, gla_norm_w=gla_norm_w,
                fox_b_f=fox_b_f, w_out_even=w_out_even, w_in_odd=w_in_odd, rel_bias=rel_bias, conv_w=conv_w, conv_b=conv_b,
                lru_w_a=lru_w_a, lru_b_a=lru_b_a, lru_w_x=lru_w_x, lru_b_x=lru_b_x, lru_lambda=lru_lambda,
                w_out_odd=w_out_odd, w_mlp_up=w_mlp_up, w_mlp_down=w_mlp_down)
    m_of = dict(norm_w=m_norm_w, w_in_even=m_w_in_even, gla_w_a_up=m_gla_w_a_up, gla_b_a=m_gla_b_a,
                gla_norm_w=m_gla_norm_w, fox_b_f=m_fox_b_f, w_out_even=m_w_out_even, w_in_odd=m_w_in_odd,
                rel_bias=m_rel_bias, conv_w=m_conv_w, conv_b=m_conv_b, lru_w_a=m_lru_w_a, lru_b_a=m_lru_b_a,
                lru_w_x=m_lru_w_x, lru_b_x=m_lru_b_x, lru_lambda=m_lru_lambda, w_out_odd=m_w_out_odd,
                w_mlp_up=m_w_mlp_up, w_mlp_down=m_w_mlp_down)
    v_of = dict(norm_w=v_norm_w, w_in_even=v_w_in_even, gla_w_a_up=v_gla_w_a_up, gla_b_a=v_gla_b_a,
                gla_norm_w=v_gla_norm_w, fox_b_f=v_fox_b_f, w_out_even=v_w_out_even, w_in_odd=v_w_in_odd,
                rel_bias=v_rel_bias, conv_w=v_conv_w, conv_b=v_conv_b, lru_w_a=v_lru_w_a, lru_b_a=v_lru_b_a,
                lru_w_x=v_lru_w_x, lru_b_x=v_lru_b_x, lru_lambda=v_lru_lambda, w_out_odd=v_w_out_odd,
                w_mlp_up=v_w_mlp_up, w_mlp_down=v_w_mlp_down)
    grads, deltas, new_ms, new_vs = [], [], [], []
    for n in names:
        w = w_of[n]
        view = w.shape if w.ndim <= 3 else w.shape[-3:]
        g = g_of[n].reshape(w.shape)
        d, mn, vn = _adamw(w.reshape(view), g.reshape(view), m_of[n].reshape(view), v_of[n].reshape(view),
                           "adamw_" + n)
        grads.append(g)
        deltas.append(d.reshape(w.shape))
        new_ms.append(mn.reshape(w.shape))
        new_vs.append(vn.reshape(w.shape))

    return (loss, grad_x.reshape(1, T, D), *grads, *deltas, *new_ms, *new_vs)
```

```python
import functools

import jax
import jax.numpy as jnp
from jax import lax
from jax.experimental import pallas as pl
from jax.experimental.pallas import tpu as pltpu

F32 = jnp.float32
BF16 = jnp.bfloat16
MESH = pl.DeviceIdType.MESH

T = 2048
D = 1024
DFF = 4096
EPS = 1e-6
CHUNK = 64
NCHUNK = T // CHUNK
PE = 3200
PO = 2560
AUX_BLK = 3072 // 128
FOX_LANE0 = 16
GLA_SCALE = 64 ** -0.5
ATT_SCALE = 64 ** -0.5
NEG = float(jnp.finfo(jnp.float32).min)
CA_BAND = 576
CA_PAD = 512
REL_PAD = 384

VMEM_LIMIT = 48 * 1024 * 1024

ADAM_LR, ADAM_B1, ADAM_B2, ADAM_EPS, ADAM_WD, ADAM_STEP = 0.001, 0.9, 0.999, 1e-08, 0.01, 10

GB_ROWS = 6144
GB_UP = (0, 1024)
GB_DN = (2048, 3072)
GB_IN_O = 4096
GB_OUT_E = 4736
GB_OUT_O = 4992
GB_TAIL = 5248

_DIMS = {"nn": (((1,), (0,)), ((), ())), "nt": (((1,), (1,)), ((), ())), "tn": (((0,), (0,)), ((), ()))}


def _cp(sem, **kw):
    return pltpu.CompilerParams(dimension_semantics=sem, vmem_limit_bytes=VMEM_LIMIT, **kw)


def _dot(a, b, mode):
    return lax.dot_general(a.astype(BF16), b.astype(BF16), _DIMS[mode], preferred_element_type=F32)


@functools.partial(jax.custom_vjp, nondiff_argnums=(2,))
def bdot(a, b, mode):
    return _dot(a, b, mode)


def _bdot_fwd(a, b, mode):
    return _dot(a, b, mode), (a, b)


def _bdot_bwd(mode, res, g):
    a, b = res
    if mode == "nn":
        da, db = _dot(g, b, "nt"), _dot(a, g, "tn")
    elif mode == "nt":
        da, db = _dot(g, b, "nn"), _dot(g, a, "tn")
    else:
        da, db = _dot(b, g, "nt"), _dot(a, g, "nn")
    return da.astype(a.dtype), db.astype(b.dtype)


bdot.defvjp(_bdot_fwd, _bdot_bwd)


def _hdot_raw(a, b, mode):
    return lax.dot_general(a, b, _DIMS[mode], precision=lax.Precision.HIGHEST, preferred_element_type=F32)


@functools.partial(jax.custom_vjp, nondiff_argnums=(2,))
def hdot(a, b, mode):
    return _hdot_raw(a, b, mode)


def _hdot_fwd(a, b, mode):
    return _hdot_raw(a, b, mode), (a, b)


def _hdot_bwd(mode, res, g):
    a, b = res
    if mode == "nn":
        return _hdot_raw(g, b, "nt"), _hdot_raw(a, g, "tn")
    if mode == "nt":
        return _hdot_raw(g, b, "nn"), _hdot_raw(g, a, "tn")
    return _hdot_raw(b, g, "nt"), _hdot_raw(a, g, "nn")


hdot.defvjp(_hdot_fwd, _hdot_bwd)


def _log_sigmoid(x):
    return jnp.minimum(x, 0.0) - jnp.log(1.0 + jnp.exp(-jnp.abs(x)))


def _sigmoid(x):
    return 1.0 / (1.0 + jnp.exp(-x))


def _expm1(x):
    series = x * (1.0 + x * 0.5 * (1.0 + x * (1.0 / 3.0) * (1.0 + x * 0.25)))
    return jnp.where(jnp.abs(x) < 0.03, series, jnp.exp(x) - 1.0)


def _gelu_tanh(x):
    return 0.5 * x * (1.0 + jnp.tanh(0.7978845608028654 * (x + 0.044715 * x * x * x)))


def _softmax_rows(s):
    m = jnp.max(s, axis=-1, keepdims=True)
    p = jnp.exp(s - m)
    return p / jnp.sum(p, axis=-1, keepdims=True)


def _iota(shape, dim):
    return lax.broadcasted_iota(jnp.int32, shape, dim)


def _mm(a, b, mode, *, tm, tn, tk=None, out_dtype=F32, name, b_layer=None, into=None):
    b2 = b.shape[-2:]
    if mode == "nn":
        (m, k), n = a.shape, b2[1]
    elif mode == "nt":
        (m, k), n = a.shape, b2[0]
    else:
        (k, m), n = a.shape, b2[1]
    tk = k if tk is None else tk
    assert m % tm == 0 and n % tn == 0 and k % tk == 0, (name, a.shape, b.shape)
    nk = k // tk
    a_spec = {"nn": pl.BlockSpec((tm, tk), lambda i, j, kk: (i, kk)),
              "nt": pl.BlockSpec((tm, tk), lambda i, j, kk: (i, kk)),
              "tn": pl.BlockSpec((tk, tm), lambda i, j, kk: (kk, i))}[mode]
    b_blk = {"nn": (tk, tn), "nt": (tn, tk), "tn": (tk, tn)}[mode]
    b_idx = {"nn": lambda i, j, kk: (kk, j), "nt": lambda i, j, kk: (j, kk), "tn": lambda i, j, kk: (kk, j)}[mode]
    if b_layer is None:
        b_spec = pl.BlockSpec(b_blk, b_idx)
    else:
        b_spec = pl.BlockSpec((None,) + b_blk, lambda i, j, kk: (b_layer,) + b_idx(i, j, kk))

    if into is None:
        out_spec = pl.BlockSpec((tm, tn), lambda i, j, kk: (i, j))
        out_shape = jax.ShapeDtypeStruct((m, n), out_dtype)
        extra_in, extra_specs, aliases = [], [], {}
    else:
        buf, per_slot, row_off = into
        assert m == 4 * per_slot and per_slot % tm == 0 and row_off % tm == 0 and buf.shape[2] == n, (name, buf.shape)
        bps = per_slot // tm
        out_dtype = buf.dtype
        out_spec = pl.BlockSpec((None, tm, tn), lambda i, j, kk: (i // bps, row_off // tm + i % bps, j))
        out_shape = jax.ShapeDtypeStruct(buf.shape, buf.dtype)
        extra_in, extra_specs, aliases = [buf], [pl.BlockSpec(memory_space=pl.ANY)], {2: 0}

    def body(*refs):
        a_ref, b_ref = refs[0], refs[1]
        o_ref = refs[2 + len(extra_in)]
        if nk == 1:
            o_ref[...] = _dot(a_ref[...], b_ref[...], mode).astype(out_dtype)
            return
        acc_ref = refs[-1]
        kk = pl.program_id(2)

        @pl.when(kk == 0)
        def _():
            acc_ref[...] = jnp.zeros_like(acc_ref)

        acc_ref[...] += _dot(a_ref[...], b_ref[...], mode)

        @pl.when(kk == nk - 1)
        def _():
            o_ref[...] = acc_ref[...].astype(out_dtype)

    return pl.pallas_call(
        body, name=name, grid=(m // tm, n // tn, nk),
        in_specs=[a_spec, b_spec] + extra_specs,
        out_specs=out_spec, out_shape=out_shape,
        scratch_shapes=[pltpu.VMEM((tm, tn), F32)] if nk > 1 else [],
        input_output_aliases=aliases,
        compiler_params=_cp(("parallel", "parallel", "arbitrary")),
    )(a, b, *extra_in)


ROWS = 256


def _prenorm(x, w, name):
    def body(x_ref, w_ref, o_ref):
        xv = x_ref[...]
        r = lax.rsqrt(jnp.mean(xv * xv, axis=-1, keepdims=True) + EPS)
        o_ref[...] = (xv * r * w_ref[...]).astype(BF16)

    return pl.pallas_call(
        body, name=name, grid=(T // ROWS,),
        in_specs=[pl.BlockSpec((ROWS, D), lambda i: (i, 0)), pl.BlockSpec((1, D), lambda i: (0, 0))],
        out_specs=pl.BlockSpec((ROWS, D), lambda i: (i, 0)),
        out_shape=jax.ShapeDtypeStruct((T, D), BF16),
        compiler_params=_cp(("parallel",)),
    )(x, w)


def _postnorm(x, z, w, name):
    def body(x_ref, z_ref, w_ref, o_ref):
        zv = z_ref[...]
        r = lax.rsqrt(jnp.mean(zv * zv, axis=-1, keepdims=True) + EPS)
        o_ref[...] = x_ref[...] + zv * r * w_ref[...]

    return pl.pallas_call(
        body, name=name, grid=(T // ROWS,),
        in_specs=[pl.BlockSpec((ROWS, D), lambda i: (i, 0)), pl.BlockSpec((ROWS, D), lambda i: (i, 0)),
                  pl.BlockSpec((1, D), lambda i: (0, 0))],
        out_specs=pl.BlockSpec((ROWS, D), lambda i: (i, 0)),
        out_shape=jax.ShapeDtypeStruct((T, D), F32),
        compiler_params=_cp(("parallel",)),
    )(x, z, w)


def _norm_bwd(z, w, dy, add, name):
    has_add = add is not None

    def body(*refs):
        if has_add:
            z_ref, w_ref, dy_ref, add_ref, dz_ref, dw_ref = refs
        else:
            z_ref, w_ref, dy_ref, dz_ref, dw_ref = refs
        i = pl.program_id(0)

        @pl.when(i == 0)
        def _():
            dw_ref[...] = jnp.zeros_like(dw_ref)

        zv = z_ref[...].astype(F32)
        dyv = dy_ref[...]
        r = lax.rsqrt(jnp.mean(zv * zv, axis=-1, keepdims=True) + EPS)
        wdy = dyv * w_ref[...]
        dz = r * wdy - zv * (r * r * r) * jnp.mean(zv * wdy, axis=-1, keepdims=True)
        if has_add:
            dz = dz + add_ref[...]
        dz_ref[...] = dz
        dw_ref[...] += jnp.sum(dyv * zv * r, axis=0, keepdims=True)

    row = pl.BlockSpec((ROWS, D), lambda i: (i, 0))
    vec = pl.BlockSpec((1, D), lambda i: (0, 0))
    ins = [z, w, dy] + ([add] if has_add else [])
    return pl.pallas_call(
        body, name=name, grid=(T // ROWS,),
        in_specs=[row, vec, row] + ([row] if has_add else []),
        out_specs=(row, vec),
        out_shape=(jax.ShapeDtypeStruct((T, D), F32), jax.ShapeDtypeStruct((1, D), F32)),
        compiler_params=_cp(("arbitrary",)),
    )(*ins)


def _relu2(u, name):
    def body(u_ref, o_ref):
        r = jnp.maximum(u_ref[...], 0.0)
        o_ref[...] = (r * r).astype(BF16)

    blk = pl.BlockSpec((ROWS, DFF), lambda i: (i, 0))
    return pl.pallas_call(body, name=name, grid=(T // ROWS,), in_specs=[blk], out_specs=blk,
                          out_shape=jax.ShapeDtypeStruct((T, DFF), BF16), compiler_params=_cp(("parallel",)))(u)


def _relu2_bwd(u, da, name):
    def body(u_ref, da_ref, o_ref):
        o_ref[...] = da_ref[...] * (2.0 * jnp.maximum(u_ref[...], 0.0))

    blk = pl.BlockSpec((ROWS, DFF), lambda i: (i, 0))
    return pl.pallas_call(body, name=name, grid=(T // ROWS,), in_specs=[blk, blk], out_specs=blk,
                          out_shape=jax.ShapeDtypeStruct((T, DFF), F32), compiler_params=_cp(("parallel",)))(u, da)


def _loss_and_grad(y, tgt):
    def body(y_ref, t_ref, g_ref, l_ref):
        i = pl.program_id(0)

        @pl.when(i == 0)
        def _():
            l_ref[...] = jnp.zeros_like(l_ref)

        e = y_ref[...] - t_ref[...]
        g_ref[...] = e * (1.0 / D)
        l_ref[...] += jnp.sum(e * e) * (0.5 / D)

    row = pl.BlockSpec((ROWS, D), lambda i: (i, 0))
    return pl.pallas_call(
        body, name="loss_head", grid=(T // ROWS,), in_specs=[row, row],
        out_specs=(row, pl.BlockSpec((1, 128), lambda i: (0, 0))),
        out_shape=(jax.ShapeDtypeStruct((T, D), F32), jax.ShapeDtypeStruct((1, 128), F32)),
        compiler_params=_cp(("arbitrary",)),
    )(y, tgt)


def _adamw(w, g, m, v, name):
    lead = w.shape[:-2]
    assert len(lead) <= 1 and g.shape == w.shape, (name, w.shape, g.shape)
    rows, cols = w.shape[-2:]
    tr = rows if rows <= 512 else 256
    assert rows % tr == 0, (name, w.shape)
    c1 = 1.0 - ADAM_B1 ** ADAM_STEP
    c2 = 1.0 - ADAM_B2 ** ADAM_STEP

    def body(w_ref, g_ref, m_ref, v_ref, d_ref, mo_ref, vo_ref):
        gv = g_ref[...]
        mn = ADAM_B1 * m_ref[...] + (1.0 - ADAM_B1) * gv
        vn = ADAM_B2 * v_ref[...] + (1.0 - ADAM_B2) * (gv * gv)
        m_hat = mn / c1
        v_hat = vn / c2
        d_ref[...] = -ADAM_LR * (m_hat / (jnp.sqrt(v_hat) + ADAM_EPS) + ADAM_WD * w_ref[...])
        mo_ref[...] = mn
        vo_ref[...] = vn

    if lead:
        grid = (lead[0], rows // tr)
        blk = pl.BlockSpec((None, tr, cols), lambda l, i: (l, i, 0))
    else:
        grid = (rows // tr,)
        blk = pl.BlockSpec((tr, cols), lambda i: (i, 0))
    sds = jax.ShapeDtypeStruct(w.shape, F32)
    return pl.pallas_call(body, name=name, grid=grid, in_specs=[blk] * 4, out_specs=(blk,) * 3,
                          out_shape=(sds,) * 3, compiler_params=_cp(("parallel",) * len(grid)))(w, g, m, v)


def _gla_consts():
    ltri = (_iota((CHUNK, CHUNK), 0) >= _iota((CHUNK, CHUNK), 1)).astype(F32)
    ones_c = jnp.ones((CHUNK, 128), F32)
    mask = (_iota((256, 512), 0) // 64 == _iota((256, 512), 1) // 128).astype(F32)
    return ltri, ones_c, mask


def _gla_chunk(consts, q, k, v, r, aux, s_prev, wa, ba, nw):
    ltri, ones_c, mask = consts
    la = _log_sigmoid(bdot(aux, wa, "nn") + ba) * (1.0 / 16.0)
    cum = hdot(ltri, la, "nn")
    total = jnp.sum(la, axis=0, keepdims=True)
    k_dec = k * jnp.exp(total - cum)
    inc = bdot(k_dec, v, "tn") * mask
    dec = jnp.exp(hdot(la, ones_c, "tn"))
    dec = jnp.concatenate([dec, dec, dec, dec], axis=1)
    s_new = dec * s_prev + inc
    o = bdot(q * GLA_SCALE, s_new, "nn")
    parts = []
    for h in range(4):
        oh = o[:, h * 128:(h + 1) * 128]
        parts.append(oh * lax.rsqrt(jnp.mean(oh * oh, axis=-1, keepdims=True) + EPS))
    on = jnp.concatenate(parts, axis=1)
    return s_new, on * nw * (r * _sigmoid(r))


def _gla_specs(cmap):
    return [pl.BlockSpec((CHUNK, 256), lambda c: (cmap(c), 0)),
            pl.BlockSpec((CHUNK, 256), lambda c: (cmap(c), 1)),
            pl.BlockSpec((CHUNK, 512), lambda c: (cmap(c), 1)),
            pl.BlockSpec((CHUNK, 512), lambda c: (cmap(c), 2)),
            pl.BlockSpec((CHUNK, 128), lambda c: (cmap(c), AUX_BLK))]


def _gla_fwd(proj, wa, ba, nw):
    def body(q_ref, k_ref, v_ref, r_ref, aux_ref, wa_ref, ba_ref, nw_ref, o_ref, sp_ref, s_ref):
        c = pl.program_id(0)

        @pl.when(c == 0)
        def _():
            s_ref[...] = jnp.zeros_like(s_ref)

        s_prev = s_ref[...]
        sp_ref[...] = s_prev
        s_new, out = _gla_chunk(_gla_consts(), q_ref[...], k_ref[...], v_ref[...], r_ref[...], aux_ref[...],
                                s_prev, wa_ref[...], ba_ref[...], nw_ref[...])
        s_ref[...] = s_new
        o_ref[...] = out

    full = lambda shape: pl.BlockSpec(shape, lambda c: (0,) * len(shape))
    return pl.pallas_call(
        body, name="gla_fwd", grid=(NCHUNK,),
        in_specs=_gla_specs(lambda c: c) + [full((128, 256)), full((1, 256)), full((1, 512))],
        out_specs=(pl.BlockSpec((CHUNK, 512), lambda c: (c, 0)), pl.BlockSpec((None, 256, 512), lambda c: (c, 0, 0))),
        out_shape=(jax.ShapeDtypeStruct((T, D), F32), jax.ShapeDtypeStruct((NCHUNK, 256, 512), F32)),
        scratch_shapes=[pltpu.VMEM((256, 512), F32)],
        compiler_params=_cp(("arbitrary",)),
    )(proj, proj, proj, proj, proj, wa, ba, nw)


def _gla_bwd(proj, s_prev_all, wa, ba, nw, dcat):
    rev = lambda c: NCHUNK - 1 - c

    def body(q_ref, k_ref, v_ref, r_ref, aux_ref, sp_ref, wa_ref, ba_ref, nw_ref, do_ref,
             dq_ref, dk_ref, dv_ref, dr_ref, daux_ref, dwa_ref, dba_ref, dnw_ref, ds_ref):
        c = pl.program_id(0)

        @pl.when(c == 0)
        def _():
            ds_ref[...] = jnp.zeros_like(ds_ref)
            dwa_ref[...] = jnp.zeros_like(dwa_ref)
            dba_ref[...] = jnp.zeros_like(dba_ref)
            dnw_ref[...] = jnp.zeros_like(dnw_ref)

        fn = functools.partial(_gla_chunk, _gla_consts())
        _, vjp = jax.vjp(fn, q_ref[...], k_ref[...], v_ref[...], r_ref[...], aux_ref[...], sp_ref[...],
                         wa_ref[...], ba_ref[...], nw_ref[...])
        dq, dk, dv, dr, daux, dsp, dwa, dba, dnw = vjp((ds_ref[...], do_ref[...]))
        dq_ref[...] = dq
        dk_ref[...] = dk
        dv_ref[...] = dv
        dr_ref[...] = dr
        daux_ref[...] = daux
        ds_ref[...] = dsp
        dwa_ref[...] += dwa
        dba_ref[...] += dba
        dnw_ref[...] += dnw

    full = lambda shape: pl.BlockSpec(shape, lambda c: (0,) * len(shape))
    blk = lambda w: pl.BlockSpec((CHUNK, w), lambda c: (rev(c), 0))
    sds = lambda *s: jax.ShapeDtypeStruct(s, F32)
    return pl.pallas_call(
        body, name="gla_bwd", grid=(NCHUNK,),
        in_specs=_gla_specs(rev) + [pl.BlockSpec((None, 256, 512), lambda c: (rev(c), 0, 0)),
                                    full((128, 256)), full((1, 256)), full((1, 512)), blk(512)],
        out_specs=(blk(256), blk(256), blk(512), blk(512), blk(128), full((128, 256)), full((1, 256)), full((1, 512))),
        out_shape=(sds(T, 256), sds(T, 256), sds(T, 512), sds(T, 512), sds(T, 128),
                   sds(128, 256), sds(1, 256), sds(1, 512)),
        scratch_shapes=[pltpu.VMEM((256, 512), F32)],
        compiler_params=_cp(("arbitrary",)),
    )(proj, proj, proj, proj, proj, s_prev_all, wa, ba, nw, dcat)


GATE_ROWS = 128


def _fox_gate_block(ltri, aux, bpad, carry):
    lf = _log_sigmoid(aux + bpad)
    cum = hdot(ltri, lf, "nn") + carry
    return cum, carry + jnp.sum(lf, axis=0, keepdims=True)


def _gate_ltri():
    return (_iota((GATE_ROWS, GATE_ROWS), 0) >= _iota((GATE_ROWS, GATE_ROWS), 1)).astype(F32)


def _fox_gate_fwd(proj, bpad):
    def body(aux_ref, b_ref, cum_ref, carry_ref):
        i = pl.program_id(0)

        @pl.when(i == 0)
        def _():
            carry_ref[...] = jnp.zeros_like(carry_ref)

        cum, carry = _fox_gate_block(_gate_ltri(), aux_ref[...], b_ref[...], carry_ref[...])
        cum_ref[...] = cum
        carry_ref[...] = carry

    return pl.pallas_call(
        body, name="fox_gate_fwd", grid=(T // GATE_ROWS,),
        in_specs=[pl.BlockSpec((GATE_ROWS, 128), lambda i: (i, AUX_BLK)), pl.BlockSpec((1, 128), lambda i: (0, 0))],
        out_specs=pl.BlockSpec((GATE_ROWS, 128), lambda i: (i, 0)),
        out_shape=jax.ShapeDtypeStruct((T, 128), F32),
        scratch_shapes=[pltpu.VMEM((1, 128), F32)],
        compiler_params=_cp(("arbitrary",)),
    )(proj, bpad)


def _fox_gate_bwd(proj, bpad, dcrow, dccol_t, daux_gla):
    nb = T // GATE_ROWS
    rev = lambda i: nb - 1 - i

    def body(aux_ref, b_ref, dr_ref, dc_ref, dg_ref, daux_ref, db_ref, dcarry_ref):
        i = pl.program_id(0)

        @pl.when(i == 0)
        def _():
            dcarry_ref[...] = jnp.zeros_like(dcarry_ref)
            db_ref[...] = jnp.zeros_like(db_ref)

        dcum = dr_ref[0] + dr_ref[1] + dr_ref[2] + dr_ref[3] + dc_ref[...]
        fn = functools.partial(_fox_gate_block, _gate_ltri())
        _, vjp = jax.vjp(fn, aux_ref[...], b_ref[...], jnp.zeros((1, 128), F32))
        daux, db, dcarry = vjp((dcum, dcarry_ref[...]))
        daux_ref[...] = daux + dg_ref[...]
        db_ref[...] += db
        dcarry_ref[...] = dcarry

    blk = pl.BlockSpec((GATE_ROWS, 128), lambda i: (rev(i), 0))
    vec = pl.BlockSpec((1, 128), lambda i: (0, 0))
    return pl.pallas_call(
        body, name="fox_gate_bwd", grid=(nb,),
        in_specs=[pl.BlockSpec((GATE_ROWS, 128), lambda i: (rev(i), AUX_BLK)), vec,
                  pl.BlockSpec((4, GATE_ROWS, 128), lambda i: (0, rev(i), 0)), blk, blk],
        out_specs=(blk, vec),
        out_shape=(jax.ShapeDtypeStruct((T, 128), F32), jax.ShapeDtypeStruct((1, 128), F32)),
        scratch_shapes=[pltpu.VMEM((1, 128), F32)],
        compiler_params=_cp(("arbitrary",)),
    )(proj, bpad, dcrow, dccol_t, daux_gla)


FOX_Q = 128


FOX_KEY_STEP = 512
FOX_GROUPS = T // FOX_KEY_STEP
FOX_Q_PER_GROUP = FOX_KEY_STEP // FOX_Q


def _fox_block(hp, qb, q, k, v, crow, ccol):
    kl = k.shape[0]
    lane = _iota((FOX_Q, 128), 1)
    causal = (qb * FOX_Q + _iota((FOX_Q, kl), 0)) >= _iota((FOX_Q, kl), 1)
    sub = _iota((8, kl), 0)
    outs = []
    for e in range(2):
        h = 2 * hp + e
        qm = jnp.where((lane >= 64 * e) & (lane < 64 * (e + 1)), q, 0.0)
        s = bdot(qm, k, "nt") * ATT_SCALE
        ct = jnp.sum(jnp.where(lane == FOX_LANE0 + h, crow, 0.0), axis=1, keepdims=True)
        cs = jnp.sum(jnp.where(sub == h, ccol, 0.0), axis=0, keepdims=True)
        s = jnp.where(causal, s + (ct - cs), NEG)
        outs.append(bdot(_softmax_rows(s), v, "nn"))
    return jnp.where(lane < 64, outs[0], outs[1])


def _fox_in_specs():
    return [pl.BlockSpec((FOX_Q, 128), lambda hp, qb: (qb, 12 + hp)),
            pl.BlockSpec((T, 128), lambda hp, qb: (0, 16 + hp)),
            pl.BlockSpec((T, 128), lambda hp, qb: (0, 20 + hp)),
            pl.BlockSpec((FOX_Q, 128), lambda hp, qb: (qb, 0)),
            pl.BlockSpec((8, T), lambda hp, qb: (0, 0))]


def _fox_fwd(proj, cum_r, cum_c, cat):
    def body(q_ref, k_ref, v_ref, cr_ref, cc_ref, cat_ref, o_ref):
        qb = pl.program_id(1)
        for g in range(FOX_GROUPS):
            kl = FOX_KEY_STEP * (g + 1)

            @pl.when(qb // FOX_Q_PER_GROUP == g)
            def _(kl=kl):
                o_ref[...] = _fox_block(pl.program_id(0), qb, q_ref[...], k_ref[0:kl, :], v_ref[0:kl, :],
                                        cr_ref[...], cc_ref[:, 0:kl])

    return pl.pallas_call(
        body, name="fox_fwd", grid=(4, T // FOX_Q), in_specs=_fox_in_specs() + [pl.BlockSpec(memory_space=pl.ANY)],
        out_specs=pl.BlockSpec((FOX_Q, 128), lambda hp, qb: (qb, 4 + hp)),
        out_shape=jax.ShapeDtypeStruct((T, D), F32), input_output_aliases={5: 0},
        compiler_params=_cp(("parallel", "parallel")),
    )(proj, proj, proj, cum_r, cum_c, cat)


def _fox_bwd(proj, cum_r, cum_c, dcat):
    def body(q_ref, k_ref, v_ref, cr_ref, cc_ref, do_ref, dq_ref, dk_ref, dv_ref, dcr_ref, dcc_ref):
        qb = pl.program_id(1)

        @pl.when(qb == 0)
        def _():
            dk_ref[...] = jnp.zeros_like(dk_ref)
            dv_ref[...] = jnp.zeros_like(dv_ref)
            dcc_ref[...] = jnp.zeros_like(dcc_ref)

        fn = functools.partial(_fox_block, pl.program_id(0), qb)
        for g in range(FOX_GROUPS):
            kl = FOX_KEY_STEP * (g + 1)

            @pl.when(qb // FOX_Q_PER_GROUP == g)
            def _(kl=kl):
                _, vjp = jax.vjp(fn, q_ref[...], k_ref[0:kl, :], v_ref[0:kl, :], cr_ref[...], cc_ref[:, 0:kl])
                dq, dk, dv, dcr, dcc = vjp(do_ref[...])
                dq_ref[...] = dq
                dk_ref[0:kl, :] += dk
                dv_ref[0:kl, :] += dv
                dcr_ref[...] = dcr
                dcc_ref[:, 0:kl] += dcc

    sds = lambda *s: jax.ShapeDtypeStruct(s, F32)
    return pl.pallas_call(
        body, name="fox_bwd", grid=(4, T // FOX_Q),
        in_specs=_fox_in_specs() + [pl.BlockSpec((FOX_Q, 128), lambda hp, qb: (qb, 4 + hp))],
        out_specs=(pl.BlockSpec((FOX_Q, 128), lambda hp, qb: (qb, hp)),
                   pl.BlockSpec((T, 128), lambda hp, qb: (0, hp)),
                   pl.BlockSpec((T, 128), lambda hp, qb: (0, hp)),
                   pl.BlockSpec((None, FOX_Q, 128), lambda hp, qb: (hp, qb, 0)),
                   pl.BlockSpec((None, 8, T), lambda hp, qb: (hp, 0, 0))),
        out_shape=(sds(T, 512), sds(T, 512), sds(T, 512), sds(4, T, 128), sds(4, 8, T)),
        compiler_params=_cp(("parallel", "arbitrary")),
    )(proj, proj, proj, cum_r, cum_c, dcat)


def _rel_onehot(q):
    kj = _iota((REL_PAD, CA_BAND), 1)
    rel = jnp.clip(CA_PAD + q - kj, -128, 128) + 128
    return (_iota((REL_PAD, CA_BAND), 0) == rel).astype(F32)


def _bias_build(rbp):
    def body(rb_ref, o_ref):
        o_ref[...] = _hdot_raw(rb_ref[...], _rel_onehot(pl.program_id(0)), "nn")

    return pl.pallas_call(
        body, name="ca_bias_build", grid=(CHUNK,),
        in_specs=[pl.BlockSpec((8, REL_PAD), lambda q: (0, 0))],
        out_specs=pl.BlockSpec((None, 8, CA_BAND), lambda q: (q, 0, 0)),
        out_shape=jax.ShapeDtypeStruct((CHUNK, 8, CA_BAND), F32),
        compiler_params=_cp(("parallel",)),
    )(rbp)


def _bias_grad(dbias_q):
    def body(db_ref, o_ref):
        q = pl.program_id(0)

        @pl.when(q == 0)
        def _():
            o_ref[...] = jnp.zeros_like(o_ref)

        o_ref[...] += _hdot_raw(db_ref[...], _rel_onehot(q), "nt")

    return pl.pallas_call(
        body, name="ca_bias_grad", grid=(CHUNK,),
        in_specs=[pl.BlockSpec((None, 8, CA_BAND), lambda q: (q, 0, 0))],
        out_specs=pl.BlockSpec((8, REL_PAD), lambda q: (0, 0)),
        out_shape=jax.ShapeDtypeStruct((8, REL_PAD), F32),
        compiler_params=_cp(("arbitrary",)),
    )(dbias_q)


def _ca_block(c, q, kb, vb, bias2):
    lane = _iota((CHUNK, 128), 1)
    valid = (c * CHUNK - CA_PAD + _iota((CHUNK, CA_BAND), 1)) >= 0
    outs = []
    for e in range(2):
        qm = jnp.where((lane >= 64 * e) & (lane < 64 * (e + 1)), q, 0.0)
        s = bdot(qm, kb, "nt") * ATT_SCALE
        s = jnp.where(valid, s + bias2[e], NEG)
        outs.append(bdot(_softmax_rows(s), vb, "nn"))
    return jnp.where(lane < 64, outs[0], outs[1])


def _ca_fwd(proj, kvpad, bias):
    def body(q_ref, k_ref, v_ref, b_ref, o_ref):
        c = pl.program_id(1)
        band = pl.ds(pl.multiple_of(c * CHUNK, CHUNK), CA_BAND)
        o_ref[...] = _ca_block(c, q_ref[...], k_ref[band, :], v_ref[band, :], b_ref[...])

    return pl.pallas_call(
        body, name="ca_fwd", grid=(4, NCHUNK),
        in_specs=[pl.BlockSpec((CHUNK, 128), lambda hp, c: (c, hp)),
                  pl.BlockSpec((T + CA_PAD, 128), lambda hp, c: (0, hp)),
                  pl.BlockSpec((T + CA_PAD, 128), lambda hp, c: (0, 4 + hp)),
                  pl.BlockSpec((2, CHUNK, CA_BAND), lambda hp, c: (hp, 0, 0))],
        out_specs=pl.BlockSpec((CHUNK, 128), lambda hp, c: (c, hp)),
        out_shape=jax.ShapeDtypeStruct((T, D), F32),
        compiler_params=_cp(("parallel", "parallel")),
    )(proj, kvpad, kvpad, bias)


def _ca_bwd(proj, kvpad, bias, dcat):
    def body(q_ref, k_ref, v_ref, b_ref, do_ref, dq_ref, dk_ref, dv_ref, db_ref):
        c = pl.program_id(1)

        @pl.when(c == 0)
        def _():
            dk_ref[...] = jnp.zeros_like(dk_ref)
            dv_ref[...] = jnp.zeros_like(dv_ref)
            db_ref[...] = jnp.zeros_like(db_ref)

        band = pl.ds(pl.multiple_of(c * CHUNK, CHUNK), CA_BAND)
        fn = functools.partial(_ca_block, c)
        _, vjp = jax.vjp(fn, q_ref[...], k_ref[band, :], v_ref[band, :], b_ref[...])
        dq, dkb, dvb, db = vjp(do_ref[...])
        dq_ref[...] = dq
        dk_ref[band, :] += dkb
        dv_ref[band, :] += dvb
        db_ref[...] += db

    sds = lambda *s: jax.ShapeDtypeStruct(s, F32)
    padded = lambda: pl.BlockSpec((T + CA_PAD, 128), lambda hp, c: (0, hp))
    return pl.pallas_call(
        body, name="ca_bwd", grid=(4, NCHUNK),
        in_specs=[pl.BlockSpec((CHUNK, 128), lambda hp, c: (c, hp)),
                  pl.BlockSpec((T + CA_PAD, 128), lambda hp, c: (0, hp)),
                  pl.BlockSpec((T + CA_PAD, 128), lambda hp, c: (0, 4 + hp)),
                  pl.BlockSpec((2, CHUNK, CA_BAND), lambda hp, c: (hp, 0, 0)),
                  pl.BlockSpec((CHUNK, 128), lambda hp, c: (c, hp))],
        out_specs=(pl.BlockSpec((CHUNK, 128), lambda hp, c: (c, hp)), padded(), padded(),
                   pl.BlockSpec((2, CHUNK, CA_BAND), lambda hp, c: (hp, 0, 0))),
        out_shape=(sds(T, 512), sds(T + CA_PAD, 512), sds(T + CA_PAD, 512), sds(8, CHUNK, CA_BAND)),
        compiler_params=_cp(("parallel", "arbitrary")),
    )(proj, kvpad, kvpad, bias, dcat)


def _lru_pre(xs, cw, cb, wa, ba, wx, bx, lam):
    xc = cb + xs[0] * cw[0:1, :] + xs[1] * cw[1:2, :] + xs[2] * cw[2:3, :] + xs[3] * cw[3:4, :]
    ra = _sigmoid(bdot(xc, wa, "nn") + ba)
    ii = _sigmoid(bdot(xc, wx, "nn") + bx)
    la = 8.0 * ra * _log_sigmoid(lam)
    return jnp.exp(la), jnp.sqrt(-_expm1(2.0 * la)) * (ii * xc)


def _lru_pre_specs():
    full = lambda shape: pl.BlockSpec(shape, lambda i: (0,) * len(shape))
    return [pl.BlockSpec((4, ROWS, 512), lambda i: (0, i, 0)), full((4, 512)), full((1, 512)),
            full((512, 512)), full((1, 512)), full((512, 512)), full((1, 512)), full((1, 512))]


def _lru_pre_fwd(xs, cw, cb, wa, ba, wx, bx, lam):
    def body(xs_ref, cw_ref, cb_ref, wa_ref, ba_ref, wx_ref, bx_ref, lam_ref, a_ref, b_ref):
        a, b = _lru_pre(xs_ref[...], cw_ref[...], cb_ref[...], wa_ref[...], ba_ref[...], wx_ref[...], bx_ref[...],
                        lam_ref[...])
        a_ref[...] = a
        b_ref[...] = b

    row = pl.BlockSpec((ROWS, 512), lambda i: (i, 0))
    sds = jax.ShapeDtypeStruct((T, 512), F32)
    return pl.pallas_call(body, name="lru_pre_fwd", grid=(T // ROWS,), in_specs=_lru_pre_specs(),
                          out_specs=(row, row), out_shape=(sds, sds), compiler_params=_cp(("parallel",)),
                          )(xs, cw, cb, wa, ba, wx, bx, lam)


def _lru_pre_bwd(xs, cw, cb, wa, ba, wx, bx, lam, da, db):
    def body(xs_ref, cw_ref, cb_ref, wa_ref, ba_ref, wx_ref, bx_ref, lam_ref, da_ref, db_ref,
             dxs_ref, dcw_ref, dcb_ref, dwa_ref, dba_ref, dwx_ref, dbx_ref, dlam_ref):
        acc = (dcw_ref, dcb_ref, dwa_ref, dba_ref, dwx_ref, dbx_ref, dlam_ref)

        @pl.when(pl.program_id(0) == 0)
        def _():
            for r in acc:
                r[...] = jnp.zeros_like(r)

        _, vjp = jax.vjp(_lru_pre, xs_ref[...], cw_ref[...], cb_ref[...], wa_ref[...], ba_ref[...], wx_ref[...],
                         bx_ref[...], lam_ref[...])
        grads = vjp((da_ref[...], db_ref[...]))
        dxs_ref[...] = grads[0]
        for r, g in zip(acc, grads[1:]):
            r[...] += g

    row = pl.BlockSpec((ROWS, 512), lambda i: (i, 0))
    specs = _lru_pre_specs()
    sds = lambda *s: jax.ShapeDtypeStruct(s, F32)
    return pl.pallas_call(
        body, name="lru_pre_bwd", grid=(T // ROWS,), in_specs=specs + [row, row], out_specs=tuple(specs),
        out_shape=(sds(4, T, 512), sds(4, 512), sds(1, 512), sds(512, 512), sds(1, 512), sds(512, 512), sds(1, 512),
                   sds(1, 512)),
        compiler_params=_cp(("arbitrary",)),
    )(xs, cw, cb, wa, ba, wx, bx, lam, da, db)


def _lru_scan_fwd(a, b):
    def body(a_ref, b_ref, h_ref):
        def step(t, h):
            h = a_ref[pl.ds(t, 1), :] * h + b_ref[pl.ds(t, 1), :]
            h_ref[pl.ds(t, 1), :] = h
            return h

        lax.fori_loop(0, T, step, jnp.zeros((1, 512), F32))

    return pl.pallas_call(body, name="lru_scan_fwd", out_shape=jax.ShapeDtypeStruct((T, 512), F32),
                          compiler_params=pltpu.CompilerParams(vmem_limit_bytes=VMEM_LIMIT))(a, b)


def _lru_scan_bwd(a, h, dh):
    def body(a_ref, h_ref, dh_ref, da_ref, db_ref):
        def step(i, carry):
            t = T - 1 - i
            g = dh_ref[pl.ds(t, 1), :] + carry
            db_ref[pl.ds(t, 1), :] = g
            da_ref[pl.ds(t, 1), :] = g * h_ref[pl.ds(t - 1, 1), :]
            return a_ref[pl.ds(t, 1), :] * g

        carry = lax.fori_loop(0, T - 1, step, jnp.zeros((1, 512), F32))
        db_ref[pl.ds(0, 1), :] = dh_ref[pl.ds(0, 1), :] + carry
        da_ref[pl.ds(0, 1), :] = jnp.zeros((1, 512), F32)

    sds = jax.ShapeDtypeStruct((T, 512), F32)
    return pl.pallas_call(body, name="lru_scan_bwd", out_shape=(sds, sds),
                          compiler_params=pltpu.CompilerParams(vmem_limit_bytes=VMEM_LIMIT))(a, h, dh)


def _lru_post(h, gate):
    return h * _gelu_tanh(gate)


def _lru_post_fwd(h, proj, cat):
    def body(h_ref, g_ref, cat_ref, o_ref):
        o_ref[...] = _lru_post(h_ref[...], g_ref[...])

    row = pl.BlockSpec((ROWS, 512), lambda i: (i, 0))
    return pl.pallas_call(body, name="lru_post_fwd", grid=(T // ROWS,),
                          in_specs=[row, pl.BlockSpec((ROWS, 512), lambda i: (i, 3)), pl.BlockSpec(memory_space=pl.ANY)],
                          out_specs=pl.BlockSpec((ROWS, 512), lambda i: (i, 1)),
                          out_shape=jax.ShapeDtypeStruct((T, D), F32), input_output_aliases={2: 0},
                          compiler_params=_cp(("parallel",)))(h, proj, cat)


def _lru_post_bwd(h, proj, dcat):
    def body(h_ref, g_ref, do_ref, dh_ref, dg_ref):
        _, vjp = jax.vjp(_lru_post, h_ref[...], g_ref[...])
        dh, dg = vjp(do_ref[...])
        dh_ref[...] = dh
        dg_ref[...] = dg

    row = pl.BlockSpec((ROWS, 512), lambda i: (i, 0))
    sds = jax.ShapeDtypeStruct((T, 512), F32)
    return pl.pallas_call(body, name="lru_post_bwd", grid=(T // ROWS,),
                          in_specs=[row, pl.BlockSpec((ROWS, 512), lambda i: (i, 3)),
                                    pl.BlockSpec((ROWS, 512), lambda i: (i, 1))],
                          out_specs=(row, row), out_shape=(sds, sds), compiler_params=_cp(("parallel",)))(h, proj, dcat)


def _conv_dx(dxs_shift):
    def body(d_ref, o_ref):
        o_ref[...] = d_ref[0] + d_ref[1] + d_ref[2] + d_ref[3]

    row = pl.BlockSpec((ROWS, 512), lambda i: (i, 0))
    return pl.pallas_call(body, name="lru_conv_dx", grid=(T // ROWS,),
                          in_specs=[pl.BlockSpec((4, ROWS, 512), lambda i: (0, i, 0))], out_specs=row,
                          out_shape=jax.ShapeDtypeStruct((T, 512), F32), compiler_params=_cp(("parallel",)))(dxs_shift)


def _position():
    return lax.axis_index("x"), lax.axis_index("y"), lax.axis_index("c")


def _other_chips(x, y):
    return [(1 - x, y), (x, 1 - y), (1 - x, 1 - y)]


def _al(v, n):
    return v * n if isinstance(v, int) else pl.multiple_of(v * n, n)


_AG_ITEMS = [
    ((4, 32, 128), lambda o, s, h: o.at[s, pl.ds(_al(h, 16), 16), :], lambda r, h: r.at[pl.ds(_al(h, 16), 16), :]),
    ((4, 1024, 774), lambda o, s, h: o.at[s, pl.ds(_al(h, 512), 512), :], lambda r, h: r.at[pl.ds(_al(h, 512), 512), :]),
    ((1024, 1024), lambda o, s, h: o.at[pl.ds(_al(2 * s + h, 128), 128), :], lambda r, h: r.at[pl.ds(_al(h, 128), 128), :]),
    ((2, 1024, 4096), lambda o, s, h: o.at[h, :, pl.ds(_al(s, 1024), 1024)], lambda r, h: r.at[h]),
    ((2, 4096, 1024), lambda o, s, h: o.at[h, pl.ds(_al(s, 1024), 1024), :], lambda r, h: r.at[h]),
    ((1024, 2560), lambda o, s, h: o.at[pl.ds(_al(h, 512), 512), pl.ds(_al(s, 640), 640)],
     lambda r, h: r.at[pl.ds(_al(h, 512), 512), :]),
    ((1024, 1024), lambda o, s, h: o.at[pl.ds(_al(2 * s + h, 128), 128), :], lambda r, h: r.at[pl.ds(_al(h, 128), 128), :]),
]


def _allgather_weights(shards):
    n = len(_AG_ITEMS)

    def body(*refs):
        srcs, outs = refs[:n], refs[n:2 * n]
        send_sems, recv_sems, local_sems = refs[2 * n:]
        x, y, c = _position()
        sibling = (x, y, 1 - c)
        chips = _other_chips(x, y)
        me = 2 * x + y
        waits = []
        for i, (_, dst, half) in enumerate(_AG_ITEMS):
            out_ref, src_ref = outs[i], srcs[i]

            def copy(k, slot, hc, to, src=None, dst=dst, out_ref=out_ref, i=i):
                there = dst(out_ref, slot, hc)
                return pltpu.make_async_remote_copy(
                    src_ref=there if src is None else src, dst_ref=there, send_sem=send_sems.at[6 * i + k],
                    recv_sem=recv_sems.at[6 * i + k], device_id=to, device_id_type=MESH)

            for hc in range(2):
                mine = pltpu.make_async_copy(half(src_ref, hc), dst(out_ref, me, hc), local_sems.at[2 * i + hc])
                mine.start()
                waits.append(mine.wait)
            first = [copy(j, me, c, (*chip, c), src=half(src_ref, c)) for j, chip in enumerate(chips)]
            for cp in first:
                cp.start()
                waits.append(cp.wait_send)
        for i in range(n):
            dst, out_ref = _AG_ITEMS[i][1], outs[i]

            def copy(k, slot, hc, to, dst=dst, out_ref=out_ref, i=i):
                there = dst(out_ref, slot, hc)
                return pltpu.make_async_remote_copy(
                    src_ref=there, dst_ref=there, send_sem=send_sems.at[6 * i + k], recv_sem=recv_sems.at[6 * i + k],
                    device_id=to, device_id_type=MESH)

            for j, chip in enumerate(chips):
                slot = 2 * chip[0] + chip[1]
                copy(j, slot, c, (x, y, c)).wait_recv()
                passed = copy(3 + j, slot, c, sibling)
                passed.start()
                waits.append(passed.wait_send)
        for i in range(n):
            dst, out_ref = _AG_ITEMS[i][1], outs[i]
            for j, chip in enumerate(chips):
                there = dst(out_ref, 2 * chip[0] + chip[1], 1 - c)
                pltpu.make_async_remote_copy(
                    src_ref=there, dst_ref=there, send_sem=send_sems.at[6 * i + 3 + j],
                    recv_sem=recv_sems.at[6 * i + 3 + j], device_id=(x, y, c), device_id_type=MESH).wait_recv()
        for w in waits:
            w()

    any_spec = pl.BlockSpec(memory_space=pl.ANY)
    return pl.pallas_call(
        body, name="allgather_weights",
        in_specs=[any_spec] * n, out_specs=(any_spec,) * n,
        out_shape=tuple(jax.ShapeDtypeStruct(shape, s.dtype) for (shape, _, _), s in zip(_AG_ITEMS, shards)),
        scratch_shapes=[pltpu.SemaphoreType.DMA((6 * n,)), pltpu.SemaphoreType.DMA((6 * n,)),
                        pltpu.SemaphoreType.DMA((2 * n,))],
    )(*shards)


def _pair_swap_cols(gb):
    _, rows, cols = gb.shape
    hc = cols // 2

    def body(g_ref, out_ref, send_sem, recv_sem):
        x, y, c = _position()
        cp = pltpu.make_async_remote_copy(src_ref=g_ref.at[:, :, pl.ds(_al(1 - c, hc), hc)], dst_ref=out_ref,
                                          send_sem=send_sem, recv_sem=recv_sem, device_id=(x, y, 1 - c),
                                          device_id_type=MESH)
        cp.start()
        cp.wait()

    return pl.pallas_call(
        body, name="grad_pair_swap",
        in_specs=[pl.BlockSpec(memory_space=pl.ANY)], out_specs=pl.BlockSpec(memory_space=pl.ANY),
        out_shape=jax.ShapeDtypeStruct((4, rows, hc), gb.dtype),
        scratch_shapes=[pltpu.SemaphoreType.DMA, pltpu.SemaphoreType.DMA],
    )(gb)


def _handover(half):
    rows, hc = half.shape

    def body(h_ref, out_ref, send_sem, recv_sem, local_sem):
        x, y, c = _position()
        mine = pltpu.make_async_copy(h_ref, out_ref.at[c], local_sem)
        mine.start()
        cp = pltpu.make_async_remote_copy(src_ref=h_ref, dst_ref=out_ref.at[c], send_sem=send_sem, recv_sem=recv_sem,
                                          device_id=(x, y, 1 - c), device_id_type=MESH)
        cp.start()
        theirs = out_ref.at[1 - c]
        pltpu.make_async_remote_copy(src_ref=theirs, dst_ref=theirs, send_sem=send_sem, recv_sem=recv_sem,
                                     device_id=(x, y, c), device_id_type=MESH).wait_recv()
        cp.wait_send()
        mine.wait()

    return pl.pallas_call(
        body, name="grad_handover",
        in_specs=[pl.BlockSpec(memory_space=pl.ANY)], out_specs=pl.BlockSpec(memory_space=pl.ANY),
        out_shape=jax.ShapeDtypeStruct((2, rows, hc), half.dtype),
        scratch_shapes=[pltpu.SemaphoreType.DMA, pltpu.SemaphoreType.DMA, pltpu.SemaphoreType.DMA],
    )(half)


def _chip_alltoall(p):
    def body(p_ref, q_ref, send_sems, recv_sems, local_sem):
        x, y, c = _position()
        me = 2 * x + y
        chips = _other_chips(x, y)
        mine = pltpu.make_async_copy(p_ref.at[me], q_ref.at[me], local_sem)
        mine.start()
        sends = [pltpu.make_async_remote_copy(
            src_ref=p_ref.at[2 * chip[0] + chip[1]], dst_ref=q_ref.at[me], send_sem=send_sems.at[j],
            recv_sem=recv_sems.at[j], device_id=(*chip, c), device_id_type=MESH) for j, chip in enumerate(chips)]
        for cp in sends:
            cp.start()
        for j, chip in enumerate(chips):
            slot = q_ref.at[2 * chip[0] + chip[1]]
            pltpu.make_async_remote_copy(src_ref=slot, dst_ref=slot, send_sem=send_sems.at[j], recv_sem=recv_sems.at[j],
                                         device_id=(x, y, c), device_id_type=MESH).wait_recv()
        for cp in sends:
            cp.wait_send()
        mine.wait()

    return pl.pallas_call(
        body, name="grad_alltoall",
        in_specs=[pl.BlockSpec(memory_space=pl.ANY)], out_specs=pl.BlockSpec(memory_space=pl.ANY),
        out_shape=jax.ShapeDtypeStruct(p.shape, p.dtype),
        scratch_shapes=[pltpu.SemaphoreType.DMA((3,)), pltpu.SemaphoreType.DMA((3,)), pltpu.SemaphoreType.DMA],
    )(p)


COMM_ROWS = 512


def _pair_add(gb, recv, c_idx):
    _, rows, cols = gb.shape
    hc = cols // 2

    def body(c_ref, g_ref, r_ref, o_ref):
        o_ref[...] = (g_ref[...].astype(F32) + r_ref[...].astype(F32)).astype(o_ref.dtype)

    return pl.pallas_call(
        body, name="grad_pair_add",
        grid_spec=pltpu.PrefetchScalarGridSpec(
            num_scalar_prefetch=1, grid=(4, rows // COMM_ROWS),
            in_specs=[pl.BlockSpec((None, COMM_ROWS, hc), lambda s, j, c_ref: (s, j, c_ref[0])),
                      pl.BlockSpec((None, COMM_ROWS, hc), lambda s, j, c_ref: (s, j, 0))],
            out_specs=pl.BlockSpec((None, COMM_ROWS, hc), lambda s, j, c_ref: (s, j, 0))),
        out_shape=jax.ShapeDtypeStruct((4, rows, hc), gb.dtype),
        compiler_params=_cp(("parallel", "parallel")),
    )(c_idx, gb, recv)


def _sum_chips(q):
    _, rows, hc = q.shape

    def body(q_ref, o_ref):
        o_ref[...] = ((q_ref[0].astype(F32) + q_ref[1].astype(F32)) + q_ref[2].astype(F32)) + q_ref[3].astype(F32)

    return pl.pallas_call(
        body, name="grad_sum_chips", grid=(rows // COMM_ROWS,),
        in_specs=[pl.BlockSpec((4, COMM_ROWS, hc), lambda j: (0, j, 0))],
        out_specs=pl.BlockSpec((COMM_ROWS, hc), lambda j: (j, 0)),
        out_shape=jax.ShapeDtypeStruct((rows, hc), F32),
        compiler_params=_cp(("parallel",)),
    )(q)


def _shard_major(g, axis):
    shape = g.shape
    g = g.reshape(shape[:axis] + (4, shape[axis] // 4) + shape[axis + 1:])
    return jnp.moveaxis(g, axis, 0).reshape(4, -1)


def _unshard(g4, shape, axis):
    n = shape[axis] // 4
    g = g4.reshape((4,) + shape[:axis] + (n,) + shape[axis + 1:])
    return jnp.moveaxis(g, 0, axis).reshape(shape)


def _split(flat, shapes):
    out, off = [], 0
    for shp in shapes:
        n = 1
        for d in shp:
            n *= d
        out.append(flat[..., off:off + n].reshape(flat.shape[:-1] + tuple(shp)))
        off += n
    return out


def _even_cols_to_kernel(w):
    return jnp.concatenate([w[:, :1536], w[:, 1552:3088], w[:, 1536:1552], w[:, 3088:3096],
                            jnp.zeros((w.shape[0], PE - 3096), w.dtype)], axis=1)


def _block_diag(w):
    out = jnp.zeros((512, 512), w.dtype)
    for n in range(8):
        out = lax.dynamic_update_slice(out, w[n], (64 * n, 64 * n))
    return out


def _diag_blocks(g):
    return jnp.stack([g[64 * n:64 * (n + 1), 64 * n:64 * (n + 1)] for n in range(8)])


def _shift_down(a, s):
    return a if s == 0 else jnp.pad(a, ((s, 0), (0, 0)))[:a.shape[0]]


def _shift_up(a, s):
    return a if s == 0 else jnp.pad(a, ((0, s), (0, 0)))[s:]


SMALL_SHARDED_SHAPES = [(2, 4, 256), (16, 64), (4, 128), (128,), (128,), (128,), (128,)]
REPL_SHAPES = [(256,), (512,), (8,), (8, 257), (8, 64, 64), (8, 64, 64)]


def kernel(x, norm_w, w_in_even, gla_w_a_up, gla_b_a, gla_norm_w, fox_b_f, w_out_even, w_in_odd, rel_bias, conv_w, conv_b, lru_w_a, lru_b_a, lru_w_x, lru_b_x, lru_lambda, w_out_odd, w_mlp_up, w_mlp_down, loss_target, m_norm_w, m_w_in_even, m_gla_w_a_up, m_gla_b_a, m_gla_norm_w, m_fox_b_f, m_w_out_even, m_w_in_odd, m_rel_bias, m_conv_w, m_conv_b, m_lru_w_a, m_lru_b_a, m_lru_w_x, m_lru_b_x, m_lru_lambda, m_w_out_odd, m_w_mlp_up, m_w_mlp_down, v_norm_w, v_w_in_even, v_gla_w_a_up, v_gla_b_a, v_gla_norm_w, v_fox_b_f, v_w_out_even, v_w_in_odd, v_rel_bias, v_conv_w, v_conv_b, v_lru_w_a, v_lru_b_a, v_lru_w_x, v_lru_b_x, v_lru_lambda, v_w_out_odd, v_w_mlp_up, v_w_mlp_down):
    c_idx = lax.axis_index("c")

    small_local = [norm_w, gla_w_a_up[0], conv_w[0], conv_b[0], lru_b_a[0], lru_b_x[0], lru_lambda[0]]
    small_src = jnp.concatenate([a.reshape(-1) for a in small_local]).reshape(32, 128)
    small4, w_in_e4, w_out_e, w_up, w_dn, w_in_o, w_out_o = _allgather_weights(
        [small_src, w_in_even[0].astype(BF16), w_out_even[0].astype(BF16), w_mlp_up.astype(BF16),
         w_mlp_down.astype(BF16), w_in_odd[0].astype(BF16), w_out_odd[0].astype(BF16)])

    w_in_e = _even_cols_to_kernel(_unshard(w_in_e4, (1024, 3096), 1))
    g_small = _split(small4.reshape(4, 32 * 128), SMALL_SHARDED_SHAPES)
    nw_full = _unshard(g_small[0], (2, 4, 1024), 2)
    wa_up = _unshard(g_small[1], (16, 256), 1)
    cw = _unshard(g_small[2], (4, 512), 1)
    cb, lba, lbx, lam = [_unshard(g, (512,), 0).reshape(1, 512) for g in g_small[3:]]
    nw = lambda layer, i: nw_full[layer, i].reshape(1, D)

    wa_pad = jnp.pad(wa_up, ((0, 128 - 16), (0, 0)))
    gla_ba = gla_b_a.reshape(1, 256)
    gla_nw = gla_norm_w.reshape(1, 512)
    fox_bpad = jnp.pad(fox_b_f.reshape(1, 8), ((0, 0), (FOX_LANE0, 128 - FOX_LANE0 - 8)))
    rbp = jnp.pad(rel_bias[0], ((0, 0), (0, REL_PAD - 257)))
    wa_bd = _block_diag(lru_w_a[0])
    wx_bd = _block_diag(lru_w_x[0])

    x0 = x[0]
    tgt = loss_target[0]

    h0 = _prenorm(x0, nw(0, 0), "prenorm_l0_mix")
    proj_e = _mm(h0, w_in_e, "nn", tm=1024, tn=640, name="mm_in_even")
    cat0, s_prev = _gla_fwd(proj_e, wa_pad, gla_ba, gla_nw)
    cum_r = _fox_gate_fwd(proj_e, fox_bpad)
    cum_c = cum_r[:, FOX_LANE0:FOX_LANE0 + 8].T
    cat0 = _fox_fwd(proj_e, cum_r, cum_c, cat0)
    mix0 = _mm(cat0, w_out_e, "nn", tm=1024, tn=512, name="mm_out_even")
    x1 = _postnorm(x0, mix0, nw(0, 1), "postnorm_l0_mix")
    h1 = _prenorm(x1, nw(0, 2), "prenorm_l0_mlp")
    u0 = _mm(h1, w_up, "nn", tm=1024, tn=1024, b_layer=0, name="mm_up_l0")
    a0 = _relu2(u0, "relu2_l0")
    d0 = _mm(a0, w_dn, "nn", tm=1024, tn=512, tk=2048, b_layer=0, name="mm_down_l0")
    x2 = _postnorm(x1, d0, nw(0, 3), "postnorm_l0_mlp")

    h2 = _prenorm(x2, nw(1, 0), "prenorm_l1_mix")
    proj_o = _mm(h2, w_in_o, "nn", tm=1024, tn=640, name="mm_in_odd")
    bias_q = _bias_build(rbp)
    bias = bias_q.transpose(1, 0, 2)
    kvpad = jnp.pad(proj_o[:, 512:1536], ((CA_PAD, 0), (0, 0)))
    cat1 = _ca_fwd(proj_o, kvpad, bias)
    x_in = proj_o[:, 2048:2560]
    xs = jnp.stack([_shift_down(x_in, 3 - j) for j in range(4)])
    lru_a, lru_b = _lru_pre_fwd(xs, cw, cb, wa_bd, lba, wx_bd, lbx, lam)
    hh = _lru_scan_fwd(lru_a, lru_b)
    cat1 = _lru_post_fwd(hh, proj_o, cat1)
    mix1 = _mm(cat1, w_out_o, "nn", tm=1024, tn=512, name="mm_out_odd")
    x3 = _postnorm(x2, mix1, nw(1, 1), "postnorm_l1_mix")
    h3 = _prenorm(x3, nw(1, 2), "prenorm_l1_mlp")
    u1 = _mm(h3, w_up, "nn", tm=1024, tn=1024, b_layer=1, name="mm_up_l1")
    a1 = _relu2(u1, "relu2_l1")
    d1 = _mm(a1, w_dn, "nn", tm=1024, tn=512, tk=2048, b_layer=1, name="mm_down_l1")
    x4 = _postnorm(x3, d1, nw(1, 3), "postnorm_l1_mlp")

    g4, loss_part = _loss_and_grad(x4, tgt)
    loss = lax.psum(loss_part[0, 0], ("x", "y", "c"))

    gb = lax.empty((4, GB_ROWS, D), BF16)
    dd1, dnw13 = _norm_bwd(d1, nw(1, 3), g4, None, "postnorm_l1_mlp_bwd")
    gb = _mm(a1, dd1, "tn", tm=512, tn=1024, into=(gb, 1024, GB_DN[1]), name="mm_down_l1_dw")
    da1 = _mm(dd1, w_dn, "nt", tm=1024, tn=1024, b_layer=1, name="mm_down_l1_dx")
    du1 = _relu2_bwd(u1, da1, "relu2_l1_bwd")
    gb = _mm(du1, h3, "tn", tm=512, tn=1024, into=(gb, 1024, GB_UP[1]), name="mm_up_l1_dw")
    dh3 = _mm(du1, w_up, "nt", tm=1024, tn=512, tk=2048, b_layer=1, name="mm_up_l1_dx")
    g3, dnw12 = _norm_bwd(x3, nw(1, 2), dh3, g4, "prenorm_l1_mlp_bwd")
    dmix1, dnw11 = _norm_bwd(mix1, nw(1, 1), g3, None, "postnorm_l1_mix_bwd")
    gb = _mm(cat1, dmix1, "tn", tm=128, tn=1024, into=(gb, 256, GB_OUT_O), name="mm_out_odd_dw")
    dcat1 = _mm(dmix1, w_out_o, "nt", tm=1024, tn=512, name="mm_out_odd_dx")

    dq_c, dkpad, dvpad, dbias = _ca_bwd(proj_o, kvpad, bias, dcat1)
    g_rel = _bias_grad(dbias.transpose(1, 0, 2))[:, :257]
    dhh, dgate = _lru_post_bwd(hh, proj_o, dcat1)
    da_l, db_l = _lru_scan_bwd(lru_a, hh, dhh)
    dxs, g_cw, g_cb, g_wa_bd, g_lba, g_wx_bd, g_lbx, g_lam = _lru_pre_bwd(xs, cw, cb, wa_bd, lba, wx_bd, lbx, lam, da_l, db_l)
    dx_in = _conv_dx(jnp.stack([_shift_up(dxs[j], 3 - j) for j in range(4)]))
    dproj_o = jnp.concatenate([dq_c, dkpad[CA_PAD:], dvpad[CA_PAD:], dgate, dx_in], axis=1)
    gb = _mm(dproj_o, h2, "tn", tm=128, tn=1024, into=(gb, 640, GB_IN_O), name="mm_in_odd_dw")
    dh2 = _mm(dproj_o, w_in_o, "nt", tm=1024, tn=512, tk=1280, name="mm_in_odd_dx")
    g2, dnw10 = _norm_bwd(x2, nw(1, 0), dh2, g3, "prenorm_l1_mix_bwd")

    dd0, dnw03 = _norm_bwd(d0, nw(0, 3), g2, None, "postnorm_l0_mlp_bwd")
    gb = _mm(a0, dd0, "tn", tm=512, tn=1024, into=(gb, 1024, GB_DN[0]), name="mm_down_l0_dw")
    da0 = _mm(dd0, w_dn, "nt", tm=1024, tn=1024, b_layer=0, name="mm_down_l0_dx")
    du0 = _relu2_bwd(u0, da0, "relu2_l0_bwd")
    gb = _mm(du0, h1, "tn", tm=512, tn=1024, into=(gb, 1024, GB_UP[0]), name="mm_up_l0_dw")
    dh1 = _mm(du0, w_up, "nt", tm=1024, tn=512, tk=2048, b_layer=0, name="mm_up_l0_dx")
    g1, dnw02 = _norm_bwd(x1, nw(0, 2), dh1, g2, "prenorm_l0_mlp_bwd")
    dmix0, dnw01 = _norm_bwd(mix0, nw(0, 1), g1, None, "postnorm_l0_mix_bwd")
    gb = _mm(cat0, dmix0, "tn", tm=128, tn=1024, into=(gb, 256, GB_OUT_E), name="mm_out_even_dw")
    dcat0 = _mm(dmix0, w_out_e, "nt", tm=1024, tn=512, name="mm_out_even_dx")

    dq_g, dk_g, dv_g, dr_g, daux_g, g_wa_pad, g_gla_ba, g_gla_nw = _gla_bwd(proj_e, s_prev, wa_pad, gla_ba, gla_nw, dcat0)
    dq_f, dk_f, dv_f, dcrow, dccol = _fox_bwd(proj_e, cum_r, cum_c, dcat0)
    dccol_t = jnp.pad(dccol.sum(axis=0).T, ((0, 0), (FOX_LANE0, 128 - FOX_LANE0 - 8)))
    daux, g_fox_bpad = _fox_gate_bwd(proj_e, fox_bpad, dcrow, dccol_t, daux_g)
    dproj_e = jnp.concatenate([dq_g, dk_g, dv_g, dr_g, dq_f, dk_f, dv_f, daux], axis=1)
    gt_in_e = _mm(dproj_e, h0, "tn", tm=640, tn=1024, name="mm_in_even_dw")
    dh0 = _mm(dproj_e, w_in_e, "nt", tm=1024, tn=512, tk=640, name="mm_in_even_dx")
    grad_x, dnw00 = _norm_bwd(x0, nw(0, 0), dh0, g1, "prenorm_l0_mix_bwd")

    g_norm = jnp.stack([jnp.concatenate([dnw00, dnw01, dnw02, dnw03]), jnp.concatenate([dnw10, dnw11, dnw12, dnw13])])
    sharded = [(g_norm, 2), (g_wa_pad[:16], 1), (g_cw, 1), (g_cb[0], 0), (g_lba[0], 0), (g_lbx[0], 0), (g_lam[0], 0)]
    replicated = [g_gla_ba[0], g_gla_nw[0], g_fox_bpad[0, FOX_LANE0:FOX_LANE0 + 8], g_rel, _diag_blocks(g_wa_bd),
                  _diag_blocks(g_wx_bd)]
    small4 = jnp.concatenate([_shard_major(g, ax) for g, ax in sharded]
                             + [jnp.broadcast_to(g.reshape(1, -1), (4, g.size)) for g in replicated], axis=1)
    n_small = small4.shape[1]
    small_rows = GB_ROWS - GB_TAIL - 774
    small4 = jnp.pad(small4, ((0, 0), (0, small_rows * D - n_small))).reshape(4, small_rows, D)
    gt_rows = jnp.concatenate([gt_in_e[:1536], gt_in_e[3072:3088], gt_in_e[1536:3072], gt_in_e[3088:3096]], axis=0)
    tail = jnp.concatenate([gt_rows.reshape(4, 774, D), small4], axis=1).astype(BF16)
    gb = lax.dynamic_update_slice(gb, tail, (0, GB_TAIL, 0))

    recv_half = _pair_swap_cols(gb)
    pair_sum = _pair_add(gb, recv_half, c_idx.reshape(1).astype(jnp.int32))
    from_chips = _chip_alltoall(pair_sum)
    halves = _handover(_sum_chips(from_chips))
    reduced = jnp.concatenate([halves[0], halves[1]], axis=1)

    g_up = reduced[GB_UP[0]:GB_UP[0] + 2048].reshape(2, 1024, 1024).transpose(0, 2, 1)
    g_dn = reduced[GB_DN[0]:GB_DN[0] + 2048].reshape(2, 1024, 1024)
    g_small = _split(reduced[GB_TAIL + 774:].reshape(-1)[:n_small], SMALL_SHARDED_SHAPES + REPL_SHAPES)
    g_of = dict(zip(["norm_w", "gla_w_a_up", "conv_w", "conv_b", "lru_b_a", "lru_b_x", "lru_lambda", "gla_b_a",
                     "gla_norm_w", "fox_b_f", "rel_bias", "lru_w_a", "lru_w_x"], g_small))
    g_of.update(w_mlp_up=g_up, w_mlp_down=g_dn, w_in_odd=reduced[GB_IN_O:GB_IN_O + 640].T,
                w_out_even=reduced[GB_OUT_E:GB_OUT_E + 256], w_out_odd=reduced[GB_OUT_O:GB_OUT_O + 256],
                w_in_even=reduced[GB_TAIL:GB_TAIL + 774].T)

    names = ["norm_w", "w_in_even", "gla_w_a_up", "gla_b_a", "gla_norm_w", "fox_b_f", "w_out_even", "w_in_odd", "rel_bias",
             "conv_w", "conv_b", "lru_w_a", "lru_b_a", "lru_w_x", "lru_b_x", "lru_lambda", "w_out_odd", "w_mlp_up",
             "w_mlp_down"]
    w_of = dict(norm_w=norm_w, w_in_even=w_in_even, gla_w_a_up=gla_w_a_up, gla_b_a=gla_b_a, gla_norm_w=gla_norm_w,
                fox_b_f=fox_b_f, w_out_even=w_out_even, w_in_odd=w_in_odd, rel_bias=rel_bias, conv_w=conv_w, conv_b=conv_b,
                lru_w_a=lru_w_a, lru_b_a=lru_b_a, lru_w_x=lru_w_x, lru_b_x=lru_b_x, lru_lambda=lru_lambda,
                w_out_odd=w_out_odd, w_mlp_up=w_mlp_up, w_mlp_down=w_mlp_down)
    m_of = dict(norm_w=m_norm_w, w_in_even=m_w_in_even, gla_w_a_up=m_gla_w_a_up, gla_b_a=m_gla_b_a,
                gla_norm_w=m_gla_norm_w, fox_b_f=m_fox_b_f, w_out_even=m_w_out_even, w_in_odd=m_w_in_odd,
                rel_bias=m_rel_bias, conv_w=m_conv_w, conv_b=m_conv_b, lru_w_a=m_lru_w_a, lru_b_a=m_lru_b_a,
                lru_w_x=m_lru_w_x, lru_b_x=m_lru_b_x, lru_lambda=m_lru_lambda, w_out_odd=m_w_out_odd,
                w_mlp_up=m_w_mlp_up, w_mlp_down=m_w_mlp_down)
    v_of = dict(norm_w=v_norm_w, w_in_even=v_w_in_even, gla_w_a_up=v_gla_w_a_up, gla_b_a=v_gla_b_a,
                gla_norm_w=v_gla_norm_w, fox_b_f=v_fox_b_f, w_out_even=v_w_out_even, w_in_odd=v_w_in_odd,
                rel_bias=v_rel_bias, conv_w=v_conv_w, conv_b=v_conv_b, lru_w_a=v_lru_w_a, lru_b_a=v_lru_b_a,
                lru_w_x=v_lru_w_x, lru_b_x=v_lru_b_x, lru_lambda=v_lru_lambda, w_out_odd=v_w_out_odd,
                w_mlp_up=v_w_mlp_up, w_mlp_down=v_w_mlp_down)
    grads, deltas, new_ms, new_vs = [], [], [], []
    for n in names:
        w = w_of[n]
        view = w.shape if w.ndim <= 3 else w.shape[-3:]
        g = g_of[n].reshape(w.shape)
        d, mn, vn = _adamw(w.reshape(view), g.reshape(view), m_of[n].reshape(view), v_of[n].reshape(view),
                           "adamw_" + n)
        grads.append(g)
        deltas.append(d.reshape(w.shape))
        new_ms.append(mn.reshape(w.shape))
        new_vs.append(vn.reshape(w.shape))

    return (loss, grad_x.reshape(1, T, D), *grads, *deltas, *new_ms, *new_vs)
```

```python
import functools

import jax
import jax.numpy as jnp
from jax import lax
from jax.experimental import pallas as pl
from jax.experimental.pallas import tpu as pltpu

F32 = jnp.float32
BF16 = jnp.bfloat16
MESH = pl.DeviceIdType.MESH

T = 2048
D = 1024
DFF = 4096
EPS = 1e-6
CHUNK = 64
NCHUNK = T // CHUNK
PE = 3200
PO = 2560
AUX_BLK = 3072 // 128
FOX_LANE0 = 16
GLA_SCALE = 64 ** -0.5
ATT_SCALE = 64 ** -0.5
NEG = float(jnp.finfo(jnp.float32).min)
CA_BAND = 576
CA_PAD = 512
REL_PAD = 384

VMEM_LIMIT = 48 * 1024 * 1024

ADAM_LR, ADAM_B1, ADAM_B2, ADAM_EPS, ADAM_WD, ADAM_STEP = 0.001, 0.9, 0.999, 1e-08, 0.01, 10

GB_ROWS = 6144
GB_UP = (0, 1024)
GB_DN = (2048, 3072)
GB_IN_O = 4096
GB_OUT_E = 4736
GB_OUT_O = 4992
GB_TAIL = 5248

_DIMS = {"nn": (((1,), (0,)), ((), ())), "nt": (((1,), (1,)), ((), ())), "tn": (((0,), (0,)), ((), ()))}


def _cp(sem, **kw):
    return pltpu.CompilerParams(dimension_semantics=sem, vmem_limit_bytes=VMEM_LIMIT, **kw)


def _dot(a, b, mode):
    return lax.dot_general(a.astype(BF16), b.astype(BF16), _DIMS[mode], preferred_element_type=F32)


@functools.partial(jax.custom_vjp, nondiff_argnums=(2,))
def bdot(a, b, mode):
    return _dot(a, b, mode)


def _bdot_fwd(a, b, mode):
    return _dot(a, b, mode), (a, b)


def _bdot_bwd(mode, res, g):
    a, b = res
    if mode == "nn":
        da, db = _dot(g, b, "nt"), _dot(a, g, "tn")
    elif mode == "nt":
        da, db = _dot(g, b, "nn"), _dot(g, a, "tn")
    else:
        da, db = _dot(b, g, "nt"), _dot(a, g, "nn")
    return da.astype(a.dtype), db.astype(b.dtype)


bdot.defvjp(_bdot_fwd, _bdot_bwd)


def _hdot_raw(a, b, mode):
    return lax.dot_general(a, b, _DIMS[mode], precision=lax.Precision.HIGHEST, preferred_element_type=F32)


@functools.partial(jax.custom_vjp, nondiff_argnums=(2,))
def hdot(a, b, mode):
    return _hdot_raw(a, b, mode)


def _hdot_fwd(a, b, mode):
    return _hdot_raw(a, b, mode), (a, b)


def _hdot_bwd(mode, res, g):
    a, b = res
    if mode == "nn":
        return _hdot_raw(g, b, "nt"), _hdot_raw(a, g, "tn")
    if mode == "nt":
        return _hdot_raw(g, b, "nn"), _hdot_raw(g, a, "tn")
    return _hdot_raw(b, g, "nt"), _hdot_raw(a, g, "nn")


hdot.defvjp(_hdot_fwd, _hdot_bwd)


def _log_sigmoid(x):
    return jnp.minimum(x, 0.0) - jnp.log(1.0 + jnp.exp(-jnp.abs(x)))


def _sigmoid(x):
    return 1.0 / (1.0 + jnp.exp(-x))


def _expm1(x):
    series = x * (1.0 + x * 0.5 * (1.0 + x * (1.0 / 3.0) * (1.0 + x * 0.25)))
    return jnp.where(jnp.abs(x) < 0.03, series, jnp.exp(x) - 1.0)


def _gelu_tanh(x):
    return 0.5 * x * (1.0 + jnp.tanh(0.7978845608028654 * (x + 0.044715 * x * x * x)))


def _softmax_rows(s):
    m = jnp.max(s, axis=-1, keepdims=True)
    p = jnp.exp(s - m)
    return p / jnp.sum(p, axis=-1, keepdims=True)


def _iota(shape, dim):
    return lax.broadcasted_iota(jnp.int32, shape, dim)


def _mm(a, b, mode, *, tm, tn, tk=None, out_dtype=F32, name, b_layer=None, into=None, relu_pair=False, times2=None):
    b2 = b.shape[-2:]
    if mode == "nn":
        (m, k), n = a.shape, b2[1]
    elif mode == "nt":
        (m, k), n = a.shape, b2[0]
    else:
        (k, m), n = a.shape, b2[1]
    tk = k if tk is None else tk
    assert m % tm == 0 and n % tn == 0 and k % tk == 0, (name, a.shape, b.shape)
    nk = k // tk
    a_spec = {"nn": pl.BlockSpec((tm, tk), lambda i, j, kk: (i, kk)),
              "nt": pl.BlockSpec((tm, tk), lambda i, j, kk: (i, kk)),
              "tn": pl.BlockSpec((tk, tm), lambda i, j, kk: (kk, i))}[mode]
    b_blk = {"nn": (tk, tn), "nt": (tn, tk), "tn": (tk, tn)}[mode]
    b_idx = {"nn": lambda i, j, kk: (kk, j), "nt": lambda i, j, kk: (j, kk), "tn": lambda i, j, kk: (kk, j)}[mode]
    if b_layer is None:
        b_spec = pl.BlockSpec(b_blk, b_idx)
    else:
        b_spec = pl.BlockSpec((None,) + b_blk, lambda i, j, kk: (b_layer,) + b_idx(i, j, kk))

    tile = pl.BlockSpec((tm, tn), lambda i, j, kk: (i, j))
    if into is not None:
        buf, per_slot, row_off = into
        assert m == 4 * per_slot and per_slot % tm == 0 and row_off % tm == 0 and buf.shape[2] == n, (name, buf.shape)
        bps = per_slot // tm
        out_specs = pl.BlockSpec((None, tm, tn), lambda i, j, kk: (i // bps, row_off // tm + i % bps, j))
        out_shape = jax.ShapeDtypeStruct(buf.shape, buf.dtype)
        extra_in, extra_specs, aliases = [buf], [pl.BlockSpec(memory_space=pl.ANY)], {2: 0}
        finish = lambda acc, extra: [acc.astype(buf.dtype)]
    elif relu_pair:
        out_specs = (tile, tile)
        out_shape = (jax.ShapeDtypeStruct((m, n), BF16),) * 2
        extra_in, extra_specs, aliases = [], [], {}

        def finish(acc, extra):
            r = jnp.maximum(acc, 0.0)
            return [(r * r).astype(BF16), r.astype(BF16)]
    elif times2 is not None:
        out_specs = tile
        out_shape = jax.ShapeDtypeStruct((m, n), out_dtype)
        extra_in, extra_specs, aliases = [times2], [tile], {}
        finish = lambda acc, extra: [(acc * (2.0 * extra[...].astype(F32))).astype(out_dtype)]
    else:
        out_specs = tile
        out_shape = jax.ShapeDtypeStruct((m, n), out_dtype)
        extra_in, extra_specs, aliases = [], [], {}
        finish = lambda acc, extra: [acc.astype(out_dtype)]
    n_out = 2 if relu_pair else 1

    def body(*refs):
        a_ref, b_ref = refs[0], refs[1]
        extra = refs[2] if extra_in else None
        o_refs = refs[2 + len(extra_in):2 + len(extra_in) + n_out]

        def store(acc):
            for o_ref, val in zip(o_refs, finish(acc, extra)):
                o_ref[...] = val

        if nk == 1:
            store(_dot(a_ref[...], b_ref[...], mode))
            return
        acc_ref = refs[-1]
        kk = pl.program_id(2)

        @pl.when(kk == 0)
        def _():
            acc_ref[...] = jnp.zeros_like(acc_ref)

        acc_ref[...] += _dot(a_ref[...], b_ref[...], mode)

        @pl.when(kk == nk - 1)
        def _():
            store(acc_ref[...])

    return pl.pallas_call(
        body, name=name, grid=(m // tm, n // tn, nk),
        in_specs=[a_spec, b_spec] + extra_specs,
        out_specs=out_specs, out_shape=out_shape,
        scratch_shapes=[pltpu.VMEM((tm, tn), F32)] if nk > 1 else [],
        input_output_aliases=aliases,
        compiler_params=_cp(("parallel", "parallel", "arbitrary")),
    )(a, b, *extra_in)


ROWS = 256


def _prenorm(x, w, name):
    def body(x_ref, w_ref, o_ref):
        xv = x_ref[...]
        r = lax.rsqrt(jnp.mean(xv * xv, axis=-1, keepdims=True) + EPS)
        o_ref[...] = (xv * r * w_ref[...]).astype(BF16)

    return pl.pallas_call(
        body, name=name, grid=(T // ROWS,),
        in_specs=[pl.BlockSpec((ROWS, D), lambda i: (i, 0)), pl.BlockSpec((1, D), lambda i: (0, 0))],
        out_specs=pl.BlockSpec((ROWS, D), lambda i: (i, 0)),
        out_shape=jax.ShapeDtypeStruct((T, D), BF16),
        compiler_params=_cp(("parallel",)),
    )(x, w)


def _postnorm(x, z, w, name):
    def body(x_ref, z_ref, w_ref, o_ref):
        zv = z_ref[...]
        r = lax.rsqrt(jnp.mean(zv * zv, axis=-1, keepdims=True) + EPS)
        o_ref[...] = x_ref[...] + zv * r * w_ref[...]

    return pl.pallas_call(
        body, name=name, grid=(T // ROWS,),
        in_specs=[pl.BlockSpec((ROWS, D), lambda i: (i, 0)), pl.BlockSpec((ROWS, D), lambda i: (i, 0)),
                  pl.BlockSpec((1, D), lambda i: (0, 0))],
        out_specs=pl.BlockSpec((ROWS, D), lambda i: (i, 0)),
        out_shape=jax.ShapeDtypeStruct((T, D), F32),
        compiler_params=_cp(("parallel",)),
    )(x, z, w)


def _norm_bwd(z, w, dy, add, name):
    has_add = add is not None

    def body(*refs):
        if has_add:
            z_ref, w_ref, dy_ref, add_ref, dz_ref, dw_ref = refs
        else:
            z_ref, w_ref, dy_ref, dz_ref, dw_ref = refs
        i = pl.program_id(0)

        @pl.when(i == 0)
        def _():
            dw_ref[...] = jnp.zeros_like(dw_ref)

        zv = z_ref[...].astype(F32)
        dyv = dy_ref[...]
        r = lax.rsqrt(jnp.mean(zv * zv, axis=-1, keepdims=True) + EPS)
        wdy = dyv * w_ref[...]
        dz = r * wdy - zv * (r * r * r) * jnp.mean(zv * wdy, axis=-1, keepdims=True)
        if has_add:
            dz = dz + add_ref[...]
        dz_ref[...] = dz.astype(dz_ref.dtype)
        dw_ref[...] += jnp.sum(dyv * zv * r, axis=0, keepdims=True)

    row = pl.BlockSpec((ROWS, D), lambda i: (i, 0))
    vec = pl.BlockSpec((1, D), lambda i: (0, 0))
    ins = [z, w, dy] + ([add] if has_add else [])
    dz_dtype = F32 if has_add else BF16
    return pl.pallas_call(
        body, name=name, grid=(T // ROWS,),
        in_specs=[row, vec, row] + ([row] if has_add else []),
        out_specs=(row, vec),
        out_shape=(jax.ShapeDtypeStruct((T, D), dz_dtype), jax.ShapeDtypeStruct((1, D), F32)),
        compiler_params=_cp(("arbitrary",)),
    )(*ins)


def _loss_and_grad(y, tgt):
    def body(y_ref, t_ref, g_ref, l_ref):
        i = pl.program_id(0)

        @pl.when(i == 0)
        def _():
            l_ref[...] = jnp.zeros_like(l_ref)

        e = y_ref[...] - t_ref[...]
        g_ref[...] = e * (1.0 / D)
        l_ref[...] += jnp.sum(e * e) * (0.5 / D)

    row = pl.BlockSpec((ROWS, D), lambda i: (i, 0))
    return pl.pallas_call(
        body, name="loss_head", grid=(T // ROWS,), in_specs=[row, row],
        out_specs=(row, pl.BlockSpec((1, 128), lambda i: (0, 0))),
        out_shape=(jax.ShapeDtypeStruct((T, D), F32), jax.ShapeDtypeStruct((1, 128), F32)),
        compiler_params=_cp(("arbitrary",)),
    )(y, tgt)


def _adamw(w, g, m, v, name):
    lead = w.shape[:-2]
    assert len(lead) <= 1 and g.shape == w.shape, (name, w.shape, g.shape)
    rows, cols = w.shape[-2:]
    tr = rows if rows <= 512 else 256
    assert rows % tr == 0, (name, w.shape)
    c1 = 1.0 - ADAM_B1 ** ADAM_STEP
    c2 = 1.0 - ADAM_B2 ** ADAM_STEP

    def body(w_ref, g_ref, m_ref, v_ref, d_ref, mo_ref, vo_ref):
        gv = g_ref[...]
        mn = ADAM_B1 * m_ref[...] + (1.0 - ADAM_B1) * gv
        vn = ADAM_B2 * v_ref[...] + (1.0 - ADAM_B2) * (gv * gv)
        m_hat = mn / c1
        v_hat = vn / c2
        d_ref[...] = -ADAM_LR * (m_hat / (jnp.sqrt(v_hat) + ADAM_EPS) + ADAM_WD * w_ref[...])
        mo_ref[...] = mn
        vo_ref[...] = vn

    if lead:
        grid = (lead[0], rows // tr)
        blk = pl.BlockSpec((None, tr, cols), lambda l, i: (l, i, 0))
    else:
        grid = (rows // tr,)
        blk = pl.BlockSpec((tr, cols), lambda i: (i, 0))
    sds = jax.ShapeDtypeStruct(w.shape, F32)
    return pl.pallas_call(body, name=name, grid=grid, in_specs=[blk] * 4, out_specs=(blk,) * 3,
                          out_shape=(sds,) * 3, compiler_params=_cp(("parallel",) * len(grid)))(w, g, m, v)


def _gla_consts():
    ltri = (_iota((CHUNK, CHUNK), 0) >= _iota((CHUNK, CHUNK), 1)).astype(F32)
    ones_c = jnp.ones((CHUNK, 128), F32)
    mask = (_iota((256, 512), 0) // 64 == _iota((256, 512), 1) // 128).astype(F32)
    return ltri, ones_c, mask


def _gla_chunk(consts, q, k, v, r, aux, s_prev, wa, ba, nw):
    ltri, ones_c, mask = consts
    la = _log_sigmoid(bdot(aux, wa, "nn") + ba) * (1.0 / 16.0)
    cum = hdot(ltri, la, "nn")
    total = jnp.sum(la, axis=0, keepdims=True)
    k_dec = k * jnp.exp(total - cum)
    inc = bdot(k_dec, v, "tn") * mask
    dec = jnp.exp(hdot(la, ones_c, "tn"))
    dec = jnp.concatenate([dec, dec, dec, dec], axis=1)
    s_new = dec * s_prev + inc
    o = bdot(q * GLA_SCALE, s_new, "nn")
    parts = []
    for h in range(4):
        oh = o[:, h * 128:(h + 1) * 128]
        parts.append(oh * lax.rsqrt(jnp.mean(oh * oh, axis=-1, keepdims=True) + EPS))
    on = jnp.concatenate(parts, axis=1)
    return s_new, on * nw * (r * _sigmoid(r))


def _gla_specs(cmap):
    return [pl.BlockSpec((CHUNK, 256), lambda c: (cmap(c), 0)),
            pl.BlockSpec((CHUNK, 256), lambda c: (cmap(c), 1)),
            pl.BlockSpec((CHUNK, 512), lambda c: (cmap(c), 1)),
            pl.BlockSpec((CHUNK, 512), lambda c: (cmap(c), 2)),
            pl.BlockSpec((CHUNK, 128), lambda c: (cmap(c), AUX_BLK))]


def _gla_fwd(proj, wa, ba, nw):
    def body(q_ref, k_ref, v_ref, r_ref, aux_ref, wa_ref, ba_ref, nw_ref, o_ref, sp_ref, s_ref):
        c = pl.program_id(0)

        @pl.when(c == 0)
        def _():
            s_ref[...] = jnp.zeros_like(s_ref)

        s_prev = s_ref[...]
        sp_ref[...] = s_prev
        s_new, out = _gla_chunk(_gla_consts(), q_ref[...], k_ref[...], v_ref[...], r_ref[...], aux_ref[...],
                                s_prev, wa_ref[...], ba_ref[...], nw_ref[...])
        s_ref[...] = s_new
        o_ref[...] = out

    full = lambda shape: pl.BlockSpec(shape, lambda c: (0,) * len(shape))
    return pl.pallas_call(
        body, name="gla_fwd", grid=(NCHUNK,),
        in_specs=_gla_specs(lambda c: c) + [full((128, 256)), full((1, 256)), full((1, 512))],
        out_specs=(pl.BlockSpec((CHUNK, 512), lambda c: (c, 0)), pl.BlockSpec((None, 256, 512), lambda c: (c, 0, 0))),
        out_shape=(jax.ShapeDtypeStruct((T, D), F32), jax.ShapeDtypeStruct((NCHUNK, 256, 512), F32)),
        scratch_shapes=[pltpu.VMEM((256, 512), F32)],
        compiler_params=_cp(("arbitrary",)),
    )(proj, proj, proj, proj, proj, wa, ba, nw)


def _gla_bwd(proj, s_prev_all, wa, ba, nw, dcat):
    rev = lambda c: NCHUNK - 1 - c

    def body(q_ref, k_ref, v_ref, r_ref, aux_ref, sp_ref, wa_ref, ba_ref, nw_ref, do_ref,
             dq_ref, dk_ref, dv_ref, dr_ref, daux_ref, dwa_ref, dba_ref, dnw_ref, ds_ref):
        c = pl.program_id(0)

        @pl.when(c == 0)
        def _():
            ds_ref[...] = jnp.zeros_like(ds_ref)
            dwa_ref[...] = jnp.zeros_like(dwa_ref)
            dba_ref[...] = jnp.zeros_like(dba_ref)
            dnw_ref[...] = jnp.zeros_like(dnw_ref)

        fn = functools.partial(_gla_chunk, _gla_consts())
        _, vjp = jax.vjp(fn, q_ref[...], k_ref[...], v_ref[...], r_ref[...], aux_ref[...], sp_ref[...],
                         wa_ref[...], ba_ref[...], nw_ref[...])
        dq, dk, dv, dr, daux, dsp, dwa, dba, dnw = vjp((ds_ref[...], do_ref[...]))
        dq_ref[...] = dq
        dk_ref[...] = dk
        dv_ref[...] = dv
        dr_ref[...] = dr
        daux_ref[...] = daux
        ds_ref[...] = dsp
        dwa_ref[...] += dwa
        dba_ref[...] += dba
        dnw_ref[...] += dnw

    full = lambda shape: pl.BlockSpec(shape, lambda c: (0,) * len(shape))
    blk = lambda w: pl.BlockSpec((CHUNK, w), lambda c: (rev(c), 0))
    sds = lambda *s: jax.ShapeDtypeStruct(s, F32)
    return pl.pallas_call(
        body, name="gla_bwd", grid=(NCHUNK,),
        in_specs=_gla_specs(rev) + [pl.BlockSpec((None, 256, 512), lambda c: (rev(c), 0, 0)),
                                    full((128, 256)), full((1, 256)), full((1, 512)), blk(512)],
        out_specs=(blk(256), blk(256), blk(512), blk(512), blk(128), full((128, 256)), full((1, 256)), full((1, 512))),
        out_shape=(sds(T, 256), sds(T, 256), sds(T, 512), sds(T, 512), sds(T, 128),
                   sds(128, 256), sds(1, 256), sds(1, 512)),
        scratch_shapes=[pltpu.VMEM((256, 512), F32)],
        compiler_params=_cp(("arbitrary",)),
    )(proj, proj, proj, proj, proj, s_prev_all, wa, ba, nw, dcat)


GATE_ROWS = 128


def _fox_gate_block(ltri, aux, bpad, carry):
    lf = _log_sigmoid(aux + bpad)
    cum = hdot(ltri, lf, "nn") + carry
    return cum, carry + jnp.sum(lf, axis=0, keepdims=True)


def _gate_ltri():
    return (_iota((GATE_ROWS, GATE_ROWS), 0) >= _iota((GATE_ROWS, GATE_ROWS), 1)).astype(F32)


def _fox_gate_fwd(proj, bpad):
    def body(aux_ref, b_ref, cum_ref, carry_ref):
        i = pl.program_id(0)

        @pl.when(i == 0)
        def _():
            carry_ref[...] = jnp.zeros_like(carry_ref)

        cum, carry = _fox_gate_block(_gate_ltri(), aux_ref[...], b_ref[...], carry_ref[...])
        cum_ref[...] = cum
        carry_ref[...] = carry

    return pl.pallas_call(
        body, name="fox_gate_fwd", grid=(T // GATE_ROWS,),
        in_specs=[pl.BlockSpec((GATE_ROWS, 128), lambda i: (i, AUX_BLK)), pl.BlockSpec((1, 128), lambda i: (0, 0))],
        out_specs=pl.BlockSpec((GATE_ROWS, 128), lambda i: (i, 0)),
        out_shape=jax.ShapeDtypeStruct((T, 128), F32),
        scratch_shapes=[pltpu.VMEM((1, 128), F32)],
        compiler_params=_cp(("arbitrary",)),
    )(proj, bpad)


def _fox_gate_bwd(proj, bpad, dcrow, dccol_t, daux_gla):
    nb = T // GATE_ROWS
    rev = lambda i: nb - 1 - i

    def body(aux_ref, b_ref, dr_ref, dc_ref, dg_ref, daux_ref, db_ref, dcarry_ref):
        i = pl.program_id(0)

        @pl.when(i == 0)
        def _():
            dcarry_ref[...] = jnp.zeros_like(dcarry_ref)
            db_ref[...] = jnp.zeros_like(db_ref)

        dcum = dr_ref[0] + dr_ref[1] + dr_ref[2] + dr_ref[3] + dc_ref[...]
        fn = functools.partial(_fox_gate_block, _gate_ltri())
        _, vjp = jax.vjp(fn, aux_ref[...], b_ref[...], jnp.zeros((1, 128), F32))
        daux, db, dcarry = vjp((dcum, dcarry_ref[...]))
        daux_ref[...] = daux + dg_ref[...]
        db_ref[...] += db
        dcarry_ref[...] = dcarry

    blk = pl.BlockSpec((GATE_ROWS, 128), lambda i: (rev(i), 0))
    vec = pl.BlockSpec((1, 128), lambda i: (0, 0))
    return pl.pallas_call(
        body, name="fox_gate_bwd", grid=(nb,),
        in_specs=[pl.BlockSpec((GATE_ROWS, 128), lambda i: (rev(i), AUX_BLK)), vec,
                  pl.BlockSpec((4, GATE_ROWS, 128), lambda i: (0, rev(i), 0)), blk, blk],
        out_specs=(blk, vec),
        out_shape=(jax.ShapeDtypeStruct((T, 128), F32), jax.ShapeDtypeStruct((1, 128), F32)),
        scratch_shapes=[pltpu.VMEM((1, 128), F32)],
        compiler_params=_cp(("arbitrary",)),
    )(proj, bpad, dcrow, dccol_t, daux_gla)


FOX_Q = 128


FOX_KEY_STEP = 512
FOX_GROUPS = T // FOX_KEY_STEP
FOX_Q_PER_GROUP = FOX_KEY_STEP // FOX_Q


def _fox_block(hp, qb, q, k, v, crow, ccol):
    kl = k.shape[0]
    lane = _iota((FOX_Q, 128), 1)
    causal = (qb * FOX_Q + _iota((FOX_Q, kl), 0)) >= _iota((FOX_Q, kl), 1)
    sub = _iota((8, kl), 0)
    outs = []
    for e in range(2):
        h = 2 * hp + e
        qm = jnp.where((lane >= 64 * e) & (lane < 64 * (e + 1)), q, 0.0)
        s = bdot(qm, k, "nt") * ATT_SCALE
        ct = jnp.sum(jnp.where(lane == FOX_LANE0 + h, crow, 0.0), axis=1, keepdims=True)
        cs = jnp.sum(jnp.where(sub == h, ccol, 0.0), axis=0, keepdims=True)
        s = jnp.where(causal, s + (ct - cs), NEG)
        outs.append(bdot(_softmax_rows(s), v, "nn"))
    return jnp.where(lane < 64, outs[0], outs[1])


def _fox_in_specs():
    return [pl.BlockSpec((FOX_Q, 128), lambda hp, qb: (qb, 12 + hp)),
            pl.BlockSpec((T, 128), lambda hp, qb: (0, 16 + hp)),
            pl.BlockSpec((T, 128), lambda hp, qb: (0, 20 + hp)),
            pl.BlockSpec((FOX_Q, 128), lambda hp, qb: (qb, 0)),
            pl.BlockSpec((8, T), lambda hp, qb: (0, 0))]


def _fox_fwd(proj, cum_r, cum_c, cat):
    def body(q_ref, k_ref, v_ref, cr_ref, cc_ref, cat_ref, o_ref):
        qb = pl.program_id(1)
        for g in range(FOX_GROUPS):
            kl = FOX_KEY_STEP * (g + 1)

            @pl.when(qb // FOX_Q_PER_GROUP == g)
            def _(kl=kl):
                o_ref[...] = _fox_block(pl.program_id(0), qb, q_ref[...], k_ref[0:kl, :], v_ref[0:kl, :],
                                        cr_ref[...], cc_ref[:, 0:kl])

    return pl.pallas_call(
        body, name="fox_fwd", grid=(4, T // FOX_Q), in_specs=_fox_in_specs() + [pl.BlockSpec(memory_space=pl.ANY)],
        out_specs=pl.BlockSpec((FOX_Q, 128), lambda hp, qb: (qb, 4 + hp)),
        out_shape=jax.ShapeDtypeStruct((T, D), F32), input_output_aliases={5: 0},
        compiler_params=_cp(("parallel", "parallel")),
    )(proj, proj, proj, cum_r, cum_c, cat)


def _fox_bwd(proj, cum_r, cum_c, dcat):
    def body(q_ref, k_ref, v_ref, cr_ref, cc_ref, do_ref, dq_ref, dk_ref, dv_ref, dcr_ref, dcc_ref):
        qb = pl.program_id(1)

        @pl.when(qb == 0)
        def _():
            dk_ref[...] = jnp.zeros_like(dk_ref)
            dv_ref[...] = jnp.zeros_like(dv_ref)
            dcc_ref[...] = jnp.zeros_like(dcc_ref)

        fn = functools.partial(_fox_block, pl.program_id(0), qb)
        for g in range(FOX_GROUPS):
            kl = FOX_KEY_STEP * (g + 1)

            @pl.when(qb // FOX_Q_PER_GROUP == g)
            def _(kl=kl):
                _, vjp = jax.vjp(fn, q_ref[...], k_ref[0:kl, :], v_ref[0:kl, :], cr_ref[...], cc_ref[:, 0:kl])
                dq, dk, dv, dcr, dcc = vjp(do_ref[...])
                dq_ref[...] = dq
                dk_ref[0:kl, :] += dk
                dv_ref[0:kl, :] += dv
                dcr_ref[...] = dcr
                dcc_ref[:, 0:kl] += dcc

    sds = lambda *s: jax.ShapeDtypeStruct(s, F32)
    return pl.pallas_call(
        body, name="fox_bwd", grid=(4, T // FOX_Q),
        in_specs=_fox_in_specs() + [pl.BlockSpec((FOX_Q, 128), lambda hp, qb: (qb, 4 + hp))],
        out_specs=(pl.BlockSpec((FOX_Q, 128), lambda hp, qb: (qb, hp)),
                   pl.BlockSpec((T, 128), lambda hp, qb: (0, hp)),
                   pl.BlockSpec((T, 128), lambda hp, qb: (0, hp)),
                   pl.BlockSpec((None, FOX_Q, 128), lambda hp, qb: (hp, qb, 0)),
                   pl.BlockSpec((None, 8, T), lambda hp, qb: (hp, 0, 0))),
        out_shape=(sds(T, 512), sds(T, 512), sds(T, 512), sds(4, T, 128), sds(4, 8, T)),
        compiler_params=_cp(("parallel", "arbitrary")),
    )(proj, proj, proj, cum_r, cum_c, dcat)


def _rel_onehot(q):
    kj = _iota((REL_PAD, CA_BAND), 1)
    rel = jnp.clip(CA_PAD + q - kj, -128, 128) + 128
    return (_iota((REL_PAD, CA_BAND), 0) == rel).astype(F32)


def _bias_build(rbp):
    def body(rb_ref, o_ref):
        o_ref[...] = _hdot_raw(rb_ref[...], _rel_onehot(pl.program_id(0)), "nn")

    return pl.pallas_call(
        body, name="ca_bias_build", grid=(CHUNK,),
        in_specs=[pl.BlockSpec((8, REL_PAD), lambda q: (0, 0))],
        out_specs=pl.BlockSpec((None, 8, CA_BAND), lambda q: (q, 0, 0)),
        out_shape=jax.ShapeDtypeStruct((CHUNK, 8, CA_BAND), F32),
        compiler_params=_cp(("parallel",)),
    )(rbp)


def _bias_grad(dbias_q):
    def body(db_ref, o_ref):
        q = pl.program_id(0)

        @pl.when(q == 0)
        def _():
            o_ref[...] = jnp.zeros_like(o_ref)

        o_ref[...] += _hdot_raw(db_ref[...], _rel_onehot(q), "nt")

    return pl.pallas_call(
        body, name="ca_bias_grad", grid=(CHUNK,),
        in_specs=[pl.BlockSpec((None, 8, CA_BAND), lambda q: (q, 0, 0))],
        out_specs=pl.BlockSpec((8, REL_PAD), lambda q: (0, 0)),
        out_shape=jax.ShapeDtypeStruct((8, REL_PAD), F32),
        compiler_params=_cp(("arbitrary",)),
    )(dbias_q)


def _ca_block(c, q, kb, vb, bias2):
    lane = _iota((CHUNK, 128), 1)
    valid = (c * CHUNK - CA_PAD + _iota((CHUNK, CA_BAND), 1)) >= 0
    outs = []
    for e in range(2):
        qm = jnp.where((lane >= 64 * e) & (lane < 64 * (e + 1)), q, 0.0)
        s = bdot(qm, kb, "nt") * ATT_SCALE
        s = jnp.where(valid, s + bias2[e], NEG)
        outs.append(bdot(_softmax_rows(s), vb, "nn"))
    return jnp.where(lane < 64, outs[0], outs[1])


def _ca_fwd(proj, kvpad, bias):
    def body(q_ref, k_ref, v_ref, b_ref, o_ref):
        c = pl.program_id(1)
        band = pl.ds(pl.multiple_of(c * CHUNK, CHUNK), CA_BAND)
        o_ref[...] = _ca_block(c, q_ref[...], k_ref[band, :], v_ref[band, :], b_ref[...])

    return pl.pallas_call(
        body, name="ca_fwd", grid=(4, NCHUNK),
        in_specs=[pl.BlockSpec((CHUNK, 128), lambda hp, c: (c, hp)),
                  pl.BlockSpec((T + CA_PAD, 128), lambda hp, c: (0, hp)),
                  pl.BlockSpec((T + CA_PAD, 128), lambda hp, c: (0, 4 + hp)),
                  pl.BlockSpec((2, CHUNK, CA_BAND), lambda hp, c: (hp, 0, 0))],
        out_specs=pl.BlockSpec((CHUNK, 128), lambda hp, c: (c, hp)),
        out_shape=jax.ShapeDtypeStruct((T, D), F32),
        compiler_params=_cp(("parallel", "parallel")),
    )(proj, kvpad, kvpad, bias)


def _ca_bwd(proj, kvpad, bias, dcat):
    def body(q_ref, k_ref, v_ref, b_ref, do_ref, dq_ref, dk_ref, dv_ref, db_ref):
        c = pl.program_id(1)

        @pl.when(c == 0)
        def _():
            dk_ref[...] = jnp.zeros_like(dk_ref)
            dv_ref[...] = jnp.zeros_like(dv_ref)
            db_ref[...] = jnp.zeros_like(db_ref)

        band = pl.ds(pl.multiple_of(c * CHUNK, CHUNK), CA_BAND)
        fn = functools.partial(_ca_block, c)
        _, vjp = jax.vjp(fn, q_ref[...], k_ref[band, :], v_ref[band, :], b_ref[...])
        dq, dkb, dvb, db = vjp(do_ref[...])
        dq_ref[...] = dq
        dk_ref[band, :] += dkb
        dv_ref[band, :] += dvb
        db_ref[...] += db

    sds = lambda *s: jax.ShapeDtypeStruct(s, F32)
    padded = lambda: pl.BlockSpec((T + CA_PAD, 128), lambda hp, c: (0, hp))
    return pl.pallas_call(
        body, name="ca_bwd", grid=(4, NCHUNK),
        in_specs=[pl.BlockSpec((CHUNK, 128), lambda hp, c: (c, hp)),
                  pl.BlockSpec((T + CA_PAD, 128), lambda hp, c: (0, hp)),
                  pl.BlockSpec((T + CA_PAD, 128), lambda hp, c: (0, 4 + hp)),
                  pl.BlockSpec((2, CHUNK, CA_BAND), lambda hp, c: (hp, 0, 0)),
                  pl.BlockSpec((CHUNK, 128), lambda hp, c: (c, hp))],
        out_specs=(pl.BlockSpec((CHUNK, 128), lambda hp, c: (c, hp)), padded(), padded(),
                   pl.BlockSpec((2, CHUNK, CA_BAND), lambda hp, c: (hp, 0, 0))),
        out_shape=(sds(T, 512), sds(T + CA_PAD, 512), sds(T + CA_PAD, 512), sds(8, CHUNK, CA_BAND)),
        compiler_params=_cp(("parallel", "arbitrary")),
    )(proj, kvpad, kvpad, bias, dcat)


def _lru_pre(xs, cw, cb, wa, ba, wx, bx, lam):
    xc = cb + xs[0] * cw[0:1, :] + xs[1] * cw[1:2, :] + xs[2] * cw[2:3, :] + xs[3] * cw[3:4, :]
    ra = _sigmoid(bdot(xc, wa, "nn") + ba)
    ii = _sigmoid(bdot(xc, wx, "nn") + bx)
    la = 8.0 * ra * _log_sigmoid(lam)
    return jnp.exp(la), jnp.sqrt(-_expm1(2.0 * la)) * (ii * xc)


def _lru_pre_specs():
    full = lambda shape: pl.BlockSpec(shape, lambda i: (0,) * len(shape))
    return [pl.BlockSpec((4, ROWS, 512), lambda i: (0, i, 0)), full((4, 512)), full((1, 512)),
            full((512, 512)), full((1, 512)), full((512, 512)), full((1, 512)), full((1, 512))]


def _lru_pre_fwd(xs, cw, cb, wa, ba, wx, bx, lam):
    def body(xs_ref, cw_ref, cb_ref, wa_ref, ba_ref, wx_ref, bx_ref, lam_ref, a_ref, b_ref):
        a, b = _lru_pre(xs_ref[...], cw_ref[...], cb_ref[...], wa_ref[...], ba_ref[...], wx_ref[...], bx_ref[...],
                        lam_ref[...])
        a_ref[...] = a
        b_ref[...] = b

    row = pl.BlockSpec((ROWS, 512), lambda i: (i, 0))
    sds = jax.ShapeDtypeStruct((T, 512), F32)
    return pl.pallas_call(body, name="lru_pre_fwd", grid=(T // ROWS,), in_specs=_lru_pre_specs(),
                          out_specs=(row, row), out_shape=(sds, sds), compiler_params=_cp(("parallel",)),
                          )(xs, cw, cb, wa, ba, wx, bx, lam)


def _lru_pre_bwd(xs, cw, cb, wa, ba, wx, bx, lam, da, db):
    def body(xs_ref, cw_ref, cb_ref, wa_ref, ba_ref, wx_ref, bx_ref, lam_ref, da_ref, db_ref,
             dxs_ref, dcw_ref, dcb_ref, dwa_ref, dba_ref, dwx_ref, dbx_ref, dlam_ref):
        acc = (dcw_ref, dcb_ref, dwa_ref, dba_ref, dwx_ref, dbx_ref, dlam_ref)

        @pl.when(pl.program_id(0) == 0)
        def _():
            for r in acc:
                r[...] = jnp.zeros_like(r)

        _, vjp = jax.vjp(_lru_pre, xs_ref[...], cw_ref[...], cb_ref[...], wa_ref[...], ba_ref[...], wx_ref[...],
                         bx_ref[...], lam_ref[...])
        grads = vjp((da_ref[...], db_ref[...]))
        dxs_ref[...] = grads[0]
        for r, g in zip(acc, grads[1:]):
            r[...] += g

    row = pl.BlockSpec((ROWS, 512), lambda i: (i, 0))
    specs = _lru_pre_specs()
    sds = lambda *s: jax.ShapeDtypeStruct(s, F32)
    return pl.pallas_call(
        body, name="lru_pre_bwd", grid=(T // ROWS,), in_specs=specs + [row, row], out_specs=tuple(specs),
        out_shape=(sds(4, T, 512), sds(4, 512), sds(1, 512), sds(512, 512), sds(1, 512), sds(512, 512), sds(1, 512),
                   sds(1, 512)),
        compiler_params=_cp(("arbitrary",)),
    )(xs, cw, cb, wa, ba, wx, bx, lam, da, db)


def _lru_scan_fwd(a, b):
    def body(a_ref, b_ref, h_ref):
        def step(t, h):
            h = a_ref[pl.ds(t, 1), :] * h + b_ref[pl.ds(t, 1), :]
            h_ref[pl.ds(t, 1), :] = h
            return h

        lax.fori_loop(0, T, step, jnp.zeros((1, 512), F32))

    return pl.pallas_call(body, name="lru_scan_fwd", out_shape=jax.ShapeDtypeStruct((T, 512), F32),
                          compiler_params=pltpu.CompilerParams(vmem_limit_bytes=VMEM_LIMIT))(a, b)


def _lru_scan_bwd(a, h, dh):
    def body(a_ref, h_ref, dh_ref, da_ref, db_ref):
        def step(i, carry):
            t = T - 1 - i
            g = dh_ref[pl.ds(t, 1), :] + carry
            db_ref[pl.ds(t, 1), :] = g
            da_ref[pl.ds(t, 1), :] = g * h_ref[pl.ds(t - 1, 1), :]
            return a_ref[pl.ds(t, 1), :] * g

        carry = lax.fori_loop(0, T - 1, step, jnp.zeros((1, 512), F32))
        db_ref[pl.ds(0, 1), :] = dh_ref[pl.ds(0, 1), :] + carry
        da_ref[pl.ds(0, 1), :] = jnp.zeros((1, 512), F32)

    sds = jax.ShapeDtypeStruct((T, 512), F32)
    return pl.pallas_call(body, name="lru_scan_bwd", out_shape=(sds, sds),
                          compiler_params=pltpu.CompilerParams(vmem_limit_bytes=VMEM_LIMIT))(a, h, dh)


def _lru_post(h, gate):
    return h * _gelu_tanh(gate)


def _lru_post_fwd(h, proj, cat):
    def body(h_ref, g_ref, cat_ref, o_ref):
        o_ref[...] = _lru_post(h_ref[...], g_ref[...])

    row = pl.BlockSpec((ROWS, 512), lambda i: (i, 0))
    return pl.pallas_call(body, name="lru_post_fwd", grid=(T // ROWS,),
                          in_specs=[row, pl.BlockSpec((ROWS, 512), lambda i: (i, 3)), pl.BlockSpec(memory_space=pl.ANY)],
                          out_specs=pl.BlockSpec((ROWS, 512), lambda i: (i, 1)),
                          out_shape=jax.ShapeDtypeStruct((T, D), F32), input_output_aliases={2: 0},
                          compiler_params=_cp(("parallel",)))(h, proj, cat)


def _lru_post_bwd(h, proj, dcat):
    def body(h_ref, g_ref, do_ref, dh_ref, dg_ref):
        _, vjp = jax.vjp(_lru_post, h_ref[...], g_ref[...])
        dh, dg = vjp(do_ref[...])
        dh_ref[...] = dh
        dg_ref[...] = dg

    row = pl.BlockSpec((ROWS, 512), lambda i: (i, 0))
    sds = jax.ShapeDtypeStruct((T, 512), F32)
    return pl.pallas_call(body, name="lru_post_bwd", grid=(T // ROWS,),
                          in_specs=[row, pl.BlockSpec((ROWS, 512), lambda i: (i, 3)),
                                    pl.BlockSpec((ROWS, 512), lambda i: (i, 1))],
                          out_specs=(row, row), out_shape=(sds, sds), compiler_params=_cp(("parallel",)))(h, proj, dcat)


def _conv_dx(dxs_shift):
    def body(d_ref, o_ref):
        o_ref[...] = d_ref[0] + d_ref[1] + d_ref[2] + d_ref[3]

    row = pl.BlockSpec((ROWS, 512), lambda i: (i, 0))
    return pl.pallas_call(body, name="lru_conv_dx", grid=(T // ROWS,),
                          in_specs=[pl.BlockSpec((4, ROWS, 512), lambda i: (0, i, 0))], out_specs=row,
                          out_shape=jax.ShapeDtypeStruct((T, 512), F32), compiler_params=_cp(("parallel",)))(dxs_shift)


def _position():
    return lax.axis_index("x"), lax.axis_index("y"), lax.axis_index("c")


def _other_chips(x, y):
    return [(1 - x, y), (x, 1 - y), (1 - x, 1 - y)]


def _al(v, n):
    return v * n if isinstance(v, int) else pl.multiple_of(v * n, n)


_AG_ITEMS = [
    ((4, 32, 128), lambda o, s, h: o.at[s, pl.ds(_al(h, 16), 16), :], lambda r, h: r.at[pl.ds(_al(h, 16), 16), :]),
    ((4, 1024, 774), lambda o, s, h: o.at[s, pl.ds(_al(h, 512), 512), :], lambda r, h: r.at[pl.ds(_al(h, 512), 512), :]),
    ((1024, 1024), lambda o, s, h: o.at[pl.ds(_al(2 * s + h, 128), 128), :], lambda r, h: r.at[pl.ds(_al(h, 128), 128), :]),
    ((2, 1024, 4096), lambda o, s, h: o.at[h, :, pl.ds(_al(s, 1024), 1024)], lambda r, h: r.at[h]),
    ((2, 4096, 1024), lambda o, s, h: o.at[h, pl.ds(_al(s, 1024), 1024), :], lambda r, h: r.at[h]),
    ((1024, 2560), lambda o, s, h: o.at[pl.ds(_al(h, 512), 512), pl.ds(_al(s, 640), 640)],
     lambda r, h: r.at[pl.ds(_al(h, 512), 512), :]),
    ((1024, 1024), lambda o, s, h: o.at[pl.ds(_al(2 * s + h, 128), 128), :], lambda r, h: r.at[pl.ds(_al(h, 128), 128), :]),
]


def _allgather_weights(shards):
    n = len(_AG_ITEMS)

    def body(*refs):
        srcs, outs = refs[:n], refs[n:2 * n]
        send_sems, recv_sems = refs[2 * n:]
        x, y, c = _position()
        sibling = (x, y, 1 - c)
        chips = _other_chips(x, y)
        me = 2 * x + y
        waits = []
        for i, (_, dst, half) in enumerate(_AG_ITEMS):
            out_ref, src_ref = outs[i], srcs[i]

            def copy(k, slot, hc, to, src=None, dst=dst, out_ref=out_ref, i=i):
                there = dst(out_ref, slot, hc)
                return pltpu.make_async_remote_copy(
                    src_ref=there if src is None else src, dst_ref=there, send_sem=send_sems.at[6 * i + k],
                    recv_sem=recv_sems.at[6 * i + k], device_id=to, device_id_type=MESH)

            first = [copy(j, me, c, (*chip, c), src=half(src_ref, c)) for j, chip in enumerate(chips)]
            for cp in first:
                cp.start()
                waits.append(cp.wait_send)
        for i in range(n):
            dst, out_ref = _AG_ITEMS[i][1], outs[i]

            def copy(k, slot, hc, to, dst=dst, out_ref=out_ref, i=i):
                there = dst(out_ref, slot, hc)
                return pltpu.make_async_remote_copy(
                    src_ref=there, dst_ref=there, send_sem=send_sems.at[6 * i + k], recv_sem=recv_sems.at[6 * i + k],
                    device_id=to, device_id_type=MESH)

            for j, chip in enumerate(chips):
                slot = 2 * chip[0] + chip[1]
                copy(j, slot, c, (x, y, c)).wait_recv()
                passed = copy(3 + j, slot, c, sibling)
                passed.start()
                waits.append(passed.wait_send)
        for i in range(n):
            dst, out_ref = _AG_ITEMS[i][1], outs[i]
            for j, chip in enumerate(chips):
                there = dst(out_ref, 2 * chip[0] + chip[1], 1 - c)
                pltpu.make_async_remote_copy(
                    src_ref=there, dst_ref=there, send_sem=send_sems.at[6 * i + 3 + j],
                    recv_sem=recv_sems.at[6 * i + 3 + j], device_id=(x, y, c), device_id_type=MESH).wait_recv()
        for w in waits:
            w()

    any_spec = pl.BlockSpec(memory_space=pl.ANY)
    return pl.pallas_call(
        body, name="allgather_weights",
        in_specs=[any_spec] * n, out_specs=(any_spec,) * n,
        out_shape=tuple(jax.ShapeDtypeStruct(shape, s.dtype) for (shape, _, _), s in zip(_AG_ITEMS, shards)),
        scratch_shapes=[pltpu.SemaphoreType.DMA((6 * n,)), pltpu.SemaphoreType.DMA((6 * n,))],
    )(*shards)


def _pair_swap_cols(gb):
    _, rows, cols = gb.shape
    hc = cols // 2

    def body(g_ref, out_ref, send_sem, recv_sem):
        x, y, c = _position()
        cp = pltpu.make_async_remote_copy(src_ref=g_ref.at[:, :, pl.ds(_al(1 - c, hc), hc)], dst_ref=out_ref,
                                          send_sem=send_sem, recv_sem=recv_sem, device_id=(x, y, 1 - c),
                                          device_id_type=MESH)
        cp.start()
        cp.wait()

    return pl.pallas_call(
        body, name="grad_pair_swap",
        in_specs=[pl.BlockSpec(memory_space=pl.ANY)], out_specs=pl.BlockSpec(memory_space=pl.ANY),
        out_shape=jax.ShapeDtypeStruct((4, rows, hc), gb.dtype),
        scratch_shapes=[pltpu.SemaphoreType.DMA, pltpu.SemaphoreType.DMA],
    )(gb)


def _handover(halves):
    def body(in_ref, out_ref, send_sem, recv_sem):
        x, y, c = _position()
        cp = pltpu.make_async_remote_copy(src_ref=in_ref.at[c], dst_ref=out_ref.at[c], send_sem=send_sem,
                                          recv_sem=recv_sem, device_id=(x, y, 1 - c), device_id_type=MESH)
        cp.start()
        theirs = out_ref.at[1 - c]
        pltpu.make_async_remote_copy(src_ref=theirs, dst_ref=theirs, send_sem=send_sem, recv_sem=recv_sem,
                                     device_id=(x, y, c), device_id_type=MESH).wait_recv()
        cp.wait_send()

    return pl.pallas_call(
        body, name="grad_handover",
        in_specs=[pl.BlockSpec(memory_space=pl.ANY)], out_specs=pl.BlockSpec(memory_space=pl.ANY),
        out_shape=jax.ShapeDtypeStruct(halves.shape, halves.dtype), input_output_aliases={0: 0},
        scratch_shapes=[pltpu.SemaphoreType.DMA, pltpu.SemaphoreType.DMA],
    )(halves)


def _chip_alltoall(p):
    def body(p_ref, q_ref, send_sems, recv_sems, local_sem):
        x, y, c = _position()
        me = 2 * x + y
        chips = _other_chips(x, y)
        mine = pltpu.make_async_copy(p_ref.at[me], q_ref.at[me], local_sem)
        mine.start()
        sends = [pltpu.make_async_remote_copy(
            src_ref=p_ref.at[2 * chip[0] + chip[1]], dst_ref=q_ref.at[me], send_sem=send_sems.at[j],
            recv_sem=recv_sems.at[j], device_id=(*chip, c), device_id_type=MESH) for j, chip in enumerate(chips)]
        for cp in sends:
            cp.start()
        for j, chip in enumerate(chips):
            slot = q_ref.at[2 * chip[0] + chip[1]]
            pltpu.make_async_remote_copy(src_ref=slot, dst_ref=slot, send_sem=send_sems.at[j], recv_sem=recv_sems.at[j],
                                         device_id=(x, y, c), device_id_type=MESH).wait_recv()
        for cp in sends:
            cp.wait_send()
        mine.wait()

    return pl.pallas_call(
        body, name="grad_alltoall",
        in_specs=[pl.BlockSpec(memory_space=pl.ANY)], out_specs=pl.BlockSpec(memory_space=pl.ANY),
        out_shape=jax.ShapeDtypeStruct(p.shape, p.dtype),
        scratch_shapes=[pltpu.SemaphoreType.DMA((3,)), pltpu.SemaphoreType.DMA((3,)), pltpu.SemaphoreType.DMA],
    )(p)


COMM_ROWS = 512


def _pair_add(gb, recv, c_idx):
    _, rows, cols = gb.shape
    hc = cols // 2

    def body(c_ref, g_ref, r_ref, o_ref):
        o_ref[...] = (g_ref[...].astype(F32) + r_ref[...].astype(F32)).astype(o_ref.dtype)

    return pl.pallas_call(
        body, name="grad_pair_add",
        grid_spec=pltpu.PrefetchScalarGridSpec(
            num_scalar_prefetch=1, grid=(4, rows // COMM_ROWS),
            in_specs=[pl.BlockSpec((None, COMM_ROWS, hc), lambda s, j, c_ref: (s, j, c_ref[0])),
                      pl.BlockSpec((None, COMM_ROWS, hc), lambda s, j, c_ref: (s, j, 0))],
            out_specs=pl.BlockSpec((None, COMM_ROWS, hc), lambda s, j, c_ref: (s, j, 0))),
        out_shape=jax.ShapeDtypeStruct((4, rows, hc), gb.dtype),
        compiler_params=_cp(("parallel", "parallel")),
    )(c_idx, gb, recv)


def _sum_chips(q, c_idx):
    _, rows, hc = q.shape

    def body(c_ref, q_ref, o_ref):
        o_ref[...] = ((q_ref[0].astype(F32) + q_ref[1].astype(F32)) + q_ref[2].astype(F32)) + q_ref[3].astype(F32)

    return pl.pallas_call(
        body, name="grad_sum_chips",
        grid_spec=pltpu.PrefetchScalarGridSpec(
            num_scalar_prefetch=1, grid=(rows // COMM_ROWS,),
            in_specs=[pl.BlockSpec((4, COMM_ROWS, hc), lambda j, c_ref: (0, j, 0))],
            out_specs=pl.BlockSpec((None, COMM_ROWS, hc), lambda j, c_ref: (c_ref[0], j, 0))),
        out_shape=jax.ShapeDtypeStruct((2, rows, hc), F32),
        compiler_params=_cp(("parallel",)),
    )(c_idx, q)


def _shard_major(g, axis):
    shape = g.shape
    g = g.reshape(shape[:axis] + (4, shape[axis] // 4) + shape[axis + 1:])
    return jnp.moveaxis(g, axis, 0).reshape(4, -1)


def _unshard(g4, shape, axis):
    n = shape[axis] // 4
    g = g4.reshape((4,) + shape[:axis] + (n,) + shape[axis + 1:])
    return jnp.moveaxis(g, 0, axis).reshape(shape)


def _split(flat, shapes):
    out, off = [], 0
    for shp in shapes:
        n = 1
        for d in shp:
            n *= d
        out.append(flat[..., off:off + n].reshape(flat.shape[:-1] + tuple(shp)))
        off += n
    return out


def _even_cols_to_kernel(w):
    return jnp.concatenate([w[:, :1536], w[:, 1552:3088], w[:, 1536:1552], w[:, 3088:3096],
                            jnp.zeros((w.shape[0], PE - 3096), w.dtype)], axis=1)


def _block_diag(w):
    out = jnp.zeros((512, 512), w.dtype)
    for n in range(8):
        out = lax.dynamic_update_slice(out, w[n], (64 * n, 64 * n))
    return out


def _diag_blocks(g):
    return jnp.stack([g[64 * n:64 * (n + 1), 64 * n:64 * (n + 1)] for n in range(8)])


def _shift_down(a, s):
    return a if s == 0 else jnp.pad(a, ((s, 0), (0, 0)))[:a.shape[0]]


def _shift_up(a, s):
    return a if s == 0 else jnp.pad(a, ((0, s), (0, 0)))[s:]


SMALL_SHARDED_SHAPES = [(2, 4, 256), (16, 64), (4, 128), (128,), (128,), (128,), (128,)]
REPL_SHAPES = [(256,), (512,), (8,), (8, 257), (8, 64, 64), (8, 64, 64)]


def kernel(x, norm_w, w_in_even, gla_w_a_up, gla_b_a, gla_norm_w, fox_b_f, w_out_even, w_in_odd, rel_bias, conv_w, conv_b, lru_w_a, lru_b_a, lru_w_x, lru_b_x, lru_lambda, w_out_odd, w_mlp_up, w_mlp_down, loss_target, m_norm_w, m_w_in_even, m_gla_w_a_up, m_gla_b_a, m_gla_norm_w, m_fox_b_f, m_w_out_even, m_w_in_odd, m_rel_bias, m_conv_w, m_conv_b, m_lru_w_a, m_lru_b_a, m_lru_w_x, m_lru_b_x, m_lru_lambda, m_w_out_odd, m_w_mlp_up, m_w_mlp_down, v_norm_w, v_w_in_even, v_gla_w_a_up, v_gla_b_a, v_gla_norm_w, v_fox_b_f, v_w_out_even, v_w_in_odd, v_rel_bias, v_conv_w, v_conv_b, v_lru_w_a, v_lru_b_a, v_lru_w_x, v_lru_b_x, v_lru_lambda, v_w_out_odd, v_w_mlp_up, v_w_mlp_down):
    c_idx = lax.axis_index("c")

    small_local = [norm_w, gla_w_a_up[0], conv_w[0], conv_b[0], lru_b_a[0], lru_b_x[0], lru_lambda[0]]
    small_src = jnp.concatenate([a.reshape(-1) for a in small_local]).reshape(32, 128)
    mine = [small_src, w_in_even[0].astype(BF16), w_out_even[0].astype(BF16), w_mlp_up.astype(BF16),
            w_mlp_down.astype(BF16), w_in_odd[0].astype(BF16), w_out_odd[0].astype(BF16)]
    small4, w_in_e4, w_out_e, w_up, w_dn, w_in_o, w_out_o = _allgather_weights(mine)
    me = 2 * lax.axis_index("x") + lax.axis_index("y")
    dus = lax.dynamic_update_slice
    small4 = dus(small4, mine[0][None], (me, 0, 0))
    w_in_e4 = dus(w_in_e4, mine[1][None], (me, 0, 0))
    w_out_e = dus(w_out_e, mine[2], (256 * me, 0))
    w_up = dus(w_up, mine[3], (0, 0, 1024 * me))
    w_dn = dus(w_dn, mine[4], (0, 1024 * me, 0))
    w_in_o = dus(w_in_o, mine[5], (0, 640 * me))
    w_out_o = dus(w_out_o, mine[6], (256 * me, 0))

    w_in_e = _even_cols_to_kernel(_unshard(w_in_e4, (1024, 3096), 1))
    g_small = _split(small4.reshape(4, 32 * 128), SMALL_SHARDED_SHAPES)
    nw_full = _unshard(g_small[0], (2, 4, 1024), 2)
    wa_up = _unshard(g_small[1], (16, 256), 1)
    cw = _unshard(g_small[2], (4, 512), 1)
    cb, lba, lbx, lam = [_unshard(g, (512,), 0).reshape(1, 512) for g in g_small[3:]]
    nw = lambda layer, i: nw_full[layer, i].reshape(1, D)

    wa_pad = jnp.pad(wa_up, ((0, 128 - 16), (0, 0)))
    gla_ba = gla_b_a.reshape(1, 256)
    gla_nw = gla_norm_w.reshape(1, 512)
    fox_bpad = jnp.pad(fox_b_f.reshape(1, 8), ((0, 0), (FOX_LANE0, 128 - FOX_LANE0 - 8)))
    rbp = jnp.pad(rel_bias[0], ((0, 0), (0, REL_PAD - 257)))
    wa_bd = _block_diag(lru_w_a[0])
    wx_bd = _block_diag(lru_w_x[0])

    x0 = x[0]
    tgt = loss_target[0]

    h0 = _prenorm(x0, nw(0, 0), "prenorm_l0_mix")
    proj_e = _mm(h0, w_in_e, "nn", tm=1024, tn=640, name="mm_in_even")
    cat0, s_prev = _gla_fwd(proj_e, wa_pad, gla_ba, gla_nw)
    cum_r = _fox_gate_fwd(proj_e, fox_bpad)
    cum_c = cum_r[:, FOX_LANE0:FOX_LANE0 + 8].T
    cat0 = _fox_fwd(proj_e, cum_r, cum_c, cat0)
    mix0 = _mm(cat0, w_out_e, "nn", tm=1024, tn=512, name="mm_out_even")
    x1 = _postnorm(x0, mix0, nw(0, 1), "postnorm_l0_mix")
    h1 = _prenorm(x1, nw(0, 2), "prenorm_l0_mlp")
    a0, r0 = _mm(h1, w_up, "nn", tm=1024, tn=1024, b_layer=0, relu_pair=True, name="mm_up_l0")
    d0 = _mm(a0, w_dn, "nn", tm=1024, tn=512, tk=2048, b_layer=0, name="mm_down_l0")
    x2 = _postnorm(x1, d0, nw(0, 3), "postnorm_l0_mlp")

    h2 = _prenorm(x2, nw(1, 0), "prenorm_l1_mix")
    proj_o = _mm(h2, w_in_o, "nn", tm=1024, tn=640, name="mm_in_odd")
    bias_q = _bias_build(rbp)
    bias = bias_q.transpose(1, 0, 2)
    kvpad = jnp.pad(proj_o[:, 512:1536], ((CA_PAD, 0), (0, 0)))
    cat1 = _ca_fwd(proj_o, kvpad, bias)
    x_in = proj_o[:, 2048:2560]
    xs = jnp.stack([_shift_down(x_in, 3 - j) for j in range(4)])
    lru_a, lru_b = _lru_pre_fwd(xs, cw, cb, wa_bd, lba, wx_bd, lbx, lam)
    hh = _lru_scan_fwd(lru_a, lru_b)
    cat1 = _lru_post_fwd(hh, proj_o, cat1)
    mix1 = _mm(cat1, w_out_o, "nn", tm=1024, tn=512, name="mm_out_odd")
    x3 = _postnorm(x2, mix1, nw(1, 1), "postnorm_l1_mix")
    h3 = _prenorm(x3, nw(1, 2), "prenorm_l1_mlp")
    a1, r1 = _mm(h3, w_up, "nn", tm=1024, tn=1024, b_layer=1, relu_pair=True, name="mm_up_l1")
    d1 = _mm(a1, w_dn, "nn", tm=1024, tn=512, tk=2048, b_layer=1, name="mm_down_l1")
    x4 = _postnorm(x3, d1, nw(1, 3), "postnorm_l1_mlp")

    g4, loss_part = _loss_and_grad(x4, tgt)
    loss = lax.psum(loss_part[0, 0], ("x", "y", "c"))

    gb = lax.empty((4, GB_ROWS, D), BF16)
    dd1, dnw13 = _norm_bwd(d1, nw(1, 3), g4, None, "postnorm_l1_mlp_bwd")
    gb = _mm(a1, dd1, "tn", tm=512, tn=1024, into=(gb, 1024, GB_DN[1]), name="mm_down_l1_dw")
    du1 = _mm(dd1, w_dn, "nt", tm=1024, tn=1024, b_layer=1, times2=r1, out_dtype=BF16, name="mm_down_l1_dx")
    gb = _mm(du1, h3, "tn", tm=512, tn=1024, into=(gb, 1024, GB_UP[1]), name="mm_up_l1_dw")
    dh3 = _mm(du1, w_up, "nt", tm=1024, tn=512, tk=2048, b_layer=1, name="mm_up_l1_dx")
    g3, dnw12 = _norm_bwd(x3, nw(1, 2), dh3, g4, "prenorm_l1_mlp_bwd")
    dmix1, dnw11 = _norm_bwd(mix1, nw(1, 1), g3, None, "postnorm_l1_mix_bwd")
    gb = _mm(cat1, dmix1, "tn", tm=128, tn=1024, into=(gb, 256, GB_OUT_O), name="mm_out_odd_dw")
    dcat1 = _mm(dmix1, w_out_o, "nt", tm=1024, tn=512, name="mm_out_odd_dx")

    dq_c, dkpad, dvpad, dbias = _ca_bwd(proj_o, kvpad, bias, dcat1)
    g_rel = _bias_grad(dbias.transpose(1, 0, 2))[:, :257]
    dhh, dgate = _lru_post_bwd(hh, proj_o, dcat1)
    da_l, db_l = _lru_scan_bwd(lru_a, hh, dhh)
    dxs, g_cw, g_cb, g_wa_bd, g_lba, g_wx_bd, g_lbx, g_lam = _lru_pre_bwd(xs, cw, cb, wa_bd, lba, wx_bd, lbx, lam, da_l, db_l)
    dx_in = _conv_dx(jnp.stack([_shift_up(dxs[j], 3 - j) for j in range(4)]))
    dproj_o = jnp.concatenate([dq_c, dkpad[CA_PAD:], dvpad[CA_PAD:], dgate, dx_in], axis=1)
    gb = _mm(dproj_o, h2, "tn", tm=128, tn=1024, into=(gb, 640, GB_IN_O), name="mm_in_odd_dw")
    dh2 = _mm(dproj_o, w_in_o, "nt", tm=1024, tn=512, tk=1280, name="mm_in_odd_dx")
    g2, dnw10 = _norm_bwd(x2, nw(1, 0), dh2, g3, "prenorm_l1_mix_bwd")

    dd0, dnw03 = _norm_bwd(d0, nw(0, 3), g2, None, "postnorm_l0_mlp_bwd")
    gb = _mm(a0, dd0, "tn", tm=512, tn=1024, into=(gb, 1024, GB_DN[0]), name="mm_down_l0_dw")
    du0 = _mm(dd0, w_dn, "nt", tm=1024, tn=1024, b_layer=0, times2=r0, out_dtype=BF16, name="mm_down_l0_dx")
    gb = _mm(du0, h1, "tn", tm=512, tn=1024, into=(gb, 1024, GB_UP[0]), name="mm_up_l0_dw")
    dh1 = _mm(du0, w_up, "nt", tm=1024, tn=512, tk=2048, b_layer=0, name="mm_up_l0_dx")
    g1, dnw02 = _norm_bwd(x1, nw(0, 2), dh1, g2, "prenorm_l0_mlp_bwd")
    dmix0, dnw01 = _norm_bwd(mix0, nw(0, 1), g1, None, "postnorm_l0_mix_bwd")
    gb = _mm(cat0, dmix0, "tn", tm=128, tn=1024, into=(gb, 256, GB_OUT_E), name="mm_out_even_dw")
    dcat0 = _mm(dmix0, w_out_e, "nt", tm=1024, tn=512, name="mm_out_even_dx")

    dq_g, dk_g, dv_g, dr_g, daux_g, g_wa_pad, g_gla_ba, g_gla_nw = _gla_bwd(proj_e, s_prev, wa_pad, gla_ba, gla_nw, dcat0)
    dq_f, dk_f, dv_f, dcrow, dccol = _fox_bwd(proj_e, cum_r, cum_c, dcat0)
    dccol_t = jnp.pad(dccol.sum(axis=0).T, ((0, 0), (FOX_LANE0, 128 - FOX_LANE0 - 8)))
    daux, g_fox_bpad = _fox_gate_bwd(proj_e, fox_bpad, dcrow, dccol_t, daux_g)
    dproj_e = jnp.concatenate([dq_g, dk_g, dv_g, dr_g, dq_f, dk_f, dv_f, daux], axis=1)
    gt_in_e = _mm(dproj_e, h0, "tn", tm=640, tn=1024, name="mm_in_even_dw")
    dh0 = _mm(dproj_e, w_in_e, "nt", tm=1024, tn=512, tk=640, name="mm_in_even_dx")
    grad_x, dnw00 = _norm_bwd(x0, nw(0, 0), dh0, g1, "prenorm_l0_mix_bwd")

    g_norm = jnp.stack([jnp.concatenate([dnw00, dnw01, dnw02, dnw03]), jnp.concatenate([dnw10, dnw11, dnw12, dnw13])])
    sharded = [(g_norm, 2), (g_wa_pad[:16], 1), (g_cw, 1), (g_cb[0], 0), (g_lba[0], 0), (g_lbx[0], 0), (g_lam[0], 0)]
    replicated = [g_gla_ba[0], g_gla_nw[0], g_fox_bpad[0, FOX_LANE0:FOX_LANE0 + 8], g_rel, _diag_blocks(g_wa_bd),
                  _diag_blocks(g_wx_bd)]
    small4 = jnp.concatenate([_shard_major(g, ax) for g, ax in sharded]
                             + [jnp.broadcast_to(g.reshape(1, -1), (4, g.size)) for g in replicated], axis=1)
    n_small = small4.shape[1]
    small_rows = GB_ROWS - GB_TAIL - 774
    small4 = jnp.pad(small4, ((0, 0), (0, small_rows * D - n_small))).reshape(4, small_rows, D)
    gt_rows = jnp.concatenate([gt_in_e[:1536], gt_in_e[3072:3088], gt_in_e[1536:3072], gt_in_e[3088:3096]], axis=0)
    tail = jnp.concatenate([gt_rows.reshape(4, 774, D), small4], axis=1).astype(BF16)
    gb = lax.dynamic_update_slice(gb, tail, (0, GB_TAIL, 0))

    c_arr = c_idx.reshape(1).astype(jnp.int32)
    recv_half = _pair_swap_cols(gb)
    pair_sum = _pair_add(gb, recv_half, c_arr)
    from_chips = _chip_alltoall(pair_sum)
    halves = _handover(_sum_chips(from_chips, c_arr))
    reduced = jnp.concatenate([halves[0], halves[1]], axis=1)

    g_up = reduced[GB_UP[0]:GB_UP[0] + 2048].reshape(2, 1024, 1024).transpose(0, 2, 1)
    g_dn = reduced[GB_DN[0]:GB_DN[0] + 2048].reshape(2, 1024, 1024)
    g_small = _split(reduced[GB_TAIL + 774:].reshape(-1)[:n_small], SMALL_SHARDED_SHAPES + REPL_SHAPES)
    g_of = dict(zip(["norm_w", "gla_w_a_up", "conv_w", "conv_b", "lru_b_a", "lru_b_x", "lru_lambda", "gla_b_a",
                     "gla_norm_w", "fox_b_f", "rel_bias", "lru_w_a", "lru_w_x"], g_small))
    g_of.update(w_mlp_up=g_up, w_mlp_down=g_dn, w_in_odd=reduced[GB_IN_O:GB_IN_O + 640].T,
                w_out_even=reduced[GB_OUT_E:GB_OUT_E + 256], w_out_odd=reduced[GB_OUT_O:GB_OUT_O + 256],
                w_in_even=reduced[GB_TAIL:GB_TAIL + 774].T)

    names = ["norm_w", "w_in_even", "gla_w_a_up", "gla_b_a", "gla_norm_w", "fox_b_f", "w_out_even", "w_in_odd", "rel_bias",
             "conv_w", "conv_b", "lru_w_a", "lru_b_a", "lru_w_x", "lru_b_x", "lru_lambda", "w_out_odd", "w_mlp_up",
             "w_mlp_down"]
    w_of = dict(norm_w=norm_w, w_in_even=w_in_even, gla_w_a_up=gla_w_a_up, gla_b_a=gla_b_a, gla_norm_w=gla_norm_w,
                fox_b_f=fox_b_f, w_out_even=w_out_even, w_in_odd=w_in_odd, rel_bias=rel_bias, conv_w=conv_w, conv_b=conv_b,
                lru_w_a=lru_w_a, lru_b_a=lru_b_a, lru_w_x=lru_w_x, lru_b_x=lru_b_x, lru_lambda=lru_lambda,
                w_out_odd=w_out_odd, w_mlp_up=w_mlp_up, w_mlp_down=w_mlp_down)
    m_of = dict(norm_w=m_norm_w, w_in_even=m_w_in_even, gla_w_a_up=m_gla_w_a_up, gla_b_a=m_gla_b_a,
                gla_norm_w=m_gla_norm_w, fox_b_f=m_fox_b_f, w_out_even=m_w_out_even, w_in_odd=m_w_in_odd,
                rel_bias=m_rel_bias, conv_w=m_conv_w, conv_b=m_conv_b, lru_w_a=m_lru_w_a, lru_b_a=m_lru_b_a,
                lru_w_x=m_lru_w_x, lru_b_x=m_lru_b_x, lru_lambda=m_lru_lambda, w_out_odd=m_w_out_odd,
                w_mlp_up=m_w_mlp_up, w_mlp_down=m_w_mlp_down)
    v_of = dict(norm_w=v_norm_w, w_in_even=v_w_in_even, gla_w_a_up=v_gla_w_a_up, gla_b_a=v_gla_b_a,
                gla_norm_w=v_gla_norm_w, fox_b_f=v_fox_b_f, w_out_even=v_w_out_even, w_in_odd=v_w_in_odd,
                rel_bias=v_rel_bias, conv_w=v_conv_w, conv_b=v_conv_b, lru_w_a=v_lru_w_a, lru_b_a=v_lru_b_a,
                lru_w_x=v_lru_w_x, lru_b_x=v_lru_b_x, lru_lambda=v_lru_lambda, w_out_odd=v_w_out_odd,
                w_mlp_up=v_w_mlp_up, w_mlp_down=v_w_mlp_down)
    grads, deltas, new_ms, new_vs = [], [], [], []
    for n in names:
        w = w_of[n]
        view = w.shape if w.ndim <= 3 else w.shape[-3:]
        g = g_of[n].reshape(w.shape)
        d, mn, vn = _adamw(w.reshape(view), g.reshape(view), m_of[n].reshape(view), v_of[n].reshape(view),
                           "adamw_" + n)
        grads.append(g)
        deltas.append(d.reshape(w.shape))
        new_ms.append(mn.reshape(w.shape))
        new_vs.append(vn.reshape(w.shape))

    return (loss, grad_x.reshape(1, T, D), *grads, *deltas, *new_ms, *new_vs)
```

```python
import functools

import jax
import jax.numpy as jnp
from jax import lax
from jax.experimental import pallas as pl
from jax.experimental.pallas import tpu as pltpu

F32 = jnp.float32
BF16 = jnp.bfloat16
MESH = pl.DeviceIdType.MESH

T = 2048
D = 1024
DFF = 4096
EPS = 1e-6
CHUNK = 64
NCHUNK = T // CHUNK
PE = 3200
PO = 2560
AUX_BLK = 3072 // 128
FOX_LANE0 = 16
GLA_SCALE = 64 ** -0.5
ATT_SCALE = 64 ** -0.5
NEG = float(jnp.finfo(jnp.float32).min)
CA_BAND = 576
CA_PAD = 512
REL_PAD = 384

VMEM_LIMIT = 48 * 1024 * 1024

ADAM_LR, ADAM_B1, ADAM_B2, ADAM_EPS, ADAM_WD, ADAM_STEP = 0.001, 0.9, 0.999, 1e-08, 0.01, 10

GA_ROWS, GA_UP, GA_DN, GA_IN_O, GA_OUT_O = 3072, 0, 1024, 2048, 2688
GB_ROWS, GB_UP, GB_DN = 2048, 0, 1024
GC_ROWS, GC_OUT_E, GC_TAIL = 1152, 0, 256

_DIMS = {"nn": (((1,), (0,)), ((), ())), "nt": (((1,), (1,)), ((), ())), "tn": (((0,), (0,)), ((), ()))}


def _cp(sem, **kw):
    return pltpu.CompilerParams(dimension_semantics=sem, vmem_limit_bytes=VMEM_LIMIT, **kw)


def _dot(a, b, mode):
    return lax.dot_general(a.astype(BF16), b.astype(BF16), _DIMS[mode], preferred_element_type=F32)


@functools.partial(jax.custom_vjp, nondiff_argnums=(2,))
def bdot(a, b, mode):
    return _dot(a, b, mode)


def _bdot_fwd(a, b, mode):
    return _dot(a, b, mode), (a, b)


def _bdot_bwd(mode, res, g):
    a, b = res
    if mode == "nn":
        da, db = _dot(g, b, "nt"), _dot(a, g, "tn")
    elif mode == "nt":
        da, db = _dot(g, b, "nn"), _dot(g, a, "tn")
    else:
        da, db = _dot(b, g, "nt"), _dot(a, g, "nn")
    return da.astype(a.dtype), db.astype(b.dtype)


bdot.defvjp(_bdot_fwd, _bdot_bwd)


def _hdot_raw(a, b, mode):
    return lax.dot_general(a, b, _DIMS[mode], precision=lax.Precision.HIGHEST, preferred_element_type=F32)


@functools.partial(jax.custom_vjp, nondiff_argnums=(2,))
def hdot(a, b, mode):
    return _hdot_raw(a, b, mode)


def _hdot_fwd(a, b, mode):
    return _hdot_raw(a, b, mode), (a, b)


def _hdot_bwd(mode, res, g):
    a, b = res
    if mode == "nn":
        return _hdot_raw(g, b, "nt"), _hdot_raw(a, g, "tn")
    if mode == "nt":
        return _hdot_raw(g, b, "nn"), _hdot_raw(g, a, "tn")
    return _hdot_raw(b, g, "nt"), _hdot_raw(a, g, "nn")


hdot.defvjp(_hdot_fwd, _hdot_bwd)


def _log_sigmoid(x):
    return jnp.minimum(x, 0.0) - jnp.log(1.0 + jnp.exp(-jnp.abs(x)))


def _sigmoid(x):
    return 1.0 / (1.0 + jnp.exp(-x))


def _expm1(x):
    series = x * (1.0 + x * 0.5 * (1.0 + x * (1.0 / 3.0) * (1.0 + x * 0.25)))
    return jnp.where(jnp.abs(x) < 0.03, series, jnp.exp(x) - 1.0)


def _gelu_tanh(x):
    return 0.5 * x * (1.0 + jnp.tanh(0.7978845608028654 * (x + 0.044715 * x * x * x)))


def _softmax_rows(s):
    m = jnp.max(s, axis=-1, keepdims=True)
    p = jnp.exp(s - m)
    return p / jnp.sum(p, axis=-1, keepdims=True)


def _iota(shape, dim):
    return lax.broadcasted_iota(jnp.int32, shape, dim)


def _mm(a, b, mode, *, tm, tn, tk=None, out_dtype=F32, name, b_layer=None, into=None, relu_pair=False, times2=None):
    b2 = b.shape[-2:]
    if mode == "nn":
        (m, k), n = a.shape, b2[1]
    elif mode == "nt":
        (m, k), n = a.shape, b2[0]
    else:
        (k, m), n = a.shape, b2[1]
    tk = k if tk is None else tk
    assert m % tm == 0 and n % tn == 0 and k % tk == 0, (name, a.shape, b.shape)
    nk = k // tk
    a_spec = {"nn": pl.BlockSpec((tm, tk), lambda i, j, kk: (i, kk)),
              "nt": pl.BlockSpec((tm, tk), lambda i, j, kk: (i, kk)),
              "tn": pl.BlockSpec((tk, tm), lambda i, j, kk: (kk, i))}[mode]
    b_blk = {"nn": (tk, tn), "nt": (tn, tk), "tn": (tk, tn)}[mode]
    b_idx = {"nn": lambda i, j, kk: (kk, j), "nt": lambda i, j, kk: (j, kk), "tn": lambda i, j, kk: (kk, j)}[mode]
    if b_layer is None:
        b_spec = pl.BlockSpec(b_blk, b_idx)
    else:
        b_spec = pl.BlockSpec((None,) + b_blk, lambda i, j, kk: (b_layer,) + b_idx(i, j, kk))

    tile = pl.BlockSpec((tm, tn), lambda i, j, kk: (i, j))
    if into is not None:
        buf, per_slot, row_off = into
        assert m == 4 * per_slot and per_slot % tm == 0 and row_off % tm == 0 and buf.shape[2] == n, (name, buf.shape)
        bps = per_slot // tm
        out_specs = pl.BlockSpec((None, tm, tn), lambda i, j, kk: (i // bps, row_off // tm + i % bps, j))
        out_shape = jax.ShapeDtypeStruct(buf.shape, buf.dtype)
        extra_in, extra_specs, aliases = [buf], [pl.BlockSpec(memory_space=pl.ANY)], {2: 0}
        finish = lambda acc, extra: [acc.astype(buf.dtype)]
    elif relu_pair:
        out_specs = (tile, tile)
        out_shape = (jax.ShapeDtypeStruct((m, n), BF16),) * 2
        extra_in, extra_specs, aliases = [], [], {}

        def finish(acc, extra):
            r = jnp.maximum(acc, 0.0)
            return [(r * r).astype(BF16), r.astype(BF16)]
    elif times2 is not None:
        out_specs = tile
        out_shape = jax.ShapeDtypeStruct((m, n), out_dtype)
        extra_in, extra_specs, aliases = [times2], [tile], {}
        finish = lambda acc, extra: [(acc * (2.0 * extra[...].astype(F32))).astype(out_dtype)]
    else:
        out_specs = tile
        out_shape = jax.ShapeDtypeStruct((m, n), out_dtype)
        extra_in, extra_specs, aliases = [], [], {}
        finish = lambda acc, extra: [acc.astype(out_dtype)]
    n_out = 2 if relu_pair else 1

    def body(*refs):
        a_ref, b_ref = refs[0], refs[1]
        extra = refs[2] if extra_in else None
        o_refs = refs[2 + len(extra_in):2 + len(extra_in) + n_out]

        def store(acc):
            for o_ref, val in zip(o_refs, finish(acc, extra)):
                o_ref[...] = val

        if nk == 1:
            store(_dot(a_ref[...], b_ref[...], mode))
            return
        acc_ref = refs[-1]
        kk = pl.program_id(2)

        @pl.when(kk == 0)
        def _():
            acc_ref[...] = jnp.zeros_like(acc_ref)

        acc_ref[...] += _dot(a_ref[...], b_ref[...], mode)

        @pl.when(kk == nk - 1)
        def _():
            store(acc_ref[...])

    return pl.pallas_call(
        body, name=name, grid=(m // tm, n // tn, nk),
        in_specs=[a_spec, b_spec] + extra_specs,
        out_specs=out_specs, out_shape=out_shape,
        scratch_shapes=[pltpu.VMEM((tm, tn), F32)] if nk > 1 else [],
        input_output_aliases=aliases,
        compiler_params=_cp(("parallel", "parallel", "arbitrary")),
    )(a, b, *extra_in)


ROWS = 256


def _prenorm(x, w, name):
    def body(x_ref, w_ref, o_ref):
        xv = x_ref[...]
        r = lax.rsqrt(jnp.mean(xv * xv, axis=-1, keepdims=True) + EPS)
        o_ref[...] = (xv * r * w_ref[...]).astype(BF16)

    return pl.pallas_call(
        body, name=name, grid=(T // ROWS,),
        in_specs=[pl.BlockSpec((ROWS, D), lambda i: (i, 0)), pl.BlockSpec((1, D), lambda i: (0, 0))],
        out_specs=pl.BlockSpec((ROWS, D), lambda i: (i, 0)),
        out_shape=jax.ShapeDtypeStruct((T, D), BF16),
        compiler_params=_cp(("parallel",)),
    )(x, w)


def _postnorm(x, z, w, name):
    def body(x_ref, z_ref, w_ref, o_ref):
        zv = z_ref[...]
        r = lax.rsqrt(jnp.mean(zv * zv, axis=-1, keepdims=True) + EPS)
        o_ref[...] = x_ref[...] + zv * r * w_ref[...]

    return pl.pallas_call(
        body, name=name, grid=(T // ROWS,),
        in_specs=[pl.BlockSpec((ROWS, D), lambda i: (i, 0)), pl.BlockSpec((ROWS, D), lambda i: (i, 0)),
                  pl.BlockSpec((1, D), lambda i: (0, 0))],
        out_specs=pl.BlockSpec((ROWS, D), lambda i: (i, 0)),
        out_shape=jax.ShapeDtypeStruct((T, D), F32),
        compiler_params=_cp(("parallel",)),
    )(x, z, w)


def _norm_bwd(z, w, dy, add, name):
    has_add = add is not None

    def body(*refs):
        if has_add:
            z_ref, w_ref, dy_ref, add_ref, dz_ref, dw_ref = refs
        else:
            z_ref, w_ref, dy_ref, dz_ref, dw_ref = refs
        i = pl.program_id(0)

        @pl.when(i == 0)
        def _():
            dw_ref[...] = jnp.zeros_like(dw_ref)

        zv = z_ref[...].astype(F32)
        dyv = dy_ref[...]
        r = lax.rsqrt(jnp.mean(zv * zv, axis=-1, keepdims=True) + EPS)
        wdy = dyv * w_ref[...]
        dz = r * wdy - zv * (r * r * r) * jnp.mean(zv * wdy, axis=-1, keepdims=True)
        if has_add:
            dz = dz + add_ref[...]
        dz_ref[...] = dz.astype(dz_ref.dtype)
        dw_ref[...] += jnp.sum(dyv * zv * r, axis=0, keepdims=True)

    row = pl.BlockSpec((ROWS, D), lambda i: (i, 0))
    vec = pl.BlockSpec((1, D), lambda i: (0, 0))
    ins = [z, w, dy] + ([add] if has_add else [])
    dz_dtype = F32 if has_add else BF16
    return pl.pallas_call(
        body, name=name, grid=(T // ROWS,),
        in_specs=[row, vec, row] + ([row] if has_add else []),
        out_specs=(row, vec),
        out_shape=(jax.ShapeDtypeStruct((T, D), dz_dtype), jax.ShapeDtypeStruct((1, D), F32)),
        compiler_params=_cp(("arbitrary",)),
    )(*ins)


def _loss_and_grad(y, tgt):
    def body(y_ref, t_ref, g_ref, l_ref):
        i = pl.program_id(0)

        @pl.when(i == 0)
        def _():
            l_ref[...] = jnp.zeros_like(l_ref)

        e = y_ref[...] - t_ref[...]
        g_ref[...] = e * (1.0 / D)
        l_ref[...] += jnp.sum(e * e) * (0.5 / D)

    row = pl.BlockSpec((ROWS, D), lambda i: (i, 0))
    return pl.pallas_call(
        body, name="loss_head", grid=(T // ROWS,), in_specs=[row, row],
        out_specs=(row, pl.BlockSpec((1, 128), lambda i: (0, 0))),
        out_shape=(jax.ShapeDtypeStruct((T, D), F32), jax.ShapeDtypeStruct((1, 128), F32)),
        compiler_params=_cp(("arbitrary",)),
    )(y, tgt)


def _adamw(w, g, m, v, name):
    lead = w.shape[:-2]
    assert len(lead) <= 1 and g.shape == w.shape, (name, w.shape, g.shape)
    rows, cols = w.shape[-2:]
    tr = rows if rows <= 512 else 256
    assert rows % tr == 0, (name, w.shape)
    c1 = 1.0 - ADAM_B1 ** ADAM_STEP
    c2 = 1.0 - ADAM_B2 ** ADAM_STEP

    def body(w_ref, g_ref, m_ref, v_ref, d_ref, mo_ref, vo_ref):
        gv = g_ref[...]
        mn = ADAM_B1 * m_ref[...] + (1.0 - ADAM_B1) * gv
        vn = ADAM_B2 * v_ref[...] + (1.0 - ADAM_B2) * (gv * gv)
        m_hat = mn / c1
        v_hat = vn / c2
        d_ref[...] = -ADAM_LR * (m_hat / (jnp.sqrt(v_hat) + ADAM_EPS) + ADAM_WD * w_ref[...])
        mo_ref[...] = mn
        vo_ref[...] = vn

    if lead:
        grid = (lead[0], rows // tr)
        blk = pl.BlockSpec((None, tr, cols), lambda l, i: (l, i, 0))
    else:
        grid = (rows // tr,)
        blk = pl.BlockSpec((tr, cols), lambda i: (i, 0))
    sds = jax.ShapeDtypeStruct(w.shape, F32)
    return pl.pallas_call(body, name=name, grid=grid, in_specs=[blk] * 4, out_specs=(blk,) * 3,
                          out_shape=(sds,) * 3, compiler_params=_cp(("parallel",) * len(grid)))(w, g, m, v)


def _gla_consts():
    ltri = (_iota((CHUNK, CHUNK), 0) >= _iota((CHUNK, CHUNK), 1)).astype(F32)
    ones_c = jnp.ones((CHUNK, 128), F32)
    mask = (_iota((256, 512), 0) // 64 == _iota((256, 512), 1) // 128).astype(F32)
    return ltri, ones_c, mask


def _gla_chunk(consts, q, k, v, r, aux, s_prev, wa, ba, nw):
    ltri, ones_c, mask = consts
    la = _log_sigmoid(bdot(aux, wa, "nn") + ba) * (1.0 / 16.0)
    cum = hdot(ltri, la, "nn")
    total = jnp.sum(la, axis=0, keepdims=True)
    k_dec = k * jnp.exp(total - cum)
    inc = bdot(k_dec, v, "tn") * mask
    dec = jnp.exp(hdot(la, ones_c, "tn"))
    dec = jnp.concatenate([dec, dec, dec, dec], axis=1)
    s_new = dec * s_prev + inc
    o = bdot(q * GLA_SCALE, s_new, "nn")
    parts = []
    for h in range(4):
        oh = o[:, h * 128:(h + 1) * 128]
        parts.append(oh * lax.rsqrt(jnp.mean(oh * oh, axis=-1, keepdims=True) + EPS))
    on = jnp.concatenate(parts, axis=1)
    return s_new, on * nw * (r * _sigmoid(r))


def _gla_specs(cmap):
    return [pl.BlockSpec((CHUNK, 256), lambda c: (cmap(c), 0)),
            pl.BlockSpec((CHUNK, 256), lambda c: (cmap(c), 1)),
            pl.BlockSpec((CHUNK, 512), lambda c: (cmap(c), 1)),
            pl.BlockSpec((CHUNK, 512), lambda c: (cmap(c), 2)),
            pl.BlockSpec((CHUNK, 128), lambda c: (cmap(c), AUX_BLK))]


def _gla_fwd(proj, wa, ba, nw):
    def body(q_ref, k_ref, v_ref, r_ref, aux_ref, wa_ref, ba_ref, nw_ref, o_ref, sp_ref, s_ref):
        c = pl.program_id(0)

        @pl.when(c == 0)
        def _():
            s_ref[...] = jnp.zeros_like(s_ref)

        s_prev = s_ref[...]
        sp_ref[...] = s_prev
        s_new, out = _gla_chunk(_gla_consts(), q_ref[...], k_ref[...], v_ref[...], r_ref[...], aux_ref[...],
                                s_prev, wa_ref[...], ba_ref[...], nw_ref[...])
        s_ref[...] = s_new
        o_ref[...] = out

    full = lambda shape: pl.BlockSpec(shape, lambda c: (0,) * len(shape))
    return pl.pallas_call(
        body, name="gla_fwd", grid=(NCHUNK,),
        in_specs=_gla_specs(lambda c: c) + [full((128, 256)), full((1, 256)), full((1, 512))],
        out_specs=(pl.BlockSpec((CHUNK, 512), lambda c: (c, 0)), pl.BlockSpec((None, 256, 512), lambda c: (c, 0, 0))),
        out_shape=(jax.ShapeDtypeStruct((T, D), F32), jax.ShapeDtypeStruct((NCHUNK, 256, 512), F32)),
        scratch_shapes=[pltpu.VMEM((256, 512), F32)],
        compiler_params=_cp(("arbitrary",)),
    )(proj, proj, proj, proj, proj, wa, ba, nw)


def _gla_bwd(proj, s_prev_all, wa, ba, nw, dcat):
    rev = lambda c: NCHUNK - 1 - c

    def body(q_ref, k_ref, v_ref, r_ref, aux_ref, sp_ref, wa_ref, ba_ref, nw_ref, do_ref,
             dq_ref, dk_ref, dv_ref, dr_ref, daux_ref, dwa_ref, dba_ref, dnw_ref, ds_ref):
        c = pl.program_id(0)

        @pl.when(c == 0)
        def _():
            ds_ref[...] = jnp.zeros_like(ds_ref)
            dwa_ref[...] = jnp.zeros_like(dwa_ref)
            dba_ref[...] = jnp.zeros_like(dba_ref)
            dnw_ref[...] = jnp.zeros_like(dnw_ref)

        fn = functools.partial(_gla_chunk, _gla_consts())
        _, vjp = jax.vjp(fn, q_ref[...], k_ref[...], v_ref[...], r_ref[...], aux_ref[...], sp_ref[...],
                         wa_ref[...], ba_ref[...], nw_ref[...])
        dq, dk, dv, dr, daux, dsp, dwa, dba, dnw = vjp((ds_ref[...], do_ref[...]))
        dq_ref[...] = dq
        dk_ref[...] = dk
        dv_ref[...] = dv
        dr_ref[...] = dr
        daux_ref[...] = daux
        ds_ref[...] = dsp
        dwa_ref[...] += dwa
        dba_ref[...] += dba
        dnw_ref[...] += dnw

    full = lambda shape: pl.BlockSpec(shape, lambda c: (0,) * len(shape))
    blk = lambda w: pl.BlockSpec((CHUNK, w), lambda c: (rev(c), 0))
    sds = lambda *s: jax.ShapeDtypeStruct(s, F32)
    return pl.pallas_call(
        body, name="gla_bwd", grid=(NCHUNK,),
        in_specs=_gla_specs(rev) + [pl.BlockSpec((None, 256, 512), lambda c: (rev(c), 0, 0)),
                                    full((128, 256)), full((1, 256)), full((1, 512)), blk(512)],
        out_specs=(blk(256), blk(256), blk(512), blk(512), blk(128), full((128, 256)), full((1, 256)), full((1, 512))),
        out_shape=(sds(T, 256), sds(T, 256), sds(T, 512), sds(T, 512), sds(T, 128),
                   sds(128, 256), sds(1, 256), sds(1, 512)),
        scratch_shapes=[pltpu.VMEM((256, 512), F32)],
        compiler_params=_cp(("arbitrary",)),
    )(proj, proj, proj, proj, proj, s_prev_all, wa, ba, nw, dcat)


GATE_ROWS = 128


def _fox_gate_block(ltri, aux, bpad, carry):
    lf = _log_sigmoid(aux + bpad)
    cum = hdot(ltri, lf, "nn") + carry
    return cum, carry + jnp.sum(lf, axis=0, keepdims=True)


def _gate_ltri():
    return (_iota((GATE_ROWS, GATE_ROWS), 0) >= _iota((GATE_ROWS, GATE_ROWS), 1)).astype(F32)


def _fox_gate_fwd(proj, bpad):
    def body(aux_ref, b_ref, cum_ref, carry_ref):
        i = pl.program_id(0)

        @pl.when(i == 0)
        def _():
            carry_ref[...] = jnp.zeros_like(carry_ref)

        cum, carry = _fox_gate_block(_gate_ltri(), aux_ref[...], b_ref[...], carry_ref[...])
        cum_ref[...] = cum
        carry_ref[...] = carry

    return pl.pallas_call(
        body, name="fox_gate_fwd", grid=(T // GATE_ROWS,),
        in_specs=[pl.BlockSpec((GATE_ROWS, 128), lambda i: (i, AUX_BLK)), pl.BlockSpec((1, 128), lambda i: (0, 0))],
        out_specs=pl.BlockSpec((GATE_ROWS, 128), lambda i: (i, 0)),
        out_shape=jax.ShapeDtypeStruct((T, 128), F32),
        scratch_shapes=[pltpu.VMEM((1, 128), F32)],
        compiler_params=_cp(("arbitrary",)),
    )(proj, bpad)


def _fox_gate_bwd(proj, bpad, dcrow, dccol_t, daux_gla):
    nb = T // GATE_ROWS
    rev = lambda i: nb - 1 - i

    def body(aux_ref, b_ref, dr_ref, dc_ref, dg_ref, daux_ref, db_ref, dcarry_ref):
        i = pl.program_id(0)

        @pl.when(i == 0)
        def _():
            dcarry_ref[...] = jnp.zeros_like(dcarry_ref)
            db_ref[...] = jnp.zeros_like(db_ref)

        dcum = dr_ref[0] + dr_ref[1] + dr_ref[2] + dr_ref[3] + dc_ref[...]
        fn = functools.partial(_fox_gate_block, _gate_ltri())
        _, vjp = jax.vjp(fn, aux_ref[...], b_ref[...], jnp.zeros((1, 128), F32))
        daux, db, dcarry = vjp((dcum, dcarry_ref[...]))
        daux_ref[...] = daux + dg_ref[...]
        db_ref[...] += db
        dcarry_ref[...] = dcarry

    blk = pl.BlockSpec((GATE_ROWS, 128), lambda i: (rev(i), 0))
    vec = pl.BlockSpec((1, 128), lambda i: (0, 0))
    return pl.pallas_call(
        body, name="fox_gate_bwd", grid=(nb,),
        in_specs=[pl.BlockSpec((GATE_ROWS, 128), lambda i: (rev(i), AUX_BLK)), vec,
                  pl.BlockSpec((4, GATE_ROWS, 128), lambda i: (0, rev(i), 0)), blk, blk],
        out_specs=(blk, vec),
        out_shape=(jax.ShapeDtypeStruct((T, 128), F32), jax.ShapeDtypeStruct((1, 128), F32)),
        scratch_shapes=[pltpu.VMEM((1, 128), F32)],
        compiler_params=_cp(("arbitrary",)),
    )(proj, bpad, dcrow, dccol_t, daux_gla)


FOX_Q = 128


FOX_KEY_STEP = 512
FOX_GROUPS = T // FOX_KEY_STEP
FOX_Q_PER_GROUP = FOX_KEY_STEP // FOX_Q


def _fox_block(hp, qb, q, k, v, crow, ccol):
    kl = k.shape[0]
    lane = _iota((FOX_Q, 128), 1)
    causal = (qb * FOX_Q + _iota((FOX_Q, kl), 0)) >= _iota((FOX_Q, kl), 1)
    sub = _iota((8, kl), 0)
    outs = []
    for e in range(2):
        h = 2 * hp + e
        qm = jnp.where((lane >= 64 * e) & (lane < 64 * (e + 1)), q, 0.0)
        s = bdot(qm, k, "nt") * ATT_SCALE
        ct = jnp.sum(jnp.where(lane == FOX_LANE0 + h, crow, 0.0), axis=1, keepdims=True)
        cs = jnp.sum(jnp.where(sub == h, ccol, 0.0), axis=0, keepdims=True)
        s = jnp.where(causal, s + (ct - cs), NEG)
        outs.append(bdot(_softmax_rows(s), v, "nn"))
    return jnp.where(lane < 64, outs[0], outs[1])


def _fox_in_specs():
    return [pl.BlockSpec((FOX_Q, 128), lambda hp, qb: (qb, 12 + hp)),
            pl.BlockSpec((T, 128), lambda hp, qb: (0, 16 + hp)),
            pl.BlockSpec((T, 128), lambda hp, qb: (0, 20 + hp)),
            pl.BlockSpec((FOX_Q, 128), lambda hp, qb: (qb, 0)),
            pl.BlockSpec((8, T), lambda hp, qb: (0, 0))]


def _fox_fwd(proj, cum_r, cum_c, cat):
    def body(q_ref, k_ref, v_ref, cr_ref, cc_ref, cat_ref, o_ref):
        qb = pl.program_id(1)
        for g in range(FOX_GROUPS):
            kl = FOX_KEY_STEP * (g + 1)

            @pl.when(qb // FOX_Q_PER_GROUP == g)
            def _(kl=kl):
                o_ref[...] = _fox_block(pl.program_id(0), qb, q_ref[...], k_ref[0:kl, :], v_ref[0:kl, :],
                                        cr_ref[...], cc_ref[:, 0:kl])

    return pl.pallas_call(
        body, name="fox_fwd", grid=(4, T // FOX_Q), in_specs=_fox_in_specs() + [pl.BlockSpec(memory_space=pl.ANY)],
        out_specs=pl.BlockSpec((FOX_Q, 128), lambda hp, qb: (qb, 4 + hp)),
        out_shape=jax.ShapeDtypeStruct((T, D), F32), input_output_aliases={5: 0},
        compiler_params=_cp(("parallel", "parallel")),
    )(proj, proj, proj, cum_r, cum_c, cat)


def _fox_bwd(proj, cum_r, cum_c, dcat):
    def body(q_ref, k_ref, v_ref, cr_ref, cc_ref, do_ref, dq_ref, dk_ref, dv_ref, dcr_ref, dcc_ref):
        qb = pl.program_id(1)

        @pl.when(qb == 0)
        def _():
            dk_ref[...] = jnp.zeros_like(dk_ref)
            dv_ref[...] = jnp.zeros_like(dv_ref)
            dcc_ref[...] = jnp.zeros_like(dcc_ref)

        fn = functools.partial(_fox_block, pl.program_id(0), qb)
        for g in range(FOX_GROUPS):
            kl = FOX_KEY_STEP * (g + 1)

            @pl.when(qb // FOX_Q_PER_GROUP == g)
            def _(kl=kl):
                _, vjp = jax.vjp(fn, q_ref[...], k_ref[0:kl, :], v_ref[0:kl, :], cr_ref[...], cc_ref[:, 0:kl])
                dq, dk, dv, dcr, dcc = vjp(do_ref[...])
                dq_ref[...] = dq
                dk_ref[0:kl, :] += dk
                dv_ref[0:kl, :] += dv
                dcr_ref[...] = dcr
                dcc_ref[:, 0:kl] += dcc

    sds = lambda *s: jax.ShapeDtypeStruct(s, F32)
    return pl.pallas_call(
        body, name="fox_bwd", grid=(4, T // FOX_Q),
        in_specs=_fox_in_specs() + [pl.BlockSpec((FOX_Q, 128), lambda hp, qb: (qb, 4 + hp))],
        out_specs=(pl.BlockSpec((FOX_Q, 128), lambda hp, qb: (qb, hp)),
                   pl.BlockSpec((T, 128), lambda hp, qb: (0, hp)),
                   pl.BlockSpec((T, 128), lambda hp, qb: (0, hp)),
                   pl.BlockSpec((None, FOX_Q, 128), lambda hp, qb: (hp, qb, 0)),
                   pl.BlockSpec((None, 8, T), lambda hp, qb: (hp, 0, 0))),
        out_shape=(sds(T, 512), sds(T, 512), sds(T, 512), sds(4, T, 128), sds(4, 8, T)),
        compiler_params=_cp(("parallel", "arbitrary")),
    )(proj, proj, proj, cum_r, cum_c, dcat)


def _rel_onehot(q):
    kj = _iota((REL_PAD, CA_BAND), 1)
    rel = jnp.clip(CA_PAD + q - kj, -128, 128) + 128
    return (_iota((REL_PAD, CA_BAND), 0) == rel).astype(F32)


def _bias_build(rbp):
    def body(rb_ref, o_ref):
        o_ref[...] = _hdot_raw(rb_ref[...], _rel_onehot(pl.program_id(0)), "nn")

    return pl.pallas_call(
        body, name="ca_bias_build", grid=(CHUNK,),
        in_specs=[pl.BlockSpec((8, REL_PAD), lambda q: (0, 0))],
        out_specs=pl.BlockSpec((None, 8, CA_BAND), lambda q: (q, 0, 0)),
        out_shape=jax.ShapeDtypeStruct((CHUNK, 8, CA_BAND), F32),
        compiler_params=_cp(("parallel",)),
    )(rbp)


def _bias_grad(dbias_q):
    def body(db_ref, o_ref):
        q = pl.program_id(0)

        @pl.when(q == 0)
        def _():
            o_ref[...] = jnp.zeros_like(o_ref)

        o_ref[...] += _hdot_raw(db_ref[...], _rel_onehot(q), "nt")

    return pl.pallas_call(
        body, name="ca_bias_grad", grid=(CHUNK,),
        in_specs=[pl.BlockSpec((None, 8, CA_BAND), lambda q: (q, 0, 0))],
        out_specs=pl.BlockSpec((8, REL_PAD), lambda q: (0, 0)),
        out_shape=jax.ShapeDtypeStruct((8, REL_PAD), F32),
        compiler_params=_cp(("arbitrary",)),
    )(dbias_q)


def _ca_block(c, q, kb, vb, bias2):
    lane = _iota((CHUNK, 128), 1)
    valid = (c * CHUNK - CA_PAD + _iota((CHUNK, CA_BAND), 1)) >= 0
    outs = []
    for e in range(2):
        qm = jnp.where((lane >= 64 * e) & (lane < 64 * (e + 1)), q, 0.0)
        s = bdot(qm, kb, "nt") * ATT_SCALE
        s = jnp.where(valid, s + bias2[e], NEG)
        outs.append(bdot(_softmax_rows(s), vb, "nn"))
    return jnp.where(lane < 64, outs[0], outs[1])


def _ca_fwd(proj, kvpad, bias):
    def body(q_ref, k_ref, v_ref, b_ref, o_ref):
        c = pl.program_id(1)
        band = pl.ds(pl.multiple_of(c * CHUNK, CHUNK), CA_BAND)
        o_ref[...] = _ca_block(c, q_ref[...], k_ref[band, :], v_ref[band, :], b_ref[...])

    return pl.pallas_call(
        body, name="ca_fwd", grid=(4, NCHUNK),
        in_specs=[pl.BlockSpec((CHUNK, 128), lambda hp, c: (c, hp)),
                  pl.BlockSpec((T + CA_PAD, 128), lambda hp, c: (0, hp)),
                  pl.BlockSpec((T + CA_PAD, 128), lambda hp, c: (0, 4 + hp)),
                  pl.BlockSpec((2, CHUNK, CA_BAND), lambda hp, c: (hp, 0, 0))],
        out_specs=pl.BlockSpec((CHUNK, 128), lambda hp, c: (c, hp)),
        out_shape=jax.ShapeDtypeStruct((T, D), F32),
        compiler_params=_cp(("parallel", "parallel")),
    )(proj, kvpad, kvpad, bias)


def _ca_bwd(proj, kvpad, bias, dcat):
    def body(q_ref, k_ref, v_ref, b_ref, do_ref, dq_ref, dk_ref, dv_ref, db_ref):
        c = pl.program_id(1)

        @pl.when(c == 0)
        def _():
            dk_ref[...] = jnp.zeros_like(dk_ref)
            dv_ref[...] = jnp.zeros_like(dv_ref)
            db_ref[...] = jnp.zeros_like(db_ref)

        band = pl.ds(pl.multiple_of(c * CHUNK, CHUNK), CA_BAND)
        fn = functools.partial(_ca_block, c)
        _, vjp = jax.vjp(fn, q_ref[...], k_ref[band, :], v_ref[band, :], b_ref[...])
        dq, dkb, dvb, db = vjp(do_ref[...])
        dq_ref[...] = dq
        dk_ref[band, :] += dkb
        dv_ref[band, :] += dvb
        db_ref[...] += db

    sds = lambda *s: jax.ShapeDtypeStruct(s, F32)
    padded = lambda: pl.BlockSpec((T + CA_PAD, 128), lambda hp, c: (0, hp))
    return pl.pallas_call(
        body, name="ca_bwd", grid=(4, NCHUNK),
        in_specs=[pl.BlockSpec((CHUNK, 128), lambda hp, c: (c, hp)),
                  pl.BlockSpec((T + CA_PAD, 128), lambda hp, c: (0, hp)),
                  pl.BlockSpec((T + CA_PAD, 128), lambda hp, c: (0, 4 + hp)),
                  pl.BlockSpec((2, CHUNK, CA_BAND), lambda hp, c: (hp, 0, 0)),
                  pl.BlockSpec((CHUNK, 128), lambda hp, c: (c, hp))],
        out_specs=(pl.BlockSpec((CHUNK, 128), lambda hp, c: (c, hp)), padded(), padded(),
                   pl.BlockSpec((2, CHUNK, CA_BAND), lambda hp, c: (hp, 0, 0))),
        out_shape=(sds(T, 512), sds(T + CA_PAD, 512), sds(T + CA_PAD, 512), sds(8, CHUNK, CA_BAND)),
        compiler_params=_cp(("parallel", "arbitrary")),
    )(proj, kvpad, kvpad, bias, dcat)


def _lru_pre(xs, cw, cb, wa, ba, wx, bx, lam):
    xc = cb + xs[0] * cw[0:1, :] + xs[1] * cw[1:2, :] + xs[2] * cw[2:3, :] + xs[3] * cw[3:4, :]
    ra = _sigmoid(bdot(xc, wa, "nn") + ba)
    ii = _sigmoid(bdot(xc, wx, "nn") + bx)
    la = 8.0 * ra * _log_sigmoid(lam)
    return jnp.exp(la), jnp.sqrt(-_expm1(2.0 * la)) * (ii * xc)


def _lru_pre_specs():
    full = lambda shape: pl.BlockSpec(shape, lambda i: (0,) * len(shape))
    return [pl.BlockSpec((4, ROWS, 512), lambda i: (0, i, 0)), full((4, 512)), full((1, 512)),
            full((512, 512)), full((1, 512)), full((512, 512)), full((1, 512)), full((1, 512))]


def _lru_pre_fwd(xs, cw, cb, wa, ba, wx, bx, lam):
    def body(xs_ref, cw_ref, cb_ref, wa_ref, ba_ref, wx_ref, bx_ref, lam_ref, a_ref, b_ref):
        a, b = _lru_pre(xs_ref[...], cw_ref[...], cb_ref[...], wa_ref[...], ba_ref[...], wx_ref[...], bx_ref[...],
                        lam_ref[...])
        a_ref[...] = a
        b_ref[...] = b

    row = pl.BlockSpec((ROWS, 512), lambda i: (i, 0))
    sds = jax.ShapeDtypeStruct((T, 512), F32)
    return pl.pallas_call(body, name="lru_pre_fwd", grid=(T // ROWS,), in_specs=_lru_pre_specs(),
                          out_specs=(row, row), out_shape=(sds, sds), compiler_params=_cp(("parallel",)),
                          )(xs, cw, cb, wa, ba, wx, bx, lam)


def _lru_pre_bwd(xs, cw, cb, wa, ba, wx, bx, lam, da, db):
    def body(xs_ref, cw_ref, cb_ref, wa_ref, ba_ref, wx_ref, bx_ref, lam_ref, da_ref, db_ref,
             dxs_ref, dcw_ref, dcb_ref, dwa_ref, dba_ref, dwx_ref, dbx_ref, dlam_ref):
        acc = (dcw_ref, dcb_ref, dwa_ref, dba_ref, dwx_ref, dbx_ref, dlam_ref)

        @pl.when(pl.program_id(0) == 0)
        def _():
            for r in acc:
                r[...] = jnp.zeros_like(r)

        _, vjp = jax.vjp(_lru_pre, xs_ref[...], cw_ref[...], cb_ref[...], wa_ref[...], ba_ref[...], wx_ref[...],
                         bx_ref[...], lam_ref[...])
        grads = vjp((da_ref[...], db_ref[...]))
        dxs_ref[...] = grads[0]
        for r, g in zip(acc, grads[1:]):
            r[...] += g

    row = pl.BlockSpec((ROWS, 512), lambda i: (i, 0))
    specs = _lru_pre_specs()
    sds = lambda *s: jax.ShapeDtypeStruct(s, F32)
    return pl.pallas_call(
        body, name="lru_pre_bwd", grid=(T // ROWS,), in_specs=specs + [row, row], out_specs=tuple(specs),
        out_shape=(sds(4, T, 512), sds(4, 512), sds(1, 512), sds(512, 512), sds(1, 512), sds(512, 512), sds(1, 512),
                   sds(1, 512)),
        compiler_params=_cp(("arbitrary",)),
    )(xs, cw, cb, wa, ba, wx, bx, lam, da, db)


def _lru_scan_fwd(a, b):
    def body(a_ref, b_ref, h_ref):
        def step(t, h):
            h = a_ref[pl.ds(t, 1), :] * h + b_ref[pl.ds(t, 1), :]
            h_ref[pl.ds(t, 1), :] = h
            return h

        lax.fori_loop(0, T, step, jnp.zeros((1, 512), F32))

    return pl.pallas_call(body, name="lru_scan_fwd", out_shape=jax.ShapeDtypeStruct((T, 512), F32),
                          compiler_params=pltpu.CompilerParams(vmem_limit_bytes=VMEM_LIMIT))(a, b)


def _lru_scan_bwd(a, h, dh):
    def body(a_ref, h_ref, dh_ref, da_ref, db_ref):
        def step(i, carry):
            t = T - 1 - i
            g = dh_ref[pl.ds(t, 1), :] + carry
            db_ref[pl.ds(t, 1), :] = g
            da_ref[pl.ds(t, 1), :] = g * h_ref[pl.ds(t - 1, 1), :]
            return a_ref[pl.ds(t, 1), :] * g

        carry = lax.fori_loop(0, T - 1, step, jnp.zeros((1, 512), F32))
        db_ref[pl.ds(0, 1), :] = dh_ref[pl.ds(0, 1), :] + carry
        da_ref[pl.ds(0, 1), :] = jnp.zeros((1, 512), F32)

    sds = jax.ShapeDtypeStruct((T, 512), F32)
    return pl.pallas_call(body, name="lru_scan_bwd", out_shape=(sds, sds),
                          compiler_params=pltpu.CompilerParams(vmem_limit_bytes=VMEM_LIMIT))(a, h, dh)


def _lru_post(h, gate):
    return h * _gelu_tanh(gate)


def _lru_post_fwd(h, proj, cat):
    def body(h_ref, g_ref, cat_ref, o_ref):
        o_ref[...] = _lru_post(h_ref[...], g_ref[...])

    row = pl.BlockSpec((ROWS, 512), lambda i: (i, 0))
    return pl.pallas_call(body, name="lru_post_fwd", grid=(T // ROWS,),
                          in_specs=[row, pl.BlockSpec((ROWS, 512), lambda i: (i, 3)), pl.BlockSpec(memory_space=pl.ANY)],
                          out_specs=pl.BlockSpec((ROWS, 512), lambda i: (i, 1)),
                          out_shape=jax.ShapeDtypeStruct((T, D), F32), input_output_aliases={2: 0},
                          compiler_params=_cp(("parallel",)))(h, proj, cat)


def _lru_post_bwd(h, proj, dcat):
    def body(h_ref, g_ref, do_ref, dh_ref, dg_ref):
        _, vjp = jax.vjp(_lru_post, h_ref[...], g_ref[...])
        dh, dg = vjp(do_ref[...])
        dh_ref[...] = dh
        dg_ref[...] = dg

    row = pl.BlockSpec((ROWS, 512), lambda i: (i, 0))
    sds = jax.ShapeDtypeStruct((T, 512), F32)
    return pl.pallas_call(body, name="lru_post_bwd", grid=(T // ROWS,),
                          in_specs=[row, pl.BlockSpec((ROWS, 512), lambda i: (i, 3)),
                                    pl.BlockSpec((ROWS, 512), lambda i: (i, 1))],
                          out_specs=(row, row), out_shape=(sds, sds), compiler_params=_cp(("parallel",)))(h, proj, dcat)


def _conv_dx(dxs_shift):
    def body(d_ref, o_ref):
        o_ref[...] = d_ref[0] + d_ref[1] + d_ref[2] + d_ref[3]

    row = pl.BlockSpec((ROWS, 512), lambda i: (i, 0))
    return pl.pallas_call(body, name="lru_conv_dx", grid=(T // ROWS,),
                          in_specs=[pl.BlockSpec((4, ROWS, 512), lambda i: (0, i, 0))], out_specs=row,
                          out_shape=jax.ShapeDtypeStruct((T, 512), F32), compiler_params=_cp(("parallel",)))(dxs_shift)


def _position():
    return lax.axis_index("x"), lax.axis_index("y"), lax.axis_index("c")


def _other_chips(x, y):
    return [(1 - x, y), (x, 1 - y), (1 - x, 1 - y)]


def _al(v, n):
    return v * n if isinstance(v, int) else pl.multiple_of(v * n, n)


_AG_ITEMS = [
    ((4, 32, 128), lambda o, s, h: o.at[s, pl.ds(_al(h, 16), 16), :], lambda r, h: r.at[pl.ds(_al(h, 16), 16), :]),
    ((4, 1024, 774), lambda o, s, h: o.at[s, pl.ds(_al(h, 512), 512), :], lambda r, h: r.at[pl.ds(_al(h, 512), 512), :]),
    ((1024, 1024), lambda o, s, h: o.at[pl.ds(_al(2 * s + h, 128), 128), :], lambda r, h: r.at[pl.ds(_al(h, 128), 128), :]),
    ((2, 1024, 4096), lambda o, s, h: o.at[h, :, pl.ds(_al(s, 1024), 1024)], lambda r, h: r.at[h]),
    ((2, 4096, 1024), lambda o, s, h: o.at[h, pl.ds(_al(s, 1024), 1024), :], lambda r, h: r.at[h]),
    ((1024, 2560), lambda o, s, h: o.at[pl.ds(_al(h, 512), 512), pl.ds(_al(s, 640), 640)],
     lambda r, h: r.at[pl.ds(_al(h, 512), 512), :]),
    ((1024, 1024), lambda o, s, h: o.at[pl.ds(_al(2 * s + h, 128), 128), :], lambda r, h: r.at[pl.ds(_al(h, 128), 128), :]),
]


_AG_GROUPS = [(0, 1, 2), (3, 4), (5, 6)]

_HBM = pl.BlockSpec(memory_space=pltpu.HBM)
_SEM = pl.BlockSpec(memory_space=pltpu.SEMAPHORE)
_SPLIT = dict(has_side_effects=pltpu.SideEffectType.DATAFLOW_SIDE_EFFECTING)


def _hbm(a):
    return pltpu.with_memory_space_constraint(a, pltpu.HBM)


def _ag_ici_copy(i, j, chip, c, slot, src_ref, land_ref, send_sems, recv_sems, k):
    _, dst, half = _AG_ITEMS[i]
    return pltpu.make_async_remote_copy(src_ref=half(src_ref, c), dst_ref=dst(land_ref, slot, c), send_sem=send_sems.at[k],
                                        recv_sem=recv_sems.at[k], device_id=(*chip, c), device_id_type=MESH)


def _ag_start(shards):
    n = len(_AG_ITEMS)
    ng = len(_AG_GROUPS)
    lands = [lax.empty(shape, s.dtype) for (shape, _, _), s in zip(_AG_ITEMS, shards)]

    def body(*refs):
        srcs, land_refs = refs[:n], refs[n:2 * n]
        sems = refs[2 * n:2 * n + 2 * ng]
        token = refs[-1]
        x, y, c = _position()
        me = 2 * x + y
        for g, items in enumerate(_AG_GROUPS):
            for t, i in enumerate(items):
                for j, chip in enumerate(_other_chips(x, y)):
                    _ag_ici_copy(i, j, chip, c, me, srcs[i], land_refs[i], sems[2 * g], sems[2 * g + 1], 3 * t + j).start()
        token[...] = jnp.zeros_like(token)

    sem_shapes = []
    for items in _AG_GROUPS:
        sem_shapes += [pltpu.SemaphoreType.DMA((3 * len(items),))] * 2
    thru = [pltpu.HBM(a.shape, a.dtype) for a in list(shards) + lands]
    out = pl.pallas_call(
        body, name="allgather_start",
        out_shape=tuple(sem_shapes) + tuple(thru) + (jax.ShapeDtypeStruct((8, 128), F32),),
        in_specs=(_HBM,) * (2 * n),
        out_specs=(_SEM,) * (2 * ng) + (_HBM,) * (2 * n) + (pl.BlockSpec(memory_space=pltpu.VMEM),),
        input_output_aliases={i: 2 * ng + i for i in range(2 * n)},
        compiler_params=pltpu.CompilerParams(**_SPLIT),
    )(*[_hbm(a) for a in list(shards) + lands])
    sems, thru, token = out[:2 * ng], out[2 * ng:-1], out[-1]
    return [(sems[2 * g], sems[2 * g + 1]) for g in range(ng)], list(thru[:n]), list(thru[n:]), token


def _ag_wait(g, sems, srcs, lands, after):
    items = _AG_GROUPS[g]
    m = len(items)

    def body(*refs):
        src_refs, land_refs = refs[:m], refs[m:2 * m]
        send_sems, recv_sems = refs[2 * m], refs[2 * m + 1]
        x, y, c = _position()
        for t, i in enumerate(items):
            for j, chip in enumerate(_other_chips(x, y)):
                cp = _ag_ici_copy(i, j, chip, c, 2 * chip[0] + chip[1], src_refs[t], land_refs[t], send_sems, recv_sems,
                                  3 * t + j)
                cp.wait_send()
                cp.wait_recv()

    ops = [srcs[i] for i in items] + [lands[i] for i in items]
    out = pl.pallas_call(
        body, name=f"allgather_wait_{g}",
        out_shape=tuple(pltpu.HBM(a.shape, a.dtype) for a in ops),
        in_specs=(_HBM,) * (2 * m) + (_SEM, _SEM, pl.BlockSpec(memory_space=pl.ANY)),
        out_specs=(_HBM,) * (2 * m),
        input_output_aliases={i: i for i in range(2 * m)},
        compiler_params=pltpu.CompilerParams(**_SPLIT),
    )(*ops, sems[0], sems[1], after)
    return list(out[:m]), list(out[m:])


def _ag_forward(g, srcs, lands):
    items = _AG_GROUPS[g]
    m = len(items)

    def body(*refs):
        src_refs, in_refs, out_refs = refs[:m], refs[m:2 * m], refs[2 * m:3 * m]
        send_sems, recv_sems = refs[3 * m:]
        x, y, c = _position()
        sibling = (x, y, 1 - c)
        me = 2 * x + y
        chips = _other_chips(x, y)
        sends = []
        for t, i in enumerate(items):
            _, dst, half = _AG_ITEMS[i]
            for j, chip in enumerate(chips):
                slot = 2 * chip[0] + chip[1]
                sends.append(pltpu.make_async_remote_copy(
                    src_ref=dst(in_refs[t], slot, c), dst_ref=dst(out_refs[t], slot, c), send_sem=send_sems.at[5 * t + j],
                    recv_sem=recv_sems.at[5 * t + j], device_id=sibling, device_id_type=MESH))
            for hc in range(2):
                sends.append(pltpu.make_async_remote_copy(
                    src_ref=half(src_refs[t], hc), dst_ref=dst(out_refs[t], me, hc), send_sem=send_sems.at[5 * t + 3 + hc],
                    recv_sem=recv_sems.at[5 * t + 3 + hc], device_id=sibling, device_id_type=MESH))
        for cp in sends:
            cp.start()
        for t, i in enumerate(items):
            _, dst, half = _AG_ITEMS[i]
            for j, chip in enumerate(chips):
                there = dst(out_refs[t], 2 * chip[0] + chip[1], 1 - c)
                pltpu.make_async_remote_copy(src_ref=there, dst_ref=there, send_sem=send_sems.at[5 * t + j],
                                             recv_sem=recv_sems.at[5 * t + j], device_id=sibling,
                                             device_id_type=MESH).wait_recv()
            for hc in range(2):
                there = dst(out_refs[t], me, hc)
                pltpu.make_async_remote_copy(src_ref=there, dst_ref=there, send_sem=send_sems.at[5 * t + 3 + hc],
                                             recv_sem=recv_sems.at[5 * t + 3 + hc], device_id=sibling,
                                             device_id_type=MESH).wait_recv()
        for cp in sends:
            cp.wait_send()

    any_spec = pl.BlockSpec(memory_space=pl.ANY)
    return pl.pallas_call(
        body, name=f"allgather_forward_{g}",
        in_specs=[any_spec] * (2 * m), out_specs=(any_spec,) * m,
        out_shape=tuple(jax.ShapeDtypeStruct(a.shape, a.dtype) for a in lands),
        input_output_aliases={m + t: t for t in range(m)},
        scratch_shapes=[pltpu.SemaphoreType.DMA((5 * m,)), pltpu.SemaphoreType.DMA((5 * m,))],
    )(*srcs, *lands)


def _pair_swap_cols(gb, tag):
    _, rows, cols = gb.shape
    hc = cols // 2

    def body(g_ref, out_ref, send_sem, recv_sem):
        x, y, c = _position()
        cp = pltpu.make_async_remote_copy(src_ref=g_ref.at[:, :, pl.ds(_al(1 - c, hc), hc)], dst_ref=out_ref,
                                          send_sem=send_sem, recv_sem=recv_sem, device_id=(x, y, 1 - c),
                                          device_id_type=MESH)
        cp.start()
        cp.wait()

    return pl.pallas_call(
        body, name="grad_pair_swap_" + tag,
        in_specs=[pl.BlockSpec(memory_space=pl.ANY)], out_specs=pl.BlockSpec(memory_space=pl.ANY),
        out_shape=jax.ShapeDtypeStruct((4, rows, hc), gb.dtype),
        scratch_shapes=[pltpu.SemaphoreType.DMA, pltpu.SemaphoreType.DMA],
    )(gb)


def _handover(halves, tag):
    def body(in_ref, out_ref, send_sem, recv_sem):
        x, y, c = _position()
        cp = pltpu.make_async_remote_copy(src_ref=in_ref.at[c], dst_ref=out_ref.at[c], send_sem=send_sem,
                                          recv_sem=recv_sem, device_id=(x, y, 1 - c), device_id_type=MESH)
        cp.start()
        theirs = out_ref.at[1 - c]
        pltpu.make_async_remote_copy(src_ref=theirs, dst_ref=theirs, send_sem=send_sem, recv_sem=recv_sem,
                                     device_id=(x, y, c), device_id_type=MESH).wait_recv()
        cp.wait_send()

    return pl.pallas_call(
        body, name="grad_handover_" + tag,
        in_specs=[pl.BlockSpec(memory_space=pl.ANY)], out_specs=pl.BlockSpec(memory_space=pl.ANY),
        out_shape=jax.ShapeDtypeStruct(halves.shape, halves.dtype), input_output_aliases={0: 0},
        scratch_shapes=[pltpu.SemaphoreType.DMA, pltpu.SemaphoreType.DMA],
    )(halves)


def _a2a_copy(j, chip, c, p_ref, q_ref, q_slot, send_sems, recv_sems):
    return pltpu.make_async_remote_copy(src_ref=p_ref.at[2 * chip[0] + chip[1]], dst_ref=q_ref.at[q_slot],
                                        send_sem=send_sems.at[j], recv_sem=recv_sems.at[j], device_id=(*chip, c),
                                        device_id_type=MESH)


def _a2a_start(p, tag):
    def body(p_ref, q_ref, send_sems, recv_sems, p_thru, q_thru, token):
        x, y, c = _position()
        for j, chip in enumerate(_other_chips(x, y)):
            _a2a_copy(j, chip, c, p_ref, q_ref, 2 * x + y, send_sems, recv_sems).start()
        token[...] = jnp.zeros_like(token)

    return pl.pallas_call(
        body, name="grad_alltoall_start_" + tag,
        out_shape=(pltpu.SemaphoreType.DMA((3,)), pltpu.SemaphoreType.DMA((3,)), pltpu.HBM(p.shape, p.dtype),
                   pltpu.HBM(p.shape, p.dtype), jax.ShapeDtypeStruct((8, 128), F32)),
        in_specs=(_HBM, _HBM), out_specs=(_SEM, _SEM, _HBM, _HBM, pl.BlockSpec(memory_space=pltpu.VMEM)),
        input_output_aliases={0: 2, 1: 3},
        compiler_params=pltpu.CompilerParams(**_SPLIT),
    )(_hbm(p), _hbm(lax.empty(p.shape, p.dtype)))


def _a2a_wait(send_sems, recv_sems, p, q, after, tag):
    def body(p_ref, q_ref, send_sems, recv_sems, after_ref, p_out, q_out):
        x, y, c = _position()
        for j, chip in enumerate(_other_chips(x, y)):
            cp = _a2a_copy(j, chip, c, p_ref, q_ref, 2 * chip[0] + chip[1], send_sems, recv_sems)
            cp.wait_send()
            cp.wait_recv()

    return pl.pallas_call(
        body, name="grad_alltoall_wait_" + tag,
        out_shape=(pltpu.HBM(p.shape, p.dtype), pltpu.HBM(q.shape, q.dtype)),
        in_specs=(_HBM, _HBM, _SEM, _SEM, pl.BlockSpec(memory_space=pl.ANY)), out_specs=(_HBM, _HBM),
        input_output_aliases={0: 0, 1: 1},
        compiler_params=pltpu.CompilerParams(**_SPLIT),
    )(p, q, send_sems, recv_sems, after)


def _comm_rows(rows):
    return next(t for t in (512, 384, 256, 128) if rows % t == 0)


def _pair_add(gb, recv, where, tag):
    _, rows, cols = gb.shape
    hc = cols // 2
    tr = _comm_rows(rows)

    def body(w_ref, g_ref, r_ref, o_ref):
        o_ref[...] = (g_ref[...].astype(F32) + r_ref[...].astype(F32)).astype(o_ref.dtype)

    return pl.pallas_call(
        body, name="grad_pair_add_" + tag,
        grid_spec=pltpu.PrefetchScalarGridSpec(
            num_scalar_prefetch=1, grid=(4, rows // tr),
            in_specs=[pl.BlockSpec((None, tr, hc), lambda s, j, w_ref: (s, j, w_ref[0])),
                      pl.BlockSpec((None, tr, hc), lambda s, j, w_ref: (s, j, 0))],
            out_specs=pl.BlockSpec((None, tr, hc), lambda s, j, w_ref: (s, j, 0))),
        out_shape=jax.ShapeDtypeStruct((4, rows, hc), gb.dtype),
        compiler_params=_cp(("parallel", "parallel")),
    )(where, gb, recv)


def _sum_chips(p, q, where, tag):
    _, rows, hc = q.shape
    tr = _comm_rows(rows)

    def body(w_ref, p_ref, qa_ref, qb_ref, qc_ref, o_ref):
        me = w_ref[1]
        own, qa, qb, qc = (r[...].astype(F32) for r in (p_ref, qa_ref, qb_ref, qc_ref))
        v0 = jnp.where(me == 0, own, qa)
        v1 = jnp.where(me == 1, own, jnp.where(me == 0, qa, qb))
        v2 = jnp.where(me == 2, own, jnp.where(me < 2, qb, qc))
        v3 = jnp.where(me == 3, own, qc)
        o_ref[...] = ((v0 + v1) + v2) + v3

    slot = lambda k: pl.BlockSpec((None, tr, hc), lambda j, w_ref: (w_ref[k], j, 0))
    return pl.pallas_call(
        body, name="grad_sum_chips_" + tag,
        grid_spec=pltpu.PrefetchScalarGridSpec(
            num_scalar_prefetch=1, grid=(rows // tr,),
            in_specs=[slot(1), slot(2), slot(3), slot(4)],
            out_specs=pl.BlockSpec((None, tr, hc), lambda j, w_ref: (w_ref[0], j, 0))),
        out_shape=jax.ShapeDtypeStruct((2, rows, hc), F32),
        compiler_params=_cp(("parallel",)),
    )(where, p, q, q, q)


def _shard_major(g, axis):
    shape = g.shape
    g = g.reshape(shape[:axis] + (4, shape[axis] // 4) + shape[axis + 1:])
    return jnp.moveaxis(g, axis, 0).reshape(4, -1)


def _unshard(g4, shape, axis):
    n = shape[axis] // 4
    g = g4.reshape((4,) + shape[:axis] + (n,) + shape[axis + 1:])
    return jnp.moveaxis(g, 0, axis).reshape(shape)


def _split(flat, shapes):
    out, off = [], 0
    for shp in shapes:
        n = 1
        for d in shp:
            n *= d
        out.append(flat[..., off:off + n].reshape(flat.shape[:-1] + tuple(shp)))
        off += n
    return out


def _even_cols_to_kernel(w):
    return jnp.concatenate([w[:, :1536], w[:, 1552:3088], w[:, 1536:1552], w[:, 3088:3096],
                            jnp.zeros((w.shape[0], PE - 3096), w.dtype)], axis=1)


def _block_diag(w):
    out = jnp.zeros((512, 512), w.dtype)
    for n in range(8):
        out = lax.dynamic_update_slice(out, w[n], (64 * n, 64 * n))
    return out


def _diag_blocks(g):
    return jnp.stack([g[64 * n:64 * (n + 1), 64 * n:64 * (n + 1)] for n in range(8)])


def _shift_down(a, s):
    return a if s == 0 else jnp.pad(a, ((s, 0), (0, 0)))[:a.shape[0]]


def _shift_up(a, s):
    return a if s == 0 else jnp.pad(a, ((0, s), (0, 0)))[s:]


SMALL_SHARDED_SHAPES = [(2, 4, 256), (16, 64), (4, 128), (128,), (128,), (128,), (128,)]
REPL_SHAPES = [(256,), (512,), (8,), (8, 257), (8, 64, 64), (8, 64, 64)]


def kernel(x, norm_w, w_in_even, gla_w_a_up, gla_b_a, gla_norm_w, fox_b_f, w_out_even, w_in_odd, rel_bias, conv_w, conv_b, lru_w_a, lru_b_a, lru_w_x, lru_b_x, lru_lambda, w_out_odd, w_mlp_up, w_mlp_down, loss_target, m_norm_w, m_w_in_even, m_gla_w_a_up, m_gla_b_a, m_gla_norm_w, m_fox_b_f, m_w_out_even, m_w_in_odd, m_rel_bias, m_conv_w, m_conv_b, m_lru_w_a, m_lru_b_a, m_lru_w_x, m_lru_b_x, m_lru_lambda, m_w_out_odd, m_w_mlp_up, m_w_mlp_down, v_norm_w, v_w_in_even, v_gla_w_a_up, v_gla_b_a, v_gla_norm_w, v_fox_b_f, v_w_out_even, v_w_in_odd, v_rel_bias, v_conv_w, v_conv_b, v_lru_w_a, v_lru_b_a, v_lru_w_x, v_lru_b_x, v_lru_lambda, v_w_out_odd, v_w_mlp_up, v_w_mlp_down):
    c_idx = lax.axis_index("c")

    small_local = [norm_w, gla_w_a_up[0], conv_w[0], conv_b[0], lru_b_a[0], lru_b_x[0], lru_lambda[0]]
    small_src = jnp.concatenate([a.reshape(-1) for a in small_local]).reshape(32, 128)
    mine = [small_src, w_in_even[0].astype(BF16), w_out_even[0].astype(BF16), w_mlp_up.astype(BF16),
            w_mlp_down.astype(BF16), w_in_odd[0].astype(BF16), w_out_odd[0].astype(BF16)]
    ag_sems, ag_srcs, ag_lands, ag_token = _ag_start(mine)

    def gathered(g, after):
        srcs_g, lands_g = _ag_wait(g, ag_sems[g], ag_srcs, ag_lands, after)
        return _ag_forward(g, srcs_g, lands_g)

    small4, w_in_e4, w_out_e = gathered(0, ag_token)
    me = 2 * lax.axis_index("x") + lax.axis_index("y")
    others = [k + (k >= me).astype(jnp.int32) for k in range(3)]
    where = jnp.stack([c_idx, me] + others).astype(jnp.int32)

    w_in_e = _even_cols_to_kernel(_unshard(w_in_e4, (1024, 3096), 1))
    g_small = _split(small4.reshape(4, 32 * 128), SMALL_SHARDED_SHAPES)
    nw_full = _unshard(g_small[0], (2, 4, 1024), 2)
    wa_up = _unshard(g_small[1], (16, 256), 1)
    cw = _unshard(g_small[2], (4, 512), 1)
    cb, lba, lbx, lam = [_unshard(g, (512,), 0).reshape(1, 512) for g in g_small[3:]]
    nw = lambda layer, i: nw_full[layer, i].reshape(1, D)

    wa_pad = jnp.pad(wa_up, ((0, 128 - 16), (0, 0)))
    gla_ba = gla_b_a.reshape(1, 256)
    gla_nw = gla_norm_w.reshape(1, 512)
    fox_bpad = jnp.pad(fox_b_f.reshape(1, 8), ((0, 0), (FOX_LANE0, 128 - FOX_LANE0 - 8)))
    rbp = jnp.pad(rel_bias[0], ((0, 0), (0, REL_PAD - 257)))
    wa_bd = _block_diag(lru_w_a[0])
    wx_bd = _block_diag(lru_w_x[0])

    x0 = x[0]
    tgt = loss_target[0]

    h0 = _prenorm(x0, nw(0, 0), "prenorm_l0_mix")
    proj_e = _mm(h0, w_in_e, "nn", tm=1024, tn=640, name="mm_in_even")
    cat0, s_prev = _gla_fwd(proj_e, wa_pad, gla_ba, gla_nw)
    cum_r = _fox_gate_fwd(proj_e, fox_bpad)
    cum_c = cum_r[:, FOX_LANE0:FOX_LANE0 + 8].T
    cat0 = _fox_fwd(proj_e, cum_r, cum_c, cat0)
    mix0 = _mm(cat0, w_out_e, "nn", tm=1024, tn=512, name="mm_out_even")
    x1 = _postnorm(x0, mix0, nw(0, 1), "postnorm_l0_mix")
    w_up, w_dn = gathered(1, x1)
    h1 = _prenorm(x1, nw(0, 2), "prenorm_l0_mlp")
    a0, r0 = _mm(h1, w_up, "nn", tm=1024, tn=1024, b_layer=0, relu_pair=True, name="mm_up_l0")
    d0 = _mm(a0, w_dn, "nn", tm=1024, tn=512, tk=2048, b_layer=0, name="mm_down_l0")
    x2 = _postnorm(x1, d0, nw(0, 3), "postnorm_l0_mlp")

    w_in_o, w_out_o = gathered(2, x2)
    h2 = _prenorm(x2, nw(1, 0), "prenorm_l1_mix")
    proj_o = _mm(h2, w_in_o, "nn", tm=1024, tn=640, name="mm_in_odd")
    bias_q = _bias_build(rbp)
    bias = bias_q.transpose(1, 0, 2)
    kvpad = jnp.pad(proj_o[:, 512:1536], ((CA_PAD, 0), (0, 0)))
    cat1 = _ca_fwd(proj_o, kvpad, bias)
    x_in = proj_o[:, 2048:2560]
    xs = jnp.stack([_shift_down(x_in, 3 - j) for j in range(4)])
    lru_a, lru_b = _lru_pre_fwd(xs, cw, cb, wa_bd, lba, wx_bd, lbx, lam)
    hh = _lru_scan_fwd(lru_a, lru_b)
    cat1 = _lru_post_fwd(hh, proj_o, cat1)
    mix1 = _mm(cat1, w_out_o, "nn", tm=1024, tn=512, name="mm_out_odd")
    x3 = _postnorm(x2, mix1, nw(1, 1), "postnorm_l1_mix")
    h3 = _prenorm(x3, nw(1, 2), "prenorm_l1_mlp")
    a1, r1 = _mm(h3, w_up, "nn", tm=1024, tn=1024, b_layer=1, relu_pair=True, name="mm_up_l1")
    d1 = _mm(a1, w_dn, "nn", tm=1024, tn=512, tk=2048, b_layer=1, name="mm_down_l1")
    x4 = _postnorm(x3, d1, nw(1, 3), "postnorm_l1_mlp")

    g4, loss_part = _loss_and_grad(x4, tgt)
    loss = lax.psum(loss_part[0, 0], ("x", "y", "c"))

    def rs_begin(gb, tag):
        pair_sum = _pair_add(gb, _pair_swap_cols(gb, tag), where, tag)
        return _a2a_start(pair_sum, tag)

    def rs_end(started, after, tag):
        send_sems, recv_sems, p, q, _ = started
        p, q = _a2a_wait(send_sems, recv_sems, p, q, after, tag)
        halves = _handover(_sum_chips(p, q, where, tag), tag)
        return jnp.concatenate([halves[0], halves[1]], axis=1)

    gba = jnp.zeros((4, GA_ROWS, D), BF16)
    dd1, dnw13 = _norm_bwd(d1, nw(1, 3), g4, None, "postnorm_l1_mlp_bwd")
    gba = _mm(a1, dd1, "tn", tm=512, tn=1024, into=(gba, 1024, GA_DN), name="mm_down_l1_dw")
    du1 = _mm(dd1, w_dn, "nt", tm=1024, tn=1024, b_layer=1, times2=r1, out_dtype=BF16, name="mm_down_l1_dx")
    gba = _mm(du1, h3, "tn", tm=512, tn=1024, into=(gba, 1024, GA_UP), name="mm_up_l1_dw")
    dh3 = _mm(du1, w_up, "nt", tm=1024, tn=512, tk=2048, b_layer=1, name="mm_up_l1_dx")
    g3, dnw12 = _norm_bwd(x3, nw(1, 2), dh3, g4, "prenorm_l1_mlp_bwd")
    dmix1, dnw11 = _norm_bwd(mix1, nw(1, 1), g3, None, "postnorm_l1_mix_bwd")
    gba = _mm(cat1, dmix1, "tn", tm=128, tn=1024, into=(gba, 256, GA_OUT_O), name="mm_out_odd_dw")
    dcat1 = _mm(dmix1, w_out_o, "nt", tm=1024, tn=512, name="mm_out_odd_dx")

    dq_c, dkpad, dvpad, dbias = _ca_bwd(proj_o, kvpad, bias, dcat1)
    g_rel = _bias_grad(dbias.transpose(1, 0, 2))[:, :257]
    dhh, dgate = _lru_post_bwd(hh, proj_o, dcat1)
    da_l, db_l = _lru_scan_bwd(lru_a, hh, dhh)
    dxs, g_cw, g_cb, g_wa_bd, g_lba, g_wx_bd, g_lbx, g_lam = _lru_pre_bwd(xs, cw, cb, wa_bd, lba, wx_bd, lbx, lam, da_l, db_l)
    dx_in = _conv_dx(jnp.stack([_shift_up(dxs[j], 3 - j) for j in range(4)]))
    dproj_o = jnp.concatenate([dq_c, dkpad[CA_PAD:], dvpad[CA_PAD:], dgate, dx_in], axis=1)
    gba = _mm(dproj_o, h2, "tn", tm=128, tn=1024, into=(gba, 640, GA_IN_O), name="mm_in_odd_dw")
    rs_a = rs_begin(gba, "a")
    dh2 = _mm(dproj_o, w_in_o, "nt", tm=1024, tn=512, tk=1280, name="mm_in_odd_dx")
    g2, dnw10 = _norm_bwd(x2, nw(1, 0) + rs_a[4][0, 0], dh2, g3, "prenorm_l1_mix_bwd")

    gbb = lax.empty((4, GB_ROWS, D), BF16)
    dd0, dnw03 = _norm_bwd(d0, nw(0, 3), g2, None, "postnorm_l0_mlp_bwd")
    gbb = _mm(a0, dd0, "tn", tm=512, tn=1024, into=(gbb, 1024, GB_DN), name="mm_down_l0_dw")
    du0 = _mm(dd0, w_dn, "nt", tm=1024, tn=1024, b_layer=0, times2=r0, out_dtype=BF16, name="mm_down_l0_dx")
    gbb = _mm(du0, h1, "tn", tm=512, tn=1024, into=(gbb, 1024, GB_UP), name="mm_up_l0_dw")
    rs_b = rs_begin(gbb, "b")
    dh1 = _mm(du0, w_up, "nt", tm=1024, tn=512, tk=2048, b_layer=0, name="mm_up_l0_dx")
    g1, dnw02 = _norm_bwd(x1, nw(0, 2) + rs_b[4][0, 0], dh1, g2, "prenorm_l0_mlp_bwd")
    dmix0, dnw01 = _norm_bwd(mix0, nw(0, 1), g1, None, "postnorm_l0_mix_bwd")
    gbc = lax.empty((4, GC_ROWS, D), BF16)
    gbc = _mm(cat0, dmix0, "tn", tm=128, tn=1024, into=(gbc, 256, GC_OUT_E), name="mm_out_even_dw")
    dcat0 = _mm(dmix0, w_out_e, "nt", tm=1024, tn=512, name="mm_out_even_dx")

    dq_g, dk_g, dv_g, dr_g, daux_g, g_wa_pad, g_gla_ba, g_gla_nw = _gla_bwd(proj_e, s_prev, wa_pad, gla_ba, gla_nw, dcat0)
    dq_f, dk_f, dv_f, dcrow, dccol = _fox_bwd(proj_e, cum_r, cum_c, dcat0)
    dccol_t = jnp.pad(dccol.sum(axis=0).T, ((0, 0), (FOX_LANE0, 128 - FOX_LANE0 - 8)))
    daux, g_fox_bpad = _fox_gate_bwd(proj_e, fox_bpad, dcrow, dccol_t, daux_g)
    dproj_e = jnp.concatenate([dq_g, dk_g, dv_g, dr_g, dq_f, dk_f, dv_f, daux], axis=1)
    gt_in_e = _mm(dproj_e, h0, "tn", tm=640, tn=1024, name="mm_in_even_dw")
    dh0 = _mm(dproj_e, w_in_e, "nt", tm=1024, tn=512, tk=640, name="mm_in_even_dx")
    grad_x, dnw00 = _norm_bwd(x0, nw(0, 0), dh0, g1, "prenorm_l0_mix_bwd")

    g_norm = jnp.stack([jnp.concatenate([dnw00, dnw01, dnw02, dnw03]), jnp.concatenate([dnw10, dnw11, dnw12, dnw13])])
    sharded = [(g_norm, 2), (g_wa_pad[:16], 1), (g_cw, 1), (g_cb[0], 0), (g_lba[0], 0), (g_lbx[0], 0), (g_lam[0], 0)]
    replicated = [g_gla_ba[0], g_gla_nw[0], g_fox_bpad[0, FOX_LANE0:FOX_LANE0 + 8], g_rel, _diag_blocks(g_wa_bd),
                  _diag_blocks(g_wx_bd)]
    small4 = jnp.concatenate([_shard_major(g, ax) for g, ax in sharded]
                             + [jnp.broadcast_to(g.reshape(1, -1), (4, g.size)) for g in replicated], axis=1)
    n_small = small4.shape[1]
    small_rows = GC_ROWS - GC_TAIL - 774
    small4 = jnp.pad(small4, ((0, 0), (0, small_rows * D - n_small))).reshape(4, small_rows, D)
    gt_rows = jnp.concatenate([gt_in_e[:1536], gt_in_e[3072:3088], gt_in_e[1536:3072], gt_in_e[3088:3096]], axis=0)
    tail = jnp.concatenate([gt_rows.reshape(4, 774, D), small4], axis=1).astype(BF16)
    gbc = lax.dynamic_update_slice(gbc, tail, (0, GC_TAIL, 0))
    rs_c = rs_begin(gbc, "c")

    red_a = rs_end(rs_a, rs_c[4], "a")
    red_b = rs_end(rs_b, red_a, "b")
    red_c = rs_end(rs_c, red_b, "c")

    g_up = jnp.stack([red_b[GB_UP:GB_UP + 1024].T, red_a[GA_UP:GA_UP + 1024].T])
    g_dn = jnp.stack([red_b[GB_DN:GB_DN + 1024], red_a[GA_DN:GA_DN + 1024]])
    g_small = _split(red_c[GC_TAIL + 774:].reshape(-1)[:n_small], SMALL_SHARDED_SHAPES + REPL_SHAPES)
    g_of = dict(zip(["norm_w", "gla_w_a_up", "conv_w", "conv_b", "lru_b_a", "lru_b_x", "lru_lambda", "gla_b_a",
                     "gla_norm_w", "fox_b_f", "rel_bias", "lru_w_a", "lru_w_x"], g_small))
    g_of.update(w_mlp_up=g_up, w_mlp_down=g_dn, w_in_odd=red_a[GA_IN_O:GA_IN_O + 640].T,
                w_out_even=red_c[GC_OUT_E:GC_OUT_E + 256], w_out_odd=red_a[GA_OUT_O:GA_OUT_O + 256],
                w_in_even=red_c[GC_TAIL:GC_TAIL + 774].T)

    names = ["norm_w", "w_in_even", "gla_w_a_up", "gla_b_a", "gla_norm_w", "fox_b_f", "w_out_even", "w_in_odd", "rel_bias",
             "conv_w", "conv_b", "lru_w_a", "lru_b_a", "lru_w_x", "lru_b_x", "lru_lambda", "w_out_odd", "w_mlp_up",
             "w_mlp_down"]
    w_of = dict(norm_w=norm_w, w_in_even=w_in_even, gla_w_a_up=gla_w_a_up, gla_b_a=gla_b_a, gla_norm_w=gla_norm_w,
                fox_b_f=fox_b_f, w_out_even=w_out_even, w_in_odd=w_in_odd, rel_bias=rel_bias, conv_w=conv_w, conv_b=conv_b,
                lru_w_a=lru_w_a, lru_b_a=lru_b_a, lru_w_x=lru_w_x, lru_b_x=lru_b_x, lru_lambda=lru_lambda,
                w_out_odd=w_out_odd, w_mlp_up=w_mlp_up, w_mlp_down=w_mlp_down)
    m_of = dict(norm_w=m_norm_w, w_in_even=m_w_in_even, gla_w_a_up=m_gla_w_a_up, gla_b_a=m_gla_b_a,
                gla_norm_w=m_gla_norm_w, fox_b_f=m_fox_b_f, w_out_even=m_w_out_even, w_in_odd=m_w_in_odd,
                rel_bias=m_rel_bias, conv_w=m_conv_w, conv_b=m_conv_b, lru_w_a=m_lru_w_a, lru_b_a=m_lru_b_a,
                lru_w_x=m_lru_w_x, lru_b_x=m_lru_b_x, lru_lambda=m_lru_lambda, w_out_odd=m_w_out_odd,
                w_mlp_up=m_w_mlp_up, w_mlp_down=m_w_mlp_down)
    v_of = dict(norm_w=v_norm_w, w_in_even=v_w_in_even, gla_w_a_up=v_gla_w_a_up, gla_b_a=v_gla_b_a,
                gla_norm_w=v_gla_norm_w, fox_b_f=v_fox_b_f, w_out_even=v_w_out_even, w_in_odd=v_w_in_odd,
                rel_bias=v_rel_bias, conv_w=v_conv_w, conv_b=v_conv_b, lru_w_a=v_lru_w_a, lru_b_a=v_lru_b_a,
                lru_w_x=v_lru_w_x, lru_b_x=v_lru_b_x, lru_lambda=v_lru_lambda, w_out_odd=v_w_out_odd,
                w_mlp_up=v_w_mlp_up, w_mlp_down=v_w_mlp_down)
    grads, deltas, new_ms, new_vs = [], [], [], []
    for n in names:
        w = w_of[n]
        view = w.shape if w.ndim <= 3 else w.shape[-3:]
        g = g_of[n].reshape(w.shape)
        d, mn, vn = _adamw(w.reshape(view), g.reshape(view), m_of[n].reshape(view), v_of[n].reshape(view),
                           "adamw_" + n)
        grads.append(g)
        deltas.append(d.reshape(w.shape))
        new_ms.append(mn.reshape(w.shape))
        new_vs.append(vn.reshape(w.shape))

    return (loss, grad_x.reshape(1, T, D), *grads, *deltas, *new_ms, *new_vs)
```

```python
import functools

import jax
import jax.numpy as jnp
from jax import lax
from jax.experimental import pallas as pl
from jax.experimental.pallas import tpu as pltpu

F32 = jnp.float32
BF16 = jnp.bfloat16
MESH = pl.DeviceIdType.MESH

T = 2048
D = 1024
DFF = 4096
EPS = 1e-6
CHUNK = 64
NCHUNK = T // CHUNK
PE = 3200
PO = 2560
AUX_BLK = 3072 // 128
FOX_LANE0 = 16
GLA_SCALE = 64 ** -0.5
ATT_SCALE = 64 ** -0.5
NEG = float(jnp.finfo(jnp.float32).min)
CA_BAND = 576
CA_PAD = 512
REL_PAD = 384

VMEM_LIMIT = 48 * 1024 * 1024

ADAM_LR, ADAM_B1, ADAM_B2, ADAM_EPS, ADAM_WD, ADAM_STEP = 0.001, 0.9, 0.999, 1e-08, 0.01, 10

GA_ROWS, GA_UP, GA_DN, GA_IN_O, GA_OUT_O = 3072, 0, 1024, 2048, 2688
GB_ROWS, GB_UP, GB_DN = 2048, 0, 1024
GC_ROWS, GC_OUT_E, GC_TAIL = 1152, 0, 256

_DIMS = {"nn": (((1,), (0,)), ((), ())), "nt": (((1,), (1,)), ((), ())), "tn": (((0,), (0,)), ((), ()))}


def _cp(sem, **kw):
    return pltpu.CompilerParams(dimension_semantics=sem, vmem_limit_bytes=VMEM_LIMIT, **kw)


def _dot(a, b, mode):
    return lax.dot_general(a.astype(BF16), b.astype(BF16), _DIMS[mode], preferred_element_type=F32)


@functools.partial(jax.custom_vjp, nondiff_argnums=(2,))
def bdot(a, b, mode):
    return _dot(a, b, mode)


def _bdot_fwd(a, b, mode):
    return _dot(a, b, mode), (a, b)


def _bdot_bwd(mode, res, g):
    a, b = res
    if mode == "nn":
        da, db = _dot(g, b, "nt"), _dot(a, g, "tn")
    elif mode == "nt":
        da, db = _dot(g, b, "nn"), _dot(g, a, "tn")
    else:
        da, db = _dot(b, g, "nt"), _dot(a, g, "nn")
    return da.astype(a.dtype), db.astype(b.dtype)


bdot.defvjp(_bdot_fwd, _bdot_bwd)


def _hdot_raw(a, b, mode):
    return lax.dot_general(a, b, _DIMS[mode], precision=lax.Precision.HIGHEST, preferred_element_type=F32)


@functools.partial(jax.custom_vjp, nondiff_argnums=(2,))
def hdot(a, b, mode):
    return _hdot_raw(a, b, mode)


def _hdot_fwd(a, b, mode):
    return _hdot_raw(a, b, mode), (a, b)


def _hdot_bwd(mode, res, g):
    a, b = res
    if mode == "nn":
        return _hdot_raw(g, b, "nt"), _hdot_raw(a, g, "tn")
    if mode == "nt":
        return _hdot_raw(g, b, "nn"), _hdot_raw(g, a, "tn")
    return _hdot_raw(b, g, "nt"), _hdot_raw(a, g, "nn")


hdot.defvjp(_hdot_fwd, _hdot_bwd)


def _log_sigmoid(x):
    return jnp.minimum(x, 0.0) - jnp.log(1.0 + jnp.exp(-jnp.abs(x)))


def _sigmoid(x):
    return 1.0 / (1.0 + jnp.exp(-x))


def _expm1(x):
    series = x * (1.0 + x * 0.5 * (1.0 + x * (1.0 / 3.0) * (1.0 + x * 0.25)))
    return jnp.where(jnp.abs(x) < 0.03, series, jnp.exp(x) - 1.0)


def _gelu_tanh(x):
    return 0.5 * x * (1.0 + jnp.tanh(0.7978845608028654 * (x + 0.044715 * x * x * x)))


def _softmax_rows(s):
    m = jnp.max(s, axis=-1, keepdims=True)
    p = jnp.exp(s - m)
    return p / jnp.sum(p, axis=-1, keepdims=True)


def _iota(shape, dim):
    return lax.broadcasted_iota(jnp.int32, shape, dim)


def _mm(a, b, mode, *, tm, tn, tk=None, out_dtype=F32, name, b_layer=None, into=None, relu_pair=False, times2=None):
    b2 = b.shape[-2:]
    if mode == "nn":
        (m, k), n = a.shape, b2[1]
    elif mode == "nt":
        (m, k), n = a.shape, b2[0]
    else:
        (k, m), n = a.shape, b2[1]
    tk = k if tk is None else tk
    assert m % tm == 0 and n % tn == 0 and k % tk == 0, (name, a.shape, b.shape)
    nk = k // tk
    a_spec = {"nn": pl.BlockSpec((tm, tk), lambda i, j, kk: (i, kk)),
              "nt": pl.BlockSpec((tm, tk), lambda i, j, kk: (i, kk)),
              "tn": pl.BlockSpec((tk, tm), lambda i, j, kk: (kk, i))}[mode]
    b_blk = {"nn": (tk, tn), "nt": (tn, tk), "tn": (tk, tn)}[mode]
    b_idx = {"nn": lambda i, j, kk: (kk, j), "nt": lambda i, j, kk: (j, kk), "tn": lambda i, j, kk: (kk, j)}[mode]
    if b_layer is None:
        b_spec = pl.BlockSpec(b_blk, b_idx)
    else:
        b_spec = pl.BlockSpec((None,) + b_blk, lambda i, j, kk: (b_layer,) + b_idx(i, j, kk))

    tile = pl.BlockSpec((tm, tn), lambda i, j, kk: (i, j))
    if into is not None:
        buf, per_slot, row_off = into
        assert m == 4 * per_slot and per_slot % tm == 0 and row_off % tm == 0 and buf.shape[2] == n, (name, buf.shape)
        bps = per_slot // tm
        out_specs = pl.BlockSpec((None, tm, tn), lambda i, j, kk: (i // bps, row_off // tm + i % bps, j))
        out_shape = jax.ShapeDtypeStruct(buf.shape, buf.dtype)
        extra_in, extra_specs, aliases = [buf], [pl.BlockSpec(memory_space=pl.ANY)], {2: 0}
        finish = lambda acc, extra: [acc.astype(buf.dtype)]
    elif relu_pair:
        out_specs = (tile, tile)
        out_shape = (jax.ShapeDtypeStruct((m, n), BF16),) * 2
        extra_in, extra_specs, aliases = [], [], {}

        def finish(acc, extra):
            r = jnp.maximum(acc, 0.0)
            return [(r * r).astype(BF16), r.astype(BF16)]
    elif times2 is not None:
        out_specs = tile
        out_shape = jax.ShapeDtypeStruct((m, n), out_dtype)
        extra_in, extra_specs, aliases = [times2], [tile], {}
        finish = lambda acc, extra: [(acc * (2.0 * extra[...].astype(F32))).astype(out_dtype)]
    else:
        out_specs = tile
        out_shape = jax.ShapeDtypeStruct((m, n), out_dtype)
        extra_in, extra_specs, aliases = [], [], {}
        finish = lambda acc, extra: [acc.astype(out_dtype)]
    n_out = 2 if relu_pair else 1

    def body(*refs):
        a_ref, b_ref = refs[0], refs[1]
        extra = refs[2] if extra_in else None
        o_refs = refs[2 + len(extra_in):2 + len(extra_in) + n_out]

        def store(acc):
            for o_ref, val in zip(o_refs, finish(acc, extra)):
                o_ref[...] = val

        if nk == 1:
            store(_dot(a_ref[...], b_ref[...], mode))
            return
        acc_ref = refs[-1]
        kk = pl.program_id(2)

        @pl.when(kk == 0)
        def _():
            acc_ref[...] = jnp.zeros_like(acc_ref)

        acc_ref[...] += _dot(a_ref[...], b_ref[...], mode)

        @pl.when(kk == nk - 1)
        def _():
            store(acc_ref[...])

    return pl.pallas_call(
        body, name=name, grid=(m // tm, n // tn, nk),
        in_specs=[a_spec, b_spec] + extra_specs,
        out_specs=out_specs, out_shape=out_shape,
        scratch_shapes=[pltpu.VMEM((tm, tn), F32)] if nk > 1 else [],
        input_output_aliases=aliases,
        compiler_params=_cp(("parallel", "parallel", "arbitrary")),
    )(a, b, *extra_in)


ROWS = 256


def _prenorm(x, w, name):
    def body(x_ref, w_ref, o_ref):
        xv = x_ref[...]
        r = lax.rsqrt(jnp.mean(xv * xv, axis=-1, keepdims=True) + EPS)
        o_ref[...] = (xv * r * w_ref[...]).astype(BF16)

    return pl.pallas_call(
        body, name=name, grid=(T // ROWS,),
        in_specs=[pl.BlockSpec((ROWS, D), lambda i: (i, 0)), pl.BlockSpec((1, D), lambda i: (0, 0))],
        out_specs=pl.BlockSpec((ROWS, D), lambda i: (i, 0)),
        out_shape=jax.ShapeDtypeStruct((T, D), BF16),
        compiler_params=_cp(("parallel",)),
    )(x, w)


def _postnorm(x, z, w, name):
    def body(x_ref, z_ref, w_ref, o_ref):
        zv = z_ref[...]
        r = lax.rsqrt(jnp.mean(zv * zv, axis=-1, keepdims=True) + EPS)
        o_ref[...] = x_ref[...] + zv * r * w_ref[...]

    return pl.pallas_call(
        body, name=name, grid=(T // ROWS,),
        in_specs=[pl.BlockSpec((ROWS, D), lambda i: (i, 0)), pl.BlockSpec((ROWS, D), lambda i: (i, 0)),
                  pl.BlockSpec((1, D), lambda i: (0, 0))],
        out_specs=pl.BlockSpec((ROWS, D), lambda i: (i, 0)),
        out_shape=jax.ShapeDtypeStruct((T, D), F32),
        compiler_params=_cp(("parallel",)),
    )(x, z, w)


def _norm_bwd(z, w, dy, add, name):
    has_add = add is not None

    def body(*refs):
        if has_add:
            z_ref, w_ref, dy_ref, add_ref, dz_ref, dw_ref = refs
        else:
            z_ref, w_ref, dy_ref, dz_ref, dw_ref = refs
        i = pl.program_id(0)

        @pl.when(i == 0)
        def _():
            dw_ref[...] = jnp.zeros_like(dw_ref)

        zv = z_ref[...].astype(F32)
        dyv = dy_ref[...]
        r = lax.rsqrt(jnp.mean(zv * zv, axis=-1, keepdims=True) + EPS)
        wdy = dyv * w_ref[...]
        dz = r * wdy - zv * (r * r * r) * jnp.mean(zv * wdy, axis=-1, keepdims=True)
        if has_add:
            dz = dz + add_ref[...]
        dz_ref[...] = dz.astype(dz_ref.dtype)
        dw_ref[...] += jnp.sum(dyv * zv * r, axis=0, keepdims=True)

    row = pl.BlockSpec((ROWS, D), lambda i: (i, 0))
    vec = pl.BlockSpec((1, D), lambda i: (0, 0))
    ins = [z, w, dy] + ([add] if has_add else [])
    dz_dtype = F32 if has_add else BF16
    return pl.pallas_call(
        body, name=name, grid=(T // ROWS,),
        in_specs=[row, vec, row] + ([row] if has_add else []),
        out_specs=(row, vec),
        out_shape=(jax.ShapeDtypeStruct((T, D), dz_dtype), jax.ShapeDtypeStruct((1, D), F32)),
        compiler_params=_cp(("arbitrary",)),
    )(*ins)


def _loss_and_grad(y, tgt):
    def body(y_ref, t_ref, g_ref, l_ref):
        i = pl.program_id(0)

        @pl.when(i == 0)
        def _():
            l_ref[...] = jnp.zeros_like(l_ref)

        e = y_ref[...] - t_ref[...]
        g_ref[...] = e * (1.0 / D)
        l_ref[...] += jnp.sum(e * e) * (0.5 / D)

    row = pl.BlockSpec((ROWS, D), lambda i: (i, 0))
    return pl.pallas_call(
        body, name="loss_head", grid=(T // ROWS,), in_specs=[row, row],
        out_specs=(row, pl.BlockSpec((1, 128), lambda i: (0, 0))),
        out_shape=(jax.ShapeDtypeStruct((T, D), F32), jax.ShapeDtypeStruct((1, 128), F32)),
        compiler_params=_cp(("arbitrary",)),
    )(y, tgt)


def _adamw(w, g, m, v, name):
    lead = w.shape[:-2]
    assert len(lead) <= 1 and g.shape == w.shape, (name, w.shape, g.shape)
    rows, cols = w.shape[-2:]
    if rows <= 512:
        tr, tc = rows, cols
    elif rows % 256 == 0:
        tr, tc = 256, cols
    else:
        tr, tc = rows, 256
    assert rows % tr == 0 and cols % tc == 0, (name, w.shape)
    c1 = 1.0 - ADAM_B1 ** ADAM_STEP
    c2 = 1.0 - ADAM_B2 ** ADAM_STEP

    def body(w_ref, g_ref, m_ref, v_ref, d_ref, mo_ref, vo_ref):
        gv = g_ref[...]
        mn = ADAM_B1 * m_ref[...] + (1.0 - ADAM_B1) * gv
        vn = ADAM_B2 * v_ref[...] + (1.0 - ADAM_B2) * (gv * gv)
        m_hat = mn / c1
        v_hat = vn / c2
        d_ref[...] = -ADAM_LR * (m_hat / (jnp.sqrt(v_hat) + ADAM_EPS) + ADAM_WD * w_ref[...])
        mo_ref[...] = mn
        vo_ref[...] = vn

    if lead:
        grid = (lead[0], rows // tr, cols // tc)
        blk = pl.BlockSpec((None, tr, tc), lambda l, i, j: (l, i, j))
    else:
        grid = (rows // tr, cols // tc)
        blk = pl.BlockSpec((tr, tc), lambda i, j: (i, j))
    sds = jax.ShapeDtypeStruct(w.shape, F32)
    return pl.pallas_call(body, name=name, grid=grid, in_specs=[blk] * 4, out_specs=(blk,) * 3,
                          out_shape=(sds,) * 3, compiler_params=_cp(("parallel",) * len(grid)))(w, g, m, v)


def _gla_consts():
    ltri = (_iota((CHUNK, CHUNK), 0) >= _iota((CHUNK, CHUNK), 1)).astype(F32)
    ones_c = jnp.ones((CHUNK, 128), F32)
    mask = (_iota((256, 512), 0) // 64 == _iota((256, 512), 1) // 128).astype(F32)
    return ltri, ones_c, mask


def _gla_chunk(consts, q, k, v, r, aux, s_prev, wa, ba, nw):
    ltri, ones_c, mask = consts
    la = _log_sigmoid(bdot(aux, wa, "nn") + ba) * (1.0 / 16.0)
    cum = hdot(ltri, la, "nn")
    total = jnp.sum(la, axis=0, keepdims=True)
    k_dec = k * jnp.exp(total - cum)
    inc = bdot(k_dec, v, "tn") * mask
    dec = jnp.exp(hdot(la, ones_c, "tn"))
    dec = jnp.concatenate([dec, dec, dec, dec], axis=1)
    s_new = dec * s_prev + inc
    o = bdot(q * GLA_SCALE, s_new, "nn")
    parts = []
    for h in range(4):
        oh = o[:, h * 128:(h + 1) * 128]
        parts.append(oh * lax.rsqrt(jnp.mean(oh * oh, axis=-1, keepdims=True) + EPS))
    on = jnp.concatenate(parts, axis=1)
    return s_new, on * nw * (r * _sigmoid(r))


def _gla_specs(cmap):
    return [pl.BlockSpec((CHUNK, 256), lambda c: (cmap(c), 0)),
            pl.BlockSpec((CHUNK, 256), lambda c: (cmap(c), 1)),
            pl.BlockSpec((CHUNK, 512), lambda c: (cmap(c), 1)),
            pl.BlockSpec((CHUNK, 512), lambda c: (cmap(c), 2)),
            pl.BlockSpec((CHUNK, 128), lambda c: (cmap(c), AUX_BLK))]


def _gla_fwd(proj, wa, ba, nw):
    def body(q_ref, k_ref, v_ref, r_ref, aux_ref, wa_ref, ba_ref, nw_ref, o_ref, sp_ref, s_ref):
        c = pl.program_id(0)

        @pl.when(c == 0)
        def _():
            s_ref[...] = jnp.zeros_like(s_ref)

        s_prev = s_ref[...]
        sp_ref[...] = s_prev
        s_new, out = _gla_chunk(_gla_consts(), q_ref[...], k_ref[...], v_ref[...], r_ref[...], aux_ref[...],
                                s_prev, wa_ref[...], ba_ref[...], nw_ref[...])
        s_ref[...] = s_new
        o_ref[...] = out

    full = lambda shape: pl.BlockSpec(shape, lambda c: (0,) * len(shape))
    return pl.pallas_call(
        body, name="gla_fwd", grid=(NCHUNK,),
        in_specs=_gla_specs(lambda c: c) + [full((128, 256)), full((1, 256)), full((1, 512))],
        out_specs=(pl.BlockSpec((CHUNK, 512), lambda c: (c, 0)), pl.BlockSpec((None, 256, 512), lambda c: (c, 0, 0))),
        out_shape=(jax.ShapeDtypeStruct((T, D), F32), jax.ShapeDtypeStruct((NCHUNK, 256, 512), F32)),
        scratch_shapes=[pltpu.VMEM((256, 512), F32)],
        compiler_params=_cp(("arbitrary",)),
    )(proj, proj, proj, proj, proj, wa, ba, nw)


def _gla_bwd(proj, s_prev_all, wa, ba, nw, dcat):
    rev = lambda c: NCHUNK - 1 - c

    def body(q_ref, k_ref, v_ref, r_ref, aux_ref, sp_ref, wa_ref, ba_ref, nw_ref, do_ref,
             dq_ref, dk_ref, dv_ref, dr_ref, daux_ref, dwa_ref, dba_ref, dnw_ref, ds_ref):
        c = pl.program_id(0)

        @pl.when(c == 0)
        def _():
            ds_ref[...] = jnp.zeros_like(ds_ref)
            dwa_ref[...] = jnp.zeros_like(dwa_ref)
            dba_ref[...] = jnp.zeros_like(dba_ref)
            dnw_ref[...] = jnp.zeros_like(dnw_ref)

        fn = functools.partial(_gla_chunk, _gla_consts())
        _, vjp = jax.vjp(fn, q_ref[...], k_ref[...], v_ref[...], r_ref[...], aux_ref[...], sp_ref[...],
                         wa_ref[...], ba_ref[...], nw_ref[...])
        dq, dk, dv, dr, daux, dsp, dwa, dba, dnw = vjp((ds_ref[...], do_ref[...]))
        dq_ref[...] = dq
        dk_ref[...] = dk
        dv_ref[...] = dv
        dr_ref[...] = dr
        daux_ref[...] = daux
        ds_ref[...] = dsp
        dwa_ref[...] += dwa
        dba_ref[...] += dba
        dnw_ref[...] += dnw

    full = lambda shape: pl.BlockSpec(shape, lambda c: (0,) * len(shape))
    blk = lambda w: pl.BlockSpec((CHUNK, w), lambda c: (rev(c), 0))
    sds = lambda *s: jax.ShapeDtypeStruct(s, F32)
    return pl.pallas_call(
        body, name="gla_bwd", grid=(NCHUNK,),
        in_specs=_gla_specs(rev) + [pl.BlockSpec((None, 256, 512), lambda c: (rev(c), 0, 0)),
                                    full((128, 256)), full((1, 256)), full((1, 512)), blk(512)],
        out_specs=(blk(256), blk(256), blk(512), blk(512), blk(128), full((128, 256)), full((1, 256)), full((1, 512))),
        out_shape=(sds(T, 256), sds(T, 256), sds(T, 512), sds(T, 512), sds(T, 128),
                   sds(128, 256), sds(1, 256), sds(1, 512)),
        scratch_shapes=[pltpu.VMEM((256, 512), F32)],
        compiler_params=_cp(("arbitrary",)),
    )(proj, proj, proj, proj, proj, s_prev_all, wa, ba, nw, dcat)


GATE_ROWS = 128


def _fox_gate_block(ltri, aux, bpad, carry):
    lf = _log_sigmoid(aux + bpad)
    cum = hdot(ltri, lf, "nn") + carry
    return cum, carry + jnp.sum(lf, axis=0, keepdims=True)


def _gate_ltri():
    return (_iota((GATE_ROWS, GATE_ROWS), 0) >= _iota((GATE_ROWS, GATE_ROWS), 1)).astype(F32)


def _fox_gate_fwd(proj, bpad):
    def body(aux_ref, b_ref, cum_ref, carry_ref):
        i = pl.program_id(0)

        @pl.when(i == 0)
        def _():
            carry_ref[...] = jnp.zeros_like(carry_ref)

        cum, carry = _fox_gate_block(_gate_ltri(), aux_ref[...], b_ref[...], carry_ref[...])
        cum_ref[...] = cum
        carry_ref[...] = carry

    return pl.pallas_call(
        body, name="fox_gate_fwd", grid=(T // GATE_ROWS,),
        in_specs=[pl.BlockSpec((GATE_ROWS, 128), lambda i: (i, AUX_BLK)), pl.BlockSpec((1, 128), lambda i: (0, 0))],
        out_specs=pl.BlockSpec((GATE_ROWS, 128), lambda i: (i, 0)),
        out_shape=jax.ShapeDtypeStruct((T, 128), F32),
        scratch_shapes=[pltpu.VMEM((1, 128), F32)],
        compiler_params=_cp(("arbitrary",)),
    )(proj, bpad)


def _fox_gate_bwd(proj, bpad, dcrow, dccol_t, daux_gla):
    nb = T // GATE_ROWS
    rev = lambda i: nb - 1 - i

    def body(aux_ref, b_ref, dr_ref, dc_ref, dg_ref, daux_ref, db_ref, dcarry_ref):
        i = pl.program_id(0)

        @pl.when(i == 0)
        def _():
            dcarry_ref[...] = jnp.zeros_like(dcarry_ref)
            db_ref[...] = jnp.zeros_like(db_ref)

        dcum = dr_ref[0] + dr_ref[1] + dr_ref[2] + dr_ref[3] + dc_ref[...]
        fn = functools.partial(_fox_gate_block, _gate_ltri())
        _, vjp = jax.vjp(fn, aux_ref[...], b_ref[...], jnp.zeros((1, 128), F32))
        daux, db, dcarry = vjp((dcum, dcarry_ref[...]))
        daux_ref[...] = daux + dg_ref[...]
        db_ref[...] += db
        dcarry_ref[...] = dcarry

    blk = pl.BlockSpec((GATE_ROWS, 128), lambda i: (rev(i), 0))
    vec = pl.BlockSpec((1, 128), lambda i: (0, 0))
    return pl.pallas_call(
        body, name="fox_gate_bwd", grid=(nb,),
        in_specs=[pl.BlockSpec((GATE_ROWS, 128), lambda i: (rev(i), AUX_BLK)), vec,
                  pl.BlockSpec((4, GATE_ROWS, 128), lambda i: (0, rev(i), 0)), blk, blk],
        out_specs=(blk, vec),
        out_shape=(jax.ShapeDtypeStruct((T, 128), F32), jax.ShapeDtypeStruct((1, 128), F32)),
        scratch_shapes=[pltpu.VMEM((1, 128), F32)],
        compiler_params=_cp(("arbitrary",)),
    )(proj, bpad, dcrow, dccol_t, daux_gla)


FOX_Q = 128


FOX_KEY_STEP = 512
FOX_GROUPS = T // FOX_KEY_STEP
FOX_Q_PER_GROUP = FOX_KEY_STEP // FOX_Q


def _fox_block(hp, qb, q, k, v, crow, ccol):
    kl = k.shape[0]
    lane = _iota((FOX_Q, 128), 1)
    causal = (qb * FOX_Q + _iota((FOX_Q, kl), 0)) >= _iota((FOX_Q, kl), 1)
    sub = _iota((8, kl), 0)
    outs = []
    for e in range(2):
        h = 2 * hp + e
        qm = jnp.where((lane >= 64 * e) & (lane < 64 * (e + 1)), q, 0.0)
        s = bdot(qm, k, "nt") * ATT_SCALE
        ct = jnp.sum(jnp.where(lane == FOX_LANE0 + h, crow, 0.0), axis=1, keepdims=True)
        cs = jnp.sum(jnp.where(sub == h, ccol, 0.0), axis=0, keepdims=True)
        s = jnp.where(causal, s + (ct - cs), NEG)
        outs.append(bdot(_softmax_rows(s), v, "nn"))
    return jnp.where(lane < 64, outs[0], outs[1])


def _fox_in_specs():
    return [pl.BlockSpec((FOX_Q, 128), lambda hp, qb: (qb, 12 + hp)),
            pl.BlockSpec((T, 128), lambda hp, qb: (0, 16 + hp)),
            pl.BlockSpec((T, 128), lambda hp, qb: (0, 20 + hp)),
            pl.BlockSpec((FOX_Q, 128), lambda hp, qb: (qb, 0)),
            pl.BlockSpec((8, T), lambda hp, qb: (0, 0))]


def _fox_fwd(proj, cum_r, cum_c, cat):
    def body(q_ref, k_ref, v_ref, cr_ref, cc_ref, cat_ref, o_ref):
        qb = pl.program_id(1)
        for g in range(FOX_GROUPS):
            kl = FOX_KEY_STEP * (g + 1)

            @pl.when(qb // FOX_Q_PER_GROUP == g)
            def _(kl=kl):
                o_ref[...] = _fox_block(pl.program_id(0), qb, q_ref[...], k_ref[0:kl, :], v_ref[0:kl, :],
                                        cr_ref[...], cc_ref[:, 0:kl])

    return pl.pallas_call(
        body, name="fox_fwd", grid=(4, T // FOX_Q), in_specs=_fox_in_specs() + [pl.BlockSpec(memory_space=pl.ANY)],
        out_specs=pl.BlockSpec((FOX_Q, 128), lambda hp, qb: (qb, 4 + hp)),
        out_shape=jax.ShapeDtypeStruct((T, D), F32), input_output_aliases={5: 0},
        compiler_params=_cp(("parallel", "parallel")),
    )(proj, proj, proj, cum_r, cum_c, cat)


def _fox_bwd(proj, cum_r, cum_c, dcat):
    def body(q_ref, k_ref, v_ref, cr_ref, cc_ref, do_ref, dq_ref, dk_ref, dv_ref, dcr_ref, dcc_ref):
        qb = pl.program_id(1)

        @pl.when(qb == 0)
        def _():
            dk_ref[...] = jnp.zeros_like(dk_ref)
            dv_ref[...] = jnp.zeros_like(dv_ref)
            dcc_ref[...] = jnp.zeros_like(dcc_ref)

        fn = functools.partial(_fox_block, pl.program_id(0), qb)
        for g in range(FOX_GROUPS):
            kl = FOX_KEY_STEP * (g + 1)

            @pl.when(qb // FOX_Q_PER_GROUP == g)
            def _(kl=kl):
                _, vjp = jax.vjp(fn, q_ref[...], k_ref[0:kl, :], v_ref[0:kl, :], cr_ref[...], cc_ref[:, 0:kl])
                dq, dk, dv, dcr, dcc = vjp(do_ref[...])
                dq_ref[...] = dq
                dk_ref[0:kl, :] += dk
                dv_ref[0:kl, :] += dv
                dcr_ref[...] = dcr
                dcc_ref[:, 0:kl] += dcc

    sds = lambda *s: jax.ShapeDtypeStruct(s, F32)
    return pl.pallas_call(
        body, name="fox_bwd", grid=(4, T // FOX_Q),
        in_specs=_fox_in_specs() + [pl.BlockSpec((FOX_Q, 128), lambda hp, qb: (qb, 4 + hp))],
        out_specs=(pl.BlockSpec((FOX_Q, 128), lambda hp, qb: (qb, hp)),
                   pl.BlockSpec((T, 128), lambda hp, qb: (0, hp)),
                   pl.BlockSpec((T, 128), lambda hp, qb: (0, hp)),
                   pl.BlockSpec((None, FOX_Q, 128), lambda hp, qb: (hp, qb, 0)),
                   pl.BlockSpec((None, 8, T), lambda hp, qb: (hp, 0, 0))),
        out_shape=(sds(T, 512), sds(T, 512), sds(T, 512), sds(4, T, 128), sds(4, 8, T)),
        compiler_params=_cp(("parallel", "arbitrary")),
    )(proj, proj, proj, cum_r, cum_c, dcat)


BIAS_Q = 8


def _rel_onehot(q):
    kj = _iota((REL_PAD, CA_BAND), 1)
    rel = jnp.clip(CA_PAD + q - kj, -128, 128) + 128
    return (_iota((REL_PAD, CA_BAND), 0) == rel).astype(F32)


def _bias_build(rbp):
    def body(rb_ref, o_ref):
        for j in range(BIAS_Q):
            o_ref[j] = _hdot_raw(rb_ref[...], _rel_onehot(pl.program_id(0) * BIAS_Q + j), "nn")

    return pl.pallas_call(
        body, name="ca_bias_build", grid=(CHUNK // BIAS_Q,),
        in_specs=[pl.BlockSpec((8, REL_PAD), lambda q: (0, 0))],
        out_specs=pl.BlockSpec((BIAS_Q, 8, CA_BAND), lambda q: (q, 0, 0)),
        out_shape=jax.ShapeDtypeStruct((CHUNK, 8, CA_BAND), F32),
        compiler_params=_cp(("parallel",)),
    )(rbp)


def _bias_grad(dbias_q):
    def body(db_ref, o_ref):
        q = pl.program_id(0)

        @pl.when(q == 0)
        def _():
            o_ref[...] = jnp.zeros_like(o_ref)

        acc = o_ref[...]
        for j in range(BIAS_Q):
            acc = acc + _hdot_raw(db_ref[j], _rel_onehot(q * BIAS_Q + j), "nt")
        o_ref[...] = acc

    return pl.pallas_call(
        body, name="ca_bias_grad", grid=(CHUNK // BIAS_Q,),
        in_specs=[pl.BlockSpec((BIAS_Q, 8, CA_BAND), lambda q: (q, 0, 0))],
        out_specs=pl.BlockSpec((8, REL_PAD), lambda q: (0, 0)),
        out_shape=jax.ShapeDtypeStruct((8, REL_PAD), F32),
        compiler_params=_cp(("arbitrary",)),
    )(dbias_q)


def _ca_block(c, q, kb, vb, bias2):
    lane = _iota((CHUNK, 128), 1)
    valid = (c * CHUNK - CA_PAD + _iota((CHUNK, CA_BAND), 1)) >= 0
    outs = []
    for e in range(2):
        qm = jnp.where((lane >= 64 * e) & (lane < 64 * (e + 1)), q, 0.0)
        s = bdot(qm, kb, "nt") * ATT_SCALE
        s = jnp.where(valid, s + bias2[e], NEG)
        outs.append(bdot(_softmax_rows(s), vb, "nn"))
    return jnp.where(lane < 64, outs[0], outs[1])


CA_PER_STEP = 4
CA_ROWS = CA_PER_STEP * CHUNK


def _ca_fwd(proj, kvpad, bias):
    def body(q_ref, k_ref, v_ref, b_ref, o_ref):
        for i in range(CA_PER_STEP):
            c = pl.program_id(1) * CA_PER_STEP + i
            band = pl.ds(pl.multiple_of(c * CHUNK, CHUNK), CA_BAND)
            rows = slice(i * CHUNK, (i + 1) * CHUNK)
            o_ref[rows, :] = _ca_block(c, q_ref[rows, :], k_ref[band, :], v_ref[band, :], b_ref[...])

    return pl.pallas_call(
        body, name="ca_fwd", grid=(4, NCHUNK // CA_PER_STEP),
        in_specs=[pl.BlockSpec((CA_ROWS, 128), lambda hp, c: (c, hp)),
                  pl.BlockSpec((T + CA_PAD, 128), lambda hp, c: (0, hp)),
                  pl.BlockSpec((T + CA_PAD, 128), lambda hp, c: (0, 4 + hp)),
                  pl.BlockSpec((2, CHUNK, CA_BAND), lambda hp, c: (hp, 0, 0))],
        out_specs=pl.BlockSpec((CA_ROWS, 128), lambda hp, c: (c, hp)),
        out_shape=jax.ShapeDtypeStruct((T, D), F32),
        compiler_params=_cp(("parallel", "parallel")),
    )(proj, kvpad, kvpad, bias)


def _ca_bwd(proj, kvpad, bias, dcat):
    def body(q_ref, k_ref, v_ref, b_ref, do_ref, dq_ref, dk_ref, dv_ref, db_ref):
        c = pl.program_id(1)

        @pl.when(c == 0)
        def _():
            dk_ref[...] = jnp.zeros_like(dk_ref)
            dv_ref[...] = jnp.zeros_like(dv_ref)
            db_ref[...] = jnp.zeros_like(db_ref)

        for i in range(CA_PER_STEP):
            ci = c * CA_PER_STEP + i
            band = pl.ds(pl.multiple_of(ci * CHUNK, CHUNK), CA_BAND)
            rows = slice(i * CHUNK, (i + 1) * CHUNK)
            fn = functools.partial(_ca_block, ci)
            _, vjp = jax.vjp(fn, q_ref[rows, :], k_ref[band, :], v_ref[band, :], b_ref[...])
            dq, dkb, dvb, db = vjp(do_ref[rows, :])
            dq_ref[rows, :] = dq
            dk_ref[band, :] += dkb
            dv_ref[band, :] += dvb
            db_ref[...] += db

    sds = lambda *s: jax.ShapeDtypeStruct(s, F32)
    padded = lambda: pl.BlockSpec((T + CA_PAD, 128), lambda hp, c: (0, hp))
    return pl.pallas_call(
        body, name="ca_bwd", grid=(4, NCHUNK // CA_PER_STEP),
        in_specs=[pl.BlockSpec((CA_ROWS, 128), lambda hp, c: (c, hp)),
                  pl.BlockSpec((T + CA_PAD, 128), lambda hp, c: (0, hp)),
                  pl.BlockSpec((T + CA_PAD, 128), lambda hp, c: (0, 4 + hp)),
                  pl.BlockSpec((2, CHUNK, CA_BAND), lambda hp, c: (hp, 0, 0)),
                  pl.BlockSpec((CA_ROWS, 128), lambda hp, c: (c, hp))],
        out_specs=(pl.BlockSpec((CA_ROWS, 128), lambda hp, c: (c, hp)), padded(), padded(),
                   pl.BlockSpec((2, CHUNK, CA_BAND), lambda hp, c: (hp, 0, 0))),
        out_shape=(sds(T, 512), sds(T + CA_PAD, 512), sds(T + CA_PAD, 512), sds(8, CHUNK, CA_BAND)),
        compiler_params=_cp(("parallel", "arbitrary")),
    )(proj, kvpad, kvpad, bias, dcat)


def _lru_pre(xs, cw, cb, wa, ba, wx, bx, lam):
    xc = cb + xs[0] * cw[0:1, :] + xs[1] * cw[1:2, :] + xs[2] * cw[2:3, :] + xs[3] * cw[3:4, :]
    ra = _sigmoid(bdot(xc, wa, "nn") + ba)
    ii = _sigmoid(bdot(xc, wx, "nn") + bx)
    la = 8.0 * ra * _log_sigmoid(lam)
    return jnp.exp(la), jnp.sqrt(-_expm1(2.0 * la)) * (ii * xc)


def _lru_pre_specs():
    full = lambda shape: pl.BlockSpec(shape, lambda i: (0,) * len(shape))
    return [pl.BlockSpec((4, ROWS, 512), lambda i: (0, i, 0)), full((4, 512)), full((1, 512)),
            full((512, 512)), full((1, 512)), full((512, 512)), full((1, 512)), full((1, 512))]


def _lru_pre_fwd(xs, cw, cb, wa, ba, wx, bx, lam):
    def body(xs_ref, cw_ref, cb_ref, wa_ref, ba_ref, wx_ref, bx_ref, lam_ref, a_ref, b_ref):
        a, b = _lru_pre(xs_ref[...], cw_ref[...], cb_ref[...], wa_ref[...], ba_ref[...], wx_ref[...], bx_ref[...],
                        lam_ref[...])
        a_ref[...] = a
        b_ref[...] = b

    row = pl.BlockSpec((ROWS, 512), lambda i: (i, 0))
    sds = jax.ShapeDtypeStruct((T, 512), F32)
    return pl.pallas_call(body, name="lru_pre_fwd", grid=(T // ROWS,), in_specs=_lru_pre_specs(),
                          out_specs=(row, row), out_shape=(sds, sds), compiler_params=_cp(("parallel",)),
                          )(xs, cw, cb, wa, ba, wx, bx, lam)


def _lru_pre_bwd(xs, cw, cb, wa, ba, wx, bx, lam, da, db):
    def body(xs_ref, cw_ref, cb_ref, wa_ref, ba_ref, wx_ref, bx_ref, lam_ref, da_ref, db_ref,
             dxs_ref, dcw_ref, dcb_ref, dwa_ref, dba_ref, dwx_ref, dbx_ref, dlam_ref):
        acc = (dcw_ref, dcb_ref, dwa_ref, dba_ref, dwx_ref, dbx_ref, dlam_ref)

        @pl.when(pl.program_id(0) == 0)
        def _():
            for r in acc:
                r[...] = jnp.zeros_like(r)

        _, vjp = jax.vjp(_lru_pre, xs_ref[...], cw_ref[...], cb_ref[...], wa_ref[...], ba_ref[...], wx_ref[...],
                         bx_ref[...], lam_ref[...])
        grads = vjp((da_ref[...], db_ref[...]))
        dxs_ref[...] = grads[0]
        for r, g in zip(acc, grads[1:]):
            r[...] += g

    row = pl.BlockSpec((ROWS, 512), lambda i: (i, 0))
    specs = _lru_pre_specs()
    sds = lambda *s: jax.ShapeDtypeStruct(s, F32)
    return pl.pallas_call(
        body, name="lru_pre_bwd", grid=(T // ROWS,), in_specs=specs + [row, row], out_specs=tuple(specs),
        out_shape=(sds(4, T, 512), sds(4, 512), sds(1, 512), sds(512, 512), sds(1, 512), sds(512, 512), sds(1, 512),
                   sds(1, 512)),
        compiler_params=_cp(("arbitrary",)),
    )(xs, cw, cb, wa, ba, wx, bx, lam, da, db)


def _lru_scan_fwd(a, b):
    def body(a_ref, b_ref, h_ref):
        def step(t, h):
            h = a_ref[pl.ds(t, 1), :] * h + b_ref[pl.ds(t, 1), :]
            h_ref[pl.ds(t, 1), :] = h
            return h

        lax.fori_loop(0, T, step, jnp.zeros((1, 512), F32))

    return pl.pallas_call(body, name="lru_scan_fwd", out_shape=jax.ShapeDtypeStruct((T, 512), F32),
                          compiler_params=pltpu.CompilerParams(vmem_limit_bytes=VMEM_LIMIT))(a, b)


def _lru_scan_bwd(a, h, dh):
    def body(a_ref, h_ref, dh_ref, da_ref, db_ref):
        def step(i, carry):
            t = T - 1 - i
            g = dh_ref[pl.ds(t, 1), :] + carry
            db_ref[pl.ds(t, 1), :] = g
            da_ref[pl.ds(t, 1), :] = g * h_ref[pl.ds(t - 1, 1), :]
            return a_ref[pl.ds(t, 1), :] * g

        carry = lax.fori_loop(0, T - 1, step, jnp.zeros((1, 512), F32))
        db_ref[pl.ds(0, 1), :] = dh_ref[pl.ds(0, 1), :] + carry
        da_ref[pl.ds(0, 1), :] = jnp.zeros((1, 512), F32)

    sds = jax.ShapeDtypeStruct((T, 512), F32)
    return pl.pallas_call(body, name="lru_scan_bwd", out_shape=(sds, sds),
                          compiler_params=pltpu.CompilerParams(vmem_limit_bytes=VMEM_LIMIT))(a, h, dh)


def _lru_post(h, gate):
    return h * _gelu_tanh(gate)


def _lru_post_fwd(h, proj, cat):
    def body(h_ref, g_ref, cat_ref, o_ref):
        o_ref[...] = _lru_post(h_ref[...], g_ref[...])

    row = pl.BlockSpec((ROWS, 512), lambda i: (i, 0))
    return pl.pallas_call(body, name="lru_post_fwd", grid=(T // ROWS,),
                          in_specs=[row, pl.BlockSpec((ROWS, 512), lambda i: (i, 3)), pl.BlockSpec(memory_space=pl.ANY)],
                          out_specs=pl.BlockSpec((ROWS, 512), lambda i: (i, 1)),
                          out_shape=jax.ShapeDtypeStruct((T, D), F32), input_output_aliases={2: 0},
                          compiler_params=_cp(("parallel",)))(h, proj, cat)


def _lru_post_bwd(h, proj, dcat):
    def body(h_ref, g_ref, do_ref, dh_ref, dg_ref):
        _, vjp = jax.vjp(_lru_post, h_ref[...], g_ref[...])
        dh, dg = vjp(do_ref[...])
        dh_ref[...] = dh
        dg_ref[...] = dg

    row = pl.BlockSpec((ROWS, 512), lambda i: (i, 0))
    sds = jax.ShapeDtypeStruct((T, 512), F32)
    return pl.pallas_call(body, name="lru_post_bwd", grid=(T // ROWS,),
                          in_specs=[row, pl.BlockSpec((ROWS, 512), lambda i: (i, 3)),
                                    pl.BlockSpec((ROWS, 512), lambda i: (i, 1))],
                          out_specs=(row, row), out_shape=(sds, sds), compiler_params=_cp(("parallel",)))(h, proj, dcat)


def _conv_dx(dxs_shift):
    def body(d_ref, o_ref):
        o_ref[...] = d_ref[0] + d_ref[1] + d_ref[2] + d_ref[3]

    row = pl.BlockSpec((ROWS, 512), lambda i: (i, 0))
    return pl.pallas_call(body, name="lru_conv_dx", grid=(T // ROWS,),
                          in_specs=[pl.BlockSpec((4, ROWS, 512), lambda i: (0, i, 0))], out_specs=row,
                          out_shape=jax.ShapeDtypeStruct((T, 512), F32), compiler_params=_cp(("parallel",)))(dxs_shift)


def _position():
    return lax.axis_index("x"), lax.axis_index("y"), lax.axis_index("c")


def _other_chips(x, y):
    return [(1 - x, y), (x, 1 - y), (1 - x, 1 - y)]


def _al(v, n):
    return v * n if isinstance(v, int) else pl.multiple_of(v * n, n)


_AG_ITEMS = [
    ((4, 32, 128), lambda o, s, h: o.at[s, pl.ds(_al(h, 16), 16), :], lambda r, h: r.at[pl.ds(_al(h, 16), 16), :]),
    ((4, 774, 1024), lambda o, s, h: o.at[s, :, pl.ds(_al(h, 512), 512)], lambda r, h: r.at[:, pl.ds(_al(h, 512), 512)]),
    ((1024, 1024), lambda o, s, h: o.at[pl.ds(_al(2 * s + h, 128), 128), :], lambda r, h: r.at[pl.ds(_al(h, 128), 128), :]),
    ((2, 1024, 4096), lambda o, s, h: o.at[h, :, pl.ds(_al(s, 1024), 1024)], lambda r, h: r.at[h]),
    ((2, 4096, 1024), lambda o, s, h: o.at[h, pl.ds(_al(s, 1024), 1024), :], lambda r, h: r.at[h]),
    ((1024, 2560), lambda o, s, h: o.at[pl.ds(_al(h, 512), 512), pl.ds(_al(s, 640), 640)],
     lambda r, h: r.at[pl.ds(_al(h, 512), 512), :]),
    ((1024, 1024), lambda o, s, h: o.at[pl.ds(_al(2 * s + h, 128), 128), :], lambda r, h: r.at[pl.ds(_al(h, 128), 128), :]),
]


_AG_GROUPS = [(0, 1, 2), (3, 4), (5, 6)]

_HBM = pl.BlockSpec(memory_space=pltpu.HBM)
_SEM = pl.BlockSpec(memory_space=pltpu.SEMAPHORE)
_SPLIT = dict(has_side_effects=pltpu.SideEffectType.DATAFLOW_SIDE_EFFECTING)


def _hbm(a):
    return pltpu.with_memory_space_constraint(a, pltpu.HBM)


def _ag_ici_copy(i, j, chip, c, slot, src_ref, land_ref, send_sems, recv_sems, k):
    _, dst, half = _AG_ITEMS[i]
    return pltpu.make_async_remote_copy(src_ref=half(src_ref, c), dst_ref=dst(land_ref, slot, c), send_sem=send_sems.at[k],
                                        recv_sem=recv_sems.at[k], device_id=(*chip, c), device_id_type=MESH)


def _ag_start(shards):
    n = len(_AG_ITEMS)
    ng = len(_AG_GROUPS)
    lands = [lax.empty(shape, s.dtype) for (shape, _, _), s in zip(_AG_ITEMS, shards)]

    def body(*refs):
        srcs, land_refs = refs[:n], refs[n:2 * n]
        sems = refs[2 * n:2 * n + 2 * ng]
        token = refs[-1]
        x, y, c = _position()
        me = 2 * x + y
        for g, items in enumerate(_AG_GROUPS):
            for t, i in enumerate(items):
                for j, chip in enumerate(_other_chips(x, y)):
                    _ag_ici_copy(i, j, chip, c, me, srcs[i], land_refs[i], sems[2 * g], sems[2 * g + 1], 3 * t + j).start()
        token[...] = jnp.zeros_like(token)

    sem_shapes = []
    for items in _AG_GROUPS:
        sem_shapes += [pltpu.SemaphoreType.DMA((3 * len(items),))] * 2
    thru = [pltpu.HBM(a.shape, a.dtype) for a in list(shards) + lands]
    out = pl.pallas_call(
        body, name="allgather_start",
        out_shape=tuple(sem_shapes) + tuple(thru) + (jax.ShapeDtypeStruct((8, 128), F32),),
        in_specs=(_HBM,) * (2 * n),
        out_specs=(_SEM,) * (2 * ng) + (_HBM,) * (2 * n) + (pl.BlockSpec(memory_space=pltpu.VMEM),),
        input_output_aliases={i: 2 * ng + i for i in range(2 * n)},
        compiler_params=pltpu.CompilerParams(**_SPLIT),
    )(*[_hbm(a) for a in list(shards) + lands])
    sems, thru, token = out[:2 * ng], out[2 * ng:-1], out[-1]
    return [(sems[2 * g], sems[2 * g + 1]) for g in range(ng)], list(thru[:n]), list(thru[n:]), token


def _ag_wait(g, sems, srcs, lands, after):
    items = _AG_GROUPS[g]
    m = len(items)

    def body(*refs):
        src_refs, land_refs = refs[:m], refs[m:2 * m]
        send_sems, recv_sems = refs[2 * m], refs[2 * m + 1]
        x, y, c = _position()
        for t, i in enumerate(items):
            for j, chip in enumerate(_other_chips(x, y)):
                cp = _ag_ici_copy(i, j, chip, c, 2 * chip[0] + chip[1], src_refs[t], land_refs[t], send_sems, recv_sems,
                                  3 * t + j)
                cp.wait_send()
                cp.wait_recv()

    ops = [srcs[i] for i in items] + [lands[i] for i in items]
    out = pl.pallas_call(
        body, name=f"allgather_wait_{g}",
        out_shape=tuple(pltpu.HBM(a.shape, a.dtype) for a in ops),
        in_specs=(_HBM,) * (2 * m) + (_SEM, _SEM, pl.BlockSpec(memory_space=pl.ANY)),
        out_specs=(_HBM,) * (2 * m),
        input_output_aliases={i: i for i in range(2 * m)},
        compiler_params=pltpu.CompilerParams(**_SPLIT),
    )(*ops, sems[0], sems[1], after)
    return list(out[:m]), list(out[m:])


def _ag_forward(g, srcs, lands):
    items = _AG_GROUPS[g]
    m = len(items)

    def body(*refs):
        src_refs, in_refs, out_refs = refs[:m], refs[m:2 * m], refs[2 * m:3 * m]
        send_sems, recv_sems = refs[3 * m:]
        x, y, c = _position()
        sibling = (x, y, 1 - c)
        me = 2 * x + y
        chips = _other_chips(x, y)
        sends = []
        for t, i in enumerate(items):
            _, dst, half = _AG_ITEMS[i]
            for j, chip in enumerate(chips):
                slot = 2 * chip[0] + chip[1]
                sends.append(pltpu.make_async_remote_copy(
                    src_ref=dst(in_refs[t], slot, c), dst_ref=dst(out_refs[t], slot, c), send_sem=send_sems.at[5 * t + j],
                    recv_sem=recv_sems.at[5 * t + j], device_id=sibling, device_id_type=MESH))
            for hc in range(2):
                sends.append(pltpu.make_async_remote_copy(
                    src_ref=half(src_refs[t], hc), dst_ref=dst(out_refs[t], me, hc), send_sem=send_sems.at[5 * t + 3 + hc],
                    recv_sem=recv_sems.at[5 * t + 3 + hc], device_id=sibling, device_id_type=MESH))
        for cp in sends:
            cp.start()
        for t, i in enumerate(items):
            _, dst, half = _AG_ITEMS[i]
            for j, chip in enumerate(chips):
                there = dst(out_refs[t], 2 * chip[0] + chip[1], 1 - c)
                pltpu.make_async_remote_copy(src_ref=there, dst_ref=there, send_sem=send_sems.at[5 * t + j],
                                             recv_sem=recv_sems.at[5 * t + j], device_id=sibling,
                                             device_id_type=MESH).wait_recv()
            for hc in range(2):
                there = dst(out_refs[t], me, hc)
                pltpu.make_async_remote_copy(src_ref=there, dst_ref=there, send_sem=send_sems.at[5 * t + 3 + hc],
                                             recv_sem=recv_sems.at[5 * t + 3 + hc], device_id=sibling,
                                             device_id_type=MESH).wait_recv()
        for cp in sends:
            cp.wait_send()

    any_spec = pl.BlockSpec(memory_space=pl.ANY)
    return pl.pallas_call(
        body, name=f"allgather_forward_{g}",
        in_specs=[any_spec] * (2 * m), out_specs=(any_spec,) * m,
        out_shape=tuple(jax.ShapeDtypeStruct(a.shape, a.dtype) for a in lands),
        input_output_aliases={m + t: t for t in range(m)},
        scratch_shapes=[pltpu.SemaphoreType.DMA((5 * m,)), pltpu.SemaphoreType.DMA((5 * m,))],
    )(*srcs, *lands)


def _pair_swap_cols(gb, tag):
    _, rows, cols = gb.shape
    hc = cols // 2

    def body(g_ref, out_ref, send_sem, recv_sem):
        x, y, c = _position()
        cp = pltpu.make_async_remote_copy(src_ref=g_ref.at[:, :, pl.ds(_al(1 - c, hc), hc)], dst_ref=out_ref,
                                          send_sem=send_sem, recv_sem=recv_sem, device_id=(x, y, 1 - c),
                                          device_id_type=MESH)
        cp.start()
        cp.wait()

    return pl.pallas_call(
        body, name="grad_pair_swap_" + tag,
        in_specs=[pl.BlockSpec(memory_space=pl.ANY)], out_specs=pl.BlockSpec(memory_space=pl.ANY),
        out_shape=jax.ShapeDtypeStruct((4, rows, hc), gb.dtype),
        scratch_shapes=[pltpu.SemaphoreType.DMA, pltpu.SemaphoreType.DMA],
    )(gb)


def _handover(halves, tag):
    def body(in_ref, out_ref, send_sem, recv_sem):
        x, y, c = _position()
        cp = pltpu.make_async_remote_copy(src_ref=in_ref.at[c], dst_ref=out_ref.at[c], send_sem=send_sem,
                                          recv_sem=recv_sem, device_id=(x, y, 1 - c), device_id_type=MESH)
        cp.start()
        theirs = out_ref.at[1 - c]
        pltpu.make_async_remote_copy(src_ref=theirs, dst_ref=theirs, send_sem=send_sem, recv_sem=recv_sem,
                                     device_id=(x, y, c), device_id_type=MESH).wait_recv()
        cp.wait_send()

    return pl.pallas_call(
        body, name="grad_handover_" + tag,
        in_specs=[pl.BlockSpec(memory_space=pl.ANY)], out_specs=pl.BlockSpec(memory_space=pl.ANY),
        out_shape=jax.ShapeDtypeStruct(halves.shape, halves.dtype), input_output_aliases={0: 0},
        scratch_shapes=[pltpu.SemaphoreType.DMA, pltpu.SemaphoreType.DMA],
    )(halves)


def _a2a_copy(j, chip, c, p_ref, q_ref, q_slot, send_sems, recv_sems):
    return pltpu.make_async_remote_copy(src_ref=p_ref.at[2 * chip[0] + chip[1]], dst_ref=q_ref.at[q_slot],
                                        send_sem=send_sems.at[j], recv_sem=recv_sems.at[j], device_id=(*chip, c),
                                        device_id_type=MESH)


def _a2a_start(p, tag):
    def body(p_ref, q_ref, send_sems, recv_sems, p_thru, q_thru, token):
        x, y, c = _position()
        for j, chip in enumerate(_other_chips(x, y)):
            _a2a_copy(j, chip, c, p_ref, q_ref, 2 * x + y, send_sems, recv_sems).start()
        token[...] = jnp.zeros_like(token)

    return pl.pallas_call(
        body, name="grad_alltoall_start_" + tag,
        out_shape=(pltpu.SemaphoreType.DMA((3,)), pltpu.SemaphoreType.DMA((3,)), pltpu.HBM(p.shape, p.dtype),
                   pltpu.HBM(p.shape, p.dtype), jax.ShapeDtypeStruct((8, 128), F32)),
        in_specs=(_HBM, _HBM), out_specs=(_SEM, _SEM, _HBM, _HBM, pl.BlockSpec(memory_space=pltpu.VMEM)),
        input_output_aliases={0: 2, 1: 3},
        compiler_params=pltpu.CompilerParams(**_SPLIT),
    )(_hbm(p), _hbm(lax.empty(p.shape, p.dtype)))


def _a2a_wait(send_sems, recv_sems, p, q, after, tag):
    def body(p_ref, q_ref, send_sems, recv_sems, after_ref, p_out, q_out):
        x, y, c = _position()
        for j, chip in enumerate(_other_chips(x, y)):
            cp = _a2a_copy(j, chip, c, p_ref, q_ref, 2 * chip[0] + chip[1], send_sems, recv_sems)
            cp.wait_send()
            cp.wait_recv()

    return pl.pallas_call(
        body, name="grad_alltoall_wait_" + tag,
        out_shape=(pltpu.HBM(p.shape, p.dtype), pltpu.HBM(q.shape, q.dtype)),
        in_specs=(_HBM, _HBM, _SEM, _SEM, pl.BlockSpec(memory_space=pl.ANY)), out_specs=(_HBM, _HBM),
        input_output_aliases={0: 0, 1: 1},
        compiler_params=pltpu.CompilerParams(**_SPLIT),
    )(p, q, send_sems, recv_sems, after)


def _comm_rows(rows):
    return next(t for t in (512, 384, 256, 128) if rows % t == 0)


def _pair_add(gb, recv, where, tag):
    _, rows, cols = gb.shape
    hc = cols // 2
    tr = _comm_rows(rows)

    def body(w_ref, g_ref, r_ref, o_ref):
        o_ref[...] = (g_ref[...].astype(F32) + r_ref[...].astype(F32)).astype(o_ref.dtype)

    return pl.pallas_call(
        body, name="grad_pair_add_" + tag,
        grid_spec=pltpu.PrefetchScalarGridSpec(
            num_scalar_prefetch=1, grid=(4, rows // tr),
            in_specs=[pl.BlockSpec((None, tr, hc), lambda s, j, w_ref: (s, j, w_ref[0])),
                      pl.BlockSpec((None, tr, hc), lambda s, j, w_ref: (s, j, 0))],
            out_specs=pl.BlockSpec((None, tr, hc), lambda s, j, w_ref: (s, j, 0))),
        out_shape=jax.ShapeDtypeStruct((4, rows, hc), gb.dtype),
        compiler_params=_cp(("parallel", "parallel")),
    )(where, gb, recv)


def _sum_chips(p, q, where, tag):
    _, rows, hc = q.shape
    tr = _comm_rows(rows)

    def body(w_ref, p_ref, qa_ref, qb_ref, qc_ref, o_ref):
        me = w_ref[1]
        own, qa, qb, qc = (r[...].astype(F32) for r in (p_ref, qa_ref, qb_ref, qc_ref))
        v0 = jnp.where(me == 0, own, qa)
        v1 = jnp.where(me == 1, own, jnp.where(me == 0, qa, qb))
        v2 = jnp.where(me == 2, own, jnp.where(me < 2, qb, qc))
        v3 = jnp.where(me == 3, own, qc)
        o_ref[...] = ((v0 + v1) + v2) + v3

    slot = lambda k: pl.BlockSpec((None, tr, hc), lambda j, w_ref: (w_ref[k], j, 0))
    return pl.pallas_call(
        body, name="grad_sum_chips_" + tag,
        grid_spec=pltpu.PrefetchScalarGridSpec(
            num_scalar_prefetch=1, grid=(rows // tr,),
            in_specs=[slot(1), slot(2), slot(3), slot(4)],
            out_specs=pl.BlockSpec((None, tr, hc), lambda j, w_ref: (w_ref[0], j, 0))),
        out_shape=jax.ShapeDtypeStruct((2, rows, hc), F32),
        compiler_params=_cp(("parallel",)),
    )(where, p, q, q, q)


def _shard_major(g, axis):
    shape = g.shape
    g = g.reshape(shape[:axis] + (4, shape[axis] // 4) + shape[axis + 1:])
    return jnp.moveaxis(g, axis, 0).reshape(4, -1)


def _unshard(g4, shape, axis):
    n = shape[axis] // 4
    g = g4.reshape((4,) + shape[:axis] + (n,) + shape[axis + 1:])
    return jnp.moveaxis(g, 0, axis).reshape(shape)


def _split(flat, shapes):
    out, off = [], 0
    for shp in shapes:
        n = 1
        for d in shp:
            n *= d
        out.append(flat[..., off:off + n].reshape(flat.shape[:-1] + tuple(shp)))
        off += n
    return out


def _even_rows_to_kernel(wt):
    return jnp.concatenate([wt[:1536], wt[1552:3088], wt[1536:1552], wt[3088:3096],
                            jnp.zeros((PE - 3096, wt.shape[1]), wt.dtype)], axis=0)


def _block_diag(w):
    eye = jnp.eye(8, dtype=w.dtype)
    return (w[:, :, None, :] * eye[:, None, :, None]).reshape(512, 512)


def _diag_blocks(g):
    eye = jnp.eye(8, dtype=g.dtype)
    return (g.reshape(8, 64, 8, 64) * eye[:, None, :, None]).sum(axis=2)


def _shift_down(a, s):
    return a if s == 0 else jnp.pad(a, ((s, 0), (0, 0)))[:a.shape[0]]


def _shift_up(a, s):
    return a if s == 0 else jnp.pad(a, ((0, s), (0, 0)))[s:]


SMALL_SHARDED_SHAPES = [(2, 4, 256), (16, 64), (4, 128), (128,), (128,), (128,), (128,)]
REPL_SHAPES = [(256,), (512,), (8,), (8, 257), (8, 64, 64), (8, 64, 64)]


def kernel(x, norm_w, w_in_even, gla_w_a_up, gla_b_a, gla_norm_w, fox_b_f, w_out_even, w_in_odd, rel_bias, conv_w, conv_b, lru_w_a, lru_b_a, lru_w_x, lru_b_x, lru_lambda, w_out_odd, w_mlp_up, w_mlp_down, loss_target, m_norm_w, m_w_in_even, m_gla_w_a_up, m_gla_b_a, m_gla_norm_w, m_fox_b_f, m_w_out_even, m_w_in_odd, m_rel_bias, m_conv_w, m_conv_b, m_lru_w_a, m_lru_b_a, m_lru_w_x, m_lru_b_x, m_lru_lambda, m_w_out_odd, m_w_mlp_up, m_w_mlp_down, v_norm_w, v_w_in_even, v_gla_w_a_up, v_gla_b_a, v_gla_norm_w, v_fox_b_f, v_w_out_even, v_w_in_odd, v_rel_bias, v_conv_w, v_conv_b, v_lru_w_a, v_lru_b_a, v_lru_w_x, v_lru_b_x, v_lru_lambda, v_w_out_odd, v_w_mlp_up, v_w_mlp_down):
    c_idx = lax.axis_index("c")

    small_local = [norm_w, gla_w_a_up[0], conv_w[0], conv_b[0], lru_b_a[0], lru_b_x[0], lru_lambda[0]]
    small_src = jnp.concatenate([a.reshape(-1) for a in small_local]).reshape(32, 128)
    mine = [small_src, w_in_even[0].T.astype(BF16), w_out_even[0].astype(BF16), w_mlp_up.astype(BF16),
            w_mlp_down.astype(BF16), w_in_odd[0].astype(BF16), w_out_odd[0].astype(BF16)]
    ag_sems, ag_srcs, ag_lands, ag_token = _ag_start(mine)

    def gathered(g, after):
        srcs_g, lands_g = _ag_wait(g, ag_sems[g], ag_srcs, ag_lands, after)
        return _ag_forward(g, srcs_g, lands_g)

    small4, w_in_e4, w_out_e = gathered(0, ag_token)
    me = 2 * lax.axis_index("x") + lax.axis_index("y")
    others = [k + (k >= me).astype(jnp.int32) for k in range(3)]
    where = jnp.stack([c_idx, me] + others).astype(jnp.int32)

    w_in_e_t = _even_rows_to_kernel(w_in_e4.reshape(3096, D))
    g_small = _split(small4.reshape(4, 32 * 128), SMALL_SHARDED_SHAPES)
    nw_full = _unshard(g_small[0], (2, 4, 1024), 2)
    wa_up = _unshard(g_small[1], (16, 256), 1)
    cw = _unshard(g_small[2], (4, 512), 1)
    cb, lba, lbx, lam = [_unshard(g, (512,), 0).reshape(1, 512) for g in g_small[3:]]
    nw = lambda layer, i: nw_full[layer, i].reshape(1, D)

    wa_pad = jnp.pad(wa_up, ((0, 128 - 16), (0, 0)))
    gla_ba = gla_b_a.reshape(1, 256)
    gla_nw = gla_norm_w.reshape(1, 512)
    fox_bpad = jnp.pad(fox_b_f.reshape(1, 8), ((0, 0), (FOX_LANE0, 128 - FOX_LANE0 - 8)))
    rbp = jnp.pad(rel_bias[0], ((0, 0), (0, REL_PAD - 257)))
    wa_bd = _block_diag(lru_w_a[0])
    wx_bd = _block_diag(lru_w_x[0])

    x0 = x[0]
    tgt = loss_target[0]

    h0 = _prenorm(x0, nw(0, 0), "prenorm_l0_mix")
    proj_e = _mm(h0, w_in_e_t, "nt", tm=1024, tn=640, name="mm_in_even")
    cat0, s_prev = _gla_fwd(proj_e, wa_pad, gla_ba, gla_nw)
    cum_r = _fox_gate_fwd(proj_e, fox_bpad)
    cum_c = cum_r[:, FOX_LANE0:FOX_LANE0 + 8].T
    cat0 = _fox_fwd(proj_e, cum_r, cum_c, cat0)
    mix0 = _mm(cat0, w_out_e, "nn", tm=1024, tn=512, name="mm_out_even")
    x1 = _postnorm(x0, mix0, nw(0, 1), "postnorm_l0_mix")
    w_up, w_dn = gathered(1, x1)
    h1 = _prenorm(x1, nw(0, 2), "prenorm_l0_mlp")
    a0, r0 = _mm(h1, w_up, "nn", tm=1024, tn=1024, b_layer=0, relu_pair=True, name="mm_up_l0")
    d0 = _mm(a0, w_dn, "nn", tm=1024, tn=512, tk=2048, b_layer=0, name="mm_down_l0")
    x2 = _postnorm(x1, d0, nw(0, 3), "postnorm_l0_mlp")

    w_in_o, w_out_o = gathered(2, x2)
    h2 = _prenorm(x2, nw(1, 0), "prenorm_l1_mix")
    proj_o = _mm(h2, w_in_o, "nn", tm=1024, tn=640, name="mm_in_odd")
    bias_q = _bias_build(rbp)
    bias = bias_q.transpose(1, 0, 2)
    kvpad = jnp.pad(proj_o[:, 512:1536], ((CA_PAD, 0), (0, 0)))
    cat1 = _ca_fwd(proj_o, kvpad, bias)
    x_in = proj_o[:, 2048:2560]
    xs = jnp.stack([_shift_down(x_in, 3 - j) for j in range(4)])
    lru_a, lru_b = _lru_pre_fwd(xs, cw, cb, wa_bd, lba, wx_bd, lbx, lam)
    hh = _lru_scan_fwd(lru_a, lru_b)
    cat1 = _lru_post_fwd(hh, proj_o, cat1)
    mix1 = _mm(cat1, w_out_o, "nn", tm=1024, tn=512, name="mm_out_odd")
    x3 = _postnorm(x2, mix1, nw(1, 1), "postnorm_l1_mix")
    h3 = _prenorm(x3, nw(1, 2), "prenorm_l1_mlp")
    a1, r1 = _mm(h3, w_up, "nn", tm=1024, tn=1024, b_layer=1, relu_pair=True, name="mm_up_l1")
    d1 = _mm(a1, w_dn, "nn", tm=1024, tn=512, tk=2048, b_layer=1, name="mm_down_l1")
    x4 = _postnorm(x3, d1, nw(1, 3), "postnorm_l1_mlp")

    g4, loss_part = _loss_and_grad(x4, tgt)
    loss = lax.psum(loss_part[0, 0], ("x", "y", "c"))

    def rs_begin(gb, tag):
        pair_sum = _pair_add(gb, _pair_swap_cols(gb, tag), where, tag)
        return _a2a_start(pair_sum, tag)

    def rs_end(started, after, tag):
        send_sems, recv_sems, p, q, _ = started
        p, q = _a2a_wait(send_sems, recv_sems, p, q, after, tag)
        halves = _handover(_sum_chips(p, q, where, tag), tag)
        return jnp.concatenate([halves[0], halves[1]], axis=1)

    gba = jnp.zeros((4, GA_ROWS, D), BF16)
    dd1, dnw13 = _norm_bwd(d1, nw(1, 3), g4, None, "postnorm_l1_mlp_bwd")
    gba = _mm(a1, dd1, "tn", tm=512, tn=1024, into=(gba, 1024, GA_DN), name="mm_down_l1_dw")
    du1 = _mm(dd1, w_dn, "nt", tm=1024, tn=1024, b_layer=1, times2=r1, out_dtype=BF16, name="mm_down_l1_dx")
    gba = _mm(du1, h3, "tn", tm=512, tn=1024, into=(gba, 1024, GA_UP), name="mm_up_l1_dw")
    dh3 = _mm(du1, w_up, "nt", tm=1024, tn=512, tk=2048, b_layer=1, name="mm_up_l1_dx")
    g3, dnw12 = _norm_bwd(x3, nw(1, 2), dh3, g4, "prenorm_l1_mlp_bwd")
    dmix1, dnw11 = _norm_bwd(mix1, nw(1, 1), g3, None, "postnorm_l1_mix_bwd")
    gba = _mm(cat1, dmix1, "tn", tm=128, tn=1024, into=(gba, 256, GA_OUT_O), name="mm_out_odd_dw")
    dcat1 = _mm(dmix1, w_out_o, "nt", tm=1024, tn=512, name="mm_out_odd_dx")

    dq_c, dkpad, dvpad, dbias = _ca_bwd(proj_o, kvpad, bias, dcat1)
    g_rel = _bias_grad(dbias.transpose(1, 0, 2))[:, :257]
    dhh, dgate = _lru_post_bwd(hh, proj_o, dcat1)
    da_l, db_l = _lru_scan_bwd(lru_a, hh, dhh)
    dxs, g_cw, g_cb, g_wa_bd, g_lba, g_wx_bd, g_lbx, g_lam = _lru_pre_bwd(xs, cw, cb, wa_bd, lba, wx_bd, lbx, lam, da_l, db_l)
    dx_in = _conv_dx(jnp.stack([_shift_up(dxs[j], 3 - j) for j in range(4)]))
    dproj_o = jnp.concatenate([dq_c, dkpad[CA_PAD:], dvpad[CA_PAD:], dgate, dx_in], axis=1)
    gba = _mm(dproj_o, h2, "tn", tm=128, tn=1024, into=(gba, 640, GA_IN_O), name="mm_in_odd_dw")
    rs_a = rs_begin(gba, "a")
    dh2 = _mm(dproj_o, w_in_o, "nt", tm=1024, tn=512, tk=1280, name="mm_in_odd_dx")
    g2, dnw10 = _norm_bwd(x2, nw(1, 0) + rs_a[4][0, 0], dh2, g3, "prenorm_l1_mix_bwd")

    gbb = lax.empty((4, GB_ROWS, D), BF16)
    dd0, dnw03 = _norm_bwd(d0, nw(0, 3), g2, None, "postnorm_l0_mlp_bwd")
    gbb = _mm(a0, dd0, "tn", tm=512, tn=1024, into=(gbb, 1024, GB_DN), name="mm_down_l0_dw")
    du0 = _mm(dd0, w_dn, "nt", tm=1024, tn=1024, b_layer=0, times2=r0, out_dtype=BF16, name="mm_down_l0_dx")
    gbb = _mm(du0, h1, "tn", tm=512, tn=1024, into=(gbb, 1024, GB_UP), name="mm_up_l0_dw")
    rs_b = rs_begin(gbb, "b")
    dh1 = _mm(du0, w_up, "nt", tm=1024, tn=512, tk=2048, b_layer=0, name="mm_up_l0_dx")
    g1, dnw02 = _norm_bwd(x1, nw(0, 2) + rs_b[4][0, 0], dh1, g2, "prenorm_l0_mlp_bwd")
    dmix0, dnw01 = _norm_bwd(mix0, nw(0, 1), g1, None, "postnorm_l0_mix_bwd")
    gbc = lax.empty((4, GC_ROWS, D), BF16)
    gbc = _mm(cat0, dmix0, "tn", tm=128, tn=1024, into=(gbc, 256, GC_OUT_E), name="mm_out_even_dw")
    dcat0 = _mm(dmix0, w_out_e, "nt", tm=1024, tn=512, name="mm_out_even_dx")

    dq_g, dk_g, dv_g, dr_g, daux_g, g_wa_pad, g_gla_ba, g_gla_nw = _gla_bwd(proj_e, s_prev, wa_pad, gla_ba, gla_nw, dcat0)
    dq_f, dk_f, dv_f, dcrow, dccol = _fox_bwd(proj_e, cum_r, cum_c, dcat0)
    dccol_t = jnp.pad(dccol.sum(axis=0).T, ((0, 0), (FOX_LANE0, 128 - FOX_LANE0 - 8)))
    daux, g_fox_bpad = _fox_gate_bwd(proj_e, fox_bpad, dcrow, dccol_t, daux_g)
    dproj_e = jnp.concatenate([dq_g, dk_g, dv_g, dr_g, dq_f, dk_f, dv_f, daux], axis=1)
    gt_in_e = _mm(dproj_e, h0, "tn", tm=640, tn=1024, out_dtype=BF16, name="mm_in_even_dw")
    dh0 = _mm(dproj_e, w_in_e_t, "nn", tm=1024, tn=512, tk=640, name="mm_in_even_dx")
    grad_x, dnw00 = _norm_bwd(x0, nw(0, 0), dh0, g1, "prenorm_l0_mix_bwd")

    g_norm = jnp.stack([jnp.concatenate([dnw00, dnw01, dnw02, dnw03]), jnp.concatenate([dnw10, dnw11, dnw12, dnw13])])
    sharded = [(g_norm, 2), (g_wa_pad[:16], 1), (g_cw, 1), (g_cb[0], 0), (g_lba[0], 0), (g_lbx[0], 0), (g_lam[0], 0)]
    replicated = [g_gla_ba[0], g_gla_nw[0], g_fox_bpad[0, FOX_LANE0:FOX_LANE0 + 8], g_rel, _diag_blocks(g_wa_bd),
                  _diag_blocks(g_wx_bd)]
    small4 = jnp.concatenate([_shard_major(g, ax) for g, ax in sharded]
                             + [jnp.broadcast_to(g.reshape(1, -1), (4, g.size)) for g in replicated], axis=1)
    n_small = small4.shape[1]
    small_rows = GC_ROWS - GC_TAIL - 774
    small4 = jnp.pad(small4, ((0, 0), (0, small_rows * D - n_small))).reshape(4, small_rows, D)
    gt_rows = jnp.concatenate([gt_in_e[:1536], gt_in_e[3072:3088], gt_in_e[1536:3072], gt_in_e[3088:3096]], axis=0)
    tail = jnp.concatenate([gt_rows.reshape(4, 774, D), small4.astype(BF16)], axis=1)
    gbc = lax.dynamic_update_slice(gbc, tail, (0, GC_TAIL, 0))
    rs_c = rs_begin(gbc, "c")

    red_a = rs_end(rs_a, rs_c[4], "a")
    red_b = rs_end(rs_b, red_a, "b")
    red_c = rs_end(rs_c, red_b, "c")

    g_up = jnp.stack([red_b[GB_UP:GB_UP + 1024].T, red_a[GA_UP:GA_UP + 1024].T])
    g_dn = jnp.stack([red_b[GB_DN:GB_DN + 1024], red_a[GA_DN:GA_DN + 1024]])
    g_small = _split(red_c[GC_TAIL + 774:].reshape(-1)[:n_small], SMALL_SHARDED_SHAPES + REPL_SHAPES)
    g_of = dict(zip(["norm_w", "gla_w_a_up", "conv_w", "conv_b", "lru_b_a", "lru_b_x", "lru_lambda", "gla_b_a",
                     "gla_norm_w", "fox_b_f", "rel_bias", "lru_w_a", "lru_w_x"], g_small))
    g_of.update(w_mlp_up=g_up, w_mlp_down=g_dn, w_in_odd=red_a[GA_IN_O:GA_IN_O + 640].T,
                w_out_even=red_c[GC_OUT_E:GC_OUT_E + 256], w_out_odd=red_a[GA_OUT_O:GA_OUT_O + 256],
                w_in_even=red_c[GC_TAIL:GC_TAIL + 774])

    names = ["norm_w", "w_in_even", "gla_w_a_up", "gla_b_a", "gla_norm_w", "fox_b_f", "w_out_even", "w_in_odd", "rel_bias",
             "conv_w", "conv_b", "lru_w_a", "lru_b_a", "lru_w_x", "lru_b_x", "lru_lambda", "w_out_odd", "w_mlp_up",
             "w_mlp_down"]
    w_of = dict(norm_w=norm_w, w_in_even=w_in_even, gla_w_a_up=gla_w_a_up, gla_b_a=gla_b_a, gla_norm_w=gla_norm_w,
                fox_b_f=fox_b_f, w_out_even=w_out_even, w_in_odd=w_in_odd, rel_bias=rel_bias, conv_w=conv_w, conv_b=conv_b,
                lru_w_a=lru_w_a, lru_b_a=lru_b_a, lru_w_x=lru_w_x, lru_b_x=lru_b_x, lru_lambda=lru_lambda,
                w_out_odd=w_out_odd, w_mlp_up=w_mlp_up, w_mlp_down=w_mlp_down)
    m_of = dict(norm_w=m_norm_w, w_in_even=m_w_in_even, gla_w_a_up=m_gla_w_a_up, gla_b_a=m_gla_b_a,
                gla_norm_w=m_gla_norm_w, fox_b_f=m_fox_b_f, w_out_even=m_w_out_even, w_in_odd=m_w_in_odd,
                rel_bias=m_rel_bias, conv_w=m_conv_w, conv_b=m_conv_b, lru_w_a=m_lru_w_a, lru_b_a=m_lru_b_a,
                lru_w_x=m_lru_w_x, lru_b_x=m_lru_b_x, lru_lambda=m_lru_lambda, w_out_odd=m_w_out_odd,
                w_mlp_up=m_w_mlp_up, w_mlp_down=m_w_mlp_down)
    v_of = dict(norm_w=v_norm_w, w_in_even=v_w_in_even, gla_w_a_up=v_gla_w_a_up, gla_b_a=v_gla_b_a,
                gla_norm_w=v_gla_norm_w, fox_b_f=v_fox_b_f, w_out_even=v_w_out_even, w_in_odd=v_w_in_odd,
                rel_bias=v_rel_bias, conv_w=v_conv_w, conv_b=v_conv_b, lru_w_a=v_lru_w_a, lru_b_a=v_lru_b_a,
                lru_w_x=v_lru_w_x, lru_b_x=v_lru_b_x, lru_lambda=v_lru_lambda, w_out_odd=v_w_out_odd,
                w_mlp_up=v_w_mlp_up, w_mlp_down=v_w_mlp_down)
    grads, deltas, new_ms, new_vs = [], [], [], []
    for n in names:
        w = w_of[n]
        if n == "w_in_even":
            to_view = lambda a: a[0].T
            from_view = lambda a: a.T[None]
        else:
            view = w.shape if w.ndim <= 3 else w.shape[-3:]
            to_view = lambda a, view=view: a.reshape(view)
            from_view = lambda a, w=w: a.reshape(w.shape)
        g = g_of[n] if n == "w_in_even" else to_view(g_of[n])
        d, mn, vn = _adamw(to_view(w), g, to_view(m_of[n]), to_view(v_of[n]), "adamw_" + n)
        grads.append(from_view(g))
        deltas.append(from_view(d))
        new_ms.append(from_view(mn))
        new_vs.append(from_view(vn))

    return (loss, grad_x.reshape(1, T, D), *grads, *deltas, *new_ms, *new_vs)
```

```python
import functools

import jax
import jax.numpy as jnp
from jax import lax
from jax.experimental import pallas as pl
from jax.experimental.pallas import tpu as pltpu

F32 = jnp.float32
BF16 = jnp.bfloat16
MESH = pl.DeviceIdType.MESH

T = 2048
D = 1024
DFF = 4096
EPS = 1e-6
CHUNK = 64
NCHUNK = T // CHUNK
PE = 3200
PO = 2560
AUX_BLK = 3072 // 128
FOX_LANE0 = 16
GLA_SCALE = 64 ** -0.5
ATT_SCALE = 64 ** -0.5
NEG = float(jnp.finfo(jnp.float32).min)
CA_BAND = 576
CA_PAD = 512
REL_PAD = 384

VMEM_LIMIT = 48 * 1024 * 1024

ADAM_LR, ADAM_B1, ADAM_B2, ADAM_EPS, ADAM_WD, ADAM_STEP = 0.001, 0.9, 0.999, 1e-08, 0.01, 10

GA_ROWS, GA_UP, GA_DN, GA_IN_O, GA_OUT_O = 3072, 0, 1024, 2048, 2688
GB_ROWS, GB_UP, GB_DN = 2048, 0, 1024
GC_ROWS, GC_OUT_E, GC_TAIL = 1152, 0, 256

_DIMS = {"nn": (((1,), (0,)), ((), ())), "nt": (((1,), (1,)), ((), ())), "tn": (((0,), (0,)), ((), ()))}


def _cp(sem, **kw):
    return pltpu.CompilerParams(dimension_semantics=sem, vmem_limit_bytes=VMEM_LIMIT, **kw)


def _dot(a, b, mode):
    return lax.dot_general(a.astype(BF16), b.astype(BF16), _DIMS[mode], preferred_element_type=F32)


@functools.partial(jax.custom_vjp, nondiff_argnums=(2,))
def bdot(a, b, mode):
    return _dot(a, b, mode)


def _bdot_fwd(a, b, mode):
    return _dot(a, b, mode), (a, b)


def _bdot_bwd(mode, res, g):
    a, b = res
    if mode == "nn":
        da, db = _dot(g, b, "nt"), _dot(a, g, "tn")
    elif mode == "nt":
        da, db = _dot(g, b, "nn"), _dot(g, a, "tn")
    else:
        da, db = _dot(b, g, "nt"), _dot(a, g, "nn")
    return da.astype(a.dtype), db.astype(b.dtype)


bdot.defvjp(_bdot_fwd, _bdot_bwd)


def _hdot_raw(a, b, mode):
    return lax.dot_general(a, b, _DIMS[mode], precision=lax.Precision.HIGHEST, preferred_element_type=F32)


@functools.partial(jax.custom_vjp, nondiff_argnums=(2,))
def hdot(a, b, mode):
    return _hdot_raw(a, b, mode)


def _hdot_fwd(a, b, mode):
    return _hdot_raw(a, b, mode), (a, b)


def _hdot_bwd(mode, res, g):
    a, b = res
    if mode == "nn":
        return _hdot_raw(g, b, "nt"), _hdot_raw(a, g, "tn")
    if mode == "nt":
        return _hdot_raw(g, b, "nn"), _hdot_raw(g, a, "tn")
    return _hdot_raw(b, g, "nt"), _hdot_raw(a, g, "nn")


hdot.defvjp(_hdot_fwd, _hdot_bwd)


def _log_sigmoid(x):
    return jnp.minimum(x, 0.0) - jnp.log(1.0 + jnp.exp(-jnp.abs(x)))


def _sigmoid(x):
    return 1.0 / (1.0 + jnp.exp(-x))


def _expm1(x):
    series = x * (1.0 + x * 0.5 * (1.0 + x * (1.0 / 3.0) * (1.0 + x * 0.25)))
    return jnp.where(jnp.abs(x) < 0.03, series, jnp.exp(x) - 1.0)


def _gelu_tanh(x):
    return 0.5 * x * (1.0 + jnp.tanh(0.7978845608028654 * (x + 0.044715 * x * x * x)))


def _softmax_rows(s):
    m = jnp.max(s, axis=-1, keepdims=True)
    p = jnp.exp(s - m)
    return p / jnp.sum(p, axis=-1, keepdims=True)


def _iota(shape, dim):
    return lax.broadcasted_iota(jnp.int32, shape, dim)


def _mm(a, b, mode, *, tm, tn, tk=None, out_dtype=F32, name, b_layer=None, into=None, relu_pair=False, times2=None):
    b2 = b.shape[-2:]
    if mode == "nn":
        (m, k), n = a.shape, b2[1]
    elif mode == "nt":
        (m, k), n = a.shape, b2[0]
    else:
        (k, m), n = a.shape, b2[1]
    tk = k if tk is None else tk
    assert m % tm == 0 and n % tn == 0 and k % tk == 0, (name, a.shape, b.shape)
    nk = k // tk
    a_spec = {"nn": pl.BlockSpec((tm, tk), lambda i, j, kk: (i, kk)),
              "nt": pl.BlockSpec((tm, tk), lambda i, j, kk: (i, kk)),
              "tn": pl.BlockSpec((tk, tm), lambda i, j, kk: (kk, i))}[mode]
    b_blk = {"nn": (tk, tn), "nt": (tn, tk), "tn": (tk, tn)}[mode]
    b_idx = {"nn": lambda i, j, kk: (kk, j), "nt": lambda i, j, kk: (j, kk), "tn": lambda i, j, kk: (kk, j)}[mode]
    if b_layer is None:
        b_spec = pl.BlockSpec(b_blk, b_idx)
    else:
        b_spec = pl.BlockSpec((None,) + b_blk, lambda i, j, kk: (b_layer,) + b_idx(i, j, kk))

    tile = pl.BlockSpec((tm, tn), lambda i, j, kk: (i, j))
    if into is not None:
        buf, per_slot, row_off = into
        assert m == 4 * per_slot and per_slot % tm == 0 and row_off % tm == 0 and buf.shape[2] == n, (name, buf.shape)
        bps = per_slot // tm
        out_specs = pl.BlockSpec((None, tm, tn), lambda i, j, kk: (i // bps, row_off // tm + i % bps, j))
        out_shape = jax.ShapeDtypeStruct(buf.shape, buf.dtype)
        extra_in, extra_specs, aliases = [buf], [pl.BlockSpec(memory_space=pl.ANY)], {2: 0}
        finish = lambda acc, extra: [acc.astype(buf.dtype)]
    elif relu_pair:
        out_specs = (tile, tile)
        out_shape = (jax.ShapeDtypeStruct((m, n), BF16),) * 2
        extra_in, extra_specs, aliases = [], [], {}

        def finish(acc, extra):
            r = jnp.maximum(acc, 0.0)
            return [(r * r).astype(BF16), r.astype(BF16)]
    elif times2 is not None:
        out_specs = tile
        out_shape = jax.ShapeDtypeStruct((m, n), out_dtype)
        extra_in, extra_specs, aliases = [times2], [tile], {}
        finish = lambda acc, extra: [(acc * (2.0 * extra[...].astype(F32))).astype(out_dtype)]
    else:
        out_specs = tile
        out_shape = jax.ShapeDtypeStruct((m, n), out_dtype)
        extra_in, extra_specs, aliases = [], [], {}
        finish = lambda acc, extra: [acc.astype(out_dtype)]
    n_out = 2 if relu_pair else 1

    def body(*refs):
        a_ref, b_ref = refs[0], refs[1]
        extra = refs[2] if extra_in else None
        o_refs = refs[2 + len(extra_in):2 + len(extra_in) + n_out]

        def store(acc):
            for o_ref, val in zip(o_refs, finish(acc, extra)):
                o_ref[...] = val

        if nk == 1:
            store(_dot(a_ref[...], b_ref[...], mode))
            return
        acc_ref = refs[-1]
        kk = pl.program_id(2)

        @pl.when(kk == 0)
        def _():
            acc_ref[...] = jnp.zeros_like(acc_ref)

        acc_ref[...] += _dot(a_ref[...], b_ref[...], mode)

        @pl.when(kk == nk - 1)
        def _():
            store(acc_ref[...])

    return pl.pallas_call(
        body, name=name, grid=(m // tm, n // tn, nk),
        in_specs=[a_spec, b_spec] + extra_specs,
        out_specs=out_specs, out_shape=out_shape,
        scratch_shapes=[pltpu.VMEM((tm, tn), F32)] if nk > 1 else [],
        input_output_aliases=aliases,
        compiler_params=_cp(("parallel", "parallel", "arbitrary")),
    )(a, b, *extra_in)


ROWS = 256


def _prenorm(x, w, name):
    def body(x_ref, w_ref, o_ref):
        xv = x_ref[...]
        r = lax.rsqrt(jnp.mean(xv * xv, axis=-1, keepdims=True) + EPS)
        o_ref[...] = (xv * r * w_ref[...]).astype(BF16)

    return pl.pallas_call(
        body, name=name, grid=(T // ROWS,),
        in_specs=[pl.BlockSpec((ROWS, D), lambda i: (i, 0)), pl.BlockSpec((1, D), lambda i: (0, 0))],
        out_specs=pl.BlockSpec((ROWS, D), lambda i: (i, 0)),
        out_shape=jax.ShapeDtypeStruct((T, D), BF16),
        compiler_params=_cp(("parallel",)),
    )(x, w)


def _postnorm(x, z, w, name):
    def body(x_ref, z_ref, w_ref, o_ref):
        zv = z_ref[...]
        r = lax.rsqrt(jnp.mean(zv * zv, axis=-1, keepdims=True) + EPS)
        o_ref[...] = x_ref[...] + zv * r * w_ref[...]

    return pl.pallas_call(
        body, name=name, grid=(T // ROWS,),
        in_specs=[pl.BlockSpec((ROWS, D), lambda i: (i, 0)), pl.BlockSpec((ROWS, D), lambda i: (i, 0)),
                  pl.BlockSpec((1, D), lambda i: (0, 0))],
        out_specs=pl.BlockSpec((ROWS, D), lambda i: (i, 0)),
        out_shape=jax.ShapeDtypeStruct((T, D), F32),
        compiler_params=_cp(("parallel",)),
    )(x, z, w)


def _norm_bwd(z, w, dy, add, name):
    has_add = add is not None

    def body(*refs):
        if has_add:
            z_ref, w_ref, dy_ref, add_ref, dz_ref, dw_ref = refs
        else:
            z_ref, w_ref, dy_ref, dz_ref, dw_ref = refs
        i = pl.program_id(0)

        @pl.when(i == 0)
        def _():
            dw_ref[...] = jnp.zeros_like(dw_ref)

        zv = z_ref[...].astype(F32)
        dyv = dy_ref[...]
        r = lax.rsqrt(jnp.mean(zv * zv, axis=-1, keepdims=True) + EPS)
        wdy = dyv * w_ref[...]
        dz = r * wdy - zv * (r * r * r) * jnp.mean(zv * wdy, axis=-1, keepdims=True)
        if has_add:
            dz = dz + add_ref[...]
        dz_ref[...] = dz.astype(dz_ref.dtype)
        dw_ref[...] += jnp.sum(dyv * zv * r, axis=0, keepdims=True)

    row = pl.BlockSpec((ROWS, D), lambda i: (i, 0))
    vec = pl.BlockSpec((1, D), lambda i: (0, 0))
    ins = [z, w, dy] + ([add] if has_add else [])
    dz_dtype = F32 if has_add else BF16
    return pl.pallas_call(
        body, name=name, grid=(T // ROWS,),
        in_specs=[row, vec, row] + ([row] if has_add else []),
        out_specs=(row, vec),
        out_shape=(jax.ShapeDtypeStruct((T, D), dz_dtype), jax.ShapeDtypeStruct((1, D), F32)),
        compiler_params=_cp(("arbitrary",)),
    )(*ins)


def _loss_and_grad(y, tgt):
    def body(y_ref, t_ref, g_ref, l_ref):
        i = pl.program_id(0)

        @pl.when(i == 0)
        def _():
            l_ref[...] = jnp.zeros_like(l_ref)

        e = y_ref[...] - t_ref[...]
        g_ref[...] = e * (1.0 / D)
        l_ref[...] += jnp.sum(e * e) * (0.5 / D)

    row = pl.BlockSpec((ROWS, D), lambda i: (i, 0))
    return pl.pallas_call(
        body, name="loss_head", grid=(T // ROWS,), in_specs=[row, row],
        out_specs=(row, pl.BlockSpec((1, 128), lambda i: (0, 0))),
        out_shape=(jax.ShapeDtypeStruct((T, D), F32), jax.ShapeDtypeStruct((1, 128), F32)),
        compiler_params=_cp(("arbitrary",)),
    )(y, tgt)


def _adamw(w, g, m, v, name):
    lead = w.shape[:-2]
    assert len(lead) <= 1 and g.shape == w.shape, (name, w.shape, g.shape)
    rows, cols = w.shape[-2:]
    if rows <= 512:
        tr, tc = rows, cols
    elif rows % 256 == 0:
        tr, tc = 256, cols
    else:
        tr, tc = rows, 256
    assert rows % tr == 0 and cols % tc == 0, (name, w.shape)
    c1 = 1.0 - ADAM_B1 ** ADAM_STEP
    c2 = 1.0 - ADAM_B2 ** ADAM_STEP

    def body(w_ref, g_ref, m_ref, v_ref, d_ref, mo_ref, vo_ref):
        gv = g_ref[...]
        mn = ADAM_B1 * m_ref[...] + (1.0 - ADAM_B1) * gv
        vn = ADAM_B2 * v_ref[...] + (1.0 - ADAM_B2) * (gv * gv)
        m_hat = mn / c1
        v_hat = vn / c2
        d_ref[...] = -ADAM_LR * (m_hat / (jnp.sqrt(v_hat) + ADAM_EPS) + ADAM_WD * w_ref[...])
        mo_ref[...] = mn
        vo_ref[...] = vn

    if lead:
        grid = (lead[0], rows // tr, cols // tc)
        blk = pl.BlockSpec((None, tr, tc), lambda l, i, j: (l, i, j))
    else:
        grid = (rows // tr, cols // tc)
        blk = pl.BlockSpec((tr, tc), lambda i, j: (i, j))
    sds = jax.ShapeDtypeStruct(w.shape, F32)
    return pl.pallas_call(body, name=name, grid=grid, in_specs=[blk] * 4, out_specs=(blk,) * 3,
                          out_shape=(sds,) * 3, compiler_params=_cp(("parallel",) * len(grid)))(w, g, m, v)


def _gla_consts():
    ltri = (_iota((CHUNK, CHUNK), 0) >= _iota((CHUNK, CHUNK), 1)).astype(F32)
    ones_c = jnp.ones((CHUNK, 128), F32)
    mask = (_iota((256, 512), 0) // 64 == _iota((256, 512), 1) // 128).astype(F32)
    return ltri, ones_c, mask


def _gla_chunk(consts, q, k, v, r, aux, s_prev, wa, ba, nw):
    ltri, ones_c, mask = consts
    la = _log_sigmoid(bdot(aux, wa, "nn") + ba) * (1.0 / 16.0)
    cum = hdot(ltri, la, "nn")
    total = jnp.sum(la, axis=0, keepdims=True)
    k_dec = k * jnp.exp(total - cum)
    inc = bdot(k_dec, v, "tn") * mask
    dec = jnp.exp(hdot(la, ones_c, "tn"))
    dec = jnp.concatenate([dec, dec, dec, dec], axis=1)
    s_new = dec * s_prev + inc
    o = bdot(q * GLA_SCALE, s_new, "nn")
    parts = []
    for h in range(4):
        oh = o[:, h * 128:(h + 1) * 128]
        parts.append(oh * lax.rsqrt(jnp.mean(oh * oh, axis=-1, keepdims=True) + EPS))
    on = jnp.concatenate(parts, axis=1)
    return s_new, on * nw * (r * _sigmoid(r))


def _gla_specs(cmap):
    return [pl.BlockSpec((CHUNK, 256), lambda c: (cmap(c), 0)),
            pl.BlockSpec((CHUNK, 256), lambda c: (cmap(c), 1)),
            pl.BlockSpec((CHUNK, 512), lambda c: (cmap(c), 1)),
            pl.BlockSpec((CHUNK, 512), lambda c: (cmap(c), 2)),
            pl.BlockSpec((CHUNK, 128), lambda c: (cmap(c), AUX_BLK))]


def _gla_fwd(proj, wa, ba, nw):
    def body(q_ref, k_ref, v_ref, r_ref, aux_ref, wa_ref, ba_ref, nw_ref, o_ref, sp_ref, s_ref):
        c = pl.program_id(0)

        @pl.when(c == 0)
        def _():
            s_ref[...] = jnp.zeros_like(s_ref)

        s_prev = s_ref[...]
        sp_ref[...] = s_prev
        s_new, out = _gla_chunk(_gla_consts(), q_ref[...], k_ref[...], v_ref[...], r_ref[...], aux_ref[...],
                                s_prev, wa_ref[...], ba_ref[...], nw_ref[...])
        s_ref[...] = s_new
        o_ref[...] = out

    full = lambda shape: pl.BlockSpec(shape, lambda c: (0,) * len(shape))
    return pl.pallas_call(
        body, name="gla_fwd", grid=(NCHUNK,),
        in_specs=_gla_specs(lambda c: c) + [full((128, 256)), full((1, 256)), full((1, 512))],
        out_specs=(pl.BlockSpec((CHUNK, 512), lambda c: (c, 0)), pl.BlockSpec((None, 256, 512), lambda c: (c, 0, 0))),
        out_shape=(jax.ShapeDtypeStruct((T, D), F32), jax.ShapeDtypeStruct((NCHUNK, 256, 512), F32)),
        scratch_shapes=[pltpu.VMEM((256, 512), F32)],
        compiler_params=_cp(("arbitrary",)),
    )(proj, proj, proj, proj, proj, wa, ba, nw)


def _gla_bwd(proj, s_prev_all, wa, ba, nw, dcat):
    rev = lambda c: NCHUNK - 1 - c

    def body(q_ref, k_ref, v_ref, r_ref, aux_ref, sp_ref, wa_ref, ba_ref, nw_ref, do_ref,
             dq_ref, dk_ref, dv_ref, dr_ref, daux_ref, dwa_ref, dba_ref, dnw_ref, ds_ref):
        c = pl.program_id(0)

        @pl.when(c == 0)
        def _():
            ds_ref[...] = jnp.zeros_like(ds_ref)
            dwa_ref[...] = jnp.zeros_like(dwa_ref)
            dba_ref[...] = jnp.zeros_like(dba_ref)
            dnw_ref[...] = jnp.zeros_like(dnw_ref)

        fn = functools.partial(_gla_chunk, _gla_consts())
        _, vjp = jax.vjp(fn, q_ref[...], k_ref[...], v_ref[...], r_ref[...], aux_ref[...], sp_ref[...],
                         wa_ref[...], ba_ref[...], nw_ref[...])
        dq, dk, dv, dr, daux, dsp, dwa, dba, dnw = vjp((ds_ref[...], do_ref[...]))
        dq_ref[...] = dq
        dk_ref[...] = dk
        dv_ref[...] = dv
        dr_ref[...] = dr
        daux_ref[...] = daux
        ds_ref[...] = dsp
        dwa_ref[...] += dwa
        dba_ref[...] += dba
        dnw_ref[...] += dnw

    full = lambda shape: pl.BlockSpec(shape, lambda c: (0,) * len(shape))
    blk = lambda w: pl.BlockSpec((CHUNK, w), lambda c: (rev(c), 0))
    sds = lambda *s: jax.ShapeDtypeStruct(s, F32)
    return pl.pallas_call(
        body, name="gla_bwd", grid=(NCHUNK,),
        in_specs=_gla_specs(rev) + [pl.BlockSpec((None, 256, 512), lambda c: (rev(c), 0, 0)),
                                    full((128, 256)), full((1, 256)), full((1, 512)), blk(512)],
        out_specs=(blk(256), blk(256), blk(512), blk(512), blk(128), full((128, 256)), full((1, 256)), full((1, 512))),
        out_shape=(sds(T, 256), sds(T, 256), sds(T, 512), sds(T, 512), sds(T, 128),
                   sds(128, 256), sds(1, 256), sds(1, 512)),
        scratch_shapes=[pltpu.VMEM((256, 512), F32)],
        compiler_params=_cp(("arbitrary",)),
    )(proj, proj, proj, proj, proj, s_prev_all, wa, ba, nw, dcat)


GATE_ROWS = 128


def _fox_gate_block(ltri, aux, bpad, carry):
    lf = _log_sigmoid(aux + bpad)
    cum = hdot(ltri, lf, "nn") + carry
    return cum, carry + jnp.sum(lf, axis=0, keepdims=True)


def _gate_ltri():
    return (_iota((GATE_ROWS, GATE_ROWS), 0) >= _iota((GATE_ROWS, GATE_ROWS), 1)).astype(F32)


def _fox_gate_fwd(proj, bpad):
    def body(aux_ref, b_ref, cum_ref, carry_ref):
        i = pl.program_id(0)

        @pl.when(i == 0)
        def _():
            carry_ref[...] = jnp.zeros_like(carry_ref)

        cum, carry = _fox_gate_block(_gate_ltri(), aux_ref[...], b_ref[...], carry_ref[...])
        cum_ref[...] = cum
        carry_ref[...] = carry

    return pl.pallas_call(
        body, name="fox_gate_fwd", grid=(T // GATE_ROWS,),
        in_specs=[pl.BlockSpec((GATE_ROWS, 128), lambda i: (i, AUX_BLK)), pl.BlockSpec((1, 128), lambda i: (0, 0))],
        out_specs=pl.BlockSpec((GATE_ROWS, 128), lambda i: (i, 0)),
        out_shape=jax.ShapeDtypeStruct((T, 128), F32),
        scratch_shapes=[pltpu.VMEM((1, 128), F32)],
        compiler_params=_cp(("arbitrary",)),
    )(proj, bpad)


def _fox_gate_bwd(proj, bpad, dccol_t, daux_gla):
    nb = T // GATE_ROWS
    rev = lambda i: nb - 1 - i

    def body(aux_ref, b_ref, dc_ref, dg_ref, daux_ref, db_ref, dcarry_ref):
        i = pl.program_id(0)

        @pl.when(i == 0)
        def _():
            dcarry_ref[...] = jnp.zeros_like(dcarry_ref)
            db_ref[...] = jnp.zeros_like(db_ref)

        dcum = dc_ref[...]
        fn = functools.partial(_fox_gate_block, _gate_ltri())
        _, vjp = jax.vjp(fn, aux_ref[...], b_ref[...], jnp.zeros((1, 128), F32))
        daux, db, dcarry = vjp((dcum, dcarry_ref[...]))
        daux_ref[...] = daux + dg_ref[...]
        db_ref[...] += db
        dcarry_ref[...] = dcarry

    blk = pl.BlockSpec((GATE_ROWS, 128), lambda i: (rev(i), 0))
    vec = pl.BlockSpec((1, 128), lambda i: (0, 0))
    return pl.pallas_call(
        body, name="fox_gate_bwd", grid=(nb,),
        in_specs=[pl.BlockSpec((GATE_ROWS, 128), lambda i: (rev(i), AUX_BLK)), vec, blk, blk],
        out_specs=(blk, vec),
        out_shape=(jax.ShapeDtypeStruct((T, 128), F32), jax.ShapeDtypeStruct((1, 128), F32)),
        scratch_shapes=[pltpu.VMEM((1, 128), F32)],
        compiler_params=_cp(("arbitrary",)),
    )(proj, bpad, dccol_t, daux_gla)


FOX_Q = 128


FOX_QB = T // FOX_Q


@jax.custom_vjp
def _attend(s, v):
    return _attend_fwd(s, v)[0]


def _attend_fwd(s, v):
    e = jnp.exp(s - jnp.max(s, axis=-1, keepdims=True))
    r = 1.0 / jnp.sum(e, axis=-1, keepdims=True)
    return _dot(e, v, "nn") * r, (e, r, v)


def _attend_bwd(res, do):
    e, r, v = res
    do_r = do * r
    dpr = _dot(do_r, v, "nt")
    ds = e * (dpr - r * jnp.sum(e * dpr, axis=-1, keepdims=True))
    return ds, _dot(e, do_r, "tn").astype(v.dtype)


_attend.defvjp(_attend_fwd, _attend_bwd)


def _fox_block(hp, q, k, v, ccol):
    kl = k.shape[0]
    lane = _iota((FOX_Q, 128), 1)
    tri = _iota((FOX_Q, FOX_Q), 0) >= _iota((FOX_Q, FOX_Q), 1)
    sub = _iota((8, kl), 0)
    outs = []
    for e in range(2):
        qm = jnp.where((lane >= 64 * e) & (lane < 64 * (e + 1)), q * ATT_SCALE, 0.0)
        cs = jnp.sum(jnp.where(sub == 2 * hp + e, ccol, 0.0), axis=0, keepdims=True)
        s = bdot(qm, k, "nt") - cs
        diag = jnp.where(tri, s[:, kl - FOX_Q:], NEG)
        s = diag if kl == FOX_Q else jnp.concatenate([s[:, :kl - FOX_Q], diag], axis=1)
        outs.append(_attend(s, v))
    return jnp.where(lane < 64, outs[0], outs[1])


def _fox_in_specs():
    return [pl.BlockSpec((FOX_Q, 128), lambda hp, qb: (qb, 12 + hp)),
            pl.BlockSpec((T, 128), lambda hp, qb: (0, 16 + hp)),
            pl.BlockSpec((T, 128), lambda hp, qb: (0, 20 + hp)),
            pl.BlockSpec((8, T), lambda hp, qb: (0, 0))]


def _fox_fwd(proj, cum_c, cat):
    def body(q_ref, k_ref, v_ref, cc_ref, cat_ref, o_ref):
        qb = pl.program_id(1)
        for g in range(FOX_QB):
            kl = FOX_Q * (g + 1)

            @pl.when(qb == g)
            def _(kl=kl):
                o_ref[...] = _fox_block(pl.program_id(0), q_ref[...], k_ref[0:kl, :], v_ref[0:kl, :], cc_ref[:, 0:kl])

    return pl.pallas_call(
        body, name="fox_fwd", grid=(4, FOX_QB), in_specs=_fox_in_specs() + [pl.BlockSpec(memory_space=pl.ANY)],
        out_specs=pl.BlockSpec((FOX_Q, 128), lambda hp, qb: (qb, 4 + hp)),
        out_shape=jax.ShapeDtypeStruct((T, D), F32), input_output_aliases={4: 0},
        compiler_params=_cp(("parallel", "parallel")),
    )(proj, proj, proj, cum_c, cat)


def _fox_bwd(proj, cum_c, dcat):
    def body(q_ref, k_ref, v_ref, cc_ref, do_ref, dq_ref, dk_ref, dv_ref, dcc_ref):
        qb = pl.program_id(1)

        @pl.when(qb == 0)
        def _():
            dk_ref[...] = jnp.zeros_like(dk_ref)
            dv_ref[...] = jnp.zeros_like(dv_ref)
            dcc_ref[...] = jnp.zeros_like(dcc_ref)

        fn = functools.partial(_fox_block, pl.program_id(0))
        for g in range(FOX_QB):
            kl = FOX_Q * (g + 1)

            @pl.when(qb == g)
            def _(kl=kl):
                _, vjp = jax.vjp(fn, q_ref[...], k_ref[0:kl, :], v_ref[0:kl, :], cc_ref[:, 0:kl])
                dq, dk, dv, dcc = vjp(do_ref[...])
                dq_ref[...] = dq
                dk_ref[0:kl, :] += dk
                dv_ref[0:kl, :] += dv
                dcc_ref[:, 0:kl] += dcc

    sds = lambda *s: jax.ShapeDtypeStruct(s, F32)
    return pl.pallas_call(
        body, name="fox_bwd", grid=(4, FOX_QB),
        in_specs=_fox_in_specs() + [pl.BlockSpec((FOX_Q, 128), lambda hp, qb: (qb, 4 + hp))],
        out_specs=(pl.BlockSpec((FOX_Q, 128), lambda hp, qb: (qb, hp)),
                   pl.BlockSpec((T, 128), lambda hp, qb: (0, hp)),
                   pl.BlockSpec((T, 128), lambda hp, qb: (0, hp)),
                   pl.BlockSpec((None, 8, T), lambda hp, qb: (hp, 0, 0))),
        out_shape=(sds(T, 512), sds(T, 512), sds(T, 512), sds(4, 8, T)),
        compiler_params=_cp(("parallel", "arbitrary")),
    )(proj, proj, proj, cum_c, dcat)


BIAS_Q = 8


def _rel_onehot(q):
    kj = _iota((REL_PAD, CA_BAND), 1)
    rel = jnp.clip(CA_PAD + q - kj, -128, 128) + 128
    return (_iota((REL_PAD, CA_BAND), 0) == rel).astype(F32)


def _bias_build(rbp):
    def body(rb_ref, o_ref):
        for j in range(BIAS_Q):
            o_ref[j] = _hdot_raw(rb_ref[...], _rel_onehot(pl.program_id(0) * BIAS_Q + j), "nn")

    return pl.pallas_call(
        body, name="ca_bias_build", grid=(CHUNK // BIAS_Q,),
        in_specs=[pl.BlockSpec((8, REL_PAD), lambda q: (0, 0))],
        out_specs=pl.BlockSpec((BIAS_Q, 8, CA_BAND), lambda q: (q, 0, 0)),
        out_shape=jax.ShapeDtypeStruct((CHUNK, 8, CA_BAND), F32),
        compiler_params=_cp(("parallel",)),
    )(rbp)


def _bias_grad(dbias_q):
    def body(db_ref, o_ref):
        q = pl.program_id(0)

        @pl.when(q == 0)
        def _():
            o_ref[...] = jnp.zeros_like(o_ref)

        acc = o_ref[...]
        for j in range(BIAS_Q):
            acc = acc + _hdot_raw(db_ref[j], _rel_onehot(q * BIAS_Q + j), "nt")
        o_ref[...] = acc

    return pl.pallas_call(
        body, name="ca_bias_grad", grid=(CHUNK // BIAS_Q,),
        in_specs=[pl.BlockSpec((BIAS_Q, 8, CA_BAND), lambda q: (q, 0, 0))],
        out_specs=pl.BlockSpec((8, REL_PAD), lambda q: (0, 0)),
        out_shape=jax.ShapeDtypeStruct((8, REL_PAD), F32),
        compiler_params=_cp(("arbitrary",)),
    )(dbias_q)


def _ca_block(c, masked, q, kb, vb, bias2):
    lane = _iota((CHUNK, 128), 1)
    outs = []
    for e in range(2):
        qm = jnp.where((lane >= 64 * e) & (lane < 64 * (e + 1)), q * ATT_SCALE, 0.0)
        s = bdot(qm, kb, "nt") + bias2[e]
        if masked:
            s = jnp.where((c * CHUNK - CA_PAD + _iota((CHUNK, CA_BAND), 1)) >= 0, s, NEG)
        outs.append(_attend(s, vb))
    return jnp.where(lane < 64, outs[0], outs[1])


CA_PER_STEP = 4
CA_ROWS = CA_PER_STEP * CHUNK
CA_MASKED_STEPS = CA_PAD // CA_ROWS


def _ca_fwd(proj, kvpad, bias):
    def body(q_ref, k_ref, v_ref, b_ref, o_ref):
        def run(masked):
            for i in range(CA_PER_STEP):
                c = pl.program_id(1) * CA_PER_STEP + i
                band = pl.ds(pl.multiple_of(c * CHUNK, CHUNK), CA_BAND)
                rows = slice(i * CHUNK, (i + 1) * CHUNK)
                o_ref[rows, :] = _ca_block(c, masked, q_ref[rows, :], k_ref[band, :], v_ref[band, :], b_ref[...])

        pl.when(pl.program_id(1) < CA_MASKED_STEPS)(lambda: run(True))
        pl.when(pl.program_id(1) >= CA_MASKED_STEPS)(lambda: run(False))

    return pl.pallas_call(
        body, name="ca_fwd", grid=(4, NCHUNK // CA_PER_STEP),
        in_specs=[pl.BlockSpec((CA_ROWS, 128), lambda hp, c: (c, hp)),
                  pl.BlockSpec((T + CA_PAD, 128), lambda hp, c: (0, hp)),
                  pl.BlockSpec((T + CA_PAD, 128), lambda hp, c: (0, 4 + hp)),
                  pl.BlockSpec((2, CHUNK, CA_BAND), lambda hp, c: (hp, 0, 0))],
        out_specs=pl.BlockSpec((CA_ROWS, 128), lambda hp, c: (c, hp)),
        out_shape=jax.ShapeDtypeStruct((T, D), F32),
        compiler_params=_cp(("parallel", "parallel")),
    )(proj, kvpad, kvpad, bias)


def _ca_bwd(proj, kvpad, bias, dcat):
    def body(q_ref, k_ref, v_ref, b_ref, do_ref, dq_ref, dk_ref, dv_ref, db_ref):
        c = pl.program_id(1)

        @pl.when(c == 0)
        def _():
            dk_ref[...] = jnp.zeros_like(dk_ref)
            dv_ref[...] = jnp.zeros_like(dv_ref)
            db_ref[...] = jnp.zeros_like(db_ref)

        def run(masked):
            for i in range(CA_PER_STEP):
                ci = c * CA_PER_STEP + i
                band = pl.ds(pl.multiple_of(ci * CHUNK, CHUNK), CA_BAND)
                rows = slice(i * CHUNK, (i + 1) * CHUNK)
                fn = functools.partial(_ca_block, ci, masked)
                _, vjp = jax.vjp(fn, q_ref[rows, :], k_ref[band, :], v_ref[band, :], b_ref[...])
                dq, dkb, dvb, db = vjp(do_ref[rows, :])
                dq_ref[rows, :] = dq
                dk_ref[band, :] += dkb
                dv_ref[band, :] += dvb
                db_ref[...] += db

        pl.when(c < CA_MASKED_STEPS)(lambda: run(True))
        pl.when(c >= CA_MASKED_STEPS)(lambda: run(False))

    sds = lambda *s: jax.ShapeDtypeStruct(s, F32)
    padded = lambda: pl.BlockSpec((T + CA_PAD, 128), lambda hp, c: (0, hp))
    return pl.pallas_call(
        body, name="ca_bwd", grid=(4, NCHUNK // CA_PER_STEP),
        in_specs=[pl.BlockSpec((CA_ROWS, 128), lambda hp, c: (c, hp)),
                  pl.BlockSpec((T + CA_PAD, 128), lambda hp, c: (0, hp)),
                  pl.BlockSpec((T + CA_PAD, 128), lambda hp, c: (0, 4 + hp)),
                  pl.BlockSpec((2, CHUNK, CA_BAND), lambda hp, c: (hp, 0, 0)),
                  pl.BlockSpec((CA_ROWS, 128), lambda hp, c: (c, hp))],
        out_specs=(pl.BlockSpec((CA_ROWS, 128), lambda hp, c: (c, hp)), padded(), padded(),
                   pl.BlockSpec((2, CHUNK, CA_BAND), lambda hp, c: (hp, 0, 0))),
        out_shape=(sds(T, 512), sds(T + CA_PAD, 512), sds(T + CA_PAD, 512), sds(8, CHUNK, CA_BAND)),
        compiler_params=_cp(("parallel", "arbitrary")),
    )(proj, kvpad, kvpad, bias, dcat)


def _lru_pre(xs, cw, cb, wa, ba, wx, bx, lam):
    xc = cb + xs[0] * cw[0:1, :] + xs[1] * cw[1:2, :] + xs[2] * cw[2:3, :] + xs[3] * cw[3:4, :]
    ra = _sigmoid(bdot(xc, wa, "nn") + ba)
    ii = _sigmoid(bdot(xc, wx, "nn") + bx)
    la = 8.0 * ra * _log_sigmoid(lam)
    return jnp.exp(la), jnp.sqrt(-_expm1(2.0 * la)) * (ii * xc)


def _lru_pre_specs():
    full = lambda shape: pl.BlockSpec(shape, lambda i: (0,) * len(shape))
    return [pl.BlockSpec((4, ROWS, 512), lambda i: (0, i, 0)), full((4, 512)), full((1, 512)),
            full((512, 512)), full((1, 512)), full((512, 512)), full((1, 512)), full((1, 512))]


def _lru_pre_fwd(xs, cw, cb, wa, ba, wx, bx, lam):
    def body(xs_ref, cw_ref, cb_ref, wa_ref, ba_ref, wx_ref, bx_ref, lam_ref, a_ref, b_ref):
        a, b = _lru_pre(xs_ref[...], cw_ref[...], cb_ref[...], wa_ref[...], ba_ref[...], wx_ref[...], bx_ref[...],
                        lam_ref[...])
        a_ref[...] = a
        b_ref[...] = b

    row = pl.BlockSpec((ROWS, 512), lambda i: (i, 0))
    sds = jax.ShapeDtypeStruct((T, 512), F32)
    return pl.pallas_call(body, name="lru_pre_fwd", grid=(T // ROWS,), in_specs=_lru_pre_specs(),
                          out_specs=(row, row), out_shape=(sds, sds), compiler_params=_cp(("parallel",)),
                          )(xs, cw, cb, wa, ba, wx, bx, lam)


def _lru_pre_bwd(xs, cw, cb, wa, ba, wx, bx, lam, da, db):
    def body(xs_ref, cw_ref, cb_ref, wa_ref, ba_ref, wx_ref, bx_ref, lam_ref, da_ref, db_ref,
             dxs_ref, dcw_ref, dcb_ref, dwa_ref, dba_ref, dwx_ref, dbx_ref, dlam_ref):
        acc = (dcw_ref, dcb_ref, dwa_ref, dba_ref, dwx_ref, dbx_ref, dlam_ref)

        @pl.when(pl.program_id(0) == 0)
        def _():
            for r in acc:
                r[...] = jnp.zeros_like(r)

        _, vjp = jax.vjp(_lru_pre, xs_ref[...], cw_ref[...], cb_ref[...], wa_ref[...], ba_ref[...], wx_ref[...],
                         bx_ref[...], lam_ref[...])
        grads = vjp((da_ref[...], db_ref[...]))
        dxs_ref[...] = grads[0]
        for r, g in zip(acc, grads[1:]):
            r[...] += g

    row = pl.BlockSpec((ROWS, 512), lambda i: (i, 0))
    specs = _lru_pre_specs()
    sds = lambda *s: jax.ShapeDtypeStruct(s, F32)
    return pl.pallas_call(
        body, name="lru_pre_bwd", grid=(T // ROWS,), in_specs=specs + [row, row], out_specs=tuple(specs),
        out_shape=(sds(4, T, 512), sds(4, 512), sds(1, 512), sds(512, 512), sds(1, 512), sds(512, 512), sds(1, 512),
                   sds(1, 512)),
        compiler_params=_cp(("arbitrary",)),
    )(xs, cw, cb, wa, ba, wx, bx, lam, da, db)


def _lru_scan_fwd(a, b):
    def body(a_ref, b_ref, h_ref):
        def step(t, h):
            h = a_ref[pl.ds(t, 1), :] * h + b_ref[pl.ds(t, 1), :]
            h_ref[pl.ds(t, 1), :] = h
            return h

        lax.fori_loop(0, T, step, jnp.zeros((1, 512), F32))

    return pl.pallas_call(body, name="lru_scan_fwd", out_shape=jax.ShapeDtypeStruct((T, 512), F32),
                          compiler_params=pltpu.CompilerParams(vmem_limit_bytes=VMEM_LIMIT))(a, b)


def _lru_scan_bwd(a, h, dh):
    def body(a_ref, h_ref, dh_ref, da_ref, db_ref):
        def step(i, carry):
            t = T - 1 - i
            g = dh_ref[pl.ds(t, 1), :] + carry
            db_ref[pl.ds(t, 1), :] = g
            da_ref[pl.ds(t, 1), :] = g * h_ref[pl.ds(t - 1, 1), :]
            return a_ref[pl.ds(t, 1), :] * g

        carry = lax.fori_loop(0, T - 1, step, jnp.zeros((1, 512), F32))
        db_ref[pl.ds(0, 1), :] = dh_ref[pl.ds(0, 1), :] + carry
        da_ref[pl.ds(0, 1), :] = jnp.zeros((1, 512), F32)

    sds = jax.ShapeDtypeStruct((T, 512), F32)
    return pl.pallas_call(body, name="lru_scan_bwd", out_shape=(sds, sds),
                          compiler_params=pltpu.CompilerParams(vmem_limit_bytes=VMEM_LIMIT))(a, h, dh)


def _lru_post(h, gate):
    return h * _gelu_tanh(gate)


def _lru_post_fwd(h, proj, cat):
    def body(h_ref, g_ref, cat_ref, o_ref):
        o_ref[...] = _lru_post(h_ref[...], g_ref[...])

    row = pl.BlockSpec((ROWS, 512), lambda i: (i, 0))
    return pl.pallas_call(body, name="lru_post_fwd", grid=(T // ROWS,),
                          in_specs=[row, pl.BlockSpec((ROWS, 512), lambda i: (i, 3)), pl.BlockSpec(memory_space=pl.ANY)],
                          out_specs=pl.BlockSpec((ROWS, 512), lambda i: (i, 1)),
                          out_shape=jax.ShapeDtypeStruct((T, D), F32), input_output_aliases={2: 0},
                          compiler_params=_cp(("parallel",)))(h, proj, cat)


def _lru_post_bwd(h, proj, dcat):
    def body(h_ref, g_ref, do_ref, dh_ref, dg_ref):
        _, vjp = jax.vjp(_lru_post, h_ref[...], g_ref[...])
        dh, dg = vjp(do_ref[...])
        dh_ref[...] = dh
        dg_ref[...] = dg

    row = pl.BlockSpec((ROWS, 512), lambda i: (i, 0))
    sds = jax.ShapeDtypeStruct((T, 512), F32)
    return pl.pallas_call(body, name="lru_post_bwd", grid=(T // ROWS,),
                          in_specs=[row, pl.BlockSpec((ROWS, 512), lambda i: (i, 3)),
                                    pl.BlockSpec((ROWS, 512), lambda i: (i, 1))],
                          out_specs=(row, row), out_shape=(sds, sds), compiler_params=_cp(("parallel",)))(h, proj, dcat)


def _conv_dx(dxs_shift):
    def body(d_ref, o_ref):
        o_ref[...] = d_ref[0] + d_ref[1] + d_ref[2] + d_ref[3]

    row = pl.BlockSpec((ROWS, 512), lambda i: (i, 0))
    return pl.pallas_call(body, name="lru_conv_dx", grid=(T // ROWS,),
                          in_specs=[pl.BlockSpec((4, ROWS, 512), lambda i: (0, i, 0))], out_specs=row,
                          out_shape=jax.ShapeDtypeStruct((T, 512), F32), compiler_params=_cp(("parallel",)))(dxs_shift)


def _position():
    return lax.axis_index("x"), lax.axis_index("y"), lax.axis_index("c")


def _other_chips(x, y):
    return [(1 - x, y), (x, 1 - y), (1 - x, 1 - y)]


def _al(v, n):
    return v * n if isinstance(v, int) else pl.multiple_of(v * n, n)


_AG_ITEMS = [
    ((4, 32, 128), lambda o, s, h: o.at[s, pl.ds(_al(h, 16), 16), :], lambda r, h: r.at[pl.ds(_al(h, 16), 16), :]),
    ((4, 774, 1024), lambda o, s, h: o.at[s, :, pl.ds(_al(h, 512), 512)], lambda r, h: r.at[:, pl.ds(_al(h, 512), 512)]),
    ((1024, 1024), lambda o, s, h: o.at[pl.ds(_al(2 * s + h, 128), 128), :], lambda r, h: r.at[pl.ds(_al(h, 128), 128), :]),
    ((2, 1024, 4096), lambda o, s, h: o.at[h, :, pl.ds(_al(s, 1024), 1024)], lambda r, h: r.at[h]),
    ((2, 4096, 1024), lambda o, s, h: o.at[h, pl.ds(_al(s, 1024), 1024), :], lambda r, h: r.at[h]),
    ((1024, 2560), lambda o, s, h: o.at[pl.ds(_al(h, 512), 512), pl.ds(_al(s, 640), 640)],
     lambda r, h: r.at[pl.ds(_al(h, 512), 512), :]),
    ((1024, 1024), lambda o, s, h: o.at[pl.ds(_al(2 * s + h, 128), 128), :], lambda r, h: r.at[pl.ds(_al(h, 128), 128), :]),
]


_AG_GROUPS = [(0, 1, 2), (3, 4), (5, 6)]

_HBM = pl.BlockSpec(memory_space=pltpu.HBM)
_SEM = pl.BlockSpec(memory_space=pltpu.SEMAPHORE)
_SPLIT = dict(has_side_effects=pltpu.SideEffectType.DATAFLOW_SIDE_EFFECTING)


def _hbm(a):
    return pltpu.with_memory_space_constraint(a, pltpu.HBM)


def _ag_ici_copy(i, j, chip, c, slot, src_ref, land_ref, send_sems, recv_sems, k):
    _, dst, half = _AG_ITEMS[i]
    return pltpu.make_async_remote_copy(src_ref=half(src_ref, c), dst_ref=dst(land_ref, slot, c), send_sem=send_sems.at[k],
                                        recv_sem=recv_sems.at[k], device_id=(*chip, c), device_id_type=MESH)


def _ag_start(shards):
    n = len(_AG_ITEMS)
    ng = len(_AG_GROUPS)
    lands = [lax.empty(shape, s.dtype) for (shape, _, _), s in zip(_AG_ITEMS, shards)]

    def body(*refs):
        srcs, land_refs = refs[:n], refs[n:2 * n]
        sems = refs[2 * n:2 * n + 2 * ng]
        token = refs[-1]
        x, y, c = _position()
        me = 2 * x + y
        for g, items in enumerate(_AG_GROUPS):
            for t, i in enumerate(items):
                for j, chip in enumerate(_other_chips(x, y)):
                    _ag_ici_copy(i, j, chip, c, me, srcs[i], land_refs[i], sems[2 * g], sems[2 * g + 1], 3 * t + j).start()
        token[...] = jnp.zeros_like(token)

    sem_shapes = []
    for items in _AG_GROUPS:
        sem_shapes += [pltpu.SemaphoreType.DMA((3 * len(items),))] * 2
    thru = [pltpu.HBM(a.shape, a.dtype) for a in list(shards) + lands]
    out = pl.pallas_call(
        body, name="allgather_start",
        out_shape=tuple(sem_shapes) + tuple(thru) + (jax.ShapeDtypeStruct((8, 128), F32),),
        in_specs=(_HBM,) * (2 * n),
        out_specs=(_SEM,) * (2 * ng) + (_HBM,) * (2 * n) + (pl.BlockSpec(memory_space=pltpu.VMEM),),
        input_output_aliases={i: 2 * ng + i for i in range(2 * n)},
        compiler_params=pltpu.CompilerParams(**_SPLIT),
    )(*[_hbm(a) for a in list(shards) + lands])
    sems, thru, token = out[:2 * ng], out[2 * ng:-1], out[-1]
    return [(sems[2 * g], sems[2 * g + 1]) for g in range(ng)], list(thru[:n]), list(thru[n:]), token


def _ag_wait(g, sems, srcs, lands, after):
    items = _AG_GROUPS[g]
    m = len(items)

    def body(*refs):
        src_refs, land_refs = refs[:m], refs[m:2 * m]
        send_sems, recv_sems = refs[2 * m], refs[2 * m + 1]
        x, y, c = _position()
        for t, i in enumerate(items):
            for j, chip in enumerate(_other_chips(x, y)):
                cp = _ag_ici_copy(i, j, chip, c, 2 * chip[0] + chip[1], src_refs[t], land_refs[t], send_sems, recv_sems,
                                  3 * t + j)
                cp.wait_send()
                cp.wait_recv()

    ops = [srcs[i] for i in items] + [lands[i] for i in items]
    out = pl.pallas_call(
        body, name=f"allgather_wait_{g}",
        out_shape=tuple(pltpu.HBM(a.shape, a.dtype) for a in ops),
        in_specs=(_HBM,) * (2 * m) + (_SEM, _SEM, pl.BlockSpec(memory_space=pl.ANY)),
        out_specs=(_HBM,) * (2 * m),
        input_output_aliases={i: i for i in range(2 * m)},
        compiler_params=pltpu.CompilerParams(**_SPLIT),
    )(*ops, sems[0], sems[1], after)
    return list(out[:m]), list(out[m:])


def _ag_forward(g, srcs, lands):
    items = _AG_GROUPS[g]
    m = len(items)

    def body(*refs):
        src_refs, in_refs, out_refs = refs[:m], refs[m:2 * m], refs[2 * m:3 * m]
        send_sems, recv_sems = refs[3 * m:]
        x, y, c = _position()
        sibling = (x, y, 1 - c)
        me = 2 * x + y
        chips = _other_chips(x, y)
        sends = []
        for t, i in enumerate(items):
            _, dst, half = _AG_ITEMS[i]
            for j, chip in enumerate(chips):
                slot = 2 * chip[0] + chip[1]
                sends.append(pltpu.make_async_remote_copy(
                    src_ref=dst(in_refs[t], slot, c), dst_ref=dst(out_refs[t], slot, c), send_sem=send_sems.at[5 * t + j],
                    recv_sem=recv_sems.at[5 * t + j], device_id=sibling, device_id_type=MESH))
            for hc in range(2):
                sends.append(pltpu.make_async_remote_copy(
                    src_ref=half(src_refs[t], hc), dst_ref=dst(out_refs[t], me, hc), send_sem=send_sems.at[5 * t + 3 + hc],
                    recv_sem=recv_sems.at[5 * t + 3 + hc], device_id=sibling, device_id_type=MESH))
        for cp in sends:
            cp.start()
        for t, i in enumerate(items):
            _, dst, half = _AG_ITEMS[i]
            for j, chip in enumerate(chips):
                there = dst(out_refs[t], 2 * chip[0] + chip[1], 1 - c)
                pltpu.make_async_remote_copy(src_ref=there, dst_ref=there, send_sem=send_sems.at[5 * t + j],
                                             recv_sem=recv_sems.at[5 * t + j], device_id=sibling,
                                             device_id_type=MESH).wait_recv()
            for hc in range(2):
                there = dst(out_refs[t], me, hc)
                pltpu.make_async_remote_copy(src_ref=there, dst_ref=there, send_sem=send_sems.at[5 * t + 3 + hc],
                                             recv_sem=recv_sems.at[5 * t + 3 + hc], device_id=sibling,
                                             device_id_type=MESH).wait_recv()
        for cp in sends:
            cp.wait_send()

    any_spec = pl.BlockSpec(memory_space=pl.ANY)
    return pl.pallas_call(
        body, name=f"allgather_forward_{g}",
        in_specs=[any_spec] * (2 * m), out_specs=(any_spec,) * m,
        out_shape=tuple(jax.ShapeDtypeStruct(a.shape, a.dtype) for a in lands),
        input_output_aliases={m + t: t for t in range(m)},
        scratch_shapes=[pltpu.SemaphoreType.DMA((5 * m,)), pltpu.SemaphoreType.DMA((5 * m,))],
    )(*srcs, *lands)


def _pair_swap_cols(gb, tag):
    _, rows, cols = gb.shape
    hc = cols // 2

    def body(g_ref, out_ref, send_sem, recv_sem):
        x, y, c = _position()
        cp = pltpu.make_async_remote_copy(src_ref=g_ref.at[:, :, pl.ds(_al(1 - c, hc), hc)], dst_ref=out_ref,
                                          send_sem=send_sem, recv_sem=recv_sem, device_id=(x, y, 1 - c),
                                          device_id_type=MESH)
        cp.start()
        cp.wait()

    return pl.pallas_call(
        body, name="grad_pair_swap_" + tag,
        in_specs=[pl.BlockSpec(memory_space=pl.ANY)], out_specs=pl.BlockSpec(memory_space=pl.ANY),
        out_shape=jax.ShapeDtypeStruct((4, rows, hc), gb.dtype),
        scratch_shapes=[pltpu.SemaphoreType.DMA, pltpu.SemaphoreType.DMA],
    )(gb)


def _handover(halves, tag):
    def body(in_ref, out_ref, send_sem, recv_sem):
        x, y, c = _position()
        cp = pltpu.make_async_remote_copy(src_ref=in_ref.at[c], dst_ref=out_ref.at[c], send_sem=send_sem,
                                          recv_sem=recv_sem, device_id=(x, y, 1 - c), device_id_type=MESH)
        cp.start()
        theirs = out_ref.at[1 - c]
        pltpu.make_async_remote_copy(src_ref=theirs, dst_ref=theirs, send_sem=send_sem, recv_sem=recv_sem,
                                     device_id=(x, y, c), device_id_type=MESH).wait_recv()
        cp.wait_send()

    return pl.pallas_call(
        body, name="grad_handover_" + tag,
        in_specs=[pl.BlockSpec(memory_space=pl.ANY)], out_specs=pl.BlockSpec(memory_space=pl.ANY),
        out_shape=jax.ShapeDtypeStruct(halves.shape, halves.dtype), input_output_aliases={0: 0},
        scratch_shapes=[pltpu.SemaphoreType.DMA, pltpu.SemaphoreType.DMA],
    )(halves)


def _a2a_copy(j, chip, c, p_ref, q_ref, q_slot, send_sems, recv_sems):
    return pltpu.make_async_remote_copy(src_ref=p_ref.at[2 * chip[0] + chip[1]], dst_ref=q_ref.at[q_slot],
                                        send_sem=send_sems.at[j], recv_sem=recv_sems.at[j], device_id=(*chip, c),
                                        device_id_type=MESH)


def _a2a_start(p, tag):
    def body(p_ref, q_ref, send_sems, recv_sems, p_thru, q_thru, token):
        x, y, c = _position()
        for j, chip in enumerate(_other_chips(x, y)):
            _a2a_copy(j, chip, c, p_ref, q_ref, 2 * x + y, send_sems, recv_sems).start()
        token[...] = jnp.zeros_like(token)

    return pl.pallas_call(
        body, name="grad_alltoall_start_" + tag,
        out_shape=(pltpu.SemaphoreType.DMA((3,)), pltpu.SemaphoreType.DMA((3,)), pltpu.HBM(p.shape, p.dtype),
                   pltpu.HBM(p.shape, p.dtype), jax.ShapeDtypeStruct((8, 128), F32)),
        in_specs=(_HBM, _HBM), out_specs=(_SEM, _SEM, _HBM, _HBM, pl.BlockSpec(memory_space=pltpu.VMEM)),
        input_output_aliases={0: 2, 1: 3},
        compiler_params=pltpu.CompilerParams(**_SPLIT),
    )(_hbm(p), _hbm(lax.empty(p.shape, p.dtype)))


def _a2a_wait(send_sems, recv_sems, p, q, after, tag):
    def body(p_ref, q_ref, send_sems, recv_sems, after_ref, p_out, q_out):
        x, y, c = _position()
        for j, chip in enumerate(_other_chips(x, y)):
            cp = _a2a_copy(j, chip, c, p_ref, q_ref, 2 * chip[0] + chip[1], send_sems, recv_sems)
            cp.wait_send()
            cp.wait_recv()

    return pl.pallas_call(
        body, name="grad_alltoall_wait_" + tag,
        out_shape=(pltpu.HBM(p.shape, p.dtype), pltpu.HBM(q.shape, q.dtype)),
        in_specs=(_HBM, _HBM, _SEM, _SEM, pl.BlockSpec(memory_space=pl.ANY)), out_specs=(_HBM, _HBM),
        input_output_aliases={0: 0, 1: 1},
        compiler_params=pltpu.CompilerParams(**_SPLIT),
    )(p, q, send_sems, recv_sems, after)


def _comm_rows(rows):
    return next(t for t in (512, 384, 256, 128) if rows % t == 0)


def _pair_add(gb, recv, where, tag):
    _, rows, cols = gb.shape
    hc = cols // 2
    tr = _comm_rows(rows)

    def body(w_ref, g_ref, r_ref, o_ref):
        o_ref[...] = (g_ref[...].astype(F32) + r_ref[...].astype(F32)).astype(o_ref.dtype)

    return pl.pallas_call(
        body, name="grad_pair_add_" + tag,
        grid_spec=pltpu.PrefetchScalarGridSpec(
            num_scalar_prefetch=1, grid=(4, rows // tr),
            in_specs=[pl.BlockSpec((None, tr, hc), lambda s, j, w_ref: (s, j, w_ref[0])),
                      pl.BlockSpec((None, tr, hc), lambda s, j, w_ref: (s, j, 0))],
            out_specs=pl.BlockSpec((None, tr, hc), lambda s, j, w_ref: (s, j, 0))),
        out_shape=jax.ShapeDtypeStruct((4, rows, hc), gb.dtype),
        compiler_params=_cp(("parallel", "parallel")),
    )(where, gb, recv)


def _sum_chips(p, q, where, tag):
    _, rows, hc = q.shape
    tr = _comm_rows(rows)

    def body(w_ref, p_ref, qa_ref, qb_ref, qc_ref, o_ref):
        me = w_ref[1]
        own, qa, qb, qc = (r[...].astype(F32) for r in (p_ref, qa_ref, qb_ref, qc_ref))
        v0 = jnp.where(me == 0, own, qa)
        v1 = jnp.where(me == 1, own, jnp.where(me == 0, qa, qb))
        v2 = jnp.where(me == 2, own, jnp.where(me < 2, qb, qc))
        v3 = jnp.where(me == 3, own, qc)
        o_ref[...] = ((v0 + v1) + v2) + v3

    slot = lambda k: pl.BlockSpec((None, tr, hc), lambda j, w_ref: (w_ref[k], j, 0))
    return pl.pallas_call(
        body, name="grad_sum_chips_" + tag,
        grid_spec=pltpu.PrefetchScalarGridSpec(
            num_scalar_prefetch=1, grid=(rows // tr,),
            in_specs=[slot(1), slot(2), slot(3), slot(4)],
            out_specs=pl.BlockSpec((None, tr, hc), lambda j, w_ref: (w_ref[0], j, 0))),
        out_shape=jax.ShapeDtypeStruct((2, rows, hc), F32),
        compiler_params=_cp(("parallel",)),
    )(where, p, q, q, q)


def _shard_major(g, axis):
    shape = g.shape
    g = g.reshape(shape[:axis] + (4, shape[axis] // 4) + shape[axis + 1:])
    return jnp.moveaxis(g, axis, 0).reshape(4, -1)


def _unshard(g4, shape, axis):
    n = shape[axis] // 4
    g = g4.reshape((4,) + shape[:axis] + (n,) + shape[axis + 1:])
    return jnp.moveaxis(g, 0, axis).reshape(shape)


def _split(flat, shapes):
    out, off = [], 0
    for shp in shapes:
        n = 1
        for d in shp:
            n *= d
        out.append(flat[..., off:off + n].reshape(flat.shape[:-1] + tuple(shp)))
        off += n
    return out


def _even_rows_to_kernel(wt):
    return jnp.concatenate([wt[:1536], wt[1552:3088], wt[1536:1552], wt[3088:3096],
                            jnp.zeros((PE - 3096, wt.shape[1]), wt.dtype)], axis=0)


def _block_diag(w):
    eye = jnp.eye(8, dtype=w.dtype)
    return (w[:, :, None, :] * eye[:, None, :, None]).reshape(512, 512)


def _diag_blocks(g):
    eye = jnp.eye(8, dtype=g.dtype)
    return (g.reshape(8, 64, 8, 64) * eye[:, None, :, None]).sum(axis=2)


def _shift_down(a, s):
    return a if s == 0 else jnp.pad(a, ((s, 0), (0, 0)))[:a.shape[0]]


def _shift_up(a, s):
    return a if s == 0 else jnp.pad(a, ((0, s), (0, 0)))[s:]


SMALL_SHARDED_SHAPES = [(2, 4, 256), (16, 64), (4, 128), (128,), (128,), (128,), (128,)]
REPL_SHAPES = [(256,), (512,), (8,), (8, 257), (8, 64, 64), (8, 64, 64)]


def kernel(x, norm_w, w_in_even, gla_w_a_up, gla_b_a, gla_norm_w, fox_b_f, w_out_even, w_in_odd, rel_bias, conv_w, conv_b, lru_w_a, lru_b_a, lru_w_x, lru_b_x, lru_lambda, w_out_odd, w_mlp_up, w_mlp_down, loss_target, m_norm_w, m_w_in_even, m_gla_w_a_up, m_gla_b_a, m_gla_norm_w, m_fox_b_f, m_w_out_even, m_w_in_odd, m_rel_bias, m_conv_w, m_conv_b, m_lru_w_a, m_lru_b_a, m_lru_w_x, m_lru_b_x, m_lru_lambda, m_w_out_odd, m_w_mlp_up, m_w_mlp_down, v_norm_w, v_w_in_even, v_gla_w_a_up, v_gla_b_a, v_gla_norm_w, v_fox_b_f, v_w_out_even, v_w_in_odd, v_rel_bias, v_conv_w, v_conv_b, v_lru_w_a, v_lru_b_a, v_lru_w_x, v_lru_b_x, v_lru_lambda, v_w_out_odd, v_w_mlp_up, v_w_mlp_down):
    c_idx = lax.axis_index("c")

    small_local = [norm_w, gla_w_a_up[0], conv_w[0], conv_b[0], lru_b_a[0], lru_b_x[0], lru_lambda[0]]
    small_src = jnp.concatenate([a.reshape(-1) for a in small_local]).reshape(32, 128)
    mine = [small_src, w_in_even[0].T.astype(BF16), w_out_even[0].astype(BF16), w_mlp_up.astype(BF16),
            w_mlp_down.astype(BF16), w_in_odd[0].astype(BF16), w_out_odd[0].astype(BF16)]
    ag_sems, ag_srcs, ag_lands, ag_token = _ag_start(mine)

    def gathered(g, after):
        srcs_g, lands_g = _ag_wait(g, ag_sems[g], ag_srcs, ag_lands, after)
        return _ag_forward(g, srcs_g, lands_g)

    small4, w_in_e4, w_out_e = gathered(0, ag_token)
    me = 2 * lax.axis_index("x") + lax.axis_index("y")
    others = [k + (k >= me).astype(jnp.int32) for k in range(3)]
    where = jnp.stack([c_idx, me] + others).astype(jnp.int32)

    w_in_e_t = _even_rows_to_kernel(w_in_e4.reshape(3096, D))
    g_small = _split(small4.reshape(4, 32 * 128), SMALL_SHARDED_SHAPES)
    nw_full = _unshard(g_small[0], (2, 4, 1024), 2)
    wa_up = _unshard(g_small[1], (16, 256), 1)
    cw = _unshard(g_small[2], (4, 512), 1)
    cb, lba, lbx, lam = [_unshard(g, (512,), 0).reshape(1, 512) for g in g_small[3:]]
    nw = lambda layer, i: nw_full[layer, i].reshape(1, D)

    wa_pad = jnp.pad(wa_up, ((0, 128 - 16), (0, 0)))
    gla_ba = gla_b_a.reshape(1, 256)
    gla_nw = gla_norm_w.reshape(1, 512)
    fox_bpad = jnp.pad(fox_b_f.reshape(1, 8), ((0, 0), (FOX_LANE0, 128 - FOX_LANE0 - 8)))
    rbp = jnp.pad(rel_bias[0], ((0, 0), (0, REL_PAD - 257)))
    wa_bd = _block_diag(lru_w_a[0])
    wx_bd = _block_diag(lru_w_x[0])

    x0 = x[0]
    tgt = loss_target[0]

    h0 = _prenorm(x0, nw(0, 0), "prenorm_l0_mix")
    proj_e = _mm(h0, w_in_e_t, "nt", tm=1024, tn=640, name="mm_in_even")
    cat0, s_prev = _gla_fwd(proj_e, wa_pad, gla_ba, gla_nw)
    cum_r = _fox_gate_fwd(proj_e, fox_bpad)
    cum_c = cum_r[:, FOX_LANE0:FOX_LANE0 + 8].T
    cat0 = _fox_fwd(proj_e, cum_c, cat0)
    mix0 = _mm(cat0, w_out_e, "nn", tm=1024, tn=512, name="mm_out_even")
    x1 = _postnorm(x0, mix0, nw(0, 1), "postnorm_l0_mix")
    w_up, w_dn = gathered(1, x1)
    h1 = _prenorm(x1, nw(0, 2), "prenorm_l0_mlp")
    a0, r0 = _mm(h1, w_up, "nn", tm=1024, tn=1024, b_layer=0, relu_pair=True, name="mm_up_l0")
    d0 = _mm(a0, w_dn, "nn", tm=1024, tn=512, tk=2048, b_layer=0, name="mm_down_l0")
    x2 = _postnorm(x1, d0, nw(0, 3), "postnorm_l0_mlp")

    w_in_o, w_out_o = gathered(2, x2)
    h2 = _prenorm(x2, nw(1, 0), "prenorm_l1_mix")
    proj_o = _mm(h2, w_in_o, "nn", tm=1024, tn=640, name="mm_in_odd")
    bias_q = _bias_build(rbp)
    bias = bias_q.transpose(1, 0, 2)
    kvpad = jnp.pad(proj_o[:, 512:1536], ((CA_PAD, 0), (0, 0)))
    cat1 = _ca_fwd(proj_o, kvpad, bias)
    x_in = proj_o[:, 2048:2560]
    xs = jnp.stack([_shift_down(x_in, 3 - j) for j in range(4)])
    lru_a, lru_b = _lru_pre_fwd(xs, cw, cb, wa_bd, lba, wx_bd, lbx, lam)
    hh = _lru_scan_fwd(lru_a, lru_b)
    cat1 = _lru_post_fwd(hh, proj_o, cat1)
    mix1 = _mm(cat1, w_out_o, "nn", tm=1024, tn=512, name="mm_out_odd")
    x3 = _postnorm(x2, mix1, nw(1, 1), "postnorm_l1_mix")
    h3 = _prenorm(x3, nw(1, 2), "prenorm_l1_mlp")
    a1, r1 = _mm(h3, w_up, "nn", tm=1024, tn=1024, b_layer=1, relu_pair=True, name="mm_up_l1")
    d1 = _mm(a1, w_dn, "nn", tm=1024, tn=512, tk=2048, b_layer=1, name="mm_down_l1")
    x4 = _postnorm(x3, d1, nw(1, 3), "postnorm_l1_mlp")

    g4, loss_part = _loss_and_grad(x4, tgt)
    loss = lax.psum(loss_part[0, 0], ("x", "y", "c"))

    def rs_begin(gb, tag):
        pair_sum = _pair_add(gb, _pair_swap_cols(gb, tag), where, tag)
        return _a2a_start(pair_sum, tag)

    def rs_end(started, after, tag):
        send_sems, recv_sems, p, q, _ = started
        p, q = _a2a_wait(send_sems, recv_sems, p, q, after, tag)
        halves = _handover(_sum_chips(p, q, where, tag), tag)
        return jnp.concatenate([halves[0], halves[1]], axis=1)

    gba = jnp.zeros((4, GA_ROWS, D), BF16)
    dd1, dnw13 = _norm_bwd(d1, nw(1, 3), g4, None, "postnorm_l1_mlp_bwd")
    gba = _mm(a1, dd1, "tn", tm=512, tn=1024, into=(gba, 1024, GA_DN), name="mm_down_l1_dw")
    du1 = _mm(dd1, w_dn, "nt", tm=1024, tn=1024, b_layer=1, times2=r1, out_dtype=BF16, name="mm_down_l1_dx")
    gba = _mm(du1, h3, "tn", tm=512, tn=1024, into=(gba, 1024, GA_UP), name="mm_up_l1_dw")
    dh3 = _mm(du1, w_up, "nt", tm=1024, tn=512, tk=2048, b_layer=1, name="mm_up_l1_dx")
    g3, dnw12 = _norm_bwd(x3, nw(1, 2), dh3, g4, "prenorm_l1_mlp_bwd")
    dmix1, dnw11 = _norm_bwd(mix1, nw(1, 1), g3, None, "postnorm_l1_mix_bwd")
    gba = _mm(cat1, dmix1, "tn", tm=128, tn=1024, into=(gba, 256, GA_OUT_O), name="mm_out_odd_dw")
    dcat1 = _mm(dmix1, w_out_o, "nt", tm=1024, tn=512, name="mm_out_odd_dx")

    dq_c, dkpad, dvpad, dbias = _ca_bwd(proj_o, kvpad, bias, dcat1)
    g_rel = _bias_grad(dbias.transpose(1, 0, 2))[:, :257]
    dhh, dgate = _lru_post_bwd(hh, proj_o, dcat1)
    da_l, db_l = _lru_scan_bwd(lru_a, hh, dhh)
    dxs, g_cw, g_cb, g_wa_bd, g_lba, g_wx_bd, g_lbx, g_lam = _lru_pre_bwd(xs, cw, cb, wa_bd, lba, wx_bd, lbx, lam, da_l, db_l)
    dx_in = _conv_dx(jnp.stack([_shift_up(dxs[j], 3 - j) for j in range(4)]))
    dproj_o = jnp.concatenate([dq_c, dkpad[CA_PAD:], dvpad[CA_PAD:], dgate, dx_in], axis=1)
    gba = _mm(dproj_o, h2, "tn", tm=128, tn=1024, into=(gba, 640, GA_IN_O), name="mm_in_odd_dw")
    rs_a = rs_begin(gba, "a")
    dh2 = _mm(dproj_o, w_in_o, "nt", tm=1024, tn=512, tk=1280, name="mm_in_odd_dx")
    g2, dnw10 = _norm_bwd(x2, nw(1, 0) + rs_a[4][0, 0], dh2, g3, "prenorm_l1_mix_bwd")

    gbb = lax.empty((4, GB_ROWS, D), BF16)
    dd0, dnw03 = _norm_bwd(d0, nw(0, 3), g2, None, "postnorm_l0_mlp_bwd")
    gbb = _mm(a0, dd0, "tn", tm=512, tn=1024, into=(gbb, 1024, GB_DN), name="mm_down_l0_dw")
    du0 = _mm(dd0, w_dn, "nt", tm=1024, tn=1024, b_layer=0, times2=r0, out_dtype=BF16, name="mm_down_l0_dx")
    gbb = _mm(du0, h1, "tn", tm=512, tn=1024, into=(gbb, 1024, GB_UP), name="mm_up_l0_dw")
    rs_b = rs_begin(gbb, "b")
    dh1 = _mm(du0, w_up, "nt", tm=1024, tn=512, tk=2048, b_layer=0, name="mm_up_l0_dx")
    g1, dnw02 = _norm_bwd(x1, nw(0, 2) + rs_b[4][0, 0], dh1, g2, "prenorm_l0_mlp_bwd")
    dmix0, dnw01 = _norm_bwd(mix0, nw(0, 1), g1, None, "postnorm_l0_mix_bwd")
    gbc = lax.empty((4, GC_ROWS, D), BF16)
    gbc = _mm(cat0, dmix0, "tn", tm=128, tn=1024, into=(gbc, 256, GC_OUT_E), name="mm_out_even_dw")
    dcat0 = _mm(dmix0, w_out_e, "nt", tm=1024, tn=512, name="mm_out_even_dx")

    dq_g, dk_g, dv_g, dr_g, daux_g, g_wa_pad, g_gla_ba, g_gla_nw = _gla_bwd(proj_e, s_prev, wa_pad, gla_ba, gla_nw, dcat0)
    dq_f, dk_f, dv_f, dccol = _fox_bwd(proj_e, cum_c, dcat0)
    dccol_t = jnp.pad(dccol.sum(axis=0).T, ((0, 0), (FOX_LANE0, 128 - FOX_LANE0 - 8)))
    daux, g_fox_bpad = _fox_gate_bwd(proj_e, fox_bpad, dccol_t, daux_g)
    dproj_e = jnp.concatenate([dq_g, dk_g, dv_g, dr_g, dq_f, dk_f, dv_f, daux], axis=1)
    gt_in_e = _mm(dproj_e, h0, "tn", tm=640, tn=1024, out_dtype=BF16, name="mm_in_even_dw")
    dh0 = _mm(dproj_e, w_in_e_t, "nn", tm=1024, tn=512, tk=640, name="mm_in_even_dx")
    grad_x, dnw00 = _norm_bwd(x0, nw(0, 0), dh0, g1, "prenorm_l0_mix_bwd")

    g_norm = jnp.stack([jnp.concatenate([dnw00, dnw01, dnw02, dnw03]), jnp.concatenate([dnw10, dnw11, dnw12, dnw13])])
    sharded = [(g_norm, 2), (g_wa_pad[:16], 1), (g_cw, 1), (g_cb[0], 0), (g_lba[0], 0), (g_lbx[0], 0), (g_lam[0], 0)]
    replicated = [g_gla_ba[0], g_gla_nw[0], g_fox_bpad[0, FOX_LANE0:FOX_LANE0 + 8], g_rel, _diag_blocks(g_wa_bd),
                  _diag_blocks(g_wx_bd)]
    small4 = jnp.concatenate([_shard_major(g, ax) for g, ax in sharded]
                             + [jnp.broadcast_to(g.reshape(1, -1), (4, g.size)) for g in replicated], axis=1)
    n_small = small4.shape[1]
    small_rows = GC_ROWS - GC_TAIL - 774
    small4 = jnp.pad(small4, ((0, 0), (0, small_rows * D - n_small))).reshape(4, small_rows, D)
    gt_rows = jnp.concatenate([gt_in_e[:1536], gt_in_e[3072:3088], gt_in_e[1536:3072], gt_in_e[3088:3096]], axis=0)
    tail = jnp.concatenate([gt_rows.reshape(4, 774, D), small4.astype(BF16)], axis=1)
    gbc = lax.dynamic_update_slice(gbc, tail, (0, GC_TAIL, 0))
    rs_c = rs_begin(gbc, "c")

    red_a = rs_end(rs_a, rs_c[4], "a")
    red_b = rs_end(rs_b, red_a, "b")
    red_c = rs_end(rs_c, red_b, "c")

    g_up = jnp.stack([red_b[GB_UP:GB_UP + 1024].T, red_a[GA_UP:GA_UP + 1024].T])
    g_dn = jnp.stack([red_b[GB_DN:GB_DN + 1024], red_a[GA_DN:GA_DN + 1024]])
    g_small = _split(red_c[GC_TAIL + 774:].reshape(-1)[:n_small], SMALL_SHARDED_SHAPES + REPL_SHAPES)
    g_of = dict(zip(["norm_w", "gla_w_a_up", "conv_w", "conv_b", "lru_b_a", "lru_b_x", "lru_lambda", "gla_b_a",
                     "gla_norm_w", "fox_b_f", "rel_bias", "lru_w_a", "lru_w_x"], g_small))
    g_of.update(w_mlp_up=g_up, w_mlp_down=g_dn, w_in_odd=red_a[GA_IN_O:GA_IN_O + 640].T,
                w_out_even=red_c[GC_OUT_E:GC_OUT_E + 256], w_out_odd=red_a[GA_OUT_O:GA_OUT_O + 256],
                w_in_even=red_c[GC_TAIL:GC_TAIL + 774])

    names = ["norm_w", "w_in_even", "gla_w_a_up", "gla_b_a", "gla_norm_w", "fox_b_f", "w_out_even", "w_in_odd", "rel_bias",
             "conv_w", "conv_b", "lru_w_a", "lru_b_a", "lru_w_x", "lru_b_x", "lru_lambda", "w_out_odd", "w_mlp_up",
             "w_mlp_down"]
    w_of = dict(norm_w=norm_w, w_in_even=w_in_even, gla_w_a_up=gla_w_a_up, gla_b_a=gla_b_a, gla_norm_w=gla_norm_w,
                fox_b_f=fox_b_f, w_out_even=w_out_even, w_in_odd=w_in_odd, rel_bias=rel_bias, conv_w=conv_w, conv_b=conv_b,
                lru_w_a=lru_w_a, lru_b_a=lru_b_a, lru_w_x=lru_w_x, lru_b_x=lru_b_x, lru_lambda=lru_lambda,
                w_out_odd=w_out_odd, w_mlp_up=w_mlp_up, w_mlp_down=w_mlp_down)
    m_of = dict(norm_w=m_norm_w, w_in_even=m_w_in_even, gla_w_a_up=m_gla_w_a_up, gla_b_a=m_gla_b_a,
                gla_norm_w=m_gla_norm_w, fox_b_f=m_fox_b_f, w_out_even=m_w_out_even, w_in_odd=m_w_in_odd,
                rel_bias=m_rel_bias, conv_w=m_conv_w, conv_b=m_conv_b, lru_w_a=m_lru_w_a, lru_b_a=m_lru_b_a,
                lru_w_x=m_lru_w_x, lru_b_x=m_lru_b_x, lru_lambda=m_lru_lambda, w_out_odd=m_w_out_odd,
                w_mlp_up=m_w_mlp_up, w_mlp_down=m_w_mlp_down)
    v_of = dict(norm_w=v_norm_w, w_in_even=v_w_in_even, gla_w_a_up=v_gla_w_a_up, gla_b_a=v_gla_b_a,
                gla_norm_w=v_gla_norm_w, fox_b_f=v_fox_b_f, w_out_even=v_w_out_even, w_in_odd=v_w_in_odd,
                rel_bias=v_rel_bias, conv_w=v_conv_w, conv_b=v_conv_b, lru_w_a=v_lru_w_a, lru_b_a=v_lru_b_a,
                lru_w_x=v_lru_w_x, lru_b_x=v_lru_b_x, lru_lambda=v_lru_lambda, w_out_odd=v_w_out_odd,
                w_mlp_up=v_w_mlp_up, w_mlp_down=v_w_mlp_down)
    grads, deltas, new_ms, new_vs = [], [], [], []
    for n in names:
        w = w_of[n]
        if n == "w_in_even":
            to_view = lambda a: a[0].T
            from_view = lambda a: a.T[None]
        else:
            view = w.shape if w.ndim <= 3 else w.shape[-3:]
            to_view = lambda a, view=view: a.reshape(view)
            from_view = lambda a, w=w: a.reshape(w.shape)
        g = g_of[n] if n == "w_in_even" else to_view(g_of[n])
        d, mn, vn = _adamw(to_view(w), g, to_view(m_of[n]), to_view(v_of[n]), "adamw_" + n)
        grads.append(from_view(g))
        deltas.append(from_view(d))
        new_ms.append(from_view(mn))
        new_vs.append(from_view(vn))

    return (loss, grad_x.reshape(1, T, D), *grads, *deltas, *new_ms, *new_vs)
```

```python
import functools

import jax
import jax.numpy as jnp
from jax import lax
from jax.experimental import pallas as pl
from jax.experimental.pallas import tpu as pltpu

F32 = jnp.float32
BF16 = jnp.bfloat16
MESH = pl.DeviceIdType.MESH

T = 2048
D = 1024
DFF = 4096
EPS = 1e-6
CHUNK = 64
NCHUNK = T // CHUNK
PE = 3200
PO = 2560
AUX_BLK = 3072 // 128
FOX_LANE0 = 16
GLA_SCALE = 64 ** -0.5
ATT_SCALE = 64 ** -0.5
NEG = float(jnp.finfo(jnp.float32).min)
CA_BAND = 576
CA_PAD = 512
REL_PAD = 384

VMEM_LIMIT = 48 * 1024 * 1024

ADAM_LR, ADAM_B1, ADAM_B2, ADAM_EPS, ADAM_WD, ADAM_STEP = 0.001, 0.9, 0.999, 1e-08, 0.01, 10

GA_ROWS, GA_UP, GA_DN, GA_IN_O, GA_OUT_O = 3072, 0, 1024, 2048, 2688
GB_ROWS, GB_UP, GB_DN = 2048, 0, 1024
GC_ROWS, GC_OUT_E, GC_TAIL = 1152, 0, 256

_DIMS = {"nn": (((1,), (0,)), ((), ())), "nt": (((1,), (1,)), ((), ())), "tn": (((0,), (0,)), ((), ()))}


def _cp(sem, **kw):
    return pltpu.CompilerParams(dimension_semantics=sem, vmem_limit_bytes=VMEM_LIMIT, **kw)


def _dot(a, b, mode):
    return lax.dot_general(a.astype(BF16), b.astype(BF16), _DIMS[mode], preferred_element_type=F32)


@functools.partial(jax.custom_vjp, nondiff_argnums=(2,))
def bdot(a, b, mode):
    return _dot(a, b, mode)


def _bdot_fwd(a, b, mode):
    return _dot(a, b, mode), (a, b)


def _bdot_bwd(mode, res, g):
    a, b = res
    if mode == "nn":
        da, db = _dot(g, b, "nt"), _dot(a, g, "tn")
    elif mode == "nt":
        da, db = _dot(g, b, "nn"), _dot(g, a, "tn")
    else:
        da, db = _dot(b, g, "nt"), _dot(a, g, "nn")
    return da.astype(a.dtype), db.astype(b.dtype)


bdot.defvjp(_bdot_fwd, _bdot_bwd)


def _hdot_raw(a, b, mode):
    return lax.dot_general(a, b, _DIMS[mode], precision=lax.Precision.HIGHEST, preferred_element_type=F32)


@functools.partial(jax.custom_vjp, nondiff_argnums=(2,))
def hdot(a, b, mode):
    return _hdot_raw(a, b, mode)


def _hdot_fwd(a, b, mode):
    return _hdot_raw(a, b, mode), (a, b)


def _hdot_bwd(mode, res, g):
    a, b = res
    if mode == "nn":
        return _hdot_raw(g, b, "nt"), _hdot_raw(a, g, "tn")
    if mode == "nt":
        return _hdot_raw(g, b, "nn"), _hdot_raw(g, a, "tn")
    return _hdot_raw(b, g, "nt"), _hdot_raw(a, g, "nn")


hdot.defvjp(_hdot_fwd, _hdot_bwd)


def _log_sigmoid(x):
    return jnp.minimum(x, 0.0) - jnp.log(1.0 + jnp.exp(-jnp.abs(x)))


def _sigmoid(x):
    return 1.0 / (1.0 + jnp.exp(-x))


def _expm1(x):
    series = x * (1.0 + x * 0.5 * (1.0 + x * (1.0 / 3.0) * (1.0 + x * 0.25)))
    return jnp.where(jnp.abs(x) < 0.03, series, jnp.exp(x) - 1.0)


def _gelu_tanh(x):
    return 0.5 * x * (1.0 + jnp.tanh(0.7978845608028654 * (x + 0.044715 * x * x * x)))


def _softmax_rows(s):
    m = jnp.max(s, axis=-1, keepdims=True)
    p = jnp.exp(s - m)
    return p / jnp.sum(p, axis=-1, keepdims=True)


def _iota(shape, dim):
    return lax.broadcasted_iota(jnp.int32, shape, dim)


def _mm(a, b, mode, *, tm, tn, tk=None, out_dtype=F32, name, b_layer=None, into=None, relu_pair=False, times2=None):
    b2 = b.shape[-2:]
    if mode == "nn":
        (m, k), n = a.shape, b2[1]
    elif mode == "nt":
        (m, k), n = a.shape, b2[0]
    else:
        (k, m), n = a.shape, b2[1]
    tk = k if tk is None else tk
    assert m % tm == 0 and n % tn == 0 and k % tk == 0, (name, a.shape, b.shape)
    nk = k // tk
    a_spec = {"nn": pl.BlockSpec((tm, tk), lambda i, j, kk: (i, kk)),
              "nt": pl.BlockSpec((tm, tk), lambda i, j, kk: (i, kk)),
              "tn": pl.BlockSpec((tk, tm), lambda i, j, kk: (kk, i))}[mode]
    b_blk = {"nn": (tk, tn), "nt": (tn, tk), "tn": (tk, tn)}[mode]
    b_idx = {"nn": lambda i, j, kk: (kk, j), "nt": lambda i, j, kk: (j, kk), "tn": lambda i, j, kk: (kk, j)}[mode]
    if b_layer is None:
        b_spec = pl.BlockSpec(b_blk, b_idx)
    else:
        b_spec = pl.BlockSpec((None,) + b_blk, lambda i, j, kk: (b_layer,) + b_idx(i, j, kk))

    tile = pl.BlockSpec((tm, tn), lambda i, j, kk: (i, j))
    if into is not None:
        buf, per_slot, row_off = into
        assert m == 4 * per_slot and per_slot % tm == 0 and row_off % tm == 0 and buf.shape[2] == n, (name, buf.shape)
        bps = per_slot // tm
        out_specs = pl.BlockSpec((None, tm, tn), lambda i, j, kk: (i // bps, row_off // tm + i % bps, j))
        out_shape = jax.ShapeDtypeStruct(buf.shape, buf.dtype)
        extra_in, extra_specs, aliases = [buf], [pl.BlockSpec(memory_space=pl.ANY)], {2: 0}
        finish = lambda acc, extra: [acc.astype(buf.dtype)]
    elif relu_pair:
        out_specs = (tile, tile)
        out_shape = (jax.ShapeDtypeStruct((m, n), BF16),) * 2
        extra_in, extra_specs, aliases = [], [], {}

        def finish(acc, extra):
            r = jnp.maximum(acc, 0.0)
            return [(r * r).astype(BF16), r.astype(BF16)]
    elif times2 is not None:
        out_specs = tile
        out_shape = jax.ShapeDtypeStruct((m, n), out_dtype)
        extra_in, extra_specs, aliases = [times2], [tile], {}
        finish = lambda acc, extra: [(acc * (2.0 * extra[...].astype(F32))).astype(out_dtype)]
    else:
        out_specs = tile
        out_shape = jax.ShapeDtypeStruct((m, n), out_dtype)
        extra_in, extra_specs, aliases = [], [], {}
        finish = lambda acc, extra: [acc.astype(out_dtype)]
    n_out = 2 if relu_pair else 1

    def body(*refs):
        a_ref, b_ref = refs[0], refs[1]
        extra = refs[2] if extra_in else None
        o_refs = refs[2 + len(extra_in):2 + len(extra_in) + n_out]

        def store(acc):
            for o_ref, val in zip(o_refs, finish(acc, extra)):
                o_ref[...] = val

        if nk == 1:
            store(_dot(a_ref[...], b_ref[...], mode))
            return
        acc_ref = refs[-1]
        kk = pl.program_id(2)

        @pl.when(kk == 0)
        def _():
            acc_ref[...] = jnp.zeros_like(acc_ref)

        acc_ref[...] += _dot(a_ref[...], b_ref[...], mode)

        @pl.when(kk == nk - 1)
        def _():
            store(acc_ref[...])

    return pl.pallas_call(
        body, name=name, grid=(m // tm, n // tn, nk),
        in_specs=[a_spec, b_spec] + extra_specs,
        out_specs=out_specs, out_shape=out_shape,
        scratch_shapes=[pltpu.VMEM((tm, tn), F32)] if nk > 1 else [],
        input_output_aliases=aliases,
        compiler_params=_cp(("parallel", "parallel", "arbitrary")),
    )(a, b, *extra_in)


ROWS = 256


def _prenorm(x, w, name):
    def body(x_ref, w_ref, o_ref):
        xv = x_ref[...]
        r = lax.rsqrt(jnp.mean(xv * xv, axis=-1, keepdims=True) + EPS)
        o_ref[...] = (xv * r * w_ref[...]).astype(BF16)

    return pl.pallas_call(
        body, name=name, grid=(T // ROWS,),
        in_specs=[pl.BlockSpec((ROWS, D), lambda i: (i, 0)), pl.BlockSpec((1, D), lambda i: (0, 0))],
        out_specs=pl.BlockSpec((ROWS, D), lambda i: (i, 0)),
        out_shape=jax.ShapeDtypeStruct((T, D), BF16),
        compiler_params=_cp(("parallel",)),
    )(x, w)


def _postnorm(x, z, w, name):
    def body(x_ref, z_ref, w_ref, o_ref):
        zv = z_ref[...]
        r = lax.rsqrt(jnp.mean(zv * zv, axis=-1, keepdims=True) + EPS)
        o_ref[...] = x_ref[...] + zv * r * w_ref[...]

    return pl.pallas_call(
        body, name=name, grid=(T // ROWS,),
        in_specs=[pl.BlockSpec((ROWS, D), lambda i: (i, 0)), pl.BlockSpec((ROWS, D), lambda i: (i, 0)),
                  pl.BlockSpec((1, D), lambda i: (0, 0))],
        out_specs=pl.BlockSpec((ROWS, D), lambda i: (i, 0)),
        out_shape=jax.ShapeDtypeStruct((T, D), F32),
        compiler_params=_cp(("parallel",)),
    )(x, z, w)


def _norm_bwd(z, w, dy, add, name):
    has_add = add is not None

    def body(*refs):
        if has_add:
            z_ref, w_ref, dy_ref, add_ref, dz_ref, dw_ref = refs
        else:
            z_ref, w_ref, dy_ref, dz_ref, dw_ref = refs
        i = pl.program_id(0)

        @pl.when(i == 0)
        def _():
            dw_ref[...] = jnp.zeros_like(dw_ref)

        zv = z_ref[...].astype(F32)
        dyv = dy_ref[...]
        r = lax.rsqrt(jnp.mean(zv * zv, axis=-1, keepdims=True) + EPS)
        wdy = dyv * w_ref[...]
        dz = r * wdy - zv * (r * r * r) * jnp.mean(zv * wdy, axis=-1, keepdims=True)
        if has_add:
            dz = dz + add_ref[...]
        dz_ref[...] = dz.astype(dz_ref.dtype)
        dw_ref[...] += jnp.sum(dyv * zv * r, axis=0, keepdims=True)

    row = pl.BlockSpec((ROWS, D), lambda i: (i, 0))
    vec = pl.BlockSpec((1, D), lambda i: (0, 0))
    ins = [z, w, dy] + ([add] if has_add else [])
    dz_dtype = F32 if has_add else BF16
    return pl.pallas_call(
        body, name=name, grid=(T // ROWS,),
        in_specs=[row, vec, row] + ([row] if has_add else []),
        out_specs=(row, vec),
        out_shape=(jax.ShapeDtypeStruct((T, D), dz_dtype), jax.ShapeDtypeStruct((1, D), F32)),
        compiler_params=_cp(("arbitrary",)),
    )(*ins)


def _loss_and_grad(y, tgt):
    def body(y_ref, t_ref, g_ref, l_ref):
        i = pl.program_id(0)

        @pl.when(i == 0)
        def _():
            l_ref[...] = jnp.zeros_like(l_ref)

        e = y_ref[...] - t_ref[...]
        g_ref[...] = e * (1.0 / D)
        l_ref[...] += jnp.sum(e * e) * (0.5 / D)

    row = pl.BlockSpec((ROWS, D), lambda i: (i, 0))
    return pl.pallas_call(
        body, name="loss_head", grid=(T // ROWS,), in_specs=[row, row],
        out_specs=(row, pl.BlockSpec((1, 128), lambda i: (0, 0))),
        out_shape=(jax.ShapeDtypeStruct((T, D), F32), jax.ShapeDtypeStruct((1, 128), F32)),
        compiler_params=_cp(("arbitrary",)),
    )(y, tgt)


def _adamw(w, g, m, v, name):
    lead = w.shape[:-2]
    assert len(lead) <= 1 and g.shape == w.shape, (name, w.shape, g.shape)
    rows, cols = w.shape[-2:]
    if rows <= 512:
        tr, tc = rows, cols
    elif rows % 256 == 0:
        tr, tc = 256, cols
    else:
        tr, tc = rows, 256
    assert rows % tr == 0 and cols % tc == 0, (name, w.shape)
    c1 = 1.0 - ADAM_B1 ** ADAM_STEP
    c2 = 1.0 - ADAM_B2 ** ADAM_STEP

    def body(w_ref, g_ref, m_ref, v_ref, d_ref, mo_ref, vo_ref):
        gv = g_ref[...]
        mn = ADAM_B1 * m_ref[...] + (1.0 - ADAM_B1) * gv
        vn = ADAM_B2 * v_ref[...] + (1.0 - ADAM_B2) * (gv * gv)
        m_hat = mn / c1
        v_hat = vn / c2
        d_ref[...] = -ADAM_LR * (m_hat / (jnp.sqrt(v_hat) + ADAM_EPS) + ADAM_WD * w_ref[...])
        mo_ref[...] = mn
        vo_ref[...] = vn

    if lead:
        grid = (lead[0], rows // tr, cols // tc)
        blk = pl.BlockSpec((None, tr, tc), lambda l, i, j: (l, i, j))
    else:
        grid = (rows // tr, cols // tc)
        blk = pl.BlockSpec((tr, tc), lambda i, j: (i, j))
    sds = jax.ShapeDtypeStruct(w.shape, F32)
    return pl.pallas_call(body, name=name, grid=grid, in_specs=[blk] * 4, out_specs=(blk,) * 3,
                          out_shape=(sds,) * 3, compiler_params=_cp(("parallel",) * len(grid)))(w, g, m, v)


def _gla_consts():
    ltri = (_iota((CHUNK, CHUNK), 0) >= _iota((CHUNK, CHUNK), 1)).astype(F32)
    ones_c = jnp.ones((CHUNK, 128), F32)
    mask = (_iota((256, 512), 0) // 64 == _iota((256, 512), 1) // 128).astype(F32)
    return ltri, ones_c, mask


def _gla_chunk(consts, q, k, v, r, aux, s_prev, wa, ba, nw):
    ltri, ones_c, mask = consts
    la = _log_sigmoid(bdot(aux, wa, "nn") + ba) * (1.0 / 16.0)
    cum = hdot(ltri, la, "nn")
    total = jnp.sum(la, axis=0, keepdims=True)
    k_dec = k * jnp.exp(total - cum)
    inc = bdot(k_dec, v, "tn") * mask
    dec = jnp.exp(hdot(la, ones_c, "tn"))
    dec = jnp.concatenate([dec, dec, dec, dec], axis=1)
    s_new = dec * s_prev + inc
    o = bdot(q * GLA_SCALE, s_new, "nn")
    parts = []
    for h in range(4):
        oh = o[:, h * 128:(h + 1) * 128]
        parts.append(oh * lax.rsqrt(jnp.mean(oh * oh, axis=-1, keepdims=True) + EPS))
    on = jnp.concatenate(parts, axis=1)
    return s_new, on * nw * (r * _sigmoid(r))


def _gla_specs(cmap):
    return [pl.BlockSpec((CHUNK, 256), lambda c: (cmap(c), 0)),
            pl.BlockSpec((CHUNK, 256), lambda c: (cmap(c), 1)),
            pl.BlockSpec((CHUNK, 512), lambda c: (cmap(c), 1)),
            pl.BlockSpec((CHUNK, 512), lambda c: (cmap(c), 2)),
            pl.BlockSpec((CHUNK, 128), lambda c: (cmap(c), AUX_BLK))]


def _gla_fwd(proj, wa, ba, nw):
    def body(q_ref, k_ref, v_ref, r_ref, aux_ref, wa_ref, ba_ref, nw_ref, o_ref, sp_ref, s_ref):
        c = pl.program_id(0)

        @pl.when(c == 0)
        def _():
            s_ref[...] = jnp.zeros_like(s_ref)

        s_prev = s_ref[...]
        sp_ref[...] = s_prev
        s_new, out = _gla_chunk(_gla_consts(), q_ref[...], k_ref[...], v_ref[...], r_ref[...], aux_ref[...],
                                s_prev, wa_ref[...], ba_ref[...], nw_ref[...])
        s_ref[...] = s_new
        o_ref[...] = out

    full = lambda shape: pl.BlockSpec(shape, lambda c: (0,) * len(shape))
    return pl.pallas_call(
        body, name="gla_fwd", grid=(NCHUNK,),
        in_specs=_gla_specs(lambda c: c) + [full((128, 256)), full((1, 256)), full((1, 512))],
        out_specs=(pl.BlockSpec((CHUNK, 512), lambda c: (c, 0)), pl.BlockSpec((None, 256, 512), lambda c: (c, 0, 0))),
        out_shape=(jax.ShapeDtypeStruct((T, D), F32), jax.ShapeDtypeStruct((NCHUNK, 256, 512), F32)),
        scratch_shapes=[pltpu.VMEM((256, 512), F32)],
        compiler_params=_cp(("arbitrary",)),
    )(proj, proj, proj, proj, proj, wa, ba, nw)


def _gla_bwd(proj, s_prev_all, wa, ba, nw, dcat):
    rev = lambda c: NCHUNK - 1 - c

    def body(q_ref, k_ref, v_ref, r_ref, aux_ref, sp_ref, wa_ref, ba_ref, nw_ref, do_ref,
             dq_ref, dk_ref, dv_ref, dr_ref, daux_ref, dwa_ref, dba_ref, dnw_ref, ds_ref):
        c = pl.program_id(0)

        @pl.when(c == 0)
        def _():
            ds_ref[...] = jnp.zeros_like(ds_ref)
            dwa_ref[...] = jnp.zeros_like(dwa_ref)
            dba_ref[...] = jnp.zeros_like(dba_ref)
            dnw_ref[...] = jnp.zeros_like(dnw_ref)

        fn = functools.partial(_gla_chunk, _gla_consts())
        _, vjp = jax.vjp(fn, q_ref[...], k_ref[...], v_ref[...], r_ref[...], aux_ref[...], sp_ref[...],
                         wa_ref[...], ba_ref[...], nw_ref[...])
        dq, dk, dv, dr, daux, dsp, dwa, dba, dnw = vjp((ds_ref[...], do_ref[...]))
        dq_ref[...] = dq
        dk_ref[...] = dk
        dv_ref[...] = dv
        dr_ref[...] = dr
        daux_ref[...] = daux
        ds_ref[...] = dsp
        dwa_ref[...] += dwa
        dba_ref[...] += dba
        dnw_ref[...] += dnw

    full = lambda shape: pl.BlockSpec(shape, lambda c: (0,) * len(shape))
    blk = lambda w: pl.BlockSpec((CHUNK, w), lambda c: (rev(c), 0))
    sds = lambda *s: jax.ShapeDtypeStruct(s, F32)
    return pl.pallas_call(
        body, name="gla_bwd", grid=(NCHUNK,),
        in_specs=_gla_specs(rev) + [pl.BlockSpec((None, 256, 512), lambda c: (rev(c), 0, 0)),
                                    full((128, 256)), full((1, 256)), full((1, 512)), blk(512)],
        out_specs=(blk(256), blk(256), blk(512), blk(512), blk(128), full((128, 256)), full((1, 256)), full((1, 512))),
        out_shape=(sds(T, 256), sds(T, 256), sds(T, 512), sds(T, 512), sds(T, 128),
                   sds(128, 256), sds(1, 256), sds(1, 512)),
        scratch_shapes=[pltpu.VMEM((256, 512), F32)],
        compiler_params=_cp(("arbitrary",)),
    )(proj, proj, proj, proj, proj, s_prev_all, wa, ba, nw, dcat)


GATE_ROWS = 128


def _fox_gate_block(ltri, aux, bpad, carry):
    lf = _log_sigmoid(aux + bpad)
    cum = hdot(ltri, lf, "nn") + carry
    return cum, carry + jnp.sum(lf, axis=0, keepdims=True)


def _gate_ltri():
    return (_iota((GATE_ROWS, GATE_ROWS), 0) >= _iota((GATE_ROWS, GATE_ROWS), 1)).astype(F32)


def _fox_gate_fwd(proj, bpad):
    def body(aux_ref, b_ref, cum_ref, carry_ref):
        i = pl.program_id(0)

        @pl.when(i == 0)
        def _():
            carry_ref[...] = jnp.zeros_like(carry_ref)

        cum, carry = _fox_gate_block(_gate_ltri(), aux_ref[...], b_ref[...], carry_ref[...])
        cum_ref[...] = cum
        carry_ref[...] = carry

    return pl.pallas_call(
        body, name="fox_gate_fwd", grid=(T // GATE_ROWS,),
        in_specs=[pl.BlockSpec((GATE_ROWS, 128), lambda i: (i, AUX_BLK)), pl.BlockSpec((1, 128), lambda i: (0, 0))],
        out_specs=pl.BlockSpec((GATE_ROWS, 128), lambda i: (i, 0)),
        out_shape=jax.ShapeDtypeStruct((T, 128), F32),
        scratch_shapes=[pltpu.VMEM((1, 128), F32)],
        compiler_params=_cp(("arbitrary",)),
    )(proj, bpad)


def _fox_gate_bwd(proj, bpad, dccol_t, daux_gla):
    nb = T // GATE_ROWS
    rev = lambda i: nb - 1 - i

    def body(aux_ref, b_ref, dc_ref, dg_ref, daux_ref, db_ref, dcarry_ref):
        i = pl.program_id(0)

        @pl.when(i == 0)
        def _():
            dcarry_ref[...] = jnp.zeros_like(dcarry_ref)
            db_ref[...] = jnp.zeros_like(db_ref)

        dcum = dc_ref[...]
        fn = functools.partial(_fox_gate_block, _gate_ltri())
        _, vjp = jax.vjp(fn, aux_ref[...], b_ref[...], jnp.zeros((1, 128), F32))
        daux, db, dcarry = vjp((dcum, dcarry_ref[...]))
        daux_ref[...] = daux + dg_ref[...]
        db_ref[...] += db
        dcarry_ref[...] = dcarry

    blk = pl.BlockSpec((GATE_ROWS, 128), lambda i: (rev(i), 0))
    vec = pl.BlockSpec((1, 128), lambda i: (0, 0))
    return pl.pallas_call(
        body, name="fox_gate_bwd", grid=(nb,),
        in_specs=[pl.BlockSpec((GATE_ROWS, 128), lambda i: (rev(i), AUX_BLK)), vec, blk, blk],
        out_specs=(blk, vec),
        out_shape=(jax.ShapeDtypeStruct((T, 128), F32), jax.ShapeDtypeStruct((1, 128), F32)),
        scratch_shapes=[pltpu.VMEM((1, 128), F32)],
        compiler_params=_cp(("arbitrary",)),
    )(proj, bpad, dccol_t, daux_gla)


FOX_Q = 128


FOX_QB = T // FOX_Q


@jax.custom_vjp
def _attend(s, v):
    return _attend_fwd(s, v)[0]


def _attend_fwd(s, v):
    e = jnp.exp(s - jnp.max(s, axis=-1, keepdims=True))
    r = 1.0 / jnp.sum(e, axis=-1, keepdims=True)
    return _dot(e, v, "nn") * r, (e, r, v)


def _attend_bwd(res, do):
    e, r, v = res
    do_r = do * r
    dpr = _dot(do_r, v, "nt")
    ds = e * (dpr - r * jnp.sum(e * dpr, axis=-1, keepdims=True))
    return ds, _dot(e, do_r, "tn").astype(v.dtype)


_attend.defvjp(_attend_fwd, _attend_bwd)


def _fox_block(hp, q, k, v, ccol):
    kl = k.shape[0]
    lane = _iota((FOX_Q, 128), 1)
    tri = _iota((FOX_Q, FOX_Q), 0) >= _iota((FOX_Q, FOX_Q), 1)
    sub = _iota((8, kl), 0)
    outs = []
    for e in range(2):
        qm = jnp.where((lane >= 64 * e) & (lane < 64 * (e + 1)), q * ATT_SCALE, 0.0)
        cs = jnp.sum(jnp.where(sub == 2 * hp + e, ccol, 0.0), axis=0, keepdims=True)
        s = bdot(qm, k, "nt") - cs
        diag = jnp.where(tri, s[:, kl - FOX_Q:], NEG)
        s = diag if kl == FOX_Q else jnp.concatenate([s[:, :kl - FOX_Q], diag], axis=1)
        outs.append(_attend(s, v))
    return jnp.where(lane < 64, outs[0], outs[1])


def _fox_in_specs():
    return [pl.BlockSpec((FOX_Q, 128), lambda hp, qb: (qb, 12 + hp)),
            pl.BlockSpec((T, 128), lambda hp, qb: (0, 16 + hp)),
            pl.BlockSpec((T, 128), lambda hp, qb: (0, 20 + hp)),
            pl.BlockSpec((8, T), lambda hp, qb: (0, 0))]


def _fox_fwd(proj, cum_c, cat):
    def body(q_ref, k_ref, v_ref, cc_ref, cat_ref, o_ref):
        qb = pl.program_id(1)
        for g in range(FOX_QB):
            kl = FOX_Q * (g + 1)

            @pl.when(qb == g)
            def _(kl=kl):
                o_ref[...] = _fox_block(pl.program_id(0), q_ref[...], k_ref[0:kl, :], v_ref[0:kl, :], cc_ref[:, 0:kl])

    return pl.pallas_call(
        body, name="fox_fwd", grid=(4, FOX_QB), in_specs=_fox_in_specs() + [pl.BlockSpec(memory_space=pl.ANY)],
        out_specs=pl.BlockSpec((FOX_Q, 128), lambda hp, qb: (qb, 4 + hp)),
        out_shape=jax.ShapeDtypeStruct((T, D), F32), input_output_aliases={4: 0},
        compiler_params=_cp(("parallel", "parallel")),
    )(proj, proj, proj, cum_c, cat)


def _fox_bwd(proj, cum_c, dcat):
    def body(q_ref, k_ref, v_ref, cc_ref, do_ref, dq_ref, dk_ref, dv_ref, dcc_ref):
        qb = pl.program_id(1)

        @pl.when(qb == 0)
        def _():
            dk_ref[...] = jnp.zeros_like(dk_ref)
            dv_ref[...] = jnp.zeros_like(dv_ref)
            dcc_ref[...] = jnp.zeros_like(dcc_ref)

        fn = functools.partial(_fox_block, pl.program_id(0))
        for g in range(FOX_QB):
            kl = FOX_Q * (g + 1)

            @pl.when(qb == g)
            def _(kl=kl):
                _, vjp = jax.vjp(fn, q_ref[...], k_ref[0:kl, :], v_ref[0:kl, :], cc_ref[:, 0:kl])
                dq, dk, dv, dcc = vjp(do_ref[...])
                dq_ref[...] = dq
                dk_ref[0:kl, :] += dk
                dv_ref[0:kl, :] += dv
                dcc_ref[:, 0:kl] += dcc

    sds = lambda *s: jax.ShapeDtypeStruct(s, F32)
    return pl.pallas_call(
        body, name="fox_bwd", grid=(4, FOX_QB),
        in_specs=_fox_in_specs() + [pl.BlockSpec((FOX_Q, 128), lambda hp, qb: (qb, 4 + hp))],
        out_specs=(pl.BlockSpec((FOX_Q, 128), lambda hp, qb: (qb, hp)),
                   pl.BlockSpec((T, 128), lambda hp, qb: (0, hp)),
                   pl.BlockSpec((T, 128), lambda hp, qb: (0, hp)),
                   pl.BlockSpec((None, 8, T), lambda hp, qb: (hp, 0, 0))),
        out_shape=(sds(T, 512), sds(T, 512), sds(T, 512), sds(4, 8, T)),
        compiler_params=_cp(("parallel", "arbitrary")),
    )(proj, proj, proj, cum_c, dcat)


BIAS_W = 640


def _rel_onehot():
    j = _iota((REL_PAD, BIAS_W), 1)
    rel = jnp.clip(CA_PAD + CHUNK - 1 - j, -128, 128) + 128
    return (_iota((REL_PAD, BIAS_W), 0) == rel).astype(F32)


def _bias_build(rbp):
    def body(rb_ref, o_ref):
        f = _hdot_raw(rb_ref[...], _rel_onehot(), "nn")
        for q in range(CHUNK):
            o_ref[q] = pltpu.roll(f, (BIAS_W - (CHUNK - 1 - q)) % BIAS_W, 1)[:, :CA_BAND]

    return pl.pallas_call(body, name="ca_bias_build", out_shape=jax.ShapeDtypeStruct((CHUNK, 8, CA_BAND), F32))(rbp)


def _bias_grad(dbias_q):
    def body(db_ref, o_ref):
        acc = jnp.zeros((8, BIAS_W), F32)
        for q in range(CHUNK):
            acc = acc + pltpu.roll(db_ref[q], CHUNK - 1 - q, 1)
        o_ref[...] = _hdot_raw(acc, _rel_onehot(), "nt")

    return pl.pallas_call(body, name="ca_bias_grad", out_shape=jax.ShapeDtypeStruct((8, REL_PAD), F32))(dbias_q)


def _ca_block(c, masked, q, kb, vb, bias2):
    lane = _iota((CHUNK, 128), 1)
    outs = []
    for e in range(2):
        qm = jnp.where((lane >= 64 * e) & (lane < 64 * (e + 1)), q * ATT_SCALE, 0.0)
        s = bdot(qm, kb, "nt") + bias2[e]
        if masked:
            s = jnp.where((c * CHUNK - CA_PAD + _iota((CHUNK, CA_BAND), 1)) >= 0, s, NEG)
        outs.append(_attend(s, vb))
    return jnp.where(lane < 64, outs[0], outs[1])


CA_PER_STEP = 4
CA_ROWS = CA_PER_STEP * CHUNK
CA_MASKED_STEPS = CA_PAD // CA_ROWS


def _ca_fwd(proj, kvpad, bias):
    def body(q_ref, k_ref, v_ref, b_ref, o_ref):
        def run(masked):
            for i in range(CA_PER_STEP):
                c = pl.program_id(1) * CA_PER_STEP + i
                band = pl.ds(pl.multiple_of(c * CHUNK, CHUNK), CA_BAND)
                rows = slice(i * CHUNK, (i + 1) * CHUNK)
                o_ref[rows, :] = _ca_block(c, masked, q_ref[rows, :], k_ref[band, :], v_ref[band, :], b_ref[...])

        pl.when(pl.program_id(1) < CA_MASKED_STEPS)(lambda: run(True))
        pl.when(pl.program_id(1) >= CA_MASKED_STEPS)(lambda: run(False))

    return pl.pallas_call(
        body, name="ca_fwd", grid=(4, NCHUNK // CA_PER_STEP),
        in_specs=[pl.BlockSpec((CA_ROWS, 128), lambda hp, c: (c, hp)),
                  pl.BlockSpec((T + CA_PAD, 128), lambda hp, c: (0, hp)),
                  pl.BlockSpec((T + CA_PAD, 128), lambda hp, c: (0, 4 + hp)),
                  pl.BlockSpec((2, CHUNK, CA_BAND), lambda hp, c: (hp, 0, 0))],
        out_specs=pl.BlockSpec((CA_ROWS, 128), lambda hp, c: (c, hp)),
        out_shape=jax.ShapeDtypeStruct((T, D), F32),
        compiler_params=_cp(("parallel", "parallel")),
    )(proj, kvpad, kvpad, bias)


def _ca_bwd(proj, kvpad, bias, dcat):
    def body(q_ref, k_ref, v_ref, b_ref, do_ref, dq_ref, dk_ref, dv_ref, db_ref):
        c = pl.program_id(1)

        @pl.when(c == 0)
        def _():
            dk_ref[...] = jnp.zeros_like(dk_ref)
            dv_ref[...] = jnp.zeros_like(dv_ref)
            db_ref[...] = jnp.zeros_like(db_ref)

        def run(masked):
            for i in range(CA_PER_STEP):
                ci = c * CA_PER_STEP + i
                band = pl.ds(pl.multiple_of(ci * CHUNK, CHUNK), CA_BAND)
                rows = slice(i * CHUNK, (i + 1) * CHUNK)
                fn = functools.partial(_ca_block, ci, masked)
                _, vjp = jax.vjp(fn, q_ref[rows, :], k_ref[band, :], v_ref[band, :], b_ref[...])
                dq, dkb, dvb, db = vjp(do_ref[rows, :])
                dq_ref[rows, :] = dq
                dk_ref[band, :] += dkb
                dv_ref[band, :] += dvb
                db_ref[...] += db

        pl.when(c < CA_MASKED_STEPS)(lambda: run(True))
        pl.when(c >= CA_MASKED_STEPS)(lambda: run(False))

    sds = lambda *s: jax.ShapeDtypeStruct(s, F32)
    padded = lambda: pl.BlockSpec((T + CA_PAD, 128), lambda hp, c: (0, hp))
    return pl.pallas_call(
        body, name="ca_bwd", grid=(4, NCHUNK // CA_PER_STEP),
        in_specs=[pl.BlockSpec((CA_ROWS, 128), lambda hp, c: (c, hp)),
                  pl.BlockSpec((T + CA_PAD, 128), lambda hp, c: (0, hp)),
                  pl.BlockSpec((T + CA_PAD, 128), lambda hp, c: (0, 4 + hp)),
                  pl.BlockSpec((2, CHUNK, CA_BAND), lambda hp, c: (hp, 0, 0)),
                  pl.BlockSpec((CA_ROWS, 128), lambda hp, c: (c, hp))],
        out_specs=(pl.BlockSpec((CA_ROWS, 128), lambda hp, c: (c, hp)), padded(), padded(),
                   pl.BlockSpec((2, CHUNK, CA_BAND), lambda hp, c: (hp, 0, 0))),
        out_shape=(sds(T, 512), sds(T + CA_PAD, 512), sds(T + CA_PAD, 512), sds(8, CHUNK, CA_BAND)),
        compiler_params=_cp(("parallel", "arbitrary")),
    )(proj, kvpad, kvpad, bias, dcat)


def _lru_pre(xs, cw, cb, wa, ba, wx, bx, lam):
    xc = cb + xs[0] * cw[0:1, :] + xs[1] * cw[1:2, :] + xs[2] * cw[2:3, :] + xs[3] * cw[3:4, :]
    ra = _sigmoid(bdot(xc, wa, "nn") + ba)
    ii = _sigmoid(bdot(xc, wx, "nn") + bx)
    la = 8.0 * ra * _log_sigmoid(lam)
    return jnp.exp(la), jnp.sqrt(-_expm1(2.0 * la)) * (ii * xc)


def _lru_pre_specs():
    full = lambda shape: pl.BlockSpec(shape, lambda i: (0,) * len(shape))
    return [pl.BlockSpec((4, ROWS, 512), lambda i: (0, i, 0)), full((4, 512)), full((1, 512)),
            full((512, 512)), full((1, 512)), full((512, 512)), full((1, 512)), full((1, 512))]


def _lru_pre_fwd(xs, cw, cb, wa, ba, wx, bx, lam):
    def body(xs_ref, cw_ref, cb_ref, wa_ref, ba_ref, wx_ref, bx_ref, lam_ref, a_ref, b_ref):
        a, b = _lru_pre(xs_ref[...], cw_ref[...], cb_ref[...], wa_ref[...], ba_ref[...], wx_ref[...], bx_ref[...],
                        lam_ref[...])
        a_ref[...] = a
        b_ref[...] = b

    row = pl.BlockSpec((ROWS, 512), lambda i: (i, 0))
    sds = jax.ShapeDtypeStruct((T, 512), F32)
    return pl.pallas_call(body, name="lru_pre_fwd", grid=(T // ROWS,), in_specs=_lru_pre_specs(),
                          out_specs=(row, row), out_shape=(sds, sds), compiler_params=_cp(("parallel",)),
                          )(xs, cw, cb, wa, ba, wx, bx, lam)


def _lru_pre_bwd(xs, cw, cb, wa, ba, wx, bx, lam, da, db):
    def body(xs_ref, cw_ref, cb_ref, wa_ref, ba_ref, wx_ref, bx_ref, lam_ref, da_ref, db_ref,
             dxs_ref, dcw_ref, dcb_ref, dwa_ref, dba_ref, dwx_ref, dbx_ref, dlam_ref):
        acc = (dcw_ref, dcb_ref, dwa_ref, dba_ref, dwx_ref, dbx_ref, dlam_ref)

        @pl.when(pl.program_id(0) == 0)
        def _():
            for r in acc:
                r[...] = jnp.zeros_like(r)

        _, vjp = jax.vjp(_lru_pre, xs_ref[...], cw_ref[...], cb_ref[...], wa_ref[...], ba_ref[...], wx_ref[...],
                         bx_ref[...], lam_ref[...])
        grads = vjp((da_ref[...], db_ref[...]))
        dxs_ref[...] = grads[0]
        for r, g in zip(acc, grads[1:]):
            r[...] += g

    row = pl.BlockSpec((ROWS, 512), lambda i: (i, 0))
    specs = _lru_pre_specs()
    sds = lambda *s: jax.ShapeDtypeStruct(s, F32)
    return pl.pallas_call(
        body, name="lru_pre_bwd", grid=(T // ROWS,), in_specs=specs + [row, row], out_specs=tuple(specs),
        out_shape=(sds(4, T, 512), sds(4, 512), sds(1, 512), sds(512, 512), sds(1, 512), sds(512, 512), sds(1, 512),
                   sds(1, 512)),
        compiler_params=_cp(("arbitrary",)),
    )(xs, cw, cb, wa, ba, wx, bx, lam, da, db)


def _lru_scan_fwd(a, b):
    def body(a_ref, b_ref, h_ref):
        def step(t, h):
            h = a_ref[pl.ds(t, 1), :] * h + b_ref[pl.ds(t, 1), :]
            h_ref[pl.ds(t, 1), :] = h
            return h

        lax.fori_loop(0, T, step, jnp.zeros((1, 512), F32), unroll=8)

    return pl.pallas_call(body, name="lru_scan_fwd", out_shape=jax.ShapeDtypeStruct((T, 512), F32),
                          compiler_params=pltpu.CompilerParams(vmem_limit_bytes=VMEM_LIMIT))(a, b)


def _lru_scan_bwd(a, h, dh):
    def body(a_ref, h_ref, dh_ref, da_ref, db_ref):
        def step(i, carry):
            t = T - 1 - i
            g = dh_ref[pl.ds(t, 1), :] + carry
            db_ref[pl.ds(t, 1), :] = g
            da_ref[pl.ds(t, 1), :] = g * h_ref[pl.ds(t - 1, 1), :]
            return a_ref[pl.ds(t, 1), :] * g

        carry = lax.fori_loop(0, T - 1, step, jnp.zeros((1, 512), F32), unroll=8)
        db_ref[pl.ds(0, 1), :] = dh_ref[pl.ds(0, 1), :] + carry
        da_ref[pl.ds(0, 1), :] = jnp.zeros((1, 512), F32)

    sds = jax.ShapeDtypeStruct((T, 512), F32)
    return pl.pallas_call(body, name="lru_scan_bwd", out_shape=(sds, sds),
                          compiler_params=pltpu.CompilerParams(vmem_limit_bytes=VMEM_LIMIT))(a, h, dh)


def _lru_post(h, gate):
    return h * _gelu_tanh(gate)


def _lru_post_fwd(h, proj, cat):
    def body(h_ref, g_ref, cat_ref, o_ref):
        o_ref[...] = _lru_post(h_ref[...], g_ref[...])

    row = pl.BlockSpec((ROWS, 512), lambda i: (i, 0))
    return pl.pallas_call(body, name="lru_post_fwd", grid=(T // ROWS,),
                          in_specs=[row, pl.BlockSpec((ROWS, 512), lambda i: (i, 3)), pl.BlockSpec(memory_space=pl.ANY)],
                          out_specs=pl.BlockSpec((ROWS, 512), lambda i: (i, 1)),
                          out_shape=jax.ShapeDtypeStruct((T, D), F32), input_output_aliases={2: 0},
                          compiler_params=_cp(("parallel",)))(h, proj, cat)


def _lru_post_bwd(h, proj, dcat):
    def body(h_ref, g_ref, do_ref, dh_ref, dg_ref):
        _, vjp = jax.vjp(_lru_post, h_ref[...], g_ref[...])
        dh, dg = vjp(do_ref[...])
        dh_ref[...] = dh
        dg_ref[...] = dg

    row = pl.BlockSpec((ROWS, 512), lambda i: (i, 0))
    sds = jax.ShapeDtypeStruct((T, 512), F32)
    return pl.pallas_call(body, name="lru_post_bwd", grid=(T // ROWS,),
                          in_specs=[row, pl.BlockSpec((ROWS, 512), lambda i: (i, 3)),
                                    pl.BlockSpec((ROWS, 512), lambda i: (i, 1))],
                          out_specs=(row, row), out_shape=(sds, sds), compiler_params=_cp(("parallel",)))(h, proj, dcat)


def _conv_dx(dxs_shift):
    def body(d_ref, o_ref):
        o_ref[...] = d_ref[0] + d_ref[1] + d_ref[2] + d_ref[3]

    row = pl.BlockSpec((ROWS, 512), lambda i: (i, 0))
    return pl.pallas_call(body, name="lru_conv_dx", grid=(T // ROWS,),
                          in_specs=[pl.BlockSpec((4, ROWS, 512), lambda i: (0, i, 0))], out_specs=row,
                          out_shape=jax.ShapeDtypeStruct((T, 512), F32), compiler_params=_cp(("parallel",)))(dxs_shift)


def _position():
    return lax.axis_index("x"), lax.axis_index("y"), lax.axis_index("c")


def _other_chips(x, y):
    return [(1 - x, y), (x, 1 - y), (1 - x, 1 - y)]


def _al(v, n):
    return v * n if isinstance(v, int) else pl.multiple_of(v * n, n)


_AG_ITEMS = [
    ((4, 32, 128), lambda o, s, h: o.at[s, pl.ds(_al(h, 16), 16), :], lambda r, h: r.at[pl.ds(_al(h, 16), 16), :]),
    ((4, 774, 1024), lambda o, s, h: o.at[s, :, pl.ds(_al(h, 512), 512)], lambda r, h: r.at[:, pl.ds(_al(h, 512), 512)]),
    ((1024, 1024), lambda o, s, h: o.at[pl.ds(_al(2 * s + h, 128), 128), :], lambda r, h: r.at[pl.ds(_al(h, 128), 128), :]),
    ((2, 1024, 4096), lambda o, s, h: o.at[h, :, pl.ds(_al(s, 1024), 1024)], lambda r, h: r.at[h]),
    ((2, 4096, 1024), lambda o, s, h: o.at[h, pl.ds(_al(s, 1024), 1024), :], lambda r, h: r.at[h]),
    ((1024, 2560), lambda o, s, h: o.at[pl.ds(_al(h, 512), 512), pl.ds(_al(s, 640), 640)],
     lambda r, h: r.at[pl.ds(_al(h, 512), 512), :]),
    ((1024, 1024), lambda o, s, h: o.at[pl.ds(_al(2 * s + h, 128), 128), :], lambda r, h: r.at[pl.ds(_al(h, 128), 128), :]),
]


_AG_GROUPS = [(0, 1, 2), (3, 4), (5, 6)]

_HBM = pl.BlockSpec(memory_space=pltpu.HBM)
_SEM = pl.BlockSpec(memory_space=pltpu.SEMAPHORE)
_SPLIT = dict(has_side_effects=pltpu.SideEffectType.DATAFLOW_SIDE_EFFECTING)


def _hbm(a):
    return pltpu.with_memory_space_constraint(a, pltpu.HBM)


def _ag_ici_copy(i, j, chip, c, slot, src_ref, land_ref, send_sems, recv_sems, k):
    _, dst, half = _AG_ITEMS[i]
    return pltpu.make_async_remote_copy(src_ref=half(src_ref, c), dst_ref=dst(land_ref, slot, c), send_sem=send_sems.at[k],
                                        recv_sem=recv_sems.at[k], device_id=(*chip, c), device_id_type=MESH)


def _ag_start(shards):
    n = len(_AG_ITEMS)
    ng = len(_AG_GROUPS)
    lands = [lax.empty(shape, s.dtype) for (shape, _, _), s in zip(_AG_ITEMS, shards)]

    def body(*refs):
        srcs, land_refs = refs[:n], refs[n:2 * n]
        sems = refs[2 * n:2 * n + 2 * ng]
        token = refs[-1]
        x, y, c = _position()
        me = 2 * x + y
        for g, items in enumerate(_AG_GROUPS):
            for t, i in enumerate(items):
                for j, chip in enumerate(_other_chips(x, y)):
                    _ag_ici_copy(i, j, chip, c, me, srcs[i], land_refs[i], sems[2 * g], sems[2 * g + 1], 3 * t + j).start()
        token[...] = jnp.zeros_like(token)

    sem_shapes = []
    for items in _AG_GROUPS:
        sem_shapes += [pltpu.SemaphoreType.DMA((3 * len(items),))] * 2
    thru = [pltpu.HBM(a.shape, a.dtype) for a in list(shards) + lands]
    out = pl.pallas_call(
        body, name="allgather_start",
        out_shape=tuple(sem_shapes) + tuple(thru) + (jax.ShapeDtypeStruct((8, 128), F32),),
        in_specs=(_HBM,) * (2 * n),
        out_specs=(_SEM,) * (2 * ng) + (_HBM,) * (2 * n) + (pl.BlockSpec(memory_space=pltpu.VMEM),),
        input_output_aliases={i: 2 * ng + i for i in range(2 * n)},
        compiler_params=pltpu.CompilerParams(**_SPLIT),
    )(*[_hbm(a) for a in list(shards) + lands])
    sems, thru, token = out[:2 * ng], out[2 * ng:-1], out[-1]
    return [(sems[2 * g], sems[2 * g + 1]) for g in range(ng)], list(thru[:n]), list(thru[n:]), token


def _ag_wait(g, sems, srcs, lands, after):
    items = _AG_GROUPS[g]
    m = len(items)

    def body(*refs):
        src_refs, land_refs = refs[:m], refs[m:2 * m]
        send_sems, recv_sems = refs[2 * m], refs[2 * m + 1]
        x, y, c = _position()
        for t, i in enumerate(items):
            for j, chip in enumerate(_other_chips(x, y)):
                cp = _ag_ici_copy(i, j, chip, c, 2 * chip[0] + chip[1], src_refs[t], land_refs[t], send_sems, recv_sems,
                                  3 * t + j)
                cp.wait_send()
                cp.wait_recv()

    ops = [srcs[i] for i in items] + [lands[i] for i in items]
    out = pl.pallas_call(
        body, name=f"allgather_wait_{g}",
        out_shape=tuple(pltpu.HBM(a.shape, a.dtype) for a in ops),
        in_specs=(_HBM,) * (2 * m) + (_SEM, _SEM, pl.BlockSpec(memory_space=pl.ANY)),
        out_specs=(_HBM,) * (2 * m),
        input_output_aliases={i: i for i in range(2 * m)},
        compiler_params=pltpu.CompilerParams(**_SPLIT),
    )(*ops, sems[0], sems[1], after)
    return list(out[:m]), list(out[m:])


def _ag_forward(g, srcs, lands):
    items = _AG_GROUPS[g]
    m = len(items)

    def body(*refs):
        src_refs, in_refs, out_refs = refs[:m], refs[m:2 * m], refs[2 * m:3 * m]
        send_sems, recv_sems = refs[3 * m:]
        x, y, c = _position()
        sibling = (x, y, 1 - c)
        me = 2 * x + y
        chips = _other_chips(x, y)
        sends = []
        for t, i in enumerate(items):
            _, dst, half = _AG_ITEMS[i]
            for j, chip in enumerate(chips):
                slot = 2 * chip[0] + chip[1]
                sends.append(pltpu.make_async_remote_copy(
                    src_ref=dst(in_refs[t], slot, c), dst_ref=dst(out_refs[t], slot, c), send_sem=send_sems.at[5 * t + j],
                    recv_sem=recv_sems.at[5 * t + j], device_id=sibling, device_id_type=MESH))
            for hc in range(2):
                sends.append(pltpu.make_async_remote_copy(
                    src_ref=half(src_refs[t], hc), dst_ref=dst(out_refs[t], me, hc), send_sem=send_sems.at[5 * t + 3 + hc],
                    recv_sem=recv_sems.at[5 * t + 3 + hc], device_id=sibling, device_id_type=MESH))
        for cp in sends:
            cp.start()
        for t, i in enumerate(items):
            _, dst, half = _AG_ITEMS[i]
            for j, chip in enumerate(chips):
                there = dst(out_refs[t], 2 * chip[0] + chip[1], 1 - c)
                pltpu.make_async_remote_copy(src_ref=there, dst_ref=there, send_sem=send_sems.at[5 * t + j],
                                             recv_sem=recv_sems.at[5 * t + j], device_id=sibling,
                                             device_id_type=MESH).wait_recv()
            for hc in range(2):
                there = dst(out_refs[t], me, hc)
                pltpu.make_async_remote_copy(src_ref=there, dst_ref=there, send_sem=send_sems.at[5 * t + 3 + hc],
                                             recv_sem=recv_sems.at[5 * t + 3 + hc], device_id=sibling,
                                             device_id_type=MESH).wait_recv()
        for cp in sends:
            cp.wait_send()

    any_spec = pl.BlockSpec(memory_space=pl.ANY)
    return pl.pallas_call(
        body, name=f"allgather_forward_{g}",
        in_specs=[any_spec] * (2 * m), out_specs=(any_spec,) * m,
        out_shape=tuple(jax.ShapeDtypeStruct(a.shape, a.dtype) for a in lands),
        input_output_aliases={m + t: t for t in range(m)},
        scratch_shapes=[pltpu.SemaphoreType.DMA((5 * m,)), pltpu.SemaphoreType.DMA((5 * m,))],
    )(*srcs, *lands)


def _pair_swap_cols(gb, tag):
    _, rows, cols = gb.shape
    hc = cols // 2

    def body(g_ref, out_ref, send_sem, recv_sem):
        x, y, c = _position()
        cp = pltpu.make_async_remote_copy(src_ref=g_ref.at[:, :, pl.ds(_al(1 - c, hc), hc)], dst_ref=out_ref,
                                          send_sem=send_sem, recv_sem=recv_sem, device_id=(x, y, 1 - c),
                                          device_id_type=MESH)
        cp.start()
        cp.wait()

    return pl.pallas_call(
        body, name="grad_pair_swap_" + tag,
        in_specs=[pl.BlockSpec(memory_space=pl.ANY)], out_specs=pl.BlockSpec(memory_space=pl.ANY),
        out_shape=jax.ShapeDtypeStruct((4, rows, hc), gb.dtype),
        scratch_shapes=[pltpu.SemaphoreType.DMA, pltpu.SemaphoreType.DMA],
    )(gb)


def _handover(halves, tag):
    def body(in_ref, out_ref, send_sem, recv_sem):
        x, y, c = _position()
        cp = pltpu.make_async_remote_copy(src_ref=in_ref.at[c], dst_ref=out_ref.at[c], send_sem=send_sem,
                                          recv_sem=recv_sem, device_id=(x, y, 1 - c), device_id_type=MESH)
        cp.start()
        theirs = out_ref.at[1 - c]
        pltpu.make_async_remote_copy(src_ref=theirs, dst_ref=theirs, send_sem=send_sem, recv_sem=recv_sem,
                                     device_id=(x, y, c), device_id_type=MESH).wait_recv()
        cp.wait_send()

    return pl.pallas_call(
        body, name="grad_handover_" + tag,
        in_specs=[pl.BlockSpec(memory_space=pl.ANY)], out_specs=pl.BlockSpec(memory_space=pl.ANY),
        out_shape=jax.ShapeDtypeStruct(halves.shape, halves.dtype), input_output_aliases={0: 0},
        scratch_shapes=[pltpu.SemaphoreType.DMA, pltpu.SemaphoreType.DMA],
    )(halves)


def _a2a_copy(j, chip, c, p_ref, q_ref, q_slot, send_sems, recv_sems):
    return pltpu.make_async_remote_copy(src_ref=p_ref.at[2 * chip[0] + chip[1]], dst_ref=q_ref.at[q_slot],
                                        send_sem=send_sems.at[j], recv_sem=recv_sems.at[j], device_id=(*chip, c),
                                        device_id_type=MESH)


def _a2a_start(p, tag):
    def body(p_ref, q_ref, send_sems, recv_sems, p_thru, q_thru, token):
        x, y, c = _position()
        for j, chip in enumerate(_other_chips(x, y)):
            _a2a_copy(j, chip, c, p_ref, q_ref, 2 * x + y, send_sems, recv_sems).start()
        token[...] = jnp.zeros_like(token)

    return pl.pallas_call(
        body, name="grad_alltoall_start_" + tag,
        out_shape=(pltpu.SemaphoreType.DMA((3,)), pltpu.SemaphoreType.DMA((3,)), pltpu.HBM(p.shape, p.dtype),
                   pltpu.HBM(p.shape, p.dtype), jax.ShapeDtypeStruct((8, 128), F32)),
        in_specs=(_HBM, _HBM), out_specs=(_SEM, _SEM, _HBM, _HBM, pl.BlockSpec(memory_space=pltpu.VMEM)),
        input_output_aliases={0: 2, 1: 3},
        compiler_params=pltpu.CompilerParams(**_SPLIT),
    )(_hbm(p), _hbm(lax.empty(p.shape, p.dtype)))


def _a2a_wait(send_sems, recv_sems, p, q, after, tag):
    def body(p_ref, q_ref, send_sems, recv_sems, after_ref, p_out, q_out):
        x, y, c = _position()
        for j, chip in enumerate(_other_chips(x, y)):
            cp = _a2a_copy(j, chip, c, p_ref, q_ref, 2 * chip[0] + chip[1], send_sems, recv_sems)
            cp.wait_send()
            cp.wait_recv()

    return pl.pallas_call(
        body, name="grad_alltoall_wait_" + tag,
        out_shape=(pltpu.HBM(p.shape, p.dtype), pltpu.HBM(q.shape, q.dtype)),
        in_specs=(_HBM, _HBM, _SEM, _SEM, pl.BlockSpec(memory_space=pl.ANY)), out_specs=(_HBM, _HBM),
        input_output_aliases={0: 0, 1: 1},
        compiler_params=pltpu.CompilerParams(**_SPLIT),
    )(p, q, send_sems, recv_sems, after)


def _comm_rows(rows):
    return next(t for t in (512, 384, 256, 128) if rows % t == 0)


def _pair_add(gb, recv, where, tag):
    _, rows, cols = gb.shape
    hc = cols // 2
    tr = _comm_rows(rows)

    def body(w_ref, g_ref, r_ref, o_ref):
        o_ref[...] = (g_ref[...].astype(F32) + r_ref[...].astype(F32)).astype(o_ref.dtype)

    return pl.pallas_call(
        body, name="grad_pair_add_" + tag,
        grid_spec=pltpu.PrefetchScalarGridSpec(
            num_scalar_prefetch=1, grid=(4, rows // tr),
            in_specs=[pl.BlockSpec((None, tr, hc), lambda s, j, w_ref: (s, j, w_ref[0])),
                      pl.BlockSpec((None, tr, hc), lambda s, j, w_ref: (s, j, 0))],
            out_specs=pl.BlockSpec((None, tr, hc), lambda s, j, w_ref: (s, j, 0))),
        out_shape=jax.ShapeDtypeStruct((4, rows, hc), gb.dtype),
        compiler_params=_cp(("parallel", "parallel")),
    )(where, gb, recv)


def _sum_chips(p, q, where, tag):
    _, rows, hc = q.shape
    tr = _comm_rows(rows)

    def body(w_ref, p_ref, qa_ref, qb_ref, qc_ref, o_ref):
        me = w_ref[1]
        own, qa, qb, qc = (r[...].astype(F32) for r in (p_ref, qa_ref, qb_ref, qc_ref))
        v0 = jnp.where(me == 0, own, qa)
        v1 = jnp.where(me == 1, own, jnp.where(me == 0, qa, qb))
        v2 = jnp.where(me == 2, own, jnp.where(me < 2, qb, qc))
        v3 = jnp.where(me == 3, own, qc)
        o_ref[...] = ((v0 + v1) + v2) + v3

    slot = lambda k: pl.BlockSpec((None, tr, hc), lambda j, w_ref: (w_ref[k], j, 0))
    return pl.pallas_call(
        body, name="grad_sum_chips_" + tag,
        grid_spec=pltpu.PrefetchScalarGridSpec(
            num_scalar_prefetch=1, grid=(rows // tr,),
            in_specs=[slot(1), slot(2), slot(3), slot(4)],
            out_specs=pl.BlockSpec((None, tr, hc), lambda j, w_ref: (w_ref[0], j, 0))),
        out_shape=jax.ShapeDtypeStruct((2, rows, hc), F32),
        compiler_params=_cp(("parallel",)),
    )(where, p, q, q, q)


def _shard_major(g, axis):
    shape = g.shape
    g = g.reshape(shape[:axis] + (4, shape[axis] // 4) + shape[axis + 1:])
    return jnp.moveaxis(g, axis, 0).reshape(4, -1)


def _unshard(g4, shape, axis):
    n = shape[axis] // 4
    g = g4.reshape((4,) + shape[:axis] + (n,) + shape[axis + 1:])
    return jnp.moveaxis(g, 0, axis).reshape(shape)


def _split(flat, shapes):
    out, off = [], 0
    for shp in shapes:
        n = 1
        for d in shp:
            n *= d
        out.append(flat[..., off:off + n].reshape(flat.shape[:-1] + tuple(shp)))
        off += n
    return out


def _even_rows_to_kernel(wt):
    return jnp.concatenate([wt[:1536], wt[1552:3088], wt[1536:1552], wt[3088:3096],
                            jnp.zeros((PE - 3096, wt.shape[1]), wt.dtype)], axis=0)


def _block_diag(w):
    eye = jnp.eye(8, dtype=w.dtype)
    return (w[:, :, None, :] * eye[:, None, :, None]).reshape(512, 512)


def _diag_blocks(g):
    eye = jnp.eye(8, dtype=g.dtype)
    return (g.reshape(8, 64, 8, 64) * eye[:, None, :, None]).sum(axis=2)


def _shift_down(a, s):
    return a if s == 0 else jnp.pad(a, ((s, 0), (0, 0)))[:a.shape[0]]


def _shift_up(a, s):
    return a if s == 0 else jnp.pad(a, ((0, s), (0, 0)))[s:]


SMALL_SHARDED_SHAPES = [(2, 4, 256), (16, 64), (4, 128), (128,), (128,), (128,), (128,)]
REPL_SHAPES = [(256,), (512,), (8,), (8, 257), (8, 64, 64), (8, 64, 64)]


def kernel(x, norm_w, w_in_even, gla_w_a_up, gla_b_a, gla_norm_w, fox_b_f, w_out_even, w_in_odd, rel_bias, conv_w, conv_b, lru_w_a, lru_b_a, lru_w_x, lru_b_x, lru_lambda, w_out_odd, w_mlp_up, w_mlp_down, loss_target, m_norm_w, m_w_in_even, m_gla_w_a_up, m_gla_b_a, m_gla_norm_w, m_fox_b_f, m_w_out_even, m_w_in_odd, m_rel_bias, m_conv_w, m_conv_b, m_lru_w_a, m_lru_b_a, m_lru_w_x, m_lru_b_x, m_lru_lambda, m_w_out_odd, m_w_mlp_up, m_w_mlp_down, v_norm_w, v_w_in_even, v_gla_w_a_up, v_gla_b_a, v_gla_norm_w, v_fox_b_f, v_w_out_even, v_w_in_odd, v_rel_bias, v_conv_w, v_conv_b, v_lru_w_a, v_lru_b_a, v_lru_w_x, v_lru_b_x, v_lru_lambda, v_w_out_odd, v_w_mlp_up, v_w_mlp_down):
    c_idx = lax.axis_index("c")

    small_local = [norm_w, gla_w_a_up[0], conv_w[0], conv_b[0], lru_b_a[0], lru_b_x[0], lru_lambda[0]]
    small_src = jnp.concatenate([a.reshape(-1) for a in small_local]).reshape(32, 128)
    mine = [small_src, w_in_even[0].T.astype(BF16), w_out_even[0].astype(BF16), w_mlp_up.astype(BF16),
            w_mlp_down.astype(BF16), w_in_odd[0].astype(BF16), w_out_odd[0].astype(BF16)]
    ag_sems, ag_srcs, ag_lands, ag_token = _ag_start(mine)

    def gathered(g, after):
        srcs_g, lands_g = _ag_wait(g, ag_sems[g], ag_srcs, ag_lands, after)
        return _ag_forward(g, srcs_g, lands_g)

    small4, w_in_e4, w_out_e = gathered(0, ag_token)
    me = 2 * lax.axis_index("x") + lax.axis_index("y")
    others = [k + (k >= me).astype(jnp.int32) for k in range(3)]
    where = jnp.stack([c_idx, me] + others).astype(jnp.int32)

    w_in_e_t = _even_rows_to_kernel(w_in_e4.reshape(3096, D))
    g_small = _split(small4.reshape(4, 32 * 128), SMALL_SHARDED_SHAPES)
    nw_full = _unshard(g_small[0], (2, 4, 1024), 2)
    wa_up = _unshard(g_small[1], (16, 256), 1)
    cw = _unshard(g_small[2], (4, 512), 1)
    cb, lba, lbx, lam = [_unshard(g, (512,), 0).reshape(1, 512) for g in g_small[3:]]
    nw = lambda layer, i: nw_full[layer, i].reshape(1, D)

    wa_pad = jnp.pad(wa_up, ((0, 128 - 16), (0, 0)))
    gla_ba = gla_b_a.reshape(1, 256)
    gla_nw = gla_norm_w.reshape(1, 512)
    fox_bpad = jnp.pad(fox_b_f.reshape(1, 8), ((0, 0), (FOX_LANE0, 128 - FOX_LANE0 - 8)))
    rbp = jnp.pad(rel_bias[0], ((0, 0), (0, REL_PAD - 257)))
    wa_bd = _block_diag(lru_w_a[0])
    wx_bd = _block_diag(lru_w_x[0])

    x0 = x[0]
    tgt = loss_target[0]

    h0 = _prenorm(x0, nw(0, 0), "prenorm_l0_mix")
    proj_e = _mm(h0, w_in_e_t, "nt", tm=1024, tn=640, name="mm_in_even")
    cat0, s_prev = _gla_fwd(proj_e, wa_pad, gla_ba, gla_nw)
    cum_r = _fox_gate_fwd(proj_e, fox_bpad)
    cum_c = cum_r[:, FOX_LANE0:FOX_LANE0 + 8].T
    cat0 = _fox_fwd(proj_e, cum_c, cat0)
    mix0 = _mm(cat0, w_out_e, "nn", tm=1024, tn=512, name="mm_out_even")
    x1 = _postnorm(x0, mix0, nw(0, 1), "postnorm_l0_mix")
    w_up, w_dn = gathered(1, x1)
    h1 = _prenorm(x1, nw(0, 2), "prenorm_l0_mlp")
    a0, r0 = _mm(h1, w_up, "nn", tm=1024, tn=1024, b_layer=0, relu_pair=True, name="mm_up_l0")
    d0 = _mm(a0, w_dn, "nn", tm=1024, tn=512, b_layer=0, name="mm_down_l0")
    x2 = _postnorm(x1, d0, nw(0, 3), "postnorm_l0_mlp")

    w_in_o, w_out_o = gathered(2, x2)
    h2 = _prenorm(x2, nw(1, 0), "prenorm_l1_mix")
    proj_o = _mm(h2, w_in_o, "nn", tm=1024, tn=640, name="mm_in_odd")
    bias_q = _bias_build(rbp)
    bias = bias_q.transpose(1, 0, 2)
    kvpad = jnp.pad(proj_o[:, 512:1536], ((CA_PAD, 0), (0, 0)))
    cat1 = _ca_fwd(proj_o, kvpad, bias)
    x_in = proj_o[:, 2048:2560]
    xs = jnp.stack([_shift_down(x_in, 3 - j) for j in range(4)])
    lru_a, lru_b = _lru_pre_fwd(xs, cw, cb, wa_bd, lba, wx_bd, lbx, lam)
    hh = _lru_scan_fwd(lru_a, lru_b)
    cat1 = _lru_post_fwd(hh, proj_o, cat1)
    mix1 = _mm(cat1, w_out_o, "nn", tm=1024, tn=512, name="mm_out_odd")
    x3 = _postnorm(x2, mix1, nw(1, 1), "postnorm_l1_mix")
    h3 = _prenorm(x3, nw(1, 2), "prenorm_l1_mlp")
    a1, r1 = _mm(h3, w_up, "nn", tm=1024, tn=1024, b_layer=1, relu_pair=True, name="mm_up_l1")
    d1 = _mm(a1, w_dn, "nn", tm=1024, tn=512, b_layer=1, name="mm_down_l1")
    x4 = _postnorm(x3, d1, nw(1, 3), "postnorm_l1_mlp")

    g4, loss_part = _loss_and_grad(x4, tgt)
    loss = lax.psum(loss_part[0, 0], ("x", "y", "c"))

    def rs_begin(gb, tag):
        pair_sum = _pair_add(gb, _pair_swap_cols(gb, tag), where, tag)
        return _a2a_start(pair_sum, tag)

    def rs_end(started, after, tag):
        send_sems, recv_sems, p, q, _ = started
        p, q = _a2a_wait(send_sems, recv_sems, p, q, after, tag)
        halves = _handover(_sum_chips(p, q, where, tag), tag)
        return jnp.concatenate([halves[0], halves[1]], axis=1)

    gba = jnp.zeros((4, GA_ROWS, D), BF16)
    dd1, dnw13 = _norm_bwd(d1, nw(1, 3), g4, None, "postnorm_l1_mlp_bwd")
    gba = _mm(a1, dd1, "tn", tm=512, tn=1024, into=(gba, 1024, GA_DN), name="mm_down_l1_dw")
    du1 = _mm(dd1, w_dn, "nt", tm=1024, tn=1024, b_layer=1, times2=r1, out_dtype=BF16, name="mm_down_l1_dx")
    gba = _mm(du1, h3, "tn", tm=512, tn=1024, into=(gba, 1024, GA_UP), name="mm_up_l1_dw")
    dh3 = _mm(du1, w_up, "nt", tm=1024, tn=512, b_layer=1, name="mm_up_l1_dx")
    g3, dnw12 = _norm_bwd(x3, nw(1, 2), dh3, g4, "prenorm_l1_mlp_bwd")
    dmix1, dnw11 = _norm_bwd(mix1, nw(1, 1), g3, None, "postnorm_l1_mix_bwd")
    gba = _mm(cat1, dmix1, "tn", tm=128, tn=1024, into=(gba, 256, GA_OUT_O), name="mm_out_odd_dw")
    dcat1 = _mm(dmix1, w_out_o, "nt", tm=1024, tn=512, name="mm_out_odd_dx")

    dq_c, dkpad, dvpad, dbias = _ca_bwd(proj_o, kvpad, bias, dcat1)
    g_rel = _bias_grad(jnp.pad(dbias.transpose(1, 0, 2), ((0, 0), (0, 0), (0, BIAS_W - CA_BAND))))[:, :257]
    dhh, dgate = _lru_post_bwd(hh, proj_o, dcat1)
    da_l, db_l = _lru_scan_bwd(lru_a, hh, dhh)
    dxs, g_cw, g_cb, g_wa_bd, g_lba, g_wx_bd, g_lbx, g_lam = _lru_pre_bwd(xs, cw, cb, wa_bd, lba, wx_bd, lbx, lam, da_l, db_l)
    dx_in = _conv_dx(jnp.stack([_shift_up(dxs[j], 3 - j) for j in range(4)]))
    dproj_o = jnp.concatenate([dq_c, dkpad[CA_PAD:], dvpad[CA_PAD:], dgate, dx_in], axis=1).astype(BF16)
    gba = _mm(dproj_o, h2, "tn", tm=128, tn=1024, into=(gba, 640, GA_IN_O), name="mm_in_odd_dw")
    rs_a = rs_begin(gba, "a")
    dh2 = _mm(dproj_o, w_in_o, "nt", tm=1024, tn=512, name="mm_in_odd_dx")
    g2, dnw10 = _norm_bwd(x2, nw(1, 0) + rs_a[4][0, 0], dh2, g3, "prenorm_l1_mix_bwd")

    gbb = lax.empty((4, GB_ROWS, D), BF16)
    dd0, dnw03 = _norm_bwd(d0, nw(0, 3), g2, None, "postnorm_l0_mlp_bwd")
    gbb = _mm(a0, dd0, "tn", tm=512, tn=1024, into=(gbb, 1024, GB_DN), name="mm_down_l0_dw")
    du0 = _mm(dd0, w_dn, "nt", tm=1024, tn=1024, b_layer=0, times2=r0, out_dtype=BF16, name="mm_down_l0_dx")
    gbb = _mm(du0, h1, "tn", tm=512, tn=1024, into=(gbb, 1024, GB_UP), name="mm_up_l0_dw")
    rs_b = rs_begin(gbb, "b")
    dh1 = _mm(du0, w_up, "nt", tm=1024, tn=512, b_layer=0, name="mm_up_l0_dx")
    g1, dnw02 = _norm_bwd(x1, nw(0, 2) + rs_b[4][0, 0], dh1, g2, "prenorm_l0_mlp_bwd")
    dmix0, dnw01 = _norm_bwd(mix0, nw(0, 1), g1, None, "postnorm_l0_mix_bwd")
    gbc = lax.empty((4, GC_ROWS, D), BF16)
    gbc = _mm(cat0, dmix0, "tn", tm=128, tn=1024, into=(gbc, 256, GC_OUT_E), name="mm_out_even_dw")
    dcat0 = _mm(dmix0, w_out_e, "nt", tm=1024, tn=512, name="mm_out_even_dx")

    dq_g, dk_g, dv_g, dr_g, daux_g, g_wa_pad, g_gla_ba, g_gla_nw = _gla_bwd(proj_e, s_prev, wa_pad, gla_ba, gla_nw, dcat0)
    dq_f, dk_f, dv_f, dccol = _fox_bwd(proj_e, cum_c, dcat0)
    dccol_t = jnp.pad(dccol.sum(axis=0).T, ((0, 0), (FOX_LANE0, 128 - FOX_LANE0 - 8)))
    daux, g_fox_bpad = _fox_gate_bwd(proj_e, fox_bpad, dccol_t, daux_g)
    dproj_e = jnp.concatenate([dq_g, dk_g, dv_g, dr_g, dq_f, dk_f, dv_f, daux], axis=1).astype(BF16)
    gt_in_e = _mm(dproj_e, h0, "tn", tm=640, tn=1024, out_dtype=BF16, name="mm_in_even_dw")
    dh0 = _mm(dproj_e, w_in_e_t, "nn", tm=1024, tn=512, name="mm_in_even_dx")
    grad_x, dnw00 = _norm_bwd(x0, nw(0, 0), dh0, g1, "prenorm_l0_mix_bwd")

    g_norm = jnp.stack([jnp.concatenate([dnw00, dnw01, dnw02, dnw03]), jnp.concatenate([dnw10, dnw11, dnw12, dnw13])])
    sharded = [(g_norm, 2), (g_wa_pad[:16], 1), (g_cw, 1), (g_cb[0], 0), (g_lba[0], 0), (g_lbx[0], 0), (g_lam[0], 0)]
    replicated = [g_gla_ba[0], g_gla_nw[0], g_fox_bpad[0, FOX_LANE0:FOX_LANE0 + 8], g_rel, _diag_blocks(g_wa_bd),
                  _diag_blocks(g_wx_bd)]
    small4 = jnp.concatenate([_shard_major(g, ax) for g, ax in sharded]
                             + [jnp.broadcast_to(g.reshape(1, -1), (4, g.size)) for g in replicated], axis=1)
    n_small = small4.shape[1]
    small_rows = GC_ROWS - GC_TAIL - 774
    small4 = jnp.pad(small4, ((0, 0), (0, small_rows * D - n_small))).reshape(4, small_rows, D)
    gt_rows = jnp.concatenate([gt_in_e[:1536], gt_in_e[3072:3088], gt_in_e[1536:3072], gt_in_e[3088:3096]], axis=0)
    tail = jnp.concatenate([gt_rows.reshape(4, 774, D), small4.astype(BF16)], axis=1)
    gbc = lax.dynamic_update_slice(gbc, tail, (0, GC_TAIL, 0))
    rs_c = rs_begin(gbc, "c")

    red_a = rs_end(rs_a, rs_c[4], "a")
    red_b = rs_end(rs_b, red_a, "b")
    red_c = rs_end(rs_c, red_b, "c")

    g_up = jnp.stack([red_b[GB_UP:GB_UP + 1024].T, red_a[GA_UP:GA_UP + 1024].T])
    g_dn = jnp.stack([red_b[GB_DN:GB_DN + 1024], red_a[GA_DN:GA_DN + 1024]])
    g_small = _split(red_c[GC_TAIL + 774:].reshape(-1)[:n_small], SMALL_SHARDED_SHAPES + REPL_SHAPES)
    g_of = dict(zip(["norm_w", "gla_w_a_up", "conv_w", "conv_b", "lru_b_a", "lru_b_x", "lru_lambda", "gla_b_a",
                     "gla_norm_w", "fox_b_f", "rel_bias", "lru_w_a", "lru_w_x"], g_small))
    g_of.update(w_mlp_up=g_up, w_mlp_down=g_dn, w_in_odd=red_a[GA_IN_O:GA_IN_O + 640].T,
                w_out_even=red_c[GC_OUT_E:GC_OUT_E + 256], w_out_odd=red_a[GA_OUT_O:GA_OUT_O + 256],
                w_in_even=red_c[GC_TAIL:GC_TAIL + 774])

    names = ["norm_w", "w_in_even", "gla_w_a_up", "gla_b_a", "gla_norm_w", "fox_b_f", "w_out_even", "w_in_odd", "rel_bias",
             "conv_w", "conv_b", "lru_w_a", "lru_b_a", "lru_w_x", "lru_b_x", "lru_lambda", "w_out_odd", "w_mlp_up",
             "w_mlp_down"]
    w_of = dict(norm_w=norm_w, w_in_even=w_in_even, gla_w_a_up=gla_w_a_up, gla_b_a=gla_b_a, gla_norm_w=gla_norm_w,
                fox_b_f=fox_b_f, w_out_even=w_out_even, w_in_odd=w_in_odd, rel_bias=rel_bias, conv_w=conv_w, conv_b=conv_b,
                lru_w_a=lru_w_a, lru_b_a=lru_b_a, lru_w_x=lru_w_x, lru_b_x=lru_b_x, lru_lambda=lru_lambda,
                w_out_odd=w_out_odd, w_mlp_up=w_mlp_up, w_mlp_down=w_mlp_down)
    m_of = dict(norm_w=m_norm_w, w_in_even=m_w_in_even, gla_w_a_up=m_gla_w_a_up, gla_b_a=m_gla_b_a,
                gla_norm_w=m_gla_norm_w, fox_b_f=m_fox_b_f, w_out_even=m_w_out_even, w_in_odd=m_w_in_odd,
                rel_bias=m_rel_bias, conv_w=m_conv_w, conv_b=m_conv_b, lru_w_a=m_lru_w_a, lru_b_a=m_lru_b_a,
                lru_w_x=m_lru_w_x, lru_b_x=m_lru_b_x, lru_lambda=m_lru_lambda, w_out_odd=m_w_out_odd,
                w_mlp_up=m_w_mlp_up, w_mlp_down=m_w_mlp_down)
    v_of = dict(norm_w=v_norm_w, w_in_even=v_w_in_even, gla_w_a_up=v_gla_w_a_up, gla_b_a=v_gla_b_a,
                gla_norm_w=v_gla_norm_w, fox_b_f=v_fox_b_f, w_out_even=v_w_out_even, w_in_odd=v_w_in_odd,
                rel_bias=v_rel_bias, conv_w=v_conv_w, conv_b=v_conv_b, lru_w_a=v_lru_w_a, lru_b_a=v_lru_b_a,
                lru_w_x=v_lru_w_x, lru_b_x=v_lru_b_x, lru_lambda=v_lru_lambda, w_out_odd=v_w_out_odd,
                w_mlp_up=v_w_mlp_up, w_mlp_down=v_w_mlp_down)
    grads, deltas, new_ms, new_vs = [], [], [], []
    for n in names:
        w = w_of[n]
        if n == "w_in_even":
            to_view = lambda a: a[0].T
            from_view = lambda a: a.T[None]
        else:
            view = w.shape if w.ndim <= 3 else w.shape[-3:]
            to_view = lambda a, view=view: a.reshape(view)
            from_view = lambda a, w=w: a.reshape(w.shape)
        g = g_of[n] if n == "w_in_even" else to_view(g_of[n])
        d, mn, vn = _adamw(to_view(w), g, to_view(m_of[n]), to_view(v_of[n]), "adamw_" + n)
        grads.append(from_view(g))
        deltas.append(from_view(d))
        new_ms.append(from_view(mn))
        new_vs.append(from_view(vn))

    return (loss, grad_x.reshape(1, T, D), *grads, *deltas, *new_ms, *new_vs)
```

```python
import functools

import jax
import jax.numpy as jnp
from jax import lax
from jax.experimental import pallas as pl
from jax.experimental.pallas import tpu as pltpu

F32 = jnp.float32
BF16 = jnp.bfloat16
MESH = pl.DeviceIdType.MESH

T = 2048
D = 1024
DFF = 4096
EPS = 1e-6
CHUNK = 64
NCHUNK = T // CHUNK
PE = 3200
PO = 2560
AUX_BLK = 3072 // 128
FOX_LANE0 = 16
GLA_SCALE = 64 ** -0.5
ATT_SCALE = 64 ** -0.5
NEG = float(jnp.finfo(jnp.float32).min)
CA_BAND = 576
CA_PAD = 512
REL_PAD = 384

VMEM_LIMIT = 48 * 1024 * 1024

ADAM_LR, ADAM_B1, ADAM_B2, ADAM_EPS, ADAM_WD, ADAM_STEP = 0.001, 0.9, 0.999, 1e-08, 0.01, 10

GA_ROWS, GA_UP, GA_DN, GA_IN_O, GA_OUT_O, GA_USED = 3072, 0, 1024, 2048, 2688, 2944
GB_ROWS, GB_UP, GB_DN = 2048, 0, 1024
GC_ROWS, GC_OUT_E, GC_TAIL = 1152, 0, 256

_DIMS = {"nn": (((1,), (0,)), ((), ())), "nt": (((1,), (1,)), ((), ())), "tn": (((0,), (0,)), ((), ()))}


def _cp(sem, **kw):
    return pltpu.CompilerParams(dimension_semantics=sem, vmem_limit_bytes=VMEM_LIMIT, **kw)


def _dot(a, b, mode):
    return lax.dot_general(a.astype(BF16), b.astype(BF16), _DIMS[mode], preferred_element_type=F32)


@functools.partial(jax.custom_vjp, nondiff_argnums=(2,))
def bdot(a, b, mode):
    return _dot(a, b, mode)


def _bdot_fwd(a, b, mode):
    return _dot(a, b, mode), (a, b)


def _bdot_bwd(mode, res, g):
    a, b = res
    if mode == "nn":
        da, db = _dot(g, b, "nt"), _dot(a, g, "tn")
    elif mode == "nt":
        da, db = _dot(g, b, "nn"), _dot(g, a, "tn")
    else:
        da, db = _dot(b, g, "nt"), _dot(a, g, "nn")
    return da.astype(a.dtype), db.astype(b.dtype)


bdot.defvjp(_bdot_fwd, _bdot_bwd)


def _hdot_raw(a, b, mode):
    return lax.dot_general(a, b, _DIMS[mode], precision=lax.Precision.HIGHEST, preferred_element_type=F32)


@functools.partial(jax.custom_vjp, nondiff_argnums=(2,))
def hdot(a, b, mode):
    return _hdot_raw(a, b, mode)


def _hdot_fwd(a, b, mode):
    return _hdot_raw(a, b, mode), (a, b)


def _hdot_bwd(mode, res, g):
    a, b = res
    if mode == "nn":
        return _hdot_raw(g, b, "nt"), _hdot_raw(a, g, "tn")
    if mode == "nt":
        return _hdot_raw(g, b, "nn"), _hdot_raw(g, a, "tn")
    return _hdot_raw(b, g, "nt"), _hdot_raw(a, g, "nn")


hdot.defvjp(_hdot_fwd, _hdot_bwd)


def _log_sigmoid(x):
    return jnp.minimum(x, 0.0) - jnp.log(1.0 + jnp.exp(-jnp.abs(x)))


def _sigmoid(x):
    return 1.0 / (1.0 + jnp.exp(-x))


def _expm1(x):
    series = x * (1.0 + x * 0.5 * (1.0 + x * (1.0 / 3.0) * (1.0 + x * 0.25)))
    return jnp.where(jnp.abs(x) < 0.03, series, jnp.exp(x) - 1.0)


def _gelu_tanh(x):
    return 0.5 * x * (1.0 + jnp.tanh(0.7978845608028654 * (x + 0.044715 * x * x * x)))


def _softmax_rows(s):
    m = jnp.max(s, axis=-1, keepdims=True)
    p = jnp.exp(s - m)
    return p / jnp.sum(p, axis=-1, keepdims=True)


def _iota(shape, dim):
    return lax.broadcasted_iota(jnp.int32, shape, dim)


def _mm(a, b, mode, *, tm, tn, tk=None, out_dtype=F32, name, b_layer=None, into=None, relu_pair=False, times2=None):
    b2 = b.shape[-2:]
    if mode == "nn":
        (m, k), n = a.shape, b2[1]
    elif mode == "nt":
        (m, k), n = a.shape, b2[0]
    else:
        (k, m), n = a.shape, b2[1]
    tk = k if tk is None else tk
    assert m % tm == 0 and n % tn == 0 and k % tk == 0, (name, a.shape, b.shape)
    nk = k // tk
    a_spec = {"nn": pl.BlockSpec((tm, tk), lambda i, j, kk: (i, kk)),
              "nt": pl.BlockSpec((tm, tk), lambda i, j, kk: (i, kk)),
              "tn": pl.BlockSpec((tk, tm), lambda i, j, kk: (kk, i))}[mode]
    b_blk = {"nn": (tk, tn), "nt": (tn, tk), "tn": (tk, tn)}[mode]
    b_idx = {"nn": lambda i, j, kk: (kk, j), "nt": lambda i, j, kk: (j, kk), "tn": lambda i, j, kk: (kk, j)}[mode]
    if b_layer is None:
        b_spec = pl.BlockSpec(b_blk, b_idx)
    else:
        b_spec = pl.BlockSpec((None,) + b_blk, lambda i, j, kk: (b_layer,) + b_idx(i, j, kk))

    tile = pl.BlockSpec((tm, tn), lambda i, j, kk: (i, j))
    if into is not None:
        buf, per_slot, row_off = into
        assert m == 4 * per_slot and per_slot % tm == 0 and row_off % tm == 0 and buf.shape[2] == n, (name, buf.shape)
        bps = per_slot // tm
        out_specs = pl.BlockSpec((None, tm, tn), lambda i, j, kk: (i // bps, row_off // tm + i % bps, j))
        out_shape = jax.ShapeDtypeStruct(buf.shape, buf.dtype)
        extra_in, extra_specs, aliases = [buf], [pl.BlockSpec(memory_space=pl.ANY)], {2: 0}
        finish = lambda acc, extra: [acc.astype(buf.dtype)]
    elif relu_pair:
        out_specs = (tile, tile)
        out_shape = (jax.ShapeDtypeStruct((m, n), BF16),) * 2
        extra_in, extra_specs, aliases = [], [], {}

        def finish(acc, extra):
            r = jnp.maximum(acc, 0.0)
            return [(r * r).astype(BF16), r.astype(BF16)]
    elif times2 is not None:
        out_specs = tile
        out_shape = jax.ShapeDtypeStruct((m, n), out_dtype)
        extra_in, extra_specs, aliases = [times2], [tile], {}
        finish = lambda acc, extra: [(acc * (2.0 * extra[...].astype(F32))).astype(out_dtype)]
    else:
        out_specs = tile
        out_shape = jax.ShapeDtypeStruct((m, n), out_dtype)
        extra_in, extra_specs, aliases = [], [], {}
        finish = lambda acc, extra: [acc.astype(out_dtype)]
    n_out = 2 if relu_pair else 1

    def body(*refs):
        a_ref, b_ref = refs[0], refs[1]
        extra = refs[2] if extra_in else None
        o_refs = refs[2 + len(extra_in):2 + len(extra_in) + n_out]

        def store(acc):
            for o_ref, val in zip(o_refs, finish(acc, extra)):
                o_ref[...] = val

        if nk == 1:
            store(_dot(a_ref[...], b_ref[...], mode))
            return
        acc_ref = refs[-1]
        kk = pl.program_id(2)

        @pl.when(kk == 0)
        def _():
            acc_ref[...] = jnp.zeros_like(acc_ref)

        acc_ref[...] += _dot(a_ref[...], b_ref[...], mode)

        @pl.when(kk == nk - 1)
        def _():
            store(acc_ref[...])

    return pl.pallas_call(
        body, name=name, grid=(m // tm, n // tn, nk),
        in_specs=[a_spec, b_spec] + extra_specs,
        out_specs=out_specs, out_shape=out_shape,
        scratch_shapes=[pltpu.VMEM((tm, tn), F32)] if nk > 1 else [],
        input_output_aliases=aliases,
        compiler_params=_cp(("parallel", "parallel", "arbitrary")),
    )(a, b, *extra_in)


ROWS = 256


def _prenorm(x, w, name):
    def body(x_ref, w_ref, o_ref):
        xv = x_ref[...]
        r = lax.rsqrt(jnp.mean(xv * xv, axis=-1, keepdims=True) + EPS)
        o_ref[...] = (xv * r * w_ref[...]).astype(BF16)

    return pl.pallas_call(
        body, name=name, grid=(T // ROWS,),
        in_specs=[pl.BlockSpec((ROWS, D), lambda i: (i, 0)), pl.BlockSpec((1, D), lambda i: (0, 0))],
        out_specs=pl.BlockSpec((ROWS, D), lambda i: (i, 0)),
        out_shape=jax.ShapeDtypeStruct((T, D), BF16),
        compiler_params=_cp(("parallel",)),
    )(x, w)


def _postnorm(x, z, w, name):
    def body(x_ref, z_ref, w_ref, o_ref):
        zv = z_ref[...]
        r = lax.rsqrt(jnp.mean(zv * zv, axis=-1, keepdims=True) + EPS)
        o_ref[...] = x_ref[...] + zv * r * w_ref[...]

    return pl.pallas_call(
        body, name=name, grid=(T // ROWS,),
        in_specs=[pl.BlockSpec((ROWS, D), lambda i: (i, 0)), pl.BlockSpec((ROWS, D), lambda i: (i, 0)),
                  pl.BlockSpec((1, D), lambda i: (0, 0))],
        out_specs=pl.BlockSpec((ROWS, D), lambda i: (i, 0)),
        out_shape=jax.ShapeDtypeStruct((T, D), F32),
        compiler_params=_cp(("parallel",)),
    )(x, z, w)


def _norm_bwd(z, w, dy, add, name):
    has_add = add is not None

    def body(*refs):
        if has_add:
            z_ref, w_ref, dy_ref, add_ref, dz_ref, dw_ref = refs
        else:
            z_ref, w_ref, dy_ref, dz_ref, dw_ref = refs
        i = pl.program_id(0)

        @pl.when(i == 0)
        def _():
            dw_ref[...] = jnp.zeros_like(dw_ref)

        zv = z_ref[...].astype(F32)
        dyv = dy_ref[...]
        r = lax.rsqrt(jnp.mean(zv * zv, axis=-1, keepdims=True) + EPS)
        wdy = dyv * w_ref[...]
        dz = r * wdy - zv * (r * r * r) * jnp.mean(zv * wdy, axis=-1, keepdims=True)
        if has_add:
            dz = dz + add_ref[...]
        dz_ref[...] = dz.astype(dz_ref.dtype)
        dw_ref[...] += jnp.sum(dyv * zv * r, axis=0, keepdims=True)

    row = pl.BlockSpec((ROWS, D), lambda i: (i, 0))
    vec = pl.BlockSpec((1, D), lambda i: (0, 0))
    ins = [z, w, dy] + ([add] if has_add else [])
    dz_dtype = F32 if has_add else BF16
    return pl.pallas_call(
        body, name=name, grid=(T // ROWS,),
        in_specs=[row, vec, row] + ([row] if has_add else []),
        out_specs=(row, vec),
        out_shape=(jax.ShapeDtypeStruct((T, D), dz_dtype), jax.ShapeDtypeStruct((1, D), F32)),
        compiler_params=_cp(("arbitrary",)),
    )(*ins)


def _loss_and_grad(y, tgt):
    def body(y_ref, t_ref, g_ref, l_ref):
        i = pl.program_id(0)

        @pl.when(i == 0)
        def _():
            l_ref[...] = jnp.zeros_like(l_ref)

        e = y_ref[...] - t_ref[...]
        g_ref[...] = e * (1.0 / D)
        l_ref[...] += jnp.sum(e * e) * (0.5 / D)

    row = pl.BlockSpec((ROWS, D), lambda i: (i, 0))
    return pl.pallas_call(
        body, name="loss_head", grid=(T // ROWS,), in_specs=[row, row],
        out_specs=(row, pl.BlockSpec((1, 128), lambda i: (0, 0))),
        out_shape=(jax.ShapeDtypeStruct((T, D), F32), jax.ShapeDtypeStruct((1, 128), F32)),
        compiler_params=_cp(("arbitrary",)),
    )(y, tgt)


def _adamw(w, g, m, v, name):
    lead = w.shape[:-2]
    assert len(lead) <= 1 and g.shape == w.shape, (name, w.shape, g.shape)
    rows, cols = w.shape[-2:]
    if rows <= 512:
        tr, tc = rows, cols
    elif rows % 256 == 0:
        tr, tc = 256, cols
    else:
        tr, tc = rows, 256
    assert rows % tr == 0 and cols % tc == 0, (name, w.shape)
    c1 = 1.0 - ADAM_B1 ** ADAM_STEP
    c2 = 1.0 - ADAM_B2 ** ADAM_STEP

    def body(w_ref, g_ref, m_ref, v_ref, d_ref, mo_ref, vo_ref):
        gv = g_ref[...]
        mn = ADAM_B1 * m_ref[...] + (1.0 - ADAM_B1) * gv
        vn = ADAM_B2 * v_ref[...] + (1.0 - ADAM_B2) * (gv * gv)
        m_hat = mn / c1
        v_hat = vn / c2
        d_ref[...] = -ADAM_LR * (m_hat / (jnp.sqrt(v_hat) + ADAM_EPS) + ADAM_WD * w_ref[...])
        mo_ref[...] = mn
        vo_ref[...] = vn

    if lead:
        grid = (lead[0], rows // tr, cols // tc)
        blk = pl.BlockSpec((None, tr, tc), lambda l, i, j: (l, i, j))
    else:
        grid = (rows // tr, cols // tc)
        blk = pl.BlockSpec((tr, tc), lambda i, j: (i, j))
    sds = jax.ShapeDtypeStruct(w.shape, F32)
    return pl.pallas_call(body, name=name, grid=grid, in_specs=[blk] * 4, out_specs=(blk,) * 3,
                          out_shape=(sds,) * 3, compiler_params=_cp(("parallel",) * len(grid)))(w, g, m, v)


def _gla_consts():
    ltri = (_iota((CHUNK, CHUNK), 0) >= _iota((CHUNK, CHUNK), 1)).astype(F32)
    ones_c = jnp.ones((CHUNK, 128), F32)
    mask = (_iota((256, 512), 0) // 64 == _iota((256, 512), 1) // 128).astype(F32)
    return ltri, ones_c, mask


def _gla_chunk(consts, q, k, v, r, aux, s_prev, wa, ba, nw):
    ltri, ones_c, mask = consts
    la = _log_sigmoid(bdot(aux, wa, "nn") + ba) * (1.0 / 16.0)
    cum = hdot(ltri, la, "nn")
    total = jnp.sum(la, axis=0, keepdims=True)
    k_dec = k * jnp.exp(total - cum)
    inc = bdot(k_dec, v, "tn") * mask
    dec = jnp.exp(hdot(la, ones_c, "tn"))
    dec = jnp.concatenate([dec, dec, dec, dec], axis=1)
    s_new = dec * s_prev + inc
    o = bdot(q * GLA_SCALE, s_new, "nn")
    parts = []
    for h in range(4):
        oh = o[:, h * 128:(h + 1) * 128]
        parts.append(oh * lax.rsqrt(jnp.mean(oh * oh, axis=-1, keepdims=True) + EPS))
    on = jnp.concatenate(parts, axis=1)
    return s_new, on * nw * (r * _sigmoid(r))


def _gla_specs(cmap):
    return [pl.BlockSpec((CHUNK, 256), lambda c: (cmap(c), 0)),
            pl.BlockSpec((CHUNK, 256), lambda c: (cmap(c), 1)),
            pl.BlockSpec((CHUNK, 512), lambda c: (cmap(c), 1)),
            pl.BlockSpec((CHUNK, 512), lambda c: (cmap(c), 2)),
            pl.BlockSpec((CHUNK, 128), lambda c: (cmap(c), AUX_BLK))]


def _gla_fwd(proj, wa, ba, nw):
    def body(q_ref, k_ref, v_ref, r_ref, aux_ref, wa_ref, ba_ref, nw_ref, o_ref, sp_ref, s_ref):
        c = pl.program_id(0)

        @pl.when(c == 0)
        def _():
            s_ref[...] = jnp.zeros_like(s_ref)

        s_prev = s_ref[...]
        sp_ref[...] = s_prev
        s_new, out = _gla_chunk(_gla_consts(), q_ref[...], k_ref[...], v_ref[...], r_ref[...], aux_ref[...],
                                s_prev, wa_ref[...], ba_ref[...], nw_ref[...])
        s_ref[...] = s_new
        o_ref[...] = out

    full = lambda shape: pl.BlockSpec(shape, lambda c: (0,) * len(shape))
    return pl.pallas_call(
        body, name="gla_fwd", grid=(NCHUNK,),
        in_specs=_gla_specs(lambda c: c) + [full((128, 256)), full((1, 256)), full((1, 512))],
        out_specs=(pl.BlockSpec((CHUNK, 512), lambda c: (c, 0)), pl.BlockSpec((None, 256, 512), lambda c: (c, 0, 0))),
        out_shape=(jax.ShapeDtypeStruct((T, D), F32), jax.ShapeDtypeStruct((NCHUNK, 256, 512), F32)),
        scratch_shapes=[pltpu.VMEM((256, 512), F32)],
        compiler_params=_cp(("arbitrary",)),
    )(proj, proj, proj, proj, proj, wa, ba, nw)


def _gla_bwd(proj, s_prev_all, wa, ba, nw, dcat):
    rev = lambda c: NCHUNK - 1 - c

    def body(q_ref, k_ref, v_ref, r_ref, aux_ref, sp_ref, wa_ref, ba_ref, nw_ref, do_ref,
             dq_ref, dk_ref, dv_ref, dr_ref, daux_ref, dwa_ref, dba_ref, dnw_ref, ds_ref):
        c = pl.program_id(0)

        @pl.when(c == 0)
        def _():
            ds_ref[...] = jnp.zeros_like(ds_ref)
            dwa_ref[...] = jnp.zeros_like(dwa_ref)
            dba_ref[...] = jnp.zeros_like(dba_ref)
            dnw_ref[...] = jnp.zeros_like(dnw_ref)

        fn = functools.partial(_gla_chunk, _gla_consts())
        _, vjp = jax.vjp(fn, q_ref[...], k_ref[...], v_ref[...], r_ref[...], aux_ref[...], sp_ref[...],
                         wa_ref[...], ba_ref[...], nw_ref[...])
        dq, dk, dv, dr, daux, dsp, dwa, dba, dnw = vjp((ds_ref[...], do_ref[...]))
        dq_ref[...] = dq
        dk_ref[...] = dk
        dv_ref[...] = dv
        dr_ref[...] = dr
        daux_ref[...] = daux
        ds_ref[...] = dsp
        dwa_ref[...] += dwa
        dba_ref[...] += dba
        dnw_ref[...] += dnw

    full = lambda shape: pl.BlockSpec(shape, lambda c: (0,) * len(shape))
    blk = lambda w: pl.BlockSpec((CHUNK, w), lambda c: (rev(c), 0))
    sds = lambda *s: jax.ShapeDtypeStruct(s, F32)
    return pl.pallas_call(
        body, name="gla_bwd", grid=(NCHUNK,),
        in_specs=_gla_specs(rev) + [pl.BlockSpec((None, 256, 512), lambda c: (rev(c), 0, 0)),
                                    full((128, 256)), full((1, 256)), full((1, 512)), blk(512)],
        out_specs=(blk(256), blk(256), blk(512), blk(512), blk(128), full((128, 256)), full((1, 256)), full((1, 512))),
        out_shape=(sds(T, 256), sds(T, 256), sds(T, 512), sds(T, 512), sds(T, 128),
                   sds(128, 256), sds(1, 256), sds(1, 512)),
        scratch_shapes=[pltpu.VMEM((256, 512), F32)],
        compiler_params=_cp(("arbitrary",)),
    )(proj, proj, proj, proj, proj, s_prev_all, wa, ba, nw, dcat)


GATE_ROWS = 128


def _fox_gate_block(ltri, aux, bpad, carry):
    lf = _log_sigmoid(aux + bpad)
    cum = hdot(ltri, lf, "nn") + carry
    return cum, carry + jnp.sum(lf, axis=0, keepdims=True)


def _gate_ltri():
    return (_iota((GATE_ROWS, GATE_ROWS), 0) >= _iota((GATE_ROWS, GATE_ROWS), 1)).astype(F32)


def _fox_gate_fwd(proj, bpad):
    def body(aux_ref, b_ref, cum_ref, carry_ref):
        i = pl.program_id(0)

        @pl.when(i == 0)
        def _():
            carry_ref[...] = jnp.zeros_like(carry_ref)

        cum, carry = _fox_gate_block(_gate_ltri(), aux_ref[...], b_ref[...], carry_ref[...])
        cum_ref[...] = cum
        carry_ref[...] = carry

    return pl.pallas_call(
        body, name="fox_gate_fwd", grid=(T // GATE_ROWS,),
        in_specs=[pl.BlockSpec((GATE_ROWS, 128), lambda i: (i, AUX_BLK)), pl.BlockSpec((1, 128), lambda i: (0, 0))],
        out_specs=pl.BlockSpec((GATE_ROWS, 128), lambda i: (i, 0)),
        out_shape=jax.ShapeDtypeStruct((T, 128), F32),
        scratch_shapes=[pltpu.VMEM((1, 128), F32)],
        compiler_params=_cp(("arbitrary",)),
    )(proj, bpad)


def _fox_gate_bwd(proj, bpad, dccol_t, daux_gla):
    nb = T // GATE_ROWS
    rev = lambda i: nb - 1 - i

    def body(aux_ref, b_ref, dc_ref, dg_ref, daux_ref, db_ref, dcarry_ref):
        i = pl.program_id(0)

        @pl.when(i == 0)
        def _():
            dcarry_ref[...] = jnp.zeros_like(dcarry_ref)
            db_ref[...] = jnp.zeros_like(db_ref)

        dcum = dc_ref[...]
        fn = functools.partial(_fox_gate_block, _gate_ltri())
        _, vjp = jax.vjp(fn, aux_ref[...], b_ref[...], jnp.zeros((1, 128), F32))
        daux, db, dcarry = vjp((dcum, dcarry_ref[...]))
        daux_ref[...] = daux + dg_ref[...]
        db_ref[...] += db
        dcarry_ref[...] = dcarry

    blk = pl.BlockSpec((GATE_ROWS, 128), lambda i: (rev(i), 0))
    vec = pl.BlockSpec((1, 128), lambda i: (0, 0))
    return pl.pallas_call(
        body, name="fox_gate_bwd", grid=(nb,),
        in_specs=[pl.BlockSpec((GATE_ROWS, 128), lambda i: (rev(i), AUX_BLK)), vec, blk, blk],
        out_specs=(blk, vec),
        out_shape=(jax.ShapeDtypeStruct((T, 128), F32), jax.ShapeDtypeStruct((1, 128), F32)),
        scratch_shapes=[pltpu.VMEM((1, 128), F32)],
        compiler_params=_cp(("arbitrary",)),
    )(proj, bpad, dccol_t, daux_gla)


FOX_Q = 128


FOX_QB = T // FOX_Q


@jax.custom_vjp
def _attend(s, v):
    return _attend_fwd(s, v)[0]


def _attend_fwd(s, v):
    e = jnp.exp(s - jnp.max(s, axis=-1, keepdims=True))
    r = 1.0 / jnp.sum(e, axis=-1, keepdims=True)
    return _dot(e, v, "nn") * r, (e, r, v)


def _attend_bwd(res, do):
    e, r, v = res
    do_r = do * r
    dpr = _dot(do_r, v, "nt")
    ds = e * (dpr - r * jnp.sum(e * dpr, axis=-1, keepdims=True))
    return ds, _dot(e, do_r, "tn").astype(v.dtype)


_attend.defvjp(_attend_fwd, _attend_bwd)


def _fox_block(hp, q, k, v, ccol):
    kl = k.shape[0]
    lane = _iota((FOX_Q, 128), 1)
    tri = _iota((FOX_Q, FOX_Q), 0) >= _iota((FOX_Q, FOX_Q), 1)
    sub = _iota((8, kl), 0)
    outs = []
    for e in range(2):
        qm = jnp.where((lane >= 64 * e) & (lane < 64 * (e + 1)), q * ATT_SCALE, 0.0)
        cs = jnp.sum(jnp.where(sub == 2 * hp + e, ccol, 0.0), axis=0, keepdims=True)
        s = bdot(qm, k, "nt") - cs
        diag = jnp.where(tri, s[:, kl - FOX_Q:], NEG)
        s = diag if kl == FOX_Q else jnp.concatenate([s[:, :kl - FOX_Q], diag], axis=1)
        outs.append(_attend(s, v))
    return jnp.where(lane < 64, outs[0], outs[1])


def _fox_in_specs():
    return [pl.BlockSpec((FOX_Q, 128), lambda hp, qb: (qb, 12 + hp)),
            pl.BlockSpec((T, 128), lambda hp, qb: (0, 16 + hp)),
            pl.BlockSpec((T, 128), lambda hp, qb: (0, 20 + hp)),
            pl.BlockSpec((8, T), lambda hp, qb: (0, 0))]


def _fox_fwd(proj, cum_c, cat):
    def body(q_ref, k_ref, v_ref, cc_ref, cat_ref, o_ref):
        qb = pl.program_id(1)
        for g in range(FOX_QB):
            kl = FOX_Q * (g + 1)

            @pl.when(qb == g)
            def _(kl=kl):
                o_ref[...] = _fox_block(pl.program_id(0), q_ref[...], k_ref[0:kl, :], v_ref[0:kl, :], cc_ref[:, 0:kl])

    return pl.pallas_call(
        body, name="fox_fwd", grid=(4, FOX_QB), in_specs=_fox_in_specs() + [pl.BlockSpec(memory_space=pl.ANY)],
        out_specs=pl.BlockSpec((FOX_Q, 128), lambda hp, qb: (qb, 4 + hp)),
        out_shape=jax.ShapeDtypeStruct((T, D), F32), input_output_aliases={4: 0},
        compiler_params=_cp(("parallel", "parallel")),
    )(proj, proj, proj, cum_c, cat)


def _fox_bwd(proj, cum_c, dcat):
    def body(q_ref, k_ref, v_ref, cc_ref, do_ref, dq_ref, dk_ref, dv_ref, dcc_ref):
        qb = pl.program_id(1)

        @pl.when(qb == 0)
        def _():
            dk_ref[...] = jnp.zeros_like(dk_ref)
            dv_ref[...] = jnp.zeros_like(dv_ref)
            dcc_ref[...] = jnp.zeros_like(dcc_ref)

        fn = functools.partial(_fox_block, pl.program_id(0))
        for g in range(FOX_QB):
            kl = FOX_Q * (g + 1)

            @pl.when(qb == g)
            def _(kl=kl):
                _, vjp = jax.vjp(fn, q_ref[...], k_ref[0:kl, :], v_ref[0:kl, :], cc_ref[:, 0:kl])
                dq, dk, dv, dcc = vjp(do_ref[...])
                dq_ref[...] = dq
                dk_ref[0:kl, :] += dk
                dv_ref[0:kl, :] += dv
                dcc_ref[:, 0:kl] += dcc

    sds = lambda *s: jax.ShapeDtypeStruct(s, F32)
    return pl.pallas_call(
        body, name="fox_bwd", grid=(4, FOX_QB),
        in_specs=_fox_in_specs() + [pl.BlockSpec((FOX_Q, 128), lambda hp, qb: (qb, 4 + hp))],
        out_specs=(pl.BlockSpec((FOX_Q, 128), lambda hp, qb: (qb, hp)),
                   pl.BlockSpec((T, 128), lambda hp, qb: (0, hp)),
                   pl.BlockSpec((T, 128), lambda hp, qb: (0, hp)),
                   pl.BlockSpec((None, 8, T), lambda hp, qb: (hp, 0, 0))),
        out_shape=(sds(T, 512), sds(T, 512), sds(T, 512), sds(4, 8, T)),
        compiler_params=_cp(("parallel", "arbitrary")),
    )(proj, proj, proj, cum_c, dcat)


BIAS_W = 640


def _rel_onehot():
    j = _iota((REL_PAD, BIAS_W), 1)
    rel = jnp.clip(CA_PAD + CHUNK - 1 - j, -128, 128) + 128
    return (_iota((REL_PAD, BIAS_W), 0) == rel).astype(F32)


def _bias_build(rbp):
    def body(rb_ref, o_ref):
        f = _hdot_raw(rb_ref[...], _rel_onehot(), "nn")
        for q in range(CHUNK):
            o_ref[q] = pltpu.roll(f, (BIAS_W - (CHUNK - 1 - q)) % BIAS_W, 1)[:, :CA_BAND]

    return pl.pallas_call(body, name="ca_bias_build", out_shape=jax.ShapeDtypeStruct((CHUNK, 8, CA_BAND), F32))(rbp)


def _bias_grad(dbias_q):
    def body(db_ref, o_ref):
        acc = jnp.zeros((8, BIAS_W), F32)
        for q in range(CHUNK):
            acc = acc + pltpu.roll(db_ref[q], CHUNK - 1 - q, 1)
        o_ref[...] = _hdot_raw(acc, _rel_onehot(), "nt")

    return pl.pallas_call(body, name="ca_bias_grad", out_shape=jax.ShapeDtypeStruct((8, REL_PAD), F32))(dbias_q)


def _ca_block(c, masked, q, kb, vb, bias2):
    lane = _iota((CHUNK, 128), 1)
    outs = []
    for e in range(2):
        qm = jnp.where((lane >= 64 * e) & (lane < 64 * (e + 1)), q * ATT_SCALE, 0.0)
        s = bdot(qm, kb, "nt") + bias2[e]
        if masked:
            s = jnp.where((c * CHUNK - CA_PAD + _iota((CHUNK, CA_BAND), 1)) >= 0, s, NEG)
        outs.append(_attend(s, vb))
    return jnp.where(lane < 64, outs[0], outs[1])


CA_PER_STEP = 4
CA_ROWS = CA_PER_STEP * CHUNK
CA_MASKED_STEPS = CA_PAD // CA_ROWS


def _ca_fwd(proj, kvpad, bias):
    def body(q_ref, k_ref, v_ref, b_ref, o_ref):
        def run(masked):
            for i in range(CA_PER_STEP):
                c = pl.program_id(1) * CA_PER_STEP + i
                band = pl.ds(pl.multiple_of(c * CHUNK, CHUNK), CA_BAND)
                rows = slice(i * CHUNK, (i + 1) * CHUNK)
                o_ref[rows, :] = _ca_block(c, masked, q_ref[rows, :], k_ref[band, :], v_ref[band, :], b_ref[...])

        pl.when(pl.program_id(1) < CA_MASKED_STEPS)(lambda: run(True))
        pl.when(pl.program_id(1) >= CA_MASKED_STEPS)(lambda: run(False))

    return pl.pallas_call(
        body, name="ca_fwd", grid=(4, NCHUNK // CA_PER_STEP),
        in_specs=[pl.BlockSpec((CA_ROWS, 128), lambda hp, c: (c, hp)),
                  pl.BlockSpec((T + CA_PAD, 128), lambda hp, c: (0, hp)),
                  pl.BlockSpec((T + CA_PAD, 128), lambda hp, c: (0, 4 + hp)),
                  pl.BlockSpec((2, CHUNK, CA_BAND), lambda hp, c: (hp, 0, 0))],
        out_specs=pl.BlockSpec((CA_ROWS, 128), lambda hp, c: (c, hp)),
        out_shape=jax.ShapeDtypeStruct((T, D), F32),
        compiler_params=_cp(("parallel", "parallel")),
    )(proj, kvpad, kvpad, bias)


def _ca_bwd(proj, kvpad, bias, dcat):
    def body(q_ref, k_ref, v_ref, b_ref, do_ref, dq_ref, dk_ref, dv_ref, db_ref):
        c = pl.program_id(1)

        @pl.when(c == 0)
        def _():
            dk_ref[...] = jnp.zeros_like(dk_ref)
            dv_ref[...] = jnp.zeros_like(dv_ref)
            db_ref[...] = jnp.zeros_like(db_ref)

        def run(masked):
            for i in range(CA_PER_STEP):
                ci = c * CA_PER_STEP + i
                band = pl.ds(pl.multiple_of(ci * CHUNK, CHUNK), CA_BAND)
                rows = slice(i * CHUNK, (i + 1) * CHUNK)
                fn = functools.partial(_ca_block, ci, masked)
                _, vjp = jax.vjp(fn, q_ref[rows, :], k_ref[band, :], v_ref[band, :], b_ref[...])
                dq, dkb, dvb, db = vjp(do_ref[rows, :])
                dq_ref[rows, :] = dq
                dk_ref[band, :] += dkb
                dv_ref[band, :] += dvb
                db_ref[...] += db

        pl.when(c < CA_MASKED_STEPS)(lambda: run(True))
        pl.when(c >= CA_MASKED_STEPS)(lambda: run(False))

    sds = lambda *s: jax.ShapeDtypeStruct(s, F32)
    padded = lambda: pl.BlockSpec((T + CA_PAD, 128), lambda hp, c: (0, hp))
    return pl.pallas_call(
        body, name="ca_bwd", grid=(4, NCHUNK // CA_PER_STEP),
        in_specs=[pl.BlockSpec((CA_ROWS, 128), lambda hp, c: (c, hp)),
                  pl.BlockSpec((T + CA_PAD, 128), lambda hp, c: (0, hp)),
                  pl.BlockSpec((T + CA_PAD, 128), lambda hp, c: (0, 4 + hp)),
                  pl.BlockSpec((2, CHUNK, CA_BAND), lambda hp, c: (hp, 0, 0)),
                  pl.BlockSpec((CA_ROWS, 128), lambda hp, c: (c, hp))],
        out_specs=(pl.BlockSpec((CA_ROWS, 128), lambda hp, c: (c, hp)), padded(), padded(),
                   pl.BlockSpec((2, CHUNK, CA_BAND), lambda hp, c: (hp, 0, 0))),
        out_shape=(sds(T, 512), sds(T + CA_PAD, 512), sds(T + CA_PAD, 512), sds(8, CHUNK, CA_BAND)),
        compiler_params=_cp(("parallel", "arbitrary")),
    )(proj, kvpad, kvpad, bias, dcat)


def _lru_pre(xs, cw, cb, wa, ba, wx, bx, lam):
    xc = cb + xs[0] * cw[0:1, :] + xs[1] * cw[1:2, :] + xs[2] * cw[2:3, :] + xs[3] * cw[3:4, :]
    ra = _sigmoid(bdot(xc, wa, "nn") + ba)
    ii = _sigmoid(bdot(xc, wx, "nn") + bx)
    la = 8.0 * ra * _log_sigmoid(lam)
    return jnp.exp(la), jnp.sqrt(-_expm1(2.0 * la)) * (ii * xc)


def _lru_pre_specs():
    full = lambda shape: pl.BlockSpec(shape, lambda i: (0,) * len(shape))
    return [pl.BlockSpec((4, ROWS, 512), lambda i: (0, i, 0)), full((4, 512)), full((1, 512)),
            full((512, 512)), full((1, 512)), full((512, 512)), full((1, 512)), full((1, 512))]


def _lru_pre_fwd(xs, cw, cb, wa, ba, wx, bx, lam):
    def body(xs_ref, cw_ref, cb_ref, wa_ref, ba_ref, wx_ref, bx_ref, lam_ref, a_ref, b_ref):
        a, b = _lru_pre(xs_ref[...], cw_ref[...], cb_ref[...], wa_ref[...], ba_ref[...], wx_ref[...], bx_ref[...],
                        lam_ref[...])
        a_ref[...] = a
        b_ref[...] = b

    row = pl.BlockSpec((ROWS, 512), lambda i: (i, 0))
    sds = jax.ShapeDtypeStruct((T, 512), F32)
    return pl.pallas_call(body, name="lru_pre_fwd", grid=(T // ROWS,), in_specs=_lru_pre_specs(),
                          out_specs=(row, row), out_shape=(sds, sds), compiler_params=_cp(("parallel",)),
                          )(xs, cw, cb, wa, ba, wx, bx, lam)


def _lru_pre_bwd(xs, cw, cb, wa, ba, wx, bx, lam, da, db):
    def body(xs_ref, cw_ref, cb_ref, wa_ref, ba_ref, wx_ref, bx_ref, lam_ref, da_ref, db_ref,
             dxs_ref, dcw_ref, dcb_ref, dwa_ref, dba_ref, dwx_ref, dbx_ref, dlam_ref):
        acc = (dcw_ref, dcb_ref, dwa_ref, dba_ref, dwx_ref, dbx_ref, dlam_ref)

        @pl.when(pl.program_id(0) == 0)
        def _():
            for r in acc:
                r[...] = jnp.zeros_like(r)

        _, vjp = jax.vjp(_lru_pre, xs_ref[...], cw_ref[...], cb_ref[...], wa_ref[...], ba_ref[...], wx_ref[...],
                         bx_ref[...], lam_ref[...])
        grads = vjp((da_ref[...], db_ref[...]))
        dxs_ref[...] = grads[0]
        for r, g in zip(acc, grads[1:]):
            r[...] += g

    row = pl.BlockSpec((ROWS, 512), lambda i: (i, 0))
    specs = _lru_pre_specs()
    sds = lambda *s: jax.ShapeDtypeStruct(s, F32)
    return pl.pallas_call(
        body, name="lru_pre_bwd", grid=(T // ROWS,), in_specs=specs + [row, row], out_specs=tuple(specs),
        out_shape=(sds(4, T, 512), sds(4, 512), sds(1, 512), sds(512, 512), sds(1, 512), sds(512, 512), sds(1, 512),
                   sds(1, 512)),
        compiler_params=_cp(("arbitrary",)),
    )(xs, cw, cb, wa, ba, wx, bx, lam, da, db)


def _lru_scan_fwd(a, b):
    def body(a_ref, b_ref, h_ref):
        def step(t, h):
            h = a_ref[pl.ds(t, 1), :] * h + b_ref[pl.ds(t, 1), :]
            h_ref[pl.ds(t, 1), :] = h
            return h

        lax.fori_loop(0, T, step, jnp.zeros((1, 512), F32), unroll=8)

    return pl.pallas_call(body, name="lru_scan_fwd", out_shape=jax.ShapeDtypeStruct((T, 512), F32),
                          compiler_params=pltpu.CompilerParams(vmem_limit_bytes=VMEM_LIMIT))(a, b)


def _lru_scan_bwd(a, h, dh):
    def body(a_ref, h_ref, dh_ref, da_ref, db_ref):
        def step(i, carry):
            t = T - 1 - i
            g = dh_ref[pl.ds(t, 1), :] + carry
            db_ref[pl.ds(t, 1), :] = g
            da_ref[pl.ds(t, 1), :] = g * h_ref[pl.ds(t - 1, 1), :]
            return a_ref[pl.ds(t, 1), :] * g

        carry = lax.fori_loop(0, T - 1, step, jnp.zeros((1, 512), F32), unroll=8)
        db_ref[pl.ds(0, 1), :] = dh_ref[pl.ds(0, 1), :] + carry
        da_ref[pl.ds(0, 1), :] = jnp.zeros((1, 512), F32)

    sds = jax.ShapeDtypeStruct((T, 512), F32)
    return pl.pallas_call(body, name="lru_scan_bwd", out_shape=(sds, sds),
                          compiler_params=pltpu.CompilerParams(vmem_limit_bytes=VMEM_LIMIT))(a, h, dh)


def _lru_post(h, gate):
    return h * _gelu_tanh(gate)


def _lru_post_fwd(h, proj, cat):
    def body(h_ref, g_ref, cat_ref, o_ref):
        o_ref[...] = _lru_post(h_ref[...], g_ref[...])

    row = pl.BlockSpec((ROWS, 512), lambda i: (i, 0))
    return pl.pallas_call(body, name="lru_post_fwd", grid=(T // ROWS,),
                          in_specs=[row, pl.BlockSpec((ROWS, 512), lambda i: (i, 3)), pl.BlockSpec(memory_space=pl.ANY)],
                          out_specs=pl.BlockSpec((ROWS, 512), lambda i: (i, 1)),
                          out_shape=jax.ShapeDtypeStruct((T, D), F32), input_output_aliases={2: 0},
                          compiler_params=_cp(("parallel",)))(h, proj, cat)


def _lru_post_bwd(h, proj, dcat):
    def body(h_ref, g_ref, do_ref, dh_ref, dg_ref):
        _, vjp = jax.vjp(_lru_post, h_ref[...], g_ref[...])
        dh, dg = vjp(do_ref[...])
        dh_ref[...] = dh
        dg_ref[...] = dg

    row = pl.BlockSpec((ROWS, 512), lambda i: (i, 0))
    sds = jax.ShapeDtypeStruct((T, 512), F32)
    return pl.pallas_call(body, name="lru_post_bwd", grid=(T // ROWS,),
                          in_specs=[row, pl.BlockSpec((ROWS, 512), lambda i: (i, 3)),
                                    pl.BlockSpec((ROWS, 512), lambda i: (i, 1))],
                          out_specs=(row, row), out_shape=(sds, sds), compiler_params=_cp(("parallel",)))(h, proj, dcat)


def _conv_dx(dxs_shift):
    def body(d_ref, o_ref):
        o_ref[...] = d_ref[0] + d_ref[1] + d_ref[2] + d_ref[3]

    row = pl.BlockSpec((ROWS, 512), lambda i: (i, 0))
    return pl.pallas_call(body, name="lru_conv_dx", grid=(T // ROWS,),
                          in_specs=[pl.BlockSpec((4, ROWS, 512), lambda i: (0, i, 0))], out_specs=row,
                          out_shape=jax.ShapeDtypeStruct((T, 512), F32), compiler_params=_cp(("parallel",)))(dxs_shift)


def _position():
    return lax.axis_index("x"), lax.axis_index("y"), lax.axis_index("c")


def _other_chips(x, y):
    return [(1 - x, y), (x, 1 - y), (1 - x, 1 - y)]


def _al(v, n):
    return v * n if isinstance(v, int) else pl.multiple_of(v * n, n)


_AG_ITEMS = [
    ((4, 32, 128), lambda o, s, h: o.at[s, pl.ds(_al(h, 16), 16), :], lambda r, h: r.at[pl.ds(_al(h, 16), 16), :]),
    ((4, 774, 1024), lambda o, s, h: o.at[s, :, pl.ds(_al(h, 512), 512)], lambda r, h: r.at[:, pl.ds(_al(h, 512), 512)]),
    ((1024, 1024), lambda o, s, h: o.at[pl.ds(_al(2 * s + h, 128), 128), :], lambda r, h: r.at[pl.ds(_al(h, 128), 128), :]),
    ((2, 1024, 4096), lambda o, s, h: o.at[h, :, pl.ds(_al(s, 1024), 1024)], lambda r, h: r.at[h]),
    ((2, 4096, 1024), lambda o, s, h: o.at[h, pl.ds(_al(s, 1024), 1024), :], lambda r, h: r.at[h]),
    ((1024, 2560), lambda o, s, h: o.at[pl.ds(_al(h, 512), 512), pl.ds(_al(s, 640), 640)],
     lambda r, h: r.at[pl.ds(_al(h, 512), 512), :]),
    ((1024, 1024), lambda o, s, h: o.at[pl.ds(_al(2 * s + h, 128), 128), :], lambda r, h: r.at[pl.ds(_al(h, 128), 128), :]),
]


_AG_GROUPS = [(0, 1, 2), (3, 4), (5, 6)]

_HBM = pl.BlockSpec(memory_space=pltpu.HBM)
_SEM = pl.BlockSpec(memory_space=pltpu.SEMAPHORE)
_SPLIT = dict(has_side_effects=pltpu.SideEffectType.DATAFLOW_SIDE_EFFECTING)


def _hbm(a):
    return pltpu.with_memory_space_constraint(a, pltpu.HBM)


def _ag_ici_copy(i, j, chip, c, slot, src_ref, land_ref, send_sems, recv_sems, k):
    _, dst, half = _AG_ITEMS[i]
    return pltpu.make_async_remote_copy(src_ref=half(src_ref, c), dst_ref=dst(land_ref, slot, c), send_sem=send_sems.at[k],
                                        recv_sem=recv_sems.at[k], device_id=(*chip, c), device_id_type=MESH)


def _ag_start(shards):
    n = len(_AG_ITEMS)
    ng = len(_AG_GROUPS)
    lands = [lax.empty(shape, s.dtype) for (shape, _, _), s in zip(_AG_ITEMS, shards)]

    def body(*refs):
        srcs, land_refs = refs[:n], refs[n:2 * n]
        sems = refs[2 * n:2 * n + 2 * ng]
        token = refs[-1]
        x, y, c = _position()
        me = 2 * x + y
        for g, items in enumerate(_AG_GROUPS):
            for t, i in enumerate(items):
                for j, chip in enumerate(_other_chips(x, y)):
                    _ag_ici_copy(i, j, chip, c, me, srcs[i], land_refs[i], sems[2 * g], sems[2 * g + 1], 3 * t + j).start()
        token[...] = jnp.zeros_like(token)

    sem_shapes = []
    for items in _AG_GROUPS:
        sem_shapes += [pltpu.SemaphoreType.DMA((3 * len(items),))] * 2
    thru = [pltpu.HBM(a.shape, a.dtype) for a in list(shards) + lands]
    out = pl.pallas_call(
        body, name="allgather_start",
        out_shape=tuple(sem_shapes) + tuple(thru) + (jax.ShapeDtypeStruct((8, 128), F32),),
        in_specs=(_HBM,) * (2 * n),
        out_specs=(_SEM,) * (2 * ng) + (_HBM,) * (2 * n) + (pl.BlockSpec(memory_space=pltpu.VMEM),),
        input_output_aliases={i: 2 * ng + i for i in range(2 * n)},
        compiler_params=pltpu.CompilerParams(**_SPLIT),
    )(*[_hbm(a) for a in list(shards) + lands])
    sems, thru, token = out[:2 * ng], out[2 * ng:-1], out[-1]
    return [(sems[2 * g], sems[2 * g + 1]) for g in range(ng)], list(thru[:n]), list(thru[n:]), token


def _ag_wait(g, sems, srcs, lands, after):
    items = _AG_GROUPS[g]
    m = len(items)

    def body(*refs):
        src_refs, land_refs = refs[:m], refs[m:2 * m]
        send_sems, recv_sems = refs[2 * m], refs[2 * m + 1]
        x, y, c = _position()
        for t, i in enumerate(items):
            for j, chip in enumerate(_other_chips(x, y)):
                cp = _ag_ici_copy(i, j, chip, c, 2 * chip[0] + chip[1], src_refs[t], land_refs[t], send_sems, recv_sems,
                                  3 * t + j)
                cp.wait_send()
                cp.wait_recv()

    ops = [srcs[i] for i in items] + [lands[i] for i in items]
    out = pl.pallas_call(
        body, name=f"allgather_wait_{g}",
        out_shape=tuple(pltpu.HBM(a.shape, a.dtype) for a in ops),
        in_specs=(_HBM,) * (2 * m) + (_SEM, _SEM, pl.BlockSpec(memory_space=pl.ANY)),
        out_specs=(_HBM,) * (2 * m),
        input_output_aliases={i: i for i in range(2 * m)},
        compiler_params=pltpu.CompilerParams(**_SPLIT),
    )(*ops, sems[0], sems[1], after)
    return list(out[:m]), list(out[m:])


def _ag_forward(g, srcs, lands):
    items = _AG_GROUPS[g]
    m = len(items)

    def body(*refs):
        src_refs, in_refs, out_refs = refs[:m], refs[m:2 * m], refs[2 * m:3 * m]
        send_sems, recv_sems = refs[3 * m:]
        x, y, c = _position()
        sibling = (x, y, 1 - c)
        me = 2 * x + y
        chips = _other_chips(x, y)
        sends = []
        for t, i in enumerate(items):
            _, dst, half = _AG_ITEMS[i]
            for j, chip in enumerate(chips):
                slot = 2 * chip[0] + chip[1]
                sends.append(pltpu.make_async_remote_copy(
                    src_ref=dst(in_refs[t], slot, c), dst_ref=dst(out_refs[t], slot, c), send_sem=send_sems.at[5 * t + j],
                    recv_sem=recv_sems.at[5 * t + j], device_id=sibling, device_id_type=MESH))
            for hc in range(2):
                sends.append(pltpu.make_async_remote_copy(
                    src_ref=half(src_refs[t], hc), dst_ref=dst(out_refs[t], me, hc), send_sem=send_sems.at[5 * t + 3 + hc],
                    recv_sem=recv_sems.at[5 * t + 3 + hc], device_id=sibling, device_id_type=MESH))
        for cp in sends:
            cp.start()
        for t, i in enumerate(items):
            _, dst, half = _AG_ITEMS[i]
            for j, chip in enumerate(chips):
                there = dst(out_refs[t], 2 * chip[0] + chip[1], 1 - c)
                pltpu.make_async_remote_copy(src_ref=there, dst_ref=there, send_sem=send_sems.at[5 * t + j],
                                             recv_sem=recv_sems.at[5 * t + j], device_id=sibling,
                                             device_id_type=MESH).wait_recv()
            for hc in range(2):
                there = dst(out_refs[t], me, hc)
                pltpu.make_async_remote_copy(src_ref=there, dst_ref=there, send_sem=send_sems.at[5 * t + 3 + hc],
                                             recv_sem=recv_sems.at[5 * t + 3 + hc], device_id=sibling,
                                             device_id_type=MESH).wait_recv()
        for cp in sends:
            cp.wait_send()

    any_spec = pl.BlockSpec(memory_space=pl.ANY)
    return pl.pallas_call(
        body, name=f"allgather_forward_{g}",
        in_specs=[any_spec] * (2 * m), out_specs=(any_spec,) * m,
        out_shape=tuple(jax.ShapeDtypeStruct(a.shape, a.dtype) for a in lands),
        input_output_aliases={m + t: t for t in range(m)},
        scratch_shapes=[pltpu.SemaphoreType.DMA((5 * m,)), pltpu.SemaphoreType.DMA((5 * m,))],
    )(*srcs, *lands)


def _pair_swap_copy(g_ref, r_ref, send_sem, recv_sem):
    x, y, c = _position()
    hc = g_ref.shape[2] // 2
    return pltpu.make_async_remote_copy(src_ref=g_ref.at[:, :, pl.ds(_al(1 - c, hc), hc)], dst_ref=r_ref,
                                        send_sem=send_sem, recv_sem=recv_sem, device_id=(x, y, 1 - c),
                                        device_id_type=MESH)


def _pair_swap_start(gb, tag):
    _, rows, cols = gb.shape
    recv = lax.empty((4, rows, cols // 2), gb.dtype)

    def body(g_ref, r_ref, send_sem, recv_sem, g_thru, r_thru, token):
        _pair_swap_copy(g_ref, r_ref, send_sem, recv_sem).start()
        token[...] = jnp.zeros_like(token)

    return pl.pallas_call(
        body, name="grad_pair_swap_start_" + tag,
        out_shape=(pltpu.SemaphoreType.DMA(()), pltpu.SemaphoreType.DMA(()), pltpu.HBM(gb.shape, gb.dtype),
                   pltpu.HBM(recv.shape, recv.dtype), jax.ShapeDtypeStruct((8, 128), F32)),
        in_specs=(_HBM, _HBM), out_specs=(_SEM, _SEM, _HBM, _HBM, pl.BlockSpec(memory_space=pltpu.VMEM)),
        input_output_aliases={0: 2, 1: 3},
        compiler_params=pltpu.CompilerParams(**_SPLIT),
    )(_hbm(gb), _hbm(recv))


def _pair_swap_wait(started, after, tag):
    send_sem, recv_sem, gb, recv, _ = started

    def body(g_ref, r_ref, send_sem, recv_sem, after_ref, g_out, r_out):
        cp = _pair_swap_copy(g_ref, r_ref, send_sem, recv_sem)
        cp.wait_send()
        cp.wait_recv()

    return pl.pallas_call(
        body, name="grad_pair_swap_wait_" + tag,
        out_shape=(pltpu.HBM(gb.shape, gb.dtype), pltpu.HBM(recv.shape, recv.dtype)),
        in_specs=(_HBM, _HBM, _SEM, _SEM, pl.BlockSpec(memory_space=pl.ANY)), out_specs=(_HBM, _HBM),
        input_output_aliases={0: 0, 1: 1},
        compiler_params=pltpu.CompilerParams(**_SPLIT),
    )(gb, recv, send_sem, recv_sem, after)


def _handover(red, tag):
    hc = red.shape[1] // 2

    def body(in_ref, out_ref, send_sem, recv_sem):
        x, y, c = _position()
        mine = pl.ds(_al(c, hc), hc)
        cp = pltpu.make_async_remote_copy(src_ref=in_ref.at[:, mine], dst_ref=out_ref.at[:, mine], send_sem=send_sem,
                                          recv_sem=recv_sem, device_id=(x, y, 1 - c), device_id_type=MESH)
        cp.start()
        theirs = out_ref.at[:, pl.ds(_al(1 - c, hc), hc)]
        pltpu.make_async_remote_copy(src_ref=theirs, dst_ref=theirs, send_sem=send_sem, recv_sem=recv_sem,
                                     device_id=(x, y, c), device_id_type=MESH).wait_recv()
        cp.wait_send()

    return pl.pallas_call(
        body, name="grad_handover_" + tag,
        in_specs=[pl.BlockSpec(memory_space=pl.ANY)], out_specs=pl.BlockSpec(memory_space=pl.ANY),
        out_shape=jax.ShapeDtypeStruct(red.shape, red.dtype), input_output_aliases={0: 0},
        scratch_shapes=[pltpu.SemaphoreType.DMA, pltpu.SemaphoreType.DMA],
    )(red)


def _a2a_copy(j, chip, c, p_ref, q_ref, q_slot, send_sems, recv_sems):
    return pltpu.make_async_remote_copy(src_ref=p_ref.at[2 * chip[0] + chip[1]], dst_ref=q_ref.at[q_slot],
                                        send_sem=send_sems.at[j], recv_sem=recv_sems.at[j], device_id=(*chip, c),
                                        device_id_type=MESH)


def _a2a_start(p, tag):
    def body(p_ref, q_ref, send_sems, recv_sems, p_thru, q_thru, token):
        x, y, c = _position()
        for j, chip in enumerate(_other_chips(x, y)):
            _a2a_copy(j, chip, c, p_ref, q_ref, 2 * x + y, send_sems, recv_sems).start()
        token[...] = jnp.zeros_like(token)

    return pl.pallas_call(
        body, name="grad_alltoall_start_" + tag,
        out_shape=(pltpu.SemaphoreType.DMA((3,)), pltpu.SemaphoreType.DMA((3,)), pltpu.HBM(p.shape, p.dtype),
                   pltpu.HBM(p.shape, p.dtype), jax.ShapeDtypeStruct((8, 128), F32)),
        in_specs=(_HBM, _HBM), out_specs=(_SEM, _SEM, _HBM, _HBM, pl.BlockSpec(memory_space=pltpu.VMEM)),
        input_output_aliases={0: 2, 1: 3},
        compiler_params=pltpu.CompilerParams(**_SPLIT),
    )(_hbm(p), _hbm(lax.empty(p.shape, p.dtype)))


def _a2a_wait(send_sems, recv_sems, p, q, after, tag):
    def body(p_ref, q_ref, send_sems, recv_sems, after_ref, p_out, q_out):
        x, y, c = _position()
        for j, chip in enumerate(_other_chips(x, y)):
            cp = _a2a_copy(j, chip, c, p_ref, q_ref, 2 * chip[0] + chip[1], send_sems, recv_sems)
            cp.wait_send()
            cp.wait_recv()

    return pl.pallas_call(
        body, name="grad_alltoall_wait_" + tag,
        out_shape=(pltpu.HBM(p.shape, p.dtype), pltpu.HBM(q.shape, q.dtype)),
        in_specs=(_HBM, _HBM, _SEM, _SEM, pl.BlockSpec(memory_space=pl.ANY)), out_specs=(_HBM, _HBM),
        input_output_aliases={0: 0, 1: 1},
        compiler_params=pltpu.CompilerParams(**_SPLIT),
    )(p, q, send_sems, recv_sems, after)


def _comm_rows(rows):
    return next(t for t in (512, 384, 256, 128) if rows % t == 0)


def _pair_add(gb, recv, where, tag):
    _, rows, cols = gb.shape
    hc = cols // 2
    tr = _comm_rows(rows)

    def body(w_ref, g_ref, r_ref, o_ref):
        o_ref[...] = (g_ref[...].astype(F32) + r_ref[...].astype(F32)).astype(o_ref.dtype)

    return pl.pallas_call(
        body, name="grad_pair_add_" + tag,
        grid_spec=pltpu.PrefetchScalarGridSpec(
            num_scalar_prefetch=1, grid=(4, rows // tr),
            in_specs=[pl.BlockSpec((None, tr, hc), lambda s, j, w_ref: (s, j, w_ref[0])),
                      pl.BlockSpec((None, tr, hc), lambda s, j, w_ref: (s, j, 0))],
            out_specs=pl.BlockSpec((None, tr, hc), lambda s, j, w_ref: (s, j, 0))),
        out_shape=jax.ShapeDtypeStruct((4, rows, hc), gb.dtype),
        compiler_params=_cp(("parallel", "parallel")),
    )(where, gb, recv)


def _sum_chips(p, q, where, tag):
    _, rows, hc = q.shape
    tr = _comm_rows(rows)

    def body(w_ref, p_ref, qa_ref, qb_ref, qc_ref, o_ref):
        me = w_ref[1]
        own, qa, qb, qc = (r[...].astype(F32) for r in (p_ref, qa_ref, qb_ref, qc_ref))
        v0 = jnp.where(me == 0, own, qa)
        v1 = jnp.where(me == 1, own, jnp.where(me == 0, qa, qb))
        v2 = jnp.where(me == 2, own, jnp.where(me < 2, qb, qc))
        v3 = jnp.where(me == 3, own, qc)
        o_ref[...] = ((v0 + v1) + v2) + v3

    slot = lambda k: pl.BlockSpec((None, tr, hc), lambda j, w_ref: (w_ref[k], j, 0))
    return pl.pallas_call(
        body, name="grad_sum_chips_" + tag,
        grid_spec=pltpu.PrefetchScalarGridSpec(
            num_scalar_prefetch=1, grid=(rows // tr,),
            in_specs=[slot(1), slot(2), slot(3), slot(4)],
            out_specs=pl.BlockSpec((tr, hc), lambda j, w_ref: (j, w_ref[0]))),
        out_shape=jax.ShapeDtypeStruct((rows, 2 * hc), F32),
        compiler_params=_cp(("parallel",)),
    )(where, p, q, q, q)


def _shard_major(g, axis):
    shape = g.shape
    g = g.reshape(shape[:axis] + (4, shape[axis] // 4) + shape[axis + 1:])
    return jnp.moveaxis(g, axis, 0).reshape(4, -1)


def _unshard(g4, shape, axis):
    n = shape[axis] // 4
    g = g4.reshape((4,) + shape[:axis] + (n,) + shape[axis + 1:])
    return jnp.moveaxis(g, 0, axis).reshape(shape)


def _split(flat, shapes):
    out, off = [], 0
    for shp in shapes:
        n = 1
        for d in shp:
            n *= d
        out.append(flat[..., off:off + n].reshape(flat.shape[:-1] + tuple(shp)))
        off += n
    return out


def _even_rows_to_kernel(wt):
    return jnp.concatenate([wt[:1536], wt[1552:3088], wt[1536:1552], wt[3088:3096],
                            jnp.zeros((PE - 3096, wt.shape[1]), wt.dtype)], axis=0)


def _block_diag(w):
    eye = jnp.eye(8, dtype=w.dtype)
    return (w[:, :, None, :] * eye[:, None, :, None]).reshape(512, 512)


def _diag_blocks(g):
    eye = jnp.eye(8, dtype=g.dtype)
    return (g.reshape(8, 64, 8, 64) * eye[:, None, :, None]).sum(axis=2)


def _shift_down(a, s):
    return a if s == 0 else jnp.pad(a, ((s, 0), (0, 0)))[:a.shape[0]]


def _shift_up(a, s):
    return a if s == 0 else jnp.pad(a, ((0, s), (0, 0)))[s:]


SMALL_SHARDED_SHAPES = [(2, 4, 256), (16, 64), (4, 128), (128,), (128,), (128,), (128,)]
REPL_SHAPES = [(256,), (512,), (8,), (8, 257), (8, 64, 64), (8, 64, 64)]


def kernel(x, norm_w, w_in_even, gla_w_a_up, gla_b_a, gla_norm_w, fox_b_f, w_out_even, w_in_odd, rel_bias, conv_w, conv_b, lru_w_a, lru_b_a, lru_w_x, lru_b_x, lru_lambda, w_out_odd, w_mlp_up, w_mlp_down, loss_target, m_norm_w, m_w_in_even, m_gla_w_a_up, m_gla_b_a, m_gla_norm_w, m_fox_b_f, m_w_out_even, m_w_in_odd, m_rel_bias, m_conv_w, m_conv_b, m_lru_w_a, m_lru_b_a, m_lru_w_x, m_lru_b_x, m_lru_lambda, m_w_out_odd, m_w_mlp_up, m_w_mlp_down, v_norm_w, v_w_in_even, v_gla_w_a_up, v_gla_b_a, v_gla_norm_w, v_fox_b_f, v_w_out_even, v_w_in_odd, v_rel_bias, v_conv_w, v_conv_b, v_lru_w_a, v_lru_b_a, v_lru_w_x, v_lru_b_x, v_lru_lambda, v_w_out_odd, v_w_mlp_up, v_w_mlp_down):
    c_idx = lax.axis_index("c")

    small_local = [norm_w, gla_w_a_up[0], conv_w[0], conv_b[0], lru_b_a[0], lru_b_x[0], lru_lambda[0]]
    small_src = jnp.concatenate([a.reshape(-1) for a in small_local]).reshape(32, 128)
    mine = [small_src, w_in_even[0].T.astype(BF16), w_out_even[0].astype(BF16), w_mlp_up.astype(BF16),
            w_mlp_down.astype(BF16), w_in_odd[0].astype(BF16), w_out_odd[0].astype(BF16)]
    ag_sems, ag_srcs, ag_lands, ag_token = _ag_start(mine)

    def gathered(g, after):
        srcs_g, lands_g = _ag_wait(g, ag_sems[g], ag_srcs, ag_lands, after)
        return _ag_forward(g, srcs_g, lands_g)

    small4, w_in_e4, w_out_e = gathered(0, ag_token)
    me = 2 * lax.axis_index("x") + lax.axis_index("y")
    others = [k + (k >= me).astype(jnp.int32) for k in range(3)]
    where = jnp.stack([c_idx, me] + others).astype(jnp.int32)

    w_in_e_t = _even_rows_to_kernel(w_in_e4.reshape(3096, D))
    g_small = _split(small4.reshape(4, 32 * 128), SMALL_SHARDED_SHAPES)
    nw_full = _unshard(g_small[0], (2, 4, 1024), 2)
    wa_up = _unshard(g_small[1], (16, 256), 1)
    cw = _unshard(g_small[2], (4, 512), 1)
    cb, lba, lbx, lam = [_unshard(g, (512,), 0).reshape(1, 512) for g in g_small[3:]]
    nw = lambda layer, i: nw_full[layer, i].reshape(1, D)

    wa_pad = jnp.pad(wa_up, ((0, 128 - 16), (0, 0)))
    gla_ba = gla_b_a.reshape(1, 256)
    gla_nw = gla_norm_w.reshape(1, 512)
    fox_bpad = jnp.pad(fox_b_f.reshape(1, 8), ((0, 0), (FOX_LANE0, 128 - FOX_LANE0 - 8)))
    rbp = jnp.pad(rel_bias[0], ((0, 0), (0, REL_PAD - 257)))
    wa_bd = _block_diag(lru_w_a[0])
    wx_bd = _block_diag(lru_w_x[0])

    x0 = x[0]
    tgt = loss_target[0]

    h0 = _prenorm(x0, nw(0, 0), "prenorm_l0_mix")
    proj_e = _mm(h0, w_in_e_t, "nt", tm=1024, tn=640, name="mm_in_even")
    cat0, s_prev = _gla_fwd(proj_e, wa_pad, gla_ba, gla_nw)
    cum_r = _fox_gate_fwd(proj_e, fox_bpad)
    cum_c = cum_r[:, FOX_LANE0:FOX_LANE0 + 8].T
    cat0 = _fox_fwd(proj_e, cum_c, cat0)
    mix0 = _mm(cat0, w_out_e, "nn", tm=1024, tn=512, name="mm_out_even")
    x1 = _postnorm(x0, mix0, nw(0, 1), "postnorm_l0_mix")
    w_up, w_dn = gathered(1, x1)
    h1 = _prenorm(x1, nw(0, 2), "prenorm_l0_mlp")
    a0, r0 = _mm(h1, w_up, "nn", tm=1024, tn=1024, b_layer=0, relu_pair=True, name="mm_up_l0")
    d0 = _mm(a0, w_dn, "nn", tm=1024, tn=512, b_layer=0, name="mm_down_l0")
    x2 = _postnorm(x1, d0, nw(0, 3), "postnorm_l0_mlp")

    w_in_o, w_out_o = gathered(2, x2)
    h2 = _prenorm(x2, nw(1, 0), "prenorm_l1_mix")
    proj_o = _mm(h2, w_in_o, "nn", tm=1024, tn=640, name="mm_in_odd")
    bias_q = _bias_build(rbp)
    bias = bias_q.transpose(1, 0, 2)
    kvpad = jnp.pad(proj_o[:, 512:1536], ((CA_PAD, 0), (0, 0)))
    cat1 = _ca_fwd(proj_o, kvpad, bias)
    x_in = proj_o[:, 2048:2560]
    xs = jnp.stack([_shift_down(x_in, 3 - j) for j in range(4)])
    lru_a, lru_b = _lru_pre_fwd(xs, cw, cb, wa_bd, lba, wx_bd, lbx, lam)
    hh = _lru_scan_fwd(lru_a, lru_b)
    cat1 = _lru_post_fwd(hh, proj_o, cat1)
    mix1 = _mm(cat1, w_out_o, "nn", tm=1024, tn=512, name="mm_out_odd")
    x3 = _postnorm(x2, mix1, nw(1, 1), "postnorm_l1_mix")
    h3 = _prenorm(x3, nw(1, 2), "prenorm_l1_mlp")
    a1, r1 = _mm(h3, w_up, "nn", tm=1024, tn=1024, b_layer=1, relu_pair=True, name="mm_up_l1")
    d1 = _mm(a1, w_dn, "nn", tm=1024, tn=512, b_layer=1, name="mm_down_l1")
    x4 = _postnorm(x3, d1, nw(1, 3), "postnorm_l1_mlp")

    g4, loss_part = _loss_and_grad(x4, tgt)
    loss = lax.psum(loss_part[0, 0], ("x", "y", "c"))

    def rs_begin(swap, after, tag):
        gb, recv = _pair_swap_wait(swap, after, tag)
        return _a2a_start(_pair_add(gb, recv, where, tag), tag)

    def rs_end(started, after, tag):
        send_sems, recv_sems, p, q, _ = started
        p, q = _a2a_wait(send_sems, recv_sems, p, q, after, tag)
        return _handover(_sum_chips(p, q, where, tag), tag)

    gba = lax.dynamic_update_slice(lax.empty((4, GA_ROWS, D), BF16), jnp.zeros((4, GA_ROWS - GA_USED, D), BF16),
                                   (0, GA_USED, 0))
    dd1, dnw13 = _norm_bwd(d1, nw(1, 3), g4, None, "postnorm_l1_mlp_bwd")
    gba = _mm(a1, dd1, "tn", tm=512, tn=1024, into=(gba, 1024, GA_DN), name="mm_down_l1_dw")
    du1 = _mm(dd1, w_dn, "nt", tm=1024, tn=1024, b_layer=1, times2=r1, out_dtype=BF16, name="mm_down_l1_dx")
    gba = _mm(du1, h3, "tn", tm=512, tn=1024, into=(gba, 1024, GA_UP), name="mm_up_l1_dw")
    dh3 = _mm(du1, w_up, "nt", tm=1024, tn=512, b_layer=1, name="mm_up_l1_dx")
    g3, dnw12 = _norm_bwd(x3, nw(1, 2), dh3, g4, "prenorm_l1_mlp_bwd")
    dmix1, dnw11 = _norm_bwd(mix1, nw(1, 1), g3, None, "postnorm_l1_mix_bwd")
    gba = _mm(cat1, dmix1, "tn", tm=128, tn=1024, into=(gba, 256, GA_OUT_O), name="mm_out_odd_dw")
    dcat1 = _mm(dmix1, w_out_o, "nt", tm=1024, tn=512, name="mm_out_odd_dx")

    dq_c, dkpad, dvpad, dbias = _ca_bwd(proj_o, kvpad, bias, dcat1)
    g_rel = _bias_grad(jnp.pad(dbias.transpose(1, 0, 2), ((0, 0), (0, 0), (0, BIAS_W - CA_BAND))))[:, :257]
    dhh, dgate = _lru_post_bwd(hh, proj_o, dcat1)
    da_l, db_l = _lru_scan_bwd(lru_a, hh, dhh)
    dxs, g_cw, g_cb, g_wa_bd, g_lba, g_wx_bd, g_lbx, g_lam = _lru_pre_bwd(xs, cw, cb, wa_bd, lba, wx_bd, lbx, lam, da_l, db_l)
    dx_in = _conv_dx(jnp.stack([_shift_up(dxs[j], 3 - j) for j in range(4)]))
    dproj_o = jnp.concatenate([dq_c, dkpad[CA_PAD:], dvpad[CA_PAD:], dgate, dx_in], axis=1).astype(BF16)
    gba = _mm(dproj_o, h2, "tn", tm=128, tn=1024, into=(gba, 640, GA_IN_O), name="mm_in_odd_dw")
    swap_a = _pair_swap_start(gba, "a")
    dh2 = _mm(dproj_o, w_in_o, "nt", tm=1024, tn=512, name="mm_in_odd_dx")
    g2, dnw10 = _norm_bwd(x2, nw(1, 0) + swap_a[4][0, 0], dh2, g3, "prenorm_l1_mix_bwd")
    rs_a = rs_begin(swap_a, g2, "a")

    gbb = lax.empty((4, GB_ROWS, D), BF16)
    dd0, dnw03 = _norm_bwd(d0, nw(0, 3) + rs_a[4][0, 0], g2, None, "postnorm_l0_mlp_bwd")
    gbb = _mm(a0, dd0, "tn", tm=512, tn=1024, into=(gbb, 1024, GB_DN), name="mm_down_l0_dw")
    du0 = _mm(dd0, w_dn, "nt", tm=1024, tn=1024, b_layer=0, times2=r0, out_dtype=BF16, name="mm_down_l0_dx")
    gbb = _mm(du0, h1, "tn", tm=512, tn=1024, into=(gbb, 1024, GB_UP), name="mm_up_l0_dw")
    swap_b = _pair_swap_start(gbb, "b")
    dh1 = _mm(du0, w_up, "nt", tm=1024, tn=512, b_layer=0, name="mm_up_l0_dx")
    g1, dnw02 = _norm_bwd(x1, nw(0, 2) + swap_b[4][0, 0], dh1, g2, "prenorm_l0_mlp_bwd")
    rs_b = rs_begin(swap_b, g1, "b")
    dmix0, dnw01 = _norm_bwd(mix0, nw(0, 1) + rs_b[4][0, 0], g1, None, "postnorm_l0_mix_bwd")
    gbc = lax.empty((4, GC_ROWS, D), BF16)
    gbc = _mm(cat0, dmix0, "tn", tm=128, tn=1024, into=(gbc, 256, GC_OUT_E), name="mm_out_even_dw")
    dcat0 = _mm(dmix0, w_out_e, "nt", tm=1024, tn=512, name="mm_out_even_dx")

    dq_g, dk_g, dv_g, dr_g, daux_g, g_wa_pad, g_gla_ba, g_gla_nw = _gla_bwd(proj_e, s_prev, wa_pad, gla_ba, gla_nw, dcat0)
    dq_f, dk_f, dv_f, dccol = _fox_bwd(proj_e, cum_c, dcat0)
    dccol_t = jnp.pad(dccol.sum(axis=0).T, ((0, 0), (FOX_LANE0, 128 - FOX_LANE0 - 8)))
    daux, g_fox_bpad = _fox_gate_bwd(proj_e, fox_bpad, dccol_t, daux_g)
    dproj_e = jnp.concatenate([dq_g, dk_g, dv_g, dr_g, dq_f, dk_f, dv_f, daux], axis=1).astype(BF16)
    gt_in_e = _mm(dproj_e, h0, "tn", tm=640, tn=1024, out_dtype=BF16, name="mm_in_even_dw")
    dh0 = _mm(dproj_e, w_in_e_t, "nn", tm=1024, tn=512, name="mm_in_even_dx")
    grad_x, dnw00 = _norm_bwd(x0, nw(0, 0), dh0, g1, "prenorm_l0_mix_bwd")

    g_norm = jnp.stack([jnp.concatenate([dnw00, dnw01, dnw02, dnw03]), jnp.concatenate([dnw10, dnw11, dnw12, dnw13])])
    sharded = [(g_norm, 2), (g_wa_pad[:16], 1), (g_cw, 1), (g_cb[0], 0), (g_lba[0], 0), (g_lbx[0], 0), (g_lam[0], 0)]
    replicated = [g_gla_ba[0], g_gla_nw[0], g_fox_bpad[0, FOX_LANE0:FOX_LANE0 + 8], g_rel, _diag_blocks(g_wa_bd),
                  _diag_blocks(g_wx_bd)]
    small4 = jnp.concatenate([_shard_major(g, ax) for g, ax in sharded]
                             + [jnp.broadcast_to(g.reshape(1, -1), (4, g.size)) for g in replicated], axis=1)
    n_small = small4.shape[1]
    small_rows = GC_ROWS - GC_TAIL - 774
    small4 = jnp.pad(small4, ((0, 0), (0, small_rows * D - n_small))).reshape(4, small_rows, D)
    gt_rows = jnp.concatenate([gt_in_e[:1536], gt_in_e[3072:3088], gt_in_e[1536:3072], gt_in_e[3088:3096]], axis=0)
    tail = jnp.concatenate([gt_rows.reshape(4, 774, D), small4.astype(BF16)], axis=1)
    gbc = lax.dynamic_update_slice(gbc, tail, (0, GC_TAIL, 0))
    swap_c = _pair_swap_start(gbc, "c")
    rs_c = rs_begin(swap_c, swap_c[4], "c")

    red_a = rs_end(rs_a, rs_c[4], "a")
    red_b = rs_end(rs_b, red_a, "b")
    red_c = rs_end(rs_c, red_b, "c")

    g_up = jnp.stack([red_b[GB_UP:GB_UP + 1024].T, red_a[GA_UP:GA_UP + 1024].T])
    g_dn = jnp.stack([red_b[GB_DN:GB_DN + 1024], red_a[GA_DN:GA_DN + 1024]])
    g_small = _split(red_c[GC_TAIL + 774:].reshape(-1)[:n_small], SMALL_SHARDED_SHAPES + REPL_SHAPES)
    g_of = dict(zip(["norm_w", "gla_w_a_up", "conv_w", "conv_b", "lru_b_a", "lru_b_x", "lru_lambda", "gla_b_a",
                     "gla_norm_w", "fox_b_f", "rel_bias", "lru_w_a", "lru_w_x"], g_small))
    g_of.update(w_mlp_up=g_up, w_mlp_down=g_dn, w_in_odd=red_a[GA_IN_O:GA_IN_O + 640].T,
                w_out_even=red_c[GC_OUT_E:GC_OUT_E + 256], w_out_odd=red_a[GA_OUT_O:GA_OUT_O + 256],
                w_in_even=red_c[GC_TAIL:GC_TAIL + 774])

    names = ["norm_w", "w_in_even", "gla_w_a_up", "gla_b_a", "gla_norm_w", "fox_b_f", "w_out_even", "w_in_odd", "rel_bias",
             "conv_w", "conv_b", "lru_w_a", "lru_b_a", "lru_w_x", "lru_b_x", "lru_lambda", "w_out_odd", "w_mlp_up",
             "w_mlp_down"]
    w_of = dict(norm_w=norm_w, w_in_even=w_in_even, gla_w_a_up=gla_w_a_up, gla_b_a=gla_b_a, gla_norm_w=gla_norm_w,
                fox_b_f=fox_b_f, w_out_even=w_out_even, w_in_odd=w_in_odd, rel_bias=rel_bias, conv_w=conv_w, conv_b=conv_b,
                lru_w_a=lru_w_a, lru_b_a=lru_b_a, lru_w_x=lru_w_x, lru_b_x=lru_b_x, lru_lambda=lru_lambda,
                w_out_odd=w_out_odd, w_mlp_up=w_mlp_up, w_mlp_down=w_mlp_down)
    m_of = dict(norm_w=m_norm_w, w_in_even=m_w_in_even, gla_w_a_up=m_gla_w_a_up, gla_b_a=m_gla_b_a,
                gla_norm_w=m_gla_norm_w, fox_b_f=m_fox_b_f, w_out_even=m_w_out_even, w_in_odd=m_w_in_odd,
                rel_bias=m_rel_bias, conv_w=m_conv_w, conv_b=m_conv_b, lru_w_a=m_lru_w_a, lru_b_a=m_lru_b_a,
                lru_w_x=m_lru_w_x, lru_b_x=m_lru_b_x, lru_lambda=m_lru_lambda, w_out_odd=m_w_out_odd,
                w_mlp_up=m_w_mlp_up, w_mlp_down=m_w_mlp_down)
    v_of = dict(norm_w=v_norm_w, w_in_even=v_w_in_even, gla_w_a_up=v_gla_w_a_up, gla_b_a=v_gla_b_a,
                gla_norm_w=v_gla_norm_w, fox_b_f=v_fox_b_f, w_out_even=v_w_out_even, w_in_odd=v_w_in_odd,
                rel_bias=v_rel_bias, conv_w=v_conv_w, conv_b=v_conv_b, lru_w_a=v_lru_w_a, lru_b_a=v_lru_b_a,
                lru_w_x=v_lru_w_x, lru_b_x=v_lru_b_x, lru_lambda=v_lru_lambda, w_out_odd=v_w_out_odd,
                w_mlp_up=v_w_mlp_up, w_mlp_down=v_w_mlp_down)
    grads, deltas, new_ms, new_vs = [], [], [], []
    for n in names:
        w = w_of[n]
        if n == "w_in_even":
            to_view = lambda a: a[0].T
            from_view = lambda a: a.T[None]
        else:
            view = w.shape if w.ndim <= 3 else w.shape[-3:]
            to_view = lambda a, view=view: a.reshape(view)
            from_view = lambda a, w=w: a.reshape(w.shape)
        g = g_of[n] if n == "w_in_even" else to_view(g_of[n])
        d, mn, vn = _adamw(to_view(w), g, to_view(m_of[n]), to_view(v_of[n]), "adamw_" + n)
        grads.append(from_view(g))
        deltas.append(from_view(d))
        new_ms.append(from_view(mn))
        new_vs.append(from_view(vn))

    return (loss, grad_x.reshape(1, T, D), *grads, *deltas, *new_ms, *new_vs)
```

```python
import functools

import jax
import jax.numpy as jnp
from jax import lax
from jax.experimental import pallas as pl
from jax.experimental.pallas import tpu as pltpu

F32 = jnp.float32
BF16 = jnp.bfloat16
MESH = pl.DeviceIdType.MESH

T = 2048
D = 1024
DFF = 4096
EPS = 1e-6
CHUNK = 64
NCHUNK = T // CHUNK
PE = 3200
PO = 2560
AUX_BLK = 3072 // 128
FOX_LANE0 = 16
GLA_SCALE = 64 ** -0.5
ATT_SCALE = 64 ** -0.5
NEG = float(jnp.finfo(jnp.float32).min)
CA_BAND = 576
CA_PAD = 512
REL_PAD = 384

VMEM_LIMIT = 48 * 1024 * 1024

ADAM_LR, ADAM_B1, ADAM_B2, ADAM_EPS, ADAM_WD, ADAM_STEP = 0.001, 0.9, 0.999, 1e-08, 0.01, 10

GA_ROWS, GA_UP, GA_DN, GA_IN_O, GA_OUT_O, GA_USED = 3072, 0, 1024, 2048, 2688, 2944
GB_ROWS, GB_UP, GB_DN = 2048, 0, 1024
GC_ROWS, GC_OUT_E, GC_TAIL = 1152, 0, 256

_DIMS = {"nn": (((1,), (0,)), ((), ())), "nt": (((1,), (1,)), ((), ())), "tn": (((0,), (0,)), ((), ()))}


def _cp(sem, **kw):
    return pltpu.CompilerParams(dimension_semantics=sem, vmem_limit_bytes=VMEM_LIMIT, **kw)


def _dot(a, b, mode):
    return lax.dot_general(a.astype(BF16), b.astype(BF16), _DIMS[mode], preferred_element_type=F32)


@functools.partial(jax.custom_vjp, nondiff_argnums=(2,))
def bdot(a, b, mode):
    return _dot(a, b, mode)


def _bdot_fwd(a, b, mode):
    return _dot(a, b, mode), (a, b)


def _bdot_bwd(mode, res, g):
    a, b = res
    if mode == "nn":
        da, db = _dot(g, b, "nt"), _dot(a, g, "tn")
    elif mode == "nt":
        da, db = _dot(g, b, "nn"), _dot(g, a, "tn")
    else:
        da, db = _dot(b, g, "nt"), _dot(a, g, "nn")
    return da.astype(a.dtype), db.astype(b.dtype)


bdot.defvjp(_bdot_fwd, _bdot_bwd)


def _hdot_raw(a, b, mode):
    return lax.dot_general(a, b, _DIMS[mode], precision=lax.Precision.HIGHEST, preferred_element_type=F32)


@functools.partial(jax.custom_vjp, nondiff_argnums=(2,))
def hdot(a, b, mode):
    return _hdot_raw(a, b, mode)


def _hdot_fwd(a, b, mode):
    return _hdot_raw(a, b, mode), (a, b)


def _hdot_bwd(mode, res, g):
    a, b = res
    if mode == "nn":
        return _hdot_raw(g, b, "nt"), _hdot_raw(a, g, "tn")
    if mode == "nt":
        return _hdot_raw(g, b, "nn"), _hdot_raw(g, a, "tn")
    return _hdot_raw(b, g, "nt"), _hdot_raw(a, g, "nn")


hdot.defvjp(_hdot_fwd, _hdot_bwd)


def _log_sigmoid(x):
    return jnp.minimum(x, 0.0) - jnp.log(1.0 + jnp.exp(-jnp.abs(x)))


def _sigmoid(x):
    return 1.0 / (1.0 + jnp.exp(-x))


def _expm1(x):
    series = x * (1.0 + x * 0.5 * (1.0 + x * (1.0 / 3.0) * (1.0 + x * 0.25)))
    return jnp.where(jnp.abs(x) < 0.03, series, jnp.exp(x) - 1.0)


def _gelu_tanh(x):
    return 0.5 * x * (1.0 + jnp.tanh(0.7978845608028654 * (x + 0.044715 * x * x * x)))


def _softmax_rows(s):
    m = jnp.max(s, axis=-1, keepdims=True)
    p = jnp.exp(s - m)
    return p / jnp.sum(p, axis=-1, keepdims=True)


def _iota(shape, dim):
    return lax.broadcasted_iota(jnp.int32, shape, dim)


def _mm(a, b, mode, *, tm, tn, tk=None, out_dtype=F32, name, b_layer=None, into=None, relu_pair=False, times2=None):
    b2 = b.shape[-2:]
    if mode == "nn":
        (m, k), n = a.shape, b2[1]
    elif mode == "nt":
        (m, k), n = a.shape, b2[0]
    else:
        (k, m), n = a.shape, b2[1]
    tk = k if tk is None else tk
    assert m % tm == 0 and n % tn == 0 and k % tk == 0, (name, a.shape, b.shape)
    nk = k // tk
    a_spec = {"nn": pl.BlockSpec((tm, tk), lambda i, j, kk: (i, kk)),
              "nt": pl.BlockSpec((tm, tk), lambda i, j, kk: (i, kk)),
              "tn": pl.BlockSpec((tk, tm), lambda i, j, kk: (kk, i))}[mode]
    b_blk = {"nn": (tk, tn), "nt": (tn, tk), "tn": (tk, tn)}[mode]
    b_idx = {"nn": lambda i, j, kk: (kk, j), "nt": lambda i, j, kk: (j, kk), "tn": lambda i, j, kk: (kk, j)}[mode]
    if b_layer is None:
        b_spec = pl.BlockSpec(b_blk, b_idx)
    else:
        b_spec = pl.BlockSpec((None,) + b_blk, lambda i, j, kk: (b_layer,) + b_idx(i, j, kk))

    tile = pl.BlockSpec((tm, tn), lambda i, j, kk: (i, j))
    if into is not None:
        buf, per_slot, row_off = into
        assert m == 4 * per_slot and per_slot % tm == 0 and row_off % tm == 0 and buf.shape[2] == n, (name, buf.shape)
        bps = per_slot // tm
        out_specs = pl.BlockSpec((None, tm, tn), lambda i, j, kk: (i // bps, row_off // tm + i % bps, j))
        out_shape = jax.ShapeDtypeStruct(buf.shape, buf.dtype)
        extra_in, extra_specs, aliases = [buf], [pl.BlockSpec(memory_space=pl.ANY)], {2: 0}
        finish = lambda acc, extra: [acc.astype(buf.dtype)]
    elif relu_pair:
        out_specs = (tile, tile)
        out_shape = (jax.ShapeDtypeStruct((m, n), BF16),) * 2
        extra_in, extra_specs, aliases = [], [], {}

        def finish(acc, extra):
            r = jnp.maximum(acc, 0.0)
            return [(r * r).astype(BF16), r.astype(BF16)]
    elif times2 is not None:
        out_specs = tile
        out_shape = jax.ShapeDtypeStruct((m, n), out_dtype)
        extra_in, extra_specs, aliases = [times2], [tile], {}
        finish = lambda acc, extra: [(acc * (2.0 * extra[...].astype(F32))).astype(out_dtype)]
    else:
        out_specs = tile
        out_shape = jax.ShapeDtypeStruct((m, n), out_dtype)
        extra_in, extra_specs, aliases = [], [], {}
        finish = lambda acc, extra: [acc.astype(out_dtype)]
    n_out = 2 if relu_pair else 1

    def body(*refs):
        a_ref, b_ref = refs[0], refs[1]
        extra = refs[2] if extra_in else None
        o_refs = refs[2 + len(extra_in):2 + len(extra_in) + n_out]

        def store(acc):
            for o_ref, val in zip(o_refs, finish(acc, extra)):
                o_ref[...] = val

        if nk == 1:
            store(_dot(a_ref[...], b_ref[...], mode))
            return
        acc_ref = refs[-1]
        kk = pl.program_id(2)

        @pl.when(kk == 0)
        def _():
            acc_ref[...] = jnp.zeros_like(acc_ref)

        acc_ref[...] += _dot(a_ref[...], b_ref[...], mode)

        @pl.when(kk == nk - 1)
        def _():
            store(acc_ref[...])

    return pl.pallas_call(
        body, name=name, grid=(m // tm, n // tn, nk),
        in_specs=[a_spec, b_spec] + extra_specs,
        out_specs=out_specs, out_shape=out_shape,
        scratch_shapes=[pltpu.VMEM((tm, tn), F32)] if nk > 1 else [],
        input_output_aliases=aliases,
        compiler_params=_cp(("parallel", "parallel", "arbitrary")),
    )(a, b, *extra_in)


ROWS = 256


def _prenorm(x, w, name):
    def body(x_ref, w_ref, o_ref):
        xv = x_ref[...]
        r = lax.rsqrt(jnp.mean(xv * xv, axis=-1, keepdims=True) + EPS)
        o_ref[...] = (xv * r * w_ref[...]).astype(BF16)

    return pl.pallas_call(
        body, name=name, grid=(T // ROWS,),
        in_specs=[pl.BlockSpec((ROWS, D), lambda i: (i, 0)), pl.BlockSpec((1, D), lambda i: (0, 0))],
        out_specs=pl.BlockSpec((ROWS, D), lambda i: (i, 0)),
        out_shape=jax.ShapeDtypeStruct((T, D), BF16),
        compiler_params=_cp(("parallel",)),
    )(x, w)


def _postnorm(x, z, w, name):
    def body(x_ref, z_ref, w_ref, o_ref):
        zv = z_ref[...]
        r = lax.rsqrt(jnp.mean(zv * zv, axis=-1, keepdims=True) + EPS)
        o_ref[...] = x_ref[...] + zv * r * w_ref[...]

    return pl.pallas_call(
        body, name=name, grid=(T // ROWS,),
        in_specs=[pl.BlockSpec((ROWS, D), lambda i: (i, 0)), pl.BlockSpec((ROWS, D), lambda i: (i, 0)),
                  pl.BlockSpec((1, D), lambda i: (0, 0))],
        out_specs=pl.BlockSpec((ROWS, D), lambda i: (i, 0)),
        out_shape=jax.ShapeDtypeStruct((T, D), F32),
        compiler_params=_cp(("parallel",)),
    )(x, z, w)


def _norm_bwd(z, w, dy, add, name):
    has_add = add is not None

    def body(*refs):
        if has_add:
            z_ref, w_ref, dy_ref, add_ref, dz_ref, dw_ref = refs
        else:
            z_ref, w_ref, dy_ref, dz_ref, dw_ref = refs
        i = pl.program_id(0)

        @pl.when(i == 0)
        def _():
            dw_ref[...] = jnp.zeros_like(dw_ref)

        zv = z_ref[...].astype(F32)
        dyv = dy_ref[...]
        r = lax.rsqrt(jnp.mean(zv * zv, axis=-1, keepdims=True) + EPS)
        wdy = dyv * w_ref[...]
        dz = r * wdy - zv * (r * r * r) * jnp.mean(zv * wdy, axis=-1, keepdims=True)
        if has_add:
            dz = dz + add_ref[...]
        dz_ref[...] = dz.astype(dz_ref.dtype)
        dw_ref[...] += jnp.sum(dyv * zv * r, axis=0, keepdims=True)

    row = pl.BlockSpec((ROWS, D), lambda i: (i, 0))
    vec = pl.BlockSpec((1, D), lambda i: (0, 0))
    ins = [z, w, dy] + ([add] if has_add else [])
    dz_dtype = F32 if has_add else BF16
    return pl.pallas_call(
        body, name=name, grid=(T // ROWS,),
        in_specs=[row, vec, row] + ([row] if has_add else []),
        out_specs=(row, vec),
        out_shape=(jax.ShapeDtypeStruct((T, D), dz_dtype), jax.ShapeDtypeStruct((1, D), F32)),
        compiler_params=_cp(("arbitrary",)),
    )(*ins)


def _loss_and_grad(y, tgt):
    def body(y_ref, t_ref, g_ref, l_ref):
        i = pl.program_id(0)

        @pl.when(i == 0)
        def _():
            l_ref[...] = jnp.zeros_like(l_ref)

        e = y_ref[...] - t_ref[...]
        g_ref[...] = e * (1.0 / D)
        l_ref[...] += jnp.sum(e * e) * (0.5 / D)

    row = pl.BlockSpec((ROWS, D), lambda i: (i, 0))
    return pl.pallas_call(
        body, name="loss_head", grid=(T // ROWS,), in_specs=[row, row],
        out_specs=(row, pl.BlockSpec((1, 128), lambda i: (0, 0))),
        out_shape=(jax.ShapeDtypeStruct((T, D), F32), jax.ShapeDtypeStruct((1, 128), F32)),
        compiler_params=_cp(("arbitrary",)),
    )(y, tgt)


def _adamw(w, g, m, v, name):
    lead = w.shape[:-2]
    assert len(lead) <= 1 and g.shape == w.shape, (name, w.shape, g.shape)
    rows, cols = w.shape[-2:]
    if rows <= 512:
        tr, tc = rows, cols
    elif rows % 256 == 0:
        tr, tc = 256, cols
    else:
        tr, tc = rows, 256
    assert rows % tr == 0 and cols % tc == 0, (name, w.shape)
    c1 = 1.0 - ADAM_B1 ** ADAM_STEP
    c2 = 1.0 - ADAM_B2 ** ADAM_STEP

    def body(w_ref, g_ref, m_ref, v_ref, d_ref, mo_ref, vo_ref):
        gv = g_ref[...]
        mn = ADAM_B1 * m_ref[...] + (1.0 - ADAM_B1) * gv
        vn = ADAM_B2 * v_ref[...] + (1.0 - ADAM_B2) * (gv * gv)
        m_hat = mn / c1
        v_hat = vn / c2
        d_ref[...] = -ADAM_LR * (m_hat / (jnp.sqrt(v_hat) + ADAM_EPS) + ADAM_WD * w_ref[...])
        mo_ref[...] = mn
        vo_ref[...] = vn

    if lead:
        grid = (lead[0], rows // tr, cols // tc)
        blk = pl.BlockSpec((None, tr, tc), lambda l, i, j: (l, i, j))
    else:
        grid = (rows // tr, cols // tc)
        blk = pl.BlockSpec((tr, tc), lambda i, j: (i, j))
    sds = jax.ShapeDtypeStruct(w.shape, F32)
    return pl.pallas_call(body, name=name, grid=grid, in_specs=[blk] * 4, out_specs=(blk,) * 3,
                          out_shape=(sds,) * 3, compiler_params=_cp(("parallel",) * len(grid)))(w, g, m, v)


def _gla_consts():
    ltri = (_iota((CHUNK, CHUNK), 0) >= _iota((CHUNK, CHUNK), 1)).astype(F32)
    ones_c = jnp.ones((CHUNK, 128), F32)
    mask = (_iota((256, 512), 0) // 64 == _iota((256, 512), 1) // 128).astype(F32)
    return ltri, ones_c, mask


def _gla_chunk(consts, q, k, v, r, aux, s_prev, wa, ba, nw):
    ltri, ones_c, mask = consts
    la = _log_sigmoid(bdot(aux, wa, "nn") + ba) * (1.0 / 16.0)
    cum = hdot(ltri, la, "nn")
    total = jnp.sum(la, axis=0, keepdims=True)
    k_dec = k * jnp.exp(total - cum)
    inc = bdot(k_dec, v, "tn") * mask
    dec = jnp.exp(hdot(la, ones_c, "tn"))
    dec = jnp.concatenate([dec, dec, dec, dec], axis=1)
    s_new = dec * s_prev + inc
    o = bdot(q * GLA_SCALE, s_new, "nn")
    parts = []
    for h in range(4):
        oh = o[:, h * 128:(h + 1) * 128]
        parts.append(oh * lax.rsqrt(jnp.mean(oh * oh, axis=-1, keepdims=True) + EPS))
    on = jnp.concatenate(parts, axis=1)
    return s_new, on * nw * (r * _sigmoid(r))


def _gla_specs(cmap):
    return [pl.BlockSpec((CHUNK, 256), lambda c: (cmap(c), 0)),
            pl.BlockSpec((CHUNK, 256), lambda c: (cmap(c), 1)),
            pl.BlockSpec((CHUNK, 512), lambda c: (cmap(c), 1)),
            pl.BlockSpec((CHUNK, 512), lambda c: (cmap(c), 2)),
            pl.BlockSpec((CHUNK, 128), lambda c: (cmap(c), AUX_BLK))]


def _gla_fwd(proj, wa, ba, nw):
    def body(q_ref, k_ref, v_ref, r_ref, aux_ref, wa_ref, ba_ref, nw_ref, o_ref, sp_ref, s_ref):
        c = pl.program_id(0)

        @pl.when(c == 0)
        def _():
            s_ref[...] = jnp.zeros_like(s_ref)

        s_prev = s_ref[...]
        sp_ref[...] = s_prev
        s_new, out = _gla_chunk(_gla_consts(), q_ref[...], k_ref[...], v_ref[...], r_ref[...], aux_ref[...],
                                s_prev, wa_ref[...], ba_ref[...], nw_ref[...])
        s_ref[...] = s_new
        o_ref[...] = out

    full = lambda shape: pl.BlockSpec(shape, lambda c: (0,) * len(shape))
    return pl.pallas_call(
        body, name="gla_fwd", grid=(NCHUNK,),
        in_specs=_gla_specs(lambda c: c) + [full((128, 256)), full((1, 256)), full((1, 512))],
        out_specs=(pl.BlockSpec((CHUNK, 512), lambda c: (c, 0)), pl.BlockSpec((None, 256, 512), lambda c: (c, 0, 0))),
        out_shape=(jax.ShapeDtypeStruct((T, D), F32), jax.ShapeDtypeStruct((NCHUNK, 256, 512), F32)),
        scratch_shapes=[pltpu.VMEM((256, 512), F32)],
        compiler_params=_cp(("arbitrary",)),
    )(proj, proj, proj, proj, proj, wa, ba, nw)


def _gla_bwd(proj, s_prev_all, wa, ba, nw, dcat):
    rev = lambda c: NCHUNK - 1 - c

    def body(q_ref, k_ref, v_ref, r_ref, aux_ref, sp_ref, wa_ref, ba_ref, nw_ref, do_ref,
             dq_ref, dk_ref, dv_ref, dr_ref, daux_ref, dwa_ref, dba_ref, dnw_ref, ds_ref):
        c = pl.program_id(0)

        @pl.when(c == 0)
        def _():
            ds_ref[...] = jnp.zeros_like(ds_ref)
            dwa_ref[...] = jnp.zeros_like(dwa_ref)
            dba_ref[...] = jnp.zeros_like(dba_ref)
            dnw_ref[...] = jnp.zeros_like(dnw_ref)

        fn = functools.partial(_gla_chunk, _gla_consts())
        _, vjp = jax.vjp(fn, q_ref[...], k_ref[...], v_ref[...], r_ref[...], aux_ref[...], sp_ref[...],
                         wa_ref[...], ba_ref[...], nw_ref[...])
        dq, dk, dv, dr, daux, dsp, dwa, dba, dnw = vjp((ds_ref[...], do_ref[...]))
        dq_ref[...] = dq
        dk_ref[...] = dk
        dv_ref[...] = dv
        dr_ref[...] = dr
        daux_ref[...] = daux
        ds_ref[...] = dsp
        dwa_ref[...] += dwa
        dba_ref[...] += dba
        dnw_ref[...] += dnw

    full = lambda shape: pl.BlockSpec(shape, lambda c: (0,) * len(shape))
    blk = lambda w: pl.BlockSpec((CHUNK, w), lambda c: (rev(c), 0))
    sds = lambda *s: jax.ShapeDtypeStruct(s, F32)
    return pl.pallas_call(
        body, name="gla_bwd", grid=(NCHUNK,),
        in_specs=_gla_specs(rev) + [pl.BlockSpec((None, 256, 512), lambda c: (rev(c), 0, 0)),
                                    full((128, 256)), full((1, 256)), full((1, 512)), blk(512)],
        out_specs=(blk(256), blk(256), blk(512), blk(512), blk(128), full((128, 256)), full((1, 256)), full((1, 512))),
        out_shape=(sds(T, 256), sds(T, 256), sds(T, 512), sds(T, 512), sds(T, 128),
                   sds(128, 256), sds(1, 256), sds(1, 512)),
        scratch_shapes=[pltpu.VMEM((256, 512), F32)],
        compiler_params=_cp(("arbitrary",)),
    )(proj, proj, proj, proj, proj, s_prev_all, wa, ba, nw, dcat)


GATE_ROWS = 128


def _fox_gate_block(ltri, aux, bpad, carry):
    lf = _log_sigmoid(aux + bpad)
    cum = hdot(ltri, lf, "nn") + carry
    return cum, carry + jnp.sum(lf, axis=0, keepdims=True)


def _gate_ltri():
    return (_iota((GATE_ROWS, GATE_ROWS), 0) >= _iota((GATE_ROWS, GATE_ROWS), 1)).astype(F32)


def _fox_gate_fwd(proj, bpad):
    def body(aux_ref, b_ref, cum_ref, carry_ref):
        i = pl.program_id(0)

        @pl.when(i == 0)
        def _():
            carry_ref[...] = jnp.zeros_like(carry_ref)

        cum, carry = _fox_gate_block(_gate_ltri(), aux_ref[...], b_ref[...], carry_ref[...])
        cum_ref[...] = cum
        carry_ref[...] = carry

    return pl.pallas_call(
        body, name="fox_gate_fwd", grid=(T // GATE_ROWS,),
        in_specs=[pl.BlockSpec((GATE_ROWS, 128), lambda i: (i, AUX_BLK)), pl.BlockSpec((1, 128), lambda i: (0, 0))],
        out_specs=pl.BlockSpec((GATE_ROWS, 128), lambda i: (i, 0)),
        out_shape=jax.ShapeDtypeStruct((T, 128), F32),
        scratch_shapes=[pltpu.VMEM((1, 128), F32)],
        compiler_params=_cp(("arbitrary",)),
    )(proj, bpad)


def _fox_gate_bwd(proj, bpad, dccol_t, daux_gla):
    nb = T // GATE_ROWS
    rev = lambda i: nb - 1 - i

    def body(aux_ref, b_ref, dc_ref, dg_ref, daux_ref, db_ref, dcarry_ref):
        i = pl.program_id(0)

        @pl.when(i == 0)
        def _():
            dcarry_ref[...] = jnp.zeros_like(dcarry_ref)
            db_ref[...] = jnp.zeros_like(db_ref)

        dcum = dc_ref[...]
        fn = functools.partial(_fox_gate_block, _gate_ltri())
        _, vjp = jax.vjp(fn, aux_ref[...], b_ref[...], jnp.zeros((1, 128), F32))
        daux, db, dcarry = vjp((dcum, dcarry_ref[...]))
        daux_ref[...] = daux + dg_ref[...]
        db_ref[...] += db
        dcarry_ref[...] = dcarry

    blk = pl.BlockSpec((GATE_ROWS, 128), lambda i: (rev(i), 0))
    vec = pl.BlockSpec((1, 128), lambda i: (0, 0))
    return pl.pallas_call(
        body, name="fox_gate_bwd", grid=(nb,),
        in_specs=[pl.BlockSpec((GATE_ROWS, 128), lambda i: (rev(i), AUX_BLK)), vec, blk, blk],
        out_specs=(blk, vec),
        out_shape=(jax.ShapeDtypeStruct((T, 128), F32), jax.ShapeDtypeStruct((1, 128), F32)),
        scratch_shapes=[pltpu.VMEM((1, 128), F32)],
        compiler_params=_cp(("arbitrary",)),
    )(proj, bpad, dccol_t, daux_gla)


FOX_Q = 128


FOX_QB = T // FOX_Q


@jax.custom_vjp
def _attend(s, v):
    return _attend_fwd(s, v)[0]


def _attend_fwd(s, v):
    e = jnp.exp(s - jnp.max(s, axis=-1, keepdims=True))
    r = 1.0 / jnp.sum(e, axis=-1, keepdims=True)
    return _dot(e, v, "nn") * r, (e, r, v)


def _attend_bwd(res, do):
    e, r, v = res
    do_r = do * r
    dpr = _dot(do_r, v, "nt")
    ds = e * (dpr - r * jnp.sum(e * dpr, axis=-1, keepdims=True))
    return ds, _dot(e, do_r, "tn").astype(v.dtype)


_attend.defvjp(_attend_fwd, _attend_bwd)


def _fox_block(hp, q, k, v, ccol):
    kl = k.shape[0]
    lane = _iota((FOX_Q, 128), 1)
    tri = jnp.bitwise_and(_iota((2 * FOX_Q, FOX_Q), 0), FOX_Q - 1) >= _iota((2 * FOX_Q, FOX_Q), 1)
    sub = _iota((8, kl), 0)
    qs = q * ATT_SCALE
    q2 = jnp.concatenate([jnp.where(lane < 64, qs, 0.0), jnp.where(lane >= 64, qs, 0.0)], axis=0)
    s = bdot(q2, k, "nt")
    cs = [jnp.sum(jnp.where(sub == 2 * hp + e, ccol, 0.0), axis=0, keepdims=True) for e in range(2)]
    s = jnp.concatenate([s[:FOX_Q] - cs[0], s[FOX_Q:] - cs[1]], axis=0)
    diag = jnp.where(tri, s[:, kl - FOX_Q:], NEG)
    s = diag if kl == FOX_Q else jnp.concatenate([s[:, :kl - FOX_Q], diag], axis=1)
    o2 = _attend(s, v)
    return jnp.where(lane < 64, o2[:FOX_Q], o2[FOX_Q:])


def _fox_in_specs():
    return [pl.BlockSpec((FOX_Q, 128), lambda hp, qb: (qb, 12 + hp)),
            pl.BlockSpec((T, 128), lambda hp, qb: (0, 16 + hp)),
            pl.BlockSpec((T, 128), lambda hp, qb: (0, 20 + hp)),
            pl.BlockSpec((8, T), lambda hp, qb: (0, 0))]


def _fox_fwd(proj, cum_c, cat):
    def body(q_ref, k_ref, v_ref, cc_ref, cat_ref, o_ref):
        qb = pl.program_id(1)
        for g in range(FOX_QB):
            kl = FOX_Q * (g + 1)

            @pl.when(qb == g)
            def _(kl=kl):
                o_ref[...] = _fox_block(pl.program_id(0), q_ref[...], k_ref[0:kl, :], v_ref[0:kl, :], cc_ref[:, 0:kl])

    return pl.pallas_call(
        body, name="fox_fwd", grid=(4, FOX_QB), in_specs=_fox_in_specs() + [pl.BlockSpec(memory_space=pl.ANY)],
        out_specs=pl.BlockSpec((FOX_Q, 128), lambda hp, qb: (qb, 4 + hp)),
        out_shape=jax.ShapeDtypeStruct((T, D), F32), input_output_aliases={4: 0},
        compiler_params=_cp(("parallel", "parallel")),
    )(proj, proj, proj, cum_c, cat)


def _fox_bwd(proj, cum_c, dcat):
    def body(q_ref, k_ref, v_ref, cc_ref, do_ref, dq_ref, dk_ref, dv_ref, dcc_ref):
        qb = pl.program_id(1)

        @pl.when(qb == 0)
        def _():
            dk_ref[...] = jnp.zeros_like(dk_ref)
            dv_ref[...] = jnp.zeros_like(dv_ref)
            dcc_ref[...] = jnp.zeros_like(dcc_ref)

        fn = functools.partial(_fox_block, pl.program_id(0))
        for g in range(FOX_QB):
            kl = FOX_Q * (g + 1)

            @pl.when(qb == g)
            def _(kl=kl):
                _, vjp = jax.vjp(fn, q_ref[...], k_ref[0:kl, :], v_ref[0:kl, :], cc_ref[:, 0:kl])
                dq, dk, dv, dcc = vjp(do_ref[...])
                dq_ref[...] = dq
                dk_ref[0:kl, :] += dk
                dv_ref[0:kl, :] += dv
                dcc_ref[:, 0:kl] += dcc

    sds = lambda *s: jax.ShapeDtypeStruct(s, F32)
    return pl.pallas_call(
        body, name="fox_bwd", grid=(4, FOX_QB),
        in_specs=_fox_in_specs() + [pl.BlockSpec((FOX_Q, 128), lambda hp, qb: (qb, 4 + hp))],
        out_specs=(pl.BlockSpec((FOX_Q, 128), lambda hp, qb: (qb, hp)),
                   pl.BlockSpec((T, 128), lambda hp, qb: (0, hp)),
                   pl.BlockSpec((T, 128), lambda hp, qb: (0, hp)),
                   pl.BlockSpec((None, 8, T), lambda hp, qb: (hp, 0, 0))),
        out_shape=(sds(T, 512), sds(T, 512), sds(T, 512), sds(4, 8, T)),
        compiler_params=_cp(("parallel", "arbitrary")),
    )(proj, proj, proj, cum_c, dcat)


BIAS_W = 640


def _rel_onehot():
    j = _iota((REL_PAD, BIAS_W), 1)
    rel = jnp.clip(CA_PAD + CHUNK - 1 - j, -128, 128) + 128
    return (_iota((REL_PAD, BIAS_W), 0) == rel).astype(F32)


def _bias_build(rbp):
    def body(rb_ref, o_ref):
        f = _hdot_raw(rb_ref[...], _rel_onehot(), "nn")
        for q in range(CHUNK):
            o_ref[q] = pltpu.roll(f, (BIAS_W - (CHUNK - 1 - q)) % BIAS_W, 1)[:, :CA_BAND]

    return pl.pallas_call(body, name="ca_bias_build", out_shape=jax.ShapeDtypeStruct((CHUNK, 8, CA_BAND), F32))(rbp)


def _bias_grad(dbias_q):
    def body(db_ref, o_ref):
        acc = jnp.zeros((8, BIAS_W), F32)
        for q in range(CHUNK):
            acc = acc + pltpu.roll(db_ref[q], CHUNK - 1 - q, 1)
        o_ref[...] = _hdot_raw(acc, _rel_onehot(), "nt")

    return pl.pallas_call(body, name="ca_bias_grad", out_shape=jax.ShapeDtypeStruct((8, REL_PAD), F32))(dbias_q)


def _ca_block(c, masked, q, kb, vb, bias2):
    lane = _iota((CHUNK, 128), 1)
    qs = q * ATT_SCALE
    q2 = jnp.concatenate([jnp.where(lane < 64, qs, 0.0), jnp.where(lane >= 64, qs, 0.0)], axis=0)
    s = bdot(q2, kb, "nt") + bias2.reshape(2 * CHUNK, CA_BAND)
    if masked:
        s = jnp.where((c * CHUNK - CA_PAD + _iota((2 * CHUNK, CA_BAND), 1)) >= 0, s, NEG)
    o2 = _attend(s, vb)
    return jnp.where(lane < 64, o2[:CHUNK], o2[CHUNK:])


CA_PER_STEP = 4
CA_ROWS = CA_PER_STEP * CHUNK
CA_MASKED_STEPS = CA_PAD // CA_ROWS


def _ca_fwd(proj, kvpad, bias):
    def body(q_ref, k_ref, v_ref, b_ref, o_ref):
        def run(masked):
            outs = []
            for i in range(CA_PER_STEP):
                c = pl.program_id(1) * CA_PER_STEP + i
                band = pl.ds(pl.multiple_of(c * CHUNK, CHUNK), CA_BAND)
                rows = slice(i * CHUNK, (i + 1) * CHUNK)
                outs.append(_ca_block(c, masked, q_ref[rows, :], k_ref[band, :], v_ref[band, :], b_ref[...]))
            for i in range(CA_PER_STEP):
                o_ref[i * CHUNK:(i + 1) * CHUNK, :] = outs[i]

        pl.when(pl.program_id(1) < CA_MASKED_STEPS)(lambda: run(True))
        pl.when(pl.program_id(1) >= CA_MASKED_STEPS)(lambda: run(False))

    return pl.pallas_call(
        body, name="ca_fwd", grid=(4, NCHUNK // CA_PER_STEP),
        in_specs=[pl.BlockSpec((CA_ROWS, 128), lambda hp, c: (c, hp)),
                  pl.BlockSpec((T + CA_PAD, 128), lambda hp, c: (0, hp)),
                  pl.BlockSpec((T + CA_PAD, 128), lambda hp, c: (0, 4 + hp)),
                  pl.BlockSpec((2, CHUNK, CA_BAND), lambda hp, c: (hp, 0, 0))],
        out_specs=pl.BlockSpec((CA_ROWS, 128), lambda hp, c: (c, hp)),
        out_shape=jax.ShapeDtypeStruct((T, D), F32),
        compiler_params=_cp(("parallel", "parallel")),
    )(proj, kvpad, kvpad, bias)


def _ca_bwd(proj, kvpad, bias, dcat):
    def body(q_ref, k_ref, v_ref, b_ref, do_ref, dq_ref, dk_ref, dv_ref, db_ref):
        c = pl.program_id(1)

        @pl.when(c == 0)
        def _():
            dk_ref[...] = jnp.zeros_like(dk_ref)
            dv_ref[...] = jnp.zeros_like(dv_ref)
            db_ref[...] = jnp.zeros_like(db_ref)

        def run(masked):
            grads, bands = [], []
            for i in range(CA_PER_STEP):
                ci = c * CA_PER_STEP + i
                band = pl.ds(pl.multiple_of(ci * CHUNK, CHUNK), CA_BAND)
                rows = slice(i * CHUNK, (i + 1) * CHUNK)
                fn = functools.partial(_ca_block, ci, masked)
                _, vjp = jax.vjp(fn, q_ref[rows, :], k_ref[band, :], v_ref[band, :], b_ref[...])
                grads.append(vjp(do_ref[rows, :]))
                bands.append(band)
            for i, (dq, _, _, _) in enumerate(grads):
                dq_ref[i * CHUNK:(i + 1) * CHUNK, :] = dq
            for band, (_, dkb, dvb, _) in zip(bands, grads):
                dk_ref[band, :] += dkb
                dv_ref[band, :] += dvb
            db_ref[...] += (grads[0][3] + grads[1][3]) + (grads[2][3] + grads[3][3])

        pl.when(c < CA_MASKED_STEPS)(lambda: run(True))
        pl.when(c >= CA_MASKED_STEPS)(lambda: run(False))

    sds = lambda *s: jax.ShapeDtypeStruct(s, F32)
    padded = lambda: pl.BlockSpec((T + CA_PAD, 128), lambda hp, c: (0, hp))
    return pl.pallas_call(
        body, name="ca_bwd", grid=(4, NCHUNK // CA_PER_STEP),
        in_specs=[pl.BlockSpec((CA_ROWS, 128), lambda hp, c: (c, hp)),
                  pl.BlockSpec((T + CA_PAD, 128), lambda hp, c: (0, hp)),
                  pl.BlockSpec((T + CA_PAD, 128), lambda hp, c: (0, 4 + hp)),
                  pl.BlockSpec((2, CHUNK, CA_BAND), lambda hp, c: (hp, 0, 0)),
                  pl.BlockSpec((CA_ROWS, 128), lambda hp, c: (c, hp))],
        out_specs=(pl.BlockSpec((CA_ROWS, 128), lambda hp, c: (c, hp)), padded(), padded(),
                   pl.BlockSpec((2, CHUNK, CA_BAND), lambda hp, c: (hp, 0, 0))),
        out_shape=(sds(T, 512), sds(T + CA_PAD, 512), sds(T + CA_PAD, 512), sds(8, CHUNK, CA_BAND)),
        compiler_params=_cp(("parallel", "arbitrary")),
    )(proj, kvpad, kvpad, bias, dcat)


def _lru_pre(xs, cw, cb, wa, ba, wx, bx, lam):
    xc = cb + xs[0] * cw[0:1, :] + xs[1] * cw[1:2, :] + xs[2] * cw[2:3, :] + xs[3] * cw[3:4, :]
    ra = _sigmoid(bdot(xc, wa, "nn") + ba)
    ii = _sigmoid(bdot(xc, wx, "nn") + bx)
    la = 8.0 * ra * _log_sigmoid(lam)
    return jnp.exp(la), jnp.sqrt(-_expm1(2.0 * la)) * (ii * xc)


def _lru_pre_specs():
    full = lambda shape: pl.BlockSpec(shape, lambda i: (0,) * len(shape))
    return [pl.BlockSpec((4, ROWS, 512), lambda i: (0, i, 0)), full((4, 512)), full((1, 512)),
            full((512, 512)), full((1, 512)), full((512, 512)), full((1, 512)), full((1, 512))]


def _lru_pre_fwd(xs, cw, cb, wa, ba, wx, bx, lam):
    def body(xs_ref, cw_ref, cb_ref, wa_ref, ba_ref, wx_ref, bx_ref, lam_ref, a_ref, b_ref):
        a, b = _lru_pre(xs_ref[...], cw_ref[...], cb_ref[...], wa_ref[...], ba_ref[...], wx_ref[...], bx_ref[...],
                        lam_ref[...])
        a_ref[...] = a
        b_ref[...] = b

    row = pl.BlockSpec((ROWS, 512), lambda i: (i, 0))
    sds = jax.ShapeDtypeStruct((T, 512), F32)
    return pl.pallas_call(body, name="lru_pre_fwd", grid=(T // ROWS,), in_specs=_lru_pre_specs(),
                          out_specs=(row, row), out_shape=(sds, sds), compiler_params=_cp(("parallel",)),
                          )(xs, cw, cb, wa, ba, wx, bx, lam)


def _lru_pre_bwd(xs, cw, cb, wa, ba, wx, bx, lam, da, db):
    def body(xs_ref, cw_ref, cb_ref, wa_ref, ba_ref, wx_ref, bx_ref, lam_ref, da_ref, db_ref,
             dxs_ref, dcw_ref, dcb_ref, dwa_ref, dba_ref, dwx_ref, dbx_ref, dlam_ref):
        acc = (dcw_ref, dcb_ref, dwa_ref, dba_ref, dwx_ref, dbx_ref, dlam_ref)

        @pl.when(pl.program_id(0) == 0)
        def _():
            for r in acc:
                r[...] = jnp.zeros_like(r)

        _, vjp = jax.vjp(_lru_pre, xs_ref[...], cw_ref[...], cb_ref[...], wa_ref[...], ba_ref[...], wx_ref[...],
                         bx_ref[...], lam_ref[...])
        grads = vjp((da_ref[...], db_ref[...]))
        dxs_ref[...] = grads[0]
        for r, g in zip(acc, grads[1:]):
            r[...] += g

    row = pl.BlockSpec((ROWS, 512), lambda i: (i, 0))
    specs = _lru_pre_specs()
    sds = lambda *s: jax.ShapeDtypeStruct(s, F32)
    return pl.pallas_call(
        body, name="lru_pre_bwd", grid=(T // ROWS,), in_specs=specs + [row, row], out_specs=tuple(specs),
        out_shape=(sds(4, T, 512), sds(4, 512), sds(1, 512), sds(512, 512), sds(1, 512), sds(512, 512), sds(1, 512),
                   sds(1, 512)),
        compiler_params=_cp(("arbitrary",)),
    )(xs, cw, cb, wa, ba, wx, bx, lam, da, db)


def _lru_scan_fwd(a, b):
    def body(a_ref, b_ref, h_ref):
        def step(t, h):
            h = a_ref[pl.ds(t, 1), :] * h + b_ref[pl.ds(t, 1), :]
            h_ref[pl.ds(t, 1), :] = h
            return h

        lax.fori_loop(0, T, step, jnp.zeros((1, 512), F32), unroll=8)

    return pl.pallas_call(body, name="lru_scan_fwd", out_shape=jax.ShapeDtypeStruct((T, 512), F32),
                          compiler_params=pltpu.CompilerParams(vmem_limit_bytes=VMEM_LIMIT))(a, b)


def _lru_scan_bwd(a, h, dh):
    def body(a_ref, h_ref, dh_ref, da_ref, db_ref):
        def step(i, carry):
            t = T - 1 - i
            g = dh_ref[pl.ds(t, 1), :] + carry
            db_ref[pl.ds(t, 1), :] = g
            da_ref[pl.ds(t, 1), :] = g * h_ref[pl.ds(t - 1, 1), :]
            return a_ref[pl.ds(t, 1), :] * g

        carry = lax.fori_loop(0, T - 1, step, jnp.zeros((1, 512), F32), unroll=8)
        db_ref[pl.ds(0, 1), :] = dh_ref[pl.ds(0, 1), :] + carry
        da_ref[pl.ds(0, 1), :] = jnp.zeros((1, 512), F32)

    sds = jax.ShapeDtypeStruct((T, 512), F32)
    return pl.pallas_call(body, name="lru_scan_bwd", out_shape=(sds, sds),
                          compiler_params=pltpu.CompilerParams(vmem_limit_bytes=VMEM_LIMIT))(a, h, dh)


def _lru_post(h, gate):
    return h * _gelu_tanh(gate)


def _lru_post_fwd(h, proj, cat):
    def body(h_ref, g_ref, cat_ref, o_ref):
        o_ref[...] = _lru_post(h_ref[...], g_ref[...])

    row = pl.BlockSpec((ROWS, 512), lambda i: (i, 0))
    return pl.pallas_call(body, name="lru_post_fwd", grid=(T // ROWS,),
                          in_specs=[row, pl.BlockSpec((ROWS, 512), lambda i: (i, 3)), pl.BlockSpec(memory_space=pl.ANY)],
                          out_specs=pl.BlockSpec((ROWS, 512), lambda i: (i, 1)),
                          out_shape=jax.ShapeDtypeStruct((T, D), F32), input_output_aliases={2: 0},
                          compiler_params=_cp(("parallel",)))(h, proj, cat)


def _lru_post_bwd(h, proj, dcat):
    def body(h_ref, g_ref, do_ref, dh_ref, dg_ref):
        _, vjp = jax.vjp(_lru_post, h_ref[...], g_ref[...])
        dh, dg = vjp(do_ref[...])
        dh_ref[...] = dh
        dg_ref[...] = dg

    row = pl.BlockSpec((ROWS, 512), lambda i: (i, 0))
    sds = jax.ShapeDtypeStruct((T, 512), F32)
    return pl.pallas_call(body, name="lru_post_bwd", grid=(T // ROWS,),
                          in_specs=[row, pl.BlockSpec((ROWS, 512), lambda i: (i, 3)),
                                    pl.BlockSpec((ROWS, 512), lambda i: (i, 1))],
                          out_specs=(row, row), out_shape=(sds, sds), compiler_params=_cp(("parallel",)))(h, proj, dcat)


def _conv_dx(dxs_shift):
    def body(d_ref, o_ref):
        o_ref[...] = d_ref[0] + d_ref[1] + d_ref[2] + d_ref[3]

    row = pl.BlockSpec((ROWS, 512), lambda i: (i, 0))
    return pl.pallas_call(body, name="lru_conv_dx", grid=(T // ROWS,),
                          in_specs=[pl.BlockSpec((4, ROWS, 512), lambda i: (0, i, 0))], out_specs=row,
                          out_shape=jax.ShapeDtypeStruct((T, 512), F32), compiler_params=_cp(("parallel",)))(dxs_shift)


def _position():
    return lax.axis_index("x"), lax.axis_index("y"), lax.axis_index("c")


def _other_chips(x, y):
    return [(1 - x, y), (x, 1 - y), (1 - x, 1 - y)]


def _al(v, n):
    return v * n if isinstance(v, int) else pl.multiple_of(v * n, n)


_AG_ITEMS = [
    ((4, 32, 128), lambda o, s, h: o.at[s, pl.ds(_al(h, 16), 16), :], lambda r, h: r.at[pl.ds(_al(h, 16), 16), :]),
    ((4, 774, 1024), lambda o, s, h: o.at[s, :, pl.ds(_al(h, 512), 512)], lambda r, h: r.at[:, pl.ds(_al(h, 512), 512)]),
    ((1024, 1024), lambda o, s, h: o.at[pl.ds(_al(2 * s + h, 128), 128), :], lambda r, h: r.at[pl.ds(_al(h, 128), 128), :]),
    ((2, 1024, 4096), lambda o, s, h: o.at[h, :, pl.ds(_al(s, 1024), 1024)], lambda r, h: r.at[h]),
    ((2, 4096, 1024), lambda o, s, h: o.at[h, pl.ds(_al(s, 1024), 1024), :], lambda r, h: r.at[h]),
    ((1024, 2560), lambda o, s, h: o.at[pl.ds(_al(h, 512), 512), pl.ds(_al(s, 640), 640)],
     lambda r, h: r.at[pl.ds(_al(h, 512), 512), :]),
    ((1024, 1024), lambda o, s, h: o.at[pl.ds(_al(2 * s + h, 128), 128), :], lambda r, h: r.at[pl.ds(_al(h, 128), 128), :]),
]


_AG_GROUPS = [(0, 1, 2), (3, 4), (5, 6)]

_HBM = pl.BlockSpec(memory_space=pltpu.HBM)
_SEM = pl.BlockSpec(memory_space=pltpu.SEMAPHORE)
_SPLIT = dict(has_side_effects=pltpu.SideEffectType.DATAFLOW_SIDE_EFFECTING)


def _hbm(a):
    return pltpu.with_memory_space_constraint(a, pltpu.HBM)


def _ag_ici_copy(i, j, chip, c, slot, src_ref, land_ref, send_sems, recv_sems, k):
    _, dst, half = _AG_ITEMS[i]
    return pltpu.make_async_remote_copy(src_ref=half(src_ref, c), dst_ref=dst(land_ref, slot, c), send_sem=send_sems.at[k],
                                        recv_sem=recv_sems.at[k], device_id=(*chip, c), device_id_type=MESH)


def _ag_start(shards):
    n = len(_AG_ITEMS)
    ng = len(_AG_GROUPS)
    lands = [lax.empty(shape, s.dtype) for (shape, _, _), s in zip(_AG_ITEMS, shards)]

    def body(*refs):
        srcs, land_refs = refs[:n], refs[n:2 * n]
        sems = refs[2 * n:2 * n + 2 * ng]
        token = refs[-1]
        x, y, c = _position()
        me = 2 * x + y
        for g, items in enumerate(_AG_GROUPS):
            for t, i in enumerate(items):
                for j, chip in enumerate(_other_chips(x, y)):
                    _ag_ici_copy(i, j, chip, c, me, srcs[i], land_refs[i], sems[2 * g], sems[2 * g + 1], 3 * t + j).start()
        token[...] = jnp.zeros_like(token)

    sem_shapes = []
    for items in _AG_GROUPS:
        sem_shapes += [pltpu.SemaphoreType.DMA((3 * len(items),))] * 2
    thru = [pltpu.HBM(a.shape, a.dtype) for a in list(shards) + lands]
    out = pl.pallas_call(
        body, name="allgather_start",
        out_shape=tuple(sem_shapes) + tuple(thru) + (jax.ShapeDtypeStruct((8, 128), F32),),
        in_specs=(_HBM,) * (2 * n),
        out_specs=(_SEM,) * (2 * ng) + (_HBM,) * (2 * n) + (pl.BlockSpec(memory_space=pltpu.VMEM),),
        input_output_aliases={i: 2 * ng + i for i in range(2 * n)},
        compiler_params=pltpu.CompilerParams(**_SPLIT),
    )(*[_hbm(a) for a in list(shards) + lands])
    sems, thru, token = out[:2 * ng], out[2 * ng:-1], out[-1]
    return [(sems[2 * g], sems[2 * g + 1]) for g in range(ng)], list(thru[:n]), list(thru[n:]), token


def _ag_wait(g, sems, srcs, lands, after):
    items = _AG_GROUPS[g]
    m = len(items)

    def body(*refs):
        src_refs, land_refs = refs[:m], refs[m:2 * m]
        send_sems, recv_sems = refs[2 * m], refs[2 * m + 1]
        x, y, c = _position()
        for t, i in enumerate(items):
            for j, chip in enumerate(_other_chips(x, y)):
                cp = _ag_ici_copy(i, j, chip, c, 2 * chip[0] + chip[1], src_refs[t], land_refs[t], send_sems, recv_sems,
                                  3 * t + j)
                cp.wait_send()
                cp.wait_recv()

    ops = [srcs[i] for i in items] + [lands[i] for i in items]
    out = pl.pallas_call(
        body, name=f"allgather_wait_{g}",
        out_shape=tuple(pltpu.HBM(a.shape, a.dtype) for a in ops),
        in_specs=(_HBM,) * (2 * m) + (_SEM, _SEM, pl.BlockSpec(memory_space=pl.ANY)),
        out_specs=(_HBM,) * (2 * m),
        input_output_aliases={i: i for i in range(2 * m)},
        compiler_params=pltpu.CompilerParams(**_SPLIT),
    )(*ops, sems[0], sems[1], after)
    return list(out[:m]), list(out[m:])


def _ag_forward(g, srcs, lands):
    items = _AG_GROUPS[g]
    m = len(items)

    def body(*refs):
        src_refs, in_refs, out_refs = refs[:m], refs[m:2 * m], refs[2 * m:3 * m]
        send_sems, recv_sems = refs[3 * m:]
        x, y, c = _position()
        sibling = (x, y, 1 - c)
        me = 2 * x + y
        chips = _other_chips(x, y)
        sends = []
        for t, i in enumerate(items):
            _, dst, half = _AG_ITEMS[i]
            for j, chip in enumerate(chips):
                slot = 2 * chip[0] + chip[1]
                sends.append(pltpu.make_async_remote_copy(
                    src_ref=dst(in_refs[t], slot, c), dst_ref=dst(out_refs[t], slot, c), send_sem=send_sems.at[5 * t + j],
                    recv_sem=recv_sems.at[5 * t + j], device_id=sibling, device_id_type=MESH))
            for hc in range(2):
                sends.append(pltpu.make_async_remote_copy(
                    src_ref=half(src_refs[t], hc), dst_ref=dst(out_refs[t], me, hc), send_sem=send_sems.at[5 * t + 3 + hc],
                    recv_sem=recv_sems.at[5 * t + 3 + hc], device_id=sibling, device_id_type=MESH))
        for cp in sends:
            cp.start()
        for t, i in enumerate(items):
            _, dst, half = _AG_ITEMS[i]
            for j, chip in enumerate(chips):
                there = dst(out_refs[t], 2 * chip[0] + chip[1], 1 - c)
                pltpu.make_async_remote_copy(src_ref=there, dst_ref=there, send_sem=send_sems.at[5 * t + j],
                                             recv_sem=recv_sems.at[5 * t + j], device_id=sibling,
                                             device_id_type=MESH).wait_recv()
            for hc in range(2):
                there = dst(out_refs[t], me, hc)
                pltpu.make_async_remote_copy(src_ref=there, dst_ref=there, send_sem=send_sems.at[5 * t + 3 + hc],
                                             recv_sem=recv_sems.at[5 * t + 3 + hc], device_id=sibling,
                                             device_id_type=MESH).wait_recv()
        for cp in sends:
            cp.wait_send()

    any_spec = pl.BlockSpec(memory_space=pl.ANY)
    return pl.pallas_call(
        body, name=f"allgather_forward_{g}",
        in_specs=[any_spec] * (2 * m), out_specs=(any_spec,) * m,
        out_shape=tuple(jax.ShapeDtypeStruct(a.shape, a.dtype) for a in lands),
        input_output_aliases={m + t: t for t in range(m)},
        scratch_shapes=[pltpu.SemaphoreType.DMA((5 * m,)), pltpu.SemaphoreType.DMA((5 * m,))],
    )(*srcs, *lands)


def _pair_swap_copy(g_ref, r_ref, send_sem, recv_sem):
    x, y, c = _position()
    hc = g_ref.shape[2] // 2
    return pltpu.make_async_remote_copy(src_ref=g_ref.at[:, :, pl.ds(_al(1 - c, hc), hc)], dst_ref=r_ref,
                                        send_sem=send_sem, recv_sem=recv_sem, device_id=(x, y, 1 - c),
                                        device_id_type=MESH)


def _pair_swap_start(gb, tag):
    _, rows, cols = gb.shape
    recv = lax.empty((4, rows, cols // 2), gb.dtype)

    def body(g_ref, r_ref, send_sem, recv_sem, g_thru, r_thru, token):
        _pair_swap_copy(g_ref, r_ref, send_sem, recv_sem).start()
        token[...] = jnp.zeros_like(token)

    return pl.pallas_call(
        body, name="grad_pair_swap_start_" + tag,
        out_shape=(pltpu.SemaphoreType.DMA(()), pltpu.SemaphoreType.DMA(()), pltpu.HBM(gb.shape, gb.dtype),
                   pltpu.HBM(recv.shape, recv.dtype), jax.ShapeDtypeStruct((8, 128), F32)),
        in_specs=(_HBM, _HBM), out_specs=(_SEM, _SEM, _HBM, _HBM, pl.BlockSpec(memory_space=pltpu.VMEM)),
        input_output_aliases={0: 2, 1: 3},
        compiler_params=pltpu.CompilerParams(**_SPLIT),
    )(_hbm(gb), _hbm(recv))


def _pair_swap_wait(started, after, tag):
    send_sem, recv_sem, gb, recv, _ = started

    def body(g_ref, r_ref, send_sem, recv_sem, after_ref, g_out, r_out):
        cp = _pair_swap_copy(g_ref, r_ref, send_sem, recv_sem)
        cp.wait_send()
        cp.wait_recv()

    return pl.pallas_call(
        body, name="grad_pair_swap_wait_" + tag,
        out_shape=(pltpu.HBM(gb.shape, gb.dtype), pltpu.HBM(recv.shape, recv.dtype)),
        in_specs=(_HBM, _HBM, _SEM, _SEM, pl.BlockSpec(memory_space=pl.ANY)), out_specs=(_HBM, _HBM),
        input_output_aliases={0: 0, 1: 1},
        compiler_params=pltpu.CompilerParams(**_SPLIT),
    )(gb, recv, send_sem, recv_sem, after)


def _handover(red, tag):
    hc = red.shape[1] // 2

    def body(in_ref, out_ref, send_sem, recv_sem):
        x, y, c = _position()
        mine = pl.ds(_al(c, hc), hc)
        cp = pltpu.make_async_remote_copy(src_ref=in_ref.at[:, mine], dst_ref=out_ref.at[:, mine], send_sem=send_sem,
                                          recv_sem=recv_sem, device_id=(x, y, 1 - c), device_id_type=MESH)
        cp.start()
        theirs = out_ref.at[:, pl.ds(_al(1 - c, hc), hc)]
        pltpu.make_async_remote_copy(src_ref=theirs, dst_ref=theirs, send_sem=send_sem, recv_sem=recv_sem,
                                     device_id=(x, y, c), device_id_type=MESH).wait_recv()
        cp.wait_send()

    return pl.pallas_call(
        body, name="grad_handover_" + tag,
        in_specs=[pl.BlockSpec(memory_space=pl.ANY)], out_specs=pl.BlockSpec(memory_space=pl.ANY),
        out_shape=jax.ShapeDtypeStruct(red.shape, red.dtype), input_output_aliases={0: 0},
        scratch_shapes=[pltpu.SemaphoreType.DMA, pltpu.SemaphoreType.DMA],
    )(red)


def _a2a_copy(j, chip, c, p_ref, q_ref, q_slot, send_sems, recv_sems):
    return pltpu.make_async_remote_copy(src_ref=p_ref.at[2 * chip[0] + chip[1]], dst_ref=q_ref.at[q_slot],
                                        send_sem=send_sems.at[j], recv_sem=recv_sems.at[j], device_id=(*chip, c),
                                        device_id_type=MESH)


def _a2a_start(p, tag):
    def body(p_ref, q_ref, send_sems, recv_sems, p_thru, q_thru, token):
        x, y, c = _position()
        for j, chip in enumerate(_other_chips(x, y)):
            _a2a_copy(j, chip, c, p_ref, q_ref, 2 * x + y, send_sems, recv_sems).start()
        token[...] = jnp.zeros_like(token)

    return pl.pallas_call(
        body, name="grad_alltoall_start_" + tag,
        out_shape=(pltpu.SemaphoreType.DMA((3,)), pltpu.SemaphoreType.DMA((3,)), pltpu.HBM(p.shape, p.dtype),
                   pltpu.HBM(p.shape, p.dtype), jax.ShapeDtypeStruct((8, 128), F32)),
        in_specs=(_HBM, _HBM), out_specs=(_SEM, _SEM, _HBM, _HBM, pl.BlockSpec(memory_space=pltpu.VMEM)),
        input_output_aliases={0: 2, 1: 3},
        compiler_params=pltpu.CompilerParams(**_SPLIT),
    )(_hbm(p), _hbm(lax.empty(p.shape, p.dtype)))


def _a2a_wait(send_sems, recv_sems, p, q, after, tag):
    def body(p_ref, q_ref, send_sems, recv_sems, after_ref, p_out, q_out):
        x, y, c = _position()
        for j, chip in enumerate(_other_chips(x, y)):
            cp = _a2a_copy(j, chip, c, p_ref, q_ref, 2 * chip[0] + chip[1], send_sems, recv_sems)
            cp.wait_send()
            cp.wait_recv()

    return pl.pallas_call(
        body, name="grad_alltoall_wait_" + tag,
        out_shape=(pltpu.HBM(p.shape, p.dtype), pltpu.HBM(q.shape, q.dtype)),
        in_specs=(_HBM, _HBM, _SEM, _SEM, pl.BlockSpec(memory_space=pl.ANY)), out_specs=(_HBM, _HBM),
        input_output_aliases={0: 0, 1: 1},
        compiler_params=pltpu.CompilerParams(**_SPLIT),
    )(p, q, send_sems, recv_sems, after)


def _comm_rows(rows):
    return next(t for t in (512, 384, 256, 128) if rows % t == 0)


def _pair_add(gb, recv, where, tag):
    _, rows, cols = gb.shape
    hc = cols // 2
    tr = _comm_rows(rows)

    def body(w_ref, g_ref, r_ref, o_ref):
        o_ref[...] = (g_ref[...].astype(F32) + r_ref[...].astype(F32)).astype(o_ref.dtype)

    return pl.pallas_call(
        body, name="grad_pair_add_" + tag,
        grid_spec=pltpu.PrefetchScalarGridSpec(
            num_scalar_prefetch=1, grid=(4, rows // tr),
            in_specs=[pl.BlockSpec((None, tr, hc), lambda s, j, w_ref: (s, j, w_ref[0])),
                      pl.BlockSpec((None, tr, hc), lambda s, j, w_ref: (s, j, 0))],
            out_specs=pl.BlockSpec((None, tr, hc), lambda s, j, w_ref: (s, j, 0))),
        out_shape=jax.ShapeDtypeStruct((4, rows, hc), gb.dtype),
        compiler_params=_cp(("parallel", "parallel")),
    )(where, gb, recv)


def _sum_chips(p, q, where, tag):
    _, rows, hc = q.shape
    tr = _comm_rows(rows)

    def body(w_ref, p_ref, qa_ref, qb_ref, qc_ref, o_ref):
        me = w_ref[1]
        own, qa, qb, qc = (r[...].astype(F32) for r in (p_ref, qa_ref, qb_ref, qc_ref))
        v0 = jnp.where(me == 0, own, qa)
        v1 = jnp.where(me == 1, own, jnp.where(me == 0, qa, qb))
        v2 = jnp.where(me == 2, own, jnp.where(me < 2, qb, qc))
        v3 = jnp.where(me == 3, own, qc)
        o_ref[...] = ((v0 + v1) + v2) + v3

    slot = lambda k: pl.BlockSpec((None, tr, hc), lambda j, w_ref: (w_ref[k], j, 0))
    return pl.pallas_call(
        body, name="grad_sum_chips_" + tag,
        grid_spec=pltpu.PrefetchScalarGridSpec(
            num_scalar_prefetch=1, grid=(rows // tr,),
            in_specs=[slot(1), slot(2), slot(3), slot(4)],
            out_specs=pl.BlockSpec((tr, hc), lambda j, w_ref: (j, w_ref[0]))),
        out_shape=jax.ShapeDtypeStruct((rows, 2 * hc), F32),
        compiler_params=_cp(("parallel",)),
    )(where, p, q, q, q)


def _shard_major(g, axis):
    shape = g.shape
    g = g.reshape(shape[:axis] + (4, shape[axis] // 4) + shape[axis + 1:])
    return jnp.moveaxis(g, axis, 0).reshape(4, -1)


def _unshard(g4, shape, axis):
    n = shape[axis] // 4
    g = g4.reshape((4,) + shape[:axis] + (n,) + shape[axis + 1:])
    return jnp.moveaxis(g, 0, axis).reshape(shape)


def _split(flat, shapes):
    out, off = [], 0
    for shp in shapes:
        n = 1
        for d in shp:
            n *= d
        out.append(flat[..., off:off + n].reshape(flat.shape[:-1] + tuple(shp)))
        off += n
    return out


def _even_rows_to_kernel(wt):
    return jnp.concatenate([wt[:1536], wt[1552:3088], wt[1536:1552], wt[3088:3096],
                            jnp.zeros((PE - 3096, wt.shape[1]), wt.dtype)], axis=0)


def _block_diag(w):
    eye = jnp.eye(8, dtype=w.dtype)
    return (w[:, :, None, :] * eye[:, None, :, None]).reshape(512, 512)


def _diag_blocks(g):
    eye = jnp.eye(8, dtype=g.dtype)
    return (g.reshape(8, 64, 8, 64) * eye[:, None, :, None]).sum(axis=2)


def _shift_down(a, s):
    return a if s == 0 else jnp.pad(a, ((s, 0), (0, 0)))[:a.shape[0]]


def _shift_up(a, s):
    return a if s == 0 else jnp.pad(a, ((0, s), (0, 0)))[s:]


SMALL_SHARDED_SHAPES = [(2, 4, 256), (16, 64), (4, 128), (128,), (128,), (128,), (128,)]
REPL_SHAPES = [(256,), (512,), (8,), (8, 257), (8, 64, 64), (8, 64, 64)]


def kernel(x, norm_w, w_in_even, gla_w_a_up, gla_b_a, gla_norm_w, fox_b_f, w_out_even, w_in_odd, rel_bias, conv_w, conv_b, lru_w_a, lru_b_a, lru_w_x, lru_b_x, lru_lambda, w_out_odd, w_mlp_up, w_mlp_down, loss_target, m_norm_w, m_w_in_even, m_gla_w_a_up, m_gla_b_a, m_gla_norm_w, m_fox_b_f, m_w_out_even, m_w_in_odd, m_rel_bias, m_conv_w, m_conv_b, m_lru_w_a, m_lru_b_a, m_lru_w_x, m_lru_b_x, m_lru_lambda, m_w_out_odd, m_w_mlp_up, m_w_mlp_down, v_norm_w, v_w_in_even, v_gla_w_a_up, v_gla_b_a, v_gla_norm_w, v_fox_b_f, v_w_out_even, v_w_in_odd, v_rel_bias, v_conv_w, v_conv_b, v_lru_w_a, v_lru_b_a, v_lru_w_x, v_lru_b_x, v_lru_lambda, v_w_out_odd, v_w_mlp_up, v_w_mlp_down):
    c_idx = lax.axis_index("c")

    small_local = [norm_w, gla_w_a_up[0], conv_w[0], conv_b[0], lru_b_a[0], lru_b_x[0], lru_lambda[0]]
    small_src = jnp.concatenate([a.reshape(-1) for a in small_local]).reshape(32, 128)
    mine = [small_src, w_in_even[0].T.astype(BF16), w_out_even[0].astype(BF16), w_mlp_up.astype(BF16),
            w_mlp_down.astype(BF16), w_in_odd[0].astype(BF16), w_out_odd[0].astype(BF16)]
    ag_sems, ag_srcs, ag_lands, ag_token = _ag_start(mine)

    def gathered(g, after):
        srcs_g, lands_g = _ag_wait(g, ag_sems[g], ag_srcs, ag_lands, after)
        return _ag_forward(g, srcs_g, lands_g)

    small4, w_in_e4, w_out_e = gathered(0, ag_token)
    me = 2 * lax.axis_index("x") + lax.axis_index("y")
    others = [k + (k >= me).astype(jnp.int32) for k in range(3)]
    where = jnp.stack([c_idx, me] + others).astype(jnp.int32)

    w_in_e_t = _even_rows_to_kernel(w_in_e4.reshape(3096, D))
    g_small = _split(small4.reshape(4, 32 * 128), SMALL_SHARDED_SHAPES)
    nw_full = _unshard(g_small[0], (2, 4, 1024), 2)
    wa_up = _unshard(g_small[1], (16, 256), 1)
    cw = _unshard(g_small[2], (4, 512), 1)
    cb, lba, lbx, lam = [_unshard(g, (512,), 0).reshape(1, 512) for g in g_small[3:]]
    nw = lambda layer, i: nw_full[layer, i].reshape(1, D)

    wa_pad = jnp.pad(wa_up, ((0, 128 - 16), (0, 0)))
    gla_ba = gla_b_a.reshape(1, 256)
    gla_nw = gla_norm_w.reshape(1, 512)
    fox_bpad = jnp.pad(fox_b_f.reshape(1, 8), ((0, 0), (FOX_LANE0, 128 - FOX_LANE0 - 8)))
    rbp = jnp.pad(rel_bias[0], ((0, 0), (0, REL_PAD - 257)))
    wa_bd = _block_diag(lru_w_a[0])
    wx_bd = _block_diag(lru_w_x[0])

    x0 = x[0]
    tgt = loss_target[0]

    h0 = _prenorm(x0, nw(0, 0), "prenorm_l0_mix")
    proj_e = _mm(h0, w_in_e_t, "nt", tm=1024, tn=640, name="mm_in_even")
    cat0, s_prev = _gla_fwd(proj_e, wa_pad, gla_ba, gla_nw)
    cum_r = _fox_gate_fwd(proj_e, fox_bpad)
    cum_c = cum_r[:, FOX_LANE0:FOX_LANE0 + 8].T
    cat0 = _fox_fwd(proj_e, cum_c, cat0)
    mix0 = _mm(cat0, w_out_e, "nn", tm=1024, tn=512, name="mm_out_even")
    x1 = _postnorm(x0, mix0, nw(0, 1), "postnorm_l0_mix")
    w_up, w_dn = gathered(1, x1)
    h1 = _prenorm(x1, nw(0, 2), "prenorm_l0_mlp")
    a0, r0 = _mm(h1, w_up, "nn", tm=1024, tn=1024, b_layer=0, relu_pair=True, name="mm_up_l0")
    d0 = _mm(a0, w_dn, "nn", tm=1024, tn=512, b_layer=0, name="mm_down_l0")
    x2 = _postnorm(x1, d0, nw(0, 3), "postnorm_l0_mlp")

    w_in_o, w_out_o = gathered(2, x2)
    h2 = _prenorm(x2, nw(1, 0), "prenorm_l1_mix")
    proj_o = _mm(h2, w_in_o, "nn", tm=1024, tn=640, name="mm_in_odd")
    bias_q = _bias_build(rbp)
    bias = bias_q.transpose(1, 0, 2)
    kvpad = jnp.pad(proj_o[:, 512:1536], ((CA_PAD, 0), (0, 0)))
    cat1 = _ca_fwd(proj_o, kvpad, bias)
    x_in = proj_o[:, 2048:2560]
    xs = jnp.stack([_shift_down(x_in, 3 - j) for j in range(4)])
    lru_a, lru_b = _lru_pre_fwd(xs, cw, cb, wa_bd, lba, wx_bd, lbx, lam)
    hh = _lru_scan_fwd(lru_a, lru_b)
    cat1 = _lru_post_fwd(hh, proj_o, cat1)
    mix1 = _mm(cat1, w_out_o, "nn", tm=1024, tn=512, name="mm_out_odd")
    x3 = _postnorm(x2, mix1, nw(1, 1), "postnorm_l1_mix")
    h3 = _prenorm(x3, nw(1, 2), "prenorm_l1_mlp")
    a1, r1 = _mm(h3, w_up, "nn", tm=1024, tn=1024, b_layer=1, relu_pair=True, name="mm_up_l1")
    d1 = _mm(a1, w_dn, "nn", tm=1024, tn=512, b_layer=1, name="mm_down_l1")
    x4 = _postnorm(x3, d1, nw(1, 3), "postnorm_l1_mlp")

    g4, loss_part = _loss_and_grad(x4, tgt)
    loss = lax.psum(loss_part[0, 0], ("x", "y", "c"))

    def rs_begin(swap, after, tag):
        gb, recv = _pair_swap_wait(swap, after, tag)
        return _a2a_start(_pair_add(gb, recv, where, tag), tag)

    def rs_end(started, after, tag):
        send_sems, recv_sems, p, q, _ = started
        p, q = _a2a_wait(send_sems, recv_sems, p, q, after, tag)
        return _handover(_sum_chips(p, q, where, tag), tag)

    gba = lax.dynamic_update_slice(lax.empty((4, GA_ROWS, D), BF16), jnp.zeros((4, GA_ROWS - GA_USED, D), BF16),
                                   (0, GA_USED, 0))
    dd1, dnw13 = _norm_bwd(d1, nw(1, 3), g4, None, "postnorm_l1_mlp_bwd")
    gba = _mm(a1, dd1, "tn", tm=512, tn=1024, into=(gba, 1024, GA_DN), name="mm_down_l1_dw")
    du1 = _mm(dd1, w_dn, "nt", tm=1024, tn=1024, b_layer=1, times2=r1, out_dtype=BF16, name="mm_down_l1_dx")
    gba = _mm(du1, h3, "tn", tm=512, tn=1024, into=(gba, 1024, GA_UP), name="mm_up_l1_dw")
    dh3 = _mm(du1, w_up, "nt", tm=1024, tn=512, b_layer=1, name="mm_up_l1_dx")
    g3, dnw12 = _norm_bwd(x3, nw(1, 2), dh3, g4, "prenorm_l1_mlp_bwd")
    dmix1, dnw11 = _norm_bwd(mix1, nw(1, 1), g3, None, "postnorm_l1_mix_bwd")
    gba = _mm(cat1, dmix1, "tn", tm=128, tn=1024, into=(gba, 256, GA_OUT_O), name="mm_out_odd_dw")
    dcat1 = _mm(dmix1, w_out_o, "nt", tm=1024, tn=512, name="mm_out_odd_dx")

    dq_c, dkpad, dvpad, dbias = _ca_bwd(proj_o, kvpad, bias, dcat1)
    g_rel = _bias_grad(jnp.pad(dbias.transpose(1, 0, 2), ((0, 0), (0, 0), (0, BIAS_W - CA_BAND))))[:, :257]
    dhh, dgate = _lru_post_bwd(hh, proj_o, dcat1)
    da_l, db_l = _lru_scan_bwd(lru_a, hh, dhh)
    dxs, g_cw, g_cb, g_wa_bd, g_lba, g_wx_bd, g_lbx, g_lam = _lru_pre_bwd(xs, cw, cb, wa_bd, lba, wx_bd, lbx, lam, da_l, db_l)
    dx_in = _conv_dx(jnp.stack([_shift_up(dxs[j], 3 - j) for j in range(4)]))
    dproj_o = jnp.concatenate([dq_c, dkpad[CA_PAD:], dvpad[CA_PAD:], dgate, dx_in], axis=1).astype(BF16)
    gba = _mm(dproj_o, h2, "tn", tm=128, tn=1024, into=(gba, 640, GA_IN_O), name="mm_in_odd_dw")
    swap_a = _pair_swap_start(gba, "a")
    dh2 = _mm(dproj_o, w_in_o, "nt", tm=1024, tn=512, name="mm_in_odd_dx")
    g2, dnw10 = _norm_bwd(x2, nw(1, 0) + swap_a[4][0, 0], dh2, g3, "prenorm_l1_mix_bwd")
    rs_a = rs_begin(swap_a, g2, "a")

    gbb = lax.empty((4, GB_ROWS, D), BF16)
    dd0, dnw03 = _norm_bwd(d0, nw(0, 3) + rs_a[4][0, 0], g2, None, "postnorm_l0_mlp_bwd")
    gbb = _mm(a0, dd0, "tn", tm=512, tn=1024, into=(gbb, 1024, GB_DN), name="mm_down_l0_dw")
    du0 = _mm(dd0, w_dn, "nt", tm=1024, tn=1024, b_layer=0, times2=r0, out_dtype=BF16, name="mm_down_l0_dx")
    gbb = _mm(du0, h1, "tn", tm=512, tn=1024, into=(gbb, 1024, GB_UP), name="mm_up_l0_dw")
    swap_b = _pair_swap_start(gbb, "b")
    dh1 = _mm(du0, w_up, "nt", tm=1024, tn=512, b_layer=0, name="mm_up_l0_dx")
    g1, dnw02 = _norm_bwd(x1, nw(0, 2) + swap_b[4][0, 0], dh1, g2, "prenorm_l0_mlp_bwd")
    rs_b = rs_begin(swap_b, g1, "b")
    dmix0, dnw01 = _norm_bwd(mix0, nw(0, 1) + rs_b[4][0, 0], g1, None, "postnorm_l0_mix_bwd")
    gbc = lax.empty((4, GC_ROWS, D), BF16)
    gbc = _mm(cat0, dmix0, "tn", tm=128, tn=1024, into=(gbc, 256, GC_OUT_E), name="mm_out_even_dw")
    dcat0 = _mm(dmix0, w_out_e, "nt", tm=1024, tn=512, name="mm_out_even_dx")

    dq_g, dk_g, dv_g, dr_g, daux_g, g_wa_pad, g_gla_ba, g_gla_nw = _gla_bwd(proj_e, s_prev, wa_pad, gla_ba, gla_nw, dcat0)
    dq_f, dk_f, dv_f, dccol = _fox_bwd(proj_e, cum_c, dcat0)
    dccol_t = jnp.pad(dccol.sum(axis=0).T, ((0, 0), (FOX_LANE0, 128 - FOX_LANE0 - 8)))
    daux, g_fox_bpad = _fox_gate_bwd(proj_e, fox_bpad, dccol_t, daux_g)
    dproj_e = jnp.concatenate([dq_g, dk_g, dv_g, dr_g, dq_f, dk_f, dv_f, daux], axis=1).astype(BF16)
    gt_in_e = _mm(dproj_e, h0, "tn", tm=640, tn=1024, out_dtype=BF16, name="mm_in_even_dw")
    dh0 = _mm(dproj_e, w_in_e_t, "nn", tm=1024, tn=512, name="mm_in_even_dx")
    grad_x, dnw00 = _norm_bwd(x0, nw(0, 0), dh0, g1, "prenorm_l0_mix_bwd")

    g_norm = jnp.stack([jnp.concatenate([dnw00, dnw01, dnw02, dnw03]), jnp.concatenate([dnw10, dnw11, dnw12, dnw13])])
    sharded = [(g_norm, 2), (g_wa_pad[:16], 1), (g_cw, 1), (g_cb[0], 0), (g_lba[0], 0), (g_lbx[0], 0), (g_lam[0], 0)]
    replicated = [g_gla_ba[0], g_gla_nw[0], g_fox_bpad[0, FOX_LANE0:FOX_LANE0 + 8], g_rel, _diag_blocks(g_wa_bd),
                  _diag_blocks(g_wx_bd)]
    small4 = jnp.concatenate([_shard_major(g, ax) for g, ax in sharded]
                             + [jnp.broadcast_to(g.reshape(1, -1), (4, g.size)) for g in replicated], axis=1)
    n_small = small4.shape[1]
    small_rows = GC_ROWS - GC_TAIL - 774
    small4 = jnp.pad(small4, ((0, 0), (0, small_rows * D - n_small))).reshape(4, small_rows, D)
    gt_rows = jnp.concatenate([gt_in_e[:1536], gt_in_e[3072:3088], gt_in_e[1536:3072], gt_in_e[3088:3096]], axis=0)
    tail = jnp.concatenate([gt_rows.reshape(4, 774, D), small4.astype(BF16)], axis=1)
    gbc = lax.dynamic_update_slice(gbc, tail, (0, GC_TAIL, 0))
    swap_c = _pair_swap_start(gbc, "c")
    rs_c = rs_begin(swap_c, swap_c[4], "c")

    red_a = rs_end(rs_a, rs_c[4], "a")
    red_b = rs_end(rs_b, red_a, "b")
    red_c = rs_end(rs_c, red_b, "c")

    g_up = jnp.stack([red_b[GB_UP:GB_UP + 1024].T, red_a[GA_UP:GA_UP + 1024].T])
    g_dn = jnp.stack([red_b[GB_DN:GB_DN + 1024], red_a[GA_DN:GA_DN + 1024]])
    g_small = _split(red_c[GC_TAIL + 774:].reshape(-1)[:n_small], SMALL_SHARDED_SHAPES + REPL_SHAPES)
    g_of = dict(zip(["norm_w", "gla_w_a_up", "conv_w", "conv_b", "lru_b_a", "lru_b_x", "lru_lambda", "gla_b_a",
                     "gla_norm_w", "fox_b_f", "rel_bias", "lru_w_a", "lru_w_x"], g_small))
    g_of.update(w_mlp_up=g_up, w_mlp_down=g_dn, w_in_odd=red_a[GA_IN_O:GA_IN_O + 640].T,
                w_out_even=red_c[GC_OUT_E:GC_OUT_E + 256], w_out_odd=red_a[GA_OUT_O:GA_OUT_O + 256],
                w_in_even=red_c[GC_TAIL:GC_TAIL + 774])

    names = ["norm_w", "w_in_even", "gla_w_a_up", "gla_b_a", "gla_norm_w", "fox_b_f", "w_out_even", "w_in_odd", "rel_bias",
             "conv_w", "conv_b", "lru_w_a", "lru_b_a", "lru_w_x", "lru_b_x", "lru_lambda", "w_out_odd", "w_mlp_up",
             "w_mlp_down"]
    w_of = dict(norm_w=norm_w, w_in_even=w_in_even, gla_w_a_up=gla_w_a_up, gla_b_a=gla_b_a, gla_norm_w=gla_norm_w,
                fox_b_f=fox_b_f, w_out_even=w_out_even, w_in_odd=w_in_odd, rel_bias=rel_bias, conv_w=conv_w, conv_b=conv_b,
                lru_w_a=lru_w_a, lru_b_a=lru_b_a, lru_w_x=lru_w_x, lru_b_x=lru_b_x, lru_lambda=lru_lambda,
                w_out_odd=w_out_odd, w_mlp_up=w_mlp_up, w_mlp_down=w_mlp_down)
    m_of = dict(norm_w=m_norm_w, w_in_even=m_w_in_even, gla_w_a_up=m_gla_w_a_up, gla_b_a=m_gla_b_a,
                gla_norm_w=m_gla_norm_w, fox_b_f=m_fox_b_f, w_out_even=m_w_out_even, w_in_odd=m_w_in_odd,
                rel_bias=m_rel_bias, conv_w=m_conv_w, conv_b=m_conv_b, lru_w_a=m_lru_w_a, lru_b_a=m_lru_b_a,
                lru_w_x=m_lru_w_x, lru_b_x=m_lru_b_x, lru_lambda=m_lru_lambda, w_out_odd=m_w_out_odd,
                w_mlp_up=m_w_mlp_up, w_mlp_down=m_w_mlp_down)
    v_of = dict(norm_w=v_norm_w, w_in_even=v_w_in_even, gla_w_a_up=v_gla_w_a_up, gla_b_a=v_gla_b_a,
                gla_norm_w=v_gla_norm_w, fox_b_f=v_fox_b_f, w_out_even=v_w_out_even, w_in_odd=v_w_in_odd,
                rel_bias=v_rel_bias, conv_w=v_conv_w, conv_b=v_conv_b, lru_w_a=v_lru_w_a, lru_b_a=v_lru_b_a,
                lru_w_x=v_lru_w_x, lru_b_x=v_lru_b_x, lru_lambda=v_lru_lambda, w_out_odd=v_w_out_odd,
                w_mlp_up=v_w_mlp_up, w_mlp_down=v_w_mlp_down)
    grads, deltas, new_ms, new_vs = [], [], [], []
    for n in names:
        w = w_of[n]
        if n == "w_in_even":
            to_view = lambda a: a[0].T
            from_view = lambda a: a.T[None]
        else:
            view = w.shape if w.ndim <= 3 else w.shape[-3:]
            to_view = lambda a, view=view: a.reshape(view)
            from_view = lambda a, w=w: a.reshape(w.shape)
        g = g_of[n] if n == "w_in_even" else to_view(g_of[n])
        d, mn, vn = _adamw(to_view(w), g, to_view(m_of[n]), to_view(v_of[n]), "adamw_" + n)
        grads.append(from_view(g))
        deltas.append(from_view(d))
        new_ms.append(from_view(mn))
        new_vs.append(from_view(vn))

    return (loss, grad_x.reshape(1, T, D), *grads, *deltas, *new_ms, *new_vs)
```

```python
import functools

import jax
import jax.numpy as jnp
from jax import lax
from jax.experimental import pallas as pl
from jax.experimental.pallas import tpu as pltpu

F32 = jnp.float32
BF16 = jnp.bfloat16
MESH = pl.DeviceIdType.MESH

T = 2048
D = 1024
DFF = 4096
EPS = 1e-6
CHUNK = 64
NCHUNK = T // CHUNK
PE = 3200
PO = 2560
AUX_BLK = 3072 // 128
FOX_LANE0 = 16
GLA_SCALE = 64 ** -0.5
ATT_SCALE = 64 ** -0.5
NEG = float(jnp.finfo(jnp.float32).min)
CA_BAND = 576
CA_PAD = 512
REL_PAD = 384

VMEM_LIMIT = 48 * 1024 * 1024

ADAM_LR, ADAM_B1, ADAM_B2, ADAM_EPS, ADAM_WD, ADAM_STEP = 0.001, 0.9, 0.999, 1e-08, 0.01, 10

GA_ROWS, GA_IN_O, GA_OUT_O, GA_GAP, GA_UP, GA_DN = 3072, 0, 640, 896, 1024, 2048
GB_ROWS, GB_UP, GB_DN = 2048, 0, 1024
GC_ROWS, GC_OUT_E, GC_TAIL = 1152, 0, 256

_DIMS = {"nn": (((1,), (0,)), ((), ())), "nt": (((1,), (1,)), ((), ())), "tn": (((0,), (0,)), ((), ()))}


def _cp(sem, **kw):
    return pltpu.CompilerParams(dimension_semantics=sem, vmem_limit_bytes=VMEM_LIMIT, **kw)


def _dot(a, b, mode):
    return lax.dot_general(a.astype(BF16), b.astype(BF16), _DIMS[mode], preferred_element_type=F32)


@functools.partial(jax.custom_vjp, nondiff_argnums=(2,))
def bdot(a, b, mode):
    return _dot(a, b, mode)


def _bdot_fwd(a, b, mode):
    return _dot(a, b, mode), (a, b)


def _bdot_bwd(mode, res, g):
    a, b = res
    if mode == "nn":
        da, db = _dot(g, b, "nt"), _dot(a, g, "tn")
    elif mode == "nt":
        da, db = _dot(g, b, "nn"), _dot(g, a, "tn")
    else:
        da, db = _dot(b, g, "nt"), _dot(a, g, "nn")
    return da.astype(a.dtype), db.astype(b.dtype)


bdot.defvjp(_bdot_fwd, _bdot_bwd)


def _hdot_raw(a, b, mode):
    return lax.dot_general(a, b, _DIMS[mode], precision=lax.Precision.HIGHEST, preferred_element_type=F32)


@functools.partial(jax.custom_vjp, nondiff_argnums=(2,))
def hdot(a, b, mode):
    return _hdot_raw(a, b, mode)


def _hdot_fwd(a, b, mode):
    return _hdot_raw(a, b, mode), (a, b)


def _hdot_bwd(mode, res, g):
    a, b = res
    if mode == "nn":
        return _hdot_raw(g, b, "nt"), _hdot_raw(a, g, "tn")
    if mode == "nt":
        return _hdot_raw(g, b, "nn"), _hdot_raw(g, a, "tn")
    return _hdot_raw(b, g, "nt"), _hdot_raw(a, g, "nn")


hdot.defvjp(_hdot_fwd, _hdot_bwd)


def _log_sigmoid(x):
    return jnp.minimum(x, 0.0) - jnp.log(1.0 + jnp.exp(-jnp.abs(x)))


def _sigmoid(x):
    return 1.0 / (1.0 + jnp.exp(-x))


def _expm1(x):
    series = x * (1.0 + x * 0.5 * (1.0 + x * (1.0 / 3.0) * (1.0 + x * 0.25)))
    return jnp.where(jnp.abs(x) < 0.03, series, jnp.exp(x) - 1.0)


def _gelu_tanh(x):
    return 0.5 * x * (1.0 + jnp.tanh(0.7978845608028654 * (x + 0.044715 * x * x * x)))


def _softmax_rows(s):
    m = jnp.max(s, axis=-1, keepdims=True)
    p = jnp.exp(s - m)
    return p / jnp.sum(p, axis=-1, keepdims=True)


def _iota(shape, dim):
    return lax.broadcasted_iota(jnp.int32, shape, dim)


def _mm(a, b, mode, *, tm, tn, tk=None, out_dtype=F32, name, b_layer=None, into=None, relu_pair=False, times2=None):
    b2 = b.shape[-2:]
    if mode == "nn":
        (m, k), n = a.shape, b2[1]
    elif mode == "nt":
        (m, k), n = a.shape, b2[0]
    else:
        (k, m), n = a.shape, b2[1]
    tk = k if tk is None else tk
    assert m % tm == 0 and n % tn == 0 and k % tk == 0, (name, a.shape, b.shape)
    nk = k // tk
    a_spec = {"nn": pl.BlockSpec((tm, tk), lambda i, j, kk: (i, kk)),
              "nt": pl.BlockSpec((tm, tk), lambda i, j, kk: (i, kk)),
              "tn": pl.BlockSpec((tk, tm), lambda i, j, kk: (kk, i))}[mode]
    b_blk = {"nn": (tk, tn), "nt": (tn, tk), "tn": (tk, tn)}[mode]
    b_idx = {"nn": lambda i, j, kk: (kk, j), "nt": lambda i, j, kk: (j, kk), "tn": lambda i, j, kk: (kk, j)}[mode]
    if b_layer is None:
        b_spec = pl.BlockSpec(b_blk, b_idx)
    else:
        b_spec = pl.BlockSpec((None,) + b_blk, lambda i, j, kk: (b_layer,) + b_idx(i, j, kk))

    tile = pl.BlockSpec((tm, tn), lambda i, j, kk: (i, j))
    if into is not None:
        buf, per_slot, row_off = into
        assert m == 4 * per_slot and per_slot % tm == 0 and row_off % tm == 0 and buf.shape[2] == n, (name, buf.shape)
        bps = per_slot // tm
        out_specs = pl.BlockSpec((None, tm, tn), lambda i, j, kk: (i // bps, row_off // tm + i % bps, j))
        out_shape = jax.ShapeDtypeStruct(buf.shape, buf.dtype)
        extra_in, extra_specs, aliases = [buf], [pl.BlockSpec(memory_space=pl.ANY)], {2: 0}
        finish = lambda acc, extra: [acc.astype(buf.dtype)]
    elif relu_pair:
        out_specs = (tile, tile)
        out_shape = (jax.ShapeDtypeStruct((m, n), BF16),) * 2
        extra_in, extra_specs, aliases = [], [], {}

        def finish(acc, extra):
            r = jnp.maximum(acc, 0.0)
            return [(r * r).astype(BF16), r.astype(BF16)]
    elif times2 is not None:
        out_specs = tile
        out_shape = jax.ShapeDtypeStruct((m, n), out_dtype)
        extra_in, extra_specs, aliases = [times2], [tile], {}
        finish = lambda acc, extra: [(acc * (2.0 * extra[...].astype(F32))).astype(out_dtype)]
    else:
        out_specs = tile
        out_shape = jax.ShapeDtypeStruct((m, n), out_dtype)
        extra_in, extra_specs, aliases = [], [], {}
        finish = lambda acc, extra: [acc.astype(out_dtype)]
    n_out = 2 if relu_pair else 1

    def body(*refs):
        a_ref, b_ref = refs[0], refs[1]
        extra = refs[2] if extra_in else None
        o_refs = refs[2 + len(extra_in):2 + len(extra_in) + n_out]

        def store(acc):
            for o_ref, val in zip(o_refs, finish(acc, extra)):
                o_ref[...] = val

        if nk == 1:
            store(_dot(a_ref[...], b_ref[...], mode))
            return
        acc_ref = refs[-1]
        kk = pl.program_id(2)

        @pl.when(kk == 0)
        def _():
            acc_ref[...] = jnp.zeros_like(acc_ref)

        acc_ref[...] += _dot(a_ref[...], b_ref[...], mode)

        @pl.when(kk == nk - 1)
        def _():
            store(acc_ref[...])

    return pl.pallas_call(
        body, name=name, grid=(m // tm, n // tn, nk),
        in_specs=[a_spec, b_spec] + extra_specs,
        out_specs=out_specs, out_shape=out_shape,
        scratch_shapes=[pltpu.VMEM((tm, tn), F32)] if nk > 1 else [],
        input_output_aliases=aliases,
        compiler_params=_cp(("parallel", "parallel", "arbitrary")),
    )(a, b, *extra_in)


ROWS = 256


def _prenorm(x, w, name):
    def body(x_ref, w_ref, o_ref):
        xv = x_ref[...]
        r = lax.rsqrt(jnp.mean(xv * xv, axis=-1, keepdims=True) + EPS)
        o_ref[...] = (xv * r * w_ref[...]).astype(BF16)

    return pl.pallas_call(
        body, name=name, grid=(T // ROWS,),
        in_specs=[pl.BlockSpec((ROWS, D), lambda i: (i, 0)), pl.BlockSpec((1, D), lambda i: (0, 0))],
        out_specs=pl.BlockSpec((ROWS, D), lambda i: (i, 0)),
        out_shape=jax.ShapeDtypeStruct((T, D), BF16),
        compiler_params=_cp(("parallel",)),
    )(x, w)


def _postnorm(x, z, w, name):
    def body(x_ref, z_ref, w_ref, o_ref):
        zv = z_ref[...]
        r = lax.rsqrt(jnp.mean(zv * zv, axis=-1, keepdims=True) + EPS)
        o_ref[...] = x_ref[...] + zv * r * w_ref[...]

    return pl.pallas_call(
        body, name=name, grid=(T // ROWS,),
        in_specs=[pl.BlockSpec((ROWS, D), lambda i: (i, 0)), pl.BlockSpec((ROWS, D), lambda i: (i, 0)),
                  pl.BlockSpec((1, D), lambda i: (0, 0))],
        out_specs=pl.BlockSpec((ROWS, D), lambda i: (i, 0)),
        out_shape=jax.ShapeDtypeStruct((T, D), F32),
        compiler_params=_cp(("parallel",)),
    )(x, z, w)


def _norm_bwd(z, w, dy, add, name):
    has_add = add is not None

    def body(*refs):
        if has_add:
            z_ref, w_ref, dy_ref, add_ref, dz_ref, dw_ref = refs
        else:
            z_ref, w_ref, dy_ref, dz_ref, dw_ref = refs
        i = pl.program_id(0)

        @pl.when(i == 0)
        def _():
            dw_ref[...] = jnp.zeros_like(dw_ref)

        zv = z_ref[...].astype(F32)
        dyv = dy_ref[...]
        r = lax.rsqrt(jnp.mean(zv * zv, axis=-1, keepdims=True) + EPS)
        wdy = dyv * w_ref[...]
        dz = r * wdy - zv * (r * r * r) * jnp.mean(zv * wdy, axis=-1, keepdims=True)
        if has_add:
            dz = dz + add_ref[...]
        dz_ref[...] = dz.astype(dz_ref.dtype)
        dw_ref[...] += jnp.sum(dyv * zv * r, axis=0, keepdims=True)

    row = pl.BlockSpec((ROWS, D), lambda i: (i, 0))
    vec = pl.BlockSpec((1, D), lambda i: (0, 0))
    ins = [z, w, dy] + ([add] if has_add else [])
    dz_dtype = F32 if has_add else BF16
    return pl.pallas_call(
        body, name=name, grid=(T // ROWS,),
        in_specs=[row, vec, row] + ([row] if has_add else []),
        out_specs=(row, vec),
        out_shape=(jax.ShapeDtypeStruct((T, D), dz_dtype), jax.ShapeDtypeStruct((1, D), F32)),
        compiler_params=_cp(("arbitrary",)),
    )(*ins)


def _loss_and_grad(y, tgt):
    def body(y_ref, t_ref, g_ref, l_ref):
        i = pl.program_id(0)

        @pl.when(i == 0)
        def _():
            l_ref[...] = jnp.zeros_like(l_ref)

        e = y_ref[...] - t_ref[...]
        g_ref[...] = e * (1.0 / D)
        l_ref[...] += jnp.sum(e * e) * (0.5 / D)

    row = pl.BlockSpec((ROWS, D), lambda i: (i, 0))
    return pl.pallas_call(
        body, name="loss_head", grid=(T // ROWS,), in_specs=[row, row],
        out_specs=(row, pl.BlockSpec((1, 128), lambda i: (0, 0))),
        out_shape=(jax.ShapeDtypeStruct((T, D), F32), jax.ShapeDtypeStruct((1, 128), F32)),
        compiler_params=_cp(("arbitrary",)),
    )(y, tgt)


def _adamw_math(w, g, m, v):
    c1 = 1.0 - ADAM_B1 ** ADAM_STEP
    c2 = 1.0 - ADAM_B2 ** ADAM_STEP
    mn = ADAM_B1 * m + (1.0 - ADAM_B1) * g
    vn = ADAM_B2 * v + (1.0 - ADAM_B2) * (g * g)
    return -ADAM_LR * ((mn / c1) / (jnp.sqrt(vn / c2) + ADAM_EPS) + ADAM_WD * w), mn, vn


def _adamw_from(w, m, v, sources, tr, name):
    layers, rows, cols = w.shape
    assert len(sources) == layers and rows % tr == 0, (name, w.shape)
    g_specs = []
    for buf, row0, transposed in sources:
        if transposed:
            assert row0 % cols == 0 and buf.shape[1] == rows, (name, row0)
            g_specs.append(pl.BlockSpec((cols, tr), lambda l, i, b=row0 // cols: (b, i)))
        else:
            assert row0 % tr == 0 and buf.shape[1] == cols, (name, row0)
            g_specs.append(pl.BlockSpec((tr, cols), lambda l, i, b=row0 // tr: (b + i, 0)))

    def body(*refs):
        w_ref, m_ref, v_ref = refs[:3]
        g_refs = refs[3:3 + layers]
        g_out, d_ref, mo_ref, vo_ref = refs[3 + layers:]
        gs = [r[...].T if src[2] else r[...] for r, src in zip(g_refs, sources)]
        g = gs[0] if layers == 1 else jnp.where(pl.program_id(0) == 0, gs[0], gs[1])
        g_out[...] = g
        d_ref[...], mo_ref[...], vo_ref[...] = _adamw_math(w_ref[...], g, m_ref[...], v_ref[...])

    blk = pl.BlockSpec((None, tr, cols), lambda l, i: (l, i, 0))
    sds = jax.ShapeDtypeStruct(w.shape, F32)
    return pl.pallas_call(body, name=name, grid=(layers, rows // tr), in_specs=[blk] * 3 + g_specs,
                          out_specs=(blk,) * 4, out_shape=(sds,) * 4,
                          compiler_params=_cp(("parallel", "parallel")))(w, m, v, *[s[0] for s in sources])


def _adamw(w, g, m, v, name):
    lead = w.shape[:-2]
    assert len(lead) <= 1 and g.shape == w.shape, (name, w.shape, g.shape)
    rows, cols = w.shape[-2:]
    if rows <= 512:
        tr, tc = rows, cols
    elif rows % 256 == 0:
        tr, tc = 256, cols
    else:
        tr, tc = rows, 256
    assert rows % tr == 0 and cols % tc == 0, (name, w.shape)
    c1 = 1.0 - ADAM_B1 ** ADAM_STEP
    c2 = 1.0 - ADAM_B2 ** ADAM_STEP

    def body(w_ref, g_ref, m_ref, v_ref, d_ref, mo_ref, vo_ref):
        gv = g_ref[...]
        mn = ADAM_B1 * m_ref[...] + (1.0 - ADAM_B1) * gv
        vn = ADAM_B2 * v_ref[...] + (1.0 - ADAM_B2) * (gv * gv)
        m_hat = mn / c1
        v_hat = vn / c2
        d_ref[...] = -ADAM_LR * (m_hat / (jnp.sqrt(v_hat) + ADAM_EPS) + ADAM_WD * w_ref[...])
        mo_ref[...] = mn
        vo_ref[...] = vn

    if lead:
        grid = (lead[0], rows // tr, cols // tc)
        blk = pl.BlockSpec((None, tr, tc), lambda l, i, j: (l, i, j))
    else:
        grid = (rows // tr, cols // tc)
        blk = pl.BlockSpec((tr, tc), lambda i, j: (i, j))
    sds = jax.ShapeDtypeStruct(w.shape, F32)
    return pl.pallas_call(body, name=name, grid=grid, in_specs=[blk] * 4, out_specs=(blk,) * 3,
                          out_shape=(sds,) * 3, compiler_params=_cp(("parallel",) * len(grid)))(w, g, m, v)


def _gla_consts():
    ltri = (_iota((CHUNK, CHUNK), 0) >= _iota((CHUNK, CHUNK), 1)).astype(F32)
    ones_c = jnp.ones((CHUNK, 128), F32)
    mask = (_iota((256, 512), 0) // 64 == _iota((256, 512), 1) // 128).astype(F32)
    return ltri, ones_c, mask


def _gla_chunk(consts, q, k, v, r, aux, s_prev, wa, ba, nw):
    ltri, ones_c, mask = consts
    la = _log_sigmoid(bdot(aux, wa, "nn") + ba) * (1.0 / 16.0)
    cum = hdot(ltri, la, "nn")
    total = jnp.sum(la, axis=0, keepdims=True)
    k_dec = k * jnp.exp(total - cum)
    inc = bdot(k_dec, v, "tn") * mask
    dec = jnp.exp(hdot(la, ones_c, "tn"))
    dec = jnp.concatenate([dec, dec, dec, dec], axis=1)
    s_new = dec * s_prev + inc
    o = bdot(q * GLA_SCALE, s_new, "nn")
    parts = []
    for h in range(4):
        oh = o[:, h * 128:(h + 1) * 128]
        parts.append(oh * lax.rsqrt(jnp.mean(oh * oh, axis=-1, keepdims=True) + EPS))
    on = jnp.concatenate(parts, axis=1)
    return s_new, on * nw * (r * _sigmoid(r))


GLA_PER_STEP = 4
GLA_ROWS = GLA_PER_STEP * CHUNK
GLA_STEPS = NCHUNK // GLA_PER_STEP


def _gla_specs(cmap):
    return [pl.BlockSpec((GLA_ROWS, 256), lambda c: (cmap(c), 0)),
            pl.BlockSpec((GLA_ROWS, 256), lambda c: (cmap(c), 1)),
            pl.BlockSpec((GLA_ROWS, 512), lambda c: (cmap(c), 1)),
            pl.BlockSpec((GLA_ROWS, 512), lambda c: (cmap(c), 2)),
            pl.BlockSpec((GLA_ROWS, 128), lambda c: (cmap(c), AUX_BLK))]


def _gla_fwd(proj, wa, ba, nw):
    def body(q_ref, k_ref, v_ref, r_ref, aux_ref, wa_ref, ba_ref, nw_ref, o_ref, sp_ref, s_ref):
        @pl.when(pl.program_id(0) == 0)
        def _():
            s_ref[...] = jnp.zeros_like(s_ref)

        s = s_ref[...]
        consts = _gla_consts()
        outs, states = [], []
        for i in range(GLA_PER_STEP):
            rows = slice(i * CHUNK, (i + 1) * CHUNK)
            states.append(s)
            s, out = _gla_chunk(consts, q_ref[rows, :], k_ref[rows, :], v_ref[rows, :], r_ref[rows, :], aux_ref[rows, :],
                                s, wa_ref[...], ba_ref[...], nw_ref[...])
            outs.append(out)
        s_ref[...] = s
        for i in range(GLA_PER_STEP):
            o_ref[i * CHUNK:(i + 1) * CHUNK, :] = outs[i]
            sp_ref[i] = states[i]

    full = lambda shape: pl.BlockSpec(shape, lambda c: (0,) * len(shape))
    return pl.pallas_call(
        body, name="gla_fwd", grid=(GLA_STEPS,),
        in_specs=_gla_specs(lambda c: c) + [full((128, 256)), full((1, 256)), full((1, 512))],
        out_specs=(pl.BlockSpec((GLA_ROWS, 512), lambda c: (c, 0)),
                   pl.BlockSpec((GLA_PER_STEP, 256, 512), lambda c: (c, 0, 0))),
        out_shape=(jax.ShapeDtypeStruct((T, D), F32), jax.ShapeDtypeStruct((NCHUNK, 256, 512), F32)),
        scratch_shapes=[pltpu.VMEM((256, 512), F32)],
        compiler_params=_cp(("arbitrary",)),
    )(proj, proj, proj, proj, proj, wa, ba, nw)


def _gla_bwd(proj, s_prev_all, wa, ba, nw, dcat):
    rev = lambda c: GLA_STEPS - 1 - c

    def body(q_ref, k_ref, v_ref, r_ref, aux_ref, sp_ref, wa_ref, ba_ref, nw_ref, do_ref,
             dq_ref, dk_ref, dv_ref, dr_ref, daux_ref, dwa_ref, dba_ref, dnw_ref, ds_ref):
        @pl.when(pl.program_id(0) == 0)
        def _():
            ds_ref[...] = jnp.zeros_like(ds_ref)
            dwa_ref[...] = jnp.zeros_like(dwa_ref)
            dba_ref[...] = jnp.zeros_like(dba_ref)
            dnw_ref[...] = jnp.zeros_like(dnw_ref)

        fn = functools.partial(_gla_chunk, _gla_consts())
        ds = ds_ref[...]
        dwa, dba, dnw = dwa_ref[...], dba_ref[...], dnw_ref[...]
        grads = {}
        for i in reversed(range(GLA_PER_STEP)):
            rows = slice(i * CHUNK, (i + 1) * CHUNK)
            _, vjp = jax.vjp(fn, q_ref[rows, :], k_ref[rows, :], v_ref[rows, :], r_ref[rows, :], aux_ref[rows, :],
                             sp_ref[i], wa_ref[...], ba_ref[...], nw_ref[...])
            *grads[i], ds, dwa_i, dba_i, dnw_i = vjp((ds, do_ref[rows, :]))
            dwa, dba, dnw = dwa + dwa_i, dba + dba_i, dnw + dnw_i
        ds_ref[...] = ds
        dwa_ref[...] = dwa
        dba_ref[...] = dba
        dnw_ref[...] = dnw
        for i in range(GLA_PER_STEP):
            rows = slice(i * CHUNK, (i + 1) * CHUNK)
            for ref, g in zip((dq_ref, dk_ref, dv_ref, dr_ref, daux_ref), grads[i]):
                ref[rows, :] = g

    full = lambda shape: pl.BlockSpec(shape, lambda c: (0,) * len(shape))
    blk = lambda w: pl.BlockSpec((GLA_ROWS, w), lambda c: (rev(c), 0))
    sds = lambda *s: jax.ShapeDtypeStruct(s, F32)
    return pl.pallas_call(
        body, name="gla_bwd", grid=(GLA_STEPS,),
        in_specs=_gla_specs(rev) + [pl.BlockSpec((GLA_PER_STEP, 256, 512), lambda c: (rev(c), 0, 0)),
                                    full((128, 256)), full((1, 256)), full((1, 512)), blk(512)],
        out_specs=(blk(256), blk(256), blk(512), blk(512), blk(128), full((128, 256)), full((1, 256)), full((1, 512))),
        out_shape=(sds(T, 256), sds(T, 256), sds(T, 512), sds(T, 512), sds(T, 128),
                   sds(128, 256), sds(1, 256), sds(1, 512)),
        scratch_shapes=[pltpu.VMEM((256, 512), F32)],
        compiler_params=_cp(("arbitrary",)),
    )(proj, proj, proj, proj, proj, s_prev_all, wa, ba, nw, dcat)


GATE_ROWS = 128


def _fox_gate_block(ltri, aux, bpad, carry):
    lf = _log_sigmoid(aux + bpad)
    cum = hdot(ltri, lf, "nn") + carry
    return cum, carry + jnp.sum(lf, axis=0, keepdims=True)


def _gate_ltri():
    return (_iota((GATE_ROWS, GATE_ROWS), 0) >= _iota((GATE_ROWS, GATE_ROWS), 1)).astype(F32)


def _fox_gate_fwd(proj, bpad):
    def body(aux_ref, b_ref, cum_ref, carry_ref):
        i = pl.program_id(0)

        @pl.when(i == 0)
        def _():
            carry_ref[...] = jnp.zeros_like(carry_ref)

        cum, carry = _fox_gate_block(_gate_ltri(), aux_ref[...], b_ref[...], carry_ref[...])
        cum_ref[...] = cum
        carry_ref[...] = carry

    return pl.pallas_call(
        body, name="fox_gate_fwd", grid=(T // GATE_ROWS,),
        in_specs=[pl.BlockSpec((GATE_ROWS, 128), lambda i: (i, AUX_BLK)), pl.BlockSpec((1, 128), lambda i: (0, 0))],
        out_specs=pl.BlockSpec((GATE_ROWS, 128), lambda i: (i, 0)),
        out_shape=jax.ShapeDtypeStruct((T, 128), F32),
        scratch_shapes=[pltpu.VMEM((1, 128), F32)],
        compiler_params=_cp(("arbitrary",)),
    )(proj, bpad)


def _fox_gate_bwd(proj, bpad, dccol_t, daux_gla):
    nb = T // GATE_ROWS
    rev = lambda i: nb - 1 - i

    def body(aux_ref, b_ref, dc_ref, dg_ref, daux_ref, db_ref, dcarry_ref):
        i = pl.program_id(0)

        @pl.when(i == 0)
        def _():
            dcarry_ref[...] = jnp.zeros_like(dcarry_ref)
            db_ref[...] = jnp.zeros_like(db_ref)

        dcum = dc_ref[...]
        fn = functools.partial(_fox_gate_block, _gate_ltri())
        _, vjp = jax.vjp(fn, aux_ref[...], b_ref[...], jnp.zeros((1, 128), F32))
        daux, db, dcarry = vjp((dcum, dcarry_ref[...]))
        daux_ref[...] = daux + dg_ref[...]
        db_ref[...] += db
        dcarry_ref[...] = dcarry

    blk = pl.BlockSpec((GATE_ROWS, 128), lambda i: (rev(i), 0))
    vec = pl.BlockSpec((1, 128), lambda i: (0, 0))
    return pl.pallas_call(
        body, name="fox_gate_bwd", grid=(nb,),
        in_specs=[pl.BlockSpec((GATE_ROWS, 128), lambda i: (rev(i), AUX_BLK)), vec, blk, blk],
        out_specs=(blk, vec),
        out_shape=(jax.ShapeDtypeStruct((T, 128), F32), jax.ShapeDtypeStruct((1, 128), F32)),
        scratch_shapes=[pltpu.VMEM((1, 128), F32)],
        compiler_params=_cp(("arbitrary",)),
    )(proj, bpad, dccol_t, daux_gla)


FOX_Q = 128


FOX_QB = T // FOX_Q


@jax.custom_vjp
def _attend(s, v):
    return _attend_fwd(s, v)[0]


def _attend_fwd(s, v):
    e = jnp.exp(s - jnp.max(s, axis=-1, keepdims=True))
    r = 1.0 / jnp.sum(e, axis=-1, keepdims=True)
    return _dot(e, v, "nn") * r, (e, r, v)


def _attend_bwd(res, do):
    e, r, v = res
    do_r = do * r
    dpr = _dot(do_r, v, "nt")
    ds = e * (dpr - r * jnp.sum(e * dpr, axis=-1, keepdims=True))
    return ds, _dot(e, do_r, "tn").astype(v.dtype)


_attend.defvjp(_attend_fwd, _attend_bwd)


def _fox_block(hp, q, k, v, ccol):
    kl = k.shape[0]
    lane = _iota((FOX_Q, 128), 1)
    tri = jnp.bitwise_and(_iota((2 * FOX_Q, FOX_Q), 0), FOX_Q - 1) >= _iota((2 * FOX_Q, FOX_Q), 1)
    sub = _iota((8, kl), 0)
    qs = q * ATT_SCALE
    q2 = jnp.concatenate([jnp.where(lane < 64, qs, 0.0), jnp.where(lane >= 64, qs, 0.0)], axis=0)
    s = bdot(q2, k, "nt")
    cs = [jnp.sum(jnp.where(sub == 2 * hp + e, ccol, 0.0), axis=0, keepdims=True) for e in range(2)]
    s = jnp.concatenate([s[:FOX_Q] - cs[0], s[FOX_Q:] - cs[1]], axis=0)
    diag = jnp.where(tri, s[:, kl - FOX_Q:], NEG)
    s = diag if kl == FOX_Q else jnp.concatenate([s[:, :kl - FOX_Q], diag], axis=1)
    o2 = _attend(s, v)
    return jnp.where(lane < 64, o2[:FOX_Q], o2[FOX_Q:])


def _fox_in_specs():
    return [pl.BlockSpec((FOX_Q, 128), lambda hp, qb: (qb, 12 + hp)),
            pl.BlockSpec((T, 128), lambda hp, qb: (0, 16 + hp)),
            pl.BlockSpec((T, 128), lambda hp, qb: (0, 20 + hp)),
            pl.BlockSpec((8, T), lambda hp, qb: (0, 0))]


def _fox_fwd(proj, cum_c, cat):
    def body(q_ref, k_ref, v_ref, cc_ref, cat_ref, o_ref):
        qb = pl.program_id(1)
        for g in range(FOX_QB):
            kl = FOX_Q * (g + 1)

            @pl.when(qb == g)
            def _(kl=kl):
                o_ref[...] = _fox_block(pl.program_id(0), q_ref[...], k_ref[0:kl, :], v_ref[0:kl, :], cc_ref[:, 0:kl])

    return pl.pallas_call(
        body, name="fox_fwd", grid=(4, FOX_QB), in_specs=_fox_in_specs() + [pl.BlockSpec(memory_space=pl.ANY)],
        out_specs=pl.BlockSpec((FOX_Q, 128), lambda hp, qb: (qb, 4 + hp)),
        out_shape=jax.ShapeDtypeStruct((T, D), F32), input_output_aliases={4: 0},
        compiler_params=_cp(("parallel", "parallel")),
    )(proj, proj, proj, cum_c, cat)


def _fox_bwd(proj, cum_c, dcat):
    def body(q_ref, k_ref, v_ref, cc_ref, do_ref, dq_ref, dk_ref, dv_ref, dcc_ref):
        qb = pl.program_id(1)

        @pl.when(qb == 0)
        def _():
            dk_ref[...] = jnp.zeros_like(dk_ref)
            dv_ref[...] = jnp.zeros_like(dv_ref)
            dcc_ref[...] = jnp.zeros_like(dcc_ref)

        fn = functools.partial(_fox_block, pl.program_id(0))
        for g in range(FOX_QB):
            kl = FOX_Q * (g + 1)

            @pl.when(qb == g)
            def _(kl=kl):
                _, vjp = jax.vjp(fn, q_ref[...], k_ref[0:kl, :], v_ref[0:kl, :], cc_ref[:, 0:kl])
                dq, dk, dv, dcc = vjp(do_ref[...])
                dq_ref[...] = dq
                dk_ref[0:kl, :] += dk
                dv_ref[0:kl, :] += dv
                dcc_ref[:, 0:kl] += dcc

    sds = lambda *s: jax.ShapeDtypeStruct(s, F32)
    return pl.pallas_call(
        body, name="fox_bwd", grid=(4, FOX_QB),
        in_specs=_fox_in_specs() + [pl.BlockSpec((FOX_Q, 128), lambda hp, qb: (qb, 4 + hp))],
        out_specs=(pl.BlockSpec((FOX_Q, 128), lambda hp, qb: (qb, hp)),
                   pl.BlockSpec((T, 128), lambda hp, qb: (0, hp)),
                   pl.BlockSpec((T, 128), lambda hp, qb: (0, hp)),
                   pl.BlockSpec((None, 8, T), lambda hp, qb: (hp, 0, 0))),
        out_shape=(sds(T, 512), sds(T, 512), sds(T, 512), sds(4, 8, T)),
        compiler_params=_cp(("parallel", "arbitrary")),
    )(proj, proj, proj, cum_c, dcat)


BIAS_W = 640


def _rel_onehot():
    j = _iota((REL_PAD, BIAS_W), 1)
    rel = jnp.clip(CA_PAD + CHUNK - 1 - j, -128, 128) + 128
    return (_iota((REL_PAD, BIAS_W), 0) == rel).astype(F32)


def _bias_build(rbp):
    def body(rb_ref, o_ref):
        f = _hdot_raw(rb_ref[...], _rel_onehot(), "nn")
        for q in range(CHUNK):
            o_ref[q] = pltpu.roll(f, (BIAS_W - (CHUNK - 1 - q)) % BIAS_W, 1)[:, :CA_BAND]

    return pl.pallas_call(body, name="ca_bias_build", out_shape=jax.ShapeDtypeStruct((CHUNK, 8, CA_BAND), F32))(rbp)


def _bias_grad(dbias_q):
    def body(db_ref, o_ref):
        acc = jnp.zeros((8, BIAS_W), F32)
        for q in range(CHUNK):
            acc = acc + pltpu.roll(db_ref[q], CHUNK - 1 - q, 1)
        o_ref[...] = _hdot_raw(acc, _rel_onehot(), "nt")

    return pl.pallas_call(body, name="ca_bias_grad", out_shape=jax.ShapeDtypeStruct((8, REL_PAD), F32))(dbias_q)


def _ca_block(c, masked, q, kb, vb, bias2):
    lane = _iota((CHUNK, 128), 1)
    qs = q * ATT_SCALE
    q2 = jnp.concatenate([jnp.where(lane < 64, qs, 0.0), jnp.where(lane >= 64, qs, 0.0)], axis=0)
    s = bdot(q2, kb, "nt") + bias2.reshape(2 * CHUNK, CA_BAND)
    if masked:
        s = jnp.where((c * CHUNK - CA_PAD + _iota((2 * CHUNK, CA_BAND), 1)) >= 0, s, NEG)
    o2 = _attend(s, vb)
    return jnp.where(lane < 64, o2[:CHUNK], o2[CHUNK:])


CA_PER_STEP = 8
CA_ROWS = CA_PER_STEP * CHUNK
CA_MASKED_STEPS = CA_PAD // CA_ROWS


def _ca_fwd(proj, kvpad, bias):
    def body(q_ref, k_ref, v_ref, b_ref, o_ref):
        def run(masked):
            outs = []
            for i in range(CA_PER_STEP):
                c = pl.program_id(1) * CA_PER_STEP + i
                band = pl.ds(pl.multiple_of(c * CHUNK, CHUNK), CA_BAND)
                rows = slice(i * CHUNK, (i + 1) * CHUNK)
                outs.append(_ca_block(c, masked, q_ref[rows, :], k_ref[band, :], v_ref[band, :], b_ref[...]))
            for i in range(CA_PER_STEP):
                o_ref[i * CHUNK:(i + 1) * CHUNK, :] = outs[i]

        pl.when(pl.program_id(1) < CA_MASKED_STEPS)(lambda: run(True))
        pl.when(pl.program_id(1) >= CA_MASKED_STEPS)(lambda: run(False))

    return pl.pallas_call(
        body, name="ca_fwd", grid=(4, NCHUNK // CA_PER_STEP),
        in_specs=[pl.BlockSpec((CA_ROWS, 128), lambda hp, c: (c, hp)),
                  pl.BlockSpec((T + CA_PAD, 128), lambda hp, c: (0, hp)),
                  pl.BlockSpec((T + CA_PAD, 128), lambda hp, c: (0, 4 + hp)),
                  pl.BlockSpec((2, CHUNK, CA_BAND), lambda hp, c: (hp, 0, 0))],
        out_specs=pl.BlockSpec((CA_ROWS, 128), lambda hp, c: (c, hp)),
        out_shape=jax.ShapeDtypeStruct((T, D), F32),
        compiler_params=_cp(("parallel", "parallel")),
    )(proj, kvpad, kvpad, bias)


def _ca_bwd(proj, kvpad, bias, dcat):
    def body(q_ref, k_ref, v_ref, b_ref, do_ref, dq_ref, dk_ref, dv_ref, db_ref):
        c = pl.program_id(1)

        @pl.when(c == 0)
        def _():
            dk_ref[...] = jnp.zeros_like(dk_ref)
            dv_ref[...] = jnp.zeros_like(dv_ref)
            db_ref[...] = jnp.zeros_like(db_ref)

        def run(masked):
            grads, bands = [], []
            for i in range(CA_PER_STEP):
                ci = c * CA_PER_STEP + i
                band = pl.ds(pl.multiple_of(ci * CHUNK, CHUNK), CA_BAND)
                rows = slice(i * CHUNK, (i + 1) * CHUNK)
                fn = functools.partial(_ca_block, ci, masked)
                _, vjp = jax.vjp(fn, q_ref[rows, :], k_ref[band, :], v_ref[band, :], b_ref[...])
                grads.append(vjp(do_ref[rows, :]))
                bands.append(band)
            for i, (dq, _, _, _) in enumerate(grads):
                dq_ref[i * CHUNK:(i + 1) * CHUNK, :] = dq
            for band, (_, dkb, dvb, _) in zip(bands, grads):
                dk_ref[band, :] += dkb
                dv_ref[band, :] += dvb
            db_ref[...] += functools.reduce(lambda a, b: a + b, [g[3] for g in grads])

        pl.when(c < CA_MASKED_STEPS)(lambda: run(True))
        pl.when(c >= CA_MASKED_STEPS)(lambda: run(False))

    sds = lambda *s: jax.ShapeDtypeStruct(s, F32)
    padded = lambda: pl.BlockSpec((T + CA_PAD, 128), lambda hp, c: (0, hp))
    return pl.pallas_call(
        body, name="ca_bwd", grid=(4, NCHUNK // CA_PER_STEP),
        in_specs=[pl.BlockSpec((CA_ROWS, 128), lambda hp, c: (c, hp)),
                  pl.BlockSpec((T + CA_PAD, 128), lambda hp, c: (0, hp)),
                  pl.BlockSpec((T + CA_PAD, 128), lambda hp, c: (0, 4 + hp)),
                  pl.BlockSpec((2, CHUNK, CA_BAND), lambda hp, c: (hp, 0, 0)),
                  pl.BlockSpec((CA_ROWS, 128), lambda hp, c: (c, hp))],
        out_specs=(pl.BlockSpec((CA_ROWS, 128), lambda hp, c: (c, hp)), padded(), padded(),
                   pl.BlockSpec((2, CHUNK, CA_BAND), lambda hp, c: (hp, 0, 0))),
        out_shape=(sds(T, 512), sds(T + CA_PAD, 512), sds(T + CA_PAD, 512), sds(8, CHUNK, CA_BAND)),
        compiler_params=_cp(("parallel", "arbitrary")),
    )(proj, kvpad, kvpad, bias, dcat)


def _lru_pre(xs, cw, cb, wa, ba, wx, bx, lam):
    xc = cb + xs[0] * cw[0:1, :] + xs[1] * cw[1:2, :] + xs[2] * cw[2:3, :] + xs[3] * cw[3:4, :]
    ra = _sigmoid(bdot(xc, wa, "nn") + ba)
    ii = _sigmoid(bdot(xc, wx, "nn") + bx)
    la = 8.0 * ra * _log_sigmoid(lam)
    return jnp.exp(la), jnp.sqrt(-_expm1(2.0 * la)) * (ii * xc)


def _lru_pre_specs():
    full = lambda shape: pl.BlockSpec(shape, lambda i: (0,) * len(shape))
    return [pl.BlockSpec((4, ROWS, 512), lambda i: (0, i, 0)), full((4, 512)), full((1, 512)),
            full((512, 512)), full((1, 512)), full((512, 512)), full((1, 512)), full((1, 512))]


def _lru_pre_fwd(xs, cw, cb, wa, ba, wx, bx, lam):
    def body(xs_ref, cw_ref, cb_ref, wa_ref, ba_ref, wx_ref, bx_ref, lam_ref, a_ref, b_ref):
        a, b = _lru_pre(xs_ref[...], cw_ref[...], cb_ref[...], wa_ref[...], ba_ref[...], wx_ref[...], bx_ref[...],
                        lam_ref[...])
        a_ref[...] = a
        b_ref[...] = b

    row = pl.BlockSpec((ROWS, 512), lambda i: (i, 0))
    sds = jax.ShapeDtypeStruct((T, 512), F32)
    return pl.pallas_call(body, name="lru_pre_fwd", grid=(T // ROWS,), in_specs=_lru_pre_specs(),
                          out_specs=(row, row), out_shape=(sds, sds), compiler_params=_cp(("parallel",)),
                          )(xs, cw, cb, wa, ba, wx, bx, lam)


def _lru_pre_bwd(xs, cw, cb, wa, ba, wx, bx, lam, da, db):
    def body(xs_ref, cw_ref, cb_ref, wa_ref, ba_ref, wx_ref, bx_ref, lam_ref, da_ref, db_ref,
             dxs_ref, dcw_ref, dcb_ref, dwa_ref, dba_ref, dwx_ref, dbx_ref, dlam_ref):
        acc = (dcw_ref, dcb_ref, dwa_ref, dba_ref, dwx_ref, dbx_ref, dlam_ref)

        @pl.when(pl.program_id(0) == 0)
        def _():
            for r in acc:
                r[...] = jnp.zeros_like(r)

        _, vjp = jax.vjp(_lru_pre, xs_ref[...], cw_ref[...], cb_ref[...], wa_ref[...], ba_ref[...], wx_ref[...],
                         bx_ref[...], lam_ref[...])
        grads = vjp((da_ref[...], db_ref[...]))
        dxs_ref[...] = grads[0]
        for r, g in zip(acc, grads[1:]):
            r[...] += g

    row = pl.BlockSpec((ROWS, 512), lambda i: (i, 0))
    specs = _lru_pre_specs()
    sds = lambda *s: jax.ShapeDtypeStruct(s, F32)
    return pl.pallas_call(
        body, name="lru_pre_bwd", grid=(T // ROWS,), in_specs=specs + [row, row], out_specs=tuple(specs),
        out_shape=(sds(4, T, 512), sds(4, 512), sds(1, 512), sds(512, 512), sds(1, 512), sds(512, 512), sds(1, 512),
                   sds(1, 512)),
        compiler_params=_cp(("arbitrary",)),
    )(xs, cw, cb, wa, ba, wx, bx, lam, da, db)


def _lru_scan_fwd(a, b):
    def body(a_ref, b_ref, h_ref):
        def step(t, h):
            h = a_ref[pl.ds(t, 1), :] * h + b_ref[pl.ds(t, 1), :]
            h_ref[pl.ds(t, 1), :] = h
            return h

        lax.fori_loop(0, T, step, jnp.zeros((1, 512), F32), unroll=8)

    return pl.pallas_call(body, name="lru_scan_fwd", out_shape=jax.ShapeDtypeStruct((T, 512), F32),
                          compiler_params=pltpu.CompilerParams(vmem_limit_bytes=VMEM_LIMIT))(a, b)


def _lru_scan_bwd(a, h, dh):
    def body(a_ref, h_ref, dh_ref, da_ref, db_ref):
        def step(i, carry):
            t = T - 1 - i
            g = dh_ref[pl.ds(t, 1), :] + carry
            db_ref[pl.ds(t, 1), :] = g
            da_ref[pl.ds(t, 1), :] = g * h_ref[pl.ds(t - 1, 1), :]
            return a_ref[pl.ds(t, 1), :] * g

        carry = lax.fori_loop(0, T - 1, step, jnp.zeros((1, 512), F32), unroll=8)
        db_ref[pl.ds(0, 1), :] = dh_ref[pl.ds(0, 1), :] + carry
        da_ref[pl.ds(0, 1), :] = jnp.zeros((1, 512), F32)

    sds = jax.ShapeDtypeStruct((T, 512), F32)
    return pl.pallas_call(body, name="lru_scan_bwd", out_shape=(sds, sds),
                          compiler_params=pltpu.CompilerParams(vmem_limit_bytes=VMEM_LIMIT))(a, h, dh)


def _lru_post(h, gate):
    return h * _gelu_tanh(gate)


def _lru_post_fwd(h, proj, cat):
    def body(h_ref, g_ref, cat_ref, o_ref):
        o_ref[...] = _lru_post(h_ref[...], g_ref[...])

    row = pl.BlockSpec((ROWS, 512), lambda i: (i, 0))
    return pl.pallas_call(body, name="lru_post_fwd", grid=(T // ROWS,),
                          in_specs=[row, pl.BlockSpec((ROWS, 512), lambda i: (i, 3)), pl.BlockSpec(memory_space=pl.ANY)],
                          out_specs=pl.BlockSpec((ROWS, 512), lambda i: (i, 1)),
                          out_shape=jax.ShapeDtypeStruct((T, D), F32), input_output_aliases={2: 0},
                          compiler_params=_cp(("parallel",)))(h, proj, cat)


def _lru_post_bwd(h, proj, dcat):
    def body(h_ref, g_ref, do_ref, dh_ref, dg_ref):
        _, vjp = jax.vjp(_lru_post, h_ref[...], g_ref[...])
        dh, dg = vjp(do_ref[...])
        dh_ref[...] = dh
        dg_ref[...] = dg

    row = pl.BlockSpec((ROWS, 512), lambda i: (i, 0))
    sds = jax.ShapeDtypeStruct((T, 512), F32)
    return pl.pallas_call(body, name="lru_post_bwd", grid=(T // ROWS,),
                          in_specs=[row, pl.BlockSpec((ROWS, 512), lambda i: (i, 3)),
                                    pl.BlockSpec((ROWS, 512), lambda i: (i, 1))],
                          out_specs=(row, row), out_shape=(sds, sds), compiler_params=_cp(("parallel",)))(h, proj, dcat)


def _conv_dx(dxs_shift):
    def body(d_ref, o_ref):
        o_ref[...] = d_ref[0] + d_ref[1] + d_ref[2] + d_ref[3]

    row = pl.BlockSpec((ROWS, 512), lambda i: (i, 0))
    return pl.pallas_call(body, name="lru_conv_dx", grid=(T // ROWS,),
                          in_specs=[pl.BlockSpec((4, ROWS, 512), lambda i: (0, i, 0))], out_specs=row,
                          out_shape=jax.ShapeDtypeStruct((T, 512), F32), compiler_params=_cp(("parallel",)))(dxs_shift)


def _position():
    return lax.axis_index("x"), lax.axis_index("y"), lax.axis_index("c")


def _other_chips(x, y):
    return [(1 - x, y), (x, 1 - y), (1 - x, 1 - y)]


def _al(v, n):
    return v * n if isinstance(v, int) else pl.multiple_of(v * n, n)


_AG_ITEMS = [
    ((4, 32, 128), lambda o, s, h: o.at[s, pl.ds(_al(h, 16), 16), :], lambda r, h: r.at[pl.ds(_al(h, 16), 16), :]),
    ((4, 774, 1024), lambda o, s, h: o.at[s, :, pl.ds(_al(h, 512), 512)], lambda r, h: r.at[:, pl.ds(_al(h, 512), 512)]),
    ((1024, 1024), lambda o, s, h: o.at[pl.ds(_al(2 * s + h, 128), 128), :], lambda r, h: r.at[pl.ds(_al(h, 128), 128), :]),
    ((2, 1024, 4096), lambda o, s, h: o.at[h, :, pl.ds(_al(s, 1024), 1024)], lambda r, h: r.at[h]),
    ((2, 4096, 1024), lambda o, s, h: o.at[h, pl.ds(_al(s, 1024), 1024), :], lambda r, h: r.at[h]),
    ((1024, 2560), lambda o, s, h: o.at[pl.ds(_al(h, 512), 512), pl.ds(_al(s, 640), 640)],
     lambda r, h: r.at[pl.ds(_al(h, 512), 512), :]),
    ((1024, 1024), lambda o, s, h: o.at[pl.ds(_al(2 * s + h, 128), 128), :], lambda r, h: r.at[pl.ds(_al(h, 128), 128), :]),
]


_AG_GROUPS = [(0, 1, 2), (3, 4), (5, 6)]

_HBM = pl.BlockSpec(memory_space=pltpu.HBM)
_SEM = pl.BlockSpec(memory_space=pltpu.SEMAPHORE)
_SPLIT = dict(has_side_effects=pltpu.SideEffectType.DATAFLOW_SIDE_EFFECTING)


def _hbm(a):
    return pltpu.with_memory_space_constraint(a, pltpu.HBM)


def _ag_ici_copy(i, j, chip, c, slot, src_ref, land_ref, send_sems, recv_sems, k):
    _, dst, half = _AG_ITEMS[i]
    return pltpu.make_async_remote_copy(src_ref=half(src_ref, c), dst_ref=dst(land_ref, slot, c), send_sem=send_sems.at[k],
                                        recv_sem=recv_sems.at[k], device_id=(*chip, c), device_id_type=MESH)


def _ag_start(shards):
    n = len(_AG_ITEMS)
    ng = len(_AG_GROUPS)
    lands = [lax.empty(shape, s.dtype) for (shape, _, _), s in zip(_AG_ITEMS, shards)]

    def body(*refs):
        srcs, land_refs = refs[:n], refs[n:2 * n]
        sems = refs[2 * n:2 * n + 2 * ng]
        token = refs[-1]
        x, y, c = _position()
        me = 2 * x + y
        for g, items in enumerate(_AG_GROUPS):
            for t, i in enumerate(items):
                for j, chip in enumerate(_other_chips(x, y)):
                    _ag_ici_copy(i, j, chip, c, me, srcs[i], land_refs[i], sems[2 * g], sems[2 * g + 1], 3 * t + j).start()
        token[...] = jnp.zeros_like(token)

    sem_shapes = []
    for items in _AG_GROUPS:
        sem_shapes += [pltpu.SemaphoreType.DMA((3 * len(items),))] * 2
    thru = [pltpu.HBM(a.shape, a.dtype) for a in list(shards) + lands]
    out = pl.pallas_call(
        body, name="allgather_start",
        out_shape=tuple(sem_shapes) + tuple(thru) + (jax.ShapeDtypeStruct((8, 128), F32),),
        in_specs=(_HBM,) * (2 * n),
        out_specs=(_SEM,) * (2 * ng) + (_HBM,) * (2 * n) + (pl.BlockSpec(memory_space=pltpu.VMEM),),
        input_output_aliases={i: 2 * ng + i for i in range(2 * n)},
        compiler_params=pltpu.CompilerParams(**_SPLIT),
    )(*[_hbm(a) for a in list(shards) + lands])
    sems, thru, token = out[:2 * ng], out[2 * ng:-1], out[-1]
    return [(sems[2 * g], sems[2 * g + 1]) for g in range(ng)], list(thru[:n]), list(thru[n:]), token


def _ag_wait(g, sems, srcs, lands, after):
    items = _AG_GROUPS[g]
    m = len(items)

    def body(*refs):
        src_refs, land_refs = refs[:m], refs[m:2 * m]
        send_sems, recv_sems = refs[2 * m], refs[2 * m + 1]
        x, y, c = _position()
        for t, i in enumerate(items):
            for j, chip in enumerate(_other_chips(x, y)):
                cp = _ag_ici_copy(i, j, chip, c, 2 * chip[0] + chip[1], src_refs[t], land_refs[t], send_sems, recv_sems,
                                  3 * t + j)
                cp.wait_send()
                cp.wait_recv()

    ops = [srcs[i] for i in items] + [lands[i] for i in items]
    out = pl.pallas_call(
        body, name=f"allgather_wait_{g}",
        out_shape=tuple(pltpu.HBM(a.shape, a.dtype) for a in ops),
        in_specs=(_HBM,) * (2 * m) + (_SEM, _SEM, pl.BlockSpec(memory_space=pl.ANY)),
        out_specs=(_HBM,) * (2 * m),
        input_output_aliases={i: i for i in range(2 * m)},
        compiler_params=pltpu.CompilerParams(**_SPLIT),
    )(*ops, sems[0], sems[1], after)
    return list(out[:m]), list(out[m:])


def _ag_forward(g, srcs, lands):
    items = _AG_GROUPS[g]
    m = len(items)

    def body(*refs):
        src_refs, in_refs, out_refs = refs[:m], refs[m:2 * m], refs[2 * m:3 * m]
        send_sems, recv_sems = refs[3 * m:]
        x, y, c = _position()
        sibling = (x, y, 1 - c)
        me = 2 * x + y
        chips = _other_chips(x, y)
        sends = []
        for t, i in enumerate(items):
            _, dst, half = _AG_ITEMS[i]
            for j, chip in enumerate(chips):
                slot = 2 * chip[0] + chip[1]
                sends.append(pltpu.make_async_remote_copy(
                    src_ref=dst(in_refs[t], slot, c), dst_ref=dst(out_refs[t], slot, c), send_sem=send_sems.at[5 * t + j],
                    recv_sem=recv_sems.at[5 * t + j], device_id=sibling, device_id_type=MESH))
            for hc in range(2):
                sends.append(pltpu.make_async_remote_copy(
                    src_ref=half(src_refs[t], hc), dst_ref=dst(out_refs[t], me, hc), send_sem=send_sems.at[5 * t + 3 + hc],
                    recv_sem=recv_sems.at[5 * t + 3 + hc], device_id=sibling, device_id_type=MESH))
        for cp in sends:
            cp.start()
        for t, i in enumerate(items):
            _, dst, half = _AG_ITEMS[i]
            for j, chip in enumerate(chips):
                there = dst(out_refs[t], 2 * chip[0] + chip[1], 1 - c)
                pltpu.make_async_remote_copy(src_ref=there, dst_ref=there, send_sem=send_sems.at[5 * t + j],
                                             recv_sem=recv_sems.at[5 * t + j], device_id=sibling,
                                             device_id_type=MESH).wait_recv()
            for hc in range(2):
                there = dst(out_refs[t], me, hc)
                pltpu.make_async_remote_copy(src_ref=there, dst_ref=there, send_sem=send_sems.at[5 * t + 3 + hc],
                                             recv_sem=recv_sems.at[5 * t + 3 + hc], device_id=sibling,
                                             device_id_type=MESH).wait_recv()
        for cp in sends:
            cp.wait_send()

    any_spec = pl.BlockSpec(memory_space=pl.ANY)
    return pl.pallas_call(
        body, name=f"allgather_forward_{g}",
        in_specs=[any_spec] * (2 * m), out_specs=(any_spec,) * m,
        out_shape=tuple(jax.ShapeDtypeStruct(a.shape, a.dtype) for a in lands),
        input_output_aliases={m + t: t for t in range(m)},
        scratch_shapes=[pltpu.SemaphoreType.DMA((5 * m,)), pltpu.SemaphoreType.DMA((5 * m,))],
    )(*srcs, *lands)


def _pair_swap_copy(g_ref, r_ref, send_sem, recv_sem):
    x, y, c = _position()
    hc = g_ref.shape[2] // 2
    return pltpu.make_async_remote_copy(src_ref=g_ref.at[:, :, pl.ds(_al(1 - c, hc), hc)], dst_ref=r_ref,
                                        send_sem=send_sem, recv_sem=recv_sem, device_id=(x, y, 1 - c),
                                        device_id_type=MESH)


def _pair_swap_start(gb, tag):
    _, rows, cols = gb.shape
    recv = lax.empty((4, rows, cols // 2), gb.dtype)

    def body(g_ref, r_ref, send_sem, recv_sem, g_thru, r_thru, token):
        _pair_swap_copy(g_ref, r_ref, send_sem, recv_sem).start()
        token[...] = jnp.zeros_like(token)

    return pl.pallas_call(
        body, name="grad_pair_swap_start_" + tag,
        out_shape=(pltpu.SemaphoreType.DMA(()), pltpu.SemaphoreType.DMA(()), pltpu.HBM(gb.shape, gb.dtype),
                   pltpu.HBM(recv.shape, recv.dtype), jax.ShapeDtypeStruct((8, 128), F32)),
        in_specs=(_HBM, _HBM), out_specs=(_SEM, _SEM, _HBM, _HBM, pl.BlockSpec(memory_space=pltpu.VMEM)),
        input_output_aliases={0: 2, 1: 3},
        compiler_params=pltpu.CompilerParams(**_SPLIT),
    )(_hbm(gb), _hbm(recv))


def _pair_swap_wait(started, after, tag):
    send_sem, recv_sem, gb, recv, _ = started

    def body(g_ref, r_ref, send_sem, recv_sem, after_ref, g_out, r_out):
        cp = _pair_swap_copy(g_ref, r_ref, send_sem, recv_sem)
        cp.wait_send()
        cp.wait_recv()

    return pl.pallas_call(
        body, name="grad_pair_swap_wait_" + tag,
        out_shape=(pltpu.HBM(gb.shape, gb.dtype), pltpu.HBM(recv.shape, recv.dtype)),
        in_specs=(_HBM, _HBM, _SEM, _SEM, pl.BlockSpec(memory_space=pl.ANY)), out_specs=(_HBM, _HBM),
        input_output_aliases={0: 0, 1: 1},
        compiler_params=pltpu.CompilerParams(**_SPLIT),
    )(gb, recv, send_sem, recv_sem, after)


def _handover(red, tag):
    hc = red.shape[1] // 2

    def body(in_ref, out_ref, send_sem, recv_sem):
        x, y, c = _position()
        mine = pl.ds(_al(c, hc), hc)
        cp = pltpu.make_async_remote_copy(src_ref=in_ref.at[:, mine], dst_ref=out_ref.at[:, mine], send_sem=send_sem,
                                          recv_sem=recv_sem, device_id=(x, y, 1 - c), device_id_type=MESH)
        cp.start()
        theirs = out_ref.at[:, pl.ds(_al(1 - c, hc), hc)]
        pltpu.make_async_remote_copy(src_ref=theirs, dst_ref=theirs, send_sem=send_sem, recv_sem=recv_sem,
                                     device_id=(x, y, c), device_id_type=MESH).wait_recv()
        cp.wait_send()

    return pl.pallas_call(
        body, name="grad_handover_" + tag,
        in_specs=[pl.BlockSpec(memory_space=pl.ANY)], out_specs=pl.BlockSpec(memory_space=pl.ANY),
        out_shape=jax.ShapeDtypeStruct(red.shape, red.dtype), input_output_aliases={0: 0},
        scratch_shapes=[pltpu.SemaphoreType.DMA, pltpu.SemaphoreType.DMA],
    )(red)


def _a2a_copy(j, chip, c, p_ref, q_ref, q_slot, send_sems, recv_sems):
    return pltpu.make_async_remote_copy(src_ref=p_ref.at[2 * chip[0] + chip[1]], dst_ref=q_ref.at[q_slot],
                                        send_sem=send_sems.at[j], recv_sem=recv_sems.at[j], device_id=(*chip, c),
                                        device_id_type=MESH)


def _a2a_start(p, tag):
    def body(p_ref, q_ref, send_sems, recv_sems, p_thru, q_thru, token):
        x, y, c = _position()
        for j, chip in enumerate(_other_chips(x, y)):
            _a2a_copy(j, chip, c, p_ref, q_ref, 2 * x + y, send_sems, recv_sems).start()
        token[...] = jnp.zeros_like(token)

    return pl.pallas_call(
        body, name="grad_alltoall_start_" + tag,
        out_shape=(pltpu.SemaphoreType.DMA((3,)), pltpu.SemaphoreType.DMA((3,)), pltpu.HBM(p.shape, p.dtype),
                   pltpu.HBM(p.shape, p.dtype), jax.ShapeDtypeStruct((8, 128), F32)),
        in_specs=(_HBM, _HBM), out_specs=(_SEM, _SEM, _HBM, _HBM, pl.BlockSpec(memory_space=pltpu.VMEM)),
        input_output_aliases={0: 2, 1: 3},
        compiler_params=pltpu.CompilerParams(**_SPLIT),
    )(_hbm(p), _hbm(lax.empty(p.shape, p.dtype)))


def _a2a_wait(send_sems, recv_sems, p, q, after, tag):
    def body(p_ref, q_ref, send_sems, recv_sems, after_ref, p_out, q_out):
        x, y, c = _position()
        for j, chip in enumerate(_other_chips(x, y)):
            cp = _a2a_copy(j, chip, c, p_ref, q_ref, 2 * chip[0] + chip[1], send_sems, recv_sems)
            cp.wait_send()
            cp.wait_recv()

    return pl.pallas_call(
        body, name="grad_alltoall_wait_" + tag,
        out_shape=(pltpu.HBM(p.shape, p.dtype), pltpu.HBM(q.shape, q.dtype)),
        in_specs=(_HBM, _HBM, _SEM, _SEM, pl.BlockSpec(memory_space=pl.ANY)), out_specs=(_HBM, _HBM),
        input_output_aliases={0: 0, 1: 1},
        compiler_params=pltpu.CompilerParams(**_SPLIT),
    )(p, q, send_sems, recv_sems, after)


def _comm_rows(rows):
    return next(t for t in (512, 384, 256, 128) if rows % t == 0)


def _pair_add(gb, recv, where, tag):
    _, rows, cols = gb.shape
    hc = cols // 2
    tr = _comm_rows(rows)

    def body(w_ref, g_ref, r_ref, o_ref):
        o_ref[...] = (g_ref[...].astype(F32) + r_ref[...].astype(F32)).astype(o_ref.dtype)

    return pl.pallas_call(
        body, name="grad_pair_add_" + tag,
        grid_spec=pltpu.PrefetchScalarGridSpec(
            num_scalar_prefetch=1, grid=(4, rows // tr),
            in_specs=[pl.BlockSpec((None, tr, hc), lambda s, j, w_ref: (s, j, w_ref[0])),
                      pl.BlockSpec((None, tr, hc), lambda s, j, w_ref: (s, j, 0))],
            out_specs=pl.BlockSpec((None, tr, hc), lambda s, j, w_ref: (s, j, 0))),
        out_shape=jax.ShapeDtypeStruct((4, rows, hc), gb.dtype),
        compiler_params=_cp(("parallel", "parallel")),
    )(where, gb, recv)


def _sum_chips(p, q, where, tag):
    _, rows, hc = q.shape
    tr = _comm_rows(rows)

    def body(w_ref, p_ref, qa_ref, qb_ref, qc_ref, o_ref):
        me = w_ref[1]
        own, qa, qb, qc = (r[...].astype(F32) for r in (p_ref, qa_ref, qb_ref, qc_ref))
        v0 = jnp.where(me == 0, own, qa)
        v1 = jnp.where(me == 1, own, jnp.where(me == 0, qa, qb))
        v2 = jnp.where(me == 2, own, jnp.where(me < 2, qb, qc))
        v3 = jnp.where(me == 3, own, qc)
        o_ref[...] = ((v0 + v1) + v2) + v3

    slot = lambda k: pl.BlockSpec((None, tr, hc), lambda j, w_ref: (w_ref[k], j, 0))
    return pl.pallas_call(
        body, name="grad_sum_chips_" + tag,
        grid_spec=pltpu.PrefetchScalarGridSpec(
            num_scalar_prefetch=1, grid=(rows // tr,),
            in_specs=[slot(1), slot(2), slot(3), slot(4)],
            out_specs=pl.BlockSpec((tr, hc), lambda j, w_ref: (j, w_ref[0]))),
        out_shape=jax.ShapeDtypeStruct((rows, 2 * hc), F32),
        compiler_params=_cp(("parallel",)),
    )(where, p, q, q, q)


def _shard_major(g, axis):
    shape = g.shape
    g = g.reshape(shape[:axis] + (4, shape[axis] // 4) + shape[axis + 1:])
    return jnp.moveaxis(g, axis, 0).reshape(4, -1)


def _unshard(g4, shape, axis):
    n = shape[axis] // 4
    g = g4.reshape((4,) + shape[:axis] + (n,) + shape[axis + 1:])
    return jnp.moveaxis(g, 0, axis).reshape(shape)


def _split(flat, shapes):
    out, off = [], 0
    for shp in shapes:
        n = 1
        for d in shp:
            n *= d
        out.append(flat[..., off:off + n].reshape(flat.shape[:-1] + tuple(shp)))
        off += n
    return out


def _even_rows_to_kernel(wt):
    return jnp.concatenate([wt[:1536], wt[1552:3088], wt[1536:1552], wt[3088:3096],
                            jnp.zeros((PE - 3096, wt.shape[1]), wt.dtype)], axis=0)


def _block_diag(w):
    eye = jnp.eye(8, dtype=w.dtype)
    return (w[:, :, None, :] * eye[:, None, :, None]).reshape(512, 512)


def _diag_blocks(g):
    eye = jnp.eye(8, dtype=g.dtype)
    return (g.reshape(8, 64, 8, 64) * eye[:, None, :, None]).sum(axis=2)


def _shift_down(a, s):
    return a if s == 0 else jnp.pad(a, ((s, 0), (0, 0)))[:a.shape[0]]


def _shift_up(a, s):
    return a if s == 0 else jnp.pad(a, ((0, s), (0, 0)))[s:]


SMALL_SHARDED_SHAPES = [(2, 4, 256), (16, 64), (4, 128), (128,), (128,), (128,), (128,)]
REPL_SHAPES = [(256,), (512,), (8,), (8, 257), (8, 64, 64), (8, 64, 64)]


def kernel(x, norm_w, w_in_even, gla_w_a_up, gla_b_a, gla_norm_w, fox_b_f, w_out_even, w_in_odd, rel_bias, conv_w, conv_b, lru_w_a, lru_b_a, lru_w_x, lru_b_x, lru_lambda, w_out_odd, w_mlp_up, w_mlp_down, loss_target, m_norm_w, m_w_in_even, m_gla_w_a_up, m_gla_b_a, m_gla_norm_w, m_fox_b_f, m_w_out_even, m_w_in_odd, m_rel_bias, m_conv_w, m_conv_b, m_lru_w_a, m_lru_b_a, m_lru_w_x, m_lru_b_x, m_lru_lambda, m_w_out_odd, m_w_mlp_up, m_w_mlp_down, v_norm_w, v_w_in_even, v_gla_w_a_up, v_gla_b_a, v_gla_norm_w, v_fox_b_f, v_w_out_even, v_w_in_odd, v_rel_bias, v_conv_w, v_conv_b, v_lru_w_a, v_lru_b_a, v_lru_w_x, v_lru_b_x, v_lru_lambda, v_w_out_odd, v_w_mlp_up, v_w_mlp_down):
    c_idx = lax.axis_index("c")

    small_local = [norm_w, gla_w_a_up[0], conv_w[0], conv_b[0], lru_b_a[0], lru_b_x[0], lru_lambda[0]]
    small_src = jnp.concatenate([a.reshape(-1) for a in small_local]).reshape(32, 128)
    mine = [small_src, w_in_even[0].T.astype(BF16), w_out_even[0].astype(BF16), w_mlp_up.astype(BF16),
            w_mlp_down.astype(BF16), w_in_odd[0].astype(BF16), w_out_odd[0].astype(BF16)]
    ag_sems, ag_srcs, ag_lands, ag_token = _ag_start(mine)

    def gathered(g, after):
        srcs_g, lands_g = _ag_wait(g, ag_sems[g], ag_srcs, ag_lands, after)
        return _ag_forward(g, srcs_g, lands_g)

    small4, w_in_e4, w_out_e = gathered(0, ag_token)
    me = 2 * lax.axis_index("x") + lax.axis_index("y")
    others = [k + (k >= me).astype(jnp.int32) for k in range(3)]
    where = jnp.stack([c_idx, me] + others).astype(jnp.int32)

    w_in_e_t = _even_rows_to_kernel(w_in_e4.reshape(3096, D))
    g_small = _split(small4.reshape(4, 32 * 128), SMALL_SHARDED_SHAPES)
    nw_full = _unshard(g_small[0], (2, 4, 1024), 2)
    wa_up = _unshard(g_small[1], (16, 256), 1)
    cw = _unshard(g_small[2], (4, 512), 1)
    cb, lba, lbx, lam = [_unshard(g, (512,), 0).reshape(1, 512) for g in g_small[3:]]
    nw = lambda layer, i: nw_full[layer, i].reshape(1, D)

    wa_pad = jnp.pad(wa_up, ((0, 128 - 16), (0, 0)))
    gla_ba = gla_b_a.reshape(1, 256)
    gla_nw = gla_norm_w.reshape(1, 512)
    fox_bpad = jnp.pad(fox_b_f.reshape(1, 8), ((0, 0), (FOX_LANE0, 128 - FOX_LANE0 - 8)))
    rbp = jnp.pad(rel_bias[0], ((0, 0), (0, REL_PAD - 257)))
    wa_bd = _block_diag(lru_w_a[0])
    wx_bd = _block_diag(lru_w_x[0])

    x0 = x[0]
    tgt = loss_target[0]

    h0 = _prenorm(x0, nw(0, 0), "prenorm_l0_mix")
    proj_e = _mm(h0, w_in_e_t, "nt", tm=1024, tn=640, name="mm_in_even")
    cat0, s_prev = _gla_fwd(proj_e, wa_pad, gla_ba, gla_nw)
    cum_r = _fox_gate_fwd(proj_e, fox_bpad)
    cum_c = cum_r[:, FOX_LANE0:FOX_LANE0 + 8].T
    cat0 = _fox_fwd(proj_e, cum_c, cat0)
    mix0 = _mm(cat0, w_out_e, "nn", tm=1024, tn=512, name="mm_out_even")
    x1 = _postnorm(x0, mix0, nw(0, 1), "postnorm_l0_mix")
    w_up, w_dn = gathered(1, x1)
    h1 = _prenorm(x1, nw(0, 2), "prenorm_l0_mlp")
    a0, r0 = _mm(h1, w_up, "nn", tm=1024, tn=1024, b_layer=0, relu_pair=True, name="mm_up_l0")
    d0 = _mm(a0, w_dn, "nn", tm=1024, tn=512, b_layer=0, name="mm_down_l0")
    x2 = _postnorm(x1, d0, nw(0, 3), "postnorm_l0_mlp")

    w_in_o, w_out_o = gathered(2, x2)
    h2 = _prenorm(x2, nw(1, 0), "prenorm_l1_mix")
    proj_o = _mm(h2, w_in_o, "nn", tm=1024, tn=640, name="mm_in_odd")
    bias_q = _bias_build(rbp)
    bias = bias_q.transpose(1, 0, 2)
    kvpad = jnp.pad(proj_o[:, 512:1536], ((CA_PAD, 0), (0, 0)))
    cat1 = _ca_fwd(proj_o, kvpad, bias)
    x_in = proj_o[:, 2048:2560]
    xs = jnp.stack([_shift_down(x_in, 3 - j) for j in range(4)])
    lru_a, lru_b = _lru_pre_fwd(xs, cw, cb, wa_bd, lba, wx_bd, lbx, lam)
    hh = _lru_scan_fwd(lru_a, lru_b)
    cat1 = _lru_post_fwd(hh, proj_o, cat1)
    mix1 = _mm(cat1, w_out_o, "nn", tm=1024, tn=512, name="mm_out_odd")
    x3 = _postnorm(x2, mix1, nw(1, 1), "postnorm_l1_mix")
    h3 = _prenorm(x3, nw(1, 2), "prenorm_l1_mlp")
    a1, r1 = _mm(h3, w_up, "nn", tm=1024, tn=1024, b_layer=1, relu_pair=True, name="mm_up_l1")
    d1 = _mm(a1, w_dn, "nn", tm=1024, tn=512, b_layer=1, name="mm_down_l1")
    x4 = _postnorm(x3, d1, nw(1, 3), "postnorm_l1_mlp")

    g4, loss_part = _loss_and_grad(x4, tgt)
    loss = lax.psum(loss_part[0, 0], ("x", "y", "c"))

    def rs_begin(swap, after, tag):
        gb, recv = _pair_swap_wait(swap, after, tag)
        return _a2a_start(_pair_add(gb, recv, where, tag), tag)

    def rs_end(started, after, tag):
        send_sems, recv_sems, p, q, _ = started
        p, q = _a2a_wait(send_sems, recv_sems, p, q, after, tag)
        return _handover(_sum_chips(p, q, where, tag), tag)

    gba = lax.dynamic_update_slice(lax.empty((4, GA_ROWS, D), BF16), jnp.zeros((4, GA_UP - GA_GAP, D), BF16),
                                   (0, GA_GAP, 0))
    dd1, dnw13 = _norm_bwd(d1, nw(1, 3), g4, None, "postnorm_l1_mlp_bwd")
    gba = _mm(a1, dd1, "tn", tm=512, tn=1024, into=(gba, 1024, GA_DN), name="mm_down_l1_dw")
    du1 = _mm(dd1, w_dn, "nt", tm=1024, tn=1024, b_layer=1, times2=r1, out_dtype=BF16, name="mm_down_l1_dx")
    gba = _mm(du1, h3, "tn", tm=512, tn=1024, into=(gba, 1024, GA_UP), name="mm_up_l1_dw")
    dh3 = _mm(du1, w_up, "nt", tm=1024, tn=512, b_layer=1, name="mm_up_l1_dx")
    g3, dnw12 = _norm_bwd(x3, nw(1, 2), dh3, g4, "prenorm_l1_mlp_bwd")
    dmix1, dnw11 = _norm_bwd(mix1, nw(1, 1), g3, None, "postnorm_l1_mix_bwd")
    gba = _mm(cat1, dmix1, "tn", tm=128, tn=1024, into=(gba, 256, GA_OUT_O), name="mm_out_odd_dw")
    dcat1 = _mm(dmix1, w_out_o, "nt", tm=1024, tn=512, name="mm_out_odd_dx")

    dq_c, dkpad, dvpad, dbias = _ca_bwd(proj_o, kvpad, bias, dcat1)
    g_rel = _bias_grad(jnp.pad(dbias.transpose(1, 0, 2), ((0, 0), (0, 0), (0, BIAS_W - CA_BAND))))[:, :257]
    dhh, dgate = _lru_post_bwd(hh, proj_o, dcat1)
    da_l, db_l = _lru_scan_bwd(lru_a, hh, dhh)
    dxs, g_cw, g_cb, g_wa_bd, g_lba, g_wx_bd, g_lbx, g_lam = _lru_pre_bwd(xs, cw, cb, wa_bd, lba, wx_bd, lbx, lam, da_l, db_l)
    dx_in = _conv_dx(jnp.stack([_shift_up(dxs[j], 3 - j) for j in range(4)]))
    dproj_o = jnp.concatenate([dq_c, dkpad[CA_PAD:], dvpad[CA_PAD:], dgate, dx_in], axis=1).astype(BF16)
    gba = _mm(dproj_o, h2, "tn", tm=128, tn=1024, into=(gba, 640, GA_IN_O), name="mm_in_odd_dw")
    swap_a = _pair_swap_start(gba, "a")
    dh2 = _mm(dproj_o, w_in_o, "nt", tm=1024, tn=512, name="mm_in_odd_dx")
    g2, dnw10 = _norm_bwd(x2, nw(1, 0) + swap_a[4][0, 0], dh2, g3, "prenorm_l1_mix_bwd")
    rs_a = rs_begin(swap_a, g2, "a")

    gbb = lax.empty((4, GB_ROWS, D), BF16)
    dd0, dnw03 = _norm_bwd(d0, nw(0, 3) + rs_a[4][0, 0], g2, None, "postnorm_l0_mlp_bwd")
    gbb = _mm(a0, dd0, "tn", tm=512, tn=1024, into=(gbb, 1024, GB_DN), name="mm_down_l0_dw")
    du0 = _mm(dd0, w_dn, "nt", tm=1024, tn=1024, b_layer=0, times2=r0, out_dtype=BF16, name="mm_down_l0_dx")
    gbb = _mm(du0, h1, "tn", tm=512, tn=1024, into=(gbb, 1024, GB_UP), name="mm_up_l0_dw")
    swap_b = _pair_swap_start(gbb, "b")
    dh1 = _mm(du0, w_up, "nt", tm=1024, tn=512, b_layer=0, name="mm_up_l0_dx")
    g1, dnw02 = _norm_bwd(x1, nw(0, 2) + swap_b[4][0, 0], dh1, g2, "prenorm_l0_mlp_bwd")
    rs_b = rs_begin(swap_b, g1, "b")
    dmix0, dnw01 = _norm_bwd(mix0, nw(0, 1) + rs_b[4][0, 0], g1, None, "postnorm_l0_mix_bwd")
    gbc = lax.empty((4, GC_ROWS, D), BF16)
    gbc = _mm(cat0, dmix0, "tn", tm=128, tn=1024, into=(gbc, 256, GC_OUT_E), name="mm_out_even_dw")
    dcat0 = _mm(dmix0, w_out_e, "nt", tm=1024, tn=512, name="mm_out_even_dx")

    dq_g, dk_g, dv_g, dr_g, daux_g, g_wa_pad, g_gla_ba, g_gla_nw = _gla_bwd(proj_e, s_prev, wa_pad, gla_ba, gla_nw, dcat0)
    dq_f, dk_f, dv_f, dccol = _fox_bwd(proj_e, cum_c, dcat0)
    dccol_t = jnp.pad(dccol.sum(axis=0).T, ((0, 0), (FOX_LANE0, 128 - FOX_LANE0 - 8)))
    daux, g_fox_bpad = _fox_gate_bwd(proj_e, fox_bpad, dccol_t, daux_g)
    dproj_e = jnp.concatenate([dq_g, dk_g, dv_g, dr_g, dq_f, dk_f, dv_f, daux], axis=1).astype(BF16)
    gt_in_e = _mm(dproj_e, h0, "tn", tm=640, tn=1024, out_dtype=BF16, name="mm_in_even_dw")
    dh0 = _mm(dproj_e, w_in_e_t, "nn", tm=1024, tn=512, name="mm_in_even_dx")
    grad_x, dnw00 = _norm_bwd(x0, nw(0, 0), dh0, g1, "prenorm_l0_mix_bwd")

    g_norm = jnp.stack([jnp.concatenate([dnw00, dnw01, dnw02, dnw03]), jnp.concatenate([dnw10, dnw11, dnw12, dnw13])])
    sharded = [(g_norm, 2), (g_wa_pad[:16], 1), (g_cw, 1), (g_cb[0], 0), (g_lba[0], 0), (g_lbx[0], 0), (g_lam[0], 0)]
    replicated = [g_gla_ba[0], g_gla_nw[0], g_fox_bpad[0, FOX_LANE0:FOX_LANE0 + 8], g_rel, _diag_blocks(g_wa_bd),
                  _diag_blocks(g_wx_bd)]
    small4 = jnp.concatenate([_shard_major(g, ax) for g, ax in sharded]
                             + [jnp.broadcast_to(g.reshape(1, -1), (4, g.size)) for g in replicated], axis=1)
    n_small = small4.shape[1]
    small_rows = GC_ROWS - GC_TAIL - 774
    small4 = jnp.pad(small4, ((0, 0), (0, small_rows * D - n_small))).reshape(4, small_rows, D)
    gt_rows = jnp.concatenate([gt_in_e[:1536], gt_in_e[3072:3088], gt_in_e[1536:3072], gt_in_e[3088:3096]], axis=0)
    tail = jnp.concatenate([gt_rows.reshape(4, 774, D), small4.astype(BF16)], axis=1)
    gbc = lax.dynamic_update_slice(gbc, tail, (0, GC_TAIL, 0))
    swap_c = _pair_swap_start(gbc, "c")
    rs_c = rs_begin(swap_c, swap_c[4], "c")

    red_a = rs_end(rs_a, rs_c[4], "a")
    red_b = rs_end(rs_b, red_a, "b")
    red_c = rs_end(rs_c, red_b, "c")

    g_small = _split(red_c[GC_TAIL + 774:].reshape(-1)[:n_small], SMALL_SHARDED_SHAPES + REPL_SHAPES)
    g_of = dict(zip(["norm_w", "gla_w_a_up", "conv_w", "conv_b", "lru_b_a", "lru_b_x", "lru_lambda", "gla_b_a",
                     "gla_norm_w", "fox_b_f", "rel_bias", "lru_w_a", "lru_w_x"], g_small))
    g_of.update(w_in_even=red_c[GC_TAIL:GC_TAIL + 774])
    g_in_place = dict(w_mlp_up=([(red_b, GB_UP, True), (red_a, GA_UP, True)], 256),
                      w_mlp_down=([(red_b, GB_DN, False), (red_a, GA_DN, False)], 256),
                      w_in_odd=([(red_a, GA_IN_O, True)], 256),
                      w_out_odd=([(red_a, GA_OUT_O, False)], 128),
                      w_out_even=([(red_c, GC_OUT_E, False)], 256))

    names = ["norm_w", "w_in_even", "gla_w_a_up", "gla_b_a", "gla_norm_w", "fox_b_f", "w_out_even", "w_in_odd", "rel_bias",
             "conv_w", "conv_b", "lru_w_a", "lru_b_a", "lru_w_x", "lru_b_x", "lru_lambda", "w_out_odd", "w_mlp_up",
             "w_mlp_down"]
    w_of = dict(norm_w=norm_w, w_in_even=w_in_even, gla_w_a_up=gla_w_a_up, gla_b_a=gla_b_a, gla_norm_w=gla_norm_w,
                fox_b_f=fox_b_f, w_out_even=w_out_even, w_in_odd=w_in_odd, rel_bias=rel_bias, conv_w=conv_w, conv_b=conv_b,
                lru_w_a=lru_w_a, lru_b_a=lru_b_a, lru_w_x=lru_w_x, lru_b_x=lru_b_x, lru_lambda=lru_lambda,
                w_out_odd=w_out_odd, w_mlp_up=w_mlp_up, w_mlp_down=w_mlp_down)
    m_of = dict(norm_w=m_norm_w, w_in_even=m_w_in_even, gla_w_a_up=m_gla_w_a_up, gla_b_a=m_gla_b_a,
                gla_norm_w=m_gla_norm_w, fox_b_f=m_fox_b_f, w_out_even=m_w_out_even, w_in_odd=m_w_in_odd,
                rel_bias=m_rel_bias, conv_w=m_conv_w, conv_b=m_conv_b, lru_w_a=m_lru_w_a, lru_b_a=m_lru_b_a,
                lru_w_x=m_lru_w_x, lru_b_x=m_lru_b_x, lru_lambda=m_lru_lambda, w_out_odd=m_w_out_odd,
                w_mlp_up=m_w_mlp_up, w_mlp_down=m_w_mlp_down)
    v_of = dict(norm_w=v_norm_w, w_in_even=v_w_in_even, gla_w_a_up=v_gla_w_a_up, gla_b_a=v_gla_b_a,
                gla_norm_w=v_gla_norm_w, fox_b_f=v_fox_b_f, w_out_even=v_w_out_even, w_in_odd=v_w_in_odd,
                rel_bias=v_rel_bias, conv_w=v_conv_w, conv_b=v_conv_b, lru_w_a=v_lru_w_a, lru_b_a=v_lru_b_a,
                lru_w_x=v_lru_w_x, lru_b_x=v_lru_b_x, lru_lambda=v_lru_lambda, w_out_odd=v_w_out_odd,
                w_mlp_up=v_w_mlp_up, w_mlp_down=v_w_mlp_down)
    grads, deltas, new_ms, new_vs = [], [], [], []
    for n in names:
        w = w_of[n]
        if n in g_in_place:
            sources, tr = g_in_place[n]
            g, d, mn, vn = _adamw_from(w, m_of[n], v_of[n], sources, tr, "adamw_" + n)
            grads.append(g)
            deltas.append(d)
            new_ms.append(mn)
            new_vs.append(vn)
            continue
        if n == "w_in_even":
            to_view = lambda a: a[0].T
            from_view = lambda a: a.T[None]
        else:
            view = w.shape if w.ndim <= 3 else w.shape[-3:]
            to_view = lambda a, view=view: a.reshape(view)
            from_view = lambda a, w=w: a.reshape(w.shape)
        g = g_of[n] if n == "w_in_even" else to_view(g_of[n])
        d, mn, vn = _adamw(to_view(w), g, to_view(m_of[n]), to_view(v_of[n]), "adamw_" + n)
        grads.append(from_view(g))
        deltas.append(from_view(d))
        new_ms.append(from_view(mn))
        new_vs.append(from_view(vn))

    return (loss, grad_x.reshape(1, T, D), *grads, *deltas, *new_ms, *new_vs)
```

```python
import functools

import jax
import jax.numpy as jnp
from jax import lax
from jax.experimental import pallas as pl
from jax.experimental.pallas import tpu as pltpu

F32 = jnp.float32
BF16 = jnp.bfloat16
MESH = pl.DeviceIdType.MESH

T = 2048
D = 1024
DFF = 4096
EPS = 1e-6
CHUNK = 64
NCHUNK = T // CHUNK
PE = 3200
PO = 2560
AUX_BLK = 3072 // 128
FOX_LANE0 = 16
GLA_SCALE = 64 ** -0.5
ATT_SCALE = 64 ** -0.5
NEG = float(jnp.finfo(jnp.float32).min)
CA_BAND = 576
CA_PAD = 512
REL_PAD = 384

VMEM_LIMIT = 48 * 1024 * 1024

ADAM_LR, ADAM_B1, ADAM_B2, ADAM_EPS, ADAM_WD, ADAM_STEP = 0.001, 0.9, 0.999, 1e-08, 0.01, 10

GA_ROWS, GA_IN_O, GA_OUT_O, GA_GAP, GA_UP, GA_DN = 3072, 0, 640, 896, 1024, 2048
GB_ROWS, GB_UP, GB_DN = 2048, 0, 1024
GC_ROWS, GC_OUT_E, GC_TAIL = 1152, 0, 256

_DIMS = {"nn": (((1,), (0,)), ((), ())), "nt": (((1,), (1,)), ((), ())), "tn": (((0,), (0,)), ((), ()))}


def _cp(sem, **kw):
    return pltpu.CompilerParams(dimension_semantics=sem, vmem_limit_bytes=VMEM_LIMIT, **kw)


def _dot(a, b, mode):
    return lax.dot_general(a.astype(BF16), b.astype(BF16), _DIMS[mode], preferred_element_type=F32)


@functools.partial(jax.custom_vjp, nondiff_argnums=(2,))
def bdot(a, b, mode):
    return _dot(a, b, mode)


def _bdot_fwd(a, b, mode):
    return _dot(a, b, mode), (a, b)


def _bdot_bwd(mode, res, g):
    a, b = res
    if mode == "nn":
        da, db = _dot(g, b, "nt"), _dot(a, g, "tn")
    elif mode == "nt":
        da, db = _dot(g, b, "nn"), _dot(g, a, "tn")
    else:
        da, db = _dot(b, g, "nt"), _dot(a, g, "nn")
    return da.astype(a.dtype), db.astype(b.dtype)


bdot.defvjp(_bdot_fwd, _bdot_bwd)


def _hdot_raw(a, b, mode):
    return lax.dot_general(a, b, _DIMS[mode], precision=lax.Precision.HIGHEST, preferred_element_type=F32)


@functools.partial(jax.custom_vjp, nondiff_argnums=(2,))
def hdot(a, b, mode):
    return _hdot_raw(a, b, mode)


def _hdot_fwd(a, b, mode):
    return _hdot_raw(a, b, mode), (a, b)


def _hdot_bwd(mode, res, g):
    a, b = res
    if mode == "nn":
        return _hdot_raw(g, b, "nt"), _hdot_raw(a, g, "tn")
    if mode == "nt":
        return _hdot_raw(g, b, "nn"), _hdot_raw(g, a, "tn")
    return _hdot_raw(b, g, "nt"), _hdot_raw(a, g, "nn")


hdot.defvjp(_hdot_fwd, _hdot_bwd)


def _log_sigmoid(x):
    return jnp.minimum(x, 0.0) - jnp.log(1.0 + jnp.exp(-jnp.abs(x)))


def _sigmoid(x):
    return 1.0 / (1.0 + jnp.exp(-x))


def _expm1(x):
    series = x * (1.0 + x * 0.5 * (1.0 + x * (1.0 / 3.0) * (1.0 + x * 0.25)))
    return jnp.where(jnp.abs(x) < 0.03, series, jnp.exp(x) - 1.0)


def _gelu_tanh(x):
    return 0.5 * x * (1.0 + jnp.tanh(0.7978845608028654 * (x + 0.044715 * x * x * x)))


def _iota(shape, dim):
    return lax.broadcasted_iota(jnp.int32, shape, dim)


def _mm(a, b, mode, *, tm, tn, tk=None, out_dtype=F32, name, b_layer=None, into=None, relu_pair=False, times2=None):
    b2 = b.shape[-2:]
    if mode == "nn":
        (m, k), n = a.shape, b2[1]
    elif mode == "nt":
        (m, k), n = a.shape, b2[0]
    else:
        (k, m), n = a.shape, b2[1]
    tk = k if tk is None else tk
    assert m % tm == 0 and n % tn == 0 and k % tk == 0, (name, a.shape, b.shape)
    nk = k // tk
    a_spec = {"nn": pl.BlockSpec((tm, tk), lambda i, j, kk: (i, kk)),
              "nt": pl.BlockSpec((tm, tk), lambda i, j, kk: (i, kk)),
              "tn": pl.BlockSpec((tk, tm), lambda i, j, kk: (kk, i))}[mode]
    b_blk = {"nn": (tk, tn), "nt": (tn, tk), "tn": (tk, tn)}[mode]
    b_idx = {"nn": lambda i, j, kk: (kk, j), "nt": lambda i, j, kk: (j, kk), "tn": lambda i, j, kk: (kk, j)}[mode]
    if b_layer is None:
        b_spec = pl.BlockSpec(b_blk, b_idx)
    else:
        b_spec = pl.BlockSpec((None,) + b_blk, lambda i, j, kk: (b_layer,) + b_idx(i, j, kk))

    tile = pl.BlockSpec((tm, tn), lambda i, j, kk: (i, j))
    if into is not None:
        buf, per_slot, row_off = into
        assert m == 4 * per_slot and per_slot % tm == 0 and row_off % tm == 0 and buf.shape[2] == n, (name, buf.shape)
        bps = per_slot // tm
        out_specs = pl.BlockSpec((None, tm, tn), lambda i, j, kk: (i // bps, row_off // tm + i % bps, j))
        out_shape = jax.ShapeDtypeStruct(buf.shape, buf.dtype)
        extra_in, extra_specs, aliases = [buf], [pl.BlockSpec(memory_space=pl.ANY)], {2: 0}
        finish = lambda acc, extra: [acc.astype(buf.dtype)]
    elif relu_pair:
        out_specs = (tile, tile)
        out_shape = (jax.ShapeDtypeStruct((m, n), BF16),) * 2
        extra_in, extra_specs, aliases = [], [], {}

        def finish(acc, extra):
            r = jnp.maximum(acc, 0.0)
            return [(r * r).astype(BF16), r.astype(BF16)]
    elif times2 is not None:
        out_specs = tile
        out_shape = jax.ShapeDtypeStruct((m, n), out_dtype)
        extra_in, extra_specs, aliases = [times2], [tile], {}
        finish = lambda acc, extra: [(acc * (2.0 * extra[...].astype(F32))).astype(out_dtype)]
    else:
        out_specs = tile
        out_shape = jax.ShapeDtypeStruct((m, n), out_dtype)
        extra_in, extra_specs, aliases = [], [], {}
        finish = lambda acc, extra: [acc.astype(out_dtype)]
    n_out = 2 if relu_pair else 1

    def body(*refs):
        a_ref, b_ref = refs[0], refs[1]
        extra = refs[2] if extra_in else None
        o_refs = refs[2 + len(extra_in):2 + len(extra_in) + n_out]

        def store(acc):
            for o_ref, val in zip(o_refs, finish(acc, extra)):
                o_ref[...] = val

        if nk == 1:
            store(_dot(a_ref[...], b_ref[...], mode))
            return
        acc_ref = refs[-1]
        kk = pl.program_id(2)

        @pl.when(kk == 0)
        def _():
            acc_ref[...] = jnp.zeros_like(acc_ref)

        acc_ref[...] += _dot(a_ref[...], b_ref[...], mode)

        @pl.when(kk == nk - 1)
        def _():
            store(acc_ref[...])

    return pl.pallas_call(
        body, name=name, grid=(m // tm, n // tn, nk),
        in_specs=[a_spec, b_spec] + extra_specs,
        out_specs=out_specs, out_shape=out_shape,
        scratch_shapes=[pltpu.VMEM((tm, tn), F32)] if nk > 1 else [],
        input_output_aliases=aliases,
        compiler_params=_cp(("parallel", "parallel", "arbitrary")),
    )(a, b, *extra_in)


ROWS = 256


def _prenorm(x, w, name):
    def body(x_ref, w_ref, o_ref):
        xv = x_ref[...]
        r = lax.rsqrt(jnp.mean(xv * xv, axis=-1, keepdims=True) + EPS)
        o_ref[...] = (xv * r * w_ref[...]).astype(BF16)

    return pl.pallas_call(
        body, name=name, grid=(T // ROWS,),
        in_specs=[pl.BlockSpec((ROWS, D), lambda i: (i, 0)), pl.BlockSpec((1, D), lambda i: (0, 0))],
        out_specs=pl.BlockSpec((ROWS, D), lambda i: (i, 0)),
        out_shape=jax.ShapeDtypeStruct((T, D), BF16),
        compiler_params=_cp(("parallel",)),
    )(x, w)


def _rms(z):
    return lax.rsqrt(jnp.mean(z * z, axis=-1, keepdims=True) + EPS)


def _rms_bwd(z, w, dy):
    r = _rms(z)
    wdy = dy * w
    dz = r * wdy - z * (r * r * r) * jnp.mean(z * wdy, axis=-1, keepdims=True)
    return dz, jnp.sum(dy * z * r, axis=0, keepdims=True)


_ROW = pl.BlockSpec((ROWS, D), lambda i: (i, 0))
_VEC = pl.BlockSpec((1, D), lambda i: (0, 0))


def _post_pre_fwd(x, z, w_post, w_pre, name):
    def body(x_ref, z_ref, wp_ref, wn_ref, x_out, h_out):
        zv = z_ref[...]
        xn = x_ref[...] + zv * _rms(zv) * wp_ref[...]
        x_out[...] = xn
        h_out[...] = (xn * _rms(xn) * wn_ref[...]).astype(BF16)

    return pl.pallas_call(
        body, name=name, grid=(T // ROWS,), in_specs=[_ROW, _ROW, _VEC, _VEC], out_specs=(_ROW, _ROW),
        out_shape=(jax.ShapeDtypeStruct((T, D), F32), jax.ShapeDtypeStruct((T, D), BF16)),
        compiler_params=_cp(("parallel",)),
    )(x, z, w_post, w_pre)


def _post_loss(x, z, w_post, tgt):
    def body(x_ref, z_ref, w_ref, t_ref, g_ref, l_ref, dz_ref, dw_ref):
        @pl.when(pl.program_id(0) == 0)
        def _():
            l_ref[...] = jnp.zeros_like(l_ref)
            dw_ref[...] = jnp.zeros_like(dw_ref)

        zv = z_ref[...]
        e = x_ref[...] + zv * _rms(zv) * w_ref[...] - t_ref[...]
        g = e * (1.0 / D)
        g_ref[...] = g
        l_ref[...] += jnp.sum(e * e) * (0.5 / D)
        dz, dw = _rms_bwd(zv, w_ref[...], g)
        dz_ref[...] = dz.astype(BF16)
        dw_ref[...] += dw

    return pl.pallas_call(
        body, name="postnorm_loss", grid=(T // ROWS,), in_specs=[_ROW, _ROW, _VEC, _ROW],
        out_specs=(_ROW, pl.BlockSpec((1, 128), lambda i: (0, 0)), _ROW, _VEC),
        out_shape=(jax.ShapeDtypeStruct((T, D), F32), jax.ShapeDtypeStruct((1, 128), F32),
                   jax.ShapeDtypeStruct((T, D), BF16), jax.ShapeDtypeStruct((1, D), F32)),
        compiler_params=_cp(("arbitrary",)),
    )(x, z, w_post, tgt)


def _pre_post_bwd(x, w_pre, dh, add, z, w_post, name):
    def body(x_ref, wn_ref, dh_ref, add_ref, z_ref, wp_ref, g_ref, dz_ref, dwn_ref, dwp_ref):
        @pl.when(pl.program_id(0) == 0)
        def _():
            dwn_ref[...] = jnp.zeros_like(dwn_ref)
            dwp_ref[...] = jnp.zeros_like(dwp_ref)

        dx, dwn = _rms_bwd(x_ref[...], wn_ref[...], dh_ref[...])
        g = dx + add_ref[...]
        g_ref[...] = g
        dz, dwp = _rms_bwd(z_ref[...], wp_ref[...], g)
        dz_ref[...] = dz.astype(BF16)
        dwn_ref[...] += dwn
        dwp_ref[...] += dwp

    return pl.pallas_call(
        body, name=name, grid=(T // ROWS,), in_specs=[_ROW, _VEC, _ROW, _ROW, _ROW, _VEC],
        out_specs=(_ROW, _ROW, _VEC, _VEC),
        out_shape=(jax.ShapeDtypeStruct((T, D), F32), jax.ShapeDtypeStruct((T, D), BF16),
                   jax.ShapeDtypeStruct((1, D), F32), jax.ShapeDtypeStruct((1, D), F32)),
        compiler_params=_cp(("arbitrary",)),
    )(x, w_pre, dh, add, z, w_post)


def _norm_bwd(z, w, dy, add, name):
    has_add = add is not None

    def body(*refs):
        if has_add:
            z_ref, w_ref, dy_ref, add_ref, dz_ref, dw_ref = refs
        else:
            z_ref, w_ref, dy_ref, dz_ref, dw_ref = refs
        i = pl.program_id(0)

        @pl.when(i == 0)
        def _():
            dw_ref[...] = jnp.zeros_like(dw_ref)

        zv = z_ref[...].astype(F32)
        dyv = dy_ref[...]
        r = lax.rsqrt(jnp.mean(zv * zv, axis=-1, keepdims=True) + EPS)
        wdy = dyv * w_ref[...]
        dz = r * wdy - zv * (r * r * r) * jnp.mean(zv * wdy, axis=-1, keepdims=True)
        if has_add:
            dz = dz + add_ref[...]
        dz_ref[...] = dz.astype(dz_ref.dtype)
        dw_ref[...] += jnp.sum(dyv * zv * r, axis=0, keepdims=True)

    row = pl.BlockSpec((ROWS, D), lambda i: (i, 0))
    vec = pl.BlockSpec((1, D), lambda i: (0, 0))
    ins = [z, w, dy] + ([add] if has_add else [])
    dz_dtype = F32 if has_add else BF16
    return pl.pallas_call(
        body, name=name, grid=(T // ROWS,),
        in_specs=[row, vec, row] + ([row] if has_add else []),
        out_specs=(row, vec),
        out_shape=(jax.ShapeDtypeStruct((T, D), dz_dtype), jax.ShapeDtypeStruct((1, D), F32)),
        compiler_params=_cp(("arbitrary",)),
    )(*ins)


def _adamw_math(w, g, m, v):
    c1 = 1.0 - ADAM_B1 ** ADAM_STEP
    c2 = 1.0 - ADAM_B2 ** ADAM_STEP
    mn = ADAM_B1 * m + (1.0 - ADAM_B1) * g
    vn = ADAM_B2 * v + (1.0 - ADAM_B2) * (g * g)
    return -ADAM_LR * ((mn / c1) / (jnp.sqrt(vn / c2) + ADAM_EPS) + ADAM_WD * w), mn, vn


def _adamw_from(w, m, v, sources, tr, name):
    layers, rows, cols = w.shape
    assert len(sources) == layers and rows % tr == 0, (name, w.shape)
    g_specs = []
    for buf, row0, transposed in sources:
        if transposed:
            assert row0 % cols == 0 and buf.shape[1] == rows, (name, row0)
            g_specs.append(pl.BlockSpec((cols, tr), lambda l, i, b=row0 // cols: (b, i)))
        else:
            assert row0 % tr == 0 and buf.shape[1] == cols, (name, row0)
            g_specs.append(pl.BlockSpec((tr, cols), lambda l, i, b=row0 // tr: (b + i, 0)))

    def body(*refs):
        w_ref, m_ref, v_ref = refs[:3]
        g_refs = refs[3:3 + layers]
        g_out, d_ref, mo_ref, vo_ref = refs[3 + layers:]
        gs = [r[...].T if src[2] else r[...] for r, src in zip(g_refs, sources)]
        g = gs[0] if layers == 1 else jnp.where(pl.program_id(0) == 0, gs[0], gs[1])
        g_out[...] = g
        d_ref[...], mo_ref[...], vo_ref[...] = _adamw_math(w_ref[...], g, m_ref[...], v_ref[...])

    blk = pl.BlockSpec((None, tr, cols), lambda l, i: (l, i, 0))
    sds = jax.ShapeDtypeStruct(w.shape, F32)
    return pl.pallas_call(body, name=name, grid=(layers, rows // tr), in_specs=[blk] * 3 + g_specs,
                          out_specs=(blk,) * 4, out_shape=(sds,) * 4,
                          compiler_params=_cp(("parallel", "parallel")))(w, m, v, *[s[0] for s in sources])


def _adamw(w, g, m, v, name):
    lead = w.shape[:-2]
    assert len(lead) <= 1 and g.shape == w.shape, (name, w.shape, g.shape)
    rows, cols = w.shape[-2:]
    if rows <= 512:
        tr, tc = rows, cols
    elif rows % 256 == 0:
        tr, tc = 256, cols
    else:
        tr, tc = rows, 256
    assert rows % tr == 0 and cols % tc == 0, (name, w.shape)
    c1 = 1.0 - ADAM_B1 ** ADAM_STEP
    c2 = 1.0 - ADAM_B2 ** ADAM_STEP

    def body(w_ref, g_ref, m_ref, v_ref, d_ref, mo_ref, vo_ref):
        gv = g_ref[...]
        mn = ADAM_B1 * m_ref[...] + (1.0 - ADAM_B1) * gv
        vn = ADAM_B2 * v_ref[...] + (1.0 - ADAM_B2) * (gv * gv)
        m_hat = mn / c1
        v_hat = vn / c2
        d_ref[...] = -ADAM_LR * (m_hat / (jnp.sqrt(v_hat) + ADAM_EPS) + ADAM_WD * w_ref[...])
        mo_ref[...] = mn
        vo_ref[...] = vn

    if lead:
        grid = (lead[0], rows // tr, cols // tc)
        blk = pl.BlockSpec((None, tr, tc), lambda l, i, j: (l, i, j))
    else:
        grid = (rows // tr, cols // tc)
        blk = pl.BlockSpec((tr, tc), lambda i, j: (i, j))
    sds = jax.ShapeDtypeStruct(w.shape, F32)
    return pl.pallas_call(body, name=name, grid=grid, in_specs=[blk] * 4, out_specs=(blk,) * 3,
                          out_shape=(sds,) * 3, compiler_params=_cp(("parallel",) * len(grid)))(w, g, m, v)


def _gla_consts():
    ltri = (_iota((CHUNK, CHUNK), 0) >= _iota((CHUNK, CHUNK), 1)).astype(F32)
    ones_c = jnp.ones((CHUNK, 128), F32)
    mask = (_iota((256, 512), 0) // 64 == _iota((256, 512), 1) // 128).astype(F32)
    return ltri, ones_c, mask


def _gla_chunk(consts, q, k, v, r, aux, s_prev, wa, ba, nw):
    ltri, ones_c, mask = consts
    la = _log_sigmoid(bdot(aux, wa, "nn") + ba) * (1.0 / 16.0)
    cum = hdot(ltri, la, "nn")
    total = jnp.sum(la, axis=0, keepdims=True)
    k_dec = k * jnp.exp(total - cum)
    inc = bdot(k_dec, v, "tn") * mask
    dec = jnp.exp(hdot(la, ones_c, "tn"))
    dec = jnp.concatenate([dec, dec, dec, dec], axis=1)
    s_new = dec * s_prev + inc
    o = bdot(q * GLA_SCALE, s_new, "nn")
    parts = []
    for h in range(4):
        oh = o[:, h * 128:(h + 1) * 128]
        parts.append(oh * lax.rsqrt(jnp.mean(oh * oh, axis=-1, keepdims=True) + EPS))
    on = jnp.concatenate(parts, axis=1)
    return s_new, on * nw * (r * _sigmoid(r))


GLA_PER_STEP = 4
GLA_ROWS = GLA_PER_STEP * CHUNK
GLA_STEPS = NCHUNK // GLA_PER_STEP


def _gla_specs(cmap):
    return [pl.BlockSpec((GLA_ROWS, 256), lambda c: (cmap(c), 0)),
            pl.BlockSpec((GLA_ROWS, 256), lambda c: (cmap(c), 1)),
            pl.BlockSpec((GLA_ROWS, 512), lambda c: (cmap(c), 1)),
            pl.BlockSpec((GLA_ROWS, 512), lambda c: (cmap(c), 2)),
            pl.BlockSpec((GLA_ROWS, 128), lambda c: (cmap(c), AUX_BLK))]


def _gla_fwd(proj, wa, ba, nw):
    def body(q_ref, k_ref, v_ref, r_ref, aux_ref, wa_ref, ba_ref, nw_ref, o_ref, sp_ref, s_ref):
        @pl.when(pl.program_id(0) == 0)
        def _():
            s_ref[...] = jnp.zeros_like(s_ref)

        s = s_ref[...]
        consts = _gla_consts()
        outs, states = [], []
        for i in range(GLA_PER_STEP):
            rows = slice(i * CHUNK, (i + 1) * CHUNK)
            states.append(s)
            s, out = _gla_chunk(consts, q_ref[rows, :], k_ref[rows, :], v_ref[rows, :], r_ref[rows, :], aux_ref[rows, :],
                                s, wa_ref[...], ba_ref[...], nw_ref[...])
            outs.append(out)
        s_ref[...] = s
        for i in range(GLA_PER_STEP):
            o_ref[i * CHUNK:(i + 1) * CHUNK, :] = outs[i]
            sp_ref[i] = states[i]

    full = lambda shape: pl.BlockSpec(shape, lambda c: (0,) * len(shape))
    return pl.pallas_call(
        body, name="gla_fwd", grid=(GLA_STEPS,),
        in_specs=_gla_specs(lambda c: c) + [full((128, 256)), full((1, 256)), full((1, 512))],
        out_specs=(pl.BlockSpec((GLA_ROWS, 512), lambda c: (c, 0)),
                   pl.BlockSpec((GLA_PER_STEP, 256, 512), lambda c: (c, 0, 0))),
        out_shape=(jax.ShapeDtypeStruct((T, D), F32), jax.ShapeDtypeStruct((NCHUNK, 256, 512), F32)),
        scratch_shapes=[pltpu.VMEM((256, 512), F32)],
        compiler_params=_cp(("arbitrary",)),
    )(proj, proj, proj, proj, proj, wa, ba, nw)


def _gla_bwd(proj, s_prev_all, wa, ba, nw, dcat):
    rev = lambda c: GLA_STEPS - 1 - c

    def body(q_ref, k_ref, v_ref, r_ref, aux_ref, sp_ref, wa_ref, ba_ref, nw_ref, do_ref,
             dq_ref, dk_ref, dv_ref, dr_ref, daux_ref, dwa_ref, dba_ref, dnw_ref, ds_ref):
        @pl.when(pl.program_id(0) == 0)
        def _():
            ds_ref[...] = jnp.zeros_like(ds_ref)
            dwa_ref[...] = jnp.zeros_like(dwa_ref)
            dba_ref[...] = jnp.zeros_like(dba_ref)
            dnw_ref[...] = jnp.zeros_like(dnw_ref)

        fn = functools.partial(_gla_chunk, _gla_consts())
        ds = ds_ref[...]
        dwa, dba, dnw = dwa_ref[...], dba_ref[...], dnw_ref[...]
        grads = {}
        for i in reversed(range(GLA_PER_STEP)):
            rows = slice(i * CHUNK, (i + 1) * CHUNK)
            _, vjp = jax.vjp(fn, q_ref[rows, :], k_ref[rows, :], v_ref[rows, :], r_ref[rows, :], aux_ref[rows, :],
                             sp_ref[i], wa_ref[...], ba_ref[...], nw_ref[...])
            *grads[i], ds, dwa_i, dba_i, dnw_i = vjp((ds, do_ref[rows, :]))
            dwa, dba, dnw = dwa + dwa_i, dba + dba_i, dnw + dnw_i
        ds_ref[...] = ds
        dwa_ref[...] = dwa
        dba_ref[...] = dba
        dnw_ref[...] = dnw
        for i in range(GLA_PER_STEP):
            rows = slice(i * CHUNK, (i + 1) * CHUNK)
            for ref, g in zip((dq_ref, dk_ref, dv_ref, dr_ref, daux_ref), grads[i]):
                ref[rows, :] = g

    full = lambda shape: pl.BlockSpec(shape, lambda c: (0,) * len(shape))
    blk = lambda w: pl.BlockSpec((GLA_ROWS, w), lambda c: (rev(c), 0))
    sds = lambda *s: jax.ShapeDtypeStruct(s, F32)
    return pl.pallas_call(
        body, name="gla_bwd", grid=(GLA_STEPS,),
        in_specs=_gla_specs(rev) + [pl.BlockSpec((GLA_PER_STEP, 256, 512), lambda c: (rev(c), 0, 0)),
                                    full((128, 256)), full((1, 256)), full((1, 512)), blk(512)],
        out_specs=(blk(256), blk(256), blk(512), blk(512), blk(128), full((128, 256)), full((1, 256)), full((1, 512))),
        out_shape=(sds(T, 256), sds(T, 256), sds(T, 512), sds(T, 512), sds(T, 128),
                   sds(128, 256), sds(1, 256), sds(1, 512)),
        scratch_shapes=[pltpu.VMEM((256, 512), F32)],
        compiler_params=_cp(("arbitrary",)),
    )(proj, proj, proj, proj, proj, s_prev_all, wa, ba, nw, dcat)


GATE_ROWS = 128


def _fox_gate_block(ltri, aux, bpad, carry):
    lf = _log_sigmoid(aux + bpad)
    cum = hdot(ltri, lf, "nn") + carry
    return cum, carry + jnp.sum(lf, axis=0, keepdims=True)


def _gate_ltri():
    return (_iota((GATE_ROWS, GATE_ROWS), 0) >= _iota((GATE_ROWS, GATE_ROWS), 1)).astype(F32)


def _fox_gate_fwd(proj, bpad):
    def body(aux_ref, b_ref, cum_ref, carry_ref):
        i = pl.program_id(0)

        @pl.when(i == 0)
        def _():
            carry_ref[...] = jnp.zeros_like(carry_ref)

        cum, carry = _fox_gate_block(_gate_ltri(), aux_ref[...], b_ref[...], carry_ref[...])
        cum_ref[...] = cum
        carry_ref[...] = carry

    return pl.pallas_call(
        body, name="fox_gate_fwd", grid=(T // GATE_ROWS,),
        in_specs=[pl.BlockSpec((GATE_ROWS, 128), lambda i: (i, AUX_BLK)), pl.BlockSpec((1, 128), lambda i: (0, 0))],
        out_specs=pl.BlockSpec((GATE_ROWS, 128), lambda i: (i, 0)),
        out_shape=jax.ShapeDtypeStruct((T, 128), F32),
        scratch_shapes=[pltpu.VMEM((1, 128), F32)],
        compiler_params=_cp(("arbitrary",)),
    )(proj, bpad)


def _fox_gate_bwd(proj, bpad, dccol_t, daux_gla):
    nb = T // GATE_ROWS
    rev = lambda i: nb - 1 - i

    def body(aux_ref, b_ref, dc_ref, dg_ref, daux_ref, db_ref, dcarry_ref):
        i = pl.program_id(0)

        @pl.when(i == 0)
        def _():
            dcarry_ref[...] = jnp.zeros_like(dcarry_ref)
            db_ref[...] = jnp.zeros_like(db_ref)

        dcum = dc_ref[...]
        fn = functools.partial(_fox_gate_block, _gate_ltri())
        _, vjp = jax.vjp(fn, aux_ref[...], b_ref[...], jnp.zeros((1, 128), F32))
        daux, db, dcarry = vjp((dcum, dcarry_ref[...]))
        daux_ref[...] = daux + dg_ref[...]
        db_ref[...] += db
        dcarry_ref[...] = dcarry

    blk = pl.BlockSpec((GATE_ROWS, 128), lambda i: (rev(i), 0))
    vec = pl.BlockSpec((1, 128), lambda i: (0, 0))
    return pl.pallas_call(
        body, name="fox_gate_bwd", grid=(nb,),
        in_specs=[pl.BlockSpec((GATE_ROWS, 128), lambda i: (rev(i), AUX_BLK)), vec, blk, blk],
        out_specs=(blk, vec),
        out_shape=(jax.ShapeDtypeStruct((T, 128), F32), jax.ShapeDtypeStruct((1, 128), F32)),
        scratch_shapes=[pltpu.VMEM((1, 128), F32)],
        compiler_params=_cp(("arbitrary",)),
    )(proj, bpad, dccol_t, daux_gla)


FOX_Q = 128


FOX_QB = T // FOX_Q


@jax.custom_vjp
def _attend(s, v):
    return _attend_fwd(s, v)[0]


def _attend_fwd(s, v):
    e = jnp.exp(s - jnp.max(s, axis=-1, keepdims=True))
    r = 1.0 / jnp.sum(e, axis=-1, keepdims=True)
    return _dot(e, v, "nn") * r, (e, r, v)


def _attend_bwd(res, do):
    e, r, v = res
    do_r = do * r
    dpr = _dot(do_r, v, "nt")
    ds = e * (dpr - r * jnp.sum(e * dpr, axis=-1, keepdims=True))
    return ds, _dot(e, do_r, "tn").astype(v.dtype)


_attend.defvjp(_attend_fwd, _attend_bwd)


def _fox_block(hp, q, k, v, ccol):
    kl = k.shape[0]
    lane = _iota((FOX_Q, 128), 1)
    tri = jnp.bitwise_and(_iota((2 * FOX_Q, FOX_Q), 0), FOX_Q - 1) >= _iota((2 * FOX_Q, FOX_Q), 1)
    sub = _iota((8, kl), 0)
    qs = q * ATT_SCALE
    q2 = jnp.concatenate([jnp.where(lane < 64, qs, 0.0), jnp.where(lane >= 64, qs, 0.0)], axis=0)
    s = bdot(q2, k, "nt")
    cs = [jnp.sum(jnp.where(sub == 2 * hp + e, ccol, 0.0), axis=0, keepdims=True) for e in range(2)]
    s = jnp.concatenate([s[:FOX_Q] - cs[0], s[FOX_Q:] - cs[1]], axis=0)
    diag = jnp.where(tri, s[:, kl - FOX_Q:], NEG)
    s = diag if kl == FOX_Q else jnp.concatenate([s[:, :kl - FOX_Q], diag], axis=1)
    o2 = _attend(s, v)
    return jnp.where(lane < 64, o2[:FOX_Q], o2[FOX_Q:])


def _fox_in_specs():
    return [pl.BlockSpec((FOX_Q, 128), lambda hp, qb: (qb, 12 + hp)),
            pl.BlockSpec((T, 128), lambda hp, qb: (0, 16 + hp)),
            pl.BlockSpec((T, 128), lambda hp, qb: (0, 20 + hp)),
            pl.BlockSpec((8, T), lambda hp, qb: (0, 0))]


def _fox_fwd(proj, cum_c, cat):
    def body(q_ref, k_ref, v_ref, cc_ref, cat_ref, o_ref):
        qb = pl.program_id(1)
        for g in range(FOX_QB):
            kl = FOX_Q * (g + 1)

            @pl.when(qb == g)
            def _(kl=kl):
                o_ref[...] = _fox_block(pl.program_id(0), q_ref[...], k_ref[0:kl, :], v_ref[0:kl, :], cc_ref[:, 0:kl])

    return pl.pallas_call(
        body, name="fox_fwd", grid=(4, FOX_QB), in_specs=_fox_in_specs() + [pl.BlockSpec(memory_space=pl.ANY)],
        out_specs=pl.BlockSpec((FOX_Q, 128), lambda hp, qb: (qb, 4 + hp)),
        out_shape=jax.ShapeDtypeStruct((T, D), F32), input_output_aliases={4: 0},
        compiler_params=_cp(("parallel", "parallel")),
    )(proj, proj, proj, cum_c, cat)


def _fox_bwd(proj, cum_c, dcat):
    def body(q_ref, k_ref, v_ref, cc_ref, do_ref, dq_ref, dk_ref, dv_ref, dcc_ref):
        qb = pl.program_id(1)

        @pl.when(qb == 0)
        def _():
            dk_ref[...] = jnp.zeros_like(dk_ref)
            dv_ref[...] = jnp.zeros_like(dv_ref)
            dcc_ref[...] = jnp.zeros_like(dcc_ref)

        fn = functools.partial(_fox_block, pl.program_id(0))
        for g in range(FOX_QB):
            kl = FOX_Q * (g + 1)

            @pl.when(qb == g)
            def _(kl=kl):
                _, vjp = jax.vjp(fn, q_ref[...], k_ref[0:kl, :], v_ref[0:kl, :], cc_ref[:, 0:kl])
                dq, dk, dv, dcc = vjp(do_ref[...])
                dq_ref[...] = dq
                dk_ref[0:kl, :] += dk
                dv_ref[0:kl, :] += dv
                dcc_ref[:, 0:kl] += dcc

    sds = lambda *s: jax.ShapeDtypeStruct(s, F32)
    return pl.pallas_call(
        body, name="fox_bwd", grid=(4, FOX_QB),
        in_specs=_fox_in_specs() + [pl.BlockSpec((FOX_Q, 128), lambda hp, qb: (qb, 4 + hp))],
        out_specs=(pl.BlockSpec((FOX_Q, 128), lambda hp, qb: (qb, hp)),
                   pl.BlockSpec((T, 128), lambda hp, qb: (0, hp)),
                   pl.BlockSpec((T, 128), lambda hp, qb: (0, hp)),
                   pl.BlockSpec((None, 8, T), lambda hp, qb: (hp, 0, 0))),
        out_shape=(sds(T, 512), sds(T, 512), sds(T, 512), sds(4, 8, T)),
        compiler_params=_cp(("parallel", "arbitrary")),
    )(proj, proj, proj, cum_c, dcat)


BIAS_W = 640


def _rel_onehot():
    j = _iota((REL_PAD, BIAS_W), 1)
    rel = jnp.clip(CA_PAD + CHUNK - 1 - j, -128, 128) + 128
    return (_iota((REL_PAD, BIAS_W), 0) == rel).astype(F32)


def _bias_build(rbp):
    def body(rb_ref, o_ref):
        f = _hdot_raw(rb_ref[...], _rel_onehot(), "nn")
        for q in range(CHUNK):
            o_ref[q] = pltpu.roll(f, (BIAS_W - (CHUNK - 1 - q)) % BIAS_W, 1)[:, :CA_BAND]

    return pl.pallas_call(body, name="ca_bias_build", out_shape=jax.ShapeDtypeStruct((CHUNK, 8, CA_BAND), F32))(rbp)


def _bias_grad(dbias_q):
    def body(db_ref, o_ref):
        acc = jnp.zeros((8, BIAS_W), F32)
        for q in range(CHUNK):
            acc = acc + pltpu.roll(db_ref[q], CHUNK - 1 - q, 1)
        o_ref[...] = _hdot_raw(acc, _rel_onehot(), "nt")

    return pl.pallas_call(body, name="ca_bias_grad", out_shape=jax.ShapeDtypeStruct((8, REL_PAD), F32))(dbias_q)


def _ca_block(c, masked, q, kb, vb, bias2):
    lane = _iota((CHUNK, 128), 1)
    qs = q * ATT_SCALE
    q2 = jnp.concatenate([jnp.where(lane < 64, qs, 0.0), jnp.where(lane >= 64, qs, 0.0)], axis=0)
    s = bdot(q2, kb, "nt") + bias2.reshape(2 * CHUNK, CA_BAND)
    if masked:
        s = jnp.where((c * CHUNK - CA_PAD + _iota((2 * CHUNK, CA_BAND), 1)) >= 0, s, NEG)
    o2 = _attend(s, vb)
    return jnp.where(lane < 64, o2[:CHUNK], o2[CHUNK:])


CA_PER_STEP = 8
CA_ROWS = CA_PER_STEP * CHUNK
CA_MASKED_STEPS = CA_PAD // CA_ROWS


def _ca_fwd(proj, kvpad, bias):
    def body(q_ref, k_ref, v_ref, b_ref, o_ref):
        def run(masked):
            outs = []
            for i in range(CA_PER_STEP):
                c = pl.program_id(1) * CA_PER_STEP + i
                band = pl.ds(pl.multiple_of(c * CHUNK, CHUNK), CA_BAND)
                rows = slice(i * CHUNK, (i + 1) * CHUNK)
                outs.append(_ca_block(c, masked, q_ref[rows, :], k_ref[band, :], v_ref[band, :], b_ref[...]))
            for i in range(CA_PER_STEP):
                o_ref[i * CHUNK:(i + 1) * CHUNK, :] = outs[i]

        pl.when(pl.program_id(1) < CA_MASKED_STEPS)(lambda: run(True))
        pl.when(pl.program_id(1) >= CA_MASKED_STEPS)(lambda: run(False))

    return pl.pallas_call(
        body, name="ca_fwd", grid=(4, NCHUNK // CA_PER_STEP),
        in_specs=[pl.BlockSpec((CA_ROWS, 128), lambda hp, c: (c, hp)),
                  pl.BlockSpec((T + CA_PAD, 128), lambda hp, c: (0, hp)),
                  pl.BlockSpec((T + CA_PAD, 128), lambda hp, c: (0, 4 + hp)),
                  pl.BlockSpec((2, CHUNK, CA_BAND), lambda hp, c: (hp, 0, 0))],
        out_specs=pl.BlockSpec((CA_ROWS, 128), lambda hp, c: (c, hp)),
        out_shape=jax.ShapeDtypeStruct((T, D), F32),
        compiler_params=_cp(("parallel", "parallel")),
    )(proj, kvpad, kvpad, bias)


def _ca_bwd(proj, kvpad, bias, dcat):
    def body(q_ref, k_ref, v_ref, b_ref, do_ref, dq_ref, dk_ref, dv_ref, db_ref):
        c = pl.program_id(1)

        @pl.when(c == 0)
        def _():
            dk_ref[...] = jnp.zeros_like(dk_ref)
            dv_ref[...] = jnp.zeros_like(dv_ref)
            db_ref[...] = jnp.zeros_like(db_ref)

        def run(masked):
            grads, bands = [], []
            for i in range(CA_PER_STEP):
                ci = c * CA_PER_STEP + i
                band = pl.ds(pl.multiple_of(ci * CHUNK, CHUNK), CA_BAND)
                rows = slice(i * CHUNK, (i + 1) * CHUNK)
                fn = functools.partial(_ca_block, ci, masked)
                _, vjp = jax.vjp(fn, q_ref[rows, :], k_ref[band, :], v_ref[band, :], b_ref[...])
                grads.append(vjp(do_ref[rows, :]))
                bands.append(band)
            for i, (dq, _, _, _) in enumerate(grads):
                dq_ref[i * CHUNK:(i + 1) * CHUNK, :] = dq
            for band, (_, dkb, dvb, _) in zip(bands, grads):
                dk_ref[band, :] += dkb
                dv_ref[band, :] += dvb
            db_ref[...] += functools.reduce(lambda a, b: a + b, [g[3] for g in grads])

        pl.when(c < CA_MASKED_STEPS)(lambda: run(True))
        pl.when(c >= CA_MASKED_STEPS)(lambda: run(False))

    sds = lambda *s: jax.ShapeDtypeStruct(s, F32)
    padded = lambda: pl.BlockSpec((T + CA_PAD, 128), lambda hp, c: (0, hp))
    return pl.pallas_call(
        body, name="ca_bwd", grid=(4, NCHUNK // CA_PER_STEP),
        in_specs=[pl.BlockSpec((CA_ROWS, 128), lambda hp, c: (c, hp)),
                  pl.BlockSpec((T + CA_PAD, 128), lambda hp, c: (0, hp)),
                  pl.BlockSpec((T + CA_PAD, 128), lambda hp, c: (0, 4 + hp)),
                  pl.BlockSpec((2, CHUNK, CA_BAND), lambda hp, c: (hp, 0, 0)),
                  pl.BlockSpec((CA_ROWS, 128), lambda hp, c: (c, hp))],
        out_specs=(pl.BlockSpec((CA_ROWS, 128), lambda hp, c: (c, hp)), padded(), padded(),
                   pl.BlockSpec((2, CHUNK, CA_BAND), lambda hp, c: (hp, 0, 0))),
        out_shape=(sds(T, 512), sds(T + CA_PAD, 512), sds(T + CA_PAD, 512), sds(8, CHUNK, CA_BAND)),
        compiler_params=_cp(("parallel", "arbitrary")),
    )(proj, kvpad, kvpad, bias, dcat)


def _lru_pre(xs, cw, cb, wa, ba, wx, bx, lam):
    xc = cb + xs[0] * cw[0:1, :] + xs[1] * cw[1:2, :] + xs[2] * cw[2:3, :] + xs[3] * cw[3:4, :]
    ra = _sigmoid(bdot(xc, wa, "nn") + ba)
    ii = _sigmoid(bdot(xc, wx, "nn") + bx)
    la = 8.0 * ra * _log_sigmoid(lam)
    return jnp.exp(la), jnp.sqrt(-_expm1(2.0 * la)) * (ii * xc)


def _lru_pre_specs():
    full = lambda shape: pl.BlockSpec(shape, lambda i: (0,) * len(shape))
    return [pl.BlockSpec((4, ROWS, 512), lambda i: (0, i, 0)), full((4, 512)), full((1, 512)),
            full((512, 512)), full((1, 512)), full((512, 512)), full((1, 512)), full((1, 512))]


def _lru_pre_fwd(xs, cw, cb, wa, ba, wx, bx, lam):
    def body(xs_ref, cw_ref, cb_ref, wa_ref, ba_ref, wx_ref, bx_ref, lam_ref, a_ref, b_ref):
        a, b = _lru_pre(xs_ref[...], cw_ref[...], cb_ref[...], wa_ref[...], ba_ref[...], wx_ref[...], bx_ref[...],
                        lam_ref[...])
        a_ref[...] = a
        b_ref[...] = b

    row = pl.BlockSpec((ROWS, 512), lambda i: (i, 0))
    sds = jax.ShapeDtypeStruct((T, 512), F32)
    return pl.pallas_call(body, name="lru_pre_fwd", grid=(T // ROWS,), in_specs=_lru_pre_specs(),
                          out_specs=(row, row), out_shape=(sds, sds), compiler_params=_cp(("parallel",)),
                          )(xs, cw, cb, wa, ba, wx, bx, lam)


def _lru_pre_bwd(xs, cw, cb, wa, ba, wx, bx, lam, da, db):
    def body(xs_ref, cw_ref, cb_ref, wa_ref, ba_ref, wx_ref, bx_ref, lam_ref, da_ref, db_ref,
             dxs_ref, dcw_ref, dcb_ref, dwa_ref, dba_ref, dwx_ref, dbx_ref, dlam_ref):
        acc = (dcw_ref, dcb_ref, dwa_ref, dba_ref, dwx_ref, dbx_ref, dlam_ref)

        @pl.when(pl.program_id(0) == 0)
        def _():
            for r in acc:
                r[...] = jnp.zeros_like(r)

        _, vjp = jax.vjp(_lru_pre, xs_ref[...], cw_ref[...], cb_ref[...], wa_ref[...], ba_ref[...], wx_ref[...],
                         bx_ref[...], lam_ref[...])
        grads = vjp((da_ref[...], db_ref[...]))
        dxs_ref[...] = grads[0]
        for r, g in zip(acc, grads[1:]):
            r[...] += g

    row = pl.BlockSpec((ROWS, 512), lambda i: (i, 0))
    specs = _lru_pre_specs()
    sds = lambda *s: jax.ShapeDtypeStruct(s, F32)
    return pl.pallas_call(
        body, name="lru_pre_bwd", grid=(T // ROWS,), in_specs=specs + [row, row], out_specs=tuple(specs),
        out_shape=(sds(4, T, 512), sds(4, 512), sds(1, 512), sds(512, 512), sds(1, 512), sds(512, 512), sds(1, 512),
                   sds(1, 512)),
        compiler_params=_cp(("arbitrary",)),
    )(xs, cw, cb, wa, ba, wx, bx, lam, da, db)


def _lru_scan_fwd(a, b):
    def body(a_ref, b_ref, h_ref):
        def step(t, h):
            h = a_ref[pl.ds(t, 1), :] * h + b_ref[pl.ds(t, 1), :]
            h_ref[pl.ds(t, 1), :] = h
            return h

        lax.fori_loop(0, T, step, jnp.zeros((1, 512), F32), unroll=8)

    return pl.pallas_call(body, name="lru_scan_fwd", out_shape=jax.ShapeDtypeStruct((T, 512), F32),
                          compiler_params=pltpu.CompilerParams(vmem_limit_bytes=VMEM_LIMIT))(a, b)


def _lru_scan_bwd(a, h, dh):
    def body(a_ref, h_ref, dh_ref, da_ref, db_ref):
        def step(i, carry):
            t = T - 1 - i
            g = dh_ref[pl.ds(t, 1), :] + carry
            db_ref[pl.ds(t, 1), :] = g
            da_ref[pl.ds(t, 1), :] = g * h_ref[pl.ds(t - 1, 1), :]
            return a_ref[pl.ds(t, 1), :] * g

        carry = lax.fori_loop(0, T - 1, step, jnp.zeros((1, 512), F32), unroll=8)
        db_ref[pl.ds(0, 1), :] = dh_ref[pl.ds(0, 1), :] + carry
        da_ref[pl.ds(0, 1), :] = jnp.zeros((1, 512), F32)

    sds = jax.ShapeDtypeStruct((T, 512), F32)
    return pl.pallas_call(body, name="lru_scan_bwd", out_shape=(sds, sds),
                          compiler_params=pltpu.CompilerParams(vmem_limit_bytes=VMEM_LIMIT))(a, h, dh)


def _lru_post(h, gate):
    return h * _gelu_tanh(gate)


def _lru_post_fwd(h, proj, cat):
    def body(h_ref, g_ref, cat_ref, o_ref):
        o_ref[...] = _lru_post(h_ref[...], g_ref[...])

    row = pl.BlockSpec((ROWS, 512), lambda i: (i, 0))
    return pl.pallas_call(body, name="lru_post_fwd", grid=(T // ROWS,),
                          in_specs=[row, pl.BlockSpec((ROWS, 512), lambda i: (i, 3)), pl.BlockSpec(memory_space=pl.ANY)],
                          out_specs=pl.BlockSpec((ROWS, 512), lambda i: (i, 1)),
                          out_shape=jax.ShapeDtypeStruct((T, D), F32), input_output_aliases={2: 0},
                          compiler_params=_cp(("parallel",)))(h, proj, cat)


def _lru_post_bwd(h, proj, dcat):
    def body(h_ref, g_ref, do_ref, dh_ref, dg_ref):
        _, vjp = jax.vjp(_lru_post, h_ref[...], g_ref[...])
        dh, dg = vjp(do_ref[...])
        dh_ref[...] = dh
        dg_ref[...] = dg

    row = pl.BlockSpec((ROWS, 512), lambda i: (i, 0))
    sds = jax.ShapeDtypeStruct((T, 512), F32)
    return pl.pallas_call(body, name="lru_post_bwd", grid=(T // ROWS,),
                          in_specs=[row, pl.BlockSpec((ROWS, 512), lambda i: (i, 3)),
                                    pl.BlockSpec((ROWS, 512), lambda i: (i, 1))],
                          out_specs=(row, row), out_shape=(sds, sds), compiler_params=_cp(("parallel",)))(h, proj, dcat)


def _conv_dx(dxs_shift):
    def body(d_ref, o_ref):
        o_ref[...] = d_ref[0] + d_ref[1] + d_ref[2] + d_ref[3]

    row = pl.BlockSpec((ROWS, 512), lambda i: (i, 0))
    return pl.pallas_call(body, name="lru_conv_dx", grid=(T // ROWS,),
                          in_specs=[pl.BlockSpec((4, ROWS, 512), lambda i: (0, i, 0))], out_specs=row,
                          out_shape=jax.ShapeDtypeStruct((T, 512), F32), compiler_params=_cp(("parallel",)))(dxs_shift)


def _position():
    return lax.axis_index("x"), lax.axis_index("y"), lax.axis_index("c")


def _other_chips(x, y):
    return [(1 - x, y), (x, 1 - y), (1 - x, 1 - y)]


def _al(v, n):
    return v * n if isinstance(v, int) else pl.multiple_of(v * n, n)


_AG_ITEMS = [
    ((4, 32, 128), lambda o, s, h: o.at[s, pl.ds(_al(h, 16), 16), :], lambda r, h: r.at[pl.ds(_al(h, 16), 16), :]),
    ((4, 774, 1024), lambda o, s, h: o.at[s, :, pl.ds(_al(h, 512), 512)], lambda r, h: r.at[:, pl.ds(_al(h, 512), 512)]),
    ((1024, 1024), lambda o, s, h: o.at[pl.ds(_al(2 * s + h, 128), 128), :], lambda r, h: r.at[pl.ds(_al(h, 128), 128), :]),
    ((2, 1024, 4096), lambda o, s, h: o.at[h, :, pl.ds(_al(s, 1024), 1024)], lambda r, h: r.at[h]),
    ((2, 4096, 1024), lambda o, s, h: o.at[h, pl.ds(_al(s, 1024), 1024), :], lambda r, h: r.at[h]),
    ((1024, 2560), lambda o, s, h: o.at[pl.ds(_al(h, 512), 512), pl.ds(_al(s, 640), 640)],
     lambda r, h: r.at[pl.ds(_al(h, 512), 512), :]),
    ((1024, 1024), lambda o, s, h: o.at[pl.ds(_al(2 * s + h, 128), 128), :], lambda r, h: r.at[pl.ds(_al(h, 128), 128), :]),
]


_AG_GROUPS = [(0, 1, 2), (3, 4), (5, 6)]

_HBM = pl.BlockSpec(memory_space=pltpu.HBM)
_SEM = pl.BlockSpec(memory_space=pltpu.SEMAPHORE)
_SPLIT = dict(has_side_effects=pltpu.SideEffectType.DATAFLOW_SIDE_EFFECTING)


def _hbm(a):
    return pltpu.with_memory_space_constraint(a, pltpu.HBM)


def _ag_ici_copy(i, j, chip, c, slot, src_ref, land_ref, send_sems, recv_sems, k):
    _, dst, half = _AG_ITEMS[i]
    return pltpu.make_async_remote_copy(src_ref=half(src_ref, c), dst_ref=dst(land_ref, slot, c), send_sem=send_sems.at[k],
                                        recv_sem=recv_sems.at[k], device_id=(*chip, c), device_id_type=MESH)


def _ag_start(shards):
    n = len(_AG_ITEMS)
    ng = len(_AG_GROUPS)
    lands = [lax.empty(shape, s.dtype) for (shape, _, _), s in zip(_AG_ITEMS, shards)]

    def body(*refs):
        srcs, land_refs = refs[:n], refs[n:2 * n]
        sems = refs[2 * n:2 * n + 2 * ng]
        token = refs[-1]
        x, y, c = _position()
        me = 2 * x + y
        for g, items in enumerate(_AG_GROUPS):
            for t, i in enumerate(items):
                for j, chip in enumerate(_other_chips(x, y)):
                    _ag_ici_copy(i, j, chip, c, me, srcs[i], land_refs[i], sems[2 * g], sems[2 * g + 1], 3 * t + j).start()
        token[...] = jnp.zeros_like(token)

    sem_shapes = []
    for items in _AG_GROUPS:
        sem_shapes += [pltpu.SemaphoreType.DMA((3 * len(items),))] * 2
    thru = [pltpu.HBM(a.shape, a.dtype) for a in list(shards) + lands]
    out = pl.pallas_call(
        body, name="allgather_start",
        out_shape=tuple(sem_shapes) + tuple(thru) + (jax.ShapeDtypeStruct((8, 128), F32),),
        in_specs=(_HBM,) * (2 * n),
        out_specs=(_SEM,) * (2 * ng) + (_HBM,) * (2 * n) + (pl.BlockSpec(memory_space=pltpu.VMEM),),
        input_output_aliases={i: 2 * ng + i for i in range(2 * n)},
        compiler_params=pltpu.CompilerParams(**_SPLIT),
    )(*[_hbm(a) for a in list(shards) + lands])
    sems, thru, token = out[:2 * ng], out[2 * ng:-1], out[-1]
    return [(sems[2 * g], sems[2 * g + 1]) for g in range(ng)], list(thru[:n]), list(thru[n:]), token


def _ag_wait(g, sems, srcs, lands, after):
    items = _AG_GROUPS[g]
    m = len(items)

    def body(*refs):
        src_refs, land_refs = refs[:m], refs[m:2 * m]
        send_sems, recv_sems = refs[2 * m], refs[2 * m + 1]
        x, y, c = _position()
        for t, i in enumerate(items):
            for j, chip in enumerate(_other_chips(x, y)):
                cp = _ag_ici_copy(i, j, chip, c, 2 * chip[0] + chip[1], src_refs[t], land_refs[t], send_sems, recv_sems,
                                  3 * t + j)
                cp.wait_send()
                cp.wait_recv()

    ops = [srcs[i] for i in items] + [lands[i] for i in items]
    out = pl.pallas_call(
        body, name=f"allgather_wait_{g}",
        out_shape=tuple(pltpu.HBM(a.shape, a.dtype) for a in ops),
        in_specs=(_HBM,) * (2 * m) + (_SEM, _SEM, pl.BlockSpec(memory_space=pl.ANY)),
        out_specs=(_HBM,) * (2 * m),
        input_output_aliases={i: i for i in range(2 * m)},
        compiler_params=pltpu.CompilerParams(**_SPLIT),
    )(*ops, sems[0], sems[1], after)
    return list(out[:m]), list(out[m:])


def _ag_forward(g, srcs, lands):
    items = _AG_GROUPS[g]
    m = len(items)

    def body(*refs):
        src_refs, in_refs, out_refs = refs[:m], refs[m:2 * m], refs[2 * m:3 * m]
        send_sems, recv_sems = refs[3 * m:]
        x, y, c = _position()
        sibling = (x, y, 1 - c)
        me = 2 * x + y
        chips = _other_chips(x, y)
        sends = []
        for t, i in enumerate(items):
            _, dst, half = _AG_ITEMS[i]
            for j, chip in enumerate(chips):
                slot = 2 * chip[0] + chip[1]
                sends.append(pltpu.make_async_remote_copy(
                    src_ref=dst(in_refs[t], slot, c), dst_ref=dst(out_refs[t], slot, c), send_sem=send_sems.at[5 * t + j],
                    recv_sem=recv_sems.at[5 * t + j], device_id=sibling, device_id_type=MESH))
            for hc in range(2):
                sends.append(pltpu.make_async_remote_copy(
                    src_ref=half(src_refs[t], hc), dst_ref=dst(out_refs[t], me, hc), send_sem=send_sems.at[5 * t + 3 + hc],
                    recv_sem=recv_sems.at[5 * t + 3 + hc], device_id=sibling, device_id_type=MESH))
        for cp in sends:
            cp.start()
        for t, i in enumerate(items):
            _, dst, half = _AG_ITEMS[i]
            for j, chip in enumerate(chips):
                there = dst(out_refs[t], 2 * chip[0] + chip[1], 1 - c)
                pltpu.make_async_remote_copy(src_ref=there, dst_ref=there, send_sem=send_sems.at[5 * t + j],
                                             recv_sem=recv_sems.at[5 * t + j], device_id=sibling,
                                             device_id_type=MESH).wait_recv()
            for hc in range(2):
                there = dst(out_refs[t], me, hc)
                pltpu.make_async_remote_copy(src_ref=there, dst_ref=there, send_sem=send_sems.at[5 * t + 3 + hc],
                                             recv_sem=recv_sems.at[5 * t + 3 + hc], device_id=sibling,
                                             device_id_type=MESH).wait_recv()
        for cp in sends:
            cp.wait_send()

    any_spec = pl.BlockSpec(memory_space=pl.ANY)
    return pl.pallas_call(
        body, name=f"allgather_forward_{g}",
        in_specs=[any_spec] * (2 * m), out_specs=(any_spec,) * m,
        out_shape=tuple(jax.ShapeDtypeStruct(a.shape, a.dtype) for a in lands),
        input_output_aliases={m + t: t for t in range(m)},
        scratch_shapes=[pltpu.SemaphoreType.DMA((5 * m,)), pltpu.SemaphoreType.DMA((5 * m,))],
    )(*srcs, *lands)


def _pair_swap_copy(g_ref, r_ref, send_sem, recv_sem):
    x, y, c = _position()
    hc = g_ref.shape[2] // 2
    return pltpu.make_async_remote_copy(src_ref=g_ref.at[:, :, pl.ds(_al(1 - c, hc), hc)], dst_ref=r_ref,
                                        send_sem=send_sem, recv_sem=recv_sem, device_id=(x, y, 1 - c),
                                        device_id_type=MESH)


def _pair_swap_start(gb, tag):
    _, rows, cols = gb.shape
    recv = lax.empty((4, rows, cols // 2), gb.dtype)

    def body(g_ref, r_ref, send_sem, recv_sem, g_thru, r_thru, token):
        _pair_swap_copy(g_ref, r_ref, send_sem, recv_sem).start()
        token[...] = jnp.zeros_like(token)

    return pl.pallas_call(
        body, name="grad_pair_swap_start_" + tag,
        out_shape=(pltpu.SemaphoreType.DMA(()), pltpu.SemaphoreType.DMA(()), pltpu.HBM(gb.shape, gb.dtype),
                   pltpu.HBM(recv.shape, recv.dtype), jax.ShapeDtypeStruct((8, 128), F32)),
        in_specs=(_HBM, _HBM), out_specs=(_SEM, _SEM, _HBM, _HBM, pl.BlockSpec(memory_space=pltpu.VMEM)),
        input_output_aliases={0: 2, 1: 3},
        compiler_params=pltpu.CompilerParams(**_SPLIT),
    )(_hbm(gb), _hbm(recv))


def _pair_swap_wait(started, after, tag):
    send_sem, recv_sem, gb, recv, _ = started

    def body(g_ref, r_ref, send_sem, recv_sem, after_ref, g_out, r_out):
        cp = _pair_swap_copy(g_ref, r_ref, send_sem, recv_sem)
        cp.wait_send()
        cp.wait_recv()

    return pl.pallas_call(
        body, name="grad_pair_swap_wait_" + tag,
        out_shape=(pltpu.HBM(gb.shape, gb.dtype), pltpu.HBM(recv.shape, recv.dtype)),
        in_specs=(_HBM, _HBM, _SEM, _SEM, pl.BlockSpec(memory_space=pl.ANY)), out_specs=(_HBM, _HBM),
        input_output_aliases={0: 0, 1: 1},
        compiler_params=pltpu.CompilerParams(**_SPLIT),
    )(gb, recv, send_sem, recv_sem, after)


def _handover(red, tag):
    hc = red.shape[1] // 2

    def body(in_ref, out_ref, send_sem, recv_sem):
        x, y, c = _position()
        mine = pl.ds(_al(c, hc), hc)
        cp = pltpu.make_async_remote_copy(src_ref=in_ref.at[:, mine], dst_ref=out_ref.at[:, mine], send_sem=send_sem,
                                          recv_sem=recv_sem, device_id=(x, y, 1 - c), device_id_type=MESH)
        cp.start()
        theirs = out_ref.at[:, pl.ds(_al(1 - c, hc), hc)]
        pltpu.make_async_remote_copy(src_ref=theirs, dst_ref=theirs, send_sem=send_sem, recv_sem=recv_sem,
                                     device_id=(x, y, c), device_id_type=MESH).wait_recv()
        cp.wait_send()

    return pl.pallas_call(
        body, name="grad_handover_" + tag,
        in_specs=[pl.BlockSpec(memory_space=pl.ANY)], out_specs=pl.BlockSpec(memory_space=pl.ANY),
        out_shape=jax.ShapeDtypeStruct(red.shape, red.dtype), input_output_aliases={0: 0},
        scratch_shapes=[pltpu.SemaphoreType.DMA, pltpu.SemaphoreType.DMA],
    )(red)


def _a2a_copy(j, chip, c, p_ref, q_ref, q_slot, send_sems, recv_sems):
    return pltpu.make_async_remote_copy(src_ref=p_ref.at[2 * chip[0] + chip[1]], dst_ref=q_ref.at[q_slot],
                                        send_sem=send_sems.at[j], recv_sem=recv_sems.at[j], device_id=(*chip, c),
                                        device_id_type=MESH)


def _a2a_start(p, tag):
    def body(p_ref, q_ref, send_sems, recv_sems, p_thru, q_thru, token):
        x, y, c = _position()
        for j, chip in enumerate(_other_chips(x, y)):
            _a2a_copy(j, chip, c, p_ref, q_ref, 2 * x + y, send_sems, recv_sems).start()
        token[...] = jnp.zeros_like(token)

    return pl.pallas_call(
        body, name="grad_alltoall_start_" + tag,
        out_shape=(pltpu.SemaphoreType.DMA((3,)), pltpu.SemaphoreType.DMA((3,)), pltpu.HBM(p.shape, p.dtype),
                   pltpu.HBM(p.shape, p.dtype), jax.ShapeDtypeStruct((8, 128), F32)),
        in_specs=(_HBM, _HBM), out_specs=(_SEM, _SEM, _HBM, _HBM, pl.BlockSpec(memory_space=pltpu.VMEM)),
        input_output_aliases={0: 2, 1: 3},
        compiler_params=pltpu.CompilerParams(**_SPLIT),
    )(_hbm(p), _hbm(lax.empty(p.shape, p.dtype)))


def _a2a_wait(send_sems, recv_sems, p, q, after, tag):
    def body(p_ref, q_ref, send_sems, recv_sems, after_ref, p_out, q_out):
        x, y, c = _position()
        for j, chip in enumerate(_other_chips(x, y)):
            cp = _a2a_copy(j, chip, c, p_ref, q_ref, 2 * chip[0] + chip[1], send_sems, recv_sems)
            cp.wait_send()
            cp.wait_recv()

    return pl.pallas_call(
        body, name="grad_alltoall_wait_" + tag,
        out_shape=(pltpu.HBM(p.shape, p.dtype), pltpu.HBM(q.shape, q.dtype)),
        in_specs=(_HBM, _HBM, _SEM, _SEM, pl.BlockSpec(memory_space=pl.ANY)), out_specs=(_HBM, _HBM),
        input_output_aliases={0: 0, 1: 1},
        compiler_params=pltpu.CompilerParams(**_SPLIT),
    )(p, q, send_sems, recv_sems, after)


def _comm_rows(rows):
    return next(t for t in (512, 384, 256, 128) if rows % t == 0)


def _pair_add(gb, recv, where, tag):
    _, rows, cols = gb.shape
    hc = cols // 2
    tr = _comm_rows(rows)

    def body(w_ref, g_ref, r_ref, o_ref):
        o_ref[...] = (g_ref[...].astype(F32) + r_ref[...].astype(F32)).astype(o_ref.dtype)

    return pl.pallas_call(
        body, name="grad_pair_add_" + tag,
        grid_spec=pltpu.PrefetchScalarGridSpec(
            num_scalar_prefetch=1, grid=(4, rows // tr),
            in_specs=[pl.BlockSpec((None, tr, hc), lambda s, j, w_ref: (s, j, w_ref[0])),
                      pl.BlockSpec((None, tr, hc), lambda s, j, w_ref: (s, j, 0))],
            out_specs=pl.BlockSpec((None, tr, hc), lambda s, j, w_ref: (s, j, 0))),
        out_shape=jax.ShapeDtypeStruct((4, rows, hc), gb.dtype),
        compiler_params=_cp(("parallel", "parallel")),
    )(where, gb, recv)


def _sum_chips(p, q, where, tag):
    _, rows, hc = q.shape
    tr = _comm_rows(rows)

    def body(w_ref, p_ref, qa_ref, qb_ref, qc_ref, o_ref):
        me = w_ref[1]
        own, qa, qb, qc = (r[...].astype(F32) for r in (p_ref, qa_ref, qb_ref, qc_ref))
        v0 = jnp.where(me == 0, own, qa)
        v1 = jnp.where(me == 1, own, jnp.where(me == 0, qa, qb))
        v2 = jnp.where(me == 2, own, jnp.where(me < 2, qb, qc))
        v3 = jnp.where(me == 3, own, qc)
        o_ref[...] = ((v0 + v1) + v2) + v3

    slot = lambda k: pl.BlockSpec((None, tr, hc), lambda j, w_ref: (w_ref[k], j, 0))
    return pl.pallas_call(
        body, name="grad_sum_chips_" + tag,
        grid_spec=pltpu.PrefetchScalarGridSpec(
            num_scalar_prefetch=1, grid=(rows // tr,),
            in_specs=[slot(1), slot(2), slot(3), slot(4)],
            out_specs=pl.BlockSpec((tr, hc), lambda j, w_ref: (j, w_ref[0]))),
        out_shape=jax.ShapeDtypeStruct((rows, 2 * hc), F32),
        compiler_params=_cp(("parallel",)),
    )(where, p, q, q, q)


def _shard_major(g, axis):
    shape = g.shape
    g = g.reshape(shape[:axis] + (4, shape[axis] // 4) + shape[axis + 1:])
    return jnp.moveaxis(g, axis, 0).reshape(4, -1)


def _unshard(g4, shape, axis):
    n = shape[axis] // 4
    g = g4.reshape((4,) + shape[:axis] + (n,) + shape[axis + 1:])
    return jnp.moveaxis(g, 0, axis).reshape(shape)


def _split(flat, shapes):
    out, off = [], 0
    for shp in shapes:
        n = 1
        for d in shp:
            n *= d
        out.append(flat[..., off:off + n].reshape(flat.shape[:-1] + tuple(shp)))
        off += n
    return out


def _even_rows_to_kernel(wt):
    return jnp.concatenate([wt[:1536], wt[1552:3088], wt[1536:1552], wt[3088:3096],
                            jnp.zeros((PE - 3096, wt.shape[1]), wt.dtype)], axis=0)


def _block_diag(w):
    eye = jnp.eye(8, dtype=w.dtype)
    return (w[:, :, None, :] * eye[:, None, :, None]).reshape(512, 512)


def _diag_blocks(g):
    eye = jnp.eye(8, dtype=g.dtype)
    return (g.reshape(8, 64, 8, 64) * eye[:, None, :, None]).sum(axis=2)


def _shift_down(a, s):
    return a if s == 0 else jnp.pad(a, ((s, 0), (0, 0)))[:a.shape[0]]


def _shift_up(a, s):
    return a if s == 0 else jnp.pad(a, ((0, s), (0, 0)))[s:]


SMALL_SHARDED_SHAPES = [(2, 4, 256), (16, 64), (4, 128), (128,), (128,), (128,), (128,)]
REPL_SHAPES = [(256,), (512,), (8,), (8, 257), (8, 64, 64), (8, 64, 64)]


def kernel(x, norm_w, w_in_even, gla_w_a_up, gla_b_a, gla_norm_w, fox_b_f, w_out_even, w_in_odd, rel_bias, conv_w, conv_b, lru_w_a, lru_b_a, lru_w_x, lru_b_x, lru_lambda, w_out_odd, w_mlp_up, w_mlp_down, loss_target, m_norm_w, m_w_in_even, m_gla_w_a_up, m_gla_b_a, m_gla_norm_w, m_fox_b_f, m_w_out_even, m_w_in_odd, m_rel_bias, m_conv_w, m_conv_b, m_lru_w_a, m_lru_b_a, m_lru_w_x, m_lru_b_x, m_lru_lambda, m_w_out_odd, m_w_mlp_up, m_w_mlp_down, v_norm_w, v_w_in_even, v_gla_w_a_up, v_gla_b_a, v_gla_norm_w, v_fox_b_f, v_w_out_even, v_w_in_odd, v_rel_bias, v_conv_w, v_conv_b, v_lru_w_a, v_lru_b_a, v_lru_w_x, v_lru_b_x, v_lru_lambda, v_w_out_odd, v_w_mlp_up, v_w_mlp_down):
    c_idx = lax.axis_index("c")

    small_local = [norm_w, gla_w_a_up[0], conv_w[0], conv_b[0], lru_b_a[0], lru_b_x[0], lru_lambda[0]]
    small_src = jnp.concatenate([a.reshape(-1) for a in small_local]).reshape(32, 128)
    mine = [small_src, w_in_even[0].T.astype(BF16), w_out_even[0].astype(BF16), w_mlp_up.astype(BF16),
            w_mlp_down.astype(BF16), w_in_odd[0].astype(BF16), w_out_odd[0].astype(BF16)]
    ag_sems, ag_srcs, ag_lands, ag_token = _ag_start(mine)

    def gathered(g, after):
        srcs_g, lands_g = _ag_wait(g, ag_sems[g], ag_srcs, ag_lands, after)
        return _ag_forward(g, srcs_g, lands_g)

    small4, w_in_e4, w_out_e = gathered(0, ag_token)
    me = 2 * lax.axis_index("x") + lax.axis_index("y")
    others = [k + (k >= me).astype(jnp.int32) for k in range(3)]
    where = jnp.stack([c_idx, me] + others).astype(jnp.int32)

    w_in_e_t = _even_rows_to_kernel(w_in_e4.reshape(3096, D))
    g_small = _split(small4.reshape(4, 32 * 128), SMALL_SHARDED_SHAPES)
    nw_full = _unshard(g_small[0], (2, 4, 1024), 2)
    wa_up = _unshard(g_small[1], (16, 256), 1)
    cw = _unshard(g_small[2], (4, 512), 1)
    cb, lba, lbx, lam = [_unshard(g, (512,), 0).reshape(1, 512) for g in g_small[3:]]
    nw = lambda layer, i: nw_full[layer, i].reshape(1, D)

    wa_pad = jnp.pad(wa_up, ((0, 128 - 16), (0, 0)))
    gla_ba = gla_b_a.reshape(1, 256)
    gla_nw = gla_norm_w.reshape(1, 512)
    fox_bpad = jnp.pad(fox_b_f.reshape(1, 8), ((0, 0), (FOX_LANE0, 128 - FOX_LANE0 - 8)))
    rbp = jnp.pad(rel_bias[0], ((0, 0), (0, REL_PAD - 257)))
    wa_bd = _block_diag(lru_w_a[0])
    wx_bd = _block_diag(lru_w_x[0])

    x0 = x[0]
    tgt = loss_target[0]

    h0 = _prenorm(x0, nw(0, 0), "prenorm_l0_mix")
    proj_e = _mm(h0, w_in_e_t, "nt", tm=1024, tn=640, name="mm_in_even")
    cat0, s_prev = _gla_fwd(proj_e, wa_pad, gla_ba, gla_nw)
    cum_r = _fox_gate_fwd(proj_e, fox_bpad)
    cum_c = cum_r[:, FOX_LANE0:FOX_LANE0 + 8].T
    cat0 = _fox_fwd(proj_e, cum_c, cat0)
    mix0 = _mm(cat0, w_out_e, "nn", tm=1024, tn=512, name="mm_out_even")
    x1, h1 = _post_pre_fwd(x0, mix0, nw(0, 1), nw(0, 2), "post_pre_l0_mix")
    w_up, w_dn = gathered(1, x1)
    a0, r0 = _mm(h1, w_up, "nn", tm=1024, tn=1024, b_layer=0, relu_pair=True, name="mm_up_l0")
    d0 = _mm(a0, w_dn, "nn", tm=1024, tn=512, b_layer=0, name="mm_down_l0")
    x2, h2 = _post_pre_fwd(x1, d0, nw(0, 3), nw(1, 0), "post_pre_l0_mlp")

    w_in_o, w_out_o = gathered(2, x2)
    proj_o = _mm(h2, w_in_o, "nn", tm=1024, tn=640, name="mm_in_odd")
    bias_q = _bias_build(rbp)
    bias = bias_q.transpose(1, 0, 2)
    kvpad = jnp.pad(proj_o[:, 512:1536], ((CA_PAD, 0), (0, 0)))
    cat1 = _ca_fwd(proj_o, kvpad, bias)
    x_in = proj_o[:, 2048:2560]
    xs = jnp.stack([_shift_down(x_in, 3 - j) for j in range(4)])
    lru_a, lru_b = _lru_pre_fwd(xs, cw, cb, wa_bd, lba, wx_bd, lbx, lam)
    hh = _lru_scan_fwd(lru_a, lru_b)
    cat1 = _lru_post_fwd(hh, proj_o, cat1)
    mix1 = _mm(cat1, w_out_o, "nn", tm=1024, tn=512, name="mm_out_odd")
    x3, h3 = _post_pre_fwd(x2, mix1, nw(1, 1), nw(1, 2), "post_pre_l1_mix")
    a1, r1 = _mm(h3, w_up, "nn", tm=1024, tn=1024, b_layer=1, relu_pair=True, name="mm_up_l1")
    d1 = _mm(a1, w_dn, "nn", tm=1024, tn=512, b_layer=1, name="mm_down_l1")
    g4, loss_part, dd1, dnw13 = _post_loss(x3, d1, nw(1, 3), tgt)
    loss = lax.psum(loss_part[0, 0], ("x", "y", "c"))

    def rs_begin(swap, after, tag):
        gb, recv = _pair_swap_wait(swap, after, tag)
        return _a2a_start(_pair_add(gb, recv, where, tag), tag)

    def rs_end(started, after, tag):
        send_sems, recv_sems, p, q, _ = started
        p, q = _a2a_wait(send_sems, recv_sems, p, q, after, tag)
        return _handover(_sum_chips(p, q, where, tag), tag)

    gba = lax.dynamic_update_slice(lax.empty((4, GA_ROWS, D), BF16), jnp.zeros((4, GA_UP - GA_GAP, D), BF16),
                                   (0, GA_GAP, 0))
    gba = _mm(a1, dd1, "tn", tm=512, tn=1024, into=(gba, 1024, GA_DN), name="mm_down_l1_dw")
    du1 = _mm(dd1, w_dn, "nt", tm=1024, tn=1024, b_layer=1, times2=r1, out_dtype=BF16, name="mm_down_l1_dx")
    gba = _mm(du1, h3, "tn", tm=512, tn=1024, into=(gba, 1024, GA_UP), name="mm_up_l1_dw")
    dh3 = _mm(du1, w_up, "nt", tm=1024, tn=512, b_layer=1, name="mm_up_l1_dx")
    g3, dmix1, dnw12, dnw11 = _pre_post_bwd(x3, nw(1, 2), dh3, g4, mix1, nw(1, 1), "pre_post_bwd_l1_mlp")
    gba = _mm(cat1, dmix1, "tn", tm=128, tn=1024, into=(gba, 256, GA_OUT_O), name="mm_out_odd_dw")
    dcat1 = _mm(dmix1, w_out_o, "nt", tm=1024, tn=512, name="mm_out_odd_dx")

    dq_c, dkpad, dvpad, dbias = _ca_bwd(proj_o, kvpad, bias, dcat1)
    g_rel = _bias_grad(jnp.pad(dbias.transpose(1, 0, 2), ((0, 0), (0, 0), (0, BIAS_W - CA_BAND))))[:, :257]
    dhh, dgate = _lru_post_bwd(hh, proj_o, dcat1)
    da_l, db_l = _lru_scan_bwd(lru_a, hh, dhh)
    dxs, g_cw, g_cb, g_wa_bd, g_lba, g_wx_bd, g_lbx, g_lam = _lru_pre_bwd(xs, cw, cb, wa_bd, lba, wx_bd, lbx, lam, da_l, db_l)
    dx_in = _conv_dx(jnp.stack([_shift_up(dxs[j], 3 - j) for j in range(4)]))
    dproj_o = jnp.concatenate([dq_c, dkpad[CA_PAD:], dvpad[CA_PAD:], dgate, dx_in], axis=1).astype(BF16)
    gba = _mm(dproj_o, h2, "tn", tm=128, tn=1024, into=(gba, 640, GA_IN_O), name="mm_in_odd_dw")
    swap_a = _pair_swap_start(gba, "a")
    dh2 = _mm(dproj_o, w_in_o, "nt", tm=1024, tn=512, name="mm_in_odd_dx")
    g2, dd0, dnw10, dnw03 = _pre_post_bwd(x2, nw(1, 0) + swap_a[4][0, 0], dh2, g3, d0, nw(0, 3), "pre_post_bwd_l1_mix")
    rs_a = rs_begin(swap_a, g2, "a")

    gbb = lax.empty((4, GB_ROWS, D), BF16)
    gbb = _mm(a0, dd0, "tn", tm=512, tn=1024, into=(gbb, 1024, GB_DN), name="mm_down_l0_dw")
    du0 = _mm(dd0, w_dn, "nt", tm=1024, tn=1024, b_layer=0, times2=r0, out_dtype=BF16, name="mm_down_l0_dx")
    gbb = _mm(du0, h1, "tn", tm=512, tn=1024, into=(gbb, 1024, GB_UP), name="mm_up_l0_dw")
    swap_b = _pair_swap_start(gbb, "b")
    dh1 = _mm(du0, w_up, "nt", tm=1024, tn=512, b_layer=0, name="mm_up_l0_dx")
    g1, dmix0, dnw02, dnw01 = _pre_post_bwd(x1, nw(0, 2) + (swap_b[4][0, 0] + rs_a[4][0, 0]), dh1, g2, mix0, nw(0, 1),
                                            "pre_post_bwd_l0_mlp")
    rs_b = rs_begin(swap_b, g1, "b")
    gbc = lax.empty((4, GC_ROWS, D), BF16)
    gbc = _mm(cat0, dmix0, "tn", tm=128, tn=1024, into=(gbc, 256, GC_OUT_E), name="mm_out_even_dw")
    dcat0 = _mm(dmix0, w_out_e, "nt", tm=1024, tn=512, name="mm_out_even_dx")

    dq_g, dk_g, dv_g, dr_g, daux_g, g_wa_pad, g_gla_ba, g_gla_nw = _gla_bwd(
        proj_e, s_prev, wa_pad, gla_ba, gla_nw + rs_b[4][0, 0], dcat0)
    dq_f, dk_f, dv_f, dccol = _fox_bwd(proj_e, cum_c, dcat0)
    dccol_t = jnp.pad(dccol.sum(axis=0).T, ((0, 0), (FOX_LANE0, 128 - FOX_LANE0 - 8)))
    daux, g_fox_bpad = _fox_gate_bwd(proj_e, fox_bpad, dccol_t, daux_g)
    dproj_e = jnp.concatenate([dq_g, dk_g, dv_g, dr_g, dq_f, dk_f, dv_f, daux], axis=1).astype(BF16)
    gt_in_e = _mm(dproj_e, h0, "tn", tm=640, tn=1024, out_dtype=BF16, name="mm_in_even_dw")
    dh0 = _mm(dproj_e, w_in_e_t, "nn", tm=1024, tn=512, name="mm_in_even_dx")
    grad_x, dnw00 = _norm_bwd(x0, nw(0, 0), dh0, g1, "prenorm_l0_mix_bwd")

    g_norm = jnp.stack([jnp.concatenate([dnw00, dnw01, dnw02, dnw03]), jnp.concatenate([dnw10, dnw11, dnw12, dnw13])])
    sharded = [(g_norm, 2), (g_wa_pad[:16], 1), (g_cw, 1), (g_cb[0], 0), (g_lba[0], 0), (g_lbx[0], 0), (g_lam[0], 0)]
    replicated = [g_gla_ba[0], g_gla_nw[0], g_fox_bpad[0, FOX_LANE0:FOX_LANE0 + 8], g_rel, _diag_blocks(g_wa_bd),
                  _diag_blocks(g_wx_bd)]
    small4 = jnp.concatenate([_shard_major(g, ax) for g, ax in sharded]
                             + [jnp.broadcast_to(g.reshape(1, -1), (4, g.size)) for g in replicated], axis=1)
    n_small = small4.shape[1]
    small_rows = GC_ROWS - GC_TAIL - 774
    small4 = jnp.pad(small4, ((0, 0), (0, small_rows * D - n_small))).reshape(4, small_rows, D)
    gt_rows = jnp.concatenate([gt_in_e[:1536], gt_in_e[3072:3088], gt_in_e[1536:3072], gt_in_e[3088:3096]], axis=0)
    tail = jnp.concatenate([gt_rows.reshape(4, 774, D), small4.astype(BF16)], axis=1)
    gbc = lax.dynamic_update_slice(gbc, tail, (0, GC_TAIL, 0))
    swap_c = _pair_swap_start(gbc, "c")
    rs_c = rs_begin(swap_c, swap_c[4], "c")

    red_a = rs_end(rs_a, rs_c[4], "a")
    red_b = rs_end(rs_b, red_a, "b")
    red_c = rs_end(rs_c, red_b, "c")

    g_small = _split(red_c[GC_TAIL + 774:].reshape(-1)[:n_small], SMALL_SHARDED_SHAPES + REPL_SHAPES)
    g_of = dict(zip(["norm_w", "gla_w_a_up", "conv_w", "conv_b", "lru_b_a", "lru_b_x", "lru_lambda", "gla_b_a",
                     "gla_norm_w", "fox_b_f", "rel_bias", "lru_w_a", "lru_w_x"], g_small))
    g_of.update(w_in_even=red_c[GC_TAIL:GC_TAIL + 774])
    g_in_place = dict(w_mlp_up=([(red_b, GB_UP, True), (red_a, GA_UP, True)], 256),
                      w_mlp_down=([(red_b, GB_DN, False), (red_a, GA_DN, False)], 256),
                      w_in_odd=([(red_a, GA_IN_O, True)], 256),
                      w_out_odd=([(red_a, GA_OUT_O, False)], 128),
                      w_out_even=([(red_c, GC_OUT_E, False)], 256))

    names = ["norm_w", "w_in_even", "gla_w_a_up", "gla_b_a", "gla_norm_w", "fox_b_f", "w_out_even", "w_in_odd", "rel_bias",
             "conv_w", "conv_b", "lru_w_a", "lru_b_a", "lru_w_x", "lru_b_x", "lru_lambda", "w_out_odd", "w_mlp_up",
             "w_mlp_down"]
    w_of = dict(norm_w=norm_w, w_in_even=w_in_even, gla_w_a_up=gla_w_a_up, gla_b_a=gla_b_a, gla_norm_w=gla_norm_w,
                fox_b_f=fox_b_f, w_out_even=w_out_even, w_in_odd=w_in_odd, rel_bias=rel_bias, conv_w=conv_w, conv_b=conv_b,
                lru_w_a=lru_w_a, lru_b_a=lru_b_a, lru_w_x=lru_w_x, lru_b_x=lru_b_x, lru_lambda=lru_lambda,
                w_out_odd=w_out_odd, w_mlp_up=w_mlp_up, w_mlp_down=w_mlp_down)
    m_of = dict(norm_w=m_norm_w, w_in_even=m_w_in_even, gla_w_a_up=m_gla_w_a_up, gla_b_a=m_gla_b_a,
                gla_norm_w=m_gla_norm_w, fox_b_f=m_fox_b_f, w_out_even=m_w_out_even, w_in_odd=m_w_in_odd,
                rel_bias=m_rel_bias, conv_w=m_conv_w, conv_b=m_conv_b, lru_w_a=m_lru_w_a, lru_b_a=m_lru_b_a,
                lru_w_x=m_lru_w_x, lru_b_x=m_lru_b_x, lru_lambda=m_lru_lambda, w_out_odd=m_w_out_odd,
                w_mlp_up=m_w_mlp_up, w_mlp_down=m_w_mlp_down)
    v_of = dict(norm_w=v_norm_w, w_in_even=v_w_in_even, gla_w_a_up=v_gla_w_a_up, gla_b_a=v_gla_b_a,
                gla_norm_w=v_gla_norm_w, fox_b_f=v_fox_b_f, w_out_even=v_w_out_even, w_in_odd=v_w_in_odd,
                rel_bias=v_rel_bias, conv_w=v_conv_w, conv_b=v_conv_b, lru_w_a=v_lru_w_a, lru_b_a=v_lru_b_a,
                lru_w_x=v_lru_w_x, lru_b_x=v_lru_b_x, lru_lambda=v_lru_lambda, w_out_odd=v_w_out_odd,
                w_mlp_up=v_w_mlp_up, w_mlp_down=v_w_mlp_down)
    grads, deltas, new_ms, new_vs = [], [], [], []
    for n in names:
        w = w_of[n]
        if n in g_in_place:
            sources, tr = g_in_place[n]
            g, d, mn, vn = _adamw_from(w, m_of[n], v_of[n], sources, tr, "adamw_" + n)
            grads.append(g)
            deltas.append(d)
            new_ms.append(mn)
            new_vs.append(vn)
            continue
        if n == "w_in_even":
            to_view = lambda a: a[0].T
            from_view = lambda a: a.T[None]
        else:
            view = w.shape if w.ndim <= 3 else w.shape[-3:]
            to_view = lambda a, view=view: a.reshape(view)
            from_view = lambda a, w=w: a.reshape(w.shape)
        g = g_of[n] if n == "w_in_even" else to_view(g_of[n])
        d, mn, vn = _adamw(to_view(w), g, to_view(m_of[n]), to_view(v_of[n]), "adamw_" + n)
        grads.append(from_view(g))
        deltas.append(from_view(d))
        new_ms.append(from_view(mn))
        new_vs.append(from_view(vn))

    return (loss, grad_x.reshape(1, T, D), *grads, *deltas, *new_ms, *new_vs)
```

```python
import functools

import jax
import jax.numpy as jnp
from jax import lax
from jax.experimental import pallas as pl
from jax.experimental.pallas import tpu as pltpu

F32 = jnp.float32
BF16 = jnp.bfloat16
MESH = pl.DeviceIdType.MESH

T = 2048
D = 1024
DFF = 4096
EPS = 1e-6
CHUNK = 64
NCHUNK = T // CHUNK
PE = 3200
PO = 2560
AUX_BLK = 3072 // 128
FOX_LANE0 = 16
GLA_SCALE = 64 ** -0.5
ATT_SCALE = 64 ** -0.5
NEG = float(jnp.finfo(jnp.float32).min)
CA_BAND = 576
CA_PAD = 512
REL_PAD = 384

VMEM_LIMIT = 48 * 1024 * 1024

ADAM_LR, ADAM_B1, ADAM_B2, ADAM_EPS, ADAM_WD, ADAM_STEP = 0.001, 0.9, 0.999, 1e-08, 0.01, 10

GA_ROWS, GA_IN_O, GA_OUT_O, GA_GAP, GA_UP, GA_DN = 3072, 0, 640, 896, 1024, 2048
GB_ROWS, GB_UP, GB_DN = 2048, 0, 1024
GC_ROWS, GC_OUT_E, GC_TAIL = 1152, 0, 256

_DIMS = {"nn": (((1,), (0,)), ((), ())), "nt": (((1,), (1,)), ((), ())), "tn": (((0,), (0,)), ((), ()))}


def _cp(sem, **kw):
    return pltpu.CompilerParams(dimension_semantics=sem, vmem_limit_bytes=VMEM_LIMIT, **kw)


def _dot(a, b, mode):
    return lax.dot_general(a.astype(BF16), b.astype(BF16), _DIMS[mode], preferred_element_type=F32)


@functools.partial(jax.custom_vjp, nondiff_argnums=(2,))
def bdot(a, b, mode):
    return _dot(a, b, mode)


def _bdot_fwd(a, b, mode):
    return _dot(a, b, mode), (a, b)


def _bdot_bwd(mode, res, g):
    a, b = res
    if mode == "nn":
        da, db = _dot(g, b, "nt"), _dot(a, g, "tn")
    elif mode == "nt":
        da, db = _dot(g, b, "nn"), _dot(g, a, "tn")
    else:
        da, db = _dot(b, g, "nt"), _dot(a, g, "nn")
    return da.astype(a.dtype), db.astype(b.dtype)


bdot.defvjp(_bdot_fwd, _bdot_bwd)


def _hdot_raw(a, b, mode):
    return lax.dot_general(a, b, _DIMS[mode], precision=lax.Precision.HIGHEST, preferred_element_type=F32)


@functools.partial(jax.custom_vjp, nondiff_argnums=(2,))
def hdot(a, b, mode):
    return _hdot_raw(a, b, mode)


def _hdot_fwd(a, b, mode):
    return _hdot_raw(a, b, mode), (a, b)


def _hdot_bwd(mode, res, g):
    a, b = res
    if mode == "nn":
        return _hdot_raw(g, b, "nt"), _hdot_raw(a, g, "tn")
    if mode == "nt":
        return _hdot_raw(g, b, "nn"), _hdot_raw(g, a, "tn")
    return _hdot_raw(b, g, "nt"), _hdot_raw(a, g, "nn")


hdot.defvjp(_hdot_fwd, _hdot_bwd)


def _log_sigmoid(x):
    return jnp.minimum(x, 0.0) - jnp.log(1.0 + jnp.exp(-jnp.abs(x)))


def _sigmoid(x):
    return 1.0 / (1.0 + jnp.exp(-x))


def _expm1(x):
    series = x * (1.0 + x * 0.5 * (1.0 + x * (1.0 / 3.0) * (1.0 + x * 0.25)))
    return jnp.where(jnp.abs(x) < 0.03, series, jnp.exp(x) - 1.0)


def _gelu_tanh(x):
    return 0.5 * x * (1.0 + jnp.tanh(0.7978845608028654 * (x + 0.044715 * x * x * x)))


def _iota(shape, dim):
    return lax.broadcasted_iota(jnp.int32, shape, dim)


def _mm(a, b, mode, *, tm, tn, tk=None, out_dtype=F32, name, b_layer=None, into=None, relu_pair=False, times2=None):
    b2 = b.shape[-2:]
    if mode == "nn":
        (m, k), n = a.shape, b2[1]
    elif mode == "nt":
        (m, k), n = a.shape, b2[0]
    else:
        (k, m), n = a.shape, b2[1]
    tk = k if tk is None else tk
    assert m % tm == 0 and n % tn == 0 and k % tk == 0, (name, a.shape, b.shape)
    nk = k // tk
    a_spec = {"nn": pl.BlockSpec((tm, tk), lambda i, j, kk: (i, kk)),
              "nt": pl.BlockSpec((tm, tk), lambda i, j, kk: (i, kk)),
              "tn": pl.BlockSpec((tk, tm), lambda i, j, kk: (kk, i))}[mode]
    b_blk = {"nn": (tk, tn), "nt": (tn, tk), "tn": (tk, tn)}[mode]
    b_idx = {"nn": lambda i, j, kk: (kk, j), "nt": lambda i, j, kk: (j, kk), "tn": lambda i, j, kk: (kk, j)}[mode]
    if b_layer is None:
        b_spec = pl.BlockSpec(b_blk, b_idx)
    else:
        b_spec = pl.BlockSpec((None,) + b_blk, lambda i, j, kk: (b_layer,) + b_idx(i, j, kk))

    tile = pl.BlockSpec((tm, tn), lambda i, j, kk: (i, j))
    if into is not None:
        buf, per_slot, row_off = into
        assert m == 4 * per_slot and per_slot % tm == 0 and row_off % tm == 0 and buf.shape[2] == n, (name, buf.shape)
        bps = per_slot // tm
        out_specs = pl.BlockSpec((None, tm, tn), lambda i, j, kk: (i // bps, row_off // tm + i % bps, j))
        out_shape = jax.ShapeDtypeStruct(buf.shape, buf.dtype)
        extra_in, extra_specs, aliases = [buf], [pl.BlockSpec(memory_space=pl.ANY)], {2: 0}
        finish = lambda acc, extra: [acc.astype(buf.dtype)]
    elif relu_pair:
        out_specs = (tile, tile)
        out_shape = (jax.ShapeDtypeStruct((m, n), BF16),) * 2
        extra_in, extra_specs, aliases = [], [], {}

        def finish(acc, extra):
            r = jnp.maximum(acc, 0.0)
            return [(r * r).astype(BF16), r.astype(BF16)]
    elif times2 is not None:
        out_specs = tile
        out_shape = jax.ShapeDtypeStruct((m, n), out_dtype)
        extra_in, extra_specs, aliases = [times2], [tile], {}
        finish = lambda acc, extra: [(acc * (2.0 * extra[...].astype(F32))).astype(out_dtype)]
    else:
        out_specs = tile
        out_shape = jax.ShapeDtypeStruct((m, n), out_dtype)
        extra_in, extra_specs, aliases = [], [], {}
        finish = lambda acc, extra: [acc.astype(out_dtype)]
    n_out = 2 if relu_pair else 1

    def body(*refs):
        a_ref, b_ref = refs[0], refs[1]
        extra = refs[2] if extra_in else None
        o_refs = refs[2 + len(extra_in):2 + len(extra_in) + n_out]

        def store(acc):
            for o_ref, val in zip(o_refs, finish(acc, extra)):
                o_ref[...] = val

        if nk == 1:
            store(_dot(a_ref[...], b_ref[...], mode))
            return
        acc_ref = refs[-1]
        kk = pl.program_id(2)

        @pl.when(kk == 0)
        def _():
            acc_ref[...] = jnp.zeros_like(acc_ref)

        acc_ref[...] += _dot(a_ref[...], b_ref[...], mode)

        @pl.when(kk == nk - 1)
        def _():
            store(acc_ref[...])

    return pl.pallas_call(
        body, name=name, grid=(m // tm, n // tn, nk),
        in_specs=[a_spec, b_spec] + extra_specs,
        out_specs=out_specs, out_shape=out_shape,
        scratch_shapes=[pltpu.VMEM((tm, tn), F32)] if nk > 1 else [],
        input_output_aliases=aliases,
        compiler_params=_cp(("parallel", "parallel", "arbitrary")),
    )(a, b, *extra_in)


ROWS = 256


def _prenorm(x, w, name):
    def body(x_ref, w_ref, o_ref):
        xv = x_ref[...]
        r = lax.rsqrt(jnp.mean(xv * xv, axis=-1, keepdims=True) + EPS)
        o_ref[...] = (xv * r * w_ref[...]).astype(BF16)

    return pl.pallas_call(
        body, name=name, grid=(T // ROWS,),
        in_specs=[pl.BlockSpec((ROWS, D), lambda i: (i, 0)), pl.BlockSpec((1, D), lambda i: (0, 0))],
        out_specs=pl.BlockSpec((ROWS, D), lambda i: (i, 0)),
        out_shape=jax.ShapeDtypeStruct((T, D), BF16),
        compiler_params=_cp(("parallel",)),
    )(x, w)


def _rms(z):
    return lax.rsqrt(jnp.mean(z * z, axis=-1, keepdims=True) + EPS)


def _rms_bwd(z, w, dy):
    r = _rms(z)
    wdy = dy * w
    dz = r * wdy - z * (r * r * r) * jnp.mean(z * wdy, axis=-1, keepdims=True)
    return dz, jnp.sum(dy * z * r, axis=0, keepdims=True)


_ROW = pl.BlockSpec((ROWS, D), lambda i: (i, 0))
_VEC = pl.BlockSpec((1, D), lambda i: (0, 0))


def _post_pre_fwd(x, z, w_post, w_pre, name):
    def body(x_ref, z_ref, wp_ref, wn_ref, x_out, h_out):
        zv = z_ref[...]
        xn = x_ref[...] + zv * _rms(zv) * wp_ref[...]
        x_out[...] = xn
        h_out[...] = (xn * _rms(xn) * wn_ref[...]).astype(BF16)

    return pl.pallas_call(
        body, name=name, grid=(T // ROWS,), in_specs=[_ROW, _ROW, _VEC, _VEC], out_specs=(_ROW, _ROW),
        out_shape=(jax.ShapeDtypeStruct((T, D), F32), jax.ShapeDtypeStruct((T, D), BF16)),
        compiler_params=_cp(("parallel",)),
    )(x, z, w_post, w_pre)


def _post_loss(x, z, w_post, tgt):
    def body(x_ref, z_ref, w_ref, t_ref, g_ref, l_ref, dz_ref, dw_ref):
        @pl.when(pl.program_id(0) == 0)
        def _():
            l_ref[...] = jnp.zeros_like(l_ref)
            dw_ref[...] = jnp.zeros_like(dw_ref)

        zv = z_ref[...]
        e = x_ref[...] + zv * _rms(zv) * w_ref[...] - t_ref[...]
        g = e * (1.0 / D)
        g_ref[...] = g
        l_ref[...] += jnp.sum(e * e) * (0.5 / D)
        dz, dw = _rms_bwd(zv, w_ref[...], g)
        dz_ref[...] = dz.astype(BF16)
        dw_ref[...] += dw

    return pl.pallas_call(
        body, name="postnorm_loss", grid=(T // ROWS,), in_specs=[_ROW, _ROW, _VEC, _ROW],
        out_specs=(_ROW, pl.BlockSpec((1, 128), lambda i: (0, 0)), _ROW, _VEC),
        out_shape=(jax.ShapeDtypeStruct((T, D), F32), jax.ShapeDtypeStruct((1, 128), F32),
                   jax.ShapeDtypeStruct((T, D), BF16), jax.ShapeDtypeStruct((1, D), F32)),
        compiler_params=_cp(("arbitrary",)),
    )(x, z, w_post, tgt)


def _pre_post_bwd(x, w_pre, dh, add, z, w_post, name):
    def body(x_ref, wn_ref, dh_ref, add_ref, z_ref, wp_ref, g_ref, dz_ref, dwn_ref, dwp_ref):
        @pl.when(pl.program_id(0) == 0)
        def _():
            dwn_ref[...] = jnp.zeros_like(dwn_ref)
            dwp_ref[...] = jnp.zeros_like(dwp_ref)

        dx, dwn = _rms_bwd(x_ref[...], wn_ref[...], dh_ref[...])
        g = dx + add_ref[...]
        g_ref[...] = g
        dz, dwp = _rms_bwd(z_ref[...], wp_ref[...], g)
        dz_ref[...] = dz.astype(BF16)
        dwn_ref[...] += dwn
        dwp_ref[...] += dwp

    return pl.pallas_call(
        body, name=name, grid=(T // ROWS,), in_specs=[_ROW, _VEC, _ROW, _ROW, _ROW, _VEC],
        out_specs=(_ROW, _ROW, _VEC, _VEC),
        out_shape=(jax.ShapeDtypeStruct((T, D), F32), jax.ShapeDtypeStruct((T, D), BF16),
                   jax.ShapeDtypeStruct((1, D), F32), jax.ShapeDtypeStruct((1, D), F32)),
        compiler_params=_cp(("arbitrary",)),
    )(x, w_pre, dh, add, z, w_post)


def _norm_bwd(z, w, dy, add, name):
    has_add = add is not None

    def body(*refs):
        if has_add:
            z_ref, w_ref, dy_ref, add_ref, dz_ref, dw_ref = refs
        else:
            z_ref, w_ref, dy_ref, dz_ref, dw_ref = refs
        i = pl.program_id(0)

        @pl.when(i == 0)
        def _():
            dw_ref[...] = jnp.zeros_like(dw_ref)

        zv = z_ref[...].astype(F32)
        dyv = dy_ref[...]
        r = lax.rsqrt(jnp.mean(zv * zv, axis=-1, keepdims=True) + EPS)
        wdy = dyv * w_ref[...]
        dz = r * wdy - zv * (r * r * r) * jnp.mean(zv * wdy, axis=-1, keepdims=True)
        if has_add:
            dz = dz + add_ref[...]
        dz_ref[...] = dz.astype(dz_ref.dtype)
        dw_ref[...] += jnp.sum(dyv * zv * r, axis=0, keepdims=True)

    row = pl.BlockSpec((ROWS, D), lambda i: (i, 0))
    vec = pl.BlockSpec((1, D), lambda i: (0, 0))
    ins = [z, w, dy] + ([add] if has_add else [])
    dz_dtype = F32 if has_add else BF16
    return pl.pallas_call(
        body, name=name, grid=(T // ROWS,),
        in_specs=[row, vec, row] + ([row] if has_add else []),
        out_specs=(row, vec),
        out_shape=(jax.ShapeDtypeStruct((T, D), dz_dtype), jax.ShapeDtypeStruct((1, D), F32)),
        compiler_params=_cp(("arbitrary",)),
    )(*ins)


def _adamw_math(w, g, m, v):
    c1 = 1.0 - ADAM_B1 ** ADAM_STEP
    c2 = 1.0 - ADAM_B2 ** ADAM_STEP
    mn = ADAM_B1 * m + (1.0 - ADAM_B1) * g
    vn = ADAM_B2 * v + (1.0 - ADAM_B2) * (g * g)
    return -ADAM_LR * ((mn / c1) / (jnp.sqrt(vn / c2) + ADAM_EPS) + ADAM_WD * w), mn, vn


def _adamw_from(w, m, v, sources, tr, name):
    layers, rows, cols = w.shape
    assert len(sources) == layers and rows % tr == 0, (name, w.shape)
    g_specs = []
    for buf, row0, transposed in sources:
        if transposed:
            assert row0 % cols == 0 and buf.shape[1] == rows, (name, row0)
            g_specs.append(pl.BlockSpec((cols, tr), lambda l, i, b=row0 // cols: (b, i)))
        else:
            assert row0 % tr == 0 and buf.shape[1] == cols, (name, row0)
            g_specs.append(pl.BlockSpec((tr, cols), lambda l, i, b=row0 // tr: (b + i, 0)))

    def body(*refs):
        w_ref, m_ref, v_ref = refs[:3]
        g_refs = refs[3:3 + layers]
        g_out, d_ref, mo_ref, vo_ref = refs[3 + layers:]
        gs = [r[...].T if src[2] else r[...] for r, src in zip(g_refs, sources)]
        g = gs[0] if layers == 1 else jnp.where(pl.program_id(0) == 0, gs[0], gs[1])
        g_out[...] = g
        d_ref[...], mo_ref[...], vo_ref[...] = _adamw_math(w_ref[...], g, m_ref[...], v_ref[...])

    blk = pl.BlockSpec((None, tr, cols), lambda l, i: (l, i, 0))
    sds = jax.ShapeDtypeStruct(w.shape, F32)
    return pl.pallas_call(body, name=name, grid=(layers, rows // tr), in_specs=[blk] * 3 + g_specs,
                          out_specs=(blk,) * 4, out_shape=(sds,) * 4,
                          compiler_params=_cp(("parallel", "parallel")))(w, m, v, *[s[0] for s in sources])


def _adamw(w, g, m, v, name):
    lead = w.shape[:-2]
    assert len(lead) <= 1 and g.shape == w.shape, (name, w.shape, g.shape)
    rows, cols = w.shape[-2:]
    if rows <= 512:
        tr, tc = rows, cols
    elif rows % 256 == 0:
        tr, tc = 256, cols
    else:
        tr, tc = rows, 256
    assert rows % tr == 0 and cols % tc == 0, (name, w.shape)
    c1 = 1.0 - ADAM_B1 ** ADAM_STEP
    c2 = 1.0 - ADAM_B2 ** ADAM_STEP

    def body(w_ref, g_ref, m_ref, v_ref, d_ref, mo_ref, vo_ref):
        gv = g_ref[...]
        mn = ADAM_B1 * m_ref[...] + (1.0 - ADAM_B1) * gv
        vn = ADAM_B2 * v_ref[...] + (1.0 - ADAM_B2) * (gv * gv)
        m_hat = mn / c1
        v_hat = vn / c2
        d_ref[...] = -ADAM_LR * (m_hat / (jnp.sqrt(v_hat) + ADAM_EPS) + ADAM_WD * w_ref[...])
        mo_ref[...] = mn
        vo_ref[...] = vn

    if lead:
        grid = (lead[0], rows // tr, cols // tc)
        blk = pl.BlockSpec((None, tr, tc), lambda l, i, j: (l, i, j))
    else:
        grid = (rows // tr, cols // tc)
        blk = pl.BlockSpec((tr, tc), lambda i, j: (i, j))
    sds = jax.ShapeDtypeStruct(w.shape, F32)
    return pl.pallas_call(body, name=name, grid=grid, in_specs=[blk] * 4, out_specs=(blk,) * 3,
                          out_shape=(sds,) * 3, compiler_params=_cp(("parallel",) * len(grid)))(w, g, m, v)


def _gla_consts():
    ltri = (_iota((CHUNK, CHUNK), 0) >= _iota((CHUNK, CHUNK), 1)).astype(F32)
    ones_c = jnp.ones((CHUNK, 128), F32)
    mask = (_iota((256, 512), 0) // 64 == _iota((256, 512), 1) // 128).astype(F32)
    return ltri, ones_c, mask


def _gla_chunk(consts, q, k, v, r, aux, s_prev, wa, ba, nw):
    ltri, ones_c, mask = consts
    la = _log_sigmoid(bdot(aux, wa, "nn") + ba) * (1.0 / 16.0)
    cum = hdot(ltri, la, "nn")
    total = jnp.sum(la, axis=0, keepdims=True)
    k_dec = k * jnp.exp(total - cum)
    inc = bdot(k_dec, v, "tn") * mask
    dec = jnp.exp(hdot(la, ones_c, "tn"))
    dec = jnp.concatenate([dec, dec, dec, dec], axis=1)
    s_new = dec * s_prev + inc
    o = bdot(q * GLA_SCALE, s_new, "nn")
    parts = []
    for h in range(4):
        oh = o[:, h * 128:(h + 1) * 128]
        parts.append(oh * lax.rsqrt(jnp.mean(oh * oh, axis=-1, keepdims=True) + EPS))
    on = jnp.concatenate(parts, axis=1)
    return s_new, on * nw * (r * _sigmoid(r))


GLA_PER_STEP = 4
GLA_ROWS = GLA_PER_STEP * CHUNK
GLA_STEPS = NCHUNK // GLA_PER_STEP


def _gla_specs(cmap):
    return [pl.BlockSpec((GLA_ROWS, 256), lambda c: (cmap(c), 0)),
            pl.BlockSpec((GLA_ROWS, 256), lambda c: (cmap(c), 1)),
            pl.BlockSpec((GLA_ROWS, 512), lambda c: (cmap(c), 1)),
            pl.BlockSpec((GLA_ROWS, 512), lambda c: (cmap(c), 2)),
            pl.BlockSpec((GLA_ROWS, 128), lambda c: (cmap(c), AUX_BLK))]


def _gla_fwd(proj, wa, ba, nw):
    def body(q_ref, k_ref, v_ref, r_ref, aux_ref, wa_ref, ba_ref, nw_ref, o_ref, sp_ref, s_ref):
        @pl.when(pl.program_id(0) == 0)
        def _():
            s_ref[...] = jnp.zeros_like(s_ref)

        s = s_ref[...]
        consts = _gla_consts()
        outs, states = [], []
        for i in range(GLA_PER_STEP):
            rows = slice(i * CHUNK, (i + 1) * CHUNK)
            states.append(s)
            s, out = _gla_chunk(consts, q_ref[rows, :], k_ref[rows, :], v_ref[rows, :], r_ref[rows, :], aux_ref[rows, :],
                                s, wa_ref[...], ba_ref[...], nw_ref[...])
            outs.append(out)
        s_ref[...] = s
        for i in range(GLA_PER_STEP):
            o_ref[i * CHUNK:(i + 1) * CHUNK, :] = outs[i]
            sp_ref[i] = states[i]

    full = lambda shape: pl.BlockSpec(shape, lambda c: (0,) * len(shape))
    return pl.pallas_call(
        body, name="gla_fwd", grid=(GLA_STEPS,),
        in_specs=_gla_specs(lambda c: c) + [full((128, 256)), full((1, 256)), full((1, 512))],
        out_specs=(pl.BlockSpec((GLA_ROWS, 512), lambda c: (c, 0)),
                   pl.BlockSpec((GLA_PER_STEP, 256, 512), lambda c: (c, 0, 0))),
        out_shape=(jax.ShapeDtypeStruct((T, D), F32), jax.ShapeDtypeStruct((NCHUNK, 256, 512), F32)),
        scratch_shapes=[pltpu.VMEM((256, 512), F32)],
        compiler_params=_cp(("arbitrary",)),
    )(proj, proj, proj, proj, proj, wa, ba, nw)


def _gla_bwd(proj, s_prev_all, wa, ba, nw, dcat):
    rev = lambda c: GLA_STEPS - 1 - c

    def body(q_ref, k_ref, v_ref, r_ref, aux_ref, sp_ref, wa_ref, ba_ref, nw_ref, do_ref,
             dq_ref, dk_ref, dv_ref, dr_ref, daux_ref, dwa_ref, dba_ref, dnw_ref, ds_ref):
        @pl.when(pl.program_id(0) == 0)
        def _():
            ds_ref[...] = jnp.zeros_like(ds_ref)
            dwa_ref[...] = jnp.zeros_like(dwa_ref)
            dba_ref[...] = jnp.zeros_like(dba_ref)
            dnw_ref[...] = jnp.zeros_like(dnw_ref)

        fn = functools.partial(_gla_chunk, _gla_consts())
        ds = ds_ref[...]
        dwa, dba, dnw = dwa_ref[...], dba_ref[...], dnw_ref[...]
        grads = {}
        for i in reversed(range(GLA_PER_STEP)):
            rows = slice(i * CHUNK, (i + 1) * CHUNK)
            _, vjp = jax.vjp(fn, q_ref[rows, :], k_ref[rows, :], v_ref[rows, :], r_ref[rows, :], aux_ref[rows, :],
                             sp_ref[i], wa_ref[...], ba_ref[...], nw_ref[...])
            *grads[i], ds, dwa_i, dba_i, dnw_i = vjp((ds, do_ref[rows, :]))
            dwa, dba, dnw = dwa + dwa_i, dba + dba_i, dnw + dnw_i
        ds_ref[...] = ds
        dwa_ref[...] = dwa
        dba_ref[...] = dba
        dnw_ref[...] = dnw
        for i in range(GLA_PER_STEP):
            rows = slice(i * CHUNK, (i + 1) * CHUNK)
            for ref, g in zip((dq_ref, dk_ref, dv_ref, dr_ref, daux_ref), grads[i]):
                ref[rows, :] = g

    full = lambda shape: pl.BlockSpec(shape, lambda c: (0,) * len(shape))
    blk = lambda w: pl.BlockSpec((GLA_ROWS, w), lambda c: (rev(c), 0))
    sds = lambda *s: jax.ShapeDtypeStruct(s, F32)
    return pl.pallas_call(
        body, name="gla_bwd", grid=(GLA_STEPS,),
        in_specs=_gla_specs(rev) + [pl.BlockSpec((GLA_PER_STEP, 256, 512), lambda c: (rev(c), 0, 0)),
                                    full((128, 256)), full((1, 256)), full((1, 512)), blk(512)],
        out_specs=(blk(256), blk(256), blk(512), blk(512), blk(128), full((128, 256)), full((1, 256)), full((1, 512))),
        out_shape=(sds(T, 256), sds(T, 256), sds(T, 512), sds(T, 512), sds(T, 128),
                   sds(128, 256), sds(1, 256), sds(1, 512)),
        scratch_shapes=[pltpu.VMEM((256, 512), F32)],
        compiler_params=_cp(("arbitrary",)),
    )(proj, proj, proj, proj, proj, s_prev_all, wa, ba, nw, dcat)


GATE_ROWS = 128


def _fox_gate_block(ltri, aux, bpad, carry):
    lf = _log_sigmoid(aux + bpad)
    cum = hdot(ltri, lf, "nn") + carry
    return cum, carry + jnp.sum(lf, axis=0, keepdims=True)


def _gate_ltri():
    return (_iota((GATE_ROWS, GATE_ROWS), 0) >= _iota((GATE_ROWS, GATE_ROWS), 1)).astype(F32)


def _fox_gate_fwd(proj, bpad):
    def body(aux_ref, b_ref, cum_ref, carry_ref):
        i = pl.program_id(0)

        @pl.when(i == 0)
        def _():
            carry_ref[...] = jnp.zeros_like(carry_ref)

        cum, carry = _fox_gate_block(_gate_ltri(), aux_ref[...], b_ref[...], carry_ref[...])
        cum_ref[...] = cum
        carry_ref[...] = carry

    return pl.pallas_call(
        body, name="fox_gate_fwd", grid=(T // GATE_ROWS,),
        in_specs=[pl.BlockSpec((GATE_ROWS, 128), lambda i: (i, AUX_BLK)), pl.BlockSpec((1, 128), lambda i: (0, 0))],
        out_specs=pl.BlockSpec((GATE_ROWS, 128), lambda i: (i, 0)),
        out_shape=jax.ShapeDtypeStruct((T, 128), F32),
        scratch_shapes=[pltpu.VMEM((1, 128), F32)],
        compiler_params=_cp(("arbitrary",)),
    )(proj, bpad)


def _fox_gate_bwd(proj, bpad, dccol_t, daux_gla):
    nb = T // GATE_ROWS
    rev = lambda i: nb - 1 - i

    def body(aux_ref, b_ref, dc_ref, dg_ref, daux_ref, db_ref, dcarry_ref):
        i = pl.program_id(0)

        @pl.when(i == 0)
        def _():
            dcarry_ref[...] = jnp.zeros_like(dcarry_ref)
            db_ref[...] = jnp.zeros_like(db_ref)

        dcum = dc_ref[...]
        fn = functools.partial(_fox_gate_block, _gate_ltri())
        _, vjp = jax.vjp(fn, aux_ref[...], b_ref[...], jnp.zeros((1, 128), F32))
        daux, db, dcarry = vjp((dcum, dcarry_ref[...]))
        daux_ref[...] = daux + dg_ref[...]
        db_ref[...] += db
        dcarry_ref[...] = dcarry

    blk = pl.BlockSpec((GATE_ROWS, 128), lambda i: (rev(i), 0))
    vec = pl.BlockSpec((1, 128), lambda i: (0, 0))
    return pl.pallas_call(
        body, name="fox_gate_bwd", grid=(nb,),
        in_specs=[pl.BlockSpec((GATE_ROWS, 128), lambda i: (rev(i), AUX_BLK)), vec, blk, blk],
        out_specs=(blk, vec),
        out_shape=(jax.ShapeDtypeStruct((T, 128), F32), jax.ShapeDtypeStruct((1, 128), F32)),
        scratch_shapes=[pltpu.VMEM((1, 128), F32)],
        compiler_params=_cp(("arbitrary",)),
    )(proj, bpad, dccol_t, daux_gla)


FOX_Q = 128


FOX_QB = T // FOX_Q


@jax.custom_vjp
def _attend(s, v):
    return _attend_fwd(s, v)[0]


def _attend_fwd(s, v):
    e = jnp.exp(s - jnp.max(s, axis=-1, keepdims=True))
    r = 1.0 / jnp.sum(e, axis=-1, keepdims=True)
    return _dot(e, v, "nn") * r, (e, r, v)


def _attend_bwd(res, do):
    e, r, v = res
    do_r = do * r
    dpr = _dot(do_r, v, "nt")
    ds = e * (dpr - r * jnp.sum(e * dpr, axis=-1, keepdims=True))
    return ds, _dot(e, do_r, "tn").astype(v.dtype)


_attend.defvjp(_attend_fwd, _attend_bwd)


def _fox_block(hp, q, k, v, ccol):
    kl = k.shape[0]
    lane = _iota((FOX_Q, 128), 1)
    tri = jnp.bitwise_and(_iota((2 * FOX_Q, FOX_Q), 0), FOX_Q - 1) >= _iota((2 * FOX_Q, FOX_Q), 1)
    sub = _iota((8, kl), 0)
    qs = q * ATT_SCALE
    q2 = jnp.concatenate([jnp.where(lane < 64, qs, 0.0), jnp.where(lane >= 64, qs, 0.0)], axis=0)
    s = bdot(q2, k, "nt")
    cs = [jnp.sum(jnp.where(sub == 2 * hp + e, ccol, 0.0), axis=0, keepdims=True) for e in range(2)]
    s = jnp.concatenate([s[:FOX_Q] - cs[0], s[FOX_Q:] - cs[1]], axis=0)
    diag = jnp.where(tri, s[:, kl - FOX_Q:], NEG)
    s = diag if kl == FOX_Q else jnp.concatenate([s[:, :kl - FOX_Q], diag], axis=1)
    o2 = _attend(s, v)
    return jnp.where(lane < 64, o2[:FOX_Q], o2[FOX_Q:])


def _fox_in_specs():
    return [pl.BlockSpec((FOX_Q, 128), lambda hp, qb: (qb, 12 + hp)),
            pl.BlockSpec((T, 128), lambda hp, qb: (0, 16 + hp)),
            pl.BlockSpec((T, 128), lambda hp, qb: (0, 20 + hp)),
            pl.BlockSpec((8, T), lambda hp, qb: (0, 0))]


def _fox_fwd(proj, cum_c, cat):
    def body(q_ref, k_ref, v_ref, cc_ref, cat_ref, o_ref):
        qb = pl.program_id(1)
        for g in range(FOX_QB):
            kl = FOX_Q * (g + 1)

            @pl.when(qb == g)
            def _(kl=kl):
                o_ref[...] = _fox_block(pl.program_id(0), q_ref[...], k_ref[0:kl, :], v_ref[0:kl, :], cc_ref[:, 0:kl])

    return pl.pallas_call(
        body, name="fox_fwd", grid=(4, FOX_QB), in_specs=_fox_in_specs() + [pl.BlockSpec(memory_space=pl.ANY)],
        out_specs=pl.BlockSpec((FOX_Q, 128), lambda hp, qb: (qb, 4 + hp)),
        out_shape=jax.ShapeDtypeStruct((T, D), F32), input_output_aliases={4: 0},
        compiler_params=_cp(("parallel", "parallel")),
    )(proj, proj, proj, cum_c, cat)


def _fox_bwd(proj, cum_c, dcat):
    def body(q_ref, k_ref, v_ref, cc_ref, do_ref, dq_ref, dk_ref, dv_ref, dcc_ref):
        qb = pl.program_id(1)

        @pl.when(qb == 0)
        def _():
            dk_ref[...] = jnp.zeros_like(dk_ref)
            dv_ref[...] = jnp.zeros_like(dv_ref)
            dcc_ref[...] = jnp.zeros_like(dcc_ref)

        fn = functools.partial(_fox_block, pl.program_id(0))
        for g in range(FOX_QB):
            kl = FOX_Q * (g + 1)

            @pl.when(qb == g)
            def _(kl=kl):
                _, vjp = jax.vjp(fn, q_ref[...], k_ref[0:kl, :], v_ref[0:kl, :], cc_ref[:, 0:kl])
                dq, dk, dv, dcc = vjp(do_ref[...])
                dq_ref[...] = dq
                dk_ref[0:kl, :] += dk
                dv_ref[0:kl, :] += dv
                dcc_ref[:, 0:kl] += dcc

    sds = lambda *s: jax.ShapeDtypeStruct(s, F32)
    return pl.pallas_call(
        body, name="fox_bwd", grid=(4, FOX_QB),
        in_specs=_fox_in_specs() + [pl.BlockSpec((FOX_Q, 128), lambda hp, qb: (qb, 4 + hp))],
        out_specs=(pl.BlockSpec((FOX_Q, 128), lambda hp, qb: (qb, hp)),
                   pl.BlockSpec((T, 128), lambda hp, qb: (0, hp)),
                   pl.BlockSpec((T, 128), lambda hp, qb: (0, hp)),
                   pl.BlockSpec((None, 8, T), lambda hp, qb: (hp, 0, 0))),
        out_shape=(sds(T, 512), sds(T, 512), sds(T, 512), sds(4, 8, T)),
        compiler_params=_cp(("parallel", "arbitrary")),
    )(proj, proj, proj, cum_c, dcat)


BIAS_W = 640


def _rel_onehot():
    j = _iota((REL_PAD, BIAS_W), 1)
    rel = jnp.clip(CA_PAD + CHUNK - 1 - j, -128, 128) + 128
    return (_iota((REL_PAD, BIAS_W), 0) == rel).astype(F32)


def _bias_build(rbp):
    def body(rb_ref, o_ref):
        f = _hdot_raw(rb_ref[...], _rel_onehot(), "nn")
        for q in range(CHUNK):
            o_ref[q] = pltpu.roll(f, (BIAS_W - (CHUNK - 1 - q)) % BIAS_W, 1)[:, :CA_BAND]

    return pl.pallas_call(body, name="ca_bias_build", out_shape=jax.ShapeDtypeStruct((CHUNK, 8, CA_BAND), F32))(rbp)


def _bias_grad(dbias_q):
    def body(db_ref, o_ref):
        acc = jnp.zeros((8, BIAS_W), F32)
        for q in range(CHUNK):
            acc = acc + pltpu.roll(db_ref[q], CHUNK - 1 - q, 1)
        o_ref[...] = _hdot_raw(acc, _rel_onehot(), "nt")

    return pl.pallas_call(body, name="ca_bias_grad", out_shape=jax.ShapeDtypeStruct((8, REL_PAD), F32))(dbias_q)


def _ca_block(c, masked, q, kb, vb, bias2):
    lane = _iota((CHUNK, 128), 1)
    qs = q * ATT_SCALE
    q2 = jnp.concatenate([jnp.where(lane < 64, qs, 0.0), jnp.where(lane >= 64, qs, 0.0)], axis=0)
    s = bdot(q2, kb, "nt") + bias2.reshape(2 * CHUNK, CA_BAND)
    if masked:
        s = jnp.where((c * CHUNK - CA_PAD + _iota((2 * CHUNK, CA_BAND), 1)) >= 0, s, NEG)
    o2 = _attend(s, vb)
    return jnp.where(lane < 64, o2[:CHUNK], o2[CHUNK:])


CA_PER_STEP = 8
CA_ROWS = CA_PER_STEP * CHUNK
CA_MASKED_STEPS = CA_PAD // CA_ROWS


def _ca_fwd(proj, kvpad, bias):
    def body(q_ref, k_ref, v_ref, b_ref, o_ref):
        def run(masked):
            outs = []
            for i in range(CA_PER_STEP):
                c = pl.program_id(1) * CA_PER_STEP + i
                band = pl.ds(pl.multiple_of(c * CHUNK, CHUNK), CA_BAND)
                rows = slice(i * CHUNK, (i + 1) * CHUNK)
                outs.append(_ca_block(c, masked, q_ref[rows, :], k_ref[band, :], v_ref[band, :], b_ref[...]))
            for i in range(CA_PER_STEP):
                o_ref[i * CHUNK:(i + 1) * CHUNK, :] = outs[i]

        pl.when(pl.program_id(1) < CA_MASKED_STEPS)(lambda: run(True))
        pl.when(pl.program_id(1) >= CA_MASKED_STEPS)(lambda: run(False))

    return pl.pallas_call(
        body, name="ca_fwd", grid=(4, NCHUNK // CA_PER_STEP),
        in_specs=[pl.BlockSpec((CA_ROWS, 128), lambda hp, c: (c, hp)),
                  pl.BlockSpec((T + CA_PAD, 128), lambda hp, c: (0, hp)),
                  pl.BlockSpec((T + CA_PAD, 128), lambda hp, c: (0, 4 + hp)),
                  pl.BlockSpec((2, CHUNK, CA_BAND), lambda hp, c: (hp, 0, 0))],
        out_specs=pl.BlockSpec((CA_ROWS, 128), lambda hp, c: (c, hp)),
        out_shape=jax.ShapeDtypeStruct((T, D), F32),
        compiler_params=_cp(("parallel", "parallel")),
    )(proj, kvpad, kvpad, bias)


def _ca_bwd(proj, kvpad, bias, dcat):
    def body(q_ref, k_ref, v_ref, b_ref, do_ref, dq_ref, dk_ref, dv_ref, db_ref):
        c = pl.program_id(1)

        @pl.when(c == 0)
        def _():
            dk_ref[...] = jnp.zeros_like(dk_ref)
            dv_ref[...] = jnp.zeros_like(dv_ref)
            db_ref[...] = jnp.zeros_like(db_ref)

        def run(masked):
            grads, bands = [], []
            for i in range(CA_PER_STEP):
                ci = c * CA_PER_STEP + i
                band = pl.ds(pl.multiple_of(ci * CHUNK, CHUNK), CA_BAND)
                rows = slice(i * CHUNK, (i + 1) * CHUNK)
                fn = functools.partial(_ca_block, ci, masked)
                _, vjp = jax.vjp(fn, q_ref[rows, :], k_ref[band, :], v_ref[band, :], b_ref[...])
                grads.append(vjp(do_ref[rows, :]))
                bands.append(band)
            for i, (dq, _, _, _) in enumerate(grads):
                dq_ref[i * CHUNK:(i + 1) * CHUNK, :] = dq
            for band, (_, dkb, dvb, _) in zip(bands, grads):
                dk_ref[band, :] += dkb
                dv_ref[band, :] += dvb
            db_ref[...] += functools.reduce(lambda a, b: a + b, [g[3] for g in grads])

        pl.when(c < CA_MASKED_STEPS)(lambda: run(True))
        pl.when(c >= CA_MASKED_STEPS)(lambda: run(False))

    sds = lambda *s: jax.ShapeDtypeStruct(s, F32)
    padded = lambda: pl.BlockSpec((T + CA_PAD, 128), lambda hp, c: (0, hp))
    return pl.pallas_call(
        body, name="ca_bwd", grid=(4, NCHUNK // CA_PER_STEP),
        in_specs=[pl.BlockSpec((CA_ROWS, 128), lambda hp, c: (c, hp)),
                  pl.BlockSpec((T + CA_PAD, 128), lambda hp, c: (0, hp)),
                  pl.BlockSpec((T + CA_PAD, 128), lambda hp, c: (0, 4 + hp)),
                  pl.BlockSpec((2, CHUNK, CA_BAND), lambda hp, c: (hp, 0, 0)),
                  pl.BlockSpec((CA_ROWS, 128), lambda hp, c: (c, hp))],
        out_specs=(pl.BlockSpec((CA_ROWS, 128), lambda hp, c: (c, hp)), padded(), padded(),
                   pl.BlockSpec((2, CHUNK, CA_BAND), lambda hp, c: (hp, 0, 0))),
        out_shape=(sds(T, 512), sds(T + CA_PAD, 512), sds(T + CA_PAD, 512), sds(8, CHUNK, CA_BAND)),
        compiler_params=_cp(("parallel", "arbitrary")),
    )(proj, kvpad, kvpad, bias, dcat)


def _lru_pre(xs, cw, cb, wa, ba, wx, bx, lam):
    xc = cb + xs[0] * cw[0:1, :] + xs[1] * cw[1:2, :] + xs[2] * cw[2:3, :] + xs[3] * cw[3:4, :]
    ra = _sigmoid(bdot(xc, wa, "nn") + ba)
    ii = _sigmoid(bdot(xc, wx, "nn") + bx)
    la = 8.0 * ra * _log_sigmoid(lam)
    return jnp.exp(la), jnp.sqrt(-_expm1(2.0 * la)) * (ii * xc)


def _lru_pre_specs():
    full = lambda shape: pl.BlockSpec(shape, lambda i: (0,) * len(shape))
    return [pl.BlockSpec((4, ROWS, 512), lambda i: (0, i, 0)), full((4, 512)), full((1, 512)),
            full((512, 512)), full((1, 512)), full((512, 512)), full((1, 512)), full((1, 512))]


def _lru_pre_fwd(xs, cw, cb, wa, ba, wx, bx, lam):
    def body(xs_ref, cw_ref, cb_ref, wa_ref, ba_ref, wx_ref, bx_ref, lam_ref, a_ref, b_ref):
        a, b = _lru_pre(xs_ref[...], cw_ref[...], cb_ref[...], wa_ref[...], ba_ref[...], wx_ref[...], bx_ref[...],
                        lam_ref[...])
        a_ref[...] = a
        b_ref[...] = b

    row = pl.BlockSpec((ROWS, 512), lambda i: (i, 0))
    sds = jax.ShapeDtypeStruct((T, 512), F32)
    return pl.pallas_call(body, name="lru_pre_fwd", grid=(T // ROWS,), in_specs=_lru_pre_specs(),
                          out_specs=(row, row), out_shape=(sds, sds), compiler_params=_cp(("parallel",)),
                          )(xs, cw, cb, wa, ba, wx, bx, lam)


def _lru_pre_bwd(xs, cw, cb, wa, ba, wx, bx, lam, da, db):
    def body(xs_ref, cw_ref, cb_ref, wa_ref, ba_ref, wx_ref, bx_ref, lam_ref, da_ref, db_ref,
             dxs_ref, dcw_ref, dcb_ref, dwa_ref, dba_ref, dwx_ref, dbx_ref, dlam_ref):
        acc = (dcw_ref, dcb_ref, dwa_ref, dba_ref, dwx_ref, dbx_ref, dlam_ref)

        @pl.when(pl.program_id(0) == 0)
        def _():
            for r in acc:
                r[...] = jnp.zeros_like(r)

        _, vjp = jax.vjp(_lru_pre, xs_ref[...], cw_ref[...], cb_ref[...], wa_ref[...], ba_ref[...], wx_ref[...],
                         bx_ref[...], lam_ref[...])
        grads = vjp((da_ref[...], db_ref[...]))
        dxs_ref[...] = grads[0]
        for r, g in zip(acc, grads[1:]):
            r[...] += g

    row = pl.BlockSpec((ROWS, 512), lambda i: (i, 0))
    specs = _lru_pre_specs()
    sds = lambda *s: jax.ShapeDtypeStruct(s, F32)
    return pl.pallas_call(
        body, name="lru_pre_bwd", grid=(T // ROWS,), in_specs=specs + [row, row], out_specs=tuple(specs),
        out_shape=(sds(4, T, 512), sds(4, 512), sds(1, 512), sds(512, 512), sds(1, 512), sds(512, 512), sds(1, 512),
                   sds(1, 512)),
        compiler_params=_cp(("arbitrary",)),
    )(xs, cw, cb, wa, ba, wx, bx, lam, da, db)


def _lru_scan_fwd(a, b):
    def body(a_ref, b_ref, h_ref):
        def step(t, h):
            h = a_ref[pl.ds(t, 1), :] * h + b_ref[pl.ds(t, 1), :]
            h_ref[pl.ds(t, 1), :] = h
            return h

        lax.fori_loop(0, T, step, jnp.zeros((1, 512), F32), unroll=8)

    return pl.pallas_call(body, name="lru_scan_fwd", out_shape=jax.ShapeDtypeStruct((T, 512), F32),
                          compiler_params=pltpu.CompilerParams(vmem_limit_bytes=VMEM_LIMIT))(a, b)


def _lru_scan_bwd(a, h, dh):
    def body(a_ref, h_ref, dh_ref, da_ref, db_ref):
        def step(i, carry):
            t = T - 1 - i
            g = dh_ref[pl.ds(t, 1), :] + carry
            db_ref[pl.ds(t, 1), :] = g
            da_ref[pl.ds(t, 1), :] = g * h_ref[pl.ds(t - 1, 1), :]
            return a_ref[pl.ds(t, 1), :] * g

        carry = lax.fori_loop(0, T - 1, step, jnp.zeros((1, 512), F32), unroll=8)
        db_ref[pl.ds(0, 1), :] = dh_ref[pl.ds(0, 1), :] + carry
        da_ref[pl.ds(0, 1), :] = jnp.zeros((1, 512), F32)

    sds = jax.ShapeDtypeStruct((T, 512), F32)
    return pl.pallas_call(body, name="lru_scan_bwd", out_shape=(sds, sds),
                          compiler_params=pltpu.CompilerParams(vmem_limit_bytes=VMEM_LIMIT))(a, h, dh)


def _lru_post(h, gate):
    return h * _gelu_tanh(gate)


def _lru_post_fwd(h, proj, cat):
    def body(h_ref, g_ref, cat_ref, o_ref):
        o_ref[...] = _lru_post(h_ref[...], g_ref[...])

    row = pl.BlockSpec((ROWS, 512), lambda i: (i, 0))
    return pl.pallas_call(body, name="lru_post_fwd", grid=(T // ROWS,),
                          in_specs=[row, pl.BlockSpec((ROWS, 512), lambda i: (i, 3)), pl.BlockSpec(memory_space=pl.ANY)],
                          out_specs=pl.BlockSpec((ROWS, 512), lambda i: (i, 1)),
                          out_shape=jax.ShapeDtypeStruct((T, D), F32), input_output_aliases={2: 0},
                          compiler_params=_cp(("parallel",)))(h, proj, cat)


def _lru_post_bwd(h, proj, dcat):
    def body(h_ref, g_ref, do_ref, dh_ref, dg_ref):
        _, vjp = jax.vjp(_lru_post, h_ref[...], g_ref[...])
        dh, dg = vjp(do_ref[...])
        dh_ref[...] = dh
        dg_ref[...] = dg

    row = pl.BlockSpec((ROWS, 512), lambda i: (i, 0))
    sds = jax.ShapeDtypeStruct((T, 512), F32)
    return pl.pallas_call(body, name="lru_post_bwd", grid=(T // ROWS,),
                          in_specs=[row, pl.BlockSpec((ROWS, 512), lambda i: (i, 3)),
                                    pl.BlockSpec((ROWS, 512), lambda i: (i, 1))],
                          out_specs=(row, row), out_shape=(sds, sds), compiler_params=_cp(("parallel",)))(h, proj, dcat)


def _conv_dx(dxs_shift):
    def body(d_ref, o_ref):
        o_ref[...] = d_ref[0] + d_ref[1] + d_ref[2] + d_ref[3]

    row = pl.BlockSpec((ROWS, 512), lambda i: (i, 0))
    return pl.pallas_call(body, name="lru_conv_dx", grid=(T // ROWS,),
                          in_specs=[pl.BlockSpec((4, ROWS, 512), lambda i: (0, i, 0))], out_specs=row,
                          out_shape=jax.ShapeDtypeStruct((T, 512), F32), compiler_params=_cp(("parallel",)))(dxs_shift)


def _position():
    return lax.axis_index("x"), lax.axis_index("y"), lax.axis_index("c")


def _other_chips(x, y):
    return [(1 - x, y), (x, 1 - y), (1 - x, 1 - y)]


def _al(v, n):
    return v * n if isinstance(v, int) else pl.multiple_of(v * n, n)


_AG_ITEMS = [
    ((4, 32, 128), lambda o, s, h: o.at[s, pl.ds(_al(h, 16), 16), :], lambda r, h: r.at[pl.ds(_al(h, 16), 16), :]),
    ((4, 774, 1024), lambda o, s, h: o.at[s, :, pl.ds(_al(h, 512), 512)], lambda r, h: r.at[:, pl.ds(_al(h, 512), 512)]),
    ((1024, 1024), lambda o, s, h: o.at[pl.ds(_al(2 * s + h, 128), 128), :], lambda r, h: r.at[pl.ds(_al(h, 128), 128), :]),
    ((2, 1024, 4096), lambda o, s, h: o.at[h, :, pl.ds(_al(s, 1024), 1024)], lambda r, h: r.at[h]),
    ((2, 4096, 1024), lambda o, s, h: o.at[h, pl.ds(_al(s, 1024), 1024), :], lambda r, h: r.at[h]),
    ((1024, 2560), lambda o, s, h: o.at[pl.ds(_al(h, 512), 512), pl.ds(_al(s, 640), 640)],
     lambda r, h: r.at[pl.ds(_al(h, 512), 512), :]),
    ((1024, 1024), lambda o, s, h: o.at[pl.ds(_al(2 * s + h, 128), 128), :], lambda r, h: r.at[pl.ds(_al(h, 128), 128), :]),
]


_AG_GROUPS = [(0, 1, 2), (3, 4), (5, 6)]

_HBM = pl.BlockSpec(memory_space=pltpu.HBM)
_SEM = pl.BlockSpec(memory_space=pltpu.SEMAPHORE)
_SPLIT = dict(has_side_effects=pltpu.SideEffectType.DATAFLOW_SIDE_EFFECTING)


def _hbm(a):
    return pltpu.with_memory_space_constraint(a, pltpu.HBM)


def _ag_ici_copy(i, j, chip, c, slot, src_ref, land_ref, send_sems, recv_sems, k):
    _, dst, half = _AG_ITEMS[i]
    return pltpu.make_async_remote_copy(src_ref=half(src_ref, c), dst_ref=dst(land_ref, slot, c), send_sem=send_sems.at[k],
                                        recv_sem=recv_sems.at[k], device_id=(*chip, c), device_id_type=MESH)


def _ag_start(shards):
    n = len(_AG_ITEMS)
    ng = len(_AG_GROUPS)
    lands = [lax.empty(shape, s.dtype) for (shape, _, _), s in zip(_AG_ITEMS, shards)]

    def body(*refs):
        srcs, land_refs = refs[:n], refs[n:2 * n]
        sems = refs[2 * n:2 * n + 2 * ng]
        token = refs[-1]
        x, y, c = _position()
        me = 2 * x + y
        for g, items in enumerate(_AG_GROUPS):
            for t, i in enumerate(items):
                for j, chip in enumerate(_other_chips(x, y)):
                    _ag_ici_copy(i, j, chip, c, me, srcs[i], land_refs[i], sems[2 * g], sems[2 * g + 1], 3 * t + j).start()
        token[...] = jnp.zeros_like(token)

    sem_shapes = []
    for items in _AG_GROUPS:
        sem_shapes += [pltpu.SemaphoreType.DMA((3 * len(items),))] * 2
    thru = [pltpu.HBM(a.shape, a.dtype) for a in list(shards) + lands]
    out = pl.pallas_call(
        body, name="allgather_start",
        out_shape=tuple(sem_shapes) + tuple(thru) + (jax.ShapeDtypeStruct((8, 128), F32),),
        in_specs=(_HBM,) * (2 * n),
        out_specs=(_SEM,) * (2 * ng) + (_HBM,) * (2 * n) + (pl.BlockSpec(memory_space=pltpu.VMEM),),
        input_output_aliases={i: 2 * ng + i for i in range(2 * n)},
        compiler_params=pltpu.CompilerParams(**_SPLIT),
    )(*[_hbm(a) for a in list(shards) + lands])
    sems, thru, token = out[:2 * ng], out[2 * ng:-1], out[-1]
    return [(sems[2 * g], sems[2 * g + 1]) for g in range(ng)], list(thru[:n]), list(thru[n:]), token


def _ag_wait(g, sems, srcs, lands, after):
    items = _AG_GROUPS[g]
    m = len(items)

    def body(*refs):
        src_refs, land_refs = refs[:m], refs[m:2 * m]
        send_sems, recv_sems = refs[2 * m], refs[2 * m + 1]
        x, y, c = _position()
        for t, i in enumerate(items):
            for j, chip in enumerate(_other_chips(x, y)):
                cp = _ag_ici_copy(i, j, chip, c, 2 * chip[0] + chip[1], src_refs[t], land_refs[t], send_sems, recv_sems,
                                  3 * t + j)
                cp.wait_send()
                cp.wait_recv()

    ops = [srcs[i] for i in items] + [lands[i] for i in items]
    out = pl.pallas_call(
        body, name=f"allgather_wait_{g}",
        out_shape=tuple(pltpu.HBM(a.shape, a.dtype) for a in ops),
        in_specs=(_HBM,) * (2 * m) + (_SEM, _SEM, pl.BlockSpec(memory_space=pl.ANY)),
        out_specs=(_HBM,) * (2 * m),
        input_output_aliases={i: i for i in range(2 * m)},
        compiler_params=pltpu.CompilerParams(**_SPLIT),
    )(*ops, sems[0], sems[1], after)
    return list(out[:m]), list(out[m:])


def _ag_forward(g, srcs, lands):
    return _ag_sibling(_AG_GROUPS[g], srcs, lands, False, f"allgather_forward_{g}")


def _ag_push_own(srcs, lands):
    return _ag_sibling(tuple(range(len(_AG_ITEMS))), srcs, lands, True, "allgather_push_own")


def _ag_sibling(items, srcs, lands, own, name):
    m = len(items)
    per = 2 if own else 3

    def body(*refs):
        src_refs, in_refs, out_refs = refs[:m], refs[m:2 * m], refs[2 * m:3 * m]
        send_sems, recv_sems = refs[3 * m:]
        x, y, c = _position()
        sibling = (x, y, 1 - c)
        me = 2 * x + y
        if own:
            mine = theirs = [(me, 0), (me, 1)]
        else:
            slots = [2 * chip[0] + chip[1] for chip in _other_chips(x, y)]
            mine, theirs = [(s, c) for s in slots], [(s, 1 - c) for s in slots]
        sends = []
        for t, i in enumerate(items):
            _, dst, half = _AG_ITEMS[i]
            for k, (slot, hc) in enumerate(mine):
                src = half(src_refs[t], hc) if own else dst(in_refs[t], slot, hc)
                sends.append(pltpu.make_async_remote_copy(
                    src_ref=src, dst_ref=dst(out_refs[t], slot, hc), send_sem=send_sems.at[per * t + k],
                    recv_sem=recv_sems.at[per * t + k], device_id=sibling, device_id_type=MESH))
        for cp in sends:
            cp.start()
        for t, i in enumerate(items):
            dst = _AG_ITEMS[i][1]
            for k, (slot, hc) in enumerate(theirs):
                there = dst(out_refs[t], slot, hc)
                pltpu.make_async_remote_copy(src_ref=there, dst_ref=there, send_sem=send_sems.at[per * t + k],
                                             recv_sem=recv_sems.at[per * t + k], device_id=sibling,
                                             device_id_type=MESH).wait_recv()
        for cp in sends:
            cp.wait_send()

    any_spec = pl.BlockSpec(memory_space=pl.ANY)
    return pl.pallas_call(
        body, name=name,
        in_specs=[any_spec] * (2 * m), out_specs=(any_spec,) * m,
        out_shape=tuple(jax.ShapeDtypeStruct(a.shape, a.dtype) for a in lands),
        input_output_aliases={m + t: t for t in range(m)},
        scratch_shapes=[pltpu.SemaphoreType.DMA((per * m,)), pltpu.SemaphoreType.DMA((per * m,))],
    )(*srcs, *lands)


def _pair_swap_copy(g_ref, r_ref, send_sem, recv_sem):
    x, y, c = _position()
    hc = g_ref.shape[2] // 2
    return pltpu.make_async_remote_copy(src_ref=g_ref.at[:, :, pl.ds(_al(1 - c, hc), hc)], dst_ref=r_ref,
                                        send_sem=send_sem, recv_sem=recv_sem, device_id=(x, y, 1 - c),
                                        device_id_type=MESH)


def _pair_swap_start(gb, tag):
    _, rows, cols = gb.shape
    recv = lax.empty((4, rows, cols // 2), gb.dtype)

    def body(g_ref, r_ref, send_sem, recv_sem, g_thru, r_thru, token):
        _pair_swap_copy(g_ref, r_ref, send_sem, recv_sem).start()
        token[...] = jnp.zeros_like(token)

    return pl.pallas_call(
        body, name="grad_pair_swap_start_" + tag,
        out_shape=(pltpu.SemaphoreType.DMA(()), pltpu.SemaphoreType.DMA(()), pltpu.HBM(gb.shape, gb.dtype),
                   pltpu.HBM(recv.shape, recv.dtype), jax.ShapeDtypeStruct((8, 128), F32)),
        in_specs=(_HBM, _HBM), out_specs=(_SEM, _SEM, _HBM, _HBM, pl.BlockSpec(memory_space=pltpu.VMEM)),
        input_output_aliases={0: 2, 1: 3},
        compiler_params=pltpu.CompilerParams(**_SPLIT),
    )(_hbm(gb), _hbm(recv))


def _pair_swap_wait(started, after, tag):
    send_sem, recv_sem, gb, recv, _ = started

    def body(g_ref, r_ref, send_sem, recv_sem, after_ref, g_out, r_out):
        cp = _pair_swap_copy(g_ref, r_ref, send_sem, recv_sem)
        cp.wait_send()
        cp.wait_recv()

    return pl.pallas_call(
        body, name="grad_pair_swap_wait_" + tag,
        out_shape=(pltpu.HBM(gb.shape, gb.dtype), pltpu.HBM(recv.shape, recv.dtype)),
        in_specs=(_HBM, _HBM, _SEM, _SEM, pl.BlockSpec(memory_space=pl.ANY)), out_specs=(_HBM, _HBM),
        input_output_aliases={0: 0, 1: 1},
        compiler_params=pltpu.CompilerParams(**_SPLIT),
    )(gb, recv, send_sem, recv_sem, after)


def _handover(red, tag):
    hc = red.shape[1] // 2

    def body(in_ref, out_ref, send_sem, recv_sem):
        x, y, c = _position()
        mine = pl.ds(_al(c, hc), hc)
        cp = pltpu.make_async_remote_copy(src_ref=in_ref.at[:, mine], dst_ref=out_ref.at[:, mine], send_sem=send_sem,
                                          recv_sem=recv_sem, device_id=(x, y, 1 - c), device_id_type=MESH)
        cp.start()
        theirs = out_ref.at[:, pl.ds(_al(1 - c, hc), hc)]
        pltpu.make_async_remote_copy(src_ref=theirs, dst_ref=theirs, send_sem=send_sem, recv_sem=recv_sem,
                                     device_id=(x, y, c), device_id_type=MESH).wait_recv()
        cp.wait_send()

    return pl.pallas_call(
        body, name="grad_handover_" + tag,
        in_specs=[pl.BlockSpec(memory_space=pl.ANY)], out_specs=pl.BlockSpec(memory_space=pl.ANY),
        out_shape=jax.ShapeDtypeStruct(red.shape, red.dtype), input_output_aliases={0: 0},
        scratch_shapes=[pltpu.SemaphoreType.DMA, pltpu.SemaphoreType.DMA],
    )(red)


def _a2a_copy(j, chip, c, p_ref, q_ref, q_slot, send_sems, recv_sems):
    return pltpu.make_async_remote_copy(src_ref=p_ref.at[2 * chip[0] + chip[1]], dst_ref=q_ref.at[q_slot],
                                        send_sem=send_sems.at[j], recv_sem=recv_sems.at[j], device_id=(*chip, c),
                                        device_id_type=MESH)


def _a2a_start(p, tag):
    def body(p_ref, q_ref, send_sems, recv_sems, p_thru, q_thru, token):
        x, y, c = _position()
        for j, chip in enumerate(_other_chips(x, y)):
            _a2a_copy(j, chip, c, p_ref, q_ref, 2 * x + y, send_sems, recv_sems).start()
        token[...] = jnp.zeros_like(token)

    return pl.pallas_call(
        body, name="grad_alltoall_start_" + tag,
        out_shape=(pltpu.SemaphoreType.DMA((3,)), pltpu.SemaphoreType.DMA((3,)), pltpu.HBM(p.shape, p.dtype),
                   pltpu.HBM(p.shape, p.dtype), jax.ShapeDtypeStruct((8, 128), F32)),
        in_specs=(_HBM, _HBM), out_specs=(_SEM, _SEM, _HBM, _HBM, pl.BlockSpec(memory_space=pltpu.VMEM)),
        input_output_aliases={0: 2, 1: 3},
        compiler_params=pltpu.CompilerParams(**_SPLIT),
    )(_hbm(p), _hbm(lax.empty(p.shape, p.dtype)))


def _a2a_wait(send_sems, recv_sems, p, q, after, tag):
    def body(p_ref, q_ref, send_sems, recv_sems, after_ref, p_out, q_out):
        x, y, c = _position()
        for j, chip in enumerate(_other_chips(x, y)):
            cp = _a2a_copy(j, chip, c, p_ref, q_ref, 2 * chip[0] + chip[1], send_sems, recv_sems)
            cp.wait_send()
            cp.wait_recv()

    return pl.pallas_call(
        body, name="grad_alltoall_wait_" + tag,
        out_shape=(pltpu.HBM(p.shape, p.dtype), pltpu.HBM(q.shape, q.dtype)),
        in_specs=(_HBM, _HBM, _SEM, _SEM, pl.BlockSpec(memory_space=pl.ANY)), out_specs=(_HBM, _HBM),
        input_output_aliases={0: 0, 1: 1},
        compiler_params=pltpu.CompilerParams(**_SPLIT),
    )(p, q, send_sems, recv_sems, after)


def _comm_rows(rows):
    return next(t for t in (512, 384, 256, 128) if rows % t == 0)


def _pair_add(gb, recv, where, tag):
    _, rows, cols = gb.shape
    hc = cols // 2
    tr = _comm_rows(rows)

    def body(w_ref, g_ref, r_ref, o_ref):
        o_ref[...] = (g_ref[...].astype(F32) + r_ref[...].astype(F32)).astype(o_ref.dtype)

    return pl.pallas_call(
        body, name="grad_pair_add_" + tag,
        grid_spec=pltpu.PrefetchScalarGridSpec(
            num_scalar_prefetch=1, grid=(4, rows // tr),
            in_specs=[pl.BlockSpec((None, tr, hc), lambda s, j, w_ref: (s, j, w_ref[0])),
                      pl.BlockSpec((None, tr, hc), lambda s, j, w_ref: (s, j, 0))],
            out_specs=pl.BlockSpec((None, tr, hc), lambda s, j, w_ref: (s, j, 0))),
        out_shape=jax.ShapeDtypeStruct((4, rows, hc), gb.dtype),
        compiler_params=_cp(("parallel", "parallel")),
    )(where, gb, recv)


def _sum_chips(p, q, where, tag):
    _, rows, hc = q.shape
    tr = _comm_rows(rows)

    def body(w_ref, p_ref, qa_ref, qb_ref, qc_ref, o_ref):
        me = w_ref[1]
        own, qa, qb, qc = (r[...].astype(F32) for r in (p_ref, qa_ref, qb_ref, qc_ref))
        v0 = jnp.where(me == 0, own, qa)
        v1 = jnp.where(me == 1, own, jnp.where(me == 0, qa, qb))
        v2 = jnp.where(me == 2, own, jnp.where(me < 2, qb, qc))
        v3 = jnp.where(me == 3, own, qc)
        o_ref[...] = ((v0 + v1) + v2) + v3

    slot = lambda k: pl.BlockSpec((None, tr, hc), lambda j, w_ref: (w_ref[k], j, 0))
    return pl.pallas_call(
        body, name="grad_sum_chips_" + tag,
        grid_spec=pltpu.PrefetchScalarGridSpec(
            num_scalar_prefetch=1, grid=(rows // tr,),
            in_specs=[slot(1), slot(2), slot(3), slot(4)],
            out_specs=pl.BlockSpec((tr, hc), lambda j, w_ref: (j, w_ref[0]))),
        out_shape=jax.ShapeDtypeStruct((rows, 2 * hc), F32),
        compiler_params=_cp(("parallel",)),
    )(where, p, q, q, q)


def _shard_major(g, axis):
    shape = g.shape
    g = g.reshape(shape[:axis] + (4, shape[axis] // 4) + shape[axis + 1:])
    return jnp.moveaxis(g, axis, 0).reshape(4, -1)


def _unshard(g4, shape, axis):
    n = shape[axis] // 4
    g = g4.reshape((4,) + shape[:axis] + (n,) + shape[axis + 1:])
    return jnp.moveaxis(g, 0, axis).reshape(shape)


def _split(flat, shapes):
    out, off = [], 0
    for shp in shapes:
        n = 1
        for d in shp:
            n *= d
        out.append(flat[..., off:off + n].reshape(flat.shape[:-1] + tuple(shp)))
        off += n
    return out


def _even_rows_to_kernel(wt):
    return jnp.concatenate([wt[:1536], wt[1552:3088], wt[1536:1552], wt[3088:3096],
                            jnp.zeros((PE - 3096, wt.shape[1]), wt.dtype)], axis=0)


def _block_diag(w):
    eye = jnp.eye(8, dtype=w.dtype)
    return (w[:, :, None, :] * eye[:, None, :, None]).reshape(512, 512)


def _diag_blocks(g):
    eye = jnp.eye(8, dtype=g.dtype)
    return (g.reshape(8, 64, 8, 64) * eye[:, None, :, None]).sum(axis=2)


def _shift_down(a, s):
    return a if s == 0 else jnp.pad(a, ((s, 0), (0, 0)))[:a.shape[0]]


def _shift_up(a, s):
    return a if s == 0 else jnp.pad(a, ((0, s), (0, 0)))[s:]


SMALL_SHARDED_SHAPES = [(2, 4, 256), (16, 64), (4, 128), (128,), (128,), (128,), (128,)]
REPL_SHAPES = [(256,), (512,), (8,), (8, 257), (8, 64, 64), (8, 64, 64)]


def kernel(x, norm_w, w_in_even, gla_w_a_up, gla_b_a, gla_norm_w, fox_b_f, w_out_even, w_in_odd, rel_bias, conv_w, conv_b, lru_w_a, lru_b_a, lru_w_x, lru_b_x, lru_lambda, w_out_odd, w_mlp_up, w_mlp_down, loss_target, m_norm_w, m_w_in_even, m_gla_w_a_up, m_gla_b_a, m_gla_norm_w, m_fox_b_f, m_w_out_even, m_w_in_odd, m_rel_bias, m_conv_w, m_conv_b, m_lru_w_a, m_lru_b_a, m_lru_w_x, m_lru_b_x, m_lru_lambda, m_w_out_odd, m_w_mlp_up, m_w_mlp_down, v_norm_w, v_w_in_even, v_gla_w_a_up, v_gla_b_a, v_gla_norm_w, v_fox_b_f, v_w_out_even, v_w_in_odd, v_rel_bias, v_conv_w, v_conv_b, v_lru_w_a, v_lru_b_a, v_lru_w_x, v_lru_b_x, v_lru_lambda, v_w_out_odd, v_w_mlp_up, v_w_mlp_down):
    c_idx = lax.axis_index("c")

    small_local = [norm_w, gla_w_a_up[0], conv_w[0], conv_b[0], lru_b_a[0], lru_b_x[0], lru_lambda[0]]
    small_src = jnp.concatenate([a.reshape(-1) for a in small_local]).reshape(32, 128)
    mine = [small_src, w_in_even[0].T.astype(BF16), w_out_even[0].astype(BF16), w_mlp_up.astype(BF16),
            w_mlp_down.astype(BF16), w_in_odd[0].astype(BF16), w_out_odd[0].astype(BF16)]
    ag_sems, ag_srcs, ag_lands, ag_token = _ag_start(mine)
    ag_lands = list(_ag_push_own(ag_srcs, ag_lands))

    def gathered(g, after):
        srcs_g, lands_g = _ag_wait(g, ag_sems[g], ag_srcs, ag_lands, after)
        return _ag_forward(g, srcs_g, lands_g)

    small4, w_in_e4, w_out_e = gathered(0, ag_token)
    me = 2 * lax.axis_index("x") + lax.axis_index("y")
    others = [k + (k >= me).astype(jnp.int32) for k in range(3)]
    where = jnp.stack([c_idx, me] + others).astype(jnp.int32)

    w_in_e_t = _even_rows_to_kernel(w_in_e4.reshape(3096, D))
    g_small = _split(small4.reshape(4, 32 * 128), SMALL_SHARDED_SHAPES)
    nw_full = _unshard(g_small[0], (2, 4, 1024), 2)
    wa_up = _unshard(g_small[1], (16, 256), 1)
    cw = _unshard(g_small[2], (4, 512), 1)
    cb, lba, lbx, lam = [_unshard(g, (512,), 0).reshape(1, 512) for g in g_small[3:]]
    nw = lambda layer, i: nw_full[layer, i].reshape(1, D)

    wa_pad = jnp.pad(wa_up, ((0, 128 - 16), (0, 0)))
    gla_ba = gla_b_a.reshape(1, 256)
    gla_nw = gla_norm_w.reshape(1, 512)
    fox_bpad = jnp.pad(fox_b_f.reshape(1, 8), ((0, 0), (FOX_LANE0, 128 - FOX_LANE0 - 8)))
    rbp = jnp.pad(rel_bias[0], ((0, 0), (0, REL_PAD - 257)))
    wa_bd = _block_diag(lru_w_a[0])
    wx_bd = _block_diag(lru_w_x[0])

    x0 = x[0]
    tgt = loss_target[0]

    h0 = _prenorm(x0, nw(0, 0), "prenorm_l0_mix")
    proj_e = _mm(h0, w_in_e_t, "nt", tm=1024, tn=640, name="mm_in_even")
    cat0, s_prev = _gla_fwd(proj_e, wa_pad, gla_ba, gla_nw)
    cum_r = _fox_gate_fwd(proj_e, fox_bpad)
    cum_c = cum_r[:, FOX_LANE0:FOX_LANE0 + 8].T
    cat0 = _fox_fwd(proj_e, cum_c, cat0)
    mix0 = _mm(cat0, w_out_e, "nn", tm=1024, tn=512, name="mm_out_even")
    x1, h1 = _post_pre_fwd(x0, mix0, nw(0, 1), nw(0, 2), "post_pre_l0_mix")
    w_up, w_dn = gathered(1, x1)
    a0, r0 = _mm(h1, w_up, "nn", tm=1024, tn=1024, b_layer=0, relu_pair=True, name="mm_up_l0")
    d0 = _mm(a0, w_dn, "nn", tm=1024, tn=512, b_layer=0, name="mm_down_l0")
    x2, h2 = _post_pre_fwd(x1, d0, nw(0, 3), nw(1, 0), "post_pre_l0_mlp")

    w_in_o, w_out_o = gathered(2, x2)
    proj_o = _mm(h2, w_in_o, "nn", tm=1024, tn=640, name="mm_in_odd")
    bias_q = _bias_build(rbp)
    bias = bias_q.transpose(1, 0, 2)
    kvpad = jnp.pad(proj_o[:, 512:1536], ((CA_PAD, 0), (0, 0)))
    cat1 = _ca_fwd(proj_o, kvpad, bias)
    x_in = proj_o[:, 2048:2560]
    xs = jnp.stack([_shift_down(x_in, 3 - j) for j in range(4)])
    lru_a, lru_b = _lru_pre_fwd(xs, cw, cb, wa_bd, lba, wx_bd, lbx, lam)
    hh = _lru_scan_fwd(lru_a, lru_b)
    cat1 = _lru_post_fwd(hh, proj_o, cat1)
    mix1 = _mm(cat1, w_out_o, "nn", tm=1024, tn=512, name="mm_out_odd")
    x3, h3 = _post_pre_fwd(x2, mix1, nw(1, 1), nw(1, 2), "post_pre_l1_mix")
    a1, r1 = _mm(h3, w_up, "nn", tm=1024, tn=1024, b_layer=1, relu_pair=True, name="mm_up_l1")
    d1 = _mm(a1, w_dn, "nn", tm=1024, tn=512, b_layer=1, name="mm_down_l1")
    g4, loss_part, dd1, dnw13 = _post_loss(x3, d1, nw(1, 3), tgt)
    loss = lax.psum(loss_part[0, 0], ("x", "y", "c"))

    def rs_begin(swap, after, tag):
        gb, recv = _pair_swap_wait(swap, after, tag)
        return _a2a_start(_pair_add(gb, recv, where, tag), tag)

    def rs_end(started, after, tag):
        send_sems, recv_sems, p, q, _ = started
        p, q = _a2a_wait(send_sems, recv_sems, p, q, after, tag)
        return _handover(_sum_chips(p, q, where, tag), tag)

    gba = lax.dynamic_update_slice(lax.empty((4, GA_ROWS, D), BF16), jnp.zeros((4, GA_UP - GA_GAP, D), BF16),
                                   (0, GA_GAP, 0))
    gba = _mm(a1, dd1, "tn", tm=512, tn=1024, into=(gba, 1024, GA_DN), name="mm_down_l1_dw")
    du1 = _mm(dd1, w_dn, "nt", tm=1024, tn=1024, b_layer=1, times2=r1, out_dtype=BF16, name="mm_down_l1_dx")
    gba = _mm(du1, h3, "tn", tm=512, tn=1024, into=(gba, 1024, GA_UP), name="mm_up_l1_dw")
    dh3 = _mm(du1, w_up, "nt", tm=1024, tn=512, b_layer=1, name="mm_up_l1_dx")
    g3, dmix1, dnw12, dnw11 = _pre_post_bwd(x3, nw(1, 2), dh3, g4, mix1, nw(1, 1), "pre_post_bwd_l1_mlp")
    gba = _mm(cat1, dmix1, "tn", tm=128, tn=1024, into=(gba, 256, GA_OUT_O), name="mm_out_odd_dw")
    dcat1 = _mm(dmix1, w_out_o, "nt", tm=1024, tn=512, name="mm_out_odd_dx")

    dq_c, dkpad, dvpad, dbias = _ca_bwd(proj_o, kvpad, bias, dcat1)
    g_rel = _bias_grad(jnp.pad(dbias.transpose(1, 0, 2), ((0, 0), (0, 0), (0, BIAS_W - CA_BAND))))[:, :257]
    dhh, dgate = _lru_post_bwd(hh, proj_o, dcat1)
    da_l, db_l = _lru_scan_bwd(lru_a, hh, dhh)
    dxs, g_cw, g_cb, g_wa_bd, g_lba, g_wx_bd, g_lbx, g_lam = _lru_pre_bwd(xs, cw, cb, wa_bd, lba, wx_bd, lbx, lam, da_l, db_l)
    dx_in = _conv_dx(jnp.stack([_shift_up(dxs[j], 3 - j) for j in range(4)]))
    dproj_o = jnp.concatenate([dq_c, dkpad[CA_PAD:], dvpad[CA_PAD:], dgate, dx_in], axis=1).astype(BF16)
    gba = _mm(dproj_o, h2, "tn", tm=128, tn=1024, into=(gba, 640, GA_IN_O), name="mm_in_odd_dw")
    swap_a = _pair_swap_start(gba, "a")
    dh2 = _mm(dproj_o, w_in_o, "nt", tm=1024, tn=512, name="mm_in_odd_dx")
    g2, dd0, dnw10, dnw03 = _pre_post_bwd(x2, nw(1, 0) + swap_a[4][0, 0], dh2, g3, d0, nw(0, 3), "pre_post_bwd_l1_mix")
    rs_a = rs_begin(swap_a, g2, "a")

    gbb = lax.empty((4, GB_ROWS, D), BF16)
    gbb = _mm(a0, dd0, "tn", tm=512, tn=1024, into=(gbb, 1024, GB_DN), name="mm_down_l0_dw")
    du0 = _mm(dd0, w_dn, "nt", tm=1024, tn=1024, b_layer=0, times2=r0, out_dtype=BF16, name="mm_down_l0_dx")
    gbb = _mm(du0, h1, "tn", tm=512, tn=1024, into=(gbb, 1024, GB_UP), name="mm_up_l0_dw")
    swap_b = _pair_swap_start(gbb, "b")
    dh1 = _mm(du0, w_up, "nt", tm=1024, tn=512, b_layer=0, name="mm_up_l0_dx")
    g1, dmix0, dnw02, dnw01 = _pre_post_bwd(x1, nw(0, 2) + (swap_b[4][0, 0] + rs_a[4][0, 0]), dh1, g2, mix0, nw(0, 1),
                                            "pre_post_bwd_l0_mlp")
    rs_b = rs_begin(swap_b, g1, "b")
    gbc = lax.empty((4, GC_ROWS, D), BF16)
    gbc = _mm(cat0, dmix0, "tn", tm=128, tn=1024, into=(gbc, 256, GC_OUT_E), name="mm_out_even_dw")
    dcat0 = _mm(dmix0, w_out_e, "nt", tm=1024, tn=512, name="mm_out_even_dx")

    dq_g, dk_g, dv_g, dr_g, daux_g, g_wa_pad, g_gla_ba, g_gla_nw = _gla_bwd(
        proj_e, s_prev, wa_pad, gla_ba, gla_nw + rs_b[4][0, 0], dcat0)
    dq_f, dk_f, dv_f, dccol = _fox_bwd(proj_e, cum_c, dcat0)
    dccol_t = jnp.pad(dccol.sum(axis=0).T, ((0, 0), (FOX_LANE0, 128 - FOX_LANE0 - 8)))
    daux, g_fox_bpad = _fox_gate_bwd(proj_e, fox_bpad, dccol_t, daux_g)
    dproj_e = jnp.concatenate([dq_g, dk_g, dv_g, dr_g, dq_f, dk_f, dv_f, daux], axis=1).astype(BF16)
    gt_in_e = _mm(dproj_e, h0, "tn", tm=640, tn=1024, out_dtype=BF16, name="mm_in_even_dw")
    dh0 = _mm(dproj_e, w_in_e_t, "nn", tm=1024, tn=512, name="mm_in_even_dx")
    grad_x, dnw00 = _norm_bwd(x0, nw(0, 0), dh0, g1, "prenorm_l0_mix_bwd")

    g_norm = jnp.stack([jnp.concatenate([dnw00, dnw01, dnw02, dnw03]), jnp.concatenate([dnw10, dnw11, dnw12, dnw13])])
    sharded = [(g_norm, 2), (g_wa_pad[:16], 1), (g_cw, 1), (g_cb[0], 0), (g_lba[0], 0), (g_lbx[0], 0), (g_lam[0], 0)]
    replicated = [g_gla_ba[0], g_gla_nw[0], g_fox_bpad[0, FOX_LANE0:FOX_LANE0 + 8], g_rel, _diag_blocks(g_wa_bd),
                  _diag_blocks(g_wx_bd)]
    small4 = jnp.concatenate([_shard_major(g, ax) for g, ax in sharded]
                             + [jnp.broadcast_to(g.reshape(1, -1), (4, g.size)) for g in replicated], axis=1)
    n_small = small4.shape[1]
    small_rows = GC_ROWS - GC_TAIL - 774
    small4 = jnp.pad(small4, ((0, 0), (0, small_rows * D - n_small))).reshape(4, small_rows, D)
    gt_rows = jnp.concatenate([gt_in_e[:1536], gt_in_e[3072:3088], gt_in_e[1536:3072], gt_in_e[3088:3096]], axis=0)
    tail = jnp.concatenate([gt_rows.reshape(4, 774, D), small4.astype(BF16)], axis=1)
    gbc = lax.dynamic_update_slice(gbc, tail, (0, GC_TAIL, 0))
    swap_c = _pair_swap_start(gbc, "c")
    rs_c = rs_begin(swap_c, swap_c[4], "c")

    red_a = rs_end(rs_a, rs_c[4], "a")
    red_b = rs_end(rs_b, red_a, "b")
    early = dict(
        w_mlp_up=_adamw_from(w_mlp_up, m_w_mlp_up, v_w_mlp_up, [(red_b, GB_UP, True), (red_a, GA_UP, True)], 256,
                             "adamw_w_mlp_up"),
        w_mlp_down=_adamw_from(w_mlp_down, m_w_mlp_down, v_w_mlp_down, [(red_b, GB_DN, False), (red_a, GA_DN, False)],
                               256, "adamw_w_mlp_down"),
        w_in_odd=_adamw_from(w_in_odd, m_w_in_odd, v_w_in_odd, [(red_a, GA_IN_O, True)], 256, "adamw_w_in_odd"),
        w_out_odd=_adamw_from(w_out_odd, m_w_out_odd, v_w_out_odd, [(red_a, GA_OUT_O, False)], 128, "adamw_w_out_odd"))
    red_c = rs_end(rs_c, early["w_out_odd"][3], "c")

    g_small = _split(red_c[GC_TAIL + 774:].reshape(-1)[:n_small], SMALL_SHARDED_SHAPES + REPL_SHAPES)
    g_of = dict(zip(["norm_w", "gla_w_a_up", "conv_w", "conv_b", "lru_b_a", "lru_b_x", "lru_lambda", "gla_b_a",
                     "gla_norm_w", "fox_b_f", "rel_bias", "lru_w_a", "lru_w_x"], g_small))
    g_of.update(w_in_even=red_c[GC_TAIL:GC_TAIL + 774])
    early["w_out_even"] = _adamw_from(w_out_even, m_w_out_even, v_w_out_even, [(red_c, GC_OUT_E, False)], 256,
                                      "adamw_w_out_even")

    names = ["norm_w", "w_in_even", "gla_w_a_up", "gla_b_a", "gla_norm_w", "fox_b_f", "w_out_even", "w_in_odd", "rel_bias",
             "conv_w", "conv_b", "lru_w_a", "lru_b_a", "lru_w_x", "lru_b_x", "lru_lambda", "w_out_odd", "w_mlp_up",
             "w_mlp_down"]
    w_of = dict(norm_w=norm_w, w_in_even=w_in_even, gla_w_a_up=gla_w_a_up, gla_b_a=gla_b_a, gla_norm_w=gla_norm_w,
                fox_b_f=fox_b_f, w_out_even=w_out_even, w_in_odd=w_in_odd, rel_bias=rel_bias, conv_w=conv_w, conv_b=conv_b,
                lru_w_a=lru_w_a, lru_b_a=lru_b_a, lru_w_x=lru_w_x, lru_b_x=lru_b_x, lru_lambda=lru_lambda,
                w_out_odd=w_out_odd, w_mlp_up=w_mlp_up, w_mlp_down=w_mlp_down)
    m_of = dict(norm_w=m_norm_w, w_in_even=m_w_in_even, gla_w_a_up=m_gla_w_a_up, gla_b_a=m_gla_b_a,
                gla_norm_w=m_gla_norm_w, fox_b_f=m_fox_b_f, w_out_even=m_w_out_even, w_in_odd=m_w_in_odd,
                rel_bias=m_rel_bias, conv_w=m_conv_w, conv_b=m_conv_b, lru_w_a=m_lru_w_a, lru_b_a=m_lru_b_a,
                lru_w_x=m_lru_w_x, lru_b_x=m_lru_b_x, lru_lambda=m_lru_lambda, w_out_odd=m_w_out_odd,
                w_mlp_up=m_w_mlp_up, w_mlp_down=m_w_mlp_down)
    v_of = dict(norm_w=v_norm_w, w_in_even=v_w_in_even, gla_w_a_up=v_gla_w_a_up, gla_b_a=v_gla_b_a,
                gla_norm_w=v_gla_norm_w, fox_b_f=v_fox_b_f, w_out_even=v_w_out_even, w_in_odd=v_w_in_odd,
                rel_bias=v_rel_bias, conv_w=v_conv_w, conv_b=v_conv_b, lru_w_a=v_lru_w_a, lru_b_a=v_lru_b_a,
                lru_w_x=v_lru_w_x, lru_b_x=v_lru_b_x, lru_lambda=v_lru_lambda, w_out_odd=v_w_out_odd,
                w_mlp_up=v_w_mlp_up, w_mlp_down=v_w_mlp_down)
    grads, deltas, new_ms, new_vs = [], [], [], []
    for n in names:
        w = w_of[n]
        if n in early:
            g, d, mn, vn = early[n]
            grads.append(g)
            deltas.append(d)
            new_ms.append(mn)
            new_vs.append(vn)
            continue
        if n == "w_in_even":
            to_view = lambda a: a[0].T
            from_view = lambda a: a.T[None]
        else:
            view = w.shape if w.ndim <= 3 else w.shape[-3:]
            to_view = lambda a, view=view: a.reshape(view)
            from_view = lambda a, w=w: a.reshape(w.shape)
        g = g_of[n] if n == "w_in_even" else to_view(g_of[n])
        d, mn, vn = _adamw(to_view(w), g, to_view(m_of[n]), to_view(v_of[n]), "adamw_" + n)
        grads.append(from_view(g))
        deltas.append(from_view(d))
        new_ms.append(from_view(mn))
        new_vs.append(from_view(vn))

    return (loss, grad_x.reshape(1, T, D), *grads, *deltas, *new_ms, *new_vs)
```

```python
import functools

import jax
import jax.numpy as jnp
from jax import lax
from jax.experimental import pallas as pl
from jax.experimental.pallas import tpu as pltpu

F32 = jnp.float32
BF16 = jnp.bfloat16
MESH = pl.DeviceIdType.MESH

T = 2048
D = 1024
DFF = 4096
EPS = 1e-6
CHUNK = 64
NCHUNK = T // CHUNK
PE = 3200
PO = 2560
AUX_BLK = 3072 // 128
FOX_LANE0 = 16
GLA_SCALE = 64 ** -0.5
ATT_SCALE = 64 ** -0.5
NEG = float(jnp.finfo(jnp.float32).min)
CA_BAND = 576
CA_PAD = 512
REL_PAD = 384

VMEM_LIMIT = 48 * 1024 * 1024

ADAM_LR, ADAM_B1, ADAM_B2, ADAM_EPS, ADAM_WD, ADAM_STEP = 0.001, 0.9, 0.999, 1e-08, 0.01, 10

GA_ROWS, GA_IN_O, GA_OUT_O, GA_GAP, GA_UP, GA_DN = 3072, 0, 640, 896, 1024, 2048
GB_ROWS, GB_UP, GB_DN = 2048, 0, 1024
GC_ROWS, GC_OUT_E, GC_TAIL = 1152, 0, 256

_DIMS = {"nn": (((1,), (0,)), ((), ())), "nt": (((1,), (1,)), ((), ())), "tn": (((0,), (0,)), ((), ()))}


def _cp(sem, **kw):
    return pltpu.CompilerParams(dimension_semantics=sem, vmem_limit_bytes=VMEM_LIMIT, **kw)


def _dot(a, b, mode):
    return lax.dot_general(a.astype(BF16), b.astype(BF16), _DIMS[mode], preferred_element_type=F32)


@functools.partial(jax.custom_vjp, nondiff_argnums=(2,))
def bdot(a, b, mode):
    return _dot(a, b, mode)


def _bdot_fwd(a, b, mode):
    return _dot(a, b, mode), (a, b)


def _bdot_bwd(mode, res, g):
    a, b = res
    if mode == "nn":
        da, db = _dot(g, b, "nt"), _dot(a, g, "tn")
    elif mode == "nt":
        da, db = _dot(g, b, "nn"), _dot(g, a, "tn")
    else:
        da, db = _dot(b, g, "nt"), _dot(a, g, "nn")
    return da.astype(a.dtype), db.astype(b.dtype)


bdot.defvjp(_bdot_fwd, _bdot_bwd)


def _hdot_raw(a, b, mode):
    return lax.dot_general(a, b, _DIMS[mode], precision=lax.Precision.HIGHEST, preferred_element_type=F32)


@functools.partial(jax.custom_vjp, nondiff_argnums=(2,))
def hdot(a, b, mode):
    return _hdot_raw(a, b, mode)


def _hdot_fwd(a, b, mode):
    return _hdot_raw(a, b, mode), (a, b)


def _hdot_bwd(mode, res, g):
    a, b = res
    if mode == "nn":
        return _hdot_raw(g, b, "nt"), _hdot_raw(a, g, "tn")
    if mode == "nt":
        return _hdot_raw(g, b, "nn"), _hdot_raw(g, a, "tn")
    return _hdot_raw(b, g, "nt"), _hdot_raw(a, g, "nn")


hdot.defvjp(_hdot_fwd, _hdot_bwd)


def _log_sigmoid(x):
    return jnp.minimum(x, 0.0) - jnp.log(1.0 + jnp.exp(-jnp.abs(x)))


def _sigmoid(x):
    return 1.0 / (1.0 + jnp.exp(-x))


def _expm1(x):
    series = x * (1.0 + x * 0.5 * (1.0 + x * (1.0 / 3.0) * (1.0 + x * 0.25)))
    return jnp.where(jnp.abs(x) < 0.03, series, jnp.exp(x) - 1.0)


def _gelu_tanh(x):
    return 0.5 * x * (1.0 + jnp.tanh(0.7978845608028654 * (x + 0.044715 * x * x * x)))


def _iota(shape, dim):
    return lax.broadcasted_iota(jnp.int32, shape, dim)


def _mm(a, b, mode, *, tm, tn, tk=None, out_dtype=F32, name, b_layer=None, into=None, relu_pair=False, times2=None):
    b2 = b.shape[-2:]
    if mode == "nn":
        (m, k), n = a.shape, b2[1]
    elif mode == "nt":
        (m, k), n = a.shape, b2[0]
    else:
        (k, m), n = a.shape, b2[1]
    tk = k if tk is None else tk
    assert m % tm == 0 and n % tn == 0 and k % tk == 0, (name, a.shape, b.shape)
    nk = k // tk
    a_spec = {"nn": pl.BlockSpec((tm, tk), lambda i, j, kk: (i, kk)),
              "nt": pl.BlockSpec((tm, tk), lambda i, j, kk: (i, kk)),
              "tn": pl.BlockSpec((tk, tm), lambda i, j, kk: (kk, i))}[mode]
    b_blk = {"nn": (tk, tn), "nt": (tn, tk), "tn": (tk, tn)}[mode]
    b_idx = {"nn": lambda i, j, kk: (kk, j), "nt": lambda i, j, kk: (j, kk), "tn": lambda i, j, kk: (kk, j)}[mode]
    if b_layer is None:
        b_spec = pl.BlockSpec(b_blk, b_idx)
    else:
        b_spec = pl.BlockSpec((None,) + b_blk, lambda i, j, kk: (b_layer,) + b_idx(i, j, kk))

    tile = pl.BlockSpec((tm, tn), lambda i, j, kk: (i, j))
    if into is not None:
        buf, per_slot, row_off = into
        assert m == 4 * per_slot and per_slot % tm == 0 and row_off % tm == 0 and buf.shape[2] == n, (name, buf.shape)
        bps = per_slot // tm
        out_specs = pl.BlockSpec((None, tm, tn), lambda i, j, kk: (i // bps, row_off // tm + i % bps, j))
        out_shape = jax.ShapeDtypeStruct(buf.shape, buf.dtype)
        extra_in, extra_specs, aliases = [buf], [pl.BlockSpec(memory_space=pl.ANY)], {2: 0}
        finish = lambda acc, extra: [acc.astype(buf.dtype)]
    elif relu_pair:
        out_specs = (tile, tile)
        out_shape = (jax.ShapeDtypeStruct((m, n), BF16),) * 2
        extra_in, extra_specs, aliases = [], [], {}

        def finish(acc, extra):
            r = jnp.maximum(acc, 0.0)
            return [(r * r).astype(BF16), r.astype(BF16)]
    elif times2 is not None:
        out_specs = tile
        out_shape = jax.ShapeDtypeStruct((m, n), out_dtype)
        extra_in, extra_specs, aliases = [times2], [tile], {}
        finish = lambda acc, extra: [(acc * (2.0 * extra[...].astype(F32))).astype(out_dtype)]
    else:
        out_specs = tile
        out_shape = jax.ShapeDtypeStruct((m, n), out_dtype)
        extra_in, extra_specs, aliases = [], [], {}
        finish = lambda acc, extra: [acc.astype(out_dtype)]
    n_out = 2 if relu_pair else 1

    def body(*refs):
        a_ref, b_ref = refs[0], refs[1]
        extra = refs[2] if extra_in else None
        o_refs = refs[2 + len(extra_in):2 + len(extra_in) + n_out]

        def store(acc):
            for o_ref, val in zip(o_refs, finish(acc, extra)):
                o_ref[...] = val

        if nk == 1:
            store(_dot(a_ref[...], b_ref[...], mode))
            return
        acc_ref = refs[-1]
        kk = pl.program_id(2)

        @pl.when(kk == 0)
        def _():
            acc_ref[...] = jnp.zeros_like(acc_ref)

        acc_ref[...] += _dot(a_ref[...], b_ref[...], mode)

        @pl.when(kk == nk - 1)
        def _():
            store(acc_ref[...])

    return pl.pallas_call(
        body, name=name, grid=(m // tm, n // tn, nk),
        in_specs=[a_spec, b_spec] + extra_specs,
        out_specs=out_specs, out_shape=out_shape,
        scratch_shapes=[pltpu.VMEM((tm, tn), F32)] if nk > 1 else [],
        input_output_aliases=aliases,
        compiler_params=_cp(("parallel", "parallel", "arbitrary")),
    )(a, b, *extra_in)


ROWS = 256


def _prenorm(x, w, name):
    def body(x_ref, w_ref, o_ref):
        xv = x_ref[...]
        r = lax.rsqrt(jnp.mean(xv * xv, axis=-1, keepdims=True) + EPS)
        o_ref[...] = (xv * r * w_ref[...]).astype(BF16)

    return pl.pallas_call(
        body, name=name, grid=(T // ROWS,),
        in_specs=[pl.BlockSpec((ROWS, D), lambda i: (i, 0)), pl.BlockSpec((1, D), lambda i: (0, 0))],
        out_specs=pl.BlockSpec((ROWS, D), lambda i: (i, 0)),
        out_shape=jax.ShapeDtypeStruct((T, D), BF16),
        compiler_params=_cp(("parallel",)),
    )(x, w)


def _rms(z):
    return lax.rsqrt(jnp.mean(z * z, axis=-1, keepdims=True) + EPS)


def _rms_bwd(z, w, dy):
    r = _rms(z)
    wdy = dy * w
    dz = r * wdy - z * (r * r * r) * jnp.mean(z * wdy, axis=-1, keepdims=True)
    return dz, jnp.sum(dy * z * r, axis=0, keepdims=True)


_ROW = pl.BlockSpec((ROWS, D), lambda i: (i, 0))
_VEC = pl.BlockSpec((1, D), lambda i: (0, 0))


def _post_pre_fwd(x, z, w_post, w_pre, name):
    def body(x_ref, z_ref, wp_ref, wn_ref, x_out, h_out):
        zv = z_ref[...]
        xn = x_ref[...] + zv * _rms(zv) * wp_ref[...]
        x_out[...] = xn
        h_out[...] = (xn * _rms(xn) * wn_ref[...]).astype(BF16)

    return pl.pallas_call(
        body, name=name, grid=(T // ROWS,), in_specs=[_ROW, _ROW, _VEC, _VEC], out_specs=(_ROW, _ROW),
        out_shape=(jax.ShapeDtypeStruct((T, D), F32), jax.ShapeDtypeStruct((T, D), BF16)),
        compiler_params=_cp(("parallel",)),
    )(x, z, w_post, w_pre)


def _post_loss(x, z, w_post, tgt):
    def body(x_ref, z_ref, w_ref, t_ref, g_ref, l_ref, dz_ref, dw_ref):
        @pl.when(pl.program_id(0) == 0)
        def _():
            l_ref[...] = jnp.zeros_like(l_ref)
            dw_ref[...] = jnp.zeros_like(dw_ref)

        zv = z_ref[...]
        e = x_ref[...] + zv * _rms(zv) * w_ref[...] - t_ref[...]
        g = e * (1.0 / D)
        g_ref[...] = g
        l_ref[...] += jnp.sum(e * e) * (0.5 / D)
        dz, dw = _rms_bwd(zv, w_ref[...], g)
        dz_ref[...] = dz.astype(BF16)
        dw_ref[...] += dw

    return pl.pallas_call(
        body, name="postnorm_loss", grid=(T // ROWS,), in_specs=[_ROW, _ROW, _VEC, _ROW],
        out_specs=(_ROW, pl.BlockSpec((1, 128), lambda i: (0, 0)), _ROW, _VEC),
        out_shape=(jax.ShapeDtypeStruct((T, D), F32), jax.ShapeDtypeStruct((1, 128), F32),
                   jax.ShapeDtypeStruct((T, D), BF16), jax.ShapeDtypeStruct((1, D), F32)),
        compiler_params=_cp(("arbitrary",)),
    )(x, z, w_post, tgt)


def _pre_post_bwd(x, w_pre, dh, add, z, w_post, name):
    def body(x_ref, wn_ref, dh_ref, add_ref, z_ref, wp_ref, g_ref, dz_ref, dwn_ref, dwp_ref):
        @pl.when(pl.program_id(0) == 0)
        def _():
            dwn_ref[...] = jnp.zeros_like(dwn_ref)
            dwp_ref[...] = jnp.zeros_like(dwp_ref)

        dx, dwn = _rms_bwd(x_ref[...], wn_ref[...], dh_ref[...])
        g = dx + add_ref[...]
        g_ref[...] = g
        dz, dwp = _rms_bwd(z_ref[...], wp_ref[...], g)
        dz_ref[...] = dz.astype(BF16)
        dwn_ref[...] += dwn
        dwp_ref[...] += dwp

    return pl.pallas_call(
        body, name=name, grid=(T // ROWS,), in_specs=[_ROW, _VEC, _ROW, _ROW, _ROW, _VEC],
        out_specs=(_ROW, _ROW, _VEC, _VEC),
        out_shape=(jax.ShapeDtypeStruct((T, D), F32), jax.ShapeDtypeStruct((T, D), BF16),
                   jax.ShapeDtypeStruct((1, D), F32), jax.ShapeDtypeStruct((1, D), F32)),
        compiler_params=_cp(("arbitrary",)),
    )(x, w_pre, dh, add, z, w_post)


def _norm_bwd(z, w, dy, add, name):
    has_add = add is not None

    def body(*refs):
        if has_add:
            z_ref, w_ref, dy_ref, add_ref, dz_ref, dw_ref = refs
        else:
            z_ref, w_ref, dy_ref, dz_ref, dw_ref = refs
        i = pl.program_id(0)

        @pl.when(i == 0)
        def _():
            dw_ref[...] = jnp.zeros_like(dw_ref)

        zv = z_ref[...].astype(F32)
        dyv = dy_ref[...]
        r = lax.rsqrt(jnp.mean(zv * zv, axis=-1, keepdims=True) + EPS)
        wdy = dyv * w_ref[...]
        dz = r * wdy - zv * (r * r * r) * jnp.mean(zv * wdy, axis=-1, keepdims=True)
        if has_add:
            dz = dz + add_ref[...]
        dz_ref[...] = dz.astype(dz_ref.dtype)
        dw_ref[...] += jnp.sum(dyv * zv * r, axis=0, keepdims=True)

    row = pl.BlockSpec((ROWS, D), lambda i: (i, 0))
    vec = pl.BlockSpec((1, D), lambda i: (0, 0))
    ins = [z, w, dy] + ([add] if has_add else [])
    dz_dtype = F32 if has_add else BF16
    return pl.pallas_call(
        body, name=name, grid=(T // ROWS,),
        in_specs=[row, vec, row] + ([row] if has_add else []),
        out_specs=(row, vec),
        out_shape=(jax.ShapeDtypeStruct((T, D), dz_dtype), jax.ShapeDtypeStruct((1, D), F32)),
        compiler_params=_cp(("arbitrary",)),
    )(*ins)


def _adamw_math(w, g, m, v):
    c1 = 1.0 - ADAM_B1 ** ADAM_STEP
    c2 = 1.0 - ADAM_B2 ** ADAM_STEP
    mn = ADAM_B1 * m + (1.0 - ADAM_B1) * g
    vn = ADAM_B2 * v + (1.0 - ADAM_B2) * (g * g)
    return -ADAM_LR * ((mn / c1) / (jnp.sqrt(vn / c2) + ADAM_EPS) + ADAM_WD * w), mn, vn


def _adamw_from(w, m, v, sources, tr, name):
    layers, rows, cols = w.shape
    assert len(sources) == layers and rows % tr == 0, (name, w.shape)
    g_specs = []
    for buf, row0, transposed in sources:
        if transposed:
            assert row0 % cols == 0 and buf.shape[1] == rows, (name, row0)
            g_specs.append(pl.BlockSpec((cols, tr), lambda l, i, b=row0 // cols: (b, i)))
        else:
            assert row0 % tr == 0 and buf.shape[1] == cols, (name, row0)
            g_specs.append(pl.BlockSpec((tr, cols), lambda l, i, b=row0 // tr: (b + i, 0)))

    def body(*refs):
        w_ref, m_ref, v_ref = refs[:3]
        g_refs = refs[3:3 + layers]
        g_out, d_ref, mo_ref, vo_ref = refs[3 + layers:]
        gs = [r[...].T if src[2] else r[...] for r, src in zip(g_refs, sources)]
        g = gs[0] if layers == 1 else jnp.where(pl.program_id(0) == 0, gs[0], gs[1])
        g_out[...] = g
        d_ref[...], mo_ref[...], vo_ref[...] = _adamw_math(w_ref[...], g, m_ref[...], v_ref[...])

    blk = pl.BlockSpec((None, tr, cols), lambda l, i: (l, i, 0))
    sds = jax.ShapeDtypeStruct(w.shape, F32)
    return pl.pallas_call(body, name=name, grid=(layers, rows // tr), in_specs=[blk] * 3 + g_specs,
                          out_specs=(blk,) * 4, out_shape=(sds,) * 4,
                          compiler_params=_cp(("parallel", "parallel")))(w, m, v, *[s[0] for s in sources])


def _adamw(w, g, m, v, name):
    lead = w.shape[:-2]
    assert len(lead) <= 1 and g.shape == w.shape, (name, w.shape, g.shape)
    rows, cols = w.shape[-2:]
    if rows <= 512:
        tr, tc = rows, cols
    elif rows % 256 == 0:
        tr, tc = 256, cols
    else:
        tr, tc = rows, 256
    assert rows % tr == 0 and cols % tc == 0, (name, w.shape)
    c1 = 1.0 - ADAM_B1 ** ADAM_STEP
    c2 = 1.0 - ADAM_B2 ** ADAM_STEP

    def body(w_ref, g_ref, m_ref, v_ref, d_ref, mo_ref, vo_ref):
        gv = g_ref[...]
        mn = ADAM_B1 * m_ref[...] + (1.0 - ADAM_B1) * gv
        vn = ADAM_B2 * v_ref[...] + (1.0 - ADAM_B2) * (gv * gv)
        m_hat = mn / c1
        v_hat = vn / c2
        d_ref[...] = -ADAM_LR * (m_hat / (jnp.sqrt(v_hat) + ADAM_EPS) + ADAM_WD * w_ref[...])
        mo_ref[...] = mn
        vo_ref[...] = vn

    if lead:
        grid = (lead[0], rows // tr, cols // tc)
        blk = pl.BlockSpec((None, tr, tc), lambda l, i, j: (l, i, j))
    else:
        grid = (rows // tr, cols // tc)
        blk = pl.BlockSpec((tr, tc), lambda i, j: (i, j))
    sds = jax.ShapeDtypeStruct(w.shape, F32)
    return pl.pallas_call(body, name=name, grid=grid, in_specs=[blk] * 4, out_specs=(blk,) * 3,
                          out_shape=(sds,) * 3, compiler_params=_cp(("parallel",) * len(grid)))(w, g, m, v)


def _gla_consts():
    ltri = (_iota((CHUNK, CHUNK), 0) >= _iota((CHUNK, CHUNK), 1)).astype(F32)
    ones_c = jnp.ones((CHUNK, 128), F32)
    mask = (_iota((256, 512), 0) // 64 == _iota((256, 512), 1) // 128).astype(F32)
    return ltri, ones_c, mask


def _gla_chunk(consts, q, k, v, r, aux, s_prev, wa, ba, nw):
    ltri, ones_c, mask = consts
    la = _log_sigmoid(bdot(aux, wa, "nn") + ba) * (1.0 / 16.0)
    cum = hdot(ltri, la, "nn")
    total = jnp.sum(la, axis=0, keepdims=True)
    k_dec = k * jnp.exp(total - cum)
    inc = bdot(k_dec, v, "tn") * mask
    dec = jnp.exp(hdot(la, ones_c, "tn"))
    dec = jnp.concatenate([dec, dec, dec, dec], axis=1)
    s_new = dec * s_prev + inc
    o = bdot(q * GLA_SCALE, s_new, "nn")
    parts = []
    for h in range(4):
        oh = o[:, h * 128:(h + 1) * 128]
        parts.append(oh * lax.rsqrt(jnp.mean(oh * oh, axis=-1, keepdims=True) + EPS))
    on = jnp.concatenate(parts, axis=1)
    return s_new, on * nw * (r * _sigmoid(r))


GLA_PER_STEP = 4
GLA_ROWS = GLA_PER_STEP * CHUNK
GLA_STEPS = NCHUNK // GLA_PER_STEP


def _gla_specs(cmap):
    return [pl.BlockSpec((GLA_ROWS, 256), lambda c: (cmap(c), 0)),
            pl.BlockSpec((GLA_ROWS, 256), lambda c: (cmap(c), 1)),
            pl.BlockSpec((GLA_ROWS, 512), lambda c: (cmap(c), 1)),
            pl.BlockSpec((GLA_ROWS, 512), lambda c: (cmap(c), 2)),
            pl.BlockSpec((GLA_ROWS, 128), lambda c: (cmap(c), AUX_BLK))]


def _gla_fwd(proj, wa, ba, nw):
    def body(q_ref, k_ref, v_ref, r_ref, aux_ref, wa_ref, ba_ref, nw_ref, o_ref, sp_ref, s_ref):
        @pl.when(pl.program_id(0) == 0)
        def _():
            s_ref[...] = jnp.zeros_like(s_ref)

        s = s_ref[...]
        consts = _gla_consts()
        outs, states = [], []
        for i in range(GLA_PER_STEP):
            rows = slice(i * CHUNK, (i + 1) * CHUNK)
            states.append(s)
            s, out = _gla_chunk(consts, q_ref[rows, :], k_ref[rows, :], v_ref[rows, :], r_ref[rows, :], aux_ref[rows, :],
                                s, wa_ref[...], ba_ref[...], nw_ref[...])
            outs.append(out)
        s_ref[...] = s
        for i in range(GLA_PER_STEP):
            o_ref[i * CHUNK:(i + 1) * CHUNK, :] = outs[i]
            sp_ref[i] = states[i]

    full = lambda shape: pl.BlockSpec(shape, lambda c: (0,) * len(shape))
    return pl.pallas_call(
        body, name="gla_fwd", grid=(GLA_STEPS,),
        in_specs=_gla_specs(lambda c: c) + [full((128, 256)), full((1, 256)), full((1, 512))],
        out_specs=(pl.BlockSpec((GLA_ROWS, 512), lambda c: (c, 0)),
                   pl.BlockSpec((GLA_PER_STEP, 256, 512), lambda c: (c, 0, 0))),
        out_shape=(jax.ShapeDtypeStruct((T, D), F32), jax.ShapeDtypeStruct((NCHUNK, 256, 512), F32)),
        scratch_shapes=[pltpu.VMEM((256, 512), F32)],
        compiler_params=_cp(("arbitrary",)),
    )(proj, proj, proj, proj, proj, wa, ba, nw)


def _gla_bwd(proj, s_prev_all, wa, ba, nw, dcat):
    rev = lambda c: GLA_STEPS - 1 - c

    def body(q_ref, k_ref, v_ref, r_ref, aux_ref, sp_ref, wa_ref, ba_ref, nw_ref, do_ref,
             dq_ref, dk_ref, dv_ref, dr_ref, daux_ref, dwa_ref, dba_ref, dnw_ref, ds_ref):
        @pl.when(pl.program_id(0) == 0)
        def _():
            ds_ref[...] = jnp.zeros_like(ds_ref)
            dwa_ref[...] = jnp.zeros_like(dwa_ref)
            dba_ref[...] = jnp.zeros_like(dba_ref)
            dnw_ref[...] = jnp.zeros_like(dnw_ref)

        fn = functools.partial(_gla_chunk, _gla_consts())
        ds = ds_ref[...]
        dwa, dba, dnw = dwa_ref[...], dba_ref[...], dnw_ref[...]
        grads = {}
        for i in reversed(range(GLA_PER_STEP)):
            rows = slice(i * CHUNK, (i + 1) * CHUNK)
            _, vjp = jax.vjp(fn, q_ref[rows, :], k_ref[rows, :], v_ref[rows, :], r_ref[rows, :], aux_ref[rows, :],
                             sp_ref[i], wa_ref[...], ba_ref[...], nw_ref[...])
            *grads[i], ds, dwa_i, dba_i, dnw_i = vjp((ds, do_ref[rows, :]))
            dwa, dba, dnw = dwa + dwa_i, dba + dba_i, dnw + dnw_i
        ds_ref[...] = ds
        dwa_ref[...] = dwa
        dba_ref[...] = dba
        dnw_ref[...] = dnw
        for i in range(GLA_PER_STEP):
            rows = slice(i * CHUNK, (i + 1) * CHUNK)
            for ref, g in zip((dq_ref, dk_ref, dv_ref, dr_ref, daux_ref), grads[i]):
                ref[rows, :] = g

    full = lambda shape: pl.BlockSpec(shape, lambda c: (0,) * len(shape))
    blk = lambda w: pl.BlockSpec((GLA_ROWS, w), lambda c: (rev(c), 0))
    sds = lambda *s: jax.ShapeDtypeStruct(s, F32)
    return pl.pallas_call(
        body, name="gla_bwd", grid=(GLA_STEPS,),
        in_specs=_gla_specs(rev) + [pl.BlockSpec((GLA_PER_STEP, 256, 512), lambda c: (rev(c), 0, 0)),
                                    full((128, 256)), full((1, 256)), full((1, 512)), blk(512)],
        out_specs=(blk(256), blk(256), blk(512), blk(512), blk(128), full((128, 256)), full((1, 256)), full((1, 512))),
        out_shape=(sds(T, 256), sds(T, 256), sds(T, 512), sds(T, 512), sds(T, 128),
                   sds(128, 256), sds(1, 256), sds(1, 512)),
        scratch_shapes=[pltpu.VMEM((256, 512), F32)],
        compiler_params=_cp(("arbitrary",)),
    )(proj, proj, proj, proj, proj, s_prev_all, wa, ba, nw, dcat)


GATE_ROWS = 128


def _fox_gate_block(ltri, aux, bpad, carry):
    lf = _log_sigmoid(aux + bpad)
    cum = hdot(ltri, lf, "nn") + carry
    return cum, carry + jnp.sum(lf, axis=0, keepdims=True)


def _gate_ltri():
    return (_iota((GATE_ROWS, GATE_ROWS), 0) >= _iota((GATE_ROWS, GATE_ROWS), 1)).astype(F32)


def _fox_gate_fwd(proj, bpad):
    def body(aux_ref, b_ref, cum_ref, carry_ref):
        i = pl.program_id(0)

        @pl.when(i == 0)
        def _():
            carry_ref[...] = jnp.zeros_like(carry_ref)

        cum, carry = _fox_gate_block(_gate_ltri(), aux_ref[...], b_ref[...], carry_ref[...])
        cum_ref[...] = cum
        carry_ref[...] = carry

    return pl.pallas_call(
        body, name="fox_gate_fwd", grid=(T // GATE_ROWS,),
        in_specs=[pl.BlockSpec((GATE_ROWS, 128), lambda i: (i, AUX_BLK)), pl.BlockSpec((1, 128), lambda i: (0, 0))],
        out_specs=pl.BlockSpec((GATE_ROWS, 128), lambda i: (i, 0)),
        out_shape=jax.ShapeDtypeStruct((T, 128), F32),
        scratch_shapes=[pltpu.VMEM((1, 128), F32)],
        compiler_params=_cp(("arbitrary",)),
    )(proj, bpad)


def _fox_gate_bwd(proj, bpad, dccol_t, daux_gla):
    nb = T // GATE_ROWS
    rev = lambda i: nb - 1 - i

    def body(aux_ref, b_ref, dc_ref, dg_ref, daux_ref, db_ref, dcarry_ref):
        i = pl.program_id(0)

        @pl.when(i == 0)
        def _():
            dcarry_ref[...] = jnp.zeros_like(dcarry_ref)
            db_ref[...] = jnp.zeros_like(db_ref)

        dcum = dc_ref[...]
        fn = functools.partial(_fox_gate_block, _gate_ltri())
        _, vjp = jax.vjp(fn, aux_ref[...], b_ref[...], jnp.zeros((1, 128), F32))
        daux, db, dcarry = vjp((dcum, dcarry_ref[...]))
        daux_ref[...] = daux + dg_ref[...]
        db_ref[...] += db
        dcarry_ref[...] = dcarry

    blk = pl.BlockSpec((GATE_ROWS, 128), lambda i: (rev(i), 0))
    vec = pl.BlockSpec((1, 128), lambda i: (0, 0))
    return pl.pallas_call(
        body, name="fox_gate_bwd", grid=(nb,),
        in_specs=[pl.BlockSpec((GATE_ROWS, 128), lambda i: (rev(i), AUX_BLK)), vec, blk, blk],
        out_specs=(blk, vec),
        out_shape=(jax.ShapeDtypeStruct((T, 128), F32), jax.ShapeDtypeStruct((1, 128), F32)),
        scratch_shapes=[pltpu.VMEM((1, 128), F32)],
        compiler_params=_cp(("arbitrary",)),
    )(proj, bpad, dccol_t, daux_gla)


FOX_Q = 128


FOX_QB = T // FOX_Q


@jax.custom_vjp
def _attend(s, v):
    return _attend_fwd(s, v)[0]


def _attend_fwd(s, v):
    e = jnp.exp(s - jnp.max(s, axis=-1, keepdims=True))
    r = 1.0 / jnp.sum(e, axis=-1, keepdims=True)
    return _dot(e, v, "nn") * r, (e, r, v)


def _attend_bwd(res, do):
    e, r, v = res
    do_r = do * r
    dpr = _dot(do_r, v, "nt")
    ds = e * (dpr - r * jnp.sum(e * dpr, axis=-1, keepdims=True))
    return ds, _dot(e, do_r, "tn").astype(v.dtype)


_attend.defvjp(_attend_fwd, _attend_bwd)


def _fox_block(hp, q, k, v, ccol):
    kl = k.shape[0]
    lane = _iota((FOX_Q, 128), 1)
    tri = jnp.bitwise_and(_iota((2 * FOX_Q, FOX_Q), 0), FOX_Q - 1) >= _iota((2 * FOX_Q, FOX_Q), 1)
    sub = _iota((8, kl), 0)
    qs = q * ATT_SCALE
    q2 = jnp.concatenate([jnp.where(lane < 64, qs, 0.0), jnp.where(lane >= 64, qs, 0.0)], axis=0)
    s = bdot(q2, k, "nt")
    cs = [jnp.sum(jnp.where(sub == 2 * hp + e, ccol, 0.0), axis=0, keepdims=True) for e in range(2)]
    s = jnp.concatenate([s[:FOX_Q] - cs[0], s[FOX_Q:] - cs[1]], axis=0)
    diag = jnp.where(tri, s[:, kl - FOX_Q:], NEG)
    s = diag if kl == FOX_Q else jnp.concatenate([s[:, :kl - FOX_Q], diag], axis=1)
    o2 = _attend(s, v)
    return jnp.where(lane < 64, o2[:FOX_Q], o2[FOX_Q:])


def _fox_in_specs():
    return [pl.BlockSpec((FOX_Q, 128), lambda hp, qb: (qb, 12 + hp)),
            pl.BlockSpec((T, 128), lambda hp, qb: (0, 16 + hp)),
            pl.BlockSpec((T, 128), lambda hp, qb: (0, 20 + hp)),
            pl.BlockSpec((8, T), lambda hp, qb: (0, 0))]


def _fox_fwd(proj, cum_c, cat):
    def body(q_ref, k_ref, v_ref, cc_ref, cat_ref, o_ref):
        qb = pl.program_id(1)
        for g in range(FOX_QB):
            kl = FOX_Q * (g + 1)

            @pl.when(qb == g)
            def _(kl=kl):
                o_ref[...] = _fox_block(pl.program_id(0), q_ref[...], k_ref[0:kl, :], v_ref[0:kl, :], cc_ref[:, 0:kl])

    return pl.pallas_call(
        body, name="fox_fwd", grid=(4, FOX_QB), in_specs=_fox_in_specs() + [pl.BlockSpec(memory_space=pl.ANY)],
        out_specs=pl.BlockSpec((FOX_Q, 128), lambda hp, qb: (qb, 4 + hp)),
        out_shape=jax.ShapeDtypeStruct((T, D), F32), input_output_aliases={4: 0},
        compiler_params=_cp(("parallel", "parallel")),
    )(proj, proj, proj, cum_c, cat)


def _fox_bwd(proj, cum_c, dcat):
    def body(q_ref, k_ref, v_ref, cc_ref, do_ref, dq_ref, dk_ref, dv_ref, dcc_ref):
        qb = pl.program_id(1)

        @pl.when(qb == 0)
        def _():
            dk_ref[...] = jnp.zeros_like(dk_ref)
            dv_ref[...] = jnp.zeros_like(dv_ref)
            dcc_ref[...] = jnp.zeros_like(dcc_ref)

        fn = functools.partial(_fox_block, pl.program_id(0))
        for g in range(FOX_QB):
            kl = FOX_Q * (g + 1)

            @pl.when(qb == g)
            def _(kl=kl):
                _, vjp = jax.vjp(fn, q_ref[...], k_ref[0:kl, :], v_ref[0:kl, :], cc_ref[:, 0:kl])
                dq, dk, dv, dcc = vjp(do_ref[...])
                dq_ref[...] = dq
                dk_ref[0:kl, :] += dk
                dv_ref[0:kl, :] += dv
                dcc_ref[:, 0:kl] += dcc

    sds = lambda *s: jax.ShapeDtypeStruct(s, F32)
    return pl.pallas_call(
        body, name="fox_bwd", grid=(4, FOX_QB),
        in_specs=_fox_in_specs() + [pl.BlockSpec((FOX_Q, 128), lambda hp, qb: (qb, 4 + hp))],
        out_specs=(pl.BlockSpec((FOX_Q, 128), lambda hp, qb: (qb, hp)),
                   pl.BlockSpec((T, 128), lambda hp, qb: (0, hp)),
                   pl.BlockSpec((T, 128), lambda hp, qb: (0, hp)),
                   pl.BlockSpec((None, 8, T), lambda hp, qb: (hp, 0, 0))),
        out_shape=(sds(T, 512), sds(T, 512), sds(T, 512), sds(4, 8, T)),
        compiler_params=_cp(("parallel", "arbitrary")),
    )(proj, proj, proj, cum_c, dcat)


BIAS_W = 640


def _rel_onehot():
    j = _iota((REL_PAD, BIAS_W), 1)
    rel = jnp.clip(CA_PAD + CHUNK - 1 - j, -128, 128) + 128
    return (_iota((REL_PAD, BIAS_W), 0) == rel).astype(F32)


def _bias_build(rbp):
    def body(rb_ref, o_ref):
        f = _hdot_raw(rb_ref[...], _rel_onehot(), "nn")
        for q in range(CHUNK):
            o_ref[q] = pltpu.roll(f, (BIAS_W - (CHUNK - 1 - q)) % BIAS_W, 1)[:, :CA_BAND]

    return pl.pallas_call(body, name="ca_bias_build", out_shape=jax.ShapeDtypeStruct((CHUNK, 8, CA_BAND), F32))(rbp)


def _bias_grad(dbias_q):
    def body(db_ref, o_ref):
        acc = jnp.zeros((8, BIAS_W), F32)
        for q in range(CHUNK):
            acc = acc + pltpu.roll(db_ref[q], CHUNK - 1 - q, 1)
        o_ref[...] = _hdot_raw(acc, _rel_onehot(), "nt")

    return pl.pallas_call(body, name="ca_bias_grad", out_shape=jax.ShapeDtypeStruct((8, REL_PAD), F32))(dbias_q)


def _ca_block(c, masked, q, kb, vb, bias2):
    lane = _iota((CHUNK, 128), 1)
    qs = q * ATT_SCALE
    q2 = jnp.concatenate([jnp.where(lane < 64, qs, 0.0), jnp.where(lane >= 64, qs, 0.0)], axis=0)
    s = bdot(q2, kb, "nt") + bias2.reshape(2 * CHUNK, CA_BAND)
    if masked:
        s = jnp.where((c * CHUNK - CA_PAD + _iota((2 * CHUNK, CA_BAND), 1)) >= 0, s, NEG)
    o2 = _attend(s, vb)
    return jnp.where(lane < 64, o2[:CHUNK], o2[CHUNK:])


CA_PER_STEP = 8
CA_ROWS = CA_PER_STEP * CHUNK
CA_MASKED_STEPS = CA_PAD // CA_ROWS


def _ca_fwd(proj, kvpad, bias):
    def body(q_ref, k_ref, v_ref, b_ref, o_ref):
        def run(masked):
            outs = []
            for i in range(CA_PER_STEP):
                c = pl.program_id(1) * CA_PER_STEP + i
                band = pl.ds(pl.multiple_of(c * CHUNK, CHUNK), CA_BAND)
                rows = slice(i * CHUNK, (i + 1) * CHUNK)
                outs.append(_ca_block(c, masked, q_ref[rows, :], k_ref[band, :], v_ref[band, :], b_ref[...]))
            for i in range(CA_PER_STEP):
                o_ref[i * CHUNK:(i + 1) * CHUNK, :] = outs[i]

        pl.when(pl.program_id(1) < CA_MASKED_STEPS)(lambda: run(True))
        pl.when(pl.program_id(1) >= CA_MASKED_STEPS)(lambda: run(False))

    return pl.pallas_call(
        body, name="ca_fwd", grid=(4, NCHUNK // CA_PER_STEP),
        in_specs=[pl.BlockSpec((CA_ROWS, 128), lambda hp, c: (c, hp)),
                  pl.BlockSpec((T + CA_PAD, 128), lambda hp, c: (0, hp)),
                  pl.BlockSpec((T + CA_PAD, 128), lambda hp, c: (0, 4 + hp)),
                  pl.BlockSpec((2, CHUNK, CA_BAND), lambda hp, c: (hp, 0, 0))],
        out_specs=pl.BlockSpec((CA_ROWS, 128), lambda hp, c: (c, hp)),
        out_shape=jax.ShapeDtypeStruct((T, D), F32),
        compiler_params=_cp(("parallel", "parallel")),
    )(proj, kvpad, kvpad, bias)


def _ca_bwd(proj, kvpad, bias, dcat):
    def body(q_ref, k_ref, v_ref, b_ref, do_ref, dq_ref, dk_ref, dv_ref, db_ref):
        c = pl.program_id(1)

        @pl.when(c == 0)
        def _():
            dk_ref[...] = jnp.zeros_like(dk_ref)
            dv_ref[...] = jnp.zeros_like(dv_ref)
            db_ref[...] = jnp.zeros_like(db_ref)

        def run(masked):
            grads, bands = [], []
            for i in range(CA_PER_STEP):
                ci = c * CA_PER_STEP + i
                band = pl.ds(pl.multiple_of(ci * CHUNK, CHUNK), CA_BAND)
                rows = slice(i * CHUNK, (i + 1) * CHUNK)
                fn = functools.partial(_ca_block, ci, masked)
                _, vjp = jax.vjp(fn, q_ref[rows, :], k_ref[band, :], v_ref[band, :], b_ref[...])
                grads.append(vjp(do_ref[rows, :]))
                bands.append(band)
            for i, (dq, _, _, _) in enumerate(grads):
                dq_ref[i * CHUNK:(i + 1) * CHUNK, :] = dq
            for band, (_, dkb, dvb, _) in zip(bands, grads):
                dk_ref[band, :] += dkb
                dv_ref[band, :] += dvb
            db_ref[...] += functools.reduce(lambda a, b: a + b, [g[3] for g in grads])

        pl.when(c < CA_MASKED_STEPS)(lambda: run(True))
        pl.when(c >= CA_MASKED_STEPS)(lambda: run(False))

    sds = lambda *s: jax.ShapeDtypeStruct(s, F32)
    padded = lambda: pl.BlockSpec((T + CA_PAD, 128), lambda hp, c: (0, hp))
    return pl.pallas_call(
        body, name="ca_bwd", grid=(4, NCHUNK // CA_PER_STEP),
        in_specs=[pl.BlockSpec((CA_ROWS, 128), lambda hp, c: (c, hp)),
                  pl.BlockSpec((T + CA_PAD, 128), lambda hp, c: (0, hp)),
                  pl.BlockSpec((T + CA_PAD, 128), lambda hp, c: (0, 4 + hp)),
                  pl.BlockSpec((2, CHUNK, CA_BAND), lambda hp, c: (hp, 0, 0)),
                  pl.BlockSpec((CA_ROWS, 128), lambda hp, c: (c, hp))],
        out_specs=(pl.BlockSpec((CA_ROWS, 128), lambda hp, c: (c, hp)), padded(), padded(),
                   pl.BlockSpec((2, CHUNK, CA_BAND), lambda hp, c: (hp, 0, 0))),
        out_shape=(sds(T, 512), sds(T + CA_PAD, 512), sds(T + CA_PAD, 512), sds(8, CHUNK, CA_BAND)),
        compiler_params=_cp(("parallel", "arbitrary")),
    )(proj, kvpad, kvpad, bias, dcat)


def _lru_pre(xs, cw, cb, wa, ba, wx, bx, lam):
    xc = cb + xs[0] * cw[0:1, :] + xs[1] * cw[1:2, :] + xs[2] * cw[2:3, :] + xs[3] * cw[3:4, :]
    ra = _sigmoid(bdot(xc, wa, "nn") + ba)
    ii = _sigmoid(bdot(xc, wx, "nn") + bx)
    la = 8.0 * ra * _log_sigmoid(lam)
    return jnp.exp(la), jnp.sqrt(-_expm1(2.0 * la)) * (ii * xc)


def _lru_pre_specs():
    full = lambda shape: pl.BlockSpec(shape, lambda i: (0,) * len(shape))
    return [pl.BlockSpec((4, ROWS, 512), lambda i: (0, i, 0)), full((4, 512)), full((1, 512)),
            full((512, 512)), full((1, 512)), full((512, 512)), full((1, 512)), full((1, 512))]


def _lru_pre_fwd(xs, cw, cb, wa, ba, wx, bx, lam):
    def body(xs_ref, cw_ref, cb_ref, wa_ref, ba_ref, wx_ref, bx_ref, lam_ref, a_ref, b_ref):
        a, b = _lru_pre(xs_ref[...], cw_ref[...], cb_ref[...], wa_ref[...], ba_ref[...], wx_ref[...], bx_ref[...],
                        lam_ref[...])
        a_ref[...] = a
        b_ref[...] = b

    row = pl.BlockSpec((ROWS, 512), lambda i: (i, 0))
    sds = jax.ShapeDtypeStruct((T, 512), F32)
    return pl.pallas_call(body, name="lru_pre_fwd", grid=(T // ROWS,), in_specs=_lru_pre_specs(),
                          out_specs=(row, row), out_shape=(sds, sds), compiler_params=_cp(("parallel",)),
                          )(xs, cw, cb, wa, ba, wx, bx, lam)


def _lru_pre_bwd(xs, cw, cb, wa, ba, wx, bx, lam, da, db):
    def body(xs_ref, cw_ref, cb_ref, wa_ref, ba_ref, wx_ref, bx_ref, lam_ref, da_ref, db_ref,
             dxs_ref, dcw_ref, dcb_ref, dwa_ref, dba_ref, dwx_ref, dbx_ref, dlam_ref):
        acc = (dcw_ref, dcb_ref, dwa_ref, dba_ref, dwx_ref, dbx_ref, dlam_ref)

        @pl.when(pl.program_id(0) == 0)
        def _():
            for r in acc:
                r[...] = jnp.zeros_like(r)

        _, vjp = jax.vjp(_lru_pre, xs_ref[...], cw_ref[...], cb_ref[...], wa_ref[...], ba_ref[...], wx_ref[...],
                         bx_ref[...], lam_ref[...])
        grads = vjp((da_ref[...], db_ref[...]))
        dxs_ref[...] = grads[0]
        for r, g in zip(acc, grads[1:]):
            r[...] += g

    row = pl.BlockSpec((ROWS, 512), lambda i: (i, 0))
    specs = _lru_pre_specs()
    sds = lambda *s: jax.ShapeDtypeStruct(s, F32)
    return pl.pallas_call(
        body, name="lru_pre_bwd", grid=(T // ROWS,), in_specs=specs + [row, row], out_specs=tuple(specs),
        out_shape=(sds(4, T, 512), sds(4, 512), sds(1, 512), sds(512, 512), sds(1, 512), sds(512, 512), sds(1, 512),
                   sds(1, 512)),
        compiler_params=_cp(("arbitrary",)),
    )(xs, cw, cb, wa, ba, wx, bx, lam, da, db)


SCAN_ROWS = 8


def _scan8(a, b, towards_later):
    row = _iota((SCAN_ROWS, 512), 0)
    for s in (1, 2, 4):
        if towards_later:
            keep, shift = row >= s, s
        else:
            keep, shift = row < SCAN_ROWS - s, SCAN_ROWS - s
        a_s = jnp.where(keep, pltpu.roll(a, shift, 0), 1.0)
        b_s = jnp.where(keep, pltpu.roll(b, shift, 0), 0.0)
        b = a * b_s + b
        a = a * a_s
    return a, b


def _lru_scan_fwd(a, b):
    def body(a_ref, b_ref, h_ref):
        def step(i, carry):
            rows = pl.ds(pl.multiple_of(i * SCAN_ROWS, SCAN_ROWS), SCAN_ROWS)
            a8, b8 = _scan8(a_ref[rows, :], b_ref[rows, :], True)
            h = a8 * carry + b8
            h_ref[rows, :] = h
            return jnp.broadcast_to(h[SCAN_ROWS - 1:, :], (SCAN_ROWS, 512))

        lax.fori_loop(0, T // SCAN_ROWS, step, jnp.zeros((SCAN_ROWS, 512), F32), unroll=2)

    return pl.pallas_call(body, name="lru_scan_fwd", out_shape=jax.ShapeDtypeStruct((T, 512), F32),
                          compiler_params=pltpu.CompilerParams(vmem_limit_bytes=VMEM_LIMIT))(a, b)


def _lru_scan_bwd(a_next, h_prev, dh):
    def body(a_ref, h_ref, dh_ref, da_ref, db_ref):
        def step(i, carry):
            start = T - SCAN_ROWS * (i + 1)
            rows = pl.ds(pl.multiple_of(start, SCAN_ROWS), SCAN_ROWS)
            a8, b8 = _scan8(a_ref[rows, :], dh_ref[rows, :], False)
            g = a8 * carry + b8
            db_ref[rows, :] = g
            da_ref[rows, :] = g * h_ref[rows, :]
            return jnp.broadcast_to(g[:1, :], (SCAN_ROWS, 512))

        lax.fori_loop(0, T // SCAN_ROWS, step, jnp.zeros((SCAN_ROWS, 512), F32), unroll=2)

    sds = jax.ShapeDtypeStruct((T, 512), F32)
    return pl.pallas_call(body, name="lru_scan_bwd", out_shape=(sds, sds),
                          compiler_params=pltpu.CompilerParams(vmem_limit_bytes=VMEM_LIMIT))(a_next, h_prev, dh)


def _lru_post(h, gate):
    return h * _gelu_tanh(gate)


def _lru_post_fwd(h, proj, cat):
    def body(h_ref, g_ref, cat_ref, o_ref):
        o_ref[...] = _lru_post(h_ref[...], g_ref[...])

    row = pl.BlockSpec((ROWS, 512), lambda i: (i, 0))
    return pl.pallas_call(body, name="lru_post_fwd", grid=(T // ROWS,),
                          in_specs=[row, pl.BlockSpec((ROWS, 512), lambda i: (i, 3)), pl.BlockSpec(memory_space=pl.ANY)],
                          out_specs=pl.BlockSpec((ROWS, 512), lambda i: (i, 1)),
                          out_shape=jax.ShapeDtypeStruct((T, D), F32), input_output_aliases={2: 0},
                          compiler_params=_cp(("parallel",)))(h, proj, cat)


def _lru_post_bwd(h, proj, dcat):
    def body(h_ref, g_ref, do_ref, dh_ref, dg_ref):
        _, vjp = jax.vjp(_lru_post, h_ref[...], g_ref[...])
        dh, dg = vjp(do_ref[...])
        dh_ref[...] = dh
        dg_ref[...] = dg

    row = pl.BlockSpec((ROWS, 512), lambda i: (i, 0))
    sds = jax.ShapeDtypeStruct((T, 512), F32)
    return pl.pallas_call(body, name="lru_post_bwd", grid=(T // ROWS,),
                          in_specs=[row, pl.BlockSpec((ROWS, 512), lambda i: (i, 3)),
                                    pl.BlockSpec((ROWS, 512), lambda i: (i, 1))],
                          out_specs=(row, row), out_shape=(sds, sds), compiler_params=_cp(("parallel",)))(h, proj, dcat)


def _conv_dx(dxs_shift):
    def body(d_ref, o_ref):
        o_ref[...] = d_ref[0] + d_ref[1] + d_ref[2] + d_ref[3]

    row = pl.BlockSpec((ROWS, 512), lambda i: (i, 0))
    return pl.pallas_call(body, name="lru_conv_dx", grid=(T // ROWS,),
                          in_specs=[pl.BlockSpec((4, ROWS, 512), lambda i: (0, i, 0))], out_specs=row,
                          out_shape=jax.ShapeDtypeStruct((T, 512), F32), compiler_params=_cp(("parallel",)))(dxs_shift)


def _position():
    return lax.axis_index("x"), lax.axis_index("y"), lax.axis_index("c")


def _other_chips(x, y):
    return [(1 - x, y), (x, 1 - y), (1 - x, 1 - y)]


def _al(v, n):
    return v * n if isinstance(v, int) else pl.multiple_of(v * n, n)


_AG_ITEMS = [
    ((4, 32, 128), lambda o, s, h: o.at[s, pl.ds(_al(h, 16), 16), :], lambda r, h: r.at[pl.ds(_al(h, 16), 16), :]),
    ((4, 774, 1024), lambda o, s, h: o.at[s, :, pl.ds(_al(h, 512), 512)], lambda r, h: r.at[:, pl.ds(_al(h, 512), 512)]),
    ((1024, 1024), lambda o, s, h: o.at[pl.ds(_al(2 * s + h, 128), 128), :], lambda r, h: r.at[pl.ds(_al(h, 128), 128), :]),
    ((2, 1024, 4096), lambda o, s, h: o.at[h, :, pl.ds(_al(s, 1024), 1024)], lambda r, h: r.at[h]),
    ((2, 4096, 1024), lambda o, s, h: o.at[h, pl.ds(_al(s, 1024), 1024), :], lambda r, h: r.at[h]),
    ((1024, 2560), lambda o, s, h: o.at[pl.ds(_al(h, 512), 512), pl.ds(_al(s, 640), 640)],
     lambda r, h: r.at[pl.ds(_al(h, 512), 512), :]),
    ((1024, 1024), lambda o, s, h: o.at[pl.ds(_al(2 * s + h, 128), 128), :], lambda r, h: r.at[pl.ds(_al(h, 128), 128), :]),
]


_AG_GROUPS = [(0, 1, 2), (3, 4), (5, 6)]

_HBM = pl.BlockSpec(memory_space=pltpu.HBM)
_SEM = pl.BlockSpec(memory_space=pltpu.SEMAPHORE)
_SPLIT = dict(has_side_effects=pltpu.SideEffectType.DATAFLOW_SIDE_EFFECTING)


def _hbm(a):
    return pltpu.with_memory_space_constraint(a, pltpu.HBM)


def _ag_ici_copy(i, j, chip, c, slot, src_ref, land_ref, send_sems, recv_sems, k):
    _, dst, half = _AG_ITEMS[i]
    return pltpu.make_async_remote_copy(src_ref=half(src_ref, c), dst_ref=dst(land_ref, slot, c), send_sem=send_sems.at[k],
                                        recv_sem=recv_sems.at[k], device_id=(*chip, c), device_id_type=MESH)


def _ag_start(shards):
    n = len(_AG_ITEMS)
    ng = len(_AG_GROUPS)
    lands = [lax.empty(shape, s.dtype) for (shape, _, _), s in zip(_AG_ITEMS, shards)]

    def body(*refs):
        srcs, land_refs = refs[:n], refs[n:2 * n]
        sems = refs[2 * n:2 * n + 2 * ng]
        token = refs[-1]
        x, y, c = _position()
        me = 2 * x + y
        for g, items in enumerate(_AG_GROUPS):
            for t, i in enumerate(items):
                for j, chip in enumerate(_other_chips(x, y)):
                    _ag_ici_copy(i, j, chip, c, me, srcs[i], land_refs[i], sems[2 * g], sems[2 * g + 1], 3 * t + j).start()
        token[...] = jnp.zeros_like(token)

    sem_shapes = []
    for items in _AG_GROUPS:
        sem_shapes += [pltpu.SemaphoreType.DMA((3 * len(items),))] * 2
    thru = [pltpu.HBM(a.shape, a.dtype) for a in list(shards) + lands]
    out = pl.pallas_call(
        body, name="allgather_start",
        out_shape=tuple(sem_shapes) + tuple(thru) + (jax.ShapeDtypeStruct((8, 128), F32),),
        in_specs=(_HBM,) * (2 * n),
        out_specs=(_SEM,) * (2 * ng) + (_HBM,) * (2 * n) + (pl.BlockSpec(memory_space=pltpu.VMEM),),
        input_output_aliases={i: 2 * ng + i for i in range(2 * n)},
        compiler_params=pltpu.CompilerParams(**_SPLIT),
    )(*[_hbm(a) for a in list(shards) + lands])
    sems, thru, token = out[:2 * ng], out[2 * ng:-1], out[-1]
    return [(sems[2 * g], sems[2 * g + 1]) for g in range(ng)], list(thru[:n]), list(thru[n:]), token


def _ag_wait(g, sems, srcs, lands, after):
    items = _AG_GROUPS[g]
    m = len(items)

    def body(*refs):
        src_refs, land_refs = refs[:m], refs[m:2 * m]
        send_sems, recv_sems = refs[2 * m], refs[2 * m + 1]
        x, y, c = _position()
        for t, i in enumerate(items):
            for j, chip in enumerate(_other_chips(x, y)):
                cp = _ag_ici_copy(i, j, chip, c, 2 * chip[0] + chip[1], src_refs[t], land_refs[t], send_sems, recv_sems,
                                  3 * t + j)
                cp.wait_send()
                cp.wait_recv()

    ops = [srcs[i] for i in items] + [lands[i] for i in items]
    out = pl.pallas_call(
        body, name=f"allgather_wait_{g}",
        out_shape=tuple(pltpu.HBM(a.shape, a.dtype) for a in ops),
        in_specs=(_HBM,) * (2 * m) + (_SEM, _SEM, pl.BlockSpec(memory_space=pl.ANY)),
        out_specs=(_HBM,) * (2 * m),
        input_output_aliases={i: i for i in range(2 * m)},
        compiler_params=pltpu.CompilerParams(**_SPLIT),
    )(*ops, sems[0], sems[1], after)
    return list(out[:m]), list(out[m:])


def _ag_forward(g, srcs, lands):
    return _ag_sibling(_AG_GROUPS[g], srcs, lands, False, f"allgather_forward_{g}")


def _ag_push_own(srcs, lands):
    return _ag_sibling(tuple(range(len(_AG_ITEMS))), srcs, lands, True, "allgather_push_own")


def _ag_sibling(items, srcs, lands, own, name):
    m = len(items)
    per = 2 if own else 3

    def body(*refs):
        src_refs, in_refs, out_refs = refs[:m], refs[m:2 * m], refs[2 * m:3 * m]
        send_sems, recv_sems = refs[3 * m:]
        x, y, c = _position()
        sibling = (x, y, 1 - c)
        me = 2 * x + y
        if own:
            mine = theirs = [(me, 0), (me, 1)]
        else:
            slots = [2 * chip[0] + chip[1] for chip in _other_chips(x, y)]
            mine, theirs = [(s, c) for s in slots], [(s, 1 - c) for s in slots]
        sends = []
        for t, i in enumerate(items):
            _, dst, half = _AG_ITEMS[i]
            for k, (slot, hc) in enumerate(mine):
                src = half(src_refs[t], hc) if own else dst(in_refs[t], slot, hc)
                sends.append(pltpu.make_async_remote_copy(
                    src_ref=src, dst_ref=dst(out_refs[t], slot, hc), send_sem=send_sems.at[per * t + k],
                    recv_sem=recv_sems.at[per * t + k], device_id=sibling, device_id_type=MESH))
        for cp in sends:
            cp.start()
        for t, i in enumerate(items):
            dst = _AG_ITEMS[i][1]
            for k, (slot, hc) in enumerate(theirs):
                there = dst(out_refs[t], slot, hc)
                pltpu.make_async_remote_copy(src_ref=there, dst_ref=there, send_sem=send_sems.at[per * t + k],
                                             recv_sem=recv_sems.at[per * t + k], device_id=sibling,
                                             device_id_type=MESH).wait_recv()
        for cp in sends:
            cp.wait_send()

    any_spec = pl.BlockSpec(memory_space=pl.ANY)
    return pl.pallas_call(
        body, name=name,
        in_specs=[any_spec] * (2 * m), out_specs=(any_spec,) * m,
        out_shape=tuple(jax.ShapeDtypeStruct(a.shape, a.dtype) for a in lands),
        input_output_aliases={m + t: t for t in range(m)},
        scratch_shapes=[pltpu.SemaphoreType.DMA((per * m,)), pltpu.SemaphoreType.DMA((per * m,))],
    )(*srcs, *lands)


def _pair_swap_copy(g_ref, r_ref, send_sem, recv_sem):
    x, y, c = _position()
    hc = g_ref.shape[2] // 2
    return pltpu.make_async_remote_copy(src_ref=g_ref.at[:, :, pl.ds(_al(1 - c, hc), hc)], dst_ref=r_ref,
                                        send_sem=send_sem, recv_sem=recv_sem, device_id=(x, y, 1 - c),
                                        device_id_type=MESH)


def _pair_swap_start(gb, tag):
    _, rows, cols = gb.shape
    recv = lax.empty((4, rows, cols // 2), gb.dtype)

    def body(g_ref, r_ref, send_sem, recv_sem, g_thru, r_thru, token):
        _pair_swap_copy(g_ref, r_ref, send_sem, recv_sem).start()
        token[...] = jnp.zeros_like(token)

    return pl.pallas_call(
        body, name="grad_pair_swap_start_" + tag,
        out_shape=(pltpu.SemaphoreType.DMA(()), pltpu.SemaphoreType.DMA(()), pltpu.HBM(gb.shape, gb.dtype),
                   pltpu.HBM(recv.shape, recv.dtype), jax.ShapeDtypeStruct((8, 128), F32)),
        in_specs=(_HBM, _HBM), out_specs=(_SEM, _SEM, _HBM, _HBM, pl.BlockSpec(memory_space=pltpu.VMEM)),
        input_output_aliases={0: 2, 1: 3},
        compiler_params=pltpu.CompilerParams(**_SPLIT),
    )(_hbm(gb), _hbm(recv))


def _pair_swap_wait(started, after, tag):
    send_sem, recv_sem, gb, recv, _ = started

    def body(g_ref, r_ref, send_sem, recv_sem, after_ref, g_out, r_out):
        cp = _pair_swap_copy(g_ref, r_ref, send_sem, recv_sem)
        cp.wait_send()
        cp.wait_recv()

    return pl.pallas_call(
        body, name="grad_pair_swap_wait_" + tag,
        out_shape=(pltpu.HBM(gb.shape, gb.dtype), pltpu.HBM(recv.shape, recv.dtype)),
        in_specs=(_HBM, _HBM, _SEM, _SEM, pl.BlockSpec(memory_space=pl.ANY)), out_specs=(_HBM, _HBM),
        input_output_aliases={0: 0, 1: 1},
        compiler_params=pltpu.CompilerParams(**_SPLIT),
    )(gb, recv, send_sem, recv_sem, after)


def _handover_copy(r_ref, send_sem, recv_sem, core):
    x, y, c = _position()
    hc = r_ref.shape[1] // 2
    cols = r_ref.at[:, pl.ds(_al(core, hc), hc)]
    return pltpu.make_async_remote_copy(src_ref=cols, dst_ref=cols, send_sem=send_sem, recv_sem=recv_sem,
                                        device_id=(x, y, 1 - c), device_id_type=MESH)


def _handover_start(red, tag):
    def body(r_ref, send_sem, recv_sem, r_thru, token):
        _handover_copy(r_ref, send_sem, recv_sem, lax.axis_index("c")).start()
        token[...] = jnp.zeros_like(token)

    return pl.pallas_call(
        body, name="grad_handover_start_" + tag,
        out_shape=(pltpu.SemaphoreType.DMA(()), pltpu.SemaphoreType.DMA(()), pltpu.HBM(red.shape, red.dtype),
                   jax.ShapeDtypeStruct((8, 128), F32)),
        in_specs=(_HBM,), out_specs=(_SEM, _SEM, _HBM, pl.BlockSpec(memory_space=pltpu.VMEM)),
        input_output_aliases={0: 2},
        compiler_params=pltpu.CompilerParams(**_SPLIT),
    )(_hbm(red))


def _handover_wait(started, after, tag):
    send_sem, recv_sem, red, _ = started

    def body(r_ref, send_sem, recv_sem, after_ref, r_out):
        c = lax.axis_index("c")
        _handover_copy(r_ref, send_sem, recv_sem, c).wait_send()
        _handover_copy(r_ref, send_sem, recv_sem, 1 - c).wait_recv()

    return pl.pallas_call(
        body, name="grad_handover_wait_" + tag,
        out_shape=pltpu.HBM(red.shape, red.dtype),
        in_specs=(_HBM, _SEM, _SEM, pl.BlockSpec(memory_space=pl.ANY)), out_specs=_HBM,
        input_output_aliases={0: 0},
        compiler_params=pltpu.CompilerParams(**_SPLIT),
    )(red, send_sem, recv_sem, after)


def _handover(red, tag):
    started = _handover_start(red, tag)
    return _handover_wait(started, started[3], tag)


def _a2a_copy(j, chip, c, p_ref, q_ref, q_slot, send_sems, recv_sems):
    return pltpu.make_async_remote_copy(src_ref=p_ref.at[2 * chip[0] + chip[1]], dst_ref=q_ref.at[q_slot],
                                        send_sem=send_sems.at[j], recv_sem=recv_sems.at[j], device_id=(*chip, c),
                                        device_id_type=MESH)


def _a2a_start(p, tag):
    def body(p_ref, q_ref, send_sems, recv_sems, p_thru, q_thru, token):
        x, y, c = _position()
        for j, chip in enumerate(_other_chips(x, y)):
            _a2a_copy(j, chip, c, p_ref, q_ref, 2 * x + y, send_sems, recv_sems).start()
        token[...] = jnp.zeros_like(token)

    return pl.pallas_call(
        body, name="grad_alltoall_start_" + tag,
        out_shape=(pltpu.SemaphoreType.DMA((3,)), pltpu.SemaphoreType.DMA((3,)), pltpu.HBM(p.shape, p.dtype),
                   pltpu.HBM(p.shape, p.dtype), jax.ShapeDtypeStruct((8, 128), F32)),
        in_specs=(_HBM, _HBM), out_specs=(_SEM, _SEM, _HBM, _HBM, pl.BlockSpec(memory_space=pltpu.VMEM)),
        input_output_aliases={0: 2, 1: 3},
        compiler_params=pltpu.CompilerParams(**_SPLIT),
    )(_hbm(p), _hbm(lax.empty(p.shape, p.dtype)))


def _a2a_wait(send_sems, recv_sems, p, q, after, tag):
    def body(p_ref, q_ref, send_sems, recv_sems, after_ref, p_out, q_out):
        x, y, c = _position()
        for j, chip in enumerate(_other_chips(x, y)):
            cp = _a2a_copy(j, chip, c, p_ref, q_ref, 2 * chip[0] + chip[1], send_sems, recv_sems)
            cp.wait_send()
            cp.wait_recv()

    return pl.pallas_call(
        body, name="grad_alltoall_wait_" + tag,
        out_shape=(pltpu.HBM(p.shape, p.dtype), pltpu.HBM(q.shape, q.dtype)),
        in_specs=(_HBM, _HBM, _SEM, _SEM, pl.BlockSpec(memory_space=pl.ANY)), out_specs=(_HBM, _HBM),
        input_output_aliases={0: 0, 1: 1},
        compiler_params=pltpu.CompilerParams(**_SPLIT),
    )(p, q, send_sems, recv_sems, after)


def _comm_rows(rows):
    return next(t for t in (512, 384, 256, 128) if rows % t == 0)


def _pair_add(gb, recv, where, tag):
    _, rows, cols = gb.shape
    hc = cols // 2
    tr = _comm_rows(rows)

    def body(w_ref, g_ref, r_ref, o_ref):
        o_ref[...] = (g_ref[...].astype(F32) + r_ref[...].astype(F32)).astype(o_ref.dtype)

    return pl.pallas_call(
        body, name="grad_pair_add_" + tag,
        grid_spec=pltpu.PrefetchScalarGridSpec(
            num_scalar_prefetch=1, grid=(4, rows // tr),
            in_specs=[pl.BlockSpec((None, tr, hc), lambda s, j, w_ref: (s, j, w_ref[0])),
                      pl.BlockSpec((None, tr, hc), lambda s, j, w_ref: (s, j, 0))],
            out_specs=pl.BlockSpec((None, tr, hc), lambda s, j, w_ref: (s, j, 0))),
        out_shape=jax.ShapeDtypeStruct((4, rows, hc), gb.dtype),
        compiler_params=_cp(("parallel", "parallel")),
    )(where, gb, recv)


def _sum_chips(p, q, where, tag):
    _, rows, hc = q.shape
    tr = _comm_rows(rows)

    def body(w_ref, p_ref, qa_ref, qb_ref, qc_ref, o_ref):
        me = w_ref[1]
        own, qa, qb, qc = (r[...].astype(F32) for r in (p_ref, qa_ref, qb_ref, qc_ref))
        v0 = jnp.where(me == 0, own, qa)
        v1 = jnp.where(me == 1, own, jnp.where(me == 0, qa, qb))
        v2 = jnp.where(me == 2, own, jnp.where(me < 2, qb, qc))
        v3 = jnp.where(me == 3, own, qc)
        o_ref[...] = ((v0 + v1) + v2) + v3

    slot = lambda k: pl.BlockSpec((None, tr, hc), lambda j, w_ref: (w_ref[k], j, 0))
    return pl.pallas_call(
        body, name="grad_sum_chips_" + tag,
        grid_spec=pltpu.PrefetchScalarGridSpec(
            num_scalar_prefetch=1, grid=(rows // tr,),
            in_specs=[slot(1), slot(2), slot(3), slot(4)],
            out_specs=pl.BlockSpec((tr, hc), lambda j, w_ref: (j, w_ref[0]))),
        out_shape=jax.ShapeDtypeStruct((rows, 2 * hc), F32),
        compiler_params=_cp(("parallel",)),
    )(where, p, q, q, q)


def _shard_major(g, axis):
    shape = g.shape
    g = g.reshape(shape[:axis] + (4, shape[axis] // 4) + shape[axis + 1:])
    return jnp.moveaxis(g, axis, 0).reshape(4, -1)


def _unshard(g4, shape, axis):
    n = shape[axis] // 4
    g = g4.reshape((4,) + shape[:axis] + (n,) + shape[axis + 1:])
    return jnp.moveaxis(g, 0, axis).reshape(shape)


def _split(flat, shapes):
    out, off = [], 0
    for shp in shapes:
        n = 1
        for d in shp:
            n *= d
        out.append(flat[..., off:off + n].reshape(flat.shape[:-1] + tuple(shp)))
        off += n
    return out


def _even_rows_to_kernel(wt):
    return jnp.concatenate([wt[:1536], wt[1552:3088], wt[1536:1552], wt[3088:3096],
                            jnp.zeros((PE - 3096, wt.shape[1]), wt.dtype)], axis=0)


def _block_diag(w):
    eye = jnp.eye(8, dtype=w.dtype)
    return (w[:, :, None, :] * eye[:, None, :, None]).reshape(512, 512)


def _diag_blocks(g):
    eye = jnp.eye(8, dtype=g.dtype)
    return (g.reshape(8, 64, 8, 64) * eye[:, None, :, None]).sum(axis=2)


def _shift_down(a, s):
    return a if s == 0 else jnp.pad(a, ((s, 0), (0, 0)))[:a.shape[0]]


def _shift_up(a, s):
    return a if s == 0 else jnp.pad(a, ((0, s), (0, 0)))[s:]


SMALL_SHARDED_SHAPES = [(2, 4, 256), (16, 64), (4, 128), (128,), (128,), (128,), (128,)]
REPL_SHAPES = [(256,), (512,), (8,), (8, 257), (8, 64, 64), (8, 64, 64)]


def kernel(x, norm_w, w_in_even, gla_w_a_up, gla_b_a, gla_norm_w, fox_b_f, w_out_even, w_in_odd, rel_bias, conv_w, conv_b, lru_w_a, lru_b_a, lru_w_x, lru_b_x, lru_lambda, w_out_odd, w_mlp_up, w_mlp_down, loss_target, m_norm_w, m_w_in_even, m_gla_w_a_up, m_gla_b_a, m_gla_norm_w, m_fox_b_f, m_w_out_even, m_w_in_odd, m_rel_bias, m_conv_w, m_conv_b, m_lru_w_a, m_lru_b_a, m_lru_w_x, m_lru_b_x, m_lru_lambda, m_w_out_odd, m_w_mlp_up, m_w_mlp_down, v_norm_w, v_w_in_even, v_gla_w_a_up, v_gla_b_a, v_gla_norm_w, v_fox_b_f, v_w_out_even, v_w_in_odd, v_rel_bias, v_conv_w, v_conv_b, v_lru_w_a, v_lru_b_a, v_lru_w_x, v_lru_b_x, v_lru_lambda, v_w_out_odd, v_w_mlp_up, v_w_mlp_down):
    c_idx = lax.axis_index("c")

    small_local = [norm_w, gla_w_a_up[0], conv_w[0], conv_b[0], lru_b_a[0], lru_b_x[0], lru_lambda[0]]
    small_src = jnp.concatenate([a.reshape(-1) for a in small_local]).reshape(32, 128)
    mine = [small_src, w_in_even[0].T.astype(BF16), w_out_even[0].astype(BF16), w_mlp_up.astype(BF16),
            w_mlp_down.astype(BF16), w_in_odd[0].astype(BF16), w_out_odd[0].astype(BF16)]
    ag_sems, ag_srcs, ag_lands, ag_token = _ag_start(mine)
    ag_lands = list(_ag_push_own(ag_srcs, ag_lands))

    def gathered(g, after):
        srcs_g, lands_g = _ag_wait(g, ag_sems[g], ag_srcs, ag_lands, after)
        return _ag_forward(g, srcs_g, lands_g)

    small4, w_in_e4, w_out_e = gathered(0, ag_token)
    me = 2 * lax.axis_index("x") + lax.axis_index("y")
    others = [k + (k >= me).astype(jnp.int32) for k in range(3)]
    where = jnp.stack([c_idx, me] + others).astype(jnp.int32)

    w_in_e_t = _even_rows_to_kernel(w_in_e4.reshape(3096, D))
    g_small = _split(small4.reshape(4, 32 * 128), SMALL_SHARDED_SHAPES)
    nw_full = _unshard(g_small[0], (2, 4, 1024), 2)
    wa_up = _unshard(g_small[1], (16, 256), 1)
    cw = _unshard(g_small[2], (4, 512), 1)
    cb, lba, lbx, lam = [_unshard(g, (512,), 0).reshape(1, 512) for g in g_small[3:]]
    nw = lambda layer, i: nw_full[layer, i].reshape(1, D)

    wa_pad = jnp.pad(wa_up, ((0, 128 - 16), (0, 0)))
    gla_ba = gla_b_a.reshape(1, 256)
    gla_nw = gla_norm_w.reshape(1, 512)
    fox_bpad = jnp.pad(fox_b_f.reshape(1, 8), ((0, 0), (FOX_LANE0, 128 - FOX_LANE0 - 8)))
    rbp = jnp.pad(rel_bias[0], ((0, 0), (0, REL_PAD - 257)))
    wa_bd = _block_diag(lru_w_a[0])
    wx_bd = _block_diag(lru_w_x[0])

    x0 = x[0]
    tgt = loss_target[0]

    h0 = _prenorm(x0, nw(0, 0), "prenorm_l0_mix")
    proj_e = _mm(h0, w_in_e_t, "nt", tm=1024, tn=640, name="mm_in_even")
    cat0, s_prev = _gla_fwd(proj_e, wa_pad, gla_ba, gla_nw)
    cum_r = _fox_gate_fwd(proj_e, fox_bpad)
    cum_c = cum_r[:, FOX_LANE0:FOX_LANE0 + 8].T
    cat0 = _fox_fwd(proj_e, cum_c, cat0)
    mix0 = _mm(cat0, w_out_e, "nn", tm=1024, tn=512, name="mm_out_even")
    x1, h1 = _post_pre_fwd(x0, mix0, nw(0, 1), nw(0, 2), "post_pre_l0_mix")
    w_up, w_dn = gathered(1, x1)
    a0, r0 = _mm(h1, w_up, "nn", tm=1024, tn=1024, b_layer=0, relu_pair=True, name="mm_up_l0")
    d0 = _mm(a0, w_dn, "nn", tm=1024, tn=512, b_layer=0, name="mm_down_l0")
    x2, h2 = _post_pre_fwd(x1, d0, nw(0, 3), nw(1, 0), "post_pre_l0_mlp")

    w_in_o, w_out_o = gathered(2, x2)
    proj_o = _mm(h2, w_in_o, "nn", tm=1024, tn=640, name="mm_in_odd")
    bias_q = _bias_build(rbp)
    bias = bias_q.transpose(1, 0, 2)
    kvpad = jnp.pad(proj_o[:, 512:1536], ((CA_PAD, 0), (0, 0)))
    cat1 = _ca_fwd(proj_o, kvpad, bias)
    x_in = proj_o[:, 2048:2560]
    xs = jnp.stack([_shift_down(x_in, 3 - j) for j in range(4)])
    lru_a, lru_b = _lru_pre_fwd(xs, cw, cb, wa_bd, lba, wx_bd, lbx, lam)
    hh = _lru_scan_fwd(lru_a, lru_b)
    cat1 = _lru_post_fwd(hh, proj_o, cat1)
    mix1 = _mm(cat1, w_out_o, "nn", tm=1024, tn=512, name="mm_out_odd")
    x3, h3 = _post_pre_fwd(x2, mix1, nw(1, 1), nw(1, 2), "post_pre_l1_mix")
    a1, r1 = _mm(h3, w_up, "nn", tm=1024, tn=1024, b_layer=1, relu_pair=True, name="mm_up_l1")
    d1 = _mm(a1, w_dn, "nn", tm=1024, tn=512, b_layer=1, name="mm_down_l1")
    g4, loss_part, dd1, dnw13 = _post_loss(x3, d1, nw(1, 3), tgt)
    loss = lax.psum(loss_part[0, 0], ("x", "y", "c"))

    def rs_begin(swap, after, tag):
        gb, recv = _pair_swap_wait(swap, after, tag)
        return _a2a_start(_pair_add(gb, recv, where, tag), tag)

    def rs_end(started, after, tag):
        send_sems, recv_sems, p, q, _ = started
        p, q = _a2a_wait(send_sems, recv_sems, p, q, after, tag)
        return _handover(_sum_chips(p, q, where, tag), tag)

    gba = lax.dynamic_update_slice(lax.empty((4, GA_ROWS, D), BF16), jnp.zeros((4, GA_UP - GA_GAP, D), BF16),
                                   (0, GA_GAP, 0))
    gba = _mm(a1, dd1, "tn", tm=512, tn=1024, into=(gba, 1024, GA_DN), name="mm_down_l1_dw")
    du1 = _mm(dd1, w_dn, "nt", tm=1024, tn=1024, b_layer=1, times2=r1, out_dtype=BF16, name="mm_down_l1_dx")
    gba = _mm(du1, h3, "tn", tm=512, tn=1024, into=(gba, 1024, GA_UP), name="mm_up_l1_dw")
    dh3 = _mm(du1, w_up, "nt", tm=1024, tn=512, b_layer=1, name="mm_up_l1_dx")
    g3, dmix1, dnw12, dnw11 = _pre_post_bwd(x3, nw(1, 2), dh3, g4, mix1, nw(1, 1), "pre_post_bwd_l1_mlp")
    gba = _mm(cat1, dmix1, "tn", tm=128, tn=1024, into=(gba, 256, GA_OUT_O), name="mm_out_odd_dw")
    dcat1 = _mm(dmix1, w_out_o, "nt", tm=1024, tn=512, name="mm_out_odd_dx")

    dq_c, dkpad, dvpad, dbias = _ca_bwd(proj_o, kvpad, bias, dcat1)
    g_rel = _bias_grad(jnp.pad(dbias.transpose(1, 0, 2), ((0, 0), (0, 0), (0, BIAS_W - CA_BAND))))[:, :257]
    dhh, dgate = _lru_post_bwd(hh, proj_o, dcat1)
    da_l, db_l = _lru_scan_bwd(_shift_up(lru_a, 1), _shift_down(hh, 1), dhh)
    dxs, g_cw, g_cb, g_wa_bd, g_lba, g_wx_bd, g_lbx, g_lam = _lru_pre_bwd(xs, cw, cb, wa_bd, lba, wx_bd, lbx, lam, da_l, db_l)
    dx_in = _conv_dx(jnp.stack([_shift_up(dxs[j], 3 - j) for j in range(4)]))
    dproj_o = jnp.concatenate([dq_c, dkpad[CA_PAD:], dvpad[CA_PAD:], dgate, dx_in], axis=1).astype(BF16)
    gba = _mm(dproj_o, h2, "tn", tm=128, tn=1024, into=(gba, 640, GA_IN_O), name="mm_in_odd_dw")
    swap_a = _pair_swap_start(gba, "a")
    dh2 = _mm(dproj_o, w_in_o, "nt", tm=1024, tn=512, name="mm_in_odd_dx")
    g2, dd0, dnw10, dnw03 = _pre_post_bwd(x2, nw(1, 0) + swap_a[4][0, 0], dh2, g3, d0, nw(0, 3), "pre_post_bwd_l1_mix")
    rs_a = rs_begin(swap_a, g2, "a")

    gbb = lax.empty((4, GB_ROWS, D), BF16)
    gbb = _mm(a0, dd0, "tn", tm=512, tn=1024, into=(gbb, 1024, GB_DN), name="mm_down_l0_dw")
    du0 = _mm(dd0, w_dn, "nt", tm=1024, tn=1024, b_layer=0, times2=r0, out_dtype=BF16, name="mm_down_l0_dx")
    gbb = _mm(du0, h1, "tn", tm=512, tn=1024, into=(gbb, 1024, GB_UP), name="mm_up_l0_dw")
    swap_b = _pair_swap_start(gbb, "b")
    dh1 = _mm(du0, w_up, "nt", tm=1024, tn=512, b_layer=0, name="mm_up_l0_dx")
    g1, dmix0, dnw02, dnw01 = _pre_post_bwd(x1, nw(0, 2) + (swap_b[4][0, 0] + rs_a[4][0, 0]), dh1, g2, mix0, nw(0, 1),
                                            "pre_post_bwd_l0_mlp")
    rs_b = rs_begin(swap_b, g1, "b")
    gbc = lax.empty((4, GC_ROWS, D), BF16)
    gbc = _mm(cat0, dmix0, "tn", tm=128, tn=1024, into=(gbc, 256, GC_OUT_E), name="mm_out_even_dw")
    dcat0 = _mm(dmix0, w_out_e, "nt", tm=1024, tn=512, name="mm_out_even_dx")

    dq_g, dk_g, dv_g, dr_g, daux_g, g_wa_pad, g_gla_ba, g_gla_nw = _gla_bwd(
        proj_e, s_prev, wa_pad, gla_ba, gla_nw + rs_b[4][0, 0], dcat0)
    dq_f, dk_f, dv_f, dccol = _fox_bwd(proj_e, cum_c, dcat0)
    dccol_t = jnp.pad(dccol.sum(axis=0).T, ((0, 0), (FOX_LANE0, 128 - FOX_LANE0 - 8)))
    daux, g_fox_bpad = _fox_gate_bwd(proj_e, fox_bpad, dccol_t, daux_g)
    dproj_e = jnp.concatenate([dq_g, dk_g, dv_g, dr_g, dq_f, dk_f, dv_f, daux], axis=1).astype(BF16)
    gt_in_e = _mm(dproj_e, h0, "tn", tm=640, tn=1024, out_dtype=BF16, name="mm_in_even_dw")
    dh0 = _mm(dproj_e, w_in_e_t, "nn", tm=1024, tn=512, name="mm_in_even_dx")
    grad_x, dnw00 = _norm_bwd(x0, nw(0, 0), dh0, g1, "prenorm_l0_mix_bwd")

    def rs_reduce(started, after, tag):
        send_sems, recv_sems, p, q, _ = started
        p, q = _a2a_wait(send_sems, recv_sems, p, q, after, tag)
        return _handover_start(_sum_chips(p, q, where, tag), tag)

    ho_a = rs_reduce(rs_a, grad_x, "a")
    ho_b = rs_reduce(rs_b, ho_a[3], "b")

    g_norm = jnp.stack([jnp.concatenate([dnw00, dnw01, dnw02, dnw03]), jnp.concatenate([dnw10, dnw11, dnw12, dnw13])])
    sharded = [(g_norm, 2), (g_wa_pad[:16], 1), (g_cw, 1), (g_cb[0], 0), (g_lba[0], 0), (g_lbx[0], 0), (g_lam[0], 0)]
    replicated = [g_gla_ba[0], g_gla_nw[0], g_fox_bpad[0, FOX_LANE0:FOX_LANE0 + 8], g_rel, _diag_blocks(g_wa_bd),
                  _diag_blocks(g_wx_bd)]
    small4 = jnp.concatenate([_shard_major(g, ax) for g, ax in sharded]
                             + [jnp.broadcast_to(g.reshape(1, -1), (4, g.size)) for g in replicated], axis=1)
    n_small = small4.shape[1]
    small_rows = GC_ROWS - GC_TAIL - 774
    small4 = jnp.pad(small4, ((0, 0), (0, small_rows * D - n_small))).reshape(4, small_rows, D)
    gt_rows = jnp.concatenate([gt_in_e[:1536], gt_in_e[3072:3088], gt_in_e[1536:3072], gt_in_e[3088:3096]], axis=0)
    tail = jnp.concatenate([gt_rows.reshape(4, 774, D), small4.astype(BF16)], axis=1)
    gbc = lax.dynamic_update_slice(gbc, tail, (0, GC_TAIL, 0))
    swap_c = _pair_swap_start(gbc, "c")
    rs_c = rs_begin(swap_c, swap_c[4], "c")

    red_a = _handover_wait(ho_a, rs_c[4], "a")
    red_b = _handover_wait(ho_b, red_a, "b")
    early = dict(
        w_mlp_up=_adamw_from(w_mlp_up, m_w_mlp_up, v_w_mlp_up, [(red_b, GB_UP, True), (red_a, GA_UP, True)], 256,
                             "adamw_w_mlp_up"),
        w_mlp_down=_adamw_from(w_mlp_down, m_w_mlp_down, v_w_mlp_down, [(red_b, GB_DN, False), (red_a, GA_DN, False)],
                               256, "adamw_w_mlp_down"),
        w_in_odd=_adamw_from(w_in_odd, m_w_in_odd, v_w_in_odd, [(red_a, GA_IN_O, True)], 256, "adamw_w_in_odd"),
        w_out_odd=_adamw_from(w_out_odd, m_w_out_odd, v_w_out_odd, [(red_a, GA_OUT_O, False)], 128, "adamw_w_out_odd"))
    red_c = rs_end(rs_c, early["w_out_odd"][3], "c")

    g_small = _split(red_c[GC_TAIL + 774:].reshape(-1)[:n_small], SMALL_SHARDED_SHAPES + REPL_SHAPES)
    g_of = dict(zip(["norm_w", "gla_w_a_up", "conv_w", "conv_b", "lru_b_a", "lru_b_x", "lru_lambda", "gla_b_a",
                     "gla_norm_w", "fox_b_f", "rel_bias", "lru_w_a", "lru_w_x"], g_small))
    g_of.update(w_in_even=red_c[GC_TAIL:GC_TAIL + 774])
    early["w_out_even"] = _adamw_from(w_out_even, m_w_out_even, v_w_out_even, [(red_c, GC_OUT_E, False)], 256,
                                      "adamw_w_out_even")

    names = ["norm_w", "w_in_even", "gla_w_a_up", "gla_b_a", "gla_norm_w", "fox_b_f", "w_out_even", "w_in_odd", "rel_bias",
             "conv_w", "conv_b", "lru_w_a", "lru_b_a", "lru_w_x", "lru_b_x", "lru_lambda", "w_out_odd", "w_mlp_up",
             "w_mlp_down"]
    w_of = dict(norm_w=norm_w, w_in_even=w_in_even, gla_w_a_up=gla_w_a_up, gla_b_a=gla_b_a, gla_norm_w=gla_norm_w,
                fox_b_f=fox_b_f, w_out_even=w_out_even, w_in_odd=w_in_odd, rel_bias=rel_bias, conv_w=conv_w, conv_b=conv_b,
                lru_w_a=lru_w_a, lru_b_a=lru_b_a, lru_w_x=lru_w_x, lru_b_x=lru_b_x, lru_lambda=lru_lambda,
                w_out_odd=w_out_odd, w_mlp_up=w_mlp_up, w_mlp_down=w_mlp_down)
    m_of = dict(norm_w=m_norm_w, w_in_even=m_w_in_even, gla_w_a_up=m_gla_w_a_up, gla_b_a=m_gla_b_a,
                gla_norm_w=m_gla_norm_w, fox_b_f=m_fox_b_f, w_out_even=m_w_out_even, w_in_odd=m_w_in_odd,
                rel_bias=m_rel_bias, conv_w=m_conv_w, conv_b=m_conv_b, lru_w_a=m_lru_w_a, lru_b_a=m_lru_b_a,
                lru_w_x=m_lru_w_x, lru_b_x=m_lru_b_x, lru_lambda=m_lru_lambda, w_out_odd=m_w_out_odd,
                w_mlp_up=m_w_mlp_up, w_mlp_down=m_w_mlp_down)
    v_of = dict(norm_w=v_norm_w, w_in_even=v_w_in_even, gla_w_a_up=v_gla_w_a_up, gla_b_a=v_gla_b_a,
                gla_norm_w=v_gla_norm_w, fox_b_f=v_fox_b_f, w_out_even=v_w_out_even, w_in_odd=v_w_in_odd,
                rel_bias=v_rel_bias, conv_w=v_conv_w, conv_b=v_conv_b, lru_w_a=v_lru_w_a, lru_b_a=v_lru_b_a,
                lru_w_x=v_lru_w_x, lru_b_x=v_lru_b_x, lru_lambda=v_lru_lambda, w_out_odd=v_w_out_odd,
                w_mlp_up=v_w_mlp_up, w_mlp_down=v_w_mlp_down)
    grads, deltas, new_ms, new_vs = [], [], [], []
    for n in names:
        w = w_of[n]
        if n in early:
            g, d, mn, vn = early[n]
            grads.append(g)
            deltas.append(d)
            new_ms.append(mn)
            new_vs.append(vn)
            continue
        if n == "w_in_even":
            to_view = lambda a: a[0].T
            from_view = lambda a: a.T[None]
        else:
            view = w.shape if w.ndim <= 3 else w.shape[-3:]
            to_view = lambda a, view=view: a.reshape(view)
            from_view = lambda a, w=w: a.reshape(w.shape)
        g = g_of[n] if n == "w_in_even" else to_view(g_of[n])
        d, mn, vn = _adamw(to_view(w), g, to_view(m_of[n]), to_view(v_of[n]), "adamw_" + n)
        grads.append(from_view(g))
        deltas.append(from_view(d))
        new_ms.append(from_view(mn))
        new_vs.append(from_view(vn))

    return (loss, grad_x.reshape(1, T, D), *grads, *deltas, *new_ms, *new_vs)
```

```python
import functools

import jax
import jax.numpy as jnp
from jax import lax
from jax.experimental import pallas as pl
from jax.experimental.pallas import tpu as pltpu

F32 = jnp.float32
BF16 = jnp.bfloat16
MESH = pl.DeviceIdType.MESH

T = 2048
D = 1024
DFF = 4096
EPS = 1e-6
CHUNK = 64
NCHUNK = T // CHUNK
PE = 3200
PO = 2560
AUX_BLK = 3072 // 128
FOX_LANE0 = 16
GLA_SCALE = 64 ** -0.5
ATT_SCALE = 64 ** -0.5
NEG = float(jnp.finfo(jnp.float32).min)
CA_BAND = 576
CA_PAD = 512
REL_PAD = 384

VMEM_LIMIT = 48 * 1024 * 1024

ADAM_LR, ADAM_B1, ADAM_B2, ADAM_EPS, ADAM_WD, ADAM_STEP = 0.001, 0.9, 0.999, 1e-08, 0.01, 10

GA_ROWS, GA_IN_O, GA_OUT_O, GA_GAP, GA_UP, GA_DN = 3072, 0, 640, 896, 1024, 2048
GB_ROWS, GB_UP, GB_DN = 2048, 0, 1024
GC_ROWS, GC_OUT_E, GC_TAIL = 1152, 0, 256

_DIMS = {"nn": (((1,), (0,)), ((), ())), "nt": (((1,), (1,)), ((), ())), "tn": (((0,), (0,)), ((), ()))}


def _cp(sem, **kw):
    return pltpu.CompilerParams(dimension_semantics=sem, vmem_limit_bytes=VMEM_LIMIT, **kw)


def _dot(a, b, mode):
    return lax.dot_general(a.astype(BF16), b.astype(BF16), _DIMS[mode], preferred_element_type=F32)


@functools.partial(jax.custom_vjp, nondiff_argnums=(2,))
def bdot(a, b, mode):
    return _dot(a, b, mode)


def _bdot_fwd(a, b, mode):
    return _dot(a, b, mode), (a, b)


def _bdot_bwd(mode, res, g):
    a, b = res
    if mode == "nn":
        da, db = _dot(g, b, "nt"), _dot(a, g, "tn")
    elif mode == "nt":
        da, db = _dot(g, b, "nn"), _dot(g, a, "tn")
    else:
        da, db = _dot(b, g, "nt"), _dot(a, g, "nn")
    return da.astype(a.dtype), db.astype(b.dtype)


bdot.defvjp(_bdot_fwd, _bdot_bwd)


def _hdot_raw(a, b, mode):
    return lax.dot_general(a, b, _DIMS[mode], precision=lax.Precision.HIGHEST, preferred_element_type=F32)


@functools.partial(jax.custom_vjp, nondiff_argnums=(2,))
def hdot(a, b, mode):
    return _hdot_raw(a, b, mode)


def _hdot_fwd(a, b, mode):
    return _hdot_raw(a, b, mode), (a, b)


def _hdot_bwd(mode, res, g):
    a, b = res
    if mode == "nn":
        return _hdot_raw(g, b, "nt"), _hdot_raw(a, g, "tn")
    if mode == "nt":
        return _hdot_raw(g, b, "nn"), _hdot_raw(g, a, "tn")
    return _hdot_raw(b, g, "nt"), _hdot_raw(a, g, "nn")


hdot.defvjp(_hdot_fwd, _hdot_bwd)


def _log_sigmoid(x):
    return jnp.minimum(x, 0.0) - jnp.log(1.0 + jnp.exp(-jnp.abs(x)))


def _sigmoid(x):
    return 1.0 / (1.0 + jnp.exp(-x))


def _expm1(x):
    series = x * (1.0 + x * 0.5 * (1.0 + x * (1.0 / 3.0) * (1.0 + x * 0.25)))
    return jnp.where(jnp.abs(x) < 0.03, series, jnp.exp(x) - 1.0)


def _gelu_tanh(x):
    return 0.5 * x * (1.0 + jnp.tanh(0.7978845608028654 * (x + 0.044715 * x * x * x)))


def _iota(shape, dim):
    return lax.broadcasted_iota(jnp.int32, shape, dim)


def _mm(a, b, mode, *, tm, tn, tk=None, out_dtype=F32, name, b_layer=None, into=None, relu_pair=False, times2=None):
    b2 = b.shape[-2:]
    if mode == "nn":
        (m, k), n = a.shape, b2[1]
    elif mode == "nt":
        (m, k), n = a.shape, b2[0]
    else:
        (k, m), n = a.shape, b2[1]
    tk = k if tk is None else tk
    assert m % tm == 0 and n % tn == 0 and k % tk == 0, (name, a.shape, b.shape)
    nk = k // tk
    a_spec = {"nn": pl.BlockSpec((tm, tk), lambda i, j, kk: (i, kk)),
              "nt": pl.BlockSpec((tm, tk), lambda i, j, kk: (i, kk)),
              "tn": pl.BlockSpec((tk, tm), lambda i, j, kk: (kk, i))}[mode]
    b_blk = {"nn": (tk, tn), "nt": (tn, tk), "tn": (tk, tn)}[mode]
    b_idx = {"nn": lambda i, j, kk: (kk, j), "nt": lambda i, j, kk: (j, kk), "tn": lambda i, j, kk: (kk, j)}[mode]
    if b_layer is None:
        b_spec = pl.BlockSpec(b_blk, b_idx)
    else:
        b_spec = pl.BlockSpec((None,) + b_blk, lambda i, j, kk: (b_layer,) + b_idx(i, j, kk))

    tile = pl.BlockSpec((tm, tn), lambda i, j, kk: (i, j))
    if into is not None:
        buf, per_slot, row_off = into
        assert m == 4 * per_slot and per_slot % tm == 0 and row_off % tm == 0 and buf.shape[2] == n, (name, buf.shape)
        bps = per_slot // tm
        out_specs = pl.BlockSpec((None, tm, tn), lambda i, j, kk: (i // bps, row_off // tm + i % bps, j))
        out_shape = jax.ShapeDtypeStruct(buf.shape, buf.dtype)
        extra_in, extra_specs, aliases = [buf], [pl.BlockSpec(memory_space=pl.ANY)], {2: 0}
        finish = lambda acc, extra: [acc.astype(buf.dtype)]
    elif relu_pair:
        out_specs = (tile, tile)
        out_shape = (jax.ShapeDtypeStruct((m, n), BF16),) * 2
        extra_in, extra_specs, aliases = [], [], {}

        def finish(acc, extra):
            r = jnp.maximum(acc, 0.0)
            return [(r * r).astype(BF16), r.astype(BF16)]
    elif times2 is not None:
        out_specs = tile
        out_shape = jax.ShapeDtypeStruct((m, n), out_dtype)
        extra_in, extra_specs, aliases = [times2], [tile], {}
        finish = lambda acc, extra: [(acc * (2.0 * extra[...].astype(F32))).astype(out_dtype)]
    else:
        out_specs = tile
        out_shape = jax.ShapeDtypeStruct((m, n), out_dtype)
        extra_in, extra_specs, aliases = [], [], {}
        finish = lambda acc, extra: [acc.astype(out_dtype)]
    n_out = 2 if relu_pair else 1

    def body(*refs):
        a_ref, b_ref = refs[0], refs[1]
        extra = refs[2] if extra_in else None
        o_refs = refs[2 + len(extra_in):2 + len(extra_in) + n_out]

        def store(acc):
            for o_ref, val in zip(o_refs, finish(acc, extra)):
                o_ref[...] = val

        if nk == 1:
            store(_dot(a_ref[...], b_ref[...], mode))
            return
        acc_ref = refs[-1]
        kk = pl.program_id(2)

        @pl.when(kk == 0)
        def _():
            acc_ref[...] = jnp.zeros_like(acc_ref)

        acc_ref[...] += _dot(a_ref[...], b_ref[...], mode)

        @pl.when(kk == nk - 1)
        def _():
            store(acc_ref[...])

    return pl.pallas_call(
        body, name=name, grid=(m // tm, n // tn, nk),
        in_specs=[a_spec, b_spec] + extra_specs,
        out_specs=out_specs, out_shape=out_shape,
        scratch_shapes=[pltpu.VMEM((tm, tn), F32)] if nk > 1 else [],
        input_output_aliases=aliases,
        compiler_params=_cp(("parallel", "parallel", "arbitrary")),
    )(a, b, *extra_in)


ROWS = 256


def _prenorm(x, w, name):
    def body(x_ref, w_ref, o_ref):
        xv = x_ref[...]
        r = lax.rsqrt(jnp.mean(xv * xv, axis=-1, keepdims=True) + EPS)
        o_ref[...] = (xv * r * w_ref[...]).astype(BF16)

    return pl.pallas_call(
        body, name=name, grid=(T // ROWS,),
        in_specs=[pl.BlockSpec((ROWS, D), lambda i: (i, 0)), pl.BlockSpec((1, D), lambda i: (0, 0))],
        out_specs=pl.BlockSpec((ROWS, D), lambda i: (i, 0)),
        out_shape=jax.ShapeDtypeStruct((T, D), BF16),
        compiler_params=_cp(("parallel",)),
    )(x, w)


def _rms(z):
    return lax.rsqrt(jnp.mean(z * z, axis=-1, keepdims=True) + EPS)


def _rms_bwd(z, w, dy):
    r = _rms(z)
    wdy = dy * w
    dz = r * wdy - z * (r * r * r) * jnp.mean(z * wdy, axis=-1, keepdims=True)
    return dz, jnp.sum(dy * z * r, axis=0, keepdims=True)


_ROW = pl.BlockSpec((ROWS, D), lambda i: (i, 0))
_VEC = pl.BlockSpec((1, D), lambda i: (0, 0))


def _post_pre_fwd(x, z, w_post, w_pre, name):
    def body(x_ref, z_ref, wp_ref, wn_ref, x_out, h_out):
        zv = z_ref[...]
        xn = x_ref[...] + zv * _rms(zv) * wp_ref[...]
        x_out[...] = xn
        h_out[...] = (xn * _rms(xn) * wn_ref[...]).astype(BF16)

    return pl.pallas_call(
        body, name=name, grid=(T // ROWS,), in_specs=[_ROW, _ROW, _VEC, _VEC], out_specs=(_ROW, _ROW),
        out_shape=(jax.ShapeDtypeStruct((T, D), F32), jax.ShapeDtypeStruct((T, D), BF16)),
        compiler_params=_cp(("parallel",)),
    )(x, z, w_post, w_pre)


def _post_loss(x, z, w_post, tgt):
    def body(x_ref, z_ref, w_ref, t_ref, g_ref, l_ref, dz_ref, dw_ref):
        @pl.when(pl.program_id(0) == 0)
        def _():
            l_ref[...] = jnp.zeros_like(l_ref)
            dw_ref[...] = jnp.zeros_like(dw_ref)

        zv = z_ref[...]
        e = x_ref[...] + zv * _rms(zv) * w_ref[...] - t_ref[...]
        g = e * (1.0 / D)
        g_ref[...] = g
        l_ref[...] += jnp.sum(e * e) * (0.5 / D)
        dz, dw = _rms_bwd(zv, w_ref[...], g)
        dz_ref[...] = dz.astype(BF16)
        dw_ref[...] += dw

    return pl.pallas_call(
        body, name="postnorm_loss", grid=(T // ROWS,), in_specs=[_ROW, _ROW, _VEC, _ROW],
        out_specs=(_ROW, pl.BlockSpec((1, 128), lambda i: (0, 0)), _ROW, _VEC),
        out_shape=(jax.ShapeDtypeStruct((T, D), F32), jax.ShapeDtypeStruct((1, 128), F32),
                   jax.ShapeDtypeStruct((T, D), BF16), jax.ShapeDtypeStruct((1, D), F32)),
        compiler_params=_cp(("arbitrary",)),
    )(x, z, w_post, tgt)


def _pre_post_bwd(x, w_pre, dh, add, z, w_post, name):
    def body(x_ref, wn_ref, dh_ref, add_ref, z_ref, wp_ref, g_ref, dz_ref, dwn_ref, dwp_ref):
        @pl.when(pl.program_id(0) == 0)
        def _():
            dwn_ref[...] = jnp.zeros_like(dwn_ref)
            dwp_ref[...] = jnp.zeros_like(dwp_ref)

        dx, dwn = _rms_bwd(x_ref[...], wn_ref[...], dh_ref[...])
        g = dx + add_ref[...]
        g_ref[...] = g
        dz, dwp = _rms_bwd(z_ref[...], wp_ref[...], g)
        dz_ref[...] = dz.astype(BF16)
        dwn_ref[...] += dwn
        dwp_ref[...] += dwp

    return pl.pallas_call(
        body, name=name, grid=(T // ROWS,), in_specs=[_ROW, _VEC, _ROW, _ROW, _ROW, _VEC],
        out_specs=(_ROW, _ROW, _VEC, _VEC),
        out_shape=(jax.ShapeDtypeStruct((T, D), F32), jax.ShapeDtypeStruct((T, D), BF16),
                   jax.ShapeDtypeStruct((1, D), F32), jax.ShapeDtypeStruct((1, D), F32)),
        compiler_params=_cp(("arbitrary",)),
    )(x, w_pre, dh, add, z, w_post)


def _norm_bwd(z, w, dy, add, name):
    has_add = add is not None

    def body(*refs):
        if has_add:
            z_ref, w_ref, dy_ref, add_ref, dz_ref, dw_ref = refs
        else:
            z_ref, w_ref, dy_ref, dz_ref, dw_ref = refs
        i = pl.program_id(0)

        @pl.when(i == 0)
        def _():
            dw_ref[...] = jnp.zeros_like(dw_ref)

        zv = z_ref[...].astype(F32)
        dyv = dy_ref[...]
        r = lax.rsqrt(jnp.mean(zv * zv, axis=-1, keepdims=True) + EPS)
        wdy = dyv * w_ref[...]
        dz = r * wdy - zv * (r * r * r) * jnp.mean(zv * wdy, axis=-1, keepdims=True)
        if has_add:
            dz = dz + add_ref[...]
        dz_ref[...] = dz.astype(dz_ref.dtype)
        dw_ref[...] += jnp.sum(dyv * zv * r, axis=0, keepdims=True)

    row = pl.BlockSpec((ROWS, D), lambda i: (i, 0))
    vec = pl.BlockSpec((1, D), lambda i: (0, 0))
    ins = [z, w, dy] + ([add] if has_add else [])
    dz_dtype = F32 if has_add else BF16
    return pl.pallas_call(
        body, name=name, grid=(T // ROWS,),
        in_specs=[row, vec, row] + ([row] if has_add else []),
        out_specs=(row, vec),
        out_shape=(jax.ShapeDtypeStruct((T, D), dz_dtype), jax.ShapeDtypeStruct((1, D), F32)),
        compiler_params=_cp(("arbitrary",)),
    )(*ins)


def _adamw_math(w, g, m, v):
    c1 = 1.0 - ADAM_B1 ** ADAM_STEP
    c2 = 1.0 - ADAM_B2 ** ADAM_STEP
    mn = ADAM_B1 * m + (1.0 - ADAM_B1) * g
    vn = ADAM_B2 * v + (1.0 - ADAM_B2) * (g * g)
    return -ADAM_LR * ((mn / c1) / (jnp.sqrt(vn / c2) + ADAM_EPS) + ADAM_WD * w), mn, vn


def _adamw_from(w, m, v, sources, tr, name):
    layers, rows, cols = w.shape
    assert len(sources) == layers and rows % tr == 0, (name, w.shape)
    g_specs = []
    for buf, row0, transposed in sources:
        if transposed:
            assert row0 % cols == 0 and buf.shape[1] == rows, (name, row0)
            g_specs.append(pl.BlockSpec((cols, tr), lambda l, i, b=row0 // cols: (b, i)))
        else:
            assert row0 % tr == 0 and buf.shape[1] == cols, (name, row0)
            g_specs.append(pl.BlockSpec((tr, cols), lambda l, i, b=row0 // tr: (b + i, 0)))

    def body(*refs):
        w_ref, m_ref, v_ref = refs[:3]
        g_refs = refs[3:3 + layers]
        g_out, d_ref, mo_ref, vo_ref = refs[3 + layers:]
        gs = [r[...].T if src[2] else r[...] for r, src in zip(g_refs, sources)]
        g = gs[0] if layers == 1 else jnp.where(pl.program_id(0) == 0, gs[0], gs[1])
        g_out[...] = g
        d_ref[...], mo_ref[...], vo_ref[...] = _adamw_math(w_ref[...], g, m_ref[...], v_ref[...])

    blk = pl.BlockSpec((None, tr, cols), lambda l, i: (l, i, 0))
    sds = jax.ShapeDtypeStruct(w.shape, F32)
    return pl.pallas_call(body, name=name, grid=(layers, rows // tr), in_specs=[blk] * 3 + g_specs,
                          out_specs=(blk,) * 4, out_shape=(sds,) * 4,
                          compiler_params=_cp(("parallel", "parallel")))(w, m, v, *[s[0] for s in sources])


def _adamw(w, g, m, v, name):
    lead = w.shape[:-2]
    assert len(lead) <= 1 and g.shape == w.shape, (name, w.shape, g.shape)
    rows, cols = w.shape[-2:]
    if rows <= 512:
        tr, tc = rows, cols
    elif rows % 256 == 0:
        tr, tc = 256, cols
    else:
        tr, tc = rows, 256
    assert rows % tr == 0 and cols % tc == 0, (name, w.shape)
    c1 = 1.0 - ADAM_B1 ** ADAM_STEP
    c2 = 1.0 - ADAM_B2 ** ADAM_STEP

    def body(w_ref, g_ref, m_ref, v_ref, d_ref, mo_ref, vo_ref):
        gv = g_ref[...]
        mn = ADAM_B1 * m_ref[...] + (1.0 - ADAM_B1) * gv
        vn = ADAM_B2 * v_ref[...] + (1.0 - ADAM_B2) * (gv * gv)
        m_hat = mn / c1
        v_hat = vn / c2
        d_ref[...] = -ADAM_LR * (m_hat / (jnp.sqrt(v_hat) + ADAM_EPS) + ADAM_WD * w_ref[...])
        mo_ref[...] = mn
        vo_ref[...] = vn

    if lead:
        grid = (lead[0], rows // tr, cols // tc)
        blk = pl.BlockSpec((None, tr, tc), lambda l, i, j: (l, i, j))
    else:
        grid = (rows // tr, cols // tc)
        blk = pl.BlockSpec((tr, tc), lambda i, j: (i, j))
    sds = jax.ShapeDtypeStruct(w.shape, F32)
    return pl.pallas_call(body, name=name, grid=grid, in_specs=[blk] * 4, out_specs=(blk,) * 3,
                          out_shape=(sds,) * 3, compiler_params=_cp(("parallel",) * len(grid)))(w, g, m, v)


def _gla_consts():
    ltri = (_iota((CHUNK, CHUNK), 0) >= _iota((CHUNK, CHUNK), 1)).astype(F32)
    ones_c = jnp.ones((CHUNK, 128), F32)
    mask = (_iota((256, 512), 0) // 64 == _iota((256, 512), 1) // 128).astype(F32)
    return ltri, ones_c, mask


def _gla_chunk(consts, q, k, v, r, aux, s_prev, wa, ba, nw):
    ltri, ones_c, mask = consts
    la = _log_sigmoid(bdot(aux, wa, "nn") + ba) * (1.0 / 16.0)
    cum = hdot(ltri, la, "nn")
    total = jnp.sum(la, axis=0, keepdims=True)
    k_dec = k * jnp.exp(total - cum)
    inc = bdot(k_dec, v, "tn") * mask
    dec = jnp.exp(hdot(la, ones_c, "tn"))
    dec = jnp.concatenate([dec, dec, dec, dec], axis=1)
    s_new = dec * s_prev + inc
    o = bdot(q * GLA_SCALE, s_new, "nn")
    parts = []
    for h in range(4):
        oh = o[:, h * 128:(h + 1) * 128]
        parts.append(oh * lax.rsqrt(jnp.mean(oh * oh, axis=-1, keepdims=True) + EPS))
    on = jnp.concatenate(parts, axis=1)
    return s_new, on * nw * (r * _sigmoid(r))


GLA_PER_STEP = 8
GLA_ROWS = GLA_PER_STEP * CHUNK
GLA_STEPS = NCHUNK // GLA_PER_STEP


def _gla_specs(cmap):
    return [pl.BlockSpec((GLA_ROWS, 256), lambda c: (cmap(c), 0)),
            pl.BlockSpec((GLA_ROWS, 256), lambda c: (cmap(c), 1)),
            pl.BlockSpec((GLA_ROWS, 512), lambda c: (cmap(c), 1)),
            pl.BlockSpec((GLA_ROWS, 512), lambda c: (cmap(c), 2)),
            pl.BlockSpec((GLA_ROWS, 128), lambda c: (cmap(c), AUX_BLK))]


def _gla_fwd(proj, wa, ba, nw):
    def body(q_ref, k_ref, v_ref, r_ref, aux_ref, wa_ref, ba_ref, nw_ref, o_ref, sp_ref, s_ref):
        @pl.when(pl.program_id(0) == 0)
        def _():
            s_ref[...] = jnp.zeros_like(s_ref)

        s = s_ref[...]
        consts = _gla_consts()
        outs, states = [], []
        for i in range(GLA_PER_STEP):
            rows = slice(i * CHUNK, (i + 1) * CHUNK)
            states.append(s)
            s, out = _gla_chunk(consts, q_ref[rows, :], k_ref[rows, :], v_ref[rows, :], r_ref[rows, :], aux_ref[rows, :],
                                s, wa_ref[...], ba_ref[...], nw_ref[...])
            outs.append(out)
        s_ref[...] = s
        for i in range(GLA_PER_STEP):
            o_ref[i * CHUNK:(i + 1) * CHUNK, :] = outs[i]
            sp_ref[i] = states[i]

    full = lambda shape: pl.BlockSpec(shape, lambda c: (0,) * len(shape))
    return pl.pallas_call(
        body, name="gla_fwd", grid=(GLA_STEPS,),
        in_specs=_gla_specs(lambda c: c) + [full((128, 256)), full((1, 256)), full((1, 512))],
        out_specs=(pl.BlockSpec((GLA_ROWS, 512), lambda c: (c, 0)),
                   pl.BlockSpec((GLA_PER_STEP, 256, 512), lambda c: (c, 0, 0))),
        out_shape=(jax.ShapeDtypeStruct((T, D), F32), jax.ShapeDtypeStruct((NCHUNK, 256, 512), F32)),
        scratch_shapes=[pltpu.VMEM((256, 512), F32)],
        compiler_params=_cp(("arbitrary",)),
    )(proj, proj, proj, proj, proj, wa, ba, nw)


def _gla_bwd(proj, s_prev_all, wa, ba, nw, dcat):
    rev = lambda c: GLA_STEPS - 1 - c

    def body(q_ref, k_ref, v_ref, r_ref, aux_ref, sp_ref, wa_ref, ba_ref, nw_ref, do_ref,
             dq_ref, dk_ref, dv_ref, dr_ref, daux_ref, dwa_ref, dba_ref, dnw_ref, ds_ref):
        @pl.when(pl.program_id(0) == 0)
        def _():
            ds_ref[...] = jnp.zeros_like(ds_ref)
            dwa_ref[...] = jnp.zeros_like(dwa_ref)
            dba_ref[...] = jnp.zeros_like(dba_ref)
            dnw_ref[...] = jnp.zeros_like(dnw_ref)

        fn = functools.partial(_gla_chunk, _gla_consts())
        ds = ds_ref[...]
        dwa, dba, dnw = dwa_ref[...], dba_ref[...], dnw_ref[...]
        grads = {}
        for i in reversed(range(GLA_PER_STEP)):
            rows = slice(i * CHUNK, (i + 1) * CHUNK)
            _, vjp = jax.vjp(fn, q_ref[rows, :], k_ref[rows, :], v_ref[rows, :], r_ref[rows, :], aux_ref[rows, :],
                             sp_ref[i], wa_ref[...], ba_ref[...], nw_ref[...])
            *grads[i], ds, dwa_i, dba_i, dnw_i = vjp((ds, do_ref[rows, :]))
            dwa, dba, dnw = dwa + dwa_i, dba + dba_i, dnw + dnw_i
        ds_ref[...] = ds
        dwa_ref[...] = dwa
        dba_ref[...] = dba
        dnw_ref[...] = dnw
        for i in range(GLA_PER_STEP):
            rows = slice(i * CHUNK, (i + 1) * CHUNK)
            for ref, g in zip((dq_ref, dk_ref, dv_ref, dr_ref, daux_ref), grads[i]):
                ref[rows, :] = g

    full = lambda shape: pl.BlockSpec(shape, lambda c: (0,) * len(shape))
    blk = lambda w: pl.BlockSpec((GLA_ROWS, w), lambda c: (rev(c), 0))
    sds = lambda *s: jax.ShapeDtypeStruct(s, F32)
    return pl.pallas_call(
        body, name="gla_bwd", grid=(GLA_STEPS,),
        in_specs=_gla_specs(rev) + [pl.BlockSpec((GLA_PER_STEP, 256, 512), lambda c: (rev(c), 0, 0)),
                                    full((128, 256)), full((1, 256)), full((1, 512)), blk(512)],
        out_specs=(blk(256), blk(256), blk(512), blk(512), blk(128), full((128, 256)), full((1, 256)), full((1, 512))),
        out_shape=(sds(T, 256), sds(T, 256), sds(T, 512), sds(T, 512), sds(T, 128),
                   sds(128, 256), sds(1, 256), sds(1, 512)),
        scratch_shapes=[pltpu.VMEM((256, 512), F32)],
        compiler_params=_cp(("arbitrary",)),
    )(proj, proj, proj, proj, proj, s_prev_all, wa, ba, nw, dcat)


GATE_ROWS = 128


def _fox_gate_block(ltri, aux, bpad, carry):
    lf = _log_sigmoid(aux + bpad)
    cum = hdot(ltri, lf, "nn") + carry
    return cum, carry + jnp.sum(lf, axis=0, keepdims=True)


def _gate_ltri():
    return (_iota((GATE_ROWS, GATE_ROWS), 0) >= _iota((GATE_ROWS, GATE_ROWS), 1)).astype(F32)


def _fox_gate_fwd(proj, bpad):
    def body(aux_ref, b_ref, cum_ref, carry_ref):
        i = pl.program_id(0)

        @pl.when(i == 0)
        def _():
            carry_ref[...] = jnp.zeros_like(carry_ref)

        cum, carry = _fox_gate_block(_gate_ltri(), aux_ref[...], b_ref[...], carry_ref[...])
        cum_ref[...] = cum
        carry_ref[...] = carry

    return pl.pallas_call(
        body, name="fox_gate_fwd", grid=(T // GATE_ROWS,),
        in_specs=[pl.BlockSpec((GATE_ROWS, 128), lambda i: (i, AUX_BLK)), pl.BlockSpec((1, 128), lambda i: (0, 0))],
        out_specs=pl.BlockSpec((GATE_ROWS, 128), lambda i: (i, 0)),
        out_shape=jax.ShapeDtypeStruct((T, 128), F32),
        scratch_shapes=[pltpu.VMEM((1, 128), F32)],
        compiler_params=_cp(("arbitrary",)),
    )(proj, bpad)


def _fox_gate_bwd(proj, bpad, dccol_t, daux_gla):
    nb = T // GATE_ROWS
    rev = lambda i: nb - 1 - i

    def body(aux_ref, b_ref, dc_ref, dg_ref, daux_ref, db_ref, dcarry_ref):
        i = pl.program_id(0)

        @pl.when(i == 0)
        def _():
            dcarry_ref[...] = jnp.zeros_like(dcarry_ref)
            db_ref[...] = jnp.zeros_like(db_ref)

        dcum = dc_ref[...]
        fn = functools.partial(_fox_gate_block, _gate_ltri())
        _, vjp = jax.vjp(fn, aux_ref[...], b_ref[...], jnp.zeros((1, 128), F32))
        daux, db, dcarry = vjp((dcum, dcarry_ref[...]))
        daux_ref[...] = daux + dg_ref[...]
        db_ref[...] += db
        dcarry_ref[...] = dcarry

    blk = pl.BlockSpec((GATE_ROWS, 128), lambda i: (rev(i), 0))
    vec = pl.BlockSpec((1, 128), lambda i: (0, 0))
    return pl.pallas_call(
        body, name="fox_gate_bwd", grid=(nb,),
        in_specs=[pl.BlockSpec((GATE_ROWS, 128), lambda i: (rev(i), AUX_BLK)), vec, blk, blk],
        out_specs=(blk, vec),
        out_shape=(jax.ShapeDtypeStruct((T, 128), F32), jax.ShapeDtypeStruct((1, 128), F32)),
        scratch_shapes=[pltpu.VMEM((1, 128), F32)],
        compiler_params=_cp(("arbitrary",)),
    )(proj, bpad, dccol_t, daux_gla)


FOX_Q = 256


FOX_QB = T // FOX_Q


@jax.custom_vjp
def _attend(s, v):
    return _attend_fwd(s, v)[0]


def _attend_fwd(s, v):
    e = jnp.exp(s - jnp.max(s, axis=-1, keepdims=True))
    r = 1.0 / jnp.sum(e, axis=-1, keepdims=True)
    return _dot(e, v, "nn") * r, (e, r, v)


def _attend_bwd(res, do):
    e, r, v = res
    do_r = do * r
    dpr = _dot(do_r, v, "nt")
    ds = e * (dpr - r * jnp.sum(e * dpr, axis=-1, keepdims=True))
    return ds, _dot(e, do_r, "tn").astype(v.dtype)


_attend.defvjp(_attend_fwd, _attend_bwd)


def _fox_block(hp, q, k, v, ccol):
    kl = k.shape[0]
    lane = _iota((FOX_Q, 128), 1)
    tri = jnp.bitwise_and(_iota((2 * FOX_Q, FOX_Q), 0), FOX_Q - 1) >= _iota((2 * FOX_Q, FOX_Q), 1)
    sub = _iota((8, kl), 0)
    qs = q * ATT_SCALE
    q2 = jnp.concatenate([jnp.where(lane < 64, qs, 0.0), jnp.where(lane >= 64, qs, 0.0)], axis=0)
    s = bdot(q2, k, "nt")
    cs = [jnp.sum(jnp.where(sub == 2 * hp + e, ccol, 0.0), axis=0, keepdims=True) for e in range(2)]
    s = jnp.concatenate([s[:FOX_Q] - cs[0], s[FOX_Q:] - cs[1]], axis=0)
    diag = jnp.where(tri, s[:, kl - FOX_Q:], NEG)
    s = diag if kl == FOX_Q else jnp.concatenate([s[:, :kl - FOX_Q], diag], axis=1)
    o2 = _attend(s, v)
    return jnp.where(lane < 64, o2[:FOX_Q], o2[FOX_Q:])


def _fox_in_specs():
    return [pl.BlockSpec((FOX_Q, 128), lambda hp, qb: (qb, 12 + hp)),
            pl.BlockSpec((T, 128), lambda hp, qb: (0, 16 + hp)),
            pl.BlockSpec((T, 128), lambda hp, qb: (0, 20 + hp)),
            pl.BlockSpec((8, T), lambda hp, qb: (0, 0))]


def _fox_fwd(proj, cum_c, cat):
    def body(q_ref, k_ref, v_ref, cc_ref, cat_ref, o_ref):
        qb = pl.program_id(1)
        for g in range(FOX_QB):
            kl = FOX_Q * (g + 1)

            @pl.when(qb == g)
            def _(kl=kl):
                o_ref[...] = _fox_block(pl.program_id(0), q_ref[...], k_ref[0:kl, :], v_ref[0:kl, :], cc_ref[:, 0:kl])

    return pl.pallas_call(
        body, name="fox_fwd", grid=(4, FOX_QB), in_specs=_fox_in_specs() + [pl.BlockSpec(memory_space=pl.ANY)],
        out_specs=pl.BlockSpec((FOX_Q, 128), lambda hp, qb: (qb, 4 + hp)),
        out_shape=jax.ShapeDtypeStruct((T, D), F32), input_output_aliases={4: 0},
        compiler_params=_cp(("parallel", "parallel")),
    )(proj, proj, proj, cum_c, cat)


def _fox_bwd(proj, cum_c, dcat):
    def body(q_ref, k_ref, v_ref, cc_ref, do_ref, dq_ref, dk_ref, dv_ref, dcc_ref):
        qb = pl.program_id(1)

        @pl.when(qb == 0)
        def _():
            dk_ref[...] = jnp.zeros_like(dk_ref)
            dv_ref[...] = jnp.zeros_like(dv_ref)
            dcc_ref[...] = jnp.zeros_like(dcc_ref)

        fn = functools.partial(_fox_block, pl.program_id(0))
        for g in range(FOX_QB):
            kl = FOX_Q * (g + 1)

            @pl.when(qb == g)
            def _(kl=kl):
                _, vjp = jax.vjp(fn, q_ref[...], k_ref[0:kl, :], v_ref[0:kl, :], cc_ref[:, 0:kl])
                dq, dk, dv, dcc = vjp(do_ref[...])
                dq_ref[...] = dq
                dk_ref[0:kl, :] += dk
                dv_ref[0:kl, :] += dv
                dcc_ref[:, 0:kl] += dcc

    sds = lambda *s: jax.ShapeDtypeStruct(s, F32)
    return pl.pallas_call(
        body, name="fox_bwd", grid=(4, FOX_QB),
        in_specs=_fox_in_specs() + [pl.BlockSpec((FOX_Q, 128), lambda hp, qb: (qb, 4 + hp))],
        out_specs=(pl.BlockSpec((FOX_Q, 128), lambda hp, qb: (qb, hp)),
                   pl.BlockSpec((T, 128), lambda hp, qb: (0, hp)),
                   pl.BlockSpec((T, 128), lambda hp, qb: (0, hp)),
                   pl.BlockSpec((None, 8, T), lambda hp, qb: (hp, 0, 0))),
        out_shape=(sds(T, 512), sds(T, 512), sds(T, 512), sds(4, 8, T)),
        compiler_params=_cp(("parallel", "arbitrary")),
    )(proj, proj, proj, cum_c, dcat)


BIAS_W = 640


def _rel_onehot():
    j = _iota((REL_PAD, BIAS_W), 1)
    rel = jnp.clip(CA_PAD + CHUNK - 1 - j, -128, 128) + 128
    return (_iota((REL_PAD, BIAS_W), 0) == rel).astype(F32)


def _bias_build(rbp):
    def body(rb_ref, o_ref):
        f = _hdot_raw(rb_ref[...], _rel_onehot(), "nn")
        for q in range(CHUNK):
            o_ref[q] = pltpu.roll(f, (BIAS_W - (CHUNK - 1 - q)) % BIAS_W, 1)[:, :CA_BAND]

    return pl.pallas_call(body, name="ca_bias_build", out_shape=jax.ShapeDtypeStruct((CHUNK, 8, CA_BAND), F32))(rbp)


def _bias_grad(dbias_q):
    def body(db_ref, o_ref):
        acc = jnp.zeros((8, BIAS_W), F32)
        for q in range(CHUNK):
            acc = acc + pltpu.roll(db_ref[q], CHUNK - 1 - q, 1)
        o_ref[...] = _hdot_raw(acc, _rel_onehot(), "nt")

    return pl.pallas_call(body, name="ca_bias_grad", out_shape=jax.ShapeDtypeStruct((8, REL_PAD), F32))(dbias_q)


def _ca_block(c, masked, q, kb, vb, bias2):
    lane = _iota((CHUNK, 128), 1)
    qs = q * ATT_SCALE
    q2 = jnp.concatenate([jnp.where(lane < 64, qs, 0.0), jnp.where(lane >= 64, qs, 0.0)], axis=0)
    s = bdot(q2, kb, "nt") + bias2.reshape(2 * CHUNK, CA_BAND)
    if masked:
        s = jnp.where((c * CHUNK - CA_PAD + _iota((2 * CHUNK, CA_BAND), 1)) >= 0, s, NEG)
    o2 = _attend(s, vb)
    return jnp.where(lane < 64, o2[:CHUNK], o2[CHUNK:])


CA_PER_STEP = 8
CA_ROWS = CA_PER_STEP * CHUNK
CA_MASKED_STEPS = -(-CA_PAD // CA_ROWS)


def _ca_fwd(proj, kvpad, bias):
    def body(q_ref, k_ref, v_ref, b_ref, o_ref):
        def run(masked):
            outs = []
            for i in range(CA_PER_STEP):
                c = pl.program_id(1) * CA_PER_STEP + i
                band = pl.ds(pl.multiple_of(c * CHUNK, CHUNK), CA_BAND)
                rows = slice(i * CHUNK, (i + 1) * CHUNK)
                outs.append(_ca_block(c, masked, q_ref[rows, :], k_ref[band, :], v_ref[band, :], b_ref[...]))
            for i in range(CA_PER_STEP):
                o_ref[i * CHUNK:(i + 1) * CHUNK, :] = outs[i]

        pl.when(pl.program_id(1) < CA_MASKED_STEPS)(lambda: run(True))
        pl.when(pl.program_id(1) >= CA_MASKED_STEPS)(lambda: run(False))

    return pl.pallas_call(
        body, name="ca_fwd", grid=(4, NCHUNK // CA_PER_STEP),
        in_specs=[pl.BlockSpec((CA_ROWS, 128), lambda hp, c: (c, hp)),
                  pl.BlockSpec((T + CA_PAD, 128), lambda hp, c: (0, hp)),
                  pl.BlockSpec((T + CA_PAD, 128), lambda hp, c: (0, 4 + hp)),
                  pl.BlockSpec((2, CHUNK, CA_BAND), lambda hp, c: (hp, 0, 0))],
        out_specs=pl.BlockSpec((CA_ROWS, 128), lambda hp, c: (c, hp)),
        out_shape=jax.ShapeDtypeStruct((T, D), F32),
        compiler_params=_cp(("parallel", "parallel")),
    )(proj, kvpad, kvpad, bias)


def _ca_bwd(proj, kvpad, bias, dcat):
    def body(q_ref, k_ref, v_ref, b_ref, do_ref, dq_ref, dk_ref, dv_ref, db_ref):
        c = pl.program_id(1)

        @pl.when(c == 0)
        def _():
            dk_ref[...] = jnp.zeros_like(dk_ref)
            dv_ref[...] = jnp.zeros_like(dv_ref)
            db_ref[...] = jnp.zeros_like(db_ref)

        def run(masked):
            grads, bands = [], []
            for i in range(CA_PER_STEP):
                ci = c * CA_PER_STEP + i
                band = pl.ds(pl.multiple_of(ci * CHUNK, CHUNK), CA_BAND)
                rows = slice(i * CHUNK, (i + 1) * CHUNK)
                fn = functools.partial(_ca_block, ci, masked)
                _, vjp = jax.vjp(fn, q_ref[rows, :], k_ref[band, :], v_ref[band, :], b_ref[...])
                grads.append(vjp(do_ref[rows, :]))
                bands.append(band)
            for i, (dq, _, _, _) in enumerate(grads):
                dq_ref[i * CHUNK:(i + 1) * CHUNK, :] = dq
            for band, (_, dkb, dvb, _) in zip(bands, grads):
                dk_ref[band, :] += dkb
                dv_ref[band, :] += dvb
            db_ref[...] += functools.reduce(lambda a, b: a + b, [g[3] for g in grads])

        pl.when(c < CA_MASKED_STEPS)(lambda: run(True))
        pl.when(c >= CA_MASKED_STEPS)(lambda: run(False))

    sds = lambda *s: jax.ShapeDtypeStruct(s, F32)
    padded = lambda: pl.BlockSpec((T + CA_PAD, 128), lambda hp, c: (0, hp))
    return pl.pallas_call(
        body, name="ca_bwd", grid=(4, NCHUNK // CA_PER_STEP),
        in_specs=[pl.BlockSpec((CA_ROWS, 128), lambda hp, c: (c, hp)),
                  pl.BlockSpec((T + CA_PAD, 128), lambda hp, c: (0, hp)),
                  pl.BlockSpec((T + CA_PAD, 128), lambda hp, c: (0, 4 + hp)),
                  pl.BlockSpec((2, CHUNK, CA_BAND), lambda hp, c: (hp, 0, 0)),
                  pl.BlockSpec((CA_ROWS, 128), lambda hp, c: (c, hp))],
        out_specs=(pl.BlockSpec((CA_ROWS, 128), lambda hp, c: (c, hp)), padded(), padded(),
                   pl.BlockSpec((2, CHUNK, CA_BAND), lambda hp, c: (hp, 0, 0))),
        out_shape=(sds(T, 512), sds(T + CA_PAD, 512), sds(T + CA_PAD, 512), sds(8, CHUNK, CA_BAND)),
        compiler_params=_cp(("parallel", "arbitrary")),
    )(proj, kvpad, kvpad, bias, dcat)


def _lru_pre(xs, cw, cb, wa, ba, wx, bx, lam):
    xc = cb + xs[0] * cw[0:1, :] + xs[1] * cw[1:2, :] + xs[2] * cw[2:3, :] + xs[3] * cw[3:4, :]
    ra = _sigmoid(bdot(xc, wa, "nn") + ba)
    ii = _sigmoid(bdot(xc, wx, "nn") + bx)
    la = 8.0 * ra * _log_sigmoid(lam)
    return jnp.exp(la), jnp.sqrt(-_expm1(2.0 * la)) * (ii * xc)


def _lru_pre_specs():
    full = lambda shape: pl.BlockSpec(shape, lambda i: (0,) * len(shape))
    return [pl.BlockSpec((4, ROWS, 512), lambda i: (0, i, 0)), full((4, 512)), full((1, 512)),
            full((512, 512)), full((1, 512)), full((512, 512)), full((1, 512)), full((1, 512))]


def _lru_pre_fwd(xs, cw, cb, wa, ba, wx, bx, lam):
    def body(xs_ref, cw_ref, cb_ref, wa_ref, ba_ref, wx_ref, bx_ref, lam_ref, a_ref, b_ref):
        a, b = _lru_pre(xs_ref[...], cw_ref[...], cb_ref[...], wa_ref[...], ba_ref[...], wx_ref[...], bx_ref[...],
                        lam_ref[...])
        a_ref[...] = a
        b_ref[...] = b

    row = pl.BlockSpec((ROWS, 512), lambda i: (i, 0))
    sds = jax.ShapeDtypeStruct((T, 512), F32)
    return pl.pallas_call(body, name="lru_pre_fwd", grid=(T // ROWS,), in_specs=_lru_pre_specs(),
                          out_specs=(row, row), out_shape=(sds, sds), compiler_params=_cp(("parallel",)),
                          )(xs, cw, cb, wa, ba, wx, bx, lam)


def _lru_pre_bwd(xs, cw, cb, wa, ba, wx, bx, lam, da, db):
    def body(xs_ref, cw_ref, cb_ref, wa_ref, ba_ref, wx_ref, bx_ref, lam_ref, da_ref, db_ref,
             dxs_ref, dcw_ref, dcb_ref, dwa_ref, dba_ref, dwx_ref, dbx_ref, dlam_ref):
        acc = (dcw_ref, dcb_ref, dwa_ref, dba_ref, dwx_ref, dbx_ref, dlam_ref)

        @pl.when(pl.program_id(0) == 0)
        def _():
            for r in acc:
                r[...] = jnp.zeros_like(r)

        _, vjp = jax.vjp(_lru_pre, xs_ref[...], cw_ref[...], cb_ref[...], wa_ref[...], ba_ref[...], wx_ref[...],
                         bx_ref[...], lam_ref[...])
        grads = vjp((da_ref[...], db_ref[...]))
        dxs_ref[...] = grads[0]
        for r, g in zip(acc, grads[1:]):
            r[...] += g

    row = pl.BlockSpec((ROWS, 512), lambda i: (i, 0))
    specs = _lru_pre_specs()
    sds = lambda *s: jax.ShapeDtypeStruct(s, F32)
    return pl.pallas_call(
        body, name="lru_pre_bwd", grid=(T // ROWS,), in_specs=specs + [row, row], out_specs=tuple(specs),
        out_shape=(sds(4, T, 512), sds(4, 512), sds(1, 512), sds(512, 512), sds(1, 512), sds(512, 512), sds(1, 512),
                   sds(1, 512)),
        compiler_params=_cp(("arbitrary",)),
    )(xs, cw, cb, wa, ba, wx, bx, lam, da, db)


SCAN_ROWS = 8


def _scan8(a, b, towards_later):
    row = _iota((SCAN_ROWS, 512), 0)
    for s in (1, 2, 4):
        if towards_later:
            keep, shift = row >= s, s
        else:
            keep, shift = row < SCAN_ROWS - s, SCAN_ROWS - s
        a_s = jnp.where(keep, pltpu.roll(a, shift, 0), 1.0)
        b_s = jnp.where(keep, pltpu.roll(b, shift, 0), 0.0)
        b = a * b_s + b
        a = a * a_s
    return a, b


def _lru_scan_fwd(a, b):
    def body(a_ref, b_ref, h_ref):
        def step(i, carry):
            rows = pl.ds(pl.multiple_of(i * SCAN_ROWS, SCAN_ROWS), SCAN_ROWS)
            a8, b8 = _scan8(a_ref[rows, :], b_ref[rows, :], True)
            h = a8 * carry + b8
            h_ref[rows, :] = h
            return jnp.broadcast_to(h[SCAN_ROWS - 1:, :], (SCAN_ROWS, 512))

        lax.fori_loop(0, T // SCAN_ROWS, step, jnp.zeros((SCAN_ROWS, 512), F32), unroll=2)

    return pl.pallas_call(body, name="lru_scan_fwd", out_shape=jax.ShapeDtypeStruct((T, 512), F32),
                          compiler_params=pltpu.CompilerParams(vmem_limit_bytes=VMEM_LIMIT))(a, b)


def _lru_scan_bwd(a_next, h_prev, dh):
    def body(a_ref, h_ref, dh_ref, da_ref, db_ref):
        def step(i, carry):
            start = T - SCAN_ROWS * (i + 1)
            rows = pl.ds(pl.multiple_of(start, SCAN_ROWS), SCAN_ROWS)
            a8, b8 = _scan8(a_ref[rows, :], dh_ref[rows, :], False)
            g = a8 * carry + b8
            db_ref[rows, :] = g
            da_ref[rows, :] = g * h_ref[rows, :]
            return jnp.broadcast_to(g[:1, :], (SCAN_ROWS, 512))

        lax.fori_loop(0, T // SCAN_ROWS, step, jnp.zeros((SCAN_ROWS, 512), F32), unroll=2)

    sds = jax.ShapeDtypeStruct((T, 512), F32)
    return pl.pallas_call(body, name="lru_scan_bwd", out_shape=(sds, sds),
                          compiler_params=pltpu.CompilerParams(vmem_limit_bytes=VMEM_LIMIT))(a_next, h_prev, dh)


def _lru_post(h, gate):
    return h * _gelu_tanh(gate)


def _lru_post_fwd(h, proj, cat):
    def body(h_ref, g_ref, cat_ref, o_ref):
        o_ref[...] = _lru_post(h_ref[...], g_ref[...])

    row = pl.BlockSpec((ROWS, 512), lambda i: (i, 0))
    return pl.pallas_call(body, name="lru_post_fwd", grid=(T // ROWS,),
                          in_specs=[row, pl.BlockSpec((ROWS, 512), lambda i: (i, 3)), pl.BlockSpec(memory_space=pl.ANY)],
                          out_specs=pl.BlockSpec((ROWS, 512), lambda i: (i, 1)),
                          out_shape=jax.ShapeDtypeStruct((T, D), F32), input_output_aliases={2: 0},
                          compiler_params=_cp(("parallel",)))(h, proj, cat)


def _lru_post_bwd(h, proj, dcat):
    def body(h_ref, g_ref, do_ref, dh_ref, dg_ref):
        _, vjp = jax.vjp(_lru_post, h_ref[...], g_ref[...])
        dh, dg = vjp(do_ref[...])
        dh_ref[...] = dh
        dg_ref[...] = dg

    row = pl.BlockSpec((ROWS, 512), lambda i: (i, 0))
    sds = jax.ShapeDtypeStruct((T, 512), F32)
    return pl.pallas_call(body, name="lru_post_bwd", grid=(T // ROWS,),
                          in_specs=[row, pl.BlockSpec((ROWS, 512), lambda i: (i, 3)),
                                    pl.BlockSpec((ROWS, 512), lambda i: (i, 1))],
                          out_specs=(row, row), out_shape=(sds, sds), compiler_params=_cp(("parallel",)))(h, proj, dcat)


def _conv_dx(dxs_shift):
    def body(d_ref, o_ref):
        o_ref[...] = d_ref[0] + d_ref[1] + d_ref[2] + d_ref[3]

    row = pl.BlockSpec((ROWS, 512), lambda i: (i, 0))
    return pl.pallas_call(body, name="lru_conv_dx", grid=(T // ROWS,),
                          in_specs=[pl.BlockSpec((4, ROWS, 512), lambda i: (0, i, 0))], out_specs=row,
                          out_shape=jax.ShapeDtypeStruct((T, 512), F32), compiler_params=_cp(("parallel",)))(dxs_shift)


def _position():
    return lax.axis_index("x"), lax.axis_index("y"), lax.axis_index("c")


def _other_chips(x, y):
    return [(1 - x, y), (x, 1 - y), (1 - x, 1 - y)]


def _al(v, n):
    return v * n if isinstance(v, int) else pl.multiple_of(v * n, n)


_AG_ITEMS = [
    ((4, 32, 128), lambda o, s, h: o.at[s, pl.ds(_al(h, 16), 16), :], lambda r, h: r.at[pl.ds(_al(h, 16), 16), :]),
    ((4, 774, 1024), lambda o, s, h: o.at[s, :, pl.ds(_al(h, 512), 512)], lambda r, h: r.at[:, pl.ds(_al(h, 512), 512)]),
    ((1024, 1024), lambda o, s, h: o.at[pl.ds(_al(2 * s + h, 128), 128), :], lambda r, h: r.at[pl.ds(_al(h, 128), 128), :]),
    ((2, 1024, 4096), lambda o, s, h: o.at[h, :, pl.ds(_al(s, 1024), 1024)], lambda r, h: r.at[h]),
    ((2, 4096, 1024), lambda o, s, h: o.at[h, pl.ds(_al(s, 1024), 1024), :], lambda r, h: r.at[h]),
    ((1024, 2560), lambda o, s, h: o.at[pl.ds(_al(h, 512), 512), pl.ds(_al(s, 640), 640)],
     lambda r, h: r.at[pl.ds(_al(h, 512), 512), :]),
    ((1024, 1024), lambda o, s, h: o.at[pl.ds(_al(2 * s + h, 128), 128), :], lambda r, h: r.at[pl.ds(_al(h, 128), 128), :]),
]


_AG_GROUPS = [(0, 1, 2), (3, 4), (5, 6)]

_HBM = pl.BlockSpec(memory_space=pltpu.HBM)
_SEM = pl.BlockSpec(memory_space=pltpu.SEMAPHORE)
_SPLIT = dict(has_side_effects=pltpu.SideEffectType.DATAFLOW_SIDE_EFFECTING)


def _hbm(a):
    return pltpu.with_memory_space_constraint(a, pltpu.HBM)


def _ag_ici_copy(i, j, chip, c, slot, src_ref, land_ref, send_sems, recv_sems, k):
    _, dst, half = _AG_ITEMS[i]
    return pltpu.make_async_remote_copy(src_ref=half(src_ref, c), dst_ref=dst(land_ref, slot, c), send_sem=send_sems.at[k],
                                        recv_sem=recv_sems.at[k], device_id=(*chip, c), device_id_type=MESH)


def _ag_start(shards):
    n = len(_AG_ITEMS)
    ng = len(_AG_GROUPS)
    lands = [lax.empty(shape, s.dtype) for (shape, _, _), s in zip(_AG_ITEMS, shards)]

    def body(*refs):
        srcs, land_refs = refs[:n], refs[n:2 * n]
        sems = refs[2 * n:2 * n + 2 * ng]
        token = refs[-1]
        x, y, c = _position()
        me = 2 * x + y
        for g, items in enumerate(_AG_GROUPS):
            for t, i in enumerate(items):
                for j, chip in enumerate(_other_chips(x, y)):
                    _ag_ici_copy(i, j, chip, c, me, srcs[i], land_refs[i], sems[2 * g], sems[2 * g + 1], 3 * t + j).start()
        token[...] = jnp.zeros_like(token)

    sem_shapes = []
    for items in _AG_GROUPS:
        sem_shapes += [pltpu.SemaphoreType.DMA((3 * len(items),))] * 2
    thru = [pltpu.HBM(a.shape, a.dtype) for a in list(shards) + lands]
    out = pl.pallas_call(
        body, name="allgather_start",
        out_shape=tuple(sem_shapes) + tuple(thru) + (jax.ShapeDtypeStruct((8, 128), F32),),
        in_specs=(_HBM,) * (2 * n),
        out_specs=(_SEM,) * (2 * ng) + (_HBM,) * (2 * n) + (pl.BlockSpec(memory_space=pltpu.VMEM),),
        input_output_aliases={i: 2 * ng + i for i in range(2 * n)},
        compiler_params=pltpu.CompilerParams(**_SPLIT),
    )(*[_hbm(a) for a in list(shards) + lands])
    sems, thru, token = out[:2 * ng], out[2 * ng:-1], out[-1]
    return [(sems[2 * g], sems[2 * g + 1]) for g in range(ng)], list(thru[:n]), list(thru[n:]), token


def _ag_wait(g, sems, srcs, lands, after):
    items = _AG_GROUPS[g]
    m = len(items)

    def body(*refs):
        src_refs, land_refs = refs[:m], refs[m:2 * m]
        send_sems, recv_sems = refs[2 * m], refs[2 * m + 1]
        x, y, c = _position()
        for t, i in enumerate(items):
            for j, chip in enumerate(_other_chips(x, y)):
                cp = _ag_ici_copy(i, j, chip, c, 2 * chip[0] + chip[1], src_refs[t], land_refs[t], send_sems, recv_sems,
                                  3 * t + j)
                cp.wait_send()
                cp.wait_recv()

    ops = [srcs[i] for i in items] + [lands[i] for i in items]
    out = pl.pallas_call(
        body, name=f"allgather_wait_{g}",
        out_shape=tuple(pltpu.HBM(a.shape, a.dtype) for a in ops),
        in_specs=(_HBM,) * (2 * m) + (_SEM, _SEM, pl.BlockSpec(memory_space=pl.ANY)),
        out_specs=(_HBM,) * (2 * m),
        input_output_aliases={i: i for i in range(2 * m)},
        compiler_params=pltpu.CompilerParams(**_SPLIT),
    )(*ops, sems[0], sems[1], after)
    return list(out[:m]), list(out[m:])


def _ag_forward(g, srcs, lands):
    return _ag_sibling(_AG_GROUPS[g], srcs, lands, False, f"allgather_forward_{g}")


def _ag_push_own(srcs, lands):
    return _ag_sibling(tuple(range(len(_AG_ITEMS))), srcs, lands, True, "allgather_push_own")


def _ag_sibling(items, srcs, lands, own, name):
    m = len(items)
    per = 2 if own else 3

    def body(*refs):
        src_refs, in_refs, out_refs = refs[:m], refs[m:2 * m], refs[2 * m:3 * m]
        send_sems, recv_sems = refs[3 * m:]
        x, y, c = _position()
        sibling = (x, y, 1 - c)
        me = 2 * x + y
        if own:
            mine = theirs = [(me, 0), (me, 1)]
        else:
            slots = [2 * chip[0] + chip[1] for chip in _other_chips(x, y)]
            mine, theirs = [(s, c) for s in slots], [(s, 1 - c) for s in slots]
        sends = []
        for t, i in enumerate(items):
            _, dst, half = _AG_ITEMS[i]
            for k, (slot, hc) in enumerate(mine):
                src = half(src_refs[t], hc) if own else dst(in_refs[t], slot, hc)
                sends.append(pltpu.make_async_remote_copy(
                    src_ref=src, dst_ref=dst(out_refs[t], slot, hc), send_sem=send_sems.at[per * t + k],
                    recv_sem=recv_sems.at[per * t + k], device_id=sibling, device_id_type=MESH))
        for cp in sends:
            cp.start()
        for t, i in enumerate(items):
            dst = _AG_ITEMS[i][1]
            for k, (slot, hc) in enumerate(theirs):
                there = dst(out_refs[t], slot, hc)
                pltpu.make_async_remote_copy(src_ref=there, dst_ref=there, send_sem=send_sems.at[per * t + k],
                                             recv_sem=recv_sems.at[per * t + k], device_id=sibling,
                                             device_id_type=MESH).wait_recv()
        for cp in sends:
            cp.wait_send()

    any_spec = pl.BlockSpec(memory_space=pl.ANY)
    return pl.pallas_call(
        body, name=name,
        in_specs=[any_spec] * (2 * m), out_specs=(any_spec,) * m,
        out_shape=tuple(jax.ShapeDtypeStruct(a.shape, a.dtype) for a in lands),
        input_output_aliases={m + t: t for t in range(m)},
        scratch_shapes=[pltpu.SemaphoreType.DMA((per * m,)), pltpu.SemaphoreType.DMA((per * m,))],
    )(*srcs, *lands)


def _pair_swap_copy(g_ref, r_ref, send_sem, recv_sem):
    x, y, c = _position()
    hc = g_ref.shape[2] // 2
    return pltpu.make_async_remote_copy(src_ref=g_ref.at[:, :, pl.ds(_al(1 - c, hc), hc)], dst_ref=r_ref,
                                        send_sem=send_sem, recv_sem=recv_sem, device_id=(x, y, 1 - c),
                                        device_id_type=MESH)


def _pair_swap_start(gb, tag):
    _, rows, cols = gb.shape
    recv = lax.empty((4, rows, cols // 2), gb.dtype)

    def body(g_ref, r_ref, send_sem, recv_sem, g_thru, r_thru, token):
        _pair_swap_copy(g_ref, r_ref, send_sem, recv_sem).start()
        token[...] = jnp.zeros_like(token)

    return pl.pallas_call(
        body, name="grad_pair_swap_start_" + tag,
        out_shape=(pltpu.SemaphoreType.DMA(()), pltpu.SemaphoreType.DMA(()), pltpu.HBM(gb.shape, gb.dtype),
                   pltpu.HBM(recv.shape, recv.dtype), jax.ShapeDtypeStruct((8, 128), F32)),
        in_specs=(_HBM, _HBM), out_specs=(_SEM, _SEM, _HBM, _HBM, pl.BlockSpec(memory_space=pltpu.VMEM)),
        input_output_aliases={0: 2, 1: 3},
        compiler_params=pltpu.CompilerParams(**_SPLIT),
    )(_hbm(gb), _hbm(recv))


def _pair_swap_wait(started, after, tag):
    send_sem, recv_sem, gb, recv, _ = started

    def body(g_ref, r_ref, send_sem, recv_sem, after_ref, g_out, r_out):
        cp = _pair_swap_copy(g_ref, r_ref, send_sem, recv_sem)
        cp.wait_send()
        cp.wait_recv()

    return pl.pallas_call(
        body, name="grad_pair_swap_wait_" + tag,
        out_shape=(pltpu.HBM(gb.shape, gb.dtype), pltpu.HBM(recv.shape, recv.dtype)),
        in_specs=(_HBM, _HBM, _SEM, _SEM, pl.BlockSpec(memory_space=pl.ANY)), out_specs=(_HBM, _HBM),
        input_output_aliases={0: 0, 1: 1},
        compiler_params=pltpu.CompilerParams(**_SPLIT),
    )(gb, recv, send_sem, recv_sem, after)


def _handover_copy(r_ref, send_sem, recv_sem, core):
    x, y, c = _position()
    hc = r_ref.shape[1] // 2
    cols = r_ref.at[:, pl.ds(_al(core, hc), hc)]
    return pltpu.make_async_remote_copy(src_ref=cols, dst_ref=cols, send_sem=send_sem, recv_sem=recv_sem,
                                        device_id=(x, y, 1 - c), device_id_type=MESH)


def _handover_start(red, tag):
    def body(r_ref, send_sem, recv_sem, r_thru, token):
        _handover_copy(r_ref, send_sem, recv_sem, lax.axis_index("c")).start()
        token[...] = jnp.zeros_like(token)

    return pl.pallas_call(
        body, name="grad_handover_start_" + tag,
        out_shape=(pltpu.SemaphoreType.DMA(()), pltpu.SemaphoreType.DMA(()), pltpu.HBM(red.shape, red.dtype),
                   jax.ShapeDtypeStruct((8, 128), F32)),
        in_specs=(_HBM,), out_specs=(_SEM, _SEM, _HBM, pl.BlockSpec(memory_space=pltpu.VMEM)),
        input_output_aliases={0: 2},
        compiler_params=pltpu.CompilerParams(**_SPLIT),
    )(_hbm(red))


def _handover_wait(started, after, tag):
    send_sem, recv_sem, red, _ = started

    def body(r_ref, send_sem, recv_sem, after_ref, r_out):
        c = lax.axis_index("c")
        _handover_copy(r_ref, send_sem, recv_sem, c).wait_send()
        _handover_copy(r_ref, send_sem, recv_sem, 1 - c).wait_recv()

    return pl.pallas_call(
        body, name="grad_handover_wait_" + tag,
        out_shape=pltpu.HBM(red.shape, red.dtype),
        in_specs=(_HBM, _SEM, _SEM, pl.BlockSpec(memory_space=pl.ANY)), out_specs=_HBM,
        input_output_aliases={0: 0},
        compiler_params=pltpu.CompilerParams(**_SPLIT),
    )(red, send_sem, recv_sem, after)


def _handover(red, tag):
    started = _handover_start(red, tag)
    return _handover_wait(started, started[3], tag)


def _a2a_copy(j, chip, c, p_ref, q_ref, q_slot, send_sems, recv_sems):
    return pltpu.make_async_remote_copy(src_ref=p_ref.at[2 * chip[0] + chip[1]], dst_ref=q_ref.at[q_slot],
                                        send_sem=send_sems.at[j], recv_sem=recv_sems.at[j], device_id=(*chip, c),
                                        device_id_type=MESH)


def _a2a_start(p, tag):
    def body(p_ref, q_ref, send_sems, recv_sems, p_thru, q_thru, token):
        x, y, c = _position()
        for j, chip in enumerate(_other_chips(x, y)):
            _a2a_copy(j, chip, c, p_ref, q_ref, 2 * x + y, send_sems, recv_sems).start()
        token[...] = jnp.zeros_like(token)

    return pl.pallas_call(
        body, name="grad_alltoall_start_" + tag,
        out_shape=(pltpu.SemaphoreType.DMA((3,)), pltpu.SemaphoreType.DMA((3,)), pltpu.HBM(p.shape, p.dtype),
                   pltpu.HBM(p.shape, p.dtype), jax.ShapeDtypeStruct((8, 128), F32)),
        in_specs=(_HBM, _HBM), out_specs=(_SEM, _SEM, _HBM, _HBM, pl.BlockSpec(memory_space=pltpu.VMEM)),
        input_output_aliases={0: 2, 1: 3},
        compiler_params=pltpu.CompilerParams(**_SPLIT),
    )(_hbm(p), _hbm(lax.empty(p.shape, p.dtype)))


def _a2a_wait(send_sems, recv_sems, p, q, after, tag):
    def body(p_ref, q_ref, send_sems, recv_sems, after_ref, p_out, q_out):
        x, y, c = _position()
        for j, chip in enumerate(_other_chips(x, y)):
            cp = _a2a_copy(j, chip, c, p_ref, q_ref, 2 * chip[0] + chip[1], send_sems, recv_sems)
            cp.wait_send()
            cp.wait_recv()

    return pl.pallas_call(
        body, name="grad_alltoall_wait_" + tag,
        out_shape=(pltpu.HBM(p.shape, p.dtype), pltpu.HBM(q.shape, q.dtype)),
        in_specs=(_HBM, _HBM, _SEM, _SEM, pl.BlockSpec(memory_space=pl.ANY)), out_specs=(_HBM, _HBM),
        input_output_aliases={0: 0, 1: 1},
        compiler_params=pltpu.CompilerParams(**_SPLIT),
    )(p, q, send_sems, recv_sems, after)


def _comm_rows(rows):
    return next(t for t in (512, 384, 256, 128) if rows % t == 0)


def _pair_add(gb, recv, where, tag):
    _, rows, cols = gb.shape
    hc = cols // 2
    tr = _comm_rows(rows)

    def body(w_ref, g_ref, r_ref, o_ref):
        o_ref[...] = (g_ref[...].astype(F32) + r_ref[...].astype(F32)).astype(o_ref.dtype)

    return pl.pallas_call(
        body, name="grad_pair_add_" + tag,
        grid_spec=pltpu.PrefetchScalarGridSpec(
            num_scalar_prefetch=1, grid=(4, rows // tr),
            in_specs=[pl.BlockSpec((None, tr, hc), lambda s, j, w_ref: (s, j, w_ref[0])),
                      pl.BlockSpec((None, tr, hc), lambda s, j, w_ref: (s, j, 0))],
            out_specs=pl.BlockSpec((None, tr, hc), lambda s, j, w_ref: (s, j, 0))),
        out_shape=jax.ShapeDtypeStruct((4, rows, hc), gb.dtype),
        compiler_params=_cp(("parallel", "parallel")),
    )(where, gb, recv)


def _sum_chips(p, q, where, tag):
    _, rows, hc = q.shape
    tr = _comm_rows(rows)

    def body(w_ref, p_ref, qa_ref, qb_ref, qc_ref, o_ref):
        me = w_ref[1]
        own, qa, qb, qc = (r[...].astype(F32) for r in (p_ref, qa_ref, qb_ref, qc_ref))
        v0 = jnp.where(me == 0, own, qa)
        v1 = jnp.where(me == 1, own, jnp.where(me == 0, qa, qb))
        v2 = jnp.where(me == 2, own, jnp.where(me < 2, qb, qc))
        v3 = jnp.where(me == 3, own, qc)
        o_ref[...] = ((v0 + v1) + v2) + v3

    slot = lambda k: pl.BlockSpec((None, tr, hc), lambda j, w_ref: (w_ref[k], j, 0))
    return pl.pallas_call(
        body, name="grad_sum_chips_" + tag,
        grid_spec=pltpu.PrefetchScalarGridSpec(
            num_scalar_prefetch=1, grid=(rows // tr,),
            in_specs=[slot(1), slot(2), slot(3), slot(4)],
            out_specs=pl.BlockSpec((tr, hc), lambda j, w_ref: (j, w_ref[0]))),
        out_shape=jax.ShapeDtypeStruct((rows, 2 * hc), F32),
        compiler_params=_cp(("parallel",)),
    )(where, p, q, q, q)


def _shard_major(g, axis):
    shape = g.shape
    g = g.reshape(shape[:axis] + (4, shape[axis] // 4) + shape[axis + 1:])
    return jnp.moveaxis(g, axis, 0).reshape(4, -1)


def _unshard(g4, shape, axis):
    n = shape[axis] // 4
    g = g4.reshape((4,) + shape[:axis] + (n,) + shape[axis + 1:])
    return jnp.moveaxis(g, 0, axis).reshape(shape)


def _split(flat, shapes):
    out, off = [], 0
    for shp in shapes:
        n = 1
        for d in shp:
            n *= d
        out.append(flat[..., off:off + n].reshape(flat.shape[:-1] + tuple(shp)))
        off += n
    return out


def _even_rows_to_kernel(wt):
    return jnp.concatenate([wt[:1536], wt[1552:3088], wt[1536:1552], wt[3088:3096],
                            jnp.zeros((PE - 3096, wt.shape[1]), wt.dtype)], axis=0)


def _block_diag(w):
    eye = jnp.eye(8, dtype=w.dtype)
    return (w[:, :, None, :] * eye[:, None, :, None]).reshape(512, 512)


def _diag_blocks(g):
    eye = jnp.eye(8, dtype=g.dtype)
    return (g.reshape(8, 64, 8, 64) * eye[:, None, :, None]).sum(axis=2)


def _shift_down(a, s):
    return a if s == 0 else jnp.pad(a, ((s, 0), (0, 0)))[:a.shape[0]]


def _shift_up(a, s):
    return a if s == 0 else jnp.pad(a, ((0, s), (0, 0)))[s:]


SMALL_SHARDED_SHAPES = [(2, 4, 256), (16, 64), (4, 128), (128,), (128,), (128,), (128,)]
REPL_SHAPES = [(256,), (512,), (8,), (8, 257), (8, 64, 64), (8, 64, 64)]


def kernel(x, norm_w, w_in_even, gla_w_a_up, gla_b_a, gla_norm_w, fox_b_f, w_out_even, w_in_odd, rel_bias, conv_w, conv_b, lru_w_a, lru_b_a, lru_w_x, lru_b_x, lru_lambda, w_out_odd, w_mlp_up, w_mlp_down, loss_target, m_norm_w, m_w_in_even, m_gla_w_a_up, m_gla_b_a, m_gla_norm_w, m_fox_b_f, m_w_out_even, m_w_in_odd, m_rel_bias, m_conv_w, m_conv_b, m_lru_w_a, m_lru_b_a, m_lru_w_x, m_lru_b_x, m_lru_lambda, m_w_out_odd, m_w_mlp_up, m_w_mlp_down, v_norm_w, v_w_in_even, v_gla_w_a_up, v_gla_b_a, v_gla_norm_w, v_fox_b_f, v_w_out_even, v_w_in_odd, v_rel_bias, v_conv_w, v_conv_b, v_lru_w_a, v_lru_b_a, v_lru_w_x, v_lru_b_x, v_lru_lambda, v_w_out_odd, v_w_mlp_up, v_w_mlp_down):
    c_idx = lax.axis_index("c")

    small_local = [norm_w, gla_w_a_up[0], conv_w[0], conv_b[0], lru_b_a[0], lru_b_x[0], lru_lambda[0]]
    small_src = jnp.concatenate([a.reshape(-1) for a in small_local]).reshape(32, 128)
    mine = [small_src, w_in_even[0].T.astype(BF16), w_out_even[0].astype(BF16), w_mlp_up.astype(BF16),
            w_mlp_down.astype(BF16), w_in_odd[0].astype(BF16), w_out_odd[0].astype(BF16)]
    ag_sems, ag_srcs, ag_lands, ag_token = _ag_start(mine)
    ag_lands = list(_ag_push_own(ag_srcs, ag_lands))

    def gathered(g, after):
        srcs_g, lands_g = _ag_wait(g, ag_sems[g], ag_srcs, ag_lands, after)
        return _ag_forward(g, srcs_g, lands_g)

    small4, w_in_e4, w_out_e = gathered(0, ag_token)
    me = 2 * lax.axis_index("x") + lax.axis_index("y")
    others = [k + (k >= me).astype(jnp.int32) for k in range(3)]
    where = jnp.stack([c_idx, me] + others).astype(jnp.int32)

    w_in_e_t = _even_rows_to_kernel(w_in_e4.reshape(3096, D))
    g_small = _split(small4.reshape(4, 32 * 128), SMALL_SHARDED_SHAPES)
    nw_full = _unshard(g_small[0], (2, 4, 1024), 2)
    wa_up = _unshard(g_small[1], (16, 256), 1)
    cw = _unshard(g_small[2], (4, 512), 1)
    cb, lba, lbx, lam = [_unshard(g, (512,), 0).reshape(1, 512) for g in g_small[3:]]
    nw = lambda layer, i: nw_full[layer, i].reshape(1, D)

    wa_pad = jnp.pad(wa_up, ((0, 128 - 16), (0, 0)))
    gla_ba = gla_b_a.reshape(1, 256)
    gla_nw = gla_norm_w.reshape(1, 512)
    fox_bpad = jnp.pad(fox_b_f.reshape(1, 8), ((0, 0), (FOX_LANE0, 128 - FOX_LANE0 - 8)))
    rbp = jnp.pad(rel_bias[0], ((0, 0), (0, REL_PAD - 257)))
    wa_bd = _block_diag(lru_w_a[0])
    wx_bd = _block_diag(lru_w_x[0])

    x0 = x[0]
    tgt = loss_target[0]

    h0 = _prenorm(x0, nw(0, 0), "prenorm_l0_mix")
    proj_e = _mm(h0, w_in_e_t, "nt", tm=1024, tn=640, name="mm_in_even")
    cat0, s_prev = _gla_fwd(proj_e, wa_pad, gla_ba, gla_nw)
    cum_r = _fox_gate_fwd(proj_e, fox_bpad)
    cum_c = cum_r[:, FOX_LANE0:FOX_LANE0 + 8].T
    cat0 = _fox_fwd(proj_e, cum_c, cat0)
    mix0 = _mm(cat0, w_out_e, "nn", tm=1024, tn=512, name="mm_out_even")
    x1, h1 = _post_pre_fwd(x0, mix0, nw(0, 1), nw(0, 2), "post_pre_l0_mix")
    w_up, w_dn = gathered(1, x1)
    a0, r0 = _mm(h1, w_up, "nn", tm=1024, tn=1024, b_layer=0, relu_pair=True, name="mm_up_l0")
    d0 = _mm(a0, w_dn, "nn", tm=1024, tn=512, b_layer=0, name="mm_down_l0")
    x2, h2 = _post_pre_fwd(x1, d0, nw(0, 3), nw(1, 0), "post_pre_l0_mlp")

    w_in_o, w_out_o = gathered(2, x2)
    proj_o = _mm(h2, w_in_o, "nn", tm=1024, tn=640, name="mm_in_odd")
    bias_q = _bias_build(rbp)
    bias = bias_q.transpose(1, 0, 2)
    kvpad = jnp.pad(proj_o[:, 512:1536], ((CA_PAD, 0), (0, 0)))
    cat1 = _ca_fwd(proj_o, kvpad, bias)
    x_in = proj_o[:, 2048:2560]
    xs = jnp.stack([_shift_down(x_in, 3 - j) for j in range(4)])
    lru_a, lru_b = _lru_pre_fwd(xs, cw, cb, wa_bd, lba, wx_bd, lbx, lam)
    hh = _lru_scan_fwd(lru_a, lru_b)
    cat1 = _lru_post_fwd(hh, proj_o, cat1)
    mix1 = _mm(cat1, w_out_o, "nn", tm=1024, tn=512, name="mm_out_odd")
    x3, h3 = _post_pre_fwd(x2, mix1, nw(1, 1), nw(1, 2), "post_pre_l1_mix")
    a1, r1 = _mm(h3, w_up, "nn", tm=1024, tn=1024, b_layer=1, relu_pair=True, name="mm_up_l1")
    d1 = _mm(a1, w_dn, "nn", tm=1024, tn=512, b_layer=1, name="mm_down_l1")
    g4, loss_part, dd1, dnw13 = _post_loss(x3, d1, nw(1, 3), tgt)
    loss = lax.psum(loss_part[0, 0], ("x", "y", "c"))

    def rs_begin(swap, after, tag):
        gb, recv = _pair_swap_wait(swap, after, tag)
        return _a2a_start(_pair_add(gb, recv, where, tag), tag)

    def rs_end(started, after, tag):
        send_sems, recv_sems, p, q, _ = started
        p, q = _a2a_wait(send_sems, recv_sems, p, q, after, tag)
        return _handover(_sum_chips(p, q, where, tag), tag)

    gba = lax.dynamic_update_slice(lax.empty((4, GA_ROWS, D), BF16), jnp.zeros((4, GA_UP - GA_GAP, D), BF16),
                                   (0, GA_GAP, 0))
    gba = _mm(a1, dd1, "tn", tm=512, tn=1024, into=(gba, 1024, GA_DN), name="mm_down_l1_dw")
    du1 = _mm(dd1, w_dn, "nt", tm=1024, tn=1024, b_layer=1, times2=r1, out_dtype=BF16, name="mm_down_l1_dx")
    gba = _mm(du1, h3, "tn", tm=512, tn=1024, into=(gba, 1024, GA_UP), name="mm_up_l1_dw")
    dh3 = _mm(du1, w_up, "nt", tm=1024, tn=512, b_layer=1, name="mm_up_l1_dx")
    g3, dmix1, dnw12, dnw11 = _pre_post_bwd(x3, nw(1, 2), dh3, g4, mix1, nw(1, 1), "pre_post_bwd_l1_mlp")
    gba = _mm(cat1, dmix1, "tn", tm=128, tn=1024, into=(gba, 256, GA_OUT_O), name="mm_out_odd_dw")
    dcat1 = _mm(dmix1, w_out_o, "nt", tm=1024, tn=512, name="mm_out_odd_dx")

    dq_c, dkpad, dvpad, dbias = _ca_bwd(proj_o, kvpad, bias, dcat1)
    g_rel = _bias_grad(jnp.pad(dbias.transpose(1, 0, 2), ((0, 0), (0, 0), (0, BIAS_W - CA_BAND))))[:, :257]
    dhh, dgate = _lru_post_bwd(hh, proj_o, dcat1)
    da_l, db_l = _lru_scan_bwd(_shift_up(lru_a, 1), _shift_down(hh, 1), dhh)
    dxs, g_cw, g_cb, g_wa_bd, g_lba, g_wx_bd, g_lbx, g_lam = _lru_pre_bwd(xs, cw, cb, wa_bd, lba, wx_bd, lbx, lam, da_l, db_l)
    dx_in = _conv_dx(jnp.stack([_shift_up(dxs[j], 3 - j) for j in range(4)]))
    dproj_o = jnp.concatenate([dq_c, dkpad[CA_PAD:], dvpad[CA_PAD:], dgate, dx_in], axis=1).astype(BF16)
    gba = _mm(dproj_o, h2, "tn", tm=128, tn=1024, into=(gba, 640, GA_IN_O), name="mm_in_odd_dw")
    swap_a = _pair_swap_start(gba, "a")
    dh2 = _mm(dproj_o, w_in_o, "nt", tm=1024, tn=512, name="mm_in_odd_dx")
    g2, dd0, dnw10, dnw03 = _pre_post_bwd(x2, nw(1, 0) + swap_a[4][0, 0], dh2, g3, d0, nw(0, 3), "pre_post_bwd_l1_mix")
    rs_a = rs_begin(swap_a, g2, "a")

    gbb = lax.empty((4, GB_ROWS, D), BF16)
    gbb = _mm(a0, dd0, "tn", tm=512, tn=1024, into=(gbb, 1024, GB_DN), name="mm_down_l0_dw")
    du0 = _mm(dd0, w_dn, "nt", tm=1024, tn=1024, b_layer=0, times2=r0, out_dtype=BF16, name="mm_down_l0_dx")
    gbb = _mm(du0, h1, "tn", tm=512, tn=1024, into=(gbb, 1024, GB_UP), name="mm_up_l0_dw")
    swap_b = _pair_swap_start(gbb, "b")
    dh1 = _mm(du0, w_up, "nt", tm=1024, tn=512, b_layer=0, name="mm_up_l0_dx")
    g1, dmix0, dnw02, dnw01 = _pre_post_bwd(x1, nw(0, 2) + (swap_b[4][0, 0] + rs_a[4][0, 0]), dh1, g2, mix0, nw(0, 1),
                                            "pre_post_bwd_l0_mlp")
    rs_b = rs_begin(swap_b, g1, "b")
    gbc = lax.empty((4, GC_ROWS, D), BF16)
    gbc = _mm(cat0, dmix0, "tn", tm=128, tn=1024, into=(gbc, 256, GC_OUT_E), name="mm_out_even_dw")
    dcat0 = _mm(dmix0, w_out_e, "nt", tm=1024, tn=512, name="mm_out_even_dx")

    dq_g, dk_g, dv_g, dr_g, daux_g, g_wa_pad, g_gla_ba, g_gla_nw = _gla_bwd(
        proj_e, s_prev, wa_pad, gla_ba, gla_nw + rs_b[4][0, 0], dcat0)
    dq_f, dk_f, dv_f, dccol = _fox_bwd(proj_e, cum_c, dcat0)
    dccol_t = jnp.pad(dccol.sum(axis=0).T, ((0, 0), (FOX_LANE0, 128 - FOX_LANE0 - 8)))
    daux, g_fox_bpad = _fox_gate_bwd(proj_e, fox_bpad, dccol_t, daux_g)
    dproj_e = jnp.concatenate([dq_g, dk_g, dv_g, dr_g, dq_f, dk_f, dv_f, daux], axis=1).astype(BF16)
    gt_in_e = _mm(dproj_e, h0, "tn", tm=640, tn=1024, out_dtype=BF16, name="mm_in_even_dw")
    dh0 = _mm(dproj_e, w_in_e_t, "nn", tm=1024, tn=512, name="mm_in_even_dx")
    grad_x, dnw00 = _norm_bwd(x0, nw(0, 0), dh0, g1, "prenorm_l0_mix_bwd")

    def rs_reduce(started, after, tag):
        send_sems, recv_sems, p, q, _ = started
        p, q = _a2a_wait(send_sems, recv_sems, p, q, after, tag)
        return _handover_start(_sum_chips(p, q, where, tag), tag)

    ho_a = rs_reduce(rs_a, grad_x, "a")
    ho_b = rs_reduce(rs_b, ho_a[3], "b")

    g_norm = jnp.stack([jnp.concatenate([dnw00, dnw01, dnw02, dnw03]), jnp.concatenate([dnw10, dnw11, dnw12, dnw13])])
    sharded = [(g_norm, 2), (g_wa_pad[:16], 1), (g_cw, 1), (g_cb[0], 0), (g_lba[0], 0), (g_lbx[0], 0), (g_lam[0], 0)]
    replicated = [g_gla_ba[0], g_gla_nw[0], g_fox_bpad[0, FOX_LANE0:FOX_LANE0 + 8], g_rel, _diag_blocks(g_wa_bd),
                  _diag_blocks(g_wx_bd)]
    small4 = jnp.concatenate([_shard_major(g, ax) for g, ax in sharded]
                             + [jnp.broadcast_to(g.reshape(1, -1), (4, g.size)) for g in replicated], axis=1)
    n_small = small4.shape[1]
    small_rows = GC_ROWS - GC_TAIL - 774
    small4 = jnp.pad(small4, ((0, 0), (0, small_rows * D - n_small))).reshape(4, small_rows, D)
    gt_rows = jnp.concatenate([gt_in_e[:1536], gt_in_e[3072:3088], gt_in_e[1536:3072], gt_in_e[3088:3096]], axis=0)
    tail = jnp.concatenate([gt_rows.reshape(4, 774, D), small4.astype(BF16)], axis=1)
    gbc = lax.dynamic_update_slice(gbc, tail, (0, GC_TAIL, 0))
    swap_c = _pair_swap_start(gbc, "c")
    rs_c = rs_begin(swap_c, swap_c[4], "c")

    red_a = _handover_wait(ho_a, rs_c[4], "a")
    red_b = _handover_wait(ho_b, red_a, "b")
    early = dict(
        w_mlp_up=_adamw_from(w_mlp_up, m_w_mlp_up, v_w_mlp_up, [(red_b, GB_UP, True), (red_a, GA_UP, True)], 256,
                             "adamw_w_mlp_up"),
        w_mlp_down=_adamw_from(w_mlp_down, m_w_mlp_down, v_w_mlp_down, [(red_b, GB_DN, False), (red_a, GA_DN, False)],
                               256, "adamw_w_mlp_down"),
        w_in_odd=_adamw_from(w_in_odd, m_w_in_odd, v_w_in_odd, [(red_a, GA_IN_O, True)], 256, "adamw_w_in_odd"),
        w_out_odd=_adamw_from(w_out_odd, m_w_out_odd, v_w_out_odd, [(red_a, GA_OUT_O, False)], 128, "adamw_w_out_odd"))
    red_c = rs_end(rs_c, early["w_out_odd"][3], "c")

    g_small = _split(red_c[GC_TAIL + 774:].reshape(-1)[:n_small], SMALL_SHARDED_SHAPES + REPL_SHAPES)
    g_of = dict(zip(["norm_w", "gla_w_a_up", "conv_w", "conv_b", "lru_b_a", "lru_b_x", "lru_lambda", "gla_b_a",
                     "gla_norm_w", "fox_b_f", "rel_bias", "lru_w_a", "lru_w_x"], g_small))
    g_of.update(w_in_even=red_c[GC_TAIL:GC_TAIL + 774])
    early["w_out_even"] = _adamw_from(w_out_even, m_w_out_even, v_w_out_even, [(red_c, GC_OUT_E, False)], 256,
                                      "adamw_w_out_even")

    names = ["norm_w", "w_in_even", "gla_w_a_up", "gla_b_a", "gla_norm_w", "fox_b_f", "w_out_even", "w_in_odd", "rel_bias",
             "conv_w", "conv_b", "lru_w_a", "lru_b_a", "lru_w_x", "lru_b_x", "lru_lambda", "w_out_odd", "w_mlp_up",
             "w_mlp_down"]
    w_of = dict(norm_w=norm_w, w_in_even=w_in_even, gla_w_a_up=gla_w_a_up, gla_b_a=gla_b_a, gla_norm_w=gla_norm_w,
                fox_b_f=fox_b_f, w_out_even=w_out_even, w_in_odd=w_in_odd, rel_bias=rel_bias, conv_w=conv_w, conv_b=conv_b,
                lru_w_a=lru_w_a, lru_b_a=lru_b_a, lru_w_x=lru_w_x, lru_b_x=lru_b_x, lru_lambda=lru_lambda,
                w_out_odd=w_out_odd, w_mlp_up=w_mlp_up, w_mlp_down=w_mlp_down)
    m_of = dict(norm_w=m_norm_w, w_in_even=m_w_in_even, gla_w_a_up=m_gla_w_a_up, gla_b_a=m_gla_b_a,
                gla_norm_w=m_gla_norm_w, fox_b_f=m_fox_b_f, w_out_even=m_w_out_even, w_in_odd=m_w_in_odd,
                rel_bias=m_rel_bias, conv_w=m_conv_w, conv_b=m_conv_b, lru_w_a=m_lru_w_a, lru_b_a=m_lru_b_a,
                lru_w_x=m_lru_w_x, lru_b_x=m_lru_b_x, lru_lambda=m_lru_lambda, w_out_odd=m_w_out_odd,
                w_mlp_up=m_w_mlp_up, w_mlp_down=m_w_mlp_down)
    v_of = dict(norm_w=v_norm_w, w_in_even=v_w_in_even, gla_w_a_up=v_gla_w_a_up, gla_b_a=v_gla_b_a,
                gla_norm_w=v_gla_norm_w, fox_b_f=v_fox_b_f, w_out_even=v_w_out_even, w_in_odd=v_w_in_odd,
                rel_bias=v_rel_bias, conv_w=v_conv_w, conv_b=v_conv_b, lru_w_a=v_lru_w_a, lru_b_a=v_lru_b_a,
                lru_w_x=v_lru_w_x, lru_b_x=v_lru_b_x, lru_lambda=v_lru_lambda, w_out_odd=v_w_out_odd,
                w_mlp_up=v_w_mlp_up, w_mlp_down=v_w_mlp_down)
    grads, deltas, new_ms, new_vs = [], [], [], []
    for n in names:
        w = w_of[n]
        if n in early:
            g, d, mn, vn = early[n]
            grads.append(g)
            deltas.append(d)
            new_ms.append(mn)
            new_vs.append(vn)
            continue
        if n == "w_in_even":
            to_view = lambda a: a[0].T
            from_view = lambda a: a.T[None]
        else:
            view = w.shape if w.ndim <= 3 else w.shape[-3:]
            to_view = lambda a, view=view: a.reshape(view)
            from_view = lambda a, w=w: a.reshape(w.shape)
        g = g_of[n] if n == "w_in_even" else to_view(g_of[n])
        d, mn, vn = _adamw(to_view(w), g, to_view(m_of[n]), to_view(v_of[n]), "adamw_" + n)
        grads.append(from_view(g))
        deltas.append(from_view(d))
        new_ms.append(from_view(mn))
        new_vs.append(from_view(vn))

    return (loss, grad_x.reshape(1, T, D), *grads, *deltas, *new_ms, *new_vs)
```

```python
import functools

import jax
import jax.numpy as jnp
from jax import lax
from jax.experimental import pallas as pl
from jax.experimental.pallas import tpu as pltpu

F32 = jnp.float32
BF16 = jnp.bfloat16
MESH = pl.DeviceIdType.MESH

T = 2048
D = 1024
DFF = 4096
EPS = 1e-6
CHUNK = 64
NCHUNK = T // CHUNK
PE = 3200
PO = 2560
AUX_BLK = 3072 // 128
FOX_LANE0 = 16
GLA_SCALE = 64 ** -0.5
ATT_SCALE = 64 ** -0.5
NEG = float(jnp.finfo(jnp.float32).min)
CA_BAND = 576
CA_PAD = 512
REL_PAD = 384

VMEM_LIMIT = 48 * 1024 * 1024

ADAM_LR, ADAM_B1, ADAM_B2, ADAM_EPS, ADAM_WD, ADAM_STEP = 0.001, 0.9, 0.999, 1e-08, 0.01, 10

GA_ROWS, GA_IN_O, GA_OUT_O, GA_GAP, GA_UP, GA_DN = 3072, 0, 640, 896, 1024, 2048
GB_ROWS, GB_UP, GB_DN = 2048, 0, 1024
GC_ROWS, GC_OUT_E, GC_TAIL = 1152, 0, 256

_DIMS = {"nn": (((1,), (0,)), ((), ())), "nt": (((1,), (1,)), ((), ())), "tn": (((0,), (0,)), ((), ()))}


def _cp(sem, **kw):
    return pltpu.CompilerParams(dimension_semantics=sem, vmem_limit_bytes=VMEM_LIMIT, **kw)


def _dot(a, b, mode):
    return lax.dot_general(a.astype(BF16), b.astype(BF16), _DIMS[mode], preferred_element_type=F32)


@functools.partial(jax.custom_vjp, nondiff_argnums=(2,))
def bdot(a, b, mode):
    return _dot(a, b, mode)


def _bdot_fwd(a, b, mode):
    return _dot(a, b, mode), (a, b)


def _bdot_bwd(mode, res, g):
    a, b = res
    if mode == "nn":
        da, db = _dot(g, b, "nt"), _dot(a, g, "tn")
    elif mode == "nt":
        da, db = _dot(g, b, "nn"), _dot(g, a, "tn")
    else:
        da, db = _dot(b, g, "nt"), _dot(a, g, "nn")
    return da.astype(a.dtype), db.astype(b.dtype)


bdot.defvjp(_bdot_fwd, _bdot_bwd)


def _hdot_raw(a, b, mode):
    return lax.dot_general(a, b, _DIMS[mode], precision=lax.Precision.HIGHEST, preferred_element_type=F32)


@functools.partial(jax.custom_vjp, nondiff_argnums=(2,))
def hdot(a, b, mode):
    return _hdot_raw(a, b, mode)


def _hdot_fwd(a, b, mode):
    return _hdot_raw(a, b, mode), (a, b)


def _hdot_bwd(mode, res, g):
    a, b = res
    if mode == "nn":
        return _hdot_raw(g, b, "nt"), _hdot_raw(a, g, "tn")
    if mode == "nt":
        return _hdot_raw(g, b, "nn"), _hdot_raw(g, a, "tn")
    return _hdot_raw(b, g, "nt"), _hdot_raw(a, g, "nn")


hdot.defvjp(_hdot_fwd, _hdot_bwd)


def _log_sigmoid(x):
    return jnp.minimum(x, 0.0) - jnp.log(1.0 + jnp.exp(-jnp.abs(x)))


def _sigmoid(x):
    return 1.0 / (1.0 + jnp.exp(-x))


def _expm1(x):
    series = x * (1.0 + x * 0.5 * (1.0 + x * (1.0 / 3.0) * (1.0 + x * 0.25)))
    return jnp.where(jnp.abs(x) < 0.03, series, jnp.exp(x) - 1.0)


def _gelu_tanh(x):
    return 0.5 * x * (1.0 + jnp.tanh(0.7978845608028654 * (x + 0.044715 * x * x * x)))


def _iota(shape, dim):
    return lax.broadcasted_iota(jnp.int32, shape, dim)


def _mm(a, b, mode, *, tm, tn, tk=None, out_dtype=F32, name, b_layer=None, into=None, relu_pair=False, times2=None):
    b2 = b.shape[-2:]
    if mode == "nn":
        (m, k), n = a.shape, b2[1]
    elif mode == "nt":
        (m, k), n = a.shape, b2[0]
    else:
        (k, m), n = a.shape, b2[1]
    tk = k if tk is None else tk
    assert m % tm == 0 and n % tn == 0 and k % tk == 0, (name, a.shape, b.shape)
    nk = k // tk
    if mode == "tn":
        a_spec = pl.BlockSpec((tk, tm), lambda i, j, kk: (kk, i))
    elif m == tm and nk == 1:
        a_spec = pl.BlockSpec((tm, tk), lambda i, j, kk: (i, kk), pipeline_mode=pl.Buffered(1))
    else:
        a_spec = pl.BlockSpec((tm, tk), lambda i, j, kk: (i, kk))
    b_blk = {"nn": (tk, tn), "nt": (tn, tk), "tn": (tk, tn)}[mode]
    b_idx = {"nn": lambda i, j, kk: (kk, j), "nt": lambda i, j, kk: (j, kk), "tn": lambda i, j, kk: (kk, j)}[mode]
    if b_layer is None:
        b_spec = pl.BlockSpec(b_blk, b_idx)
    else:
        b_spec = pl.BlockSpec((None,) + b_blk, lambda i, j, kk: (b_layer,) + b_idx(i, j, kk))

    tile = pl.BlockSpec((tm, tn), lambda i, j, kk: (i, j))
    if into is not None:
        buf, per_slot, row_off = into
        assert m == 4 * per_slot and per_slot % tm == 0 and row_off % tm == 0 and buf.shape[2] == n, (name, buf.shape)
        bps = per_slot // tm
        out_specs = pl.BlockSpec((None, tm, tn), lambda i, j, kk: (i // bps, row_off // tm + i % bps, j))
        out_shape = jax.ShapeDtypeStruct(buf.shape, buf.dtype)
        extra_in, extra_specs, aliases = [buf], [pl.BlockSpec(memory_space=pl.ANY)], {2: 0}
        finish = lambda acc, extra: [acc.astype(buf.dtype)]
    elif relu_pair:
        out_specs = (tile, tile)
        out_shape = (jax.ShapeDtypeStruct((m, n), BF16),) * 2
        extra_in, extra_specs, aliases = [], [], {}

        def finish(acc, extra):
            r = jnp.maximum(acc, 0.0)
            return [(r * r).astype(BF16), r.astype(BF16)]
    elif times2 is not None:
        out_specs = tile
        out_shape = jax.ShapeDtypeStruct((m, n), out_dtype)
        extra_in, extra_specs, aliases = [times2], [tile], {}
        finish = lambda acc, extra: [(acc * (2.0 * extra[...].astype(F32))).astype(out_dtype)]
    else:
        out_specs = tile
        out_shape = jax.ShapeDtypeStruct((m, n), out_dtype)
        extra_in, extra_specs, aliases = [], [], {}
        finish = lambda acc, extra: [acc.astype(out_dtype)]
    n_out = 2 if relu_pair else 1

    def body(*refs):
        a_ref, b_ref = refs[0], refs[1]
        extra = refs[2] if extra_in else None
        o_refs = refs[2 + len(extra_in):2 + len(extra_in) + n_out]

        def store(acc):
            for o_ref, val in zip(o_refs, finish(acc, extra)):
                o_ref[...] = val

        if nk == 1:
            store(_dot(a_ref[...], b_ref[...], mode))
            return
        acc_ref = refs[-1]
        kk = pl.program_id(2)

        @pl.when(kk == 0)
        def _():
            acc_ref[...] = jnp.zeros_like(acc_ref)

        acc_ref[...] += _dot(a_ref[...], b_ref[...], mode)

        @pl.when(kk == nk - 1)
        def _():
            store(acc_ref[...])

    return pl.pallas_call(
        body, name=name, grid=(m // tm, n // tn, nk),
        in_specs=[a_spec, b_spec] + extra_specs,
        out_specs=out_specs, out_shape=out_shape,
        scratch_shapes=[pltpu.VMEM((tm, tn), F32)] if nk > 1 else [],
        input_output_aliases=aliases,
        compiler_params=_cp(("parallel", "parallel", "arbitrary")),
    )(a, b, *extra_in)


ROWS = 256


def _prenorm(x, w, name):
    def body(x_ref, w_ref, o_ref):
        xv = x_ref[...]
        r = lax.rsqrt(jnp.mean(xv * xv, axis=-1, keepdims=True) + EPS)
        o_ref[...] = (xv * r * w_ref[...]).astype(BF16)

    return pl.pallas_call(
        body, name=name, grid=(T // ROWS,),
        in_specs=[pl.BlockSpec((ROWS, D), lambda i: (i, 0)), pl.BlockSpec((1, D), lambda i: (0, 0))],
        out_specs=pl.BlockSpec((ROWS, D), lambda i: (i, 0)),
        out_shape=jax.ShapeDtypeStruct((T, D), BF16),
        compiler_params=_cp(("parallel",)),
    )(x, w)


def _rms(z):
    return lax.rsqrt(jnp.mean(z * z, axis=-1, keepdims=True) + EPS)


def _rms_bwd(z, w, dy):
    r = _rms(z)
    wdy = dy * w
    dz = r * wdy - z * (r * r * r) * jnp.mean(z * wdy, axis=-1, keepdims=True)
    return dz, jnp.sum(dy * z * r, axis=0, keepdims=True)


_ROW = pl.BlockSpec((ROWS, D), lambda i: (i, 0))
_VEC = pl.BlockSpec((1, D), lambda i: (0, 0))


def _post_pre_fwd(x, z, w_post, w_pre, name):
    def body(x_ref, z_ref, wp_ref, wn_ref, x_out, h_out):
        zv = z_ref[...]
        xn = x_ref[...] + zv * _rms(zv) * wp_ref[...]
        x_out[...] = xn
        h_out[...] = (xn * _rms(xn) * wn_ref[...]).astype(BF16)

    return pl.pallas_call(
        body, name=name, grid=(T // ROWS,), in_specs=[_ROW, _ROW, _VEC, _VEC], out_specs=(_ROW, _ROW),
        out_shape=(jax.ShapeDtypeStruct((T, D), F32), jax.ShapeDtypeStruct((T, D), BF16)),
        compiler_params=_cp(("parallel",)),
    )(x, z, w_post, w_pre)


def _post_loss(x, z, w_post, tgt):
    def body(x_ref, z_ref, w_ref, t_ref, g_ref, l_ref, dz_ref, dw_ref):
        @pl.when(pl.program_id(0) == 0)
        def _():
            l_ref[...] = jnp.zeros_like(l_ref)
            dw_ref[...] = jnp.zeros_like(dw_ref)

        zv = z_ref[...]
        e = x_ref[...] + zv * _rms(zv) * w_ref[...] - t_ref[...]
        g = e * (1.0 / D)
        g_ref[...] = g
        l_ref[...] += jnp.sum(e * e) * (0.5 / D)
        dz, dw = _rms_bwd(zv, w_ref[...], g)
        dz_ref[...] = dz.astype(BF16)
        dw_ref[...] += dw

    return pl.pallas_call(
        body, name="postnorm_loss", grid=(T // ROWS,), in_specs=[_ROW, _ROW, _VEC, _ROW],
        out_specs=(_ROW, pl.BlockSpec((1, 128), lambda i: (0, 0)), _ROW, _VEC),
        out_shape=(jax.ShapeDtypeStruct((T, D), F32), jax.ShapeDtypeStruct((1, 128), F32),
                   jax.ShapeDtypeStruct((T, D), BF16), jax.ShapeDtypeStruct((1, D), F32)),
        compiler_params=_cp(("arbitrary",)),
    )(x, z, w_post, tgt)


def _pre_post_bwd(x, w_pre, dh, add, z, w_post, name):
    def body(x_ref, wn_ref, dh_ref, add_ref, z_ref, wp_ref, g_ref, dz_ref, dwn_ref, dwp_ref):
        @pl.when(pl.program_id(0) == 0)
        def _():
            dwn_ref[...] = jnp.zeros_like(dwn_ref)
            dwp_ref[...] = jnp.zeros_like(dwp_ref)

        dx, dwn = _rms_bwd(x_ref[...], wn_ref[...], dh_ref[...])
        g = dx + add_ref[...]
        g_ref[...] = g
        dz, dwp = _rms_bwd(z_ref[...], wp_ref[...], g)
        dz_ref[...] = dz.astype(BF16)
        dwn_ref[...] += dwn
        dwp_ref[...] += dwp

    return pl.pallas_call(
        body, name=name, grid=(T // ROWS,), in_specs=[_ROW, _VEC, _ROW, _ROW, _ROW, _VEC],
        out_specs=(_ROW, _ROW, _VEC, _VEC),
        out_shape=(jax.ShapeDtypeStruct((T, D), F32), jax.ShapeDtypeStruct((T, D), BF16),
                   jax.ShapeDtypeStruct((1, D), F32), jax.ShapeDtypeStruct((1, D), F32)),
        compiler_params=_cp(("arbitrary",)),
    )(x, w_pre, dh, add, z, w_post)


def _norm_bwd(z, w, dy, add, name):
    has_add = add is not None

    def body(*refs):
        if has_add:
            z_ref, w_ref, dy_ref, add_ref, dz_ref, dw_ref = refs
        else:
            z_ref, w_ref, dy_ref, dz_ref, dw_ref = refs
        i = pl.program_id(0)

        @pl.when(i == 0)
        def _():
            dw_ref[...] = jnp.zeros_like(dw_ref)

        zv = z_ref[...].astype(F32)
        dyv = dy_ref[...]
        r = lax.rsqrt(jnp.mean(zv * zv, axis=-1, keepdims=True) + EPS)
        wdy = dyv * w_ref[...]
        dz = r * wdy - zv * (r * r * r) * jnp.mean(zv * wdy, axis=-1, keepdims=True)
        if has_add:
            dz = dz + add_ref[...]
        dz_ref[...] = dz.astype(dz_ref.dtype)
        dw_ref[...] += jnp.sum(dyv * zv * r, axis=0, keepdims=True)

    row = pl.BlockSpec((ROWS, D), lambda i: (i, 0))
    vec = pl.BlockSpec((1, D), lambda i: (0, 0))
    ins = [z, w, dy] + ([add] if has_add else [])
    dz_dtype = F32 if has_add else BF16
    return pl.pallas_call(
        body, name=name, grid=(T // ROWS,),
        in_specs=[row, vec, row] + ([row] if has_add else []),
        out_specs=(row, vec),
        out_shape=(jax.ShapeDtypeStruct((T, D), dz_dtype), jax.ShapeDtypeStruct((1, D), F32)),
        compiler_params=_cp(("arbitrary",)),
    )(*ins)


def _adamw_math(w, g, m, v):
    c1 = 1.0 - ADAM_B1 ** ADAM_STEP
    c2 = 1.0 - ADAM_B2 ** ADAM_STEP
    mn = ADAM_B1 * m + (1.0 - ADAM_B1) * g
    vn = ADAM_B2 * v + (1.0 - ADAM_B2) * (g * g)
    return -ADAM_LR * ((mn / c1) / (jnp.sqrt(vn / c2) + ADAM_EPS) + ADAM_WD * w), mn, vn


def _adamw_from(w, m, v, sources, tr, name):
    layers, rows, cols = w.shape
    assert len(sources) == layers and rows % tr == 0, (name, w.shape)
    g_specs = []
    for buf, row0, transposed in sources:
        if transposed:
            assert row0 % cols == 0 and buf.shape[1] == rows, (name, row0)
            g_specs.append(pl.BlockSpec((cols, tr), lambda l, i, b=row0 // cols: (b, i)))
        else:
            assert row0 % tr == 0 and buf.shape[1] == cols, (name, row0)
            g_specs.append(pl.BlockSpec((tr, cols), lambda l, i, b=row0 // tr: (b + i, 0)))

    def body(*refs):
        w_ref, m_ref, v_ref = refs[:3]
        g_refs = refs[3:3 + layers]
        g_out, d_ref, mo_ref, vo_ref = refs[3 + layers:]
        gs = [r[...].T if src[2] else r[...] for r, src in zip(g_refs, sources)]
        g = gs[0] if layers == 1 else jnp.where(pl.program_id(0) == 0, gs[0], gs[1])
        g_out[...] = g
        d_ref[...], mo_ref[...], vo_ref[...] = _adamw_math(w_ref[...], g, m_ref[...], v_ref[...])

    blk = pl.BlockSpec((None, tr, cols), lambda l, i: (l, i, 0))
    sds = jax.ShapeDtypeStruct(w.shape, F32)
    return pl.pallas_call(body, name=name, grid=(layers, rows // tr), in_specs=[blk] * 3 + g_specs,
                          out_specs=(blk,) * 4, out_shape=(sds,) * 4,
                          compiler_params=_cp(("parallel", "parallel")))(w, m, v, *[s[0] for s in sources])


def _adamw(w, g, m, v, name):
    lead = w.shape[:-2]
    assert len(lead) <= 1 and g.shape == w.shape, (name, w.shape, g.shape)
    rows, cols = w.shape[-2:]
    if rows <= 512:
        tr, tc = rows, cols
    elif rows % 256 == 0:
        tr, tc = 256, cols
    else:
        tr, tc = rows, 256
    assert rows % tr == 0 and cols % tc == 0, (name, w.shape)
    c1 = 1.0 - ADAM_B1 ** ADAM_STEP
    c2 = 1.0 - ADAM_B2 ** ADAM_STEP

    def body(w_ref, g_ref, m_ref, v_ref, d_ref, mo_ref, vo_ref):
        gv = g_ref[...]
        mn = ADAM_B1 * m_ref[...] + (1.0 - ADAM_B1) * gv
        vn = ADAM_B2 * v_ref[...] + (1.0 - ADAM_B2) * (gv * gv)
        m_hat = mn / c1
        v_hat = vn / c2
        d_ref[...] = -ADAM_LR * (m_hat / (jnp.sqrt(v_hat) + ADAM_EPS) + ADAM_WD * w_ref[...])
        mo_ref[...] = mn
        vo_ref[...] = vn

    if lead:
        grid = (lead[0], rows // tr, cols // tc)
        blk = pl.BlockSpec((None, tr, tc), lambda l, i, j: (l, i, j))
    else:
        grid = (rows // tr, cols // tc)
        blk = pl.BlockSpec((tr, tc), lambda i, j: (i, j))
    sds = jax.ShapeDtypeStruct(w.shape, F32)
    return pl.pallas_call(body, name=name, grid=grid, in_specs=[blk] * 4, out_specs=(blk,) * 3,
                          out_shape=(sds,) * 3, compiler_params=_cp(("parallel",) * len(grid)))(w, g, m, v)


def _gla_consts():
    ltri = (_iota((CHUNK, CHUNK), 0) >= _iota((CHUNK, CHUNK), 1)).astype(F32)
    ones_c = jnp.ones((CHUNK, 128), F32)
    mask = (_iota((256, 512), 0) // 64 == _iota((256, 512), 1) // 128).astype(F32)
    return ltri, ones_c, mask


def _gla_chunk(consts, q, k, v, r, aux, s_prev, wa, ba, nw):
    ltri, ones_c, mask = consts
    la = _log_sigmoid(bdot(aux, wa, "nn") + ba) * (1.0 / 16.0)
    cum = hdot(ltri, la, "nn")
    total = jnp.sum(la, axis=0, keepdims=True)
    k_dec = k * jnp.exp(total - cum)
    inc = bdot(k_dec, v, "tn") * mask
    dec = jnp.exp(hdot(la, ones_c, "tn"))
    dec = jnp.concatenate([dec, dec, dec, dec], axis=1)
    s_new = dec * s_prev + inc
    o = bdot(q * GLA_SCALE, s_new, "nn")
    parts = []
    for h in range(4):
        oh = o[:, h * 128:(h + 1) * 128]
        parts.append(oh * lax.rsqrt(jnp.mean(oh * oh, axis=-1, keepdims=True) + EPS))
    on = jnp.concatenate(parts, axis=1)
    return s_new, on * nw * (r * _sigmoid(r))


GLA_PER_STEP = 4
GLA_ROWS = GLA_PER_STEP * CHUNK
GLA_STEPS = NCHUNK // GLA_PER_STEP


def _gla_specs(cmap):
    return [pl.BlockSpec((GLA_ROWS, 256), lambda c: (cmap(c), 0)),
            pl.BlockSpec((GLA_ROWS, 256), lambda c: (cmap(c), 1)),
            pl.BlockSpec((GLA_ROWS, 512), lambda c: (cmap(c), 1)),
            pl.BlockSpec((GLA_ROWS, 512), lambda c: (cmap(c), 2)),
            pl.BlockSpec((GLA_ROWS, 128), lambda c: (cmap(c), AUX_BLK))]


def _gla_fwd(proj, wa, ba, nw):
    def body(q_ref, k_ref, v_ref, r_ref, aux_ref, wa_ref, ba_ref, nw_ref, o_ref, sp_ref, s_ref):
        @pl.when(pl.program_id(0) == 0)
        def _():
            s_ref[...] = jnp.zeros_like(s_ref)

        s = s_ref[...]
        consts = _gla_consts()
        outs, states = [], []
        for i in range(GLA_PER_STEP):
            rows = slice(i * CHUNK, (i + 1) * CHUNK)
            states.append(s)
            s, out = _gla_chunk(consts, q_ref[rows, :], k_ref[rows, :], v_ref[rows, :], r_ref[rows, :], aux_ref[rows, :],
                                s, wa_ref[...], ba_ref[...], nw_ref[...])
            outs.append(out)
        s_ref[...] = s
        for i in range(GLA_PER_STEP):
            o_ref[i * CHUNK:(i + 1) * CHUNK, :] = outs[i]
            sp_ref[i] = states[i]

    full = lambda shape: pl.BlockSpec(shape, lambda c: (0,) * len(shape))
    return pl.pallas_call(
        body, name="gla_fwd", grid=(GLA_STEPS,),
        in_specs=_gla_specs(lambda c: c) + [full((128, 256)), full((1, 256)), full((1, 512))],
        out_specs=(pl.BlockSpec((GLA_ROWS, 512), lambda c: (c, 0)),
                   pl.BlockSpec((GLA_PER_STEP, 256, 512), lambda c: (c, 0, 0))),
        out_shape=(jax.ShapeDtypeStruct((T, D), F32), jax.ShapeDtypeStruct((NCHUNK, 256, 512), F32)),
        scratch_shapes=[pltpu.VMEM((256, 512), F32)],
        compiler_params=_cp(("arbitrary",)),
    )(proj, proj, proj, proj, proj, wa, ba, nw)


def _gla_bwd(proj, s_prev_all, wa, ba, nw, dcat):
    rev = lambda c: GLA_STEPS - 1 - c

    def body(q_ref, k_ref, v_ref, r_ref, aux_ref, sp_ref, wa_ref, ba_ref, nw_ref, do_ref,
             dq_ref, dk_ref, dv_ref, dr_ref, daux_ref, dwa_ref, dba_ref, dnw_ref, ds_ref):
        @pl.when(pl.program_id(0) == 0)
        def _():
            ds_ref[...] = jnp.zeros_like(ds_ref)
            dwa_ref[...] = jnp.zeros_like(dwa_ref)
            dba_ref[...] = jnp.zeros_like(dba_ref)
            dnw_ref[...] = jnp.zeros_like(dnw_ref)

        fn = functools.partial(_gla_chunk, _gla_consts())
        ds = ds_ref[...]
        dwa, dba, dnw = dwa_ref[...], dba_ref[...], dnw_ref[...]
        grads = {}
        for i in reversed(range(GLA_PER_STEP)):
            rows = slice(i * CHUNK, (i + 1) * CHUNK)
            _, vjp = jax.vjp(fn, q_ref[rows, :], k_ref[rows, :], v_ref[rows, :], r_ref[rows, :], aux_ref[rows, :],
                             sp_ref[i], wa_ref[...], ba_ref[...], nw_ref[...])
            *grads[i], ds, dwa_i, dba_i, dnw_i = vjp((ds, do_ref[rows, :]))
            dwa, dba, dnw = dwa + dwa_i, dba + dba_i, dnw + dnw_i
        ds_ref[...] = ds
        dwa_ref[...] = dwa
        dba_ref[...] = dba
        dnw_ref[...] = dnw
        for i in range(GLA_PER_STEP):
            rows = slice(i * CHUNK, (i + 1) * CHUNK)
            for ref, g in zip((dq_ref, dk_ref, dv_ref, dr_ref, daux_ref), grads[i]):
                ref[rows, :] = g

    full = lambda shape: pl.BlockSpec(shape, lambda c: (0,) * len(shape))
    blk = lambda w: pl.BlockSpec((GLA_ROWS, w), lambda c: (rev(c), 0))
    sds = lambda *s: jax.ShapeDtypeStruct(s, F32)
    return pl.pallas_call(
        body, name="gla_bwd", grid=(GLA_STEPS,),
        in_specs=_gla_specs(rev) + [pl.BlockSpec((GLA_PER_STEP, 256, 512), lambda c: (rev(c), 0, 0)),
                                    full((128, 256)), full((1, 256)), full((1, 512)), blk(512)],
        out_specs=(blk(256), blk(256), blk(512), blk(512), blk(128), full((128, 256)), full((1, 256)), full((1, 512))),
        out_shape=(sds(T, 256), sds(T, 256), sds(T, 512), sds(T, 512), sds(T, 128),
                   sds(128, 256), sds(1, 256), sds(1, 512)),
        scratch_shapes=[pltpu.VMEM((256, 512), F32)],
        compiler_params=_cp(("arbitrary",)),
    )(proj, proj, proj, proj, proj, s_prev_all, wa, ba, nw, dcat)


GATE_ROWS = 128


def _fox_gate_block(ltri, aux, bpad, carry):
    lf = _log_sigmoid(aux + bpad)
    cum = hdot(ltri, lf, "nn") + carry
    return cum, carry + jnp.sum(lf, axis=0, keepdims=True)


def _gate_ltri():
    return (_iota((GATE_ROWS, GATE_ROWS), 0) >= _iota((GATE_ROWS, GATE_ROWS), 1)).astype(F32)


def _fox_gate_fwd(proj, bpad):
    def body(aux_ref, b_ref, cum_ref, carry_ref):
        i = pl.program_id(0)

        @pl.when(i == 0)
        def _():
            carry_ref[...] = jnp.zeros_like(carry_ref)

        cum, carry = _fox_gate_block(_gate_ltri(), aux_ref[...], b_ref[...], carry_ref[...])
        cum_ref[...] = cum
        carry_ref[...] = carry

    return pl.pallas_call(
        body, name="fox_gate_fwd", grid=(T // GATE_ROWS,),
        in_specs=[pl.BlockSpec((GATE_ROWS, 128), lambda i: (i, AUX_BLK)), pl.BlockSpec((1, 128), lambda i: (0, 0))],
        out_specs=pl.BlockSpec((GATE_ROWS, 128), lambda i: (i, 0)),
        out_shape=jax.ShapeDtypeStruct((T, 128), F32),
        scratch_shapes=[pltpu.VMEM((1, 128), F32)],
        compiler_params=_cp(("arbitrary",)),
    )(proj, bpad)


def _fox_gate_bwd(proj, bpad, dccol_t, daux_gla):
    nb = T // GATE_ROWS
    rev = lambda i: nb - 1 - i

    def body(aux_ref, b_ref, dc_ref, dg_ref, daux_ref, db_ref, dcarry_ref):
        i = pl.program_id(0)

        @pl.when(i == 0)
        def _():
            dcarry_ref[...] = jnp.zeros_like(dcarry_ref)
            db_ref[...] = jnp.zeros_like(db_ref)

        dcum = dc_ref[...]
        fn = functools.partial(_fox_gate_block, _gate_ltri())
        _, vjp = jax.vjp(fn, aux_ref[...], b_ref[...], jnp.zeros((1, 128), F32))
        daux, db, dcarry = vjp((dcum, dcarry_ref[...]))
        daux_ref[...] = daux + dg_ref[...]
        db_ref[...] += db
        dcarry_ref[...] = dcarry

    blk = pl.BlockSpec((GATE_ROWS, 128), lambda i: (rev(i), 0))
    vec = pl.BlockSpec((1, 128), lambda i: (0, 0))
    return pl.pallas_call(
        body, name="fox_gate_bwd", grid=(nb,),
        in_specs=[pl.BlockSpec((GATE_ROWS, 128), lambda i: (rev(i), AUX_BLK)), vec, blk, blk],
        out_specs=(blk, vec),
        out_shape=(jax.ShapeDtypeStruct((T, 128), F32), jax.ShapeDtypeStruct((1, 128), F32)),
        scratch_shapes=[pltpu.VMEM((1, 128), F32)],
        compiler_params=_cp(("arbitrary",)),
    )(proj, bpad, dccol_t, daux_gla)


FOX_Q = 256


FOX_QB = T // FOX_Q


@jax.custom_vjp
def _attend(s, v):
    return _attend_fwd(s, v)[0]


def _attend_fwd(s, v):
    e = jnp.exp(s - jnp.max(s, axis=-1, keepdims=True))
    r = 1.0 / jnp.sum(e, axis=-1, keepdims=True)
    return _dot(e, v, "nn") * r, (e, r, v)


def _attend_bwd(res, do):
    e, r, v = res
    do_r = do * r
    dpr = _dot(do_r, v, "nt")
    ds = e * (dpr - r * jnp.sum(e * dpr, axis=-1, keepdims=True))
    return ds, _dot(e, do_r, "tn").astype(v.dtype)


_attend.defvjp(_attend_fwd, _attend_bwd)


def _fox_block(hp, q, k, v, ccol):
    kl = k.shape[0]
    lane = _iota((FOX_Q, 128), 1)
    tri = jnp.bitwise_and(_iota((2 * FOX_Q, FOX_Q), 0), FOX_Q - 1) >= _iota((2 * FOX_Q, FOX_Q), 1)
    sub = _iota((8, kl), 0)
    qs = q * ATT_SCALE
    q2 = jnp.concatenate([jnp.where(lane < 64, qs, 0.0), jnp.where(lane >= 64, qs, 0.0)], axis=0)
    s = bdot(q2, k, "nt")
    cs = [jnp.sum(jnp.where(sub == 2 * hp + e, ccol, 0.0), axis=0, keepdims=True) for e in range(2)]
    s = jnp.concatenate([s[:FOX_Q] - cs[0], s[FOX_Q:] - cs[1]], axis=0)
    diag = jnp.where(tri, s[:, kl - FOX_Q:], NEG)
    s = diag if kl == FOX_Q else jnp.concatenate([s[:, :kl - FOX_Q], diag], axis=1)
    o2 = _attend(s, v)
    return jnp.where(lane < 64, o2[:FOX_Q], o2[FOX_Q:])


def _fox_in_specs():
    return [pl.BlockSpec((FOX_Q, 128), lambda hp, qb: (qb, 12 + hp)),
            pl.BlockSpec((T, 128), lambda hp, qb: (0, 16 + hp)),
            pl.BlockSpec((T, 128), lambda hp, qb: (0, 20 + hp)),
            pl.BlockSpec((8, T), lambda hp, qb: (0, 0))]


def _fox_fwd(proj, cum_c, cat):
    def body(q_ref, k_ref, v_ref, cc_ref, cat_ref, o_ref):
        qb = pl.program_id(1)
        for g in range(FOX_QB):
            kl = FOX_Q * (g + 1)

            @pl.when(qb == g)
            def _(kl=kl):
                o_ref[...] = _fox_block(pl.program_id(0), q_ref[...], k_ref[0:kl, :], v_ref[0:kl, :], cc_ref[:, 0:kl])

    return pl.pallas_call(
        body, name="fox_fwd", grid=(4, FOX_QB), in_specs=_fox_in_specs() + [pl.BlockSpec(memory_space=pl.ANY)],
        out_specs=pl.BlockSpec((FOX_Q, 128), lambda hp, qb: (qb, 4 + hp)),
        out_shape=jax.ShapeDtypeStruct((T, D), F32), input_output_aliases={4: 0},
        compiler_params=_cp(("parallel", "parallel")),
    )(proj, proj, proj, cum_c, cat)


def _fox_bwd(proj, cum_c, dcat):
    def body(q_ref, k_ref, v_ref, cc_ref, do_ref, dq_ref, dk_ref, dv_ref, dcc_ref):
        qb = pl.program_id(1)

        @pl.when(qb == 0)
        def _():
            dk_ref[...] = jnp.zeros_like(dk_ref)
            dv_ref[...] = jnp.zeros_like(dv_ref)
            dcc_ref[...] = jnp.zeros_like(dcc_ref)

        fn = functools.partial(_fox_block, pl.program_id(0))
        for g in range(FOX_QB):
            kl = FOX_Q * (g + 1)

            @pl.when(qb == g)
            def _(kl=kl):
                _, vjp = jax.vjp(fn, q_ref[...], k_ref[0:kl, :], v_ref[0:kl, :], cc_ref[:, 0:kl])
                dq, dk, dv, dcc = vjp(do_ref[...])
                dq_ref[...] = dq
                dk_ref[0:kl, :] += dk
                dv_ref[0:kl, :] += dv
                dcc_ref[:, 0:kl] += dcc

    sds = lambda *s: jax.ShapeDtypeStruct(s, F32)
    return pl.pallas_call(
        body, name="fox_bwd", grid=(4, FOX_QB),
        in_specs=_fox_in_specs() + [pl.BlockSpec((FOX_Q, 128), lambda hp, qb: (qb, 4 + hp))],
        out_specs=(pl.BlockSpec((FOX_Q, 128), lambda hp, qb: (qb, hp)),
                   pl.BlockSpec((T, 128), lambda hp, qb: (0, hp)),
                   pl.BlockSpec((T, 128), lambda hp, qb: (0, hp)),
                   pl.BlockSpec((None, 8, T), lambda hp, qb: (hp, 0, 0))),
        out_shape=(sds(T, 512), sds(T, 512), sds(T, 512), sds(4, 8, T)),
        compiler_params=_cp(("parallel", "arbitrary")),
    )(proj, proj, proj, cum_c, dcat)


BIAS_W = 640


def _rel_onehot():
    j = _iota((REL_PAD, BIAS_W), 1)
    rel = jnp.clip(CA_PAD + CHUNK - 1 - j, -128, 128) + 128
    return (_iota((REL_PAD, BIAS_W), 0) == rel).astype(F32)


def _bias_build(rbp):
    def body(rb_ref, o_ref):
        f = _hdot_raw(rb_ref[...], _rel_onehot(), "nn")
        for q in range(CHUNK):
            o_ref[q] = pltpu.roll(f, (BIAS_W - (CHUNK - 1 - q)) % BIAS_W, 1)[:, :CA_BAND]

    return pl.pallas_call(body, name="ca_bias_build", out_shape=jax.ShapeDtypeStruct((CHUNK, 8, CA_BAND), F32))(rbp)


def _bias_grad(dbias_q):
    def body(db_ref, o_ref):
        acc = jnp.zeros((8, BIAS_W), F32)
        for q in range(CHUNK):
            acc = acc + pltpu.roll(db_ref[q], CHUNK - 1 - q, 1)
        o_ref[...] = _hdot_raw(acc, _rel_onehot(), "nt")

    return pl.pallas_call(body, name="ca_bias_grad", out_shape=jax.ShapeDtypeStruct((8, REL_PAD), F32))(dbias_q)


def _ca_block(c, masked, q, kb, vb, bias2):
    lane = _iota((CHUNK, 128), 1)
    qs = q * ATT_SCALE
    q2 = jnp.concatenate([jnp.where(lane < 64, qs, 0.0), jnp.where(lane >= 64, qs, 0.0)], axis=0)
    s = bdot(q2, kb, "nt") + bias2.reshape(2 * CHUNK, CA_BAND)
    if masked:
        s = jnp.where((c * CHUNK - CA_PAD + _iota((2 * CHUNK, CA_BAND), 1)) >= 0, s, NEG)
    o2 = _attend(s, vb)
    return jnp.where(lane < 64, o2[:CHUNK], o2[CHUNK:])


CA_PER_STEP = 8
CA_ROWS = CA_PER_STEP * CHUNK
CA_MASKED_STEPS = -(-CA_PAD // CA_ROWS)


def _ca_fwd(proj, kvpad, bias):
    def body(q_ref, k_ref, v_ref, b_ref, o_ref):
        def run(masked):
            outs = []
            for i in range(CA_PER_STEP):
                c = pl.program_id(1) * CA_PER_STEP + i
                band = pl.ds(pl.multiple_of(c * CHUNK, CHUNK), CA_BAND)
                rows = slice(i * CHUNK, (i + 1) * CHUNK)
                outs.append(_ca_block(c, masked, q_ref[rows, :], k_ref[band, :], v_ref[band, :], b_ref[...]))
            for i in range(CA_PER_STEP):
                o_ref[i * CHUNK:(i + 1) * CHUNK, :] = outs[i]

        pl.when(pl.program_id(1) < CA_MASKED_STEPS)(lambda: run(True))
        pl.when(pl.program_id(1) >= CA_MASKED_STEPS)(lambda: run(False))

    return pl.pallas_call(
        body, name="ca_fwd", grid=(4, NCHUNK // CA_PER_STEP),
        in_specs=[pl.BlockSpec((CA_ROWS, 128), lambda hp, c: (c, hp)),
                  pl.BlockSpec((T + CA_PAD, 128), lambda hp, c: (0, hp)),
                  pl.BlockSpec((T + CA_PAD, 128), lambda hp, c: (0, 4 + hp)),
                  pl.BlockSpec((2, CHUNK, CA_BAND), lambda hp, c: (hp, 0, 0))],
        out_specs=pl.BlockSpec((CA_ROWS, 128), lambda hp, c: (c, hp)),
        out_shape=jax.ShapeDtypeStruct((T, D), F32),
        compiler_params=_cp(("parallel", "parallel")),
    )(proj, kvpad, kvpad, bias)


def _ca_bwd(proj, kvpad, bias, dcat):
    def body(q_ref, k_ref, v_ref, b_ref, do_ref, dq_ref, dk_ref, dv_ref, db_ref):
        c = pl.program_id(1)

        @pl.when(c == 0)
        def _():
            dk_ref[...] = jnp.zeros_like(dk_ref)
            dv_ref[...] = jnp.zeros_like(dv_ref)
            db_ref[...] = jnp.zeros_like(db_ref)

        def run(masked):
            grads, bands = [], []
            for i in range(CA_PER_STEP):
                ci = c * CA_PER_STEP + i
                band = pl.ds(pl.multiple_of(ci * CHUNK, CHUNK), CA_BAND)
                rows = slice(i * CHUNK, (i + 1) * CHUNK)
                fn = functools.partial(_ca_block, ci, masked)
                _, vjp = jax.vjp(fn, q_ref[rows, :], k_ref[band, :], v_ref[band, :], b_ref[...])
                grads.append(vjp(do_ref[rows, :]))
                bands.append(band)
            for i, (dq, _, _, _) in enumerate(grads):
                dq_ref[i * CHUNK:(i + 1) * CHUNK, :] = dq
            for band, (_, dkb, dvb, _) in zip(bands, grads):
                dk_ref[band, :] += dkb
                dv_ref[band, :] += dvb
            db_ref[...] += functools.reduce(lambda a, b: a + b, [g[3] for g in grads])

        pl.when(c < CA_MASKED_STEPS)(lambda: run(True))
        pl.when(c >= CA_MASKED_STEPS)(lambda: run(False))

    sds = lambda *s: jax.ShapeDtypeStruct(s, F32)
    padded = lambda: pl.BlockSpec((T + CA_PAD, 128), lambda hp, c: (0, hp))
    return pl.pallas_call(
        body, name="ca_bwd", grid=(4, NCHUNK // CA_PER_STEP),
        in_specs=[pl.BlockSpec((CA_ROWS, 128), lambda hp, c: (c, hp)),
                  pl.BlockSpec((T + CA_PAD, 128), lambda hp, c: (0, hp)),
                  pl.BlockSpec((T + CA_PAD, 128), lambda hp, c: (0, 4 + hp)),
                  pl.BlockSpec((2, CHUNK, CA_BAND), lambda hp, c: (hp, 0, 0)),
                  pl.BlockSpec((CA_ROWS, 128), lambda hp, c: (c, hp))],
        out_specs=(pl.BlockSpec((CA_ROWS, 128), lambda hp, c: (c, hp)), padded(), padded(),
                   pl.BlockSpec((2, CHUNK, CA_BAND), lambda hp, c: (hp, 0, 0))),
        out_shape=(sds(T, 512), sds(T + CA_PAD, 512), sds(T + CA_PAD, 512), sds(8, CHUNK, CA_BAND)),
        compiler_params=_cp(("parallel", "arbitrary")),
    )(proj, kvpad, kvpad, bias, dcat)


def _lru_pre(xs, cw, cb, wa, ba, wx, bx, lam):
    xc = cb + xs[0] * cw[0:1, :] + xs[1] * cw[1:2, :] + xs[2] * cw[2:3, :] + xs[3] * cw[3:4, :]
    ra = _sigmoid(bdot(xc, wa, "nn") + ba)
    ii = _sigmoid(bdot(xc, wx, "nn") + bx)
    la = 8.0 * ra * _log_sigmoid(lam)
    return jnp.exp(la), jnp.sqrt(-_expm1(2.0 * la)) * (ii * xc)


def _lru_pre_specs():
    full = lambda shape: pl.BlockSpec(shape, lambda i: (0,) * len(shape))
    return [pl.BlockSpec((4, ROWS, 512), lambda i: (0, i, 0)), full((4, 512)), full((1, 512)),
            full((512, 512)), full((1, 512)), full((512, 512)), full((1, 512)), full((1, 512))]


def _lru_pre_fwd(xs, cw, cb, wa, ba, wx, bx, lam):
    def body(xs_ref, cw_ref, cb_ref, wa_ref, ba_ref, wx_ref, bx_ref, lam_ref, a_ref, b_ref):
        a, b = _lru_pre(xs_ref[...], cw_ref[...], cb_ref[...], wa_ref[...], ba_ref[...], wx_ref[...], bx_ref[...],
                        lam_ref[...])
        a_ref[...] = a
        b_ref[...] = b

    row = pl.BlockSpec((ROWS, 512), lambda i: (i, 0))
    sds = jax.ShapeDtypeStruct((T, 512), F32)
    return pl.pallas_call(body, name="lru_pre_fwd", grid=(T // ROWS,), in_specs=_lru_pre_specs(),
                          out_specs=(row, row), out_shape=(sds, sds), compiler_params=_cp(("parallel",)),
                          )(xs, cw, cb, wa, ba, wx, bx, lam)


def _lru_pre_bwd(xs, cw, cb, wa, ba, wx, bx, lam, da, db):
    def body(xs_ref, cw_ref, cb_ref, wa_ref, ba_ref, wx_ref, bx_ref, lam_ref, da_ref, db_ref,
             dxs_ref, dcw_ref, dcb_ref, dwa_ref, dba_ref, dwx_ref, dbx_ref, dlam_ref):
        acc = (dcw_ref, dcb_ref, dwa_ref, dba_ref, dwx_ref, dbx_ref, dlam_ref)

        @pl.when(pl.program_id(0) == 0)
        def _():
            for r in acc:
                r[...] = jnp.zeros_like(r)

        _, vjp = jax.vjp(_lru_pre, xs_ref[...], cw_ref[...], cb_ref[...], wa_ref[...], ba_ref[...], wx_ref[...],
                         bx_ref[...], lam_ref[...])
        grads = vjp((da_ref[...], db_ref[...]))
        dxs_ref[...] = grads[0]
        for r, g in zip(acc, grads[1:]):
            r[...] += g

    row = pl.BlockSpec((ROWS, 512), lambda i: (i, 0))
    specs = _lru_pre_specs()
    sds = lambda *s: jax.ShapeDtypeStruct(s, F32)
    return pl.pallas_call(
        body, name="lru_pre_bwd", grid=(T // ROWS,), in_specs=specs + [row, row], out_specs=tuple(specs),
        out_shape=(sds(4, T, 512), sds(4, 512), sds(1, 512), sds(512, 512), sds(1, 512), sds(512, 512), sds(1, 512),
                   sds(1, 512)),
        compiler_params=_cp(("arbitrary",)),
    )(xs, cw, cb, wa, ba, wx, bx, lam, da, db)


SCAN_ROWS = 8


def _scan8(a, b, towards_later):
    row = _iota((SCAN_ROWS, 512), 0)
    for s in (1, 2, 4):
        if towards_later:
            keep, shift = row >= s, s
        else:
            keep, shift = row < SCAN_ROWS - s, SCAN_ROWS - s
        a_s = jnp.where(keep, pltpu.roll(a, shift, 0), 1.0)
        b_s = jnp.where(keep, pltpu.roll(b, shift, 0), 0.0)
        b = a * b_s + b
        a = a * a_s
    return a, b


def _lru_scan_fwd(a, b):
    def body(a_ref, b_ref, h_ref):
        def step(i, carry):
            rows = pl.ds(pl.multiple_of(i * SCAN_ROWS, SCAN_ROWS), SCAN_ROWS)
            a8, b8 = _scan8(a_ref[rows, :], b_ref[rows, :], True)
            h = a8 * carry + b8
            h_ref[rows, :] = h
            return jnp.broadcast_to(h[SCAN_ROWS - 1:, :], (SCAN_ROWS, 512))

        lax.fori_loop(0, T // SCAN_ROWS, step, jnp.zeros((SCAN_ROWS, 512), F32), unroll=2)

    return pl.pallas_call(body, name="lru_scan_fwd", out_shape=jax.ShapeDtypeStruct((T, 512), F32),
                          compiler_params=pltpu.CompilerParams(vmem_limit_bytes=VMEM_LIMIT))(a, b)


def _lru_scan_bwd(a_next, h_prev, dh):
    def body(a_ref, h_ref, dh_ref, da_ref, db_ref):
        def step(i, carry):
            start = T - SCAN_ROWS * (i + 1)
            rows = pl.ds(pl.multiple_of(start, SCAN_ROWS), SCAN_ROWS)
            a8, b8 = _scan8(a_ref[rows, :], dh_ref[rows, :], False)
            g = a8 * carry + b8
            db_ref[rows, :] = g
            da_ref[rows, :] = g * h_ref[rows, :]
            return jnp.broadcast_to(g[:1, :], (SCAN_ROWS, 512))

        lax.fori_loop(0, T // SCAN_ROWS, step, jnp.zeros((SCAN_ROWS, 512), F32), unroll=2)

    sds = jax.ShapeDtypeStruct((T, 512), F32)
    return pl.pallas_call(body, name="lru_scan_bwd", out_shape=(sds, sds),
                          compiler_params=pltpu.CompilerParams(vmem_limit_bytes=VMEM_LIMIT))(a_next, h_prev, dh)


def _lru_post(h, gate):
    return h * _gelu_tanh(gate)


def _lru_post_fwd(h, proj, cat):
    def body(h_ref, g_ref, cat_ref, o_ref):
        o_ref[...] = _lru_post(h_ref[...], g_ref[...])

    row = pl.BlockSpec((ROWS, 512), lambda i: (i, 0))
    return pl.pallas_call(body, name="lru_post_fwd", grid=(T // ROWS,),
                          in_specs=[row, pl.BlockSpec((ROWS, 512), lambda i: (i, 3)), pl.BlockSpec(memory_space=pl.ANY)],
                          out_specs=pl.BlockSpec((ROWS, 512), lambda i: (i, 1)),
                          out_shape=jax.ShapeDtypeStruct((T, D), F32), input_output_aliases={2: 0},
                          compiler_params=_cp(("parallel",)))(h, proj, cat)


def _lru_post_bwd(h, proj, dcat):
    def body(h_ref, g_ref, do_ref, dh_ref, dg_ref):
        _, vjp = jax.vjp(_lru_post, h_ref[...], g_ref[...])
        dh, dg = vjp(do_ref[...])
        dh_ref[...] = dh
        dg_ref[...] = dg

    row = pl.BlockSpec((ROWS, 512), lambda i: (i, 0))
    sds = jax.ShapeDtypeStruct((T, 512), F32)
    return pl.pallas_call(body, name="lru_post_bwd", grid=(T // ROWS,),
                          in_specs=[row, pl.BlockSpec((ROWS, 512), lambda i: (i, 3)),
                                    pl.BlockSpec((ROWS, 512), lambda i: (i, 1))],
                          out_specs=(row, row), out_shape=(sds, sds), compiler_params=_cp(("parallel",)))(h, proj, dcat)


def _conv_dx(dxs_shift):
    def body(d_ref, o_ref):
        o_ref[...] = d_ref[0] + d_ref[1] + d_ref[2] + d_ref[3]

    row = pl.BlockSpec((ROWS, 512), lambda i: (i, 0))
    return pl.pallas_call(body, name="lru_conv_dx", grid=(T // ROWS,),
                          in_specs=[pl.BlockSpec((4, ROWS, 512), lambda i: (0, i, 0))], out_specs=row,
                          out_shape=jax.ShapeDtypeStruct((T, 512), F32), compiler_params=_cp(("parallel",)))(dxs_shift)


def _position():
    return lax.axis_index("x"), lax.axis_index("y"), lax.axis_index("c")


def _other_chips(x, y):
    return [(1 - x, y), (x, 1 - y), (1 - x, 1 - y)]


def _al(v, n):
    return v * n if isinstance(v, int) else pl.multiple_of(v * n, n)


_AG_ITEMS = [
    ((4, 32, 128), lambda o, s, h: o.at[s, pl.ds(_al(h, 16), 16), :], lambda r, h: r.at[pl.ds(_al(h, 16), 16), :]),
    ((4, 774, 1024), lambda o, s, h: o.at[s, :, pl.ds(_al(h, 512), 512)], lambda r, h: r.at[:, pl.ds(_al(h, 512), 512)]),
    ((1024, 1024), lambda o, s, h: o.at[pl.ds(_al(2 * s + h, 128), 128), :], lambda r, h: r.at[pl.ds(_al(h, 128), 128), :]),
    ((2, 1024, 4096), lambda o, s, h: o.at[h, :, pl.ds(_al(s, 1024), 1024)], lambda r, h: r.at[h]),
    ((2, 4096, 1024), lambda o, s, h: o.at[h, pl.ds(_al(s, 1024), 1024), :], lambda r, h: r.at[h]),
    ((1024, 2560), lambda o, s, h: o.at[pl.ds(_al(h, 512), 512), pl.ds(_al(s, 640), 640)],
     lambda r, h: r.at[pl.ds(_al(h, 512), 512), :]),
    ((1024, 1024), lambda o, s, h: o.at[pl.ds(_al(2 * s + h, 128), 128), :], lambda r, h: r.at[pl.ds(_al(h, 128), 128), :]),
]


_AG_GROUPS = [(0, 1, 2), (3, 4), (5, 6)]

_HBM = pl.BlockSpec(memory_space=pltpu.HBM)
_SEM = pl.BlockSpec(memory_space=pltpu.SEMAPHORE)
_SPLIT = dict(has_side_effects=pltpu.SideEffectType.DATAFLOW_SIDE_EFFECTING)


def _hbm(a):
    return pltpu.with_memory_space_constraint(a, pltpu.HBM)


def _ag_ici_copy(i, j, chip, c, slot, src_ref, land_ref, send_sems, recv_sems, k):
    _, dst, half = _AG_ITEMS[i]
    return pltpu.make_async_remote_copy(src_ref=half(src_ref, c), dst_ref=dst(land_ref, slot, c), send_sem=send_sems.at[k],
                                        recv_sem=recv_sems.at[k], device_id=(*chip, c), device_id_type=MESH)


def _ag_start(shards):
    n = len(_AG_ITEMS)
    ng = len(_AG_GROUPS)
    lands = [lax.empty(shape, s.dtype) for (shape, _, _), s in zip(_AG_ITEMS, shards)]

    def body(*refs):
        srcs, land_refs = refs[:n], refs[n:2 * n]
        sems = refs[2 * n:2 * n + 2 * ng]
        token = refs[-1]
        x, y, c = _position()
        me = 2 * x + y
        for g, items in enumerate(_AG_GROUPS):
            for t, i in enumerate(items):
                for j, chip in enumerate(_other_chips(x, y)):
                    _ag_ici_copy(i, j, chip, c, me, srcs[i], land_refs[i], sems[2 * g], sems[2 * g + 1], 3 * t + j).start()
        token[...] = jnp.zeros_like(token)

    sem_shapes = []
    for items in _AG_GROUPS:
        sem_shapes += [pltpu.SemaphoreType.DMA((3 * len(items),))] * 2
    thru = [pltpu.HBM(a.shape, a.dtype) for a in list(shards) + lands]
    out = pl.pallas_call(
        body, name="allgather_start",
        out_shape=tuple(sem_shapes) + tuple(thru) + (jax.ShapeDtypeStruct((8, 128), F32),),
        in_specs=(_HBM,) * (2 * n),
        out_specs=(_SEM,) * (2 * ng) + (_HBM,) * (2 * n) + (pl.BlockSpec(memory_space=pltpu.VMEM),),
        input_output_aliases={i: 2 * ng + i for i in range(2 * n)},
        compiler_params=pltpu.CompilerParams(**_SPLIT),
    )(*[_hbm(a) for a in list(shards) + lands])
    sems, thru, token = out[:2 * ng], out[2 * ng:-1], out[-1]
    return [(sems[2 * g], sems[2 * g + 1]) for g in range(ng)], list(thru[:n]), list(thru[n:]), token


def _ag_wait(g, sems, srcs, lands, after):
    items = _AG_GROUPS[g]
    m = len(items)

    def body(*refs):
        src_refs, land_refs = refs[:m], refs[m:2 * m]
        send_sems, recv_sems = refs[2 * m], refs[2 * m + 1]
        x, y, c = _position()
        for t, i in enumerate(items):
            for j, chip in enumerate(_other_chips(x, y)):
                cp = _ag_ici_copy(i, j, chip, c, 2 * chip[0] + chip[1], src_refs[t], land_refs[t], send_sems, recv_sems,
                                  3 * t + j)
                cp.wait_send()
                cp.wait_recv()

    ops = [srcs[i] for i in items] + [lands[i] for i in items]
    out = pl.pallas_call(
        body, name=f"allgather_wait_{g}",
        out_shape=tuple(pltpu.HBM(a.shape, a.dtype) for a in ops),
        in_specs=(_HBM,) * (2 * m) + (_SEM, _SEM, pl.BlockSpec(memory_space=pl.ANY)),
        out_specs=(_HBM,) * (2 * m),
        input_output_aliases={i: i for i in range(2 * m)},
        compiler_params=pltpu.CompilerParams(**_SPLIT),
    )(*ops, sems[0], sems[1], after)
    return list(out[:m]), list(out[m:])


def _ag_forward(g, srcs, lands):
    return _ag_sibling(_AG_GROUPS[g], srcs, lands, False, f"allgather_forward_{g}")


def _ag_push_own(srcs, lands):
    return _ag_sibling(tuple(range(len(_AG_ITEMS))), srcs, lands, True, "allgather_push_own")


def _ag_sibling(items, srcs, lands, own, name):
    m = len(items)
    per = 2 if own else 3

    def body(*refs):
        src_refs, in_refs, out_refs = refs[:m], refs[m:2 * m], refs[2 * m:3 * m]
        send_sems, recv_sems = refs[3 * m:]
        x, y, c = _position()
        sibling = (x, y, 1 - c)
        me = 2 * x + y
        if own:
            mine = theirs = [(me, 0), (me, 1)]
        else:
            slots = [2 * chip[0] + chip[1] for chip in _other_chips(x, y)]
            mine, theirs = [(s, c) for s in slots], [(s, 1 - c) for s in slots]
        sends = []
        for t, i in enumerate(items):
            _, dst, half = _AG_ITEMS[i]
            for k, (slot, hc) in enumerate(mine):
                src = half(src_refs[t], hc) if own else dst(in_refs[t], slot, hc)
                sends.append(pltpu.make_async_remote_copy(
                    src_ref=src, dst_ref=dst(out_refs[t], slot, hc), send_sem=send_sems.at[per * t + k],
                    recv_sem=recv_sems.at[per * t + k], device_id=sibling, device_id_type=MESH))
        for cp in sends:
            cp.start()
        for t, i in enumerate(items):
            dst = _AG_ITEMS[i][1]
            for k, (slot, hc) in enumerate(theirs):
                there = dst(out_refs[t], slot, hc)
                pltpu.make_async_remote_copy(src_ref=there, dst_ref=there, send_sem=send_sems.at[per * t + k],
                                             recv_sem=recv_sems.at[per * t + k], device_id=sibling,
                                             device_id_type=MESH).wait_recv()
        for cp in sends:
            cp.wait_send()

    any_spec = pl.BlockSpec(memory_space=pl.ANY)
    return pl.pallas_call(
        body, name=name,
        in_specs=[any_spec] * (2 * m), out_specs=(any_spec,) * m,
        out_shape=tuple(jax.ShapeDtypeStruct(a.shape, a.dtype) for a in lands),
        input_output_aliases={m + t: t for t in range(m)},
        scratch_shapes=[pltpu.SemaphoreType.DMA((per * m,)), pltpu.SemaphoreType.DMA((per * m,))],
    )(*srcs, *lands)


def _pair_swap_copy(g_ref, r_ref, send_sem, recv_sem):
    x, y, c = _position()
    hc = g_ref.shape[2] // 2
    return pltpu.make_async_remote_copy(src_ref=g_ref.at[:, :, pl.ds(_al(1 - c, hc), hc)], dst_ref=r_ref,
                                        send_sem=send_sem, recv_sem=recv_sem, device_id=(x, y, 1 - c),
                                        device_id_type=MESH)


def _pair_swap_start(gb, tag):
    _, rows, cols = gb.shape
    recv = lax.empty((4, rows, cols // 2), gb.dtype)

    def body(g_ref, r_ref, send_sem, recv_sem, g_thru, r_thru, token):
        _pair_swap_copy(g_ref, r_ref, send_sem, recv_sem).start()
        token[...] = jnp.zeros_like(token)

    return pl.pallas_call(
        body, name="grad_pair_swap_start_" + tag,
        out_shape=(pltpu.SemaphoreType.DMA(()), pltpu.SemaphoreType.DMA(()), pltpu.HBM(gb.shape, gb.dtype),
                   pltpu.HBM(recv.shape, recv.dtype), jax.ShapeDtypeStruct((8, 128), F32)),
        in_specs=(_HBM, _HBM), out_specs=(_SEM, _SEM, _HBM, _HBM, pl.BlockSpec(memory_space=pltpu.VMEM)),
        input_output_aliases={0: 2, 1: 3},
        compiler_params=pltpu.CompilerParams(**_SPLIT),
    )(_hbm(gb), _hbm(recv))


def _pair_swap_wait(started, after, tag):
    send_sem, recv_sem, gb, recv, _ = started

    def body(g_ref, r_ref, send_sem, recv_sem, after_ref, g_out, r_out):
        cp = _pair_swap_copy(g_ref, r_ref, send_sem, recv_sem)
        cp.wait_send()
        cp.wait_recv()

    return pl.pallas_call(
        body, name="grad_pair_swap_wait_" + tag,
        out_shape=(pltpu.HBM(gb.shape, gb.dtype), pltpu.HBM(recv.shape, recv.dtype)),
        in_specs=(_HBM, _HBM, _SEM, _SEM, pl.BlockSpec(memory_space=pl.ANY)), out_specs=(_HBM, _HBM),
        input_output_aliases={0: 0, 1: 1},
        compiler_params=pltpu.CompilerParams(**_SPLIT),
    )(gb, recv, send_sem, recv_sem, after)


def _handover_copy(r_ref, send_sem, recv_sem, core):
    x, y, c = _position()
    hc = r_ref.shape[1] // 2
    cols = r_ref.at[:, pl.ds(_al(core, hc), hc)]
    return pltpu.make_async_remote_copy(src_ref=cols, dst_ref=cols, send_sem=send_sem, recv_sem=recv_sem,
                                        device_id=(x, y, 1 - c), device_id_type=MESH)


def _handover_start(red, tag):
    def body(r_ref, send_sem, recv_sem, r_thru, token):
        _handover_copy(r_ref, send_sem, recv_sem, lax.axis_index("c")).start()
        token[...] = jnp.zeros_like(token)

    return pl.pallas_call(
        body, name="grad_handover_start_" + tag,
        out_shape=(pltpu.SemaphoreType.DMA(()), pltpu.SemaphoreType.DMA(()), pltpu.HBM(red.shape, red.dtype),
                   jax.ShapeDtypeStruct((8, 128), F32)),
        in_specs=(_HBM,), out_specs=(_SEM, _SEM, _HBM, pl.BlockSpec(memory_space=pltpu.VMEM)),
        input_output_aliases={0: 2},
        compiler_params=pltpu.CompilerParams(**_SPLIT),
    )(_hbm(red))


def _handover_wait(started, after, tag):
    send_sem, recv_sem, red, _ = started

    def body(r_ref, send_sem, recv_sem, after_ref, r_out):
        c = lax.axis_index("c")
        _handover_copy(r_ref, send_sem, recv_sem, c).wait_send()
        _handover_copy(r_ref, send_sem, recv_sem, 1 - c).wait_recv()

    return pl.pallas_call(
        body, name="grad_handover_wait_" + tag,
        out_shape=pltpu.HBM(red.shape, red.dtype),
        in_specs=(_HBM, _SEM, _SEM, pl.BlockSpec(memory_space=pl.ANY)), out_specs=_HBM,
        input_output_aliases={0: 0},
        compiler_params=pltpu.CompilerParams(**_SPLIT),
    )(red, send_sem, recv_sem, after)


def _handover(red, tag):
    started = _handover_start(red, tag)
    return _handover_wait(started, started[3], tag)


def _a2a_copy(j, chip, c, p_ref, q_ref, q_slot, send_sems, recv_sems):
    return pltpu.make_async_remote_copy(src_ref=p_ref.at[2 * chip[0] + chip[1]], dst_ref=q_ref.at[q_slot],
                                        send_sem=send_sems.at[j], recv_sem=recv_sems.at[j], device_id=(*chip, c),
                                        device_id_type=MESH)


def _a2a_start(p, tag):
    def body(p_ref, q_ref, send_sems, recv_sems, p_thru, q_thru, token):
        x, y, c = _position()
        for j, chip in enumerate(_other_chips(x, y)):
            _a2a_copy(j, chip, c, p_ref, q_ref, 2 * x + y, send_sems, recv_sems).start()
        token[...] = jnp.zeros_like(token)

    return pl.pallas_call(
        body, name="grad_alltoall_start_" + tag,
        out_shape=(pltpu.SemaphoreType.DMA((3,)), pltpu.SemaphoreType.DMA((3,)), pltpu.HBM(p.shape, p.dtype),
                   pltpu.HBM(p.shape, p.dtype), jax.ShapeDtypeStruct((8, 128), F32)),
        in_specs=(_HBM, _HBM), out_specs=(_SEM, _SEM, _HBM, _HBM, pl.BlockSpec(memory_space=pltpu.VMEM)),
        input_output_aliases={0: 2, 1: 3},
        compiler_params=pltpu.CompilerParams(**_SPLIT),
    )(_hbm(p), _hbm(lax.empty(p.shape, p.dtype)))


def _a2a_wait(send_sems, recv_sems, p, q, after, tag):
    def body(p_ref, q_ref, send_sems, recv_sems, after_ref, p_out, q_out):
        x, y, c = _position()
        for j, chip in enumerate(_other_chips(x, y)):
            cp = _a2a_copy(j, chip, c, p_ref, q_ref, 2 * chip[0] + chip[1], send_sems, recv_sems)
            cp.wait_send()
            cp.wait_recv()

    return pl.pallas_call(
        body, name="grad_alltoall_wait_" + tag,
        out_shape=(pltpu.HBM(p.shape, p.dtype), pltpu.HBM(q.shape, q.dtype)),
        in_specs=(_HBM, _HBM, _SEM, _SEM, pl.BlockSpec(memory_space=pl.ANY)), out_specs=(_HBM, _HBM),
        input_output_aliases={0: 0, 1: 1},
        compiler_params=pltpu.CompilerParams(**_SPLIT),
    )(p, q, send_sems, recv_sems, after)


def _comm_rows(rows):
    return next(t for t in (512, 384, 256, 128) if rows % t == 0)


def _pair_add(gb, recv, where, tag):
    _, rows, cols = gb.shape
    hc = cols // 2
    tr = _comm_rows(rows)

    def body(w_ref, g_ref, r_ref, o_ref):
        o_ref[...] = (g_ref[...].astype(F32) + r_ref[...].astype(F32)).astype(o_ref.dtype)

    return pl.pallas_call(
        body, name="grad_pair_add_" + tag,
        grid_spec=pltpu.PrefetchScalarGridSpec(
            num_scalar_prefetch=1, grid=(4, rows // tr),
            in_specs=[pl.BlockSpec((None, tr, hc), lambda s, j, w_ref: (s, j, w_ref[0])),
                      pl.BlockSpec((None, tr, hc), lambda s, j, w_ref: (s, j, 0))],
            out_specs=pl.BlockSpec((None, tr, hc), lambda s, j, w_ref: (s, j, 0))),
        out_shape=jax.ShapeDtypeStruct((4, rows, hc), gb.dtype),
        compiler_params=_cp(("parallel", "parallel")),
    )(where, gb, recv)


def _sum_chips(p, q, where, tag):
    _, rows, hc = q.shape
    tr = _comm_rows(rows)

    def body(w_ref, p_ref, qa_ref, qb_ref, qc_ref, o_ref):
        me = w_ref[1]
        own, qa, qb, qc = (r[...].astype(F32) for r in (p_ref, qa_ref, qb_ref, qc_ref))
        v0 = jnp.where(me == 0, own, qa)
        v1 = jnp.where(me == 1, own, jnp.where(me == 0, qa, qb))
        v2 = jnp.where(me == 2, own, jnp.where(me < 2, qb, qc))
        v3 = jnp.where(me == 3, own, qc)
        o_ref[...] = ((v0 + v1) + v2) + v3

    slot = lambda k: pl.BlockSpec((None, tr, hc), lambda j, w_ref: (w_ref[k], j, 0))
    return pl.pallas_call(
        body, name="grad_sum_chips_" + tag,
        grid_spec=pltpu.PrefetchScalarGridSpec(
            num_scalar_prefetch=1, grid=(rows // tr,),
            in_specs=[slot(1), slot(2), slot(3), slot(4)],
            out_specs=pl.BlockSpec((tr, hc), lambda j, w_ref: (j, w_ref[0]))),
        out_shape=jax.ShapeDtypeStruct((rows, 2 * hc), F32),
        compiler_params=_cp(("parallel",)),
    )(where, p, q, q, q)


def _shard_major(g, axis):
    shape = g.shape
    g = g.reshape(shape[:axis] + (4, shape[axis] // 4) + shape[axis + 1:])
    return jnp.moveaxis(g, axis, 0).reshape(4, -1)


def _unshard(g4, shape, axis):
    n = shape[axis] // 4
    g = g4.reshape((4,) + shape[:axis] + (n,) + shape[axis + 1:])
    return jnp.moveaxis(g, 0, axis).reshape(shape)


def _split(flat, shapes):
    out, off = [], 0
    for shp in shapes:
        n = 1
        for d in shp:
            n *= d
        out.append(flat[..., off:off + n].reshape(flat.shape[:-1] + tuple(shp)))
        off += n
    return out


def _even_rows_to_kernel(wt):
    return jnp.concatenate([wt[:1536], wt[1552:3088], wt[1536:1552], wt[3088:3096],
                            jnp.zeros((PE - 3096, wt.shape[1]), wt.dtype)], axis=0)


def _block_diag(w):
    eye = jnp.eye(8, dtype=w.dtype)
    return (w[:, :, None, :] * eye[:, None, :, None]).reshape(512, 512)


def _diag_blocks(g):
    eye = jnp.eye(8, dtype=g.dtype)
    return (g.reshape(8, 64, 8, 64) * eye[:, None, :, None]).sum(axis=2)


def _shift_down(a, s):
    return a if s == 0 else jnp.pad(a, ((s, 0), (0, 0)))[:a.shape[0]]


def _shift_up(a, s):
    return a if s == 0 else jnp.pad(a, ((0, s), (0, 0)))[s:]


SMALL_SHARDED_SHAPES = [(2, 4, 256), (16, 64), (4, 128), (128,), (128,), (128,), (128,)]
REPL_SHAPES = [(256,), (512,), (8,), (8, 257), (8, 64, 64), (8, 64, 64)]


def kernel(x, norm_w, w_in_even, gla_w_a_up, gla_b_a, gla_norm_w, fox_b_f, w_out_even, w_in_odd, rel_bias, conv_w, conv_b, lru_w_a, lru_b_a, lru_w_x, lru_b_x, lru_lambda, w_out_odd, w_mlp_up, w_mlp_down, loss_target, m_norm_w, m_w_in_even, m_gla_w_a_up, m_gla_b_a, m_gla_norm_w, m_fox_b_f, m_w_out_even, m_w_in_odd, m_rel_bias, m_conv_w, m_conv_b, m_lru_w_a, m_lru_b_a, m_lru_w_x, m_lru_b_x, m_lru_lambda, m_w_out_odd, m_w_mlp_up, m_w_mlp_down, v_norm_w, v_w_in_even, v_gla_w_a_up, v_gla_b_a, v_gla_norm_w, v_fox_b_f, v_w_out_even, v_w_in_odd, v_rel_bias, v_conv_w, v_conv_b, v_lru_w_a, v_lru_b_a, v_lru_w_x, v_lru_b_x, v_lru_lambda, v_w_out_odd, v_w_mlp_up, v_w_mlp_down):
    c_idx = lax.axis_index("c")

    small_local = [norm_w, gla_w_a_up[0], conv_w[0], conv_b[0], lru_b_a[0], lru_b_x[0], lru_lambda[0]]
    small_src = jnp.concatenate([a.reshape(-1) for a in small_local]).reshape(32, 128)
    mine = [small_src, w_in_even[0].T.astype(BF16), w_out_even[0].astype(BF16), w_mlp_up.astype(BF16),
            w_mlp_down.astype(BF16), w_in_odd[0].astype(BF16), w_out_odd[0].astype(BF16)]
    ag_sems, ag_srcs, ag_lands, ag_token = _ag_start(mine)
    ag_lands = list(_ag_push_own(ag_srcs, ag_lands))

    def gathered(g, after):
        srcs_g, lands_g = _ag_wait(g, ag_sems[g], ag_srcs, ag_lands, after)
        return _ag_forward(g, srcs_g, lands_g)

    small4, w_in_e4, w_out_e = gathered(0, ag_token)
    me = 2 * lax.axis_index("x") + lax.axis_index("y")
    others = [k + (k >= me).astype(jnp.int32) for k in range(3)]
    where = jnp.stack([c_idx, me] + others).astype(jnp.int32)

    w_in_e_t = _even_rows_to_kernel(w_in_e4.reshape(3096, D))
    g_small = _split(small4.reshape(4, 32 * 128), SMALL_SHARDED_SHAPES)
    nw_full = _unshard(g_small[0], (2, 4, 1024), 2)
    wa_up = _unshard(g_small[1], (16, 256), 1)
    cw = _unshard(g_small[2], (4, 512), 1)
    cb, lba, lbx, lam = [_unshard(g, (512,), 0).reshape(1, 512) for g in g_small[3:]]
    nw = lambda layer, i: nw_full[layer, i].reshape(1, D)

    wa_pad = jnp.pad(wa_up, ((0, 128 - 16), (0, 0)))
    gla_ba = gla_b_a.reshape(1, 256)
    gla_nw = gla_norm_w.reshape(1, 512)
    fox_bpad = jnp.pad(fox_b_f.reshape(1, 8), ((0, 0), (FOX_LANE0, 128 - FOX_LANE0 - 8)))
    rbp = jnp.pad(rel_bias[0], ((0, 0), (0, REL_PAD - 257)))
    wa_bd = _block_diag(lru_w_a[0])
    wx_bd = _block_diag(lru_w_x[0])

    x0 = x[0]
    tgt = loss_target[0]

    h0 = _prenorm(x0, nw(0, 0), "prenorm_l0_mix")
    proj_e = _mm(h0, w_in_e_t, "nt", tm=2048, tn=640, name="mm_in_even")
    cat0, s_prev = _gla_fwd(proj_e, wa_pad, gla_ba, gla_nw)
    cum_r = _fox_gate_fwd(proj_e, fox_bpad)
    cum_c = cum_r[:, FOX_LANE0:FOX_LANE0 + 8].T
    cat0 = _fox_fwd(proj_e, cum_c, cat0)
    mix0 = _mm(cat0, w_out_e, "nn", tm=2048, tn=512, name="mm_out_even")
    x1, h1 = _post_pre_fwd(x0, mix0, nw(0, 1), nw(0, 2), "post_pre_l0_mix")
    w_up, w_dn = gathered(1, x1)
    a0, r0 = _mm(h1, w_up, "nn", tm=2048, tn=1024, b_layer=0, relu_pair=True, name="mm_up_l0")
    d0 = _mm(a0, w_dn, "nn", tm=1024, tn=512, b_layer=0, name="mm_down_l0")
    x2, h2 = _post_pre_fwd(x1, d0, nw(0, 3), nw(1, 0), "post_pre_l0_mlp")

    w_in_o, w_out_o = gathered(2, x2)
    proj_o = _mm(h2, w_in_o, "nn", tm=2048, tn=640, name="mm_in_odd")
    bias_q = _bias_build(rbp)
    bias = bias_q.transpose(1, 0, 2)
    kvpad = jnp.pad(proj_o[:, 512:1536], ((CA_PAD, 0), (0, 0)))
    cat1 = _ca_fwd(proj_o, kvpad, bias)
    x_in = proj_o[:, 2048:2560]
    xs = jnp.stack([_shift_down(x_in, 3 - j) for j in range(4)])
    lru_a, lru_b = _lru_pre_fwd(xs, cw, cb, wa_bd, lba, wx_bd, lbx, lam)
    hh = _lru_scan_fwd(lru_a, lru_b)
    cat1 = _lru_post_fwd(hh, proj_o, cat1)
    mix1 = _mm(cat1, w_out_o, "nn", tm=2048, tn=512, name="mm_out_odd")
    x3, h3 = _post_pre_fwd(x2, mix1, nw(1, 1), nw(1, 2), "post_pre_l1_mix")
    a1, r1 = _mm(h3, w_up, "nn", tm=2048, tn=1024, b_layer=1, relu_pair=True, name="mm_up_l1")
    d1 = _mm(a1, w_dn, "nn", tm=1024, tn=512, b_layer=1, name="mm_down_l1")
    g4, loss_part, dd1, dnw13 = _post_loss(x3, d1, nw(1, 3), tgt)
    loss = lax.psum(loss_part[0, 0], ("x", "y", "c"))

    def rs_begin(swap, after, tag):
        gb, recv = _pair_swap_wait(swap, after, tag)
        return _a2a_start(_pair_add(gb, recv, where, tag), tag)

    def rs_end(started, after, tag):
        send_sems, recv_sems, p, q, _ = started
        p, q = _a2a_wait(send_sems, recv_sems, p, q, after, tag)
        return _handover(_sum_chips(p, q, where, tag), tag)

    gba = lax.dynamic_update_slice(lax.empty((4, GA_ROWS, D), BF16), jnp.zeros((4, GA_UP - GA_GAP, D), BF16),
                                   (0, GA_GAP, 0))
    gba = _mm(a1, dd1, "tn", tm=512, tn=1024, into=(gba, 1024, GA_DN), name="mm_down_l1_dw")
    du1 = _mm(dd1, w_dn, "nt", tm=2048, tn=1024, b_layer=1, times2=r1, out_dtype=BF16, name="mm_down_l1_dx")
    gba = _mm(du1, h3, "tn", tm=512, tn=1024, into=(gba, 1024, GA_UP), name="mm_up_l1_dw")
    dh3 = _mm(du1, w_up, "nt", tm=1024, tn=512, b_layer=1, name="mm_up_l1_dx")
    g3, dmix1, dnw12, dnw11 = _pre_post_bwd(x3, nw(1, 2), dh3, g4, mix1, nw(1, 1), "pre_post_bwd_l1_mlp")
    gba = _mm(cat1, dmix1, "tn", tm=128, tn=1024, into=(gba, 256, GA_OUT_O), name="mm_out_odd_dw")
    dcat1 = _mm(dmix1, w_out_o, "nt", tm=2048, tn=512, name="mm_out_odd_dx")

    dq_c, dkpad, dvpad, dbias = _ca_bwd(proj_o, kvpad, bias, dcat1)
    g_rel = _bias_grad(jnp.pad(dbias.transpose(1, 0, 2), ((0, 0), (0, 0), (0, BIAS_W - CA_BAND))))[:, :257]
    dhh, dgate = _lru_post_bwd(hh, proj_o, dcat1)
    da_l, db_l = _lru_scan_bwd(_shift_up(lru_a, 1), _shift_down(hh, 1), dhh)
    dxs, g_cw, g_cb, g_wa_bd, g_lba, g_wx_bd, g_lbx, g_lam = _lru_pre_bwd(xs, cw, cb, wa_bd, lba, wx_bd, lbx, lam, da_l, db_l)
    dx_in = _conv_dx(jnp.stack([_shift_up(dxs[j], 3 - j) for j in range(4)]))
    dproj_o = jnp.concatenate([dq_c, dkpad[CA_PAD:], dvpad[CA_PAD:], dgate, dx_in], axis=1).astype(BF16)
    gba = _mm(dproj_o, h2, "tn", tm=128, tn=1024, into=(gba, 640, GA_IN_O), name="mm_in_odd_dw")
    swap_a = _pair_swap_start(gba, "a")
    dh2 = _mm(dproj_o, w_in_o, "nt", tm=1024, tn=512, name="mm_in_odd_dx")
    g2, dd0, dnw10, dnw03 = _pre_post_bwd(x2, nw(1, 0) + swap_a[4][0, 0], dh2, g3, d0, nw(0, 3), "pre_post_bwd_l1_mix")
    rs_a = rs_begin(swap_a, g2, "a")

    gbb = lax.empty((4, GB_ROWS, D), BF16)
    gbb = _mm(a0, dd0, "tn", tm=512, tn=1024, into=(gbb, 1024, GB_DN), name="mm_down_l0_dw")
    du0 = _mm(dd0, w_dn, "nt", tm=2048, tn=1024, b_layer=0, times2=r0, out_dtype=BF16, name="mm_down_l0_dx")
    gbb = _mm(du0, h1, "tn", tm=512, tn=1024, into=(gbb, 1024, GB_UP), name="mm_up_l0_dw")
    swap_b = _pair_swap_start(gbb, "b")
    dh1 = _mm(du0, w_up, "nt", tm=1024, tn=512, b_layer=0, name="mm_up_l0_dx")
    g1, dmix0, dnw02, dnw01 = _pre_post_bwd(x1, nw(0, 2) + (swap_b[4][0, 0] + rs_a[4][0, 0]), dh1, g2, mix0, nw(0, 1),
                                            "pre_post_bwd_l0_mlp")
    rs_b = rs_begin(swap_b, g1, "b")
    gbc = lax.empty((4, GC_ROWS, D), BF16)
    gbc = _mm(cat0, dmix0, "tn", tm=128, tn=1024, into=(gbc, 256, GC_OUT_E), name="mm_out_even_dw")
    dcat0 = _mm(dmix0, w_out_e, "nt", tm=2048, tn=512, name="mm_out_even_dx")

    dq_g, dk_g, dv_g, dr_g, daux_g, g_wa_pad, g_gla_ba, g_gla_nw = _gla_bwd(
        proj_e, s_prev, wa_pad, gla_ba, gla_nw + rs_b[4][0, 0], dcat0)
    dq_f, dk_f, dv_f, dccol = _fox_bwd(proj_e, cum_c, dcat0)
    dccol_t = jnp.pad(dccol.sum(axis=0).T, ((0, 0), (FOX_LANE0, 128 - FOX_LANE0 - 8)))
    daux, g_fox_bpad = _fox_gate_bwd(proj_e, fox_bpad, dccol_t, daux_g)
    dproj_e = jnp.concatenate([dq_g, dk_g, dv_g, dr_g, dq_f, dk_f, dv_f, daux], axis=1).astype(BF16)
    gt_in_e = _mm(dproj_e, h0, "tn", tm=640, tn=1024, out_dtype=BF16, name="mm_in_even_dw")
    dh0 = _mm(dproj_e, w_in_e_t, "nn", tm=1024, tn=512, name="mm_in_even_dx")
    grad_x, dnw00 = _norm_bwd(x0, nw(0, 0), dh0, g1, "prenorm_l0_mix_bwd")

    def rs_reduce(started, after, tag):
        send_sems, recv_sems, p, q, _ = started
        p, q = _a2a_wait(send_sems, recv_sems, p, q, after, tag)
        return _handover_start(_sum_chips(p, q, where, tag), tag)

    ho_a = rs_reduce(rs_a, grad_x, "a")
    ho_b = rs_reduce(rs_b, ho_a[3], "b")

    g_norm = jnp.stack([jnp.concatenate([dnw00, dnw01, dnw02, dnw03]), jnp.concatenate([dnw10, dnw11, dnw12, dnw13])])
    sharded = [(g_norm, 2), (g_wa_pad[:16], 1), (g_cw, 1), (g_cb[0], 0), (g_lba[0], 0), (g_lbx[0], 0), (g_lam[0], 0)]
    replicated = [g_gla_ba[0], g_gla_nw[0], g_fox_bpad[0, FOX_LANE0:FOX_LANE0 + 8], g_rel, _diag_blocks(g_wa_bd),
                  _diag_blocks(g_wx_bd)]
    small4 = jnp.concatenate([_shard_major(g, ax) for g, ax in sharded]
                             + [jnp.broadcast_to(g.reshape(1, -1), (4, g.size)) for g in replicated], axis=1)
    n_small = small4.shape[1]
    small_rows = GC_ROWS - GC_TAIL - 774
    small4 = jnp.pad(small4, ((0, 0), (0, small_rows * D - n_small))).reshape(4, small_rows, D)
    gt_rows = jnp.concatenate([gt_in_e[:1536], gt_in_e[3072:3088], gt_in_e[1536:3072], gt_in_e[3088:3096]], axis=0)
    tail = jnp.concatenate([gt_rows.reshape(4, 774, D), small4.astype(BF16)], axis=1)
    gbc = lax.dynamic_update_slice(gbc, tail, (0, GC_TAIL, 0))
    swap_c = _pair_swap_start(gbc, "c")
    rs_c = rs_begin(swap_c, swap_c[4], "c")

    red_a = _handover_wait(ho_a, rs_c[4], "a")
    red_b = _handover_wait(ho_b, red_a, "b")
    early = dict(
        w_mlp_up=_adamw_from(w_mlp_up, m_w_mlp_up, v_w_mlp_up, [(red_b, GB_UP, True), (red_a, GA_UP, True)], 256,
                             "adamw_w_mlp_up"),
        w_mlp_down=_adamw_from(w_mlp_down, m_w_mlp_down, v_w_mlp_down, [(red_b, GB_DN, False), (red_a, GA_DN, False)],
                               256, "adamw_w_mlp_down"),
        w_in_odd=_adamw_from(w_in_odd, m_w_in_odd, v_w_in_odd, [(red_a, GA_IN_O, True)], 256, "adamw_w_in_odd"),
        w_out_odd=_adamw_from(w_out_odd, m_w_out_odd, v_w_out_odd, [(red_a, GA_OUT_O, False)], 128, "adamw_w_out_odd"))
    red_c = rs_end(rs_c, early["w_out_odd"][3], "c")

    g_small = _split(red_c[GC_TAIL + 774:].reshape(-1)[:n_small], SMALL_SHARDED_SHAPES + REPL_SHAPES)
    g_of = dict(zip(["norm_w", "gla_w_a_up", "conv_w", "conv_b", "lru_b_a", "lru_b_x", "lru_lambda", "gla_b_a",
                     "gla_norm_w", "fox_b_f", "rel_bias", "lru_w_a", "lru_w_x"], g_small))
    g_of.update(w_in_even=red_c[GC_TAIL:GC_TAIL + 774])
    early["w_out_even"] = _adamw_from(w_out_even, m_w_out_even, v_w_out_even, [(red_c, GC_OUT_E, False)], 256,
                                      "adamw_w_out_even")

    names = ["norm_w", "w_in_even", "gla_w_a_up", "gla_b_a", "gla_norm_w", "fox_b_f", "w_out_even", "w_in_odd", "rel_bias",
             "conv_w", "conv_b", "lru_w_a", "lru_b_a", "lru_w_x", "lru_b_x", "lru_lambda", "w_out_odd", "w_mlp_up",
             "w_mlp_down"]
    w_of = dict(norm_w=norm_w, w_in_even=w_in_even, gla_w_a_up=gla_w_a_up, gla_b_a=gla_b_a, gla_norm_w=gla_norm_w,
                fox_b_f=fox_b_f, w_out_even=w_out_even, w_in_odd=w_in_odd, rel_bias=rel_bias, conv_w=conv_w, conv_b=conv_b,
                lru_w_a=lru_w_a, lru_b_a=lru_b_a, lru_w_x=lru_w_x, lru_b_x=lru_b_x, lru_lambda=lru_lambda,
                w_out_odd=w_out_odd, w_mlp_up=w_mlp_up, w_mlp_down=w_mlp_down)
    m_of = dict(norm_w=m_norm_w, w_in_even=m_w_in_even, gla_w_a_up=m_gla_w_a_up, gla_b_a=m_gla_b_a,
                gla_norm_w=m_gla_norm_w, fox_b_f=m_fox_b_f, w_out_even=m_w_out_even, w_in_odd=m_w_in_odd,
                rel_bias=m_rel_bias, conv_w=m_conv_w, conv_b=m_conv_b, lru_w_a=m_lru_w_a, lru_b_a=m_lru_b_a,
                lru_w_x=m_lru_w_x, lru_b_x=m_lru_b_x, lru_lambda=m_lru_lambda, w_out_odd=m_w_out_odd,
                w_mlp_up=m_w_mlp_up, w_mlp_down=m_w_mlp_down)
    v_of = dict(norm_w=v_norm_w, w_in_even=v_w_in_even, gla_w_a_up=v_gla_w_a_up, gla_b_a=v_gla_b_a,
                gla_norm_w=v_gla_norm_w, fox_b_f=v_fox_b_f, w_out_even=v_w_out_even, w_in_odd=v_w_in_odd,
                rel_bias=v_rel_bias, conv_w=v_conv_w, conv_b=v_conv_b, lru_w_a=v_lru_w_a, lru_b_a=v_lru_b_a,
                lru_w_x=v_lru_w_x, lru_b_x=v_lru_b_x, lru_lambda=v_lru_lambda, w_out_odd=v_w_out_odd,
                w_mlp_up=v_w_mlp_up, w_mlp_down=v_w_mlp_down)
    grads, deltas, new_ms, new_vs = [], [], [], []
    for n in names:
        w = w_of[n]
        if n in early:
            g, d, mn, vn = early[n]
            grads.append(g)
            deltas.append(d)
            new_ms.append(mn)
            new_vs.append(vn)
            continue
        if n == "w_in_even":
            to_view = lambda a: a[0].T
            from_view = lambda a: a.T[None]
        else:
            view = w.shape if w.ndim <= 3 else w.shape[-3:]
            to_view = lambda a, view=view: a.reshape(view)
            from_view = lambda a, w=w: a.reshape(w.shape)
        g = g_of[n] if n == "w_in_even" else to_view(g_of[n])
        d, mn, vn = _adamw(to_view(w), g, to_view(m_of[n]), to_view(v_of[n]), "adamw_" + n)
        grads.append(from_view(g))
        deltas.append(from_view(d))
        new_ms.append(from_view(mn))
        new_vs.append(from_view(vn))

    return (loss, grad_x.reshape(1, T, D), *grads, *deltas, *new_ms, *new_vs)
```

```python
import functools

import jax
import jax.numpy as jnp
from jax import lax
from jax.experimental import pallas as pl
from jax.experimental.pallas import tpu as pltpu

F32 = jnp.float32
BF16 = jnp.bfloat16
MESH = pl.DeviceIdType.MESH

T = 2048
D = 1024
DFF = 4096
EPS = 1e-6
CHUNK = 64
NCHUNK = T // CHUNK
PE = 3200
PO = 2560
AUX_BLK = 3072 // 128
FOX_LANE0 = 16
GLA_SCALE = 64 ** -0.5
ATT_SCALE = 64 ** -0.5
NEG = float(jnp.finfo(jnp.float32).min)
CA_BAND = 576
CA_PAD = 512
REL_PAD = 384

VMEM_LIMIT = 48 * 1024 * 1024

ADAM_LR, ADAM_B1, ADAM_B2, ADAM_EPS, ADAM_WD, ADAM_STEP = 0.001, 0.9, 0.999, 1e-08, 0.01, 10

GA_ROWS, GA_IN_O, GA_OUT_O, GA_GAP, GA_UP, GA_DN = 3072, 0, 640, 896, 1024, 2048
GB_ROWS, GB_UP, GB_DN = 2048, 0, 1024
GC_ROWS, GC_OUT_E, GC_TAIL = 1152, 0, 256

_DIMS = {"nn": (((1,), (0,)), ((), ())), "nt": (((1,), (1,)), ((), ())), "tn": (((0,), (0,)), ((), ()))}


def _cp(sem, **kw):
    return pltpu.CompilerParams(dimension_semantics=sem, vmem_limit_bytes=VMEM_LIMIT, **kw)


def _dot(a, b, mode):
    return lax.dot_general(a.astype(BF16), b.astype(BF16), _DIMS[mode], preferred_element_type=F32)


@functools.partial(jax.custom_vjp, nondiff_argnums=(2,))
def bdot(a, b, mode):
    return _dot(a, b, mode)


def _bdot_fwd(a, b, mode):
    return _dot(a, b, mode), (a, b)


def _bdot_bwd(mode, res, g):
    a, b = res
    if mode == "nn":
        da, db = _dot(g, b, "nt"), _dot(a, g, "tn")
    elif mode == "nt":
        da, db = _dot(g, b, "nn"), _dot(g, a, "tn")
    else:
        da, db = _dot(b, g, "nt"), _dot(a, g, "nn")
    return da.astype(a.dtype), db.astype(b.dtype)


bdot.defvjp(_bdot_fwd, _bdot_bwd)


def _hdot_raw(a, b, mode):
    return lax.dot_general(a, b, _DIMS[mode], precision=lax.Precision.HIGHEST, preferred_element_type=F32)


@functools.partial(jax.custom_vjp, nondiff_argnums=(2,))
def hdot(a, b, mode):
    return _hdot_raw(a, b, mode)


def _hdot_fwd(a, b, mode):
    return _hdot_raw(a, b, mode), (a, b)


def _hdot_bwd(mode, res, g):
    a, b = res
    if mode == "nn":
        return _hdot_raw(g, b, "nt"), _hdot_raw(a, g, "tn")
    if mode == "nt":
        return _hdot_raw(g, b, "nn"), _hdot_raw(g, a, "tn")
    return _hdot_raw(b, g, "nt"), _hdot_raw(a, g, "nn")


hdot.defvjp(_hdot_fwd, _hdot_bwd)


def _log_sigmoid(x):
    return jnp.minimum(x, 0.0) - jnp.log(1.0 + jnp.exp(-jnp.abs(x)))


def _sigmoid(x):
    return 1.0 / (1.0 + jnp.exp(-x))


def _expm1(x):
    series = x * (1.0 + x * 0.5 * (1.0 + x * (1.0 / 3.0) * (1.0 + x * 0.25)))
    return jnp.where(jnp.abs(x) < 0.03, series, jnp.exp(x) - 1.0)


def _gelu_tanh(x):
    return 0.5 * x * (1.0 + jnp.tanh(0.7978845608028654 * (x + 0.044715 * x * x * x)))


def _iota(shape, dim):
    return lax.broadcasted_iota(jnp.int32, shape, dim)


def _mm(a, b, mode, *, tm, tn, tk=None, out_dtype=F32, name, b_layer=None, into=None, relu_pair=False, times2=None):
    b2 = b.shape[-2:]
    if mode == "nn":
        (m, k), n = a.shape, b2[1]
    elif mode == "nt":
        (m, k), n = a.shape, b2[0]
    else:
        (k, m), n = a.shape, b2[1]
    tk = k if tk is None else tk
    assert m % tm == 0 and n % tn == 0 and k % tk == 0, (name, a.shape, b.shape)
    nk = k // tk
    if mode == "tn":
        a_spec = pl.BlockSpec((tk, tm), lambda i, j, kk: (kk, i))
    elif m == tm and nk == 1:
        a_spec = pl.BlockSpec((tm, tk), lambda i, j, kk: (i, kk), pipeline_mode=pl.Buffered(1))
    else:
        a_spec = pl.BlockSpec((tm, tk), lambda i, j, kk: (i, kk))
    b_blk = {"nn": (tk, tn), "nt": (tn, tk), "tn": (tk, tn)}[mode]
    b_idx = {"nn": lambda i, j, kk: (kk, j), "nt": lambda i, j, kk: (j, kk), "tn": lambda i, j, kk: (kk, j)}[mode]
    if b_layer is None:
        b_spec = pl.BlockSpec(b_blk, b_idx)
    else:
        b_spec = pl.BlockSpec((None,) + b_blk, lambda i, j, kk: (b_layer,) + b_idx(i, j, kk))

    tile = pl.BlockSpec((tm, tn), lambda i, j, kk: (i, j))
    if into is not None:
        buf, per_slot, row_off = into
        assert m == 4 * per_slot and per_slot % tm == 0 and row_off % tm == 0 and buf.shape[2] == n, (name, buf.shape)
        bps = per_slot // tm
        out_specs = pl.BlockSpec((None, tm, tn), lambda i, j, kk: (i // bps, row_off // tm + i % bps, j))
        out_shape = jax.ShapeDtypeStruct(buf.shape, buf.dtype)
        extra_in, extra_specs, aliases = [buf], [pl.BlockSpec(memory_space=pl.ANY)], {2: 0}
        finish = lambda acc, extra: [acc.astype(buf.dtype)]
    elif relu_pair:
        out_specs = (tile, tile)
        out_shape = (jax.ShapeDtypeStruct((m, n), BF16),) * 2
        extra_in, extra_specs, aliases = [], [], {}

        def finish(acc, extra):
            r = jnp.maximum(acc, 0.0)
            return [(r * r).astype(BF16), r.astype(BF16)]
    elif times2 is not None:
        out_specs = tile
        out_shape = jax.ShapeDtypeStruct((m, n), out_dtype)
        extra_in, extra_specs, aliases = [times2], [tile], {}
        finish = lambda acc, extra: [(acc * (2.0 * extra[...].astype(F32))).astype(out_dtype)]
    else:
        out_specs = tile
        out_shape = jax.ShapeDtypeStruct((m, n), out_dtype)
        extra_in, extra_specs, aliases = [], [], {}
        finish = lambda acc, extra: [acc.astype(out_dtype)]
    n_out = 2 if relu_pair else 1

    def body(*refs):
        a_ref, b_ref = refs[0], refs[1]
        extra = refs[2] if extra_in else None
        o_refs = refs[2 + len(extra_in):2 + len(extra_in) + n_out]

        def store(acc):
            for o_ref, val in zip(o_refs, finish(acc, extra)):
                o_ref[...] = val

        if nk == 1:
            store(_dot(a_ref[...], b_ref[...], mode))
            return
        acc_ref = refs[-1]
        kk = pl.program_id(2)

        @pl.when(kk == 0)
        def _():
            acc_ref[...] = jnp.zeros_like(acc_ref)

        acc_ref[...] += _dot(a_ref[...], b_ref[...], mode)

        @pl.when(kk == nk - 1)
        def _():
            store(acc_ref[...])

    return pl.pallas_call(
        body, name=name, grid=(m // tm, n // tn, nk),
        in_specs=[a_spec, b_spec] + extra_specs,
        out_specs=out_specs, out_shape=out_shape,
        scratch_shapes=[pltpu.VMEM((tm, tn), F32)] if nk > 1 else [],
        input_output_aliases=aliases,
        compiler_params=_cp(("parallel", "parallel", "arbitrary")),
    )(a, b, *extra_in)


ROWS = 512


def _prenorm(x, w, name):
    def body(x_ref, w_ref, o_ref):
        xv = x_ref[...]
        r = lax.rsqrt(jnp.mean(xv * xv, axis=-1, keepdims=True) + EPS)
        o_ref[...] = (xv * r * w_ref[...]).astype(BF16)

    return pl.pallas_call(
        body, name=name, grid=(T // ROWS,),
        in_specs=[pl.BlockSpec((ROWS, D), lambda i: (i, 0)), pl.BlockSpec((1, D), lambda i: (0, 0))],
        out_specs=pl.BlockSpec((ROWS, D), lambda i: (i, 0)),
        out_shape=jax.ShapeDtypeStruct((T, D), BF16),
        compiler_params=_cp(("parallel",)),
    )(x, w)


def _rms(z):
    return lax.rsqrt(jnp.mean(z * z, axis=-1, keepdims=True) + EPS)


def _rms_bwd(z, w, dy):
    r = _rms(z)
    wdy = dy * w
    dz = r * wdy - z * (r * r * r) * jnp.mean(z * wdy, axis=-1, keepdims=True)
    return dz, jnp.sum(dy * z * r, axis=0, keepdims=True)


_ROW = pl.BlockSpec((ROWS, D), lambda i: (i, 0))
_VEC = pl.BlockSpec((1, D), lambda i: (0, 0))


def _post_pre_fwd(x, z, w_post, w_pre, name):
    def body(x_ref, z_ref, wp_ref, wn_ref, x_out, h_out):
        zv = z_ref[...]
        xn = x_ref[...] + zv * _rms(zv) * wp_ref[...]
        x_out[...] = xn
        h_out[...] = (xn * _rms(xn) * wn_ref[...]).astype(BF16)

    return pl.pallas_call(
        body, name=name, grid=(T // ROWS,), in_specs=[_ROW, _ROW, _VEC, _VEC], out_specs=(_ROW, _ROW),
        out_shape=(jax.ShapeDtypeStruct((T, D), F32), jax.ShapeDtypeStruct((T, D), BF16)),
        compiler_params=_cp(("parallel",)),
    )(x, z, w_post, w_pre)


def _post_loss(x, z, w_post, tgt):
    def body(x_ref, z_ref, w_ref, t_ref, g_ref, l_ref, dz_ref, dw_ref):
        @pl.when(pl.program_id(0) == 0)
        def _():
            l_ref[...] = jnp.zeros_like(l_ref)
            dw_ref[...] = jnp.zeros_like(dw_ref)

        zv = z_ref[...]
        e = x_ref[...] + zv * _rms(zv) * w_ref[...] - t_ref[...]
        g = e * (1.0 / D)
        g_ref[...] = g
        l_ref[...] += jnp.sum(e * e) * (0.5 / D)
        dz, dw = _rms_bwd(zv, w_ref[...], g)
        dz_ref[...] = dz.astype(BF16)
        dw_ref[...] += dw

    return pl.pallas_call(
        body, name="postnorm_loss", grid=(T // ROWS,), in_specs=[_ROW, _ROW, _VEC, _ROW],
        out_specs=(_ROW, pl.BlockSpec((1, 128), lambda i: (0, 0)), _ROW, _VEC),
        out_shape=(jax.ShapeDtypeStruct((T, D), F32), jax.ShapeDtypeStruct((1, 128), F32),
                   jax.ShapeDtypeStruct((T, D), BF16), jax.ShapeDtypeStruct((1, D), F32)),
        compiler_params=_cp(("arbitrary",)),
    )(x, z, w_post, tgt)


def _pre_post_bwd(x, w_pre, dh, add, z, w_post, name):
    def body(x_ref, wn_ref, dh_ref, add_ref, z_ref, wp_ref, g_ref, dz_ref, dwn_ref, dwp_ref):
        @pl.when(pl.program_id(0) == 0)
        def _():
            dwn_ref[...] = jnp.zeros_like(dwn_ref)
            dwp_ref[...] = jnp.zeros_like(dwp_ref)

        dx, dwn = _rms_bwd(x_ref[...], wn_ref[...], dh_ref[...])
        g = dx + add_ref[...]
        g_ref[...] = g
        dz, dwp = _rms_bwd(z_ref[...], wp_ref[...], g)
        dz_ref[...] = dz.astype(BF16)
        dwn_ref[...] += dwn
        dwp_ref[...] += dwp

    return pl.pallas_call(
        body, name=name, grid=(T // ROWS,), in_specs=[_ROW, _VEC, _ROW, _ROW, _ROW, _VEC],
        out_specs=(_ROW, _ROW, _VEC, _VEC),
        out_shape=(jax.ShapeDtypeStruct((T, D), F32), jax.ShapeDtypeStruct((T, D), BF16),
                   jax.ShapeDtypeStruct((1, D), F32), jax.ShapeDtypeStruct((1, D), F32)),
        compiler_params=_cp(("arbitrary",)),
    )(x, w_pre, dh, add, z, w_post)


def _norm_bwd(z, w, dy, add, name):
    has_add = add is not None

    def body(*refs):
        if has_add:
            z_ref, w_ref, dy_ref, add_ref, dz_ref, dw_ref = refs
        else:
            z_ref, w_ref, dy_ref, dz_ref, dw_ref = refs
        i = pl.program_id(0)

        @pl.when(i == 0)
        def _():
            dw_ref[...] = jnp.zeros_like(dw_ref)

        zv = z_ref[...].astype(F32)
        dyv = dy_ref[...]
        r = lax.rsqrt(jnp.mean(zv * zv, axis=-1, keepdims=True) + EPS)
        wdy = dyv * w_ref[...]
        dz = r * wdy - zv * (r * r * r) * jnp.mean(zv * wdy, axis=-1, keepdims=True)
        if has_add:
            dz = dz + add_ref[...]
        dz_ref[...] = dz.astype(dz_ref.dtype)
        dw_ref[...] += jnp.sum(dyv * zv * r, axis=0, keepdims=True)

    row = pl.BlockSpec((ROWS, D), lambda i: (i, 0))
    vec = pl.BlockSpec((1, D), lambda i: (0, 0))
    ins = [z, w, dy] + ([add] if has_add else [])
    dz_dtype = F32 if has_add else BF16
    return pl.pallas_call(
        body, name=name, grid=(T // ROWS,),
        in_specs=[row, vec, row] + ([row] if has_add else []),
        out_specs=(row, vec),
        out_shape=(jax.ShapeDtypeStruct((T, D), dz_dtype), jax.ShapeDtypeStruct((1, D), F32)),
        compiler_params=_cp(("arbitrary",)),
    )(*ins)


def _adamw_math(w, g, m, v):
    c1 = 1.0 - ADAM_B1 ** ADAM_STEP
    c2 = 1.0 - ADAM_B2 ** ADAM_STEP
    mn = ADAM_B1 * m + (1.0 - ADAM_B1) * g
    vn = ADAM_B2 * v + (1.0 - ADAM_B2) * (g * g)
    return -ADAM_LR * ((mn / c1) / (jnp.sqrt(vn / c2) + ADAM_EPS) + ADAM_WD * w), mn, vn


def _adamw_from(w, m, v, sources, tr, name):
    layers, rows, cols = w.shape
    assert len(sources) == layers and rows % tr == 0, (name, w.shape)
    g_specs = []
    for layer, (buf, row0, transposed) in enumerate(sources):
        step = lambda l, i, layer=layer: jnp.where(l == layer, i, 0)
        if transposed:
            assert row0 % cols == 0 and buf.shape[1] == rows, (name, row0)
            g_specs.append(pl.BlockSpec((cols, tr), lambda l, i, b=row0 // cols, step=step: (b, step(l, i))))
        else:
            assert row0 % tr == 0 and buf.shape[1] == cols, (name, row0)
            g_specs.append(pl.BlockSpec((tr, cols), lambda l, i, b=row0 // tr, step=step: (b + step(l, i), 0)))

    def body(*refs):
        w_ref, m_ref, v_ref = refs[:3]
        g_refs = refs[3:3 + layers]
        g_out, d_ref, mo_ref, vo_ref = refs[3 + layers:]
        gs = [r[...].T if src[2] else r[...] for r, src in zip(g_refs, sources)]
        g = gs[0] if layers == 1 else jnp.where(pl.program_id(0) == 0, gs[0], gs[1])
        g_out[...] = g
        d_ref[...], mo_ref[...], vo_ref[...] = _adamw_math(w_ref[...], g, m_ref[...], v_ref[...])

    blk = pl.BlockSpec((None, tr, cols), lambda l, i: (l, i, 0))
    sds = jax.ShapeDtypeStruct(w.shape, F32)
    return pl.pallas_call(body, name=name, grid=(layers, rows // tr), in_specs=[blk] * 3 + g_specs,
                          out_specs=(blk,) * 4, out_shape=(sds,) * 4,
                          compiler_params=_cp(("parallel", "parallel")))(w, m, v, *[s[0] for s in sources])


def _adamw(w, g, m, v, name):
    lead = w.shape[:-2]
    assert len(lead) <= 1 and g.shape == w.shape, (name, w.shape, g.shape)
    rows, cols = w.shape[-2:]
    if rows <= 512:
        tr, tc = rows, cols
    elif rows % 256 == 0:
        tr, tc = 256, cols
    else:
        tr, tc = rows, 256
    assert rows % tr == 0 and cols % tc == 0, (name, w.shape)
    c1 = 1.0 - ADAM_B1 ** ADAM_STEP
    c2 = 1.0 - ADAM_B2 ** ADAM_STEP

    def body(w_ref, g_ref, m_ref, v_ref, d_ref, mo_ref, vo_ref):
        gv = g_ref[...]
        mn = ADAM_B1 * m_ref[...] + (1.0 - ADAM_B1) * gv
        vn = ADAM_B2 * v_ref[...] + (1.0 - ADAM_B2) * (gv * gv)
        m_hat = mn / c1
        v_hat = vn / c2
        d_ref[...] = -ADAM_LR * (m_hat / (jnp.sqrt(v_hat) + ADAM_EPS) + ADAM_WD * w_ref[...])
        mo_ref[...] = mn
        vo_ref[...] = vn

    if lead:
        grid = (lead[0], rows // tr, cols // tc)
        blk = pl.BlockSpec((None, tr, tc), lambda l, i, j: (l, i, j))
    else:
        grid = (rows // tr, cols // tc)
        blk = pl.BlockSpec((tr, tc), lambda i, j: (i, j))
    sds = jax.ShapeDtypeStruct(w.shape, F32)
    return pl.pallas_call(body, name=name, grid=grid, in_specs=[blk] * 4, out_specs=(blk,) * 3,
                          out_shape=(sds,) * 3, compiler_params=_cp(("parallel",) * len(grid)))(w, g, m, v)


def _gla_consts():
    ltri = (_iota((CHUNK, CHUNK), 0) >= _iota((CHUNK, CHUNK), 1)).astype(F32)
    ones_c = jnp.ones((CHUNK, 128), F32)
    mask = (_iota((256, 512), 0) // 64 == _iota((256, 512), 1) // 128).astype(F32)
    return ltri, ones_c, mask


def _gla_chunk(consts, q, k, v, r, aux, s_prev, wa, ba, nw):
    ltri, ones_c, mask = consts
    la = _log_sigmoid(bdot(aux, wa, "nn") + ba) * (1.0 / 16.0)
    cum = hdot(ltri, la, "nn")
    total = jnp.sum(la, axis=0, keepdims=True)
    k_dec = k * jnp.exp(total - cum)
    inc = bdot(k_dec, v, "tn") * mask
    dec = jnp.exp(hdot(la, ones_c, "tn"))
    dec = jnp.concatenate([dec, dec, dec, dec], axis=1)
    s_new = dec * s_prev + inc
    o = bdot(q * GLA_SCALE, s_new, "nn")
    parts = []
    for h in range(4):
        oh = o[:, h * 128:(h + 1) * 128]
        parts.append(oh * lax.rsqrt(jnp.mean(oh * oh, axis=-1, keepdims=True) + EPS))
    on = jnp.concatenate(parts, axis=1)
    return s_new, on * nw * (r * _sigmoid(r))


GLA_PER_STEP = 4
GLA_ROWS = GLA_PER_STEP * CHUNK
GLA_STEPS = NCHUNK // GLA_PER_STEP


def _gla_specs(cmap):
    return [pl.BlockSpec((GLA_ROWS, 256), lambda c: (cmap(c), 0)),
            pl.BlockSpec((GLA_ROWS, 256), lambda c: (cmap(c), 1)),
            pl.BlockSpec((GLA_ROWS, 512), lambda c: (cmap(c), 1)),
            pl.BlockSpec((GLA_ROWS, 512), lambda c: (cmap(c), 2)),
            pl.BlockSpec((GLA_ROWS, 128), lambda c: (cmap(c), AUX_BLK))]


def _gla_fwd(proj, wa, ba, nw):
    def body(q_ref, k_ref, v_ref, r_ref, aux_ref, wa_ref, ba_ref, nw_ref, o_ref, sp_ref, s_ref):
        @pl.when(pl.program_id(0) == 0)
        def _():
            s_ref[...] = jnp.zeros_like(s_ref)

        s = s_ref[...]
        consts = _gla_consts()
        outs, states = [], []
        for i in range(GLA_PER_STEP):
            rows = slice(i * CHUNK, (i + 1) * CHUNK)
            states.append(s)
            s, out = _gla_chunk(consts, q_ref[rows, :], k_ref[rows, :], v_ref[rows, :], r_ref[rows, :], aux_ref[rows, :],
                                s, wa_ref[...], ba_ref[...], nw_ref[...])
            outs.append(out)
        s_ref[...] = s
        for i in range(GLA_PER_STEP):
            o_ref[i * CHUNK:(i + 1) * CHUNK, :] = outs[i]
            sp_ref[i] = states[i]

    full = lambda shape: pl.BlockSpec(shape, lambda c: (0,) * len(shape))
    return pl.pallas_call(
        body, name="gla_fwd", grid=(GLA_STEPS,),
        in_specs=_gla_specs(lambda c: c) + [full((128, 256)), full((1, 256)), full((1, 512))],
        out_specs=(pl.BlockSpec((GLA_ROWS, 512), lambda c: (c, 0)),
                   pl.BlockSpec((GLA_PER_STEP, 256, 512), lambda c: (c, 0, 0))),
        out_shape=(jax.ShapeDtypeStruct((T, D), F32), jax.ShapeDtypeStruct((NCHUNK, 256, 512), F32)),
        scratch_shapes=[pltpu.VMEM((256, 512), F32)],
        compiler_params=_cp(("arbitrary",)),
    )(proj, proj, proj, proj, proj, wa, ba, nw)


def _gla_bwd(proj, s_prev_all, wa, ba, nw, dcat):
    rev = lambda c: GLA_STEPS - 1 - c

    def body(q_ref, k_ref, v_ref, r_ref, aux_ref, sp_ref, wa_ref, ba_ref, nw_ref, do_ref,
             dq_ref, dk_ref, dv_ref, dr_ref, daux_ref, dwa_ref, dba_ref, dnw_ref, ds_ref):
        @pl.when(pl.program_id(0) == 0)
        def _():
            ds_ref[...] = jnp.zeros_like(ds_ref)
            dwa_ref[...] = jnp.zeros_like(dwa_ref)
            dba_ref[...] = jnp.zeros_like(dba_ref)
            dnw_ref[...] = jnp.zeros_like(dnw_ref)

        fn = functools.partial(_gla_chunk, _gla_consts())
        ds = ds_ref[...]
        dwa, dba, dnw = dwa_ref[...], dba_ref[...], dnw_ref[...]
        grads = {}
        for i in reversed(range(GLA_PER_STEP)):
            rows = slice(i * CHUNK, (i + 1) * CHUNK)
            _, vjp = jax.vjp(fn, q_ref[rows, :], k_ref[rows, :], v_ref[rows, :], r_ref[rows, :], aux_ref[rows, :],
                             sp_ref[i], wa_ref[...], ba_ref[...], nw_ref[...])
            *grads[i], ds, dwa_i, dba_i, dnw_i = vjp((ds, do_ref[rows, :]))
            dwa, dba, dnw = dwa + dwa_i, dba + dba_i, dnw + dnw_i
        ds_ref[...] = ds
        dwa_ref[...] = dwa
        dba_ref[...] = dba
        dnw_ref[...] = dnw
        for i in range(GLA_PER_STEP):
            rows = slice(i * CHUNK, (i + 1) * CHUNK)
            for ref, g in zip((dq_ref, dk_ref, dv_ref, dr_ref, daux_ref), grads[i]):
                ref[rows, :] = g

    full = lambda shape: pl.BlockSpec(shape, lambda c: (0,) * len(shape))
    blk = lambda w: pl.BlockSpec((GLA_ROWS, w), lambda c: (rev(c), 0))
    sds = lambda *s: jax.ShapeDtypeStruct(s, F32)
    return pl.pallas_call(
        body, name="gla_bwd", grid=(GLA_STEPS,),
        in_specs=_gla_specs(rev) + [pl.BlockSpec((GLA_PER_STEP, 256, 512), lambda c: (rev(c), 0, 0)),
                                    full((128, 256)), full((1, 256)), full((1, 512)), blk(512)],
        out_specs=(blk(256), blk(256), blk(512), blk(512), blk(128), full((128, 256)), full((1, 256)), full((1, 512))),
        out_shape=(sds(T, 256), sds(T, 256), sds(T, 512), sds(T, 512), sds(T, 128),
                   sds(128, 256), sds(1, 256), sds(1, 512)),
        scratch_shapes=[pltpu.VMEM((256, 512), F32)],
        compiler_params=_cp(("arbitrary",)),
    )(proj, proj, proj, proj, proj, s_prev_all, wa, ba, nw, dcat)


GATE_ROWS = 128


def _fox_gate_block(ltri, aux, bpad, carry):
    lf = _log_sigmoid(aux + bpad)
    cum = hdot(ltri, lf, "nn") + carry
    return cum, carry + jnp.sum(lf, axis=0, keepdims=True)


def _gate_ltri():
    return (_iota((GATE_ROWS, GATE_ROWS), 0) >= _iota((GATE_ROWS, GATE_ROWS), 1)).astype(F32)


def _fox_gate_fwd(proj, bpad):
    def body(aux_ref, b_ref, cum_ref, carry_ref):
        i = pl.program_id(0)

        @pl.when(i == 0)
        def _():
            carry_ref[...] = jnp.zeros_like(carry_ref)

        cum, carry = _fox_gate_block(_gate_ltri(), aux_ref[...], b_ref[...], carry_ref[...])
        cum_ref[...] = cum
        carry_ref[...] = carry

    return pl.pallas_call(
        body, name="fox_gate_fwd", grid=(T // GATE_ROWS,),
        in_specs=[pl.BlockSpec((GATE_ROWS, 128), lambda i: (i, AUX_BLK)), pl.BlockSpec((1, 128), lambda i: (0, 0))],
        out_specs=pl.BlockSpec((GATE_ROWS, 128), lambda i: (i, 0)),
        out_shape=jax.ShapeDtypeStruct((T, 128), F32),
        scratch_shapes=[pltpu.VMEM((1, 128), F32)],
        compiler_params=_cp(("arbitrary",)),
    )(proj, bpad)


def _fox_gate_bwd(proj, bpad, dccol_t, daux_gla):
    nb = T // GATE_ROWS
    rev = lambda i: nb - 1 - i

    def body(aux_ref, b_ref, dc_ref, dg_ref, daux_ref, db_ref, dcarry_ref):
        i = pl.program_id(0)

        @pl.when(i == 0)
        def _():
            dcarry_ref[...] = jnp.zeros_like(dcarry_ref)
            db_ref[...] = jnp.zeros_like(db_ref)

        dcum = dc_ref[...]
        fn = functools.partial(_fox_gate_block, _gate_ltri())
        _, vjp = jax.vjp(fn, aux_ref[...], b_ref[...], jnp.zeros((1, 128), F32))
        daux, db, dcarry = vjp((dcum, dcarry_ref[...]))
        daux_ref[...] = daux + dg_ref[...]
        db_ref[...] += db
        dcarry_ref[...] = dcarry

    blk = pl.BlockSpec((GATE_ROWS, 128), lambda i: (rev(i), 0))
    vec = pl.BlockSpec((1, 128), lambda i: (0, 0))
    return pl.pallas_call(
        body, name="fox_gate_bwd", grid=(nb,),
        in_specs=[pl.BlockSpec((GATE_ROWS, 128), lambda i: (rev(i), AUX_BLK)), vec, blk, blk],
        out_specs=(blk, vec),
        out_shape=(jax.ShapeDtypeStruct((T, 128), F32), jax.ShapeDtypeStruct((1, 128), F32)),
        scratch_shapes=[pltpu.VMEM((1, 128), F32)],
        compiler_params=_cp(("arbitrary",)),
    )(proj, bpad, dccol_t, daux_gla)


FOX_Q = 256


FOX_QB = T // FOX_Q


@jax.custom_vjp
def _attend(s, v):
    return _attend_fwd(s, v)[0]


def _attend_fwd(s, v):
    e = jnp.exp(s - jnp.max(s, axis=-1, keepdims=True))
    r = 1.0 / jnp.sum(e, axis=-1, keepdims=True)
    return _dot(e, v, "nn") * r, (e, r, v)


def _attend_bwd(res, do):
    e, r, v = res
    do_r = do * r
    dpr = _dot(do_r, v, "nt")
    ds = e * (dpr - r * jnp.sum(e * dpr, axis=-1, keepdims=True))
    return ds, _dot(e, do_r, "tn").astype(v.dtype)


_attend.defvjp(_attend_fwd, _attend_bwd)


def _fox_block(hp, q, k, v, ccol):
    kl = k.shape[0]
    lane = _iota((FOX_Q, 128), 1)
    tri = jnp.bitwise_and(_iota((2 * FOX_Q, FOX_Q), 0), FOX_Q - 1) >= _iota((2 * FOX_Q, FOX_Q), 1)
    sub = _iota((8, kl), 0)
    qs = q * ATT_SCALE
    q2 = jnp.concatenate([jnp.where(lane < 64, qs, 0.0), jnp.where(lane >= 64, qs, 0.0)], axis=0)
    s = bdot(q2, k, "nt")
    cs = [jnp.sum(jnp.where(sub == 2 * hp + e, ccol, 0.0), axis=0, keepdims=True) for e in range(2)]
    s = jnp.concatenate([s[:FOX_Q] - cs[0], s[FOX_Q:] - cs[1]], axis=0)
    diag = jnp.where(tri, s[:, kl - FOX_Q:], NEG)
    s = diag if kl == FOX_Q else jnp.concatenate([s[:, :kl - FOX_Q], diag], axis=1)
    o2 = _attend(s, v)
    return jnp.where(lane < 64, o2[:FOX_Q], o2[FOX_Q:])


def _fox_in_specs():
    return [pl.BlockSpec((FOX_Q, 128), lambda hp, qb: (qb, 12 + hp)),
            pl.BlockSpec((T, 128), lambda hp, qb: (0, 16 + hp)),
            pl.BlockSpec((T, 128), lambda hp, qb: (0, 20 + hp)),
            pl.BlockSpec((8, T), lambda hp, qb: (0, 0))]


def _fox_fwd(proj, cum_c, cat):
    def body(q_ref, k_ref, v_ref, cc_ref, cat_ref, o_ref):
        qb = pl.program_id(1)
        for g in range(FOX_QB):
            kl = FOX_Q * (g + 1)

            @pl.when(qb == g)
            def _(kl=kl):
                o_ref[...] = _fox_block(pl.program_id(0), q_ref[...], k_ref[0:kl, :], v_ref[0:kl, :], cc_ref[:, 0:kl])

    return pl.pallas_call(
        body, name="fox_fwd", grid=(4, FOX_QB), in_specs=_fox_in_specs() + [pl.BlockSpec(memory_space=pl.ANY)],
        out_specs=pl.BlockSpec((FOX_Q, 128), lambda hp, qb: (qb, 4 + hp)),
        out_shape=jax.ShapeDtypeStruct((T, D), F32), input_output_aliases={4: 0},
        compiler_params=_cp(("parallel", "parallel")),
    )(proj, proj, proj, cum_c, cat)


def _fox_bwd(proj, cum_c, dcat):
    def body(q_ref, k_ref, v_ref, cc_ref, do_ref, dq_ref, dk_ref, dv_ref, dcc_ref):
        qb = pl.program_id(1)

        @pl.when(qb == 0)
        def _():
            dk_ref[...] = jnp.zeros_like(dk_ref)
            dv_ref[...] = jnp.zeros_like(dv_ref)
            dcc_ref[...] = jnp.zeros_like(dcc_ref)

        fn = functools.partial(_fox_block, pl.program_id(0))
        for g in range(FOX_QB):
            kl = FOX_Q * (g + 1)

            @pl.when(qb == g)
            def _(kl=kl):
                _, vjp = jax.vjp(fn, q_ref[...], k_ref[0:kl, :], v_ref[0:kl, :], cc_ref[:, 0:kl])
                dq, dk, dv, dcc = vjp(do_ref[...])
                dq_ref[...] = dq
                dk_ref[0:kl, :] += dk
                dv_ref[0:kl, :] += dv
                dcc_ref[:, 0:kl] += dcc

    sds = lambda *s: jax.ShapeDtypeStruct(s, F32)
    return pl.pallas_call(
        body, name="fox_bwd", grid=(4, FOX_QB),
        in_specs=_fox_in_specs() + [pl.BlockSpec((FOX_Q, 128), lambda hp, qb: (qb, 4 + hp))],
        out_specs=(pl.BlockSpec((FOX_Q, 128), lambda hp, qb: (qb, hp)),
                   pl.BlockSpec((T, 128), lambda hp, qb: (0, hp)),
                   pl.BlockSpec((T, 128), lambda hp, qb: (0, hp)),
                   pl.BlockSpec((None, 8, T), lambda hp, qb: (hp, 0, 0))),
        out_shape=(sds(T, 512), sds(T, 512), sds(T, 512), sds(4, 8, T)),
        compiler_params=_cp(("parallel", "arbitrary")),
    )(proj, proj, proj, cum_c, dcat)


BIAS_W = 640


def _rel_onehot():
    j = _iota((REL_PAD, BIAS_W), 1)
    rel = jnp.clip(CA_PAD + CHUNK - 1 - j, -128, 128) + 128
    return (_iota((REL_PAD, BIAS_W), 0) == rel).astype(F32)


def _bias_build(rbp):
    def body(rb_ref, o_ref):
        f = _hdot_raw(rb_ref[...], _rel_onehot(), "nn")
        for q in range(CHUNK):
            o_ref[q] = pltpu.roll(f, (BIAS_W - (CHUNK - 1 - q)) % BIAS_W, 1)[:, :CA_BAND]

    return pl.pallas_call(body, name="ca_bias_build", out_shape=jax.ShapeDtypeStruct((CHUNK, 8, CA_BAND), F32))(rbp)


def _bias_grad(dbias_q):
    def body(db_ref, o_ref):
        acc = jnp.zeros((8, BIAS_W), F32)
        for q in range(CHUNK):
            acc = acc + pltpu.roll(db_ref[q], CHUNK - 1 - q, 1)
        o_ref[...] = _hdot_raw(acc, _rel_onehot(), "nt")

    return pl.pallas_call(body, name="ca_bias_grad", out_shape=jax.ShapeDtypeStruct((8, REL_PAD), F32))(dbias_q)


def _ca_block(c, masked, q, kb, vb, bias2):
    lane = _iota((CHUNK, 128), 1)
    qs = q * ATT_SCALE
    q2 = jnp.concatenate([jnp.where(lane < 64, qs, 0.0), jnp.where(lane >= 64, qs, 0.0)], axis=0)
    s = bdot(q2, kb, "nt") + bias2.reshape(2 * CHUNK, CA_BAND)
    if masked:
        s = jnp.where((c * CHUNK - CA_PAD + _iota((2 * CHUNK, CA_BAND), 1)) >= 0, s, NEG)
    o2 = _attend(s, vb)
    return jnp.where(lane < 64, o2[:CHUNK], o2[CHUNK:])


CA_PER_STEP = 8
CA_ROWS = CA_PER_STEP * CHUNK
CA_MASKED_STEPS = -(-CA_PAD // CA_ROWS)


def _ca_fwd(proj, kvpad, bias):
    def body(q_ref, k_ref, v_ref, b_ref, o_ref):
        def run(masked):
            outs = []
            for i in range(CA_PER_STEP):
                c = pl.program_id(1) * CA_PER_STEP + i
                band = pl.ds(pl.multiple_of(c * CHUNK, CHUNK), CA_BAND)
                rows = slice(i * CHUNK, (i + 1) * CHUNK)
                outs.append(_ca_block(c, masked, q_ref[rows, :], k_ref[band, :], v_ref[band, :], b_ref[...]))
            for i in range(CA_PER_STEP):
                o_ref[i * CHUNK:(i + 1) * CHUNK, :] = outs[i]

        pl.when(pl.program_id(1) < CA_MASKED_STEPS)(lambda: run(True))
        pl.when(pl.program_id(1) >= CA_MASKED_STEPS)(lambda: run(False))

    return pl.pallas_call(
        body, name="ca_fwd", grid=(4, NCHUNK // CA_PER_STEP),
        in_specs=[pl.BlockSpec((CA_ROWS, 128), lambda hp, c: (c, hp)),
                  pl.BlockSpec((T + CA_PAD, 128), lambda hp, c: (0, hp)),
                  pl.BlockSpec((T + CA_PAD, 128), lambda hp, c: (0, 4 + hp)),
                  pl.BlockSpec((2, CHUNK, CA_BAND), lambda hp, c: (hp, 0, 0))],
        out_specs=pl.BlockSpec((CA_ROWS, 128), lambda hp, c: (c, hp)),
        out_shape=jax.ShapeDtypeStruct((T, D), F32),
        compiler_params=_cp(("parallel", "parallel")),
    )(proj, kvpad, kvpad, bias)


def _ca_bwd(proj, kvpad, bias, dcat):
    def body(q_ref, k_ref, v_ref, b_ref, do_ref, dq_ref, dk_ref, dv_ref, db_ref):
        c = pl.program_id(1)

        @pl.when(c == 0)
        def _():
            dk_ref[...] = jnp.zeros_like(dk_ref)
            dv_ref[...] = jnp.zeros_like(dv_ref)
            db_ref[...] = jnp.zeros_like(db_ref)

        def run(masked):
            grads, bands = [], []
            for i in range(CA_PER_STEP):
                ci = c * CA_PER_STEP + i
                band = pl.ds(pl.multiple_of(ci * CHUNK, CHUNK), CA_BAND)
                rows = slice(i * CHUNK, (i + 1) * CHUNK)
                fn = functools.partial(_ca_block, ci, masked)
                _, vjp = jax.vjp(fn, q_ref[rows, :], k_ref[band, :], v_ref[band, :], b_ref[...])
                grads.append(vjp(do_ref[rows, :]))
                bands.append(band)
            for i, (dq, _, _, _) in enumerate(grads):
                dq_ref[i * CHUNK:(i + 1) * CHUNK, :] = dq
            for band, (_, dkb, dvb, _) in zip(bands, grads):
                dk_ref[band, :] += dkb
                dv_ref[band, :] += dvb
            db_ref[...] += functools.reduce(lambda a, b: a + b, [g[3] for g in grads])

        pl.when(c < CA_MASKED_STEPS)(lambda: run(True))
        pl.when(c >= CA_MASKED_STEPS)(lambda: run(False))

    sds = lambda *s: jax.ShapeDtypeStruct(s, F32)
    padded = lambda: pl.BlockSpec((T + CA_PAD, 128), lambda hp, c: (0, hp))
    return pl.pallas_call(
        body, name="ca_bwd", grid=(4, NCHUNK // CA_PER_STEP),
        in_specs=[pl.BlockSpec((CA_ROWS, 128), lambda hp, c: (c, hp)),
                  pl.BlockSpec((T + CA_PAD, 128), lambda hp, c: (0, hp)),
                  pl.BlockSpec((T + CA_PAD, 128), lambda hp, c: (0, 4 + hp)),
                  pl.BlockSpec((2, CHUNK, CA_BAND), lambda hp, c: (hp, 0, 0)),
                  pl.BlockSpec((CA_ROWS, 128), lambda hp, c: (c, hp))],
        out_specs=(pl.BlockSpec((CA_ROWS, 128), lambda hp, c: (c, hp)), padded(), padded(),
                   pl.BlockSpec((2, CHUNK, CA_BAND), lambda hp, c: (hp, 0, 0))),
        out_shape=(sds(T, 512), sds(T + CA_PAD, 512), sds(T + CA_PAD, 512), sds(8, CHUNK, CA_BAND)),
        compiler_params=_cp(("parallel", "arbitrary")),
    )(proj, kvpad, kvpad, bias, dcat)


def _lru_pre(xs, cw, cb, wa, ba, wx, bx, lam):
    xc = cb + xs[0] * cw[0:1, :] + xs[1] * cw[1:2, :] + xs[2] * cw[2:3, :] + xs[3] * cw[3:4, :]
    ra = _sigmoid(bdot(xc, wa, "nn") + ba)
    ii = _sigmoid(bdot(xc, wx, "nn") + bx)
    la = 8.0 * ra * _log_sigmoid(lam)
    return jnp.exp(la), jnp.sqrt(-_expm1(2.0 * la)) * (ii * xc)


def _lru_pre_specs():
    full = lambda shape: pl.BlockSpec(shape, lambda i: (0,) * len(shape))
    return [pl.BlockSpec((4, ROWS, 512), lambda i: (0, i, 0)), full((4, 512)), full((1, 512)),
            full((512, 512)), full((1, 512)), full((512, 512)), full((1, 512)), full((1, 512))]


def _lru_pre_fwd(xs, cw, cb, wa, ba, wx, bx, lam):
    def body(xs_ref, cw_ref, cb_ref, wa_ref, ba_ref, wx_ref, bx_ref, lam_ref, a_ref, b_ref):
        a, b = _lru_pre(xs_ref[...], cw_ref[...], cb_ref[...], wa_ref[...], ba_ref[...], wx_ref[...], bx_ref[...],
                        lam_ref[...])
        a_ref[...] = a
        b_ref[...] = b

    row = pl.BlockSpec((ROWS, 512), lambda i: (i, 0))
    sds = jax.ShapeDtypeStruct((T, 512), F32)
    return pl.pallas_call(body, name="lru_pre_fwd", grid=(T // ROWS,), in_specs=_lru_pre_specs(),
                          out_specs=(row, row), out_shape=(sds, sds), compiler_params=_cp(("parallel",)),
                          )(xs, cw, cb, wa, ba, wx, bx, lam)


def _lru_pre_bwd(xs, cw, cb, wa, ba, wx, bx, lam, da, db):
    def body(xs_ref, cw_ref, cb_ref, wa_ref, ba_ref, wx_ref, bx_ref, lam_ref, da_ref, db_ref,
             dxs_ref, dcw_ref, dcb_ref, dwa_ref, dba_ref, dwx_ref, dbx_ref, dlam_ref):
        acc = (dcw_ref, dcb_ref, dwa_ref, dba_ref, dwx_ref, dbx_ref, dlam_ref)

        @pl.when(pl.program_id(0) == 0)
        def _():
            for r in acc:
                r[...] = jnp.zeros_like(r)

        _, vjp = jax.vjp(_lru_pre, xs_ref[...], cw_ref[...], cb_ref[...], wa_ref[...], ba_ref[...], wx_ref[...],
                         bx_ref[...], lam_ref[...])
        grads = vjp((da_ref[...], db_ref[...]))
        dxs_ref[...] = grads[0]
        for r, g in zip(acc, grads[1:]):
            r[...] += g

    row = pl.BlockSpec((ROWS, 512), lambda i: (i, 0))
    specs = _lru_pre_specs()
    sds = lambda *s: jax.ShapeDtypeStruct(s, F32)
    return pl.pallas_call(
        body, name="lru_pre_bwd", grid=(T // ROWS,), in_specs=specs + [row, row], out_specs=tuple(specs),
        out_shape=(sds(4, T, 512), sds(4, 512), sds(1, 512), sds(512, 512), sds(1, 512), sds(512, 512), sds(1, 512),
                   sds(1, 512)),
        compiler_params=_cp(("arbitrary",)),
    )(xs, cw, cb, wa, ba, wx, bx, lam, da, db)


SCAN_ROWS = 8


def _scan8(a, b, towards_later):
    row = _iota((SCAN_ROWS, 512), 0)
    for s in (1, 2, 4):
        if towards_later:
            keep, shift = row >= s, s
        else:
            keep, shift = row < SCAN_ROWS - s, SCAN_ROWS - s
        a_s = jnp.where(keep, pltpu.roll(a, shift, 0), 1.0)
        b_s = jnp.where(keep, pltpu.roll(b, shift, 0), 0.0)
        b = a * b_s + b
        a = a * a_s
    return a, b


def _lru_scan_fwd(a, b):
    def body(a_ref, b_ref, h_ref):
        def step(i, carry):
            rows = pl.ds(pl.multiple_of(i * SCAN_ROWS, SCAN_ROWS), SCAN_ROWS)
            a8, b8 = _scan8(a_ref[rows, :], b_ref[rows, :], True)
            h = a8 * carry + b8
            h_ref[rows, :] = h
            return jnp.broadcast_to(h[SCAN_ROWS - 1:, :], (SCAN_ROWS, 512))

        lax.fori_loop(0, T // SCAN_ROWS, step, jnp.zeros((SCAN_ROWS, 512), F32), unroll=2)

    return pl.pallas_call(body, name="lru_scan_fwd", out_shape=jax.ShapeDtypeStruct((T, 512), F32),
                          compiler_params=pltpu.CompilerParams(vmem_limit_bytes=VMEM_LIMIT))(a, b)


def _lru_scan_bwd(a_next, h_prev, dh):
    def body(a_ref, h_ref, dh_ref, da_ref, db_ref):
        def step(i, carry):
            start = T - SCAN_ROWS * (i + 1)
            rows = pl.ds(pl.multiple_of(start, SCAN_ROWS), SCAN_ROWS)
            a8, b8 = _scan8(a_ref[rows, :], dh_ref[rows, :], False)
            g = a8 * carry + b8
            db_ref[rows, :] = g
            da_ref[rows, :] = g * h_ref[rows, :]
            return jnp.broadcast_to(g[:1, :], (SCAN_ROWS, 512))

        lax.fori_loop(0, T // SCAN_ROWS, step, jnp.zeros((SCAN_ROWS, 512), F32), unroll=2)

    sds = jax.ShapeDtypeStruct((T, 512), F32)
    return pl.pallas_call(body, name="lru_scan_bwd", out_shape=(sds, sds),
                          compiler_params=pltpu.CompilerParams(vmem_limit_bytes=VMEM_LIMIT))(a_next, h_prev, dh)


def _lru_post(h, gate):
    return h * _gelu_tanh(gate)


def _lru_post_fwd(h, proj, cat):
    def body(h_ref, g_ref, cat_ref, o_ref):
        o_ref[...] = _lru_post(h_ref[...], g_ref[...])

    row = pl.BlockSpec((ROWS, 512), lambda i: (i, 0))
    return pl.pallas_call(body, name="lru_post_fwd", grid=(T // ROWS,),
                          in_specs=[row, pl.BlockSpec((ROWS, 512), lambda i: (i, 3)), pl.BlockSpec(memory_space=pl.ANY)],
                          out_specs=pl.BlockSpec((ROWS, 512), lambda i: (i, 1)),
                          out_shape=jax.ShapeDtypeStruct((T, D), F32), input_output_aliases={2: 0},
                          compiler_params=_cp(("parallel",)))(h, proj, cat)


def _lru_post_bwd(h, proj, dcat):
    def body(h_ref, g_ref, do_ref, dh_ref, dg_ref):
        _, vjp = jax.vjp(_lru_post, h_ref[...], g_ref[...])
        dh, dg = vjp(do_ref[...])
        dh_ref[...] = dh
        dg_ref[...] = dg

    row = pl.BlockSpec((ROWS, 512), lambda i: (i, 0))
    sds = jax.ShapeDtypeStruct((T, 512), F32)
    return pl.pallas_call(body, name="lru_post_bwd", grid=(T // ROWS,),
                          in_specs=[row, pl.BlockSpec((ROWS, 512), lambda i: (i, 3)),
                                    pl.BlockSpec((ROWS, 512), lambda i: (i, 1))],
                          out_specs=(row, row), out_shape=(sds, sds), compiler_params=_cp(("parallel",)))(h, proj, dcat)


def _conv_dx(dxs_shift):
    def body(d_ref, o_ref):
        o_ref[...] = d_ref[0] + d_ref[1] + d_ref[2] + d_ref[3]

    row = pl.BlockSpec((ROWS, 512), lambda i: (i, 0))
    return pl.pallas_call(body, name="lru_conv_dx", grid=(T // ROWS,),
                          in_specs=[pl.BlockSpec((4, ROWS, 512), lambda i: (0, i, 0))], out_specs=row,
                          out_shape=jax.ShapeDtypeStruct((T, 512), F32), compiler_params=_cp(("parallel",)))(dxs_shift)


def _position():
    return lax.axis_index("x"), lax.axis_index("y"), lax.axis_index("c")


def _other_chips(x, y):
    return [(1 - x, y), (x, 1 - y), (1 - x, 1 - y)]


def _al(v, n):
    return v * n if isinstance(v, int) else pl.multiple_of(v * n, n)


_AG_ITEMS = [
    ((4, 32, 128), lambda o, s, h: o.at[s, pl.ds(_al(h, 16), 16), :], lambda r, h: r.at[pl.ds(_al(h, 16), 16), :]),
    ((4, 774, 1024), lambda o, s, h: o.at[s, :, pl.ds(_al(h, 512), 512)], lambda r, h: r.at[:, pl.ds(_al(h, 512), 512)]),
    ((1024, 1024), lambda o, s, h: o.at[pl.ds(_al(2 * s + h, 128), 128), :], lambda r, h: r.at[pl.ds(_al(h, 128), 128), :]),
    ((2, 1024, 4096), lambda o, s, h: o.at[h, :, pl.ds(_al(s, 1024), 1024)], lambda r, h: r.at[h]),
    ((2, 4096, 1024), lambda o, s, h: o.at[h, pl.ds(_al(s, 1024), 1024), :], lambda r, h: r.at[h]),
    ((1024, 2560), lambda o, s, h: o.at[pl.ds(_al(h, 512), 512), pl.ds(_al(s, 640), 640)],
     lambda r, h: r.at[pl.ds(_al(h, 512), 512), :]),
    ((1024, 1024), lambda o, s, h: o.at[pl.ds(_al(2 * s + h, 128), 128), :], lambda r, h: r.at[pl.ds(_al(h, 128), 128), :]),
]


_AG_GROUPS = [(0, 1, 2), (3, 4), (5, 6)]

_HBM = pl.BlockSpec(memory_space=pltpu.HBM)
_SEM = pl.BlockSpec(memory_space=pltpu.SEMAPHORE)
_SPLIT = dict(has_side_effects=pltpu.SideEffectType.DATAFLOW_SIDE_EFFECTING)


def _hbm(a):
    return pltpu.with_memory_space_constraint(a, pltpu.HBM)


def _ag_ici_copy(i, j, chip, c, slot, src_ref, land_ref, send_sems, recv_sems, k):
    _, dst, half = _AG_ITEMS[i]
    return pltpu.make_async_remote_copy(src_ref=half(src_ref, c), dst_ref=dst(land_ref, slot, c), send_sem=send_sems.at[k],
                                        recv_sem=recv_sems.at[k], device_id=(*chip, c), device_id_type=MESH)


def _ag_start(shards):
    n = len(_AG_ITEMS)
    ng = len(_AG_GROUPS)
    lands = [lax.empty(shape, s.dtype) for (shape, _, _), s in zip(_AG_ITEMS, shards)]

    def body(*refs):
        srcs, land_refs = refs[:n], refs[n:2 * n]
        sems = refs[2 * n:2 * n + 2 * ng]
        token = refs[-1]
        x, y, c = _position()
        me = 2 * x + y
        for g, items in enumerate(_AG_GROUPS):
            for t, i in enumerate(items):
                for j, chip in enumerate(_other_chips(x, y)):
                    _ag_ici_copy(i, j, chip, c, me, srcs[i], land_refs[i], sems[2 * g], sems[2 * g + 1], 3 * t + j).start()
        token[...] = jnp.zeros_like(token)

    sem_shapes = []
    for items in _AG_GROUPS:
        sem_shapes += [pltpu.SemaphoreType.DMA((3 * len(items),))] * 2
    thru = [pltpu.HBM(a.shape, a.dtype) for a in list(shards) + lands]
    out = pl.pallas_call(
        body, name="allgather_start",
        out_shape=tuple(sem_shapes) + tuple(thru) + (jax.ShapeDtypeStruct((8, 128), F32),),
        in_specs=(_HBM,) * (2 * n),
        out_specs=(_SEM,) * (2 * ng) + (_HBM,) * (2 * n) + (pl.BlockSpec(memory_space=pltpu.VMEM),),
        input_output_aliases={i: 2 * ng + i for i in range(2 * n)},
        compiler_params=pltpu.CompilerParams(**_SPLIT),
    )(*[_hbm(a) for a in list(shards) + lands])
    sems, thru, token = out[:2 * ng], out[2 * ng:-1], out[-1]
    return [(sems[2 * g], sems[2 * g + 1]) for g in range(ng)], list(thru[:n]), list(thru[n:]), token


def _ag_wait(g, sems, srcs, lands, after):
    items = _AG_GROUPS[g]
    m = len(items)

    def body(*refs):
        src_refs, land_refs = refs[:m], refs[m:2 * m]
        send_sems, recv_sems = refs[2 * m], refs[2 * m + 1]
        x, y, c = _position()
        for t, i in enumerate(items):
            for j, chip in enumerate(_other_chips(x, y)):
                cp = _ag_ici_copy(i, j, chip, c, 2 * chip[0] + chip[1], src_refs[t], land_refs[t], send_sems, recv_sems,
                                  3 * t + j)
                cp.wait_send()
                cp.wait_recv()

    ops = [srcs[i] for i in items] + [lands[i] for i in items]
    out = pl.pallas_call(
        body, name=f"allgather_wait_{g}",
        out_shape=tuple(pltpu.HBM(a.shape, a.dtype) for a in ops),
        in_specs=(_HBM,) * (2 * m) + (_SEM, _SEM, pl.BlockSpec(memory_space=pl.ANY)),
        out_specs=(_HBM,) * (2 * m),
        input_output_aliases={i: i for i in range(2 * m)},
        compiler_params=pltpu.CompilerParams(**_SPLIT),
    )(*ops, sems[0], sems[1], after)
    return list(out[:m]), list(out[m:])


def _ag_forward(g, srcs, lands):
    return _ag_sibling(_AG_GROUPS[g], srcs, lands, False, f"allgather_forward_{g}")


def _ag_push_own(srcs, lands):
    return _ag_sibling(tuple(range(len(_AG_ITEMS))), srcs, lands, True, "allgather_push_own")


def _ag_sibling(items, srcs, lands, own, name):
    m = len(items)
    per = 2 if own else 3

    def body(*refs):
        src_refs, in_refs, out_refs = refs[:m], refs[m:2 * m], refs[2 * m:3 * m]
        send_sems, recv_sems = refs[3 * m:]
        x, y, c = _position()
        sibling = (x, y, 1 - c)
        me = 2 * x + y
        if own:
            mine = theirs = [(me, 0), (me, 1)]
        else:
            slots = [2 * chip[0] + chip[1] for chip in _other_chips(x, y)]
            mine, theirs = [(s, c) for s in slots], [(s, 1 - c) for s in slots]
        sends = []
        for t, i in enumerate(items):
            _, dst, half = _AG_ITEMS[i]
            for k, (slot, hc) in enumerate(mine):
                src = half(src_refs[t], hc) if own else dst(in_refs[t], slot, hc)
                sends.append(pltpu.make_async_remote_copy(
                    src_ref=src, dst_ref=dst(out_refs[t], slot, hc), send_sem=send_sems.at[per * t + k],
                    recv_sem=recv_sems.at[per * t + k], device_id=sibling, device_id_type=MESH))
        for cp in sends:
            cp.start()
        for t, i in enumerate(items):
            dst = _AG_ITEMS[i][1]
            for k, (slot, hc) in enumerate(theirs):
                there = dst(out_refs[t], slot, hc)
                pltpu.make_async_remote_copy(src_ref=there, dst_ref=there, send_sem=send_sems.at[per * t + k],
                                             recv_sem=recv_sems.at[per * t + k], device_id=sibling,
                                             device_id_type=MESH).wait_recv()
        for cp in sends:
            cp.wait_send()

    any_spec = pl.BlockSpec(memory_space=pl.ANY)
    return pl.pallas_call(
        body, name=name,
        in_specs=[any_spec] * (2 * m), out_specs=(any_spec,) * m,
        out_shape=tuple(jax.ShapeDtypeStruct(a.shape, a.dtype) for a in lands),
        input_output_aliases={m + t: t for t in range(m)},
        scratch_shapes=[pltpu.SemaphoreType.DMA((per * m,)), pltpu.SemaphoreType.DMA((per * m,))],
    )(*srcs, *lands)


def _pair_swap_copy(g_ref, r_ref, send_sem, recv_sem):
    x, y, c = _position()
    hc = g_ref.shape[2] // 2
    return pltpu.make_async_remote_copy(src_ref=g_ref.at[:, :, pl.ds(_al(1 - c, hc), hc)], dst_ref=r_ref,
                                        send_sem=send_sem, recv_sem=recv_sem, device_id=(x, y, 1 - c),
                                        device_id_type=MESH)


def _pair_swap_start(gb, tag):
    _, rows, cols = gb.shape
    recv = lax.empty((4, rows, cols // 2), gb.dtype)

    def body(g_ref, r_ref, send_sem, recv_sem, g_thru, r_thru, token):
        _pair_swap_copy(g_ref, r_ref, send_sem, recv_sem).start()
        token[...] = jnp.zeros_like(token)

    return pl.pallas_call(
        body, name="grad_pair_swap_start_" + tag,
        out_shape=(pltpu.SemaphoreType.DMA(()), pltpu.SemaphoreType.DMA(()), pltpu.HBM(gb.shape, gb.dtype),
                   pltpu.HBM(recv.shape, recv.dtype), jax.ShapeDtypeStruct((8, 128), F32)),
        in_specs=(_HBM, _HBM), out_specs=(_SEM, _SEM, _HBM, _HBM, pl.BlockSpec(memory_space=pltpu.VMEM)),
        input_output_aliases={0: 2, 1: 3},
        compiler_params=pltpu.CompilerParams(**_SPLIT),
    )(_hbm(gb), _hbm(recv))


def _pair_swap_wait(started, after, tag):
    send_sem, recv_sem, gb, recv, _ = started

    def body(g_ref, r_ref, send_sem, recv_sem, after_ref, g_out, r_out):
        cp = _pair_swap_copy(g_ref, r_ref, send_sem, recv_sem)
        cp.wait_send()
        cp.wait_recv()

    return pl.pallas_call(
        body, name="grad_pair_swap_wait_" + tag,
        out_shape=(pltpu.HBM(gb.shape, gb.dtype), pltpu.HBM(recv.shape, recv.dtype)),
        in_specs=(_HBM, _HBM, _SEM, _SEM, pl.BlockSpec(memory_space=pl.ANY)), out_specs=(_HBM, _HBM),
        input_output_aliases={0: 0, 1: 1},
        compiler_params=pltpu.CompilerParams(**_SPLIT),
    )(gb, recv, send_sem, recv_sem, after)


def _handover_copy(r_ref, send_sem, recv_sem, core):
    x, y, c = _position()
    hc = r_ref.shape[1] // 2
    cols = r_ref.at[:, pl.ds(_al(core, hc), hc)]
    return pltpu.make_async_remote_copy(src_ref=cols, dst_ref=cols, send_sem=send_sem, recv_sem=recv_sem,
                                        device_id=(x, y, 1 - c), device_id_type=MESH)


def _handover_start(red, tag):
    def body(r_ref, send_sem, recv_sem, r_thru, token):
        _handover_copy(r_ref, send_sem, recv_sem, lax.axis_index("c")).start()
        token[...] = jnp.zeros_like(token)

    return pl.pallas_call(
        body, name="grad_handover_start_" + tag,
        out_shape=(pltpu.SemaphoreType.DMA(()), pltpu.SemaphoreType.DMA(()), pltpu.HBM(red.shape, red.dtype),
                   jax.ShapeDtypeStruct((8, 128), F32)),
        in_specs=(_HBM,), out_specs=(_SEM, _SEM, _HBM, pl.BlockSpec(memory_space=pltpu.VMEM)),
        input_output_aliases={0: 2},
        compiler_params=pltpu.CompilerParams(**_SPLIT),
    )(_hbm(red))


def _handover_wait(started, after, tag):
    send_sem, recv_sem, red, _ = started

    def body(r_ref, send_sem, recv_sem, after_ref, r_out):
        c = lax.axis_index("c")
        _handover_copy(r_ref, send_sem, recv_sem, c).wait_send()
        _handover_copy(r_ref, send_sem, recv_sem, 1 - c).wait_recv()

    return pl.pallas_call(
        body, name="grad_handover_wait_" + tag,
        out_shape=pltpu.HBM(red.shape, red.dtype),
        in_specs=(_HBM, _SEM, _SEM, pl.BlockSpec(memory_space=pl.ANY)), out_specs=_HBM,
        input_output_aliases={0: 0},
        compiler_params=pltpu.CompilerParams(**_SPLIT),
    )(red, send_sem, recv_sem, after)


def _handover(red, tag):
    started = _handover_start(red, tag)
    return _handover_wait(started, started[3], tag)


def _a2a_copy(j, chip, c, p_ref, q_ref, q_slot, send_sems, recv_sems):
    return pltpu.make_async_remote_copy(src_ref=p_ref.at[2 * chip[0] + chip[1]], dst_ref=q_ref.at[q_slot],
                                        send_sem=send_sems.at[j], recv_sem=recv_sems.at[j], device_id=(*chip, c),
                                        device_id_type=MESH)


def _a2a_start(p, tag):
    def body(p_ref, q_ref, send_sems, recv_sems, p_thru, q_thru, token):
        x, y, c = _position()
        for j, chip in enumerate(_other_chips(x, y)):
            _a2a_copy(j, chip, c, p_ref, q_ref, 2 * x + y, send_sems, recv_sems).start()
        token[...] = jnp.zeros_like(token)

    return pl.pallas_call(
        body, name="grad_alltoall_start_" + tag,
        out_shape=(pltpu.SemaphoreType.DMA((3,)), pltpu.SemaphoreType.DMA((3,)), pltpu.HBM(p.shape, p.dtype),
                   pltpu.HBM(p.shape, p.dtype), jax.ShapeDtypeStruct((8, 128), F32)),
        in_specs=(_HBM, _HBM), out_specs=(_SEM, _SEM, _HBM, _HBM, pl.BlockSpec(memory_space=pltpu.VMEM)),
        input_output_aliases={0: 2, 1: 3},
        compiler_params=pltpu.CompilerParams(**_SPLIT),
    )(_hbm(p), _hbm(lax.empty(p.shape, p.dtype)))


def _a2a_wait(send_sems, recv_sems, p, q, after, tag):
    def body(p_ref, q_ref, send_sems, recv_sems, after_ref, p_out, q_out):
        x, y, c = _position()
        for j, chip in enumerate(_other_chips(x, y)):
            cp = _a2a_copy(j, chip, c, p_ref, q_ref, 2 * chip[0] + chip[1], send_sems, recv_sems)
            cp.wait_send()
            cp.wait_recv()

    return pl.pallas_call(
        body, name="grad_alltoall_wait_" + tag,
        out_shape=(pltpu.HBM(p.shape, p.dtype), pltpu.HBM(q.shape, q.dtype)),
        in_specs=(_HBM, _HBM, _SEM, _SEM, pl.BlockSpec(memory_space=pl.ANY)), out_specs=(_HBM, _HBM),
        input_output_aliases={0: 0, 1: 1},
        compiler_params=pltpu.CompilerParams(**_SPLIT),
    )(p, q, send_sems, recv_sems, after)


def _comm_rows(rows):
    return next(t for t in (512, 384, 256, 128) if rows % t == 0)


def _pair_add(gb, recv, where, tag):
    _, rows, cols = gb.shape
    hc = cols // 2
    tr = _comm_rows(rows)

    def body(w_ref, g_ref, r_ref, o_ref):
        o_ref[...] = (g_ref[...].astype(F32) + r_ref[...].astype(F32)).astype(o_ref.dtype)

    return pl.pallas_call(
        body, name="grad_pair_add_" + tag,
        grid_spec=pltpu.PrefetchScalarGridSpec(
            num_scalar_prefetch=1, grid=(4, rows // tr),
            in_specs=[pl.BlockSpec((None, tr, hc), lambda s, j, w_ref: (s, j, w_ref[0])),
                      pl.BlockSpec((None, tr, hc), lambda s, j, w_ref: (s, j, 0))],
            out_specs=pl.BlockSpec((None, tr, hc), lambda s, j, w_ref: (s, j, 0))),
        out_shape=jax.ShapeDtypeStruct((4, rows, hc), gb.dtype),
        compiler_params=_cp(("parallel", "parallel")),
    )(where, gb, recv)


def _sum_chips(p, q, where, tag):
    _, rows, hc = q.shape
    tr = _comm_rows(rows)

    def body(w_ref, p_ref, qa_ref, qb_ref, qc_ref, o_ref):
        me = w_ref[1]
        own, qa, qb, qc = (r[...].astype(F32) for r in (p_ref, qa_ref, qb_ref, qc_ref))
        v0 = jnp.where(me == 0, own, qa)
        v1 = jnp.where(me == 1, own, jnp.where(me == 0, qa, qb))
        v2 = jnp.where(me == 2, own, jnp.where(me < 2, qb, qc))
        v3 = jnp.where(me == 3, own, qc)
        o_ref[...] = ((v0 + v1) + v2) + v3

    slot = lambda k: pl.BlockSpec((None, tr, hc), lambda j, w_ref: (w_ref[k], j, 0))
    return pl.pallas_call(
        body, name="grad_sum_chips_" + tag,
        grid_spec=pltpu.PrefetchScalarGridSpec(
            num_scalar_prefetch=1, grid=(rows // tr,),
            in_specs=[slot(1), slot(2), slot(3), slot(4)],
            out_specs=pl.BlockSpec((tr, hc), lambda j, w_ref: (j, w_ref[0]))),
        out_shape=jax.ShapeDtypeStruct((rows, 2 * hc), F32),
        compiler_params=_cp(("parallel",)),
    )(where, p, q, q, q)


def _shard_major(g, axis):
    shape = g.shape
    g = g.reshape(shape[:axis] + (4, shape[axis] // 4) + shape[axis + 1:])
    return jnp.moveaxis(g, axis, 0).reshape(4, -1)


def _unshard(g4, shape, axis):
    n = shape[axis] // 4
    g = g4.reshape((4,) + shape[:axis] + (n,) + shape[axis + 1:])
    return jnp.moveaxis(g, 0, axis).reshape(shape)


def _split(flat, shapes):
    out, off = [], 0
    for shp in shapes:
        n = 1
        for d in shp:
            n *= d
        out.append(flat[..., off:off + n].reshape(flat.shape[:-1] + tuple(shp)))
        off += n
    return out


def _even_rows_to_kernel(wt):
    return jnp.concatenate([wt[:1536], wt[1552:3088], wt[1536:1552], wt[3088:3096],
                            jnp.zeros((PE - 3096, wt.shape[1]), wt.dtype)], axis=0)


def _block_diag(w):
    eye = jnp.eye(8, dtype=w.dtype)
    return (w[:, :, None, :] * eye[:, None, :, None]).reshape(512, 512)


def _diag_blocks(g):
    eye = jnp.eye(8, dtype=g.dtype)
    return (g.reshape(8, 64, 8, 64) * eye[:, None, :, None]).sum(axis=2)


def _shift_down(a, s):
    return a if s == 0 else jnp.pad(a, ((s, 0), (0, 0)))[:a.shape[0]]


def _shift_up(a, s):
    return a if s == 0 else jnp.pad(a, ((0, s), (0, 0)))[s:]


SMALL_SHARDED_SHAPES = [(2, 4, 256), (16, 64), (4, 128), (128,), (128,), (128,), (128,)]
REPL_SHAPES = [(256,), (512,), (8,), (8, 257), (8, 64, 64), (8, 64, 64)]


def kernel(x, norm_w, w_in_even, gla_w_a_up, gla_b_a, gla_norm_w, fox_b_f, w_out_even, w_in_odd, rel_bias, conv_w, conv_b, lru_w_a, lru_b_a, lru_w_x, lru_b_x, lru_lambda, w_out_odd, w_mlp_up, w_mlp_down, loss_target, m_norm_w, m_w_in_even, m_gla_w_a_up, m_gla_b_a, m_gla_norm_w, m_fox_b_f, m_w_out_even, m_w_in_odd, m_rel_bias, m_conv_w, m_conv_b, m_lru_w_a, m_lru_b_a, m_lru_w_x, m_lru_b_x, m_lru_lambda, m_w_out_odd, m_w_mlp_up, m_w_mlp_down, v_norm_w, v_w_in_even, v_gla_w_a_up, v_gla_b_a, v_gla_norm_w, v_fox_b_f, v_w_out_even, v_w_in_odd, v_rel_bias, v_conv_w, v_conv_b, v_lru_w_a, v_lru_b_a, v_lru_w_x, v_lru_b_x, v_lru_lambda, v_w_out_odd, v_w_mlp_up, v_w_mlp_down):
    c_idx = lax.axis_index("c")

    small_local = [norm_w, gla_w_a_up[0], conv_w[0], conv_b[0], lru_b_a[0], lru_b_x[0], lru_lambda[0]]
    small_src = jnp.concatenate([a.reshape(-1) for a in small_local]).reshape(32, 128)
    mine = [small_src, w_in_even[0].T.astype(BF16), w_out_even[0].astype(BF16), w_mlp_up.astype(BF16),
            w_mlp_down.astype(BF16), w_in_odd[0].astype(BF16), w_out_odd[0].astype(BF16)]
    ag_sems, ag_srcs, ag_lands, ag_token = _ag_start(mine)
    ag_lands = list(_ag_push_own(ag_srcs, ag_lands))

    def gathered(g, after):
        srcs_g, lands_g = _ag_wait(g, ag_sems[g], ag_srcs, ag_lands, after)
        return _ag_forward(g, srcs_g, lands_g)

    small4, w_in_e4, w_out_e = gathered(0, ag_token)
    me = 2 * lax.axis_index("x") + lax.axis_index("y")
    others = [k + (k >= me).astype(jnp.int32) for k in range(3)]
    where = jnp.stack([c_idx, me] + others).astype(jnp.int32)

    w_in_e_t = _even_rows_to_kernel(w_in_e4.reshape(3096, D))
    g_small = _split(small4.reshape(4, 32 * 128), SMALL_SHARDED_SHAPES)
    nw_full = _unshard(g_small[0], (2, 4, 1024), 2)
    wa_up = _unshard(g_small[1], (16, 256), 1)
    cw = _unshard(g_small[2], (4, 512), 1)
    cb, lba, lbx, lam = [_unshard(g, (512,), 0).reshape(1, 512) for g in g_small[3:]]
    nw = lambda layer, i: nw_full[layer, i].reshape(1, D)

    wa_pad = jnp.pad(wa_up, ((0, 128 - 16), (0, 0)))
    gla_ba = gla_b_a.reshape(1, 256)
    gla_nw = gla_norm_w.reshape(1, 512)
    fox_bpad = jnp.pad(fox_b_f.reshape(1, 8), ((0, 0), (FOX_LANE0, 128 - FOX_LANE0 - 8)))
    rbp = jnp.pad(rel_bias[0], ((0, 0), (0, REL_PAD - 257)))
    wa_bd = _block_diag(lru_w_a[0])
    wx_bd = _block_diag(lru_w_x[0])

    x0 = x[0]
    tgt = loss_target[0]

    h0 = _prenorm(x0, nw(0, 0), "prenorm_l0_mix")
    proj_e = _mm(h0, w_in_e_t, "nt", tm=2048, tn=640, name="mm_in_even")
    cat0, s_prev = _gla_fwd(proj_e, wa_pad, gla_ba, gla_nw)
    cum_r = _fox_gate_fwd(proj_e, fox_bpad)
    cum_c = cum_r[:, FOX_LANE0:FOX_LANE0 + 8].T
    cat0 = _fox_fwd(proj_e, cum_c, cat0)
    mix0 = _mm(cat0, w_out_e, "nn", tm=2048, tn=512, name="mm_out_even")
    x1, h1 = _post_pre_fwd(x0, mix0, nw(0, 1), nw(0, 2), "post_pre_l0_mix")
    w_up, w_dn = gathered(1, x1)
    a0, r0 = _mm(h1, w_up, "nn", tm=2048, tn=1024, b_layer=0, relu_pair=True, name="mm_up_l0")
    d0 = _mm(a0, w_dn, "nn", tm=1024, tn=512, b_layer=0, name="mm_down_l0")
    x2, h2 = _post_pre_fwd(x1, d0, nw(0, 3), nw(1, 0), "post_pre_l0_mlp")

    w_in_o, w_out_o = gathered(2, x2)
    proj_o = _mm(h2, w_in_o, "nn", tm=2048, tn=640, name="mm_in_odd")
    bias_q = _bias_build(rbp)
    bias = bias_q.transpose(1, 0, 2)
    kvpad = jnp.pad(proj_o[:, 512:1536], ((CA_PAD, 0), (0, 0)))
    cat1 = _ca_fwd(proj_o, kvpad, bias)
    x_in = proj_o[:, 2048:2560]
    xs = jnp.stack([_shift_down(x_in, 3 - j) for j in range(4)])
    lru_a, lru_b = _lru_pre_fwd(xs, cw, cb, wa_bd, lba, wx_bd, lbx, lam)
    hh = _lru_scan_fwd(lru_a, lru_b)
    cat1 = _lru_post_fwd(hh, proj_o, cat1)
    mix1 = _mm(cat1, w_out_o, "nn", tm=2048, tn=512, name="mm_out_odd")
    x3, h3 = _post_pre_fwd(x2, mix1, nw(1, 1), nw(1, 2), "post_pre_l1_mix")
    a1, r1 = _mm(h3, w_up, "nn", tm=2048, tn=1024, b_layer=1, relu_pair=True, name="mm_up_l1")
    d1 = _mm(a1, w_dn, "nn", tm=1024, tn=512, b_layer=1, name="mm_down_l1")
    g4, loss_part, dd1, dnw13 = _post_loss(x3, d1, nw(1, 3), tgt)
    loss = lax.psum(loss_part[0, 0], ("x", "y", "c"))

    def rs_begin(swap, after, tag):
        gb, recv = _pair_swap_wait(swap, after, tag)
        return _a2a_start(_pair_add(gb, recv, where, tag), tag)

    def rs_end(started, after, tag):
        send_sems, recv_sems, p, q, _ = started
        p, q = _a2a_wait(send_sems, recv_sems, p, q, after, tag)
        return _handover(_sum_chips(p, q, where, tag), tag)

    gba = lax.dynamic_update_slice(lax.empty((4, GA_ROWS, D), BF16), jnp.zeros((4, GA_UP - GA_GAP, D), BF16),
                                   (0, GA_GAP, 0))
    gba = _mm(a1, dd1, "tn", tm=512, tn=1024, into=(gba, 1024, GA_DN), name="mm_down_l1_dw")
    du1 = _mm(dd1, w_dn, "nt", tm=2048, tn=1024, b_layer=1, times2=r1, out_dtype=BF16, name="mm_down_l1_dx")
    gba = _mm(du1, h3, "tn", tm=512, tn=1024, into=(gba, 1024, GA_UP), name="mm_up_l1_dw")
    dh3 = _mm(du1, w_up, "nt", tm=1024, tn=512, b_layer=1, name="mm_up_l1_dx")
    g3, dmix1, dnw12, dnw11 = _pre_post_bwd(x3, nw(1, 2), dh3, g4, mix1, nw(1, 1), "pre_post_bwd_l1_mlp")
    gba = _mm(cat1, dmix1, "tn", tm=128, tn=1024, into=(gba, 256, GA_OUT_O), name="mm_out_odd_dw")
    dcat1 = _mm(dmix1, w_out_o, "nt", tm=2048, tn=512, name="mm_out_odd_dx")

    dq_c, dkpad, dvpad, dbias = _ca_bwd(proj_o, kvpad, bias, dcat1)
    g_rel = _bias_grad(jnp.pad(dbias.transpose(1, 0, 2), ((0, 0), (0, 0), (0, BIAS_W - CA_BAND))))[:, :257]
    dhh, dgate = _lru_post_bwd(hh, proj_o, dcat1)
    da_l, db_l = _lru_scan_bwd(_shift_up(lru_a, 1), _shift_down(hh, 1), dhh)
    dxs, g_cw, g_cb, g_wa_bd, g_lba, g_wx_bd, g_lbx, g_lam = _lru_pre_bwd(xs, cw, cb, wa_bd, lba, wx_bd, lbx, lam, da_l, db_l)
    dx_in = _conv_dx(jnp.stack([_shift_up(dxs[j], 3 - j) for j in range(4)]))
    dproj_o = jnp.concatenate([dq_c, dkpad[CA_PAD:], dvpad[CA_PAD:], dgate, dx_in], axis=1).astype(BF16)
    gba = _mm(dproj_o, h2, "tn", tm=128, tn=1024, into=(gba, 640, GA_IN_O), name="mm_in_odd_dw")
    swap_a = _pair_swap_start(gba, "a")
    dh2 = _mm(dproj_o, w_in_o, "nt", tm=1024, tn=512, name="mm_in_odd_dx")
    g2, dd0, dnw10, dnw03 = _pre_post_bwd(x2, nw(1, 0) + swap_a[4][0, 0], dh2, g3, d0, nw(0, 3), "pre_post_bwd_l1_mix")
    rs_a = rs_begin(swap_a, g2, "a")

    gbb = lax.empty((4, GB_ROWS, D), BF16)
    gbb = _mm(a0, dd0, "tn", tm=512, tn=1024, into=(gbb, 1024, GB_DN), name="mm_down_l0_dw")
    du0 = _mm(dd0, w_dn, "nt", tm=2048, tn=1024, b_layer=0, times2=r0, out_dtype=BF16, name="mm_down_l0_dx")
    gbb = _mm(du0, h1, "tn", tm=512, tn=1024, into=(gbb, 1024, GB_UP), name="mm_up_l0_dw")
    swap_b = _pair_swap_start(gbb, "b")
    dh1 = _mm(du0, w_up, "nt", tm=1024, tn=512, b_layer=0, name="mm_up_l0_dx")
    g1, dmix0, dnw02, dnw01 = _pre_post_bwd(x1, nw(0, 2) + (swap_b[4][0, 0] + rs_a[4][0, 0]), dh1, g2, mix0, nw(0, 1),
                                            "pre_post_bwd_l0_mlp")
    rs_b = rs_begin(swap_b, g1, "b")
    gbc = lax.empty((4, GC_ROWS, D), BF16)
    gbc = _mm(cat0, dmix0, "tn", tm=128, tn=1024, into=(gbc, 256, GC_OUT_E), name="mm_out_even_dw")
    dcat0 = _mm(dmix0, w_out_e, "nt", tm=2048, tn=512, name="mm_out_even_dx")

    dq_g, dk_g, dv_g, dr_g, daux_g, g_wa_pad, g_gla_ba, g_gla_nw = _gla_bwd(
        proj_e, s_prev, wa_pad, gla_ba, gla_nw + rs_b[4][0, 0], dcat0)
    dq_f, dk_f, dv_f, dccol = _fox_bwd(proj_e, cum_c, dcat0)
    dccol_t = jnp.pad(dccol.sum(axis=0).T, ((0, 0), (FOX_LANE0, 128 - FOX_LANE0 - 8)))
    daux, g_fox_bpad = _fox_gate_bwd(proj_e, fox_bpad, dccol_t, daux_g)
    dproj_e = jnp.concatenate([dq_g, dk_g, dv_g, dr_g, dq_f, dk_f, dv_f, daux], axis=1).astype(BF16)
    gt_in_e = _mm(dproj_e, h0, "tn", tm=640, tn=1024, out_dtype=BF16, name="mm_in_even_dw")
    dh0 = _mm(dproj_e, w_in_e_t, "nn", tm=1024, tn=512, name="mm_in_even_dx")
    grad_x, dnw00 = _norm_bwd(x0, nw(0, 0), dh0, g1, "prenorm_l0_mix_bwd")

    def rs_reduce(started, after, tag):
        send_sems, recv_sems, p, q, _ = started
        p, q = _a2a_wait(send_sems, recv_sems, p, q, after, tag)
        return _handover_start(_sum_chips(p, q, where, tag), tag)

    ho_a = rs_reduce(rs_a, grad_x, "a")
    ho_b = rs_reduce(rs_b, ho_a[3], "b")

    g_norm = jnp.stack([jnp.concatenate([dnw00, dnw01, dnw02, dnw03]), jnp.concatenate([dnw10, dnw11, dnw12, dnw13])])
    sharded = [(g_norm, 2), (g_wa_pad[:16], 1), (g_cw, 1), (g_cb[0], 0), (g_lba[0], 0), (g_lbx[0], 0), (g_lam[0], 0)]
    replicated = [g_gla_ba[0], g_gla_nw[0], g_fox_bpad[0, FOX_LANE0:FOX_LANE0 + 8], g_rel, _diag_blocks(g_wa_bd),
                  _diag_blocks(g_wx_bd)]
    small4 = jnp.concatenate([_shard_major(g, ax) for g, ax in sharded]
                             + [jnp.broadcast_to(g.reshape(1, -1), (4, g.size)) for g in replicated], axis=1)
    n_small = small4.shape[1]
    small_rows = GC_ROWS - GC_TAIL - 774
    small4 = jnp.pad(small4, ((0, 0), (0, small_rows * D - n_small))).reshape(4, small_rows, D)
    gt_rows = jnp.concatenate([gt_in_e[:1536], gt_in_e[3072:3088], gt_in_e[1536:3072], gt_in_e[3088:3096]], axis=0)
    tail = jnp.concatenate([gt_rows.reshape(4, 774, D), small4.astype(BF16)], axis=1)
    gbc = lax.dynamic_update_slice(gbc, tail, (0, GC_TAIL, 0))
    swap_c = _pair_swap_start(gbc, "c")
    rs_c = rs_begin(swap_c, swap_c[4], "c")

    red_a = _handover_wait(ho_a, rs_c[4], "a")
    red_b = _handover_wait(ho_b, red_a, "b")
    early = dict(
        w_mlp_up=_adamw_from(w_mlp_up, m_w_mlp_up, v_w_mlp_up, [(red_b, GB_UP, True), (red_a, GA_UP, True)], 256,
                             "adamw_w_mlp_up"),
        w_mlp_down=_adamw_from(w_mlp_down, m_w_mlp_down, v_w_mlp_down, [(red_b, GB_DN, False), (red_a, GA_DN, False)],
                               256, "adamw_w_mlp_down"),
        w_in_odd=_adamw_from(w_in_odd, m_w_in_odd, v_w_in_odd, [(red_a, GA_IN_O, True)], 256, "adamw_w_in_odd"),
        w_out_odd=_adamw_from(w_out_odd, m_w_out_odd, v_w_out_odd, [(red_a, GA_OUT_O, False)], 128, "adamw_w_out_odd"))
    red_c = rs_end(rs_c, early["w_out_odd"][3], "c")

    g_small = _split(red_c[GC_TAIL + 774:].reshape(-1)[:n_small], SMALL_SHARDED_SHAPES + REPL_SHAPES)
    g_of = dict(zip(["norm_w", "gla_w_a_up", "conv_w", "conv_b", "lru_b_a", "lru_b_x", "lru_lambda", "gla_b_a",
                     "gla_norm_w", "fox_b_f", "rel_bias", "lru_w_a", "lru_w_x"], g_small))
    g_of.update(w_in_even=red_c[GC_TAIL:GC_TAIL + 774])
    early["w_out_even"] = _adamw_from(w_out_even, m_w_out_even, v_w_out_even, [(red_c, GC_OUT_E, False)], 256,
                                      "adamw_w_out_even")

    names = ["norm_w", "w_in_even", "gla_w_a_up", "gla_b_a", "gla_norm_w", "fox_b_f", "w_out_even", "w_in_odd", "rel_bias",
             "conv_w", "conv_b", "lru_w_a", "lru_b_a", "lru_w_x", "lru_b_x", "lru_lambda", "w_out_odd", "w_mlp_up",
             "w_mlp_down"]
    w_of = dict(norm_w=norm_w, w_in_even=w_in_even, gla_w_a_up=gla_w_a_up, gla_b_a=gla_b_a, gla_norm_w=gla_norm_w,
                fox_b_f=fox_b_f, w_out_even=w_out_even, w_in_odd=w_in_odd, rel_bias=rel_bias, conv_w=conv_w, conv_b=conv_b,
                lru_w_a=lru_w_a, lru_b_a=lru_b_a, lru_w_x=lru_w_x, lru_b_x=lru_b_x, lru_lambda=lru_lambda,
                w_out_odd=w_out_odd, w_mlp_up=w_mlp_up, w_mlp_down=w_mlp_down)
    m_of = dict(norm_w=m_norm_w, w_in_even=m_w_in_even, gla_w_a_up=m_gla_w_a_up, gla_b_a=m_gla_b_a,
                gla_norm_w=m_gla_norm_w, fox_b_f=m_fox_b_f, w_out_even=m_w_out_even, w_in_odd=m_w_in_odd,
                rel_bias=m_rel_bias, conv_w=m_conv_w, conv_b=m_conv_b, lru_w_a=m_lru_w_a, lru_b_a=m_lru_b_a,
                lru_w_x=m_lru_w_x, lru_b_x=m_lru_b_x, lru_lambda=m_lru_lambda, w_out_odd=m_w_out_odd,
                w_mlp_up=m_w_mlp_up, w_mlp_down=m_w_mlp_down)
    v_of = dict(norm_w=v_norm_w, w_in_even=v_w_in_even, gla_w_a_up=v_gla_w_a_up, gla_b_a=v_gla_b_a,
                gla_norm_w=v_gla_norm_w, fox_b_f=v_fox_b_f, w_out_even=v_w_out_even, w_in_odd=v_w_in_odd,
                rel_bias=v_rel_bias, conv_w=v_conv_w, conv_b=v_conv_b, lru_w_a=v_lru_w_a, lru_b_a=v_lru_b_a,
                lru_w_x=v_lru_w_x, lru_b_x=v_lru_b_x, lru_lambda=v_lru_lambda, w_out_odd=v_w_out_odd,
                w_mlp_up=v_w_mlp_up, w_mlp_down=v_w_mlp_down)
    grads, deltas, new_ms, new_vs = [], [], [], []
    for n in names:
        w = w_of[n]
        if n in early:
            g, d, mn, vn = early[n]
            grads.append(g)
            deltas.append(d)
            new_ms.append(mn)
            new_vs.append(vn)
            continue
        if n == "w_in_even":
            to_view = lambda a: a[0].T
            from_view = lambda a: a.T[None]
        else:
            view = w.shape if w.ndim <= 3 else w.shape[-3:]
            to_view = lambda a, view=view: a.reshape(view)
            from_view = lambda a, w=w: a.reshape(w.shape)
        g = g_of[n] if n == "w_in_even" else to_view(g_of[n])
        d, mn, vn = _adamw(to_view(w), g, to_view(m_of[n]), to_view(v_of[n]), "adamw_" + n)
        grads.append(from_view(g))
        deltas.append(from_view(d))
        new_ms.append(from_view(mn))
        new_vs.append(from_view(vn))

    return (loss, grad_x.reshape(1, T, D), *grads, *deltas, *new_ms, *new_vs)
```

```python
import functools

import jax
import jax.numpy as jnp
from jax import lax
from jax.experimental import pallas as pl
from jax.experimental.pallas import tpu as pltpu

F32 = jnp.float32
BF16 = jnp.bfloat16
MESH = pl.DeviceIdType.MESH

T = 2048
D = 1024
DFF = 4096
EPS = 1e-6
CHUNK = 64
NCHUNK = T // CHUNK
PE = 3200
PO = 2560
AUX_BLK = 3072 // 128
FOX_LANE0 = 16
GLA_SCALE = 64 ** -0.5
ATT_SCALE = 64 ** -0.5
NEG = float(jnp.finfo(jnp.float32).min)
CA_BAND = 576
CA_PAD = 512
REL_PAD = 384

VMEM_LIMIT = 48 * 1024 * 1024

ADAM_LR, ADAM_B1, ADAM_B2, ADAM_EPS, ADAM_WD, ADAM_STEP = 0.001, 0.9, 0.999, 1e-08, 0.01, 10

GA_ROWS, GA_IN_O, GA_OUT_O, GA_GAP, GA_UP, GA_DN = 3072, 0, 640, 896, 1024, 2048
GB_ROWS, GB_UP, GB_DN = 2048, 0, 1024
GC_ROWS, GC_OUT_E, GC_TAIL = 1152, 0, 256

_DIMS = {"nn": (((1,), (0,)), ((), ())), "nt": (((1,), (1,)), ((), ())), "tn": (((0,), (0,)), ((), ()))}


def _cp(sem, **kw):
    return pltpu.CompilerParams(dimension_semantics=sem, vmem_limit_bytes=VMEM_LIMIT, **kw)


def _dot(a, b, mode):
    return lax.dot_general(a.astype(BF16), b.astype(BF16), _DIMS[mode], preferred_element_type=F32)


@functools.partial(jax.custom_vjp, nondiff_argnums=(2,))
def bdot(a, b, mode):
    return _dot(a, b, mode)


def _bdot_fwd(a, b, mode):
    return _dot(a, b, mode), (a, b)


def _bdot_bwd(mode, res, g):
    a, b = res
    if mode == "nn":
        da, db = _dot(g, b, "nt"), _dot(a, g, "tn")
    elif mode == "nt":
        da, db = _dot(g, b, "nn"), _dot(g, a, "tn")
    else:
        da, db = _dot(b, g, "nt"), _dot(a, g, "nn")
    return da.astype(a.dtype), db.astype(b.dtype)


bdot.defvjp(_bdot_fwd, _bdot_bwd)


def _hdot_raw(a, b, mode):
    return lax.dot_general(a, b, _DIMS[mode], precision=lax.Precision.HIGHEST, preferred_element_type=F32)


@functools.partial(jax.custom_vjp, nondiff_argnums=(2,))
def hdot(a, b, mode):
    return _hdot_raw(a, b, mode)


def _hdot_fwd(a, b, mode):
    return _hdot_raw(a, b, mode), (a, b)


def _hdot_bwd(mode, res, g):
    a, b = res
    if mode == "nn":
        return _hdot_raw(g, b, "nt"), _hdot_raw(a, g, "tn")
    if mode == "nt":
        return _hdot_raw(g, b, "nn"), _hdot_raw(g, a, "tn")
    return _hdot_raw(b, g, "nt"), _hdot_raw(a, g, "nn")


hdot.defvjp(_hdot_fwd, _hdot_bwd)


def _log_sigmoid(x):
    return jnp.minimum(x, 0.0) - jnp.log(1.0 + jnp.exp(-jnp.abs(x)))


def _sigmoid(x):
    return 1.0 / (1.0 + jnp.exp(-x))


def _expm1(x):
    series = x * (1.0 + x * 0.5 * (1.0 + x * (1.0 / 3.0) * (1.0 + x * 0.25)))
    return jnp.where(jnp.abs(x) < 0.03, series, jnp.exp(x) - 1.0)


def _gelu_tanh(x):
    return 0.5 * x * (1.0 + jnp.tanh(0.7978845608028654 * (x + 0.044715 * x * x * x)))


def _iota(shape, dim):
    return lax.broadcasted_iota(jnp.int32, shape, dim)


def _mm(a, b, mode, *, tm, tn, tk=None, out_dtype=F32, name, b_layer=None, into=None, relu_pair=False, times2=None):
    b2 = b.shape[-2:]
    if mode == "nn":
        (m, k), n = a.shape, b2[1]
    elif mode == "nt":
        (m, k), n = a.shape, b2[0]
    else:
        (k, m), n = a.shape, b2[1]
    tk = k if tk is None else tk
    assert m % tm == 0 and n % tn == 0 and k % tk == 0, (name, a.shape, b.shape)
    nk = k // tk
    if mode == "tn":
        a_spec = pl.BlockSpec((tk, tm), lambda i, j, kk: (kk, i))
    elif m == tm and nk == 1:
        a_spec = pl.BlockSpec((tm, tk), lambda i, j, kk: (i, kk), pipeline_mode=pl.Buffered(1))
    else:
        a_spec = pl.BlockSpec((tm, tk), lambda i, j, kk: (i, kk))
    b_blk = {"nn": (tk, tn), "nt": (tn, tk), "tn": (tk, tn)}[mode]
    b_idx = {"nn": lambda i, j, kk: (kk, j), "nt": lambda i, j, kk: (j, kk), "tn": lambda i, j, kk: (kk, j)}[mode]
    if b_layer is None:
        b_spec = pl.BlockSpec(b_blk, b_idx)
    else:
        b_spec = pl.BlockSpec((None,) + b_blk, lambda i, j, kk: (b_layer,) + b_idx(i, j, kk))

    tile = pl.BlockSpec((tm, tn), lambda i, j, kk: (i, j))
    if into is not None:
        buf, per_slot, row_off = into
        assert m == 4 * per_slot and per_slot % tm == 0 and row_off % tm == 0 and buf.shape[2] == n, (name, buf.shape)
        bps = per_slot // tm
        out_specs = pl.BlockSpec((None, tm, tn), lambda i, j, kk: (i // bps, row_off // tm + i % bps, j))
        out_shape = jax.ShapeDtypeStruct(buf.shape, buf.dtype)
        extra_in, extra_specs, aliases = [buf], [pl.BlockSpec(memory_space=pl.ANY)], {2: 0}
        finish = lambda acc, extra: [acc.astype(buf.dtype)]
    elif relu_pair:
        out_specs = (tile, tile)
        out_shape = (jax.ShapeDtypeStruct((m, n), BF16),) * 2
        extra_in, extra_specs, aliases = [], [], {}

        def finish(acc, extra):
            r = jnp.maximum(acc, 0.0)
            return [(r * r).astype(BF16), r.astype(BF16)]
    elif times2 is not None:
        out_specs = tile
        out_shape = jax.ShapeDtypeStruct((m, n), out_dtype)
        extra_in, extra_specs, aliases = [times2], [tile], {}
        finish = lambda acc, extra: [(acc * (2.0 * extra[...].astype(F32))).astype(out_dtype)]
    else:
        out_specs = tile
        out_shape = jax.ShapeDtypeStruct((m, n), out_dtype)
        extra_in, extra_specs, aliases = [], [], {}
        finish = lambda acc, extra: [acc.astype(out_dtype)]
    n_out = 2 if relu_pair else 1

    def body(*refs):
        a_ref, b_ref = refs[0], refs[1]
        extra = refs[2] if extra_in else None
        o_refs = refs[2 + len(extra_in):2 + len(extra_in) + n_out]

        def store(acc):
            for o_ref, val in zip(o_refs, finish(acc, extra)):
                o_ref[...] = val

        if nk == 1:
            store(_dot(a_ref[...], b_ref[...], mode))
            return
        acc_ref = refs[-1]
        kk = pl.program_id(2)

        @pl.when(kk == 0)
        def _():
            acc_ref[...] = jnp.zeros_like(acc_ref)

        acc_ref[...] += _dot(a_ref[...], b_ref[...], mode)

        @pl.when(kk == nk - 1)
        def _():
            store(acc_ref[...])

    return pl.pallas_call(
        body, name=name, grid=(m // tm, n // tn, nk),
        in_specs=[a_spec, b_spec] + extra_specs,
        out_specs=out_specs, out_shape=out_shape,
        scratch_shapes=[pltpu.VMEM((tm, tn), F32)] if nk > 1 else [],
        input_output_aliases=aliases,
        compiler_params=_cp(("parallel", "parallel", "arbitrary")),
    )(a, b, *extra_in)


ROWS = 512


def _prenorm(x, w, name):
    def body(x_ref, w_ref, o_ref):
        xv = x_ref[...]
        r = lax.rsqrt(jnp.mean(xv * xv, axis=-1, keepdims=True) + EPS)
        o_ref[...] = (xv * r * w_ref[...]).astype(BF16)

    return pl.pallas_call(
        body, name=name, grid=(T // ROWS,),
        in_specs=[pl.BlockSpec((ROWS, D), lambda i: (i, 0)), pl.BlockSpec((1, D), lambda i: (0, 0))],
        out_specs=pl.BlockSpec((ROWS, D), lambda i: (i, 0)),
        out_shape=jax.ShapeDtypeStruct((T, D), BF16),
        compiler_params=_cp(("parallel",)),
    )(x, w)


def _rms(z):
    return lax.rsqrt(jnp.mean(z * z, axis=-1, keepdims=True) + EPS)


def _rms_bwd(z, w, dy):
    r = _rms(z)
    wdy = dy * w
    dz = r * wdy - z * (r * r * r) * jnp.mean(z * wdy, axis=-1, keepdims=True)
    return dz, jnp.sum(dy * z * r, axis=0, keepdims=True)


_ROW = pl.BlockSpec((ROWS, D), lambda i: (i, 0))
_VEC = pl.BlockSpec((1, D), lambda i: (0, 0))


def _post_pre_fwd(x, z, w_post, w_pre, name):
    def body(x_ref, z_ref, wp_ref, wn_ref, x_out, h_out):
        zv = z_ref[...]
        xn = x_ref[...] + zv * _rms(zv) * wp_ref[...]
        x_out[...] = xn
        h_out[...] = (xn * _rms(xn) * wn_ref[...]).astype(BF16)

    return pl.pallas_call(
        body, name=name, grid=(T // ROWS,), in_specs=[_ROW, _ROW, _VEC, _VEC], out_specs=(_ROW, _ROW),
        out_shape=(jax.ShapeDtypeStruct((T, D), F32), jax.ShapeDtypeStruct((T, D), BF16)),
        compiler_params=_cp(("parallel",)),
    )(x, z, w_post, w_pre)


def _post_loss(x, z, w_post, tgt):
    def body(x_ref, z_ref, w_ref, t_ref, g_ref, l_ref, dz_ref, dw_ref):
        @pl.when(pl.program_id(0) == 0)
        def _():
            l_ref[...] = jnp.zeros_like(l_ref)
            dw_ref[...] = jnp.zeros_like(dw_ref)

        zv = z_ref[...]
        e = x_ref[...] + zv * _rms(zv) * w_ref[...] - t_ref[...]
        g = e * (1.0 / D)
        g_ref[...] = g
        l_ref[...] += jnp.sum(e * e) * (0.5 / D)
        dz, dw = _rms_bwd(zv, w_ref[...], g)
        dz_ref[...] = dz.astype(BF16)
        dw_ref[...] += dw

    return pl.pallas_call(
        body, name="postnorm_loss", grid=(T // ROWS,), in_specs=[_ROW, _ROW, _VEC, _ROW],
        out_specs=(_ROW, pl.BlockSpec((1, 128), lambda i: (0, 0)), _ROW, _VEC),
        out_shape=(jax.ShapeDtypeStruct((T, D), F32), jax.ShapeDtypeStruct((1, 128), F32),
                   jax.ShapeDtypeStruct((T, D), BF16), jax.ShapeDtypeStruct((1, D), F32)),
        compiler_params=_cp(("arbitrary",)),
    )(x, z, w_post, tgt)


def _pre_post_bwd(x, w_pre, dh, add, z, w_post, name):
    def body(x_ref, wn_ref, dh_ref, add_ref, z_ref, wp_ref, g_ref, dz_ref, dwn_ref, dwp_ref):
        @pl.when(pl.program_id(0) == 0)
        def _():
            dwn_ref[...] = jnp.zeros_like(dwn_ref)
            dwp_ref[...] = jnp.zeros_like(dwp_ref)

        dx, dwn = _rms_bwd(x_ref[...], wn_ref[...], dh_ref[...])
        g = dx + add_ref[...]
        g_ref[...] = g
        dz, dwp = _rms_bwd(z_ref[...], wp_ref[...], g)
        dz_ref[...] = dz.astype(BF16)
        dwn_ref[...] += dwn
        dwp_ref[...] += dwp

    return pl.pallas_call(
        body, name=name, grid=(T // ROWS,), in_specs=[_ROW, _VEC, _ROW, _ROW, _ROW, _VEC],
        out_specs=(_ROW, _ROW, _VEC, _VEC),
        out_shape=(jax.ShapeDtypeStruct((T, D), F32), jax.ShapeDtypeStruct((T, D), BF16),
                   jax.ShapeDtypeStruct((1, D), F32), jax.ShapeDtypeStruct((1, D), F32)),
        compiler_params=_cp(("arbitrary",)),
    )(x, w_pre, dh, add, z, w_post)


def _norm_bwd(z, w, dy, add, name):
    has_add = add is not None

    def body(*refs):
        if has_add:
            z_ref, w_ref, dy_ref, add_ref, dz_ref, dw_ref = refs
        else:
            z_ref, w_ref, dy_ref, dz_ref, dw_ref = refs
        i = pl.program_id(0)

        @pl.when(i == 0)
        def _():
            dw_ref[...] = jnp.zeros_like(dw_ref)

        zv = z_ref[...].astype(F32)
        dyv = dy_ref[...]
        r = lax.rsqrt(jnp.mean(zv * zv, axis=-1, keepdims=True) + EPS)
        wdy = dyv * w_ref[...]
        dz = r * wdy - zv * (r * r * r) * jnp.mean(zv * wdy, axis=-1, keepdims=True)
        if has_add:
            dz = dz + add_ref[...]
        dz_ref[...] = dz.astype(dz_ref.dtype)
        dw_ref[...] += jnp.sum(dyv * zv * r, axis=0, keepdims=True)

    row = pl.BlockSpec((ROWS, D), lambda i: (i, 0))
    vec = pl.BlockSpec((1, D), lambda i: (0, 0))
    ins = [z, w, dy] + ([add] if has_add else [])
    dz_dtype = F32 if has_add else BF16
    return pl.pallas_call(
        body, name=name, grid=(T // ROWS,),
        in_specs=[row, vec, row] + ([row] if has_add else []),
        out_specs=(row, vec),
        out_shape=(jax.ShapeDtypeStruct((T, D), dz_dtype), jax.ShapeDtypeStruct((1, D), F32)),
        compiler_params=_cp(("arbitrary",)),
    )(*ins)


def _adamw_math(w, g, m, v):
    c1 = 1.0 - ADAM_B1 ** ADAM_STEP
    c2 = 1.0 - ADAM_B2 ** ADAM_STEP
    mn = ADAM_B1 * m + (1.0 - ADAM_B1) * g
    vn = ADAM_B2 * v + (1.0 - ADAM_B2) * (g * g)
    return -ADAM_LR * ((mn / c1) / (jnp.sqrt(vn / c2) + ADAM_EPS) + ADAM_WD * w), mn, vn


def _adamw_from(w, m, v, sources, tr, name):
    layers, rows, cols = w.shape
    assert len(sources) == layers and rows % tr == 0, (name, w.shape)
    g_specs = []
    for layer, (buf, row0, transposed) in enumerate(sources):
        step = lambda l, i, layer=layer: jnp.where(l == layer, i, 0)
        if transposed:
            assert row0 % cols == 0 and buf.shape[1] == rows, (name, row0)
            g_specs.append(pl.BlockSpec((cols, tr), lambda l, i, b=row0 // cols, step=step: (b, step(l, i))))
        else:
            assert row0 % tr == 0 and buf.shape[1] == cols, (name, row0)
            g_specs.append(pl.BlockSpec((tr, cols), lambda l, i, b=row0 // tr, step=step: (b + step(l, i), 0)))

    def body(*refs):
        w_ref, m_ref, v_ref = refs[:3]
        g_refs = refs[3:3 + layers]
        g_out, d_ref, mo_ref, vo_ref = refs[3 + layers:]
        gs = [r[...].T if src[2] else r[...] for r, src in zip(g_refs, sources)]
        g = gs[0] if layers == 1 else jnp.where(pl.program_id(0) == 0, gs[0], gs[1])
        g_out[...] = g
        d_ref[...], mo_ref[...], vo_ref[...] = _adamw_math(w_ref[...], g, m_ref[...], v_ref[...])

    blk = pl.BlockSpec((None, tr, cols), lambda l, i: (l, i, 0))
    sds = jax.ShapeDtypeStruct(w.shape, F32)
    return pl.pallas_call(body, name=name, grid=(layers, rows // tr), in_specs=[blk] * 3 + g_specs,
                          out_specs=(blk,) * 4, out_shape=(sds,) * 4,
                          compiler_params=_cp(("parallel", "parallel")))(w, m, v, *[s[0] for s in sources])


def _adamw(w, g, m, v, name):
    lead = w.shape[:-2]
    assert len(lead) <= 1 and g.shape == w.shape, (name, w.shape, g.shape)
    rows, cols = w.shape[-2:]
    if rows <= 512:
        tr, tc = rows, cols
    elif rows % 256 == 0:
        tr, tc = 256, cols
    else:
        tr, tc = rows, 256
    assert rows % tr == 0 and cols % tc == 0, (name, w.shape)
    c1 = 1.0 - ADAM_B1 ** ADAM_STEP
    c2 = 1.0 - ADAM_B2 ** ADAM_STEP

    def body(w_ref, g_ref, m_ref, v_ref, d_ref, mo_ref, vo_ref):
        gv = g_ref[...]
        mn = ADAM_B1 * m_ref[...] + (1.0 - ADAM_B1) * gv
        vn = ADAM_B2 * v_ref[...] + (1.0 - ADAM_B2) * (gv * gv)
        m_hat = mn / c1
        v_hat = vn / c2
        d_ref[...] = -ADAM_LR * (m_hat / (jnp.sqrt(v_hat) + ADAM_EPS) + ADAM_WD * w_ref[...])
        mo_ref[...] = mn
        vo_ref[...] = vn

    if lead:
        grid = (lead[0], rows // tr, cols // tc)
        blk = pl.BlockSpec((None, tr, tc), lambda l, i, j: (l, i, j))
    else:
        grid = (rows // tr, cols // tc)
        blk = pl.BlockSpec((tr, tc), lambda i, j: (i, j))
    sds = jax.ShapeDtypeStruct(w.shape, F32)
    return pl.pallas_call(body, name=name, grid=grid, in_specs=[blk] * 4, out_specs=(blk,) * 3,
                          out_shape=(sds,) * 3, compiler_params=_cp(("parallel",) * len(grid)))(w, g, m, v)


def _gla_consts():
    ltri = (_iota((CHUNK, CHUNK), 0) >= _iota((CHUNK, CHUNK), 1)).astype(F32)
    ones_c = jnp.ones((CHUNK, 128), F32)
    mask = (_iota((256, 512), 0) // 64 == _iota((256, 512), 1) // 128).astype(F32)
    return ltri, ones_c, mask


def _gla_chunk(consts, q, k, v, r, aux, s_prev, wa, ba, nw):
    ltri, ones_c, mask = consts
    la = _log_sigmoid(bdot(aux, wa, "nn") + ba) * (1.0 / 16.0)
    cum = hdot(ltri, la, "nn")
    total = jnp.sum(la, axis=0, keepdims=True)
    k_dec = k * jnp.exp(total - cum)
    inc = bdot(k_dec, v, "tn") * mask
    dec = jnp.exp(hdot(la, ones_c, "tn"))
    dec = jnp.concatenate([dec, dec, dec, dec], axis=1)
    s_new = dec * s_prev + inc
    o = bdot(q * GLA_SCALE, s_new, "nn")
    parts = []
    for h in range(4):
        oh = o[:, h * 128:(h + 1) * 128]
        parts.append(oh * lax.rsqrt(jnp.mean(oh * oh, axis=-1, keepdims=True) + EPS))
    on = jnp.concatenate(parts, axis=1)
    return s_new, on * nw * (r * _sigmoid(r))


GLA_PER_STEP = 4
GLA_ROWS = GLA_PER_STEP * CHUNK
GLA_STEPS = NCHUNK // GLA_PER_STEP


def _gla_specs(cmap):
    return [pl.BlockSpec((GLA_ROWS, 256), lambda c: (cmap(c), 0)),
            pl.BlockSpec((GLA_ROWS, 256), lambda c: (cmap(c), 1)),
            pl.BlockSpec((GLA_ROWS, 512), lambda c: (cmap(c), 1)),
            pl.BlockSpec((GLA_ROWS, 512), lambda c: (cmap(c), 2)),
            pl.BlockSpec((GLA_ROWS, 128), lambda c: (cmap(c), AUX_BLK))]


def _gla_fwd(proj, wa, ba, nw):
    def body(q_ref, k_ref, v_ref, r_ref, aux_ref, wa_ref, ba_ref, nw_ref, o_ref, sp_ref, s_ref):
        @pl.when(pl.program_id(0) == 0)
        def _():
            s_ref[...] = jnp.zeros_like(s_ref)

        s = s_ref[...]
        consts = _gla_consts()
        outs, states = [], []
        for i in range(GLA_PER_STEP):
            rows = slice(i * CHUNK, (i + 1) * CHUNK)
            states.append(s)
            s, out = _gla_chunk(consts, q_ref[rows, :], k_ref[rows, :], v_ref[rows, :], r_ref[rows, :], aux_ref[rows, :],
                                s, wa_ref[...], ba_ref[...], nw_ref[...])
            outs.append(out)
        s_ref[...] = s
        for i in range(GLA_PER_STEP):
            o_ref[i * CHUNK:(i + 1) * CHUNK, :] = outs[i]
            sp_ref[i] = states[i]

    full = lambda shape: pl.BlockSpec(shape, lambda c: (0,) * len(shape))
    return pl.pallas_call(
        body, name="gla_fwd", grid=(GLA_STEPS,),
        in_specs=_gla_specs(lambda c: c) + [full((128, 256)), full((1, 256)), full((1, 512))],
        out_specs=(pl.BlockSpec((GLA_ROWS, 512), lambda c: (c, 0)),
                   pl.BlockSpec((GLA_PER_STEP, 256, 512), lambda c: (c, 0, 0))),
        out_shape=(jax.ShapeDtypeStruct((T, D), F32), jax.ShapeDtypeStruct((NCHUNK, 256, 512), F32)),
        scratch_shapes=[pltpu.VMEM((256, 512), F32)],
        compiler_params=_cp(("arbitrary",)),
    )(proj, proj, proj, proj, proj, wa, ba, nw)


def _gla_bwd(proj, s_prev_all, wa, ba, nw, dcat):
    rev = lambda c: GLA_STEPS - 1 - c

    def body(q_ref, k_ref, v_ref, r_ref, aux_ref, sp_ref, wa_ref, ba_ref, nw_ref, do_ref,
             dq_ref, dk_ref, dv_ref, dr_ref, daux_ref, dwa_ref, dba_ref, dnw_ref, ds_ref):
        @pl.when(pl.program_id(0) == 0)
        def _():
            ds_ref[...] = jnp.zeros_like(ds_ref)
            dwa_ref[...] = jnp.zeros_like(dwa_ref)
            dba_ref[...] = jnp.zeros_like(dba_ref)
            dnw_ref[...] = jnp.zeros_like(dnw_ref)

        fn = functools.partial(_gla_chunk, _gla_consts())
        ds = ds_ref[...]
        dwa, dba, dnw = dwa_ref[...], dba_ref[...], dnw_ref[...]
        grads = {}
        for i in reversed(range(GLA_PER_STEP)):
            rows = slice(i * CHUNK, (i + 1) * CHUNK)
            _, vjp = jax.vjp(fn, q_ref[rows, :], k_ref[rows, :], v_ref[rows, :], r_ref[rows, :], aux_ref[rows, :],
                             sp_ref[i], wa_ref[...], ba_ref[...], nw_ref[...])
            *grads[i], ds, dwa_i, dba_i, dnw_i = vjp((ds, do_ref[rows, :]))
            dwa, dba, dnw = dwa + dwa_i, dba + dba_i, dnw + dnw_i
        ds_ref[...] = ds
        dwa_ref[...] = dwa
        dba_ref[...] = dba
        dnw_ref[...] = dnw
        for i in range(GLA_PER_STEP):
            rows = slice(i * CHUNK, (i + 1) * CHUNK)
            for ref, g in zip((dq_ref, dk_ref, dv_ref, dr_ref, daux_ref), grads[i]):
                ref[rows, :] = g

    full = lambda shape: pl.BlockSpec(shape, lambda c: (0,) * len(shape))
    blk = lambda w: pl.BlockSpec((GLA_ROWS, w), lambda c: (rev(c), 0))
    sds = lambda *s: jax.ShapeDtypeStruct(s, F32)
    return pl.pallas_call(
        body, name="gla_bwd", grid=(GLA_STEPS,),
        in_specs=_gla_specs(rev) + [pl.BlockSpec((GLA_PER_STEP, 256, 512), lambda c: (rev(c), 0, 0)),
                                    full((128, 256)), full((1, 256)), full((1, 512)), blk(512)],
        out_specs=(blk(256), blk(256), blk(512), blk(512), blk(128), full((128, 256)), full((1, 256)), full((1, 512))),
        out_shape=(sds(T, 256), sds(T, 256), sds(T, 512), sds(T, 512), sds(T, 128),
                   sds(128, 256), sds(1, 256), sds(1, 512)),
        scratch_shapes=[pltpu.VMEM((256, 512), F32)],
        compiler_params=_cp(("arbitrary",)),
    )(proj, proj, proj, proj, proj, s_prev_all, wa, ba, nw, dcat)


GATE_ROWS = 128


def _fox_gate_block(ltri, aux, bpad, carry):
    lf = _log_sigmoid(aux + bpad)
    cum = hdot(ltri, lf, "nn") + carry
    return cum, carry + jnp.sum(lf, axis=0, keepdims=True)


def _gate_ltri():
    return (_iota((GATE_ROWS, GATE_ROWS), 0) >= _iota((GATE_ROWS, GATE_ROWS), 1)).astype(F32)


def _fox_gate_fwd(proj, bpad):
    def body(aux_ref, b_ref, cum_ref, carry_ref):
        i = pl.program_id(0)

        @pl.when(i == 0)
        def _():
            carry_ref[...] = jnp.zeros_like(carry_ref)

        cum, carry = _fox_gate_block(_gate_ltri(), aux_ref[...], b_ref[...], carry_ref[...])
        cum_ref[...] = cum
        carry_ref[...] = carry

    return pl.pallas_call(
        body, name="fox_gate_fwd", grid=(T // GATE_ROWS,),
        in_specs=[pl.BlockSpec((GATE_ROWS, 128), lambda i: (i, AUX_BLK)), pl.BlockSpec((1, 128), lambda i: (0, 0))],
        out_specs=pl.BlockSpec((GATE_ROWS, 128), lambda i: (i, 0)),
        out_shape=jax.ShapeDtypeStruct((T, 128), F32),
        scratch_shapes=[pltpu.VMEM((1, 128), F32)],
        compiler_params=_cp(("arbitrary",)),
    )(proj, bpad)


def _fox_gate_bwd(proj, bpad, dccol_t, daux_gla):
    nb = T // GATE_ROWS
    rev = lambda i: nb - 1 - i

    def body(aux_ref, b_ref, dc_ref, dg_ref, daux_ref, db_ref, dcarry_ref):
        i = pl.program_id(0)

        @pl.when(i == 0)
        def _():
            dcarry_ref[...] = jnp.zeros_like(dcarry_ref)
            db_ref[...] = jnp.zeros_like(db_ref)

        dcum = dc_ref[...]
        fn = functools.partial(_fox_gate_block, _gate_ltri())
        _, vjp = jax.vjp(fn, aux_ref[...], b_ref[...], jnp.zeros((1, 128), F32))
        daux, db, dcarry = vjp((dcum, dcarry_ref[...]))
        daux_ref[...] = daux + dg_ref[...]
        db_ref[...] += db
        dcarry_ref[...] = dcarry

    blk = pl.BlockSpec((GATE_ROWS, 128), lambda i: (rev(i), 0))
    vec = pl.BlockSpec((1, 128), lambda i: (0, 0))
    return pl.pallas_call(
        body, name="fox_gate_bwd", grid=(nb,),
        in_specs=[pl.BlockSpec((GATE_ROWS, 128), lambda i: (rev(i), AUX_BLK)), vec, blk, blk],
        out_specs=(blk, vec),
        out_shape=(jax.ShapeDtypeStruct((T, 128), F32), jax.ShapeDtypeStruct((1, 128), F32)),
        scratch_shapes=[pltpu.VMEM((1, 128), F32)],
        compiler_params=_cp(("arbitrary",)),
    )(proj, bpad, dccol_t, daux_gla)


FOX_Q = 256


FOX_QB = T // FOX_Q


@jax.custom_vjp
def _attend(s, v):
    return _attend_fwd(s, v)[0]


def _attend_fwd(s, v):
    e = jnp.exp(s - jnp.max(s, axis=-1, keepdims=True))
    r = 1.0 / jnp.sum(e, axis=-1, keepdims=True)
    return _dot(e, v, "nn") * r, (e, r, v)


def _attend_bwd(res, do):
    e, r, v = res
    do_r = do * r
    dpr = _dot(do_r, v, "nt")
    ds = e * (dpr - r * jnp.sum(e * dpr, axis=-1, keepdims=True))
    return ds, _dot(e, do_r, "tn").astype(v.dtype)


_attend.defvjp(_attend_fwd, _attend_bwd)


def _fox_block(hp, q, k, v, ccol):
    kl = k.shape[0]
    lane = _iota((FOX_Q, 128), 1)
    tri = jnp.bitwise_and(_iota((2 * FOX_Q, FOX_Q), 0), FOX_Q - 1) >= _iota((2 * FOX_Q, FOX_Q), 1)
    sub = _iota((8, kl), 0)
    qs = q * ATT_SCALE
    q2 = jnp.concatenate([jnp.where(lane < 64, qs, 0.0), jnp.where(lane >= 64, qs, 0.0)], axis=0)
    s = bdot(q2, k, "nt")
    cs = [jnp.sum(jnp.where(sub == 2 * hp + e, ccol, 0.0), axis=0, keepdims=True) for e in range(2)]
    s = jnp.concatenate([s[:FOX_Q] - cs[0], s[FOX_Q:] - cs[1]], axis=0)
    diag = jnp.where(tri, s[:, kl - FOX_Q:], NEG)
    s = diag if kl == FOX_Q else jnp.concatenate([s[:, :kl - FOX_Q], diag], axis=1)
    o2 = _attend(s, v)
    return jnp.where(lane < 64, o2[:FOX_Q], o2[FOX_Q:])


def _fox_in_specs():
    return [pl.BlockSpec((FOX_Q, 128), lambda hp, qb: (qb, 12 + hp)),
            pl.BlockSpec((T, 128), lambda hp, qb: (0, 16 + hp)),
            pl.BlockSpec((T, 128), lambda hp, qb: (0, 20 + hp)),
            pl.BlockSpec((8, T), lambda hp, qb: (0, 0))]


def _fox_fwd(proj, cum_c, cat):
    def body(q_ref, k_ref, v_ref, cc_ref, cat_ref, o_ref):
        qb = pl.program_id(1)
        for g in range(FOX_QB):
            kl = FOX_Q * (g + 1)

            @pl.when(qb == g)
            def _(kl=kl):
                o_ref[...] = _fox_block(pl.program_id(0), q_ref[...], k_ref[0:kl, :], v_ref[0:kl, :], cc_ref[:, 0:kl])

    return pl.pallas_call(
        body, name="fox_fwd", grid=(4, FOX_QB), in_specs=_fox_in_specs() + [pl.BlockSpec(memory_space=pl.ANY)],
        out_specs=pl.BlockSpec((FOX_Q, 128), lambda hp, qb: (qb, 4 + hp)),
        out_shape=jax.ShapeDtypeStruct((T, D), F32), input_output_aliases={4: 0},
        compiler_params=_cp(("parallel", "parallel")),
    )(proj, proj, proj, cum_c, cat)


def _fox_bwd(proj, cum_c, dcat):
    def body(q_ref, k_ref, v_ref, cc_ref, do_ref, dq_ref, dk_ref, dv_ref, dcc_ref):
        qb = pl.program_id(1)

        @pl.when(qb == 0)
        def _():
            dk_ref[...] = jnp.zeros_like(dk_ref)
            dv_ref[...] = jnp.zeros_like(dv_ref)
            dcc_ref[...] = jnp.zeros_like(dcc_ref)

        fn = functools.partial(_fox_block, pl.program_id(0))
        for g in range(FOX_QB):
            kl = FOX_Q * (g + 1)

            @pl.when(qb == g)
            def _(kl=kl):
                _, vjp = jax.vjp(fn, q_ref[...], k_ref[0:kl, :], v_ref[0:kl, :], cc_ref[:, 0:kl])
                dq, dk, dv, dcc = vjp(do_ref[...])
                dq_ref[...] = dq
                dk_ref[0:kl, :] += dk
                dv_ref[0:kl, :] += dv
                dcc_ref[:, 0:kl] += dcc

    sds = lambda *s: jax.ShapeDtypeStruct(s, F32)
    return pl.pallas_call(
        body, name="fox_bwd", grid=(4, FOX_QB),
        in_specs=_fox_in_specs() + [pl.BlockSpec((FOX_Q, 128), lambda hp, qb: (qb, 4 + hp))],
        out_specs=(pl.BlockSpec((FOX_Q, 128), lambda hp, qb: (qb, hp)),
                   pl.BlockSpec((T, 128), lambda hp, qb: (0, hp)),
                   pl.BlockSpec((T, 128), lambda hp, qb: (0, hp)),
                   pl.BlockSpec((None, 8, T), lambda hp, qb: (hp, 0, 0))),
        out_shape=(sds(T, 512), sds(T, 512), sds(T, 512), sds(4, 8, T)),
        compiler_params=_cp(("parallel", "arbitrary")),
    )(proj, proj, proj, cum_c, dcat)


BIAS_W = 640


def _rel_onehot():
    j = _iota((REL_PAD, BIAS_W), 1)
    rel = jnp.clip(CA_PAD + CHUNK - 1 - j, -128, 128) + 128
    return (_iota((REL_PAD, BIAS_W), 0) == rel).astype(F32)


def _bias_build(rbp):
    def body(rb_ref, o_ref):
        f = _hdot_raw(rb_ref[...], _rel_onehot(), "nn")
        for q in range(CHUNK):
            o_ref[q] = pltpu.roll(f, (BIAS_W - (CHUNK - 1 - q)) % BIAS_W, 1)[:, :CA_BAND]

    return pl.pallas_call(body, name="ca_bias_build", out_shape=jax.ShapeDtypeStruct((CHUNK, 8, CA_BAND), F32))(rbp)


def _bias_grad(dbias_q):
    def body(db_ref, o_ref):
        acc = jnp.zeros((8, BIAS_W), F32)
        for q in range(CHUNK):
            acc = acc + pltpu.roll(db_ref[q], CHUNK - 1 - q, 1)
        o_ref[...] = _hdot_raw(acc, _rel_onehot(), "nt")

    return pl.pallas_call(body, name="ca_bias_grad", out_shape=jax.ShapeDtypeStruct((8, REL_PAD), F32))(dbias_q)


def _ca_block(c, masked, q, kb, vb, bias2):
    lane = _iota((CHUNK, 128), 1)
    qs = q * ATT_SCALE
    q2 = jnp.concatenate([jnp.where(lane < 64, qs, 0.0), jnp.where(lane >= 64, qs, 0.0)], axis=0)
    s = bdot(q2, kb, "nt") + bias2.reshape(2 * CHUNK, CA_BAND)
    if masked:
        s = jnp.where((c * CHUNK - CA_PAD + _iota((2 * CHUNK, CA_BAND), 1)) >= 0, s, NEG)
    o2 = _attend(s, vb)
    return jnp.where(lane < 64, o2[:CHUNK], o2[CHUNK:])


CA_PER_STEP = 8
CA_ROWS = CA_PER_STEP * CHUNK
CA_MASKED_STEPS = -(-CA_PAD // CA_ROWS)


def _ca_fwd(proj, kvpad, bias):
    def body(q_ref, k_ref, v_ref, b_ref, o_ref):
        def run(masked):
            outs = []
            for i in range(CA_PER_STEP):
                c = pl.program_id(1) * CA_PER_STEP + i
                band = pl.ds(pl.multiple_of(c * CHUNK, CHUNK), CA_BAND)
                rows = slice(i * CHUNK, (i + 1) * CHUNK)
                outs.append(_ca_block(c, masked, q_ref[rows, :], k_ref[band, :], v_ref[band, :], b_ref[...]))
            for i in range(CA_PER_STEP):
                o_ref[i * CHUNK:(i + 1) * CHUNK, :] = outs[i]

        pl.when(pl.program_id(1) < CA_MASKED_STEPS)(lambda: run(True))
        pl.when(pl.program_id(1) >= CA_MASKED_STEPS)(lambda: run(False))

    return pl.pallas_call(
        body, name="ca_fwd", grid=(4, NCHUNK // CA_PER_STEP),
        in_specs=[pl.BlockSpec((CA_ROWS, 128), lambda hp, c: (c, hp)),
                  pl.BlockSpec((T + CA_PAD, 128), lambda hp, c: (0, hp)),
                  pl.BlockSpec((T + CA_PAD, 128), lambda hp, c: (0, 4 + hp)),
                  pl.BlockSpec((2, CHUNK, CA_BAND), lambda hp, c: (hp, 0, 0))],
        out_specs=pl.BlockSpec((CA_ROWS, 128), lambda hp, c: (c, hp)),
        out_shape=jax.ShapeDtypeStruct((T, D), F32),
        compiler_params=_cp(("parallel", "parallel")),
    )(proj, kvpad, kvpad, bias)


def _ca_bwd(proj, kvpad, bias, dcat):
    def body(q_ref, k_ref, v_ref, b_ref, do_ref, dq_ref, dk_ref, dv_ref, db_ref):
        c = pl.program_id(1)

        @pl.when(c == 0)
        def _():
            dk_ref[...] = jnp.zeros_like(dk_ref)
            dv_ref[...] = jnp.zeros_like(dv_ref)
            db_ref[...] = jnp.zeros_like(db_ref)

        def run(masked):
            grads, bands = [], []
            for i in range(CA_PER_STEP):
                ci = c * CA_PER_STEP + i
                band = pl.ds(pl.multiple_of(ci * CHUNK, CHUNK), CA_BAND)
                rows = slice(i * CHUNK, (i + 1) * CHUNK)
                fn = functools.partial(_ca_block, ci, masked)
                _, vjp = jax.vjp(fn, q_ref[rows, :], k_ref[band, :], v_ref[band, :], b_ref[...])
                grads.append(vjp(do_ref[rows, :]))
                bands.append(band)
            for i, (dq, _, _, _) in enumerate(grads):
                dq_ref[i * CHUNK:(i + 1) * CHUNK, :] = dq
            for band, (_, dkb, dvb, _) in zip(bands, grads):
                dk_ref[band, :] += dkb
                dv_ref[band, :] += dvb
            db_ref[...] += functools.reduce(lambda a, b: a + b, [g[3] for g in grads])

        pl.when(c < CA_MASKED_STEPS)(lambda: run(True))
        pl.when(c >= CA_MASKED_STEPS)(lambda: run(False))

    sds = lambda *s: jax.ShapeDtypeStruct(s, F32)
    padded = lambda: pl.BlockSpec((T + CA_PAD, 128), lambda hp, c: (0, hp))
    return pl.pallas_call(
        body, name="ca_bwd", grid=(4, NCHUNK // CA_PER_STEP),
        in_specs=[pl.BlockSpec((CA_ROWS, 128), lambda hp, c: (c, hp)),
                  pl.BlockSpec((T + CA_PAD, 128), lambda hp, c: (0, hp)),
                  pl.BlockSpec((T + CA_PAD, 128), lambda hp, c: (0, 4 + hp)),
                  pl.BlockSpec((2, CHUNK, CA_BAND), lambda hp, c: (hp, 0, 0)),
                  pl.BlockSpec((CA_ROWS, 128), lambda hp, c: (c, hp))],
        out_specs=(pl.BlockSpec((CA_ROWS, 128), lambda hp, c: (c, hp)), padded(), padded(),
                   pl.BlockSpec((2, CHUNK, CA_BAND), lambda hp, c: (hp, 0, 0))),
        out_shape=(sds(T, 512), sds(T + CA_PAD, 512), sds(T + CA_PAD, 512), sds(8, CHUNK, CA_BAND)),
        compiler_params=_cp(("parallel", "arbitrary")),
    )(proj, kvpad, kvpad, bias, dcat)


def _lru_pre(xs, cw, cb, wa, ba, wx, bx, lam):
    xc = cb + xs[0] * cw[0:1, :] + xs[1] * cw[1:2, :] + xs[2] * cw[2:3, :] + xs[3] * cw[3:4, :]
    ra = _sigmoid(bdot(xc, wa, "nn") + ba)
    ii = _sigmoid(bdot(xc, wx, "nn") + bx)
    la = 8.0 * ra * _log_sigmoid(lam)
    return jnp.exp(la), jnp.sqrt(-_expm1(2.0 * la)) * (ii * xc)


def _lru_pre_specs():
    full = lambda shape: pl.BlockSpec(shape, lambda i: (0,) * len(shape))
    return [pl.BlockSpec((4, ROWS, 512), lambda i: (0, i, 0)), full((4, 512)), full((1, 512)),
            full((512, 512)), full((1, 512)), full((512, 512)), full((1, 512)), full((1, 512))]


def _lru_pre_fwd(xs, cw, cb, wa, ba, wx, bx, lam):
    def body(xs_ref, cw_ref, cb_ref, wa_ref, ba_ref, wx_ref, bx_ref, lam_ref, a_ref, b_ref):
        a, b = _lru_pre(xs_ref[...], cw_ref[...], cb_ref[...], wa_ref[...], ba_ref[...], wx_ref[...], bx_ref[...],
                        lam_ref[...])
        a_ref[...] = a
        b_ref[...] = b

    row = pl.BlockSpec((ROWS, 512), lambda i: (i, 0))
    sds = jax.ShapeDtypeStruct((T, 512), F32)
    return pl.pallas_call(body, name="lru_pre_fwd", grid=(T // ROWS,), in_specs=_lru_pre_specs(),
                          out_specs=(row, row), out_shape=(sds, sds), compiler_params=_cp(("parallel",)),
                          )(xs, cw, cb, wa, ba, wx, bx, lam)


def _lru_pre_bwd(xs, cw, cb, wa, ba, wx, bx, lam, da, db):
    def body(xs_ref, cw_ref, cb_ref, wa_ref, ba_ref, wx_ref, bx_ref, lam_ref, da_ref, db_ref,
             dxs_ref, dcw_ref, dcb_ref, dwa_ref, dba_ref, dwx_ref, dbx_ref, dlam_ref):
        acc = (dcw_ref, dcb_ref, dwa_ref, dba_ref, dwx_ref, dbx_ref, dlam_ref)

        @pl.when(pl.program_id(0) == 0)
        def _():
            for r in acc:
                r[...] = jnp.zeros_like(r)

        _, vjp = jax.vjp(_lru_pre, xs_ref[...], cw_ref[...], cb_ref[...], wa_ref[...], ba_ref[...], wx_ref[...],
                         bx_ref[...], lam_ref[...])
        grads = vjp((da_ref[...], db_ref[...]))
        dxs_ref[...] = grads[0]
        for r, g in zip(acc, grads[1:]):
            r[...] += g

    row = pl.BlockSpec((ROWS, 512), lambda i: (i, 0))
    specs = _lru_pre_specs()
    sds = lambda *s: jax.ShapeDtypeStruct(s, F32)
    return pl.pallas_call(
        body, name="lru_pre_bwd", grid=(T // ROWS,), in_specs=specs + [row, row], out_specs=tuple(specs),
        out_shape=(sds(4, T, 512), sds(4, 512), sds(1, 512), sds(512, 512), sds(1, 512), sds(512, 512), sds(1, 512),
                   sds(1, 512)),
        compiler_params=_cp(("arbitrary",)),
    )(xs, cw, cb, wa, ba, wx, bx, lam, da, db)


SCAN_ROWS = 8


def _scan8(a, b, towards_later):
    row = _iota((SCAN_ROWS, 512), 0)
    for s in (1, 2, 4):
        if towards_later:
            keep, shift = row >= s, s
        else:
            keep, shift = row < SCAN_ROWS - s, SCAN_ROWS - s
        a_s = jnp.where(keep, pltpu.roll(a, shift, 0), 1.0)
        b_s = jnp.where(keep, pltpu.roll(b, shift, 0), 0.0)
        b = a * b_s + b
        a = a * a_s
    return a, b


def _lru_scan_fwd(a, b):
    def body(a_ref, b_ref, h_ref):
        def step(i, carry):
            rows = pl.ds(pl.multiple_of(i * SCAN_ROWS, SCAN_ROWS), SCAN_ROWS)
            a8, b8 = _scan8(a_ref[rows, :], b_ref[rows, :], True)
            h = a8 * carry + b8
            h_ref[rows, :] = h
            return jnp.broadcast_to(h[SCAN_ROWS - 1:, :], (SCAN_ROWS, 512))

        lax.fori_loop(0, T // SCAN_ROWS, step, jnp.zeros((SCAN_ROWS, 512), F32), unroll=2)

    return pl.pallas_call(body, name="lru_scan_fwd", out_shape=jax.ShapeDtypeStruct((T, 512), F32),
                          compiler_params=pltpu.CompilerParams(vmem_limit_bytes=VMEM_LIMIT))(a, b)


def _lru_scan_bwd(a_next, h_prev, dh):
    def body(a_ref, h_ref, dh_ref, da_ref, db_ref):
        def step(i, carry):
            start = T - SCAN_ROWS * (i + 1)
            rows = pl.ds(pl.multiple_of(start, SCAN_ROWS), SCAN_ROWS)
            a8, b8 = _scan8(a_ref[rows, :], dh_ref[rows, :], False)
            g = a8 * carry + b8
            db_ref[rows, :] = g
            da_ref[rows, :] = g * h_ref[rows, :]
            return jnp.broadcast_to(g[:1, :], (SCAN_ROWS, 512))

        lax.fori_loop(0, T // SCAN_ROWS, step, jnp.zeros((SCAN_ROWS, 512), F32), unroll=2)

    sds = jax.ShapeDtypeStruct((T, 512), F32)
    return pl.pallas_call(body, name="lru_scan_bwd", out_shape=(sds, sds),
                          compiler_params=pltpu.CompilerParams(vmem_limit_bytes=VMEM_LIMIT))(a_next, h_prev, dh)


def _lru_post(h, gate):
    return h * _gelu_tanh(gate)


def _lru_post_fwd(h, proj, cat):
    def body(h_ref, g_ref, cat_ref, o_ref):
        o_ref[...] = _lru_post(h_ref[...], g_ref[...])

    row = pl.BlockSpec((ROWS, 512), lambda i: (i, 0))
    return pl.pallas_call(body, name="lru_post_fwd", grid=(T // ROWS,),
                          in_specs=[row, pl.BlockSpec((ROWS, 512), lambda i: (i, 3)), pl.BlockSpec(memory_space=pl.ANY)],
                          out_specs=pl.BlockSpec((ROWS, 512), lambda i: (i, 1)),
                          out_shape=jax.ShapeDtypeStruct((T, D), F32), input_output_aliases={2: 0},
                          compiler_params=_cp(("parallel",)))(h, proj, cat)


def _lru_post_bwd(h, proj, dcat):
    def body(h_ref, g_ref, do_ref, dh_ref, dg_ref):
        _, vjp = jax.vjp(_lru_post, h_ref[...], g_ref[...])
        dh, dg = vjp(do_ref[...])
        dh_ref[...] = dh
        dg_ref[...] = dg

    row = pl.BlockSpec((ROWS, 512), lambda i: (i, 0))
    sds = jax.ShapeDtypeStruct((T, 512), F32)
    return pl.pallas_call(body, name="lru_post_bwd", grid=(T // ROWS,),
                          in_specs=[row, pl.BlockSpec((ROWS, 512), lambda i: (i, 3)),
                                    pl.BlockSpec((ROWS, 512), lambda i: (i, 1))],
                          out_specs=(row, row), out_shape=(sds, sds), compiler_params=_cp(("parallel",)))(h, proj, dcat)


def _conv_dx(dxs_shift):
    def body(d_ref, o_ref):
        o_ref[...] = d_ref[0] + d_ref[1] + d_ref[2] + d_ref[3]

    row = pl.BlockSpec((ROWS, 512), lambda i: (i, 0))
    return pl.pallas_call(body, name="lru_conv_dx", grid=(T // ROWS,),
                          in_specs=[pl.BlockSpec((4, ROWS, 512), lambda i: (0, i, 0))], out_specs=row,
                          out_shape=jax.ShapeDtypeStruct((T, 512), F32), compiler_params=_cp(("parallel",)))(dxs_shift)


def _position():
    return lax.axis_index("x"), lax.axis_index("y"), lax.axis_index("c")


def _other_chips(x, y):
    return [(1 - x, y), (x, 1 - y), (1 - x, 1 - y)]


def _al(v, n):
    return v * n if isinstance(v, int) else pl.multiple_of(v * n, n)


_AG_ITEMS = [
    ((4, 32, 128), lambda o, s, h: o.at[s, pl.ds(_al(h, 16), 16), :], lambda r, h: r.at[pl.ds(_al(h, 16), 16), :]),
    ((4, 774, 1024), lambda o, s, h: o.at[s, :, pl.ds(_al(h, 512), 512)], lambda r, h: r.at[:, pl.ds(_al(h, 512), 512)]),
    ((1024, 1024), lambda o, s, h: o.at[pl.ds(_al(2 * s + h, 128), 128), :], lambda r, h: r.at[pl.ds(_al(h, 128), 128), :]),
    ((2, 1024, 4096), lambda o, s, h: o.at[h, :, pl.ds(_al(s, 1024), 1024)], lambda r, h: r.at[h]),
    ((2, 4096, 1024), lambda o, s, h: o.at[h, pl.ds(_al(s, 1024), 1024), :], lambda r, h: r.at[h]),
    ((1024, 2560), lambda o, s, h: o.at[pl.ds(_al(h, 512), 512), pl.ds(_al(s, 640), 640)],
     lambda r, h: r.at[pl.ds(_al(h, 512), 512), :]),
    ((1024, 1024), lambda o, s, h: o.at[pl.ds(_al(2 * s + h, 128), 128), :], lambda r, h: r.at[pl.ds(_al(h, 128), 128), :]),
]


_AG_GROUPS = [(0, 1, 2), (3, 4), (5, 6)]

_HBM = pl.BlockSpec(memory_space=pltpu.HBM)
_SEM = pl.BlockSpec(memory_space=pltpu.SEMAPHORE)
_SPLIT = dict(has_side_effects=pltpu.SideEffectType.DATAFLOW_SIDE_EFFECTING)


def _hbm(a):
    return pltpu.with_memory_space_constraint(a, pltpu.HBM)


def _ag_ici_copy(i, j, chip, c, slot, src_ref, land_ref, send_sems, recv_sems, k):
    _, dst, half = _AG_ITEMS[i]
    return pltpu.make_async_remote_copy(src_ref=half(src_ref, c), dst_ref=dst(land_ref, slot, c), send_sem=send_sems.at[k],
                                        recv_sem=recv_sems.at[k], device_id=(*chip, c), device_id_type=MESH)


def _ag_start(groups, shards, name):
    items_all = [i for g in groups for i in _AG_GROUPS[g]]
    n = len(items_all)
    ng = len(groups)
    lands = [lax.empty(_AG_ITEMS[i][0], shards[i].dtype) for i in items_all]

    def body(*refs):
        srcs, land_refs = dict(zip(items_all, refs[:n])), dict(zip(items_all, refs[n:2 * n]))
        sems = refs[2 * n:2 * n + 2 * ng]
        token = refs[-1]
        x, y, c = _position()
        me = 2 * x + y
        for gi, g in enumerate(groups):
            for t, i in enumerate(_AG_GROUPS[g]):
                for j, chip in enumerate(_other_chips(x, y)):
                    _ag_ici_copy(i, j, chip, c, me, srcs[i], land_refs[i], sems[2 * gi], sems[2 * gi + 1], 3 * t + j).start()
        token[...] = jnp.zeros_like(token)

    sem_shapes = []
    for g in groups:
        sem_shapes += [pltpu.SemaphoreType.DMA((3 * len(_AG_GROUPS[g]),))] * 2
    ops = [shards[i] for i in items_all] + lands
    out = pl.pallas_call(
        body, name=name,
        out_shape=tuple(sem_shapes) + tuple(pltpu.HBM(a.shape, a.dtype) for a in ops) + (jax.ShapeDtypeStruct((8, 128), F32),),
        in_specs=(_HBM,) * (2 * n),
        out_specs=(_SEM,) * (2 * ng) + (_HBM,) * (2 * n) + (pl.BlockSpec(memory_space=pltpu.VMEM),),
        input_output_aliases={i: 2 * ng + i for i in range(2 * n)},
        compiler_params=pltpu.CompilerParams(**_SPLIT),
    )(*[_hbm(a) for a in ops])
    sems, thru, token = out[:2 * ng], out[2 * ng:-1], out[-1]
    return ({g: (sems[2 * gi], sems[2 * gi + 1]) for gi, g in enumerate(groups)},
            dict(zip(items_all, thru[:n])), dict(zip(items_all, thru[n:])), token)


def _ag_wait(g, sems, srcs, lands, after):
    items = _AG_GROUPS[g]
    m = len(items)

    def body(*refs):
        src_refs, land_refs = refs[:m], refs[m:2 * m]
        send_sems, recv_sems = refs[2 * m], refs[2 * m + 1]
        x, y, c = _position()
        for t, i in enumerate(items):
            for j, chip in enumerate(_other_chips(x, y)):
                cp = _ag_ici_copy(i, j, chip, c, 2 * chip[0] + chip[1], src_refs[t], land_refs[t], send_sems, recv_sems,
                                  3 * t + j)
                cp.wait_send()
                cp.wait_recv()

    ops = [srcs[i] for i in items] + [lands[i] for i in items]
    out = pl.pallas_call(
        body, name=f"allgather_wait_{g}",
        out_shape=tuple(pltpu.HBM(a.shape, a.dtype) for a in ops),
        in_specs=(_HBM,) * (2 * m) + (_SEM, _SEM, pl.BlockSpec(memory_space=pl.ANY)),
        out_specs=(_HBM,) * (2 * m),
        input_output_aliases={i: i for i in range(2 * m)},
        compiler_params=pltpu.CompilerParams(**_SPLIT),
    )(*ops, sems[0], sems[1], after)
    return list(out[:m]), list(out[m:])


def _ag_forward(g, srcs, lands):
    return _ag_sibling(_AG_GROUPS[g], srcs, lands, False, f"allgather_forward_{g}")


def _ag_push_own(srcs, lands):
    items = tuple(sorted(lands))
    out = _ag_sibling(items, [srcs[i] for i in items], [lands[i] for i in items], True, "allgather_push_own")
    return dict(zip(items, out))


def _ag_sibling(items, srcs, lands, own, name):
    m = len(items)
    per = 2 if own else 3

    def body(*refs):
        src_refs, in_refs, out_refs = refs[:m], refs[m:2 * m], refs[2 * m:3 * m]
        send_sems, recv_sems = refs[3 * m:]
        x, y, c = _position()
        sibling = (x, y, 1 - c)
        me = 2 * x + y
        if own:
            mine = theirs = [(me, 0), (me, 1)]
        else:
            slots = [2 * chip[0] + chip[1] for chip in _other_chips(x, y)]
            mine, theirs = [(s, c) for s in slots], [(s, 1 - c) for s in slots]
        sends = []
        for t, i in enumerate(items):
            _, dst, half = _AG_ITEMS[i]
            for k, (slot, hc) in enumerate(mine):
                src = half(src_refs[t], hc) if own else dst(in_refs[t], slot, hc)
                sends.append(pltpu.make_async_remote_copy(
                    src_ref=src, dst_ref=dst(out_refs[t], slot, hc), send_sem=send_sems.at[per * t + k],
                    recv_sem=recv_sems.at[per * t + k], device_id=sibling, device_id_type=MESH))
        for cp in sends:
            cp.start()
        for t, i in enumerate(items):
            dst = _AG_ITEMS[i][1]
            for k, (slot, hc) in enumerate(theirs):
                there = dst(out_refs[t], slot, hc)
                pltpu.make_async_remote_copy(src_ref=there, dst_ref=there, send_sem=send_sems.at[per * t + k],
                                             recv_sem=recv_sems.at[per * t + k], device_id=sibling,
                                             device_id_type=MESH).wait_recv()
        for cp in sends:
            cp.wait_send()

    any_spec = pl.BlockSpec(memory_space=pl.ANY)
    return pl.pallas_call(
        body, name=name,
        in_specs=[any_spec] * (2 * m), out_specs=(any_spec,) * m,
        out_shape=tuple(jax.ShapeDtypeStruct(a.shape, a.dtype) for a in lands),
        input_output_aliases={m + t: t for t in range(m)},
        scratch_shapes=[pltpu.SemaphoreType.DMA((per * m,)), pltpu.SemaphoreType.DMA((per * m,))],
    )(*srcs, *lands)


def _pair_swap_copy(g_ref, r_ref, send_sem, recv_sem):
    x, y, c = _position()
    hc = g_ref.shape[2] // 2
    return pltpu.make_async_remote_copy(src_ref=g_ref.at[:, :, pl.ds(_al(1 - c, hc), hc)], dst_ref=r_ref,
                                        send_sem=send_sem, recv_sem=recv_sem, device_id=(x, y, 1 - c),
                                        device_id_type=MESH)


def _pair_swap_start(gb, tag):
    _, rows, cols = gb.shape
    recv = lax.empty((4, rows, cols // 2), gb.dtype)

    def body(g_ref, r_ref, send_sem, recv_sem, g_thru, r_thru, token):
        _pair_swap_copy(g_ref, r_ref, send_sem, recv_sem).start()
        token[...] = jnp.zeros_like(token)

    return pl.pallas_call(
        body, name="grad_pair_swap_start_" + tag,
        out_shape=(pltpu.SemaphoreType.DMA(()), pltpu.SemaphoreType.DMA(()), pltpu.HBM(gb.shape, gb.dtype),
                   pltpu.HBM(recv.shape, recv.dtype), jax.ShapeDtypeStruct((8, 128), F32)),
        in_specs=(_HBM, _HBM), out_specs=(_SEM, _SEM, _HBM, _HBM, pl.BlockSpec(memory_space=pltpu.VMEM)),
        input_output_aliases={0: 2, 1: 3},
        compiler_params=pltpu.CompilerParams(**_SPLIT),
    )(_hbm(gb), _hbm(recv))


def _pair_swap_wait(started, after, tag):
    send_sem, recv_sem, gb, recv, _ = started

    def body(g_ref, r_ref, send_sem, recv_sem, after_ref, g_out, r_out):
        cp = _pair_swap_copy(g_ref, r_ref, send_sem, recv_sem)
        cp.wait_send()
        cp.wait_recv()

    return pl.pallas_call(
        body, name="grad_pair_swap_wait_" + tag,
        out_shape=(pltpu.HBM(gb.shape, gb.dtype), pltpu.HBM(recv.shape, recv.dtype)),
        in_specs=(_HBM, _HBM, _SEM, _SEM, pl.BlockSpec(memory_space=pl.ANY)), out_specs=(_HBM, _HBM),
        input_output_aliases={0: 0, 1: 1},
        compiler_params=pltpu.CompilerParams(**_SPLIT),
    )(gb, recv, send_sem, recv_sem, after)


def _handover_copy(r_ref, send_sem, recv_sem, core):
    x, y, c = _position()
    hc = r_ref.shape[1] // 2
    cols = r_ref.at[:, pl.ds(_al(core, hc), hc)]
    return pltpu.make_async_remote_copy(src_ref=cols, dst_ref=cols, send_sem=send_sem, recv_sem=recv_sem,
                                        device_id=(x, y, 1 - c), device_id_type=MESH)


def _handover_start(red, tag):
    def body(r_ref, send_sem, recv_sem, r_thru, token):
        _handover_copy(r_ref, send_sem, recv_sem, lax.axis_index("c")).start()
        token[...] = jnp.zeros_like(token)

    return pl.pallas_call(
        body, name="grad_handover_start_" + tag,
        out_shape=(pltpu.SemaphoreType.DMA(()), pltpu.SemaphoreType.DMA(()), pltpu.HBM(red.shape, red.dtype),
                   jax.ShapeDtypeStruct((8, 128), F32)),
        in_specs=(_HBM,), out_specs=(_SEM, _SEM, _HBM, pl.BlockSpec(memory_space=pltpu.VMEM)),
        input_output_aliases={0: 2},
        compiler_params=pltpu.CompilerParams(**_SPLIT),
    )(_hbm(red))


def _handover_wait(started, after, tag):
    send_sem, recv_sem, red, _ = started

    def body(r_ref, send_sem, recv_sem, after_ref, r_out):
        c = lax.axis_index("c")
        _handover_copy(r_ref, send_sem, recv_sem, c).wait_send()
        _handover_copy(r_ref, send_sem, recv_sem, 1 - c).wait_recv()

    return pl.pallas_call(
        body, name="grad_handover_wait_" + tag,
        out_shape=pltpu.HBM(red.shape, red.dtype),
        in_specs=(_HBM, _SEM, _SEM, pl.BlockSpec(memory_space=pl.ANY)), out_specs=_HBM,
        input_output_aliases={0: 0},
        compiler_params=pltpu.CompilerParams(**_SPLIT),
    )(red, send_sem, recv_sem, after)


def _handover(red, tag):
    started = _handover_start(red, tag)
    return _handover_wait(started, started[3], tag)


def _a2a_copy(j, chip, c, p_ref, q_ref, q_slot, send_sems, recv_sems):
    return pltpu.make_async_remote_copy(src_ref=p_ref.at[2 * chip[0] + chip[1]], dst_ref=q_ref.at[q_slot],
                                        send_sem=send_sems.at[j], recv_sem=recv_sems.at[j], device_id=(*chip, c),
                                        device_id_type=MESH)


def _a2a_start(p, tag):
    def body(p_ref, q_ref, send_sems, recv_sems, p_thru, q_thru, token):
        x, y, c = _position()
        for j, chip in enumerate(_other_chips(x, y)):
            _a2a_copy(j, chip, c, p_ref, q_ref, 2 * x + y, send_sems, recv_sems).start()
        token[...] = jnp.zeros_like(token)

    return pl.pallas_call(
        body, name="grad_alltoall_start_" + tag,
        out_shape=(pltpu.SemaphoreType.DMA((3,)), pltpu.SemaphoreType.DMA((3,)), pltpu.HBM(p.shape, p.dtype),
                   pltpu.HBM(p.shape, p.dtype), jax.ShapeDtypeStruct((8, 128), F32)),
        in_specs=(_HBM, _HBM), out_specs=(_SEM, _SEM, _HBM, _HBM, pl.BlockSpec(memory_space=pltpu.VMEM)),
        input_output_aliases={0: 2, 1: 3},
        compiler_params=pltpu.CompilerParams(**_SPLIT),
    )(_hbm(p), _hbm(lax.empty(p.shape, p.dtype)))


def _a2a_wait(send_sems, recv_sems, p, q, after, tag):
    def body(p_ref, q_ref, send_sems, recv_sems, after_ref, p_out, q_out):
        x, y, c = _position()
        for j, chip in enumerate(_other_chips(x, y)):
            cp = _a2a_copy(j, chip, c, p_ref, q_ref, 2 * chip[0] + chip[1], send_sems, recv_sems)
            cp.wait_send()
            cp.wait_recv()

    return pl.pallas_call(
        body, name="grad_alltoall_wait_" + tag,
        out_shape=(pltpu.HBM(p.shape, p.dtype), pltpu.HBM(q.shape, q.dtype)),
        in_specs=(_HBM, _HBM, _SEM, _SEM, pl.BlockSpec(memory_space=pl.ANY)), out_specs=(_HBM, _HBM),
        input_output_aliases={0: 0, 1: 1},
        compiler_params=pltpu.CompilerParams(**_SPLIT),
    )(p, q, send_sems, recv_sems, after)


def _comm_rows(rows):
    return next(t for t in (512, 384, 256, 128) if rows % t == 0)


def _pair_add(gb, recv, where, tag):
    _, rows, cols = gb.shape
    hc = cols // 2
    tr = _comm_rows(rows)

    def body(w_ref, g_ref, r_ref, o_ref):
        o_ref[...] = (g_ref[...].astype(F32) + r_ref[...].astype(F32)).astype(o_ref.dtype)

    return pl.pallas_call(
        body, name="grad_pair_add_" + tag,
        grid_spec=pltpu.PrefetchScalarGridSpec(
            num_scalar_prefetch=1, grid=(4, rows // tr),
            in_specs=[pl.BlockSpec((None, tr, hc), lambda s, j, w_ref: (s, j, w_ref[0])),
                      pl.BlockSpec((None, tr, hc), lambda s, j, w_ref: (s, j, 0))],
            out_specs=pl.BlockSpec((None, tr, hc), lambda s, j, w_ref: (s, j, 0))),
        out_shape=jax.ShapeDtypeStruct((4, rows, hc), gb.dtype),
        compiler_params=_cp(("parallel", "parallel")),
    )(where, gb, recv)


def _sum_chips(p, q, where, tag):
    _, rows, hc = q.shape
    tr = _comm_rows(rows)

    def body(w_ref, p_ref, qa_ref, qb_ref, qc_ref, o_ref):
        me = w_ref[1]
        own, qa, qb, qc = (r[...].astype(F32) for r in (p_ref, qa_ref, qb_ref, qc_ref))
        v0 = jnp.where(me == 0, own, qa)
        v1 = jnp.where(me == 1, own, jnp.where(me == 0, qa, qb))
        v2 = jnp.where(me == 2, own, jnp.where(me < 2, qb, qc))
        v3 = jnp.where(me == 3, own, qc)
        o_ref[...] = ((v0 + v1) + v2) + v3

    slot = lambda k: pl.BlockSpec((None, tr, hc), lambda j, w_ref: (w_ref[k], j, 0))
    return pl.pallas_call(
        body, name="grad_sum_chips_" + tag,
        grid_spec=pltpu.PrefetchScalarGridSpec(
            num_scalar_prefetch=1, grid=(rows // tr,),
            in_specs=[slot(1), slot(2), slot(3), slot(4)],
            out_specs=pl.BlockSpec((tr, hc), lambda j, w_ref: (j, w_ref[0]))),
        out_shape=jax.ShapeDtypeStruct((rows, 2 * hc), F32),
        compiler_params=_cp(("parallel",)),
    )(where, p, q, q, q)


def _shard_major(g, axis):
    shape = g.shape
    g = g.reshape(shape[:axis] + (4, shape[axis] // 4) + shape[axis + 1:])
    return jnp.moveaxis(g, axis, 0).reshape(4, -1)


def _unshard(g4, shape, axis):
    n = shape[axis] // 4
    g = g4.reshape((4,) + shape[:axis] + (n,) + shape[axis + 1:])
    return jnp.moveaxis(g, 0, axis).reshape(shape)


def _split(flat, shapes):
    out, off = [], 0
    for shp in shapes:
        n = 1
        for d in shp:
            n *= d
        out.append(flat[..., off:off + n].reshape(flat.shape[:-1] + tuple(shp)))
        off += n
    return out


def _even_rows_to_kernel(wt):
    return jnp.concatenate([wt[:1536], wt[1552:3088], wt[1536:1552], wt[3088:3096],
                            jnp.zeros((PE - 3096, wt.shape[1]), wt.dtype)], axis=0)


def _block_diag(w):
    eye = jnp.eye(8, dtype=w.dtype)
    return (w[:, :, None, :] * eye[:, None, :, None]).reshape(512, 512)


def _diag_blocks(g):
    eye = jnp.eye(8, dtype=g.dtype)
    return (g.reshape(8, 64, 8, 64) * eye[:, None, :, None]).sum(axis=2)


def _shift_down(a, s):
    return a if s == 0 else jnp.pad(a, ((s, 0), (0, 0)))[:a.shape[0]]


def _shift_up(a, s):
    return a if s == 0 else jnp.pad(a, ((0, s), (0, 0)))[s:]


SMALL_SHARDED_SHAPES = [(2, 4, 256), (16, 64), (4, 128), (128,), (128,), (128,), (128,)]
REPL_SHAPES = [(256,), (512,), (8,), (8, 257), (8, 64, 64), (8, 64, 64)]


def kernel(x, norm_w, w_in_even, gla_w_a_up, gla_b_a, gla_norm_w, fox_b_f, w_out_even, w_in_odd, rel_bias, conv_w, conv_b, lru_w_a, lru_b_a, lru_w_x, lru_b_x, lru_lambda, w_out_odd, w_mlp_up, w_mlp_down, loss_target, m_norm_w, m_w_in_even, m_gla_w_a_up, m_gla_b_a, m_gla_norm_w, m_fox_b_f, m_w_out_even, m_w_in_odd, m_rel_bias, m_conv_w, m_conv_b, m_lru_w_a, m_lru_b_a, m_lru_w_x, m_lru_b_x, m_lru_lambda, m_w_out_odd, m_w_mlp_up, m_w_mlp_down, v_norm_w, v_w_in_even, v_gla_w_a_up, v_gla_b_a, v_gla_norm_w, v_fox_b_f, v_w_out_even, v_w_in_odd, v_rel_bias, v_conv_w, v_conv_b, v_lru_w_a, v_lru_b_a, v_lru_w_x, v_lru_b_x, v_lru_lambda, v_w_out_odd, v_w_mlp_up, v_w_mlp_down):
    c_idx = lax.axis_index("c")

    small_local = [norm_w, gla_w_a_up[0], conv_w[0], conv_b[0], lru_b_a[0], lru_b_x[0], lru_lambda[0]]
    small_src = jnp.concatenate([a.reshape(-1) for a in small_local]).reshape(32, 128)
    first = {0: small_src, 1: w_in_even[0].T.astype(BF16), 2: w_out_even[0].astype(BF16)}
    sems0, srcs0, lands0, ag_token = _ag_start([0], first, "allgather_start_0")
    zero = ag_token[0, 0]
    later = {3: (w_mlp_up + zero).astype(BF16), 4: (w_mlp_down + zero).astype(BF16),
             5: (w_in_odd[0] + zero).astype(BF16), 6: (w_out_odd[0] + zero).astype(BF16)}
    sems1, srcs1, lands1, ag_token = _ag_start([1, 2], later, "allgather_start_1")
    ag_sems, ag_srcs = {**sems0, **sems1}, {**srcs0, **srcs1}
    ag_lands = _ag_push_own(ag_srcs, {**lands0, **lands1})

    def gathered(g, after):
        srcs_g, lands_g = _ag_wait(g, ag_sems[g], ag_srcs, ag_lands, after)
        return _ag_forward(g, srcs_g, lands_g)

    small4, w_in_e4, w_out_e = gathered(0, ag_token)
    me = 2 * lax.axis_index("x") + lax.axis_index("y")
    others = [k + (k >= me).astype(jnp.int32) for k in range(3)]
    where = jnp.stack([c_idx, me] + others).astype(jnp.int32)

    w_in_e_t = _even_rows_to_kernel(w_in_e4.reshape(3096, D))
    g_small = _split(small4.reshape(4, 32 * 128), SMALL_SHARDED_SHAPES)
    nw_full = _unshard(g_small[0], (2, 4, 1024), 2)
    wa_up = _unshard(g_small[1], (16, 256), 1)
    cw = _unshard(g_small[2], (4, 512), 1)
    cb, lba, lbx, lam = [_unshard(g, (512,), 0).reshape(1, 512) for g in g_small[3:]]
    nw = lambda layer, i: nw_full[layer, i].reshape(1, D)

    wa_pad = jnp.pad(wa_up, ((0, 128 - 16), (0, 0)))
    gla_ba = gla_b_a.reshape(1, 256)
    gla_nw = gla_norm_w.reshape(1, 512)
    fox_bpad = jnp.pad(fox_b_f.reshape(1, 8), ((0, 0), (FOX_LANE0, 128 - FOX_LANE0 - 8)))
    rbp = jnp.pad(rel_bias[0], ((0, 0), (0, REL_PAD - 257)))
    wa_bd = _block_diag(lru_w_a[0])
    wx_bd = _block_diag(lru_w_x[0])

    x0 = x[0]
    tgt = loss_target[0]

    h0 = _prenorm(x0, nw(0, 0), "prenorm_l0_mix")
    proj_e = _mm(h0, w_in_e_t, "nt", tm=2048, tn=640, name="mm_in_even")
    cat0, s_prev = _gla_fwd(proj_e, wa_pad, gla_ba, gla_nw)
    cum_r = _fox_gate_fwd(proj_e, fox_bpad)
    cum_c = cum_r[:, FOX_LANE0:FOX_LANE0 + 8].T
    cat0 = _fox_fwd(proj_e, cum_c, cat0)
    mix0 = _mm(cat0, w_out_e, "nn", tm=2048, tn=512, name="mm_out_even")
    x1, h1 = _post_pre_fwd(x0, mix0, nw(0, 1), nw(0, 2), "post_pre_l0_mix")
    w_up, w_dn = gathered(1, x1)
    a0, r0 = _mm(h1, w_up, "nn", tm=2048, tn=1024, b_layer=0, relu_pair=True, name="mm_up_l0")
    d0 = _mm(a0, w_dn, "nn", tm=1024, tn=512, b_layer=0, name="mm_down_l0")
    x2, h2 = _post_pre_fwd(x1, d0, nw(0, 3), nw(1, 0), "post_pre_l0_mlp")

    w_in_o, w_out_o = gathered(2, x2)
    proj_o = _mm(h2, w_in_o, "nn", tm=2048, tn=640, name="mm_in_odd")
    bias_q = _bias_build(rbp)
    bias = bias_q.transpose(1, 0, 2)
    kvpad = jnp.pad(proj_o[:, 512:1536], ((CA_PAD, 0), (0, 0)))
    cat1 = _ca_fwd(proj_o, kvpad, bias)
    x_in = proj_o[:, 2048:2560]
    xs = jnp.stack([_shift_down(x_in, 3 - j) for j in range(4)])
    lru_a, lru_b = _lru_pre_fwd(xs, cw, cb, wa_bd, lba, wx_bd, lbx, lam)
    hh = _lru_scan_fwd(lru_a, lru_b)
    cat1 = _lru_post_fwd(hh, proj_o, cat1)
    mix1 = _mm(cat1, w_out_o, "nn", tm=2048, tn=512, name="mm_out_odd")
    x3, h3 = _post_pre_fwd(x2, mix1, nw(1, 1), nw(1, 2), "post_pre_l1_mix")
    a1, r1 = _mm(h3, w_up, "nn", tm=2048, tn=1024, b_layer=1, relu_pair=True, name="mm_up_l1")
    d1 = _mm(a1, w_dn, "nn", tm=1024, tn=512, b_layer=1, name="mm_down_l1")
    g4, loss_part, dd1, dnw13 = _post_loss(x3, d1, nw(1, 3), tgt)
    loss = lax.psum(loss_part[0, 0], ("x", "y", "c"))

    def rs_begin(swap, after, tag):
        gb, recv = _pair_swap_wait(swap, after, tag)
        return _a2a_start(_pair_add(gb, recv, where, tag), tag)

    def rs_end(started, after, tag):
        send_sems, recv_sems, p, q, _ = started
        p, q = _a2a_wait(send_sems, recv_sems, p, q, after, tag)
        return _handover(_sum_chips(p, q, where, tag), tag)

    gba = lax.dynamic_update_slice(lax.empty((4, GA_ROWS, D), BF16), jnp.zeros((4, GA_UP - GA_GAP, D), BF16),
                                   (0, GA_GAP, 0))
    gba = _mm(a1, dd1, "tn", tm=512, tn=1024, into=(gba, 1024, GA_DN), name="mm_down_l1_dw")
    du1 = _mm(dd1, w_dn, "nt", tm=2048, tn=1024, b_layer=1, times2=r1, out_dtype=BF16, name="mm_down_l1_dx")
    gba = _mm(du1, h3, "tn", tm=512, tn=1024, into=(gba, 1024, GA_UP), name="mm_up_l1_dw")
    dh3 = _mm(du1, w_up, "nt", tm=1024, tn=512, b_layer=1, name="mm_up_l1_dx")
    g3, dmix1, dnw12, dnw11 = _pre_post_bwd(x3, nw(1, 2), dh3, g4, mix1, nw(1, 1), "pre_post_bwd_l1_mlp")
    gba = _mm(cat1, dmix1, "tn", tm=128, tn=1024, into=(gba, 256, GA_OUT_O), name="mm_out_odd_dw")
    dcat1 = _mm(dmix1, w_out_o, "nt", tm=2048, tn=512, name="mm_out_odd_dx")

    dq_c, dkpad, dvpad, dbias = _ca_bwd(proj_o, kvpad, bias, dcat1)
    g_rel = _bias_grad(jnp.pad(dbias.transpose(1, 0, 2), ((0, 0), (0, 0), (0, BIAS_W - CA_BAND))))[:, :257]
    dhh, dgate = _lru_post_bwd(hh, proj_o, dcat1)
    da_l, db_l = _lru_scan_bwd(_shift_up(lru_a, 1), _shift_down(hh, 1), dhh)
    dxs, g_cw, g_cb, g_wa_bd, g_lba, g_wx_bd, g_lbx, g_lam = _lru_pre_bwd(xs, cw, cb, wa_bd, lba, wx_bd, lbx, lam, da_l, db_l)
    dx_in = _conv_dx(jnp.stack([_shift_up(dxs[j], 3 - j) for j in range(4)]))
    dproj_o = jnp.concatenate([dq_c, dkpad[CA_PAD:], dvpad[CA_PAD:], dgate, dx_in], axis=1).astype(BF16)
    gba = _mm(dproj_o, h2, "tn", tm=128, tn=1024, into=(gba, 640, GA_IN_O), name="mm_in_odd_dw")
    swap_a = _pair_swap_start(gba, "a")
    dh2 = _mm(dproj_o, w_in_o, "nt", tm=1024, tn=512, name="mm_in_odd_dx")
    g2, dd0, dnw10, dnw03 = _pre_post_bwd(x2, nw(1, 0) + swap_a[4][0, 0], dh2, g3, d0, nw(0, 3), "pre_post_bwd_l1_mix")
    rs_a = rs_begin(swap_a, g2, "a")

    gbb = lax.empty((4, GB_ROWS, D), BF16)
    gbb = _mm(a0, dd0, "tn", tm=512, tn=1024, into=(gbb, 1024, GB_DN), name="mm_down_l0_dw")
    du0 = _mm(dd0, w_dn, "nt", tm=2048, tn=1024, b_layer=0, times2=r0, out_dtype=BF16, name="mm_down_l0_dx")
    gbb = _mm(du0, h1, "tn", tm=512, tn=1024, into=(gbb, 1024, GB_UP), name="mm_up_l0_dw")
    swap_b = _pair_swap_start(gbb, "b")
    dh1 = _mm(du0, w_up, "nt", tm=1024, tn=512, b_layer=0, name="mm_up_l0_dx")
    g1, dmix0, dnw02, dnw01 = _pre_post_bwd(x1, nw(0, 2) + (swap_b[4][0, 0] + rs_a[4][0, 0]), dh1, g2, mix0, nw(0, 1),
                                            "pre_post_bwd_l0_mlp")
    rs_b = rs_begin(swap_b, g1, "b")
    gbc = lax.empty((4, GC_ROWS, D), BF16)
    gbc = _mm(cat0, dmix0, "tn", tm=128, tn=1024, into=(gbc, 256, GC_OUT_E), name="mm_out_even_dw")
    dcat0 = _mm(dmix0, w_out_e, "nt", tm=2048, tn=512, name="mm_out_even_dx")

    dq_g, dk_g, dv_g, dr_g, daux_g, g_wa_pad, g_gla_ba, g_gla_nw = _gla_bwd(
        proj_e, s_prev, wa_pad, gla_ba, gla_nw + rs_b[4][0, 0], dcat0)
    dq_f, dk_f, dv_f, dccol = _fox_bwd(proj_e, cum_c, dcat0)
    dccol_t = jnp.pad(dccol.sum(axis=0).T, ((0, 0), (FOX_LANE0, 128 - FOX_LANE0 - 8)))
    daux, g_fox_bpad = _fox_gate_bwd(proj_e, fox_bpad, dccol_t, daux_g)
    dproj_e = jnp.concatenate([dq_g, dk_g, dv_g, dr_g, dq_f, dk_f, dv_f, daux], axis=1).astype(BF16)
    gt_in_e = _mm(dproj_e, h0, "tn", tm=640, tn=1024, out_dtype=BF16, name="mm_in_even_dw")
    dh0 = _mm(dproj_e, w_in_e_t, "nn", tm=1024, tn=512, name="mm_in_even_dx")
    grad_x, dnw00 = _norm_bwd(x0, nw(0, 0), dh0, g1, "prenorm_l0_mix_bwd")

    def rs_reduce(started, after, tag):
        send_sems, recv_sems, p, q, _ = started
        p, q = _a2a_wait(send_sems, recv_sems, p, q, after, tag)
        return _handover_start(_sum_chips(p, q, where, tag), tag)

    ho_a = rs_reduce(rs_a, grad_x, "a")
    ho_b = rs_reduce(rs_b, ho_a[3], "b")

    g_norm = jnp.stack([jnp.concatenate([dnw00, dnw01, dnw02, dnw03]), jnp.concatenate([dnw10, dnw11, dnw12, dnw13])])
    sharded = [(g_norm, 2), (g_wa_pad[:16], 1), (g_cw, 1), (g_cb[0], 0), (g_lba[0], 0), (g_lbx[0], 0), (g_lam[0], 0)]
    replicated = [g_gla_ba[0], g_gla_nw[0], g_fox_bpad[0, FOX_LANE0:FOX_LANE0 + 8], g_rel, _diag_blocks(g_wa_bd),
                  _diag_blocks(g_wx_bd)]
    small4 = jnp.concatenate([_shard_major(g, ax) for g, ax in sharded]
                             + [jnp.broadcast_to(g.reshape(1, -1), (4, g.size)) for g in replicated], axis=1)
    n_small = small4.shape[1]
    small_rows = GC_ROWS - GC_TAIL - 774
    small4 = jnp.pad(small4, ((0, 0), (0, small_rows * D - n_small))).reshape(4, small_rows, D)
    gt_rows = jnp.concatenate([gt_in_e[:1536], gt_in_e[3072:3088], gt_in_e[1536:3072], gt_in_e[3088:3096]], axis=0)
    tail = jnp.concatenate([gt_rows.reshape(4, 774, D), small4.astype(BF16)], axis=1)
    gbc = lax.dynamic_update_slice(gbc, tail, (0, GC_TAIL, 0))
    swap_c = _pair_swap_start(gbc, "c")
    rs_c = rs_begin(swap_c, swap_c[4], "c")

    red_a = _handover_wait(ho_a, rs_c[4], "a")
    red_b = _handover_wait(ho_b, red_a, "b")
    early = dict(
        w_mlp_up=_adamw_from(w_mlp_up, m_w_mlp_up, v_w_mlp_up, [(red_b, GB_UP, True), (red_a, GA_UP, True)], 256,
                             "adamw_w_mlp_up"),
        w_mlp_down=_adamw_from(w_mlp_down, m_w_mlp_down, v_w_mlp_down, [(red_b, GB_DN, False), (red_a, GA_DN, False)],
                               256, "adamw_w_mlp_down"),
        w_in_odd=_adamw_from(w_in_odd, m_w_in_odd, v_w_in_odd, [(red_a, GA_IN_O, True)], 256, "adamw_w_in_odd"),
        w_out_odd=_adamw_from(w_out_odd, m_w_out_odd, v_w_out_odd, [(red_a, GA_OUT_O, False)], 128, "adamw_w_out_odd"))
    red_c = rs_end(rs_c, early["w_out_odd"][3], "c")

    g_small = _split(red_c[GC_TAIL + 774:].reshape(-1)[:n_small], SMALL_SHARDED_SHAPES + REPL_SHAPES)
    g_of = dict(zip(["norm_w", "gla_w_a_up", "conv_w", "conv_b", "lru_b_a", "lru_b_x", "lru_lambda", "gla_b_a",
                     "gla_norm_w", "fox_b_f", "rel_bias", "lru_w_a", "lru_w_x"], g_small))
    g_of.update(w_in_even=red_c[GC_TAIL:GC_TAIL + 774])
    early["w_out_even"] = _adamw_from(w_out_even, m_w_out_even, v_w_out_even, [(red_c, GC_OUT_E, False)], 256,
                                      "adamw_w_out_even")

    names = ["norm_w", "w_in_even", "gla_w_a_up", "gla_b_a", "gla_norm_w", "fox_b_f", "w_out_even", "w_in_odd", "rel_bias",
             "conv_w", "conv_b", "lru_w_a", "lru_b_a", "lru_w_x", "lru_b_x", "lru_lambda", "w_out_odd", "w_mlp_up",
             "w_mlp_down"]
    w_of = dict(norm_w=norm_w, w_in_even=w_in_even, gla_w_a_up=gla_w_a_up, gla_b_a=gla_b_a, gla_norm_w=gla_norm_w,
                fox_b_f=fox_b_f, w_out_even=w_out_even, w_in_odd=w_in_odd, rel_bias=rel_bias, conv_w=conv_w, conv_b=conv_b,
                lru_w_a=lru_w_a, lru_b_a=lru_b_a, lru_w_x=lru_w_x, lru_b_x=lru_b_x, lru_lambda=lru_lambda,
                w_out_odd=w_out_odd, w_mlp_up=w_mlp_up, w_mlp_down=w_mlp_down)
    m_of = dict(norm_w=m_norm_w, w_in_even=m_w_in_even, gla_w_a_up=m_gla_w_a_up, gla_b_a=m_gla_b_a,
                gla_norm_w=m_gla_norm_w, fox_b_f=m_fox_b_f, w_out_even=m_w_out_even, w_in_odd=m_w_in_odd,
                rel_bias=m_rel_bias, conv_w=m_conv_w, conv_b=m_conv_b, lru_w_a=m_lru_w_a, lru_b_a=m_lru_b_a,
                lru_w_x=m_lru_w_x, lru_b_x=m_lru_b_x, lru_lambda=m_lru_lambda, w_out_odd=m_w_out_odd,
                w_mlp_up=m_w_mlp_up, w_mlp_down=m_w_mlp_down)
    v_of = dict(norm_w=v_norm_w, w_in_even=v_w_in_even, gla_w_a_up=v_gla_w_a_up, gla_b_a=v_gla_b_a,
                gla_norm_w=v_gla_norm_w, fox_b_f=v_fox_b_f, w_out_even=v_w_out_even, w_in_odd=v_w_in_odd,
                rel_bias=v_rel_bias, conv_w=v_conv_w, conv_b=v_conv_b, lru_w_a=v_lru_w_a, lru_b_a=v_lru_b_a,
                lru_w_x=v_lru_w_x, lru_b_x=v_lru_b_x, lru_lambda=v_lru_lambda, w_out_odd=v_w_out_odd,
                w_mlp_up=v_w_mlp_up, w_mlp_down=v_w_mlp_down)
    grads, deltas, new_ms, new_vs = [], [], [], []
    for n in names:
        w = w_of[n]
        if n in early:
            g, d, mn, vn = early[n]
            grads.append(g)
            deltas.append(d)
            new_ms.append(mn)
            new_vs.append(vn)
            continue
        if n == "w_in_even":
            to_view = lambda a: a[0].T
            from_view = lambda a: a.T[None]
        else:
            view = w.shape if w.ndim <= 3 else w.shape[-3:]
            to_view = lambda a, view=view: a.reshape(view)
            from_view = lambda a, w=w: a.reshape(w.shape)
        g = g_of[n] if n == "w_in_even" else to_view(g_of[n])
        d, mn, vn = _adamw(to_view(w), g, to_view(m_of[n]), to_view(v_of[n]), "adamw_" + n)
        grads.append(from_view(g))
        deltas.append(from_view(d))
        new_ms.append(from_view(mn))
        new_vs.append(from_view(vn))

    return (loss, grad_x.reshape(1, T, D), *grads, *deltas, *new_ms, *new_vs)
```

```python
import functools

import jax
import jax.numpy as jnp
from jax import lax
from jax.experimental import pallas as pl
from jax.experimental.pallas import tpu as pltpu

F32 = jnp.float32
BF16 = jnp.bfloat16
MESH = pl.DeviceIdType.MESH

T = 2048
D = 1024
DFF = 4096
EPS = 1e-6
CHUNK = 64
NCHUNK = T // CHUNK
PE = 3200
PO = 2560
AUX_BLK = 3072 // 128
FOX_LANE0 = 16
GLA_SCALE = 64 ** -0.5
ATT_SCALE = 64 ** -0.5
NEG = float(jnp.finfo(jnp.float32).min)
CA_BAND = 576
CA_PAD = 512
REL_PAD = 384

VMEM_LIMIT = 48 * 1024 * 1024

ADAM_LR, ADAM_B1, ADAM_B2, ADAM_EPS, ADAM_WD, ADAM_STEP = 0.001, 0.9, 0.999, 1e-08, 0.01, 10

GA_ROWS, GA_IN_O, GA_OUT_O, GA_GAP, GA_UP, GA_DN = 3072, 0, 640, 896, 1024, 2048
GB_ROWS, GB_UP, GB_DN = 2048, 0, 1024
GC_ROWS, GC_OUT_E, GC_TAIL = 1152, 0, 256

_DIMS = {"nn": (((1,), (0,)), ((), ())), "nt": (((1,), (1,)), ((), ())), "tn": (((0,), (0,)), ((), ()))}


def _cp(sem, **kw):
    return pltpu.CompilerParams(dimension_semantics=sem, vmem_limit_bytes=VMEM_LIMIT, **kw)


def _dot(a, b, mode):
    return lax.dot_general(a.astype(BF16), b.astype(BF16), _DIMS[mode], preferred_element_type=F32)


@functools.partial(jax.custom_vjp, nondiff_argnums=(2,))
def bdot(a, b, mode):
    return _dot(a, b, mode)


def _bdot_fwd(a, b, mode):
    return _dot(a, b, mode), (a, b)


def _bdot_bwd(mode, res, g):
    a, b = res
    if mode == "nn":
        da, db = _dot(g, b, "nt"), _dot(a, g, "tn")
    elif mode == "nt":
        da, db = _dot(g, b, "nn"), _dot(g, a, "tn")
    else:
        da, db = _dot(b, g, "nt"), _dot(a, g, "nn")
    return da.astype(a.dtype), db.astype(b.dtype)


bdot.defvjp(_bdot_fwd, _bdot_bwd)


def _hdot_raw(a, b, mode):
    return lax.dot_general(a, b, _DIMS[mode], precision=lax.Precision.HIGHEST, preferred_element_type=F32)


@functools.partial(jax.custom_vjp, nondiff_argnums=(2,))
def hdot(a, b, mode):
    return _hdot_raw(a, b, mode)


def _hdot_fwd(a, b, mode):
    return _hdot_raw(a, b, mode), (a, b)


def _hdot_bwd(mode, res, g):
    a, b = res
    if mode == "nn":
        return _hdot_raw(g, b, "nt"), _hdot_raw(a, g, "tn")
    if mode == "nt":
        return _hdot_raw(g, b, "nn"), _hdot_raw(g, a, "tn")
    return _hdot_raw(b, g, "nt"), _hdot_raw(a, g, "nn")


hdot.defvjp(_hdot_fwd, _hdot_bwd)


def _log_sigmoid(x):
    return jnp.minimum(x, 0.0) - jnp.log(1.0 + jnp.exp(-jnp.abs(x)))


def _sigmoid(x):
    return 1.0 / (1.0 + jnp.exp(-x))


def _expm1(x):
    series = x * (1.0 + x * 0.5 * (1.0 + x * (1.0 / 3.0) * (1.0 + x * 0.25)))
    return jnp.where(jnp.abs(x) < 0.03, series, jnp.exp(x) - 1.0)


def _gelu_tanh(x):
    return 0.5 * x * (1.0 + jnp.tanh(0.7978845608028654 * (x + 0.044715 * x * x * x)))


def _iota(shape, dim):
    return lax.broadcasted_iota(jnp.int32, shape, dim)


def _mm(a, b, mode, *, tm, tn, tk=None, out_dtype=F32, name, b_layer=None, into=None, relu_pair=False, times2=None):
    b2 = b.shape[-2:]
    if mode == "nn":
        (m, k), n = a.shape, b2[1]
    elif mode == "nt":
        (m, k), n = a.shape, b2[0]
    else:
        (k, m), n = a.shape, b2[1]
    tk = k if tk is None else tk
    assert m % tm == 0 and n % tn == 0 and k % tk == 0, (name, a.shape, b.shape)
    nk = k // tk
    if mode == "tn":
        a_spec = pl.BlockSpec((tk, tm), lambda i, j, kk: (kk, i))
    elif m == tm and nk == 1:
        a_spec = pl.BlockSpec((tm, tk), lambda i, j, kk: (i, kk), pipeline_mode=pl.Buffered(1))
    else:
        a_spec = pl.BlockSpec((tm, tk), lambda i, j, kk: (i, kk))
    b_blk = {"nn": (tk, tn), "nt": (tn, tk), "tn": (tk, tn)}[mode]
    b_idx = {"nn": lambda i, j, kk: (kk, j), "nt": lambda i, j, kk: (j, kk), "tn": lambda i, j, kk: (kk, j)}[mode]
    if b_layer is None:
        b_spec = pl.BlockSpec(b_blk, b_idx)
    else:
        b_spec = pl.BlockSpec((None,) + b_blk, lambda i, j, kk: (b_layer,) + b_idx(i, j, kk))

    tile = pl.BlockSpec((tm, tn), lambda i, j, kk: (i, j))
    if into is not None:
        buf, per_slot, row_off = into
        assert m == 4 * per_slot and per_slot % tm == 0 and row_off % tm == 0 and buf.shape[2] == n, (name, buf.shape)
        bps = per_slot // tm
        out_specs = pl.BlockSpec((None, tm, tn), lambda i, j, kk: (i // bps, row_off // tm + i % bps, j))
        out_shape = jax.ShapeDtypeStruct(buf.shape, buf.dtype)
        extra_in, extra_specs, aliases = [buf], [pl.BlockSpec(memory_space=pl.ANY)], {2: 0}
        finish = lambda acc, extra: [acc.astype(buf.dtype)]
    elif relu_pair:
        out_specs = (tile, tile)
        out_shape = (jax.ShapeDtypeStruct((m, n), BF16),) * 2
        extra_in, extra_specs, aliases = [], [], {}

        def finish(acc, extra):
            r = jnp.maximum(acc, 0.0)
            return [(r * r).astype(BF16), r.astype(BF16)]
    elif times2 is not None:
        out_specs = tile
        out_shape = jax.ShapeDtypeStruct((m, n), out_dtype)
        extra_in, extra_specs, aliases = [times2], [tile], {}
        finish = lambda acc, extra: [(acc * (2.0 * extra[...].astype(F32))).astype(out_dtype)]
    else:
        out_specs = tile
        out_shape = jax.ShapeDtypeStruct((m, n), out_dtype)
        extra_in, extra_specs, aliases = [], [], {}
        finish = lambda acc, extra: [acc.astype(out_dtype)]
    n_out = 2 if relu_pair else 1

    def body(*refs):
        a_ref, b_ref = refs[0], refs[1]
        extra = refs[2] if extra_in else None
        o_refs = refs[2 + len(extra_in):2 + len(extra_in) + n_out]

        def store(acc):
            for o_ref, val in zip(o_refs, finish(acc, extra)):
                o_ref[...] = val

        if nk == 1:
            store(_dot(a_ref[...], b_ref[...], mode))
            return
        acc_ref = refs[-1]
        kk = pl.program_id(2)

        @pl.when(kk == 0)
        def _():
            acc_ref[...] = jnp.zeros_like(acc_ref)

        acc_ref[...] += _dot(a_ref[...], b_ref[...], mode)

        @pl.when(kk == nk - 1)
        def _():
            store(acc_ref[...])

    return pl.pallas_call(
        body, name=name, grid=(m // tm, n // tn, nk),
        in_specs=[a_spec, b_spec] + extra_specs,
        out_specs=out_specs, out_shape=out_shape,
        scratch_shapes=[pltpu.VMEM((tm, tn), F32)] if nk > 1 else [],
        input_output_aliases=aliases,
        compiler_params=_cp(("parallel", "parallel", "arbitrary")),
    )(a, b, *extra_in)


ROWS = 512


def _prenorm(x, w, name):
    def body(x_ref, w_ref, o_ref):
        xv = x_ref[...]
        r = lax.rsqrt(jnp.mean(xv * xv, axis=-1, keepdims=True) + EPS)
        o_ref[...] = (xv * r * w_ref[...]).astype(BF16)

    return pl.pallas_call(
        body, name=name, grid=(T // ROWS,),
        in_specs=[pl.BlockSpec((ROWS, D), lambda i: (i, 0)), pl.BlockSpec((1, D), lambda i: (0, 0))],
        out_specs=pl.BlockSpec((ROWS, D), lambda i: (i, 0)),
        out_shape=jax.ShapeDtypeStruct((T, D), BF16),
        compiler_params=_cp(("parallel",)),
    )(x, w)


def _rms(z):
    return lax.rsqrt(jnp.mean(z * z, axis=-1, keepdims=True) + EPS)


def _rms_bwd(z, w, dy):
    r = _rms(z)
    wdy = dy * w
    dz = r * wdy - z * (r * r * r) * jnp.mean(z * wdy, axis=-1, keepdims=True)
    return dz, jnp.sum(dy * z * r, axis=0, keepdims=True)


_ROW = pl.BlockSpec((ROWS, D), lambda i: (i, 0))
_VEC = pl.BlockSpec((1, D), lambda i: (0, 0))


def _post_pre_fwd(x, z, w_post, w_pre, name):
    def body(x_ref, z_ref, wp_ref, wn_ref, x_out, h_out):
        zv = z_ref[...]
        xn = x_ref[...] + zv * _rms(zv) * wp_ref[...]
        x_out[...] = xn
        h_out[...] = (xn * _rms(xn) * wn_ref[...]).astype(BF16)

    return pl.pallas_call(
        body, name=name, grid=(T // ROWS,), in_specs=[_ROW, _ROW, _VEC, _VEC], out_specs=(_ROW, _ROW),
        out_shape=(jax.ShapeDtypeStruct((T, D), F32), jax.ShapeDtypeStruct((T, D), BF16)),
        compiler_params=_cp(("parallel",)),
    )(x, z, w_post, w_pre)


def _post_loss(x, z, w_post, tgt):
    def body(x_ref, z_ref, w_ref, t_ref, g_ref, l_ref, dz_ref, dw_ref):
        @pl.when(pl.program_id(0) == 0)
        def _():
            l_ref[...] = jnp.zeros_like(l_ref)
            dw_ref[...] = jnp.zeros_like(dw_ref)

        zv = z_ref[...]
        e = x_ref[...] + zv * _rms(zv) * w_ref[...] - t_ref[...]
        g = e * (1.0 / D)
        g_ref[...] = g
        l_ref[...] += jnp.sum(e * e) * (0.5 / D)
        dz, dw = _rms_bwd(zv, w_ref[...], g)
        dz_ref[...] = dz.astype(BF16)
        dw_ref[...] += dw

    return pl.pallas_call(
        body, name="postnorm_loss", grid=(T // ROWS,), in_specs=[_ROW, _ROW, _VEC, _ROW],
        out_specs=(_ROW, pl.BlockSpec((1, 128), lambda i: (0, 0)), _ROW, _VEC),
        out_shape=(jax.ShapeDtypeStruct((T, D), F32), jax.ShapeDtypeStruct((1, 128), F32),
                   jax.ShapeDtypeStruct((T, D), BF16), jax.ShapeDtypeStruct((1, D), F32)),
        compiler_params=_cp(("arbitrary",)),
    )(x, z, w_post, tgt)


def _pre_post_bwd(x, w_pre, dh, add, z, w_post, name):
    def body(x_ref, wn_ref, dh_ref, add_ref, z_ref, wp_ref, g_ref, dz_ref, dwn_ref, dwp_ref):
        @pl.when(pl.program_id(0) == 0)
        def _():
            dwn_ref[...] = jnp.zeros_like(dwn_ref)
            dwp_ref[...] = jnp.zeros_like(dwp_ref)

        dx, dwn = _rms_bwd(x_ref[...], wn_ref[...], dh_ref[...])
        g = dx + add_ref[...]
        g_ref[...] = g
        dz, dwp = _rms_bwd(z_ref[...], wp_ref[...], g)
        dz_ref[...] = dz.astype(BF16)
        dwn_ref[...] += dwn
        dwp_ref[...] += dwp

    return pl.pallas_call(
        body, name=name, grid=(T // ROWS,), in_specs=[_ROW, _VEC, _ROW, _ROW, _ROW, _VEC],
        out_specs=(_ROW, _ROW, _VEC, _VEC),
        out_shape=(jax.ShapeDtypeStruct((T, D), F32), jax.ShapeDtypeStruct((T, D), BF16),
                   jax.ShapeDtypeStruct((1, D), F32), jax.ShapeDtypeStruct((1, D), F32)),
        compiler_params=_cp(("arbitrary",)),
    )(x, w_pre, dh, add, z, w_post)


def _norm_bwd(z, w, dy, add, name):
    has_add = add is not None

    def body(*refs):
        if has_add:
            z_ref, w_ref, dy_ref, add_ref, dz_ref, dw_ref = refs
        else:
            z_ref, w_ref, dy_ref, dz_ref, dw_ref = refs
        i = pl.program_id(0)

        @pl.when(i == 0)
        def _():
            dw_ref[...] = jnp.zeros_like(dw_ref)

        zv = z_ref[...].astype(F32)
        dyv = dy_ref[...]
        r = lax.rsqrt(jnp.mean(zv * zv, axis=-1, keepdims=True) + EPS)
        wdy = dyv * w_ref[...]
        dz = r * wdy - zv * (r * r * r) * jnp.mean(zv * wdy, axis=-1, keepdims=True)
        if has_add:
            dz = dz + add_ref[...]
        dz_ref[...] = dz.astype(dz_ref.dtype)
        dw_ref[...] += jnp.sum(dyv * zv * r, axis=0, keepdims=True)

    row = pl.BlockSpec((ROWS, D), lambda i: (i, 0))
    vec = pl.BlockSpec((1, D), lambda i: (0, 0))
    ins = [z, w, dy] + ([add] if has_add else [])
    dz_dtype = F32 if has_add else BF16
    return pl.pallas_call(
        body, name=name, grid=(T // ROWS,),
        in_specs=[row, vec, row] + ([row] if has_add else []),
        out_specs=(row, vec),
        out_shape=(jax.ShapeDtypeStruct((T, D), dz_dtype), jax.ShapeDtypeStruct((1, D), F32)),
        compiler_params=_cp(("arbitrary",)),
    )(*ins)


def _adamw_math(w, g, m, v):
    c1 = 1.0 - ADAM_B1 ** ADAM_STEP
    c2 = 1.0 - ADAM_B2 ** ADAM_STEP
    mn = ADAM_B1 * m + (1.0 - ADAM_B1) * g
    vn = ADAM_B2 * v + (1.0 - ADAM_B2) * (g * g)
    return -ADAM_LR * ((mn / c1) / (jnp.sqrt(vn / c2) + ADAM_EPS) + ADAM_WD * w), mn, vn


def _adamw_from(w, m, v, sources, tr, name):
    layers, rows, cols = w.shape
    assert len(sources) == layers and rows % tr == 0, (name, w.shape)
    g_specs = []
    for layer, (buf, row0, transposed) in enumerate(sources):
        step = lambda l, i, layer=layer: jnp.where(l == layer, i, 0)
        if transposed:
            assert row0 % cols == 0 and buf.shape[1] == rows, (name, row0)
            g_specs.append(pl.BlockSpec((cols, tr), lambda l, i, b=row0 // cols, step=step: (b, step(l, i))))
        else:
            assert row0 % tr == 0 and buf.shape[1] == cols, (name, row0)
            g_specs.append(pl.BlockSpec((tr, cols), lambda l, i, b=row0 // tr, step=step: (b + step(l, i), 0)))

    def body(*refs):
        w_ref, m_ref, v_ref = refs[:3]
        g_refs = refs[3:3 + layers]
        g_out, d_ref, mo_ref, vo_ref = refs[3 + layers:]
        gs = [r[...].T if src[2] else r[...] for r, src in zip(g_refs, sources)]
        g = gs[0] if layers == 1 else jnp.where(pl.program_id(0) == 0, gs[0], gs[1])
        g_out[...] = g
        d_ref[...], mo_ref[...], vo_ref[...] = _adamw_math(w_ref[...], g, m_ref[...], v_ref[...])

    blk = pl.BlockSpec((None, tr, cols), lambda l, i: (l, i, 0))
    sds = jax.ShapeDtypeStruct(w.shape, F32)
    return pl.pallas_call(body, name=name, grid=(layers, rows // tr), in_specs=[blk] * 3 + g_specs,
                          out_specs=(blk,) * 4, out_shape=(sds,) * 4,
                          compiler_params=_cp(("parallel", "parallel")))(w, m, v, *[s[0] for s in sources])


def _adamw(w, g, m, v, name):
    lead = w.shape[:-2]
    assert len(lead) <= 1 and g.shape == w.shape, (name, w.shape, g.shape)
    rows, cols = w.shape[-2:]
    if rows <= 512:
        tr, tc = rows, cols
    elif rows % 256 == 0:
        tr, tc = 256, cols
    else:
        tr, tc = rows, 256
    assert rows % tr == 0 and cols % tc == 0, (name, w.shape)
    c1 = 1.0 - ADAM_B1 ** ADAM_STEP
    c2 = 1.0 - ADAM_B2 ** ADAM_STEP

    def body(w_ref, g_ref, m_ref, v_ref, d_ref, mo_ref, vo_ref):
        gv = g_ref[...]
        mn = ADAM_B1 * m_ref[...] + (1.0 - ADAM_B1) * gv
        vn = ADAM_B2 * v_ref[...] + (1.0 - ADAM_B2) * (gv * gv)
        m_hat = mn / c1
        v_hat = vn / c2
        d_ref[...] = -ADAM_LR * (m_hat / (jnp.sqrt(v_hat) + ADAM_EPS) + ADAM_WD * w_ref[...])
        mo_ref[...] = mn
        vo_ref[...] = vn

    if lead:
        grid = (lead[0], rows // tr, cols // tc)
        blk = pl.BlockSpec((None, tr, tc), lambda l, i, j: (l, i, j))
    else:
        grid = (rows // tr, cols // tc)
        blk = pl.BlockSpec((tr, tc), lambda i, j: (i, j))
    sds = jax.ShapeDtypeStruct(w.shape, F32)
    return pl.pallas_call(body, name=name, grid=grid, in_specs=[blk] * 4, out_specs=(blk,) * 3,
                          out_shape=(sds,) * 3, compiler_params=_cp(("parallel",) * len(grid)))(w, g, m, v)


def _gla_consts():
    ltri = (_iota((CHUNK, CHUNK), 0) >= _iota((CHUNK, CHUNK), 1)).astype(F32)
    ones_c = jnp.ones((CHUNK, 128), F32)
    mask = (_iota((256, 512), 0) // 64 == _iota((256, 512), 1) // 128).astype(F32)
    return ltri, ones_c, mask


def _gla_chunk(consts, q, k, v, r, aux, s_prev, wa, ba, nw):
    ltri, ones_c, mask = consts
    la = _log_sigmoid(bdot(aux, wa, "nn") + ba) * (1.0 / 16.0)
    cum = hdot(ltri, la, "nn")
    total = jnp.sum(la, axis=0, keepdims=True)
    k_dec = k * jnp.exp(total - cum)
    inc = bdot(k_dec, v, "tn") * mask
    dec = jnp.exp(jnp.broadcast_to(total, (128, 256)).T)
    dec = jnp.concatenate([dec, dec, dec, dec], axis=1)
    s_new = dec * s_prev + inc
    o = bdot(q * GLA_SCALE, s_new, "nn")
    parts = []
    for h in range(4):
        oh = o[:, h * 128:(h + 1) * 128]
        parts.append(oh * lax.rsqrt(jnp.mean(oh * oh, axis=-1, keepdims=True) + EPS))
    on = jnp.concatenate(parts, axis=1)
    return s_new, on * nw * (r * _sigmoid(r))


GLA_PER_STEP = 4
GLA_ROWS = GLA_PER_STEP * CHUNK
GLA_STEPS = NCHUNK // GLA_PER_STEP


def _gla_specs(cmap):
    return [pl.BlockSpec((GLA_ROWS, 256), lambda c: (cmap(c), 0)),
            pl.BlockSpec((GLA_ROWS, 256), lambda c: (cmap(c), 1)),
            pl.BlockSpec((GLA_ROWS, 512), lambda c: (cmap(c), 1)),
            pl.BlockSpec((GLA_ROWS, 512), lambda c: (cmap(c), 2)),
            pl.BlockSpec((GLA_ROWS, 128), lambda c: (cmap(c), AUX_BLK))]


def _gla_fwd(proj, wa, ba, nw):
    def body(q_ref, k_ref, v_ref, r_ref, aux_ref, wa_ref, ba_ref, nw_ref, o_ref, sp_ref, s_ref):
        @pl.when(pl.program_id(0) == 0)
        def _():
            s_ref[...] = jnp.zeros_like(s_ref)

        s = s_ref[...]
        consts = _gla_consts()
        outs, states = [], []
        for i in range(GLA_PER_STEP):
            rows = slice(i * CHUNK, (i + 1) * CHUNK)
            states.append(s)
            s, out = _gla_chunk(consts, q_ref[rows, :], k_ref[rows, :], v_ref[rows, :], r_ref[rows, :], aux_ref[rows, :],
                                s, wa_ref[...], ba_ref[...], nw_ref[...])
            outs.append(out)
        s_ref[...] = s
        for i in range(GLA_PER_STEP):
            o_ref[i * CHUNK:(i + 1) * CHUNK, :] = outs[i]
            sp_ref[i] = states[i]

    full = lambda shape: pl.BlockSpec(shape, lambda c: (0,) * len(shape))
    return pl.pallas_call(
        body, name="gla_fwd", grid=(GLA_STEPS,),
        in_specs=_gla_specs(lambda c: c) + [full((128, 256)), full((1, 256)), full((1, 512))],
        out_specs=(pl.BlockSpec((GLA_ROWS, 512), lambda c: (c, 0)),
                   pl.BlockSpec((GLA_PER_STEP, 256, 512), lambda c: (c, 0, 0))),
        out_shape=(jax.ShapeDtypeStruct((T, D), F32), jax.ShapeDtypeStruct((NCHUNK, 256, 512), F32)),
        scratch_shapes=[pltpu.VMEM((256, 512), F32)],
        compiler_params=_cp(("arbitrary",)),
    )(proj, proj, proj, proj, proj, wa, ba, nw)


def _gla_bwd(proj, s_prev_all, wa, ba, nw, dcat):
    rev = lambda c: GLA_STEPS - 1 - c

    def body(q_ref, k_ref, v_ref, r_ref, aux_ref, sp_ref, wa_ref, ba_ref, nw_ref, do_ref,
             dq_ref, dk_ref, dv_ref, dr_ref, daux_ref, dwa_ref, dba_ref, dnw_ref, ds_ref):
        @pl.when(pl.program_id(0) == 0)
        def _():
            ds_ref[...] = jnp.zeros_like(ds_ref)
            dwa_ref[...] = jnp.zeros_like(dwa_ref)
            dba_ref[...] = jnp.zeros_like(dba_ref)
            dnw_ref[...] = jnp.zeros_like(dnw_ref)

        fn = functools.partial(_gla_chunk, _gla_consts())
        ds = ds_ref[...]
        dwa, dba, dnw = dwa_ref[...], dba_ref[...], dnw_ref[...]
        grads = {}
        for i in reversed(range(GLA_PER_STEP)):
            rows = slice(i * CHUNK, (i + 1) * CHUNK)
            _, vjp = jax.vjp(fn, q_ref[rows, :], k_ref[rows, :], v_ref[rows, :], r_ref[rows, :], aux_ref[rows, :],
                             sp_ref[i], wa_ref[...], ba_ref[...], nw_ref[...])
            *grads[i], ds, dwa_i, dba_i, dnw_i = vjp((ds, do_ref[rows, :]))
            dwa, dba, dnw = dwa + dwa_i, dba + dba_i, dnw + dnw_i
        ds_ref[...] = ds
        dwa_ref[...] = dwa
        dba_ref[...] = dba
        dnw_ref[...] = dnw
        for i in range(GLA_PER_STEP):
            rows = slice(i * CHUNK, (i + 1) * CHUNK)
            for ref, g in zip((dq_ref, dk_ref, dv_ref, dr_ref, daux_ref), grads[i]):
                ref[rows, :] = g

    full = lambda shape: pl.BlockSpec(shape, lambda c: (0,) * len(shape))
    blk = lambda w: pl.BlockSpec((GLA_ROWS, w), lambda c: (rev(c), 0))
    sds = lambda *s: jax.ShapeDtypeStruct(s, F32)
    return pl.pallas_call(
        body, name="gla_bwd", grid=(GLA_STEPS,),
        in_specs=_gla_specs(rev) + [pl.BlockSpec((GLA_PER_STEP, 256, 512), lambda c: (rev(c), 0, 0)),
                                    full((128, 256)), full((1, 256)), full((1, 512)), blk(512)],
        out_specs=(blk(256), blk(256), blk(512), blk(512), blk(128), full((128, 256)), full((1, 256)), full((1, 512))),
        out_shape=(sds(T, 256), sds(T, 256), sds(T, 512), sds(T, 512), sds(T, 128),
                   sds(128, 256), sds(1, 256), sds(1, 512)),
        scratch_shapes=[pltpu.VMEM((256, 512), F32)],
        compiler_params=_cp(("arbitrary",)),
    )(proj, proj, proj, proj, proj, s_prev_all, wa, ba, nw, dcat)


GATE_ROWS = 128


def _prefix8(x, towards_later):
    row = _iota(x.shape, 0)
    for s in (1, 2, 4):
        if towards_later:
            keep, shift = row >= s, s
        else:
            keep, shift = row < 8 - s, 8 - s
        x = x + jnp.where(keep, pltpu.roll(x, shift, 0), 0.0)
    return x


def _fox_gate_fwd(proj, bpad):
    def body(aux_ref, b_ref, cum_ref):
        def step(i, carry):
            rows = pl.ds(pl.multiple_of(i * 8, 8), 8)
            cum = _prefix8(_log_sigmoid(aux_ref[rows, :] + b_ref[...]), True) + carry
            cum_ref[rows, :] = cum
            return jnp.broadcast_to(cum[7:, :], (8, 128))

        lax.fori_loop(0, T // 8, step, jnp.zeros((8, 128), F32), unroll=4)

    return pl.pallas_call(
        body, name="fox_gate_fwd", grid=(1,),
        in_specs=[pl.BlockSpec((T, 128), lambda i: (0, AUX_BLK)), pl.BlockSpec((1, 128), lambda i: (0, 0))],
        out_specs=pl.BlockSpec((T, 128), lambda i: (0, 0)),
        out_shape=jax.ShapeDtypeStruct((T, 128), F32),
        compiler_params=_cp(("arbitrary",)),
    )(proj, bpad)


def _fox_gate_bwd(proj, bpad, dccol_t, daux_gla):
    def body(aux_ref, b_ref, dc_ref, dg_ref, daux_ref, db_ref):
        def step(i, state):
            carry, db = state
            rows = pl.ds(pl.multiple_of(T - 8 * (i + 1), 8), 8)
            dlf = _prefix8(dc_ref[rows, :], False) + carry
            dz = dlf * _sigmoid(-(aux_ref[rows, :] + b_ref[...]))
            daux_ref[rows, :] = dz + dg_ref[rows, :]
            return jnp.broadcast_to(dlf[:1, :], (8, 128)), db + dz

        zeros = jnp.zeros((8, 128), F32)
        _, db = lax.fori_loop(0, T // 8, step, (zeros, zeros), unroll=4)
        db_ref[...] = jnp.sum(db, axis=0, keepdims=True)

    whole = pl.BlockSpec((T, 128), lambda i: (0, 0))
    vec = pl.BlockSpec((1, 128), lambda i: (0, 0))
    return pl.pallas_call(
        body, name="fox_gate_bwd", grid=(1,),
        in_specs=[pl.BlockSpec((T, 128), lambda i: (0, AUX_BLK)), vec, whole, whole],
        out_specs=(whole, vec),
        out_shape=(jax.ShapeDtypeStruct((T, 128), F32), jax.ShapeDtypeStruct((1, 128), F32)),
        compiler_params=_cp(("arbitrary",)),
    )(proj, bpad, dccol_t, daux_gla)


FOX_Q = 256


FOX_QB = T // FOX_Q


@jax.custom_vjp
def _attend(s, v):
    return _attend_fwd(s, v)[0]


def _attend_fwd(s, v):
    e = jnp.exp(s - jnp.max(s, axis=-1, keepdims=True))
    r = 1.0 / jnp.sum(e, axis=-1, keepdims=True)
    return _dot(e, v, "nn") * r, (e, r, v)


def _attend_bwd(res, do):
    e, r, v = res
    do_r = do * r
    dpr = _dot(do_r, v, "nt")
    ds = e * (dpr - r * jnp.sum(e * dpr, axis=-1, keepdims=True))
    return ds, _dot(e, do_r, "tn").astype(v.dtype)


_attend.defvjp(_attend_fwd, _attend_bwd)


def _fox_block(hp, q, k, v, ccol):
    kl = k.shape[0]
    lane = _iota((FOX_Q, 128), 1)
    tri = jnp.bitwise_and(_iota((2 * FOX_Q, FOX_Q), 0), FOX_Q - 1) >= _iota((2 * FOX_Q, FOX_Q), 1)
    sub = _iota((8, kl), 0)
    qs = q * ATT_SCALE
    q2 = jnp.concatenate([jnp.where(lane < 64, qs, 0.0), jnp.where(lane >= 64, qs, 0.0)], axis=0)
    s = bdot(q2, k, "nt")
    cs = [jnp.sum(jnp.where(sub == 2 * hp + e, ccol, 0.0), axis=0, keepdims=True) for e in range(2)]
    s = jnp.concatenate([s[:FOX_Q] - cs[0], s[FOX_Q:] - cs[1]], axis=0)
    diag = jnp.where(tri, s[:, kl - FOX_Q:], NEG)
    s = diag if kl == FOX_Q else jnp.concatenate([s[:, :kl - FOX_Q], diag], axis=1)
    o2 = _attend(s, v)
    return jnp.where(lane < 64, o2[:FOX_Q], o2[FOX_Q:])


def _fox_in_specs():
    return [pl.BlockSpec((FOX_Q, 128), lambda hp, qb: (qb, 12 + hp)),
            pl.BlockSpec((T, 128), lambda hp, qb: (0, 16 + hp)),
            pl.BlockSpec((T, 128), lambda hp, qb: (0, 20 + hp)),
            pl.BlockSpec((8, T), lambda hp, qb: (0, 0))]


def _fox_fwd(proj, cum_c, cat):
    def body(q_ref, k_ref, v_ref, cc_ref, cat_ref, o_ref):
        qb = pl.program_id(1)
        for g in range(FOX_QB):
            kl = FOX_Q * (g + 1)

            @pl.when(qb == g)
            def _(kl=kl):
                o_ref[...] = _fox_block(pl.program_id(0), q_ref[...], k_ref[0:kl, :], v_ref[0:kl, :], cc_ref[:, 0:kl])

    return pl.pallas_call(
        body, name="fox_fwd", grid=(4, FOX_QB), in_specs=_fox_in_specs() + [pl.BlockSpec(memory_space=pl.ANY)],
        out_specs=pl.BlockSpec((FOX_Q, 128), lambda hp, qb: (qb, 4 + hp)),
        out_shape=jax.ShapeDtypeStruct((T, D), F32), input_output_aliases={4: 0},
        compiler_params=_cp(("parallel", "parallel")),
    )(proj, proj, proj, cum_c, cat)


def _fox_bwd(proj, cum_c, dcat):
    def body(q_ref, k_ref, v_ref, cc_ref, do_ref, dq_ref, dk_ref, dv_ref, dcc_ref):
        qb = pl.program_id(1)

        @pl.when(qb == 0)
        def _():
            dk_ref[...] = jnp.zeros_like(dk_ref)
            dv_ref[...] = jnp.zeros_like(dv_ref)
            dcc_ref[...] = jnp.zeros_like(dcc_ref)

        fn = functools.partial(_fox_block, pl.program_id(0))
        for g in range(FOX_QB):
            kl = FOX_Q * (g + 1)

            @pl.when(qb == g)
            def _(kl=kl):
                _, vjp = jax.vjp(fn, q_ref[...], k_ref[0:kl, :], v_ref[0:kl, :], cc_ref[:, 0:kl])
                dq, dk, dv, dcc = vjp(do_ref[...])
                dq_ref[...] = dq
                dk_ref[0:kl, :] += dk
                dv_ref[0:kl, :] += dv
                dcc_ref[:, 0:kl] += dcc

    sds = lambda *s: jax.ShapeDtypeStruct(s, F32)
    return pl.pallas_call(
        body, name="fox_bwd", grid=(4, FOX_QB),
        in_specs=_fox_in_specs() + [pl.BlockSpec((FOX_Q, 128), lambda hp, qb: (qb, 4 + hp))],
        out_specs=(pl.BlockSpec((FOX_Q, 128), lambda hp, qb: (qb, hp)),
                   pl.BlockSpec((T, 128), lambda hp, qb: (0, hp)),
                   pl.BlockSpec((T, 128), lambda hp, qb: (0, hp)),
                   pl.BlockSpec((None, 8, T), lambda hp, qb: (hp, 0, 0))),
        out_shape=(sds(T, 512), sds(T, 512), sds(T, 512), sds(4, 8, T)),
        compiler_params=_cp(("parallel", "arbitrary")),
    )(proj, proj, proj, cum_c, dcat)


BIAS_W = 640


def _rel_onehot():
    j = _iota((REL_PAD, BIAS_W), 1)
    rel = jnp.clip(CA_PAD + CHUNK - 1 - j, -128, 128) + 128
    return (_iota((REL_PAD, BIAS_W), 0) == rel).astype(F32)


def _bias_build(rbp):
    def body(rb_ref, o_ref):
        f = _hdot_raw(rb_ref[...], _rel_onehot(), "nn")
        for q in range(CHUNK):
            o_ref[q] = pltpu.roll(f, (BIAS_W - (CHUNK - 1 - q)) % BIAS_W, 1)[:, :CA_BAND]

    return pl.pallas_call(body, name="ca_bias_build", out_shape=jax.ShapeDtypeStruct((CHUNK, 8, CA_BAND), F32))(rbp)


def _bias_grad(dbias_q):
    def body(db_ref, o_ref):
        acc = jnp.zeros((8, BIAS_W), F32)
        for q in range(CHUNK):
            acc = acc + pltpu.roll(db_ref[q], CHUNK - 1 - q, 1)
        o_ref[...] = _hdot_raw(acc, _rel_onehot(), "nt")

    return pl.pallas_call(body, name="ca_bias_grad", out_shape=jax.ShapeDtypeStruct((8, REL_PAD), F32))(dbias_q)


def _ca_block(c, masked, q, kb, vb, bias2):
    lane = _iota((CHUNK, 128), 1)
    qs = q * ATT_SCALE
    q2 = jnp.concatenate([jnp.where(lane < 64, qs, 0.0), jnp.where(lane >= 64, qs, 0.0)], axis=0)
    s = bdot(q2, kb, "nt") + bias2.reshape(2 * CHUNK, CA_BAND)
    if masked:
        s = jnp.where((c * CHUNK - CA_PAD + _iota((2 * CHUNK, CA_BAND), 1)) >= 0, s, NEG)
    o2 = _attend(s, vb)
    return jnp.where(lane < 64, o2[:CHUNK], o2[CHUNK:])


CA_PER_STEP = 8
CA_ROWS = CA_PER_STEP * CHUNK
CA_MASKED_STEPS = -(-CA_PAD // CA_ROWS)


def _ca_fwd(proj, kvpad, bias):
    def body(q_ref, k_ref, v_ref, b_ref, o_ref):
        def run(masked):
            outs = []
            for i in range(CA_PER_STEP):
                c = pl.program_id(1) * CA_PER_STEP + i
                band = pl.ds(pl.multiple_of(c * CHUNK, CHUNK), CA_BAND)
                rows = slice(i * CHUNK, (i + 1) * CHUNK)
                outs.append(_ca_block(c, masked, q_ref[rows, :], k_ref[band, :], v_ref[band, :], b_ref[...]))
            for i in range(CA_PER_STEP):
                o_ref[i * CHUNK:(i + 1) * CHUNK, :] = outs[i]

        pl.when(pl.program_id(1) < CA_MASKED_STEPS)(lambda: run(True))
        pl.when(pl.program_id(1) >= CA_MASKED_STEPS)(lambda: run(False))

    return pl.pallas_call(
        body, name="ca_fwd", grid=(4, NCHUNK // CA_PER_STEP),
        in_specs=[pl.BlockSpec((CA_ROWS, 128), lambda hp, c: (c, hp)),
                  pl.BlockSpec((T + CA_PAD, 128), lambda hp, c: (0, hp)),
                  pl.BlockSpec((T + CA_PAD, 128), lambda hp, c: (0, 4 + hp)),
                  pl.BlockSpec((2, CHUNK, CA_BAND), lambda hp, c: (hp, 0, 0))],
        out_specs=pl.BlockSpec((CA_ROWS, 128), lambda hp, c: (c, hp)),
        out_shape=jax.ShapeDtypeStruct((T, D), F32),
        compiler_params=_cp(("parallel", "parallel")),
    )(proj, kvpad, kvpad, bias)


def _ca_bwd(proj, kvpad, bias, dcat):
    def body(q_ref, k_ref, v_ref, b_ref, do_ref, dq_ref, dk_ref, dv_ref, db_ref):
        c = pl.program_id(1)

        @pl.when(c == 0)
        def _():
            dk_ref[...] = jnp.zeros_like(dk_ref)
            dv_ref[...] = jnp.zeros_like(dv_ref)
            db_ref[...] = jnp.zeros_like(db_ref)

        def run(masked):
            grads, bands = [], []
            for i in range(CA_PER_STEP):
                ci = c * CA_PER_STEP + i
                band = pl.ds(pl.multiple_of(ci * CHUNK, CHUNK), CA_BAND)
                rows = slice(i * CHUNK, (i + 1) * CHUNK)
                fn = functools.partial(_ca_block, ci, masked)
                _, vjp = jax.vjp(fn, q_ref[rows, :], k_ref[band, :], v_ref[band, :], b_ref[...])
                grads.append(vjp(do_ref[rows, :]))
                bands.append(band)
            for i, (dq, _, _, _) in enumerate(grads):
                dq_ref[i * CHUNK:(i + 1) * CHUNK, :] = dq
            for band, (_, dkb, dvb, _) in zip(bands, grads):
                dk_ref[band, :] += dkb
                dv_ref[band, :] += dvb
            db_ref[...] += functools.reduce(lambda a, b: a + b, [g[3] for g in grads])

        pl.when(c < CA_MASKED_STEPS)(lambda: run(True))
        pl.when(c >= CA_MASKED_STEPS)(lambda: run(False))

    sds = lambda *s: jax.ShapeDtypeStruct(s, F32)
    padded = lambda: pl.BlockSpec((T + CA_PAD, 128), lambda hp, c: (0, hp))
    return pl.pallas_call(
        body, name="ca_bwd", grid=(4, NCHUNK // CA_PER_STEP),
        in_specs=[pl.BlockSpec((CA_ROWS, 128), lambda hp, c: (c, hp)),
                  pl.BlockSpec((T + CA_PAD, 128), lambda hp, c: (0, hp)),
                  pl.BlockSpec((T + CA_PAD, 128), lambda hp, c: (0, 4 + hp)),
                  pl.BlockSpec((2, CHUNK, CA_BAND), lambda hp, c: (hp, 0, 0)),
                  pl.BlockSpec((CA_ROWS, 128), lambda hp, c: (c, hp))],
        out_specs=(pl.BlockSpec((CA_ROWS, 128), lambda hp, c: (c, hp)), padded(), padded(),
                   pl.BlockSpec((2, CHUNK, CA_BAND), lambda hp, c: (hp, 0, 0))),
        out_shape=(sds(T, 512), sds(T + CA_PAD, 512), sds(T + CA_PAD, 512), sds(8, CHUNK, CA_BAND)),
        compiler_params=_cp(("parallel", "arbitrary")),
    )(proj, kvpad, kvpad, bias, dcat)


def _lru_pre(xs, cw, cb, wa, ba, wx, bx, lam):
    xc = cb + xs[0] * cw[0:1, :] + xs[1] * cw[1:2, :] + xs[2] * cw[2:3, :] + xs[3] * cw[3:4, :]
    ra = _sigmoid(bdot(xc, wa, "nn") + ba)
    ii = _sigmoid(bdot(xc, wx, "nn") + bx)
    la = 8.0 * ra * _log_sigmoid(lam)
    return jnp.exp(la), jnp.sqrt(-_expm1(2.0 * la)) * (ii * xc)


def _lru_pre_specs():
    full = lambda shape: pl.BlockSpec(shape, lambda i: (0,) * len(shape))
    return [pl.BlockSpec((4, ROWS, 512), lambda i: (0, i, 0)), full((4, 512)), full((1, 512)),
            full((512, 512)), full((1, 512)), full((512, 512)), full((1, 512)), full((1, 512))]


def _lru_pre_fwd(xs, cw, cb, wa, ba, wx, bx, lam):
    def body(xs_ref, cw_ref, cb_ref, wa_ref, ba_ref, wx_ref, bx_ref, lam_ref, a_ref, b_ref):
        a, b = _lru_pre(xs_ref[...], cw_ref[...], cb_ref[...], wa_ref[...], ba_ref[...], wx_ref[...], bx_ref[...],
                        lam_ref[...])
        a_ref[...] = a
        b_ref[...] = b

    row = pl.BlockSpec((ROWS, 512), lambda i: (i, 0))
    sds = jax.ShapeDtypeStruct((T, 512), F32)
    return pl.pallas_call(body, name="lru_pre_fwd", grid=(T // ROWS,), in_specs=_lru_pre_specs(),
                          out_specs=(row, row), out_shape=(sds, sds), compiler_params=_cp(("parallel",)),
                          )(xs, cw, cb, wa, ba, wx, bx, lam)


def _lru_pre_bwd(xs, cw, cb, wa, ba, wx, bx, lam, da, db):
    def body(xs_ref, cw_ref, cb_ref, wa_ref, ba_ref, wx_ref, bx_ref, lam_ref, da_ref, db_ref,
             dxs_ref, dcw_ref, dcb_ref, dwa_ref, dba_ref, dwx_ref, dbx_ref, dlam_ref):
        acc = (dcw_ref, dcb_ref, dwa_ref, dba_ref, dwx_ref, dbx_ref, dlam_ref)

        @pl.when(pl.program_id(0) == 0)
        def _():
            for r in acc:
                r[...] = jnp.zeros_like(r)

        _, vjp = jax.vjp(_lru_pre, xs_ref[...], cw_ref[...], cb_ref[...], wa_ref[...], ba_ref[...], wx_ref[...],
                         bx_ref[...], lam_ref[...])
        grads = vjp((da_ref[...], db_ref[...]))
        dxs_ref[...] = grads[0]
        for r, g in zip(acc, grads[1:]):
            r[...] += g

    row = pl.BlockSpec((ROWS, 512), lambda i: (i, 0))
    specs = _lru_pre_specs()
    sds = lambda *s: jax.ShapeDtypeStruct(s, F32)
    return pl.pallas_call(
        body, name="lru_pre_bwd", grid=(T // ROWS,), in_specs=specs + [row, row], out_specs=tuple(specs),
        out_shape=(sds(4, T, 512), sds(4, 512), sds(1, 512), sds(512, 512), sds(1, 512), sds(512, 512), sds(1, 512),
                   sds(1, 512)),
        compiler_params=_cp(("arbitrary",)),
    )(xs, cw, cb, wa, ba, wx, bx, lam, da, db)


SCAN_ROWS = 8


def _scan8(a, b, towards_later):
    row = _iota((SCAN_ROWS, 512), 0)
    for s in (1, 2, 4):
        if towards_later:
            keep, shift = row >= s, s
        else:
            keep, shift = row < SCAN_ROWS - s, SCAN_ROWS - s
        a_s = jnp.where(keep, pltpu.roll(a, shift, 0), 1.0)
        b_s = jnp.where(keep, pltpu.roll(b, shift, 0), 0.0)
        b = a * b_s + b
        a = a * a_s
    return a, b


def _lru_scan_fwd(a, b):
    def body(a_ref, b_ref, h_ref):
        def step(i, carry):
            rows = pl.ds(pl.multiple_of(i * SCAN_ROWS, SCAN_ROWS), SCAN_ROWS)
            a8, b8 = _scan8(a_ref[rows, :], b_ref[rows, :], True)
            h = a8 * carry + b8
            h_ref[rows, :] = h
            return jnp.broadcast_to(h[SCAN_ROWS - 1:, :], (SCAN_ROWS, 512))

        lax.fori_loop(0, T // SCAN_ROWS, step, jnp.zeros((SCAN_ROWS, 512), F32), unroll=2)

    return pl.pallas_call(body, name="lru_scan_fwd", out_shape=jax.ShapeDtypeStruct((T, 512), F32),
                          compiler_params=pltpu.CompilerParams(vmem_limit_bytes=VMEM_LIMIT))(a, b)


def _lru_scan_bwd(a_next, h_prev, dh):
    def body(a_ref, h_ref, dh_ref, da_ref, db_ref):
        def step(i, carry):
            start = T - SCAN_ROWS * (i + 1)
            rows = pl.ds(pl.multiple_of(start, SCAN_ROWS), SCAN_ROWS)
            a8, b8 = _scan8(a_ref[rows, :], dh_ref[rows, :], False)
            g = a8 * carry + b8
            db_ref[rows, :] = g
            da_ref[rows, :] = g * h_ref[rows, :]
            return jnp.broadcast_to(g[:1, :], (SCAN_ROWS, 512))

        lax.fori_loop(0, T // SCAN_ROWS, step, jnp.zeros((SCAN_ROWS, 512), F32), unroll=2)

    sds = jax.ShapeDtypeStruct((T, 512), F32)
    return pl.pallas_call(body, name="lru_scan_bwd", out_shape=(sds, sds),
                          compiler_params=pltpu.CompilerParams(vmem_limit_bytes=VMEM_LIMIT))(a_next, h_prev, dh)


def _lru_post(h, gate):
    return h * _gelu_tanh(gate)


def _lru_post_fwd(h, proj, cat):
    def body(h_ref, g_ref, cat_ref, o_ref):
        o_ref[...] = _lru_post(h_ref[...], g_ref[...])

    row = pl.BlockSpec((ROWS, 512), lambda i: (i, 0))
    return pl.pallas_call(body, name="lru_post_fwd", grid=(T // ROWS,),
                          in_specs=[row, pl.BlockSpec((ROWS, 512), lambda i: (i, 3)), pl.BlockSpec(memory_space=pl.ANY)],
                          out_specs=pl.BlockSpec((ROWS, 512), lambda i: (i, 1)),
                          out_shape=jax.ShapeDtypeStruct((T, D), F32), input_output_aliases={2: 0},
                          compiler_params=_cp(("parallel",)))(h, proj, cat)


def _lru_post_bwd(h, proj, dcat):
    def body(h_ref, g_ref, do_ref, dh_ref, dg_ref):
        _, vjp = jax.vjp(_lru_post, h_ref[...], g_ref[...])
        dh, dg = vjp(do_ref[...])
        dh_ref[...] = dh
        dg_ref[...] = dg

    row = pl.BlockSpec((ROWS, 512), lambda i: (i, 0))
    sds = jax.ShapeDtypeStruct((T, 512), F32)
    return pl.pallas_call(body, name="lru_post_bwd", grid=(T // ROWS,),
                          in_specs=[row, pl.BlockSpec((ROWS, 512), lambda i: (i, 3)),
                                    pl.BlockSpec((ROWS, 512), lambda i: (i, 1))],
                          out_specs=(row, row), out_shape=(sds, sds), compiler_params=_cp(("parallel",)))(h, proj, dcat)


def _conv_dx(dxs_shift):
    def body(d_ref, o_ref):
        o_ref[...] = d_ref[0] + d_ref[1] + d_ref[2] + d_ref[3]

    row = pl.BlockSpec((ROWS, 512), lambda i: (i, 0))
    return pl.pallas_call(body, name="lru_conv_dx", grid=(T // ROWS,),
                          in_specs=[pl.BlockSpec((4, ROWS, 512), lambda i: (0, i, 0))], out_specs=row,
                          out_shape=jax.ShapeDtypeStruct((T, 512), F32), compiler_params=_cp(("parallel",)))(dxs_shift)


def _position():
    return lax.axis_index("x"), lax.axis_index("y"), lax.axis_index("c")


def _other_chips(x, y):
    return [(1 - x, y), (x, 1 - y), (1 - x, 1 - y)]


def _al(v, n):
    return v * n if isinstance(v, int) else pl.multiple_of(v * n, n)


_AG_ITEMS = [
    ((4, 32, 128), lambda o, s, h: o.at[s, pl.ds(_al(h, 16), 16), :], lambda r, h: r.at[pl.ds(_al(h, 16), 16), :]),
    ((4, 774, 1024), lambda o, s, h: o.at[s, :, pl.ds(_al(h, 512), 512)], lambda r, h: r.at[:, pl.ds(_al(h, 512), 512)]),
    ((1024, 1024), lambda o, s, h: o.at[pl.ds(_al(2 * s + h, 128), 128), :], lambda r, h: r.at[pl.ds(_al(h, 128), 128), :]),
    ((2, 1024, 4096), lambda o, s, h: o.at[h, :, pl.ds(_al(s, 1024), 1024)], lambda r, h: r.at[h]),
    ((2, 4096, 1024), lambda o, s, h: o.at[h, pl.ds(_al(s, 1024), 1024), :], lambda r, h: r.at[h]),
    ((1024, 2560), lambda o, s, h: o.at[pl.ds(_al(h, 512), 512), pl.ds(_al(s, 640), 640)],
     lambda r, h: r.at[pl.ds(_al(h, 512), 512), :]),
    ((1024, 1024), lambda o, s, h: o.at[pl.ds(_al(2 * s + h, 128), 128), :], lambda r, h: r.at[pl.ds(_al(h, 128), 128), :]),
]


_AG_GROUPS = [(0, 1, 2), (3, 4), (5, 6)]

_HBM = pl.BlockSpec(memory_space=pltpu.HBM)
_SEM = pl.BlockSpec(memory_space=pltpu.SEMAPHORE)
_SPLIT = dict(has_side_effects=pltpu.SideEffectType.DATAFLOW_SIDE_EFFECTING)


def _hbm(a):
    return pltpu.with_memory_space_constraint(a, pltpu.HBM)


def _ag_ici_copy(i, j, chip, c, slot, src_ref, land_ref, send_sems, recv_sems, k):
    _, dst, half = _AG_ITEMS[i]
    return pltpu.make_async_remote_copy(src_ref=half(src_ref, c), dst_ref=dst(land_ref, slot, c), send_sem=send_sems.at[k],
                                        recv_sem=recv_sems.at[k], device_id=(*chip, c), device_id_type=MESH)


def _ag_start(groups, shards, name):
    items_all = [i for g in groups for i in _AG_GROUPS[g]]
    n = len(items_all)
    ng = len(groups)
    lands = [lax.empty(_AG_ITEMS[i][0], shards[i].dtype) for i in items_all]

    def body(*refs):
        srcs, land_refs = dict(zip(items_all, refs[:n])), dict(zip(items_all, refs[n:2 * n]))
        sems = refs[2 * n:2 * n + 2 * ng]
        token = refs[-1]
        x, y, c = _position()
        me = 2 * x + y
        for gi, g in enumerate(groups):
            for t, i in enumerate(_AG_GROUPS[g]):
                for j, chip in enumerate(_other_chips(x, y)):
                    _ag_ici_copy(i, j, chip, c, me, srcs[i], land_refs[i], sems[2 * gi], sems[2 * gi + 1], 3 * t + j).start()
        token[...] = jnp.zeros_like(token)

    sem_shapes = []
    for g in groups:
        sem_shapes += [pltpu.SemaphoreType.DMA((3 * len(_AG_GROUPS[g]),))] * 2
    ops = [shards[i] for i in items_all] + lands
    out = pl.pallas_call(
        body, name=name,
        out_shape=tuple(sem_shapes) + tuple(pltpu.HBM(a.shape, a.dtype) for a in ops) + (jax.ShapeDtypeStruct((8, 128), F32),),
        in_specs=(_HBM,) * (2 * n),
        out_specs=(_SEM,) * (2 * ng) + (_HBM,) * (2 * n) + (pl.BlockSpec(memory_space=pltpu.VMEM),),
        input_output_aliases={i: 2 * ng + i for i in range(2 * n)},
        compiler_params=pltpu.CompilerParams(**_SPLIT),
    )(*[_hbm(a) for a in ops])
    sems, thru, token = out[:2 * ng], out[2 * ng:-1], out[-1]
    return ({g: (sems[2 * gi], sems[2 * gi + 1]) for gi, g in enumerate(groups)},
            dict(zip(items_all, thru[:n])), dict(zip(items_all, thru[n:])), token)


def _ag_wait(g, sems, srcs, lands, after):
    items = _AG_GROUPS[g]
    m = len(items)

    def body(*refs):
        src_refs, land_refs = refs[:m], refs[m:2 * m]
        send_sems, recv_sems = refs[2 * m], refs[2 * m + 1]
        x, y, c = _position()
        for t, i in enumerate(items):
            for j, chip in enumerate(_other_chips(x, y)):
                cp = _ag_ici_copy(i, j, chip, c, 2 * chip[0] + chip[1], src_refs[t], land_refs[t], send_sems, recv_sems,
                                  3 * t + j)
                cp.wait_send()
                cp.wait_recv()

    ops = [srcs[i] for i in items] + [lands[i] for i in items]
    out = pl.pallas_call(
        body, name=f"allgather_wait_{g}",
        out_shape=tuple(pltpu.HBM(a.shape, a.dtype) for a in ops),
        in_specs=(_HBM,) * (2 * m) + (_SEM, _SEM, pl.BlockSpec(memory_space=pl.ANY)),
        out_specs=(_HBM,) * (2 * m),
        input_output_aliases={i: i for i in range(2 * m)},
        compiler_params=pltpu.CompilerParams(**_SPLIT),
    )(*ops, sems[0], sems[1], after)
    return list(out[:m]), list(out[m:])


def _ag_forward(g, srcs, lands):
    return _ag_sibling(_AG_GROUPS[g], srcs, lands, False, f"allgather_forward_{g}")


def _ag_push_own(srcs, lands):
    items = tuple(sorted(lands))
    out = _ag_sibling(items, [srcs[i] for i in items], [lands[i] for i in items], True, "allgather_push_own")
    return dict(zip(items, out))


def _ag_sibling(items, srcs, lands, own, name):
    m = len(items)
    per = 2 if own else 3

    def body(*refs):
        src_refs, in_refs, out_refs = refs[:m], refs[m:2 * m], refs[2 * m:3 * m]
        send_sems, recv_sems = refs[3 * m:]
        x, y, c = _position()
        sibling = (x, y, 1 - c)
        me = 2 * x + y
        if own:
            mine = theirs = [(me, 0), (me, 1)]
        else:
            slots = [2 * chip[0] + chip[1] for chip in _other_chips(x, y)]
            mine, theirs = [(s, c) for s in slots], [(s, 1 - c) for s in slots]
        sends = []
        for t, i in enumerate(items):
            _, dst, half = _AG_ITEMS[i]
            for k, (slot, hc) in enumerate(mine):
                src = half(src_refs[t], hc) if own else dst(in_refs[t], slot, hc)
                sends.append(pltpu.make_async_remote_copy(
                    src_ref=src, dst_ref=dst(out_refs[t], slot, hc), send_sem=send_sems.at[per * t + k],
                    recv_sem=recv_sems.at[per * t + k], device_id=sibling, device_id_type=MESH))
        for cp in sends:
            cp.start()
        for t, i in enumerate(items):
            dst = _AG_ITEMS[i][1]
            for k, (slot, hc) in enumerate(theirs):
                there = dst(out_refs[t], slot, hc)
                pltpu.make_async_remote_copy(src_ref=there, dst_ref=there, send_sem=send_sems.at[per * t + k],
                                             recv_sem=recv_sems.at[per * t + k], device_id=sibling,
                                             device_id_type=MESH).wait_recv()
        for cp in sends:
            cp.wait_send()

    any_spec = pl.BlockSpec(memory_space=pl.ANY)
    return pl.pallas_call(
        body, name=name,
        in_specs=[any_spec] * (2 * m), out_specs=(any_spec,) * m,
        out_shape=tuple(jax.ShapeDtypeStruct(a.shape, a.dtype) for a in lands),
        input_output_aliases={m + t: t for t in range(m)},
        scratch_shapes=[pltpu.SemaphoreType.DMA((per * m,)), pltpu.SemaphoreType.DMA((per * m,))],
    )(*srcs, *lands)


def _pair_swap_copy(g_ref, r_ref, send_sem, recv_sem):
    x, y, c = _position()
    hc = g_ref.shape[2] // 2
    return pltpu.make_async_remote_copy(src_ref=g_ref.at[:, :, pl.ds(_al(1 - c, hc), hc)], dst_ref=r_ref,
                                        send_sem=send_sem, recv_sem=recv_sem, device_id=(x, y, 1 - c),
                                        device_id_type=MESH)


def _pair_swap_start(gb, tag):
    _, rows, cols = gb.shape
    recv = lax.empty((4, rows, cols // 2), gb.dtype)

    def body(g_ref, r_ref, send_sem, recv_sem, g_thru, r_thru, token):
        _pair_swap_copy(g_ref, r_ref, send_sem, recv_sem).start()
        token[...] = jnp.zeros_like(token)

    return pl.pallas_call(
        body, name="grad_pair_swap_start_" + tag,
        out_shape=(pltpu.SemaphoreType.DMA(()), pltpu.SemaphoreType.DMA(()), pltpu.HBM(gb.shape, gb.dtype),
                   pltpu.HBM(recv.shape, recv.dtype), jax.ShapeDtypeStruct((8, 128), F32)),
        in_specs=(_HBM, _HBM), out_specs=(_SEM, _SEM, _HBM, _HBM, pl.BlockSpec(memory_space=pltpu.VMEM)),
        input_output_aliases={0: 2, 1: 3},
        compiler_params=pltpu.CompilerParams(**_SPLIT),
    )(_hbm(gb), _hbm(recv))


def _pair_swap_wait(started, after, tag):
    send_sem, recv_sem, gb, recv, _ = started

    def body(g_ref, r_ref, send_sem, recv_sem, after_ref, g_out, r_out):
        cp = _pair_swap_copy(g_ref, r_ref, send_sem, recv_sem)
        cp.wait_send()
        cp.wait_recv()

    return pl.pallas_call(
        body, name="grad_pair_swap_wait_" + tag,
        out_shape=(pltpu.HBM(gb.shape, gb.dtype), pltpu.HBM(recv.shape, recv.dtype)),
        in_specs=(_HBM, _HBM, _SEM, _SEM, pl.BlockSpec(memory_space=pl.ANY)), out_specs=(_HBM, _HBM),
        input_output_aliases={0: 0, 1: 1},
        compiler_params=pltpu.CompilerParams(**_SPLIT),
    )(gb, recv, send_sem, recv_sem, after)


def _handover_copy(r_ref, send_sem, recv_sem, core):
    x, y, c = _position()
    hc = r_ref.shape[1] // 2
    cols = r_ref.at[:, pl.ds(_al(core, hc), hc)]
    return pltpu.make_async_remote_copy(src_ref=cols, dst_ref=cols, send_sem=send_sem, recv_sem=recv_sem,
                                        device_id=(x, y, 1 - c), device_id_type=MESH)


def _handover_start(red, tag):
    def body(r_ref, send_sem, recv_sem, r_thru, token):
        _handover_copy(r_ref, send_sem, recv_sem, lax.axis_index("c")).start()
        token[...] = jnp.zeros_like(token)

    return pl.pallas_call(
        body, name="grad_handover_start_" + tag,
        out_shape=(pltpu.SemaphoreType.DMA(()), pltpu.SemaphoreType.DMA(()), pltpu.HBM(red.shape, red.dtype),
                   jax.ShapeDtypeStruct((8, 128), F32)),
        in_specs=(_HBM,), out_specs=(_SEM, _SEM, _HBM, pl.BlockSpec(memory_space=pltpu.VMEM)),
        input_output_aliases={0: 2},
        compiler_params=pltpu.CompilerParams(**_SPLIT),
    )(_hbm(red))


def _handover_wait(started, after, tag):
    send_sem, recv_sem, red, _ = started

    def body(r_ref, send_sem, recv_sem, after_ref, r_out):
        c = lax.axis_index("c")
        _handover_copy(r_ref, send_sem, recv_sem, c).wait_send()
        _handover_copy(r_ref, send_sem, recv_sem, 1 - c).wait_recv()

    return pl.pallas_call(
        body, name="grad_handover_wait_" + tag,
        out_shape=pltpu.HBM(red.shape, red.dtype),
        in_specs=(_HBM, _SEM, _SEM, pl.BlockSpec(memory_space=pl.ANY)), out_specs=_HBM,
        input_output_aliases={0: 0},
        compiler_params=pltpu.CompilerParams(**_SPLIT),
    )(red, send_sem, recv_sem, after)


def _handover(red, tag):
    started = _handover_start(red, tag)
    return _handover_wait(started, started[3], tag)


def _a2a_copy(j, chip, c, p_ref, q_ref, q_slot, send_sems, recv_sems):
    return pltpu.make_async_remote_copy(src_ref=p_ref.at[2 * chip[0] + chip[1]], dst_ref=q_ref.at[q_slot],
                                        send_sem=send_sems.at[j], recv_sem=recv_sems.at[j], device_id=(*chip, c),
                                        device_id_type=MESH)


def _a2a_start(p, tag):
    def body(p_ref, q_ref, send_sems, recv_sems, p_thru, q_thru, token):
        x, y, c = _position()
        for j, chip in enumerate(_other_chips(x, y)):
            _a2a_copy(j, chip, c, p_ref, q_ref, 2 * x + y, send_sems, recv_sems).start()
        token[...] = jnp.zeros_like(token)

    return pl.pallas_call(
        body, name="grad_alltoall_start_" + tag,
        out_shape=(pltpu.SemaphoreType.DMA((3,)), pltpu.SemaphoreType.DMA((3,)), pltpu.HBM(p.shape, p.dtype),
                   pltpu.HBM(p.shape, p.dtype), jax.ShapeDtypeStruct((8, 128), F32)),
        in_specs=(_HBM, _HBM), out_specs=(_SEM, _SEM, _HBM, _HBM, pl.BlockSpec(memory_space=pltpu.VMEM)),
        input_output_aliases={0: 2, 1: 3},
        compiler_params=pltpu.CompilerParams(**_SPLIT),
    )(_hbm(p), _hbm(lax.empty(p.shape, p.dtype)))


def _a2a_wait(send_sems, recv_sems, p, q, after, tag):
    def body(p_ref, q_ref, send_sems, recv_sems, after_ref, p_out, q_out):
        x, y, c = _position()
        for j, chip in enumerate(_other_chips(x, y)):
            cp = _a2a_copy(j, chip, c, p_ref, q_ref, 2 * chip[0] + chip[1], send_sems, recv_sems)
            cp.wait_send()
            cp.wait_recv()

    return pl.pallas_call(
        body, name="grad_alltoall_wait_" + tag,
        out_shape=(pltpu.HBM(p.shape, p.dtype), pltpu.HBM(q.shape, q.dtype)),
        in_specs=(_HBM, _HBM, _SEM, _SEM, pl.BlockSpec(memory_space=pl.ANY)), out_specs=(_HBM, _HBM),
        input_output_aliases={0: 0, 1: 1},
        compiler_params=pltpu.CompilerParams(**_SPLIT),
    )(p, q, send_sems, recv_sems, after)


def _comm_rows(rows):
    return next(t for t in (512, 384, 256, 128) if rows % t == 0)


def _pair_add(gb, recv, where, tag):
    _, rows, cols = gb.shape
    hc = cols // 2
    tr = _comm_rows(rows)

    def body(w_ref, g_ref, r_ref, o_ref):
        o_ref[...] = (g_ref[...].astype(F32) + r_ref[...].astype(F32)).astype(o_ref.dtype)

    return pl.pallas_call(
        body, name="grad_pair_add_" + tag,
        grid_spec=pltpu.PrefetchScalarGridSpec(
            num_scalar_prefetch=1, grid=(4, rows // tr),
            in_specs=[pl.BlockSpec((None, tr, hc), lambda s, j, w_ref: (s, j, w_ref[0])),
                      pl.BlockSpec((None, tr, hc), lambda s, j, w_ref: (s, j, 0))],
            out_specs=pl.BlockSpec((None, tr, hc), lambda s, j, w_ref: (s, j, 0))),
        out_shape=jax.ShapeDtypeStruct((4, rows, hc), gb.dtype),
        compiler_params=_cp(("parallel", "parallel")),
    )(where, gb, recv)


def _sum_chips(p, q, where, tag):
    _, rows, hc = q.shape
    tr = _comm_rows(rows)

    def body(w_ref, p_ref, qa_ref, qb_ref, qc_ref, o_ref):
        me = w_ref[1]
        own, qa, qb, qc = (r[...].astype(F32) for r in (p_ref, qa_ref, qb_ref, qc_ref))
        v0 = jnp.where(me == 0, own, qa)
        v1 = jnp.where(me == 1, own, jnp.where(me == 0, qa, qb))
        v2 = jnp.where(me == 2, own, jnp.where(me < 2, qb, qc))
        v3 = jnp.where(me == 3, own, qc)
        o_ref[...] = ((v0 + v1) + v2) + v3

    slot = lambda k: pl.BlockSpec((None, tr, hc), lambda j, w_ref: (w_ref[k], j, 0))
    return pl.pallas_call(
        body, name="grad_sum_chips_" + tag,
        grid_spec=pltpu.PrefetchScalarGridSpec(
            num_scalar_prefetch=1, grid=(rows // tr,),
            in_specs=[slot(1), slot(2), slot(3), slot(4)],
            out_specs=pl.BlockSpec((tr, hc), lambda j, w_ref: (j, w_ref[0]))),
        out_shape=jax.ShapeDtypeStruct((rows, 2 * hc), F32),
        compiler_params=_cp(("parallel",)),
    )(where, p, q, q, q)


def _shard_major(g, axis):
    shape = g.shape
    g = g.reshape(shape[:axis] + (4, shape[axis] // 4) + shape[axis + 1:])
    return jnp.moveaxis(g, axis, 0).reshape(4, -1)


def _unshard(g4, shape, axis):
    n = shape[axis] // 4
    g = g4.reshape((4,) + shape[:axis] + (n,) + shape[axis + 1:])
    return jnp.moveaxis(g, 0, axis).reshape(shape)


def _split(flat, shapes):
    out, off = [], 0
    for shp in shapes:
        n = 1
        for d in shp:
            n *= d
        out.append(flat[..., off:off + n].reshape(flat.shape[:-1] + tuple(shp)))
        off += n
    return out


def _even_rows_to_kernel(wt):
    return jnp.concatenate([wt[:1536], wt[1552:3088], wt[1536:1552], wt[3088:3096],
                            jnp.zeros((PE - 3096, wt.shape[1]), wt.dtype)], axis=0)


def _block_diag(w):
    eye = jnp.eye(8, dtype=w.dtype)
    return (w[:, :, None, :] * eye[:, None, :, None]).reshape(512, 512)


def _diag_blocks(g):
    eye = jnp.eye(8, dtype=g.dtype)
    return (g.reshape(8, 64, 8, 64) * eye[:, None, :, None]).sum(axis=2)


def _shift_down(a, s):
    return a if s == 0 else jnp.pad(a, ((s, 0), (0, 0)))[:a.shape[0]]


def _shift_up(a, s):
    return a if s == 0 else jnp.pad(a, ((0, s), (0, 0)))[s:]


SMALL_SHARDED_SHAPES = [(2, 4, 256), (16, 64), (4, 128), (128,), (128,), (128,), (128,)]
REPL_SHAPES = [(256,), (512,), (8,), (8, 257), (8, 64, 64), (8, 64, 64)]


def kernel(x, norm_w, w_in_even, gla_w_a_up, gla_b_a, gla_norm_w, fox_b_f, w_out_even, w_in_odd, rel_bias, conv_w, conv_b, lru_w_a, lru_b_a, lru_w_x, lru_b_x, lru_lambda, w_out_odd, w_mlp_up, w_mlp_down, loss_target, m_norm_w, m_w_in_even, m_gla_w_a_up, m_gla_b_a, m_gla_norm_w, m_fox_b_f, m_w_out_even, m_w_in_odd, m_rel_bias, m_conv_w, m_conv_b, m_lru_w_a, m_lru_b_a, m_lru_w_x, m_lru_b_x, m_lru_lambda, m_w_out_odd, m_w_mlp_up, m_w_mlp_down, v_norm_w, v_w_in_even, v_gla_w_a_up, v_gla_b_a, v_gla_norm_w, v_fox_b_f, v_w_out_even, v_w_in_odd, v_rel_bias, v_conv_w, v_conv_b, v_lru_w_a, v_lru_b_a, v_lru_w_x, v_lru_b_x, v_lru_lambda, v_w_out_odd, v_w_mlp_up, v_w_mlp_down):
    c_idx = lax.axis_index("c")

    small_local = [norm_w, gla_w_a_up[0], conv_w[0], conv_b[0], lru_b_a[0], lru_b_x[0], lru_lambda[0]]
    small_src = jnp.concatenate([a.reshape(-1) for a in small_local]).reshape(32, 128)
    first = {0: small_src, 1: w_in_even[0].T.astype(BF16), 2: w_out_even[0].astype(BF16)}
    sems0, srcs0, lands0, ag_token = _ag_start([0], first, "allgather_start_0")
    zero = ag_token[0, 0]
    later = {3: (w_mlp_up + zero).astype(BF16), 4: (w_mlp_down + zero).astype(BF16),
             5: (w_in_odd[0] + zero).astype(BF16), 6: (w_out_odd[0] + zero).astype(BF16)}
    sems1, srcs1, lands1, ag_token = _ag_start([1, 2], later, "allgather_start_1")
    ag_sems, ag_srcs = {**sems0, **sems1}, {**srcs0, **srcs1}
    ag_lands = _ag_push_own(ag_srcs, {**lands0, **lands1})

    def gathered(g, after):
        srcs_g, lands_g = _ag_wait(g, ag_sems[g], ag_srcs, ag_lands, after)
        return _ag_forward(g, srcs_g, lands_g)

    small4, w_in_e4, w_out_e = gathered(0, ag_token)
    me = 2 * lax.axis_index("x") + lax.axis_index("y")
    others = [k + (k >= me).astype(jnp.int32) for k in range(3)]
    where = jnp.stack([c_idx, me] + others).astype(jnp.int32)

    w_in_e_t = _even_rows_to_kernel(w_in_e4.reshape(3096, D))
    g_small = _split(small4.reshape(4, 32 * 128), SMALL_SHARDED_SHAPES)
    nw_full = _unshard(g_small[0], (2, 4, 1024), 2)
    wa_up = _unshard(g_small[1], (16, 256), 1)
    cw = _unshard(g_small[2], (4, 512), 1)
    cb, lba, lbx, lam = [_unshard(g, (512,), 0).reshape(1, 512) for g in g_small[3:]]
    nw = lambda layer, i: nw_full[layer, i].reshape(1, D)

    wa_pad = jnp.pad(wa_up, ((0, 128 - 16), (0, 0)))
    gla_ba = gla_b_a.reshape(1, 256)
    gla_nw = gla_norm_w.reshape(1, 512)
    fox_bpad = jnp.pad(fox_b_f.reshape(1, 8), ((0, 0), (FOX_LANE0, 128 - FOX_LANE0 - 8)))
    rbp = jnp.pad(rel_bias[0], ((0, 0), (0, REL_PAD - 257)))
    wa_bd = _block_diag(lru_w_a[0])
    wx_bd = _block_diag(lru_w_x[0])

    x0 = x[0]
    tgt = loss_target[0]

    h0 = _prenorm(x0, nw(0, 0), "prenorm_l0_mix")
    proj_e = _mm(h0, w_in_e_t, "nt", tm=2048, tn=640, name="mm_in_even")
    cat0, s_prev = _gla_fwd(proj_e, wa_pad, gla_ba, gla_nw)
    cum_r = _fox_gate_fwd(proj_e, fox_bpad)
    cum_c = cum_r[:, FOX_LANE0:FOX_LANE0 + 8].T
    cat0 = _fox_fwd(proj_e, cum_c, cat0)
    mix0 = _mm(cat0, w_out_e, "nn", tm=2048, tn=512, name="mm_out_even")
    x1, h1 = _post_pre_fwd(x0, mix0, nw(0, 1), nw(0, 2), "post_pre_l0_mix")
    w_up, w_dn = gathered(1, x1)
    a0, r0 = _mm(h1, w_up, "nn", tm=2048, tn=1024, b_layer=0, relu_pair=True, name="mm_up_l0")
    d0 = _mm(a0, w_dn, "nn", tm=1024, tn=512, b_layer=0, name="mm_down_l0")
    x2, h2 = _post_pre_fwd(x1, d0, nw(0, 3), nw(1, 0), "post_pre_l0_mlp")

    w_in_o, w_out_o = gathered(2, x2)
    proj_o = _mm(h2, w_in_o, "nn", tm=2048, tn=640, name="mm_in_odd")
    bias_q = _bias_build(rbp)
    bias = bias_q.transpose(1, 0, 2)
    kvpad = jnp.pad(proj_o[:, 512:1536], ((CA_PAD, 0), (0, 0)))
    cat1 = _ca_fwd(proj_o, kvpad, bias)
    x_in = proj_o[:, 2048:2560]
    xs = jnp.stack([_shift_down(x_in, 3 - j) for j in range(4)])
    lru_a, lru_b = _lru_pre_fwd(xs, cw, cb, wa_bd, lba, wx_bd, lbx, lam)
    hh = _lru_scan_fwd(lru_a, lru_b)
    cat1 = _lru_post_fwd(hh, proj_o, cat1)
    mix1 = _mm(cat1, w_out_o, "nn", tm=2048, tn=512, name="mm_out_odd")
    x3, h3 = _post_pre_fwd(x2, mix1, nw(1, 1), nw(1, 2), "post_pre_l1_mix")
    a1, r1 = _mm(h3, w_up, "nn", tm=2048, tn=1024, b_layer=1, relu_pair=True, name="mm_up_l1")
    d1 = _mm(a1, w_dn, "nn", tm=1024, tn=512, b_layer=1, name="mm_down_l1")
    g4, loss_part, dd1, dnw13 = _post_loss(x3, d1, nw(1, 3), tgt)
    loss = lax.psum(loss_part[0, 0], ("x", "y", "c"))

    def rs_begin(swap, after, tag):
        gb, recv = _pair_swap_wait(swap, after, tag)
        return _a2a_start(_pair_add(gb, recv, where, tag), tag)

    def rs_end(started, after, tag):
        send_sems, recv_sems, p, q, _ = started
        p, q = _a2a_wait(send_sems, recv_sems, p, q, after, tag)
        return _handover(_sum_chips(p, q, where, tag), tag)

    gba = lax.dynamic_update_slice(lax.empty((4, GA_ROWS, D), BF16), jnp.zeros((4, GA_UP - GA_GAP, D), BF16),
                                   (0, GA_GAP, 0))
    gba = _mm(a1, dd1, "tn", tm=512, tn=1024, into=(gba, 1024, GA_DN), name="mm_down_l1_dw")
    du1 = _mm(dd1, w_dn, "nt", tm=2048, tn=1024, b_layer=1, times2=r1, out_dtype=BF16, name="mm_down_l1_dx")
    gba = _mm(du1, h3, "tn", tm=512, tn=1024, into=(gba, 1024, GA_UP), name="mm_up_l1_dw")
    dh3 = _mm(du1, w_up, "nt", tm=1024, tn=512, b_layer=1, name="mm_up_l1_dx")
    g3, dmix1, dnw12, dnw11 = _pre_post_bwd(x3, nw(1, 2), dh3, g4, mix1, nw(1, 1), "pre_post_bwd_l1_mlp")
    gba = _mm(cat1, dmix1, "tn", tm=128, tn=1024, into=(gba, 256, GA_OUT_O), name="mm_out_odd_dw")
    dcat1 = _mm(dmix1, w_out_o, "nt", tm=2048, tn=512, name="mm_out_odd_dx")

    dq_c, dkpad, dvpad, dbias = _ca_bwd(proj_o, kvpad, bias, dcat1)
    g_rel = _bias_grad(jnp.pad(dbias.transpose(1, 0, 2), ((0, 0), (0, 0), (0, BIAS_W - CA_BAND))))[:, :257]
    dhh, dgate = _lru_post_bwd(hh, proj_o, dcat1)
    da_l, db_l = _lru_scan_bwd(_shift_up(lru_a, 1), _shift_down(hh, 1), dhh)
    dxs, g_cw, g_cb, g_wa_bd, g_lba, g_wx_bd, g_lbx, g_lam = _lru_pre_bwd(xs, cw, cb, wa_bd, lba, wx_bd, lbx, lam, da_l, db_l)
    dx_in = _conv_dx(jnp.stack([_shift_up(dxs[j], 3 - j) for j in range(4)]))
    dproj_o = jnp.concatenate([dq_c, dkpad[CA_PAD:], dvpad[CA_PAD:], dgate, dx_in], axis=1).astype(BF16)
    gba = _mm(dproj_o, h2, "tn", tm=128, tn=1024, into=(gba, 640, GA_IN_O), name="mm_in_odd_dw")
    swap_a = _pair_swap_start(gba, "a")
    dh2 = _mm(dproj_o, w_in_o, "nt", tm=1024, tn=512, name="mm_in_odd_dx")
    g2, dd0, dnw10, dnw03 = _pre_post_bwd(x2, nw(1, 0) + swap_a[4][0, 0], dh2, g3, d0, nw(0, 3), "pre_post_bwd_l1_mix")
    rs_a = rs_begin(swap_a, g2, "a")

    gbb = lax.empty((4, GB_ROWS, D), BF16)
    gbb = _mm(a0, dd0, "tn", tm=512, tn=1024, into=(gbb, 1024, GB_DN), name="mm_down_l0_dw")
    du0 = _mm(dd0, w_dn, "nt", tm=2048, tn=1024, b_layer=0, times2=r0, out_dtype=BF16, name="mm_down_l0_dx")
    gbb = _mm(du0, h1, "tn", tm=512, tn=1024, into=(gbb, 1024, GB_UP), name="mm_up_l0_dw")
    swap_b = _pair_swap_start(gbb, "b")
    dh1 = _mm(du0, w_up, "nt", tm=1024, tn=512, b_layer=0, name="mm_up_l0_dx")
    g1, dmix0, dnw02, dnw01 = _pre_post_bwd(x1, nw(0, 2) + (swap_b[4][0, 0] + rs_a[4][0, 0]), dh1, g2, mix0, nw(0, 1),
                                            "pre_post_bwd_l0_mlp")
    rs_b = rs_begin(swap_b, g1, "b")
    gbc = lax.empty((4, GC_ROWS, D), BF16)
    gbc = _mm(cat0, dmix0, "tn", tm=128, tn=1024, into=(gbc, 256, GC_OUT_E), name="mm_out_even_dw")
    dcat0 = _mm(dmix0, w_out_e, "nt", tm=2048, tn=512, name="mm_out_even_dx")

    dq_g, dk_g, dv_g, dr_g, daux_g, g_wa_pad, g_gla_ba, g_gla_nw = _gla_bwd(
        proj_e, s_prev, wa_pad, gla_ba, gla_nw + rs_b[4][0, 0], dcat0)
    dq_f, dk_f, dv_f, dccol = _fox_bwd(proj_e, cum_c, dcat0)
    dccol_t = jnp.pad(dccol.sum(axis=0).T, ((0, 0), (FOX_LANE0, 128 - FOX_LANE0 - 8)))
    daux, g_fox_bpad = _fox_gate_bwd(proj_e, fox_bpad, dccol_t, daux_g)
    dproj_e = jnp.concatenate([dq_g, dk_g, dv_g, dr_g, dq_f, dk_f, dv_f, daux], axis=1).astype(BF16)
    gt_in_e = _mm(dproj_e, h0, "tn", tm=640, tn=1024, out_dtype=BF16, name="mm_in_even_dw")
    dh0 = _mm(dproj_e, w_in_e_t, "nn", tm=1024, tn=512, name="mm_in_even_dx")
    grad_x, dnw00 = _norm_bwd(x0, nw(0, 0), dh0, g1, "prenorm_l0_mix_bwd")

    def rs_reduce(started, after, tag):
        send_sems, recv_sems, p, q, _ = started
        p, q = _a2a_wait(send_sems, recv_sems, p, q, after, tag)
        return _handover_start(_sum_chips(p, q, where, tag), tag)

    ho_a = rs_reduce(rs_a, grad_x, "a")
    ho_b = rs_reduce(rs_b, ho_a[3], "b")

    g_norm = jnp.stack([jnp.concatenate([dnw00, dnw01, dnw02, dnw03]), jnp.concatenate([dnw10, dnw11, dnw12, dnw13])])
    sharded = [(g_norm, 2), (g_wa_pad[:16], 1), (g_cw, 1), (g_cb[0], 0), (g_lba[0], 0), (g_lbx[0], 0), (g_lam[0], 0)]
    replicated = [g_gla_ba[0], g_gla_nw[0], g_fox_bpad[0, FOX_LANE0:FOX_LANE0 + 8], g_rel, _diag_blocks(g_wa_bd),
                  _diag_blocks(g_wx_bd)]
    small4 = jnp.concatenate([_shard_major(g, ax) for g, ax in sharded]
                             + [jnp.broadcast_to(g.reshape(1, -1), (4, g.size)) for g in replicated], axis=1)
    n_small = small4.shape[1]
    small_rows = GC_ROWS - GC_TAIL - 774
    small4 = jnp.pad(small4, ((0, 0), (0, small_rows * D - n_small))).reshape(4, small_rows, D)
    gt_rows = jnp.concatenate([gt_in_e[:1536], gt_in_e[3072:3088], gt_in_e[1536:3072], gt_in_e[3088:3096]], axis=0)
    tail = jnp.concatenate([gt_rows.reshape(4, 774, D), small4.astype(BF16)], axis=1)
    gbc = lax.dynamic_update_slice(gbc, tail, (0, GC_TAIL, 0))
    swap_c = _pair_swap_start(gbc, "c")
    rs_c = rs_begin(swap_c, swap_c[4], "c")

    red_a = _handover_wait(ho_a, rs_c[4], "a")
    red_b = _handover_wait(ho_b, red_a, "b")
    early = dict(
        w_mlp_up=_adamw_from(w_mlp_up, m_w_mlp_up, v_w_mlp_up, [(red_b, GB_UP, True), (red_a, GA_UP, True)], 256,
                             "adamw_w_mlp_up"),
        w_mlp_down=_adamw_from(w_mlp_down, m_w_mlp_down, v_w_mlp_down, [(red_b, GB_DN, False), (red_a, GA_DN, False)],
                               256, "adamw_w_mlp_down"),
        w_in_odd=_adamw_from(w_in_odd, m_w_in_odd, v_w_in_odd, [(red_a, GA_IN_O, True)], 256, "adamw_w_in_odd"),
        w_out_odd=_adamw_from(w_out_odd, m_w_out_odd, v_w_out_odd, [(red_a, GA_OUT_O, False)], 128, "adamw_w_out_odd"))
    red_c = rs_end(rs_c, early["w_out_odd"][3], "c")

    g_small = _split(red_c[GC_TAIL + 774:].reshape(-1)[:n_small], SMALL_SHARDED_SHAPES + REPL_SHAPES)
    g_of = dict(zip(["norm_w", "gla_w_a_up", "conv_w", "conv_b", "lru_b_a", "lru_b_x", "lru_lambda", "gla_b_a",
                     "gla_norm_w", "fox_b_f", "rel_bias", "lru_w_a", "lru_w_x"], g_small))
    g_of.update(w_in_even=red_c[GC_TAIL:GC_TAIL + 774])
    early["w_out_even"] = _adamw_from(w_out_even, m_w_out_even, v_w_out_even, [(red_c, GC_OUT_E, False)], 256,
                                      "adamw_w_out_even")

    names = ["norm_w", "w_in_even", "gla_w_a_up", "gla_b_a", "gla_norm_w", "fox_b_f", "w_out_even", "w_in_odd", "rel_bias",
             "conv_w", "conv_b", "lru_w_a", "lru_b_a", "lru_w_x", "lru_b_x", "lru_lambda", "w_out_odd", "w_mlp_up",
             "w_mlp_down"]
    w_of = dict(norm_w=norm_w, w_in_even=w_in_even, gla_w_a_up=gla_w_a_up, gla_b_a=gla_b_a, gla_norm_w=gla_norm_w,
                fox_b_f=fox_b_f, w_out_even=w_out_even, w_in_odd=w_in_odd, rel_bias=rel_bias, conv_w=conv_w, conv_b=conv_b,
                lru_w_a=lru_w_a, lru_b_a=lru_b_a, lru_w_x=lru_w_x, lru_b_x=lru_b_x, lru_lambda=lru_lambda,
                w_out_odd=w_out_odd, w_mlp_up=w_mlp_up, w_mlp_down=w_mlp_down)
    m_of = dict(norm_w=m_norm_w, w_in_even=m_w_in_even, gla_w_a_up=m_gla_w_a_up, gla_b_a=m_gla_b_a,
                gla_norm_w=m_gla_norm_w, fox_b_f=m_fox_b_f, w_out_even=m_w_out_even, w_in_odd=m_w_in_odd,
                rel_bias=m_rel_bias, conv_w=m_conv_w, conv_b=m_conv_b, lru_w_a=m_lru_w_a, lru_b_a=m_lru_b_a,
                lru_w_x=m_lru_w_x, lru_b_x=m_lru_b_x, lru_lambda=m_lru_lambda, w_out_odd=m_w_out_odd,
                w_mlp_up=m_w_mlp_up, w_mlp_down=m_w_mlp_down)
    v_of = dict(norm_w=v_norm_w, w_in_even=v_w_in_even, gla_w_a_up=v_gla_w_a_up, gla_b_a=v_gla_b_a,
                gla_norm_w=v_gla_norm_w, fox_b_f=v_fox_b_f, w_out_even=v_w_out_even, w_in_odd=v_w_in_odd,
                rel_bias=v_rel_bias, conv_w=v_conv_w, conv_b=v_conv_b, lru_w_a=v_lru_w_a, lru_b_a=v_lru_b_a,
                lru_w_x=v_lru_w_x, lru_b_x=v_lru_b_x, lru_lambda=v_lru_lambda, w_out_odd=v_w_out_odd,
                w_mlp_up=v_w_mlp_up, w_mlp_down=v_w_mlp_down)
    grads, deltas, new_ms, new_vs = [], [], [], []
    for n in names:
        w = w_of[n]
        if n in early:
            g, d, mn, vn = early[n]
            grads.append(g)
            deltas.append(d)
            new_ms.append(mn)
            new_vs.append(vn)
            continue
        if n == "w_in_even":
            to_view = lambda a: a[0].T
            from_view = lambda a: a.T[None]
        else:
            view = w.shape if w.ndim <= 3 else w.shape[-3:]
            to_view = lambda a, view=view: a.reshape(view)
            from_view = lambda a, w=w: a.reshape(w.shape)
        g = g_of[n] if n == "w_in_even" else to_view(g_of[n])
        d, mn, vn = _adamw(to_view(w), g, to_view(m_of[n]), to_view(v_of[n]), "adamw_" + n)
        grads.append(from_view(g))
        deltas.append(from_view(d))
        new_ms.append(from_view(mn))
        new_vs.append(from_view(vn))

    return (loss, grad_x.reshape(1, T, D), *grads, *deltas, *new_ms, *new_vs)
```

```python
import functools

import jax
import jax.numpy as jnp
from jax import lax
from jax.experimental import pallas as pl
from jax.experimental.pallas import tpu as pltpu

F32 = jnp.float32
BF16 = jnp.bfloat16
MESH = pl.DeviceIdType.MESH

T = 2048
D = 1024
DFF = 4096
EPS = 1e-6
CHUNK = 64
NCHUNK = T // CHUNK
PE = 3200
PO = 2560
AUX_BLK = 3072 // 128
FOX_LANE0 = 16
GLA_SCALE = 64 ** -0.5
ATT_SCALE = 64 ** -0.5
NEG = float(jnp.finfo(jnp.float32).min)
CA_BAND = 576
CA_PAD = 512
REL_PAD = 384

VMEM_LIMIT = 48 * 1024 * 1024

ADAM_LR, ADAM_B1, ADAM_B2, ADAM_EPS, ADAM_WD, ADAM_STEP = 0.001, 0.9, 0.999, 1e-08, 0.01, 10

GA_ROWS, GA_IN_O, GA_OUT_O, GA_GAP, GA_UP, GA_DN = 3072, 0, 640, 896, 1024, 2048
GB_ROWS, GB_UP, GB_DN = 2048, 0, 1024
GC_ROWS, GC_OUT_E, GC_TAIL = 1152, 0, 256

_DIMS = {"nn": (((1,), (0,)), ((), ())), "nt": (((1,), (1,)), ((), ())), "tn": (((0,), (0,)), ((), ()))}


def _cp(sem, **kw):
    return pltpu.CompilerParams(dimension_semantics=sem, vmem_limit_bytes=VMEM_LIMIT, **kw)


def _dot(a, b, mode):
    return lax.dot_general(a.astype(BF16), b.astype(BF16), _DIMS[mode], preferred_element_type=F32)


@functools.partial(jax.custom_vjp, nondiff_argnums=(2,))
def bdot(a, b, mode):
    return _dot(a, b, mode)


def _bdot_fwd(a, b, mode):
    return _dot(a, b, mode), (a, b)


def _bdot_bwd(mode, res, g):
    a, b = res
    if mode == "nn":
        da, db = _dot(g, b, "nt"), _dot(a, g, "tn")
    elif mode == "nt":
        da, db = _dot(g, b, "nn"), _dot(g, a, "tn")
    else:
        da, db = _dot(b, g, "nt"), _dot(a, g, "nn")
    return da.astype(a.dtype), db.astype(b.dtype)


bdot.defvjp(_bdot_fwd, _bdot_bwd)


def _hdot_raw(a, b, mode):
    return lax.dot_general(a, b, _DIMS[mode], precision=lax.Precision.HIGHEST, preferred_element_type=F32)


def _log_sigmoid(x):
    return jnp.minimum(x, 0.0) - jnp.log(1.0 + jnp.exp(-jnp.abs(x)))


def _sigmoid(x):
    return 1.0 / (1.0 + jnp.exp(-x))


def _expm1(x):
    series = x * (1.0 + x * 0.5 * (1.0 + x * (1.0 / 3.0) * (1.0 + x * 0.25)))
    return jnp.where(jnp.abs(x) < 0.03, series, jnp.exp(x) - 1.0)


def _gelu_tanh(x):
    return 0.5 * x * (1.0 + jnp.tanh(0.7978845608028654 * (x + 0.044715 * x * x * x)))


def _iota(shape, dim):
    return lax.broadcasted_iota(jnp.int32, shape, dim)


def _mm(a, b, mode, *, tm, tn, tk=None, out_dtype=F32, name, b_layer=None, into=None, relu_pair=False, times2=None):
    b2 = b.shape[-2:]
    if mode == "nn":
        (m, k), n = a.shape, b2[1]
    elif mode == "nt":
        (m, k), n = a.shape, b2[0]
    else:
        (k, m), n = a.shape, b2[1]
    tk = k if tk is None else tk
    assert m % tm == 0 and n % tn == 0 and k % tk == 0, (name, a.shape, b.shape)
    nk = k // tk
    if mode == "tn":
        a_spec = pl.BlockSpec((tk, tm), lambda i, j, kk: (kk, i))
    elif m == tm and nk == 1:
        a_spec = pl.BlockSpec((tm, tk), lambda i, j, kk: (i, kk), pipeline_mode=pl.Buffered(1))
    else:
        a_spec = pl.BlockSpec((tm, tk), lambda i, j, kk: (i, kk))
    b_blk = {"nn": (tk, tn), "nt": (tn, tk), "tn": (tk, tn)}[mode]
    b_idx = {"nn": lambda i, j, kk: (kk, j), "nt": lambda i, j, kk: (j, kk), "tn": lambda i, j, kk: (kk, j)}[mode]
    if b_layer is None:
        b_spec = pl.BlockSpec(b_blk, b_idx)
    else:
        b_spec = pl.BlockSpec((None,) + b_blk, lambda i, j, kk: (b_layer,) + b_idx(i, j, kk))

    tile = pl.BlockSpec((tm, tn), lambda i, j, kk: (i, j))
    if into is not None:
        buf, per_slot, row_off = into
        assert m == 4 * per_slot and per_slot % tm == 0 and row_off % tm == 0 and buf.shape[2] == n, (name, buf.shape)
        bps = per_slot // tm
        out_specs = pl.BlockSpec((None, tm, tn), lambda i, j, kk: (i // bps, row_off // tm + i % bps, j))
        out_shape = jax.ShapeDtypeStruct(buf.shape, buf.dtype)
        extra_in, extra_specs, aliases = [buf], [pl.BlockSpec(memory_space=pl.ANY)], {2: 0}
        finish = lambda acc, extra: [acc.astype(buf.dtype)]
    elif relu_pair:
        out_specs = (tile, tile)
        out_shape = (jax.ShapeDtypeStruct((m, n), BF16),) * 2
        extra_in, extra_specs, aliases = [], [], {}

        def finish(acc, extra):
            r = jnp.maximum(acc, 0.0)
            return [(r * r).astype(BF16), r.astype(BF16)]
    elif times2 is not None:
        out_specs = tile
        out_shape = jax.ShapeDtypeStruct((m, n), out_dtype)
        extra_in, extra_specs, aliases = [times2], [tile], {}
        finish = lambda acc, extra: [(acc * (2.0 * extra[...].astype(F32))).astype(out_dtype)]
    else:
        out_specs = tile
        out_shape = jax.ShapeDtypeStruct((m, n), out_dtype)
        extra_in, extra_specs, aliases = [], [], {}
        finish = lambda acc, extra: [acc.astype(out_dtype)]
    n_out = 2 if relu_pair else 1

    def body(*refs):
        a_ref, b_ref = refs[0], refs[1]
        extra = refs[2] if extra_in else None
        o_refs = refs[2 + len(extra_in):2 + len(extra_in) + n_out]

        def store(acc):
            for o_ref, val in zip(o_refs, finish(acc, extra)):
                o_ref[...] = val

        if nk == 1:
            store(_dot(a_ref[...], b_ref[...], mode))
            return
        acc_ref = refs[-1]
        kk = pl.program_id(2)

        @pl.when(kk == 0)
        def _():
            acc_ref[...] = jnp.zeros_like(acc_ref)

        acc_ref[...] += _dot(a_ref[...], b_ref[...], mode)

        @pl.when(kk == nk - 1)
        def _():
            store(acc_ref[...])

    return pl.pallas_call(
        body, name=name, grid=(m // tm, n // tn, nk),
        in_specs=[a_spec, b_spec] + extra_specs,
        out_specs=out_specs, out_shape=out_shape,
        scratch_shapes=[pltpu.VMEM((tm, tn), F32)] if nk > 1 else [],
        input_output_aliases=aliases,
        compiler_params=_cp(("parallel", "parallel", "arbitrary")),
    )(a, b, *extra_in)


ROWS = 512


def _prenorm(x, w, name):
    def body(x_ref, w_ref, o_ref):
        xv = x_ref[...]
        r = lax.rsqrt(jnp.mean(xv * xv, axis=-1, keepdims=True) + EPS)
        o_ref[...] = (xv * r * w_ref[...]).astype(BF16)

    return pl.pallas_call(
        body, name=name, grid=(T // ROWS,),
        in_specs=[pl.BlockSpec((ROWS, D), lambda i: (i, 0)), pl.BlockSpec((1, D), lambda i: (0, 0))],
        out_specs=pl.BlockSpec((ROWS, D), lambda i: (i, 0)),
        out_shape=jax.ShapeDtypeStruct((T, D), BF16),
        compiler_params=_cp(("parallel",)),
    )(x, w)


def _rms(z):
    return lax.rsqrt(jnp.mean(z * z, axis=-1, keepdims=True) + EPS)


def _rms_bwd(z, w, dy):
    r = _rms(z)
    wdy = dy * w
    dz = r * wdy - z * (r * r * r) * jnp.mean(z * wdy, axis=-1, keepdims=True)
    return dz, jnp.sum(dy * z * r, axis=0, keepdims=True)


_ROW = pl.BlockSpec((ROWS, D), lambda i: (i, 0))
_VEC = pl.BlockSpec((1, D), lambda i: (0, 0))


def _post_pre_fwd(x, z, w_post, w_pre, name):
    def body(x_ref, z_ref, wp_ref, wn_ref, x_out, h_out):
        zv = z_ref[...]
        xn = x_ref[...] + zv * _rms(zv) * wp_ref[...]
        x_out[...] = xn
        h_out[...] = (xn * _rms(xn) * wn_ref[...]).astype(BF16)

    return pl.pallas_call(
        body, name=name, grid=(T // ROWS,), in_specs=[_ROW, _ROW, _VEC, _VEC], out_specs=(_ROW, _ROW),
        out_shape=(jax.ShapeDtypeStruct((T, D), F32), jax.ShapeDtypeStruct((T, D), BF16)),
        compiler_params=_cp(("parallel",)),
    )(x, z, w_post, w_pre)


def _post_loss(x, z, w_post, tgt):
    def body(x_ref, z_ref, w_ref, t_ref, g_ref, l_ref, dz_ref, dw_ref):
        @pl.when(pl.program_id(0) == 0)
        def _():
            l_ref[...] = jnp.zeros_like(l_ref)
            dw_ref[...] = jnp.zeros_like(dw_ref)

        zv = z_ref[...]
        e = x_ref[...] + zv * _rms(zv) * w_ref[...] - t_ref[...]
        g = e * (1.0 / D)
        g_ref[...] = g
        l_ref[...] += jnp.sum(e * e) * (0.5 / D)
        dz, dw = _rms_bwd(zv, w_ref[...], g)
        dz_ref[...] = dz.astype(BF16)
        dw_ref[...] += dw

    return pl.pallas_call(
        body, name="postnorm_loss", grid=(T // ROWS,), in_specs=[_ROW, _ROW, _VEC, _ROW],
        out_specs=(_ROW, pl.BlockSpec((1, 128), lambda i: (0, 0)), _ROW, _VEC),
        out_shape=(jax.ShapeDtypeStruct((T, D), F32), jax.ShapeDtypeStruct((1, 128), F32),
                   jax.ShapeDtypeStruct((T, D), BF16), jax.ShapeDtypeStruct((1, D), F32)),
        compiler_params=_cp(("arbitrary",)),
    )(x, z, w_post, tgt)


def _pre_post_bwd(x, w_pre, dh, add, z, w_post, name):
    def body(x_ref, wn_ref, dh_ref, add_ref, z_ref, wp_ref, g_ref, dz_ref, dwn_ref, dwp_ref):
        @pl.when(pl.program_id(0) == 0)
        def _():
            dwn_ref[...] = jnp.zeros_like(dwn_ref)
            dwp_ref[...] = jnp.zeros_like(dwp_ref)

        dx, dwn = _rms_bwd(x_ref[...], wn_ref[...], dh_ref[...])
        g = dx + add_ref[...]
        g_ref[...] = g
        dz, dwp = _rms_bwd(z_ref[...], wp_ref[...], g)
        dz_ref[...] = dz.astype(BF16)
        dwn_ref[...] += dwn
        dwp_ref[...] += dwp

    return pl.pallas_call(
        body, name=name, grid=(T // ROWS,), in_specs=[_ROW, _VEC, _ROW, _ROW, _ROW, _VEC],
        out_specs=(_ROW, _ROW, _VEC, _VEC),
        out_shape=(jax.ShapeDtypeStruct((T, D), F32), jax.ShapeDtypeStruct((T, D), BF16),
                   jax.ShapeDtypeStruct((1, D), F32), jax.ShapeDtypeStruct((1, D), F32)),
        compiler_params=_cp(("arbitrary",)),
    )(x, w_pre, dh, add, z, w_post)


def _norm_bwd(z, w, dy, add, name):
    has_add = add is not None

    def body(*refs):
        if has_add:
            z_ref, w_ref, dy_ref, add_ref, dz_ref, dw_ref = refs
        else:
            z_ref, w_ref, dy_ref, dz_ref, dw_ref = refs
        i = pl.program_id(0)

        @pl.when(i == 0)
        def _():
            dw_ref[...] = jnp.zeros_like(dw_ref)

        zv = z_ref[...].astype(F32)
        dyv = dy_ref[...]
        r = lax.rsqrt(jnp.mean(zv * zv, axis=-1, keepdims=True) + EPS)
        wdy = dyv * w_ref[...]
        dz = r * wdy - zv * (r * r * r) * jnp.mean(zv * wdy, axis=-1, keepdims=True)
        if has_add:
            dz = dz + add_ref[...]
        dz_ref[...] = dz.astype(dz_ref.dtype)
        dw_ref[...] += jnp.sum(dyv * zv * r, axis=0, keepdims=True)

    row = pl.BlockSpec((ROWS, D), lambda i: (i, 0))
    vec = pl.BlockSpec((1, D), lambda i: (0, 0))
    ins = [z, w, dy] + ([add] if has_add else [])
    dz_dtype = F32 if has_add else BF16
    return pl.pallas_call(
        body, name=name, grid=(T // ROWS,),
        in_specs=[row, vec, row] + ([row] if has_add else []),
        out_specs=(row, vec),
        out_shape=(jax.ShapeDtypeStruct((T, D), dz_dtype), jax.ShapeDtypeStruct((1, D), F32)),
        compiler_params=_cp(("arbitrary",)),
    )(*ins)


def _adamw_math(w, g, m, v):
    c1 = 1.0 - ADAM_B1 ** ADAM_STEP
    c2 = 1.0 - ADAM_B2 ** ADAM_STEP
    mn = ADAM_B1 * m + (1.0 - ADAM_B1) * g
    vn = ADAM_B2 * v + (1.0 - ADAM_B2) * (g * g)
    return -ADAM_LR * ((mn / c1) / (jnp.sqrt(vn / c2) + ADAM_EPS) + ADAM_WD * w), mn, vn


def _adamw_from(w, m, v, sources, tr, name):
    layers, rows, cols = w.shape
    assert len(sources) == layers and rows % tr == 0, (name, w.shape)
    g_specs = []
    for layer, (buf, row0, transposed) in enumerate(sources):
        step = lambda l, i, layer=layer: jnp.where(l == layer, i, 0)
        if transposed:
            assert row0 % cols == 0 and buf.shape[1] == rows, (name, row0)
            g_specs.append(pl.BlockSpec((cols, tr), lambda l, i, b=row0 // cols, step=step: (b, step(l, i))))
        else:
            assert row0 % tr == 0 and buf.shape[1] == cols, (name, row0)
            g_specs.append(pl.BlockSpec((tr, cols), lambda l, i, b=row0 // tr, step=step: (b + step(l, i), 0)))

    def body(*refs):
        w_ref, m_ref, v_ref = refs[:3]
        g_refs = refs[3:3 + layers]
        g_out, d_ref, mo_ref, vo_ref = refs[3 + layers:]
        gs = [r[...].T if src[2] else r[...] for r, src in zip(g_refs, sources)]
        g = gs[0] if layers == 1 else jnp.where(pl.program_id(0) == 0, gs[0], gs[1])
        g_out[...] = g
        d_ref[...], mo_ref[...], vo_ref[...] = _adamw_math(w_ref[...], g, m_ref[...], v_ref[...])

    blk = pl.BlockSpec((None, tr, cols), lambda l, i: (l, i, 0))
    sds = jax.ShapeDtypeStruct(w.shape, F32)
    return pl.pallas_call(body, name=name, grid=(layers, rows // tr), in_specs=[blk] * 3 + g_specs,
                          out_specs=(blk,) * 4, out_shape=(sds,) * 4,
                          compiler_params=_cp(("parallel", "parallel")))(w, m, v, *[s[0] for s in sources])


def _adamw(w, g, m, v, name):
    lead = w.shape[:-2]
    assert len(lead) <= 1 and g.shape == w.shape, (name, w.shape, g.shape)
    rows, cols = w.shape[-2:]
    if rows <= 512:
        tr, tc = rows, cols
    elif rows % 256 == 0:
        tr, tc = 256, cols
    else:
        tr, tc = rows, 256
    assert rows % tr == 0 and cols % tc == 0, (name, w.shape)
    c1 = 1.0 - ADAM_B1 ** ADAM_STEP
    c2 = 1.0 - ADAM_B2 ** ADAM_STEP

    def body(w_ref, g_ref, m_ref, v_ref, d_ref, mo_ref, vo_ref):
        gv = g_ref[...]
        mn = ADAM_B1 * m_ref[...] + (1.0 - ADAM_B1) * gv
        vn = ADAM_B2 * v_ref[...] + (1.0 - ADAM_B2) * (gv * gv)
        m_hat = mn / c1
        v_hat = vn / c2
        d_ref[...] = -ADAM_LR * (m_hat / (jnp.sqrt(v_hat) + ADAM_EPS) + ADAM_WD * w_ref[...])
        mo_ref[...] = mn
        vo_ref[...] = vn

    if lead:
        grid = (lead[0], rows // tr, cols // tc)
        blk = pl.BlockSpec((None, tr, tc), lambda l, i, j: (l, i, j))
    else:
        grid = (rows // tr, cols // tc)
        blk = pl.BlockSpec((tr, tc), lambda i, j: (i, j))
    sds = jax.ShapeDtypeStruct(w.shape, F32)
    return pl.pallas_call(body, name=name, grid=grid, in_specs=[blk] * 4, out_specs=(blk,) * 3,
                          out_shape=(sds,) * 3, compiler_params=_cp(("parallel",) * len(grid)))(w, g, m, v)


def _running_sum(x, towards_later):
    n = x.shape[0]
    row = _iota(x.shape, 0)
    s = 1
    while s < n:
        if towards_later:
            x = x + jnp.where(row >= s, pltpu.roll(x, s, 0), 0.0)
        else:
            x = x + jnp.where(row < n - s, pltpu.roll(x, n - s, 0), 0.0)
        s *= 2
    return x


@jax.custom_vjp
def _cumsum_rows(x):
    return _running_sum(x, True)


_cumsum_rows.defvjp(lambda x: (_running_sum(x, True), None), lambda _, g: (_running_sum(g, False),))


def _gla_consts():
    return (_iota((256, 512), 0) // 64 == _iota((256, 512), 1) // 128).astype(F32)


def _gla_chunk(mask, q, k, v, r, aux, s_prev, wa, ba, nw):
    la = _log_sigmoid(bdot(aux, wa, "nn") + ba) * (1.0 / 16.0)
    cum = _cumsum_rows(la)
    total = jnp.sum(la, axis=0, keepdims=True)
    k_dec = k * jnp.exp(total - cum)
    inc = bdot(k_dec, v, "tn") * mask
    dec = jnp.exp(jnp.broadcast_to(total, (128, 256)).T)
    dec = jnp.concatenate([dec, dec, dec, dec], axis=1)
    s_new = dec * s_prev + inc
    o = bdot(q * GLA_SCALE, s_new, "nn")
    parts = []
    for h in range(4):
        oh = o[:, h * 128:(h + 1) * 128]
        parts.append(oh * lax.rsqrt(jnp.mean(oh * oh, axis=-1, keepdims=True) + EPS))
    on = jnp.concatenate(parts, axis=1)
    return s_new, on * nw * (r * _sigmoid(r))


GLA_PER_STEP = 4
GLA_ROWS = GLA_PER_STEP * CHUNK
GLA_STEPS = NCHUNK // GLA_PER_STEP


def _gla_specs(cmap):
    return [pl.BlockSpec((GLA_ROWS, 256), lambda c: (cmap(c), 0)),
            pl.BlockSpec((GLA_ROWS, 256), lambda c: (cmap(c), 1)),
            pl.BlockSpec((GLA_ROWS, 512), lambda c: (cmap(c), 1)),
            pl.BlockSpec((GLA_ROWS, 512), lambda c: (cmap(c), 2)),
            pl.BlockSpec((GLA_ROWS, 128), lambda c: (cmap(c), AUX_BLK))]


def _gla_fwd(proj, wa, ba, nw):
    def body(q_ref, k_ref, v_ref, r_ref, aux_ref, wa_ref, ba_ref, nw_ref, o_ref, sp_ref, s_ref):
        @pl.when(pl.program_id(0) == 0)
        def _():
            s_ref[...] = jnp.zeros_like(s_ref)

        s = s_ref[...]
        consts = _gla_consts()
        outs, states = [], []
        for i in range(GLA_PER_STEP):
            rows = slice(i * CHUNK, (i + 1) * CHUNK)
            states.append(s)
            s, out = _gla_chunk(consts, q_ref[rows, :], k_ref[rows, :], v_ref[rows, :], r_ref[rows, :], aux_ref[rows, :],
                                s, wa_ref[...], ba_ref[...], nw_ref[...])
            outs.append(out)
        s_ref[...] = s
        for i in range(GLA_PER_STEP):
            o_ref[i * CHUNK:(i + 1) * CHUNK, :] = outs[i]
            sp_ref[i] = states[i]

    full = lambda shape: pl.BlockSpec(shape, lambda c: (0,) * len(shape))
    return pl.pallas_call(
        body, name="gla_fwd", grid=(GLA_STEPS,),
        in_specs=_gla_specs(lambda c: c) + [full((128, 256)), full((1, 256)), full((1, 512))],
        out_specs=(pl.BlockSpec((GLA_ROWS, 512), lambda c: (c, 0)),
                   pl.BlockSpec((GLA_PER_STEP, 256, 512), lambda c: (c, 0, 0))),
        out_shape=(jax.ShapeDtypeStruct((T, D), F32), jax.ShapeDtypeStruct((NCHUNK, 256, 512), F32)),
        scratch_shapes=[pltpu.VMEM((256, 512), F32)],
        compiler_params=_cp(("arbitrary",)),
    )(proj, proj, proj, proj, proj, wa, ba, nw)


def _gla_bwd(proj, s_prev_all, wa, ba, nw, dcat):
    rev = lambda c: GLA_STEPS - 1 - c

    def body(q_ref, k_ref, v_ref, r_ref, aux_ref, sp_ref, wa_ref, ba_ref, nw_ref, do_ref,
             dq_ref, dk_ref, dv_ref, dr_ref, daux_ref, dwa_ref, dba_ref, dnw_ref, ds_ref):
        @pl.when(pl.program_id(0) == 0)
        def _():
            ds_ref[...] = jnp.zeros_like(ds_ref)
            dwa_ref[...] = jnp.zeros_like(dwa_ref)
            dba_ref[...] = jnp.zeros_like(dba_ref)
            dnw_ref[...] = jnp.zeros_like(dnw_ref)

        fn = functools.partial(_gla_chunk, _gla_consts())
        ds = ds_ref[...]
        dwa, dba, dnw = dwa_ref[...], dba_ref[...], dnw_ref[...]
        grads = {}
        for i in reversed(range(GLA_PER_STEP)):
            rows = slice(i * CHUNK, (i + 1) * CHUNK)
            _, vjp = jax.vjp(fn, q_ref[rows, :], k_ref[rows, :], v_ref[rows, :], r_ref[rows, :], aux_ref[rows, :],
                             sp_ref[i], wa_ref[...], ba_ref[...], nw_ref[...])
            *grads[i], ds, dwa_i, dba_i, dnw_i = vjp((ds, do_ref[rows, :]))
            dwa, dba, dnw = dwa + dwa_i, dba + dba_i, dnw + dnw_i
        ds_ref[...] = ds
        dwa_ref[...] = dwa
        dba_ref[...] = dba
        dnw_ref[...] = dnw
        for i in range(GLA_PER_STEP):
            rows = slice(i * CHUNK, (i + 1) * CHUNK)
            for ref, g in zip((dq_ref, dk_ref, dv_ref, dr_ref, daux_ref), grads[i]):
                ref[rows, :] = g

    full = lambda shape: pl.BlockSpec(shape, lambda c: (0,) * len(shape))
    blk = lambda w: pl.BlockSpec((GLA_ROWS, w), lambda c: (rev(c), 0))
    sds = lambda *s: jax.ShapeDtypeStruct(s, F32)
    return pl.pallas_call(
        body, name="gla_bwd", grid=(GLA_STEPS,),
        in_specs=_gla_specs(rev) + [pl.BlockSpec((GLA_PER_STEP, 256, 512), lambda c: (rev(c), 0, 0)),
                                    full((128, 256)), full((1, 256)), full((1, 512)), blk(512)],
        out_specs=(blk(256), blk(256), blk(512), blk(512), blk(128), full((128, 256)), full((1, 256)), full((1, 512))),
        out_shape=(sds(T, 256), sds(T, 256), sds(T, 512), sds(T, 512), sds(T, 128),
                   sds(128, 256), sds(1, 256), sds(1, 512)),
        scratch_shapes=[pltpu.VMEM((256, 512), F32)],
        compiler_params=_cp(("arbitrary",)),
    )(proj, proj, proj, proj, proj, s_prev_all, wa, ba, nw, dcat)


def _prefix8(x, towards_later):
    row = _iota(x.shape, 0)
    for s in (1, 2, 4):
        if towards_later:
            keep, shift = row >= s, s
        else:
            keep, shift = row < 8 - s, 8 - s
        x = x + jnp.where(keep, pltpu.roll(x, shift, 0), 0.0)
    return x


def _fox_gate_fwd(proj, bpad):
    def body(aux_ref, b_ref, cum_ref):
        def step(i, carry):
            rows = pl.ds(pl.multiple_of(i * 8, 8), 8)
            cum = _prefix8(_log_sigmoid(aux_ref[rows, :] + b_ref[...]), True) + carry
            cum_ref[rows, :] = cum
            return jnp.broadcast_to(cum[7:, :], (8, 128))

        lax.fori_loop(0, T // 8, step, jnp.zeros((8, 128), F32), unroll=4)

    return pl.pallas_call(
        body, name="fox_gate_fwd", grid=(1,),
        in_specs=[pl.BlockSpec((T, 128), lambda i: (0, AUX_BLK)), pl.BlockSpec((1, 128), lambda i: (0, 0))],
        out_specs=pl.BlockSpec((T, 128), lambda i: (0, 0)),
        out_shape=jax.ShapeDtypeStruct((T, 128), F32),
        compiler_params=_cp(("arbitrary",)),
    )(proj, bpad)


def _fox_gate_bwd(proj, bpad, dccol_t, daux_gla):
    def body(aux_ref, b_ref, dc_ref, dg_ref, daux_ref, db_ref):
        def step(i, state):
            carry, db = state
            rows = pl.ds(pl.multiple_of(T - 8 * (i + 1), 8), 8)
            dlf = _prefix8(dc_ref[rows, :], False) + carry
            dz = dlf * _sigmoid(-(aux_ref[rows, :] + b_ref[...]))
            daux_ref[rows, :] = dz + dg_ref[rows, :]
            return jnp.broadcast_to(dlf[:1, :], (8, 128)), db + dz

        zeros = jnp.zeros((8, 128), F32)
        _, db = lax.fori_loop(0, T // 8, step, (zeros, zeros), unroll=4)
        db_ref[...] = jnp.sum(db, axis=0, keepdims=True)

    whole = pl.BlockSpec((T, 128), lambda i: (0, 0))
    vec = pl.BlockSpec((1, 128), lambda i: (0, 0))
    return pl.pallas_call(
        body, name="fox_gate_bwd", grid=(1,),
        in_specs=[pl.BlockSpec((T, 128), lambda i: (0, AUX_BLK)), vec, whole, whole],
        out_specs=(whole, vec),
        out_shape=(jax.ShapeDtypeStruct((T, 128), F32), jax.ShapeDtypeStruct((1, 128), F32)),
        compiler_params=_cp(("arbitrary",)),
    )(proj, bpad, dccol_t, daux_gla)


FOX_Q = 256


FOX_QB = T // FOX_Q


@jax.custom_vjp
def _attend(s, v):
    return _attend_fwd(s, v)[0]


def _attend_fwd(s, v):
    e = jnp.exp(s - jnp.max(s, axis=-1, keepdims=True))
    r = 1.0 / jnp.sum(e, axis=-1, keepdims=True)
    return _dot(e, v, "nn") * r, (e, r, v)


def _attend_bwd(res, do):
    e, r, v = res
    do_r = do * r
    dpr = _dot(do_r, v, "nt")
    ds = e * (dpr - r * jnp.sum(e * dpr, axis=-1, keepdims=True))
    return ds, _dot(e, do_r, "tn").astype(v.dtype)


_attend.defvjp(_attend_fwd, _attend_bwd)


def _fox_block(hp, q, k, v, ccol):
    kl = k.shape[0]
    lane = _iota((FOX_Q, 128), 1)
    tri = jnp.bitwise_and(_iota((2 * FOX_Q, FOX_Q), 0), FOX_Q - 1) >= _iota((2 * FOX_Q, FOX_Q), 1)
    sub = _iota((8, kl), 0)
    qs = q * ATT_SCALE
    q2 = jnp.concatenate([jnp.where(lane < 64, qs, 0.0), jnp.where(lane >= 64, qs, 0.0)], axis=0)
    s = bdot(q2, k, "nt")
    cs = [jnp.sum(jnp.where(sub == 2 * hp + e, ccol, 0.0), axis=0, keepdims=True) for e in range(2)]
    s = jnp.concatenate([s[:FOX_Q] - cs[0], s[FOX_Q:] - cs[1]], axis=0)
    diag = jnp.where(tri, s[:, kl - FOX_Q:], NEG)
    s = diag if kl == FOX_Q else jnp.concatenate([s[:, :kl - FOX_Q], diag], axis=1)
    o2 = _attend(s, v)
    return jnp.where(lane < 64, o2[:FOX_Q], o2[FOX_Q:])


def _fox_in_specs():
    return [pl.BlockSpec((FOX_Q, 128), lambda hp, qb: (qb, 12 + hp)),
            pl.BlockSpec((T, 128), lambda hp, qb: (0, 16 + hp)),
            pl.BlockSpec((T, 128), lambda hp, qb: (0, 20 + hp)),
            pl.BlockSpec((8, T), lambda hp, qb: (0, 0))]


def _fox_fwd(proj, cum_c, cat):
    def body(q_ref, k_ref, v_ref, cc_ref, cat_ref, o_ref):
        qb = pl.program_id(1)
        for g in range(FOX_QB):
            kl = FOX_Q * (g + 1)

            @pl.when(qb == g)
            def _(kl=kl):
                o_ref[...] = _fox_block(pl.program_id(0), q_ref[...], k_ref[0:kl, :], v_ref[0:kl, :], cc_ref[:, 0:kl])

    return pl.pallas_call(
        body, name="fox_fwd", grid=(4, FOX_QB), in_specs=_fox_in_specs() + [pl.BlockSpec(memory_space=pl.ANY)],
        out_specs=pl.BlockSpec((FOX_Q, 128), lambda hp, qb: (qb, 4 + hp)),
        out_shape=jax.ShapeDtypeStruct((T, D), F32), input_output_aliases={4: 0},
        compiler_params=_cp(("parallel", "parallel")),
    )(proj, proj, proj, cum_c, cat)


def _fox_bwd(proj, cum_c, dcat):
    def body(q_ref, k_ref, v_ref, cc_ref, do_ref, dq_ref, dk_ref, dv_ref, dcc_ref):
        qb = pl.program_id(1)

        @pl.when(qb == 0)
        def _():
            dk_ref[...] = jnp.zeros_like(dk_ref)
            dv_ref[...] = jnp.zeros_like(dv_ref)
            dcc_ref[...] = jnp.zeros_like(dcc_ref)

        fn = functools.partial(_fox_block, pl.program_id(0))
        for g in range(FOX_QB):
            kl = FOX_Q * (g + 1)

            @pl.when(qb == g)
            def _(kl=kl):
                _, vjp = jax.vjp(fn, q_ref[...], k_ref[0:kl, :], v_ref[0:kl, :], cc_ref[:, 0:kl])
                dq, dk, dv, dcc = vjp(do_ref[...])
                dq_ref[...] = dq
                dk_ref[0:kl, :] += dk
                dv_ref[0:kl, :] += dv
                dcc_ref[:, 0:kl] += dcc

    sds = lambda *s: jax.ShapeDtypeStruct(s, F32)
    return pl.pallas_call(
        body, name="fox_bwd", grid=(4, FOX_QB),
        in_specs=_fox_in_specs() + [pl.BlockSpec((FOX_Q, 128), lambda hp, qb: (qb, 4 + hp))],
        out_specs=(pl.BlockSpec((FOX_Q, 128), lambda hp, qb: (qb, hp)),
                   pl.BlockSpec((T, 128), lambda hp, qb: (0, hp)),
                   pl.BlockSpec((T, 128), lambda hp, qb: (0, hp)),
                   pl.BlockSpec((None, 8, T), lambda hp, qb: (hp, 0, 0))),
        out_shape=(sds(T, 512), sds(T, 512), sds(T, 512), sds(4, 8, T)),
        compiler_params=_cp(("parallel", "arbitrary")),
    )(proj, proj, proj, cum_c, dcat)


BIAS_W = 640


def _rel_onehot():
    j = _iota((REL_PAD, BIAS_W), 1)
    rel = jnp.clip(CA_PAD + CHUNK - 1 - j, -128, 128) + 128
    return (_iota((REL_PAD, BIAS_W), 0) == rel).astype(F32)


def _bias_build(rbp):
    def body(rb_ref, o_ref):
        f = _hdot_raw(rb_ref[...], _rel_onehot(), "nn")
        for q in range(CHUNK):
            o_ref[q] = pltpu.roll(f, (BIAS_W - (CHUNK - 1 - q)) % BIAS_W, 1)[:, :CA_BAND]

    return pl.pallas_call(body, name="ca_bias_build", out_shape=jax.ShapeDtypeStruct((CHUNK, 8, CA_BAND), F32))(rbp)


def _bias_grad(dbias_q):
    def body(db_ref, o_ref):
        acc = jnp.zeros((8, BIAS_W), F32)
        for q in range(CHUNK):
            acc = acc + pltpu.roll(db_ref[q], CHUNK - 1 - q, 1)
        o_ref[...] = _hdot_raw(acc, _rel_onehot(), "nt")

    return pl.pallas_call(body, name="ca_bias_grad", out_shape=jax.ShapeDtypeStruct((8, REL_PAD), F32))(dbias_q)


def _ca_block(c, masked, q, kb, vb, bias2):
    lane = _iota((CHUNK, 128), 1)
    qs = q * ATT_SCALE
    q2 = jnp.concatenate([jnp.where(lane < 64, qs, 0.0), jnp.where(lane >= 64, qs, 0.0)], axis=0)
    s = bdot(q2, kb, "nt") + bias2.reshape(2 * CHUNK, CA_BAND)
    if masked:
        s = jnp.where((c * CHUNK - CA_PAD + _iota((2 * CHUNK, CA_BAND), 1)) >= 0, s, NEG)
    o2 = _attend(s, vb)
    return jnp.where(lane < 64, o2[:CHUNK], o2[CHUNK:])


CA_PER_STEP = 8
CA_ROWS = CA_PER_STEP * CHUNK
CA_MASKED_STEPS = -(-CA_PAD // CA_ROWS)


def _ca_fwd(proj, kvpad, bias):
    def body(q_ref, k_ref, v_ref, b_ref, o_ref):
        def run(masked):
            outs = []
            for i in range(CA_PER_STEP):
                c = pl.program_id(1) * CA_PER_STEP + i
                band = pl.ds(pl.multiple_of(c * CHUNK, CHUNK), CA_BAND)
                rows = slice(i * CHUNK, (i + 1) * CHUNK)
                outs.append(_ca_block(c, masked, q_ref[rows, :], k_ref[band, :], v_ref[band, :], b_ref[...]))
            for i in range(CA_PER_STEP):
                o_ref[i * CHUNK:(i + 1) * CHUNK, :] = outs[i]

        pl.when(pl.program_id(1) < CA_MASKED_STEPS)(lambda: run(True))
        pl.when(pl.program_id(1) >= CA_MASKED_STEPS)(lambda: run(False))

    return pl.pallas_call(
        body, name="ca_fwd", grid=(4, NCHUNK // CA_PER_STEP),
        in_specs=[pl.BlockSpec((CA_ROWS, 128), lambda hp, c: (c, hp)),
                  pl.BlockSpec((T + CA_PAD, 128), lambda hp, c: (0, hp)),
                  pl.BlockSpec((T + CA_PAD, 128), lambda hp, c: (0, 4 + hp)),
                  pl.BlockSpec((2, CHUNK, CA_BAND), lambda hp, c: (hp, 0, 0))],
        out_specs=pl.BlockSpec((CA_ROWS, 128), lambda hp, c: (c, hp)),
        out_shape=jax.ShapeDtypeStruct((T, D), F32),
        compiler_params=_cp(("parallel", "parallel")),
    )(proj, kvpad, kvpad, bias)


def _ca_bwd(proj, kvpad, bias, dcat):
    def body(q_ref, k_ref, v_ref, b_ref, do_ref, dq_ref, dk_ref, dv_ref, db_ref):
        c = pl.program_id(1)

        @pl.when(c == 0)
        def _():
            dk_ref[...] = jnp.zeros_like(dk_ref)
            dv_ref[...] = jnp.zeros_like(dv_ref)
            db_ref[...] = jnp.zeros_like(db_ref)

        def run(masked):
            grads, bands = [], []
            for i in range(CA_PER_STEP):
                ci = c * CA_PER_STEP + i
                band = pl.ds(pl.multiple_of(ci * CHUNK, CHUNK), CA_BAND)
                rows = slice(i * CHUNK, (i + 1) * CHUNK)
                fn = functools.partial(_ca_block, ci, masked)
                _, vjp = jax.vjp(fn, q_ref[rows, :], k_ref[band, :], v_ref[band, :], b_ref[...])
                grads.append(vjp(do_ref[rows, :]))
                bands.append(band)
            for i, (dq, _, _, _) in enumerate(grads):
                dq_ref[i * CHUNK:(i + 1) * CHUNK, :] = dq
            for band, (_, dkb, dvb, _) in zip(bands, grads):
                dk_ref[band, :] += dkb
                dv_ref[band, :] += dvb
            db_ref[...] += functools.reduce(lambda a, b: a + b, [g[3] for g in grads])

        pl.when(c < CA_MASKED_STEPS)(lambda: run(True))
        pl.when(c >= CA_MASKED_STEPS)(lambda: run(False))

    sds = lambda *s: jax.ShapeDtypeStruct(s, F32)
    padded = lambda: pl.BlockSpec((T + CA_PAD, 128), lambda hp, c: (0, hp))
    return pl.pallas_call(
        body, name="ca_bwd", grid=(4, NCHUNK // CA_PER_STEP),
        in_specs=[pl.BlockSpec((CA_ROWS, 128), lambda hp, c: (c, hp)),
                  pl.BlockSpec((T + CA_PAD, 128), lambda hp, c: (0, hp)),
                  pl.BlockSpec((T + CA_PAD, 128), lambda hp, c: (0, 4 + hp)),
                  pl.BlockSpec((2, CHUNK, CA_BAND), lambda hp, c: (hp, 0, 0)),
                  pl.BlockSpec((CA_ROWS, 128), lambda hp, c: (c, hp))],
        out_specs=(pl.BlockSpec((CA_ROWS, 128), lambda hp, c: (c, hp)), padded(), padded(),
                   pl.BlockSpec((2, CHUNK, CA_BAND), lambda hp, c: (hp, 0, 0))),
        out_shape=(sds(T, 512), sds(T + CA_PAD, 512), sds(T + CA_PAD, 512), sds(8, CHUNK, CA_BAND)),
        compiler_params=_cp(("parallel", "arbitrary")),
    )(proj, kvpad, kvpad, bias, dcat)


def _lru_pre(xs, cw, cb, wa, ba, wx, bx, lam):
    xc = cb + xs[0] * cw[0:1, :] + xs[1] * cw[1:2, :] + xs[2] * cw[2:3, :] + xs[3] * cw[3:4, :]
    ra = _sigmoid(bdot(xc, wa, "nn") + ba)
    ii = _sigmoid(bdot(xc, wx, "nn") + bx)
    la = 8.0 * ra * _log_sigmoid(lam)
    return jnp.exp(la), jnp.sqrt(-_expm1(2.0 * la)) * (ii * xc)


def _lru_pre_specs():
    full = lambda shape: pl.BlockSpec(shape, lambda i: (0,) * len(shape))
    return [pl.BlockSpec((4, ROWS, 512), lambda i: (0, i, 0)), full((4, 512)), full((1, 512)),
            full((512, 512)), full((1, 512)), full((512, 512)), full((1, 512)), full((1, 512))]


def _lru_pre_fwd(xs, cw, cb, wa, ba, wx, bx, lam):
    def body(xs_ref, cw_ref, cb_ref, wa_ref, ba_ref, wx_ref, bx_ref, lam_ref, a_ref, b_ref):
        a, b = _lru_pre(xs_ref[...], cw_ref[...], cb_ref[...], wa_ref[...], ba_ref[...], wx_ref[...], bx_ref[...],
                        lam_ref[...])
        a_ref[...] = a
        b_ref[...] = b

    row = pl.BlockSpec((ROWS, 512), lambda i: (i, 0))
    sds = jax.ShapeDtypeStruct((T, 512), F32)
    return pl.pallas_call(body, name="lru_pre_fwd", grid=(T // ROWS,), in_specs=_lru_pre_specs(),
                          out_specs=(row, row), out_shape=(sds, sds), compiler_params=_cp(("parallel",)),
                          )(xs, cw, cb, wa, ba, wx, bx, lam)


def _lru_pre_bwd(xs, cw, cb, wa, ba, wx, bx, lam, da, db):
    def body(xs_ref, cw_ref, cb_ref, wa_ref, ba_ref, wx_ref, bx_ref, lam_ref, da_ref, db_ref,
             dxs_ref, dcw_ref, dcb_ref, dwa_ref, dba_ref, dwx_ref, dbx_ref, dlam_ref):
        acc = (dcw_ref, dcb_ref, dwa_ref, dba_ref, dwx_ref, dbx_ref, dlam_ref)

        @pl.when(pl.program_id(0) == 0)
        def _():
            for r in acc:
                r[...] = jnp.zeros_like(r)

        _, vjp = jax.vjp(_lru_pre, xs_ref[...], cw_ref[...], cb_ref[...], wa_ref[...], ba_ref[...], wx_ref[...],
                         bx_ref[...], lam_ref[...])
        grads = vjp((da_ref[...], db_ref[...]))
        dxs_ref[...] = grads[0]
        for r, g in zip(acc, grads[1:]):
            r[...] += g

    row = pl.BlockSpec((ROWS, 512), lambda i: (i, 0))
    specs = _lru_pre_specs()
    sds = lambda *s: jax.ShapeDtypeStruct(s, F32)
    return pl.pallas_call(
        body, name="lru_pre_bwd", grid=(T // ROWS,), in_specs=specs + [row, row], out_specs=tuple(specs),
        out_shape=(sds(4, T, 512), sds(4, 512), sds(1, 512), sds(512, 512), sds(1, 512), sds(512, 512), sds(1, 512),
                   sds(1, 512)),
        compiler_params=_cp(("arbitrary",)),
    )(xs, cw, cb, wa, ba, wx, bx, lam, da, db)


SCAN_ROWS = 8


def _scan8(a, b, towards_later):
    row = _iota((SCAN_ROWS, 512), 0)
    for s in (1, 2, 4):
        if towards_later:
            keep, shift = row >= s, s
        else:
            keep, shift = row < SCAN_ROWS - s, SCAN_ROWS - s
        a_s = jnp.where(keep, pltpu.roll(a, shift, 0), 1.0)
        b_s = jnp.where(keep, pltpu.roll(b, shift, 0), 0.0)
        b = a * b_s + b
        a = a * a_s
    return a, b


def _lru_scan_fwd(a, b):
    def body(a_ref, b_ref, h_ref):
        def step(i, carry):
            rows = pl.ds(pl.multiple_of(i * SCAN_ROWS, SCAN_ROWS), SCAN_ROWS)
            a8, b8 = _scan8(a_ref[rows, :], b_ref[rows, :], True)
            h = a8 * carry + b8
            h_ref[rows, :] = h
            return jnp.broadcast_to(h[SCAN_ROWS - 1:, :], (SCAN_ROWS, 512))

        lax.fori_loop(0, T // SCAN_ROWS, step, jnp.zeros((SCAN_ROWS, 512), F32), unroll=2)

    return pl.pallas_call(body, name="lru_scan_fwd", out_shape=jax.ShapeDtypeStruct((T, 512), F32),
                          compiler_params=pltpu.CompilerParams(vmem_limit_bytes=VMEM_LIMIT))(a, b)


def _lru_scan_bwd(a_next, h_prev, dh):
    def body(a_ref, h_ref, dh_ref, da_ref, db_ref):
        def step(i, carry):
            start = T - SCAN_ROWS * (i + 1)
            rows = pl.ds(pl.multiple_of(start, SCAN_ROWS), SCAN_ROWS)
            a8, b8 = _scan8(a_ref[rows, :], dh_ref[rows, :], False)
            g = a8 * carry + b8
            db_ref[rows, :] = g
            da_ref[rows, :] = g * h_ref[rows, :]
            return jnp.broadcast_to(g[:1, :], (SCAN_ROWS, 512))

        lax.fori_loop(0, T // SCAN_ROWS, step, jnp.zeros((SCAN_ROWS, 512), F32), unroll=2)

    sds = jax.ShapeDtypeStruct((T, 512), F32)
    return pl.pallas_call(body, name="lru_scan_bwd", out_shape=(sds, sds),
                          compiler_params=pltpu.CompilerParams(vmem_limit_bytes=VMEM_LIMIT))(a_next, h_prev, dh)


def _lru_post(h, gate):
    return h * _gelu_tanh(gate)


def _lru_post_fwd(h, proj, cat):
    def body(h_ref, g_ref, cat_ref, o_ref):
        o_ref[...] = _lru_post(h_ref[...], g_ref[...])

    row = pl.BlockSpec((ROWS, 512), lambda i: (i, 0))
    return pl.pallas_call(body, name="lru_post_fwd", grid=(T // ROWS,),
                          in_specs=[row, pl.BlockSpec((ROWS, 512), lambda i: (i, 3)), pl.BlockSpec(memory_space=pl.ANY)],
                          out_specs=pl.BlockSpec((ROWS, 512), lambda i: (i, 1)),
                          out_shape=jax.ShapeDtypeStruct((T, D), F32), input_output_aliases={2: 0},
                          compiler_params=_cp(("parallel",)))(h, proj, cat)


def _lru_post_bwd(h, proj, dcat):
    def body(h_ref, g_ref, do_ref, dh_ref, dg_ref):
        _, vjp = jax.vjp(_lru_post, h_ref[...], g_ref[...])
        dh, dg = vjp(do_ref[...])
        dh_ref[...] = dh
        dg_ref[...] = dg

    row = pl.BlockSpec((ROWS, 512), lambda i: (i, 0))
    sds = jax.ShapeDtypeStruct((T, 512), F32)
    return pl.pallas_call(body, name="lru_post_bwd", grid=(T // ROWS,),
                          in_specs=[row, pl.BlockSpec((ROWS, 512), lambda i: (i, 3)),
                                    pl.BlockSpec((ROWS, 512), lambda i: (i, 1))],
                          out_specs=(row, row), out_shape=(sds, sds), compiler_params=_cp(("parallel",)))(h, proj, dcat)


def _conv_dx(dxs_shift):
    def body(d_ref, o_ref):
        o_ref[...] = d_ref[0] + d_ref[1] + d_ref[2] + d_ref[3]

    row = pl.BlockSpec((ROWS, 512), lambda i: (i, 0))
    return pl.pallas_call(body, name="lru_conv_dx", grid=(T // ROWS,),
                          in_specs=[pl.BlockSpec((4, ROWS, 512), lambda i: (0, i, 0))], out_specs=row,
                          out_shape=jax.ShapeDtypeStruct((T, 512), F32), compiler_params=_cp(("parallel",)))(dxs_shift)


def _position():
    return lax.axis_index("x"), lax.axis_index("y"), lax.axis_index("c")


def _other_chips(x, y):
    return [(1 - x, y), (x, 1 - y), (1 - x, 1 - y)]


def _al(v, n):
    return v * n if isinstance(v, int) else pl.multiple_of(v * n, n)


_AG_ITEMS = [
    ((4, 32, 128), lambda o, s, h: o.at[s, pl.ds(_al(h, 16), 16), :], lambda r, h: r.at[pl.ds(_al(h, 16), 16), :]),
    ((4, 774, 1024), lambda o, s, h: o.at[s, :, pl.ds(_al(h, 512), 512)], lambda r, h: r.at[:, pl.ds(_al(h, 512), 512)]),
    ((1024, 1024), lambda o, s, h: o.at[pl.ds(_al(2 * s + h, 128), 128), :], lambda r, h: r.at[pl.ds(_al(h, 128), 128), :]),
    ((2, 1024, 4096), lambda o, s, h: o.at[h, :, pl.ds(_al(s, 1024), 1024)], lambda r, h: r.at[h]),
    ((2, 4096, 1024), lambda o, s, h: o.at[h, pl.ds(_al(s, 1024), 1024), :], lambda r, h: r.at[h]),
    ((1024, 2560), lambda o, s, h: o.at[pl.ds(_al(h, 512), 512), pl.ds(_al(s, 640), 640)],
     lambda r, h: r.at[pl.ds(_al(h, 512), 512), :]),
    ((1024, 1024), lambda o, s, h: o.at[pl.ds(_al(2 * s + h, 128), 128), :], lambda r, h: r.at[pl.ds(_al(h, 128), 128), :]),
]


_AG_GROUPS = [(0, 1, 2), (3, 4), (5, 6)]

_HBM = pl.BlockSpec(memory_space=pltpu.HBM)
_SEM = pl.BlockSpec(memory_space=pltpu.SEMAPHORE)
_SPLIT = dict(has_side_effects=pltpu.SideEffectType.DATAFLOW_SIDE_EFFECTING)


def _hbm(a):
    return pltpu.with_memory_space_constraint(a, pltpu.HBM)


def _ag_ici_copy(i, j, chip, c, slot, src_ref, land_ref, send_sems, recv_sems, k):
    _, dst, half = _AG_ITEMS[i]
    return pltpu.make_async_remote_copy(src_ref=half(src_ref, c), dst_ref=dst(land_ref, slot, c), send_sem=send_sems.at[k],
                                        recv_sem=recv_sems.at[k], device_id=(*chip, c), device_id_type=MESH)


def _ag_start(groups, shards, name):
    items_all = [i for g in groups for i in _AG_GROUPS[g]]
    n = len(items_all)
    ng = len(groups)
    lands = [lax.empty(_AG_ITEMS[i][0], shards[i].dtype) for i in items_all]

    def body(*refs):
        srcs, land_refs = dict(zip(items_all, refs[:n])), dict(zip(items_all, refs[n:2 * n]))
        sems = refs[2 * n:2 * n + 2 * ng]
        token = refs[-1]
        x, y, c = _position()
        me = 2 * x + y
        for gi, g in enumerate(groups):
            for t, i in enumerate(_AG_GROUPS[g]):
                for j, chip in enumerate(_other_chips(x, y)):
                    _ag_ici_copy(i, j, chip, c, me, srcs[i], land_refs[i], sems[2 * gi], sems[2 * gi + 1], 3 * t + j).start()
        token[...] = jnp.zeros_like(token)

    sem_shapes = []
    for g in groups:
        sem_shapes += [pltpu.SemaphoreType.DMA((3 * len(_AG_GROUPS[g]),))] * 2
    ops = [shards[i] for i in items_all] + lands
    out = pl.pallas_call(
        body, name=name,
        out_shape=tuple(sem_shapes) + tuple(pltpu.HBM(a.shape, a.dtype) for a in ops) + (jax.ShapeDtypeStruct((8, 128), F32),),
        in_specs=(_HBM,) * (2 * n),
        out_specs=(_SEM,) * (2 * ng) + (_HBM,) * (2 * n) + (pl.BlockSpec(memory_space=pltpu.VMEM),),
        input_output_aliases={i: 2 * ng + i for i in range(2 * n)},
        compiler_params=pltpu.CompilerParams(**_SPLIT),
    )(*[_hbm(a) for a in ops])
    sems, thru, token = out[:2 * ng], out[2 * ng:-1], out[-1]
    return ({g: (sems[2 * gi], sems[2 * gi + 1]) for gi, g in enumerate(groups)},
            dict(zip(items_all, thru[:n])), dict(zip(items_all, thru[n:])), token)


def _ag_wait(g, sems, srcs, lands, after):
    items = _AG_GROUPS[g]
    m = len(items)

    def body(*refs):
        src_refs, land_refs = refs[:m], refs[m:2 * m]
        send_sems, recv_sems = refs[2 * m], refs[2 * m + 1]
        x, y, c = _position()
        for t, i in enumerate(items):
            for j, chip in enumerate(_other_chips(x, y)):
                cp = _ag_ici_copy(i, j, chip, c, 2 * chip[0] + chip[1], src_refs[t], land_refs[t], send_sems, recv_sems,
                                  3 * t + j)
                cp.wait_send()
                cp.wait_recv()

    ops = [srcs[i] for i in items] + [lands[i] for i in items]
    out = pl.pallas_call(
        body, name=f"allgather_wait_{g}",
        out_shape=tuple(pltpu.HBM(a.shape, a.dtype) for a in ops),
        in_specs=(_HBM,) * (2 * m) + (_SEM, _SEM, pl.BlockSpec(memory_space=pl.ANY)),
        out_specs=(_HBM,) * (2 * m),
        input_output_aliases={i: i for i in range(2 * m)},
        compiler_params=pltpu.CompilerParams(**_SPLIT),
    )(*ops, sems[0], sems[1], after)
    return list(out[:m]), list(out[m:])


def _ag_forward(g, srcs, lands):
    return _ag_sibling(_AG_GROUPS[g], srcs, lands, False, f"allgather_forward_{g}")


def _ag_push_own(srcs, lands):
    items = tuple(sorted(lands))
    out = _ag_sibling(items, [srcs[i] for i in items], [lands[i] for i in items], True, "allgather_push_own")
    return dict(zip(items, out))


def _ag_sibling(items, srcs, lands, own, name):
    m = len(items)
    per = 2 if own else 3

    def body(*refs):
        src_refs, in_refs, out_refs = refs[:m], refs[m:2 * m], refs[2 * m:3 * m]
        send_sems, recv_sems = refs[3 * m:]
        x, y, c = _position()
        sibling = (x, y, 1 - c)
        me = 2 * x + y
        if own:
            mine = theirs = [(me, 0), (me, 1)]
        else:
            slots = [2 * chip[0] + chip[1] for chip in _other_chips(x, y)]
            mine, theirs = [(s, c) for s in slots], [(s, 1 - c) for s in slots]
        sends = []
        for t, i in enumerate(items):
            _, dst, half = _AG_ITEMS[i]
            for k, (slot, hc) in enumerate(mine):
                src = half(src_refs[t], hc) if own else dst(in_refs[t], slot, hc)
                sends.append(pltpu.make_async_remote_copy(
                    src_ref=src, dst_ref=dst(out_refs[t], slot, hc), send_sem=send_sems.at[per * t + k],
                    recv_sem=recv_sems.at[per * t + k], device_id=sibling, device_id_type=MESH))
        for cp in sends:
            cp.start()
        for t, i in enumerate(items):
            dst = _AG_ITEMS[i][1]
            for k, (slot, hc) in enumerate(theirs):
                there = dst(out_refs[t], slot, hc)
                pltpu.make_async_remote_copy(src_ref=there, dst_ref=there, send_sem=send_sems.at[per * t + k],
                                             recv_sem=recv_sems.at[per * t + k], device_id=sibling,
                                             device_id_type=MESH).wait_recv()
        for cp in sends:
            cp.wait_send()

    any_spec = pl.BlockSpec(memory_space=pl.ANY)
    return pl.pallas_call(
        body, name=name,
        in_specs=[any_spec] * (2 * m), out_specs=(any_spec,) * m,
        out_shape=tuple(jax.ShapeDtypeStruct(a.shape, a.dtype) for a in lands),
        input_output_aliases={m + t: t for t in range(m)},
        scratch_shapes=[pltpu.SemaphoreType.DMA((per * m,)), pltpu.SemaphoreType.DMA((per * m,))],
    )(*srcs, *lands)


def _pair_swap_copy(g_ref, r_ref, send_sem, recv_sem):
    x, y, c = _position()
    hc = g_ref.shape[2] // 2
    return pltpu.make_async_remote_copy(src_ref=g_ref.at[:, :, pl.ds(_al(1 - c, hc), hc)], dst_ref=r_ref,
                                        send_sem=send_sem, recv_sem=recv_sem, device_id=(x, y, 1 - c),
                                        device_id_type=MESH)


def _pair_swap_start(gb, tag):
    _, rows, cols = gb.shape
    recv = lax.empty((4, rows, cols // 2), gb.dtype)

    def body(g_ref, r_ref, send_sem, recv_sem, g_thru, r_thru, token):
        _pair_swap_copy(g_ref, r_ref, send_sem, recv_sem).start()
        token[...] = jnp.zeros_like(token)

    return pl.pallas_call(
        body, name="grad_pair_swap_start_" + tag,
        out_shape=(pltpu.SemaphoreType.DMA(()), pltpu.SemaphoreType.DMA(()), pltpu.HBM(gb.shape, gb.dtype),
                   pltpu.HBM(recv.shape, recv.dtype), jax.ShapeDtypeStruct((8, 128), F32)),
        in_specs=(_HBM, _HBM), out_specs=(_SEM, _SEM, _HBM, _HBM, pl.BlockSpec(memory_space=pltpu.VMEM)),
        input_output_aliases={0: 2, 1: 3},
        compiler_params=pltpu.CompilerParams(**_SPLIT),
    )(_hbm(gb), _hbm(recv))


def _pair_swap_wait(started, after, tag):
    send_sem, recv_sem, gb, recv, _ = started

    def body(g_ref, r_ref, send_sem, recv_sem, after_ref, g_out, r_out):
        cp = _pair_swap_copy(g_ref, r_ref, send_sem, recv_sem)
        cp.wait_send()
        cp.wait_recv()

    return pl.pallas_call(
        body, name="grad_pair_swap_wait_" + tag,
        out_shape=(pltpu.HBM(gb.shape, gb.dtype), pltpu.HBM(recv.shape, recv.dtype)),
        in_specs=(_HBM, _HBM, _SEM, _SEM, pl.BlockSpec(memory_space=pl.ANY)), out_specs=(_HBM, _HBM),
        input_output_aliases={0: 0, 1: 1},
        compiler_params=pltpu.CompilerParams(**_SPLIT),
    )(gb, recv, send_sem, recv_sem, after)


def _handover_copy(r_ref, send_sem, recv_sem, core):
    x, y, c = _position()
    hc = r_ref.shape[1] // 2
    cols = r_ref.at[:, pl.ds(_al(core, hc), hc)]
    return pltpu.make_async_remote_copy(src_ref=cols, dst_ref=cols, send_sem=send_sem, recv_sem=recv_sem,
                                        device_id=(x, y, 1 - c), device_id_type=MESH)


def _handover_start(red, tag):
    def body(r_ref, send_sem, recv_sem, r_thru, token):
        _handover_copy(r_ref, send_sem, recv_sem, lax.axis_index("c")).start()
        token[...] = jnp.zeros_like(token)

    return pl.pallas_call(
        body, name="grad_handover_start_" + tag,
        out_shape=(pltpu.SemaphoreType.DMA(()), pltpu.SemaphoreType.DMA(()), pltpu.HBM(red.shape, red.dtype),
                   jax.ShapeDtypeStruct((8, 128), F32)),
        in_specs=(_HBM,), out_specs=(_SEM, _SEM, _HBM, pl.BlockSpec(memory_space=pltpu.VMEM)),
        input_output_aliases={0: 2},
        compiler_params=pltpu.CompilerParams(**_SPLIT),
    )(_hbm(red))


def _handover_wait(started, after, tag):
    send_sem, recv_sem, red, _ = started

    def body(r_ref, send_sem, recv_sem, after_ref, r_out):
        c = lax.axis_index("c")
        _handover_copy(r_ref, send_sem, recv_sem, c).wait_send()
        _handover_copy(r_ref, send_sem, recv_sem, 1 - c).wait_recv()

    return pl.pallas_call(
        body, name="grad_handover_wait_" + tag,
        out_shape=pltpu.HBM(red.shape, red.dtype),
        in_specs=(_HBM, _SEM, _SEM, pl.BlockSpec(memory_space=pl.ANY)), out_specs=_HBM,
        input_output_aliases={0: 0},
        compiler_params=pltpu.CompilerParams(**_SPLIT),
    )(red, send_sem, recv_sem, after)


def _handover(red, tag):
    started = _handover_start(red, tag)
    return _handover_wait(started, started[3], tag)


def _a2a_copy(j, chip, c, p_ref, q_ref, q_slot, send_sems, recv_sems):
    return pltpu.make_async_remote_copy(src_ref=p_ref.at[2 * chip[0] + chip[1]], dst_ref=q_ref.at[q_slot],
                                        send_sem=send_sems.at[j], recv_sem=recv_sems.at[j], device_id=(*chip, c),
                                        device_id_type=MESH)


def _a2a_start(p, tag):
    def body(p_ref, q_ref, send_sems, recv_sems, p_thru, q_thru, token):
        x, y, c = _position()
        for j, chip in enumerate(_other_chips(x, y)):
            _a2a_copy(j, chip, c, p_ref, q_ref, 2 * x + y, send_sems, recv_sems).start()
        token[...] = jnp.zeros_like(token)

    return pl.pallas_call(
        body, name="grad_alltoall_start_" + tag,
        out_shape=(pltpu.SemaphoreType.DMA((3,)), pltpu.SemaphoreType.DMA((3,)), pltpu.HBM(p.shape, p.dtype),
                   pltpu.HBM(p.shape, p.dtype), jax.ShapeDtypeStruct((8, 128), F32)),
        in_specs=(_HBM, _HBM), out_specs=(_SEM, _SEM, _HBM, _HBM, pl.BlockSpec(memory_space=pltpu.VMEM)),
        input_output_aliases={0: 2, 1: 3},
        compiler_params=pltpu.CompilerParams(**_SPLIT),
    )(_hbm(p), _hbm(lax.empty(p.shape, p.dtype)))


def _a2a_wait(send_sems, recv_sems, p, q, after, tag):
    def body(p_ref, q_ref, send_sems, recv_sems, after_ref, p_out, q_out):
        x, y, c = _position()
        for j, chip in enumerate(_other_chips(x, y)):
            cp = _a2a_copy(j, chip, c, p_ref, q_ref, 2 * chip[0] + chip[1], send_sems, recv_sems)
            cp.wait_send()
            cp.wait_recv()

    return pl.pallas_call(
        body, name="grad_alltoall_wait_" + tag,
        out_shape=(pltpu.HBM(p.shape, p.dtype), pltpu.HBM(q.shape, q.dtype)),
        in_specs=(_HBM, _HBM, _SEM, _SEM, pl.BlockSpec(memory_space=pl.ANY)), out_specs=(_HBM, _HBM),
        input_output_aliases={0: 0, 1: 1},
        compiler_params=pltpu.CompilerParams(**_SPLIT),
    )(p, q, send_sems, recv_sems, after)


def _comm_rows(rows):
    return next(t for t in (512, 384, 256, 128) if rows % t == 0)


def _pair_add(gb, recv, where, tag):
    _, rows, cols = gb.shape
    hc = cols // 2
    tr = _comm_rows(rows)

    def body(w_ref, g_ref, r_ref, o_ref):
        for half in range(2):
            @pl.when(w_ref[0] == half)
            def _(half=half):
                mine = g_ref[:, half * hc:(half + 1) * hc]
                o_ref[...] = (mine.astype(F32) + r_ref[...].astype(F32)).astype(o_ref.dtype)

    return pl.pallas_call(
        body, name="grad_pair_add_" + tag,
        grid_spec=pltpu.PrefetchScalarGridSpec(
            num_scalar_prefetch=1, grid=(4, rows // tr),
            in_specs=[pl.BlockSpec((None, tr, cols), lambda s, j, w_ref: (s, j, 0)),
                      pl.BlockSpec((None, tr, hc), lambda s, j, w_ref: (s, j, 0))],
            out_specs=pl.BlockSpec((None, tr, hc), lambda s, j, w_ref: (s, j, 0))),
        out_shape=jax.ShapeDtypeStruct((4, rows, hc), gb.dtype),
        compiler_params=_cp(("parallel", "parallel")),
    )(where, gb, recv)


def _sum_chips(p, q, where, tag):
    _, rows, hc = q.shape
    tr = _comm_rows(rows)

    def body(w_ref, p_ref, qa_ref, qb_ref, qc_ref, o_ref):
        me = w_ref[1]
        own, qa, qb, qc = (r[...].astype(F32) for r in (p_ref, qa_ref, qb_ref, qc_ref))
        v0 = jnp.where(me == 0, own, qa)
        v1 = jnp.where(me == 1, own, jnp.where(me == 0, qa, qb))
        v2 = jnp.where(me == 2, own, jnp.where(me < 2, qb, qc))
        v3 = jnp.where(me == 3, own, qc)
        o_ref[...] = ((v0 + v1) + v2) + v3

    slot = lambda k: pl.BlockSpec((None, tr, hc), lambda j, w_ref: (w_ref[k], j, 0))
    return pl.pallas_call(
        body, name="grad_sum_chips_" + tag,
        grid_spec=pltpu.PrefetchScalarGridSpec(
            num_scalar_prefetch=1, grid=(rows // tr,),
            in_specs=[slot(1), slot(2), slot(3), slot(4)],
            out_specs=pl.BlockSpec((tr, hc), lambda j, w_ref: (j, w_ref[0]))),
        out_shape=jax.ShapeDtypeStruct((rows, 2 * hc), F32),
        compiler_params=_cp(("parallel",)),
    )(where, p, q, q, q)


def _shard_major(g, axis):
    shape = g.shape
    g = g.reshape(shape[:axis] + (4, shape[axis] // 4) + shape[axis + 1:])
    return jnp.moveaxis(g, axis, 0).reshape(4, -1)


def _unshard(g4, shape, axis):
    n = shape[axis] // 4
    g = g4.reshape((4,) + shape[:axis] + (n,) + shape[axis + 1:])
    return jnp.moveaxis(g, 0, axis).reshape(shape)


def _split(flat, shapes):
    out, off = [], 0
    for shp in shapes:
        n = 1
        for d in shp:
            n *= d
        out.append(flat[..., off:off + n].reshape(flat.shape[:-1] + tuple(shp)))
        off += n
    return out


def _even_rows_to_kernel(wt):
    return jnp.concatenate([wt[:1536], wt[1552:3088], wt[1536:1552], wt[3088:3096],
                            jnp.zeros((PE - 3096, wt.shape[1]), wt.dtype)], axis=0)


def _block_diag(w):
    eye = jnp.eye(8, dtype=w.dtype)
    return (w[:, :, None, :] * eye[:, None, :, None]).reshape(512, 512)


def _diag_blocks(g):
    eye = jnp.eye(8, dtype=g.dtype)
    return (g.reshape(8, 64, 8, 64) * eye[:, None, :, None]).sum(axis=2)


def _shift_down(a, s):
    return a if s == 0 else jnp.pad(a, ((s, 0), (0, 0)))[:a.shape[0]]


def _shift_up(a, s):
    return a if s == 0 else jnp.pad(a, ((0, s), (0, 0)))[s:]


SMALL_SHARDED_SHAPES = [(2, 4, 256), (16, 64), (4, 128), (128,), (128,), (128,), (128,)]
REPL_SHAPES = [(256,), (512,), (8,), (8, 257), (8, 64, 64), (8, 64, 64)]


def kernel(x, norm_w, w_in_even, gla_w_a_up, gla_b_a, gla_norm_w, fox_b_f, w_out_even, w_in_odd, rel_bias, conv_w, conv_b, lru_w_a, lru_b_a, lru_w_x, lru_b_x, lru_lambda, w_out_odd, w_mlp_up, w_mlp_down, loss_target, m_norm_w, m_w_in_even, m_gla_w_a_up, m_gla_b_a, m_gla_norm_w, m_fox_b_f, m_w_out_even, m_w_in_odd, m_rel_bias, m_conv_w, m_conv_b, m_lru_w_a, m_lru_b_a, m_lru_w_x, m_lru_b_x, m_lru_lambda, m_w_out_odd, m_w_mlp_up, m_w_mlp_down, v_norm_w, v_w_in_even, v_gla_w_a_up, v_gla_b_a, v_gla_norm_w, v_fox_b_f, v_w_out_even, v_w_in_odd, v_rel_bias, v_conv_w, v_conv_b, v_lru_w_a, v_lru_b_a, v_lru_w_x, v_lru_b_x, v_lru_lambda, v_w_out_odd, v_w_mlp_up, v_w_mlp_down):
    c_idx = lax.axis_index("c")

    small_local = [norm_w, gla_w_a_up[0], conv_w[0], conv_b[0], lru_b_a[0], lru_b_x[0], lru_lambda[0]]
    small_src = jnp.concatenate([a.reshape(-1) for a in small_local]).reshape(32, 128)
    first = {0: small_src, 1: w_in_even[0].T.astype(BF16), 2: w_out_even[0].astype(BF16)}
    sems0, srcs0, lands0, ag_token = _ag_start([0], first, "allgather_start_0")
    zero = ag_token[0, 0]
    later = {3: (w_mlp_up + zero).astype(BF16), 4: (w_mlp_down + zero).astype(BF16),
             5: (w_in_odd[0] + zero).astype(BF16), 6: (w_out_odd[0] + zero).astype(BF16)}
    sems1, srcs1, lands1, ag_token = _ag_start([1, 2], later, "allgather_start_1")
    ag_sems, ag_srcs = {**sems0, **sems1}, {**srcs0, **srcs1}
    ag_lands = _ag_push_own(ag_srcs, {**lands0, **lands1})

    def gathered(g, after):
        srcs_g, lands_g = _ag_wait(g, ag_sems[g], ag_srcs, ag_lands, after)
        return _ag_forward(g, srcs_g, lands_g)

    small4, w_in_e4, w_out_e = gathered(0, ag_token)
    me = 2 * lax.axis_index("x") + lax.axis_index("y")
    others = [k + (k >= me).astype(jnp.int32) for k in range(3)]
    where = jnp.stack([c_idx, me] + others).astype(jnp.int32)

    w_in_e_t = _even_rows_to_kernel(w_in_e4.reshape(3096, D))
    g_small = _split(small4.reshape(4, 32 * 128), SMALL_SHARDED_SHAPES)
    nw_full = _unshard(g_small[0], (2, 4, 1024), 2)
    wa_up = _unshard(g_small[1], (16, 256), 1)
    cw = _unshard(g_small[2], (4, 512), 1)
    cb, lba, lbx, lam = [_unshard(g, (512,), 0).reshape(1, 512) for g in g_small[3:]]
    nw = lambda layer, i: nw_full[layer, i].reshape(1, D)

    wa_pad = jnp.pad(wa_up, ((0, 128 - 16), (0, 0)))
    gla_ba = gla_b_a.reshape(1, 256)
    gla_nw = gla_norm_w.reshape(1, 512)
    fox_bpad = jnp.pad(fox_b_f.reshape(1, 8), ((0, 0), (FOX_LANE0, 128 - FOX_LANE0 - 8)))
    rbp = jnp.pad(rel_bias[0], ((0, 0), (0, REL_PAD - 257)))
    wa_bd = _block_diag(lru_w_a[0])
    wx_bd = _block_diag(lru_w_x[0])

    x0 = x[0]
    tgt = loss_target[0]

    h0 = _prenorm(x0, nw(0, 0), "prenorm_l0_mix")
    proj_e = _mm(h0, w_in_e_t, "nt", tm=2048, tn=640, name="mm_in_even")
    cat0, s_prev = _gla_fwd(proj_e, wa_pad, gla_ba, gla_nw)
    cum_r = _fox_gate_fwd(proj_e, fox_bpad)
    cum_c = cum_r[:, FOX_LANE0:FOX_LANE0 + 8].T
    cat0 = _fox_fwd(proj_e, cum_c, cat0)
    mix0 = _mm(cat0, w_out_e, "nn", tm=2048, tn=512, name="mm_out_even")
    x1, h1 = _post_pre_fwd(x0, mix0, nw(0, 1), nw(0, 2), "post_pre_l0_mix")
    w_up, w_dn = gathered(1, x1)
    a0, r0 = _mm(h1, w_up, "nn", tm=2048, tn=1024, b_layer=0, relu_pair=True, name="mm_up_l0")
    d0 = _mm(a0, w_dn, "nn", tm=1024, tn=512, b_layer=0, name="mm_down_l0")
    x2, h2 = _post_pre_fwd(x1, d0, nw(0, 3), nw(1, 0), "post_pre_l0_mlp")

    w_in_o, w_out_o = gathered(2, x2)
    proj_o = _mm(h2, w_in_o, "nn", tm=2048, tn=640, name="mm_in_odd")
    bias_q = _bias_build(rbp)
    bias = bias_q.transpose(1, 0, 2)
    kvpad = jnp.pad(proj_o[:, 512:1536], ((CA_PAD, 0), (0, 0)))
    cat1 = _ca_fwd(proj_o, kvpad, bias)
    x_in = proj_o[:, 2048:2560]
    xs = jnp.stack([_shift_down(x_in, 3 - j) for j in range(4)])
    lru_a, lru_b = _lru_pre_fwd(xs, cw, cb, wa_bd, lba, wx_bd, lbx, lam)
    hh = _lru_scan_fwd(lru_a, lru_b)
    cat1 = _lru_post_fwd(hh, proj_o, cat1)
    mix1 = _mm(cat1, w_out_o, "nn", tm=2048, tn=512, name="mm_out_odd")
    x3, h3 = _post_pre_fwd(x2, mix1, nw(1, 1), nw(1, 2), "post_pre_l1_mix")
    a1, r1 = _mm(h3, w_up, "nn", tm=2048, tn=1024, b_layer=1, relu_pair=True, name="mm_up_l1")
    d1 = _mm(a1, w_dn, "nn", tm=1024, tn=512, b_layer=1, name="mm_down_l1")
    g4, loss_part, dd1, dnw13 = _post_loss(x3, d1, nw(1, 3), tgt)
    loss = lax.psum(loss_part[0, 0], ("x", "y", "c"))

    def rs_begin(swap, after, tag):
        gb, recv = _pair_swap_wait(swap, after, tag)
        return _a2a_start(_pair_add(gb, recv, where, tag), tag)

    def rs_end(started, after, tag):
        send_sems, recv_sems, p, q, _ = started
        p, q = _a2a_wait(send_sems, recv_sems, p, q, after, tag)
        return _handover(_sum_chips(p, q, where, tag), tag)

    gba = lax.dynamic_update_slice(lax.empty((4, GA_ROWS, D), BF16), jnp.zeros((4, GA_UP - GA_GAP, D), BF16),
                                   (0, GA_GAP, 0))
    gba = _mm(a1, dd1, "tn", tm=512, tn=1024, into=(gba, 1024, GA_DN), name="mm_down_l1_dw")
    du1 = _mm(dd1, w_dn, "nt", tm=2048, tn=1024, b_layer=1, times2=r1, out_dtype=BF16, name="mm_down_l1_dx")
    gba = _mm(du1, h3, "tn", tm=512, tn=1024, into=(gba, 1024, GA_UP), name="mm_up_l1_dw")
    dh3 = _mm(du1, w_up, "nt", tm=1024, tn=512, b_layer=1, name="mm_up_l1_dx")
    g3, dmix1, dnw12, dnw11 = _pre_post_bwd(x3, nw(1, 2), dh3, g4, mix1, nw(1, 1), "pre_post_bwd_l1_mlp")
    gba = _mm(cat1, dmix1, "tn", tm=128, tn=1024, into=(gba, 256, GA_OUT_O), name="mm_out_odd_dw")
    dcat1 = _mm(dmix1, w_out_o, "nt", tm=2048, tn=512, name="mm_out_odd_dx")

    dq_c, dkpad, dvpad, dbias = _ca_bwd(proj_o, kvpad, bias, dcat1)
    g_rel = _bias_grad(jnp.pad(dbias.transpose(1, 0, 2), ((0, 0), (0, 0), (0, BIAS_W - CA_BAND))))[:, :257]
    dhh, dgate = _lru_post_bwd(hh, proj_o, dcat1)
    da_l, db_l = _lru_scan_bwd(_shift_up(lru_a, 1), _shift_down(hh, 1), dhh)
    dxs, g_cw, g_cb, g_wa_bd, g_lba, g_wx_bd, g_lbx, g_lam = _lru_pre_bwd(xs, cw, cb, wa_bd, lba, wx_bd, lbx, lam, da_l, db_l)
    dx_in = _conv_dx(jnp.stack([_shift_up(dxs[j], 3 - j) for j in range(4)]))
    dproj_o = jnp.concatenate([dq_c, dkpad[CA_PAD:], dvpad[CA_PAD:], dgate, dx_in], axis=1).astype(BF16)
    gba = _mm(dproj_o, h2, "tn", tm=128, tn=1024, into=(gba, 640, GA_IN_O), name="mm_in_odd_dw")
    swap_a = _pair_swap_start(gba, "a")
    dh2 = _mm(dproj_o, w_in_o, "nt", tm=1024, tn=512, name="mm_in_odd_dx")
    g2, dd0, dnw10, dnw03 = _pre_post_bwd(x2, nw(1, 0) + swap_a[4][0, 0], dh2, g3, d0, nw(0, 3), "pre_post_bwd_l1_mix")
    rs_a = rs_begin(swap_a, g2, "a")

    gbb = lax.empty((4, GB_ROWS, D), BF16)
    gbb = _mm(a0, dd0, "tn", tm=512, tn=1024, into=(gbb, 1024, GB_DN), name="mm_down_l0_dw")
    du0 = _mm(dd0, w_dn, "nt", tm=2048, tn=1024, b_layer=0, times2=r0, out_dtype=BF16, name="mm_down_l0_dx")
    gbb = _mm(du0, h1, "tn", tm=512, tn=1024, into=(gbb, 1024, GB_UP), name="mm_up_l0_dw")
    swap_b = _pair_swap_start(gbb, "b")
    dh1 = _mm(du0, w_up, "nt", tm=1024, tn=512, b_layer=0, name="mm_up_l0_dx")
    g1, dmix0, dnw02, dnw01 = _pre_post_bwd(x1, nw(0, 2) + (swap_b[4][0, 0] + rs_a[4][0, 0]), dh1, g2, mix0, nw(0, 1),
                                            "pre_post_bwd_l0_mlp")
    rs_b = rs_begin(swap_b, g1, "b")
    gbc = lax.empty((4, GC_ROWS, D), BF16)
    gbc = _mm(cat0, dmix0, "tn", tm=128, tn=1024, into=(gbc, 256, GC_OUT_E), name="mm_out_even_dw")
    dcat0 = _mm(dmix0, w_out_e, "nt", tm=2048, tn=512, name="mm_out_even_dx")

    dq_g, dk_g, dv_g, dr_g, daux_g, g_wa_pad, g_gla_ba, g_gla_nw = _gla_bwd(
        proj_e, s_prev, wa_pad, gla_ba, gla_nw + rs_b[4][0, 0], dcat0)
    dq_f, dk_f, dv_f, dccol = _fox_bwd(proj_e, cum_c, dcat0)
    dccol_t = jnp.pad(dccol.sum(axis=0).T, ((0, 0), (FOX_LANE0, 128 - FOX_LANE0 - 8)))
    daux, g_fox_bpad = _fox_gate_bwd(proj_e, fox_bpad, dccol_t, daux_g)
    dproj_e = jnp.concatenate([dq_g, dk_g, dv_g, dr_g, dq_f, dk_f, dv_f, daux], axis=1).astype(BF16)
    gt_in_e = _mm(dproj_e, h0, "tn", tm=640, tn=1024, out_dtype=BF16, name="mm_in_even_dw")
    dh0 = _mm(dproj_e, w_in_e_t, "nn", tm=1024, tn=512, name="mm_in_even_dx")
    grad_x, dnw00 = _norm_bwd(x0, nw(0, 0), dh0, g1, "prenorm_l0_mix_bwd")

    def rs_reduce(started, after, tag):
        send_sems, recv_sems, p, q, _ = started
        p, q = _a2a_wait(send_sems, recv_sems, p, q, after, tag)
        return _handover_start(_sum_chips(p, q, where, tag), tag)

    ho_a = rs_reduce(rs_a, grad_x, "a")
    ho_b = rs_reduce(rs_b, ho_a[3], "b")

    g_norm = jnp.stack([jnp.concatenate([dnw00, dnw01, dnw02, dnw03]), jnp.concatenate([dnw10, dnw11, dnw12, dnw13])])
    sharded = [(g_norm, 2), (g_wa_pad[:16], 1), (g_cw, 1), (g_cb[0], 0), (g_lba[0], 0), (g_lbx[0], 0), (g_lam[0], 0)]
    replicated = [g_gla_ba[0], g_gla_nw[0], g_fox_bpad[0, FOX_LANE0:FOX_LANE0 + 8], g_rel, _diag_blocks(g_wa_bd),
                  _diag_blocks(g_wx_bd)]
    small4 = jnp.concatenate([_shard_major(g, ax) for g, ax in sharded]
                             + [jnp.broadcast_to(g.reshape(1, -1), (4, g.size)) for g in replicated], axis=1)
    n_small = small4.shape[1]
    small_rows = GC_ROWS - GC_TAIL - 774
    small4 = jnp.pad(small4, ((0, 0), (0, small_rows * D - n_small))).reshape(4, small_rows, D)
    gt_rows = jnp.concatenate([gt_in_e[:1536], gt_in_e[3072:3088], gt_in_e[1536:3072], gt_in_e[3088:3096]], axis=0)
    tail = jnp.concatenate([gt_rows.reshape(4, 774, D), small4.astype(BF16)], axis=1)
    gbc = lax.dynamic_update_slice(gbc, tail, (0, GC_TAIL, 0))
    swap_c = _pair_swap_start(gbc, "c")
    rs_c = rs_begin(swap_c, swap_c[4], "c")

    red_a = _handover_wait(ho_a, rs_c[4], "a")
    red_b = _handover_wait(ho_b, red_a, "b")
    early = dict(
        w_mlp_up=_adamw_from(w_mlp_up, m_w_mlp_up, v_w_mlp_up, [(red_b, GB_UP, True), (red_a, GA_UP, True)], 256,
                             "adamw_w_mlp_up"),
        w_mlp_down=_adamw_from(w_mlp_down, m_w_mlp_down, v_w_mlp_down, [(red_b, GB_DN, False), (red_a, GA_DN, False)],
                               256, "adamw_w_mlp_down"),
        w_in_odd=_adamw_from(w_in_odd, m_w_in_odd, v_w_in_odd, [(red_a, GA_IN_O, True)], 256, "adamw_w_in_odd"),
        w_out_odd=_adamw_from(w_out_odd, m_w_out_odd, v_w_out_odd, [(red_a, GA_OUT_O, False)], 128, "adamw_w_out_odd"))
    red_c = rs_end(rs_c, early["w_out_odd"][3], "c")

    g_small = _split(red_c[GC_TAIL + 774:].reshape(-1)[:n_small], SMALL_SHARDED_SHAPES + REPL_SHAPES)
    g_of = dict(zip(["norm_w", "gla_w_a_up", "conv_w", "conv_b", "lru_b_a", "lru_b_x", "lru_lambda", "gla_b_a",
                     "gla_norm_w", "fox_b_f", "rel_bias", "lru_w_a", "lru_w_x"], g_small))
    g_of.update(w_in_even=red_c[GC_TAIL:GC_TAIL + 774])
    early["w_out_even"] = _adamw_from(w_out_even, m_w_out_even, v_w_out_even, [(red_c, GC_OUT_E, False)], 256,
                                      "adamw_w_out_even")

    names = ["norm_w", "w_in_even", "gla_w_a_up", "gla_b_a", "gla_norm_w", "fox_b_f", "w_out_even", "w_in_odd", "rel_bias",
             "conv_w", "conv_b", "lru_w_a", "lru_b_a", "lru_w_x", "lru_b_x", "lru_lambda", "w_out_odd", "w_mlp_up",
             "w_mlp_down"]
    w_of = dict(norm_w=norm_w, w_in_even=w_in_even, gla_w_a_up=gla_w_a_up, gla_b_a=gla_b_a, gla_norm_w=gla_norm_w,
                fox_b_f=fox_b_f, w_out_even=w_out_even, w_in_odd=w_in_odd, rel_bias=rel_bias, conv_w=conv_w, conv_b=conv_b,
                lru_w_a=lru_w_a, lru_b_a=lru_b_a, lru_w_x=lru_w_x, lru_b_x=lru_b_x, lru_lambda=lru_lambda,
                w_out_odd=w_out_odd, w_mlp_up=w_mlp_up, w_mlp_down=w_mlp_down)
    m_of = dict(norm_w=m_norm_w, w_in_even=m_w_in_even, gla_w_a_up=m_gla_w_a_up, gla_b_a=m_gla_b_a,
                gla_norm_w=m_gla_norm_w, fox_b_f=m_fox_b_f, w_out_even=m_w_out_even, w_in_odd=m_w_in_odd,
                rel_bias=m_rel_bias, conv_w=m_conv_w, conv_b=m_conv_b, lru_w_a=m_lru_w_a, lru_b_a=m_lru_b_a,
                lru_w_x=m_lru_w_x, lru_b_x=m_lru_b_x, lru_lambda=m_lru_lambda, w_out_odd=m_w_out_odd,
                w_mlp_up=m_w_mlp_up, w_mlp_down=m_w_mlp_down)
    v_of = dict(norm_w=v_norm_w, w_in_even=v_w_in_even, gla_w_a_up=v_gla_w_a_up, gla_b_a=v_gla_b_a,
                gla_norm_w=v_gla_norm_w, fox_b_f=v_fox_b_f, w_out_even=v_w_out_even, w_in_odd=v_w_in_odd,
                rel_bias=v_rel_bias, conv_w=v_conv_w, conv_b=v_conv_b, lru_w_a=v_lru_w_a, lru_b_a=v_lru_b_a,
                lru_w_x=v_lru_w_x, lru_b_x=v_lru_b_x, lru_lambda=v_lru_lambda, w_out_odd=v_w_out_odd,
                w_mlp_up=v_w_mlp_up, w_mlp_down=v_w_mlp_down)
    grads, deltas, new_ms, new_vs = [], [], [], []
    for n in names:
        w = w_of[n]
        if n in early:
            g, d, mn, vn = early[n]
            grads.append(g)
            deltas.append(d)
            new_ms.append(mn)
            new_vs.append(vn)
            continue
        if n == "w_in_even":
            to_view = lambda a: a[0].T
            from_view = lambda a: a.T[None]
        else:
            view = w.shape if w.ndim <= 3 else w.shape[-3:]
            to_view = lambda a, view=view: a.reshape(view)
            from_view = lambda a, w=w: a.reshape(w.shape)
        g = g_of[n] if n == "w_in_even" else to_view(g_of[n])
        d, mn, vn = _adamw(to_view(w), g, to_view(m_of[n]), to_view(v_of[n]), "adamw_" + n)
        grads.append(from_view(g))
        deltas.append(from_view(d))
        new_ms.append(from_view(mn))
        new_vs.append(from_view(vn))

    return (loss, grad_x.reshape(1, T, D), *grads, *deltas, *new_ms, *new_vs)
```

```python
import functools

import jax
import jax.numpy as jnp
from jax import lax
from jax.experimental import pallas as pl
from jax.experimental.pallas import tpu as pltpu

F32 = jnp.float32
BF16 = jnp.bfloat16
MESH = pl.DeviceIdType.MESH

T = 2048
D = 1024
DFF = 4096
EPS = 1e-6
CHUNK = 64
NCHUNK = T // CHUNK
PE = 3200
PO = 2560
AUX_BLK = 3072 // 128
FOX_LANE0 = 16
GLA_SCALE = 64 ** -0.5
ATT_SCALE = 64 ** -0.5
NEG = float(jnp.finfo(jnp.float32).min)
CA_BAND = 576
CA_PAD = 512
REL_PAD = 384

VMEM_LIMIT = 48 * 1024 * 1024

ADAM_LR, ADAM_B1, ADAM_B2, ADAM_EPS, ADAM_WD, ADAM_STEP = 0.001, 0.9, 0.999, 1e-08, 0.01, 10

GA_ROWS, GA_IN_O, GA_OUT_O, GA_GAP, GA_UP, GA_DN = 3072, 0, 640, 896, 1024, 2048
GB_ROWS, GB_UP, GB_DN = 2048, 0, 1024
GC_ROWS, GC_OUT_E, GC_TAIL = 1152, 0, 256

_DIMS = {"nn": (((1,), (0,)), ((), ())), "nt": (((1,), (1,)), ((), ())), "tn": (((0,), (0,)), ((), ()))}


def _cp(sem, **kw):
    return pltpu.CompilerParams(dimension_semantics=sem, vmem_limit_bytes=VMEM_LIMIT, **kw)


def _dot(a, b, mode):
    return lax.dot_general(a.astype(BF16), b.astype(BF16), _DIMS[mode], preferred_element_type=F32)


@functools.partial(jax.custom_vjp, nondiff_argnums=(2,))
def bdot(a, b, mode):
    return _dot(a, b, mode)


def _bdot_fwd(a, b, mode):
    return _dot(a, b, mode), (a, b)


def _bdot_bwd(mode, res, g):
    a, b = res
    if mode == "nn":
        da, db = _dot(g, b, "nt"), _dot(a, g, "tn")
    elif mode == "nt":
        da, db = _dot(g, b, "nn"), _dot(g, a, "tn")
    else:
        da, db = _dot(b, g, "nt"), _dot(a, g, "nn")
    return da.astype(a.dtype), db.astype(b.dtype)


bdot.defvjp(_bdot_fwd, _bdot_bwd)


def _hdot_raw(a, b, mode):
    return lax.dot_general(a, b, _DIMS[mode], precision=lax.Precision.HIGHEST, preferred_element_type=F32)


def _log_sigmoid(x):
    return jnp.minimum(x, 0.0) - jnp.log(1.0 + jnp.exp(-jnp.abs(x)))


def _sigmoid(x):
    return 1.0 / (1.0 + jnp.exp(-x))


def _expm1(x):
    series = x * (1.0 + x * 0.5 * (1.0 + x * (1.0 / 3.0) * (1.0 + x * 0.25)))
    return jnp.where(jnp.abs(x) < 0.03, series, jnp.exp(x) - 1.0)


def _gelu_tanh(x):
    return 0.5 * x * (1.0 + jnp.tanh(0.7978845608028654 * (x + 0.044715 * x * x * x)))


def _iota(shape, dim):
    return lax.broadcasted_iota(jnp.int32, shape, dim)


def _mm(a, b, mode, *, tm, tn, tk=None, out_dtype=F32, name, b_layer=None, into=None, relu_pair=False, times2=None):
    b2 = b.shape[-2:]
    if mode == "nn":
        (m, k), n = a.shape, b2[1]
    elif mode == "nt":
        (m, k), n = a.shape, b2[0]
    else:
        (k, m), n = a.shape, b2[1]
    tk = k if tk is None else tk
    assert m % tm == 0 and n % tn == 0 and k % tk == 0, (name, a.shape, b.shape)
    nk = k // tk
    if mode == "tn":
        a_spec = pl.BlockSpec((tk, tm), lambda i, j, kk: (kk, i))
    elif m == tm and nk == 1:
        a_spec = pl.BlockSpec((tm, tk), lambda i, j, kk: (i, kk), pipeline_mode=pl.Buffered(1))
    else:
        a_spec = pl.BlockSpec((tm, tk), lambda i, j, kk: (i, kk))
    b_blk = {"nn": (tk, tn), "nt": (tn, tk), "tn": (tk, tn)}[mode]
    b_idx = {"nn": lambda i, j, kk: (kk, j), "nt": lambda i, j, kk: (j, kk), "tn": lambda i, j, kk: (kk, j)}[mode]
    if b_layer is None:
        b_spec = pl.BlockSpec(b_blk, b_idx)
    else:
        b_spec = pl.BlockSpec((None,) + b_blk, lambda i, j, kk: (b_layer,) + b_idx(i, j, kk))

    tile = pl.BlockSpec((tm, tn), lambda i, j, kk: (i, j))
    if into is not None:
        buf, per_slot, row_off = into
        assert m == 4 * per_slot and per_slot % tm == 0 and row_off % tm == 0 and buf.shape[2] == n, (name, buf.shape)
        bps = per_slot // tm
        out_specs = pl.BlockSpec((None, tm, tn), lambda i, j, kk: (i // bps, row_off // tm + i % bps, j))
        out_shape = jax.ShapeDtypeStruct(buf.shape, buf.dtype)
        extra_in, extra_specs, aliases = [buf], [pl.BlockSpec(memory_space=pl.ANY)], {2: 0}
        finish = lambda acc, extra: [acc.astype(buf.dtype)]
    elif relu_pair:
        out_specs = (tile, tile)
        out_shape = (jax.ShapeDtypeStruct((m, n), BF16),) * 2
        extra_in, extra_specs, aliases = [], [], {}

        def finish(acc, extra):
            r = jnp.maximum(acc, 0.0)
            return [(r * r).astype(BF16), r.astype(BF16)]
    elif times2 is not None:
        out_specs = tile
        out_shape = jax.ShapeDtypeStruct((m, n), out_dtype)
        extra_in, extra_specs, aliases = [times2], [tile], {}
        finish = lambda acc, extra: [(acc * (2.0 * extra[...].astype(F32))).astype(out_dtype)]
    else:
        out_specs = tile
        out_shape = jax.ShapeDtypeStruct((m, n), out_dtype)
        extra_in, extra_specs, aliases = [], [], {}
        finish = lambda acc, extra: [acc.astype(out_dtype)]
    n_out = 2 if relu_pair else 1

    def body(*refs):
        a_ref, b_ref = refs[0], refs[1]
        extra = refs[2] if extra_in else None
        o_refs = refs[2 + len(extra_in):2 + len(extra_in) + n_out]

        def store(acc):
            for o_ref, val in zip(o_refs, finish(acc, extra)):
                o_ref[...] = val

        if nk == 1:
            store(_dot(a_ref[...], b_ref[...], mode))
            return
        acc_ref = refs[-1]
        kk = pl.program_id(2)

        @pl.when(kk == 0)
        def _():
            acc_ref[...] = jnp.zeros_like(acc_ref)

        acc_ref[...] += _dot(a_ref[...], b_ref[...], mode)

        @pl.when(kk == nk - 1)
        def _():
            store(acc_ref[...])

    return pl.pallas_call(
        body, name=name, grid=(m // tm, n // tn, nk),
        in_specs=[a_spec, b_spec] + extra_specs,
        out_specs=out_specs, out_shape=out_shape,
        scratch_shapes=[pltpu.VMEM((tm, tn), F32)] if nk > 1 else [],
        input_output_aliases=aliases,
        compiler_params=_cp(("parallel", "parallel", "arbitrary")),
    )(a, b, *extra_in)


ROWS = 512


def _prenorm(x, w, name):
    def body(x_ref, w_ref, o_ref):
        xv = x_ref[...]
        r = lax.rsqrt(jnp.mean(xv * xv, axis=-1, keepdims=True) + EPS)
        o_ref[...] = (xv * r * w_ref[...]).astype(BF16)

    return pl.pallas_call(
        body, name=name, grid=(T // ROWS,),
        in_specs=[pl.BlockSpec((ROWS, D), lambda i: (i, 0)), pl.BlockSpec((1, D), lambda i: (0, 0))],
        out_specs=pl.BlockSpec((ROWS, D), lambda i: (i, 0)),
        out_shape=jax.ShapeDtypeStruct((T, D), BF16),
        compiler_params=_cp(("parallel",)),
    )(x, w)


def _rms(z):
    return lax.rsqrt(jnp.mean(z * z, axis=-1, keepdims=True) + EPS)


def _rms_bwd(z, w, dy):
    r = _rms(z)
    wdy = dy * w
    dz = r * wdy - z * (r * r * r) * jnp.mean(z * wdy, axis=-1, keepdims=True)
    return dz, jnp.sum(dy * z * r, axis=0, keepdims=True)


_ROW = pl.BlockSpec((ROWS, D), lambda i: (i, 0))
_VEC = pl.BlockSpec((1, D), lambda i: (0, 0))


def _post_pre_fwd(x, z, w_post, w_pre, name):
    def body(x_ref, z_ref, wp_ref, wn_ref, x_out, h_out):
        zv = z_ref[...]
        xn = x_ref[...] + zv * _rms(zv) * wp_ref[...]
        x_out[...] = xn
        h_out[...] = (xn * _rms(xn) * wn_ref[...]).astype(BF16)

    return pl.pallas_call(
        body, name=name, grid=(T // ROWS,), in_specs=[_ROW, _ROW, _VEC, _VEC], out_specs=(_ROW, _ROW),
        out_shape=(jax.ShapeDtypeStruct((T, D), F32), jax.ShapeDtypeStruct((T, D), BF16)),
        compiler_params=_cp(("parallel",)),
    )(x, z, w_post, w_pre)


def _post_loss(x, z, w_post, tgt):
    def body(x_ref, z_ref, w_ref, t_ref, g_ref, l_ref, dz_ref, dw_ref):
        @pl.when(pl.program_id(0) == 0)
        def _():
            l_ref[...] = jnp.zeros_like(l_ref)
            dw_ref[...] = jnp.zeros_like(dw_ref)

        zv = z_ref[...]
        e = x_ref[...] + zv * _rms(zv) * w_ref[...] - t_ref[...]
        g = e * (1.0 / D)
        g_ref[...] = g
        l_ref[...] += jnp.sum(e * e) * (0.5 / D)
        dz, dw = _rms_bwd(zv, w_ref[...], g)
        dz_ref[...] = dz.astype(BF16)
        dw_ref[...] += dw

    return pl.pallas_call(
        body, name="postnorm_loss", grid=(T // ROWS,), in_specs=[_ROW, _ROW, _VEC, _ROW],
        out_specs=(_ROW, pl.BlockSpec((1, 128), lambda i: (0, 0)), _ROW, _VEC),
        out_shape=(jax.ShapeDtypeStruct((T, D), F32), jax.ShapeDtypeStruct((1, 128), F32),
                   jax.ShapeDtypeStruct((T, D), BF16), jax.ShapeDtypeStruct((1, D), F32)),
        compiler_params=_cp(("arbitrary",)),
    )(x, z, w_post, tgt)


def _pre_post_bwd(x, w_pre, dh, add, z, w_post, name):
    def body(x_ref, wn_ref, dh_ref, add_ref, z_ref, wp_ref, g_ref, dz_ref, dwn_ref, dwp_ref):
        @pl.when(pl.program_id(0) == 0)
        def _():
            dwn_ref[...] = jnp.zeros_like(dwn_ref)
            dwp_ref[...] = jnp.zeros_like(dwp_ref)

        dx, dwn = _rms_bwd(x_ref[...], wn_ref[...], dh_ref[...])
        g = dx + add_ref[...]
        g_ref[...] = g
        dz, dwp = _rms_bwd(z_ref[...], wp_ref[...], g)
        dz_ref[...] = dz.astype(BF16)
        dwn_ref[...] += dwn
        dwp_ref[...] += dwp

    return pl.pallas_call(
        body, name=name, grid=(T // ROWS,), in_specs=[_ROW, _VEC, _ROW, _ROW, _ROW, _VEC],
        out_specs=(_ROW, _ROW, _VEC, _VEC),
        out_shape=(jax.ShapeDtypeStruct((T, D), F32), jax.ShapeDtypeStruct((T, D), BF16),
                   jax.ShapeDtypeStruct((1, D), F32), jax.ShapeDtypeStruct((1, D), F32)),
        compiler_params=_cp(("arbitrary",)),
    )(x, w_pre, dh, add, z, w_post)


def _norm_bwd(z, w, dy, add, name):
    has_add = add is not None

    def body(*refs):
        if has_add:
            z_ref, w_ref, dy_ref, add_ref, dz_ref, dw_ref = refs
        else:
            z_ref, w_ref, dy_ref, dz_ref, dw_ref = refs
        i = pl.program_id(0)

        @pl.when(i == 0)
        def _():
            dw_ref[...] = jnp.zeros_like(dw_ref)

        zv = z_ref[...].astype(F32)
        dyv = dy_ref[...]
        r = lax.rsqrt(jnp.mean(zv * zv, axis=-1, keepdims=True) + EPS)
        wdy = dyv * w_ref[...]
        dz = r * wdy - zv * (r * r * r) * jnp.mean(zv * wdy, axis=-1, keepdims=True)
        if has_add:
            dz = dz + add_ref[...]
        dz_ref[...] = dz.astype(dz_ref.dtype)
        dw_ref[...] += jnp.sum(dyv * zv * r, axis=0, keepdims=True)

    row = pl.BlockSpec((ROWS, D), lambda i: (i, 0))
    vec = pl.BlockSpec((1, D), lambda i: (0, 0))
    ins = [z, w, dy] + ([add] if has_add else [])
    dz_dtype = F32 if has_add else BF16
    return pl.pallas_call(
        body, name=name, grid=(T // ROWS,),
        in_specs=[row, vec, row] + ([row] if has_add else []),
        out_specs=(row, vec),
        out_shape=(jax.ShapeDtypeStruct((T, D), dz_dtype), jax.ShapeDtypeStruct((1, D), F32)),
        compiler_params=_cp(("arbitrary",)),
    )(*ins)


def _adamw_math(w, g, m, v):
    c1 = 1.0 - ADAM_B1 ** ADAM_STEP
    c2 = 1.0 - ADAM_B2 ** ADAM_STEP
    mn = ADAM_B1 * m + (1.0 - ADAM_B1) * g
    vn = ADAM_B2 * v + (1.0 - ADAM_B2) * (g * g)
    return -ADAM_LR * ((mn / c1) / (jnp.sqrt(vn / c2) + ADAM_EPS) + ADAM_WD * w), mn, vn


def _adamw_from(w, m, v, sources, tr, name):
    layers, rows, cols = w.shape
    assert len(sources) == layers and rows % tr == 0, (name, w.shape)
    g_specs = []
    for layer, (buf, row0, transposed) in enumerate(sources):
        step = lambda l, i, layer=layer: jnp.where(l == layer, i, 0)
        if transposed:
            assert row0 % cols == 0 and buf.shape[1] == rows, (name, row0)
            g_specs.append(pl.BlockSpec((cols, tr), lambda l, i, b=row0 // cols, step=step: (b, step(l, i))))
        else:
            assert row0 % tr == 0 and buf.shape[1] == cols, (name, row0)
            g_specs.append(pl.BlockSpec((tr, cols), lambda l, i, b=row0 // tr, step=step: (b + step(l, i), 0)))

    def body(*refs):
        w_ref, m_ref, v_ref = refs[:3]
        g_refs = refs[3:3 + layers]
        g_out, d_ref, mo_ref, vo_ref = refs[3 + layers:]
        gs = [r[...].T if src[2] else r[...] for r, src in zip(g_refs, sources)]
        g = gs[0] if layers == 1 else jnp.where(pl.program_id(0) == 0, gs[0], gs[1])
        g_out[...] = g
        d_ref[...], mo_ref[...], vo_ref[...] = _adamw_math(w_ref[...], g, m_ref[...], v_ref[...])

    blk = pl.BlockSpec((None, tr, cols), lambda l, i: (l, i, 0))
    sds = jax.ShapeDtypeStruct(w.shape, F32)
    return pl.pallas_call(body, name=name, grid=(layers, rows // tr), in_specs=[blk] * 3 + g_specs,
                          out_specs=(blk,) * 4, out_shape=(sds,) * 4,
                          compiler_params=_cp(("parallel", "parallel")))(w, m, v, *[s[0] for s in sources])


def _adamw(w, g, m, v, name):
    lead = w.shape[:-2]
    assert len(lead) <= 1 and g.shape == w.shape, (name, w.shape, g.shape)
    rows, cols = w.shape[-2:]
    if rows <= 512:
        tr, tc = rows, cols
    elif rows % 256 == 0:
        tr, tc = 256, cols
    else:
        tr, tc = rows, 256
    assert rows % tr == 0 and cols % tc == 0, (name, w.shape)
    c1 = 1.0 - ADAM_B1 ** ADAM_STEP
    c2 = 1.0 - ADAM_B2 ** ADAM_STEP

    def body(w_ref, g_ref, m_ref, v_ref, d_ref, mo_ref, vo_ref):
        gv = g_ref[...]
        mn = ADAM_B1 * m_ref[...] + (1.0 - ADAM_B1) * gv
        vn = ADAM_B2 * v_ref[...] + (1.0 - ADAM_B2) * (gv * gv)
        m_hat = mn / c1
        v_hat = vn / c2
        d_ref[...] = -ADAM_LR * (m_hat / (jnp.sqrt(v_hat) + ADAM_EPS) + ADAM_WD * w_ref[...])
        mo_ref[...] = mn
        vo_ref[...] = vn

    if lead:
        grid = (lead[0], rows // tr, cols // tc)
        blk = pl.BlockSpec((None, tr, tc), lambda l, i, j: (l, i, j))
    else:
        grid = (rows // tr, cols // tc)
        blk = pl.BlockSpec((tr, tc), lambda i, j: (i, j))
    sds = jax.ShapeDtypeStruct(w.shape, F32)
    return pl.pallas_call(body, name=name, grid=grid, in_specs=[blk] * 4, out_specs=(blk,) * 3,
                          out_shape=(sds,) * 3, compiler_params=_cp(("parallel",) * len(grid)))(w, g, m, v)


def _running_sum(x, towards_later):
    n = x.shape[0]
    row = _iota(x.shape, 0)
    s = 1
    while s < n:
        if towards_later:
            x = x + jnp.where(row >= s, pltpu.roll(x, s, 0), 0.0)
        else:
            x = x + jnp.where(row < n - s, pltpu.roll(x, n - s, 0), 0.0)
        s *= 2
    return x


@jax.custom_vjp
def _cumsum_rows(x):
    return _running_sum(x, True)


_cumsum_rows.defvjp(lambda x: (_running_sum(x, True), None), lambda _, g: (_running_sum(g, False),))


def _gla_consts():
    return (_iota((256, 512), 0) // 64 == _iota((256, 512), 1) // 128).astype(F32)


def _gla_chunk(mask, q, k, v, r, aux, s_prev, wa, ba, nw):
    la = _log_sigmoid(bdot(aux, wa, "nn") + ba) * (1.0 / 16.0)
    cum = _cumsum_rows(la)
    total = jnp.sum(la, axis=0, keepdims=True)
    k_dec = k * jnp.exp(total - cum)
    inc = bdot(k_dec, v, "tn") * mask
    dec = jnp.exp(jnp.broadcast_to(total, (128, 256)).T)
    dec = jnp.concatenate([dec, dec, dec, dec], axis=1)
    s_new = dec * s_prev + inc
    o = bdot(q * GLA_SCALE, s_new, "nn")
    parts = []
    for h in range(4):
        oh = o[:, h * 128:(h + 1) * 128]
        parts.append(oh * lax.rsqrt(jnp.mean(oh * oh, axis=-1, keepdims=True) + EPS))
    on = jnp.concatenate(parts, axis=1)
    return s_new, on * nw * (r * _sigmoid(r))


GLA_PER_STEP = 4
GLA_ROWS = GLA_PER_STEP * CHUNK
GLA_STEPS = NCHUNK // GLA_PER_STEP


def _gla_specs(cmap):
    return [pl.BlockSpec((GLA_ROWS, 256), lambda c: (cmap(c), 0)),
            pl.BlockSpec((GLA_ROWS, 256), lambda c: (cmap(c), 1)),
            pl.BlockSpec((GLA_ROWS, 512), lambda c: (cmap(c), 1)),
            pl.BlockSpec((GLA_ROWS, 512), lambda c: (cmap(c), 2)),
            pl.BlockSpec((GLA_ROWS, 128), lambda c: (cmap(c), AUX_BLK))]


def _gla_fwd(proj, wa, ba, nw):
    def body(q_ref, k_ref, v_ref, r_ref, aux_ref, wa_ref, ba_ref, nw_ref, o_ref, sp_ref, s_ref):
        @pl.when(pl.program_id(0) == 0)
        def _():
            s_ref[...] = jnp.zeros_like(s_ref)

        s = s_ref[...]
        consts = _gla_consts()
        outs, states = [], []
        for i in range(GLA_PER_STEP):
            rows = slice(i * CHUNK, (i + 1) * CHUNK)
            states.append(s)
            s, out = _gla_chunk(consts, q_ref[rows, :], k_ref[rows, :], v_ref[rows, :], r_ref[rows, :], aux_ref[rows, :],
                                s, wa_ref[...], ba_ref[...], nw_ref[...])
            outs.append(out)
        s_ref[...] = s
        for i in range(GLA_PER_STEP):
            o_ref[i * CHUNK:(i + 1) * CHUNK, :] = outs[i]
            sp_ref[i] = states[i]

    full = lambda shape: pl.BlockSpec(shape, lambda c: (0,) * len(shape))
    return pl.pallas_call(
        body, name="gla_fwd", grid=(GLA_STEPS,),
        in_specs=_gla_specs(lambda c: c) + [full((128, 256)), full((1, 256)), full((1, 512))],
        out_specs=(pl.BlockSpec((GLA_ROWS, 512), lambda c: (c, 0)),
                   pl.BlockSpec((GLA_PER_STEP, 256, 512), lambda c: (c, 0, 0))),
        out_shape=(jax.ShapeDtypeStruct((T, D), F32), jax.ShapeDtypeStruct((NCHUNK, 256, 512), F32)),
        scratch_shapes=[pltpu.VMEM((256, 512), F32)],
        compiler_params=_cp(("arbitrary",)),
    )(proj, proj, proj, proj, proj, wa, ba, nw)


def _gla_bwd(proj, s_prev_all, wa, ba, nw, dcat):
    rev = lambda c: GLA_STEPS - 1 - c

    def body(q_ref, k_ref, v_ref, r_ref, aux_ref, sp_ref, wa_ref, ba_ref, nw_ref, do_ref,
             dq_ref, dk_ref, dv_ref, dr_ref, daux_ref, dwa_ref, dba_ref, dnw_ref, ds_ref):
        @pl.when(pl.program_id(0) == 0)
        def _():
            ds_ref[...] = jnp.zeros_like(ds_ref)
            dwa_ref[...] = jnp.zeros_like(dwa_ref)
            dba_ref[...] = jnp.zeros_like(dba_ref)
            dnw_ref[...] = jnp.zeros_like(dnw_ref)

        fn = functools.partial(_gla_chunk, _gla_consts())
        ds = ds_ref[...]
        dwa, dba, dnw = dwa_ref[...], dba_ref[...], dnw_ref[...]
        grads = {}
        for i in reversed(range(GLA_PER_STEP)):
            rows = slice(i * CHUNK, (i + 1) * CHUNK)
            _, vjp = jax.vjp(fn, q_ref[rows, :], k_ref[rows, :], v_ref[rows, :], r_ref[rows, :], aux_ref[rows, :],
                             sp_ref[i], wa_ref[...], ba_ref[...], nw_ref[...])
            *grads[i], ds, dwa_i, dba_i, dnw_i = vjp((ds, do_ref[rows, :]))
            dwa, dba, dnw = dwa + dwa_i, dba + dba_i, dnw + dnw_i
        ds_ref[...] = ds
        dwa_ref[...] = dwa
        dba_ref[...] = dba
        dnw_ref[...] = dnw
        for i in range(GLA_PER_STEP):
            rows = slice(i * CHUNK, (i + 1) * CHUNK)
            for ref, g in zip((dq_ref, dk_ref, dv_ref, dr_ref, daux_ref), grads[i]):
                ref[rows, :] = g

    full = lambda shape: pl.BlockSpec(shape, lambda c: (0,) * len(shape))
    blk = lambda w: pl.BlockSpec((GLA_ROWS, w), lambda c: (rev(c), 0))
    sds = lambda *s: jax.ShapeDtypeStruct(s, F32)
    return pl.pallas_call(
        body, name="gla_bwd", grid=(GLA_STEPS,),
        in_specs=_gla_specs(rev) + [pl.BlockSpec((GLA_PER_STEP, 256, 512), lambda c: (rev(c), 0, 0)),
                                    full((128, 256)), full((1, 256)), full((1, 512)), blk(512)],
        out_specs=(blk(256), blk(256), blk(512), blk(512), blk(128), full((128, 256)), full((1, 256)), full((1, 512))),
        out_shape=(sds(T, 256), sds(T, 256), sds(T, 512), sds(T, 512), sds(T, 128),
                   sds(128, 256), sds(1, 256), sds(1, 512)),
        scratch_shapes=[pltpu.VMEM((256, 512), F32)],
        compiler_params=_cp(("arbitrary",)),
    )(proj, proj, proj, proj, proj, s_prev_all, wa, ba, nw, dcat)


def _prefix8(x, towards_later):
    row = _iota(x.shape, 0)
    for s in (1, 2, 4):
        if towards_later:
            keep, shift = row >= s, s
        else:
            keep, shift = row < 8 - s, 8 - s
        x = x + jnp.where(keep, pltpu.roll(x, shift, 0), 0.0)
    return x


def _fox_gate_fwd(proj, bpad):
    def body(aux_ref, b_ref, cum_ref):
        cum_ref[...] = _log_sigmoid(aux_ref[...] + b_ref[...])

        def step(i, carry):
            rows = pl.ds(pl.multiple_of(i * 8, 8), 8)
            cum = _prefix8(cum_ref[rows, :], True) + carry
            cum_ref[rows, :] = cum
            return jnp.broadcast_to(cum[7:, :], (8, 128))

        lax.fori_loop(0, T // 8, step, jnp.zeros((8, 128), F32), unroll=4)

    return pl.pallas_call(
        body, name="fox_gate_fwd", grid=(1,),
        in_specs=[pl.BlockSpec((T, 128), lambda i: (0, AUX_BLK)), pl.BlockSpec((1, 128), lambda i: (0, 0))],
        out_specs=pl.BlockSpec((T, 128), lambda i: (0, 0)),
        out_shape=jax.ShapeDtypeStruct((T, 128), F32),
        compiler_params=_cp(("arbitrary",)),
    )(proj, bpad)


def _fox_gate_bwd(proj, bpad, dccol_t, daux_gla):
    def body(aux_ref, b_ref, dc_ref, dg_ref, daux_ref, db_ref):
        def step(i, carry):
            rows = pl.ds(pl.multiple_of(T - 8 * (i + 1), 8), 8)
            dlf = _prefix8(dc_ref[rows, :], False) + carry
            daux_ref[rows, :] = dlf
            return jnp.broadcast_to(dlf[:1, :], (8, 128))

        lax.fori_loop(0, T // 8, step, jnp.zeros((8, 128), F32), unroll=4)
        dz = daux_ref[...] * _sigmoid(-(aux_ref[...] + b_ref[...]))
        daux_ref[...] = dz + dg_ref[...]
        db_ref[...] = jnp.sum(dz, axis=0, keepdims=True)

    whole = pl.BlockSpec((T, 128), lambda i: (0, 0))
    vec = pl.BlockSpec((1, 128), lambda i: (0, 0))
    return pl.pallas_call(
        body, name="fox_gate_bwd", grid=(1,),
        in_specs=[pl.BlockSpec((T, 128), lambda i: (0, AUX_BLK)), vec, whole, whole],
        out_specs=(whole, vec),
        out_shape=(jax.ShapeDtypeStruct((T, 128), F32), jax.ShapeDtypeStruct((1, 128), F32)),
        compiler_params=_cp(("arbitrary",)),
    )(proj, bpad, dccol_t, daux_gla)


FOX_Q = 256


FOX_QB = T // FOX_Q


@jax.custom_vjp
def _attend(s, v):
    return _attend_fwd(s, v)[0]


def _attend_fwd(s, v):
    e = jnp.exp(s - jnp.max(s, axis=-1, keepdims=True))
    r = 1.0 / jnp.sum(e, axis=-1, keepdims=True)
    return _dot(e, v, "nn") * r, (e, r, v)


def _attend_bwd(res, do):
    e, r, v = res
    do_r = do * r
    dpr = _dot(do_r, v, "nt")
    ds = e * (dpr - r * jnp.sum(e * dpr, axis=-1, keepdims=True))
    return ds, _dot(e, do_r, "tn").astype(v.dtype)


_attend.defvjp(_attend_fwd, _attend_bwd)


def _fox_block(hp, q, k, v, ccol):
    kl = k.shape[0]
    lane = _iota((FOX_Q, 128), 1)
    tri = jnp.bitwise_and(_iota((2 * FOX_Q, FOX_Q), 0), FOX_Q - 1) >= _iota((2 * FOX_Q, FOX_Q), 1)
    sub = _iota((8, kl), 0)
    qs = q * ATT_SCALE
    q2 = jnp.concatenate([jnp.where(lane < 64, qs, 0.0), jnp.where(lane >= 64, qs, 0.0)], axis=0)
    s = bdot(q2, k, "nt")
    cs = [jnp.sum(jnp.where(sub == 2 * hp + e, ccol, 0.0), axis=0, keepdims=True) for e in range(2)]
    s = jnp.concatenate([s[:FOX_Q] - cs[0], s[FOX_Q:] - cs[1]], axis=0)
    diag = jnp.where(tri, s[:, kl - FOX_Q:], NEG)
    s = diag if kl == FOX_Q else jnp.concatenate([s[:, :kl - FOX_Q], diag], axis=1)
    o2 = _attend(s, v)
    return jnp.where(lane < 64, o2[:FOX_Q], o2[FOX_Q:])


def _fox_in_specs():
    return [pl.BlockSpec((FOX_Q, 128), lambda hp, qb: (qb, 12 + hp)),
            pl.BlockSpec((T, 128), lambda hp, qb: (0, 16 + hp)),
            pl.BlockSpec((T, 128), lambda hp, qb: (0, 20 + hp)),
            pl.BlockSpec((8, T), lambda hp, qb: (0, 0))]


def _fox_fwd(proj, cum_c, cat):
    def body(q_ref, k_ref, v_ref, cc_ref, cat_ref, o_ref):
        qb = pl.program_id(1)
        for g in range(FOX_QB):
            kl = FOX_Q * (g + 1)

            @pl.when(qb == g)
            def _(kl=kl):
                o_ref[...] = _fox_block(pl.program_id(0), q_ref[...], k_ref[0:kl, :], v_ref[0:kl, :], cc_ref[:, 0:kl])

    return pl.pallas_call(
        body, name="fox_fwd", grid=(4, FOX_QB), in_specs=_fox_in_specs() + [pl.BlockSpec(memory_space=pl.ANY)],
        out_specs=pl.BlockSpec((FOX_Q, 128), lambda hp, qb: (qb, 4 + hp)),
        out_shape=jax.ShapeDtypeStruct((T, D), F32), input_output_aliases={4: 0},
        compiler_params=_cp(("parallel", "parallel")),
    )(proj, proj, proj, cum_c, cat)


def _fox_bwd(proj, cum_c, dcat):
    def body(q_ref, k_ref, v_ref, cc_ref, do_ref, dq_ref, dk_ref, dv_ref, dcc_ref):
        qb = pl.program_id(1)

        @pl.when(qb == 0)
        def _():
            dk_ref[...] = jnp.zeros_like(dk_ref)
            dv_ref[...] = jnp.zeros_like(dv_ref)
            dcc_ref[...] = jnp.zeros_like(dcc_ref)

        fn = functools.partial(_fox_block, pl.program_id(0))
        for g in range(FOX_QB):
            kl = FOX_Q * (g + 1)

            @pl.when(qb == g)
            def _(kl=kl):
                _, vjp = jax.vjp(fn, q_ref[...], k_ref[0:kl, :], v_ref[0:kl, :], cc_ref[:, 0:kl])
                dq, dk, dv, dcc = vjp(do_ref[...])
                dq_ref[...] = dq
                dk_ref[0:kl, :] += dk
                dv_ref[0:kl, :] += dv
                dcc_ref[:, 0:kl] += dcc

    sds = lambda *s: jax.ShapeDtypeStruct(s, F32)
    return pl.pallas_call(
        body, name="fox_bwd", grid=(4, FOX_QB),
        in_specs=_fox_in_specs() + [pl.BlockSpec((FOX_Q, 128), lambda hp, qb: (qb, 4 + hp))],
        out_specs=(pl.BlockSpec((FOX_Q, 128), lambda hp, qb: (qb, hp)),
                   pl.BlockSpec((T, 128), lambda hp, qb: (0, hp)),
                   pl.BlockSpec((T, 128), lambda hp, qb: (0, hp)),
                   pl.BlockSpec((None, 8, T), lambda hp, qb: (hp, 0, 0))),
        out_shape=(sds(T, 512), sds(T, 512), sds(T, 512), sds(4, 8, T)),
        compiler_params=_cp(("parallel", "arbitrary")),
    )(proj, proj, proj, cum_c, dcat)


BIAS_W = 640


def _rel_onehot():
    j = _iota((REL_PAD, BIAS_W), 1)
    rel = jnp.clip(CA_PAD + CHUNK - 1 - j, -128, 128) + 128
    return (_iota((REL_PAD, BIAS_W), 0) == rel).astype(F32)


def _bias_build(rbp):
    def body(rb_ref, o_ref):
        f = _hdot_raw(rb_ref[...], _rel_onehot(), "nn")
        for q in range(CHUNK):
            o_ref[q] = pltpu.roll(f, (BIAS_W - (CHUNK - 1 - q)) % BIAS_W, 1)[:, :CA_BAND]

    return pl.pallas_call(body, name="ca_bias_build", out_shape=jax.ShapeDtypeStruct((CHUNK, 8, CA_BAND), F32))(rbp)


def _bias_grad(dbias_q):
    def body(db_ref, o_ref):
        acc = jnp.zeros((8, BIAS_W), F32)
        for q in range(CHUNK):
            acc = acc + pltpu.roll(db_ref[q], CHUNK - 1 - q, 1)
        o_ref[...] = _hdot_raw(acc, _rel_onehot(), "nt")

    return pl.pallas_call(body, name="ca_bias_grad", out_shape=jax.ShapeDtypeStruct((8, REL_PAD), F32))(dbias_q)


def _ca_block(c, masked, q, kb, vb, bias2):
    lane = _iota((CHUNK, 128), 1)
    qs = q * ATT_SCALE
    q2 = jnp.concatenate([jnp.where(lane < 64, qs, 0.0), jnp.where(lane >= 64, qs, 0.0)], axis=0)
    s = bdot(q2, kb, "nt") + bias2.reshape(2 * CHUNK, CA_BAND)
    if masked:
        s = jnp.where((c * CHUNK - CA_PAD + _iota((2 * CHUNK, CA_BAND), 1)) >= 0, s, NEG)
    o2 = _attend(s, vb)
    return jnp.where(lane < 64, o2[:CHUNK], o2[CHUNK:])


CA_PER_STEP = 8
CA_ROWS = CA_PER_STEP * CHUNK
CA_MASKED_STEPS = -(-CA_PAD // CA_ROWS)


def _ca_fwd(proj, kvpad, bias):
    def body(q_ref, k_ref, v_ref, b_ref, o_ref):
        def run(masked):
            outs = []
            for i in range(CA_PER_STEP):
                c = pl.program_id(1) * CA_PER_STEP + i
                band = pl.ds(pl.multiple_of(c * CHUNK, CHUNK), CA_BAND)
                rows = slice(i * CHUNK, (i + 1) * CHUNK)
                outs.append(_ca_block(c, masked, q_ref[rows, :], k_ref[band, :], v_ref[band, :], b_ref[...]))
            for i in range(CA_PER_STEP):
                o_ref[i * CHUNK:(i + 1) * CHUNK, :] = outs[i]

        pl.when(pl.program_id(1) < CA_MASKED_STEPS)(lambda: run(True))
        pl.when(pl.program_id(1) >= CA_MASKED_STEPS)(lambda: run(False))

    return pl.pallas_call(
        body, name="ca_fwd", grid=(4, NCHUNK // CA_PER_STEP),
        in_specs=[pl.BlockSpec((CA_ROWS, 128), lambda hp, c: (c, hp)),
                  pl.BlockSpec((T + CA_PAD, 128), lambda hp, c: (0, hp)),
                  pl.BlockSpec((T + CA_PAD, 128), lambda hp, c: (0, 4 + hp)),
                  pl.BlockSpec((2, CHUNK, CA_BAND), lambda hp, c: (hp, 0, 0))],
        out_specs=pl.BlockSpec((CA_ROWS, 128), lambda hp, c: (c, hp)),
        out_shape=jax.ShapeDtypeStruct((T, D), F32),
        compiler_params=_cp(("parallel", "parallel")),
    )(proj, kvpad, kvpad, bias)


def _ca_bwd(proj, kvpad, bias, dcat):
    def body(q_ref, k_ref, v_ref, b_ref, do_ref, dq_ref, dk_ref, dv_ref, db_ref):
        c = pl.program_id(1)

        @pl.when(c == 0)
        def _():
            dk_ref[...] = jnp.zeros_like(dk_ref)
            dv_ref[...] = jnp.zeros_like(dv_ref)
            db_ref[...] = jnp.zeros_like(db_ref)

        def run(masked):
            grads, bands = [], []
            for i in range(CA_PER_STEP):
                ci = c * CA_PER_STEP + i
                band = pl.ds(pl.multiple_of(ci * CHUNK, CHUNK), CA_BAND)
                rows = slice(i * CHUNK, (i + 1) * CHUNK)
                fn = functools.partial(_ca_block, ci, masked)
                _, vjp = jax.vjp(fn, q_ref[rows, :], k_ref[band, :], v_ref[band, :], b_ref[...])
                grads.append(vjp(do_ref[rows, :]))
                bands.append(band)
            for i, (dq, _, _, _) in enumerate(grads):
                dq_ref[i * CHUNK:(i + 1) * CHUNK, :] = dq
            for band, (_, dkb, dvb, _) in zip(bands, grads):
                dk_ref[band, :] += dkb
                dv_ref[band, :] += dvb
            db_ref[...] += functools.reduce(lambda a, b: a + b, [g[3] for g in grads])

        pl.when(c < CA_MASKED_STEPS)(lambda: run(True))
        pl.when(c >= CA_MASKED_STEPS)(lambda: run(False))

    sds = lambda *s: jax.ShapeDtypeStruct(s, F32)
    padded = lambda: pl.BlockSpec((T + CA_PAD, 128), lambda hp, c: (0, hp))
    return pl.pallas_call(
        body, name="ca_bwd", grid=(4, NCHUNK // CA_PER_STEP),
        in_specs=[pl.BlockSpec((CA_ROWS, 128), lambda hp, c: (c, hp)),
                  pl.BlockSpec((T + CA_PAD, 128), lambda hp, c: (0, hp)),
                  pl.BlockSpec((T + CA_PAD, 128), lambda hp, c: (0, 4 + hp)),
                  pl.BlockSpec((2, CHUNK, CA_BAND), lambda hp, c: (hp, 0, 0)),
                  pl.BlockSpec((CA_ROWS, 128), lambda hp, c: (c, hp))],
        out_specs=(pl.BlockSpec((CA_ROWS, 128), lambda hp, c: (c, hp)), padded(), padded(),
                   pl.BlockSpec((2, CHUNK, CA_BAND), lambda hp, c: (hp, 0, 0))),
        out_shape=(sds(T, 512), sds(T + CA_PAD, 512), sds(T + CA_PAD, 512), sds(8, CHUNK, CA_BAND)),
        compiler_params=_cp(("parallel", "arbitrary")),
    )(proj, kvpad, kvpad, bias, dcat)


def _lru_pre(xs, cw, cb, wa, ba, wx, bx, lam):
    xc = cb + xs[0] * cw[0:1, :] + xs[1] * cw[1:2, :] + xs[2] * cw[2:3, :] + xs[3] * cw[3:4, :]
    ra = _sigmoid(bdot(xc, wa, "nn") + ba)
    ii = _sigmoid(bdot(xc, wx, "nn") + bx)
    la = 8.0 * ra * _log_sigmoid(lam)
    return jnp.exp(la), jnp.sqrt(-_expm1(2.0 * la)) * (ii * xc)


def _lru_pre_specs():
    full = lambda shape: pl.BlockSpec(shape, lambda i: (0,) * len(shape))
    return [pl.BlockSpec((4, ROWS, 512), lambda i: (0, i, 0)), full((4, 512)), full((1, 512)),
            full((512, 512)), full((1, 512)), full((512, 512)), full((1, 512)), full((1, 512))]


def _lru_pre_fwd(xs, cw, cb, wa, ba, wx, bx, lam):
    def body(xs_ref, cw_ref, cb_ref, wa_ref, ba_ref, wx_ref, bx_ref, lam_ref, a_ref, b_ref):
        a, b = _lru_pre(xs_ref[...], cw_ref[...], cb_ref[...], wa_ref[...], ba_ref[...], wx_ref[...], bx_ref[...],
                        lam_ref[...])
        a_ref[...] = a
        b_ref[...] = b

    row = pl.BlockSpec((ROWS, 512), lambda i: (i, 0))
    sds = jax.ShapeDtypeStruct((T, 512), F32)
    return pl.pallas_call(body, name="lru_pre_fwd", grid=(T // ROWS,), in_specs=_lru_pre_specs(),
                          out_specs=(row, row), out_shape=(sds, sds), compiler_params=_cp(("parallel",)),
                          )(xs, cw, cb, wa, ba, wx, bx, lam)


def _lru_pre_bwd(xs, cw, cb, wa, ba, wx, bx, lam, da, db):
    def body(xs_ref, cw_ref, cb_ref, wa_ref, ba_ref, wx_ref, bx_ref, lam_ref, da_ref, db_ref,
             dxs_ref, dcw_ref, dcb_ref, dwa_ref, dba_ref, dwx_ref, dbx_ref, dlam_ref):
        acc = (dcw_ref, dcb_ref, dwa_ref, dba_ref, dwx_ref, dbx_ref, dlam_ref)

        @pl.when(pl.program_id(0) == 0)
        def _():
            for r in acc:
                r[...] = jnp.zeros_like(r)

        _, vjp = jax.vjp(_lru_pre, xs_ref[...], cw_ref[...], cb_ref[...], wa_ref[...], ba_ref[...], wx_ref[...],
                         bx_ref[...], lam_ref[...])
        grads = vjp((da_ref[...], db_ref[...]))
        dxs_ref[...] = grads[0]
        for r, g in zip(acc, grads[1:]):
            r[...] += g

    row = pl.BlockSpec((ROWS, 512), lambda i: (i, 0))
    specs = _lru_pre_specs()
    sds = lambda *s: jax.ShapeDtypeStruct(s, F32)
    return pl.pallas_call(
        body, name="lru_pre_bwd", grid=(T // ROWS,), in_specs=specs + [row, row], out_specs=tuple(specs),
        out_shape=(sds(4, T, 512), sds(4, 512), sds(1, 512), sds(512, 512), sds(1, 512), sds(512, 512), sds(1, 512),
                   sds(1, 512)),
        compiler_params=_cp(("arbitrary",)),
    )(xs, cw, cb, wa, ba, wx, bx, lam, da, db)


SCAN_ROWS = 8


def _scan8(a, b, towards_later):
    row = _iota((SCAN_ROWS, 512), 0)
    for s in (1, 2, 4):
        if towards_later:
            keep, shift = row >= s, s
        else:
            keep, shift = row < SCAN_ROWS - s, SCAN_ROWS - s
        a_s = jnp.where(keep, pltpu.roll(a, shift, 0), 1.0)
        b_s = jnp.where(keep, pltpu.roll(b, shift, 0), 0.0)
        b = a * b_s + b
        a = a * a_s
    return a, b


def _lru_scan_fwd(a, b):
    def body(a_ref, b_ref, h_ref):
        def step(i, carry):
            rows = pl.ds(pl.multiple_of(i * SCAN_ROWS, SCAN_ROWS), SCAN_ROWS)
            a8, b8 = _scan8(a_ref[rows, :], b_ref[rows, :], True)
            h = a8 * carry + b8
            h_ref[rows, :] = h
            return jnp.broadcast_to(h[SCAN_ROWS - 1:, :], (SCAN_ROWS, 512))

        lax.fori_loop(0, T // SCAN_ROWS, step, jnp.zeros((SCAN_ROWS, 512), F32), unroll=2)

    return pl.pallas_call(body, name="lru_scan_fwd", out_shape=jax.ShapeDtypeStruct((T, 512), F32),
                          compiler_params=pltpu.CompilerParams(vmem_limit_bytes=VMEM_LIMIT))(a, b)


def _lru_scan_bwd(a_next, h_prev, dh):
    def body(a_ref, h_ref, dh_ref, da_ref, db_ref):
        def step(i, carry):
            start = T - SCAN_ROWS * (i + 1)
            rows = pl.ds(pl.multiple_of(start, SCAN_ROWS), SCAN_ROWS)
            a8, b8 = _scan8(a_ref[rows, :], dh_ref[rows, :], False)
            g = a8 * carry + b8
            db_ref[rows, :] = g
            da_ref[rows, :] = g * h_ref[rows, :]
            return jnp.broadcast_to(g[:1, :], (SCAN_ROWS, 512))

        lax.fori_loop(0, T // SCAN_ROWS, step, jnp.zeros((SCAN_ROWS, 512), F32), unroll=2)

    sds = jax.ShapeDtypeStruct((T, 512), F32)
    return pl.pallas_call(body, name="lru_scan_bwd", out_shape=(sds, sds),
                          compiler_params=pltpu.CompilerParams(vmem_limit_bytes=VMEM_LIMIT))(a_next, h_prev, dh)


def _lru_post(h, gate):
    return h * _gelu_tanh(gate)


def _lru_post_fwd(h, proj, cat):
    def body(h_ref, g_ref, cat_ref, o_ref):
        o_ref[...] = _lru_post(h_ref[...], g_ref[...])

    row = pl.BlockSpec((ROWS, 512), lambda i: (i, 0))
    return pl.pallas_call(body, name="lru_post_fwd", grid=(T // ROWS,),
                          in_specs=[row, pl.BlockSpec((ROWS, 512), lambda i: (i, 3)), pl.BlockSpec(memory_space=pl.ANY)],
                          out_specs=pl.BlockSpec((ROWS, 512), lambda i: (i, 1)),
                          out_shape=jax.ShapeDtypeStruct((T, D), F32), input_output_aliases={2: 0},
                          compiler_params=_cp(("parallel",)))(h, proj, cat)


def _lru_post_bwd(h, proj, dcat):
    def body(h_ref, g_ref, do_ref, dh_ref, dg_ref):
        _, vjp = jax.vjp(_lru_post, h_ref[...], g_ref[...])
        dh, dg = vjp(do_ref[...])
        dh_ref[...] = dh
        dg_ref[...] = dg

    row = pl.BlockSpec((ROWS, 512), lambda i: (i, 0))
    sds = jax.ShapeDtypeStruct((T, 512), F32)
    return pl.pallas_call(body, name="lru_post_bwd", grid=(T // ROWS,),
                          in_specs=[row, pl.BlockSpec((ROWS, 512), lambda i: (i, 3)),
                                    pl.BlockSpec((ROWS, 512), lambda i: (i, 1))],
                          out_specs=(row, row), out_shape=(sds, sds), compiler_params=_cp(("parallel",)))(h, proj, dcat)


def _conv_dx(dxs_shift):
    def body(d_ref, o_ref):
        o_ref[...] = d_ref[0] + d_ref[1] + d_ref[2] + d_ref[3]

    row = pl.BlockSpec((ROWS, 512), lambda i: (i, 0))
    return pl.pallas_call(body, name="lru_conv_dx", grid=(T // ROWS,),
                          in_specs=[pl.BlockSpec((4, ROWS, 512), lambda i: (0, i, 0))], out_specs=row,
                          out_shape=jax.ShapeDtypeStruct((T, 512), F32), compiler_params=_cp(("parallel",)))(dxs_shift)


def _position():
    return lax.axis_index("x"), lax.axis_index("y"), lax.axis_index("c")


def _other_chips(x, y):
    return [(1 - x, y), (x, 1 - y), (1 - x, 1 - y)]


def _al(v, n):
    return v * n if isinstance(v, int) else pl.multiple_of(v * n, n)


_AG_ITEMS = [
    ((4, 32, 128), lambda o, s, h: o.at[s, pl.ds(_al(h, 16), 16), :], lambda r, h: r.at[pl.ds(_al(h, 16), 16), :]),
    ((4, 774, 1024), lambda o, s, h: o.at[s, :, pl.ds(_al(h, 512), 512)], lambda r, h: r.at[:, pl.ds(_al(h, 512), 512)]),
    ((1024, 1024), lambda o, s, h: o.at[pl.ds(_al(2 * s + h, 128), 128), :], lambda r, h: r.at[pl.ds(_al(h, 128), 128), :]),
    ((2, 1024, 4096), lambda o, s, h: o.at[h, :, pl.ds(_al(s, 1024), 1024)], lambda r, h: r.at[h]),
    ((2, 4096, 1024), lambda o, s, h: o.at[h, pl.ds(_al(s, 1024), 1024), :], lambda r, h: r.at[h]),
    ((1024, 2560), lambda o, s, h: o.at[pl.ds(_al(h, 512), 512), pl.ds(_al(s, 640), 640)],
     lambda r, h: r.at[pl.ds(_al(h, 512), 512), :]),
    ((1024, 1024), lambda o, s, h: o.at[pl.ds(_al(2 * s + h, 128), 128), :], lambda r, h: r.at[pl.ds(_al(h, 128), 128), :]),
]


_AG_GROUPS = [(0, 1, 2), (3, 4), (5, 6)]

_HBM = pl.BlockSpec(memory_space=pltpu.HBM)
_SEM = pl.BlockSpec(memory_space=pltpu.SEMAPHORE)
_SPLIT = dict(has_side_effects=pltpu.SideEffectType.DATAFLOW_SIDE_EFFECTING)


def _hbm(a):
    return pltpu.with_memory_space_constraint(a, pltpu.HBM)


def _ag_ici_copy(i, j, chip, c, slot, src_ref, land_ref, send_sems, recv_sems, k):
    _, dst, half = _AG_ITEMS[i]
    return pltpu.make_async_remote_copy(src_ref=half(src_ref, c), dst_ref=dst(land_ref, slot, c), send_sem=send_sems.at[k],
                                        recv_sem=recv_sems.at[k], device_id=(*chip, c), device_id_type=MESH)


def _ag_start(groups, shards, name):
    items_all = [i for g in groups for i in _AG_GROUPS[g]]
    n = len(items_all)
    ng = len(groups)
    lands = [lax.empty(_AG_ITEMS[i][0], shards[i].dtype) for i in items_all]

    def body(*refs):
        srcs, land_refs = dict(zip(items_all, refs[:n])), dict(zip(items_all, refs[n:2 * n]))
        sems = refs[2 * n:2 * n + 2 * ng]
        token = refs[-1]
        x, y, c = _position()
        me = 2 * x + y
        for gi, g in enumerate(groups):
            for t, i in enumerate(_AG_GROUPS[g]):
                for j, chip in enumerate(_other_chips(x, y)):
                    _ag_ici_copy(i, j, chip, c, me, srcs[i], land_refs[i], sems[2 * gi], sems[2 * gi + 1], 3 * t + j).start()
        token[...] = jnp.zeros_like(token)

    sem_shapes = []
    for g in groups:
        sem_shapes += [pltpu.SemaphoreType.DMA((3 * len(_AG_GROUPS[g]),))] * 2
    ops = [shards[i] for i in items_all] + lands
    out = pl.pallas_call(
        body, name=name,
        out_shape=tuple(sem_shapes) + tuple(pltpu.HBM(a.shape, a.dtype) for a in ops) + (jax.ShapeDtypeStruct((8, 128), F32),),
        in_specs=(_HBM,) * (2 * n),
        out_specs=(_SEM,) * (2 * ng) + (_HBM,) * (2 * n) + (pl.BlockSpec(memory_space=pltpu.VMEM),),
        input_output_aliases={i: 2 * ng + i for i in range(2 * n)},
        compiler_params=pltpu.CompilerParams(**_SPLIT),
    )(*[_hbm(a) for a in ops])
    sems, thru, token = out[:2 * ng], out[2 * ng:-1], out[-1]
    return ({g: (sems[2 * gi], sems[2 * gi + 1]) for gi, g in enumerate(groups)},
            dict(zip(items_all, thru[:n])), dict(zip(items_all, thru[n:])), token)


def _ag_wait(g, sems, srcs, lands, after):
    items = _AG_GROUPS[g]
    m = len(items)

    def body(*refs):
        src_refs, land_refs = refs[:m], refs[m:2 * m]
        send_sems, recv_sems = refs[2 * m], refs[2 * m + 1]
        x, y, c = _position()
        for t, i in enumerate(items):
            for j, chip in enumerate(_other_chips(x, y)):
                cp = _ag_ici_copy(i, j, chip, c, 2 * chip[0] + chip[1], src_refs[t], land_refs[t], send_sems, recv_sems,
                                  3 * t + j)
                cp.wait_send()
                cp.wait_recv()

    ops = [srcs[i] for i in items] + [lands[i] for i in items]
    out = pl.pallas_call(
        body, name=f"allgather_wait_{g}",
        out_shape=tuple(pltpu.HBM(a.shape, a.dtype) for a in ops),
        in_specs=(_HBM,) * (2 * m) + (_SEM, _SEM, pl.BlockSpec(memory_space=pl.ANY)),
        out_specs=(_HBM,) * (2 * m),
        input_output_aliases={i: i for i in range(2 * m)},
        compiler_params=pltpu.CompilerParams(**_SPLIT),
    )(*ops, sems[0], sems[1], after)
    return list(out[:m]), list(out[m:])


def _ag_forward(g, srcs, lands):
    return _ag_sibling(_AG_GROUPS[g], srcs, lands, False, f"allgather_forward_{g}")


def _ag_push_own(srcs, lands):
    items = tuple(sorted(lands))
    out = _ag_sibling(items, [srcs[i] for i in items], [lands[i] for i in items], True, "allgather_push_own")
    return dict(zip(items, out))


def _ag_sibling(items, srcs, lands, own, name):
    m = len(items)
    per = 2 if own else 3

    def body(*refs):
        src_refs, in_refs, out_refs = refs[:m], refs[m:2 * m], refs[2 * m:3 * m]
        send_sems, recv_sems = refs[3 * m:]
        x, y, c = _position()
        sibling = (x, y, 1 - c)
        me = 2 * x + y
        if own:
            mine = theirs = [(me, 0), (me, 1)]
        else:
            slots = [2 * chip[0] + chip[1] for chip in _other_chips(x, y)]
            mine, theirs = [(s, c) for s in slots], [(s, 1 - c) for s in slots]
        sends = []
        for t, i in enumerate(items):
            _, dst, half = _AG_ITEMS[i]
            for k, (slot, hc) in enumerate(mine):
                src = half(src_refs[t], hc) if own else dst(in_refs[t], slot, hc)
                sends.append(pltpu.make_async_remote_copy(
                    src_ref=src, dst_ref=dst(out_refs[t], slot, hc), send_sem=send_sems.at[per * t + k],
                    recv_sem=recv_sems.at[per * t + k], device_id=sibling, device_id_type=MESH))
        for cp in sends:
            cp.start()
        for t, i in enumerate(items):
            dst = _AG_ITEMS[i][1]
            for k, (slot, hc) in enumerate(theirs):
                there = dst(out_refs[t], slot, hc)
                pltpu.make_async_remote_copy(src_ref=there, dst_ref=there, send_sem=send_sems.at[per * t + k],
                                             recv_sem=recv_sems.at[per * t + k], device_id=sibling,
                                             device_id_type=MESH).wait_recv()
        for cp in sends:
            cp.wait_send()

    any_spec = pl.BlockSpec(memory_space=pl.ANY)
    return pl.pallas_call(
        body, name=name,
        in_specs=[any_spec] * (2 * m), out_specs=(any_spec,) * m,
        out_shape=tuple(jax.ShapeDtypeStruct(a.shape, a.dtype) for a in lands),
        input_output_aliases={m + t: t for t in range(m)},
        scratch_shapes=[pltpu.SemaphoreType.DMA((per * m,)), pltpu.SemaphoreType.DMA((per * m,))],
    )(*srcs, *lands)


def _pair_swap_copy(g_ref, r_ref, send_sem, recv_sem):
    x, y, c = _position()
    hc = g_ref.shape[2] // 2
    return pltpu.make_async_remote_copy(src_ref=g_ref.at[:, :, pl.ds(_al(1 - c, hc), hc)], dst_ref=r_ref,
                                        send_sem=send_sem, recv_sem=recv_sem, device_id=(x, y, 1 - c),
                                        device_id_type=MESH)


def _pair_swap_start(gb, tag):
    _, rows, cols = gb.shape
    recv = lax.empty((4, rows, cols // 2), gb.dtype)

    def body(g_ref, r_ref, send_sem, recv_sem, g_thru, r_thru, token):
        _pair_swap_copy(g_ref, r_ref, send_sem, recv_sem).start()
        token[...] = jnp.zeros_like(token)

    return pl.pallas_call(
        body, name="grad_pair_swap_start_" + tag,
        out_shape=(pltpu.SemaphoreType.DMA(()), pltpu.SemaphoreType.DMA(()), pltpu.HBM(gb.shape, gb.dtype),
                   pltpu.HBM(recv.shape, recv.dtype), jax.ShapeDtypeStruct((8, 128), F32)),
        in_specs=(_HBM, _HBM), out_specs=(_SEM, _SEM, _HBM, _HBM, pl.BlockSpec(memory_space=pltpu.VMEM)),
        input_output_aliases={0: 2, 1: 3},
        compiler_params=pltpu.CompilerParams(**_SPLIT),
    )(_hbm(gb), _hbm(recv))


def _pair_swap_wait(started, after, tag):
    send_sem, recv_sem, gb, recv, _ = started

    def body(g_ref, r_ref, send_sem, recv_sem, after_ref, g_out, r_out):
        cp = _pair_swap_copy(g_ref, r_ref, send_sem, recv_sem)
        cp.wait_send()
        cp.wait_recv()

    return pl.pallas_call(
        body, name="grad_pair_swap_wait_" + tag,
        out_shape=(pltpu.HBM(gb.shape, gb.dtype), pltpu.HBM(recv.shape, recv.dtype)),
        in_specs=(_HBM, _HBM, _SEM, _SEM, pl.BlockSpec(memory_space=pl.ANY)), out_specs=(_HBM, _HBM),
        input_output_aliases={0: 0, 1: 1},
        compiler_params=pltpu.CompilerParams(**_SPLIT),
    )(gb, recv, send_sem, recv_sem, after)


def _handover_copy(r_ref, send_sem, recv_sem, core):
    x, y, c = _position()
    hc = r_ref.shape[1] // 2
    cols = r_ref.at[:, pl.ds(_al(core, hc), hc)]
    return pltpu.make_async_remote_copy(src_ref=cols, dst_ref=cols, send_sem=send_sem, recv_sem=recv_sem,
                                        device_id=(x, y, 1 - c), device_id_type=MESH)


def _handover_start(red, tag):
    def body(r_ref, send_sem, recv_sem, r_thru, token):
        _handover_copy(r_ref, send_sem, recv_sem, lax.axis_index("c")).start()
        token[...] = jnp.zeros_like(token)

    return pl.pallas_call(
        body, name="grad_handover_start_" + tag,
        out_shape=(pltpu.SemaphoreType.DMA(()), pltpu.SemaphoreType.DMA(()), pltpu.HBM(red.shape, red.dtype),
                   jax.ShapeDtypeStruct((8, 128), F32)),
        in_specs=(_HBM,), out_specs=(_SEM, _SEM, _HBM, pl.BlockSpec(memory_space=pltpu.VMEM)),
        input_output_aliases={0: 2},
        compiler_params=pltpu.CompilerParams(**_SPLIT),
    )(_hbm(red))


def _handover_wait(started, after, tag):
    send_sem, recv_sem, red, _ = started

    def body(r_ref, send_sem, recv_sem, after_ref, r_out):
        c = lax.axis_index("c")
        _handover_copy(r_ref, send_sem, recv_sem, c).wait_send()
        _handover_copy(r_ref, send_sem, recv_sem, 1 - c).wait_recv()

    return pl.pallas_call(
        body, name="grad_handover_wait_" + tag,
        out_shape=pltpu.HBM(red.shape, red.dtype),
        in_specs=(_HBM, _SEM, _SEM, pl.BlockSpec(memory_space=pl.ANY)), out_specs=_HBM,
        input_output_aliases={0: 0},
        compiler_params=pltpu.CompilerParams(**_SPLIT),
    )(red, send_sem, recv_sem, after)


def _handover(red, tag):
    started = _handover_start(red, tag)
    return _handover_wait(started, started[3], tag)


def _a2a_copy(j, chip, c, p_ref, q_ref, q_slot, send_sems, recv_sems):
    return pltpu.make_async_remote_copy(src_ref=p_ref.at[2 * chip[0] + chip[1]], dst_ref=q_ref.at[q_slot],
                                        send_sem=send_sems.at[j], recv_sem=recv_sems.at[j], device_id=(*chip, c),
                                        device_id_type=MESH)


def _a2a_start(p, tag):
    def body(p_ref, q_ref, send_sems, recv_sems, p_thru, q_thru, token):
        x, y, c = _position()
        for j, chip in enumerate(_other_chips(x, y)):
            _a2a_copy(j, chip, c, p_ref, q_ref, 2 * x + y, send_sems, recv_sems).start()
        token[...] = jnp.zeros_like(token)

    return pl.pallas_call(
        body, name="grad_alltoall_start_" + tag,
        out_shape=(pltpu.SemaphoreType.DMA((3,)), pltpu.SemaphoreType.DMA((3,)), pltpu.HBM(p.shape, p.dtype),
                   pltpu.HBM(p.shape, p.dtype), jax.ShapeDtypeStruct((8, 128), F32)),
        in_specs=(_HBM, _HBM), out_specs=(_SEM, _SEM, _HBM, _HBM, pl.BlockSpec(memory_space=pltpu.VMEM)),
        input_output_aliases={0: 2, 1: 3},
        compiler_params=pltpu.CompilerParams(**_SPLIT),
    )(_hbm(p), _hbm(lax.empty(p.shape, p.dtype)))


def _a2a_wait(send_sems, recv_sems, p, q, after, tag):
    def body(p_ref, q_ref, send_sems, recv_sems, after_ref, p_out, q_out):
        x, y, c = _position()
        for j, chip in enumerate(_other_chips(x, y)):
            cp = _a2a_copy(j, chip, c, p_ref, q_ref, 2 * chip[0] + chip[1], send_sems, recv_sems)
            cp.wait_send()
            cp.wait_recv()

    return pl.pallas_call(
        body, name="grad_alltoall_wait_" + tag,
        out_shape=(pltpu.HBM(p.shape, p.dtype), pltpu.HBM(q.shape, q.dtype)),
        in_specs=(_HBM, _HBM, _SEM, _SEM, pl.BlockSpec(memory_space=pl.ANY)), out_specs=(_HBM, _HBM),
        input_output_aliases={0: 0, 1: 1},
        compiler_params=pltpu.CompilerParams(**_SPLIT),
    )(p, q, send_sems, recv_sems, after)


def _comm_rows(rows):
    return next(t for t in (512, 384, 256, 128) if rows % t == 0)


def _pair_add(gb, recv, where, tag):
    _, rows, cols = gb.shape
    hc = cols // 2
    tr = _comm_rows(rows)

    def body(w_ref, g_ref, r_ref, o_ref):
        o_ref[...] = (g_ref[...].astype(F32) + r_ref[...].astype(F32)).astype(o_ref.dtype)

    return pl.pallas_call(
        body, name="grad_pair_add_" + tag,
        grid_spec=pltpu.PrefetchScalarGridSpec(
            num_scalar_prefetch=1, grid=(4, rows // tr),
            in_specs=[pl.BlockSpec((None, tr, hc), lambda s, j, w_ref: (s, j, w_ref[0])),
                      pl.BlockSpec((None, tr, hc), lambda s, j, w_ref: (s, j, 0))],
            out_specs=pl.BlockSpec((None, tr, hc), lambda s, j, w_ref: (s, j, 0))),
        out_shape=jax.ShapeDtypeStruct((4, rows, hc), gb.dtype),
        compiler_params=_cp(("parallel", "parallel")),
    )(where, gb, recv)


def _sum_chips(p, q, where, tag):
    _, rows, hc = q.shape
    tr = _comm_rows(rows)

    def body(w_ref, p_ref, qa_ref, qb_ref, qc_ref, o_ref):
        me = w_ref[1]
        own, qa, qb, qc = (r[...].astype(F32) for r in (p_ref, qa_ref, qb_ref, qc_ref))
        v0 = jnp.where(me == 0, own, qa)
        v1 = jnp.where(me == 1, own, jnp.where(me == 0, qa, qb))
        v2 = jnp.where(me == 2, own, jnp.where(me < 2, qb, qc))
        v3 = jnp.where(me == 3, own, qc)
        o_ref[...] = ((v0 + v1) + v2) + v3

    slot = lambda k: pl.BlockSpec((None, tr, hc), lambda j, w_ref: (w_ref[k], j, 0))
    return pl.pallas_call(
        body, name="grad_sum_chips_" + tag,
        grid_spec=pltpu.PrefetchScalarGridSpec(
            num_scalar_prefetch=1, grid=(rows // tr,),
            in_specs=[slot(1), slot(2), slot(3), slot(4)],
            out_specs=pl.BlockSpec((tr, hc), lambda j, w_ref: (j, w_ref[0]))),
        out_shape=jax.ShapeDtypeStruct((rows, 2 * hc), F32),
        compiler_params=_cp(("parallel",)),
    )(where, p, q, q, q)


def _shard_major(g, axis):
    shape = g.shape
    g = g.reshape(shape[:axis] + (4, shape[axis] // 4) + shape[axis + 1:])
    return jnp.moveaxis(g, axis, 0).reshape(4, -1)


def _unshard(g4, shape, axis):
    n = shape[axis] // 4
    g = g4.reshape((4,) + shape[:axis] + (n,) + shape[axis + 1:])
    return jnp.moveaxis(g, 0, axis).reshape(shape)


def _split(flat, shapes):
    out, off = [], 0
    for shp in shapes:
        n = 1
        for d in shp:
            n *= d
        out.append(flat[..., off:off + n].reshape(flat.shape[:-1] + tuple(shp)))
        off += n
    return out


def _even_rows_to_kernel(wt):
    return jnp.concatenate([wt[:1536], wt[1552:3088], wt[1536:1552], wt[3088:3096],
                            jnp.zeros((PE - 3096, wt.shape[1]), wt.dtype)], axis=0)


def _block_diag(w):
    eye = jnp.eye(8, dtype=w.dtype)
    return (w[:, :, None, :] * eye[:, None, :, None]).reshape(512, 512)


def _diag_blocks(g):
    eye = jnp.eye(8, dtype=g.dtype)
    return (g.reshape(8, 64, 8, 64) * eye[:, None, :, None]).sum(axis=2)


def _shift_down(a, s):
    return a if s == 0 else jnp.pad(a, ((s, 0), (0, 0)))[:a.shape[0]]


def _shift_up(a, s):
    return a if s == 0 else jnp.pad(a, ((0, s), (0, 0)))[s:]


SMALL_SHARDED_SHAPES = [(2, 4, 256), (16, 64), (4, 128), (128,), (128,), (128,), (128,)]
REPL_SHAPES = [(256,), (512,), (8,), (8, 257), (8, 64, 64), (8, 64, 64)]


def kernel(x, norm_w, w_in_even, gla_w_a_up, gla_b_a, gla_norm_w, fox_b_f, w_out_even, w_in_odd, rel_bias, conv_w, conv_b, lru_w_a, lru_b_a, lru_w_x, lru_b_x, lru_lambda, w_out_odd, w_mlp_up, w_mlp_down, loss_target, m_norm_w, m_w_in_even, m_gla_w_a_up, m_gla_b_a, m_gla_norm_w, m_fox_b_f, m_w_out_even, m_w_in_odd, m_rel_bias, m_conv_w, m_conv_b, m_lru_w_a, m_lru_b_a, m_lru_w_x, m_lru_b_x, m_lru_lambda, m_w_out_odd, m_w_mlp_up, m_w_mlp_down, v_norm_w, v_w_in_even, v_gla_w_a_up, v_gla_b_a, v_gla_norm_w, v_fox_b_f, v_w_out_even, v_w_in_odd, v_rel_bias, v_conv_w, v_conv_b, v_lru_w_a, v_lru_b_a, v_lru_w_x, v_lru_b_x, v_lru_lambda, v_w_out_odd, v_w_mlp_up, v_w_mlp_down):
    c_idx = lax.axis_index("c")

    small_local = [norm_w, gla_w_a_up[0], conv_w[0], conv_b[0], lru_b_a[0], lru_b_x[0], lru_lambda[0]]
    small_src = jnp.concatenate([a.reshape(-1) for a in small_local]).reshape(32, 128)
    first = {0: small_src, 1: w_in_even[0].T.astype(BF16), 2: w_out_even[0].astype(BF16)}
    sems0, srcs0, lands0, ag_token = _ag_start([0], first, "allgather_start_0")
    zero = ag_token[0, 0]
    later = {3: (w_mlp_up + zero).astype(BF16), 4: (w_mlp_down + zero).astype(BF16),
             5: (w_in_odd[0] + zero).astype(BF16), 6: (w_out_odd[0] + zero).astype(BF16)}
    sems1, srcs1, lands1, ag_token = _ag_start([1, 2], later, "allgather_start_1")
    ag_sems, ag_srcs = {**sems0, **sems1}, {**srcs0, **srcs1}
    ag_lands = _ag_push_own(ag_srcs, {**lands0, **lands1})

    def gathered(g, after):
        srcs_g, lands_g = _ag_wait(g, ag_sems[g], ag_srcs, ag_lands, after)
        return _ag_forward(g, srcs_g, lands_g)

    small4, w_in_e4, w_out_e = gathered(0, ag_token)
    me = 2 * lax.axis_index("x") + lax.axis_index("y")
    others = [k + (k >= me).astype(jnp.int32) for k in range(3)]
    where = jnp.stack([c_idx, me] + others).astype(jnp.int32)

    w_in_e_t = _even_rows_to_kernel(w_in_e4.reshape(3096, D))
    g_small = _split(small4.reshape(4, 32 * 128), SMALL_SHARDED_SHAPES)
    nw_full = _unshard(g_small[0], (2, 4, 1024), 2)
    wa_up = _unshard(g_small[1], (16, 256), 1)
    cw = _unshard(g_small[2], (4, 512), 1)
    cb, lba, lbx, lam = [_unshard(g, (512,), 0).reshape(1, 512) for g in g_small[3:]]
    nw = lambda layer, i: nw_full[layer, i].reshape(1, D)

    wa_pad = jnp.pad(wa_up, ((0, 128 - 16), (0, 0)))
    gla_ba = gla_b_a.reshape(1, 256)
    gla_nw = gla_norm_w.reshape(1, 512)
    fox_bpad = jnp.pad(fox_b_f.reshape(1, 8), ((0, 0), (FOX_LANE0, 128 - FOX_LANE0 - 8)))
    rbp = jnp.pad(rel_bias[0], ((0, 0), (0, REL_PAD - 257)))
    wa_bd = _block_diag(lru_w_a[0])
    wx_bd = _block_diag(lru_w_x[0])

    x0 = x[0]
    tgt = loss_target[0]

    h0 = _prenorm(x0, nw(0, 0), "prenorm_l0_mix")
    proj_e = _mm(h0, w_in_e_t, "nt", tm=2048, tn=640, name="mm_in_even")
    cat0, s_prev = _gla_fwd(proj_e, wa_pad, gla_ba, gla_nw)
    cum_r = _fox_gate_fwd(proj_e, fox_bpad)
    cum_c = cum_r[:, FOX_LANE0:FOX_LANE0 + 8].T
    cat0 = _fox_fwd(proj_e, cum_c, cat0)
    mix0 = _mm(cat0, w_out_e, "nn", tm=2048, tn=512, name="mm_out_even")
    x1, h1 = _post_pre_fwd(x0, mix0, nw(0, 1), nw(0, 2), "post_pre_l0_mix")
    w_up, w_dn = gathered(1, x1)
    a0, r0 = _mm(h1, w_up, "nn", tm=2048, tn=1024, b_layer=0, relu_pair=True, name="mm_up_l0")
    d0 = _mm(a0, w_dn, "nn", tm=1024, tn=512, b_layer=0, name="mm_down_l0")
    x2, h2 = _post_pre_fwd(x1, d0, nw(0, 3), nw(1, 0), "post_pre_l0_mlp")

    w_in_o, w_out_o = gathered(2, x2)
    proj_o = _mm(h2, w_in_o, "nn", tm=2048, tn=640, name="mm_in_odd")
    bias_q = _bias_build(rbp)
    bias = bias_q.transpose(1, 0, 2)
    kvpad = jnp.pad(proj_o[:, 512:1536], ((CA_PAD, 0), (0, 0)))
    cat1 = _ca_fwd(proj_o, kvpad, bias)
    x_in = proj_o[:, 2048:2560]
    xs = jnp.stack([_shift_down(x_in, 3 - j) for j in range(4)])
    lru_a, lru_b = _lru_pre_fwd(xs, cw, cb, wa_bd, lba, wx_bd, lbx, lam)
    hh = _lru_scan_fwd(lru_a, lru_b)
    cat1 = _lru_post_fwd(hh, proj_o, cat1)
    mix1 = _mm(cat1, w_out_o, "nn", tm=2048, tn=512, name="mm_out_odd")
    x3, h3 = _post_pre_fwd(x2, mix1, nw(1, 1), nw(1, 2), "post_pre_l1_mix")
    a1, r1 = _mm(h3, w_up, "nn", tm=2048, tn=1024, b_layer=1, relu_pair=True, name="mm_up_l1")
    d1 = _mm(a1, w_dn, "nn", tm=1024, tn=512, b_layer=1, name="mm_down_l1")
    g4, loss_part, dd1, dnw13 = _post_loss(x3, d1, nw(1, 3), tgt)
    loss = lax.psum(loss_part[0, 0], ("x", "y", "c"))

    def rs_begin(swap, after, tag):
        gb, recv = _pair_swap_wait(swap, after, tag)
        return _a2a_start(_pair_add(gb, recv, where, tag), tag)

    def rs_end(started, after, tag):
        send_sems, recv_sems, p, q, _ = started
        p, q = _a2a_wait(send_sems, recv_sems, p, q, after, tag)
        return _handover(_sum_chips(p, q, where, tag), tag)

    gba = lax.dynamic_update_slice(lax.empty((4, GA_ROWS, D), BF16), jnp.zeros((4, GA_UP - GA_GAP, D), BF16),
                                   (0, GA_GAP, 0))
    gba = _mm(a1, dd1, "tn", tm=512, tn=1024, into=(gba, 1024, GA_DN), name="mm_down_l1_dw")
    du1 = _mm(dd1, w_dn, "nt", tm=2048, tn=1024, b_layer=1, times2=r1, out_dtype=BF16, name="mm_down_l1_dx")
    gba = _mm(du1, h3, "tn", tm=512, tn=1024, into=(gba, 1024, GA_UP), name="mm_up_l1_dw")
    dh3 = _mm(du1, w_up, "nt", tm=1024, tn=512, b_layer=1, name="mm_up_l1_dx")
    g3, dmix1, dnw12, dnw11 = _pre_post_bwd(x3, nw(1, 2), dh3, g4, mix1, nw(1, 1), "pre_post_bwd_l1_mlp")
    gba = _mm(cat1, dmix1, "tn", tm=128, tn=1024, into=(gba, 256, GA_OUT_O), name="mm_out_odd_dw")
    dcat1 = _mm(dmix1, w_out_o, "nt", tm=2048, tn=512, name="mm_out_odd_dx")

    dq_c, dkpad, dvpad, dbias = _ca_bwd(proj_o, kvpad, bias, dcat1)
    g_rel = _bias_grad(jnp.pad(dbias.transpose(1, 0, 2), ((0, 0), (0, 0), (0, BIAS_W - CA_BAND))))[:, :257]
    dhh, dgate = _lru_post_bwd(hh, proj_o, dcat1)
    da_l, db_l = _lru_scan_bwd(_shift_up(lru_a, 1), _shift_down(hh, 1), dhh)
    dxs, g_cw, g_cb, g_wa_bd, g_lba, g_wx_bd, g_lbx, g_lam = _lru_pre_bwd(xs, cw, cb, wa_bd, lba, wx_bd, lbx, lam, da_l, db_l)
    dx_in = _conv_dx(jnp.stack([_shift_up(dxs[j], 3 - j) for j in range(4)]))
    dproj_o = jnp.concatenate([dq_c, dkpad[CA_PAD:], dvpad[CA_PAD:], dgate, dx_in], axis=1).astype(BF16)
    gba = _mm(dproj_o, h2, "tn", tm=128, tn=1024, into=(gba, 640, GA_IN_O), name="mm_in_odd_dw")
    swap_a = _pair_swap_start(gba, "a")
    dh2 = _mm(dproj_o, w_in_o, "nt", tm=1024, tn=512, name="mm_in_odd_dx")
    g2, dd0, dnw10, dnw03 = _pre_post_bwd(x2, nw(1, 0) + swap_a[4][0, 0], dh2, g3, d0, nw(0, 3), "pre_post_bwd_l1_mix")
    rs_a = rs_begin(swap_a, g2, "a")

    gbb = lax.empty((4, GB_ROWS, D), BF16)
    gbb = _mm(a0, dd0, "tn", tm=512, tn=1024, into=(gbb, 1024, GB_DN), name="mm_down_l0_dw")
    du0 = _mm(dd0, w_dn, "nt", tm=2048, tn=1024, b_layer=0, times2=r0, out_dtype=BF16, name="mm_down_l0_dx")
    gbb = _mm(du0, h1, "tn", tm=512, tn=1024, into=(gbb, 1024, GB_UP), name="mm_up_l0_dw")
    swap_b = _pair_swap_start(gbb, "b")
    dh1 = _mm(du0, w_up, "nt", tm=1024, tn=512, b_layer=0, name="mm_up_l0_dx")
    g1, dmix0, dnw02, dnw01 = _pre_post_bwd(x1, nw(0, 2) + (swap_b[4][0, 0] + rs_a[4][0, 0]), dh1, g2, mix0, nw(0, 1),
                                            "pre_post_bwd_l0_mlp")
    rs_b = rs_begin(swap_b, g1, "b")
    gbc = lax.empty((4, GC_ROWS, D), BF16)
    gbc = _mm(cat0, dmix0, "tn", tm=128, tn=1024, into=(gbc, 256, GC_OUT_E), name="mm_out_even_dw")
    dcat0 = _mm(dmix0, w_out_e, "nt", tm=2048, tn=512, name="mm_out_even_dx")

    dq_g, dk_g, dv_g, dr_g, daux_g, g_wa_pad, g_gla_ba, g_gla_nw = _gla_bwd(
        proj_e, s_prev, wa_pad, gla_ba, gla_nw + rs_b[4][0, 0], dcat0)
    dq_f, dk_f, dv_f, dccol = _fox_bwd(proj_e, cum_c, dcat0)
    dccol_t = jnp.pad(dccol.sum(axis=0).T, ((0, 0), (FOX_LANE0, 128 - FOX_LANE0 - 8)))
    daux, g_fox_bpad = _fox_gate_bwd(proj_e, fox_bpad, dccol_t, daux_g)
    dproj_e = jnp.concatenate([dq_g, dk_g, dv_g, dr_g, dq_f, dk_f, dv_f, daux], axis=1).astype(BF16)
    gt_in_e = _mm(dproj_e, h0, "tn", tm=640, tn=1024, out_dtype=BF16, name="mm_in_even_dw")
    dh0 = _mm(dproj_e, w_in_e_t, "nn", tm=1024, tn=512, name="mm_in_even_dx")
    grad_x, dnw00 = _norm_bwd(x0, nw(0, 0), dh0, g1, "prenorm_l0_mix_bwd")

    def rs_reduce(started, after, tag):
        send_sems, recv_sems, p, q, _ = started
        p, q = _a2a_wait(send_sems, recv_sems, p, q, after, tag)
        return _handover_start(_sum_chips(p, q, where, tag), tag)

    ho_a = rs_reduce(rs_a, grad_x, "a")
    ho_b = rs_reduce(rs_b, ho_a[3], "b")

    g_norm = jnp.stack([jnp.concatenate([dnw00, dnw01, dnw02, dnw03]), jnp.concatenate([dnw10, dnw11, dnw12, dnw13])])
    sharded = [(g_norm, 2), (g_wa_pad[:16], 1), (g_cw, 1), (g_cb[0], 0), (g_lba[0], 0), (g_lbx[0], 0), (g_lam[0], 0)]
    replicated = [g_gla_ba[0], g_gla_nw[0], g_fox_bpad[0, FOX_LANE0:FOX_LANE0 + 8], g_rel, _diag_blocks(g_wa_bd),
                  _diag_blocks(g_wx_bd)]
    small4 = jnp.concatenate([_shard_major(g, ax) for g, ax in sharded]
                             + [jnp.broadcast_to(g.reshape(1, -1), (4, g.size)) for g in replicated], axis=1)
    n_small = small4.shape[1]
    small_rows = GC_ROWS - GC_TAIL - 774
    small4 = jnp.pad(small4, ((0, 0), (0, small_rows * D - n_small))).reshape(4, small_rows, D)
    gt_rows = jnp.concatenate([gt_in_e[:1536], gt_in_e[3072:3088], gt_in_e[1536:3072], gt_in_e[3088:3096]], axis=0)
    tail = jnp.concatenate([gt_rows.reshape(4, 774, D), small4.astype(BF16)], axis=1)
    gbc = lax.dynamic_update_slice(gbc, tail, (0, GC_TAIL, 0))
    swap_c = _pair_swap_start(gbc, "c")
    rs_c = rs_begin(swap_c, swap_c[4], "c")

    red_a = _handover_wait(ho_a, rs_c[4], "a")
    red_b = _handover_wait(ho_b, red_a, "b")
    early = dict(
        w_mlp_up=_adamw_from(w_mlp_up, m_w_mlp_up, v_w_mlp_up, [(red_b, GB_UP, True), (red_a, GA_UP, True)], 256,
                             "adamw_w_mlp_up"),
        w_mlp_down=_adamw_from(w_mlp_down, m_w_mlp_down, v_w_mlp_down, [(red_b, GB_DN, False), (red_a, GA_DN, False)],
                               256, "adamw_w_mlp_down"),
        w_in_odd=_adamw_from(w_in_odd, m_w_in_odd, v_w_in_odd, [(red_a, GA_IN_O, True)], 256, "adamw_w_in_odd"),
        w_out_odd=_adamw_from(w_out_odd, m_w_out_odd, v_w_out_odd, [(red_a, GA_OUT_O, False)], 128, "adamw_w_out_odd"))
    red_c = rs_end(rs_c, early["w_out_odd"][3], "c")

    g_small = _split(red_c[GC_TAIL + 774:].reshape(-1)[:n_small], SMALL_SHARDED_SHAPES + REPL_SHAPES)
    g_of = dict(zip(["norm_w", "gla_w_a_up", "conv_w", "conv_b", "lru_b_a", "lru_b_x", "lru_lambda", "gla_b_a",
                     "gla_norm_w", "fox_b_f", "rel_bias", "lru_w_a", "lru_w_x"], g_small))
    g_of.update(w_in_even=red_c[GC_TAIL:GC_TAIL + 774])
    early["w_out_even"] = _adamw_from(w_out_even, m_w_out_even, v_w_out_even, [(red_c, GC_OUT_E, False)], 256,
                                      "adamw_w_out_even")

    names = ["norm_w", "w_in_even", "gla_w_a_up", "gla_b_a", "gla_norm_w", "fox_b_f", "w_out_even", "w_in_odd", "rel_bias",
             "conv_w", "conv_b", "lru_w_a", "lru_b_a", "lru_w_x", "lru_b_x", "lru_lambda", "w_out_odd", "w_mlp_up",
             "w_mlp_down"]
    w_of = dict(norm_w=norm_w, w_in_even=w_in_even, gla_w_a_up=gla_w_a_up, gla_b_a=gla_b_a, gla_norm_w=gla_norm_w,
                fox_b_f=fox_b_f, w_out_even=w_out_even, w_in_odd=w_in_odd, rel_bias=rel_bias, conv_w=conv_w, conv_b=conv_b,
                lru_w_a=lru_w_a, lru_b_a=lru_b_a, lru_w_x=lru_w_x, lru_b_x=lru_b_x, lru_lambda=lru_lambda,
                w_out_odd=w_out_odd, w_mlp_up=w_mlp_up, w_mlp_down=w_mlp_down)
    m_of = dict(norm_w=m_norm_w, w_in_even=m_w_in_even, gla_w_a_up=m_gla_w_a_up, gla_b_a=m_gla_b_a,
                gla_norm_w=m_gla_norm_w, fox_b_f=m_fox_b_f, w_out_even=m_w_out_even, w_in_odd=m_w_in_odd,
                rel_bias=m_rel_bias, conv_w=m_conv_w, conv_b=m_conv_b, lru_w_a=m_lru_w_a, lru_b_a=m_lru_b_a,
                lru_w_x=m_lru_w_x, lru_b_x=m_lru_b_x, lru_lambda=m_lru_lambda, w_out_odd=m_w_out_odd,
                w_mlp_up=m_w_mlp_up, w_mlp_down=m_w_mlp_down)
    v_of = dict(norm_w=v_norm_w, w_in_even=v_w_in_even, gla_w_a_up=v_gla_w_a_up, gla_b_a=v_gla_b_a,
                gla_norm_w=v_gla_norm_w, fox_b_f=v_fox_b_f, w_out_even=v_w_out_even, w_in_odd=v_w_in_odd,
                rel_bias=v_rel_bias, conv_w=v_conv_w, conv_b=v_conv_b, lru_w_a=v_lru_w_a, lru_b_a=v_lru_b_a,
                lru_w_x=v_lru_w_x, lru_b_x=v_lru_b_x, lru_lambda=v_lru_lambda, w_out_odd=v_w_out_odd,
                w_mlp_up=v_w_mlp_up, w_mlp_down=v_w_mlp_down)
    grads, deltas, new_ms, new_vs = [], [], [], []
    for n in names:
        w = w_of[n]
        if n in early:
            g, d, mn, vn = early[n]
            grads.append(g)
            deltas.append(d)
            new_ms.append(mn)
            new_vs.append(vn)
            continue
        if n == "w_in_even":
            to_view = lambda a: a[0].T
            from_view = lambda a: a.T[None]
        else:
            view = w.shape if w.ndim <= 3 else w.shape[-3:]
            to_view = lambda a, view=view: a.reshape(view)
            from_view = lambda a, w=w: a.reshape(w.shape)
        g = g_of[n] if n == "w_in_even" else to_view(g_of[n])
        d, mn, vn = _adamw(to_view(w), g, to_view(m_of[n]), to_view(v_of[n]), "adamw_" + n)
        grads.append(from_view(g))
        deltas.append(from_view(d))
        new_ms.append(from_view(mn))
        new_vs.append(from_view(vn))

    return (loss, grad_x.reshape(1, T, D), *grads, *deltas, *new_ms, *new_vs)
```

```python
import functools

import jax
import jax.numpy as jnp
from jax import lax
from jax.experimental import pallas as pl
from jax.experimental.pallas import tpu as pltpu

F32 = jnp.float32
BF16 = jnp.bfloat16
MESH = pl.DeviceIdType.MESH

T = 2048
D = 1024
DFF = 4096
EPS = 1e-6
CHUNK = 64
NCHUNK = T // CHUNK
PE = 3200
PO = 2560
AUX_BLK = 3072 // 128
FOX_LANE0 = 16
GLA_SCALE = 64 ** -0.5
ATT_SCALE = 64 ** -0.5
NEG = float(jnp.finfo(jnp.float32).min)
CA_BAND = 576
CA_PAD = 512
REL_PAD = 384

VMEM_LIMIT = 48 * 1024 * 1024

ADAM_LR, ADAM_B1, ADAM_B2, ADAM_EPS, ADAM_WD, ADAM_STEP = 0.001, 0.9, 0.999, 1e-08, 0.01, 10

GA_ROWS, GA_IN_O, GA_OUT_O, GA_GAP, GA_UP, GA_DN = 3072, 0, 640, 896, 1024, 2048
GB_ROWS, GB_UP, GB_DN = 2048, 0, 1024
GC_ROWS, GC_OUT_E, GC_TAIL = 1152, 0, 256

_DIMS = {"nn": (((1,), (0,)), ((), ())), "nt": (((1,), (1,)), ((), ())), "tn": (((0,), (0,)), ((), ()))}


def _cp(sem, **kw):
    return pltpu.CompilerParams(dimension_semantics=sem, vmem_limit_bytes=VMEM_LIMIT, **kw)


def _dot(a, b, mode):
    return lax.dot_general(a.astype(BF16), b.astype(BF16), _DIMS[mode], preferred_element_type=F32)


@functools.partial(jax.custom_vjp, nondiff_argnums=(2,))
def bdot(a, b, mode):
    return _dot(a, b, mode)


def _bdot_fwd(a, b, mode):
    return _dot(a, b, mode), (a, b)


def _bdot_bwd(mode, res, g):
    a, b = res
    if mode == "nn":
        da, db = _dot(g, b, "nt"), _dot(a, g, "tn")
    elif mode == "nt":
        da, db = _dot(g, b, "nn"), _dot(g, a, "tn")
    else:
        da, db = _dot(b, g, "nt"), _dot(a, g, "nn")
    return da.astype(a.dtype), db.astype(b.dtype)


bdot.defvjp(_bdot_fwd, _bdot_bwd)


def _hdot_raw(a, b, mode):
    return lax.dot_general(a, b, _DIMS[mode], precision=lax.Precision.HIGHEST, preferred_element_type=F32)


def _log_sigmoid(x):
    return jnp.minimum(x, 0.0) - jnp.log(1.0 + jnp.exp(-jnp.abs(x)))


def _sigmoid(x):
    return 1.0 / (1.0 + jnp.exp(-x))


def _expm1(x):
    series = x * (1.0 + x * 0.5 * (1.0 + x * (1.0 / 3.0) * (1.0 + x * 0.25)))
    return jnp.where(jnp.abs(x) < 0.03, series, jnp.exp(x) - 1.0)


def _gelu_tanh(x):
    return 0.5 * x * (1.0 + jnp.tanh(0.7978845608028654 * (x + 0.044715 * x * x * x)))


def _iota(shape, dim):
    return lax.broadcasted_iota(jnp.int32, shape, dim)


def _mm(a, b, mode, *, tm, tn, tk=None, out_dtype=F32, name, b_layer=None, into=None, relu_pair=False, times2=None):
    b2 = b.shape[-2:]
    if mode == "nn":
        (m, k), n = a.shape, b2[1]
    elif mode == "nt":
        (m, k), n = a.shape, b2[0]
    else:
        (k, m), n = a.shape, b2[1]
    tk = k if tk is None else tk
    assert m % tm == 0 and n % tn == 0 and k % tk == 0, (name, a.shape, b.shape)
    nk = k // tk
    if mode == "tn":
        a_spec = pl.BlockSpec((tk, tm), lambda i, j, kk: (kk, i))
    elif m == tm and nk == 1:
        a_spec = pl.BlockSpec((tm, tk), lambda i, j, kk: (i, kk), pipeline_mode=pl.Buffered(1))
    else:
        a_spec = pl.BlockSpec((tm, tk), lambda i, j, kk: (i, kk))
    b_blk = {"nn": (tk, tn), "nt": (tn, tk), "tn": (tk, tn)}[mode]
    b_idx = {"nn": lambda i, j, kk: (kk, j), "nt": lambda i, j, kk: (j, kk), "tn": lambda i, j, kk: (kk, j)}[mode]
    if b_layer is None:
        b_spec = pl.BlockSpec(b_blk, b_idx)
    else:
        b_spec = pl.BlockSpec((None,) + b_blk, lambda i, j, kk: (b_layer,) + b_idx(i, j, kk))

    tile = pl.BlockSpec((tm, tn), lambda i, j, kk: (i, j))
    if into is not None:
        buf, per_slot, row_off = into
        assert m == 4 * per_slot and per_slot % tm == 0 and row_off % tm == 0 and buf.shape[2] == n, (name, buf.shape)
        bps = per_slot // tm
        out_specs = pl.BlockSpec((None, tm, tn), lambda i, j, kk: (i // bps, row_off // tm + i % bps, j))
        out_shape = jax.ShapeDtypeStruct(buf.shape, buf.dtype)
        extra_in, extra_specs, aliases = [buf], [pl.BlockSpec(memory_space=pl.ANY)], {2: 0}
        finish = lambda acc, extra: [acc.astype(buf.dtype)]
    elif relu_pair:
        out_specs = (tile, tile)
        out_shape = (jax.ShapeDtypeStruct((m, n), BF16),) * 2
        extra_in, extra_specs, aliases = [], [], {}

        def finish(acc, extra):
            r = jnp.maximum(acc, 0.0)
            return [(r * r).astype(BF16), r.astype(BF16)]
    elif times2 is not None:
        out_specs = tile
        out_shape = jax.ShapeDtypeStruct((m, n), out_dtype)
        extra_in, extra_specs, aliases = [times2], [tile], {}
        finish = lambda acc, extra: [(acc * (2.0 * extra[...].astype(F32))).astype(out_dtype)]
    else:
        out_specs = tile
        out_shape = jax.ShapeDtypeStruct((m, n), out_dtype)
        extra_in, extra_specs, aliases = [], [], {}
        finish = lambda acc, extra: [acc.astype(out_dtype)]
    n_out = 2 if relu_pair else 1

    def body(*refs):
        a_ref, b_ref = refs[0], refs[1]
        extra = refs[2] if extra_in else None
        o_refs = refs[2 + len(extra_in):2 + len(extra_in) + n_out]

        def store(acc):
            for o_ref, val in zip(o_refs, finish(acc, extra)):
                o_ref[...] = val

        if nk == 1:
            store(_dot(a_ref[...], b_ref[...], mode))
            return
        acc_ref = refs[-1]
        kk = pl.program_id(2)

        @pl.when(kk == 0)
        def _():
            acc_ref[...] = jnp.zeros_like(acc_ref)

        acc_ref[...] += _dot(a_ref[...], b_ref[...], mode)

        @pl.when(kk == nk - 1)
        def _():
            store(acc_ref[...])

    return pl.pallas_call(
        body, name=name, grid=(m // tm, n // tn, nk),
        in_specs=[a_spec, b_spec] + extra_specs,
        out_specs=out_specs, out_shape=out_shape,
        scratch_shapes=[pltpu.VMEM((tm, tn), F32)] if nk > 1 else [],
        input_output_aliases=aliases,
        compiler_params=_cp(("parallel", "parallel", "arbitrary")),
    )(a, b, *extra_in)


ROWS = 512


def _prenorm(x, w, name):
    def body(x_ref, w_ref, o_ref):
        xv = x_ref[...]
        r = lax.rsqrt(jnp.mean(xv * xv, axis=-1, keepdims=True) + EPS)
        o_ref[...] = (xv * r * w_ref[...]).astype(BF16)

    return pl.pallas_call(
        body, name=name, grid=(T // ROWS,),
        in_specs=[pl.BlockSpec((ROWS, D), lambda i: (i, 0)), pl.BlockSpec((1, D), lambda i: (0, 0))],
        out_specs=pl.BlockSpec((ROWS, D), lambda i: (i, 0)),
        out_shape=jax.ShapeDtypeStruct((T, D), BF16),
        compiler_params=_cp(("parallel",)),
    )(x, w)


def _rms(z):
    return lax.rsqrt(jnp.mean(z * z, axis=-1, keepdims=True) + EPS)


def _rms_bwd(z, w, dy):
    r = _rms(z)
    wdy = dy * w
    dz = r * wdy - z * (r * r * r) * jnp.mean(z * wdy, axis=-1, keepdims=True)
    return dz, jnp.sum(dy * z * r, axis=0, keepdims=True)


_ROW = pl.BlockSpec((ROWS, D), lambda i: (i, 0))
_VEC = pl.BlockSpec((1, D), lambda i: (0, 0))


def _post_pre_fwd(x, z, w_post, w_pre, name):
    def body(x_ref, z_ref, wp_ref, wn_ref, x_out, h_out):
        zv = z_ref[...]
        xn = x_ref[...] + zv * _rms(zv) * wp_ref[...]
        x_out[...] = xn
        h_out[...] = (xn * _rms(xn) * wn_ref[...]).astype(BF16)

    return pl.pallas_call(
        body, name=name, grid=(T // ROWS,), in_specs=[_ROW, _ROW, _VEC, _VEC], out_specs=(_ROW, _ROW),
        out_shape=(jax.ShapeDtypeStruct((T, D), F32), jax.ShapeDtypeStruct((T, D), BF16)),
        compiler_params=_cp(("parallel",)),
    )(x, z, w_post, w_pre)


def _post_loss(x, z, w_post, tgt):
    def body(x_ref, z_ref, w_ref, t_ref, g_ref, l_ref, dz_ref, dw_ref):
        @pl.when(pl.program_id(0) == 0)
        def _():
            l_ref[...] = jnp.zeros_like(l_ref)
            dw_ref[...] = jnp.zeros_like(dw_ref)

        zv = z_ref[...]
        e = x_ref[...] + zv * _rms(zv) * w_ref[...] - t_ref[...]
        g = e * (1.0 / D)
        g_ref[...] = g
        l_ref[...] += jnp.sum(e * e) * (0.5 / D)
        dz, dw = _rms_bwd(zv, w_ref[...], g)
        dz_ref[...] = dz.astype(BF16)
        dw_ref[...] += dw

    return pl.pallas_call(
        body, name="postnorm_loss", grid=(T // ROWS,), in_specs=[_ROW, _ROW, _VEC, _ROW],
        out_specs=(_ROW, pl.BlockSpec((1, 128), lambda i: (0, 0)), _ROW, _VEC),
        out_shape=(jax.ShapeDtypeStruct((T, D), F32), jax.ShapeDtypeStruct((1, 128), F32),
                   jax.ShapeDtypeStruct((T, D), BF16), jax.ShapeDtypeStruct((1, D), F32)),
        compiler_params=_cp(("arbitrary",)),
    )(x, z, w_post, tgt)


def _pre_post_bwd(x, w_pre, dh, add, z, w_post, name):
    def body(x_ref, wn_ref, dh_ref, add_ref, z_ref, wp_ref, g_ref, dz_ref, dwn_ref, dwp_ref):
        @pl.when(pl.program_id(0) == 0)
        def _():
            dwn_ref[...] = jnp.zeros_like(dwn_ref)
            dwp_ref[...] = jnp.zeros_like(dwp_ref)

        dx, dwn = _rms_bwd(x_ref[...], wn_ref[...], dh_ref[...])
        g = dx + add_ref[...]
        g_ref[...] = g
        dz, dwp = _rms_bwd(z_ref[...], wp_ref[...], g)
        dz_ref[...] = dz.astype(BF16)
        dwn_ref[...] += dwn
        dwp_ref[...] += dwp

    return pl.pallas_call(
        body, name=name, grid=(T // ROWS,), in_specs=[_ROW, _VEC, _ROW, _ROW, _ROW, _VEC],
        out_specs=(_ROW, _ROW, _VEC, _VEC),
        out_shape=(jax.ShapeDtypeStruct((T, D), F32), jax.ShapeDtypeStruct((T, D), BF16),
                   jax.ShapeDtypeStruct((1, D), F32), jax.ShapeDtypeStruct((1, D), F32)),
        compiler_params=_cp(("arbitrary",)),
    )(x, w_pre, dh, add, z, w_post)


def _norm_bwd(z, w, dy, add, name):
    has_add = add is not None

    def body(*refs):
        if has_add:
            z_ref, w_ref, dy_ref, add_ref, dz_ref, dw_ref = refs
        else:
            z_ref, w_ref, dy_ref, dz_ref, dw_ref = refs
        i = pl.program_id(0)

        @pl.when(i == 0)
        def _():
            dw_ref[...] = jnp.zeros_like(dw_ref)

        zv = z_ref[...].astype(F32)
        dyv = dy_ref[...]
        r = lax.rsqrt(jnp.mean(zv * zv, axis=-1, keepdims=True) + EPS)
        wdy = dyv * w_ref[...]
        dz = r * wdy - zv * (r * r * r) * jnp.mean(zv * wdy, axis=-1, keepdims=True)
        if has_add:
            dz = dz + add_ref[...]
        dz_ref[...] = dz.astype(dz_ref.dtype)
        dw_ref[...] += jnp.sum(dyv * zv * r, axis=0, keepdims=True)

    row = pl.BlockSpec((ROWS, D), lambda i: (i, 0))
    vec = pl.BlockSpec((1, D), lambda i: (0, 0))
    ins = [z, w, dy] + ([add] if has_add else [])
    dz_dtype = F32 if has_add else BF16
    return pl.pallas_call(
        body, name=name, grid=(T // ROWS,),
        in_specs=[row, vec, row] + ([row] if has_add else []),
        out_specs=(row, vec),
        out_shape=(jax.ShapeDtypeStruct((T, D), dz_dtype), jax.ShapeDtypeStruct((1, D), F32)),
        compiler_params=_cp(("arbitrary",)),
    )(*ins)


def _adamw_math(w, g, m, v):
    c1 = 1.0 - ADAM_B1 ** ADAM_STEP
    c2 = 1.0 - ADAM_B2 ** ADAM_STEP
    mn = ADAM_B1 * m + (1.0 - ADAM_B1) * g
    vn = ADAM_B2 * v + (1.0 - ADAM_B2) * (g * g)
    return -ADAM_LR * ((mn / c1) / (jnp.sqrt(vn / c2) + ADAM_EPS) + ADAM_WD * w), mn, vn


def _adamw_from(w, m, v, sources, tr, name):
    layers, rows, cols = w.shape
    assert len(sources) == layers and rows % tr == 0, (name, w.shape)
    g_specs = []
    for layer, (buf, row0, transposed) in enumerate(sources):
        step = lambda l, i, layer=layer: jnp.where(l == layer, i, 0)
        if transposed:
            assert row0 % cols == 0 and buf.shape[1] == rows, (name, row0)
            g_specs.append(pl.BlockSpec((cols, tr), lambda l, i, b=row0 // cols, step=step: (b, step(l, i))))
        else:
            assert row0 % tr == 0 and buf.shape[1] == cols, (name, row0)
            g_specs.append(pl.BlockSpec((tr, cols), lambda l, i, b=row0 // tr, step=step: (b + step(l, i), 0)))

    def body(*refs):
        w_ref, m_ref, v_ref = refs[:3]
        g_refs = refs[3:3 + layers]
        g_out, d_ref, mo_ref, vo_ref = refs[3 + layers:]
        gs = [r[...].T if src[2] else r[...] for r, src in zip(g_refs, sources)]
        g = gs[0] if layers == 1 else jnp.where(pl.program_id(0) == 0, gs[0], gs[1])
        g_out[...] = g
        d_ref[...], mo_ref[...], vo_ref[...] = _adamw_math(w_ref[...], g, m_ref[...], v_ref[...])

    blk = pl.BlockSpec((None, tr, cols), lambda l, i: (l, i, 0))
    sds = jax.ShapeDtypeStruct(w.shape, F32)
    return pl.pallas_call(body, name=name, grid=(layers, rows // tr), in_specs=[blk] * 3 + g_specs,
                          out_specs=(blk,) * 4, out_shape=(sds,) * 4,
                          compiler_params=_cp(("parallel", "parallel")))(w, m, v, *[s[0] for s in sources])


def _adamw(w, g, m, v, name):
    lead = w.shape[:-2]
    assert len(lead) <= 1 and g.shape == w.shape, (name, w.shape, g.shape)
    rows, cols = w.shape[-2:]
    if rows <= 512:
        tr, tc = rows, cols
    elif rows % 256 == 0:
        tr, tc = 256, cols
    else:
        tr, tc = rows, 256
    assert rows % tr == 0 and cols % tc == 0, (name, w.shape)
    c1 = 1.0 - ADAM_B1 ** ADAM_STEP
    c2 = 1.0 - ADAM_B2 ** ADAM_STEP

    def body(w_ref, g_ref, m_ref, v_ref, d_ref, mo_ref, vo_ref):
        gv = g_ref[...]
        mn = ADAM_B1 * m_ref[...] + (1.0 - ADAM_B1) * gv
        vn = ADAM_B2 * v_ref[...] + (1.0 - ADAM_B2) * (gv * gv)
        m_hat = mn / c1
        v_hat = vn / c2
        d_ref[...] = -ADAM_LR * (m_hat / (jnp.sqrt(v_hat) + ADAM_EPS) + ADAM_WD * w_ref[...])
        mo_ref[...] = mn
        vo_ref[...] = vn

    if lead:
        grid = (lead[0], rows // tr, cols // tc)
        blk = pl.BlockSpec((None, tr, tc), lambda l, i, j: (l, i, j))
    else:
        grid = (rows // tr, cols // tc)
        blk = pl.BlockSpec((tr, tc), lambda i, j: (i, j))
    sds = jax.ShapeDtypeStruct(w.shape, F32)
    return pl.pallas_call(body, name=name, grid=grid, in_specs=[blk] * 4, out_specs=(blk,) * 3,
                          out_shape=(sds,) * 3, compiler_params=_cp(("parallel",) * len(grid)))(w, g, m, v)


def _running_sum(x, towards_later):
    n = x.shape[0]
    row = _iota(x.shape, 0)
    s = 1
    while s < n:
        if towards_later:
            x = x + jnp.where(row >= s, pltpu.roll(x, s, 0), 0.0)
        else:
            x = x + jnp.where(row < n - s, pltpu.roll(x, n - s, 0), 0.0)
        s *= 2
    return x


@jax.custom_vjp
def _cumsum_rows(x):
    return _running_sum(x, True)


_cumsum_rows.defvjp(lambda x: (_running_sum(x, True), None), lambda _, g: (_running_sum(g, False),))


def _gla_consts():
    return (_iota((256, 512), 0) // 64 == _iota((256, 512), 1) // 128).astype(F32)


def _gla_chunk(mask, q, k, v, r, aux, s_prev, wa, ba, nw):
    la = _log_sigmoid(bdot(aux, wa, "nn") + ba) * (1.0 / 16.0)
    cum = _cumsum_rows(la)
    total = jnp.sum(la, axis=0, keepdims=True)
    k_dec = k * jnp.exp(total - cum)
    inc = bdot(k_dec, v, "tn") * mask
    dec = jnp.exp(jnp.broadcast_to(total, (128, 256)).T)
    dec = jnp.concatenate([dec, dec, dec, dec], axis=1)
    s_new = dec * s_prev + inc
    o = bdot(q * GLA_SCALE, s_new, "nn")
    parts = []
    for h in range(4):
        oh = o[:, h * 128:(h + 1) * 128]
        parts.append(oh * lax.rsqrt(jnp.mean(oh * oh, axis=-1, keepdims=True) + EPS))
    on = jnp.concatenate(parts, axis=1)
    return s_new, on * nw * (r * _sigmoid(r))


GLA_PER_STEP = 4
GLA_ROWS = GLA_PER_STEP * CHUNK
GLA_STEPS = NCHUNK // GLA_PER_STEP


def _gla_specs(cmap):
    return [pl.BlockSpec((GLA_ROWS, 256), lambda c: (cmap(c), 0)),
            pl.BlockSpec((GLA_ROWS, 256), lambda c: (cmap(c), 1)),
            pl.BlockSpec((GLA_ROWS, 512), lambda c: (cmap(c), 1)),
            pl.BlockSpec((GLA_ROWS, 512), lambda c: (cmap(c), 2)),
            pl.BlockSpec((GLA_ROWS, 128), lambda c: (cmap(c), AUX_BLK))]


def _gla_fwd(proj, wa, ba, nw):
    def body(q_ref, k_ref, v_ref, r_ref, aux_ref, wa_ref, ba_ref, nw_ref, o_ref, sp_ref, s_ref):
        @pl.when(pl.program_id(0) == 0)
        def _():
            s_ref[...] = jnp.zeros_like(s_ref)

        s = s_ref[...]
        consts = _gla_consts()
        outs, states = [], []
        for i in range(GLA_PER_STEP):
            rows = slice(i * CHUNK, (i + 1) * CHUNK)
            states.append(s)
            s, out = _gla_chunk(consts, q_ref[rows, :], k_ref[rows, :], v_ref[rows, :], r_ref[rows, :], aux_ref[rows, :],
                                s, wa_ref[...], ba_ref[...], nw_ref[...])
            outs.append(out)
        s_ref[...] = s
        for i in range(GLA_PER_STEP):
            o_ref[i * CHUNK:(i + 1) * CHUNK, :] = outs[i]
            sp_ref[i] = states[i]

    full = lambda shape: pl.BlockSpec(shape, lambda c: (0,) * len(shape))
    return pl.pallas_call(
        body, name="gla_fwd", grid=(GLA_STEPS,),
        in_specs=_gla_specs(lambda c: c) + [full((128, 256)), full((1, 256)), full((1, 512))],
        out_specs=(pl.BlockSpec((GLA_ROWS, 512), lambda c: (c, 0)),
                   pl.BlockSpec((GLA_PER_STEP, 256, 512), lambda c: (c, 0, 0))),
        out_shape=(jax.ShapeDtypeStruct((T, D), F32), jax.ShapeDtypeStruct((NCHUNK, 256, 512), F32)),
        scratch_shapes=[pltpu.VMEM((256, 512), F32)],
        compiler_params=_cp(("arbitrary",)),
    )(proj, proj, proj, proj, proj, wa, ba, nw)


def _gla_bwd(proj, s_prev_all, wa, ba, nw, dcat):
    rev = lambda c: GLA_STEPS - 1 - c

    def body(q_ref, k_ref, v_ref, r_ref, aux_ref, sp_ref, wa_ref, ba_ref, nw_ref, do_ref,
             dq_ref, dk_ref, dv_ref, dr_ref, daux_ref, dwa_ref, dba_ref, dnw_ref, ds_ref):
        @pl.when(pl.program_id(0) == 0)
        def _():
            ds_ref[...] = jnp.zeros_like(ds_ref)
            dwa_ref[...] = jnp.zeros_like(dwa_ref)
            dba_ref[...] = jnp.zeros_like(dba_ref)
            dnw_ref[...] = jnp.zeros_like(dnw_ref)

        fn = functools.partial(_gla_chunk, _gla_consts())
        ds = ds_ref[...]
        dwa, dba, dnw = dwa_ref[...], dba_ref[...], dnw_ref[...]
        grads = {}
        for i in reversed(range(GLA_PER_STEP)):
            rows = slice(i * CHUNK, (i + 1) * CHUNK)
            _, vjp = jax.vjp(fn, q_ref[rows, :], k_ref[rows, :], v_ref[rows, :], r_ref[rows, :], aux_ref[rows, :],
                             sp_ref[i], wa_ref[...], ba_ref[...], nw_ref[...])
            *grads[i], ds, dwa_i, dba_i, dnw_i = vjp((ds, do_ref[rows, :]))
            dwa, dba, dnw = dwa + dwa_i, dba + dba_i, dnw + dnw_i
        ds_ref[...] = ds
        dwa_ref[...] = dwa
        dba_ref[...] = dba
        dnw_ref[...] = dnw
        for i in range(GLA_PER_STEP):
            rows = slice(i * CHUNK, (i + 1) * CHUNK)
            for ref, g in zip((dq_ref, dk_ref, dv_ref, dr_ref, daux_ref), grads[i]):
                ref[rows, :] = g

    full = lambda shape: pl.BlockSpec(shape, lambda c: (0,) * len(shape))
    blk = lambda w: pl.BlockSpec((GLA_ROWS, w), lambda c: (rev(c), 0))
    sds = lambda *s: jax.ShapeDtypeStruct(s, F32)
    return pl.pallas_call(
        body, name="gla_bwd", grid=(GLA_STEPS,),
        in_specs=_gla_specs(rev) + [pl.BlockSpec((GLA_PER_STEP, 256, 512), lambda c: (rev(c), 0, 0)),
                                    full((128, 256)), full((1, 256)), full((1, 512)), blk(512)],
        out_specs=(blk(256), blk(256), blk(512), blk(512), blk(128), full((128, 256)), full((1, 256)), full((1, 512))),
        out_shape=(sds(T, 256), sds(T, 256), sds(T, 512), sds(T, 512), sds(T, 128),
                   sds(128, 256), sds(1, 256), sds(1, 512)),
        scratch_shapes=[pltpu.VMEM((256, 512), F32)],
        compiler_params=_cp(("arbitrary",)),
    )(proj, proj, proj, proj, proj, s_prev_all, wa, ba, nw, dcat)


def _prefix8(x, towards_later):
    row = _iota(x.shape, 0)
    for s in (1, 2, 4):
        if towards_later:
            keep, shift = row >= s, s
        else:
            keep, shift = row < 8 - s, 8 - s
        x = x + jnp.where(keep, pltpu.roll(x, shift, 0), 0.0)
    return x


def _fox_gate_fwd(proj, bpad):
    def body(aux_ref, b_ref, cum_ref):
        cum_ref[...] = _log_sigmoid(aux_ref[...] + b_ref[...])

        def step(i, carry):
            rows = pl.ds(pl.multiple_of(i * 8, 8), 8)
            cum = _prefix8(cum_ref[rows, :], True) + carry
            cum_ref[rows, :] = cum
            return jnp.broadcast_to(cum[7:, :], (8, 128))

        lax.fori_loop(0, T // 8, step, jnp.zeros((8, 128), F32), unroll=4)

    return pl.pallas_call(
        body, name="fox_gate_fwd", grid=(1,),
        in_specs=[pl.BlockSpec((T, 128), lambda i: (0, AUX_BLK)), pl.BlockSpec((1, 128), lambda i: (0, 0))],
        out_specs=pl.BlockSpec((T, 128), lambda i: (0, 0)),
        out_shape=jax.ShapeDtypeStruct((T, 128), F32),
        compiler_params=_cp(("arbitrary",)),
    )(proj, bpad)


def _fox_gate_bwd(proj, bpad, dccol_t, daux_gla):
    def body(aux_ref, b_ref, dc_ref, dg_ref, daux_ref, db_ref):
        def step(i, carry):
            rows = pl.ds(pl.multiple_of(T - 8 * (i + 1), 8), 8)
            dlf = _prefix8(dc_ref[rows, :], False) + carry
            daux_ref[rows, :] = dlf
            return jnp.broadcast_to(dlf[:1, :], (8, 128))

        lax.fori_loop(0, T // 8, step, jnp.zeros((8, 128), F32), unroll=4)
        dz = daux_ref[...] * _sigmoid(-(aux_ref[...] + b_ref[...]))
        daux_ref[...] = dz + dg_ref[...]
        db_ref[...] = jnp.sum(dz, axis=0, keepdims=True)

    whole = pl.BlockSpec((T, 128), lambda i: (0, 0))
    vec = pl.BlockSpec((1, 128), lambda i: (0, 0))
    return pl.pallas_call(
        body, name="fox_gate_bwd", grid=(1,),
        in_specs=[pl.BlockSpec((T, 128), lambda i: (0, AUX_BLK)), vec, whole, whole],
        out_specs=(whole, vec),
        out_shape=(jax.ShapeDtypeStruct((T, 128), F32), jax.ShapeDtypeStruct((1, 128), F32)),
        compiler_params=_cp(("arbitrary",)),
    )(proj, bpad, dccol_t, daux_gla)


FOX_Q = 256


FOX_QB = T // FOX_Q


@jax.custom_vjp
def _attend(s, v):
    return _attend_fwd(s, v)[0]


def _attend_fwd(s, v):
    e = jnp.exp(s - jnp.max(s, axis=-1, keepdims=True))
    r = 1.0 / jnp.sum(e, axis=-1, keepdims=True)
    return _dot(e, v, "nn") * r, (e, r, v)


def _attend_bwd(res, do):
    e, r, v = res
    do_r = do * r
    dpr = _dot(do_r, v, "nt")
    ds = e * (dpr - r * jnp.sum(e * dpr, axis=-1, keepdims=True))
    return ds, _dot(e, do_r, "tn").astype(v.dtype)


_attend.defvjp(_attend_fwd, _attend_bwd)


def _fox_block(hp, q, k, v, ccol):
    kl = k.shape[0]
    lane = _iota((FOX_Q, 128), 1)
    tri = jnp.bitwise_and(_iota((2 * FOX_Q, FOX_Q), 0), FOX_Q - 1) >= _iota((2 * FOX_Q, FOX_Q), 1)
    sub = _iota((8, kl), 0)
    qs = q * ATT_SCALE
    q2 = jnp.concatenate([jnp.where(lane < 64, qs, 0.0), jnp.where(lane >= 64, qs, 0.0)], axis=0)
    s = bdot(q2, k, "nt")
    cs = [jnp.sum(jnp.where(sub == 2 * hp + e, ccol, 0.0), axis=0, keepdims=True) for e in range(2)]
    s = jnp.concatenate([s[:FOX_Q] - cs[0], s[FOX_Q:] - cs[1]], axis=0)
    diag = jnp.where(tri, s[:, kl - FOX_Q:], NEG)
    s = diag if kl == FOX_Q else jnp.concatenate([s[:, :kl - FOX_Q], diag], axis=1)
    o2 = _attend(s, v)
    return jnp.where(lane < 64, o2[:FOX_Q], o2[FOX_Q:])


def _fox_in_specs():
    return [pl.BlockSpec((FOX_Q, 128), lambda hp, qb: (qb, 12 + hp)),
            pl.BlockSpec((T, 128), lambda hp, qb: (0, 16 + hp)),
            pl.BlockSpec((T, 128), lambda hp, qb: (0, 20 + hp)),
            pl.BlockSpec((8, T), lambda hp, qb: (0, 0))]


def _fox_fwd(proj, cum_c, cat):
    def body(q_ref, k_ref, v_ref, cc_ref, cat_ref, o_ref):
        qb = pl.program_id(1)
        for g in range(FOX_QB):
            kl = FOX_Q * (g + 1)

            @pl.when(qb == g)
            def _(kl=kl):
                o_ref[...] = _fox_block(pl.program_id(0), q_ref[...], k_ref[0:kl, :], v_ref[0:kl, :], cc_ref[:, 0:kl])

    return pl.pallas_call(
        body, name="fox_fwd", grid=(4, FOX_QB), in_specs=_fox_in_specs() + [pl.BlockSpec(memory_space=pl.ANY)],
        out_specs=pl.BlockSpec((FOX_Q, 128), lambda hp, qb: (qb, 4 + hp)),
        out_shape=jax.ShapeDtypeStruct((T, D), F32), input_output_aliases={4: 0},
        compiler_params=_cp(("parallel", "parallel")),
    )(proj, proj, proj, cum_c, cat)


def _fox_bwd(proj, cum_c, dcat):
    def body(q_ref, k_ref, v_ref, cc_ref, do_ref, dq_ref, dk_ref, dv_ref, dcc_ref):
        qb = pl.program_id(1)

        @pl.when(qb == 0)
        def _():
            dk_ref[...] = jnp.zeros_like(dk_ref)
            dv_ref[...] = jnp.zeros_like(dv_ref)
            dcc_ref[...] = jnp.zeros_like(dcc_ref)

        fn = functools.partial(_fox_block, pl.program_id(0))
        for g in range(FOX_QB):
            kl = FOX_Q * (g + 1)

            @pl.when(qb == g)
            def _(kl=kl):
                _, vjp = jax.vjp(fn, q_ref[...], k_ref[0:kl, :], v_ref[0:kl, :], cc_ref[:, 0:kl])
                dq, dk, dv, dcc = vjp(do_ref[...])
                dq_ref[...] = dq
                dk_ref[0:kl, :] += dk
                dv_ref[0:kl, :] += dv
                dcc_ref[:, 0:kl] += dcc

    sds = lambda *s: jax.ShapeDtypeStruct(s, F32)
    return pl.pallas_call(
        body, name="fox_bwd", grid=(4, FOX_QB),
        in_specs=_fox_in_specs() + [pl.BlockSpec((FOX_Q, 128), lambda hp, qb: (qb, 4 + hp))],
        out_specs=(pl.BlockSpec((FOX_Q, 128), lambda hp, qb: (qb, hp)),
                   pl.BlockSpec((T, 128), lambda hp, qb: (0, hp)),
                   pl.BlockSpec((T, 128), lambda hp, qb: (0, hp)),
                   pl.BlockSpec((None, 8, T), lambda hp, qb: (hp, 0, 0))),
        out_shape=(sds(T, 512), sds(T, 512), sds(T, 512), sds(4, 8, T)),
        compiler_params=_cp(("parallel", "arbitrary")),
    )(proj, proj, proj, cum_c, dcat)


BIAS_W = 640


def _rel_onehot():
    j = _iota((REL_PAD, BIAS_W), 1)
    rel = jnp.clip(CA_PAD + CHUNK - 1 - j, -128, 128) + 128
    return (_iota((REL_PAD, BIAS_W), 0) == rel).astype(F32)


def _bias_build(rbp):
    def body(rb_ref, o_ref):
        f = _hdot_raw(rb_ref[...], _rel_onehot(), "nn")
        for q in range(CHUNK):
            o_ref[q] = pltpu.roll(f, (BIAS_W - (CHUNK - 1 - q)) % BIAS_W, 1)[:, :CA_BAND]

    return pl.pallas_call(body, name="ca_bias_build", out_shape=jax.ShapeDtypeStruct((CHUNK, 8, CA_BAND), F32))(rbp)


def _bias_grad(dbias_q):
    def body(db_ref, o_ref):
        acc = jnp.zeros((8, BIAS_W), F32)
        for q in range(CHUNK):
            acc = acc + pltpu.roll(db_ref[q], CHUNK - 1 - q, 1)
        o_ref[...] = _hdot_raw(acc, _rel_onehot(), "nt")

    return pl.pallas_call(body, name="ca_bias_grad", out_shape=jax.ShapeDtypeStruct((8, REL_PAD), F32))(dbias_q)


def _ca_block(c, masked, q, kb, vb, bias2):
    lane = _iota((CHUNK, 128), 1)
    qs = q * ATT_SCALE
    q2 = jnp.concatenate([jnp.where(lane < 64, qs, 0.0), jnp.where(lane >= 64, qs, 0.0)], axis=0)
    s = bdot(q2, kb, "nt") + bias2.reshape(2 * CHUNK, CA_BAND)
    if masked:
        s = jnp.where((c * CHUNK - CA_PAD + _iota((2 * CHUNK, CA_BAND), 1)) >= 0, s, NEG)
    o2 = _attend(s, vb)
    return jnp.where(lane < 64, o2[:CHUNK], o2[CHUNK:])


CA_PER_STEP = 8
CA_ROWS = CA_PER_STEP * CHUNK
CA_MASKED_STEPS = -(-CA_PAD // CA_ROWS)


def _ca_fwd(proj, kvpad, bias):
    def body(q_ref, k_ref, v_ref, b_ref, o_ref):
        def run(masked):
            outs = []
            for i in range(CA_PER_STEP):
                c = pl.program_id(1) * CA_PER_STEP + i
                band = pl.ds(pl.multiple_of(c * CHUNK, CHUNK), CA_BAND)
                rows = slice(i * CHUNK, (i + 1) * CHUNK)
                outs.append(_ca_block(c, masked, q_ref[rows, :], k_ref[band, :], v_ref[band, :], b_ref[...]))
            for i in range(CA_PER_STEP):
                o_ref[i * CHUNK:(i + 1) * CHUNK, :] = outs[i]

        pl.when(pl.program_id(1) < CA_MASKED_STEPS)(lambda: run(True))
        pl.when(pl.program_id(1) >= CA_MASKED_STEPS)(lambda: run(False))

    return pl.pallas_call(
        body, name="ca_fwd", grid=(4, NCHUNK // CA_PER_STEP),
        in_specs=[pl.BlockSpec((CA_ROWS, 128), lambda hp, c: (c, hp)),
                  pl.BlockSpec((T + CA_PAD, 128), lambda hp, c: (0, hp)),
                  pl.BlockSpec((T + CA_PAD, 128), lambda hp, c: (0, 4 + hp)),
                  pl.BlockSpec((2, CHUNK, CA_BAND), lambda hp, c: (hp, 0, 0))],
        out_specs=pl.BlockSpec((CA_ROWS, 128), lambda hp, c: (c, hp)),
        out_shape=jax.ShapeDtypeStruct((T, D), F32),
        compiler_params=_cp(("parallel", "parallel")),
    )(proj, kvpad, kvpad, bias)


def _ca_bwd(proj, kvpad, bias, dcat):
    def body(q_ref, k_ref, v_ref, b_ref, do_ref, dq_ref, dk_ref, dv_ref, db_ref):
        c = pl.program_id(1)

        @pl.when(c == 0)
        def _():
            dk_ref[...] = jnp.zeros_like(dk_ref)
            dv_ref[...] = jnp.zeros_like(dv_ref)
            db_ref[...] = jnp.zeros_like(db_ref)

        def run(masked):
            grads, bands = [], []
            for i in range(CA_PER_STEP):
                ci = c * CA_PER_STEP + i
                band = pl.ds(pl.multiple_of(ci * CHUNK, CHUNK), CA_BAND)
                rows = slice(i * CHUNK, (i + 1) * CHUNK)
                fn = functools.partial(_ca_block, ci, masked)
                _, vjp = jax.vjp(fn, q_ref[rows, :], k_ref[band, :], v_ref[band, :], b_ref[...])
                grads.append(vjp(do_ref[rows, :]))
                bands.append(band)
            for i, (dq, _, _, _) in enumerate(grads):
                dq_ref[i * CHUNK:(i + 1) * CHUNK, :] = dq
            for band, (_, dkb, dvb, _) in zip(bands, grads):
                dk_ref[band, :] += dkb
                dv_ref[band, :] += dvb
            db_ref[...] += functools.reduce(lambda a, b: a + b, [g[3] for g in grads])

        pl.when(c < CA_MASKED_STEPS)(lambda: run(True))
        pl.when(c >= CA_MASKED_STEPS)(lambda: run(False))

    sds = lambda *s: jax.ShapeDtypeStruct(s, F32)
    padded = lambda: pl.BlockSpec((T + CA_PAD, 128), lambda hp, c: (0, hp))
    return pl.pallas_call(
        body, name="ca_bwd", grid=(4, NCHUNK // CA_PER_STEP),
        in_specs=[pl.BlockSpec((CA_ROWS, 128), lambda hp, c: (c, hp)),
                  pl.BlockSpec((T + CA_PAD, 128), lambda hp, c: (0, hp)),
                  pl.BlockSpec((T + CA_PAD, 128), lambda hp, c: (0, 4 + hp)),
                  pl.BlockSpec((2, CHUNK, CA_BAND), lambda hp, c: (hp, 0, 0)),
                  pl.BlockSpec((CA_ROWS, 128), lambda hp, c: (c, hp))],
        out_specs=(pl.BlockSpec((CA_ROWS, 128), lambda hp, c: (c, hp)), padded(), padded(),
                   pl.BlockSpec((2, CHUNK, CA_BAND), lambda hp, c: (hp, 0, 0))),
        out_shape=(sds(T, 512), sds(T + CA_PAD, 512), sds(T + CA_PAD, 512), sds(8, CHUNK, CA_BAND)),
        compiler_params=_cp(("parallel", "arbitrary")),
    )(proj, kvpad, kvpad, bias, dcat)


def _lru_pre(xs, cw, cb, wa, ba, wx, bx, lam):
    xc = cb + xs[0] * cw[0:1, :] + xs[1] * cw[1:2, :] + xs[2] * cw[2:3, :] + xs[3] * cw[3:4, :]
    ra = _sigmoid(bdot(xc, wa, "nn") + ba)
    ii = _sigmoid(bdot(xc, wx, "nn") + bx)
    la = 8.0 * ra * _log_sigmoid(lam)
    return jnp.exp(la), jnp.sqrt(-_expm1(2.0 * la)) * (ii * xc)


def _lru_pre_specs():
    full = lambda shape: pl.BlockSpec(shape, lambda i: (0,) * len(shape))
    return [pl.BlockSpec((4, ROWS, 512), lambda i: (0, i, 0)), full((4, 512)), full((1, 512)),
            full((512, 512)), full((1, 512)), full((512, 512)), full((1, 512)), full((1, 512))]


def _lru_pre_fwd(xs, cw, cb, wa, ba, wx, bx, lam):
    def body(xs_ref, cw_ref, cb_ref, wa_ref, ba_ref, wx_ref, bx_ref, lam_ref, a_ref, b_ref):
        a, b = _lru_pre(xs_ref[...], cw_ref[...], cb_ref[...], wa_ref[...], ba_ref[...], wx_ref[...], bx_ref[...],
                        lam_ref[...])
        a_ref[...] = a
        b_ref[...] = b

    row = pl.BlockSpec((ROWS, 512), lambda i: (i, 0))
    sds = jax.ShapeDtypeStruct((T, 512), F32)
    return pl.pallas_call(body, name="lru_pre_fwd", grid=(T // ROWS,), in_specs=_lru_pre_specs(),
                          out_specs=(row, row), out_shape=(sds, sds), compiler_params=_cp(("parallel",)),
                          )(xs, cw, cb, wa, ba, wx, bx, lam)


def _lru_pre_bwd(xs, cw, cb, wa, ba, wx, bx, lam, da, db):
    def body(xs_ref, cw_ref, cb_ref, wa_ref, ba_ref, wx_ref, bx_ref, lam_ref, da_ref, db_ref,
             dxs_ref, dcw_ref, dcb_ref, dwa_ref, dba_ref, dwx_ref, dbx_ref, dlam_ref):
        acc = (dcw_ref, dcb_ref, dwa_ref, dba_ref, dwx_ref, dbx_ref, dlam_ref)

        @pl.when(pl.program_id(0) == 0)
        def _():
            for r in acc:
                r[...] = jnp.zeros_like(r)

        _, vjp = jax.vjp(_lru_pre, xs_ref[...], cw_ref[...], cb_ref[...], wa_ref[...], ba_ref[...], wx_ref[...],
                         bx_ref[...], lam_ref[...])
        grads = vjp((da_ref[...], db_ref[...]))
        dxs_ref[...] = grads[0]
        for r, g in zip(acc, grads[1:]):
            r[...] += g

    row = pl.BlockSpec((ROWS, 512), lambda i: (i, 0))
    specs = _lru_pre_specs()
    sds = lambda *s: jax.ShapeDtypeStruct(s, F32)
    return pl.pallas_call(
        body, name="lru_pre_bwd", grid=(T // ROWS,), in_specs=specs + [row, row], out_specs=tuple(specs),
        out_shape=(sds(4, T, 512), sds(4, 512), sds(1, 512), sds(512, 512), sds(1, 512), sds(512, 512), sds(1, 512),
                   sds(1, 512)),
        compiler_params=_cp(("arbitrary",)),
    )(xs, cw, cb, wa, ba, wx, bx, lam, da, db)


SCAN_ROWS = 8


def _scan8(a, b, towards_later):
    row = _iota((SCAN_ROWS, 512), 0)
    for s in (1, 2, 4):
        if towards_later:
            keep, shift = row >= s, s
        else:
            keep, shift = row < SCAN_ROWS - s, SCAN_ROWS - s
        a_s = jnp.where(keep, pltpu.roll(a, shift, 0), 1.0)
        b_s = jnp.where(keep, pltpu.roll(b, shift, 0), 0.0)
        b = a * b_s + b
        a = a * a_s
    return a, b


def _lru_scan_fwd(a, b):
    def body(a_ref, b_ref, h_ref):
        def step(i, carry):
            rows = pl.ds(pl.multiple_of(i * SCAN_ROWS, SCAN_ROWS), SCAN_ROWS)
            a8, b8 = _scan8(a_ref[rows, :], b_ref[rows, :], True)
            h = a8 * carry + b8
            h_ref[rows, :] = h
            return jnp.broadcast_to(h[SCAN_ROWS - 1:, :], (SCAN_ROWS, 512))

        lax.fori_loop(0, T // SCAN_ROWS, step, jnp.zeros((SCAN_ROWS, 512), F32), unroll=2)

    return pl.pallas_call(body, name="lru_scan_fwd", out_shape=jax.ShapeDtypeStruct((T, 512), F32),
                          compiler_params=pltpu.CompilerParams(vmem_limit_bytes=VMEM_LIMIT))(a, b)


def _lru_scan_bwd(a_next, h_prev, dh):
    def body(a_ref, h_ref, dh_ref, da_ref, db_ref):
        def step(i, carry):
            start = T - SCAN_ROWS * (i + 1)
            rows = pl.ds(pl.multiple_of(start, SCAN_ROWS), SCAN_ROWS)
            a8, b8 = _scan8(a_ref[rows, :], dh_ref[rows, :], False)
            g = a8 * carry + b8
            db_ref[rows, :] = g
            da_ref[rows, :] = g * h_ref[rows, :]
            return jnp.broadcast_to(g[:1, :], (SCAN_ROWS, 512))

        lax.fori_loop(0, T // SCAN_ROWS, step, jnp.zeros((SCAN_ROWS, 512), F32), unroll=2)

    sds = jax.ShapeDtypeStruct((T, 512), F32)
    return pl.pallas_call(body, name="lru_scan_bwd", out_shape=(sds, sds),
                          compiler_params=pltpu.CompilerParams(vmem_limit_bytes=VMEM_LIMIT))(a_next, h_prev, dh)


def _lru_post(h, gate):
    return h * _gelu_tanh(gate)


def _lru_post_fwd(h, proj, cat):
    def body(h_ref, g_ref, cat_ref, o_ref):
        o_ref[...] = _lru_post(h_ref[...], g_ref[...])

    row = pl.BlockSpec((ROWS, 512), lambda i: (i, 0))
    return pl.pallas_call(body, name="lru_post_fwd", grid=(T // ROWS,),
                          in_specs=[row, pl.BlockSpec((ROWS, 512), lambda i: (i, 3)), pl.BlockSpec(memory_space=pl.ANY)],
                          out_specs=pl.BlockSpec((ROWS, 512), lambda i: (i, 1)),
                          out_shape=jax.ShapeDtypeStruct((T, D), F32), input_output_aliases={2: 0},
                          compiler_params=_cp(("parallel",)))(h, proj, cat)


def _lru_post_bwd(h, proj, dcat):
    def body(h_ref, g_ref, do_ref, dh_ref, dg_ref):
        _, vjp = jax.vjp(_lru_post, h_ref[...], g_ref[...])
        dh, dg = vjp(do_ref[...])
        dh_ref[...] = dh
        dg_ref[...] = dg

    row = pl.BlockSpec((ROWS, 512), lambda i: (i, 0))
    sds = jax.ShapeDtypeStruct((T, 512), F32)
    return pl.pallas_call(body, name="lru_post_bwd", grid=(T // ROWS,),
                          in_specs=[row, pl.BlockSpec((ROWS, 512), lambda i: (i, 3)),
                                    pl.BlockSpec((ROWS, 512), lambda i: (i, 1))],
                          out_specs=(row, row), out_shape=(sds, sds), compiler_params=_cp(("parallel",)))(h, proj, dcat)


def _conv_dx(dxs_shift):
    def body(d_ref, o_ref):
        o_ref[...] = d_ref[0] + d_ref[1] + d_ref[2] + d_ref[3]

    row = pl.BlockSpec((ROWS, 512), lambda i: (i, 0))
    return pl.pallas_call(body, name="lru_conv_dx", grid=(T // ROWS,),
                          in_specs=[pl.BlockSpec((4, ROWS, 512), lambda i: (0, i, 0))], out_specs=row,
                          out_shape=jax.ShapeDtypeStruct((T, 512), F32), compiler_params=_cp(("parallel",)))(dxs_shift)


def _position():
    return lax.axis_index("x"), lax.axis_index("y"), lax.axis_index("c")


def _other_chips(x, y):
    return [(1 - x, y), (x, 1 - y), (1 - x, 1 - y)]


def _al(v, n):
    return v * n if isinstance(v, int) else pl.multiple_of(v * n, n)


_AG_ITEMS = [
    ((4, 32, 128), lambda o, s, h: o.at[s, pl.ds(_al(h, 16), 16), :], lambda r, h: r.at[pl.ds(_al(h, 16), 16), :]),
    ((4, 774, 1024), lambda o, s, h: o.at[s, :, pl.ds(_al(h, 512), 512)], lambda r, h: r.at[:, pl.ds(_al(h, 512), 512)]),
    ((1024, 1024), lambda o, s, h: o.at[pl.ds(_al(2 * s + h, 128), 128), :], lambda r, h: r.at[pl.ds(_al(h, 128), 128), :]),
    ((2, 1024, 4096), lambda o, s, h: o.at[h, :, pl.ds(_al(s, 1024), 1024)], lambda r, h: r.at[h]),
    ((2, 4096, 1024), lambda o, s, h: o.at[h, pl.ds(_al(s, 1024), 1024), :], lambda r, h: r.at[h]),
    ((1024, 2560), lambda o, s, h: o.at[pl.ds(_al(h, 512), 512), pl.ds(_al(s, 640), 640)],
     lambda r, h: r.at[pl.ds(_al(h, 512), 512), :]),
    ((1024, 1024), lambda o, s, h: o.at[pl.ds(_al(2 * s + h, 128), 128), :], lambda r, h: r.at[pl.ds(_al(h, 128), 128), :]),
]


_AG_GROUPS = [(0, 1, 2), (3, 4), (5, 6)]

_HBM = pl.BlockSpec(memory_space=pltpu.HBM)
_SEM = pl.BlockSpec(memory_space=pltpu.SEMAPHORE)
_SPLIT = dict(has_side_effects=pltpu.SideEffectType.DATAFLOW_SIDE_EFFECTING)


def _hbm(a):
    return pltpu.with_memory_space_constraint(a, pltpu.HBM)


def _ag_ici_copy(i, j, chip, c, slot, src_ref, land_ref, send_sems, recv_sems, k):
    _, dst, half = _AG_ITEMS[i]
    return pltpu.make_async_remote_copy(src_ref=half(src_ref, c), dst_ref=dst(land_ref, slot, c), send_sem=send_sems.at[k],
                                        recv_sem=recv_sems.at[k], device_id=(*chip, c), device_id_type=MESH)


def _ag_start(groups, shards, name):
    items_all = [i for g in groups for i in _AG_GROUPS[g]]
    n = len(items_all)
    ng = len(groups)
    lands = [lax.empty(_AG_ITEMS[i][0], shards[i].dtype) for i in items_all]

    def body(*refs):
        srcs, land_refs = dict(zip(items_all, refs[:n])), dict(zip(items_all, refs[n:2 * n]))
        sems = refs[2 * n:2 * n + 2 * ng]
        token = refs[-1]
        x, y, c = _position()
        me = 2 * x + y
        for gi, g in enumerate(groups):
            for t, i in enumerate(_AG_GROUPS[g]):
                for j, chip in enumerate(_other_chips(x, y)):
                    _ag_ici_copy(i, j, chip, c, me, srcs[i], land_refs[i], sems[2 * gi], sems[2 * gi + 1], 3 * t + j).start()
        token[...] = jnp.zeros_like(token)

    sem_shapes = []
    for g in groups:
        sem_shapes += [pltpu.SemaphoreType.DMA((3 * len(_AG_GROUPS[g]),))] * 2
    ops = [shards[i] for i in items_all] + lands
    out = pl.pallas_call(
        body, name=name,
        out_shape=tuple(sem_shapes) + tuple(pltpu.HBM(a.shape, a.dtype) for a in ops) + (jax.ShapeDtypeStruct((8, 128), F32),),
        in_specs=(_HBM,) * (2 * n),
        out_specs=(_SEM,) * (2 * ng) + (_HBM,) * (2 * n) + (pl.BlockSpec(memory_space=pltpu.VMEM),),
        input_output_aliases={i: 2 * ng + i for i in range(2 * n)},
        compiler_params=pltpu.CompilerParams(**_SPLIT),
    )(*[_hbm(a) for a in ops])
    sems, thru, token = out[:2 * ng], out[2 * ng:-1], out[-1]
    return ({g: (sems[2 * gi], sems[2 * gi + 1]) for gi, g in enumerate(groups)},
            dict(zip(items_all, thru[:n])), dict(zip(items_all, thru[n:])), token)


def _ag_wait(g, sems, srcs, lands, after):
    items = _AG_GROUPS[g]
    m = len(items)

    def body(*refs):
        src_refs, land_refs = refs[:m], refs[m:2 * m]
        send_sems, recv_sems = refs[2 * m], refs[2 * m + 1]
        x, y, c = _position()
        for t, i in enumerate(items):
            for j, chip in enumerate(_other_chips(x, y)):
                cp = _ag_ici_copy(i, j, chip, c, 2 * chip[0] + chip[1], src_refs[t], land_refs[t], send_sems, recv_sems,
                                  3 * t + j)
                cp.wait_send()
                cp.wait_recv()

    ops = [srcs[i] for i in items] + [lands[i] for i in items]
    out = pl.pallas_call(
        body, name=f"allgather_wait_{g}",
        out_shape=tuple(pltpu.HBM(a.shape, a.dtype) for a in ops),
        in_specs=(_HBM,) * (2 * m) + (_SEM, _SEM, pl.BlockSpec(memory_space=pl.ANY)),
        out_specs=(_HBM,) * (2 * m),
        input_output_aliases={i: i for i in range(2 * m)},
        compiler_params=pltpu.CompilerParams(**_SPLIT),
    )(*ops, sems[0], sems[1], after)
    return list(out[:m]), list(out[m:])


def _ag_forward(g, srcs, lands):
    return _ag_sibling(_AG_GROUPS[g], srcs, lands, False, f"allgather_forward_{g}")


def _ag_push_own(srcs, lands):
    items = tuple(sorted(lands))
    out = _ag_sibling(items, [srcs[i] for i in items], [lands[i] for i in items], True, "allgather_push_own")
    return dict(zip(items, out))


def _ag_push_copy(i, hc, src_ref, land_ref, send_sems, recv_sems, k):
    x, y, c = _position()
    _, dst, half = _AG_ITEMS[i]
    return pltpu.make_async_remote_copy(src_ref=half(src_ref, hc), dst_ref=dst(land_ref, 2 * x + y, hc),
                                        send_sem=send_sems.at[k], recv_sem=recv_sems.at[k], device_id=(x, y, 1 - c),
                                        device_id_type=MESH)


def _ag_push_start(srcs, lands):
    items = tuple(sorted(lands))
    m = len(items)

    def body(*refs):
        src_refs, land_refs = refs[:m], refs[m:2 * m]
        send_sems, recv_sems, token = refs[2 * m], refs[2 * m + 1], refs[-1]
        for t, i in enumerate(items):
            for hc in range(2):
                _ag_push_copy(i, hc, src_refs[t], land_refs[t], send_sems, recv_sems, 2 * t + hc).start()
        token[...] = jnp.zeros_like(token)

    ops = [srcs[i] for i in items] + [lands[i] for i in items]
    sem = pltpu.SemaphoreType.DMA((2 * m,))
    out = pl.pallas_call(
        body, name="allgather_push_start",
        out_shape=(sem, sem) + tuple(pltpu.HBM(a.shape, a.dtype) for a in ops) + (jax.ShapeDtypeStruct((8, 128), F32),),
        in_specs=(_HBM,) * (2 * m),
        out_specs=(_SEM, _SEM) + (_HBM,) * (2 * m) + (pl.BlockSpec(memory_space=pltpu.VMEM),),
        input_output_aliases={i: 2 + i for i in range(2 * m)},
        compiler_params=pltpu.CompilerParams(**_SPLIT),
    )(*[_hbm(a) for a in ops])
    return items, out[0], out[1], out[2:2 + m], out[2 + m:2 + 2 * m]


def _ag_push_wait(started, after):
    items, send_sems, recv_sems, srcs, lands = started
    m = len(items)

    def body(*refs):
        src_refs, land_refs = refs[:m], refs[m:2 * m]
        send_sems, recv_sems = refs[2 * m], refs[2 * m + 1]
        for t, i in enumerate(items):
            for hc in range(2):
                cp = _ag_push_copy(i, hc, src_refs[t], land_refs[t], send_sems, recv_sems, 2 * t + hc)
                cp.wait_send()
                cp.wait_recv()

    ops = list(srcs) + list(lands)
    out = pl.pallas_call(
        body, name="allgather_push_wait",
        out_shape=tuple(pltpu.HBM(a.shape, a.dtype) for a in ops),
        in_specs=(_HBM,) * (2 * m) + (_SEM, _SEM, pl.BlockSpec(memory_space=pl.ANY)),
        out_specs=(_HBM,) * (2 * m),
        input_output_aliases={i: i for i in range(2 * m)},
        compiler_params=pltpu.CompilerParams(**_SPLIT),
    )(*ops, send_sems, recv_sems, after)
    return dict(zip(items, out[:m])), dict(zip(items, out[m:]))


def _ag_sibling(items, srcs, lands, own, name):
    m = len(items)
    per = 2 if own else 3

    def body(*refs):
        src_refs, in_refs, out_refs = refs[:m], refs[m:2 * m], refs[2 * m:3 * m]
        send_sems, recv_sems = refs[3 * m:]
        x, y, c = _position()
        sibling = (x, y, 1 - c)
        me = 2 * x + y
        if own:
            mine = theirs = [(me, 0), (me, 1)]
        else:
            slots = [2 * chip[0] + chip[1] for chip in _other_chips(x, y)]
            mine, theirs = [(s, c) for s in slots], [(s, 1 - c) for s in slots]
        sends = []
        for t, i in enumerate(items):
            _, dst, half = _AG_ITEMS[i]
            for k, (slot, hc) in enumerate(mine):
                src = half(src_refs[t], hc) if own else dst(in_refs[t], slot, hc)
                sends.append(pltpu.make_async_remote_copy(
                    src_ref=src, dst_ref=dst(out_refs[t], slot, hc), send_sem=send_sems.at[per * t + k],
                    recv_sem=recv_sems.at[per * t + k], device_id=sibling, device_id_type=MESH))
        for cp in sends:
            cp.start()
        for t, i in enumerate(items):
            dst = _AG_ITEMS[i][1]
            for k, (slot, hc) in enumerate(theirs):
                there = dst(out_refs[t], slot, hc)
                pltpu.make_async_remote_copy(src_ref=there, dst_ref=there, send_sem=send_sems.at[per * t + k],
                                             recv_sem=recv_sems.at[per * t + k], device_id=sibling,
                                             device_id_type=MESH).wait_recv()
        for cp in sends:
            cp.wait_send()

    any_spec = pl.BlockSpec(memory_space=pl.ANY)
    return pl.pallas_call(
        body, name=name,
        in_specs=[any_spec] * (2 * m), out_specs=(any_spec,) * m,
        out_shape=tuple(jax.ShapeDtypeStruct(a.shape, a.dtype) for a in lands),
        input_output_aliases={m + t: t for t in range(m)},
        scratch_shapes=[pltpu.SemaphoreType.DMA((per * m,)), pltpu.SemaphoreType.DMA((per * m,))],
    )(*srcs, *lands)


def _pair_swap_copy(g_ref, r_ref, send_sem, recv_sem):
    x, y, c = _position()
    hc = g_ref.shape[2] // 2
    return pltpu.make_async_remote_copy(src_ref=g_ref.at[:, :, pl.ds(_al(1 - c, hc), hc)], dst_ref=r_ref,
                                        send_sem=send_sem, recv_sem=recv_sem, device_id=(x, y, 1 - c),
                                        device_id_type=MESH)


def _pair_swap_start(gb, tag):
    _, rows, cols = gb.shape
    recv = lax.empty((4, rows, cols // 2), gb.dtype)

    def body(g_ref, r_ref, send_sem, recv_sem, g_thru, r_thru, token):
        _pair_swap_copy(g_ref, r_ref, send_sem, recv_sem).start()
        token[...] = jnp.zeros_like(token)

    return pl.pallas_call(
        body, name="grad_pair_swap_start_" + tag,
        out_shape=(pltpu.SemaphoreType.DMA(()), pltpu.SemaphoreType.DMA(()), pltpu.HBM(gb.shape, gb.dtype),
                   pltpu.HBM(recv.shape, recv.dtype), jax.ShapeDtypeStruct((8, 128), F32)),
        in_specs=(_HBM, _HBM), out_specs=(_SEM, _SEM, _HBM, _HBM, pl.BlockSpec(memory_space=pltpu.VMEM)),
        input_output_aliases={0: 2, 1: 3},
        compiler_params=pltpu.CompilerParams(**_SPLIT),
    )(_hbm(gb), _hbm(recv))


def _pair_swap_wait(started, after, tag):
    send_sem, recv_sem, gb, recv, _ = started

    def body(g_ref, r_ref, send_sem, recv_sem, after_ref, g_out, r_out):
        cp = _pair_swap_copy(g_ref, r_ref, send_sem, recv_sem)
        cp.wait_send()
        cp.wait_recv()

    return pl.pallas_call(
        body, name="grad_pair_swap_wait_" + tag,
        out_shape=(pltpu.HBM(gb.shape, gb.dtype), pltpu.HBM(recv.shape, recv.dtype)),
        in_specs=(_HBM, _HBM, _SEM, _SEM, pl.BlockSpec(memory_space=pl.ANY)), out_specs=(_HBM, _HBM),
        input_output_aliases={0: 0, 1: 1},
        compiler_params=pltpu.CompilerParams(**_SPLIT),
    )(gb, recv, send_sem, recv_sem, after)


def _handover_copy(r_ref, send_sem, recv_sem, core):
    x, y, c = _position()
    hc = r_ref.shape[1] // 2
    cols = r_ref.at[:, pl.ds(_al(core, hc), hc)]
    return pltpu.make_async_remote_copy(src_ref=cols, dst_ref=cols, send_sem=send_sem, recv_sem=recv_sem,
                                        device_id=(x, y, 1 - c), device_id_type=MESH)


def _handover_start(red, tag):
    def body(r_ref, send_sem, recv_sem, r_thru, token):
        _handover_copy(r_ref, send_sem, recv_sem, lax.axis_index("c")).start()
        token[...] = jnp.zeros_like(token)

    return pl.pallas_call(
        body, name="grad_handover_start_" + tag,
        out_shape=(pltpu.SemaphoreType.DMA(()), pltpu.SemaphoreType.DMA(()), pltpu.HBM(red.shape, red.dtype),
                   jax.ShapeDtypeStruct((8, 128), F32)),
        in_specs=(_HBM,), out_specs=(_SEM, _SEM, _HBM, pl.BlockSpec(memory_space=pltpu.VMEM)),
        input_output_aliases={0: 2},
        compiler_params=pltpu.CompilerParams(**_SPLIT),
    )(_hbm(red))


def _handover_wait(started, after, tag):
    send_sem, recv_sem, red, _ = started

    def body(r_ref, send_sem, recv_sem, after_ref, r_out):
        c = lax.axis_index("c")
        _handover_copy(r_ref, send_sem, recv_sem, c).wait_send()
        _handover_copy(r_ref, send_sem, recv_sem, 1 - c).wait_recv()

    return pl.pallas_call(
        body, name="grad_handover_wait_" + tag,
        out_shape=pltpu.HBM(red.shape, red.dtype),
        in_specs=(_HBM, _SEM, _SEM, pl.BlockSpec(memory_space=pl.ANY)), out_specs=_HBM,
        input_output_aliases={0: 0},
        compiler_params=pltpu.CompilerParams(**_SPLIT),
    )(red, send_sem, recv_sem, after)


def _handover(red, tag):
    started = _handover_start(red, tag)
    return _handover_wait(started, started[3], tag)


def _a2a_copy(j, chip, c, p_ref, q_ref, q_slot, send_sems, recv_sems):
    return pltpu.make_async_remote_copy(src_ref=p_ref.at[2 * chip[0] + chip[1]], dst_ref=q_ref.at[q_slot],
                                        send_sem=send_sems.at[j], recv_sem=recv_sems.at[j], device_id=(*chip, c),
                                        device_id_type=MESH)


def _a2a_start(p, tag):
    def body(p_ref, q_ref, send_sems, recv_sems, p_thru, q_thru, token):
        x, y, c = _position()
        for j, chip in enumerate(_other_chips(x, y)):
            _a2a_copy(j, chip, c, p_ref, q_ref, 2 * x + y, send_sems, recv_sems).start()
        token[...] = jnp.zeros_like(token)

    return pl.pallas_call(
        body, name="grad_alltoall_start_" + tag,
        out_shape=(pltpu.SemaphoreType.DMA((3,)), pltpu.SemaphoreType.DMA((3,)), pltpu.HBM(p.shape, p.dtype),
                   pltpu.HBM(p.shape, p.dtype), jax.ShapeDtypeStruct((8, 128), F32)),
        in_specs=(_HBM, _HBM), out_specs=(_SEM, _SEM, _HBM, _HBM, pl.BlockSpec(memory_space=pltpu.VMEM)),
        input_output_aliases={0: 2, 1: 3},
        compiler_params=pltpu.CompilerParams(**_SPLIT),
    )(_hbm(p), _hbm(lax.empty(p.shape, p.dtype)))


def _a2a_wait(send_sems, recv_sems, p, q, after, tag):
    def body(p_ref, q_ref, send_sems, recv_sems, after_ref, p_out, q_out):
        x, y, c = _position()
        for j, chip in enumerate(_other_chips(x, y)):
            cp = _a2a_copy(j, chip, c, p_ref, q_ref, 2 * chip[0] + chip[1], send_sems, recv_sems)
            cp.wait_send()
            cp.wait_recv()

    return pl.pallas_call(
        body, name="grad_alltoall_wait_" + tag,
        out_shape=(pltpu.HBM(p.shape, p.dtype), pltpu.HBM(q.shape, q.dtype)),
        in_specs=(_HBM, _HBM, _SEM, _SEM, pl.BlockSpec(memory_space=pl.ANY)), out_specs=(_HBM, _HBM),
        input_output_aliases={0: 0, 1: 1},
        compiler_params=pltpu.CompilerParams(**_SPLIT),
    )(p, q, send_sems, recv_sems, after)


def _comm_rows(rows):
    return next(t for t in (512, 384, 256, 128) if rows % t == 0)


def _pair_add(gb, recv, where, tag):
    _, rows, cols = gb.shape
    hc = cols // 2
    tr = _comm_rows(rows)

    def body(w_ref, g_ref, r_ref, o_ref):
        o_ref[...] = (g_ref[...].astype(F32) + r_ref[...].astype(F32)).astype(o_ref.dtype)

    return pl.pallas_call(
        body, name="grad_pair_add_" + tag,
        grid_spec=pltpu.PrefetchScalarGridSpec(
            num_scalar_prefetch=1, grid=(4, rows // tr),
            in_specs=[pl.BlockSpec((None, tr, hc), lambda s, j, w_ref: (s, j, w_ref[0])),
                      pl.BlockSpec((None, tr, hc), lambda s, j, w_ref: (s, j, 0))],
            out_specs=pl.BlockSpec((None, tr, hc), lambda s, j, w_ref: (s, j, 0))),
        out_shape=jax.ShapeDtypeStruct((4, rows, hc), gb.dtype),
        compiler_params=_cp(("parallel", "parallel")),
    )(where, gb, recv)


def _sum_chips(p, q, where, tag):
    _, rows, hc = q.shape
    tr = _comm_rows(rows)

    def body(w_ref, p_ref, qa_ref, qb_ref, qc_ref, o_ref):
        me = w_ref[1]
        own, qa, qb, qc = (r[...].astype(F32) for r in (p_ref, qa_ref, qb_ref, qc_ref))
        v0 = jnp.where(me == 0, own, qa)
        v1 = jnp.where(me == 1, own, jnp.where(me == 0, qa, qb))
        v2 = jnp.where(me == 2, own, jnp.where(me < 2, qb, qc))
        v3 = jnp.where(me == 3, own, qc)
        o_ref[...] = ((v0 + v1) + v2) + v3

    slot = lambda k: pl.BlockSpec((None, tr, hc), lambda j, w_ref: (w_ref[k], j, 0))
    return pl.pallas_call(
        body, name="grad_sum_chips_" + tag,
        grid_spec=pltpu.PrefetchScalarGridSpec(
            num_scalar_prefetch=1, grid=(rows // tr,),
            in_specs=[slot(1), slot(2), slot(3), slot(4)],
            out_specs=pl.BlockSpec((tr, hc), lambda j, w_ref: (j, w_ref[0]))),
        out_shape=jax.ShapeDtypeStruct((rows, 2 * hc), F32),
        compiler_params=_cp(("parallel",)),
    )(where, p, q, q, q)


def _shard_major(g, axis):
    shape = g.shape
    g = g.reshape(shape[:axis] + (4, shape[axis] // 4) + shape[axis + 1:])
    return jnp.moveaxis(g, axis, 0).reshape(4, -1)


def _unshard(g4, shape, axis):
    n = shape[axis] // 4
    g = g4.reshape((4,) + shape[:axis] + (n,) + shape[axis + 1:])
    return jnp.moveaxis(g, 0, axis).reshape(shape)


def _split(flat, shapes):
    out, off = [], 0
    for shp in shapes:
        n = 1
        for d in shp:
            n *= d
        out.append(flat[..., off:off + n].reshape(flat.shape[:-1] + tuple(shp)))
        off += n
    return out


def _even_rows_to_kernel(wt):
    return jnp.concatenate([wt[:1536], wt[1552:3088], wt[1536:1552], wt[3088:3096],
                            jnp.zeros((PE - 3096, wt.shape[1]), wt.dtype)], axis=0)


def _block_diag(w):
    eye = jnp.eye(8, dtype=w.dtype)
    return (w[:, :, None, :] * eye[:, None, :, None]).reshape(512, 512)


def _diag_blocks(g):
    eye = jnp.eye(8, dtype=g.dtype)
    return (g.reshape(8, 64, 8, 64) * eye[:, None, :, None]).sum(axis=2)


def _shift_down(a, s):
    return a if s == 0 else jnp.pad(a, ((s, 0), (0, 0)))[:a.shape[0]]


def _shift_up(a, s):
    return a if s == 0 else jnp.pad(a, ((0, s), (0, 0)))[s:]


SMALL_SHARDED_SHAPES = [(2, 4, 256), (16, 64), (4, 128), (128,), (128,), (128,), (128,)]
REPL_SHAPES = [(256,), (512,), (8,), (8, 257), (8, 64, 64), (8, 64, 64)]


def kernel(x, norm_w, w_in_even, gla_w_a_up, gla_b_a, gla_norm_w, fox_b_f, w_out_even, w_in_odd, rel_bias, conv_w, conv_b, lru_w_a, lru_b_a, lru_w_x, lru_b_x, lru_lambda, w_out_odd, w_mlp_up, w_mlp_down, loss_target, m_norm_w, m_w_in_even, m_gla_w_a_up, m_gla_b_a, m_gla_norm_w, m_fox_b_f, m_w_out_even, m_w_in_odd, m_rel_bias, m_conv_w, m_conv_b, m_lru_w_a, m_lru_b_a, m_lru_w_x, m_lru_b_x, m_lru_lambda, m_w_out_odd, m_w_mlp_up, m_w_mlp_down, v_norm_w, v_w_in_even, v_gla_w_a_up, v_gla_b_a, v_gla_norm_w, v_fox_b_f, v_w_out_even, v_w_in_odd, v_rel_bias, v_conv_w, v_conv_b, v_lru_w_a, v_lru_b_a, v_lru_w_x, v_lru_b_x, v_lru_lambda, v_w_out_odd, v_w_mlp_up, v_w_mlp_down):
    c_idx = lax.axis_index("c")

    small_local = [norm_w, gla_w_a_up[0], conv_w[0], conv_b[0], lru_b_a[0], lru_b_x[0], lru_lambda[0]]
    small_src = jnp.concatenate([a.reshape(-1) for a in small_local]).reshape(32, 128)
    first = {0: small_src, 1: w_in_even[0].T.astype(BF16), 2: w_out_even[0].astype(BF16)}
    sems0, srcs0, lands0, ag_token = _ag_start([0], first, "allgather_start_0")
    zero = ag_token[0, 0]
    later = {3: (w_mlp_up + zero).astype(BF16), 4: (w_mlp_down + zero).astype(BF16),
             5: (w_in_odd[0] + zero).astype(BF16), 6: (w_out_odd[0] + zero).astype(BF16)}
    sems1, srcs1, lands1, ag_token = _ag_start([1, 2], later, "allgather_start_1")
    ag_sems = {**sems0, **sems1}
    pushed = _ag_push_start(srcs1, lands1)
    ag_srcs, ag_lands = dict(srcs0), _ag_push_own(srcs0, lands0)

    def gathered(g, after):
        srcs_g, lands_g = _ag_wait(g, ag_sems[g], ag_srcs, ag_lands, after)
        return _ag_forward(g, srcs_g, lands_g)

    small4, w_in_e4, w_out_e = gathered(0, ag_token)
    me = 2 * lax.axis_index("x") + lax.axis_index("y")
    others = [k + (k >= me).astype(jnp.int32) for k in range(3)]
    where = jnp.stack([c_idx, me] + others).astype(jnp.int32)

    w_in_e_t = _even_rows_to_kernel(w_in_e4.reshape(3096, D))
    g_small = _split(small4.reshape(4, 32 * 128), SMALL_SHARDED_SHAPES)
    nw_full = _unshard(g_small[0], (2, 4, 1024), 2)
    wa_up = _unshard(g_small[1], (16, 256), 1)
    cw = _unshard(g_small[2], (4, 512), 1)
    cb, lba, lbx, lam = [_unshard(g, (512,), 0).reshape(1, 512) for g in g_small[3:]]
    nw = lambda layer, i: nw_full[layer, i].reshape(1, D)

    wa_pad = jnp.pad(wa_up, ((0, 128 - 16), (0, 0)))
    gla_ba = gla_b_a.reshape(1, 256)
    gla_nw = gla_norm_w.reshape(1, 512)
    fox_bpad = jnp.pad(fox_b_f.reshape(1, 8), ((0, 0), (FOX_LANE0, 128 - FOX_LANE0 - 8)))
    rbp = jnp.pad(rel_bias[0], ((0, 0), (0, REL_PAD - 257)))
    wa_bd = _block_diag(lru_w_a[0])
    wx_bd = _block_diag(lru_w_x[0])

    x0 = x[0]
    tgt = loss_target[0]

    h0 = _prenorm(x0, nw(0, 0), "prenorm_l0_mix")
    proj_e = _mm(h0, w_in_e_t, "nt", tm=2048, tn=640, name="mm_in_even")
    cat0, s_prev = _gla_fwd(proj_e, wa_pad, gla_ba, gla_nw)
    cum_r = _fox_gate_fwd(proj_e, fox_bpad)
    cum_c = cum_r[:, FOX_LANE0:FOX_LANE0 + 8].T
    cat0 = _fox_fwd(proj_e, cum_c, cat0)
    mix0 = _mm(cat0, w_out_e, "nn", tm=2048, tn=512, name="mm_out_even")
    x1, h1 = _post_pre_fwd(x0, mix0, nw(0, 1), nw(0, 2), "post_pre_l0_mix")
    srcs1, lands1 = _ag_push_wait(pushed, x1)
    ag_srcs.update(srcs1)
    ag_lands.update(lands1)
    w_up, w_dn = gathered(1, x1)
    a0, r0 = _mm(h1, w_up, "nn", tm=2048, tn=1024, b_layer=0, relu_pair=True, name="mm_up_l0")
    d0 = _mm(a0, w_dn, "nn", tm=1024, tn=512, b_layer=0, name="mm_down_l0")
    x2, h2 = _post_pre_fwd(x1, d0, nw(0, 3), nw(1, 0), "post_pre_l0_mlp")

    w_in_o, w_out_o = gathered(2, x2)
    proj_o = _mm(h2, w_in_o, "nn", tm=2048, tn=640, name="mm_in_odd")
    bias_q = _bias_build(rbp)
    bias = bias_q.transpose(1, 0, 2)
    kvpad = jnp.pad(proj_o[:, 512:1536], ((CA_PAD, 0), (0, 0)))
    cat1 = _ca_fwd(proj_o, kvpad, bias)
    x_in = proj_o[:, 2048:2560]
    xs = jnp.stack([_shift_down(x_in, 3 - j) for j in range(4)])
    lru_a, lru_b = _lru_pre_fwd(xs, cw, cb, wa_bd, lba, wx_bd, lbx, lam)
    hh = _lru_scan_fwd(lru_a, lru_b)
    cat1 = _lru_post_fwd(hh, proj_o, cat1)
    mix1 = _mm(cat1, w_out_o, "nn", tm=2048, tn=512, name="mm_out_odd")
    x3, h3 = _post_pre_fwd(x2, mix1, nw(1, 1), nw(1, 2), "post_pre_l1_mix")
    a1, r1 = _mm(h3, w_up, "nn", tm=2048, tn=1024, b_layer=1, relu_pair=True, name="mm_up_l1")
    d1 = _mm(a1, w_dn, "nn", tm=1024, tn=512, b_layer=1, name="mm_down_l1")
    g4, loss_part, dd1, dnw13 = _post_loss(x3, d1, nw(1, 3), tgt)
    loss = lax.psum(loss_part[0, 0], ("x", "y", "c"))

    def rs_begin(swap, after, tag):
        gb, recv = _pair_swap_wait(swap, after, tag)
        return _a2a_start(_pair_add(gb, recv, where, tag), tag)

    def rs_end(started, after, tag):
        send_sems, recv_sems, p, q, _ = started
        p, q = _a2a_wait(send_sems, recv_sems, p, q, after, tag)
        return _handover(_sum_chips(p, q, where, tag), tag)

    gba = lax.dynamic_update_slice(lax.empty((4, GA_ROWS, D), BF16), jnp.zeros((4, GA_UP - GA_GAP, D), BF16),
                                   (0, GA_GAP, 0))
    gba = _mm(a1, dd1, "tn", tm=512, tn=1024, into=(gba, 1024, GA_DN), name="mm_down_l1_dw")
    du1 = _mm(dd1, w_dn, "nt", tm=2048, tn=1024, b_layer=1, times2=r1, out_dtype=BF16, name="mm_down_l1_dx")
    gba = _mm(du1, h3, "tn", tm=512, tn=1024, into=(gba, 1024, GA_UP), name="mm_up_l1_dw")
    dh3 = _mm(du1, w_up, "nt", tm=1024, tn=512, b_layer=1, name="mm_up_l1_dx")
    g3, dmix1, dnw12, dnw11 = _pre_post_bwd(x3, nw(1, 2), dh3, g4, mix1, nw(1, 1), "pre_post_bwd_l1_mlp")
    gba = _mm(cat1, dmix1, "tn", tm=128, tn=1024, into=(gba, 256, GA_OUT_O), name="mm_out_odd_dw")
    dcat1 = _mm(dmix1, w_out_o, "nt", tm=2048, tn=512, name="mm_out_odd_dx")

    dq_c, dkpad, dvpad, dbias = _ca_bwd(proj_o, kvpad, bias, dcat1)
    g_rel = _bias_grad(jnp.pad(dbias.transpose(1, 0, 2), ((0, 0), (0, 0), (0, BIAS_W - CA_BAND))))[:, :257]
    dhh, dgate = _lru_post_bwd(hh, proj_o, dcat1)
    da_l, db_l = _lru_scan_bwd(_shift_up(lru_a, 1), _shift_down(hh, 1), dhh)
    dxs, g_cw, g_cb, g_wa_bd, g_lba, g_wx_bd, g_lbx, g_lam = _lru_pre_bwd(xs, cw, cb, wa_bd, lba, wx_bd, lbx, lam, da_l, db_l)
    dx_in = _conv_dx(jnp.stack([_shift_up(dxs[j], 3 - j) for j in range(4)]))
    dproj_o = jnp.concatenate([dq_c, dkpad[CA_PAD:], dvpad[CA_PAD:], dgate, dx_in], axis=1).astype(BF16)
    gba = _mm(dproj_o, h2, "tn", tm=128, tn=1024, into=(gba, 640, GA_IN_O), name="mm_in_odd_dw")
    swap_a = _pair_swap_start(gba, "a")
    dh2 = _mm(dproj_o, w_in_o, "nt", tm=1024, tn=512, name="mm_in_odd_dx")
    g2, dd0, dnw10, dnw03 = _pre_post_bwd(x2, nw(1, 0) + swap_a[4][0, 0], dh2, g3, d0, nw(0, 3), "pre_post_bwd_l1_mix")
    rs_a = rs_begin(swap_a, g2, "a")

    gbb = lax.empty((4, GB_ROWS, D), BF16)
    gbb = _mm(a0, dd0, "tn", tm=512, tn=1024, into=(gbb, 1024, GB_DN), name="mm_down_l0_dw")
    du0 = _mm(dd0, w_dn, "nt", tm=2048, tn=1024, b_layer=0, times2=r0, out_dtype=BF16, name="mm_down_l0_dx")
    gbb = _mm(du0, h1, "tn", tm=512, tn=1024, into=(gbb, 1024, GB_UP), name="mm_up_l0_dw")
    swap_b = _pair_swap_start(gbb, "b")
    dh1 = _mm(du0, w_up, "nt", tm=1024, tn=512, b_layer=0, name="mm_up_l0_dx")
    g1, dmix0, dnw02, dnw01 = _pre_post_bwd(x1, nw(0, 2) + (swap_b[4][0, 0] + rs_a[4][0, 0]), dh1, g2, mix0, nw(0, 1),
                                            "pre_post_bwd_l0_mlp")
    rs_b = rs_begin(swap_b, g1, "b")
    gbc = lax.empty((4, GC_ROWS, D), BF16)
    gbc = _mm(cat0, dmix0, "tn", tm=128, tn=1024, into=(gbc, 256, GC_OUT_E), name="mm_out_even_dw")
    dcat0 = _mm(dmix0, w_out_e, "nt", tm=2048, tn=512, name="mm_out_even_dx")

    dq_g, dk_g, dv_g, dr_g, daux_g, g_wa_pad, g_gla_ba, g_gla_nw = _gla_bwd(
        proj_e, s_prev, wa_pad, gla_ba, gla_nw + rs_b[4][0, 0], dcat0)
    dq_f, dk_f, dv_f, dccol = _fox_bwd(proj_e, cum_c, dcat0)
    dccol_t = jnp.pad(dccol.sum(axis=0).T, ((0, 0), (FOX_LANE0, 128 - FOX_LANE0 - 8)))
    daux, g_fox_bpad = _fox_gate_bwd(proj_e, fox_bpad, dccol_t, daux_g)
    dproj_e = jnp.concatenate([dq_g, dk_g, dv_g, dr_g, dq_f, dk_f, dv_f, daux], axis=1).astype(BF16)
    gt_in_e = _mm(dproj_e, h0, "tn", tm=640, tn=1024, out_dtype=BF16, name="mm_in_even_dw")
    dh0 = _mm(dproj_e, w_in_e_t, "nn", tm=1024, tn=512, name="mm_in_even_dx")
    grad_x, dnw00 = _norm_bwd(x0, nw(0, 0), dh0, g1, "prenorm_l0_mix_bwd")

    def rs_reduce(started, after, tag):
        send_sems, recv_sems, p, q, _ = started
        p, q = _a2a_wait(send_sems, recv_sems, p, q, after, tag)
        return _handover_start(_sum_chips(p, q, where, tag), tag)

    ho_a = rs_reduce(rs_a, grad_x, "a")
    ho_b = rs_reduce(rs_b, ho_a[3], "b")

    g_norm = jnp.stack([jnp.concatenate([dnw00, dnw01, dnw02, dnw03]), jnp.concatenate([dnw10, dnw11, dnw12, dnw13])])
    sharded = [(g_norm, 2), (g_wa_pad[:16], 1), (g_cw, 1), (g_cb[0], 0), (g_lba[0], 0), (g_lbx[0], 0), (g_lam[0], 0)]
    replicated = [g_gla_ba[0], g_gla_nw[0], g_fox_bpad[0, FOX_LANE0:FOX_LANE0 + 8], g_rel, _diag_blocks(g_wa_bd),
                  _diag_blocks(g_wx_bd)]
    small4 = jnp.concatenate([_shard_major(g, ax) for g, ax in sharded]
                             + [jnp.broadcast_to(g.reshape(1, -1), (4, g.size)) for g in replicated], axis=1)
    n_small = small4.shape[1]
    small_rows = GC_ROWS - GC_TAIL - 774
    small4 = jnp.pad(small4, ((0, 0), (0, small_rows * D - n_small))).reshape(4, small_rows, D)
    gt_rows = jnp.concatenate([gt_in_e[:1536], gt_in_e[3072:3088], gt_in_e[1536:3072], gt_in_e[3088:3096]], axis=0)
    tail = jnp.concatenate([gt_rows.reshape(4, 774, D), small4.astype(BF16)], axis=1)
    gbc = lax.dynamic_update_slice(gbc, tail, (0, GC_TAIL, 0))
    swap_c = _pair_swap_start(gbc, "c")
    rs_c = rs_begin(swap_c, swap_c[4], "c")

    red_a = _handover_wait(ho_a, rs_c[4], "a")
    red_b = _handover_wait(ho_b, red_a, "b")
    early = dict(
        w_mlp_up=_adamw_from(w_mlp_up, m_w_mlp_up, v_w_mlp_up, [(red_b, GB_UP, True), (red_a, GA_UP, True)], 256,
                             "adamw_w_mlp_up"),
        w_mlp_down=_adamw_from(w_mlp_down, m_w_mlp_down, v_w_mlp_down, [(red_b, GB_DN, False), (red_a, GA_DN, False)],
                               256, "adamw_w_mlp_down"),
        w_in_odd=_adamw_from(w_in_odd, m_w_in_odd, v_w_in_odd, [(red_a, GA_IN_O, True)], 256, "adamw_w_in_odd"),
        w_out_odd=_adamw_from(w_out_odd, m_w_out_odd, v_w_out_odd, [(red_a, GA_OUT_O, False)], 128, "adamw_w_out_odd"))
    red_c = rs_end(rs_c, early["w_out_odd"][3], "c")

    g_small = _split(red_c[GC_TAIL + 774:].reshape(-1)[:n_small], SMALL_SHARDED_SHAPES + REPL_SHAPES)
    g_of = dict(zip(["norm_w", "gla_w_a_up", "conv_w", "conv_b", "lru_b_a", "lru_b_x", "lru_lambda", "gla_b_a",
                     "gla_norm_w", "fox_b_f", "rel_bias", "lru_w_a", "lru_w_x"], g_small))
    g_of.update(w_in_even=red_c[GC_TAIL:GC_TAIL + 774])
    early["w_out_even"] = _adamw_from(w_out_even, m_w_out_even, v_w_out_even, [(red_c, GC_OUT_E, False)], 256,
                                      "adamw_w_out_even")

    names = ["norm_w", "w_in_even", "gla_w_a_up", "gla_b_a", "gla_norm_w", "fox_b_f", "w_out_even", "w_in_odd", "rel_bias",
             "conv_w", "conv_b", "lru_w_a", "lru_b_a", "lru_w_x", "lru_b_x", "lru_lambda", "w_out_odd", "w_mlp_up",
             "w_mlp_down"]
    w_of = dict(norm_w=norm_w, w_in_even=w_in_even, gla_w_a_up=gla_w_a_up, gla_b_a=gla_b_a, gla_norm_w=gla_norm_w,
                fox_b_f=fox_b_f, w_out_even=w_out_even, w_in_odd=w_in_odd, rel_bias=rel_bias, conv_w=conv_w, conv_b=conv_b,
                lru_w_a=lru_w_a, lru_b_a=lru_b_a, lru_w_x=lru_w_x, lru_b_x=lru_b_x, lru_lambda=lru_lambda,
                w_out_odd=w_out_odd, w_mlp_up=w_mlp_up, w_mlp_down=w_mlp_down)
    m_of = dict(norm_w=m_norm_w, w_in_even=m_w_in_even, gla_w_a_up=m_gla_w_a_up, gla_b_a=m_gla_b_a,
                gla_norm_w=m_gla_norm_w, fox_b_f=m_fox_b_f, w_out_even=m_w_out_even, w_in_odd=m_w_in_odd,
                rel_bias=m_rel_bias, conv_w=m_conv_w, conv_b=m_conv_b, lru_w_a=m_lru_w_a, lru_b_a=m_lru_b_a,
                lru_w_x=m_lru_w_x, lru_b_x=m_lru_b_x, lru_lambda=m_lru_lambda, w_out_odd=m_w_out_odd,
                w_mlp_up=m_w_mlp_up, w_mlp_down=m_w_mlp_down)
    v_of = dict(norm_w=v_norm_w, w_in_even=v_w_in_even, gla_w_a_up=v_gla_w_a_up, gla_b_a=v_gla_b_a,
                gla_norm_w=v_gla_norm_w, fox_b_f=v_fox_b_f, w_out_even=v_w_out_even, w_in_odd=v_w_in_odd,
                rel_bias=v_rel_bias, conv_w=v_conv_w, conv_b=v_conv_b, lru_w_a=v_lru_w_a, lru_b_a=v_lru_b_a,
                lru_w_x=v_lru_w_x, lru_b_x=v_lru_b_x, lru_lambda=v_lru_lambda, w_out_odd=v_w_out_odd,
                w_mlp_up=v_w_mlp_up, w_mlp_down=v_w_mlp_down)
    grads, deltas, new_ms, new_vs = [], [], [], []
    for n in names:
        w = w_of[n]
        if n in early:
            g, d, mn, vn = early[n]
            grads.append(g)
            deltas.append(d)
            new_ms.append(mn)
            new_vs.append(vn)
            continue
        if n == "w_in_even":
            to_view = lambda a: a[0].T
            from_view = lambda a: a.T[None]
        else:
            view = w.shape if w.ndim <= 3 else w.shape[-3:]
            to_view = lambda a, view=view: a.reshape(view)
            from_view = lambda a, w=w: a.reshape(w.shape)
        g = g_of[n] if n == "w_in_even" else to_view(g_of[n])
        d, mn, vn = _adamw(to_view(w), g, to_view(m_of[n]), to_view(v_of[n]), "adamw_" + n)
        grads.append(from_view(g))
        deltas.append(from_view(d))
        new_ms.append(from_view(mn))
        new_vs.append(from_view(vn))

    return (loss, grad_x.reshape(1, T, D), *grads, *deltas, *new_ms, *new_vs)
```

```python
import functools

import jax
import jax.numpy as jnp
from jax import lax
from jax.experimental import pallas as pl
from jax.experimental.pallas import tpu as pltpu

F32 = jnp.float32
BF16 = jnp.bfloat16
MESH = pl.DeviceIdType.MESH

T = 2048
D = 1024
DFF = 4096
EPS = 1e-6
CHUNK = 64
NCHUNK = T // CHUNK
PE = 3200
PO = 2560
AUX_BLK = 3072 // 128
FOX_LANE0 = 16
GLA_SCALE = 64 ** -0.5
ATT_SCALE = 64 ** -0.5
NEG = float(jnp.finfo(jnp.float32).min)
CA_BAND = 576
CA_PAD = 512
REL_PAD = 384

VMEM_LIMIT = 48 * 1024 * 1024

ADAM_LR, ADAM_B1, ADAM_B2, ADAM_EPS, ADAM_WD, ADAM_STEP = 0.001, 0.9, 0.999, 1e-08, 0.01, 10

GA_ROWS, GA_IN_O, GA_OUT_O, GA_GAP, GA_UP, GA_DN = 3072, 0, 640, 896, 1024, 2048
GB_ROWS, GB_UP, GB_DN = 2048, 0, 1024
GC_ROWS, GC_OUT_E, GC_TAIL = 1152, 0, 256

_DIMS = {"nn": (((1,), (0,)), ((), ())), "nt": (((1,), (1,)), ((), ())), "tn": (((0,), (0,)), ((), ()))}


def _cp(sem, **kw):
    return pltpu.CompilerParams(dimension_semantics=sem, vmem_limit_bytes=VMEM_LIMIT, **kw)


def _dot(a, b, mode):
    return lax.dot_general(a.astype(BF16), b.astype(BF16), _DIMS[mode], preferred_element_type=F32)


@functools.partial(jax.custom_vjp, nondiff_argnums=(2,))
def bdot(a, b, mode):
    return _dot(a, b, mode)


def _bdot_fwd(a, b, mode):
    return _dot(a, b, mode), (a, b)


def _bdot_bwd(mode, res, g):
    a, b = res
    if mode == "nn":
        da, db = _dot(g, b, "nt"), _dot(a, g, "tn")
    elif mode == "nt":
        da, db = _dot(g, b, "nn"), _dot(g, a, "tn")
    else:
        da, db = _dot(b, g, "nt"), _dot(a, g, "nn")
    return da.astype(a.dtype), db.astype(b.dtype)


bdot.defvjp(_bdot_fwd, _bdot_bwd)


def _hdot_raw(a, b, mode):
    return lax.dot_general(a, b, _DIMS[mode], precision=lax.Precision.HIGHEST, preferred_element_type=F32)


def _log_sigmoid(x):
    return jnp.minimum(x, 0.0) - jnp.log(1.0 + jnp.exp(-jnp.abs(x)))


def _sigmoid(x):
    return 1.0 / (1.0 + jnp.exp(-x))


def _expm1(x):
    series = x * (1.0 + x * 0.5 * (1.0 + x * (1.0 / 3.0) * (1.0 + x * 0.25)))
    return jnp.where(jnp.abs(x) < 0.03, series, jnp.exp(x) - 1.0)


def _gelu_tanh(x):
    return 0.5 * x * (1.0 + jnp.tanh(0.7978845608028654 * (x + 0.044715 * x * x * x)))


def _iota(shape, dim):
    return lax.broadcasted_iota(jnp.int32, shape, dim)


def _mm(a, b, mode, *, tm, tn, tk=None, out_dtype=F32, name, b_layer=None, into=None, relu_pair=False, times2=None):
    b2 = b.shape[-2:]
    if mode == "nn":
        (m, k), n = a.shape, b2[1]
    elif mode == "nt":
        (m, k), n = a.shape, b2[0]
    else:
        (k, m), n = a.shape, b2[1]
    tk = k if tk is None else tk
    assert m % tm == 0 and n % tn == 0 and k % tk == 0, (name, a.shape, b.shape)
    nk = k // tk
    if mode == "tn":
        a_spec = pl.BlockSpec((tk, tm), lambda i, j, kk: (kk, i))
    elif m == tm and nk == 1:
        a_spec = pl.BlockSpec((tm, tk), lambda i, j, kk: (i, kk), pipeline_mode=pl.Buffered(1))
    else:
        a_spec = pl.BlockSpec((tm, tk), lambda i, j, kk: (i, kk))
    b_blk = {"nn": (tk, tn), "nt": (tn, tk), "tn": (tk, tn)}[mode]
    b_idx = {"nn": lambda i, j, kk: (kk, j), "nt": lambda i, j, kk: (j, kk), "tn": lambda i, j, kk: (kk, j)}[mode]
    if b_layer is None:
        b_spec = pl.BlockSpec(b_blk, b_idx)
    else:
        b_spec = pl.BlockSpec((None,) + b_blk, lambda i, j, kk: (b_layer,) + b_idx(i, j, kk))

    tile = pl.BlockSpec((tm, tn), lambda i, j, kk: (i, j))
    if into is not None:
        buf, per_slot, row_off = into
        assert m == 4 * per_slot and per_slot % tm == 0 and row_off % tm == 0 and buf.shape[2] == n, (name, buf.shape)
        bps = per_slot // tm
        out_specs = pl.BlockSpec((None, tm, tn), lambda i, j, kk: (i // bps, row_off // tm + i % bps, j))
        out_shape = jax.ShapeDtypeStruct(buf.shape, buf.dtype)
        extra_in, extra_specs, aliases = [buf], [pl.BlockSpec(memory_space=pl.ANY)], {2: 0}
        finish = lambda acc, extra: [acc.astype(buf.dtype)]
    elif relu_pair:
        out_specs = (tile, tile)
        out_shape = (jax.ShapeDtypeStruct((m, n), BF16),) * 2
        extra_in, extra_specs, aliases = [], [], {}

        def finish(acc, extra):
            r = jnp.maximum(acc, 0.0)
            return [(r * r).astype(BF16), r.astype(BF16)]
    elif times2 is not None:
        out_specs = tile
        out_shape = jax.ShapeDtypeStruct((m, n), out_dtype)
        extra_in, extra_specs, aliases = [times2], [tile], {}
        finish = lambda acc, extra: [(acc * (2.0 * extra[...].astype(F32))).astype(out_dtype)]
    else:
        out_specs = tile
        out_shape = jax.ShapeDtypeStruct((m, n), out_dtype)
        extra_in, extra_specs, aliases = [], [], {}
        finish = lambda acc, extra: [acc.astype(out_dtype)]
    n_out = 2 if relu_pair else 1

    def body(*refs):
        a_ref, b_ref = refs[0], refs[1]
        extra = refs[2] if extra_in else None
        o_refs = refs[2 + len(extra_in):2 + len(extra_in) + n_out]

        def store(acc):
            for o_ref, val in zip(o_refs, finish(acc, extra)):
                o_ref[...] = val

        if nk == 1:
            store(_dot(a_ref[...], b_ref[...], mode))
            return
        acc_ref = refs[-1]
        kk = pl.program_id(2)

        @pl.when(kk == 0)
        def _():
            acc_ref[...] = jnp.zeros_like(acc_ref)

        acc_ref[...] += _dot(a_ref[...], b_ref[...], mode)

        @pl.when(kk == nk - 1)
        def _():
            store(acc_ref[...])

    return pl.pallas_call(
        body, name=name, grid=(m // tm, n // tn, nk),
        in_specs=[a_spec, b_spec] + extra_specs,
        out_specs=out_specs, out_shape=out_shape,
        scratch_shapes=[pltpu.VMEM((tm, tn), F32)] if nk > 1 else [],
        input_output_aliases=aliases,
        compiler_params=_cp(("parallel", "parallel", "arbitrary")),
    )(a, b, *extra_in)


ROWS = 512


def _prenorm(x, w, name):
    def body(x_ref, w_ref, o_ref):
        xv = x_ref[...]
        r = lax.rsqrt(jnp.mean(xv * xv, axis=-1, keepdims=True) + EPS)
        o_ref[...] = (xv * r * w_ref[...]).astype(BF16)

    return pl.pallas_call(
        body, name=name, grid=(T // ROWS,),
        in_specs=[pl.BlockSpec((ROWS, D), lambda i: (i, 0)), pl.BlockSpec((1, D), lambda i: (0, 0))],
        out_specs=pl.BlockSpec((ROWS, D), lambda i: (i, 0)),
        out_shape=jax.ShapeDtypeStruct((T, D), BF16),
        compiler_params=_cp(("parallel",)),
    )(x, w)


def _rms(z):
    return lax.rsqrt(jnp.mean(z * z, axis=-1, keepdims=True) + EPS)


def _rms_bwd(z, w, dy):
    r = _rms(z)
    wdy = dy * w
    dz = r * wdy - z * (r * r * r) * jnp.mean(z * wdy, axis=-1, keepdims=True)
    return dz, jnp.sum(dy * z * r, axis=0, keepdims=True)


_ROW = pl.BlockSpec((ROWS, D), lambda i: (i, 0))
_VEC = pl.BlockSpec((1, D), lambda i: (0, 0))


def _post_pre_fwd(x, z, w_post, w_pre, name):
    def body(x_ref, z_ref, wp_ref, wn_ref, x_out, h_out):
        zv = z_ref[...]
        xn = x_ref[...] + zv * _rms(zv) * wp_ref[...]
        x_out[...] = xn
        h_out[...] = (xn * _rms(xn) * wn_ref[...]).astype(BF16)

    return pl.pallas_call(
        body, name=name, grid=(T // ROWS,), in_specs=[_ROW, _ROW, _VEC, _VEC], out_specs=(_ROW, _ROW),
        out_shape=(jax.ShapeDtypeStruct((T, D), F32), jax.ShapeDtypeStruct((T, D), BF16)),
        compiler_params=_cp(("parallel",)),
    )(x, z, w_post, w_pre)


def _post_loss(x, z, w_post, tgt):
    def body(x_ref, z_ref, w_ref, t_ref, g_ref, l_ref, dz_ref, dw_ref):
        @pl.when(pl.program_id(0) == 0)
        def _():
            l_ref[...] = jnp.zeros_like(l_ref)
            dw_ref[...] = jnp.zeros_like(dw_ref)

        zv = z_ref[...]
        e = x_ref[...] + zv * _rms(zv) * w_ref[...] - t_ref[...]
        g = e * (1.0 / D)
        g_ref[...] = g
        l_ref[...] += jnp.sum(e * e) * (0.5 / D)
        dz, dw = _rms_bwd(zv, w_ref[...], g)
        dz_ref[...] = dz.astype(BF16)
        dw_ref[...] += dw

    return pl.pallas_call(
        body, name="postnorm_loss", grid=(T // ROWS,), in_specs=[_ROW, _ROW, _VEC, _ROW],
        out_specs=(_ROW, pl.BlockSpec((1, 128), lambda i: (0, 0)), _ROW, _VEC),
        out_shape=(jax.ShapeDtypeStruct((T, D), F32), jax.ShapeDtypeStruct((1, 128), F32),
                   jax.ShapeDtypeStruct((T, D), BF16), jax.ShapeDtypeStruct((1, D), F32)),
        compiler_params=_cp(("arbitrary",)),
    )(x, z, w_post, tgt)


def _pre_post_bwd(x, w_pre, dh, add, z, w_post, name):
    def body(x_ref, wn_ref, dh_ref, add_ref, z_ref, wp_ref, g_ref, dz_ref, dwn_ref, dwp_ref):
        @pl.when(pl.program_id(0) == 0)
        def _():
            dwn_ref[...] = jnp.zeros_like(dwn_ref)
            dwp_ref[...] = jnp.zeros_like(dwp_ref)

        dx, dwn = _rms_bwd(x_ref[...], wn_ref[...], dh_ref[...])
        g = dx + add_ref[...]
        g_ref[...] = g
        dz, dwp = _rms_bwd(z_ref[...], wp_ref[...], g)
        dz_ref[...] = dz.astype(BF16)
        dwn_ref[...] += dwn
        dwp_ref[...] += dwp

    return pl.pallas_call(
        body, name=name, grid=(T // ROWS,), in_specs=[_ROW, _VEC, _ROW, _ROW, _ROW, _VEC],
        out_specs=(_ROW, _ROW, _VEC, _VEC),
        out_shape=(jax.ShapeDtypeStruct((T, D), F32), jax.ShapeDtypeStruct((T, D), BF16),
                   jax.ShapeDtypeStruct((1, D), F32), jax.ShapeDtypeStruct((1, D), F32)),
        compiler_params=_cp(("arbitrary",)),
    )(x, w_pre, dh, add, z, w_post)


def _norm_bwd(z, w, dy, add, name):
    has_add = add is not None

    def body(*refs):
        if has_add:
            z_ref, w_ref, dy_ref, add_ref, dz_ref, dw_ref = refs
        else:
            z_ref, w_ref, dy_ref, dz_ref, dw_ref = refs
        i = pl.program_id(0)

        @pl.when(i == 0)
        def _():
            dw_ref[...] = jnp.zeros_like(dw_ref)

        zv = z_ref[...].astype(F32)
        dyv = dy_ref[...]
        r = lax.rsqrt(jnp.mean(zv * zv, axis=-1, keepdims=True) + EPS)
        wdy = dyv * w_ref[...]
        dz = r * wdy - zv * (r * r * r) * jnp.mean(zv * wdy, axis=-1, keepdims=True)
        if has_add:
            dz = dz + add_ref[...]
        dz_ref[...] = dz.astype(dz_ref.dtype)
        dw_ref[...] += jnp.sum(dyv * zv * r, axis=0, keepdims=True)

    row = pl.BlockSpec((ROWS, D), lambda i: (i, 0))
    vec = pl.BlockSpec((1, D), lambda i: (0, 0))
    ins = [z, w, dy] + ([add] if has_add else [])
    dz_dtype = F32 if has_add else BF16
    return pl.pallas_call(
        body, name=name, grid=(T // ROWS,),
        in_specs=[row, vec, row] + ([row] if has_add else []),
        out_specs=(row, vec),
        out_shape=(jax.ShapeDtypeStruct((T, D), dz_dtype), jax.ShapeDtypeStruct((1, D), F32)),
        compiler_params=_cp(("arbitrary",)),
    )(*ins)


def _adamw_math(w, g, m, v):
    c1 = 1.0 - ADAM_B1 ** ADAM_STEP
    c2 = 1.0 - ADAM_B2 ** ADAM_STEP
    mn = ADAM_B1 * m + (1.0 - ADAM_B1) * g
    vn = ADAM_B2 * v + (1.0 - ADAM_B2) * (g * g)
    return -ADAM_LR * ((mn / c1) / (jnp.sqrt(vn / c2) + ADAM_EPS) + ADAM_WD * w), mn, vn


def _adamw_from(w, m, v, sources, tr, name):
    layers, rows, cols = w.shape
    assert len(sources) == layers and rows % tr == 0, (name, w.shape)
    g_specs = []
    for layer, (buf, row0, transposed) in enumerate(sources):
        step = lambda l, i, layer=layer: jnp.where(l == layer, i, 0)
        if transposed:
            assert row0 % cols == 0 and buf.shape[1] == rows, (name, row0)
            g_specs.append(pl.BlockSpec((cols, tr), lambda l, i, b=row0 // cols, step=step: (b, step(l, i))))
        else:
            assert row0 % tr == 0 and buf.shape[1] == cols, (name, row0)
            g_specs.append(pl.BlockSpec((tr, cols), lambda l, i, b=row0 // tr, step=step: (b + step(l, i), 0)))

    def body(*refs):
        w_ref, m_ref, v_ref = refs[:3]
        g_refs = refs[3:3 + layers]
        g_out, d_ref, mo_ref, vo_ref = refs[3 + layers:]
        gs = [r[...].T if src[2] else r[...] for r, src in zip(g_refs, sources)]
        g = gs[0] if layers == 1 else jnp.where(pl.program_id(0) == 0, gs[0], gs[1])
        g_out[...] = g
        d_ref[...], mo_ref[...], vo_ref[...] = _adamw_math(w_ref[...], g, m_ref[...], v_ref[...])

    blk = pl.BlockSpec((None, tr, cols), lambda l, i: (l, i, 0))
    sds = jax.ShapeDtypeStruct(w.shape, F32)
    return pl.pallas_call(body, name=name, grid=(layers, rows // tr), in_specs=[blk] * 3 + g_specs,
                          out_specs=(blk,) * 4, out_shape=(sds,) * 4,
                          compiler_params=_cp(("parallel", "parallel")))(w, m, v, *[s[0] for s in sources])


def _adamw(w, g, m, v, name):
    lead = w.shape[:-2]
    assert len(lead) <= 1 and g.shape == w.shape, (name, w.shape, g.shape)
    rows, cols = w.shape[-2:]
    if rows <= 512:
        tr, tc = rows, cols
    elif rows % 256 == 0:
        tr, tc = 256, cols
    else:
        tr, tc = rows, 256
    assert rows % tr == 0 and cols % tc == 0, (name, w.shape)
    c1 = 1.0 - ADAM_B1 ** ADAM_STEP
    c2 = 1.0 - ADAM_B2 ** ADAM_STEP

    def body(w_ref, g_ref, m_ref, v_ref, d_ref, mo_ref, vo_ref):
        gv = g_ref[...]
        mn = ADAM_B1 * m_ref[...] + (1.0 - ADAM_B1) * gv
        vn = ADAM_B2 * v_ref[...] + (1.0 - ADAM_B2) * (gv * gv)
        m_hat = mn / c1
        v_hat = vn / c2
        d_ref[...] = -ADAM_LR * (m_hat / (jnp.sqrt(v_hat) + ADAM_EPS) + ADAM_WD * w_ref[...])
        mo_ref[...] = mn
        vo_ref[...] = vn

    if lead:
        grid = (lead[0], rows // tr, cols // tc)
        blk = pl.BlockSpec((None, tr, tc), lambda l, i, j: (l, i, j))
    else:
        grid = (rows // tr, cols // tc)
        blk = pl.BlockSpec((tr, tc), lambda i, j: (i, j))
    sds = jax.ShapeDtypeStruct(w.shape, F32)
    return pl.pallas_call(body, name=name, grid=grid, in_specs=[blk] * 4, out_specs=(blk,) * 3,
                          out_shape=(sds,) * 3, compiler_params=_cp(("parallel",) * len(grid)))(w, g, m, v)


def _running_sum(x, towards_later):
    n = x.shape[0]
    row = _iota(x.shape, 0)
    s = 1
    while s < n:
        if towards_later:
            x = x + jnp.where(row >= s, pltpu.roll(x, s, 0), 0.0)
        else:
            x = x + jnp.where(row < n - s, pltpu.roll(x, n - s, 0), 0.0)
        s *= 2
    return x


@jax.custom_vjp
def _cumsum_rows(x):
    return _running_sum(x, True)


_cumsum_rows.defvjp(lambda x: (_running_sum(x, True), None), lambda _, g: (_running_sum(g, False),))


def _gla_consts():
    return (_iota((256, 512), 0) // 64 == _iota((256, 512), 1) // 128).astype(F32)


def _gla_chunk(mask, q, k, v, r, aux, s_prev, wa, ba, nw):
    la = _log_sigmoid(bdot(aux, wa, "nn") + ba) * (1.0 / 16.0)
    cum = _cumsum_rows(la)
    total = jnp.sum(la, axis=0, keepdims=True)
    k_dec = k * jnp.exp(total - cum)
    inc = bdot(k_dec, v, "tn") * mask
    dec = jnp.exp(jnp.broadcast_to(total, (128, 256)).T)
    dec = jnp.concatenate([dec, dec, dec, dec], axis=1)
    s_new = dec * s_prev + inc
    o = bdot(q * GLA_SCALE, s_new, "nn")
    parts = []
    for h in range(4):
        oh = o[:, h * 128:(h + 1) * 128]
        parts.append(oh * lax.rsqrt(jnp.mean(oh * oh, axis=-1, keepdims=True) + EPS))
    on = jnp.concatenate(parts, axis=1)
    return s_new, on * nw * (r * _sigmoid(r))


GLA_PER_STEP = 4
GLA_ROWS = GLA_PER_STEP * CHUNK
GLA_STEPS = NCHUNK // GLA_PER_STEP


def _gla_specs(cmap):
    return [pl.BlockSpec((GLA_ROWS, 256), lambda c: (cmap(c), 0)),
            pl.BlockSpec((GLA_ROWS, 256), lambda c: (cmap(c), 1)),
            pl.BlockSpec((GLA_ROWS, 512), lambda c: (cmap(c), 1)),
            pl.BlockSpec((GLA_ROWS, 512), lambda c: (cmap(c), 2)),
            pl.BlockSpec((GLA_ROWS, 128), lambda c: (cmap(c), AUX_BLK))]


def _gla_fwd(proj, wa, ba, nw):
    def body(q_ref, k_ref, v_ref, r_ref, aux_ref, wa_ref, ba_ref, nw_ref, o_ref, sp_ref, s_ref):
        @pl.when(pl.program_id(0) == 0)
        def _():
            s_ref[...] = jnp.zeros_like(s_ref)

        s = s_ref[...]
        consts = _gla_consts()
        outs, states = [], []
        for i in range(GLA_PER_STEP):
            rows = slice(i * CHUNK, (i + 1) * CHUNK)
            states.append(s)
            s, out = _gla_chunk(consts, q_ref[rows, :], k_ref[rows, :], v_ref[rows, :], r_ref[rows, :], aux_ref[rows, :],
                                s, wa_ref[...], ba_ref[...], nw_ref[...])
            outs.append(out)
        s_ref[...] = s
        for i in range(GLA_PER_STEP):
            o_ref[i * CHUNK:(i + 1) * CHUNK, :] = outs[i]
            sp_ref[i] = states[i]

    full = lambda shape: pl.BlockSpec(shape, lambda c: (0,) * len(shape))
    return pl.pallas_call(
        body, name="gla_fwd", grid=(GLA_STEPS,),
        in_specs=_gla_specs(lambda c: c) + [full((128, 256)), full((1, 256)), full((1, 512))],
        out_specs=(pl.BlockSpec((GLA_ROWS, 512), lambda c: (c, 0)),
                   pl.BlockSpec((GLA_PER_STEP, 256, 512), lambda c: (c, 0, 0))),
        out_shape=(jax.ShapeDtypeStruct((T, D), F32), jax.ShapeDtypeStruct((NCHUNK, 256, 512), F32)),
        scratch_shapes=[pltpu.VMEM((256, 512), F32)],
        compiler_params=_cp(("arbitrary",)),
    )(proj, proj, proj, proj, proj, wa, ba, nw)


def _gla_bwd(proj, s_prev_all, wa, ba, nw, dcat):
    rev = lambda c: GLA_STEPS - 1 - c

    def body(q_ref, k_ref, v_ref, r_ref, aux_ref, sp_ref, wa_ref, ba_ref, nw_ref, do_ref,
             dq_ref, dk_ref, dv_ref, dr_ref, daux_ref, dwa_ref, dba_ref, dnw_ref, ds_ref):
        @pl.when(pl.program_id(0) == 0)
        def _():
            ds_ref[...] = jnp.zeros_like(ds_ref)
            dwa_ref[...] = jnp.zeros_like(dwa_ref)
            dba_ref[...] = jnp.zeros_like(dba_ref)
            dnw_ref[...] = jnp.zeros_like(dnw_ref)

        fn = functools.partial(_gla_chunk, _gla_consts())
        ds = ds_ref[...]
        dwa, dba, dnw = dwa_ref[...], dba_ref[...], dnw_ref[...]
        grads = {}
        for i in reversed(range(GLA_PER_STEP)):
            rows = slice(i * CHUNK, (i + 1) * CHUNK)
            _, vjp = jax.vjp(fn, q_ref[rows, :], k_ref[rows, :], v_ref[rows, :], r_ref[rows, :], aux_ref[rows, :],
                             sp_ref[i], wa_ref[...], ba_ref[...], nw_ref[...])
            *grads[i], ds, dwa_i, dba_i, dnw_i = vjp((ds, do_ref[rows, :]))
            dwa, dba, dnw = dwa + dwa_i, dba + dba_i, dnw + dnw_i
        ds_ref[...] = ds
        dwa_ref[...] = dwa
        dba_ref[...] = dba
        dnw_ref[...] = dnw
        for i in range(GLA_PER_STEP):
            rows = slice(i * CHUNK, (i + 1) * CHUNK)
            for ref, g in zip((dq_ref, dk_ref, dv_ref, dr_ref, daux_ref), grads[i]):
                ref[rows, :] = g

    full = lambda shape: pl.BlockSpec(shape, lambda c: (0,) * len(shape))
    blk = lambda w: pl.BlockSpec((GLA_ROWS, w), lambda c: (rev(c), 0))
    sds = lambda *s: jax.ShapeDtypeStruct(s, F32)
    return pl.pallas_call(
        body, name="gla_bwd", grid=(GLA_STEPS,),
        in_specs=_gla_specs(rev) + [pl.BlockSpec((GLA_PER_STEP, 256, 512), lambda c: (rev(c), 0, 0)),
                                    full((128, 256)), full((1, 256)), full((1, 512)), blk(512)],
        out_specs=(blk(256), blk(256), blk(512), blk(512), blk(128), full((128, 256)), full((1, 256)), full((1, 512))),
        out_shape=(sds(T, 256), sds(T, 256), sds(T, 512), sds(T, 512), sds(T, 128),
                   sds(128, 256), sds(1, 256), sds(1, 512)),
        scratch_shapes=[pltpu.VMEM((256, 512), F32)],
        compiler_params=_cp(("arbitrary",)),
    )(proj, proj, proj, proj, proj, s_prev_all, wa, ba, nw, dcat)


def _prefix8(x, towards_later):
    row = _iota(x.shape, 0)
    for s in (1, 2, 4):
        if towards_later:
            keep, shift = row >= s, s
        else:
            keep, shift = row < 8 - s, 8 - s
        x = x + jnp.where(keep, pltpu.roll(x, shift, 0), 0.0)
    return x


def _fox_gate_fwd(proj, bpad):
    def body(aux_ref, b_ref, cum_ref):
        cum_ref[...] = _log_sigmoid(aux_ref[...] + b_ref[...])

        def step(i, carry):
            rows = pl.ds(pl.multiple_of(i * 8, 8), 8)
            cum = _prefix8(cum_ref[rows, :], True) + carry
            cum_ref[rows, :] = cum
            return jnp.broadcast_to(cum[7:, :], (8, 128))

        lax.fori_loop(0, T // 8, step, jnp.zeros((8, 128), F32), unroll=4)

    return pl.pallas_call(
        body, name="fox_gate_fwd", grid=(1,),
        in_specs=[pl.BlockSpec((T, 128), lambda i: (0, AUX_BLK)), pl.BlockSpec((1, 128), lambda i: (0, 0))],
        out_specs=pl.BlockSpec((T, 128), lambda i: (0, 0)),
        out_shape=jax.ShapeDtypeStruct((T, 128), F32),
        compiler_params=_cp(("arbitrary",)),
    )(proj, bpad)


def _fox_gate_bwd(proj, bpad, dccol_t, daux_gla):
    def body(aux_ref, b_ref, dc_ref, dg_ref, daux_ref, db_ref):
        def step(i, carry):
            rows = pl.ds(pl.multiple_of(T - 8 * (i + 1), 8), 8)
            dlf = _prefix8(dc_ref[rows, :], False) + carry
            daux_ref[rows, :] = dlf
            return jnp.broadcast_to(dlf[:1, :], (8, 128))

        lax.fori_loop(0, T // 8, step, jnp.zeros((8, 128), F32), unroll=4)
        dz = daux_ref[...] * _sigmoid(-(aux_ref[...] + b_ref[...]))
        daux_ref[...] = dz + dg_ref[...]
        db_ref[...] = jnp.sum(dz, axis=0, keepdims=True)

    whole = pl.BlockSpec((T, 128), lambda i: (0, 0))
    vec = pl.BlockSpec((1, 128), lambda i: (0, 0))
    return pl.pallas_call(
        body, name="fox_gate_bwd", grid=(1,),
        in_specs=[pl.BlockSpec((T, 128), lambda i: (0, AUX_BLK)), vec, whole, whole],
        out_specs=(whole, vec),
        out_shape=(jax.ShapeDtypeStruct((T, 128), F32), jax.ShapeDtypeStruct((1, 128), F32)),
        compiler_params=_cp(("arbitrary",)),
    )(proj, bpad, dccol_t, daux_gla)


FOX_Q = 256


FOX_QB = T // FOX_Q


@jax.custom_vjp
def _attend(s, v):
    return _attend_fwd(s, v)[0]


def _attend_fwd(s, v):
    e = jnp.exp(s - jnp.max(s, axis=-1, keepdims=True))
    r = 1.0 / jnp.sum(e, axis=-1, keepdims=True)
    return _dot(e, v, "nn") * r, (e, r, v)


def _attend_bwd(res, do):
    e, r, v = res
    do_r = do * r
    dpr = _dot(do_r, v, "nt")
    ds = e * (dpr - r * jnp.sum(e * dpr, axis=-1, keepdims=True))
    return ds, _dot(e, do_r, "tn").astype(v.dtype)


_attend.defvjp(_attend_fwd, _attend_bwd)


def _fox_block(hp, q, k, v, ccol):
    kl = k.shape[0]
    lane = _iota((FOX_Q, 128), 1)
    tri = jnp.bitwise_and(_iota((2 * FOX_Q, FOX_Q), 0), FOX_Q - 1) >= _iota((2 * FOX_Q, FOX_Q), 1)
    sub = _iota((8, kl), 0)
    qs = q * ATT_SCALE
    q2 = jnp.concatenate([jnp.where(lane < 64, qs, 0.0), jnp.where(lane >= 64, qs, 0.0)], axis=0)
    s = bdot(q2, k, "nt")
    cs = [jnp.sum(jnp.where(sub == 2 * hp + e, ccol, 0.0), axis=0, keepdims=True) for e in range(2)]
    s = jnp.concatenate([s[:FOX_Q] - cs[0], s[FOX_Q:] - cs[1]], axis=0)
    diag = jnp.where(tri, s[:, kl - FOX_Q:], NEG)
    s = diag if kl == FOX_Q else jnp.concatenate([s[:, :kl - FOX_Q], diag], axis=1)
    o2 = _attend(s, v)
    return jnp.where(lane < 64, o2[:FOX_Q], o2[FOX_Q:])


def _fox_in_specs():
    return [pl.BlockSpec((FOX_Q, 128), lambda hp, qb: (qb, 12 + hp)),
            pl.BlockSpec((T, 128), lambda hp, qb: (0, 16 + hp)),
            pl.BlockSpec((T, 128), lambda hp, qb: (0, 20 + hp)),
            pl.BlockSpec((8, T), lambda hp, qb: (0, 0))]


def _fox_fwd(proj, cum_c, cat):
    def body(q_ref, k_ref, v_ref, cc_ref, cat_ref, o_ref):
        qb = pl.program_id(1)
        for g in range(FOX_QB):
            kl = FOX_Q * (g + 1)

            @pl.when(qb == g)
            def _(kl=kl):
                o_ref[...] = _fox_block(pl.program_id(0), q_ref[...], k_ref[0:kl, :], v_ref[0:kl, :], cc_ref[:, 0:kl])

    return pl.pallas_call(
        body, name="fox_fwd", grid=(4, FOX_QB), in_specs=_fox_in_specs() + [pl.BlockSpec(memory_space=pl.ANY)],
        out_specs=pl.BlockSpec((FOX_Q, 128), lambda hp, qb: (qb, 4 + hp)),
        out_shape=jax.ShapeDtypeStruct((T, D), F32), input_output_aliases={4: 0},
        compiler_params=_cp(("parallel", "parallel")),
    )(proj, proj, proj, cum_c, cat)


def _fox_bwd(proj, cum_c, dcat):
    def body(q_ref, k_ref, v_ref, cc_ref, do_ref, dq_ref, dk_ref, dv_ref, dcc_ref):
        qb = pl.program_id(1)

        @pl.when(qb == 0)
        def _():
            dk_ref[...] = jnp.zeros_like(dk_ref)
            dv_ref[...] = jnp.zeros_like(dv_ref)
            dcc_ref[...] = jnp.zeros_like(dcc_ref)

        fn = functools.partial(_fox_block, pl.program_id(0))
        for g in range(FOX_QB):
            kl = FOX_Q * (g + 1)

            @pl.when(qb == g)
            def _(kl=kl):
                _, vjp = jax.vjp(fn, q_ref[...], k_ref[0:kl, :], v_ref[0:kl, :], cc_ref[:, 0:kl])
                dq, dk, dv, dcc = vjp(do_ref[...])
                dq_ref[...] = dq
                dk_ref[0:kl, :] += dk
                dv_ref[0:kl, :] += dv
                dcc_ref[:, 0:kl] += dcc

    sds = lambda *s: jax.ShapeDtypeStruct(s, F32)
    return pl.pallas_call(
        body, name="fox_bwd", grid=(4, FOX_QB),
        in_specs=_fox_in_specs() + [pl.BlockSpec((FOX_Q, 128), lambda hp, qb: (qb, 4 + hp))],
        out_specs=(pl.BlockSpec((FOX_Q, 128), lambda hp, qb: (qb, hp)),
                   pl.BlockSpec((T, 128), lambda hp, qb: (0, hp)),
                   pl.BlockSpec((T, 128), lambda hp, qb: (0, hp)),
                   pl.BlockSpec((None, 8, T), lambda hp, qb: (hp, 0, 0))),
        out_shape=(sds(T, 512), sds(T, 512), sds(T, 512), sds(4, 8, T)),
        compiler_params=_cp(("parallel", "arbitrary")),
    )(proj, proj, proj, cum_c, dcat)


BIAS_W = 640


def _rel_onehot():
    j = _iota((REL_PAD, BIAS_W), 1)
    rel = jnp.clip(CA_PAD + CHUNK - 1 - j, -128, 128) + 128
    return (_iota((REL_PAD, BIAS_W), 0) == rel).astype(F32)


def _bias_build(rbp):
    def body(rb_ref, o_ref):
        f = _hdot_raw(rb_ref[...], _rel_onehot(), "nn")
        for q in range(CHUNK):
            o_ref[q] = pltpu.roll(f, (BIAS_W - (CHUNK - 1 - q)) % BIAS_W, 1)[:, :CA_BAND]

    return pl.pallas_call(body, name="ca_bias_build", out_shape=jax.ShapeDtypeStruct((CHUNK, 8, CA_BAND), F32))(rbp)


def _bias_grad(dbias_q):
    def body(db_ref, o_ref):
        acc = jnp.zeros((8, BIAS_W), F32)
        for q in range(CHUNK):
            acc = acc + pltpu.roll(db_ref[q], CHUNK - 1 - q, 1)
        o_ref[...] = _hdot_raw(acc, _rel_onehot(), "nt")

    return pl.pallas_call(body, name="ca_bias_grad", out_shape=jax.ShapeDtypeStruct((8, REL_PAD), F32))(dbias_q)


def _ca_block(c, masked, q, kb, vb, bias2):
    lane = _iota((CHUNK, 128), 1)
    qs = q * ATT_SCALE
    q2 = jnp.concatenate([jnp.where(lane < 64, qs, 0.0), jnp.where(lane >= 64, qs, 0.0)], axis=0)
    s = bdot(q2, kb, "nt") + bias2.reshape(2 * CHUNK, CA_BAND)
    if masked:
        s = jnp.where((c * CHUNK - CA_PAD + _iota((2 * CHUNK, CA_BAND), 1)) >= 0, s, NEG)
    o2 = _attend(s, vb)
    return jnp.where(lane < 64, o2[:CHUNK], o2[CHUNK:])


CA_PER_STEP = 8
CA_ROWS = CA_PER_STEP * CHUNK
CA_MASKED_STEPS = -(-CA_PAD // CA_ROWS)


def _ca_fwd(proj, kvpad, bias):
    def body(q_ref, k_ref, v_ref, b_ref, o_ref):
        def run(masked):
            outs = []
            for i in range(CA_PER_STEP):
                c = pl.program_id(1) * CA_PER_STEP + i
                band = pl.ds(pl.multiple_of(c * CHUNK, CHUNK), CA_BAND)
                rows = slice(i * CHUNK, (i + 1) * CHUNK)
                outs.append(_ca_block(c, masked, q_ref[rows, :], k_ref[band, :], v_ref[band, :], b_ref[...]))
            for i in range(CA_PER_STEP):
                o_ref[i * CHUNK:(i + 1) * CHUNK, :] = outs[i]

        pl.when(pl.program_id(1) < CA_MASKED_STEPS)(lambda: run(True))
        pl.when(pl.program_id(1) >= CA_MASKED_STEPS)(lambda: run(False))

    return pl.pallas_call(
        body, name="ca_fwd", grid=(4, NCHUNK // CA_PER_STEP),
        in_specs=[pl.BlockSpec((CA_ROWS, 128), lambda hp, c: (c, hp)),
                  pl.BlockSpec((T + CA_PAD, 128), lambda hp, c: (0, hp)),
                  pl.BlockSpec((T + CA_PAD, 128), lambda hp, c: (0, 4 + hp)),
                  pl.BlockSpec((2, CHUNK, CA_BAND), lambda hp, c: (hp, 0, 0))],
        out_specs=pl.BlockSpec((CA_ROWS, 128), lambda hp, c: (c, hp)),
        out_shape=jax.ShapeDtypeStruct((T, D), F32),
        compiler_params=_cp(("parallel", "parallel")),
    )(proj, kvpad, kvpad, bias)


def _ca_bwd(proj, kvpad, bias, dcat):
    def body(q_ref, k_ref, v_ref, b_ref, do_ref, dq_ref, dk_ref, dv_ref, db_ref):
        c = pl.program_id(1)

        @pl.when(c == 0)
        def _():
            dk_ref[...] = jnp.zeros_like(dk_ref)
            dv_ref[...] = jnp.zeros_like(dv_ref)
            db_ref[...] = jnp.zeros_like(db_ref)

        def run(masked):
            grads, bands = [], []
            for i in range(CA_PER_STEP):
                ci = c * CA_PER_STEP + i
                band = pl.ds(pl.multiple_of(ci * CHUNK, CHUNK), CA_BAND)
                rows = slice(i * CHUNK, (i + 1) * CHUNK)
                fn = functools.partial(_ca_block, ci, masked)
                _, vjp = jax.vjp(fn, q_ref[rows, :], k_ref[band, :], v_ref[band, :], b_ref[...])
                grads.append(vjp(do_ref[rows, :]))
                bands.append(band)
            for i, (dq, _, _, _) in enumerate(grads):
                dq_ref[i * CHUNK:(i + 1) * CHUNK, :] = dq
            for band, (_, dkb, dvb, _) in zip(bands, grads):
                dk_ref[band, :] += dkb
                dv_ref[band, :] += dvb
            db_ref[...] += functools.reduce(lambda a, b: a + b, [g[3] for g in grads])

        pl.when(c < CA_MASKED_STEPS)(lambda: run(True))
        pl.when(c >= CA_MASKED_STEPS)(lambda: run(False))

    sds = lambda *s: jax.ShapeDtypeStruct(s, F32)
    padded = lambda: pl.BlockSpec((T + CA_PAD, 128), lambda hp, c: (0, hp))
    return pl.pallas_call(
        body, name="ca_bwd", grid=(4, NCHUNK // CA_PER_STEP),
        in_specs=[pl.BlockSpec((CA_ROWS, 128), lambda hp, c: (c, hp)),
                  pl.BlockSpec((T + CA_PAD, 128), lambda hp, c: (0, hp)),
                  pl.BlockSpec((T + CA_PAD, 128), lambda hp, c: (0, 4 + hp)),
                  pl.BlockSpec((2, CHUNK, CA_BAND), lambda hp, c: (hp, 0, 0)),
                  pl.BlockSpec((CA_ROWS, 128), lambda hp, c: (c, hp))],
        out_specs=(pl.BlockSpec((CA_ROWS, 128), lambda hp, c: (c, hp)), padded(), padded(),
                   pl.BlockSpec((2, CHUNK, CA_BAND), lambda hp, c: (hp, 0, 0))),
        out_shape=(sds(T, 512), sds(T + CA_PAD, 512), sds(T + CA_PAD, 512), sds(8, CHUNK, CA_BAND)),
        compiler_params=_cp(("parallel", "arbitrary")),
    )(proj, kvpad, kvpad, bias, dcat)


def _block_diag_dot(x, w):
    return jnp.concatenate([bdot(x[:, :256], w[:256, :256], "nn"), bdot(x[:, 256:], w[256:, 256:], "nn")], axis=1)


def _lru_pre(xs, cw, cb, wa, ba, wx, bx, lam):
    xc = cb + xs[0] * cw[0:1, :] + xs[1] * cw[1:2, :] + xs[2] * cw[2:3, :] + xs[3] * cw[3:4, :]
    ra = _sigmoid(_block_diag_dot(xc, wa) + ba)
    ii = _sigmoid(_block_diag_dot(xc, wx) + bx)
    la = 8.0 * ra * _log_sigmoid(lam)
    return jnp.exp(la), jnp.sqrt(-_expm1(2.0 * la)) * (ii * xc)


def _lru_pre_specs():
    full = lambda shape: pl.BlockSpec(shape, lambda i: (0,) * len(shape))
    return [pl.BlockSpec((4, ROWS, 512), lambda i: (0, i, 0)), full((4, 512)), full((1, 512)),
            full((512, 512)), full((1, 512)), full((512, 512)), full((1, 512)), full((1, 512))]


def _lru_pre_fwd(xs, cw, cb, wa, ba, wx, bx, lam):
    def body(xs_ref, cw_ref, cb_ref, wa_ref, ba_ref, wx_ref, bx_ref, lam_ref, a_ref, b_ref):
        a, b = _lru_pre(xs_ref[...], cw_ref[...], cb_ref[...], wa_ref[...], ba_ref[...], wx_ref[...], bx_ref[...],
                        lam_ref[...])
        a_ref[...] = a
        b_ref[...] = b

    row = pl.BlockSpec((ROWS, 512), lambda i: (i, 0))
    sds = jax.ShapeDtypeStruct((T, 512), F32)
    return pl.pallas_call(body, name="lru_pre_fwd", grid=(T // ROWS,), in_specs=_lru_pre_specs(),
                          out_specs=(row, row), out_shape=(sds, sds), compiler_params=_cp(("parallel",)),
                          )(xs, cw, cb, wa, ba, wx, bx, lam)


def _lru_pre_bwd(xs, cw, cb, wa, ba, wx, bx, lam, da, db):
    def body(xs_ref, cw_ref, cb_ref, wa_ref, ba_ref, wx_ref, bx_ref, lam_ref, da_ref, db_ref,
             dxs_ref, dcw_ref, dcb_ref, dwa_ref, dba_ref, dwx_ref, dbx_ref, dlam_ref):
        acc = (dcw_ref, dcb_ref, dwa_ref, dba_ref, dwx_ref, dbx_ref, dlam_ref)

        @pl.when(pl.program_id(0) == 0)
        def _():
            for r in acc:
                r[...] = jnp.zeros_like(r)

        _, vjp = jax.vjp(_lru_pre, xs_ref[...], cw_ref[...], cb_ref[...], wa_ref[...], ba_ref[...], wx_ref[...],
                         bx_ref[...], lam_ref[...])
        grads = vjp((da_ref[...], db_ref[...]))
        dxs_ref[...] = grads[0]
        for r, g in zip(acc, grads[1:]):
            r[...] += g

    row = pl.BlockSpec((ROWS, 512), lambda i: (i, 0))
    specs = _lru_pre_specs()
    sds = lambda *s: jax.ShapeDtypeStruct(s, F32)
    return pl.pallas_call(
        body, name="lru_pre_bwd", grid=(T // ROWS,), in_specs=specs + [row, row], out_specs=tuple(specs),
        out_shape=(sds(4, T, 512), sds(4, 512), sds(1, 512), sds(512, 512), sds(1, 512), sds(512, 512), sds(1, 512),
                   sds(1, 512)),
        compiler_params=_cp(("arbitrary",)),
    )(xs, cw, cb, wa, ba, wx, bx, lam, da, db)


SCAN_ROWS = 8


def _scan8(a, b, towards_later):
    row = _iota((SCAN_ROWS, 512), 0)
    for s in (1, 2, 4):
        if towards_later:
            keep, shift = row >= s, s
        else:
            keep, shift = row < SCAN_ROWS - s, SCAN_ROWS - s
        a_s = jnp.where(keep, pltpu.roll(a, shift, 0), 1.0)
        b_s = jnp.where(keep, pltpu.roll(b, shift, 0), 0.0)
        b = a * b_s + b
        a = a * a_s
    return a, b


def _lru_scan_fwd(a, b):
    def body(a_ref, b_ref, h_ref):
        def step(i, carry):
            rows = pl.ds(pl.multiple_of(i * SCAN_ROWS, SCAN_ROWS), SCAN_ROWS)
            a8, b8 = _scan8(a_ref[rows, :], b_ref[rows, :], True)
            h = a8 * carry + b8
            h_ref[rows, :] = h
            return jnp.broadcast_to(h[SCAN_ROWS - 1:, :], (SCAN_ROWS, 512))

        lax.fori_loop(0, T // SCAN_ROWS, step, jnp.zeros((SCAN_ROWS, 512), F32), unroll=2)

    return pl.pallas_call(body, name="lru_scan_fwd", out_shape=jax.ShapeDtypeStruct((T, 512), F32),
                          compiler_params=pltpu.CompilerParams(vmem_limit_bytes=VMEM_LIMIT))(a, b)


def _lru_scan_bwd(a_next, h_prev, dh):
    def body(a_ref, h_ref, dh_ref, da_ref, db_ref):
        def step(i, carry):
            start = T - SCAN_ROWS * (i + 1)
            rows = pl.ds(pl.multiple_of(start, SCAN_ROWS), SCAN_ROWS)
            a8, b8 = _scan8(a_ref[rows, :], dh_ref[rows, :], False)
            g = a8 * carry + b8
            db_ref[rows, :] = g
            da_ref[rows, :] = g * h_ref[rows, :]
            return jnp.broadcast_to(g[:1, :], (SCAN_ROWS, 512))

        lax.fori_loop(0, T // SCAN_ROWS, step, jnp.zeros((SCAN_ROWS, 512), F32), unroll=2)

    sds = jax.ShapeDtypeStruct((T, 512), F32)
    return pl.pallas_call(body, name="lru_scan_bwd", out_shape=(sds, sds),
                          compiler_params=pltpu.CompilerParams(vmem_limit_bytes=VMEM_LIMIT))(a_next, h_prev, dh)


def _lru_post(h, gate):
    return h * _gelu_tanh(gate)


def _lru_post_fwd(h, proj, cat):
    def body(h_ref, g_ref, cat_ref, o_ref):
        o_ref[...] = _lru_post(h_ref[...], g_ref[...])

    row = pl.BlockSpec((ROWS, 512), lambda i: (i, 0))
    return pl.pallas_call(body, name="lru_post_fwd", grid=(T // ROWS,),
                          in_specs=[row, pl.BlockSpec((ROWS, 512), lambda i: (i, 3)), pl.BlockSpec(memory_space=pl.ANY)],
                          out_specs=pl.BlockSpec((ROWS, 512), lambda i: (i, 1)),
                          out_shape=jax.ShapeDtypeStruct((T, D), F32), input_output_aliases={2: 0},
                          compiler_params=_cp(("parallel",)))(h, proj, cat)


def _lru_post_bwd(h, proj, dcat):
    def body(h_ref, g_ref, do_ref, dh_ref, dg_ref):
        _, vjp = jax.vjp(_lru_post, h_ref[...], g_ref[...])
        dh, dg = vjp(do_ref[...])
        dh_ref[...] = dh
        dg_ref[...] = dg

    row = pl.BlockSpec((ROWS, 512), lambda i: (i, 0))
    sds = jax.ShapeDtypeStruct((T, 512), F32)
    return pl.pallas_call(body, name="lru_post_bwd", grid=(T // ROWS,),
                          in_specs=[row, pl.BlockSpec((ROWS, 512), lambda i: (i, 3)),
                                    pl.BlockSpec((ROWS, 512), lambda i: (i, 1))],
                          out_specs=(row, row), out_shape=(sds, sds), compiler_params=_cp(("parallel",)))(h, proj, dcat)


def _conv_dx(dxs_shift):
    def body(d_ref, o_ref):
        o_ref[...] = d_ref[0] + d_ref[1] + d_ref[2] + d_ref[3]

    row = pl.BlockSpec((ROWS, 512), lambda i: (i, 0))
    return pl.pallas_call(body, name="lru_conv_dx", grid=(T // ROWS,),
                          in_specs=[pl.BlockSpec((4, ROWS, 512), lambda i: (0, i, 0))], out_specs=row,
                          out_shape=jax.ShapeDtypeStruct((T, 512), F32), compiler_params=_cp(("parallel",)))(dxs_shift)


def _position():
    return lax.axis_index("x"), lax.axis_index("y"), lax.axis_index("c")


def _other_chips(x, y):
    return [(1 - x, y), (x, 1 - y), (1 - x, 1 - y)]


def _al(v, n):
    return v * n if isinstance(v, int) else pl.multiple_of(v * n, n)


_AG_ITEMS = [
    ((4, 32, 128), lambda o, s, h: o.at[s, pl.ds(_al(h, 16), 16), :], lambda r, h: r.at[pl.ds(_al(h, 16), 16), :]),
    ((4, 774, 1024), lambda o, s, h: o.at[s, :, pl.ds(_al(h, 512), 512)], lambda r, h: r.at[:, pl.ds(_al(h, 512), 512)]),
    ((1024, 1024), lambda o, s, h: o.at[pl.ds(_al(2 * s + h, 128), 128), :], lambda r, h: r.at[pl.ds(_al(h, 128), 128), :]),
    ((2, 1024, 4096), lambda o, s, h: o.at[h, :, pl.ds(_al(s, 1024), 1024)], lambda r, h: r.at[h]),
    ((2, 4096, 1024), lambda o, s, h: o.at[h, pl.ds(_al(s, 1024), 1024), :], lambda r, h: r.at[h]),
    ((1024, 2560), lambda o, s, h: o.at[pl.ds(_al(h, 512), 512), pl.ds(_al(s, 640), 640)],
     lambda r, h: r.at[pl.ds(_al(h, 512), 512), :]),
    ((1024, 1024), lambda o, s, h: o.at[pl.ds(_al(2 * s + h, 128), 128), :], lambda r, h: r.at[pl.ds(_al(h, 128), 128), :]),
]


_AG_GROUPS = [(0, 1, 2), (3, 4), (5, 6)]

_HBM = pl.BlockSpec(memory_space=pltpu.HBM)
_SEM = pl.BlockSpec(memory_space=pltpu.SEMAPHORE)
_SPLIT = dict(has_side_effects=pltpu.SideEffectType.DATAFLOW_SIDE_EFFECTING)


def _hbm(a):
    return pltpu.with_memory_space_constraint(a, pltpu.HBM)


def _ag_ici_copy(i, j, chip, c, slot, src_ref, land_ref, send_sems, recv_sems, k):
    _, dst, half = _AG_ITEMS[i]
    return pltpu.make_async_remote_copy(src_ref=half(src_ref, c), dst_ref=dst(land_ref, slot, c), send_sem=send_sems.at[k],
                                        recv_sem=recv_sems.at[k], device_id=(*chip, c), device_id_type=MESH)


def _ag_start(groups, shards, name):
    items_all = [i for g in groups for i in _AG_GROUPS[g]]
    n = len(items_all)
    ng = len(groups)
    lands = [lax.empty(_AG_ITEMS[i][0], shards[i].dtype) for i in items_all]

    def body(*refs):
        srcs, land_refs = dict(zip(items_all, refs[:n])), dict(zip(items_all, refs[n:2 * n]))
        sems = refs[2 * n:2 * n + 2 * ng]
        token = refs[-1]
        x, y, c = _position()
        me = 2 * x + y
        for gi, g in enumerate(groups):
            for t, i in enumerate(_AG_GROUPS[g]):
                for j, chip in enumerate(_other_chips(x, y)):
                    _ag_ici_copy(i, j, chip, c, me, srcs[i], land_refs[i], sems[2 * gi], sems[2 * gi + 1], 3 * t + j).start()
        token[...] = jnp.zeros_like(token)

    sem_shapes = []
    for g in groups:
        sem_shapes += [pltpu.SemaphoreType.DMA((3 * len(_AG_GROUPS[g]),))] * 2
    ops = [shards[i] for i in items_all] + lands
    out = pl.pallas_call(
        body, name=name,
        out_shape=tuple(sem_shapes) + tuple(pltpu.HBM(a.shape, a.dtype) for a in ops) + (jax.ShapeDtypeStruct((8, 128), F32),),
        in_specs=(_HBM,) * (2 * n),
        out_specs=(_SEM,) * (2 * ng) + (_HBM,) * (2 * n) + (pl.BlockSpec(memory_space=pltpu.VMEM),),
        input_output_aliases={i: 2 * ng + i for i in range(2 * n)},
        compiler_params=pltpu.CompilerParams(**_SPLIT),
    )(*[_hbm(a) for a in ops])
    sems, thru, token = out[:2 * ng], out[2 * ng:-1], out[-1]
    return ({g: (sems[2 * gi], sems[2 * gi + 1]) for gi, g in enumerate(groups)},
            dict(zip(items_all, thru[:n])), dict(zip(items_all, thru[n:])), token)


def _ag_wait(g, sems, srcs, lands, after):
    items = _AG_GROUPS[g]
    m = len(items)

    def body(*refs):
        src_refs, land_refs = refs[:m], refs[m:2 * m]
        send_sems, recv_sems = refs[2 * m], refs[2 * m + 1]
        x, y, c = _position()
        for t, i in enumerate(items):
            for j, chip in enumerate(_other_chips(x, y)):
                cp = _ag_ici_copy(i, j, chip, c, 2 * chip[0] + chip[1], src_refs[t], land_refs[t], send_sems, recv_sems,
                                  3 * t + j)
                cp.wait_send()
                cp.wait_recv()

    ops = [srcs[i] for i in items] + [lands[i] for i in items]
    out = pl.pallas_call(
        body, name=f"allgather_wait_{g}",
        out_shape=tuple(pltpu.HBM(a.shape, a.dtype) for a in ops),
        in_specs=(_HBM,) * (2 * m) + (_SEM, _SEM, pl.BlockSpec(memory_space=pl.ANY)),
        out_specs=(_HBM,) * (2 * m),
        input_output_aliases={i: i for i in range(2 * m)},
        compiler_params=pltpu.CompilerParams(**_SPLIT),
    )(*ops, sems[0], sems[1], after)
    return list(out[:m]), list(out[m:])


def _ag_forward(g, srcs, lands):
    return _ag_sibling(_AG_GROUPS[g], srcs, lands, False, f"allgather_forward_{g}")


def _ag_push_own(srcs, lands):
    items = tuple(sorted(lands))
    out = _ag_sibling(items, [srcs[i] for i in items], [lands[i] for i in items], True, "allgather_push_own")
    return dict(zip(items, out))


def _ag_sibling(items, srcs, lands, own, name):
    m = len(items)
    per = 2 if own else 3

    def body(*refs):
        src_refs, in_refs, out_refs = refs[:m], refs[m:2 * m], refs[2 * m:3 * m]
        send_sems, recv_sems = refs[3 * m:]
        x, y, c = _position()
        sibling = (x, y, 1 - c)
        me = 2 * x + y
        if own:
            mine = theirs = [(me, 0), (me, 1)]
        else:
            slots = [2 * chip[0] + chip[1] for chip in _other_chips(x, y)]
            mine, theirs = [(s, c) for s in slots], [(s, 1 - c) for s in slots]
        sends = []
        for t, i in enumerate(items):
            _, dst, half = _AG_ITEMS[i]
            for k, (slot, hc) in enumerate(mine):
                src = half(src_refs[t], hc) if own else dst(in_refs[t], slot, hc)
                sends.append(pltpu.make_async_remote_copy(
                    src_ref=src, dst_ref=dst(out_refs[t], slot, hc), send_sem=send_sems.at[per * t + k],
                    recv_sem=recv_sems.at[per * t + k], device_id=sibling, device_id_type=MESH))
        for cp in sends:
            cp.start()
        for t, i in enumerate(items):
            dst = _AG_ITEMS[i][1]
            for k, (slot, hc) in enumerate(theirs):
                there = dst(out_refs[t], slot, hc)
                pltpu.make_async_remote_copy(src_ref=there, dst_ref=there, send_sem=send_sems.at[per * t + k],
                                             recv_sem=recv_sems.at[per * t + k], device_id=sibling,
                                             device_id_type=MESH).wait_recv()
        for cp in sends:
            cp.wait_send()

    any_spec = pl.BlockSpec(memory_space=pl.ANY)
    return pl.pallas_call(
        body, name=name,
        in_specs=[any_spec] * (2 * m), out_specs=(any_spec,) * m,
        out_shape=tuple(jax.ShapeDtypeStruct(a.shape, a.dtype) for a in lands),
        input_output_aliases={m + t: t for t in range(m)},
        scratch_shapes=[pltpu.SemaphoreType.DMA((per * m,)), pltpu.SemaphoreType.DMA((per * m,))],
    )(*srcs, *lands)


def _pair_swap_copy(g_ref, r_ref, send_sem, recv_sem):
    x, y, c = _position()
    hc = g_ref.shape[2] // 2
    return pltpu.make_async_remote_copy(src_ref=g_ref.at[:, :, pl.ds(_al(1 - c, hc), hc)], dst_ref=r_ref,
                                        send_sem=send_sem, recv_sem=recv_sem, device_id=(x, y, 1 - c),
                                        device_id_type=MESH)


def _pair_swap_start(gb, tag):
    _, rows, cols = gb.shape
    recv = lax.empty((4, rows, cols // 2), gb.dtype)

    def body(g_ref, r_ref, send_sem, recv_sem, g_thru, r_thru, token):
        _pair_swap_copy(g_ref, r_ref, send_sem, recv_sem).start()
        token[...] = jnp.zeros_like(token)

    return pl.pallas_call(
        body, name="grad_pair_swap_start_" + tag,
        out_shape=(pltpu.SemaphoreType.DMA(()), pltpu.SemaphoreType.DMA(()), pltpu.HBM(gb.shape, gb.dtype),
                   pltpu.HBM(recv.shape, recv.dtype), jax.ShapeDtypeStruct((8, 128), F32)),
        in_specs=(_HBM, _HBM), out_specs=(_SEM, _SEM, _HBM, _HBM, pl.BlockSpec(memory_space=pltpu.VMEM)),
        input_output_aliases={0: 2, 1: 3},
        compiler_params=pltpu.CompilerParams(**_SPLIT),
    )(_hbm(gb), _hbm(recv))


def _pair_swap_wait(started, after, tag):
    send_sem, recv_sem, gb, recv, _ = started

    def body(g_ref, r_ref, send_sem, recv_sem, after_ref, g_out, r_out):
        cp = _pair_swap_copy(g_ref, r_ref, send_sem, recv_sem)
        cp.wait_send()
        cp.wait_recv()

    return pl.pallas_call(
        body, name="grad_pair_swap_wait_" + tag,
        out_shape=(pltpu.HBM(gb.shape, gb.dtype), pltpu.HBM(recv.shape, recv.dtype)),
        in_specs=(_HBM, _HBM, _SEM, _SEM, pl.BlockSpec(memory_space=pl.ANY)), out_specs=(_HBM, _HBM),
        input_output_aliases={0: 0, 1: 1},
        compiler_params=pltpu.CompilerParams(**_SPLIT),
    )(gb, recv, send_sem, recv_sem, after)


def _handover_copy(r_ref, send_sem, recv_sem, core):
    x, y, c = _position()
    hc = r_ref.shape[1] // 2
    cols = r_ref.at[:, pl.ds(_al(core, hc), hc)]
    return pltpu.make_async_remote_copy(src_ref=cols, dst_ref=cols, send_sem=send_sem, recv_sem=recv_sem,
                                        device_id=(x, y, 1 - c), device_id_type=MESH)


def _handover_start(red, tag):
    def body(r_ref, send_sem, recv_sem, r_thru, token):
        _handover_copy(r_ref, send_sem, recv_sem, lax.axis_index("c")).start()
        token[...] = jnp.zeros_like(token)

    return pl.pallas_call(
        body, name="grad_handover_start_" + tag,
        out_shape=(pltpu.SemaphoreType.DMA(()), pltpu.SemaphoreType.DMA(()), pltpu.HBM(red.shape, red.dtype),
                   jax.ShapeDtypeStruct((8, 128), F32)),
        in_specs=(_HBM,), out_specs=(_SEM, _SEM, _HBM, pl.BlockSpec(memory_space=pltpu.VMEM)),
        input_output_aliases={0: 2},
        compiler_params=pltpu.CompilerParams(**_SPLIT),
    )(_hbm(red))


def _handover_wait(started, after, tag):
    send_sem, recv_sem, red, _ = started

    def body(r_ref, send_sem, recv_sem, after_ref, r_out):
        c = lax.axis_index("c")
        _handover_copy(r_ref, send_sem, recv_sem, c).wait_send()
        _handover_copy(r_ref, send_sem, recv_sem, 1 - c).wait_recv()

    return pl.pallas_call(
        body, name="grad_handover_wait_" + tag,
        out_shape=pltpu.HBM(red.shape, red.dtype),
        in_specs=(_HBM, _SEM, _SEM, pl.BlockSpec(memory_space=pl.ANY)), out_specs=_HBM,
        input_output_aliases={0: 0},
        compiler_params=pltpu.CompilerParams(**_SPLIT),
    )(red, send_sem, recv_sem, after)


def _handover(red, tag):
    started = _handover_start(red, tag)
    return _handover_wait(started, started[3], tag)


def _a2a_copy(j, chip, c, p_ref, q_ref, q_slot, send_sems, recv_sems):
    return pltpu.make_async_remote_copy(src_ref=p_ref.at[2 * chip[0] + chip[1]], dst_ref=q_ref.at[q_slot],
                                        send_sem=send_sems.at[j], recv_sem=recv_sems.at[j], device_id=(*chip, c),
                                        device_id_type=MESH)


def _a2a_start(p, tag):
    def body(p_ref, q_ref, send_sems, recv_sems, p_thru, q_thru, token):
        x, y, c = _position()
        for j, chip in enumerate(_other_chips(x, y)):
            _a2a_copy(j, chip, c, p_ref, q_ref, 2 * x + y, send_sems, recv_sems).start()
        token[...] = jnp.zeros_like(token)

    return pl.pallas_call(
        body, name="grad_alltoall_start_" + tag,
        out_shape=(pltpu.SemaphoreType.DMA((3,)), pltpu.SemaphoreType.DMA((3,)), pltpu.HBM(p.shape, p.dtype),
                   pltpu.HBM(p.shape, p.dtype), jax.ShapeDtypeStruct((8, 128), F32)),
        in_specs=(_HBM, _HBM), out_specs=(_SEM, _SEM, _HBM, _HBM, pl.BlockSpec(memory_space=pltpu.VMEM)),
        input_output_aliases={0: 2, 1: 3},
        compiler_params=pltpu.CompilerParams(**_SPLIT),
    )(_hbm(p), _hbm(lax.empty(p.shape, p.dtype)))


def _a2a_wait(send_sems, recv_sems, p, q, after, tag):
    def body(p_ref, q_ref, send_sems, recv_sems, after_ref, p_out, q_out):
        x, y, c = _position()
        for j, chip in enumerate(_other_chips(x, y)):
            cp = _a2a_copy(j, chip, c, p_ref, q_ref, 2 * chip[0] + chip[1], send_sems, recv_sems)
            cp.wait_send()
            cp.wait_recv()

    return pl.pallas_call(
        body, name="grad_alltoall_wait_" + tag,
        out_shape=(pltpu.HBM(p.shape, p.dtype), pltpu.HBM(q.shape, q.dtype)),
        in_specs=(_HBM, _HBM, _SEM, _SEM, pl.BlockSpec(memory_space=pl.ANY)), out_specs=(_HBM, _HBM),
        input_output_aliases={0: 0, 1: 1},
        compiler_params=pltpu.CompilerParams(**_SPLIT),
    )(p, q, send_sems, recv_sems, after)


def _comm_rows(rows):
    return next(t for t in (512, 384, 256, 128) if rows % t == 0)


def _pair_add(gb, recv, where, tag):
    _, rows, cols = gb.shape
    hc = cols // 2
    tr = _comm_rows(rows)

    def body(w_ref, g_ref, r_ref, o_ref):
        o_ref[...] = (g_ref[...].astype(F32) + r_ref[...].astype(F32)).astype(o_ref.dtype)

    return pl.pallas_call(
        body, name="grad_pair_add_" + tag,
        grid_spec=pltpu.PrefetchScalarGridSpec(
            num_scalar_prefetch=1, grid=(4, rows // tr),
            in_specs=[pl.BlockSpec((None, tr, hc), lambda s, j, w_ref: (s, j, w_ref[0])),
                      pl.BlockSpec((None, tr, hc), lambda s, j, w_ref: (s, j, 0))],
            out_specs=pl.BlockSpec((None, tr, hc), lambda s, j, w_ref: (s, j, 0))),
        out_shape=jax.ShapeDtypeStruct((4, rows, hc), gb.dtype),
        compiler_params=_cp(("parallel", "parallel")),
    )(where, gb, recv)


def _sum_chips(p, q, where, tag):
    _, rows, hc = q.shape
    tr = _comm_rows(rows)

    def body(w_ref, p_ref, qa_ref, qb_ref, qc_ref, o_ref):
        me = w_ref[1]
        own, qa, qb, qc = (r[...].astype(F32) for r in (p_ref, qa_ref, qb_ref, qc_ref))
        v0 = jnp.where(me == 0, own, qa)
        v1 = jnp.where(me == 1, own, jnp.where(me == 0, qa, qb))
        v2 = jnp.where(me == 2, own, jnp.where(me < 2, qb, qc))
        v3 = jnp.where(me == 3, own, qc)
        o_ref[...] = ((v0 + v1) + v2) + v3

    slot = lambda k: pl.BlockSpec((None, tr, hc), lambda j, w_ref: (w_ref[k], j, 0))
    return pl.pallas_call(
        body, name="grad_sum_chips_" + tag,
        grid_spec=pltpu.PrefetchScalarGridSpec(
            num_scalar_prefetch=1, grid=(rows // tr,),
            in_specs=[slot(1), slot(2), slot(3), slot(4)],
            out_specs=pl.BlockSpec((tr, hc), lambda j, w_ref: (j, w_ref[0]))),
        out_shape=jax.ShapeDtypeStruct((rows, 2 * hc), F32),
        compiler_params=_cp(("parallel",)),
    )(where, p, q, q, q)


def _shard_major(g, axis):
    shape = g.shape
    g = g.reshape(shape[:axis] + (4, shape[axis] // 4) + shape[axis + 1:])
    return jnp.moveaxis(g, axis, 0).reshape(4, -1)


def _unshard(g4, shape, axis):
    n = shape[axis] // 4
    g = g4.reshape((4,) + shape[:axis] + (n,) + shape[axis + 1:])
    return jnp.moveaxis(g, 0, axis).reshape(shape)


def _split(flat, shapes):
    out, off = [], 0
    for shp in shapes:
        n = 1
        for d in shp:
            n *= d
        out.append(flat[..., off:off + n].reshape(flat.shape[:-1] + tuple(shp)))
        off += n
    return out


def _even_rows_to_kernel(wt):
    return jnp.concatenate([wt[:1536], wt[1552:3088], wt[1536:1552], wt[3088:3096],
                            jnp.zeros((PE - 3096, wt.shape[1]), wt.dtype)], axis=0)


def _block_diag(w):
    eye = jnp.eye(8, dtype=w.dtype)
    return (w[:, :, None, :] * eye[:, None, :, None]).reshape(512, 512)


def _diag_blocks(g):
    eye = jnp.eye(8, dtype=g.dtype)
    return (g.reshape(8, 64, 8, 64) * eye[:, None, :, None]).sum(axis=2)


def _shift_down(a, s):
    return a if s == 0 else jnp.pad(a, ((s, 0), (0, 0)))[:a.shape[0]]


def _shift_up(a, s):
    return a if s == 0 else jnp.pad(a, ((0, s), (0, 0)))[s:]


SMALL_SHARDED_SHAPES = [(2, 4, 256), (16, 64), (4, 128), (128,), (128,), (128,), (128,)]
REPL_SHAPES = [(256,), (512,), (8,), (8, 257), (8, 64, 64), (8, 64, 64)]


def kernel(x, norm_w, w_in_even, gla_w_a_up, gla_b_a, gla_norm_w, fox_b_f, w_out_even, w_in_odd, rel_bias, conv_w, conv_b, lru_w_a, lru_b_a, lru_w_x, lru_b_x, lru_lambda, w_out_odd, w_mlp_up, w_mlp_down, loss_target, m_norm_w, m_w_in_even, m_gla_w_a_up, m_gla_b_a, m_gla_norm_w, m_fox_b_f, m_w_out_even, m_w_in_odd, m_rel_bias, m_conv_w, m_conv_b, m_lru_w_a, m_lru_b_a, m_lru_w_x, m_lru_b_x, m_lru_lambda, m_w_out_odd, m_w_mlp_up, m_w_mlp_down, v_norm_w, v_w_in_even, v_gla_w_a_up, v_gla_b_a, v_gla_norm_w, v_fox_b_f, v_w_out_even, v_w_in_odd, v_rel_bias, v_conv_w, v_conv_b, v_lru_w_a, v_lru_b_a, v_lru_w_x, v_lru_b_x, v_lru_lambda, v_w_out_odd, v_w_mlp_up, v_w_mlp_down):
    c_idx = lax.axis_index("c")

    small_local = [norm_w, gla_w_a_up[0], conv_w[0], conv_b[0], lru_b_a[0], lru_b_x[0], lru_lambda[0]]
    small_src = jnp.concatenate([a.reshape(-1) for a in small_local]).reshape(32, 128)
    first = {0: small_src, 1: w_in_even[0].T.astype(BF16), 2: w_out_even[0].astype(BF16)}
    sems0, srcs0, lands0, ag_token = _ag_start([0], first, "allgather_start_0")
    zero = ag_token[0, 0]
    later = {3: (w_mlp_up + zero).astype(BF16), 4: (w_mlp_down + zero).astype(BF16),
             5: (w_in_odd[0] + zero).astype(BF16), 6: (w_out_odd[0] + zero).astype(BF16)}
    sems1, srcs1, lands1, ag_token = _ag_start([1, 2], later, "allgather_start_1")
    ag_sems, ag_srcs = {**sems0, **sems1}, {**srcs0, **srcs1}
    ag_lands = _ag_push_own(ag_srcs, {**lands0, **lands1})

    def gathered(g, after):
        srcs_g, lands_g = _ag_wait(g, ag_sems[g], ag_srcs, ag_lands, after)
        return _ag_forward(g, srcs_g, lands_g)

    small4, w_in_e4, w_out_e = gathered(0, ag_token)
    me = 2 * lax.axis_index("x") + lax.axis_index("y")
    others = [k + (k >= me).astype(jnp.int32) for k in range(3)]
    where = jnp.stack([c_idx, me] + others).astype(jnp.int32)

    w_in_e_t = _even_rows_to_kernel(w_in_e4.reshape(3096, D))
    g_small = _split(small4.reshape(4, 32 * 128), SMALL_SHARDED_SHAPES)
    nw_full = _unshard(g_small[0], (2, 4, 1024), 2)
    wa_up = _unshard(g_small[1], (16, 256), 1)
    cw = _unshard(g_small[2], (4, 512), 1)
    cb, lba, lbx, lam = [_unshard(g, (512,), 0).reshape(1, 512) for g in g_small[3:]]
    nw = lambda layer, i: nw_full[layer, i].reshape(1, D)

    wa_pad = jnp.pad(wa_up, ((0, 128 - 16), (0, 0)))
    gla_ba = gla_b_a.reshape(1, 256)
    gla_nw = gla_norm_w.reshape(1, 512)
    fox_bpad = jnp.pad(fox_b_f.reshape(1, 8), ((0, 0), (FOX_LANE0, 128 - FOX_LANE0 - 8)))
    rbp = jnp.pad(rel_bias[0], ((0, 0), (0, REL_PAD - 257)))
    wa_bd = _block_diag(lru_w_a[0])
    wx_bd = _block_diag(lru_w_x[0])

    x0 = x[0]
    tgt = loss_target[0]

    h0 = _prenorm(x0, nw(0, 0), "prenorm_l0_mix")
    proj_e = _mm(h0, w_in_e_t, "nt", tm=2048, tn=640, name="mm_in_even")
    cat0, s_prev = _gla_fwd(proj_e, wa_pad, gla_ba, gla_nw)
    cum_r = _fox_gate_fwd(proj_e, fox_bpad)
    cum_c = cum_r[:, FOX_LANE0:FOX_LANE0 + 8].T
    cat0 = _fox_fwd(proj_e, cum_c, cat0)
    mix0 = _mm(cat0, w_out_e, "nn", tm=2048, tn=512, name="mm_out_even")
    x1, h1 = _post_pre_fwd(x0, mix0, nw(0, 1), nw(0, 2), "post_pre_l0_mix")
    w_up, w_dn = gathered(1, x1)
    a0, r0 = _mm(h1, w_up, "nn", tm=2048, tn=1024, b_layer=0, relu_pair=True, name="mm_up_l0")
    d0 = _mm(a0, w_dn, "nn", tm=1024, tn=512, b_layer=0, name="mm_down_l0")
    x2, h2 = _post_pre_fwd(x1, d0, nw(0, 3), nw(1, 0), "post_pre_l0_mlp")

    w_in_o, w_out_o = gathered(2, x2)
    proj_o = _mm(h2, w_in_o, "nn", tm=2048, tn=640, name="mm_in_odd")
    bias_q = _bias_build(rbp)
    bias = bias_q.transpose(1, 0, 2)
    kvpad = jnp.pad(proj_o[:, 512:1536], ((CA_PAD, 0), (0, 0)))
    cat1 = _ca_fwd(proj_o, kvpad, bias)
    x_in = proj_o[:, 2048:2560]
    xs = jnp.stack([_shift_down(x_in, 3 - j) for j in range(4)])
    lru_a, lru_b = _lru_pre_fwd(xs, cw, cb, wa_bd, lba, wx_bd, lbx, lam)
    hh = _lru_scan_fwd(lru_a, lru_b)
    cat1 = _lru_post_fwd(hh, proj_o, cat1)
    mix1 = _mm(cat1, w_out_o, "nn", tm=2048, tn=512, name="mm_out_odd")
    x3, h3 = _post_pre_fwd(x2, mix1, nw(1, 1), nw(1, 2), "post_pre_l1_mix")
    a1, r1 = _mm(h3, w_up, "nn", tm=2048, tn=1024, b_layer=1, relu_pair=True, name="mm_up_l1")
    d1 = _mm(a1, w_dn, "nn", tm=1024, tn=512, b_layer=1, name="mm_down_l1")
    g4, loss_part, dd1, dnw13 = _post_loss(x3, d1, nw(1, 3), tgt)
    loss = lax.psum(loss_part[0, 0], ("x", "y", "c"))

    def rs_begin(swap, after, tag):
        gb, recv = _pair_swap_wait(swap, after, tag)
        return _a2a_start(_pair_add(gb, recv, where, tag), tag)

    def rs_end(started, after, tag):
        send_sems, recv_sems, p, q, _ = started
        p, q = _a2a_wait(send_sems, recv_sems, p, q, after, tag)
        return _handover(_sum_chips(p, q, where, tag), tag)

    gba = lax.dynamic_update_slice(lax.empty((4, GA_ROWS, D), BF16), jnp.zeros((4, GA_UP - GA_GAP, D), BF16),
                                   (0, GA_GAP, 0))
    gba = _mm(a1, dd1, "tn", tm=512, tn=1024, into=(gba, 1024, GA_DN), name="mm_down_l1_dw")
    du1 = _mm(dd1, w_dn, "nt", tm=2048, tn=1024, b_layer=1, times2=r1, out_dtype=BF16, name="mm_down_l1_dx")
    gba = _mm(du1, h3, "tn", tm=512, tn=1024, into=(gba, 1024, GA_UP), name="mm_up_l1_dw")
    dh3 = _mm(du1, w_up, "nt", tm=1024, tn=512, b_layer=1, name="mm_up_l1_dx")
    g3, dmix1, dnw12, dnw11 = _pre_post_bwd(x3, nw(1, 2), dh3, g4, mix1, nw(1, 1), "pre_post_bwd_l1_mlp")
    gba = _mm(cat1, dmix1, "tn", tm=128, tn=1024, into=(gba, 256, GA_OUT_O), name="mm_out_odd_dw")
    dcat1 = _mm(dmix1, w_out_o, "nt", tm=2048, tn=512, name="mm_out_odd_dx")

    dq_c, dkpad, dvpad, dbias = _ca_bwd(proj_o, kvpad, bias, dcat1)
    g_rel = _bias_grad(jnp.pad(dbias.transpose(1, 0, 2), ((0, 0), (0, 0), (0, BIAS_W - CA_BAND))))[:, :257]
    dhh, dgate = _lru_post_bwd(hh, proj_o, dcat1)
    da_l, db_l = _lru_scan_bwd(_shift_up(lru_a, 1), _shift_down(hh, 1), dhh)
    dxs, g_cw, g_cb, g_wa_bd, g_lba, g_wx_bd, g_lbx, g_lam = _lru_pre_bwd(xs, cw, cb, wa_bd, lba, wx_bd, lbx, lam, da_l, db_l)
    dx_in = _conv_dx(jnp.stack([_shift_up(dxs[j], 3 - j) for j in range(4)]))
    dproj_o = jnp.concatenate([dq_c, dkpad[CA_PAD:], dvpad[CA_PAD:], dgate, dx_in], axis=1).astype(BF16)
    gba = _mm(dproj_o, h2, "tn", tm=128, tn=1024, into=(gba, 640, GA_IN_O), name="mm_in_odd_dw")
    swap_a = _pair_swap_start(gba, "a")
    dh2 = _mm(dproj_o, w_in_o, "nt", tm=1024, tn=512, name="mm_in_odd_dx")
    g2, dd0, dnw10, dnw03 = _pre_post_bwd(x2, nw(1, 0) + swap_a[4][0, 0], dh2, g3, d0, nw(0, 3), "pre_post_bwd_l1_mix")
    rs_a = rs_begin(swap_a, g2, "a")

    gbb = lax.empty((4, GB_ROWS, D), BF16)
    gbb = _mm(a0, dd0, "tn", tm=512, tn=1024, into=(gbb, 1024, GB_DN), name="mm_down_l0_dw")
    du0 = _mm(dd0, w_dn, "nt", tm=2048, tn=1024, b_layer=0, times2=r0, out_dtype=BF16, name="mm_down_l0_dx")
    gbb = _mm(du0, h1, "tn", tm=512, tn=1024, into=(gbb, 1024, GB_UP), name="mm_up_l0_dw")
    swap_b = _pair_swap_start(gbb, "b")
    dh1 = _mm(du0, w_up, "nt", tm=1024, tn=512, b_layer=0, name="mm_up_l0_dx")
    g1, dmix0, dnw02, dnw01 = _pre_post_bwd(x1, nw(0, 2) + (swap_b[4][0, 0] + rs_a[4][0, 0]), dh1, g2, mix0, nw(0, 1),
                                            "pre_post_bwd_l0_mlp")
    rs_b = rs_begin(swap_b, g1, "b")
    gbc = lax.empty((4, GC_ROWS, D), BF16)
    gbc = _mm(cat0, dmix0, "tn", tm=128, tn=1024, into=(gbc, 256, GC_OUT_E), name="mm_out_even_dw")
    dcat0 = _mm(dmix0, w_out_e, "nt", tm=2048, tn=512, name="mm_out_even_dx")

    dq_g, dk_g, dv_g, dr_g, daux_g, g_wa_pad, g_gla_ba, g_gla_nw = _gla_bwd(
        proj_e, s_prev, wa_pad, gla_ba, gla_nw + rs_b[4][0, 0], dcat0)
    dq_f, dk_f, dv_f, dccol = _fox_bwd(proj_e, cum_c, dcat0)
    dccol_t = jnp.pad(dccol.sum(axis=0).T, ((0, 0), (FOX_LANE0, 128 - FOX_LANE0 - 8)))
    daux, g_fox_bpad = _fox_gate_bwd(proj_e, fox_bpad, dccol_t, daux_g)
    dproj_e = jnp.concatenate([dq_g, dk_g, dv_g, dr_g, dq_f, dk_f, dv_f, daux], axis=1).astype(BF16)
    gt_in_e = _mm(dproj_e, h0, "tn", tm=640, tn=1024, out_dtype=BF16, name="mm_in_even_dw")
    dh0 = _mm(dproj_e, w_in_e_t, "nn", tm=1024, tn=512, name="mm_in_even_dx")
    grad_x, dnw00 = _norm_bwd(x0, nw(0, 0), dh0, g1, "prenorm_l0_mix_bwd")

    def rs_reduce(started, after, tag):
        send_sems, recv_sems, p, q, _ = started
        p, q = _a2a_wait(send_sems, recv_sems, p, q, after, tag)
        return _handover_start(_sum_chips(p, q, where, tag), tag)

    ho_a = rs_reduce(rs_a, grad_x, "a")
    ho_b = rs_reduce(rs_b, ho_a[3], "b")

    g_norm = jnp.stack([jnp.concatenate([dnw00, dnw01, dnw02, dnw03]), jnp.concatenate([dnw10, dnw11, dnw12, dnw13])])
    sharded = [(g_norm, 2), (g_wa_pad[:16], 1), (g_cw, 1), (g_cb[0], 0), (g_lba[0], 0), (g_lbx[0], 0), (g_lam[0], 0)]
    replicated = [g_gla_ba[0], g_gla_nw[0], g_fox_bpad[0, FOX_LANE0:FOX_LANE0 + 8], g_rel, _diag_blocks(g_wa_bd),
                  _diag_blocks(g_wx_bd)]
    small4 = jnp.concatenate([_shard_major(g, ax) for g, ax in sharded]
                             + [jnp.broadcast_to(g.reshape(1, -1), (4, g.size)) for g in replicated], axis=1)
    n_small = small4.shape[1]
    small_rows = GC_ROWS - GC_TAIL - 774
    small4 = jnp.pad(small4, ((0, 0), (0, small_rows * D - n_small))).reshape(4, small_rows, D)
    gt_rows = jnp.concatenate([gt_in_e[:1536], gt_in_e[3072:3088], gt_in_e[1536:3072], gt_in_e[3088:3096]], axis=0)
    tail = jnp.concatenate([gt_rows.reshape(4, 774, D), small4.astype(BF16)], axis=1)
    gbc = lax.dynamic_update_slice(gbc, tail, (0, GC_TAIL, 0))
    swap_c = _pair_swap_start(gbc, "c")
    rs_c = rs_begin(swap_c, swap_c[4], "c")

    red_a = _handover_wait(ho_a, rs_c[4], "a")
    red_b = _handover_wait(ho_b, red_a, "b")
    early = dict(
        w_mlp_up=_adamw_from(w_mlp_up, m_w_mlp_up, v_w_mlp_up, [(red_b, GB_UP, True), (red_a, GA_UP, True)], 256,
                             "adamw_w_mlp_up"),
        w_mlp_down=_adamw_from(w_mlp_down, m_w_mlp_down, v_w_mlp_down, [(red_b, GB_DN, False), (red_a, GA_DN, False)],
                               256, "adamw_w_mlp_down"),
        w_in_odd=_adamw_from(w_in_odd, m_w_in_odd, v_w_in_odd, [(red_a, GA_IN_O, True)], 256, "adamw_w_in_odd"),
        w_out_odd=_adamw_from(w_out_odd, m_w_out_odd, v_w_out_odd, [(red_a, GA_OUT_O, False)], 128, "adamw_w_out_odd"))
    red_c = rs_end(rs_c, early["w_out_odd"][3], "c")

    g_small = _split(red_c[GC_TAIL + 774:].reshape(-1)[:n_small], SMALL_SHARDED_SHAPES + REPL_SHAPES)
    g_of = dict(zip(["norm_w", "gla_w_a_up", "conv_w", "conv_b", "lru_b_a", "lru_b_x", "lru_lambda", "gla_b_a",
                     "gla_norm_w", "fox_b_f", "rel_bias", "lru_w_a", "lru_w_x"], g_small))
    g_of.update(w_in_even=red_c[GC_TAIL:GC_TAIL + 774])
    early["w_out_even"] = _adamw_from(w_out_even, m_w_out_even, v_w_out_even, [(red_c, GC_OUT_E, False)], 256,
                                      "adamw_w_out_even")

    names = ["norm_w", "w_in_even", "gla_w_a_up", "gla_b_a", "gla_norm_w", "fox_b_f", "w_out_even", "w_in_odd", "rel_bias",
             "conv_w", "conv_b", "lru_w_a", "lru_b_a", "lru_w_x", "lru_b_x", "lru_lambda", "w_out_odd", "w_mlp_up",
             "w_mlp_down"]
    w_of = dict(norm_w=norm_w, w_in_even=w_in_even, gla_w_a_up=gla_w_a_up, gla_b_a=gla_b_a, gla_norm_w=gla_norm_w,
                fox_b_f=fox_b_f, w_out_even=w_out_even, w_in_odd=w_in_odd, rel_bias=rel_bias, conv_w=conv_w, conv_b=conv_b,
                lru_w_a=lru_w_a, lru_b_a=lru_b_a, lru_w_x=lru_w_x, lru_b_x=lru_b_x, lru_lambda=lru_lambda,
                w_out_odd=w_out_odd, w_mlp_up=w_mlp_up, w_mlp_down=w_mlp_down)
    m_of = dict(norm_w=m_norm_w, w_in_even=m_w_in_even, gla_w_a_up=m_gla_w_a_up, gla_b_a=m_gla_b_a,
                gla_norm_w=m_gla_norm_w, fox_b_f=m_fox_b_f, w_out_even=m_w_out_even, w_in_odd=m_w_in_odd,
                rel_bias=m_rel_bias, conv_w=m_conv_w, conv_b=m_conv_b, lru_w_a=m_lru_w_a, lru_b_a=m_lru_b_a,
                lru_w_x=m_lru_w_x, lru_b_x=m_lru_b_x, lru_lambda=m_lru_lambda, w_out_odd=m_w_out_odd,
                w_mlp_up=m_w_mlp_up, w_mlp_down=m_w_mlp_down)
    v_of = dict(norm_w=v_norm_w, w_in_even=v_w_in_even, gla_w_a_up=v_gla_w_a_up, gla_b_a=v_gla_b_a,
                gla_norm_w=v_gla_norm_w, fox_b_f=v_fox_b_f, w_out_even=v_w_out_even, w_in_odd=v_w_in_odd,
                rel_bias=v_rel_bias, conv_w=v_conv_w, conv_b=v_conv_b, lru_w_a=v_lru_w_a, lru_b_a=v_lru_b_a,
                lru_w_x=v_lru_w_x, lru_b_x=v_lru_b_x, lru_lambda=v_lru_lambda, w_out_odd=v_w_out_odd,
                w_mlp_up=v_w_mlp_up, w_mlp_down=v_w_mlp_down)
    grads, deltas, new_ms, new_vs = [], [], [], []
    for n in names:
        w = w_of[n]
        if n in early:
            g, d, mn, vn = early[n]
            grads.append(g)
            deltas.append(d)
            new_ms.append(mn)
            new_vs.append(vn)
            continue
        if n == "w_in_even":
            to_view = lambda a: a[0].T
            from_view = lambda a: a.T[None]
        else:
            view = w.shape if w.ndim <= 3 else w.shape[-3:]
            to_view = lambda a, view=view: a.reshape(view)
            from_view = lambda a, w=w: a.reshape(w.shape)
        g = g_of[n] if n == "w_in_even" else to_view(g_of[n])
        d, mn, vn = _adamw(to_view(w), g, to_view(m_of[n]), to_view(v_of[n]), "adamw_" + n)
        grads.append(from_view(g))
        deltas.append(from_view(d))
        new_ms.append(from_view(mn))
        new_vs.append(from_view(vn))

    return (loss, grad_x.reshape(1, T, D), *grads, *deltas, *new_ms, *new_vs)
```

```python
import functools

import jax
import jax.numpy as jnp
from jax import lax
from jax.experimental import pallas as pl
from jax.experimental.pallas import tpu as pltpu

F32 = jnp.float32
BF16 = jnp.bfloat16
MESH = pl.DeviceIdType.MESH

T = 2048
D = 1024
DFF = 4096
EPS = 1e-6
CHUNK = 64
NCHUNK = T // CHUNK
PE = 3200
PO = 2560
AUX_BLK = 3072 // 128
FOX_LANE0 = 16
GLA_SCALE = 64 ** -0.5
ATT_SCALE = 64 ** -0.5
NEG = float(jnp.finfo(jnp.float32).min)
CA_BAND = 576
CA_PAD = 512
REL_PAD = 384

VMEM_LIMIT = 48 * 1024 * 1024

ADAM_LR, ADAM_B1, ADAM_B2, ADAM_EPS, ADAM_WD, ADAM_STEP = 0.001, 0.9, 0.999, 1e-08, 0.01, 10

GA_ROWS, GA_IN_O, GA_OUT_O, GA_GAP, GA_UP, GA_DN = 3072, 0, 640, 896, 1024, 2048
GB_ROWS, GB_UP, GB_DN = 2048, 0, 1024
GC_ROWS, GC_OUT_E, GC_TAIL = 1152, 0, 256

_DIMS = {"nn": (((1,), (0,)), ((), ())), "nt": (((1,), (1,)), ((), ())), "tn": (((0,), (0,)), ((), ()))}


def _cp(sem, **kw):
    return pltpu.CompilerParams(dimension_semantics=sem, vmem_limit_bytes=VMEM_LIMIT, **kw)


def _dot(a, b, mode):
    return lax.dot_general(a.astype(BF16), b.astype(BF16), _DIMS[mode], preferred_element_type=F32)


@functools.partial(jax.custom_vjp, nondiff_argnums=(2,))
def bdot(a, b, mode):
    return _dot(a, b, mode)


def _bdot_fwd(a, b, mode):
    return _dot(a, b, mode), (a, b)


def _bdot_bwd(mode, res, g):
    a, b = res
    if mode == "nn":
        da, db = _dot(g, b, "nt"), _dot(a, g, "tn")
    elif mode == "nt":
        da, db = _dot(g, b, "nn"), _dot(g, a, "tn")
    else:
        da, db = _dot(b, g, "nt"), _dot(a, g, "nn")
    return da.astype(a.dtype), db.astype(b.dtype)


bdot.defvjp(_bdot_fwd, _bdot_bwd)


def _hdot_raw(a, b, mode):
    return lax.dot_general(a, b, _DIMS[mode], precision=lax.Precision.HIGHEST, preferred_element_type=F32)


def _log_sigmoid(x):
    return jnp.minimum(x, 0.0) - jnp.log(1.0 + jnp.exp(-jnp.abs(x)))


def _sigmoid(x):
    return 1.0 / (1.0 + jnp.exp(-x))


def _expm1(x):
    series = x * (1.0 + x * 0.5 * (1.0 + x * (1.0 / 3.0) * (1.0 + x * 0.25)))
    return jnp.where(jnp.abs(x) < 0.03, series, jnp.exp(x) - 1.0)


def _gelu_tanh(x):
    return 0.5 * x * (1.0 + jnp.tanh(0.7978845608028654 * (x + 0.044715 * x * x * x)))


def _iota(shape, dim):
    return lax.broadcasted_iota(jnp.int32, shape, dim)


def _mm(a, b, mode, *, tm, tn, tk=None, out_dtype=F32, name, b_layer=None, into=None, relu_pair=False, times2=None):
    b2 = b.shape[-2:]
    if mode == "nn":
        (m, k), n = a.shape, b2[1]
    elif mode == "nt":
        (m, k), n = a.shape, b2[0]
    else:
        (k, m), n = a.shape, b2[1]
    tk = k if tk is None else tk
    assert m % tm == 0 and n % tn == 0 and k % tk == 0, (name, a.shape, b.shape)
    nk = k // tk
    if mode == "tn":
        a_spec = pl.BlockSpec((tk, tm), lambda i, j, kk: (kk, i))
    elif m == tm and nk == 1:
        a_spec = pl.BlockSpec((tm, tk), lambda i, j, kk: (i, kk), pipeline_mode=pl.Buffered(1))
    else:
        a_spec = pl.BlockSpec((tm, tk), lambda i, j, kk: (i, kk))
    b_blk = {"nn": (tk, tn), "nt": (tn, tk), "tn": (tk, tn)}[mode]
    b_idx = {"nn": lambda i, j, kk: (kk, j), "nt": lambda i, j, kk: (j, kk), "tn": lambda i, j, kk: (kk, j)}[mode]
    if b_layer is None:
        b_spec = pl.BlockSpec(b_blk, b_idx)
    else:
        b_spec = pl.BlockSpec((None,) + b_blk, lambda i, j, kk: (b_layer,) + b_idx(i, j, kk))

    tile = pl.BlockSpec((tm, tn), lambda i, j, kk: (i, j))
    if into is not None:
        buf, per_slot, row_off = into
        assert m == 4 * per_slot and per_slot % tm == 0 and row_off % tm == 0 and buf.shape[2] == n, (name, buf.shape)
        bps = per_slot // tm
        out_specs = pl.BlockSpec((None, tm, tn), lambda i, j, kk: (i // bps, row_off // tm + i % bps, j))
        out_shape = jax.ShapeDtypeStruct(buf.shape, buf.dtype)
        extra_in, extra_specs, aliases = [buf], [pl.BlockSpec(memory_space=pl.ANY)], {2: 0}
        finish = lambda acc, extra: [acc.astype(buf.dtype)]
    elif relu_pair:
        out_specs = (tile, tile)
        out_shape = (jax.ShapeDtypeStruct((m, n), BF16),) * 2
        extra_in, extra_specs, aliases = [], [], {}

        def finish(acc, extra):
            r = jnp.maximum(acc, 0.0)
            return [(r * r).astype(BF16), r.astype(BF16)]
    elif times2 is not None:
        out_specs = tile
        out_shape = jax.ShapeDtypeStruct((m, n), out_dtype)
        extra_in, extra_specs, aliases = [times2], [tile], {}
        finish = lambda acc, extra: [(acc * (2.0 * extra[...].astype(F32))).astype(out_dtype)]
    else:
        out_specs = tile
        out_shape = jax.ShapeDtypeStruct((m, n), out_dtype)
        extra_in, extra_specs, aliases = [], [], {}
        finish = lambda acc, extra: [acc.astype(out_dtype)]
    n_out = 2 if relu_pair else 1

    def body(*refs):
        a_ref, b_ref = refs[0], refs[1]
        extra = refs[2] if extra_in else None
        o_refs = refs[2 + len(extra_in):2 + len(extra_in) + n_out]

        def store(acc):
            for o_ref, val in zip(o_refs, finish(acc, extra)):
                o_ref[...] = val

        if nk == 1:
            store(_dot(a_ref[...], b_ref[...], mode))
            return
        acc_ref = refs[-1]
        kk = pl.program_id(2)

        @pl.when(kk == 0)
        def _():
            acc_ref[...] = jnp.zeros_like(acc_ref)

        acc_ref[...] += _dot(a_ref[...], b_ref[...], mode)

        @pl.when(kk == nk - 1)
        def _():
            store(acc_ref[...])

    return pl.pallas_call(
        body, name=name, grid=(m // tm, n // tn, nk),
        in_specs=[a_spec, b_spec] + extra_specs,
        out_specs=out_specs, out_shape=out_shape,
        scratch_shapes=[pltpu.VMEM((tm, tn), F32)] if nk > 1 else [],
        input_output_aliases=aliases,
        compiler_params=_cp(("parallel", "parallel", "arbitrary")),
    )(a, b, *extra_in)


ROWS = 512


def _prenorm(x, w, name):
    def body(x_ref, w_ref, o_ref):
        xv = x_ref[...]
        r = lax.rsqrt(jnp.mean(xv * xv, axis=-1, keepdims=True) + EPS)
        o_ref[...] = (xv * r * w_ref[...]).astype(BF16)

    return pl.pallas_call(
        body, name=name, grid=(T // ROWS,),
        in_specs=[pl.BlockSpec((ROWS, D), lambda i: (i, 0)), pl.BlockSpec((1, D), lambda i: (0, 0))],
        out_specs=pl.BlockSpec((ROWS, D), lambda i: (i, 0)),
        out_shape=jax.ShapeDtypeStruct((T, D), BF16),
        compiler_params=_cp(("parallel",)),
    )(x, w)


def _rms(z):
    return lax.rsqrt(jnp.mean(z * z, axis=-1, keepdims=True) + EPS)


def _rms_bwd(z, w, dy):
    r = _rms(z)
    wdy = dy * w
    dz = r * wdy - z * (r * r * r) * jnp.mean(z * wdy, axis=-1, keepdims=True)
    return dz, jnp.sum(dy * z * r, axis=0, keepdims=True)


_ROW = pl.BlockSpec((ROWS, D), lambda i: (i, 0))
_VEC = pl.BlockSpec((1, D), lambda i: (0, 0))


def _post_pre_fwd(x, z, w_post, w_pre, name):
    def body(x_ref, z_ref, wp_ref, wn_ref, x_out, h_out):
        zv = z_ref[...]
        xn = x_ref[...] + zv * _rms(zv) * wp_ref[...]
        x_out[...] = xn
        h_out[...] = (xn * _rms(xn) * wn_ref[...]).astype(BF16)

    return pl.pallas_call(
        body, name=name, grid=(T // ROWS,), in_specs=[_ROW, _ROW, _VEC, _VEC], out_specs=(_ROW, _ROW),
        out_shape=(jax.ShapeDtypeStruct((T, D), F32), jax.ShapeDtypeStruct((T, D), BF16)),
        compiler_params=_cp(("parallel",)),
    )(x, z, w_post, w_pre)


def _post_loss(x, z, w_post, tgt):
    def body(x_ref, z_ref, w_ref, t_ref, g_ref, l_ref, dz_ref, dw_ref):
        @pl.when(pl.program_id(0) == 0)
        def _():
            l_ref[...] = jnp.zeros_like(l_ref)
            dw_ref[...] = jnp.zeros_like(dw_ref)

        zv = z_ref[...]
        e = x_ref[...] + zv * _rms(zv) * w_ref[...] - t_ref[...]
        g = e * (1.0 / D)
        g_ref[...] = g
        l_ref[...] += jnp.sum(e * e) * (0.5 / D)
        dz, dw = _rms_bwd(zv, w_ref[...], g)
        dz_ref[...] = dz.astype(BF16)
        dw_ref[...] += dw

    return pl.pallas_call(
        body, name="postnorm_loss", grid=(T // ROWS,), in_specs=[_ROW, _ROW, _VEC, _ROW],
        out_specs=(_ROW, pl.BlockSpec((1, 128), lambda i: (0, 0)), _ROW, _VEC),
        out_shape=(jax.ShapeDtypeStruct((T, D), F32), jax.ShapeDtypeStruct((1, 128), F32),
                   jax.ShapeDtypeStruct((T, D), BF16), jax.ShapeDtypeStruct((1, D), F32)),
        compiler_params=_cp(("arbitrary",)),
    )(x, z, w_post, tgt)


def _pre_post_bwd(x, w_pre, dh, add, z, w_post, name):
    def body(x_ref, wn_ref, dh_ref, add_ref, z_ref, wp_ref, g_ref, dz_ref, dwn_ref, dwp_ref):
        @pl.when(pl.program_id(0) == 0)
        def _():
            dwn_ref[...] = jnp.zeros_like(dwn_ref)
            dwp_ref[...] = jnp.zeros_like(dwp_ref)

        dx, dwn = _rms_bwd(x_ref[...], wn_ref[...], dh_ref[...])
        g = dx + add_ref[...]
        g_ref[...] = g
        dz, dwp = _rms_bwd(z_ref[...], wp_ref[...], g)
        dz_ref[...] = dz.astype(BF16)
        dwn_ref[...] += dwn
        dwp_ref[...] += dwp

    return pl.pallas_call(
        body, name=name, grid=(T // ROWS,), in_specs=[_ROW, _VEC, _ROW, _ROW, _ROW, _VEC],
        out_specs=(_ROW, _ROW, _VEC, _VEC),
        out_shape=(jax.ShapeDtypeStruct((T, D), F32), jax.ShapeDtypeStruct((T, D), BF16),
                   jax.ShapeDtypeStruct((1, D), F32), jax.ShapeDtypeStruct((1, D), F32)),
        compiler_params=_cp(("arbitrary",)),
    )(x, w_pre, dh, add, z, w_post)


def _norm_bwd(z, w, dy, add, name):
    has_add = add is not None

    def body(*refs):
        if has_add:
            z_ref, w_ref, dy_ref, add_ref, dz_ref, dw_ref = refs
        else:
            z_ref, w_ref, dy_ref, dz_ref, dw_ref = refs
        i = pl.program_id(0)

        @pl.when(i == 0)
        def _():
            dw_ref[...] = jnp.zeros_like(dw_ref)

        zv = z_ref[...].astype(F32)
        dyv = dy_ref[...]
        r = lax.rsqrt(jnp.mean(zv * zv, axis=-1, keepdims=True) + EPS)
        wdy = dyv * w_ref[...]
        dz = r * wdy - zv * (r * r * r) * jnp.mean(zv * wdy, axis=-1, keepdims=True)
        if has_add:
            dz = dz + add_ref[...]
        dz_ref[...] = dz.astype(dz_ref.dtype)
        dw_ref[...] += jnp.sum(dyv * zv * r, axis=0, keepdims=True)

    row = pl.BlockSpec((ROWS, D), lambda i: (i, 0))
    vec = pl.BlockSpec((1, D), lambda i: (0, 0))
    ins = [z, w, dy] + ([add] if has_add else [])
    dz_dtype = F32 if has_add else BF16
    return pl.pallas_call(
        body, name=name, grid=(T // ROWS,),
        in_specs=[row, vec, row] + ([row] if has_add else []),
        out_specs=(row, vec),
        out_shape=(jax.ShapeDtypeStruct((T, D), dz_dtype), jax.ShapeDtypeStruct((1, D), F32)),
        compiler_params=_cp(("arbitrary",)),
    )(*ins)


def _adamw_math(w, g, m, v):
    c1 = 1.0 - ADAM_B1 ** ADAM_STEP
    c2 = 1.0 - ADAM_B2 ** ADAM_STEP
    mn = ADAM_B1 * m + (1.0 - ADAM_B1) * g
    vn = ADAM_B2 * v + (1.0 - ADAM_B2) * (g * g)
    return -ADAM_LR * ((mn / c1) / (jnp.sqrt(vn / c2) + ADAM_EPS) + ADAM_WD * w), mn, vn


def _adamw_from(w, m, v, sources, tr, name):
    layers, rows, cols = w.shape
    assert len(sources) == layers and rows % tr == 0, (name, w.shape)
    g_specs = []
    for layer, (buf, row0, transposed) in enumerate(sources):
        step = lambda l, i, layer=layer: jnp.where(l == layer, i, 0)
        if transposed:
            assert row0 % cols == 0 and buf.shape[1] == rows, (name, row0)
            g_specs.append(pl.BlockSpec((cols, tr), lambda l, i, b=row0 // cols, step=step: (b, step(l, i))))
        else:
            assert row0 % tr == 0 and buf.shape[1] == cols, (name, row0)
            g_specs.append(pl.BlockSpec((tr, cols), lambda l, i, b=row0 // tr, step=step: (b + step(l, i), 0)))

    def body(*refs):
        w_ref, m_ref, v_ref = refs[:3]
        g_refs = refs[3:3 + layers]
        g_out, d_ref, mo_ref, vo_ref = refs[3 + layers:]
        gs = [r[...].T if src[2] else r[...] for r, src in zip(g_refs, sources)]
        g = gs[0] if layers == 1 else jnp.where(pl.program_id(0) == 0, gs[0], gs[1])
        g_out[...] = g
        d_ref[...], mo_ref[...], vo_ref[...] = _adamw_math(w_ref[...], g, m_ref[...], v_ref[...])

    blk = pl.BlockSpec((None, tr, cols), lambda l, i: (l, i, 0))
    sds = jax.ShapeDtypeStruct(w.shape, F32)
    return pl.pallas_call(body, name=name, grid=(layers, rows // tr), in_specs=[blk] * 3 + g_specs,
                          out_specs=(blk,) * 4, out_shape=(sds,) * 4,
                          compiler_params=_cp(("parallel", "parallel")))(w, m, v, *[s[0] for s in sources])


def _adamw(w, g, m, v, name):
    lead = w.shape[:-2]
    assert len(lead) <= 1 and g.shape == w.shape, (name, w.shape, g.shape)
    rows, cols = w.shape[-2:]
    if rows <= 512:
        tr, tc = rows, cols
    elif rows % 256 == 0:
        tr, tc = 256, cols
    else:
        tr, tc = rows, 256
    assert rows % tr == 0 and cols % tc == 0, (name, w.shape)
    c1 = 1.0 - ADAM_B1 ** ADAM_STEP
    c2 = 1.0 - ADAM_B2 ** ADAM_STEP

    def body(w_ref, g_ref, m_ref, v_ref, d_ref, mo_ref, vo_ref):
        gv = g_ref[...]
        mn = ADAM_B1 * m_ref[...] + (1.0 - ADAM_B1) * gv
        vn = ADAM_B2 * v_ref[...] + (1.0 - ADAM_B2) * (gv * gv)
        m_hat = mn / c1
        v_hat = vn / c2
        d_ref[...] = -ADAM_LR * (m_hat / (jnp.sqrt(v_hat) + ADAM_EPS) + ADAM_WD * w_ref[...])
        mo_ref[...] = mn
        vo_ref[...] = vn

    if lead:
        grid = (lead[0], rows // tr, cols // tc)
        blk = pl.BlockSpec((None, tr, tc), lambda l, i, j: (l, i, j))
    else:
        grid = (rows // tr, cols // tc)
        blk = pl.BlockSpec((tr, tc), lambda i, j: (i, j))
    sds = jax.ShapeDtypeStruct(w.shape, F32)
    return pl.pallas_call(body, name=name, grid=grid, in_specs=[blk] * 4, out_specs=(blk,) * 3,
                          out_shape=(sds,) * 3, compiler_params=_cp(("parallel",) * len(grid)))(w, g, m, v)


def _running_sum(x, towards_later):
    n = x.shape[0]
    row = _iota(x.shape, 0)
    s = 1
    while s < n:
        if towards_later:
            x = x + jnp.where(row >= s, pltpu.roll(x, s, 0), 0.0)
        else:
            x = x + jnp.where(row < n - s, pltpu.roll(x, n - s, 0), 0.0)
        s *= 2
    return x


@jax.custom_vjp
def _cumsum_rows(x):
    return _running_sum(x, True)


_cumsum_rows.defvjp(lambda x: (_running_sum(x, True), None), lambda _, g: (_running_sum(g, False),))


def _gla_consts():
    return (_iota((256, 512), 0) // 64 == _iota((256, 512), 1) // 128).astype(F32)


def _gla_chunk(mask, q, k, v, r, aux, s_prev, wa, ba, nw):
    la = _log_sigmoid(bdot(aux, wa, "nn") + ba) * (1.0 / 16.0)
    cum = _cumsum_rows(la)
    total = jnp.sum(la, axis=0, keepdims=True)
    k_dec = k * jnp.exp(total - cum)
    inc = bdot(k_dec, v, "tn") * mask
    dec = jnp.exp(jnp.broadcast_to(total, (128, 256)).T)
    dec = jnp.concatenate([dec, dec, dec, dec], axis=1)
    s_new = dec * s_prev + inc
    o = bdot(q * GLA_SCALE, s_new, "nn")
    parts = []
    for h in range(4):
        oh = o[:, h * 128:(h + 1) * 128]
        parts.append(oh * lax.rsqrt(jnp.mean(oh * oh, axis=-1, keepdims=True) + EPS))
    on = jnp.concatenate(parts, axis=1)
    return s_new, on * nw * (r * _sigmoid(r))


GLA_PER_STEP = 4
GLA_ROWS = GLA_PER_STEP * CHUNK
GLA_STEPS = NCHUNK // GLA_PER_STEP


def _gla_specs(cmap):
    return [pl.BlockSpec((GLA_ROWS, 256), lambda c: (cmap(c), 0)),
            pl.BlockSpec((GLA_ROWS, 256), lambda c: (cmap(c), 1)),
            pl.BlockSpec((GLA_ROWS, 512), lambda c: (cmap(c), 1)),
            pl.BlockSpec((GLA_ROWS, 512), lambda c: (cmap(c), 2)),
            pl.BlockSpec((GLA_ROWS, 128), lambda c: (cmap(c), AUX_BLK))]


def _gla_fwd(proj, wa, ba, nw):
    def body(q_ref, k_ref, v_ref, r_ref, aux_ref, wa_ref, ba_ref, nw_ref, o_ref, sp_ref, s_ref):
        @pl.when(pl.program_id(0) == 0)
        def _():
            s_ref[...] = jnp.zeros_like(s_ref)

        s = s_ref[...]
        consts = _gla_consts()
        outs, states = [], []
        for i in range(GLA_PER_STEP):
            rows = slice(i * CHUNK, (i + 1) * CHUNK)
            states.append(s)
            s, out = _gla_chunk(consts, q_ref[rows, :], k_ref[rows, :], v_ref[rows, :], r_ref[rows, :], aux_ref[rows, :],
                                s, wa_ref[...], ba_ref[...], nw_ref[...])
            outs.append(out)
        s_ref[...] = s
        for i in range(GLA_PER_STEP):
            o_ref[i * CHUNK:(i + 1) * CHUNK, :] = outs[i]
            sp_ref[i] = states[i]

    full = lambda shape: pl.BlockSpec(shape, lambda c: (0,) * len(shape))
    return pl.pallas_call(
        body, name="gla_fwd", grid=(GLA_STEPS,),
        in_specs=_gla_specs(lambda c: c) + [full((128, 256)), full((1, 256)), full((1, 512))],
        out_specs=(pl.BlockSpec((GLA_ROWS, 512), lambda c: (c, 0)),
                   pl.BlockSpec((GLA_PER_STEP, 256, 512), lambda c: (c, 0, 0))),
        out_shape=(jax.ShapeDtypeStruct((T, D), F32), jax.ShapeDtypeStruct((NCHUNK, 256, 512), F32)),
        scratch_shapes=[pltpu.VMEM((256, 512), F32)],
        compiler_params=_cp(("arbitrary",)),
    )(proj, proj, proj, proj, proj, wa, ba, nw)


def _gla_bwd(proj, s_prev_all, wa, ba, nw, dcat):
    rev = lambda c: GLA_STEPS - 1 - c

    def body(q_ref, k_ref, v_ref, r_ref, aux_ref, sp_ref, wa_ref, ba_ref, nw_ref, do_ref,
             dq_ref, dk_ref, dv_ref, dr_ref, daux_ref, dwa_ref, dba_ref, dnw_ref, ds_ref):
        @pl.when(pl.program_id(0) == 0)
        def _():
            ds_ref[...] = jnp.zeros_like(ds_ref)
            dwa_ref[...] = jnp.zeros_like(dwa_ref)
            dba_ref[...] = jnp.zeros_like(dba_ref)
            dnw_ref[...] = jnp.zeros_like(dnw_ref)

        fn = functools.partial(_gla_chunk, _gla_consts())
        ds = ds_ref[...]
        dwa, dba, dnw = dwa_ref[...], dba_ref[...], dnw_ref[...]
        grads = {}
        for i in reversed(range(GLA_PER_STEP)):
            rows = slice(i * CHUNK, (i + 1) * CHUNK)
            _, vjp = jax.vjp(fn, q_ref[rows, :], k_ref[rows, :], v_ref[rows, :], r_ref[rows, :], aux_ref[rows, :],
                             sp_ref[i], wa_ref[...], ba_ref[...], nw_ref[...])
            *grads[i], ds, dwa_i, dba_i, dnw_i = vjp((ds, do_ref[rows, :]))
            dwa, dba, dnw = dwa + dwa_i, dba + dba_i, dnw + dnw_i
        ds_ref[...] = ds
        dwa_ref[...] = dwa
        dba_ref[...] = dba
        dnw_ref[...] = dnw
        for i in range(GLA_PER_STEP):
            rows = slice(i * CHUNK, (i + 1) * CHUNK)
            for ref, g in zip((dq_ref, dk_ref, dv_ref, dr_ref, daux_ref), grads[i]):
                ref[rows, :] = g

    full = lambda shape: pl.BlockSpec(shape, lambda c: (0,) * len(shape))
    blk = lambda w: pl.BlockSpec((GLA_ROWS, w), lambda c: (rev(c), 0))
    sds = lambda *s: jax.ShapeDtypeStruct(s, F32)
    return pl.pallas_call(
        body, name="gla_bwd", grid=(GLA_STEPS,),
        in_specs=_gla_specs(rev) + [pl.BlockSpec((GLA_PER_STEP, 256, 512), lambda c: (rev(c), 0, 0)),
                                    full((128, 256)), full((1, 256)), full((1, 512)), blk(512)],
        out_specs=(blk(256), blk(256), blk(512), blk(512), blk(128), full((128, 256)), full((1, 256)), full((1, 512))),
        out_shape=(sds(T, 256), sds(T, 256), sds(T, 512), sds(T, 512), sds(T, 128),
                   sds(128, 256), sds(1, 256), sds(1, 512)),
        scratch_shapes=[pltpu.VMEM((256, 512), F32)],
        compiler_params=_cp(("arbitrary",)),
    )(proj, proj, proj, proj, proj, s_prev_all, wa, ba, nw, dcat)


def _prefix8(x, towards_later):
    row = _iota(x.shape, 0)
    for s in (1, 2, 4):
        if towards_later:
            keep, shift = row >= s, s
        else:
            keep, shift = row < 8 - s, 8 - s
        x = x + jnp.where(keep, pltpu.roll(x, shift, 0), 0.0)
    return x


def _fox_gate_fwd(proj, bpad):
    def body(aux_ref, b_ref, cum_ref):
        cum_ref[...] = _log_sigmoid(aux_ref[...] + b_ref[...])

        def step(i, carry):
            rows = pl.ds(pl.multiple_of(i * 8, 8), 8)
            cum = _prefix8(cum_ref[rows, :], True) + carry
            cum_ref[rows, :] = cum
            return jnp.broadcast_to(cum[7:, :], (8, 128))

        lax.fori_loop(0, T // 8, step, jnp.zeros((8, 128), F32), unroll=4)

    return pl.pallas_call(
        body, name="fox_gate_fwd", grid=(1,),
        in_specs=[pl.BlockSpec((T, 128), lambda i: (0, AUX_BLK)), pl.BlockSpec((1, 128), lambda i: (0, 0))],
        out_specs=pl.BlockSpec((T, 128), lambda i: (0, 0)),
        out_shape=jax.ShapeDtypeStruct((T, 128), F32),
        compiler_params=_cp(("arbitrary",)),
    )(proj, bpad)


def _fox_gate_bwd(proj, bpad, dccol_t, daux_gla):
    def body(aux_ref, b_ref, dc_ref, dg_ref, daux_ref, db_ref):
        def step(i, carry):
            rows = pl.ds(pl.multiple_of(T - 8 * (i + 1), 8), 8)
            dlf = _prefix8(dc_ref[rows, :], False) + carry
            daux_ref[rows, :] = dlf
            return jnp.broadcast_to(dlf[:1, :], (8, 128))

        lax.fori_loop(0, T // 8, step, jnp.zeros((8, 128), F32), unroll=4)
        dz = daux_ref[...] * _sigmoid(-(aux_ref[...] + b_ref[...]))
        daux_ref[...] = dz + dg_ref[...]
        db_ref[...] = jnp.sum(dz, axis=0, keepdims=True)

    whole = pl.BlockSpec((T, 128), lambda i: (0, 0))
    vec = pl.BlockSpec((1, 128), lambda i: (0, 0))
    return pl.pallas_call(
        body, name="fox_gate_bwd", grid=(1,),
        in_specs=[pl.BlockSpec((T, 128), lambda i: (0, AUX_BLK)), vec, whole, whole],
        out_specs=(whole, vec),
        out_shape=(jax.ShapeDtypeStruct((T, 128), F32), jax.ShapeDtypeStruct((1, 128), F32)),
        compiler_params=_cp(("arbitrary",)),
    )(proj, bpad, dccol_t, daux_gla)


FOX_Q = 256


FOX_QB = T // FOX_Q


@jax.custom_vjp
def _attend(s, v):
    return _attend_fwd(s, v)[0]


def _attend_fwd(s, v):
    e = jnp.exp(s - jnp.max(s, axis=-1, keepdims=True))
    r = 1.0 / jnp.sum(e, axis=-1, keepdims=True)
    return _dot(e, v, "nn") * r, (e, r, v)


def _attend_bwd(res, do):
    e, r, v = res
    do_r = do * r
    dpr = _dot(do_r, v, "nt")
    ds = e * (dpr - r * jnp.sum(e * dpr, axis=-1, keepdims=True))
    return ds, _dot(e, do_r, "tn").astype(v.dtype)


_attend.defvjp(_attend_fwd, _attend_bwd)


def _fox_block(hp, q, k, v, ccol):
    kl = k.shape[0]
    lane = _iota((FOX_Q, 128), 1)
    tri = jnp.bitwise_and(_iota((2 * FOX_Q, FOX_Q), 0), FOX_Q - 1) >= _iota((2 * FOX_Q, FOX_Q), 1)
    sub = _iota((8, kl), 0)
    qs = q * ATT_SCALE
    q2 = jnp.concatenate([jnp.where(lane < 64, qs, 0.0), jnp.where(lane >= 64, qs, 0.0)], axis=0)
    s = bdot(q2, k, "nt")
    cs = [jnp.sum(jnp.where(sub == 2 * hp + e, ccol, 0.0), axis=0, keepdims=True) for e in range(2)]
    s = jnp.concatenate([s[:FOX_Q] - cs[0], s[FOX_Q:] - cs[1]], axis=0)
    diag = jnp.where(tri, s[:, kl - FOX_Q:], NEG)
    s = diag if kl == FOX_Q else jnp.concatenate([s[:, :kl - FOX_Q], diag], axis=1)
    o2 = _attend(s, v)
    return jnp.where(lane < 64, o2[:FOX_Q], o2[FOX_Q:])


def _fox_in_specs():
    return [pl.BlockSpec((FOX_Q, 128), lambda hp, qb: (qb, 12 + hp)),
            pl.BlockSpec((T, 128), lambda hp, qb: (0, 16 + hp)),
            pl.BlockSpec((T, 128), lambda hp, qb: (0, 20 + hp)),
            pl.BlockSpec((8, T), lambda hp, qb: (0, 0))]


def _fox_fwd(proj, cum_c, cat):
    def body(q_ref, k_ref, v_ref, cc_ref, cat_ref, o_ref):
        qb = pl.program_id(1)
        for g in range(FOX_QB):
            kl = FOX_Q * (g + 1)

            @pl.when(qb == g)
            def _(kl=kl):
                o_ref[...] = _fox_block(pl.program_id(0), q_ref[...], k_ref[0:kl, :], v_ref[0:kl, :], cc_ref[:, 0:kl])

    return pl.pallas_call(
        body, name="fox_fwd", grid=(4, FOX_QB), in_specs=_fox_in_specs() + [pl.BlockSpec(memory_space=pl.ANY)],
        out_specs=pl.BlockSpec((FOX_Q, 128), lambda hp, qb: (qb, 4 + hp)),
        out_shape=jax.ShapeDtypeStruct((T, D), F32), input_output_aliases={4: 0},
        compiler_params=_cp(("parallel", "parallel")),
    )(proj, proj, proj, cum_c, cat)


def _fox_bwd(proj, cum_c, dcat):
    def body(q_ref, k_ref, v_ref, cc_ref, do_ref, dq_ref, dk_ref, dv_ref, dcc_ref):
        qb = pl.program_id(1)

        @pl.when(qb == 0)
        def _():
            dk_ref[...] = jnp.zeros_like(dk_ref)
            dv_ref[...] = jnp.zeros_like(dv_ref)
            dcc_ref[...] = jnp.zeros_like(dcc_ref)

        fn = functools.partial(_fox_block, pl.program_id(0))
        for g in range(FOX_QB):
            kl = FOX_Q * (g + 1)

            @pl.when(qb == g)
            def _(kl=kl):
                _, vjp = jax.vjp(fn, q_ref[...], k_ref[0:kl, :], v_ref[0:kl, :], cc_ref[:, 0:kl])
                dq, dk, dv, dcc = vjp(do_ref[...])
                dq_ref[...] = dq
                dk_ref[0:kl, :] += dk
                dv_ref[0:kl, :] += dv
                dcc_ref[:, 0:kl] += dcc

    sds = lambda *s: jax.ShapeDtypeStruct(s, F32)
    return pl.pallas_call(
        body, name="fox_bwd", grid=(4, FOX_QB),
        in_specs=_fox_in_specs() + [pl.BlockSpec((FOX_Q, 128), lambda hp, qb: (qb, 4 + hp))],
        out_specs=(pl.BlockSpec((FOX_Q, 128), lambda hp, qb: (qb, hp)),
                   pl.BlockSpec((T, 128), lambda hp, qb: (0, hp)),
                   pl.BlockSpec((T, 128), lambda hp, qb: (0, hp)),
                   pl.BlockSpec((None, 8, T), lambda hp, qb: (hp, 0, 0))),
        out_shape=(sds(T, 512), sds(T, 512), sds(T, 512), sds(4, 8, T)),
        compiler_params=_cp(("parallel", "arbitrary")),
    )(proj, proj, proj, cum_c, dcat)


BIAS_W = 640


def _rel_onehot():
    j = _iota((REL_PAD, BIAS_W), 1)
    rel = jnp.clip(CA_PAD + CHUNK - 1 - j, -128, 128) + 128
    return (_iota((REL_PAD, BIAS_W), 0) == rel).astype(F32)


def _bias_build(rbp):
    def body(rb_ref, o_ref):
        f = _hdot_raw(rb_ref[...], _rel_onehot(), "nn")
        for q in range(CHUNK):
            o_ref[q] = pltpu.roll(f, (BIAS_W - (CHUNK - 1 - q)) % BIAS_W, 1)[:, :CA_BAND]

    return pl.pallas_call(body, name="ca_bias_build", out_shape=jax.ShapeDtypeStruct((CHUNK, 8, CA_BAND), F32))(rbp)


def _bias_grad(dbias_q):
    def body(db_ref, o_ref):
        acc = jnp.zeros((8, BIAS_W), F32)
        for q in range(CHUNK):
            acc = acc + pltpu.roll(db_ref[q], CHUNK - 1 - q, 1)
        o_ref[...] = _hdot_raw(acc, _rel_onehot(), "nt")

    return pl.pallas_call(body, name="ca_bias_grad", out_shape=jax.ShapeDtypeStruct((8, REL_PAD), F32))(dbias_q)


def _ca_block(c, masked, q, kb, vb, bias2):
    lane = _iota((CHUNK, 128), 1)
    qs = q * ATT_SCALE
    q2 = jnp.concatenate([jnp.where(lane < 64, qs, 0.0), jnp.where(lane >= 64, qs, 0.0)], axis=0)
    s = bdot(q2, kb, "nt") + bias2.reshape(2 * CHUNK, CA_BAND)
    if masked:
        s = jnp.where((c * CHUNK - CA_PAD + _iota((2 * CHUNK, CA_BAND), 1)) >= 0, s, NEG)
    o2 = _attend(s, vb)
    return jnp.where(lane < 64, o2[:CHUNK], o2[CHUNK:])


CA_PER_STEP = 16
CA_ROWS = CA_PER_STEP * CHUNK
CA_MASKED_CHUNKS = CA_PAD // CHUNK
assert CA_PER_STEP >= CA_MASKED_CHUNKS
CA_MASKED_STEPS = 1


def _ca_fwd(proj, kvpad, bias):
    def body(q_ref, k_ref, v_ref, b_ref, o_ref):
        def run(masked):
            outs = []
            for i in range(CA_PER_STEP):
                c = pl.program_id(1) * CA_PER_STEP + i
                band = pl.ds(pl.multiple_of(c * CHUNK, CHUNK), CA_BAND)
                rows = slice(i * CHUNK, (i + 1) * CHUNK)
                outs.append(_ca_block(c, masked and i < CA_MASKED_CHUNKS, q_ref[rows, :], k_ref[band, :], v_ref[band, :],
                                      b_ref[...]))
            for i in range(CA_PER_STEP):
                o_ref[i * CHUNK:(i + 1) * CHUNK, :] = outs[i]

        pl.when(pl.program_id(1) < CA_MASKED_STEPS)(lambda: run(True))
        pl.when(pl.program_id(1) >= CA_MASKED_STEPS)(lambda: run(False))

    return pl.pallas_call(
        body, name="ca_fwd", grid=(4, NCHUNK // CA_PER_STEP),
        in_specs=[pl.BlockSpec((CA_ROWS, 128), lambda hp, c: (c, hp)),
                  pl.BlockSpec((T + CA_PAD, 128), lambda hp, c: (0, hp)),
                  pl.BlockSpec((T + CA_PAD, 128), lambda hp, c: (0, 4 + hp)),
                  pl.BlockSpec((2, CHUNK, CA_BAND), lambda hp, c: (hp, 0, 0))],
        out_specs=pl.BlockSpec((CA_ROWS, 128), lambda hp, c: (c, hp)),
        out_shape=jax.ShapeDtypeStruct((T, D), F32),
        compiler_params=_cp(("parallel", "parallel")),
    )(proj, kvpad, kvpad, bias)


def _ca_bwd(proj, kvpad, bias, dcat):
    def body(q_ref, k_ref, v_ref, b_ref, do_ref, dq_ref, dk_ref, dv_ref, db_ref):
        c = pl.program_id(1)

        @pl.when(c == 0)
        def _():
            dk_ref[...] = jnp.zeros_like(dk_ref)
            dv_ref[...] = jnp.zeros_like(dv_ref)
            db_ref[...] = jnp.zeros_like(db_ref)

        def run(masked):
            grads, bands = [], []
            for i in range(CA_PER_STEP):
                ci = c * CA_PER_STEP + i
                band = pl.ds(pl.multiple_of(ci * CHUNK, CHUNK), CA_BAND)
                rows = slice(i * CHUNK, (i + 1) * CHUNK)
                fn = functools.partial(_ca_block, ci, masked and i < CA_MASKED_CHUNKS)
                _, vjp = jax.vjp(fn, q_ref[rows, :], k_ref[band, :], v_ref[band, :], b_ref[...])
                grads.append(vjp(do_ref[rows, :]))
                bands.append(band)
            for i, (dq, _, _, _) in enumerate(grads):
                dq_ref[i * CHUNK:(i + 1) * CHUNK, :] = dq
            for band, (_, dkb, dvb, _) in zip(bands, grads):
                dk_ref[band, :] += dkb
                dv_ref[band, :] += dvb
            db_ref[...] += functools.reduce(lambda a, b: a + b, [g[3] for g in grads])

        pl.when(c < CA_MASKED_STEPS)(lambda: run(True))
        pl.when(c >= CA_MASKED_STEPS)(lambda: run(False))

    sds = lambda *s: jax.ShapeDtypeStruct(s, F32)
    padded = lambda: pl.BlockSpec((T + CA_PAD, 128), lambda hp, c: (0, hp))
    return pl.pallas_call(
        body, name="ca_bwd", grid=(4, NCHUNK // CA_PER_STEP),
        in_specs=[pl.BlockSpec((CA_ROWS, 128), lambda hp, c: (c, hp)),
                  pl.BlockSpec((T + CA_PAD, 128), lambda hp, c: (0, hp)),
                  pl.BlockSpec((T + CA_PAD, 128), lambda hp, c: (0, 4 + hp)),
                  pl.BlockSpec((2, CHUNK, CA_BAND), lambda hp, c: (hp, 0, 0)),
                  pl.BlockSpec((CA_ROWS, 128), lambda hp, c: (c, hp))],
        out_specs=(pl.BlockSpec((CA_ROWS, 128), lambda hp, c: (c, hp)), padded(), padded(),
                   pl.BlockSpec((2, CHUNK, CA_BAND), lambda hp, c: (hp, 0, 0))),
        out_shape=(sds(T, 512), sds(T + CA_PAD, 512), sds(T + CA_PAD, 512), sds(8, CHUNK, CA_BAND)),
        compiler_params=_cp(("parallel", "arbitrary")),
    )(proj, kvpad, kvpad, bias, dcat)


def _block_diag_dot(x, w):
    return jnp.concatenate([bdot(x[:, :256], w[:256, :256], "nn"), bdot(x[:, 256:], w[256:, 256:], "nn")], axis=1)


def _lru_pre(xs, cw, cb, wa, ba, wx, bx, lam):
    xc = cb + xs[0] * cw[0:1, :] + xs[1] * cw[1:2, :] + xs[2] * cw[2:3, :] + xs[3] * cw[3:4, :]
    ra = _sigmoid(_block_diag_dot(xc, wa) + ba)
    ii = _sigmoid(_block_diag_dot(xc, wx) + bx)
    la = 8.0 * ra * _log_sigmoid(lam)
    return jnp.exp(la), jnp.sqrt(-_expm1(2.0 * la)) * (ii * xc)


def _lru_pre_specs():
    full = lambda shape: pl.BlockSpec(shape, lambda i: (0,) * len(shape))
    return [pl.BlockSpec((4, ROWS, 512), lambda i: (0, i, 0)), full((4, 512)), full((1, 512)),
            full((512, 512)), full((1, 512)), full((512, 512)), full((1, 512)), full((1, 512))]


def _lru_pre_fwd(xs, cw, cb, wa, ba, wx, bx, lam):
    def body(xs_ref, cw_ref, cb_ref, wa_ref, ba_ref, wx_ref, bx_ref, lam_ref, a_ref, b_ref):
        a, b = _lru_pre(xs_ref[...], cw_ref[...], cb_ref[...], wa_ref[...], ba_ref[...], wx_ref[...], bx_ref[...],
                        lam_ref[...])
        a_ref[...] = a
        b_ref[...] = b

    row = pl.BlockSpec((ROWS, 512), lambda i: (i, 0))
    sds = jax.ShapeDtypeStruct((T, 512), F32)
    return pl.pallas_call(body, name="lru_pre_fwd", grid=(T // ROWS,), in_specs=_lru_pre_specs(),
                          out_specs=(row, row), out_shape=(sds, sds), compiler_params=_cp(("parallel",)),
                          )(xs, cw, cb, wa, ba, wx, bx, lam)


def _lru_pre_bwd(xs, cw, cb, wa, ba, wx, bx, lam, da, db):
    def body(xs_ref, cw_ref, cb_ref, wa_ref, ba_ref, wx_ref, bx_ref, lam_ref, da_ref, db_ref,
             dxs_ref, dcw_ref, dcb_ref, dwa_ref, dba_ref, dwx_ref, dbx_ref, dlam_ref):
        acc = (dcw_ref, dcb_ref, dwa_ref, dba_ref, dwx_ref, dbx_ref, dlam_ref)

        @pl.when(pl.program_id(0) == 0)
        def _():
            for r in acc:
                r[...] = jnp.zeros_like(r)

        _, vjp = jax.vjp(_lru_pre, xs_ref[...], cw_ref[...], cb_ref[...], wa_ref[...], ba_ref[...], wx_ref[...],
                         bx_ref[...], lam_ref[...])
        grads = vjp((da_ref[...], db_ref[...]))
        dxs_ref[...] = grads[0]
        for r, g in zip(acc, grads[1:]):
            r[...] += g

    row = pl.BlockSpec((ROWS, 512), lambda i: (i, 0))
    specs = _lru_pre_specs()
    sds = lambda *s: jax.ShapeDtypeStruct(s, F32)
    return pl.pallas_call(
        body, name="lru_pre_bwd", grid=(T // ROWS,), in_specs=specs + [row, row], out_specs=tuple(specs),
        out_shape=(sds(4, T, 512), sds(4, 512), sds(1, 512), sds(512, 512), sds(1, 512), sds(512, 512), sds(1, 512),
                   sds(1, 512)),
        compiler_params=_cp(("arbitrary",)),
    )(xs, cw, cb, wa, ba, wx, bx, lam, da, db)


SCAN_ROWS = 8


def _scan8(a, b, towards_later):
    row = _iota((SCAN_ROWS, 512), 0)
    for s in (1, 2, 4):
        if towards_later:
            keep, shift = row >= s, s
        else:
            keep, shift = row < SCAN_ROWS - s, SCAN_ROWS - s
        a_s = jnp.where(keep, pltpu.roll(a, shift, 0), 1.0)
        b_s = jnp.where(keep, pltpu.roll(b, shift, 0), 0.0)
        b = a * b_s + b
        a = a * a_s
    return a, b


def _lru_scan_fwd(a, b):
    def body(a_ref, b_ref, h_ref):
        def step(i, carry):
            rows = pl.ds(pl.multiple_of(i * SCAN_ROWS, SCAN_ROWS), SCAN_ROWS)
            a8, b8 = _scan8(a_ref[rows, :], b_ref[rows, :], True)
            h = a8 * carry + b8
            h_ref[rows, :] = h
            return jnp.broadcast_to(h[SCAN_ROWS - 1:, :], (SCAN_ROWS, 512))

        lax.fori_loop(0, T // SCAN_ROWS, step, jnp.zeros((SCAN_ROWS, 512), F32), unroll=2)

    return pl.pallas_call(body, name="lru_scan_fwd", out_shape=jax.ShapeDtypeStruct((T, 512), F32),
                          compiler_params=pltpu.CompilerParams(vmem_limit_bytes=VMEM_LIMIT))(a, b)


def _lru_scan_bwd(a_next, h_prev, dh):
    def body(a_ref, h_ref, dh_ref, da_ref, db_ref):
        def step(i, carry):
            start = T - SCAN_ROWS * (i + 1)
            rows = pl.ds(pl.multiple_of(start, SCAN_ROWS), SCAN_ROWS)
            a8, b8 = _scan8(a_ref[rows, :], dh_ref[rows, :], False)
            g = a8 * carry + b8
            db_ref[rows, :] = g
            da_ref[rows, :] = g * h_ref[rows, :]
            return jnp.broadcast_to(g[:1, :], (SCAN_ROWS, 512))

        lax.fori_loop(0, T // SCAN_ROWS, step, jnp.zeros((SCAN_ROWS, 512), F32), unroll=2)

    sds = jax.ShapeDtypeStruct((T, 512), F32)
    return pl.pallas_call(body, name="lru_scan_bwd", out_shape=(sds, sds),
                          compiler_params=pltpu.CompilerParams(vmem_limit_bytes=VMEM_LIMIT))(a_next, h_prev, dh)


def _lru_post(h, gate):
    return h * _gelu_tanh(gate)


def _lru_post_fwd(h, proj, cat):
    def body(h_ref, g_ref, cat_ref, o_ref):
        o_ref[...] = _lru_post(h_ref[...], g_ref[...])

    row = pl.BlockSpec((ROWS, 512), lambda i: (i, 0))
    return pl.pallas_call(body, name="lru_post_fwd", grid=(T // ROWS,),
                          in_specs=[row, pl.BlockSpec((ROWS, 512), lambda i: (i, 3)), pl.BlockSpec(memory_space=pl.ANY)],
                          out_specs=pl.BlockSpec((ROWS, 512), lambda i: (i, 1)),
                          out_shape=jax.ShapeDtypeStruct((T, D), F32), input_output_aliases={2: 0},
                          compiler_params=_cp(("parallel",)))(h, proj, cat)


def _lru_post_bwd(h, proj, dcat):
    def body(h_ref, g_ref, do_ref, dh_ref, dg_ref):
        _, vjp = jax.vjp(_lru_post, h_ref[...], g_ref[...])
        dh, dg = vjp(do_ref[...])
        dh_ref[...] = dh
        dg_ref[...] = dg

    row = pl.BlockSpec((ROWS, 512), lambda i: (i, 0))
    sds = jax.ShapeDtypeStruct((T, 512), F32)
    return pl.pallas_call(body, name="lru_post_bwd", grid=(T // ROWS,),
                          in_specs=[row, pl.BlockSpec((ROWS, 512), lambda i: (i, 3)),
                                    pl.BlockSpec((ROWS, 512), lambda i: (i, 1))],
                          out_specs=(row, row), out_shape=(sds, sds), compiler_params=_cp(("parallel",)))(h, proj, dcat)


def _conv_dx(dxs_shift):
    def body(d_ref, o_ref):
        o_ref[...] = d_ref[0] + d_ref[1] + d_ref[2] + d_ref[3]

    row = pl.BlockSpec((ROWS, 512), lambda i: (i, 0))
    return pl.pallas_call(body, name="lru_conv_dx", grid=(T // ROWS,),
                          in_specs=[pl.BlockSpec((4, ROWS, 512), lambda i: (0, i, 0))], out_specs=row,
                          out_shape=jax.ShapeDtypeStruct((T, 512), F32), compiler_params=_cp(("parallel",)))(dxs_shift)


def _position():
    return lax.axis_index("x"), lax.axis_index("y"), lax.axis_index("c")


def _other_chips(x, y):
    return [(1 - x, y), (x, 1 - y), (1 - x, 1 - y)]


def _al(v, n):
    return v * n if isinstance(v, int) else pl.multiple_of(v * n, n)


_AG_ITEMS = [
    ((4, 32, 128), lambda o, s, h: o.at[s, pl.ds(_al(h, 16), 16), :], lambda r, h: r.at[pl.ds(_al(h, 16), 16), :]),
    ((4, 774, 1024), lambda o, s, h: o.at[s, :, pl.ds(_al(h, 512), 512)], lambda r, h: r.at[:, pl.ds(_al(h, 512), 512)]),
    ((1024, 1024), lambda o, s, h: o.at[pl.ds(_al(2 * s + h, 128), 128), :], lambda r, h: r.at[pl.ds(_al(h, 128), 128), :]),
    ((2, 1024, 4096), lambda o, s, h: o.at[h, :, pl.ds(_al(s, 1024), 1024)], lambda r, h: r.at[h]),
    ((2, 4096, 1024), lambda o, s, h: o.at[h, pl.ds(_al(s, 1024), 1024), :], lambda r, h: r.at[h]),
    ((1024, 2560), lambda o, s, h: o.at[pl.ds(_al(h, 512), 512), pl.ds(_al(s, 640), 640)],
     lambda r, h: r.at[pl.ds(_al(h, 512), 512), :]),
    ((1024, 1024), lambda o, s, h: o.at[pl.ds(_al(2 * s + h, 128), 128), :], lambda r, h: r.at[pl.ds(_al(h, 128), 128), :]),
]


_AG_GROUPS = [(0, 1, 2), (3, 4), (5, 6)]

_HBM = pl.BlockSpec(memory_space=pltpu.HBM)
_SEM = pl.BlockSpec(memory_space=pltpu.SEMAPHORE)
_SPLIT = dict(has_side_effects=pltpu.SideEffectType.DATAFLOW_SIDE_EFFECTING)


def _hbm(a):
    return pltpu.with_memory_space_constraint(a, pltpu.HBM)


def _ag_ici_copy(i, j, chip, c, slot, src_ref, land_ref, send_sems, recv_sems, k):
    _, dst, half = _AG_ITEMS[i]
    return pltpu.make_async_remote_copy(src_ref=half(src_ref, c), dst_ref=dst(land_ref, slot, c), send_sem=send_sems.at[k],
                                        recv_sem=recv_sems.at[k], device_id=(*chip, c), device_id_type=MESH)


def _ag_start(groups, shards, name):
    items_all = [i for g in groups for i in _AG_GROUPS[g]]
    n = len(items_all)
    ng = len(groups)
    lands = [lax.empty(_AG_ITEMS[i][0], shards[i].dtype) for i in items_all]

    def body(*refs):
        srcs, land_refs = dict(zip(items_all, refs[:n])), dict(zip(items_all, refs[n:2 * n]))
        sems = refs[2 * n:2 * n + 2 * ng]
        token = refs[-1]
        x, y, c = _position()
        me = 2 * x + y
        for gi, g in enumerate(groups):
            for t, i in enumerate(_AG_GROUPS[g]):
                for j, chip in enumerate(_other_chips(x, y)):
                    _ag_ici_copy(i, j, chip, c, me, srcs[i], land_refs[i], sems[2 * gi], sems[2 * gi + 1], 3 * t + j).start()
        token[...] = jnp.zeros_like(token)

    sem_shapes = []
    for g in groups:
        sem_shapes += [pltpu.SemaphoreType.DMA((3 * len(_AG_GROUPS[g]),))] * 2
    ops = [shards[i] for i in items_all] + lands
    out = pl.pallas_call(
        body, name=name,
        out_shape=tuple(sem_shapes) + tuple(pltpu.HBM(a.shape, a.dtype) for a in ops) + (jax.ShapeDtypeStruct((8, 128), F32),),
        in_specs=(_HBM,) * (2 * n),
        out_specs=(_SEM,) * (2 * ng) + (_HBM,) * (2 * n) + (pl.BlockSpec(memory_space=pltpu.VMEM),),
        input_output_aliases={i: 2 * ng + i for i in range(2 * n)},
        compiler_params=pltpu.CompilerParams(**_SPLIT),
    )(*[_hbm(a) for a in ops])
    sems, thru, token = out[:2 * ng], out[2 * ng:-1], out[-1]
    return ({g: (sems[2 * gi], sems[2 * gi + 1]) for gi, g in enumerate(groups)},
            dict(zip(items_all, thru[:n])), dict(zip(items_all, thru[n:])), token)


def _ag_wait(g, sems, srcs, lands, after):
    items = _AG_GROUPS[g]
    m = len(items)

    def body(*refs):
        src_refs, land_refs = refs[:m], refs[m:2 * m]
        send_sems, recv_sems = refs[2 * m], refs[2 * m + 1]
        x, y, c = _position()
        for t, i in enumerate(items):
            for j, chip in enumerate(_other_chips(x, y)):
                cp = _ag_ici_copy(i, j, chip, c, 2 * chip[0] + chip[1], src_refs[t], land_refs[t], send_sems, recv_sems,
                                  3 * t + j)
                cp.wait_send()
                cp.wait_recv()

    ops = [srcs[i] for i in items] + [lands[i] for i in items]
    out = pl.pallas_call(
        body, name=f"allgather_wait_{g}",
        out_shape=tuple(pltpu.HBM(a.shape, a.dtype) for a in ops),
        in_specs=(_HBM,) * (2 * m) + (_SEM, _SEM, pl.BlockSpec(memory_space=pl.ANY)),
        out_specs=(_HBM,) * (2 * m),
        input_output_aliases={i: i for i in range(2 * m)},
        compiler_params=pltpu.CompilerParams(**_SPLIT),
    )(*ops, sems[0], sems[1], after)
    return list(out[:m]), list(out[m:])


def _ag_forward(g, srcs, lands):
    return _ag_sibling(_AG_GROUPS[g], srcs, lands, False, f"allgather_forward_{g}")


def _ag_push_own(srcs, lands):
    items = tuple(sorted(lands))
    out = _ag_sibling(items, [srcs[i] for i in items], [lands[i] for i in items], True, "allgather_push_own")
    return dict(zip(items, out))


def _ag_sibling(items, srcs, lands, own, name):
    m = len(items)
    per = 2 if own else 3

    def body(*refs):
        src_refs, in_refs, out_refs = refs[:m], refs[m:2 * m], refs[2 * m:3 * m]
        send_sems, recv_sems = refs[3 * m:]
        x, y, c = _position()
        sibling = (x, y, 1 - c)
        me = 2 * x + y
        if own:
            mine = theirs = [(me, 0), (me, 1)]
        else:
            slots = [2 * chip[0] + chip[1] for chip in _other_chips(x, y)]
            mine, theirs = [(s, c) for s in slots], [(s, 1 - c) for s in slots]
        sends = []
        for t, i in enumerate(items):
            _, dst, half = _AG_ITEMS[i]
            for k, (slot, hc) in enumerate(mine):
                src = half(src_refs[t], hc) if own else dst(in_refs[t], slot, hc)
                sends.append(pltpu.make_async_remote_copy(
                    src_ref=src, dst_ref=dst(out_refs[t], slot, hc), send_sem=send_sems.at[per * t + k],
                    recv_sem=recv_sems.at[per * t + k], device_id=sibling, device_id_type=MESH))
        for cp in sends:
            cp.start()
        for t, i in enumerate(items):
            dst = _AG_ITEMS[i][1]
            for k, (slot, hc) in enumerate(theirs):
                there = dst(out_refs[t], slot, hc)
                pltpu.make_async_remote_copy(src_ref=there, dst_ref=there, send_sem=send_sems.at[per * t + k],
                                             recv_sem=recv_sems.at[per * t + k], device_id=sibling,
                                             device_id_type=MESH).wait_recv()
        for cp in sends:
            cp.wait_send()

    any_spec = pl.BlockSpec(memory_space=pl.ANY)
    return pl.pallas_call(
        body, name=name,
        in_specs=[any_spec] * (2 * m), out_specs=(any_spec,) * m,
        out_shape=tuple(jax.ShapeDtypeStruct(a.shape, a.dtype) for a in lands),
        input_output_aliases={m + t: t for t in range(m)},
        scratch_shapes=[pltpu.SemaphoreType.DMA((per * m,)), pltpu.SemaphoreType.DMA((per * m,))],
    )(*srcs, *lands)


def _pair_swap_copy(g_ref, r_ref, send_sem, recv_sem):
    x, y, c = _position()
    hc = g_ref.shape[2] // 2
    return pltpu.make_async_remote_copy(src_ref=g_ref.at[:, :, pl.ds(_al(1 - c, hc), hc)], dst_ref=r_ref,
                                        send_sem=send_sem, recv_sem=recv_sem, device_id=(x, y, 1 - c),
                                        device_id_type=MESH)


def _pair_swap_start(gb, tag):
    _, rows, cols = gb.shape
    recv = lax.empty((4, rows, cols // 2), gb.dtype)

    def body(g_ref, r_ref, send_sem, recv_sem, g_thru, r_thru, token):
        _pair_swap_copy(g_ref, r_ref, send_sem, recv_sem).start()
        token[...] = jnp.zeros_like(token)

    return pl.pallas_call(
        body, name="grad_pair_swap_start_" + tag,
        out_shape=(pltpu.SemaphoreType.DMA(()), pltpu.SemaphoreType.DMA(()), pltpu.HBM(gb.shape, gb.dtype),
                   pltpu.HBM(recv.shape, recv.dtype), jax.ShapeDtypeStruct((8, 128), F32)),
        in_specs=(_HBM, _HBM), out_specs=(_SEM, _SEM, _HBM, _HBM, pl.BlockSpec(memory_space=pltpu.VMEM)),
        input_output_aliases={0: 2, 1: 3},
        compiler_params=pltpu.CompilerParams(**_SPLIT),
    )(_hbm(gb), _hbm(recv))


def _pair_swap_wait(started, after, tag):
    send_sem, recv_sem, gb, recv, _ = started

    def body(g_ref, r_ref, send_sem, recv_sem, after_ref, g_out, r_out):
        cp = _pair_swap_copy(g_ref, r_ref, send_sem, recv_sem)
        cp.wait_send()
        cp.wait_recv()

    return pl.pallas_call(
        body, name="grad_pair_swap_wait_" + tag,
        out_shape=(pltpu.HBM(gb.shape, gb.dtype), pltpu.HBM(recv.shape, recv.dtype)),
        in_specs=(_HBM, _HBM, _SEM, _SEM, pl.BlockSpec(memory_space=pl.ANY)), out_specs=(_HBM, _HBM),
        input_output_aliases={0: 0, 1: 1},
        compiler_params=pltpu.CompilerParams(**_SPLIT),
    )(gb, recv, send_sem, recv_sem, after)


def _handover_copy(r_ref, send_sem, recv_sem, core):
    x, y, c = _position()
    hc = r_ref.shape[1] // 2
    cols = r_ref.at[:, pl.ds(_al(core, hc), hc)]
    return pltpu.make_async_remote_copy(src_ref=cols, dst_ref=cols, send_sem=send_sem, recv_sem=recv_sem,
                                        device_id=(x, y, 1 - c), device_id_type=MESH)


def _handover_start(red, tag):
    def body(r_ref, send_sem, recv_sem, r_thru, token):
        _handover_copy(r_ref, send_sem, recv_sem, lax.axis_index("c")).start()
        token[...] = jnp.zeros_like(token)

    return pl.pallas_call(
        body, name="grad_handover_start_" + tag,
        out_shape=(pltpu.SemaphoreType.DMA(()), pltpu.SemaphoreType.DMA(()), pltpu.HBM(red.shape, red.dtype),
                   jax.ShapeDtypeStruct((8, 128), F32)),
        in_specs=(_HBM,), out_specs=(_SEM, _SEM, _HBM, pl.BlockSpec(memory_space=pltpu.VMEM)),
        input_output_aliases={0: 2},
        compiler_params=pltpu.CompilerParams(**_SPLIT),
    )(_hbm(red))


def _handover_wait(started, after, tag):
    send_sem, recv_sem, red, _ = started

    def body(r_ref, send_sem, recv_sem, after_ref, r_out):
        c = lax.axis_index("c")
        _handover_copy(r_ref, send_sem, recv_sem, c).wait_send()
        _handover_copy(r_ref, send_sem, recv_sem, 1 - c).wait_recv()

    return pl.pallas_call(
        body, name="grad_handover_wait_" + tag,
        out_shape=pltpu.HBM(red.shape, red.dtype),
        in_specs=(_HBM, _SEM, _SEM, pl.BlockSpec(memory_space=pl.ANY)), out_specs=_HBM,
        input_output_aliases={0: 0},
        compiler_params=pltpu.CompilerParams(**_SPLIT),
    )(red, send_sem, recv_sem, after)


def _handover(red, tag):
    started = _handover_start(red, tag)
    return _handover_wait(started, started[3], tag)


def _a2a_copy(j, chip, c, p_ref, q_ref, q_slot, send_sems, recv_sems):
    return pltpu.make_async_remote_copy(src_ref=p_ref.at[2 * chip[0] + chip[1]], dst_ref=q_ref.at[q_slot],
                                        send_sem=send_sems.at[j], recv_sem=recv_sems.at[j], device_id=(*chip, c),
                                        device_id_type=MESH)


def _a2a_start(p, tag):
    def body(p_ref, q_ref, send_sems, recv_sems, p_thru, q_thru, token):
        x, y, c = _position()
        for j, chip in enumerate(_other_chips(x, y)):
            _a2a_copy(j, chip, c, p_ref, q_ref, 2 * x + y, send_sems, recv_sems).start()
        token[...] = jnp.zeros_like(token)

    return pl.pallas_call(
        body, name="grad_alltoall_start_" + tag,
        out_shape=(pltpu.SemaphoreType.DMA((3,)), pltpu.SemaphoreType.DMA((3,)), pltpu.HBM(p.shape, p.dtype),
                   pltpu.HBM(p.shape, p.dtype), jax.ShapeDtypeStruct((8, 128), F32)),
        in_specs=(_HBM, _HBM), out_specs=(_SEM, _SEM, _HBM, _HBM, pl.BlockSpec(memory_space=pltpu.VMEM)),
        input_output_aliases={0: 2, 1: 3},
        compiler_params=pltpu.CompilerParams(**_SPLIT),
    )(_hbm(p), _hbm(lax.empty(p.shape, p.dtype)))


def _a2a_wait(send_sems, recv_sems, p, q, after, tag):
    def body(p_ref, q_ref, send_sems, recv_sems, after_ref, p_out, q_out):
        x, y, c = _position()
        for j, chip in enumerate(_other_chips(x, y)):
            cp = _a2a_copy(j, chip, c, p_ref, q_ref, 2 * chip[0] + chip[1], send_sems, recv_sems)
            cp.wait_send()
            cp.wait_recv()

    return pl.pallas_call(
        body, name="grad_alltoall_wait_" + tag,
        out_shape=(pltpu.HBM(p.shape, p.dtype), pltpu.HBM(q.shape, q.dtype)),
        in_specs=(_HBM, _HBM, _SEM, _SEM, pl.BlockSpec(memory_space=pl.ANY)), out_specs=(_HBM, _HBM),
        input_output_aliases={0: 0, 1: 1},
        compiler_params=pltpu.CompilerParams(**_SPLIT),
    )(p, q, send_sems, recv_sems, after)


def _comm_rows(rows):
    return next(t for t in (512, 384, 256, 128) if rows % t == 0)


def _pair_add(gb, recv, where, tag):
    _, rows, cols = gb.shape
    hc = cols // 2
    tr = _comm_rows(rows)

    def body(w_ref, g_ref, r_ref, o_ref):
        o_ref[...] = (g_ref[...].astype(F32) + r_ref[...].astype(F32)).astype(o_ref.dtype)

    return pl.pallas_call(
        body, name="grad_pair_add_" + tag,
        grid_spec=pltpu.PrefetchScalarGridSpec(
            num_scalar_prefetch=1, grid=(4, rows // tr),
            in_specs=[pl.BlockSpec((None, tr, hc), lambda s, j, w_ref: (s, j, w_ref[0])),
                      pl.BlockSpec((None, tr, hc), lambda s, j, w_ref: (s, j, 0))],
            out_specs=pl.BlockSpec((None, tr, hc), lambda s, j, w_ref: (s, j, 0))),
        out_shape=jax.ShapeDtypeStruct((4, rows, hc), gb.dtype),
        compiler_params=_cp(("parallel", "parallel")),
    )(where, gb, recv)


def _sum_chips(p, q, where, tag):
    _, rows, hc = q.shape
    tr = _comm_rows(rows)

    def body(w_ref, p_ref, qa_ref, qb_ref, qc_ref, o_ref):
        me = w_ref[1]
        own, qa, qb, qc = (r[...].astype(F32) for r in (p_ref, qa_ref, qb_ref, qc_ref))
        v0 = jnp.where(me == 0, own, qa)
        v1 = jnp.where(me == 1, own, jnp.where(me == 0, qa, qb))
        v2 = jnp.where(me == 2, own, jnp.where(me < 2, qb, qc))
        v3 = jnp.where(me == 3, own, qc)
        o_ref[...] = ((v0 + v1) + v2) + v3

    slot = lambda k: pl.BlockSpec((None, tr, hc), lambda j, w_ref: (w_ref[k], j, 0))
    return pl.pallas_call(
        body, name="grad_sum_chips_" + tag,
        grid_spec=pltpu.PrefetchScalarGridSpec(
            num_scalar_prefetch=1, grid=(rows // tr,),
            in_specs=[slot(1), slot(2), slot(3), slot(4)],
            out_specs=pl.BlockSpec((tr, hc), lambda j, w_ref: (j, w_ref[0]))),
        out_shape=jax.ShapeDtypeStruct((rows, 2 * hc), F32),
        compiler_params=_cp(("parallel",)),
    )(where, p, q, q, q)


def _shard_major(g, axis):
    shape = g.shape
    g = g.reshape(shape[:axis] + (4, shape[axis] // 4) + shape[axis + 1:])
    return jnp.moveaxis(g, axis, 0).reshape(4, -1)


def _unshard(g4, shape, axis):
    n = shape[axis] // 4
    g = g4.reshape((4,) + shape[:axis] + (n,) + shape[axis + 1:])
    return jnp.moveaxis(g, 0, axis).reshape(shape)


def _split(flat, shapes):
    out, off = [], 0
    for shp in shapes:
        n = 1
        for d in shp:
            n *= d
        out.append(flat[..., off:off + n].reshape(flat.shape[:-1] + tuple(shp)))
        off += n
    return out


def _even_rows_to_kernel(wt):
    return jnp.concatenate([wt[:1536], wt[1552:3088], wt[1536:1552], wt[3088:3096],
                            jnp.zeros((PE - 3096, wt.shape[1]), wt.dtype)], axis=0)


def _block_diag(w):
    eye = jnp.eye(8, dtype=w.dtype)
    return (w[:, :, None, :] * eye[:, None, :, None]).reshape(512, 512)


def _diag_blocks(g):
    eye = jnp.eye(8, dtype=g.dtype)
    return (g.reshape(8, 64, 8, 64) * eye[:, None, :, None]).sum(axis=2)


def _shift_down(a, s):
    return a if s == 0 else jnp.pad(a, ((s, 0), (0, 0)))[:a.shape[0]]


def _shift_up(a, s):
    return a if s == 0 else jnp.pad(a, ((0, s), (0, 0)))[s:]


SMALL_SHARDED_SHAPES = [(2, 4, 256), (16, 64), (4, 128), (128,), (128,), (128,), (128,)]
REPL_SHAPES = [(256,), (512,), (8,), (8, 257), (8, 64, 64), (8, 64, 64)]


def kernel(x, norm_w, w_in_even, gla_w_a_up, gla_b_a, gla_norm_w, fox_b_f, w_out_even, w_in_odd, rel_bias, conv_w, conv_b, lru_w_a, lru_b_a, lru_w_x, lru_b_x, lru_lambda, w_out_odd, w_mlp_up, w_mlp_down, loss_target, m_norm_w, m_w_in_even, m_gla_w_a_up, m_gla_b_a, m_gla_norm_w, m_fox_b_f, m_w_out_even, m_w_in_odd, m_rel_bias, m_conv_w, m_conv_b, m_lru_w_a, m_lru_b_a, m_lru_w_x, m_lru_b_x, m_lru_lambda, m_w_out_odd, m_w_mlp_up, m_w_mlp_down, v_norm_w, v_w_in_even, v_gla_w_a_up, v_gla_b_a, v_gla_norm_w, v_fox_b_f, v_w_out_even, v_w_in_odd, v_rel_bias, v_conv_w, v_conv_b, v_lru_w_a, v_lru_b_a, v_lru_w_x, v_lru_b_x, v_lru_lambda, v_w_out_odd, v_w_mlp_up, v_w_mlp_down):
    c_idx = lax.axis_index("c")

    small_local = [norm_w, gla_w_a_up[0], conv_w[0], conv_b[0], lru_b_a[0], lru_b_x[0], lru_lambda[0]]
    small_src = jnp.concatenate([a.reshape(-1) for a in small_local]).reshape(32, 128)
    first = {0: small_src, 1: w_in_even[0].T.astype(BF16), 2: w_out_even[0].astype(BF16)}
    sems0, srcs0, lands0, ag_token = _ag_start([0], first, "allgather_start_0")
    zero = ag_token[0, 0]
    later = {3: (w_mlp_up + zero).astype(BF16), 4: (w_mlp_down + zero).astype(BF16),
             5: (w_in_odd[0] + zero).astype(BF16), 6: (w_out_odd[0] + zero).astype(BF16)}
    sems1, srcs1, lands1, ag_token = _ag_start([1, 2], later, "allgather_start_1")
    ag_sems, ag_srcs = {**sems0, **sems1}, {**srcs0, **srcs1}
    ag_lands = _ag_push_own(ag_srcs, {**lands0, **lands1})

    def gathered(g, after):
        srcs_g, lands_g = _ag_wait(g, ag_sems[g], ag_srcs, ag_lands, after)
        return _ag_forward(g, srcs_g, lands_g)

    small4, w_in_e4, w_out_e = gathered(0, ag_token)
    me = 2 * lax.axis_index("x") + lax.axis_index("y")
    others = [k + (k >= me).astype(jnp.int32) for k in range(3)]
    where = jnp.stack([c_idx, me] + others).astype(jnp.int32)

    w_in_e_t = _even_rows_to_kernel(w_in_e4.reshape(3096, D))
    g_small = _split(small4.reshape(4, 32 * 128), SMALL_SHARDED_SHAPES)
    nw_full = _unshard(g_small[0], (2, 4, 1024), 2)
    wa_up = _unshard(g_small[1], (16, 256), 1)
    cw = _unshard(g_small[2], (4, 512), 1)
    cb, lba, lbx, lam = [_unshard(g, (512,), 0).reshape(1, 512) for g in g_small[3:]]
    nw = lambda layer, i: nw_full[layer, i].reshape(1, D)

    wa_pad = jnp.pad(wa_up, ((0, 128 - 16), (0, 0)))
    gla_ba = gla_b_a.reshape(1, 256)
    gla_nw = gla_norm_w.reshape(1, 512)
    fox_bpad = jnp.pad(fox_b_f.reshape(1, 8), ((0, 0), (FOX_LANE0, 128 - FOX_LANE0 - 8)))
    rbp = jnp.pad(rel_bias[0], ((0, 0), (0, REL_PAD - 257)))
    wa_bd = _block_diag(lru_w_a[0])
    wx_bd = _block_diag(lru_w_x[0])

    x0 = x[0]
    tgt = loss_target[0]

    h0 = _prenorm(x0, nw(0, 0), "prenorm_l0_mix")
    proj_e = _mm(h0, w_in_e_t, "nt", tm=2048, tn=640, name="mm_in_even")
    cat0, s_prev = _gla_fwd(proj_e, wa_pad, gla_ba, gla_nw)
    cum_r = _fox_gate_fwd(proj_e, fox_bpad)
    cum_c = cum_r[:, FOX_LANE0:FOX_LANE0 + 8].T
    cat0 = _fox_fwd(proj_e, cum_c, cat0)
    mix0 = _mm(cat0, w_out_e, "nn", tm=2048, tn=512, name="mm_out_even")
    x1, h1 = _post_pre_fwd(x0, mix0, nw(0, 1), nw(0, 2), "post_pre_l0_mix")
    w_up, w_dn = gathered(1, x1)
    a0, r0 = _mm(h1, w_up, "nn", tm=2048, tn=1024, b_layer=0, relu_pair=True, name="mm_up_l0")
    d0 = _mm(a0, w_dn, "nn", tm=1024, tn=512, b_layer=0, name="mm_down_l0")
    x2, h2 = _post_pre_fwd(x1, d0, nw(0, 3), nw(1, 0), "post_pre_l0_mlp")

    w_in_o, w_out_o = gathered(2, x2)
    proj_o = _mm(h2, w_in_o, "nn", tm=2048, tn=640, name="mm_in_odd")
    bias_q = _bias_build(rbp)
    bias = bias_q.transpose(1, 0, 2)
    kvpad = jnp.pad(proj_o[:, 512:1536], ((CA_PAD, 0), (0, 0)))
    cat1 = _ca_fwd(proj_o, kvpad, bias)
    x_in = proj_o[:, 2048:2560]
    xs = jnp.stack([_shift_down(x_in, 3 - j) for j in range(4)])
    lru_a, lru_b = _lru_pre_fwd(xs, cw, cb, wa_bd, lba, wx_bd, lbx, lam)
    hh = _lru_scan_fwd(lru_a, lru_b)
    cat1 = _lru_post_fwd(hh, proj_o, cat1)
    mix1 = _mm(cat1, w_out_o, "nn", tm=2048, tn=512, name="mm_out_odd")
    x3, h3 = _post_pre_fwd(x2, mix1, nw(1, 1), nw(1, 2), "post_pre_l1_mix")
    a1, r1 = _mm(h3, w_up, "nn", tm=2048, tn=1024, b_layer=1, relu_pair=True, name="mm_up_l1")
    d1 = _mm(a1, w_dn, "nn", tm=1024, tn=512, b_layer=1, name="mm_down_l1")
    g4, loss_part, dd1, dnw13 = _post_loss(x3, d1, nw(1, 3), tgt)
    loss = lax.psum(loss_part[0, 0], ("x", "y", "c"))

    def rs_begin(swap, after, tag):
        gb, recv = _pair_swap_wait(swap, after, tag)
        return _a2a_start(_pair_add(gb, recv, where, tag), tag)

    def rs_end(started, after, tag):
        send_sems, recv_sems, p, q, _ = started
        p, q = _a2a_wait(send_sems, recv_sems, p, q, after, tag)
        return _handover(_sum_chips(p, q, where, tag), tag)

    gba = lax.dynamic_update_slice(lax.empty((4, GA_ROWS, D), BF16), jnp.zeros((4, GA_UP - GA_GAP, D), BF16),
                                   (0, GA_GAP, 0))
    gba = _mm(a1, dd1, "tn", tm=512, tn=1024, into=(gba, 1024, GA_DN), name="mm_down_l1_dw")
    du1 = _mm(dd1, w_dn, "nt", tm=2048, tn=1024, b_layer=1, times2=r1, out_dtype=BF16, name="mm_down_l1_dx")
    gba = _mm(du1, h3, "tn", tm=512, tn=1024, into=(gba, 1024, GA_UP), name="mm_up_l1_dw")
    dh3 = _mm(du1, w_up, "nt", tm=1024, tn=512, b_layer=1, name="mm_up_l1_dx")
    g3, dmix1, dnw12, dnw11 = _pre_post_bwd(x3, nw(1, 2), dh3, g4, mix1, nw(1, 1), "pre_post_bwd_l1_mlp")
    gba = _mm(cat1, dmix1, "tn", tm=128, tn=1024, into=(gba, 256, GA_OUT_O), name="mm_out_odd_dw")
    dcat1 = _mm(dmix1, w_out_o, "nt", tm=2048, tn=512, name="mm_out_odd_dx")

    dq_c, dkpad, dvpad, dbias = _ca_bwd(proj_o, kvpad, bias, dcat1)
    g_rel = _bias_grad(jnp.pad(dbias.transpose(1, 0, 2), ((0, 0), (0, 0), (0, BIAS_W - CA_BAND))))[:, :257]
    dhh, dgate = _lru_post_bwd(hh, proj_o, dcat1)
    da_l, db_l = _lru_scan_bwd(_shift_up(lru_a, 1), _shift_down(hh, 1), dhh)
    dxs, g_cw, g_cb, g_wa_bd, g_lba, g_wx_bd, g_lbx, g_lam = _lru_pre_bwd(xs, cw, cb, wa_bd, lba, wx_bd, lbx, lam, da_l, db_l)
    dx_in = _conv_dx(jnp.stack([_shift_up(dxs[j], 3 - j) for j in range(4)]))
    dproj_o = jnp.concatenate([dq_c, dkpad[CA_PAD:], dvpad[CA_PAD:], dgate, dx_in], axis=1).astype(BF16)
    gba = _mm(dproj_o, h2, "tn", tm=128, tn=1024, into=(gba, 640, GA_IN_O), name="mm_in_odd_dw")
    swap_a = _pair_swap_start(gba, "a")
    dh2 = _mm(dproj_o, w_in_o, "nt", tm=1024, tn=512, name="mm_in_odd_dx")
    g2, dd0, dnw10, dnw03 = _pre_post_bwd(x2, nw(1, 0) + swap_a[4][0, 0], dh2, g3, d0, nw(0, 3), "pre_post_bwd_l1_mix")
    rs_a = rs_begin(swap_a, g2, "a")

    gbb = lax.empty((4, GB_ROWS, D), BF16)
    gbb = _mm(a0, dd0, "tn", tm=512, tn=1024, into=(gbb, 1024, GB_DN), name="mm_down_l0_dw")
    du0 = _mm(dd0, w_dn, "nt", tm=2048, tn=1024, b_layer=0, times2=r0, out_dtype=BF16, name="mm_down_l0_dx")
    gbb = _mm(du0, h1, "tn", tm=512, tn=1024, into=(gbb, 1024, GB_UP), name="mm_up_l0_dw")
    swap_b = _pair_swap_start(gbb, "b")
    dh1 = _mm(du0, w_up, "nt", tm=1024, tn=512, b_layer=0, name="mm_up_l0_dx")
    g1, dmix0, dnw02, dnw01 = _pre_post_bwd(x1, nw(0, 2) + (swap_b[4][0, 0] + rs_a[4][0, 0]), dh1, g2, mix0, nw(0, 1),
                                            "pre_post_bwd_l0_mlp")
    rs_b = rs_begin(swap_b, g1, "b")
    gbc = lax.empty((4, GC_ROWS, D), BF16)
    gbc = _mm(cat0, dmix0, "tn", tm=128, tn=1024, into=(gbc, 256, GC_OUT_E), name="mm_out_even_dw")
    dcat0 = _mm(dmix0, w_out_e, "nt", tm=2048, tn=512, name="mm_out_even_dx")

    dq_g, dk_g, dv_g, dr_g, daux_g, g_wa_pad, g_gla_ba, g_gla_nw = _gla_bwd(
        proj_e, s_prev, wa_pad, gla_ba, gla_nw + rs_b[4][0, 0], dcat0)
    dq_f, dk_f, dv_f, dccol = _fox_bwd(proj_e, cum_c, dcat0)
    dccol_t = jnp.pad(dccol.sum(axis=0).T, ((0, 0), (FOX_LANE0, 128 - FOX_LANE0 - 8)))
    daux, g_fox_bpad = _fox_gate_bwd(proj_e, fox_bpad, dccol_t, daux_g)
    dproj_e = jnp.concatenate([dq_g, dk_g, dv_g, dr_g, dq_f, dk_f, dv_f, daux], axis=1).astype(BF16)
    gt_in_e = _mm(dproj_e, h0, "tn", tm=640, tn=1024, out_dtype=BF16, name="mm_in_even_dw")
    dh0 = _mm(dproj_e, w_in_e_t, "nn", tm=1024, tn=512, name="mm_in_even_dx")
    grad_x, dnw00 = _norm_bwd(x0, nw(0, 0), dh0, g1, "prenorm_l0_mix_bwd")

    def rs_reduce(started, after, tag):
        send_sems, recv_sems, p, q, _ = started
        p, q = _a2a_wait(send_sems, recv_sems, p, q, after, tag)
        return _handover_start(_sum_chips(p, q, where, tag), tag)

    ho_a = rs_reduce(rs_a, grad_x, "a")
    ho_b = rs_reduce(rs_b, ho_a[3], "b")

    g_norm = jnp.stack([jnp.concatenate([dnw00, dnw01, dnw02, dnw03]), jnp.concatenate([dnw10, dnw11, dnw12, dnw13])])
    sharded = [(g_norm, 2), (g_wa_pad[:16], 1), (g_cw, 1), (g_cb[0], 0), (g_lba[0], 0), (g_lbx[0], 0), (g_lam[0], 0)]
    replicated = [g_gla_ba[0], g_gla_nw[0], g_fox_bpad[0, FOX_LANE0:FOX_LANE0 + 8], g_rel, _diag_blocks(g_wa_bd),
                  _diag_blocks(g_wx_bd)]
    small4 = jnp.concatenate([_shard_major(g, ax) for g, ax in sharded]
                             + [jnp.broadcast_to(g.reshape(1, -1), (4, g.size)) for g in replicated], axis=1)
    n_small = small4.shape[1]
    small_rows = GC_ROWS - GC_TAIL - 774
    small4 = jnp.pad(small4, ((0, 0), (0, small_rows * D - n_small))).reshape(4, small_rows, D)
    gt_rows = jnp.concatenate([gt_in_e[:1536], gt_in_e[3072:3088], gt_in_e[1536:3072], gt_in_e[3088:3096]], axis=0)
    tail = jnp.concatenate([gt_rows.reshape(4, 774, D), small4.astype(BF16)], axis=1)
    gbc = lax.dynamic_update_slice(gbc, tail, (0, GC_TAIL, 0))
    swap_c = _pair_swap_start(gbc, "c")
    rs_c = rs_begin(swap_c, swap_c[4], "c")

    red_a = _handover_wait(ho_a, rs_c[4], "a")
    red_b = _handover_wait(ho_b, red_a, "b")
    early = dict(
        w_mlp_up=_adamw_from(w_mlp_up, m_w_mlp_up, v_w_mlp_up, [(red_b, GB_UP, True), (red_a, GA_UP, True)], 256,
                             "adamw_w_mlp_up"),
        w_mlp_down=_adamw_from(w_mlp_down, m_w_mlp_down, v_w_mlp_down, [(red_b, GB_DN, False), (red_a, GA_DN, False)],
                               256, "adamw_w_mlp_down"),
        w_in_odd=_adamw_from(w_in_odd, m_w_in_odd, v_w_in_odd, [(red_a, GA_IN_O, True)], 256, "adamw_w_in_odd"),
        w_out_odd=_adamw_from(w_out_odd, m_w_out_odd, v_w_out_odd, [(red_a, GA_OUT_O, False)], 128, "adamw_w_out_odd"))
    red_c = rs_end(rs_c, early["w_out_odd"][3], "c")

    g_small = _split(red_c[GC_TAIL + 774:].reshape(-1)[:n_small], SMALL_SHARDED_SHAPES + REPL_SHAPES)
    g_of = dict(zip(["norm_w", "gla_w_a_up", "conv_w", "conv_b", "lru_b_a", "lru_b_x", "lru_lambda", "gla_b_a",
                     "gla_norm_w", "fox_b_f", "rel_bias", "lru_w_a", "lru_w_x"], g_small))
    g_of.update(w_in_even=red_c[GC_TAIL:GC_TAIL + 774])
    early["w_out_even"] = _adamw_from(w_out_even, m_w_out_even, v_w_out_even, [(red_c, GC_OUT_E, False)], 256,
                                      "adamw_w_out_even")

    names = ["norm_w", "w_in_even", "gla_w_a_up", "gla_b_a", "gla_norm_w", "fox_b_f", "w_out_even", "w_in_odd", "rel_bias",
             "conv_w", "conv_b", "lru_w_a", "lru_b_a", "lru_w_x", "lru_b_x", "lru_lambda", "w_out_odd", "w_mlp_up",
             "w_mlp_down"]
    w_of = dict(norm_w=norm_w, w_in_even=w_in_even, gla_w_a_up=gla_w_a_up, gla_b_a=gla_b_a, gla_norm_w=gla_norm_w,
                fox_b_f=fox_b_f, w_out_even=w_out_even, w_in_odd=w_in_odd, rel_bias=rel_bias, conv_w=conv_w, conv_b=conv_b,
                lru_w_a=lru_w_a, lru_b_a=lru_b_a, lru_w_x=lru_w_x, lru_b_x=lru_b_x, lru_lambda=lru_lambda,
                w_out_odd=w_out_odd, w_mlp_up=w_mlp_up, w_mlp_down=w_mlp_down)
    m_of = dict(norm_w=m_norm_w, w_in_even=m_w_in_even, gla_w_a_up=m_gla_w_a_up, gla_b_a=m_gla_b_a,
                gla_norm_w=m_gla_norm_w, fox_b_f=m_fox_b_f, w_out_even=m_w_out_even, w_in_odd=m_w_in_odd,
                rel_bias=m_rel_bias, conv_w=m_conv_w, conv_b=m_conv_b, lru_w_a=m_lru_w_a, lru_b_a=m_lru_b_a,
                lru_w_x=m_lru_w_x, lru_b_x=m_lru_b_x, lru_lambda=m_lru_lambda, w_out_odd=m_w_out_odd,
                w_mlp_up=m_w_mlp_up, w_mlp_down=m_w_mlp_down)
    v_of = dict(norm_w=v_norm_w, w_in_even=v_w_in_even, gla_w_a_up=v_gla_w_a_up, gla_b_a=v_gla_b_a,
                gla_norm_w=v_gla_norm_w, fox_b_f=v_fox_b_f, w_out_even=v_w_out_even, w_in_odd=v_w_in_odd,
                rel_bias=v_rel_bias, conv_w=v_conv_w, conv_b=v_conv_b, lru_w_a=v_lru_w_a, lru_b_a=v_lru_b_a,
                lru_w_x=v_lru_w_x, lru_b_x=v_lru_b_x, lru_lambda=v_lru_lambda, w_out_odd=v_w_out_odd,
                w_mlp_up=v_w_mlp_up, w_mlp_down=v_w_mlp_down)
    grads, deltas, new_ms, new_vs = [], [], [], []
    for n in names:
        w = w_of[n]
        if n in early:
            g, d, mn, vn = early[n]
            grads.append(g)
            deltas.append(d)
            new_ms.append(mn)
            new_vs.append(vn)
            continue
        if n == "w_in_even":
            to_view = lambda a: a[0].T
            from_view = lambda a: a.T[None]
        else:
            view = w.shape if w.ndim <= 3 else w.shape[-3:]
            to_view = lambda a, view=view: a.reshape(view)
            from_view = lambda a, w=w: a.reshape(w.shape)
        g = g_of[n] if n == "w_in_even" else to_view(g_of[n])
        d, mn, vn = _adamw(to_view(w), g, to_view(m_of[n]), to_view(v_of[n]), "adamw_" + n)
        grads.append(from_view(g))
        deltas.append(from_view(d))
        new_ms.append(from_view(mn))
        new_vs.append(from_view(vn))

    return (loss, grad_x.reshape(1, T, D), *grads, *deltas, *new_ms, *new_vs)
```

```python
import functools

import jax
import jax.numpy as jnp
from jax import lax
from jax.experimental import pallas as pl
from jax.experimental.pallas import tpu as pltpu

F32 = jnp.float32
BF16 = jnp.bfloat16
MESH = pl.DeviceIdType.MESH

T = 2048
D = 1024
DFF = 4096
EPS = 1e-6
CHUNK = 64
NCHUNK = T // CHUNK
PE = 3200
PO = 2560
AUX_BLK = 3072 // 128
FOX_LANE0 = 16
GLA_SCALE = 64 ** -0.5
ATT_SCALE = 64 ** -0.5
NEG = float(jnp.finfo(jnp.float32).min)
CA_BAND = 576
CA_PAD = 512
REL_PAD = 384

VMEM_LIMIT = 48 * 1024 * 1024

ADAM_LR, ADAM_B1, ADAM_B2, ADAM_EPS, ADAM_WD, ADAM_STEP = 0.001, 0.9, 0.999, 1e-08, 0.01, 10

GA_ROWS, GA_IN_O, GA_OUT_O, GA_GAP, GA_UP, GA_DN = 3072, 0, 640, 896, 1024, 2048
GB_ROWS, GB_UP, GB_DN = 2048, 0, 1024
GC_ROWS, GC_OUT_E, GC_TAIL = 1152, 0, 256

_DIMS = {"nn": (((1,), (0,)), ((), ())), "nt": (((1,), (1,)), ((), ())), "tn": (((0,), (0,)), ((), ()))}


def _cp(sem, **kw):
    return pltpu.CompilerParams(dimension_semantics=sem, vmem_limit_bytes=VMEM_LIMIT, **kw)


def _dot(a, b, mode):
    return lax.dot_general(a.astype(BF16), b.astype(BF16), _DIMS[mode], preferred_element_type=F32)


@functools.partial(jax.custom_vjp, nondiff_argnums=(2,))
def bdot(a, b, mode):
    return _dot(a, b, mode)


def _bdot_fwd(a, b, mode):
    return _dot(a, b, mode), (a, b)


def _bdot_bwd(mode, res, g):
    a, b = res
    if mode == "nn":
        da, db = _dot(g, b, "nt"), _dot(a, g, "tn")
    elif mode == "nt":
        da, db = _dot(g, b, "nn"), _dot(g, a, "tn")
    else:
        da, db = _dot(b, g, "nt"), _dot(a, g, "nn")
    return da.astype(a.dtype), db.astype(b.dtype)


bdot.defvjp(_bdot_fwd, _bdot_bwd)


def _hdot_raw(a, b, mode):
    return lax.dot_general(a, b, _DIMS[mode], precision=lax.Precision.HIGHEST, preferred_element_type=F32)


def _log_sigmoid(x):
    return jnp.minimum(x, 0.0) - jnp.log(1.0 + jnp.exp(-jnp.abs(x)))


def _sigmoid(x):
    return 1.0 / (1.0 + jnp.exp(-x))


def _expm1(x):
    series = x * (1.0 + x * 0.5 * (1.0 + x * (1.0 / 3.0) * (1.0 + x * 0.25)))
    return jnp.where(jnp.abs(x) < 0.03, series, jnp.exp(x) - 1.0)


def _gelu_tanh(x):
    return 0.5 * x * (1.0 + jnp.tanh(0.7978845608028654 * (x + 0.044715 * x * x * x)))


def _iota(shape, dim):
    return lax.broadcasted_iota(jnp.int32, shape, dim)


def _mm(a, b, mode, *, tm, tn, tk=None, out_dtype=F32, name, b_layer=None, into=None, relu_pair=False, times2=None):
    b2 = b.shape[-2:]
    if mode == "nn":
        (m, k), n = a.shape, b2[1]
    elif mode == "nt":
        (m, k), n = a.shape, b2[0]
    else:
        (k, m), n = a.shape, b2[1]
    tk = k if tk is None else tk
    assert m % tm == 0 and n % tn == 0 and k % tk == 0, (name, a.shape, b.shape)
    nk = k // tk
    if mode == "tn":
        a_spec = pl.BlockSpec((tk, tm), lambda i, j, kk: (kk, i))
    elif m == tm and nk == 1:
        a_spec = pl.BlockSpec((tm, tk), lambda i, j, kk: (i, kk), pipeline_mode=pl.Buffered(1))
    else:
        a_spec = pl.BlockSpec((tm, tk), lambda i, j, kk: (i, kk))
    b_blk = {"nn": (tk, tn), "nt": (tn, tk), "tn": (tk, tn)}[mode]
    b_idx = {"nn": lambda i, j, kk: (kk, j), "nt": lambda i, j, kk: (j, kk), "tn": lambda i, j, kk: (kk, j)}[mode]
    if b_layer is None:
        b_spec = pl.BlockSpec(b_blk, b_idx)
    else:
        b_spec = pl.BlockSpec((None,) + b_blk, lambda i, j, kk: (b_layer,) + b_idx(i, j, kk))

    tile = pl.BlockSpec((tm, tn), lambda i, j, kk: (i, j))
    if into is not None:
        buf, per_slot, row_off = into
        assert m == 4 * per_slot and per_slot % tm == 0 and row_off % tm == 0 and buf.shape[2] == n, (name, buf.shape)
        bps = per_slot // tm
        out_specs = pl.BlockSpec((None, tm, tn), lambda i, j, kk: (i // bps, row_off // tm + i % bps, j))
        out_shape = jax.ShapeDtypeStruct(buf.shape, buf.dtype)
        extra_in, extra_specs, aliases = [buf], [pl.BlockSpec(memory_space=pl.ANY)], {2: 0}
        finish = lambda acc, extra: [acc.astype(buf.dtype)]
    elif relu_pair:
        out_specs = (tile, tile)
        out_shape = (jax.ShapeDtypeStruct((m, n), BF16),) * 2
        extra_in, extra_specs, aliases = [], [], {}

        def finish(acc, extra):
            r = jnp.maximum(acc, 0.0)
            return [(r * r).astype(BF16), r.astype(BF16)]
    elif times2 is not None:
        out_specs = tile
        out_shape = jax.ShapeDtypeStruct((m, n), out_dtype)
        extra_in, extra_specs, aliases = [times2], [tile], {}
        finish = lambda acc, extra: [(acc * (2.0 * extra[...].astype(F32))).astype(out_dtype)]
    else:
        out_specs = tile
        out_shape = jax.ShapeDtypeStruct((m, n), out_dtype)
        extra_in, extra_specs, aliases = [], [], {}
        finish = lambda acc, extra: [acc.astype(out_dtype)]
    n_out = 2 if relu_pair else 1

    def body(*refs):
        a_ref, b_ref = refs[0], refs[1]
        extra = refs[2] if extra_in else None
        o_refs = refs[2 + len(extra_in):2 + len(extra_in) + n_out]

        def store(acc):
            for o_ref, val in zip(o_refs, finish(acc, extra)):
                o_ref[...] = val

        if nk == 1:
            store(_dot(a_ref[...], b_ref[...], mode))
            return
        acc_ref = refs[-1]
        kk = pl.program_id(2)

        @pl.when(kk == 0)
        def _():
            acc_ref[...] = jnp.zeros_like(acc_ref)

        acc_ref[...] += _dot(a_ref[...], b_ref[...], mode)

        @pl.when(kk == nk - 1)
        def _():
            store(acc_ref[...])

    return pl.pallas_call(
        body, name=name, grid=(m // tm, n // tn, nk),
        in_specs=[a_spec, b_spec] + extra_specs,
        out_specs=out_specs, out_shape=out_shape,
        scratch_shapes=[pltpu.VMEM((tm, tn), F32)] if nk > 1 else [],
        input_output_aliases=aliases,
        compiler_params=_cp(("parallel", "parallel", "arbitrary")),
    )(a, b, *extra_in)


ROWS = 512


def _prenorm(x, w, name):
    def body(x_ref, w_ref, o_ref):
        xv = x_ref[...]
        r = lax.rsqrt(jnp.mean(xv * xv, axis=-1, keepdims=True) + EPS)
        o_ref[...] = (xv * r * w_ref[...]).astype(BF16)

    return pl.pallas_call(
        body, name=name, grid=(T // ROWS,),
        in_specs=[pl.BlockSpec((ROWS, D), lambda i: (i, 0)), pl.BlockSpec((1, D), lambda i: (0, 0))],
        out_specs=pl.BlockSpec((ROWS, D), lambda i: (i, 0)),
        out_shape=jax.ShapeDtypeStruct((T, D), BF16),
        compiler_params=_cp(("parallel",)),
    )(x, w)


def _rms(z):
    return lax.rsqrt(jnp.mean(z * z, axis=-1, keepdims=True) + EPS)


def _rms_bwd(z, w, dy):
    r = _rms(z)
    wdy = dy * w
    dz = r * wdy - z * (r * r * r) * jnp.mean(z * wdy, axis=-1, keepdims=True)
    return dz, jnp.sum(dy * z * r, axis=0, keepdims=True)


_ROW = pl.BlockSpec((ROWS, D), lambda i: (i, 0))
_VEC = pl.BlockSpec((1, D), lambda i: (0, 0))


def _post_pre_fwd(x, z, w_post, w_pre, name):
    def body(x_ref, z_ref, wp_ref, wn_ref, x_out, h_out):
        zv = z_ref[...]
        xn = x_ref[...] + zv * _rms(zv) * wp_ref[...]
        x_out[...] = xn
        h_out[...] = (xn * _rms(xn) * wn_ref[...]).astype(BF16)

    return pl.pallas_call(
        body, name=name, grid=(T // ROWS,), in_specs=[_ROW, _ROW, _VEC, _VEC], out_specs=(_ROW, _ROW),
        out_shape=(jax.ShapeDtypeStruct((T, D), F32), jax.ShapeDtypeStruct((T, D), BF16)),
        compiler_params=_cp(("parallel",)),
    )(x, z, w_post, w_pre)


def _post_loss(x, z, w_post, tgt):
    def body(x_ref, z_ref, w_ref, t_ref, g_ref, l_ref, dz_ref, dw_ref):
        @pl.when(pl.program_id(0) == 0)
        def _():
            l_ref[...] = jnp.zeros_like(l_ref)
            dw_ref[...] = jnp.zeros_like(dw_ref)

        zv = z_ref[...]
        e = x_ref[...] + zv * _rms(zv) * w_ref[...] - t_ref[...]
        g = e * (1.0 / D)
        g_ref[...] = g
        l_ref[...] += jnp.sum(e * e) * (0.5 / D)
        dz, dw = _rms_bwd(zv, w_ref[...], g)
        dz_ref[...] = dz.astype(BF16)
        dw_ref[...] += dw

    return pl.pallas_call(
        body, name="postnorm_loss", grid=(T // ROWS,), in_specs=[_ROW, _ROW, _VEC, _ROW],
        out_specs=(_ROW, pl.BlockSpec((1, 128), lambda i: (0, 0)), _ROW, _VEC),
        out_shape=(jax.ShapeDtypeStruct((T, D), F32), jax.ShapeDtypeStruct((1, 128), F32),
                   jax.ShapeDtypeStruct((T, D), BF16), jax.ShapeDtypeStruct((1, D), F32)),
        compiler_params=_cp(("arbitrary",)),
    )(x, z, w_post, tgt)


def _pre_post_bwd(x, w_pre, dh, add, z, w_post, name):
    def body(x_ref, wn_ref, dh_ref, add_ref, z_ref, wp_ref, g_ref, dz_ref, dwn_ref, dwp_ref):
        @pl.when(pl.program_id(0) == 0)
        def _():
            dwn_ref[...] = jnp.zeros_like(dwn_ref)
            dwp_ref[...] = jnp.zeros_like(dwp_ref)

        dx, dwn = _rms_bwd(x_ref[...], wn_ref[...], dh_ref[...])
        g = dx + add_ref[...]
        g_ref[...] = g
        dz, dwp = _rms_bwd(z_ref[...], wp_ref[...], g)
        dz_ref[...] = dz.astype(BF16)
        dwn_ref[...] += dwn
        dwp_ref[...] += dwp

    return pl.pallas_call(
        body, name=name, grid=(T // ROWS,), in_specs=[_ROW, _VEC, _ROW, _ROW, _ROW, _VEC],
        out_specs=(_ROW, _ROW, _VEC, _VEC),
        out_shape=(jax.ShapeDtypeStruct((T, D), F32), jax.ShapeDtypeStruct((T, D), BF16),
                   jax.ShapeDtypeStruct((1, D), F32), jax.ShapeDtypeStruct((1, D), F32)),
        compiler_params=_cp(("arbitrary",)),
    )(x, w_pre, dh, add, z, w_post)


def _norm_bwd(z, w, dy, add, name):
    has_add = add is not None

    def body(*refs):
        if has_add:
            z_ref, w_ref, dy_ref, add_ref, dz_ref, dw_ref = refs
        else:
            z_ref, w_ref, dy_ref, dz_ref, dw_ref = refs
        i = pl.program_id(0)

        @pl.when(i == 0)
        def _():
            dw_ref[...] = jnp.zeros_like(dw_ref)

        zv = z_ref[...].astype(F32)
        dyv = dy_ref[...]
        r = lax.rsqrt(jnp.mean(zv * zv, axis=-1, keepdims=True) + EPS)
        wdy = dyv * w_ref[...]
        dz = r * wdy - zv * (r * r * r) * jnp.mean(zv * wdy, axis=-1, keepdims=True)
        if has_add:
            dz = dz + add_ref[...]
        dz_ref[...] = dz.astype(dz_ref.dtype)
        dw_ref[...] += jnp.sum(dyv * zv * r, axis=0, keepdims=True)

    row = pl.BlockSpec((ROWS, D), lambda i: (i, 0))
    vec = pl.BlockSpec((1, D), lambda i: (0, 0))
    ins = [z, w, dy] + ([add] if has_add else [])
    dz_dtype = F32 if has_add else BF16
    return pl.pallas_call(
        body, name=name, grid=(T // ROWS,),
        in_specs=[row, vec, row] + ([row] if has_add else []),
        out_specs=(row, vec),
        out_shape=(jax.ShapeDtypeStruct((T, D), dz_dtype), jax.ShapeDtypeStruct((1, D), F32)),
        compiler_params=_cp(("arbitrary",)),
    )(*ins)


def _adamw_math(w, g, m, v):
    c1 = 1.0 - ADAM_B1 ** ADAM_STEP
    c2 = 1.0 - ADAM_B2 ** ADAM_STEP
    mn = ADAM_B1 * m + (1.0 - ADAM_B1) * g
    vn = ADAM_B2 * v + (1.0 - ADAM_B2) * (g * g)
    return -ADAM_LR * ((mn / c1) / (jnp.sqrt(vn / c2) + ADAM_EPS) + ADAM_WD * w), mn, vn


def _adamw_from(w, m, v, sources, tr, name):
    layers, rows, cols = w.shape
    assert len(sources) == layers and rows % tr == 0, (name, w.shape)
    g_specs = []
    for layer, (buf, row0, transposed) in enumerate(sources):
        step = lambda l, i, layer=layer: jnp.where(l == layer, i, 0)
        if transposed:
            assert row0 % cols == 0 and buf.shape[1] == rows, (name, row0)
            g_specs.append(pl.BlockSpec((cols, tr), lambda l, i, b=row0 // cols, step=step: (b, step(l, i))))
        else:
            assert row0 % tr == 0 and buf.shape[1] == cols, (name, row0)
            g_specs.append(pl.BlockSpec((tr, cols), lambda l, i, b=row0 // tr, step=step: (b + step(l, i), 0)))

    def body(*refs):
        w_ref, m_ref, v_ref = refs[:3]
        g_refs = refs[3:3 + layers]
        g_out, d_ref, mo_ref, vo_ref = refs[3 + layers:]
        gs = [r[...].T if src[2] else r[...] for r, src in zip(g_refs, sources)]
        g = gs[0] if layers == 1 else jnp.where(pl.program_id(0) == 0, gs[0], gs[1])
        g_out[...] = g
        d_ref[...], mo_ref[...], vo_ref[...] = _adamw_math(w_ref[...], g, m_ref[...], v_ref[...])

    blk = pl.BlockSpec((None, tr, cols), lambda l, i: (l, i, 0))
    sds = jax.ShapeDtypeStruct(w.shape, F32)
    return pl.pallas_call(body, name=name, grid=(layers, rows // tr), in_specs=[blk] * 3 + g_specs,
                          out_specs=(blk,) * 4, out_shape=(sds,) * 4,
                          compiler_params=_cp(("parallel", "parallel")))(w, m, v, *[s[0] for s in sources])


def _adamw(w, g, m, v, name):
    lead = w.shape[:-2]
    assert len(lead) <= 1 and g.shape == w.shape, (name, w.shape, g.shape)
    rows, cols = w.shape[-2:]
    if rows <= 512:
        tr, tc = rows, cols
    elif rows % 256 == 0:
        tr, tc = 256, cols
    else:
        tr, tc = rows, 256
    assert rows % tr == 0 and cols % tc == 0, (name, w.shape)
    c1 = 1.0 - ADAM_B1 ** ADAM_STEP
    c2 = 1.0 - ADAM_B2 ** ADAM_STEP

    def body(w_ref, g_ref, m_ref, v_ref, d_ref, mo_ref, vo_ref):
        gv = g_ref[...]
        mn = ADAM_B1 * m_ref[...] + (1.0 - ADAM_B1) * gv
        vn = ADAM_B2 * v_ref[...] + (1.0 - ADAM_B2) * (gv * gv)
        m_hat = mn / c1
        v_hat = vn / c2
        d_ref[...] = -ADAM_LR * (m_hat / (jnp.sqrt(v_hat) + ADAM_EPS) + ADAM_WD * w_ref[...])
        mo_ref[...] = mn
        vo_ref[...] = vn

    if lead:
        grid = (lead[0], rows // tr, cols // tc)
        blk = pl.BlockSpec((None, tr, tc), lambda l, i, j: (l, i, j))
    else:
        grid = (rows // tr, cols // tc)
        blk = pl.BlockSpec((tr, tc), lambda i, j: (i, j))
    sds = jax.ShapeDtypeStruct(w.shape, F32)
    return pl.pallas_call(body, name=name, grid=grid, in_specs=[blk] * 4, out_specs=(blk,) * 3,
                          out_shape=(sds,) * 3, compiler_params=_cp(("parallel",) * len(grid)))(w, g, m, v)


def _running_sum(x, towards_later):
    n = x.shape[0]
    row = _iota(x.shape, 0)
    s = 1
    while s < n:
        if towards_later:
            x = x + jnp.where(row >= s, pltpu.roll(x, s, 0), 0.0)
        else:
            x = x + jnp.where(row < n - s, pltpu.roll(x, n - s, 0), 0.0)
        s *= 2
    return x


@jax.custom_vjp
def _cumsum_rows(x):
    return _running_sum(x, True)


_cumsum_rows.defvjp(lambda x: (_running_sum(x, True), None), lambda _, g: (_running_sum(g, False),))


def _gla_consts():
    return (_iota((256, 512), 0) // 64 == _iota((256, 512), 1) // 128).astype(F32)


def _gla_chunk(mask, q, k, v, r, aux, s_prev, wa, ba, nw):
    la = _log_sigmoid(bdot(aux, wa, "nn") + ba) * (1.0 / 16.0)
    cum = _cumsum_rows(la)
    total = jnp.sum(la, axis=0, keepdims=True)
    k_dec = k * jnp.exp(total - cum)
    inc = bdot(k_dec, v, "tn") * mask
    dec = jnp.exp(jnp.broadcast_to(total, (128, 256)).T)
    dec = jnp.concatenate([dec, dec, dec, dec], axis=1)
    s_new = dec * s_prev + inc
    o = bdot(q * GLA_SCALE, s_new, "nn")
    parts = []
    for h in range(4):
        oh = o[:, h * 128:(h + 1) * 128]
        parts.append(oh * lax.rsqrt(jnp.mean(oh * oh, axis=-1, keepdims=True) + EPS))
    on = jnp.concatenate(parts, axis=1)
    return s_new, on * nw * (r * _sigmoid(r))


GLA_PER_STEP = 4
GLA_ROWS = GLA_PER_STEP * CHUNK
GLA_STEPS = NCHUNK // GLA_PER_STEP


def _gla_specs(cmap):
    return [pl.BlockSpec((GLA_ROWS, 256), lambda c: (cmap(c), 0)),
            pl.BlockSpec((GLA_ROWS, 256), lambda c: (cmap(c), 1)),
            pl.BlockSpec((GLA_ROWS, 512), lambda c: (cmap(c), 1)),
            pl.BlockSpec((GLA_ROWS, 512), lambda c: (cmap(c), 2)),
            pl.BlockSpec((GLA_ROWS, 128), lambda c: (cmap(c), AUX_BLK))]


def _gla_fwd(proj, wa, ba, nw):
    def body(q_ref, k_ref, v_ref, r_ref, aux_ref, wa_ref, ba_ref, nw_ref, o_ref, sp_ref, s_ref):
        @pl.when(pl.program_id(0) == 0)
        def _():
            s_ref[...] = jnp.zeros_like(s_ref)

        s = s_ref[...]
        consts = _gla_consts()
        outs, states = [], []
        for i in range(GLA_PER_STEP):
            rows = slice(i * CHUNK, (i + 1) * CHUNK)
            states.append(s)
            s, out = _gla_chunk(consts, q_ref[rows, :], k_ref[rows, :], v_ref[rows, :], r_ref[rows, :], aux_ref[rows, :],
                                s, wa_ref[...], ba_ref[...], nw_ref[...])
            outs.append(out)
        s_ref[...] = s
        for i in range(GLA_PER_STEP):
            o_ref[i * CHUNK:(i + 1) * CHUNK, :] = outs[i]
            sp_ref[i] = states[i]

    full = lambda shape: pl.BlockSpec(shape, lambda c: (0,) * len(shape))
    return pl.pallas_call(
        body, name="gla_fwd", grid=(GLA_STEPS,),
        in_specs=_gla_specs(lambda c: c) + [full((128, 256)), full((1, 256)), full((1, 512))],
        out_specs=(pl.BlockSpec((GLA_ROWS, 512), lambda c: (c, 0)),
                   pl.BlockSpec((GLA_PER_STEP, 256, 512), lambda c: (c, 0, 0))),
        out_shape=(jax.ShapeDtypeStruct((T, D), F32), jax.ShapeDtypeStruct((NCHUNK, 256, 512), F32)),
        scratch_shapes=[pltpu.VMEM((256, 512), F32)],
        compiler_params=_cp(("arbitrary",)),
    )(proj, proj, proj, proj, proj, wa, ba, nw)


def _gla_bwd(proj, s_prev_all, wa, ba, nw, dcat):
    rev = lambda c: GLA_STEPS - 1 - c

    def body(q_ref, k_ref, v_ref, r_ref, aux_ref, sp_ref, wa_ref, ba_ref, nw_ref, do_ref,
             dq_ref, dk_ref, dv_ref, dr_ref, daux_ref, dwa_ref, dba_ref, dnw_ref, ds_ref):
        @pl.when(pl.program_id(0) == 0)
        def _():
            ds_ref[...] = jnp.zeros_like(ds_ref)
            dwa_ref[...] = jnp.zeros_like(dwa_ref)
            dba_ref[...] = jnp.zeros_like(dba_ref)
            dnw_ref[...] = jnp.zeros_like(dnw_ref)

        fn = functools.partial(_gla_chunk, _gla_consts())
        ds = ds_ref[...]
        dwa, dba, dnw = dwa_ref[...], dba_ref[...], dnw_ref[...]
        grads = {}
        for i in reversed(range(GLA_PER_STEP)):
            rows = slice(i * CHUNK, (i + 1) * CHUNK)
            _, vjp = jax.vjp(fn, q_ref[rows, :], k_ref[rows, :], v_ref[rows, :], r_ref[rows, :], aux_ref[rows, :],
                             sp_ref[i], wa_ref[...], ba_ref[...], nw_ref[...])
            *grads[i], ds, dwa_i, dba_i, dnw_i = vjp((ds, do_ref[rows, :]))
            dwa, dba, dnw = dwa + dwa_i, dba + dba_i, dnw + dnw_i
        ds_ref[...] = ds
        dwa_ref[...] = dwa
        dba_ref[...] = dba
        dnw_ref[...] = dnw
        for i in range(GLA_PER_STEP):
            rows = slice(i * CHUNK, (i + 1) * CHUNK)
            for ref, g in zip((dq_ref, dk_ref, dv_ref, dr_ref, daux_ref), grads[i]):
                ref[rows, :] = g

    full = lambda shape: pl.BlockSpec(shape, lambda c: (0,) * len(shape))
    blk = lambda w: pl.BlockSpec((GLA_ROWS, w), lambda c: (rev(c), 0))
    sds = lambda *s: jax.ShapeDtypeStruct(s, F32)
    return pl.pallas_call(
        body, name="gla_bwd", grid=(GLA_STEPS,),
        in_specs=_gla_specs(rev) + [pl.BlockSpec((GLA_PER_STEP, 256, 512), lambda c: (rev(c), 0, 0)),
                                    full((128, 256)), full((1, 256)), full((1, 512)), blk(512)],
        out_specs=(blk(256), blk(256), blk(512), blk(512), blk(128), full((128, 256)), full((1, 256)), full((1, 512))),
        out_shape=(sds(T, 256), sds(T, 256), sds(T, 512), sds(T, 512), sds(T, 128),
                   sds(128, 256), sds(1, 256), sds(1, 512)),
        scratch_shapes=[pltpu.VMEM((256, 512), F32)],
        compiler_params=_cp(("arbitrary",)),
    )(proj, proj, proj, proj, proj, s_prev_all, wa, ba, nw, dcat)


def _prefix8(x, towards_later):
    row = _iota(x.shape, 0)
    for s in (1, 2, 4):
        if towards_later:
            keep, shift = row >= s, s
        else:
            keep, shift = row < 8 - s, 8 - s
        x = x + jnp.where(keep, pltpu.roll(x, shift, 0), 0.0)
    return x


def _fox_gate_fwd(proj, bpad):
    def body(aux_ref, b_ref, cum_ref):
        cum_ref[...] = _log_sigmoid(aux_ref[...] + b_ref[...])

        def step(i, carry):
            rows = pl.ds(pl.multiple_of(i * 8, 8), 8)
            cum = _prefix8(cum_ref[rows, :], True) + carry
            cum_ref[rows, :] = cum
            return jnp.broadcast_to(cum[7:, :], (8, 128))

        lax.fori_loop(0, T // 8, step, jnp.zeros((8, 128), F32), unroll=4)

    return pl.pallas_call(
        body, name="fox_gate_fwd", grid=(1,),
        in_specs=[pl.BlockSpec((T, 128), lambda i: (0, AUX_BLK)), pl.BlockSpec((1, 128), lambda i: (0, 0))],
        out_specs=pl.BlockSpec((T, 128), lambda i: (0, 0)),
        out_shape=jax.ShapeDtypeStruct((T, 128), F32),
        compiler_params=_cp(("arbitrary",)),
    )(proj, bpad)


def _fox_gate_bwd(proj, bpad, dccol_t, daux_gla):
    def body(aux_ref, b_ref, dc_ref, dg_ref, daux_ref, db_ref):
        def step(i, carry):
            rows = pl.ds(pl.multiple_of(T - 8 * (i + 1), 8), 8)
            dlf = _prefix8(dc_ref[rows, :], False) + carry
            daux_ref[rows, :] = dlf
            return jnp.broadcast_to(dlf[:1, :], (8, 128))

        lax.fori_loop(0, T // 8, step, jnp.zeros((8, 128), F32), unroll=4)
        dz = daux_ref[...] * _sigmoid(-(aux_ref[...] + b_ref[...]))
        daux_ref[...] = dz + dg_ref[...]
        db_ref[...] = jnp.sum(dz, axis=0, keepdims=True)

    whole = pl.BlockSpec((T, 128), lambda i: (0, 0))
    vec = pl.BlockSpec((1, 128), lambda i: (0, 0))
    return pl.pallas_call(
        body, name="fox_gate_bwd", grid=(1,),
        in_specs=[pl.BlockSpec((T, 128), lambda i: (0, AUX_BLK)), vec, whole, whole],
        out_specs=(whole, vec),
        out_shape=(jax.ShapeDtypeStruct((T, 128), F32), jax.ShapeDtypeStruct((1, 128), F32)),
        compiler_params=_cp(("arbitrary",)),
    )(proj, bpad, dccol_t, daux_gla)


FOX_Q = 256
FOX_QB = T // FOX_Q
FOX_QF = 512


@jax.custom_vjp
def _attend(s, v):
    return _attend_fwd(s, v)[0]


def _attend_fwd(s, v):
    e = jnp.exp(s - jnp.max(s, axis=-1, keepdims=True))
    r = 1.0 / jnp.sum(e, axis=-1, keepdims=True)
    return _dot(e, v, "nn") * r, (e, r, v)


def _attend_bwd(res, do):
    e, r, v = res
    do_r = do * r
    dpr = _dot(do_r, v, "nt")
    ds = e * (dpr - r * jnp.sum(e * dpr, axis=-1, keepdims=True))
    return ds, _dot(e, do_r, "tn").astype(v.dtype)


_attend.defvjp(_attend_fwd, _attend_bwd)


def _fox_block(hp, q, k, v, ccol):
    fq, kl = q.shape[0], k.shape[0]
    lane = _iota((fq, 128), 1)
    tri = jnp.bitwise_and(_iota((2 * fq, fq), 0), fq - 1) >= _iota((2 * fq, fq), 1)
    sub = _iota((8, kl), 0)
    qs = q * ATT_SCALE
    q2 = jnp.concatenate([jnp.where(lane < 64, qs, 0.0), jnp.where(lane >= 64, qs, 0.0)], axis=0)
    s = bdot(q2, k, "nt")
    cs = [jnp.sum(jnp.where(sub == 2 * hp + e, ccol, 0.0), axis=0, keepdims=True) for e in range(2)]
    s = jnp.concatenate([s[:fq] - cs[0], s[fq:] - cs[1]], axis=0)
    diag = jnp.where(tri, s[:, kl - fq:], NEG)
    s = diag if kl == fq else jnp.concatenate([s[:, :kl - fq], diag], axis=1)
    o2 = _attend(s, v)
    return jnp.where(lane < 64, o2[:fq], o2[fq:])


def _fox_in_specs(fq):
    return [pl.BlockSpec((fq, 128), lambda hp, qb: (qb, 12 + hp)),
            pl.BlockSpec((T, 128), lambda hp, qb: (0, 16 + hp)),
            pl.BlockSpec((T, 128), lambda hp, qb: (0, 20 + hp)),
            pl.BlockSpec((8, T), lambda hp, qb: (0, 0))]


def _fox_fwd(proj, cum_c, cat):
    def body(q_ref, k_ref, v_ref, cc_ref, cat_ref, o_ref):
        qb = pl.program_id(1)
        for g in range(T // FOX_QF):
            kl = FOX_QF * (g + 1)

            @pl.when(qb == g)
            def _(kl=kl):
                o_ref[...] = _fox_block(pl.program_id(0), q_ref[...], k_ref[0:kl, :], v_ref[0:kl, :], cc_ref[:, 0:kl])

    return pl.pallas_call(
        body, name="fox_fwd", grid=(4, T // FOX_QF),
        in_specs=_fox_in_specs(FOX_QF) + [pl.BlockSpec(memory_space=pl.ANY)],
        out_specs=pl.BlockSpec((FOX_QF, 128), lambda hp, qb: (qb, 4 + hp)),
        out_shape=jax.ShapeDtypeStruct((T, D), F32), input_output_aliases={4: 0},
        compiler_params=_cp(("parallel", "parallel")),
    )(proj, proj, proj, cum_c, cat)


def _fox_bwd(proj, cum_c, dcat):
    def body(q_ref, k_ref, v_ref, cc_ref, do_ref, dq_ref, dk_ref, dv_ref, dcc_ref):
        qb = pl.program_id(1)

        @pl.when(qb == 0)
        def _():
            dk_ref[...] = jnp.zeros_like(dk_ref)
            dv_ref[...] = jnp.zeros_like(dv_ref)
            dcc_ref[...] = jnp.zeros_like(dcc_ref)

        fn = functools.partial(_fox_block, pl.program_id(0))
        for g in range(FOX_QB):
            kl = FOX_Q * (g + 1)

            @pl.when(qb == g)
            def _(kl=kl):
                _, vjp = jax.vjp(fn, q_ref[...], k_ref[0:kl, :], v_ref[0:kl, :], cc_ref[:, 0:kl])
                dq, dk, dv, dcc = vjp(do_ref[...])
                dq_ref[...] = dq
                dk_ref[0:kl, :] += dk
                dv_ref[0:kl, :] += dv
                dcc_ref[:, 0:kl] += dcc

    sds = lambda *s: jax.ShapeDtypeStruct(s, F32)
    return pl.pallas_call(
        body, name="fox_bwd", grid=(4, FOX_QB),
        in_specs=_fox_in_specs(FOX_Q) + [pl.BlockSpec((FOX_Q, 128), lambda hp, qb: (qb, 4 + hp))],
        out_specs=(pl.BlockSpec((FOX_Q, 128), lambda hp, qb: (qb, hp)),
                   pl.BlockSpec((T, 128), lambda hp, qb: (0, hp)),
                   pl.BlockSpec((T, 128), lambda hp, qb: (0, hp)),
                   pl.BlockSpec((None, 8, T), lambda hp, qb: (hp, 0, 0))),
        out_shape=(sds(T, 512), sds(T, 512), sds(T, 512), sds(4, 8, T)),
        compiler_params=_cp(("parallel", "arbitrary")),
    )(proj, proj, proj, cum_c, dcat)


BIAS_W = 640


def _rel_onehot():
    j = _iota((REL_PAD, BIAS_W), 1)
    rel = jnp.clip(CA_PAD + CHUNK - 1 - j, -128, 128) + 128
    return (_iota((REL_PAD, BIAS_W), 0) == rel).astype(F32)


def _bias_build(rbp):
    def body(rb_ref, o_ref):
        f = _hdot_raw(rb_ref[...], _rel_onehot(), "nn")
        for q in range(CHUNK):
            o_ref[q] = pltpu.roll(f, (BIAS_W - (CHUNK - 1 - q)) % BIAS_W, 1)[:, :CA_BAND]

    return pl.pallas_call(body, name="ca_bias_build", out_shape=jax.ShapeDtypeStruct((CHUNK, 8, CA_BAND), F32))(rbp)


def _bias_grad(dbias_q):
    def body(db_ref, o_ref):
        acc = jnp.zeros((8, BIAS_W), F32)
        for q in range(CHUNK):
            acc = acc + pltpu.roll(db_ref[q], CHUNK - 1 - q, 1)
        o_ref[...] = _hdot_raw(acc, _rel_onehot(), "nt")

    return pl.pallas_call(body, name="ca_bias_grad", out_shape=jax.ShapeDtypeStruct((8, REL_PAD), F32))(dbias_q)


def _ca_block(c, masked, q, kb, vb, bias2):
    lane = _iota((CHUNK, 128), 1)
    qs = q * ATT_SCALE
    q2 = jnp.concatenate([jnp.where(lane < 64, qs, 0.0), jnp.where(lane >= 64, qs, 0.0)], axis=0)
    s = bdot(q2, kb, "nt") + bias2.reshape(2 * CHUNK, CA_BAND)
    if masked:
        s = jnp.where((c * CHUNK - CA_PAD + _iota((2 * CHUNK, CA_BAND), 1)) >= 0, s, NEG)
    o2 = _attend(s, vb)
    return jnp.where(lane < 64, o2[:CHUNK], o2[CHUNK:])


CA_PER_STEP = 16
CA_ROWS = CA_PER_STEP * CHUNK
CA_MASKED_CHUNKS = CA_PAD // CHUNK
assert CA_PER_STEP >= CA_MASKED_CHUNKS
CA_MASKED_STEPS = 1


def _ca_fwd(proj, kvpad, bias):
    def body(q_ref, k_ref, v_ref, b_ref, o_ref):
        def run(masked):
            outs = []
            for i in range(CA_PER_STEP):
                c = pl.program_id(1) * CA_PER_STEP + i
                band = pl.ds(pl.multiple_of(c * CHUNK, CHUNK), CA_BAND)
                rows = slice(i * CHUNK, (i + 1) * CHUNK)
                outs.append(_ca_block(c, masked and i < CA_MASKED_CHUNKS, q_ref[rows, :], k_ref[band, :], v_ref[band, :],
                                      b_ref[...]))
            for i in range(CA_PER_STEP):
                o_ref[i * CHUNK:(i + 1) * CHUNK, :] = outs[i]

        pl.when(pl.program_id(1) < CA_MASKED_STEPS)(lambda: run(True))
        pl.when(pl.program_id(1) >= CA_MASKED_STEPS)(lambda: run(False))

    return pl.pallas_call(
        body, name="ca_fwd", grid=(4, NCHUNK // CA_PER_STEP),
        in_specs=[pl.BlockSpec((CA_ROWS, 128), lambda hp, c: (c, hp)),
                  pl.BlockSpec((T + CA_PAD, 128), lambda hp, c: (0, hp)),
                  pl.BlockSpec((T + CA_PAD, 128), lambda hp, c: (0, 4 + hp)),
                  pl.BlockSpec((2, CHUNK, CA_BAND), lambda hp, c: (hp, 0, 0))],
        out_specs=pl.BlockSpec((CA_ROWS, 128), lambda hp, c: (c, hp)),
        out_shape=jax.ShapeDtypeStruct((T, D), F32),
        compiler_params=_cp(("parallel", "parallel")),
    )(proj, kvpad, kvpad, bias)


def _ca_bwd(proj, kvpad, bias, dcat):
    def body(q_ref, k_ref, v_ref, b_ref, do_ref, dq_ref, dk_ref, dv_ref, db_ref):
        c = pl.program_id(1)

        @pl.when(c == 0)
        def _():
            dk_ref[...] = jnp.zeros_like(dk_ref)
            dv_ref[...] = jnp.zeros_like(dv_ref)
            db_ref[...] = jnp.zeros_like(db_ref)

        def run(masked):
            grads, bands = [], []
            for i in range(CA_PER_STEP):
                ci = c * CA_PER_STEP + i
                band = pl.ds(pl.multiple_of(ci * CHUNK, CHUNK), CA_BAND)
                rows = slice(i * CHUNK, (i + 1) * CHUNK)
                fn = functools.partial(_ca_block, ci, masked and i < CA_MASKED_CHUNKS)
                _, vjp = jax.vjp(fn, q_ref[rows, :], k_ref[band, :], v_ref[band, :], b_ref[...])
                grads.append(vjp(do_ref[rows, :]))
                bands.append(band)
            for i, (dq, _, _, _) in enumerate(grads):
                dq_ref[i * CHUNK:(i + 1) * CHUNK, :] = dq
            for band, (_, dkb, dvb, _) in zip(bands, grads):
                dk_ref[band, :] += dkb
                dv_ref[band, :] += dvb
            db_ref[...] += functools.reduce(lambda a, b: a + b, [g[3] for g in grads])

        pl.when(c < CA_MASKED_STEPS)(lambda: run(True))
        pl.when(c >= CA_MASKED_STEPS)(lambda: run(False))

    sds = lambda *s: jax.ShapeDtypeStruct(s, F32)
    padded = lambda: pl.BlockSpec((T + CA_PAD, 128), lambda hp, c: (0, hp))
    return pl.pallas_call(
        body, name="ca_bwd", grid=(4, NCHUNK // CA_PER_STEP),
        in_specs=[pl.BlockSpec((CA_ROWS, 128), lambda hp, c: (c, hp)),
                  pl.BlockSpec((T + CA_PAD, 128), lambda hp, c: (0, hp)),
                  pl.BlockSpec((T + CA_PAD, 128), lambda hp, c: (0, 4 + hp)),
                  pl.BlockSpec((2, CHUNK, CA_BAND), lambda hp, c: (hp, 0, 0)),
                  pl.BlockSpec((CA_ROWS, 128), lambda hp, c: (c, hp))],
        out_specs=(pl.BlockSpec((CA_ROWS, 128), lambda hp, c: (c, hp)), padded(), padded(),
                   pl.BlockSpec((2, CHUNK, CA_BAND), lambda hp, c: (hp, 0, 0))),
        out_shape=(sds(T, 512), sds(T + CA_PAD, 512), sds(T + CA_PAD, 512), sds(8, CHUNK, CA_BAND)),
        compiler_params=_cp(("parallel", "arbitrary")),
    )(proj, kvpad, kvpad, bias, dcat)


def _block_diag_dot(x, w):
    return jnp.concatenate([bdot(x[:, :256], w[:256, :256], "nn"), bdot(x[:, 256:], w[256:, 256:], "nn")], axis=1)


def _lru_pre(xs, cw, cb, wa, ba, wx, bx, lam):
    xc = cb + xs[0] * cw[0:1, :] + xs[1] * cw[1:2, :] + xs[2] * cw[2:3, :] + xs[3] * cw[3:4, :]
    ra = _sigmoid(_block_diag_dot(xc, wa) + ba)
    ii = _sigmoid(_block_diag_dot(xc, wx) + bx)
    la = 8.0 * ra * _log_sigmoid(lam)
    return jnp.exp(la), jnp.sqrt(-_expm1(2.0 * la)) * (ii * xc)


def _lru_pre_specs():
    full = lambda shape: pl.BlockSpec(shape, lambda i: (0,) * len(shape))
    return [pl.BlockSpec((4, ROWS, 512), lambda i: (0, i, 0)), full((4, 512)), full((1, 512)),
            full((512, 512)), full((1, 512)), full((512, 512)), full((1, 512)), full((1, 512))]


def _lru_pre_fwd(xs, cw, cb, wa, ba, wx, bx, lam):
    def body(xs_ref, cw_ref, cb_ref, wa_ref, ba_ref, wx_ref, bx_ref, lam_ref, a_ref, b_ref):
        a, b = _lru_pre(xs_ref[...], cw_ref[...], cb_ref[...], wa_ref[...], ba_ref[...], wx_ref[...], bx_ref[...],
                        lam_ref[...])
        a_ref[...] = a
        b_ref[...] = b

    row = pl.BlockSpec((ROWS, 512), lambda i: (i, 0))
    sds = jax.ShapeDtypeStruct((T, 512), F32)
    return pl.pallas_call(body, name="lru_pre_fwd", grid=(T // ROWS,), in_specs=_lru_pre_specs(),
                          out_specs=(row, row), out_shape=(sds, sds), compiler_params=_cp(("parallel",)),
                          )(xs, cw, cb, wa, ba, wx, bx, lam)


def _lru_pre_bwd(xs, cw, cb, wa, ba, wx, bx, lam, da, db):
    def body(xs_ref, cw_ref, cb_ref, wa_ref, ba_ref, wx_ref, bx_ref, lam_ref, da_ref, db_ref,
             dxs_ref, dcw_ref, dcb_ref, dwa_ref, dba_ref, dwx_ref, dbx_ref, dlam_ref):
        acc = (dcw_ref, dcb_ref, dwa_ref, dba_ref, dwx_ref, dbx_ref, dlam_ref)

        @pl.when(pl.program_id(0) == 0)
        def _():
            for r in acc:
                r[...] = jnp.zeros_like(r)

        _, vjp = jax.vjp(_lru_pre, xs_ref[...], cw_ref[...], cb_ref[...], wa_ref[...], ba_ref[...], wx_ref[...],
                         bx_ref[...], lam_ref[...])
        grads = vjp((da_ref[...], db_ref[...]))
        dxs_ref[...] = grads[0]
        for r, g in zip(acc, grads[1:]):
            r[...] += g

    row = pl.BlockSpec((ROWS, 512), lambda i: (i, 0))
    specs = _lru_pre_specs()
    sds = lambda *s: jax.ShapeDtypeStruct(s, F32)
    return pl.pallas_call(
        body, name="lru_pre_bwd", grid=(T // ROWS,), in_specs=specs + [row, row], out_specs=tuple(specs),
        out_shape=(sds(4, T, 512), sds(4, 512), sds(1, 512), sds(512, 512), sds(1, 512), sds(512, 512), sds(1, 512),
                   sds(1, 512)),
        compiler_params=_cp(("arbitrary",)),
    )(xs, cw, cb, wa, ba, wx, bx, lam, da, db)


SCAN_ROWS = 8


def _scan8(a, b, towards_later):
    row = _iota((SCAN_ROWS, 512), 0)
    for s in (1, 2, 4):
        if towards_later:
            keep, shift = row >= s, s
        else:
            keep, shift = row < SCAN_ROWS - s, SCAN_ROWS - s
        a_s = jnp.where(keep, pltpu.roll(a, shift, 0), 1.0)
        b_s = jnp.where(keep, pltpu.roll(b, shift, 0), 0.0)
        b = a * b_s + b
        a = a * a_s
    return a, b


def _lru_scan_fwd(a, b):
    def body(a_ref, b_ref, h_ref):
        def step(i, carry):
            rows = pl.ds(pl.multiple_of(i * SCAN_ROWS, SCAN_ROWS), SCAN_ROWS)
            a8, b8 = _scan8(a_ref[rows, :], b_ref[rows, :], True)
            h = a8 * carry + b8
            h_ref[rows, :] = h
            return jnp.broadcast_to(h[SCAN_ROWS - 1:, :], (SCAN_ROWS, 512))

        lax.fori_loop(0, T // SCAN_ROWS, step, jnp.zeros((SCAN_ROWS, 512), F32), unroll=2)

    return pl.pallas_call(body, name="lru_scan_fwd", out_shape=jax.ShapeDtypeStruct((T, 512), F32),
                          compiler_params=pltpu.CompilerParams(vmem_limit_bytes=VMEM_LIMIT))(a, b)


def _lru_scan_bwd(a_next, h_prev, dh):
    def body(a_ref, h_ref, dh_ref, da_ref, db_ref):
        def step(i, carry):
            start = T - SCAN_ROWS * (i + 1)
            rows = pl.ds(pl.multiple_of(start, SCAN_ROWS), SCAN_ROWS)
            a8, b8 = _scan8(a_ref[rows, :], dh_ref[rows, :], False)
            g = a8 * carry + b8
            db_ref[rows, :] = g
            da_ref[rows, :] = g * h_ref[rows, :]
            return jnp.broadcast_to(g[:1, :], (SCAN_ROWS, 512))

        lax.fori_loop(0, T // SCAN_ROWS, step, jnp.zeros((SCAN_ROWS, 512), F32), unroll=2)

    sds = jax.ShapeDtypeStruct((T, 512), F32)
    return pl.pallas_call(body, name="lru_scan_bwd", out_shape=(sds, sds),
                          compiler_params=pltpu.CompilerParams(vmem_limit_bytes=VMEM_LIMIT))(a_next, h_prev, dh)


def _lru_post(h, gate):
    return h * _gelu_tanh(gate)


def _lru_post_fwd(h, proj, cat):
    def body(h_ref, g_ref, cat_ref, o_ref):
        o_ref[...] = _lru_post(h_ref[...], g_ref[...])

    row = pl.BlockSpec((ROWS, 512), lambda i: (i, 0))
    return pl.pallas_call(body, name="lru_post_fwd", grid=(T // ROWS,),
                          in_specs=[row, pl.BlockSpec((ROWS, 512), lambda i: (i, 3)), pl.BlockSpec(memory_space=pl.ANY)],
                          out_specs=pl.BlockSpec((ROWS, 512), lambda i: (i, 1)),
                          out_shape=jax.ShapeDtypeStruct((T, D), F32), input_output_aliases={2: 0},
                          compiler_params=_cp(("parallel",)))(h, proj, cat)


def _lru_post_bwd(h, proj, dcat):
    def body(h_ref, g_ref, do_ref, dh_ref, dg_ref):
        _, vjp = jax.vjp(_lru_post, h_ref[...], g_ref[...])
        dh, dg = vjp(do_ref[...])
        dh_ref[...] = dh
        dg_ref[...] = dg

    row = pl.BlockSpec((ROWS, 512), lambda i: (i, 0))
    sds = jax.ShapeDtypeStruct((T, 512), F32)
    return pl.pallas_call(body, name="lru_post_bwd", grid=(T // ROWS,),
                          in_specs=[row, pl.BlockSpec((ROWS, 512), lambda i: (i, 3)),
                                    pl.BlockSpec((ROWS, 512), lambda i: (i, 1))],
                          out_specs=(row, row), out_shape=(sds, sds), compiler_params=_cp(("parallel",)))(h, proj, dcat)


def _conv_dx(dxs_shift):
    def body(d_ref, o_ref):
        o_ref[...] = d_ref[0] + d_ref[1] + d_ref[2] + d_ref[3]

    row = pl.BlockSpec((ROWS, 512), lambda i: (i, 0))
    return pl.pallas_call(body, name="lru_conv_dx", grid=(T // ROWS,),
                          in_specs=[pl.BlockSpec((4, ROWS, 512), lambda i: (0, i, 0))], out_specs=row,
                          out_shape=jax.ShapeDtypeStruct((T, 512), F32), compiler_params=_cp(("parallel",)))(dxs_shift)


def _position():
    return lax.axis_index("x"), lax.axis_index("y"), lax.axis_index("c")


def _other_chips(x, y):
    return [(1 - x, y), (x, 1 - y), (1 - x, 1 - y)]


def _al(v, n):
    return v * n if isinstance(v, int) else pl.multiple_of(v * n, n)


_AG_ITEMS = [
    ((4, 32, 128), lambda o, s, h: o.at[s, pl.ds(_al(h, 16), 16), :], lambda r, h: r.at[pl.ds(_al(h, 16), 16), :]),
    ((4, 774, 1024), lambda o, s, h: o.at[s, :, pl.ds(_al(h, 512), 512)], lambda r, h: r.at[:, pl.ds(_al(h, 512), 512)]),
    ((1024, 1024), lambda o, s, h: o.at[pl.ds(_al(2 * s + h, 128), 128), :], lambda r, h: r.at[pl.ds(_al(h, 128), 128), :]),
    ((2, 1024, 4096), lambda o, s, h: o.at[h, :, pl.ds(_al(s, 1024), 1024)], lambda r, h: r.at[h]),
    ((2, 4096, 1024), lambda o, s, h: o.at[h, pl.ds(_al(s, 1024), 1024), :], lambda r, h: r.at[h]),
    ((1024, 2560), lambda o, s, h: o.at[pl.ds(_al(h, 512), 512), pl.ds(_al(s, 640), 640)],
     lambda r, h: r.at[pl.ds(_al(h, 512), 512), :]),
    ((1024, 1024), lambda o, s, h: o.at[pl.ds(_al(2 * s + h, 128), 128), :], lambda r, h: r.at[pl.ds(_al(h, 128), 128), :]),
]


_AG_GROUPS = [(0, 1, 2), (3, 4), (5, 6)]

_HBM = pl.BlockSpec(memory_space=pltpu.HBM)
_SEM = pl.BlockSpec(memory_space=pltpu.SEMAPHORE)
_SPLIT = dict(has_side_effects=pltpu.SideEffectType.DATAFLOW_SIDE_EFFECTING)


def _hbm(a):
    return pltpu.with_memory_space_constraint(a, pltpu.HBM)


def _ag_ici_copy(i, j, chip, c, slot, src_ref, land_ref, send_sems, recv_sems, k):
    _, dst, half = _AG_ITEMS[i]
    return pltpu.make_async_remote_copy(src_ref=half(src_ref, c), dst_ref=dst(land_ref, slot, c), send_sem=send_sems.at[k],
                                        recv_sem=recv_sems.at[k], device_id=(*chip, c), device_id_type=MESH)


def _ag_start(groups, shards, name):
    items_all = [i for g in groups for i in _AG_GROUPS[g]]
    n = len(items_all)
    ng = len(groups)
    lands = [lax.empty(_AG_ITEMS[i][0], shards[i].dtype) for i in items_all]

    def body(*refs):
        srcs, land_refs = dict(zip(items_all, refs[:n])), dict(zip(items_all, refs[n:2 * n]))
        sems = refs[2 * n:2 * n + 2 * ng]
        token = refs[-1]
        x, y, c = _position()
        me = 2 * x + y
        for gi, g in enumerate(groups):
            for t, i in enumerate(_AG_GROUPS[g]):
                for j, chip in enumerate(_other_chips(x, y)):
                    _ag_ici_copy(i, j, chip, c, me, srcs[i], land_refs[i], sems[2 * gi], sems[2 * gi + 1], 3 * t + j).start()
        token[...] = jnp.zeros_like(token)

    sem_shapes = []
    for g in groups:
        sem_shapes += [pltpu.SemaphoreType.DMA((3 * len(_AG_GROUPS[g]),))] * 2
    ops = [shards[i] for i in items_all] + lands
    out = pl.pallas_call(
        body, name=name,
        out_shape=tuple(sem_shapes) + tuple(pltpu.HBM(a.shape, a.dtype) for a in ops) + (jax.ShapeDtypeStruct((8, 128), F32),),
        in_specs=(_HBM,) * (2 * n),
        out_specs=(_SEM,) * (2 * ng) + (_HBM,) * (2 * n) + (pl.BlockSpec(memory_space=pltpu.VMEM),),
        input_output_aliases={i: 2 * ng + i for i in range(2 * n)},
        compiler_params=pltpu.CompilerParams(**_SPLIT),
    )(*[_hbm(a) for a in ops])
    sems, thru, token = out[:2 * ng], out[2 * ng:-1], out[-1]
    return ({g: (sems[2 * gi], sems[2 * gi + 1]) for gi, g in enumerate(groups)},
            dict(zip(items_all, thru[:n])), dict(zip(items_all, thru[n:])), token)


def _ag_wait(g, sems, srcs, lands, after):
    items = _AG_GROUPS[g]
    m = len(items)

    def body(*refs):
        src_refs, land_refs = refs[:m], refs[m:2 * m]
        send_sems, recv_sems = refs[2 * m], refs[2 * m + 1]
        x, y, c = _position()
        for t, i in enumerate(items):
            for j, chip in enumerate(_other_chips(x, y)):
                cp = _ag_ici_copy(i, j, chip, c, 2 * chip[0] + chip[1], src_refs[t], land_refs[t], send_sems, recv_sems,
                                  3 * t + j)
                cp.wait_send()
                cp.wait_recv()

    ops = [srcs[i] for i in items] + [lands[i] for i in items]
    out = pl.pallas_call(
        body, name=f"allgather_wait_{g}",
        out_shape=tuple(pltpu.HBM(a.shape, a.dtype) for a in ops),
        in_specs=(_HBM,) * (2 * m) + (_SEM, _SEM, pl.BlockSpec(memory_space=pl.ANY)),
        out_specs=(_HBM,) * (2 * m),
        input_output_aliases={i: i for i in range(2 * m)},
        compiler_params=pltpu.CompilerParams(**_SPLIT),
    )(*ops, sems[0], sems[1], after)
    return list(out[:m]), list(out[m:])


def _ag_forward(g, srcs, lands):
    return _ag_sibling(_AG_GROUPS[g], srcs, lands, False, f"allgather_forward_{g}")


def _ag_push_own(srcs, lands):
    items = tuple(sorted(lands))
    out = _ag_sibling(items, [srcs[i] for i in items], [lands[i] for i in items], True, "allgather_push_own")
    return dict(zip(items, out))


def _ag_sibling(items, srcs, lands, own, name):
    m = len(items)
    per = 2 if own else 3

    def body(*refs):
        src_refs, in_refs, out_refs = refs[:m], refs[m:2 * m], refs[2 * m:3 * m]
        send_sems, recv_sems = refs[3 * m:]
        x, y, c = _position()
        sibling = (x, y, 1 - c)
        me = 2 * x + y
        if own:
            mine = theirs = [(me, 0), (me, 1)]
        else:
            slots = [2 * chip[0] + chip[1] for chip in _other_chips(x, y)]
            mine, theirs = [(s, c) for s in slots], [(s, 1 - c) for s in slots]
        sends = []
        for t, i in enumerate(items):
            _, dst, half = _AG_ITEMS[i]
            for k, (slot, hc) in enumerate(mine):
                src = half(src_refs[t], hc) if own else dst(in_refs[t], slot, hc)
                sends.append(pltpu.make_async_remote_copy(
                    src_ref=src, dst_ref=dst(out_refs[t], slot, hc), send_sem=send_sems.at[per * t + k],
                    recv_sem=recv_sems.at[per * t + k], device_id=sibling, device_id_type=MESH))
        for cp in sends:
            cp.start()
        for t, i in enumerate(items):
            dst = _AG_ITEMS[i][1]
            for k, (slot, hc) in enumerate(theirs):
                there = dst(out_refs[t], slot, hc)
                pltpu.make_async_remote_copy(src_ref=there, dst_ref=there, send_sem=send_sems.at[per * t + k],
                                             recv_sem=recv_sems.at[per * t + k], device_id=sibling,
                                             device_id_type=MESH).wait_recv()
        for cp in sends:
            cp.wait_send()

    any_spec = pl.BlockSpec(memory_space=pl.ANY)
    return pl.pallas_call(
        body, name=name,
        in_specs=[any_spec] * (2 * m), out_specs=(any_spec,) * m,
        out_shape=tuple(jax.ShapeDtypeStruct(a.shape, a.dtype) for a in lands),
        input_output_aliases={m + t: t for t in range(m)},
        scratch_shapes=[pltpu.SemaphoreType.DMA((per * m,)), pltpu.SemaphoreType.DMA((per * m,))],
    )(*srcs, *lands)


def _pair_swap_copy(g_ref, r_ref, send_sem, recv_sem):
    x, y, c = _position()
    hc = g_ref.shape[2] // 2
    return pltpu.make_async_remote_copy(src_ref=g_ref.at[:, :, pl.ds(_al(1 - c, hc), hc)], dst_ref=r_ref,
                                        send_sem=send_sem, recv_sem=recv_sem, device_id=(x, y, 1 - c),
                                        device_id_type=MESH)


def _pair_swap_start(gb, tag):
    _, rows, cols = gb.shape
    recv = lax.empty((4, rows, cols // 2), gb.dtype)

    def body(g_ref, r_ref, send_sem, recv_sem, g_thru, r_thru, token):
        _pair_swap_copy(g_ref, r_ref, send_sem, recv_sem).start()
        token[...] = jnp.zeros_like(token)

    return pl.pallas_call(
        body, name="grad_pair_swap_start_" + tag,
        out_shape=(pltpu.SemaphoreType.DMA(()), pltpu.SemaphoreType.DMA(()), pltpu.HBM(gb.shape, gb.dtype),
                   pltpu.HBM(recv.shape, recv.dtype), jax.ShapeDtypeStruct((8, 128), F32)),
        in_specs=(_HBM, _HBM), out_specs=(_SEM, _SEM, _HBM, _HBM, pl.BlockSpec(memory_space=pltpu.VMEM)),
        input_output_aliases={0: 2, 1: 3},
        compiler_params=pltpu.CompilerParams(**_SPLIT),
    )(_hbm(gb), _hbm(recv))


def _pair_swap_wait(started, after, tag):
    send_sem, recv_sem, gb, recv, _ = started

    def body(g_ref, r_ref, send_sem, recv_sem, after_ref, g_out, r_out):
        cp = _pair_swap_copy(g_ref, r_ref, send_sem, recv_sem)
        cp.wait_send()
        cp.wait_recv()

    return pl.pallas_call(
        body, name="grad_pair_swap_wait_" + tag,
        out_shape=(pltpu.HBM(gb.shape, gb.dtype), pltpu.HBM(recv.shape, recv.dtype)),
        in_specs=(_HBM, _HBM, _SEM, _SEM, pl.BlockSpec(memory_space=pl.ANY)), out_specs=(_HBM, _HBM),
        input_output_aliases={0: 0, 1: 1},
        compiler_params=pltpu.CompilerParams(**_SPLIT),
    )(gb, recv, send_sem, recv_sem, after)


def _handover_copy(r_ref, send_sem, recv_sem, core):
    x, y, c = _position()
    hc = r_ref.shape[1] // 2
    cols = r_ref.at[:, pl.ds(_al(core, hc), hc)]
    return pltpu.make_async_remote_copy(src_ref=cols, dst_ref=cols, send_sem=send_sem, recv_sem=recv_sem,
                                        device_id=(x, y, 1 - c), device_id_type=MESH)


def _handover_start(red, tag):
    def body(r_ref, send_sem, recv_sem, r_thru, token):
        _handover_copy(r_ref, send_sem, recv_sem, lax.axis_index("c")).start()
        token[...] = jnp.zeros_like(token)

    return pl.pallas_call(
        body, name="grad_handover_start_" + tag,
        out_shape=(pltpu.SemaphoreType.DMA(()), pltpu.SemaphoreType.DMA(()), pltpu.HBM(red.shape, red.dtype),
                   jax.ShapeDtypeStruct((8, 128), F32)),
        in_specs=(_HBM,), out_specs=(_SEM, _SEM, _HBM, pl.BlockSpec(memory_space=pltpu.VMEM)),
        input_output_aliases={0: 2},
        compiler_params=pltpu.CompilerParams(**_SPLIT),
    )(_hbm(red))


def _handover_wait(started, after, tag):
    send_sem, recv_sem, red, _ = started

    def body(r_ref, send_sem, recv_sem, after_ref, r_out):
        c = lax.axis_index("c")
        _handover_copy(r_ref, send_sem, recv_sem, c).wait_send()
        _handover_copy(r_ref, send_sem, recv_sem, 1 - c).wait_recv()

    return pl.pallas_call(
        body, name="grad_handover_wait_" + tag,
        out_shape=pltpu.HBM(red.shape, red.dtype),
        in_specs=(_HBM, _SEM, _SEM, pl.BlockSpec(memory_space=pl.ANY)), out_specs=_HBM,
        input_output_aliases={0: 0},
        compiler_params=pltpu.CompilerParams(**_SPLIT),
    )(red, send_sem, recv_sem, after)


def _handover(red, tag):
    started = _handover_start(red, tag)
    return _handover_wait(started, started[3], tag)


def _a2a_copy(j, chip, c, p_ref, q_ref, q_slot, send_sems, recv_sems):
    return pltpu.make_async_remote_copy(src_ref=p_ref.at[2 * chip[0] + chip[1]], dst_ref=q_ref.at[q_slot],
                                        send_sem=send_sems.at[j], recv_sem=recv_sems.at[j], device_id=(*chip, c),
                                        device_id_type=MESH)


def _a2a_start(p, tag):
    def body(p_ref, q_ref, send_sems, recv_sems, p_thru, q_thru, token):
        x, y, c = _position()
        for j, chip in enumerate(_other_chips(x, y)):
            _a2a_copy(j, chip, c, p_ref, q_ref, 2 * x + y, send_sems, recv_sems).start()
        token[...] = jnp.zeros_like(token)

    return pl.pallas_call(
        body, name="grad_alltoall_start_" + tag,
        out_shape=(pltpu.SemaphoreType.DMA((3,)), pltpu.SemaphoreType.DMA((3,)), pltpu.HBM(p.shape, p.dtype),
                   pltpu.HBM(p.shape, p.dtype), jax.ShapeDtypeStruct((8, 128), F32)),
        in_specs=(_HBM, _HBM), out_specs=(_SEM, _SEM, _HBM, _HBM, pl.BlockSpec(memory_space=pltpu.VMEM)),
        input_output_aliases={0: 2, 1: 3},
        compiler_params=pltpu.CompilerParams(**_SPLIT),
    )(_hbm(p), _hbm(lax.empty(p.shape, p.dtype)))


def _a2a_wait(send_sems, recv_sems, p, q, after, tag):
    def body(p_ref, q_ref, send_sems, recv_sems, after_ref, p_out, q_out):
        x, y, c = _position()
        for j, chip in enumerate(_other_chips(x, y)):
            cp = _a2a_copy(j, chip, c, p_ref, q_ref, 2 * chip[0] + chip[1], send_sems, recv_sems)
            cp.wait_send()
            cp.wait_recv()

    return pl.pallas_call(
        body, name="grad_alltoall_wait_" + tag,
        out_shape=(pltpu.HBM(p.shape, p.dtype), pltpu.HBM(q.shape, q.dtype)),
        in_specs=(_HBM, _HBM, _SEM, _SEM, pl.BlockSpec(memory_space=pl.ANY)), out_specs=(_HBM, _HBM),
        input_output_aliases={0: 0, 1: 1},
        compiler_params=pltpu.CompilerParams(**_SPLIT),
    )(p, q, send_sems, recv_sems, after)


def _comm_rows(rows):
    return next(t for t in (512, 384, 256, 128) if rows % t == 0)


def _pair_add(gb, recv, where, tag):
    _, rows, cols = gb.shape
    hc = cols // 2
    tr = _comm_rows(rows)

    def body(w_ref, g_ref, r_ref, o_ref):
        o_ref[...] = (g_ref[...].astype(F32) + r_ref[...].astype(F32)).astype(o_ref.dtype)

    return pl.pallas_call(
        body, name="grad_pair_add_" + tag,
        grid_spec=pltpu.PrefetchScalarGridSpec(
            num_scalar_prefetch=1, grid=(4, rows // tr),
            in_specs=[pl.BlockSpec((None, tr, hc), lambda s, j, w_ref: (s, j, w_ref[0])),
                      pl.BlockSpec((None, tr, hc), lambda s, j, w_ref: (s, j, 0))],
            out_specs=pl.BlockSpec((None, tr, hc), lambda s, j, w_ref: (s, j, 0))),
        out_shape=jax.ShapeDtypeStruct((4, rows, hc), gb.dtype),
        compiler_params=_cp(("parallel", "parallel")),
    )(where, gb, recv)


def _sum_chips(p, q, where, tag):
    _, rows, hc = q.shape
    tr = _comm_rows(rows)

    def body(w_ref, p_ref, qa_ref, qb_ref, qc_ref, o_ref):
        me = w_ref[1]
        own, qa, qb, qc = (r[...].astype(F32) for r in (p_ref, qa_ref, qb_ref, qc_ref))
        v0 = jnp.where(me == 0, own, qa)
        v1 = jnp.where(me == 1, own, jnp.where(me == 0, qa, qb))
        v2 = jnp.where(me == 2, own, jnp.where(me < 2, qb, qc))
        v3 = jnp.where(me == 3, own, qc)
        o_ref[...] = ((v0 + v1) + v2) + v3

    slot = lambda k: pl.BlockSpec((None, tr, hc), lambda j, w_ref: (w_ref[k], j, 0))
    return pl.pallas_call(
        body, name="grad_sum_chips_" + tag,
        grid_spec=pltpu.PrefetchScalarGridSpec(
            num_scalar_prefetch=1, grid=(rows // tr,),
            in_specs=[slot(1), slot(2), slot(3), slot(4)],
            out_specs=pl.BlockSpec((tr, hc), lambda j, w_ref: (j, w_ref[0]))),
        out_shape=jax.ShapeDtypeStruct((rows, 2 * hc), F32),
        compiler_params=_cp(("parallel",)),
    )(where, p, q, q, q)


def _shard_major(g, axis):
    shape = g.shape
    g = g.reshape(shape[:axis] + (4, shape[axis] // 4) + shape[axis + 1:])
    return jnp.moveaxis(g, axis, 0).reshape(4, -1)


def _unshard(g4, shape, axis):
    n = shape[axis] // 4
    g = g4.reshape((4,) + shape[:axis] + (n,) + shape[axis + 1:])
    return jnp.moveaxis(g, 0, axis).reshape(shape)


def _split(flat, shapes):
    out, off = [], 0
    for shp in shapes:
        n = 1
        for d in shp:
            n *= d
        out.append(flat[..., off:off + n].reshape(flat.shape[:-1] + tuple(shp)))
        off += n
    return out


def _even_rows_to_kernel(wt):
    return jnp.concatenate([wt[:1536], wt[1552:3088], wt[1536:1552], wt[3088:3096],
                            jnp.zeros((PE - 3096, wt.shape[1]), wt.dtype)], axis=0)


def _block_diag(w):
    eye = jnp.eye(8, dtype=w.dtype)
    return (w[:, :, None, :] * eye[:, None, :, None]).reshape(512, 512)


def _diag_blocks(g):
    eye = jnp.eye(8, dtype=g.dtype)
    return (g.reshape(8, 64, 8, 64) * eye[:, None, :, None]).sum(axis=2)


def _shift_down(a, s):
    return a if s == 0 else jnp.pad(a, ((s, 0), (0, 0)))[:a.shape[0]]


def _shift_up(a, s):
    return a if s == 0 else jnp.pad(a, ((0, s), (0, 0)))[s:]


SMALL_SHARDED_SHAPES = [(2, 4, 256), (16, 64), (4, 128), (128,), (128,), (128,), (128,)]
REPL_SHAPES = [(256,), (512,), (8,), (8, 257), (8, 64, 64), (8, 64, 64)]


def kernel(x, norm_w, w_in_even, gla_w_a_up, gla_b_a, gla_norm_w, fox_b_f, w_out_even, w_in_odd, rel_bias, conv_w, conv_b, lru_w_a, lru_b_a, lru_w_x, lru_b_x, lru_lambda, w_out_odd, w_mlp_up, w_mlp_down, loss_target, m_norm_w, m_w_in_even, m_gla_w_a_up, m_gla_b_a, m_gla_norm_w, m_fox_b_f, m_w_out_even, m_w_in_odd, m_rel_bias, m_conv_w, m_conv_b, m_lru_w_a, m_lru_b_a, m_lru_w_x, m_lru_b_x, m_lru_lambda, m_w_out_odd, m_w_mlp_up, m_w_mlp_down, v_norm_w, v_w_in_even, v_gla_w_a_up, v_gla_b_a, v_gla_norm_w, v_fox_b_f, v_w_out_even, v_w_in_odd, v_rel_bias, v_conv_w, v_conv_b, v_lru_w_a, v_lru_b_a, v_lru_w_x, v_lru_b_x, v_lru_lambda, v_w_out_odd, v_w_mlp_up, v_w_mlp_down):
    c_idx = lax.axis_index("c")

    small_local = [norm_w, gla_w_a_up[0], conv_w[0], conv_b[0], lru_b_a[0], lru_b_x[0], lru_lambda[0]]
    small_src = jnp.concatenate([a.reshape(-1) for a in small_local]).reshape(32, 128)
    first = {0: small_src, 1: w_in_even[0].T.astype(BF16), 2: w_out_even[0].astype(BF16)}
    sems0, srcs0, lands0, ag_token = _ag_start([0], first, "allgather_start_0")
    zero = ag_token[0, 0]
    later = {3: (w_mlp_up + zero).astype(BF16), 4: (w_mlp_down + zero).astype(BF16),
             5: (w_in_odd[0] + zero).astype(BF16), 6: (w_out_odd[0] + zero).astype(BF16)}
    sems1, srcs1, lands1, ag_token = _ag_start([1, 2], later, "allgather_start_1")
    ag_sems, ag_srcs = {**sems0, **sems1}, {**srcs0, **srcs1}
    ag_lands = _ag_push_own(ag_srcs, {**lands0, **lands1})

    def gathered(g, after):
        srcs_g, lands_g = _ag_wait(g, ag_sems[g], ag_srcs, ag_lands, after)
        return _ag_forward(g, srcs_g, lands_g)

    small4, w_in_e4, w_out_e = gathered(0, ag_token)
    me = 2 * lax.axis_index("x") + lax.axis_index("y")
    others = [k + (k >= me).astype(jnp.int32) for k in range(3)]
    where = jnp.stack([c_idx, me] + others).astype(jnp.int32)

    w_in_e_t = _even_rows_to_kernel(w_in_e4.reshape(3096, D))
    g_small = _split(small4.reshape(4, 32 * 128), SMALL_SHARDED_SHAPES)
    nw_full = _unshard(g_small[0], (2, 4, 1024), 2)
    wa_up = _unshard(g_small[1], (16, 256), 1)
    cw = _unshard(g_small[2], (4, 512), 1)
    cb, lba, lbx, lam = [_unshard(g, (512,), 0).reshape(1, 512) for g in g_small[3:]]
    nw = lambda layer, i: nw_full[layer, i].reshape(1, D)

    wa_pad = jnp.pad(wa_up, ((0, 128 - 16), (0, 0)))
    gla_ba = gla_b_a.reshape(1, 256)
    gla_nw = gla_norm_w.reshape(1, 512)
    fox_bpad = jnp.pad(fox_b_f.reshape(1, 8), ((0, 0), (FOX_LANE0, 128 - FOX_LANE0 - 8)))
    rbp = jnp.pad(rel_bias[0], ((0, 0), (0, REL_PAD - 257)))
    wa_bd = _block_diag(lru_w_a[0])
    wx_bd = _block_diag(lru_w_x[0])

    x0 = x[0]
    tgt = loss_target[0]

    h0 = _prenorm(x0, nw(0, 0), "prenorm_l0_mix")
    proj_e = _mm(h0, w_in_e_t, "nt", tm=2048, tn=640, name="mm_in_even")
    cat0, s_prev = _gla_fwd(proj_e, wa_pad, gla_ba, gla_nw)
    cum_r = _fox_gate_fwd(proj_e, fox_bpad)
    cum_c = cum_r[:, FOX_LANE0:FOX_LANE0 + 8].T
    cat0 = _fox_fwd(proj_e, cum_c, cat0)
    mix0 = _mm(cat0, w_out_e, "nn", tm=2048, tn=512, name="mm_out_even")
    x1, h1 = _post_pre_fwd(x0, mix0, nw(0, 1), nw(0, 2), "post_pre_l0_mix")
    w_up, w_dn = gathered(1, x1)
    a0, r0 = _mm(h1, w_up, "nn", tm=2048, tn=1024, b_layer=0, relu_pair=True, name="mm_up_l0")
    d0 = _mm(a0, w_dn, "nn", tm=1024, tn=512, b_layer=0, name="mm_down_l0")
    x2, h2 = _post_pre_fwd(x1, d0, nw(0, 3), nw(1, 0), "post_pre_l0_mlp")

    w_in_o, w_out_o = gathered(2, x2)
    proj_o = _mm(h2, w_in_o, "nn", tm=2048, tn=640, name="mm_in_odd")
    bias_q = _bias_build(rbp)
    bias = bias_q.transpose(1, 0, 2)
    kvpad = jnp.pad(proj_o[:, 512:1536], ((CA_PAD, 0), (0, 0)))
    cat1 = _ca_fwd(proj_o, kvpad, bias)
    x_in = proj_o[:, 2048:2560]
    xs = jnp.stack([_shift_down(x_in, 3 - j) for j in range(4)])
    lru_a, lru_b = _lru_pre_fwd(xs, cw, cb, wa_bd, lba, wx_bd, lbx, lam)
    hh = _lru_scan_fwd(lru_a, lru_b)
    cat1 = _lru_post_fwd(hh, proj_o, cat1)
    mix1 = _mm(cat1, w_out_o, "nn", tm=2048, tn=512, name="mm_out_odd")
    x3, h3 = _post_pre_fwd(x2, mix1, nw(1, 1), nw(1, 2), "post_pre_l1_mix")
    a1, r1 = _mm(h3, w_up, "nn", tm=2048, tn=1024, b_layer=1, relu_pair=True, name="mm_up_l1")
    d1 = _mm(a1, w_dn, "nn", tm=1024, tn=512, b_layer=1, name="mm_down_l1")
    g4, loss_part, dd1, dnw13 = _post_loss(x3, d1, nw(1, 3), tgt)
    loss = lax.psum(loss_part[0, 0], ("x", "y", "c"))

    def rs_begin(swap, after, tag):
        gb, recv = _pair_swap_wait(swap, after, tag)
        return _a2a_start(_pair_add(gb, recv, where, tag), tag)

    def rs_end(started, after, tag):
        send_sems, recv_sems, p, q, _ = started
        p, q = _a2a_wait(send_sems, recv_sems, p, q, after, tag)
        return _handover(_sum_chips(p, q, where, tag), tag)

    gba = lax.dynamic_update_slice(lax.empty((4, GA_ROWS, D), BF16), jnp.zeros((4, GA_UP - GA_GAP, D), BF16),
                                   (0, GA_GAP, 0))
    gba = _mm(a1, dd1, "tn", tm=512, tn=1024, into=(gba, 1024, GA_DN), name="mm_down_l1_dw")
    du1 = _mm(dd1, w_dn, "nt", tm=2048, tn=1024, b_layer=1, times2=r1, out_dtype=BF16, name="mm_down_l1_dx")
    gba = _mm(du1, h3, "tn", tm=512, tn=1024, into=(gba, 1024, GA_UP), name="mm_up_l1_dw")
    dh3 = _mm(du1, w_up, "nt", tm=1024, tn=512, b_layer=1, name="mm_up_l1_dx")
    g3, dmix1, dnw12, dnw11 = _pre_post_bwd(x3, nw(1, 2), dh3, g4, mix1, nw(1, 1), "pre_post_bwd_l1_mlp")
    gba = _mm(cat1, dmix1, "tn", tm=128, tn=1024, into=(gba, 256, GA_OUT_O), name="mm_out_odd_dw")
    dcat1 = _mm(dmix1, w_out_o, "nt", tm=2048, tn=512, name="mm_out_odd_dx")

    dq_c, dkpad, dvpad, dbias = _ca_bwd(proj_o, kvpad, bias, dcat1)
    g_rel = _bias_grad(jnp.pad(dbias.transpose(1, 0, 2), ((0, 0), (0, 0), (0, BIAS_W - CA_BAND))))[:, :257]
    dhh, dgate = _lru_post_bwd(hh, proj_o, dcat1)
    da_l, db_l = _lru_scan_bwd(_shift_up(lru_a, 1), _shift_down(hh, 1), dhh)
    dxs, g_cw, g_cb, g_wa_bd, g_lba, g_wx_bd, g_lbx, g_lam = _lru_pre_bwd(xs, cw, cb, wa_bd, lba, wx_bd, lbx, lam, da_l, db_l)
    dx_in = _conv_dx(jnp.stack([_shift_up(dxs[j], 3 - j) for j in range(4)]))
    dproj_o = jnp.concatenate([dq_c, dkpad[CA_PAD:], dvpad[CA_PAD:], dgate, dx_in], axis=1).astype(BF16)
    gba = _mm(dproj_o, h2, "tn", tm=128, tn=1024, into=(gba, 640, GA_IN_O), name="mm_in_odd_dw")
    swap_a = _pair_swap_start(gba, "a")
    dh2 = _mm(dproj_o, w_in_o, "nt", tm=1024, tn=512, name="mm_in_odd_dx")
    g2, dd0, dnw10, dnw03 = _pre_post_bwd(x2, nw(1, 0) + swap_a[4][0, 0], dh2, g3, d0, nw(0, 3), "pre_post_bwd_l1_mix")
    rs_a = rs_begin(swap_a, g2, "a")

    gbb = lax.empty((4, GB_ROWS, D), BF16)
    gbb = _mm(a0, dd0, "tn", tm=512, tn=1024, into=(gbb, 1024, GB_DN), name="mm_down_l0_dw")
    du0 = _mm(dd0, w_dn, "nt", tm=2048, tn=1024, b_layer=0, times2=r0, out_dtype=BF16, name="mm_down_l0_dx")
    gbb = _mm(du0, h1, "tn", tm=512, tn=1024, into=(gbb, 1024, GB_UP), name="mm_up_l0_dw")
    swap_b = _pair_swap_start(gbb, "b")
    dh1 = _mm(du0, w_up, "nt", tm=1024, tn=512, b_layer=0, name="mm_up_l0_dx")
    g1, dmix0, dnw02, dnw01 = _pre_post_bwd(x1, nw(0, 2) + (swap_b[4][0, 0] + rs_a[4][0, 0]), dh1, g2, mix0, nw(0, 1),
                                            "pre_post_bwd_l0_mlp")
    rs_b = rs_begin(swap_b, g1, "b")
    gbc = lax.empty((4, GC_ROWS, D), BF16)
    gbc = _mm(cat0, dmix0, "tn", tm=128, tn=1024, into=(gbc, 256, GC_OUT_E), name="mm_out_even_dw")
    dcat0 = _mm(dmix0, w_out_e, "nt", tm=2048, tn=512, name="mm_out_even_dx")

    dq_g, dk_g, dv_g, dr_g, daux_g, g_wa_pad, g_gla_ba, g_gla_nw = _gla_bwd(
        proj_e, s_prev, wa_pad, gla_ba, gla_nw + rs_b[4][0, 0], dcat0)
    dq_f, dk_f, dv_f, dccol = _fox_bwd(proj_e, cum_c, dcat0)
    dccol_t = jnp.pad(dccol.sum(axis=0).T, ((0, 0), (FOX_LANE0, 128 - FOX_LANE0 - 8)))
    daux, g_fox_bpad = _fox_gate_bwd(proj_e, fox_bpad, dccol_t, daux_g)
    dproj_e = jnp.concatenate([dq_g, dk_g, dv_g, dr_g, dq_f, dk_f, dv_f, daux], axis=1).astype(BF16)
    gt_in_e = _mm(dproj_e, h0, "tn", tm=640, tn=1024, out_dtype=BF16, name="mm_in_even_dw")
    dh0 = _mm(dproj_e, w_in_e_t, "nn", tm=1024, tn=512, name="mm_in_even_dx")
    grad_x, dnw00 = _norm_bwd(x0, nw(0, 0), dh0, g1, "prenorm_l0_mix_bwd")

    def rs_reduce(started, after, tag):
        send_sems, recv_sems, p, q, _ = started
        p, q = _a2a_wait(send_sems, recv_sems, p, q, after, tag)
        return _handover_start(_sum_chips(p, q, where, tag), tag)

    ho_a = rs_reduce(rs_a, grad_x, "a")
    ho_b = rs_reduce(rs_b, ho_a[3], "b")

    g_norm = jnp.stack([jnp.concatenate([dnw00, dnw01, dnw02, dnw03]), jnp.concatenate([dnw10, dnw11, dnw12, dnw13])])
    sharded = [(g_norm, 2), (g_wa_pad[:16], 1), (g_cw, 1), (g_cb[0], 0), (g_lba[0], 0), (g_lbx[0], 0), (g_lam[0], 0)]
    replicated = [g_gla_ba[0], g_gla_nw[0], g_fox_bpad[0, FOX_LANE0:FOX_LANE0 + 8], g_rel, _diag_blocks(g_wa_bd),
                  _diag_blocks(g_wx_bd)]
    small4 = jnp.concatenate([_shard_major(g, ax) for g, ax in sharded]
                             + [jnp.broadcast_to(g.reshape(1, -1), (4, g.size)) for g in replicated], axis=1)
    n_small = small4.shape[1]
    small_rows = GC_ROWS - GC_TAIL - 774
    small4 = jnp.pad(small4, ((0, 0), (0, small_rows * D - n_small))).reshape(4, small_rows, D)
    gt_rows = jnp.concatenate([gt_in_e[:1536], gt_in_e[3072:3088], gt_in_e[1536:3072], gt_in_e[3088:3096]], axis=0)
    tail = jnp.concatenate([gt_rows.reshape(4, 774, D), small4.astype(BF16)], axis=1)
    gbc = lax.dynamic_update_slice(gbc, tail, (0, GC_TAIL, 0))
    swap_c = _pair_swap_start(gbc, "c")
    rs_c = rs_begin(swap_c, swap_c[4], "c")

    red_a = _handover_wait(ho_a, rs_c[4], "a")
    red_b = _handover_wait(ho_b, red_a, "b")
    early = dict(
        w_mlp_up=_adamw_from(w_mlp_up, m_w_mlp_up, v_w_mlp_up, [(red_b, GB_UP, True), (red_a, GA_UP, True)], 512,
                             "adamw_w_mlp_up"),
        w_mlp_down=_adamw_from(w_mlp_down, m_w_mlp_down, v_w_mlp_down, [(red_b, GB_DN, False), (red_a, GA_DN, False)],
                               512, "adamw_w_mlp_down"),
        w_in_odd=_adamw_from(w_in_odd, m_w_in_odd, v_w_in_odd, [(red_a, GA_IN_O, True)], 256, "adamw_w_in_odd"),
        w_out_odd=_adamw_from(w_out_odd, m_w_out_odd, v_w_out_odd, [(red_a, GA_OUT_O, False)], 128, "adamw_w_out_odd"))
    red_c = rs_end(rs_c, early["w_out_odd"][3], "c")

    g_small = _split(red_c[GC_TAIL + 774:].reshape(-1)[:n_small], SMALL_SHARDED_SHAPES + REPL_SHAPES)
    g_of = dict(zip(["norm_w", "gla_w_a_up", "conv_w", "conv_b", "lru_b_a", "lru_b_x", "lru_lambda", "gla_b_a",
                     "gla_norm_w", "fox_b_f", "rel_bias", "lru_w_a", "lru_w_x"], g_small))
    g_of.update(w_in_even=red_c[GC_TAIL:GC_TAIL + 774])
    early["w_out_even"] = _adamw_from(w_out_even, m_w_out_even, v_w_out_even, [(red_c, GC_OUT_E, False)], 256,
                                      "adamw_w_out_even")

    names = ["norm_w", "w_in_even", "gla_w_a_up", "gla_b_a", "gla_norm_w", "fox_b_f", "w_out_even", "w_in_odd", "rel_bias",
             "conv_w", "conv_b", "lru_w_a", "lru_b_a", "lru_w_x", "lru_b_x", "lru_lambda", "w_out_odd", "w_mlp_up",
             "w_mlp_down"]
    w_of = dict(norm_w=norm_w, w_in_even=w_in_even, gla_w_a_up=gla_w_a_up, gla_b_a=gla_b_a, gla_norm_w=gla_norm_w,
                fox_b_f=fox_b_f, w_out_even=w_out_even, w_in_odd=w_in_odd, rel_bias=rel_bias, conv_w=conv_w, conv_b=conv_b,
                lru_w_a=lru_w_a, lru_b_a=lru_b_a, lru_w_x=lru_w_x, lru_b_x=lru_b_x, lru_lambda=lru_lambda,
                w_out_odd=w_out_odd, w_mlp_up=w_mlp_up, w_mlp_down=w_mlp_down)
    m_of = dict(norm_w=m_norm_w, w_in_even=m_w_in_even, gla_w_a_up=m_gla_w_a_up, gla_b_a=m_gla_b_a,
                gla_norm_w=m_gla_norm_w, fox_b_f=m_fox_b_f, w_out_even=m_w_out_even, w_in_odd=m_w_in_odd,
                rel_bias=m_rel_bias, conv_w=m_conv_w, conv_b=m_conv_b, lru_w_a=m_lru_w_a, lru_b_a=m_lru_b_a,
                lru_w_x=m_lru_w_x, lru_b_x=m_lru_b_x, lru_lambda=m_lru_lambda, w_out_odd=m_w_out_odd,
                w_mlp_up=m_w_mlp_up, w_mlp_down=m_w_mlp_down)
    v_of = dict(norm_w=v_norm_w, w_in_even=v_w_in_even, gla_w_a_up=v_gla_w_a_up, gla_b_a=v_gla_b_a,
                gla_norm_w=v_gla_norm_w, fox_b_f=v_fox_b_f, w_out_even=v_w_out_even, w_in_odd=v_w_in_odd,
                rel_bias=v_rel_bias, conv_w=v_conv_w, conv_b=v_conv_b, lru_w_a=v_lru_w_a, lru_b_a=v_lru_b_a,
                lru_w_x=v_lru_w_x, lru_b_x=v_lru_b_x, lru_lambda=v_lru_lambda, w_out_odd=v_w_out_odd,
                w_mlp_up=v_w_mlp_up, w_mlp_down=v_w_mlp_down)
    grads, deltas, new_ms, new_vs = [], [], [], []
    for n in names:
        w = w_of[n]
        if n in early:
            g, d, mn, vn = early[n]
            grads.append(g)
            deltas.append(d)
            new_ms.append(mn)
            new_vs.append(vn)
            continue
        if n == "w_in_even":
            to_view = lambda a: a[0].T
            from_view = lambda a: a.T[None]
        else:
            view = w.shape if w.ndim <= 3 else w.shape[-3:]
            to_view = lambda a, view=view: a.reshape(view)
            from_view = lambda a, w=w: a.reshape(w.shape)
        g = g_of[n] if n == "w_in_even" else to_view(g_of[n])
        d, mn, vn = _adamw(to_view(w), g, to_view(m_of[n]), to_view(v_of[n]), "adamw_" + n)
        grads.append(from_view(g))
        deltas.append(from_view(d))
        new_ms.append(from_view(mn))
        new_vs.append(from_view(vn))

    return (loss, grad_x.reshape(1, T, D), *grads, *deltas, *new_ms, *new_vs)
```

```python
import functools

import jax
import jax.numpy as jnp
from jax import lax
from jax.experimental import pallas as pl
from jax.experimental.pallas import tpu as pltpu

F32 = jnp.float32
BF16 = jnp.bfloat16
MESH = pl.DeviceIdType.MESH

T = 2048
D = 1024
DFF = 4096
EPS = 1e-6
CHUNK = 64
NCHUNK = T // CHUNK
PE = 3200
PO = 2560
AUX_BLK = 3072 // 128
FOX_LANE0 = 16
GLA_SCALE = 64 ** -0.5
ATT_SCALE = 64 ** -0.5
NEG = float(jnp.finfo(jnp.float32).min)
CA_BAND = 576
CA_PAD = 512
REL_PAD = 384

VMEM_LIMIT = 48 * 1024 * 1024

ADAM_LR, ADAM_B1, ADAM_B2, ADAM_EPS, ADAM_WD, ADAM_STEP = 0.001, 0.9, 0.999, 1e-08, 0.01, 10

GA_ROWS, GA_IN_O, GA_OUT_O, GA_GAP, GA_UP, GA_DN = 3072, 0, 640, 896, 1024, 2048
GB_ROWS, GB_UP, GB_DN = 2048, 0, 1024
GC_ROWS, GC_OUT_E, GC_TAIL = 1152, 0, 256

_DIMS = {"nn": (((1,), (0,)), ((), ())), "nt": (((1,), (1,)), ((), ())), "tn": (((0,), (0,)), ((), ()))}


def _cp(sem, **kw):
    return pltpu.CompilerParams(dimension_semantics=sem, vmem_limit_bytes=VMEM_LIMIT, **kw)


def _dot(a, b, mode):
    return lax.dot_general(a.astype(BF16), b.astype(BF16), _DIMS[mode], preferred_element_type=F32)


@functools.partial(jax.custom_vjp, nondiff_argnums=(2,))
def bdot(a, b, mode):
    return _dot(a, b, mode)


def _bdot_fwd(a, b, mode):
    return _dot(a, b, mode), (a, b)


def _bdot_bwd(mode, res, g):
    a, b = res
    if mode == "nn":
        da, db = _dot(g, b, "nt"), _dot(a, g, "tn")
    elif mode == "nt":
        da, db = _dot(g, b, "nn"), _dot(g, a, "tn")
    else:
        da, db = _dot(b, g, "nt"), _dot(a, g, "nn")
    return da.astype(a.dtype), db.astype(b.dtype)


bdot.defvjp(_bdot_fwd, _bdot_bwd)


def _hdot_raw(a, b, mode):
    return lax.dot_general(a, b, _DIMS[mode], precision=lax.Precision.HIGHEST, preferred_element_type=F32)


def _log_sigmoid(x):
    return jnp.minimum(x, 0.0) - jnp.log(1.0 + jnp.exp(-jnp.abs(x)))


def _sigmoid(x):
    return 1.0 / (1.0 + jnp.exp(-x))


def _expm1(x):
    series = x * (1.0 + x * 0.5 * (1.0 + x * (1.0 / 3.0) * (1.0 + x * 0.25)))
    return jnp.where(jnp.abs(x) < 0.03, series, jnp.exp(x) - 1.0)


def _gelu_tanh(x):
    return 0.5 * x * (1.0 + jnp.tanh(0.7978845608028654 * (x + 0.044715 * x * x * x)))


def _iota(shape, dim):
    return lax.broadcasted_iota(jnp.int32, shape, dim)


def _mm(a, b, mode, *, tm, tn, tk=None, out_dtype=F32, name, b_layer=None, into=None, relu_pair=False, times2=None):
    b2 = b.shape[-2:]
    if mode == "nn":
        (m, k), n = a.shape, b2[1]
    elif mode == "nt":
        (m, k), n = a.shape, b2[0]
    else:
        (k, m), n = a.shape, b2[1]
    tk = k if tk is None else tk
    assert m % tm == 0 and n % tn == 0 and k % tk == 0, (name, a.shape, b.shape)
    nk = k // tk
    if mode == "tn":
        a_spec = pl.BlockSpec((tk, tm), lambda i, j, kk: (kk, i))
    elif m == tm and nk == 1:
        a_spec = pl.BlockSpec((tm, tk), lambda i, j, kk: (i, kk), pipeline_mode=pl.Buffered(1))
    else:
        a_spec = pl.BlockSpec((tm, tk), lambda i, j, kk: (i, kk))
    b_blk = {"nn": (tk, tn), "nt": (tn, tk), "tn": (tk, tn)}[mode]
    b_idx = {"nn": lambda i, j, kk: (kk, j), "nt": lambda i, j, kk: (j, kk), "tn": lambda i, j, kk: (kk, j)}[mode]
    if b_layer is None:
        b_spec = pl.BlockSpec(b_blk, b_idx)
    else:
        b_spec = pl.BlockSpec((None,) + b_blk, lambda i, j, kk: (b_layer,) + b_idx(i, j, kk))

    tile = pl.BlockSpec((tm, tn), lambda i, j, kk: (i, j))
    if into is not None:
        buf, per_slot, row_off = into
        assert m == 4 * per_slot and per_slot % tm == 0 and row_off % tm == 0 and buf.shape[2] == n, (name, buf.shape)
        bps = per_slot // tm
        out_specs = pl.BlockSpec((None, tm, tn), lambda i, j, kk: (i // bps, row_off // tm + i % bps, j))
        out_shape = jax.ShapeDtypeStruct(buf.shape, buf.dtype)
        extra_in, extra_specs, aliases = [buf], [pl.BlockSpec(memory_space=pl.ANY)], {2: 0}
        finish = lambda acc, extra: [acc.astype(buf.dtype)]
    elif relu_pair:
        out_specs = (tile, tile)
        out_shape = (jax.ShapeDtypeStruct((m, n), BF16),) * 2
        extra_in, extra_specs, aliases = [], [], {}

        def finish(acc, extra):
            r = jnp.maximum(acc, 0.0)
            return [(r * r).astype(BF16), r.astype(BF16)]
    elif times2 is not None:
        out_specs = tile
        out_shape = jax.ShapeDtypeStruct((m, n), out_dtype)
        extra_in, extra_specs, aliases = [times2], [tile], {}
        finish = lambda acc, extra: [(acc * (2.0 * extra[...].astype(F32))).astype(out_dtype)]
    else:
        out_specs = tile
        out_shape = jax.ShapeDtypeStruct((m, n), out_dtype)
        extra_in, extra_specs, aliases = [], [], {}
        finish = lambda acc, extra: [acc.astype(out_dtype)]
    n_out = 2 if relu_pair else 1

    def body(*refs):
        a_ref, b_ref = refs[0], refs[1]
        extra = refs[2] if extra_in else None
        o_refs = refs[2 + len(extra_in):2 + len(extra_in) + n_out]

        def store(acc):
            for o_ref, val in zip(o_refs, finish(acc, extra)):
                o_ref[...] = val

        if nk == 1:
            store(_dot(a_ref[...], b_ref[...], mode))
            return
        acc_ref = refs[-1]
        kk = pl.program_id(2)

        @pl.when(kk == 0)
        def _():
            acc_ref[...] = jnp.zeros_like(acc_ref)

        acc_ref[...] += _dot(a_ref[...], b_ref[...], mode)

        @pl.when(kk == nk - 1)
        def _():
            store(acc_ref[...])

    return pl.pallas_call(
        body, name=name, grid=(m // tm, n // tn, nk),
        in_specs=[a_spec, b_spec] + extra_specs,
        out_specs=out_specs, out_shape=out_shape,
        scratch_shapes=[pltpu.VMEM((tm, tn), F32)] if nk > 1 else [],
        input_output_aliases=aliases,
        compiler_params=_cp(("parallel", "parallel", "arbitrary")),
    )(a, b, *extra_in)


ROWS = 512


def _prenorm(x, w, name):
    def body(x_ref, w_ref, o_ref):
        xv = x_ref[...]
        r = lax.rsqrt(jnp.mean(xv * xv, axis=-1, keepdims=True) + EPS)
        o_ref[...] = (xv * r * w_ref[...]).astype(BF16)

    return pl.pallas_call(
        body, name=name, grid=(T // ROWS,),
        in_specs=[pl.BlockSpec((ROWS, D), lambda i: (i, 0)), pl.BlockSpec((1, D), lambda i: (0, 0))],
        out_specs=pl.BlockSpec((ROWS, D), lambda i: (i, 0)),
        out_shape=jax.ShapeDtypeStruct((T, D), BF16),
        compiler_params=_cp(("parallel",)),
    )(x, w)


def _rms(z):
    return lax.rsqrt(jnp.mean(z * z, axis=-1, keepdims=True) + EPS)


def _rms_bwd(z, w, dy):
    r = _rms(z)
    wdy = dy * w
    dz = r * wdy - z * (r * r * r) * jnp.mean(z * wdy, axis=-1, keepdims=True)
    return dz, jnp.sum(dy * z * r, axis=0, keepdims=True)


_ROW = pl.BlockSpec((ROWS, D), lambda i: (i, 0))
_VEC = pl.BlockSpec((1, D), lambda i: (0, 0))


def _post_pre_fwd(x, z, w_post, w_pre, name):
    def body(x_ref, z_ref, wp_ref, wn_ref, x_out, h_out):
        zv = z_ref[...]
        xn = x_ref[...] + zv * _rms(zv) * wp_ref[...]
        x_out[...] = xn
        h_out[...] = (xn * _rms(xn) * wn_ref[...]).astype(BF16)

    return pl.pallas_call(
        body, name=name, grid=(T // ROWS,), in_specs=[_ROW, _ROW, _VEC, _VEC], out_specs=(_ROW, _ROW),
        out_shape=(jax.ShapeDtypeStruct((T, D), F32), jax.ShapeDtypeStruct((T, D), BF16)),
        compiler_params=_cp(("parallel",)),
    )(x, z, w_post, w_pre)


def _post_loss(x, z, w_post, tgt):
    def body(x_ref, z_ref, w_ref, t_ref, g_ref, l_ref, dz_ref, dw_ref):
        @pl.when(pl.program_id(0) == 0)
        def _():
            l_ref[...] = jnp.zeros_like(l_ref)
            dw_ref[...] = jnp.zeros_like(dw_ref)

        zv = z_ref[...]
        e = x_ref[...] + zv * _rms(zv) * w_ref[...] - t_ref[...]
        g = e * (1.0 / D)
        g_ref[...] = g
        l_ref[...] += jnp.sum(e * e) * (0.5 / D)
        dz, dw = _rms_bwd(zv, w_ref[...], g)
        dz_ref[...] = dz.astype(BF16)
        dw_ref[...] += dw

    return pl.pallas_call(
        body, name="postnorm_loss", grid=(T // ROWS,), in_specs=[_ROW, _ROW, _VEC, _ROW],
        out_specs=(_ROW, pl.BlockSpec((1, 128), lambda i: (0, 0)), _ROW, _VEC),
        out_shape=(jax.ShapeDtypeStruct((T, D), F32), jax.ShapeDtypeStruct((1, 128), F32),
                   jax.ShapeDtypeStruct((T, D), BF16), jax.ShapeDtypeStruct((1, D), F32)),
        compiler_params=_cp(("arbitrary",)),
    )(x, z, w_post, tgt)


def _pre_post_bwd(x, w_pre, dh, add, z, w_post, name):
    def body(x_ref, wn_ref, dh_ref, add_ref, z_ref, wp_ref, g_ref, dz_ref, dwn_ref, dwp_ref):
        @pl.when(pl.program_id(0) == 0)
        def _():
            dwn_ref[...] = jnp.zeros_like(dwn_ref)
            dwp_ref[...] = jnp.zeros_like(dwp_ref)

        dx, dwn = _rms_bwd(x_ref[...], wn_ref[...], dh_ref[...])
        g = dx + add_ref[...]
        g_ref[...] = g
        dz, dwp = _rms_bwd(z_ref[...], wp_ref[...], g)
        dz_ref[...] = dz.astype(BF16)
        dwn_ref[...] += dwn
        dwp_ref[...] += dwp

    return pl.pallas_call(
        body, name=name, grid=(T // ROWS,), in_specs=[_ROW, _VEC, _ROW, _ROW, _ROW, _VEC],
        out_specs=(_ROW, _ROW, _VEC, _VEC),
        out_shape=(jax.ShapeDtypeStruct((T, D), F32), jax.ShapeDtypeStruct((T, D), BF16),
                   jax.ShapeDtypeStruct((1, D), F32), jax.ShapeDtypeStruct((1, D), F32)),
        compiler_params=_cp(("arbitrary",)),
    )(x, w_pre, dh, add, z, w_post)


def _norm_bwd(z, w, dy, add, name):
    has_add = add is not None

    def body(*refs):
        if has_add:
            z_ref, w_ref, dy_ref, add_ref, dz_ref, dw_ref = refs
        else:
            z_ref, w_ref, dy_ref, dz_ref, dw_ref = refs
        i = pl.program_id(0)

        @pl.when(i == 0)
        def _():
            dw_ref[...] = jnp.zeros_like(dw_ref)

        zv = z_ref[...].astype(F32)
        dyv = dy_ref[...]
        r = lax.rsqrt(jnp.mean(zv * zv, axis=-1, keepdims=True) + EPS)
        wdy = dyv * w_ref[...]
        dz = r * wdy - zv * (r * r * r) * jnp.mean(zv * wdy, axis=-1, keepdims=True)
        if has_add:
            dz = dz + add_ref[...]
        dz_ref[...] = dz.astype(dz_ref.dtype)
        dw_ref[...] += jnp.sum(dyv * zv * r, axis=0, keepdims=True)

    row = pl.BlockSpec((ROWS, D), lambda i: (i, 0))
    vec = pl.BlockSpec((1, D), lambda i: (0, 0))
    ins = [z, w, dy] + ([add] if has_add else [])
    dz_dtype = F32 if has_add else BF16
    return pl.pallas_call(
        body, name=name, grid=(T // ROWS,),
        in_specs=[row, vec, row] + ([row] if has_add else []),
        out_specs=(row, vec),
        out_shape=(jax.ShapeDtypeStruct((T, D), dz_dtype), jax.ShapeDtypeStruct((1, D), F32)),
        compiler_params=_cp(("arbitrary",)),
    )(*ins)


def _adamw_math(w, g, m, v):
    c1 = 1.0 - ADAM_B1 ** ADAM_STEP
    c2 = 1.0 - ADAM_B2 ** ADAM_STEP
    mn = ADAM_B1 * m + (1.0 - ADAM_B1) * g
    vn = ADAM_B2 * v + (1.0 - ADAM_B2) * (g * g)
    return -ADAM_LR * ((mn / c1) / (jnp.sqrt(vn / c2) + ADAM_EPS) + ADAM_WD * w), mn, vn


def _adamw_from(w, m, v, sources, tr, name):
    layers, rows, cols = w.shape
    assert len(sources) == layers and rows % tr == 0, (name, w.shape)
    g_specs = []
    for layer, (buf, row0, transposed) in enumerate(sources):
        step = lambda l, i, layer=layer: jnp.where(l == layer, i, 0)
        if transposed:
            assert row0 % cols == 0 and buf.shape[1] == rows, (name, row0)
            g_specs.append(pl.BlockSpec((cols, tr), lambda l, i, b=row0 // cols, step=step: (b, step(l, i))))
        else:
            assert row0 % tr == 0 and buf.shape[1] == cols, (name, row0)
            g_specs.append(pl.BlockSpec((tr, cols), lambda l, i, b=row0 // tr, step=step: (b + step(l, i), 0)))

    def body(*refs):
        w_ref, m_ref, v_ref = refs[:3]
        g_refs = refs[3:3 + layers]
        g_out, d_ref, mo_ref, vo_ref = refs[3 + layers:]
        gs = [r[...].T if src[2] else r[...] for r, src in zip(g_refs, sources)]
        g = gs[0] if layers == 1 else jnp.where(pl.program_id(0) == 0, gs[0], gs[1])
        g_out[...] = g
        d_ref[...], mo_ref[...], vo_ref[...] = _adamw_math(w_ref[...], g, m_ref[...], v_ref[...])

    blk = pl.BlockSpec((None, tr, cols), lambda l, i: (l, i, 0))
    sds = jax.ShapeDtypeStruct(w.shape, F32)
    return pl.pallas_call(body, name=name, grid=(layers, rows // tr), in_specs=[blk] * 3 + g_specs,
                          out_specs=(blk,) * 4, out_shape=(sds,) * 4,
                          compiler_params=_cp(("parallel", "parallel")))(w, m, v, *[s[0] for s in sources])


def _adamw(w, g, m, v, name):
    lead = w.shape[:-2]
    assert len(lead) <= 1 and g.shape == w.shape, (name, w.shape, g.shape)
    rows, cols = w.shape[-2:]
    if rows <= 512:
        tr, tc = rows, cols
    elif rows % 256 == 0:
        tr, tc = 256, cols
    else:
        tr, tc = rows, 256
    assert rows % tr == 0 and cols % tc == 0, (name, w.shape)
    c1 = 1.0 - ADAM_B1 ** ADAM_STEP
    c2 = 1.0 - ADAM_B2 ** ADAM_STEP

    def body(w_ref, g_ref, m_ref, v_ref, d_ref, mo_ref, vo_ref):
        gv = g_ref[...]
        mn = ADAM_B1 * m_ref[...] + (1.0 - ADAM_B1) * gv
        vn = ADAM_B2 * v_ref[...] + (1.0 - ADAM_B2) * (gv * gv)
        m_hat = mn / c1
        v_hat = vn / c2
        d_ref[...] = -ADAM_LR * (m_hat / (jnp.sqrt(v_hat) + ADAM_EPS) + ADAM_WD * w_ref[...])
        mo_ref[...] = mn
        vo_ref[...] = vn

    if lead:
        grid = (lead[0], rows // tr, cols // tc)
        blk = pl.BlockSpec((None, tr, tc), lambda l, i, j: (l, i, j))
    else:
        grid = (rows // tr, cols // tc)
        blk = pl.BlockSpec((tr, tc), lambda i, j: (i, j))
    sds = jax.ShapeDtypeStruct(w.shape, F32)
    return pl.pallas_call(body, name=name, grid=grid, in_specs=[blk] * 4, out_specs=(blk,) * 3,
                          out_shape=(sds,) * 3, compiler_params=_cp(("parallel",) * len(grid)))(w, g, m, v)


def _running_sum(x, towards_later):
    n = x.shape[0]
    row = _iota(x.shape, 0)
    s = 1
    while s < n:
        if towards_later:
            x = x + jnp.where(row >= s, pltpu.roll(x, s, 0), 0.0)
        else:
            x = x + jnp.where(row < n - s, pltpu.roll(x, n - s, 0), 0.0)
        s *= 2
    return x


@jax.custom_vjp
def _cumsum_rows(x):
    return _running_sum(x, True)


_cumsum_rows.defvjp(lambda x: (_running_sum(x, True), None), lambda _, g: (_running_sum(g, False),))


def _gla_consts():
    return (_iota((256, 512), 0) // 64 == _iota((256, 512), 1) // 128).astype(F32)


def _gla_chunk(mask, q, k, v, r, aux, s_prev, wa, ba, nw):
    la = _log_sigmoid(bdot(aux, wa, "nn") + ba) * (1.0 / 16.0)
    cum = _cumsum_rows(la)
    total = jnp.sum(la, axis=0, keepdims=True)
    k_dec = k * jnp.exp(total - cum)
    inc = bdot(k_dec, v, "tn") * mask
    dec = jnp.exp(jnp.broadcast_to(total, (128, 256)).T)
    dec = jnp.concatenate([dec, dec, dec, dec], axis=1)
    s_new = dec * s_prev + inc
    o = bdot(q * GLA_SCALE, s_new, "nn")
    parts = []
    for h in range(4):
        oh = o[:, h * 128:(h + 1) * 128]
        parts.append(oh * lax.rsqrt(jnp.mean(oh * oh, axis=-1, keepdims=True) + EPS))
    on = jnp.concatenate(parts, axis=1)
    return s_new, on * nw * (r * _sigmoid(r))


GLA_PER_STEP = 4
GLA_ROWS = GLA_PER_STEP * CHUNK
GLA_STEPS = NCHUNK // GLA_PER_STEP


def _gla_specs(cmap):
    return [pl.BlockSpec((GLA_ROWS, 256), lambda c: (cmap(c), 0)),
            pl.BlockSpec((GLA_ROWS, 256), lambda c: (cmap(c), 1)),
            pl.BlockSpec((GLA_ROWS, 512), lambda c: (cmap(c), 1)),
            pl.BlockSpec((GLA_ROWS, 512), lambda c: (cmap(c), 2)),
            pl.BlockSpec((GLA_ROWS, 128), lambda c: (cmap(c), AUX_BLK))]


def _gla_fwd(proj, wa, ba, nw):
    def body(q_ref, k_ref, v_ref, r_ref, aux_ref, wa_ref, ba_ref, nw_ref, o_ref, sp_ref, s_ref):
        @pl.when(pl.program_id(0) == 0)
        def _():
            s_ref[...] = jnp.zeros_like(s_ref)

        s = s_ref[...]
        consts = _gla_consts()
        outs, states = [], []
        for i in range(GLA_PER_STEP):
            rows = slice(i * CHUNK, (i + 1) * CHUNK)
            states.append(s)
            s, out = _gla_chunk(consts, q_ref[rows, :], k_ref[rows, :], v_ref[rows, :], r_ref[rows, :], aux_ref[rows, :],
                                s, wa_ref[...], ba_ref[...], nw_ref[...])
            outs.append(out)
        s_ref[...] = s
        for i in range(GLA_PER_STEP):
            o_ref[i * CHUNK:(i + 1) * CHUNK, :] = outs[i]
            sp_ref[i] = states[i]

    full = lambda shape: pl.BlockSpec(shape, lambda c: (0,) * len(shape))
    return pl.pallas_call(
        body, name="gla_fwd", grid=(GLA_STEPS,),
        in_specs=_gla_specs(lambda c: c) + [full((128, 256)), full((1, 256)), full((1, 512))],
        out_specs=(pl.BlockSpec((GLA_ROWS, 512), lambda c: (c, 0)),
                   pl.BlockSpec((GLA_PER_STEP, 256, 512), lambda c: (c, 0, 0))),
        out_shape=(jax.ShapeDtypeStruct((T, D), F32), jax.ShapeDtypeStruct((NCHUNK, 256, 512), F32)),
        scratch_shapes=[pltpu.VMEM((256, 512), F32)],
        compiler_params=_cp(("arbitrary",)),
    )(proj, proj, proj, proj, proj, wa, ba, nw)


def _gla_bwd(proj, s_prev_all, wa, ba, nw, dcat):
    rev = lambda c: GLA_STEPS - 1 - c

    def body(q_ref, k_ref, v_ref, r_ref, aux_ref, sp_ref, wa_ref, ba_ref, nw_ref, do_ref,
             dq_ref, dk_ref, dv_ref, dr_ref, daux_ref, dwa_ref, dba_ref, dnw_ref, ds_ref):
        @pl.when(pl.program_id(0) == 0)
        def _():
            ds_ref[...] = jnp.zeros_like(ds_ref)
            dwa_ref[...] = jnp.zeros_like(dwa_ref)
            dba_ref[...] = jnp.zeros_like(dba_ref)
            dnw_ref[...] = jnp.zeros_like(dnw_ref)

        fn = functools.partial(_gla_chunk, _gla_consts())
        ds = ds_ref[...]
        dwa, dba, dnw = dwa_ref[...], dba_ref[...], dnw_ref[...]
        grads = {}
        for i in reversed(range(GLA_PER_STEP)):
            rows = slice(i * CHUNK, (i + 1) * CHUNK)
            _, vjp = jax.vjp(fn, q_ref[rows, :], k_ref[rows, :], v_ref[rows, :], r_ref[rows, :], aux_ref[rows, :],
                             sp_ref[i], wa_ref[...], ba_ref[...], nw_ref[...])
            *grads[i], ds, dwa_i, dba_i, dnw_i = vjp((ds, do_ref[rows, :]))
            dwa, dba, dnw = dwa + dwa_i, dba + dba_i, dnw + dnw_i
        ds_ref[...] = ds
        dwa_ref[...] = dwa
        dba_ref[...] = dba
        dnw_ref[...] = dnw
        for i in range(GLA_PER_STEP):
            rows = slice(i * CHUNK, (i + 1) * CHUNK)
            for ref, g in zip((dq_ref, dk_ref, dv_ref, dr_ref, daux_ref), grads[i]):
                ref[rows, :] = g

    full = lambda shape: pl.BlockSpec(shape, lambda c: (0,) * len(shape))
    blk = lambda w: pl.BlockSpec((GLA_ROWS, w), lambda c: (rev(c), 0))
    sds = lambda *s: jax.ShapeDtypeStruct(s, F32)
    return pl.pallas_call(
        body, name="gla_bwd", grid=(GLA_STEPS,),
        in_specs=_gla_specs(rev) + [pl.BlockSpec((GLA_PER_STEP, 256, 512), lambda c: (rev(c), 0, 0)),
                                    full((128, 256)), full((1, 256)), full((1, 512)), blk(512)],
        out_specs=(blk(256), blk(256), blk(512), blk(512), blk(128), full((128, 256)), full((1, 256)), full((1, 512))),
        out_shape=(sds(T, 256), sds(T, 256), sds(T, 512), sds(T, 512), sds(T, 128),
                   sds(128, 256), sds(1, 256), sds(1, 512)),
        scratch_shapes=[pltpu.VMEM((256, 512), F32)],
        compiler_params=_cp(("arbitrary",)),
    )(proj, proj, proj, proj, proj, s_prev_all, wa, ba, nw, dcat)


def _prefix8(x, towards_later):
    row = _iota(x.shape, 0)
    for s in (1, 2, 4):
        if towards_later:
            keep, shift = row >= s, s
        else:
            keep, shift = row < 8 - s, 8 - s
        x = x + jnp.where(keep, pltpu.roll(x, shift, 0), 0.0)
    return x


def _fox_gate_fwd(proj, bpad):
    def body(aux_ref, b_ref, cum_ref):
        cum_ref[...] = _log_sigmoid(aux_ref[...] + b_ref[...])

        def step(i, carry):
            rows = pl.ds(pl.multiple_of(i * 8, 8), 8)
            cum = _prefix8(cum_ref[rows, :], True) + carry
            cum_ref[rows, :] = cum
            return jnp.broadcast_to(cum[7:, :], (8, 128))

        lax.fori_loop(0, T // 8, step, jnp.zeros((8, 128), F32), unroll=4)

    return pl.pallas_call(
        body, name="fox_gate_fwd", grid=(1,),
        in_specs=[pl.BlockSpec((T, 128), lambda i: (0, AUX_BLK)), pl.BlockSpec((1, 128), lambda i: (0, 0))],
        out_specs=pl.BlockSpec((T, 128), lambda i: (0, 0)),
        out_shape=jax.ShapeDtypeStruct((T, 128), F32),
        compiler_params=_cp(("arbitrary",)),
    )(proj, bpad)


def _fox_gate_bwd(proj, bpad, dccol_t, daux_gla):
    def body(aux_ref, b_ref, dc_ref, dg_ref, daux_ref, db_ref):
        def step(i, carry):
            rows = pl.ds(pl.multiple_of(T - 8 * (i + 1), 8), 8)
            dlf = _prefix8(dc_ref[rows, :], False) + carry
            daux_ref[rows, :] = dlf
            return jnp.broadcast_to(dlf[:1, :], (8, 128))

        lax.fori_loop(0, T // 8, step, jnp.zeros((8, 128), F32), unroll=4)
        dz = daux_ref[...] * _sigmoid(-(aux_ref[...] + b_ref[...]))
        daux_ref[...] = dz + dg_ref[...]
        db_ref[...] = jnp.sum(dz, axis=0, keepdims=True)

    whole = pl.BlockSpec((T, 128), lambda i: (0, 0))
    vec = pl.BlockSpec((1, 128), lambda i: (0, 0))
    return pl.pallas_call(
        body, name="fox_gate_bwd", grid=(1,),
        in_specs=[pl.BlockSpec((T, 128), lambda i: (0, AUX_BLK)), vec, whole, whole],
        out_specs=(whole, vec),
        out_shape=(jax.ShapeDtypeStruct((T, 128), F32), jax.ShapeDtypeStruct((1, 128), F32)),
        compiler_params=_cp(("arbitrary",)),
    )(proj, bpad, dccol_t, daux_gla)


FOX_Q = 256
FOX_QB = T // FOX_Q
FOX_QF = 512


@jax.custom_vjp
def _attend(s, v):
    return _attend_fwd(s, v)[0]


def _attend_fwd(s, v):
    e = jnp.exp(s - jnp.max(s, axis=-1, keepdims=True))
    r = 1.0 / jnp.sum(e, axis=-1, keepdims=True)
    return _dot(e, v, "nn") * r, (e, r, v)


def _attend_bwd(res, do):
    e, r, v = res
    do_r = do * r
    dpr = _dot(do_r, v, "nt")
    ds = e * (dpr - r * jnp.sum(e * dpr, axis=-1, keepdims=True))
    return ds, _dot(e, do_r, "tn").astype(v.dtype)


_attend.defvjp(_attend_fwd, _attend_bwd)


def _fox_block(hp, q, k, v, ccol):
    fq, kl = q.shape[0], k.shape[0]
    lane = _iota((fq, 128), 1)
    tri = jnp.bitwise_and(_iota((2 * fq, fq), 0), fq - 1) >= _iota((2 * fq, fq), 1)
    sub = _iota((8, kl), 0)
    qs = q * ATT_SCALE
    q2 = jnp.concatenate([jnp.where(lane < 64, qs, 0.0), jnp.where(lane >= 64, qs, 0.0)], axis=0)
    s = bdot(q2, k, "nt")
    cs = [jnp.sum(jnp.where(sub == 2 * hp + e, ccol, 0.0), axis=0, keepdims=True) for e in range(2)]
    s = jnp.concatenate([s[:fq] - cs[0], s[fq:] - cs[1]], axis=0)
    diag = jnp.where(tri, s[:, kl - fq:], NEG)
    s = diag if kl == fq else jnp.concatenate([s[:, :kl - fq], diag], axis=1)
    o2 = _attend(s, v)
    return jnp.where(lane < 64, o2[:fq], o2[fq:])


def _fox_in_specs(fq):
    return [pl.BlockSpec((fq, 128), lambda hp, qb: (qb, 12 + hp)),
            pl.BlockSpec((T, 128), lambda hp, qb: (0, 16 + hp)),
            pl.BlockSpec((T, 128), lambda hp, qb: (0, 20 + hp)),
            pl.BlockSpec((8, T), lambda hp, qb: (0, 0))]


def _fox_fwd(proj, cum_c, cat):
    def body(q_ref, k_ref, v_ref, cc_ref, cat_ref, o_ref):
        qb = pl.program_id(1)
        for g in range(T // FOX_QF):
            kl = FOX_QF * (g + 1)

            @pl.when(qb == g)
            def _(kl=kl):
                o_ref[...] = _fox_block(pl.program_id(0), q_ref[...], k_ref[0:kl, :], v_ref[0:kl, :], cc_ref[:, 0:kl])

    return pl.pallas_call(
        body, name="fox_fwd", grid=(4, T // FOX_QF),
        in_specs=_fox_in_specs(FOX_QF) + [pl.BlockSpec(memory_space=pl.ANY)],
        out_specs=pl.BlockSpec((FOX_QF, 128), lambda hp, qb: (qb, 4 + hp)),
        out_shape=jax.ShapeDtypeStruct((T, D), F32), input_output_aliases={4: 0},
        compiler_params=_cp(("parallel", "parallel")),
    )(proj, proj, proj, cum_c, cat)


def _fox_bwd(proj, cum_c, dcat):
    def body(q_ref, k_ref, v_ref, cc_ref, do_ref, dq_ref, dk_ref, dv_ref, dcc_ref):
        qb = pl.program_id(1)

        @pl.when(qb == 0)
        def _():
            dk_ref[...] = jnp.zeros_like(dk_ref)
            dv_ref[...] = jnp.zeros_like(dv_ref)
            dcc_ref[...] = jnp.zeros_like(dcc_ref)

        fn = functools.partial(_fox_block, pl.program_id(0))
        for g in range(FOX_QB):
            kl = FOX_Q * (g + 1)

            @pl.when(qb == g)
            def _(kl=kl):
                _, vjp = jax.vjp(fn, q_ref[...], k_ref[0:kl, :], v_ref[0:kl, :], cc_ref[:, 0:kl])
                dq, dk, dv, dcc = vjp(do_ref[...])
                dq_ref[...] = dq
                dk_ref[0:kl, :] += dk
                dv_ref[0:kl, :] += dv
                dcc_ref[:, 0:kl] += dcc

    sds = lambda *s: jax.ShapeDtypeStruct(s, F32)
    return pl.pallas_call(
        body, name="fox_bwd", grid=(4, FOX_QB),
        in_specs=_fox_in_specs(FOX_Q) + [pl.BlockSpec((FOX_Q, 128), lambda hp, qb: (qb, 4 + hp))],
        out_specs=(pl.BlockSpec((FOX_Q, 128), lambda hp, qb: (qb, hp)),
                   pl.BlockSpec((T, 128), lambda hp, qb: (0, hp)),
                   pl.BlockSpec((T, 128), lambda hp, qb: (0, hp)),
                   pl.BlockSpec((None, 8, T), lambda hp, qb: (hp, 0, 0))),
        out_shape=(sds(T, 512), sds(T, 512), sds(T, 512), sds(4, 8, T)),
        compiler_params=_cp(("parallel", "arbitrary")),
    )(proj, proj, proj, cum_c, dcat)


BIAS_W = 640


def _rel_onehot():
    j = _iota((REL_PAD, BIAS_W), 1)
    rel = jnp.clip(CA_PAD + CHUNK - 1 - j, -128, 128) + 128
    return (_iota((REL_PAD, BIAS_W), 0) == rel).astype(F32)


def _bias_build(rbp):
    def body(rb_ref, o_ref):
        f = _hdot_raw(rb_ref[...], _rel_onehot(), "nn")
        for q in range(CHUNK):
            o_ref[q] = pltpu.roll(f, (BIAS_W - (CHUNK - 1 - q)) % BIAS_W, 1)[:, :CA_BAND]

    return pl.pallas_call(body, name="ca_bias_build", out_shape=jax.ShapeDtypeStruct((CHUNK, 8, CA_BAND), F32))(rbp)


def _bias_grad(dbias_q):
    def body(db_ref, o_ref):
        acc = jnp.zeros((8, BIAS_W), F32)
        for q in range(CHUNK):
            acc = acc + pltpu.roll(db_ref[q], CHUNK - 1 - q, 1)
        o_ref[...] = _hdot_raw(acc, _rel_onehot(), "nt")

    return pl.pallas_call(body, name="ca_bias_grad", out_shape=jax.ShapeDtypeStruct((8, REL_PAD), F32))(dbias_q)


def _ca_block(c, masked, q, kb, vb, bias2):
    lane = _iota((CHUNK, 128), 1)
    qs = q * ATT_SCALE
    q2 = jnp.concatenate([jnp.where(lane < 64, qs, 0.0), jnp.where(lane >= 64, qs, 0.0)], axis=0)
    s = bdot(q2, kb, "nt") + bias2.reshape(2 * CHUNK, CA_BAND)
    if masked:
        s = jnp.where((c * CHUNK - CA_PAD + _iota((2 * CHUNK, CA_BAND), 1)) >= 0, s, NEG)
    o2 = _attend(s, vb)
    return jnp.where(lane < 64, o2[:CHUNK], o2[CHUNK:])


CA_PER_STEP = 16
CA_ROWS = CA_PER_STEP * CHUNK
CA_MASKED_CHUNKS = CA_PAD // CHUNK
assert CA_PER_STEP >= CA_MASKED_CHUNKS
CA_MASKED_STEPS = 1


def _ca_fwd(proj, kvpad, bias):
    def body(q_ref, k_ref, v_ref, b_ref, o_ref):
        def run(masked):
            outs = []
            for i in range(CA_PER_STEP):
                c = pl.program_id(1) * CA_PER_STEP + i
                band = pl.ds(pl.multiple_of(c * CHUNK, CHUNK), CA_BAND)
                rows = slice(i * CHUNK, (i + 1) * CHUNK)
                outs.append(_ca_block(c, masked and i < CA_MASKED_CHUNKS, q_ref[rows, :], k_ref[band, :], v_ref[band, :],
                                      b_ref[...]))
            for i in range(CA_PER_STEP):
                o_ref[i * CHUNK:(i + 1) * CHUNK, :] = outs[i]

        pl.when(pl.program_id(1) < CA_MASKED_STEPS)(lambda: run(True))
        pl.when(pl.program_id(1) >= CA_MASKED_STEPS)(lambda: run(False))

    return pl.pallas_call(
        body, name="ca_fwd", grid=(4, NCHUNK // CA_PER_STEP),
        in_specs=[pl.BlockSpec((CA_ROWS, 128), lambda hp, c: (c, hp)),
                  pl.BlockSpec((T + CA_PAD, 128), lambda hp, c: (0, hp)),
                  pl.BlockSpec((T + CA_PAD, 128), lambda hp, c: (0, 4 + hp)),
                  pl.BlockSpec((2, CHUNK, CA_BAND), lambda hp, c: (hp, 0, 0))],
        out_specs=pl.BlockSpec((CA_ROWS, 128), lambda hp, c: (c, hp)),
        out_shape=jax.ShapeDtypeStruct((T, D), F32),
        compiler_params=_cp(("parallel", "parallel")),
    )(proj, kvpad, kvpad, bias)


def _ca_bwd(proj, kvpad, bias, dcat):
    def body(q_ref, k_ref, v_ref, b_ref, do_ref, dq_ref, dk_ref, dv_ref, db_ref):
        c = pl.program_id(1)

        @pl.when(c == 0)
        def _():
            dk_ref[...] = jnp.zeros_like(dk_ref)
            dv_ref[...] = jnp.zeros_like(dv_ref)
            db_ref[...] = jnp.zeros_like(db_ref)

        def run(masked):
            grads, bands = [], []
            for i in range(CA_PER_STEP):
                ci = c * CA_PER_STEP + i
                band = pl.ds(pl.multiple_of(ci * CHUNK, CHUNK), CA_BAND)
                rows = slice(i * CHUNK, (i + 1) * CHUNK)
                fn = functools.partial(_ca_block, ci, masked and i < CA_MASKED_CHUNKS)
                _, vjp = jax.vjp(fn, q_ref[rows, :], k_ref[band, :], v_ref[band, :], b_ref[...])
                grads.append(vjp(do_ref[rows, :]))
                bands.append(band)
            for i, (dq, _, _, _) in enumerate(grads):
                dq_ref[i * CHUNK:(i + 1) * CHUNK, :] = dq
            for band, (_, dkb, dvb, _) in zip(bands, grads):
                dk_ref[band, :] += dkb
                dv_ref[band, :] += dvb
            db_ref[...] += functools.reduce(lambda a, b: a + b, [g[3] for g in grads])

        pl.when(c < CA_MASKED_STEPS)(lambda: run(True))
        pl.when(c >= CA_MASKED_STEPS)(lambda: run(False))

    sds = lambda *s: jax.ShapeDtypeStruct(s, F32)
    padded = lambda: pl.BlockSpec((T + CA_PAD, 128), lambda hp, c: (0, hp))
    return pl.pallas_call(
        body, name="ca_bwd", grid=(4, NCHUNK // CA_PER_STEP),
        in_specs=[pl.BlockSpec((CA_ROWS, 128), lambda hp, c: (c, hp)),
                  pl.BlockSpec((T + CA_PAD, 128), lambda hp, c: (0, hp)),
                  pl.BlockSpec((T + CA_PAD, 128), lambda hp, c: (0, 4 + hp)),
                  pl.BlockSpec((2, CHUNK, CA_BAND), lambda hp, c: (hp, 0, 0)),
                  pl.BlockSpec((CA_ROWS, 128), lambda hp, c: (c, hp))],
        out_specs=(pl.BlockSpec((CA_ROWS, 128), lambda hp, c: (c, hp)), padded(), padded(),
                   pl.BlockSpec((2, CHUNK, CA_BAND), lambda hp, c: (hp, 0, 0))),
        out_shape=(sds(T, 512), sds(T + CA_PAD, 512), sds(T + CA_PAD, 512), sds(8, CHUNK, CA_BAND)),
        compiler_params=_cp(("parallel", "arbitrary")),
    )(proj, kvpad, kvpad, bias, dcat)


def _block_diag_dot(x, w):
    return jnp.concatenate([bdot(x[:, :256], w[:256, :256], "nn"), bdot(x[:, 256:], w[256:, 256:], "nn")], axis=1)


def _lru_pre(xs, cw, cb, wa, ba, wx, bx, lam):
    xc = cb + xs[0] * cw[0:1, :] + xs[1] * cw[1:2, :] + xs[2] * cw[2:3, :] + xs[3] * cw[3:4, :]
    ra = _sigmoid(_block_diag_dot(xc, wa) + ba)
    ii = _sigmoid(_block_diag_dot(xc, wx) + bx)
    la = 8.0 * ra * _log_sigmoid(lam)
    return jnp.exp(la), jnp.sqrt(-_expm1(2.0 * la)) * (ii * xc)


def _lru_pre_specs():
    full = lambda shape: pl.BlockSpec(shape, lambda i: (0,) * len(shape))
    return [pl.BlockSpec((4, ROWS, 512), lambda i: (0, i, 0)), full((4, 512)), full((1, 512)),
            full((512, 512)), full((1, 512)), full((512, 512)), full((1, 512)), full((1, 512))]


def _lru_pre_fwd(xs, cw, cb, wa, ba, wx, bx, lam):
    def body(xs_ref, cw_ref, cb_ref, wa_ref, ba_ref, wx_ref, bx_ref, lam_ref, a_ref, b_ref):
        a, b = _lru_pre(xs_ref[...], cw_ref[...], cb_ref[...], wa_ref[...], ba_ref[...], wx_ref[...], bx_ref[...],
                        lam_ref[...])
        a_ref[...] = a
        b_ref[...] = b

    row = pl.BlockSpec((ROWS, 512), lambda i: (i, 0))
    sds = jax.ShapeDtypeStruct((T, 512), F32)
    return pl.pallas_call(body, name="lru_pre_fwd", grid=(T // ROWS,), in_specs=_lru_pre_specs(),
                          out_specs=(row, row), out_shape=(sds, sds), compiler_params=_cp(("parallel",)),
                          )(xs, cw, cb, wa, ba, wx, bx, lam)


def _lru_pre_bwd(xs, cw, cb, wa, ba, wx, bx, lam, da, db):
    def body(xs_ref, cw_ref, cb_ref, wa_ref, ba_ref, wx_ref, bx_ref, lam_ref, da_ref, db_ref,
             dxs_ref, dcw_ref, dcb_ref, dwa_ref, dba_ref, dwx_ref, dbx_ref, dlam_ref):
        acc = (dcw_ref, dcb_ref, dwa_ref, dba_ref, dwx_ref, dbx_ref, dlam_ref)

        @pl.when(pl.program_id(0) == 0)
        def _():
            for r in acc:
                r[...] = jnp.zeros_like(r)

        _, vjp = jax.vjp(_lru_pre, xs_ref[...], cw_ref[...], cb_ref[...], wa_ref[...], ba_ref[...], wx_ref[...],
                         bx_ref[...], lam_ref[...])
        grads = vjp((da_ref[...], db_ref[...]))
        dxs_ref[...] = grads[0]
        for r, g in zip(acc, grads[1:]):
            r[...] += g

    row = pl.BlockSpec((ROWS, 512), lambda i: (i, 0))
    specs = _lru_pre_specs()
    sds = lambda *s: jax.ShapeDtypeStruct(s, F32)
    return pl.pallas_call(
        body, name="lru_pre_bwd", grid=(T // ROWS,), in_specs=specs + [row, row], out_specs=tuple(specs),
        out_shape=(sds(4, T, 512), sds(4, 512), sds(1, 512), sds(512, 512), sds(1, 512), sds(512, 512), sds(1, 512),
                   sds(1, 512)),
        compiler_params=_cp(("arbitrary",)),
    )(xs, cw, cb, wa, ba, wx, bx, lam, da, db)


SCAN_ROWS = 8


def _scan8(a, b, towards_later):
    row = _iota((SCAN_ROWS, 512), 0)
    for s in (1, 2, 4):
        if towards_later:
            keep, shift = row >= s, s
        else:
            keep, shift = row < SCAN_ROWS - s, SCAN_ROWS - s
        a_s = jnp.where(keep, pltpu.roll(a, shift, 0), 1.0)
        b_s = jnp.where(keep, pltpu.roll(b, shift, 0), 0.0)
        b = a * b_s + b
        a = a * a_s
    return a, b


def _lru_scan_fwd(a, b):
    def body(a_ref, b_ref, h_ref):
        def step(i, carry):
            rows = pl.ds(pl.multiple_of(i * SCAN_ROWS, SCAN_ROWS), SCAN_ROWS)
            a8, b8 = _scan8(a_ref[rows, :], b_ref[rows, :], True)
            h = a8 * carry + b8
            h_ref[rows, :] = h
            return jnp.broadcast_to(h[SCAN_ROWS - 1:, :], (SCAN_ROWS, 512))

        lax.fori_loop(0, T // SCAN_ROWS, step, jnp.zeros((SCAN_ROWS, 512), F32), unroll=2)

    return pl.pallas_call(body, name="lru_scan_fwd", out_shape=jax.ShapeDtypeStruct((T, 512), F32),
                          compiler_params=pltpu.CompilerParams(vmem_limit_bytes=VMEM_LIMIT))(a, b)


def _lru_scan_bwd(a_next, h_prev, dh):
    def body(a_ref, h_ref, dh_ref, da_ref, db_ref):
        def step(i, carry):
            start = T - SCAN_ROWS * (i + 1)
            rows = pl.ds(pl.multiple_of(start, SCAN_ROWS), SCAN_ROWS)
            a8, b8 = _scan8(a_ref[rows, :], dh_ref[rows, :], False)
            g = a8 * carry + b8
            db_ref[rows, :] = g
            da_ref[rows, :] = g * h_ref[rows, :]
            return jnp.broadcast_to(g[:1, :], (SCAN_ROWS, 512))

        lax.fori_loop(0, T // SCAN_ROWS, step, jnp.zeros((SCAN_ROWS, 512), F32), unroll=2)

    sds = jax.ShapeDtypeStruct((T, 512), F32)
    return pl.pallas_call(body, name="lru_scan_bwd", out_shape=(sds, sds),
                          compiler_params=pltpu.CompilerParams(vmem_limit_bytes=VMEM_LIMIT))(a_next, h_prev, dh)


def _lru_post(h, gate):
    return h * _gelu_tanh(gate)


def _lru_post_fwd(h, proj, cat):
    def body(h_ref, g_ref, cat_ref, o_ref):
        o_ref[...] = _lru_post(h_ref[...], g_ref[...])

    row = pl.BlockSpec((ROWS, 512), lambda i: (i, 0))
    return pl.pallas_call(body, name="lru_post_fwd", grid=(T // ROWS,),
                          in_specs=[row, pl.BlockSpec((ROWS, 512), lambda i: (i, 3)), pl.BlockSpec(memory_space=pl.ANY)],
                          out_specs=pl.BlockSpec((ROWS, 512), lambda i: (i, 1)),
                          out_shape=jax.ShapeDtypeStruct((T, D), F32), input_output_aliases={2: 0},
                          compiler_params=_cp(("parallel",)))(h, proj, cat)


def _lru_post_bwd(h, proj, dcat):
    def body(h_ref, g_ref, do_ref, dh_ref, dg_ref):
        _, vjp = jax.vjp(_lru_post, h_ref[...], g_ref[...])
        dh, dg = vjp(do_ref[...])
        dh_ref[...] = dh
        dg_ref[...] = dg

    row = pl.BlockSpec((ROWS, 512), lambda i: (i, 0))
    sds = jax.ShapeDtypeStruct((T, 512), F32)
    return pl.pallas_call(body, name="lru_post_bwd", grid=(T // ROWS,),
                          in_specs=[row, pl.BlockSpec((ROWS, 512), lambda i: (i, 3)),
                                    pl.BlockSpec((ROWS, 512), lambda i: (i, 1))],
                          out_specs=(row, row), out_shape=(sds, sds), compiler_params=_cp(("parallel",)))(h, proj, dcat)


def _conv_dx(dxs_shift):
    def body(d_ref, o_ref):
        o_ref[...] = d_ref[0] + d_ref[1] + d_ref[2] + d_ref[3]

    row = pl.BlockSpec((ROWS, 512), lambda i: (i, 0))
    return pl.pallas_call(body, name="lru_conv_dx", grid=(T // ROWS,),
                          in_specs=[pl.BlockSpec((4, ROWS, 512), lambda i: (0, i, 0))], out_specs=row,
                          out_shape=jax.ShapeDtypeStruct((T, 512), F32), compiler_params=_cp(("parallel",)))(dxs_shift)


def _position():
    return lax.axis_index("x"), lax.axis_index("y"), lax.axis_index("c")


def _other_chips(x, y):
    return [(1 - x, y), (x, 1 - y), (1 - x, 1 - y)]


def _al(v, n):
    return v * n if isinstance(v, int) else pl.multiple_of(v * n, n)


_AG_ITEMS = [
    ((4, 32, 128), lambda o, s, h: o.at[s, pl.ds(_al(h, 16), 16), :], lambda r, h: r.at[pl.ds(_al(h, 16), 16), :]),
    ((4, 774, 1024), lambda o, s, h: o.at[s, :, pl.ds(_al(h, 512), 512)], lambda r, h: r.at[:, pl.ds(_al(h, 512), 512)]),
    ((1024, 1024), lambda o, s, h: o.at[pl.ds(_al(2 * s + h, 128), 128), :], lambda r, h: r.at[pl.ds(_al(h, 128), 128), :]),
    ((2, 1024, 4096), lambda o, s, h: o.at[h, :, pl.ds(_al(s, 1024), 1024)], lambda r, h: r.at[h]),
    ((2, 4096, 1024), lambda o, s, h: o.at[h, pl.ds(_al(s, 1024), 1024), :], lambda r, h: r.at[h]),
    ((1024, 2560), lambda o, s, h: o.at[pl.ds(_al(h, 512), 512), pl.ds(_al(s, 640), 640)],
     lambda r, h: r.at[pl.ds(_al(h, 512), 512), :]),
    ((1024, 1024), lambda o, s, h: o.at[pl.ds(_al(2 * s + h, 128), 128), :], lambda r, h: r.at[pl.ds(_al(h, 128), 128), :]),
]


_AG_GROUPS = [(0, 1, 2), (3, 4), (5, 6)]

_HBM = pl.BlockSpec(memory_space=pltpu.HBM)
_SEM = pl.BlockSpec(memory_space=pltpu.SEMAPHORE)
_SPLIT = dict(has_side_effects=pltpu.SideEffectType.DATAFLOW_SIDE_EFFECTING)


def _hbm(a):
    return pltpu.with_memory_space_constraint(a, pltpu.HBM)


def _ag_ici_copy(i, j, chip, c, slot, src_ref, land_ref, send_sems, recv_sems, k):
    _, dst, half = _AG_ITEMS[i]
    return pltpu.make_async_remote_copy(src_ref=half(src_ref, c), dst_ref=dst(land_ref, slot, c), send_sem=send_sems.at[k],
                                        recv_sem=recv_sems.at[k], device_id=(*chip, c), device_id_type=MESH)


def _ag_start(groups, shards, name):
    items_all = [i for g in groups for i in _AG_GROUPS[g]]
    n = len(items_all)
    ng = len(groups)
    lands = [lax.empty(_AG_ITEMS[i][0], shards[i].dtype) for i in items_all]

    def body(*refs):
        srcs, land_refs = dict(zip(items_all, refs[:n])), dict(zip(items_all, refs[n:2 * n]))
        sems = refs[2 * n:2 * n + 2 * ng]
        token = refs[-1]
        x, y, c = _position()
        me = 2 * x + y
        for gi, g in enumerate(groups):
            for t, i in enumerate(_AG_GROUPS[g]):
                for j, chip in enumerate(_other_chips(x, y)):
                    _ag_ici_copy(i, j, chip, c, me, srcs[i], land_refs[i], sems[2 * gi], sems[2 * gi + 1], 3 * t + j).start()
        token[...] = jnp.zeros_like(token)

    sem_shapes = []
    for g in groups:
        sem_shapes += [pltpu.SemaphoreType.DMA((3 * len(_AG_GROUPS[g]),))] * 2
    ops = [shards[i] for i in items_all] + lands
    out = pl.pallas_call(
        body, name=name,
        out_shape=tuple(sem_shapes) + tuple(pltpu.HBM(a.shape, a.dtype) for a in ops) + (jax.ShapeDtypeStruct((8, 128), F32),),
        in_specs=(_HBM,) * (2 * n),
        out_specs=(_SEM,) * (2 * ng) + (_HBM,) * (2 * n) + (pl.BlockSpec(memory_space=pltpu.VMEM),),
        input_output_aliases={i: 2 * ng + i for i in range(2 * n)},
        compiler_params=pltpu.CompilerParams(**_SPLIT),
    )(*[_hbm(a) for a in ops])
    sems, thru, token = out[:2 * ng], out[2 * ng:-1], out[-1]
    return ({g: (sems[2 * gi], sems[2 * gi + 1]) for gi, g in enumerate(groups)},
            dict(zip(items_all, thru[:n])), dict(zip(items_all, thru[n:])), token)


def _ag_wait(g, sems, srcs, lands, after):
    items = _AG_GROUPS[g]
    m = len(items)

    def body(*refs):
        src_refs, land_refs = refs[:m], refs[m:2 * m]
        send_sems, recv_sems = refs[2 * m], refs[2 * m + 1]
        x, y, c = _position()
        for t, i in enumerate(items):
            for j, chip in enumerate(_other_chips(x, y)):
                cp = _ag_ici_copy(i, j, chip, c, 2 * chip[0] + chip[1], src_refs[t], land_refs[t], send_sems, recv_sems,
                                  3 * t + j)
                cp.wait_send()
                cp.wait_recv()

    ops = [srcs[i] for i in items] + [lands[i] for i in items]
    out = pl.pallas_call(
        body, name=f"allgather_wait_{g}",
        out_shape=tuple(pltpu.HBM(a.shape, a.dtype) for a in ops),
        in_specs=(_HBM,) * (2 * m) + (_SEM, _SEM, pl.BlockSpec(memory_space=pl.ANY)),
        out_specs=(_HBM,) * (2 * m),
        input_output_aliases={i: i for i in range(2 * m)},
        compiler_params=pltpu.CompilerParams(**_SPLIT),
    )(*ops, sems[0], sems[1], after)
    return list(out[:m]), list(out[m:])


def _ag_forward(g, srcs, lands):
    return _ag_sibling(_AG_GROUPS[g], srcs, lands, False, f"allgather_forward_{g}")


def _ag_push_own(srcs, lands):
    items = tuple(sorted(lands))
    out = _ag_sibling(items, [srcs[i] for i in items], [lands[i] for i in items], True, "allgather_push_own")
    return dict(zip(items, out))


def _ag_sibling(items, srcs, lands, own, name):
    m = len(items)
    per = 2 if own else 3

    def body(*refs):
        src_refs, in_refs, out_refs = refs[:m], refs[m:2 * m], refs[2 * m:3 * m]
        send_sems, recv_sems = refs[3 * m:]
        x, y, c = _position()
        sibling = (x, y, 1 - c)
        me = 2 * x + y
        if own:
            mine = theirs = [(me, 0), (me, 1)]
        else:
            slots = [2 * chip[0] + chip[1] for chip in _other_chips(x, y)]
            mine, theirs = [(s, c) for s in slots], [(s, 1 - c) for s in slots]
        sends = []
        for t, i in enumerate(items):
            _, dst, half = _AG_ITEMS[i]
            for k, (slot, hc) in enumerate(mine):
                src = half(src_refs[t], hc) if own else dst(in_refs[t], slot, hc)
                sends.append(pltpu.make_async_remote_copy(
                    src_ref=src, dst_ref=dst(out_refs[t], slot, hc), send_sem=send_sems.at[per * t + k],
                    recv_sem=recv_sems.at[per * t + k], device_id=sibling, device_id_type=MESH))
        for cp in sends:
            cp.start()
        for t, i in enumerate(items):
            dst = _AG_ITEMS[i][1]
            for k, (slot, hc) in enumerate(theirs):
                there = dst(out_refs[t], slot, hc)
                pltpu.make_async_remote_copy(src_ref=there, dst_ref=there, send_sem=send_sems.at[per * t + k],
                                             recv_sem=recv_sems.at[per * t + k], device_id=sibling,
                                             device_id_type=MESH).wait_recv()
        for cp in sends:
            cp.wait_send()

    any_spec = pl.BlockSpec(memory_space=pl.ANY)
    return pl.pallas_call(
        body, name=name,
        in_specs=[any_spec] * (2 * m), out_specs=(any_spec,) * m,
        out_shape=tuple(jax.ShapeDtypeStruct(a.shape, a.dtype) for a in lands),
        input_output_aliases={m + t: t for t in range(m)},
        scratch_shapes=[pltpu.SemaphoreType.DMA((per * m,)), pltpu.SemaphoreType.DMA((per * m,))],
    )(*srcs, *lands)


def _pair_swap_copy(g_ref, r_ref, send_sem, recv_sem):
    x, y, c = _position()
    hc = g_ref.shape[2] // 2
    return pltpu.make_async_remote_copy(src_ref=g_ref.at[:, :, pl.ds(_al(1 - c, hc), hc)], dst_ref=r_ref,
                                        send_sem=send_sem, recv_sem=recv_sem, device_id=(x, y, 1 - c),
                                        device_id_type=MESH)


def _pair_swap_start(gb, tag):
    _, rows, cols = gb.shape
    recv = lax.empty((4, rows, cols // 2), gb.dtype)

    def body(g_ref, r_ref, send_sem, recv_sem, g_thru, r_thru, token):
        _pair_swap_copy(g_ref, r_ref, send_sem, recv_sem).start()
        token[...] = jnp.zeros_like(token)

    return pl.pallas_call(
        body, name="grad_pair_swap_start_" + tag,
        out_shape=(pltpu.SemaphoreType.DMA(()), pltpu.SemaphoreType.DMA(()), pltpu.HBM(gb.shape, gb.dtype),
                   pltpu.HBM(recv.shape, recv.dtype), jax.ShapeDtypeStruct((8, 128), F32)),
        in_specs=(_HBM, _HBM), out_specs=(_SEM, _SEM, _HBM, _HBM, pl.BlockSpec(memory_space=pltpu.VMEM)),
        input_output_aliases={0: 2, 1: 3},
        compiler_params=pltpu.CompilerParams(**_SPLIT),
    )(_hbm(gb), _hbm(recv))


def _pair_swap_wait(started, after, tag):
    send_sem, recv_sem, gb, recv, _ = started

    def body(g_ref, r_ref, send_sem, recv_sem, after_ref, g_out, r_out):
        cp = _pair_swap_copy(g_ref, r_ref, send_sem, recv_sem)
        cp.wait_send()
        cp.wait_recv()

    return pl.pallas_call(
        body, name="grad_pair_swap_wait_" + tag,
        out_shape=(pltpu.HBM(gb.shape, gb.dtype), pltpu.HBM(recv.shape, recv.dtype)),
        in_specs=(_HBM, _HBM, _SEM, _SEM, pl.BlockSpec(memory_space=pl.ANY)), out_specs=(_HBM, _HBM),
        input_output_aliases={0: 0, 1: 1},
        compiler_params=pltpu.CompilerParams(**_SPLIT),
    )(gb, recv, send_sem, recv_sem, after)


def _handover_copy(r_ref, send_sem, recv_sem, core):
    x, y, c = _position()
    hc = r_ref.shape[1] // 2
    cols = r_ref.at[:, pl.ds(_al(core, hc), hc)]
    return pltpu.make_async_remote_copy(src_ref=cols, dst_ref=cols, send_sem=send_sem, recv_sem=recv_sem,
                                        device_id=(x, y, 1 - c), device_id_type=MESH)


def _handover_start(red, tag):
    def body(r_ref, send_sem, recv_sem, r_thru, token):
        _handover_copy(r_ref, send_sem, recv_sem, lax.axis_index("c")).start()
        token[...] = jnp.zeros_like(token)

    return pl.pallas_call(
        body, name="grad_handover_start_" + tag,
        out_shape=(pltpu.SemaphoreType.DMA(()), pltpu.SemaphoreType.DMA(()), pltpu.HBM(red.shape, red.dtype),
                   jax.ShapeDtypeStruct((8, 128), F32)),
        in_specs=(_HBM,), out_specs=(_SEM, _SEM, _HBM, pl.BlockSpec(memory_space=pltpu.VMEM)),
        input_output_aliases={0: 2},
        compiler_params=pltpu.CompilerParams(**_SPLIT),
    )(_hbm(red))


def _handover_wait(started, after, tag):
    send_sem, recv_sem, red, _ = started

    def body(r_ref, send_sem, recv_sem, after_ref, r_out):
        c = lax.axis_index("c")
        _handover_copy(r_ref, send_sem, recv_sem, c).wait_send()
        _handover_copy(r_ref, send_sem, recv_sem, 1 - c).wait_recv()

    return pl.pallas_call(
        body, name="grad_handover_wait_" + tag,
        out_shape=pltpu.HBM(red.shape, red.dtype),
        in_specs=(_HBM, _SEM, _SEM, pl.BlockSpec(memory_space=pl.ANY)), out_specs=_HBM,
        input_output_aliases={0: 0},
        compiler_params=pltpu.CompilerParams(**_SPLIT),
    )(red, send_sem, recv_sem, after)


def _handover(red, tag):
    started = _handover_start(red, tag)
    return _handover_wait(started, started[3], tag)


def _a2a_copy(j, chip, c, p_ref, q_ref, q_slot, send_sems, recv_sems):
    return pltpu.make_async_remote_copy(src_ref=p_ref.at[2 * chip[0] + chip[1]], dst_ref=q_ref.at[q_slot],
                                        send_sem=send_sems.at[j], recv_sem=recv_sems.at[j], device_id=(*chip, c),
                                        device_id_type=MESH)


def _a2a_start(p, tag):
    def body(p_ref, q_ref, send_sems, recv_sems, p_thru, q_thru, token):
        x, y, c = _position()
        for j, chip in enumerate(_other_chips(x, y)):
            _a2a_copy(j, chip, c, p_ref, q_ref, 2 * x + y, send_sems, recv_sems).start()
        token[...] = jnp.zeros_like(token)

    return pl.pallas_call(
        body, name="grad_alltoall_start_" + tag,
        out_shape=(pltpu.SemaphoreType.DMA((3,)), pltpu.SemaphoreType.DMA((3,)), pltpu.HBM(p.shape, p.dtype),
                   pltpu.HBM(p.shape, p.dtype), jax.ShapeDtypeStruct((8, 128), F32)),
        in_specs=(_HBM, _HBM), out_specs=(_SEM, _SEM, _HBM, _HBM, pl.BlockSpec(memory_space=pltpu.VMEM)),
        input_output_aliases={0: 2, 1: 3},
        compiler_params=pltpu.CompilerParams(**_SPLIT),
    )(_hbm(p), _hbm(lax.empty(p.shape, p.dtype)))


def _a2a_wait(send_sems, recv_sems, p, q, after, tag):
    def body(p_ref, q_ref, send_sems, recv_sems, after_ref, p_out, q_out):
        x, y, c = _position()
        for j, chip in enumerate(_other_chips(x, y)):
            cp = _a2a_copy(j, chip, c, p_ref, q_ref, 2 * chip[0] + chip[1], send_sems, recv_sems)
            cp.wait_send()
            cp.wait_recv()

    return pl.pallas_call(
        body, name="grad_alltoall_wait_" + tag,
        out_shape=(pltpu.HBM(p.shape, p.dtype), pltpu.HBM(q.shape, q.dtype)),
        in_specs=(_HBM, _HBM, _SEM, _SEM, pl.BlockSpec(memory_space=pl.ANY)), out_specs=(_HBM, _HBM),
        input_output_aliases={0: 0, 1: 1},
        compiler_params=pltpu.CompilerParams(**_SPLIT),
    )(p, q, send_sems, recv_sems, after)


def _comm_rows(rows):
    return next(t for t in (1536, 1152, 1024, 512, 384, 256, 128) if rows % t == 0)


def _pair_add(gb, recv, where, tag):
    _, rows, cols = gb.shape
    hc = cols // 2
    tr = _comm_rows(rows)

    def body(w_ref, g_ref, r_ref, o_ref):
        o_ref[...] = (g_ref[...].astype(F32) + r_ref[...].astype(F32)).astype(o_ref.dtype)

    return pl.pallas_call(
        body, name="grad_pair_add_" + tag,
        grid_spec=pltpu.PrefetchScalarGridSpec(
            num_scalar_prefetch=1, grid=(4, rows // tr),
            in_specs=[pl.BlockSpec((None, tr, hc), lambda s, j, w_ref: (s, j, w_ref[0])),
                      pl.BlockSpec((None, tr, hc), lambda s, j, w_ref: (s, j, 0))],
            out_specs=pl.BlockSpec((None, tr, hc), lambda s, j, w_ref: (s, j, 0))),
        out_shape=jax.ShapeDtypeStruct((4, rows, hc), gb.dtype),
        compiler_params=_cp(("parallel", "parallel")),
    )(where, gb, recv)


def _sum_chips(p, q, where, tag):
    _, rows, hc = q.shape
    tr = _comm_rows(rows)

    def body(w_ref, p_ref, qa_ref, qb_ref, qc_ref, o_ref):
        me = w_ref[1]
        own, qa, qb, qc = (r[...].astype(F32) for r in (p_ref, qa_ref, qb_ref, qc_ref))
        v0 = jnp.where(me == 0, own, qa)
        v1 = jnp.where(me == 1, own, jnp.where(me == 0, qa, qb))
        v2 = jnp.where(me == 2, own, jnp.where(me < 2, qb, qc))
        v3 = jnp.where(me == 3, own, qc)
        o_ref[...] = ((v0 + v1) + v2) + v3

    slot = lambda k: pl.BlockSpec((None, tr, hc), lambda j, w_ref: (w_ref[k], j, 0))
    return pl.pallas_call(
        body, name="grad_sum_chips_" + tag,
        grid_spec=pltpu.PrefetchScalarGridSpec(
            num_scalar_prefetch=1, grid=(rows // tr,),
            in_specs=[slot(1), slot(2), slot(3), slot(4)],
            out_specs=pl.BlockSpec((tr, hc), lambda j, w_ref: (j, w_ref[0]))),
        out_shape=jax.ShapeDtypeStruct((rows, 2 * hc), F32),
        compiler_params=_cp(("parallel",)),
    )(where, p, q, q, q)


def _shard_major(g, axis):
    shape = g.shape
    g = g.reshape(shape[:axis] + (4, shape[axis] // 4) + shape[axis + 1:])
    return jnp.moveaxis(g, axis, 0).reshape(4, -1)


def _unshard(g4, shape, axis):
    n = shape[axis] // 4
    g = g4.reshape((4,) + shape[:axis] + (n,) + shape[axis + 1:])
    return jnp.moveaxis(g, 0, axis).reshape(shape)


def _split(flat, shapes):
    out, off = [], 0
    for shp in shapes:
        n = 1
        for d in shp:
            n *= d
        out.append(flat[..., off:off + n].reshape(flat.shape[:-1] + tuple(shp)))
        off += n
    return out


def _even_rows_to_kernel(wt):
    return jnp.concatenate([wt[:1536], wt[1552:3088], wt[1536:1552], wt[3088:3096],
                            jnp.zeros((PE - 3096, wt.shape[1]), wt.dtype)], axis=0)


def _block_diag(w):
    eye = jnp.eye(8, dtype=w.dtype)
    return (w[:, :, None, :] * eye[:, None, :, None]).reshape(512, 512)


def _diag_blocks(g):
    eye = jnp.eye(8, dtype=g.dtype)
    return (g.reshape(8, 64, 8, 64) * eye[:, None, :, None]).sum(axis=2)


def _shift_down(a, s):
    return a if s == 0 else jnp.pad(a, ((s, 0), (0, 0)))[:a.shape[0]]


def _shift_up(a, s):
    return a if s == 0 else jnp.pad(a, ((0, s), (0, 0)))[s:]


SMALL_SHARDED_SHAPES = [(2, 4, 256), (16, 64), (4, 128), (128,), (128,), (128,), (128,)]
REPL_SHAPES = [(256,), (512,), (8,), (8, 257), (8, 64, 64), (8, 64, 64)]


def kernel(x, norm_w, w_in_even, gla_w_a_up, gla_b_a, gla_norm_w, fox_b_f, w_out_even, w_in_odd, rel_bias, conv_w, conv_b, lru_w_a, lru_b_a, lru_w_x, lru_b_x, lru_lambda, w_out_odd, w_mlp_up, w_mlp_down, loss_target, m_norm_w, m_w_in_even, m_gla_w_a_up, m_gla_b_a, m_gla_norm_w, m_fox_b_f, m_w_out_even, m_w_in_odd, m_rel_bias, m_conv_w, m_conv_b, m_lru_w_a, m_lru_b_a, m_lru_w_x, m_lru_b_x, m_lru_lambda, m_w_out_odd, m_w_mlp_up, m_w_mlp_down, v_norm_w, v_w_in_even, v_gla_w_a_up, v_gla_b_a, v_gla_norm_w, v_fox_b_f, v_w_out_even, v_w_in_odd, v_rel_bias, v_conv_w, v_conv_b, v_lru_w_a, v_lru_b_a, v_lru_w_x, v_lru_b_x, v_lru_lambda, v_w_out_odd, v_w_mlp_up, v_w_mlp_down):
    c_idx = lax.axis_index("c")

    small_local = [norm_w, gla_w_a_up[0], conv_w[0], conv_b[0], lru_b_a[0], lru_b_x[0], lru_lambda[0]]
    small_src = jnp.concatenate([a.reshape(-1) for a in small_local]).reshape(32, 128)
    first = {0: small_src, 1: w_in_even[0].T.astype(BF16), 2: w_out_even[0].astype(BF16)}
    sems0, srcs0, lands0, ag_token = _ag_start([0], first, "allgather_start_0")
    zero = ag_token[0, 0]
    later = {3: (w_mlp_up + zero).astype(BF16), 4: (w_mlp_down + zero).astype(BF16),
             5: (w_in_odd[0] + zero).astype(BF16), 6: (w_out_odd[0] + zero).astype(BF16)}
    sems1, srcs1, lands1, ag_token = _ag_start([1, 2], later, "allgather_start_1")
    ag_sems, ag_srcs = {**sems0, **sems1}, {**srcs0, **srcs1}
    ag_lands = _ag_push_own(ag_srcs, {**lands0, **lands1})

    def gathered(g, after):
        srcs_g, lands_g = _ag_wait(g, ag_sems[g], ag_srcs, ag_lands, after)
        return _ag_forward(g, srcs_g, lands_g)

    small4, w_in_e4, w_out_e = gathered(0, ag_token)
    me = 2 * lax.axis_index("x") + lax.axis_index("y")
    others = [k + (k >= me).astype(jnp.int32) for k in range(3)]
    where = jnp.stack([c_idx, me] + others).astype(jnp.int32)

    w_in_e_t = _even_rows_to_kernel(w_in_e4.reshape(3096, D))
    g_small = _split(small4.reshape(4, 32 * 128), SMALL_SHARDED_SHAPES)
    nw_full = _unshard(g_small[0], (2, 4, 1024), 2)
    wa_up = _unshard(g_small[1], (16, 256), 1)
    cw = _unshard(g_small[2], (4, 512), 1)
    cb, lba, lbx, lam = [_unshard(g, (512,), 0).reshape(1, 512) for g in g_small[3:]]
    nw = lambda layer, i: nw_full[layer, i].reshape(1, D)

    wa_pad = jnp.pad(wa_up, ((0, 128 - 16), (0, 0)))
    gla_ba = gla_b_a.reshape(1, 256)
    gla_nw = gla_norm_w.reshape(1, 512)
    fox_bpad = jnp.pad(fox_b_f.reshape(1, 8), ((0, 0), (FOX_LANE0, 128 - FOX_LANE0 - 8)))
    rbp = jnp.pad(rel_bias[0], ((0, 0), (0, REL_PAD - 257)))
    wa_bd = _block_diag(lru_w_a[0])
    wx_bd = _block_diag(lru_w_x[0])

    x0 = x[0]
    tgt = loss_target[0]

    h0 = _prenorm(x0, nw(0, 0), "prenorm_l0_mix")
    proj_e = _mm(h0, w_in_e_t, "nt", tm=2048, tn=640, name="mm_in_even")
    cat0, s_prev = _gla_fwd(proj_e, wa_pad, gla_ba, gla_nw)
    cum_r = _fox_gate_fwd(proj_e, fox_bpad)
    cum_c = cum_r[:, FOX_LANE0:FOX_LANE0 + 8].T
    cat0 = _fox_fwd(proj_e, cum_c, cat0)
    mix0 = _mm(cat0, w_out_e, "nn", tm=2048, tn=512, name="mm_out_even")
    x1, h1 = _post_pre_fwd(x0, mix0, nw(0, 1), nw(0, 2), "post_pre_l0_mix")
    w_up, w_dn = gathered(1, x1)
    a0, r0 = _mm(h1, w_up, "nn", tm=2048, tn=1024, b_layer=0, relu_pair=True, name="mm_up_l0")
    d0 = _mm(a0, w_dn, "nn", tm=1024, tn=512, b_layer=0, name="mm_down_l0")
    x2, h2 = _post_pre_fwd(x1, d0, nw(0, 3), nw(1, 0), "post_pre_l0_mlp")

    w_in_o, w_out_o = gathered(2, x2)
    proj_o = _mm(h2, w_in_o, "nn", tm=2048, tn=640, name="mm_in_odd")
    bias_q = _bias_build(rbp)
    bias = bias_q.transpose(1, 0, 2)
    kvpad = jnp.pad(proj_o[:, 512:1536], ((CA_PAD, 0), (0, 0)))
    cat1 = _ca_fwd(proj_o, kvpad, bias)
    x_in = proj_o[:, 2048:2560]
    xs = jnp.stack([_shift_down(x_in, 3 - j) for j in range(4)])
    lru_a, lru_b = _lru_pre_fwd(xs, cw, cb, wa_bd, lba, wx_bd, lbx, lam)
    hh = _lru_scan_fwd(lru_a, lru_b)
    cat1 = _lru_post_fwd(hh, proj_o, cat1)
    mix1 = _mm(cat1, w_out_o, "nn", tm=2048, tn=512, name="mm_out_odd")
    x3, h3 = _post_pre_fwd(x2, mix1, nw(1, 1), nw(1, 2), "post_pre_l1_mix")
    a1, r1 = _mm(h3, w_up, "nn", tm=2048, tn=1024, b_layer=1, relu_pair=True, name="mm_up_l1")
    d1 = _mm(a1, w_dn, "nn", tm=1024, tn=512, b_layer=1, name="mm_down_l1")
    g4, loss_part, dd1, dnw13 = _post_loss(x3, d1, nw(1, 3), tgt)
    loss = lax.psum(loss_part[0, 0], ("x", "y", "c"))

    def rs_begin(swap, after, tag):
        gb, recv = _pair_swap_wait(swap, after, tag)
        return _a2a_start(_pair_add(gb, recv, where, tag), tag)

    def rs_end(started, after, tag):
        send_sems, recv_sems, p, q, _ = started
        p, q = _a2a_wait(send_sems, recv_sems, p, q, after, tag)
        return _handover(_sum_chips(p, q, where, tag), tag)

    gba = lax.dynamic_update_slice(lax.empty((4, GA_ROWS, D), BF16), jnp.zeros((4, GA_UP - GA_GAP, D), BF16),
                                   (0, GA_GAP, 0))
    gba = _mm(a1, dd1, "tn", tm=512, tn=1024, into=(gba, 1024, GA_DN), name="mm_down_l1_dw")
    du1 = _mm(dd1, w_dn, "nt", tm=2048, tn=1024, b_layer=1, times2=r1, out_dtype=BF16, name="mm_down_l1_dx")
    gba = _mm(du1, h3, "tn", tm=512, tn=1024, into=(gba, 1024, GA_UP), name="mm_up_l1_dw")
    dh3 = _mm(du1, w_up, "nt", tm=1024, tn=512, b_layer=1, name="mm_up_l1_dx")
    g3, dmix1, dnw12, dnw11 = _pre_post_bwd(x3, nw(1, 2), dh3, g4, mix1, nw(1, 1), "pre_post_bwd_l1_mlp")
    gba = _mm(cat1, dmix1, "tn", tm=128, tn=1024, into=(gba, 256, GA_OUT_O), name="mm_out_odd_dw")
    dcat1 = _mm(dmix1, w_out_o, "nt", tm=2048, tn=512, name="mm_out_odd_dx")

    dq_c, dkpad, dvpad, dbias = _ca_bwd(proj_o, kvpad, bias, dcat1)
    g_rel = _bias_grad(jnp.pad(dbias.transpose(1, 0, 2), ((0, 0), (0, 0), (0, BIAS_W - CA_BAND))))[:, :257]
    dhh, dgate = _lru_post_bwd(hh, proj_o, dcat1)
    da_l, db_l = _lru_scan_bwd(_shift_up(lru_a, 1), _shift_down(hh, 1), dhh)
    dxs, g_cw, g_cb, g_wa_bd, g_lba, g_wx_bd, g_lbx, g_lam = _lru_pre_bwd(xs, cw, cb, wa_bd, lba, wx_bd, lbx, lam, da_l, db_l)
    dx_in = _conv_dx(jnp.stack([_shift_up(dxs[j], 3 - j) for j in range(4)]))
    dproj_o = jnp.concatenate([dq_c, dkpad[CA_PAD:], dvpad[CA_PAD:], dgate, dx_in], axis=1).astype(BF16)
    gba = _mm(dproj_o, h2, "tn", tm=128, tn=1024, into=(gba, 640, GA_IN_O), name="mm_in_odd_dw")
    swap_a = _pair_swap_start(gba, "a")
    dh2 = _mm(dproj_o, w_in_o, "nt", tm=1024, tn=512, name="mm_in_odd_dx")
    g2, dd0, dnw10, dnw03 = _pre_post_bwd(x2, nw(1, 0) + swap_a[4][0, 0], dh2, g3, d0, nw(0, 3), "pre_post_bwd_l1_mix")
    rs_a = rs_begin(swap_a, g2, "a")

    gbb = lax.empty((4, GB_ROWS, D), BF16)
    gbb = _mm(a0, dd0, "tn", tm=512, tn=1024, into=(gbb, 1024, GB_DN), name="mm_down_l0_dw")
    du0 = _mm(dd0, w_dn, "nt", tm=2048, tn=1024, b_layer=0, times2=r0, out_dtype=BF16, name="mm_down_l0_dx")
    gbb = _mm(du0, h1, "tn", tm=512, tn=1024, into=(gbb, 1024, GB_UP), name="mm_up_l0_dw")
    swap_b = _pair_swap_start(gbb, "b")
    dh1 = _mm(du0, w_up, "nt", tm=1024, tn=512, b_layer=0, name="mm_up_l0_dx")
    g1, dmix0, dnw02, dnw01 = _pre_post_bwd(x1, nw(0, 2) + (swap_b[4][0, 0] + rs_a[4][0, 0]), dh1, g2, mix0, nw(0, 1),
                                            "pre_post_bwd_l0_mlp")
    rs_b = rs_begin(swap_b, g1, "b")
    gbc = lax.empty((4, GC_ROWS, D), BF16)
    gbc = _mm(cat0, dmix0, "tn", tm=128, tn=1024, into=(gbc, 256, GC_OUT_E), name="mm_out_even_dw")
    dcat0 = _mm(dmix0, w_out_e, "nt", tm=2048, tn=512, name="mm_out_even_dx")

    dq_g, dk_g, dv_g, dr_g, daux_g, g_wa_pad, g_gla_ba, g_gla_nw = _gla_bwd(
        proj_e, s_prev, wa_pad, gla_ba, gla_nw + rs_b[4][0, 0], dcat0)
    dq_f, dk_f, dv_f, dccol = _fox_bwd(proj_e, cum_c, dcat0)
    dccol_t = jnp.pad(dccol.sum(axis=0).T, ((0, 0), (FOX_LANE0, 128 - FOX_LANE0 - 8)))
    daux, g_fox_bpad = _fox_gate_bwd(proj_e, fox_bpad, dccol_t, daux_g)
    dproj_e = jnp.concatenate([dq_g, dk_g, dv_g, dr_g, dq_f, dk_f, dv_f, daux], axis=1).astype(BF16)
    gt_in_e = _mm(dproj_e, h0, "tn", tm=640, tn=1024, out_dtype=BF16, name="mm_in_even_dw")
    dh0 = _mm(dproj_e, w_in_e_t, "nn", tm=1024, tn=512, name="mm_in_even_dx")
    grad_x, dnw00 = _norm_bwd(x0, nw(0, 0), dh0, g1, "prenorm_l0_mix_bwd")

    def rs_reduce(started, after, tag):
        send_sems, recv_sems, p, q, _ = started
        p, q = _a2a_wait(send_sems, recv_sems, p, q, after, tag)
        return _handover_start(_sum_chips(p, q, where, tag), tag)

    ho_a = rs_reduce(rs_a, grad_x, "a")
    ho_b = rs_reduce(rs_b, ho_a[3], "b")

    g_norm = jnp.stack([jnp.concatenate([dnw00, dnw01, dnw02, dnw03]), jnp.concatenate([dnw10, dnw11, dnw12, dnw13])])
    sharded = [(g_norm, 2), (g_wa_pad[:16], 1), (g_cw, 1), (g_cb[0], 0), (g_lba[0], 0), (g_lbx[0], 0), (g_lam[0], 0)]
    replicated = [g_gla_ba[0], g_gla_nw[0], g_fox_bpad[0, FOX_LANE0:FOX_LANE0 + 8], g_rel, _diag_blocks(g_wa_bd),
                  _diag_blocks(g_wx_bd)]
    small4 = jnp.concatenate([_shard_major(g, ax) for g, ax in sharded]
                             + [jnp.broadcast_to(g.reshape(1, -1), (4, g.size)) for g in replicated], axis=1)
    n_small = small4.shape[1]
    small_rows = GC_ROWS - GC_TAIL - 774
    small4 = jnp.pad(small4, ((0, 0), (0, small_rows * D - n_small))).reshape(4, small_rows, D)
    gt_rows = jnp.concatenate([gt_in_e[:1536], gt_in_e[3072:3088], gt_in_e[1536:3072], gt_in_e[3088:3096]], axis=0)
    tail = jnp.concatenate([gt_rows.reshape(4, 774, D), small4.astype(BF16)], axis=1)
    gbc = lax.dynamic_update_slice(gbc, tail, (0, GC_TAIL, 0))
    swap_c = _pair_swap_start(gbc, "c")
    rs_c = rs_begin(swap_c, swap_c[4], "c")

    red_a = _handover_wait(ho_a, rs_c[4], "a")
    red_b = _handover_wait(ho_b, red_a, "b")
    early = dict(
        w_mlp_up=_adamw_from(w_mlp_up, m_w_mlp_up, v_w_mlp_up, [(red_b, GB_UP, True), (red_a, GA_UP, True)], 512,
                             "adamw_w_mlp_up"),
        w_mlp_down=_adamw_from(w_mlp_down, m_w_mlp_down, v_w_mlp_down, [(red_b, GB_DN, False), (red_a, GA_DN, False)],
                               512, "adamw_w_mlp_down"),
        w_in_odd=_adamw_from(w_in_odd, m_w_in_odd, v_w_in_odd, [(red_a, GA_IN_O, True)], 256, "adamw_w_in_odd"),
        w_out_odd=_adamw_from(w_out_odd, m_w_out_odd, v_w_out_odd, [(red_a, GA_OUT_O, False)], 128, "adamw_w_out_odd"))
    red_c = rs_end(rs_c, early["w_out_odd"][3], "c")

    g_small = _split(red_c[GC_TAIL + 774:].reshape(-1)[:n_small], SMALL_SHARDED_SHAPES + REPL_SHAPES)
    g_of = dict(zip(["norm_w", "gla_w_a_up", "conv_w", "conv_b", "lru_b_a", "lru_b_x", "lru_lambda", "gla_b_a",
                     "gla_norm_w", "fox_b_f", "rel_bias", "lru_w_a", "lru_w_x"], g_small))
    g_of.update(w_in_even=red_c[GC_TAIL:GC_TAIL + 774])
    early["w_out_even"] = _adamw_from(w_out_even, m_w_out_even, v_w_out_even, [(red_c, GC_OUT_E, False)], 256,
                                      "adamw_w_out_even")

    names = ["norm_w", "w_in_even", "gla_w_a_up", "gla_b_a", "gla_norm_w", "fox_b_f", "w_out_even", "w_in_odd", "rel_bias",
             "conv_w", "conv_b", "lru_w_a", "lru_b_a", "lru_w_x", "lru_b_x", "lru_lambda", "w_out_odd", "w_mlp_up",
             "w_mlp_down"]
    w_of = dict(norm_w=norm_w, w_in_even=w_in_even, gla_w_a_up=gla_w_a_up, gla_b_a=gla_b_a, gla_norm_w=gla_norm_w,
                fox_b_f=fox_b_f, w_out_even=w_out_even, w_in_odd=w_in_odd, rel_bias=rel_bias, conv_w=conv_w, conv_b=conv_b,
                lru_w_a=lru_w_a, lru_b_a=lru_b_a, lru_w_x=lru_w_x, lru_b_x=lru_b_x, lru_lambda=lru_lambda,
                w_out_odd=w_out_odd, w_mlp_up=w_mlp_up, w_mlp_down=w_mlp_down)
    m_of = dict(norm_w=m_norm_w, w_in_even=m_w_in_even, gla_w_a_up=m_gla_w_a_up, gla_b_a=m_gla_b_a,
                gla_norm_w=m_gla_norm_w, fox_b_f=m_fox_b_f, w_out_even=m_w_out_even, w_in_odd=m_w_in_odd,
                rel_bias=m_rel_bias, conv_w=m_conv_w, conv_b=m_conv_b, lru_w_a=m_lru_w_a, lru_b_a=m_lru_b_a,
                lru_w_x=m_lru_w_x, lru_b_x=m_lru_b_x, lru_lambda=m_lru_lambda, w_out_odd=m_w_out_odd,
                w_mlp_up=m_w_mlp_up, w_mlp_down=m_w_mlp_down)
    v_of = dict(norm_w=v_norm_w, w_in_even=v_w_in_even, gla_w_a_up=v_gla_w_a_up, gla_b_a=v_gla_b_a,
                gla_norm_w=v_gla_norm_w, fox_b_f=v_fox_b_f, w_out_even=v_w_out_even, w_in_odd=v_w_in_odd,
                rel_bias=v_rel_bias, conv_w=v_conv_w, conv_b=v_conv_b, lru_w_a=v_lru_w_a, lru_b_a=v_lru_b_a,
                lru_w_x=v_lru_w_x, lru_b_x=v_lru_b_x, lru_lambda=v_lru_lambda, w_out_odd=v_w_out_odd,
                w_mlp_up=v_w_mlp_up, w_mlp_down=v_w_mlp_down)
    grads, deltas, new_ms, new_vs = [], [], [], []
    for n in names:
        w = w_of[n]
        if n in early:
            g, d, mn, vn = early[n]
            grads.append(g)
            deltas.append(d)
            new_ms.append(mn)
            new_vs.append(vn)
            continue
        if n == "w_in_even":
            to_view = lambda a: a[0].T
            from_view = lambda a: a.T[None]
        else:
            view = w.shape if w.ndim <= 3 else w.shape[-3:]
            to_view = lambda a, view=view: a.reshape(view)
            from_view = lambda a, w=w: a.reshape(w.shape)
        g = g_of[n] if n == "w_in_even" else to_view(g_of[n])
        d, mn, vn = _adamw(to_view(w), g, to_view(m_of[n]), to_view(v_of[n]), "adamw_" + n)
        grads.append(from_view(g))
        deltas.append(from_view(d))
        new_ms.append(from_view(mn))
        new_vs.append(from_view(vn))

    return (loss, grad_x.reshape(1, T, D), *grads, *deltas, *new_ms, *new_vs)
```

```python
import functools

import jax
import jax.numpy as jnp
from jax import lax
from jax.experimental import pallas as pl
from jax.experimental.pallas import tpu as pltpu

F32 = jnp.float32
BF16 = jnp.bfloat16
MESH = pl.DeviceIdType.MESH

T = 2048
D = 1024
DFF = 4096
EPS = 1e-6
CHUNK = 64
NCHUNK = T // CHUNK
PE = 3200
PO = 2560
AUX_BLK = 3072 // 128
FOX_LANE0 = 16
GLA_SCALE = 64 ** -0.5
ATT_SCALE = 64 ** -0.5
NEG = float(jnp.finfo(jnp.float32).min)
CA_BAND = 576
CA_PAD = 512
REL_PAD = 384

VMEM_LIMIT = 48 * 1024 * 1024

ADAM_LR, ADAM_B1, ADAM_B2, ADAM_EPS, ADAM_WD, ADAM_STEP = 0.001, 0.9, 0.999, 1e-08, 0.01, 10

GA_ROWS, GA_IN_O, GA_OUT_O, GA_GAP, GA_UP, GA_DN = 3072, 0, 640, 896, 1024, 2048
GB_ROWS, GB_UP, GB_DN = 2048, 0, 1024
GC_ROWS, GC_OUT_E, GC_TAIL = 1152, 0, 256

_DIMS = {"nn": (((1,), (0,)), ((), ())), "nt": (((1,), (1,)), ((), ())), "tn": (((0,), (0,)), ((), ()))}


def _cp(sem, **kw):
    return pltpu.CompilerParams(dimension_semantics=sem, vmem_limit_bytes=VMEM_LIMIT, **kw)


def _dot(a, b, mode):
    return lax.dot_general(a.astype(BF16), b.astype(BF16), _DIMS[mode], preferred_element_type=F32)


@functools.partial(jax.custom_vjp, nondiff_argnums=(2,))
def bdot(a, b, mode):
    return _dot(a, b, mode)


def _bdot_fwd(a, b, mode):
    return _dot(a, b, mode), (a, b)


def _bdot_bwd(mode, res, g):
    a, b = res
    if mode == "nn":
        da, db = _dot(g, b, "nt"), _dot(a, g, "tn")
    elif mode == "nt":
        da, db = _dot(g, b, "nn"), _dot(g, a, "tn")
    else:
        da, db = _dot(b, g, "nt"), _dot(a, g, "nn")
    return da.astype(a.dtype), db.astype(b.dtype)


bdot.defvjp(_bdot_fwd, _bdot_bwd)


def _hdot_raw(a, b, mode):
    return lax.dot_general(a, b, _DIMS[mode], precision=lax.Precision.HIGHEST, preferred_element_type=F32)


def _log_sigmoid(x):
    return jnp.minimum(x, 0.0) - jnp.log(1.0 + jnp.exp(-jnp.abs(x)))


def _sigmoid(x):
    return 1.0 / (1.0 + jnp.exp(-x))


def _expm1(x):
    series = x * (1.0 + x * 0.5 * (1.0 + x * (1.0 / 3.0) * (1.0 + x * 0.25)))
    return jnp.where(jnp.abs(x) < 0.03, series, jnp.exp(x) - 1.0)


def _gelu_tanh(x):
    return 0.5 * x * (1.0 + jnp.tanh(0.7978845608028654 * (x + 0.044715 * x * x * x)))


def _iota(shape, dim):
    return lax.broadcasted_iota(jnp.int32, shape, dim)


def _mm(a, b, mode, *, tm, tn, tk=None, out_dtype=F32, name, b_layer=None, into=None, relu_pair=False, times2=None):
    b2 = b.shape[-2:]
    if mode == "nn":
        (m, k), n = a.shape, b2[1]
    elif mode == "nt":
        (m, k), n = a.shape, b2[0]
    else:
        (k, m), n = a.shape, b2[1]
    tk = k if tk is None else tk
    assert m % tm == 0 and n % tn == 0 and k % tk == 0, (name, a.shape, b.shape)
    nk = k // tk
    if mode == "tn":
        a_spec = pl.BlockSpec((tk, tm), lambda i, j, kk: (kk, i))
    elif m == tm and nk == 1:
        a_spec = pl.BlockSpec((tm, tk), lambda i, j, kk: (i, kk), pipeline_mode=pl.Buffered(1))
    else:
        a_spec = pl.BlockSpec((tm, tk), lambda i, j, kk: (i, kk))
    b_blk = {"nn": (tk, tn), "nt": (tn, tk), "tn": (tk, tn)}[mode]
    b_idx = {"nn": lambda i, j, kk: (kk, j), "nt": lambda i, j, kk: (j, kk), "tn": lambda i, j, kk: (kk, j)}[mode]
    if b_layer is None:
        b_spec = pl.BlockSpec(b_blk, b_idx)
    else:
        b_spec = pl.BlockSpec((None,) + b_blk, lambda i, j, kk: (b_layer,) + b_idx(i, j, kk))

    tile = pl.BlockSpec((tm, tn), lambda i, j, kk: (i, j))
    if into is not None:
        buf, per_slot, row_off = into
        assert m == 4 * per_slot and per_slot % tm == 0 and row_off % tm == 0 and buf.shape[2] == n, (name, buf.shape)
        bps = per_slot // tm
        out_specs = pl.BlockSpec((None, tm, tn), lambda i, j, kk: (i // bps, row_off // tm + i % bps, j))
        out_shape = jax.ShapeDtypeStruct(buf.shape, buf.dtype)
        extra_in, extra_specs, aliases = [buf], [pl.BlockSpec(memory_space=pl.ANY)], {2: 0}
        finish = lambda acc, extra: [acc.astype(buf.dtype)]
    elif relu_pair:
        out_specs = (tile, tile)
        out_shape = (jax.ShapeDtypeStruct((m, n), BF16),) * 2
        extra_in, extra_specs, aliases = [], [], {}

        def finish(acc, extra):
            r = jnp.maximum(acc, 0.0)
            return [(r * r).astype(BF16), r.astype(BF16)]
    elif times2 is not None:
        out_specs = tile
        out_shape = jax.ShapeDtypeStruct((m, n), out_dtype)
        extra_in, extra_specs, aliases = [times2], [tile], {}
        finish = lambda acc, extra: [(acc * (2.0 * extra[...].astype(F32))).astype(out_dtype)]
    else:
        out_specs = tile
        out_shape = jax.ShapeDtypeStruct((m, n), out_dtype)
        extra_in, extra_specs, aliases = [], [], {}
        finish = lambda acc, extra: [acc.astype(out_dtype)]
    n_out = 2 if relu_pair else 1

    def body(*refs):
        a_ref, b_ref = refs[0], refs[1]
        extra = refs[2] if extra_in else None
        o_refs = refs[2 + len(extra_in):2 + len(extra_in) + n_out]

        def store(acc):
            for o_ref, val in zip(o_refs, finish(acc, extra)):
                o_ref[...] = val

        if nk == 1:
            store(_dot(a_ref[...], b_ref[...], mode))
            return
        acc_ref = refs[-1]
        kk = pl.program_id(2)

        @pl.when(kk == 0)
        def _():
            acc_ref[...] = jnp.zeros_like(acc_ref)

        acc_ref[...] += _dot(a_ref[...], b_ref[...], mode)

        @pl.when(kk == nk - 1)
        def _():
            store(acc_ref[...])

    return pl.pallas_call(
        body, name=name, grid=(m // tm, n // tn, nk),
        in_specs=[a_spec, b_spec] + extra_specs,
        out_specs=out_specs, out_shape=out_shape,
        scratch_shapes=[pltpu.VMEM((tm, tn), F32)] if nk > 1 else [],
        input_output_aliases=aliases,
        compiler_params=_cp(("parallel", "parallel", "arbitrary")),
    )(a, b, *extra_in)


ROWS = 512


def _prenorm(x, w, name):
    def body(x_ref, w_ref, o_ref):
        xv = x_ref[...]
        r = lax.rsqrt(jnp.mean(xv * xv, axis=-1, keepdims=True) + EPS)
        o_ref[...] = (xv * r * w_ref[...]).astype(BF16)

    return pl.pallas_call(
        body, name=name, grid=(T // ROWS,),
        in_specs=[pl.BlockSpec((ROWS, D), lambda i: (i, 0)), pl.BlockSpec((1, D), lambda i: (0, 0))],
        out_specs=pl.BlockSpec((ROWS, D), lambda i: (i, 0)),
        out_shape=jax.ShapeDtypeStruct((T, D), BF16),
        compiler_params=_cp(("parallel",)),
    )(x, w)


def _rms(z):
    return lax.rsqrt(jnp.mean(z * z, axis=-1, keepdims=True) + EPS)


def _rms_bwd(z, w, dy):
    r = _rms(z)
    wdy = dy * w
    dz = r * wdy - z * (r * r * r) * jnp.mean(z * wdy, axis=-1, keepdims=True)
    return dz, jnp.sum(dy * z * r, axis=0, keepdims=True)


_ROW = pl.BlockSpec((ROWS, D), lambda i: (i, 0))
_VEC = pl.BlockSpec((1, D), lambda i: (0, 0))


def _post_pre_fwd(x, z, w_post, w_pre, name):
    def body(x_ref, z_ref, wp_ref, wn_ref, x_out, h_out):
        zv = z_ref[...]
        xn = x_ref[...] + zv * _rms(zv) * wp_ref[...]
        x_out[...] = xn
        h_out[...] = (xn * _rms(xn) * wn_ref[...]).astype(BF16)

    return pl.pallas_call(
        body, name=name, grid=(T // ROWS,), in_specs=[_ROW, _ROW, _VEC, _VEC], out_specs=(_ROW, _ROW),
        out_shape=(jax.ShapeDtypeStruct((T, D), F32), jax.ShapeDtypeStruct((T, D), BF16)),
        compiler_params=_cp(("parallel",)),
    )(x, z, w_post, w_pre)


def _post_loss(x, z, w_post, tgt):
    def body(x_ref, z_ref, w_ref, t_ref, g_ref, l_ref, dz_ref, dw_ref):
        @pl.when(pl.program_id(0) == 0)
        def _():
            l_ref[...] = jnp.zeros_like(l_ref)
            dw_ref[...] = jnp.zeros_like(dw_ref)

        zv = z_ref[...]
        e = x_ref[...] + zv * _rms(zv) * w_ref[...] - t_ref[...]
        g = e * (1.0 / D)
        g_ref[...] = g
        l_ref[...] += jnp.sum(e * e) * (0.5 / D)
        dz, dw = _rms_bwd(zv, w_ref[...], g)
        dz_ref[...] = dz.astype(BF16)
        dw_ref[...] += dw

    return pl.pallas_call(
        body, name="postnorm_loss", grid=(T // ROWS,), in_specs=[_ROW, _ROW, _VEC, _ROW],
        out_specs=(_ROW, pl.BlockSpec((1, 128), lambda i: (0, 0)), _ROW, _VEC),
        out_shape=(jax.ShapeDtypeStruct((T, D), F32), jax.ShapeDtypeStruct((1, 128), F32),
                   jax.ShapeDtypeStruct((T, D), BF16), jax.ShapeDtypeStruct((1, D), F32)),
        compiler_params=_cp(("arbitrary",)),
    )(x, z, w_post, tgt)


def _pre_post_bwd(x, w_pre, dh, add, z, w_post, name):
    def body(x_ref, wn_ref, dh_ref, add_ref, z_ref, wp_ref, g_ref, dz_ref, dwn_ref, dwp_ref):
        @pl.when(pl.program_id(0) == 0)
        def _():
            dwn_ref[...] = jnp.zeros_like(dwn_ref)
            dwp_ref[...] = jnp.zeros_like(dwp_ref)

        dx, dwn = _rms_bwd(x_ref[...], wn_ref[...], dh_ref[...])
        g = dx + add_ref[...]
        g_ref[...] = g
        dz, dwp = _rms_bwd(z_ref[...], wp_ref[...], g)
        dz_ref[...] = dz.astype(BF16)
        dwn_ref[...] += dwn
        dwp_ref[...] += dwp

    return pl.pallas_call(
        body, name=name, grid=(T // ROWS,), in_specs=[_ROW, _VEC, _ROW, _ROW, _ROW, _VEC],
        out_specs=(_ROW, _ROW, _VEC, _VEC),
        out_shape=(jax.ShapeDtypeStruct((T, D), F32), jax.ShapeDtypeStruct((T, D), BF16),
                   jax.ShapeDtypeStruct((1, D), F32), jax.ShapeDtypeStruct((1, D), F32)),
        compiler_params=_cp(("arbitrary",)),
    )(x, w_pre, dh, add, z, w_post)


def _norm_bwd(z, w, dy, add, name):
    has_add = add is not None

    def body(*refs):
        if has_add:
            z_ref, w_ref, dy_ref, add_ref, dz_ref, dw_ref = refs
        else:
            z_ref, w_ref, dy_ref, dz_ref, dw_ref = refs
        i = pl.program_id(0)

        @pl.when(i == 0)
        def _():
            dw_ref[...] = jnp.zeros_like(dw_ref)

        zv = z_ref[...].astype(F32)
        dyv = dy_ref[...]
        r = lax.rsqrt(jnp.mean(zv * zv, axis=-1, keepdims=True) + EPS)
        wdy = dyv * w_ref[...]
        dz = r * wdy - zv * (r * r * r) * jnp.mean(zv * wdy, axis=-1, keepdims=True)
        if has_add:
            dz = dz + add_ref[...]
        dz_ref[...] = dz.astype(dz_ref.dtype)
        dw_ref[...] += jnp.sum(dyv * zv * r, axis=0, keepdims=True)

    row = pl.BlockSpec((ROWS, D), lambda i: (i, 0))
    vec = pl.BlockSpec((1, D), lambda i: (0, 0))
    ins = [z, w, dy] + ([add] if has_add else [])
    dz_dtype = F32 if has_add else BF16
    return pl.pallas_call(
        body, name=name, grid=(T // ROWS,),
        in_specs=[row, vec, row] + ([row] if has_add else []),
        out_specs=(row, vec),
        out_shape=(jax.ShapeDtypeStruct((T, D), dz_dtype), jax.ShapeDtypeStruct((1, D), F32)),
        compiler_params=_cp(("arbitrary",)),
    )(*ins)


def _adamw_math(w, g, m, v):
    c1 = 1.0 - ADAM_B1 ** ADAM_STEP
    c2 = 1.0 - ADAM_B2 ** ADAM_STEP
    mn = ADAM_B1 * m + (1.0 - ADAM_B1) * g
    vn = ADAM_B2 * v + (1.0 - ADAM_B2) * (g * g)
    return -ADAM_LR * ((mn / c1) / (jnp.sqrt(vn / c2) + ADAM_EPS) + ADAM_WD * w), mn, vn


def _adamw_from(w, m, v, sources, tr, name):
    layers, rows, cols = w.shape
    assert len(sources) == layers and rows % tr == 0, (name, w.shape)
    g_specs = []
    for layer, (buf, row0, transposed) in enumerate(sources):
        step = lambda l, i, layer=layer: jnp.where(l == layer, i, 0)
        if transposed:
            assert row0 % cols == 0 and buf.shape[1] == rows, (name, row0)
            g_specs.append(pl.BlockSpec((cols, tr), lambda l, i, b=row0 // cols, step=step: (b, step(l, i))))
        else:
            assert row0 % tr == 0 and buf.shape[1] == cols, (name, row0)
            g_specs.append(pl.BlockSpec((tr, cols), lambda l, i, b=row0 // tr, step=step: (b + step(l, i), 0)))

    def body(*refs):
        w_ref, m_ref, v_ref = refs[:3]
        g_refs = refs[3:3 + layers]
        g_out, d_ref, mo_ref, vo_ref = refs[3 + layers:]
        gs = [r[...].T if src[2] else r[...] for r, src in zip(g_refs, sources)]
        g = gs[0] if layers == 1 else jnp.where(pl.program_id(0) == 0, gs[0], gs[1])
        g_out[...] = g
        d_ref[...], mo_ref[...], vo_ref[...] = _adamw_math(w_ref[...], g, m_ref[...], v_ref[...])

    blk = pl.BlockSpec((None, tr, cols), lambda l, i: (l, i, 0))
    sds = jax.ShapeDtypeStruct(w.shape, F32)
    return pl.pallas_call(body, name=name, grid=(layers, rows // tr), in_specs=[blk] * 3 + g_specs,
                          out_specs=(blk,) * 4, out_shape=(sds,) * 4,
                          compiler_params=_cp(("parallel", "parallel")))(w, m, v, *[s[0] for s in sources])


def _adamw(w, g, m, v, name):
    lead = w.shape[:-2]
    assert len(lead) <= 1 and g.shape == w.shape, (name, w.shape, g.shape)
    rows, cols = w.shape[-2:]
    if rows <= 512:
        tr, tc = rows, cols
    elif rows % 256 == 0:
        tr, tc = 256, cols
    else:
        tr, tc = rows, 256
    assert rows % tr == 0 and cols % tc == 0, (name, w.shape)
    c1 = 1.0 - ADAM_B1 ** ADAM_STEP
    c2 = 1.0 - ADAM_B2 ** ADAM_STEP

    def body(w_ref, g_ref, m_ref, v_ref, d_ref, mo_ref, vo_ref):
        gv = g_ref[...]
        mn = ADAM_B1 * m_ref[...] + (1.0 - ADAM_B1) * gv
        vn = ADAM_B2 * v_ref[...] + (1.0 - ADAM_B2) * (gv * gv)
        m_hat = mn / c1
        v_hat = vn / c2
        d_ref[...] = -ADAM_LR * (m_hat / (jnp.sqrt(v_hat) + ADAM_EPS) + ADAM_WD * w_ref[...])
        mo_ref[...] = mn
        vo_ref[...] = vn

    if lead:
        grid = (lead[0], rows // tr, cols // tc)
        blk = pl.BlockSpec((None, tr, tc), lambda l, i, j: (l, i, j))
    else:
        grid = (rows // tr, cols // tc)
        blk = pl.BlockSpec((tr, tc), lambda i, j: (i, j))
    sds = jax.ShapeDtypeStruct(w.shape, F32)
    return pl.pallas_call(body, name=name, grid=grid, in_specs=[blk] * 4, out_specs=(blk,) * 3,
                          out_shape=(sds,) * 3, compiler_params=_cp(("parallel",) * len(grid)))(w, g, m, v)


def _running_sum(x, towards_later):
    n = x.shape[0]
    row = _iota(x.shape, 0)
    s = 1
    while s < n:
        if towards_later:
            x = x + jnp.where(row >= s, pltpu.roll(x, s, 0), 0.0)
        else:
            x = x + jnp.where(row < n - s, pltpu.roll(x, n - s, 0), 0.0)
        s *= 2
    return x


@jax.custom_vjp
def _cumsum_rows(x):
    return _running_sum(x, True)


_cumsum_rows.defvjp(lambda x: (_running_sum(x, True), None), lambda _, g: (_running_sum(g, False),))


def _gla_consts():
    return (_iota((256, 512), 0) // 64 == _iota((256, 512), 1) // 128).astype(F32)


def _gla_chunk(mask, q, k, v, r, aux, s_prev, wa, ba, nw):
    la = _log_sigmoid(bdot(aux, wa, "nn") + ba) * (1.0 / 16.0)
    cum = _cumsum_rows(la)
    total = jnp.sum(la, axis=0, keepdims=True)
    k_dec = k * jnp.exp(total - cum)
    inc = bdot(k_dec, v, "tn") * mask
    dec = jnp.exp(jnp.broadcast_to(total, (128, 256)).T)
    dec = jnp.concatenate([dec, dec, dec, dec], axis=1)
    s_new = dec * s_prev + inc
    o = bdot(q * GLA_SCALE, s_new, "nn")
    parts = []
    for h in range(4):
        oh = o[:, h * 128:(h + 1) * 128]
        parts.append(oh * lax.rsqrt(jnp.mean(oh * oh, axis=-1, keepdims=True) + EPS))
    on = jnp.concatenate(parts, axis=1)
    return s_new, on * nw * (r * _sigmoid(r))


GLA_PER_STEP = 4
GLA_ROWS = GLA_PER_STEP * CHUNK
GLA_STEPS = NCHUNK // GLA_PER_STEP


def _gla_specs(cmap):
    return [pl.BlockSpec((GLA_ROWS, 256), lambda c: (cmap(c), 0)),
            pl.BlockSpec((GLA_ROWS, 256), lambda c: (cmap(c), 1)),
            pl.BlockSpec((GLA_ROWS, 512), lambda c: (cmap(c), 1)),
            pl.BlockSpec((GLA_ROWS, 512), lambda c: (cmap(c), 2)),
            pl.BlockSpec((GLA_ROWS, 128), lambda c: (cmap(c), AUX_BLK))]


def _gla_fwd(proj, wa, ba, nw):
    def body(q_ref, k_ref, v_ref, r_ref, aux_ref, wa_ref, ba_ref, nw_ref, o_ref, sp_ref, s_ref):
        @pl.when(pl.program_id(0) == 0)
        def _():
            s_ref[...] = jnp.zeros_like(s_ref)

        s = s_ref[...]
        consts = _gla_consts()
        outs, states = [], []
        for i in range(GLA_PER_STEP):
            rows = slice(i * CHUNK, (i + 1) * CHUNK)
            states.append(s)
            s, out = _gla_chunk(consts, q_ref[rows, :], k_ref[rows, :], v_ref[rows, :], r_ref[rows, :], aux_ref[rows, :],
                                s, wa_ref[...], ba_ref[...], nw_ref[...])
            outs.append(out)
        s_ref[...] = s
        for i in range(GLA_PER_STEP):
            o_ref[i * CHUNK:(i + 1) * CHUNK, :] = outs[i]
            sp_ref[i] = states[i]

    full = lambda shape: pl.BlockSpec(shape, lambda c: (0,) * len(shape))
    return pl.pallas_call(
        body, name="gla_fwd", grid=(GLA_STEPS,),
        in_specs=_gla_specs(lambda c: c) + [full((128, 256)), full((1, 256)), full((1, 512))],
        out_specs=(pl.BlockSpec((GLA_ROWS, 512), lambda c: (c, 0)),
                   pl.BlockSpec((GLA_PER_STEP, 256, 512), lambda c: (c, 0, 0))),
        out_shape=(jax.ShapeDtypeStruct((T, D), F32), jax.ShapeDtypeStruct((NCHUNK, 256, 512), F32)),
        scratch_shapes=[pltpu.VMEM((256, 512), F32)],
        compiler_params=_cp(("arbitrary",)),
    )(proj, proj, proj, proj, proj, wa, ba, nw)


def _gla_bwd(proj, s_prev_all, wa, ba, nw, dcat):
    rev = lambda c: GLA_STEPS - 1 - c

    def body(q_ref, k_ref, v_ref, r_ref, aux_ref, sp_ref, wa_ref, ba_ref, nw_ref, do_ref,
             dq_ref, dk_ref, dv_ref, dr_ref, daux_ref, dwa_ref, dba_ref, dnw_ref, ds_ref):
        @pl.when(pl.program_id(0) == 0)
        def _():
            ds_ref[...] = jnp.zeros_like(ds_ref)
            dwa_ref[...] = jnp.zeros_like(dwa_ref)
            dba_ref[...] = jnp.zeros_like(dba_ref)
            dnw_ref[...] = jnp.zeros_like(dnw_ref)

        fn = functools.partial(_gla_chunk, _gla_consts())
        ds = ds_ref[...]
        dwa, dba, dnw = dwa_ref[...], dba_ref[...], dnw_ref[...]
        grads = {}
        for i in reversed(range(GLA_PER_STEP)):
            rows = slice(i * CHUNK, (i + 1) * CHUNK)
            _, vjp = jax.vjp(fn, q_ref[rows, :], k_ref[rows, :], v_ref[rows, :], r_ref[rows, :], aux_ref[rows, :],
                             sp_ref[i], wa_ref[...], ba_ref[...], nw_ref[...])
            *grads[i], ds, dwa_i, dba_i, dnw_i = vjp((ds, do_ref[rows, :]))
            dwa, dba, dnw = dwa + dwa_i, dba + dba_i, dnw + dnw_i
        ds_ref[...] = ds
        dwa_ref[...] = dwa
        dba_ref[...] = dba
        dnw_ref[...] = dnw
        for i in range(GLA_PER_STEP):
            rows = slice(i * CHUNK, (i + 1) * CHUNK)
            for ref, g in zip((dq_ref, dk_ref, dv_ref, dr_ref, daux_ref), grads[i]):
                ref[rows, :] = g

    full = lambda shape: pl.BlockSpec(shape, lambda c: (0,) * len(shape))
    blk = lambda w: pl.BlockSpec((GLA_ROWS, w), lambda c: (rev(c), 0))
    sds = lambda *s: jax.ShapeDtypeStruct(s, F32)
    return pl.pallas_call(
        body, name="gla_bwd", grid=(GLA_STEPS,),
        in_specs=_gla_specs(rev) + [pl.BlockSpec((GLA_PER_STEP, 256, 512), lambda c: (rev(c), 0, 0)),
                                    full((128, 256)), full((1, 256)), full((1, 512)), blk(512)],
        out_specs=(blk(256), blk(256), blk(512), blk(512), blk(128), full((128, 256)), full((1, 256)), full((1, 512))),
        out_shape=(sds(T, 256), sds(T, 256), sds(T, 512), sds(T, 512), sds(T, 128),
                   sds(128, 256), sds(1, 256), sds(1, 512)),
        scratch_shapes=[pltpu.VMEM((256, 512), F32)],
        compiler_params=_cp(("arbitrary",)),
    )(proj, proj, proj, proj, proj, s_prev_all, wa, ba, nw, dcat)


def _prefix8(x, towards_later):
    row = _iota(x.shape, 0)
    for s in (1, 2, 4):
        if towards_later:
            keep, shift = row >= s, s
        else:
            keep, shift = row < 8 - s, 8 - s
        x = x + jnp.where(keep, pltpu.roll(x, shift, 0), 0.0)
    return x


def _fox_gate_fwd(proj, bpad):
    def body(aux_ref, b_ref, cum_ref):
        cum_ref[...] = _log_sigmoid(aux_ref[...] + b_ref[...])

        def step(i, carry):
            rows = pl.ds(pl.multiple_of(i * 8, 8), 8)
            cum = _prefix8(cum_ref[rows, :], True) + carry
            cum_ref[rows, :] = cum
            return jnp.broadcast_to(cum[7:, :], (8, 128))

        lax.fori_loop(0, T // 8, step, jnp.zeros((8, 128), F32), unroll=4)

    return pl.pallas_call(
        body, name="fox_gate_fwd", grid=(1,),
        in_specs=[pl.BlockSpec((T, 128), lambda i: (0, AUX_BLK)), pl.BlockSpec((1, 128), lambda i: (0, 0))],
        out_specs=pl.BlockSpec((T, 128), lambda i: (0, 0)),
        out_shape=jax.ShapeDtypeStruct((T, 128), F32),
        compiler_params=_cp(("arbitrary",)),
    )(proj, bpad)


def _fox_gate_bwd(proj, bpad, dccol_t, daux_gla):
    def body(aux_ref, b_ref, dc_ref, dg_ref, daux_ref, db_ref):
        def step(i, carry):
            rows = pl.ds(pl.multiple_of(T - 8 * (i + 1), 8), 8)
            dlf = _prefix8(dc_ref[rows, :], False) + carry
            daux_ref[rows, :] = dlf
            return jnp.broadcast_to(dlf[:1, :], (8, 128))

        lax.fori_loop(0, T // 8, step, jnp.zeros((8, 128), F32), unroll=4)
        dz = daux_ref[...] * _sigmoid(-(aux_ref[...] + b_ref[...]))
        daux_ref[...] = dz + dg_ref[...]
        db_ref[...] = jnp.sum(dz, axis=0, keepdims=True)

    whole = pl.BlockSpec((T, 128), lambda i: (0, 0))
    vec = pl.BlockSpec((1, 128), lambda i: (0, 0))
    return pl.pallas_call(
        body, name="fox_gate_bwd", grid=(1,),
        in_specs=[pl.BlockSpec((T, 128), lambda i: (0, AUX_BLK)), vec, whole, whole],
        out_specs=(whole, vec),
        out_shape=(jax.ShapeDtypeStruct((T, 128), F32), jax.ShapeDtypeStruct((1, 128), F32)),
        compiler_params=_cp(("arbitrary",)),
    )(proj, bpad, dccol_t, daux_gla)


FOX_Q = 256
FOX_QB = T // FOX_Q
FOX_QF = 512


@jax.custom_vjp
def _attend(s, v):
    return _attend_fwd(s, v)[0]


def _attend_fwd(s, v):
    e = jnp.exp(s - jnp.max(s, axis=-1, keepdims=True))
    r = 1.0 / jnp.sum(e, axis=-1, keepdims=True)
    return _dot(e, v, "nn") * r, (e, r, v)


def _attend_bwd(res, do):
    e, r, v = res
    do_r = do * r
    dpr = _dot(do_r, v, "nt")
    ds = e * (dpr - r * jnp.sum(e * dpr, axis=-1, keepdims=True))
    return ds, _dot(e, do_r, "tn").astype(v.dtype)


_attend.defvjp(_attend_fwd, _attend_bwd)


def _fox_block(hp, q, k, v, ccol):
    fq, kl = q.shape[0], k.shape[0]
    lane = _iota((fq, 128), 1)
    tri = jnp.bitwise_and(_iota((2 * fq, fq), 0), fq - 1) >= _iota((2 * fq, fq), 1)
    sub = _iota((8, kl), 0)
    qs = q * ATT_SCALE
    q2 = jnp.concatenate([jnp.where(lane < 64, qs, 0.0), jnp.where(lane >= 64, qs, 0.0)], axis=0)
    s = bdot(q2, k, "nt")
    cs = [jnp.sum(jnp.where(sub == 2 * hp + e, ccol, 0.0), axis=0, keepdims=True) for e in range(2)]
    s = jnp.concatenate([s[:fq] - cs[0], s[fq:] - cs[1]], axis=0)
    diag = jnp.where(tri, s[:, kl - fq:], NEG)
    s = diag if kl == fq else jnp.concatenate([s[:, :kl - fq], diag], axis=1)
    o2 = _attend(s, v)
    return jnp.where(lane < 64, o2[:fq], o2[fq:])


def _fox_in_specs(fq):
    return [pl.BlockSpec((fq, 128), lambda hp, qb: (qb, 12 + hp)),
            pl.BlockSpec((T, 128), lambda hp, qb: (0, 16 + hp)),
            pl.BlockSpec((T, 128), lambda hp, qb: (0, 20 + hp)),
            pl.BlockSpec((8, T), lambda hp, qb: (0, 0))]


def _fox_fwd(proj, cum_c, cat):
    def body(q_ref, k_ref, v_ref, cc_ref, cat_ref, o_ref):
        qb = pl.program_id(1)
        for g in range(T // FOX_QF):
            kl = FOX_QF * (g + 1)

            @pl.when(qb == g)
            def _(kl=kl):
                o_ref[...] = _fox_block(pl.program_id(0), q_ref[...], k_ref[0:kl, :], v_ref[0:kl, :], cc_ref[:, 0:kl])

    return pl.pallas_call(
        body, name="fox_fwd", grid=(4, T // FOX_QF),
        in_specs=_fox_in_specs(FOX_QF) + [pl.BlockSpec(memory_space=pl.ANY)],
        out_specs=pl.BlockSpec((FOX_QF, 128), lambda hp, qb: (qb, 4 + hp)),
        out_shape=jax.ShapeDtypeStruct((T, D), F32), input_output_aliases={4: 0},
        compiler_params=_cp(("parallel", "parallel")),
    )(proj, proj, proj, cum_c, cat)


def _fox_bwd(proj, cum_c, dcat):
    def body(q_ref, k_ref, v_ref, cc_ref, do_ref, dq_ref, dk_ref, dv_ref, dcc_ref):
        qb = pl.program_id(1)

        @pl.when(qb == 0)
        def _():
            dk_ref[...] = jnp.zeros_like(dk_ref)
            dv_ref[...] = jnp.zeros_like(dv_ref)
            dcc_ref[...] = jnp.zeros_like(dcc_ref)

        fn = functools.partial(_fox_block, pl.program_id(0))
        for g in range(FOX_QB):
            kl = FOX_Q * (g + 1)

            @pl.when(qb == g)
            def _(kl=kl):
                _, vjp = jax.vjp(fn, q_ref[...], k_ref[0:kl, :], v_ref[0:kl, :], cc_ref[:, 0:kl])
                dq, dk, dv, dcc = vjp(do_ref[...])
                dq_ref[...] = dq
                dk_ref[0:kl, :] += dk
                dv_ref[0:kl, :] += dv
                dcc_ref[:, 0:kl] += dcc

    sds = lambda *s: jax.ShapeDtypeStruct(s, F32)
    return pl.pallas_call(
        body, name="fox_bwd", grid=(4, FOX_QB),
        in_specs=_fox_in_specs(FOX_Q) + [pl.BlockSpec((FOX_Q, 128), lambda hp, qb: (qb, 4 + hp))],
        out_specs=(pl.BlockSpec((FOX_Q, 128), lambda hp, qb: (qb, hp)),
                   pl.BlockSpec((T, 128), lambda hp, qb: (0, hp)),
                   pl.BlockSpec((T, 128), lambda hp, qb: (0, hp)),
                   pl.BlockSpec((None, 8, T), lambda hp, qb: (hp, 0, 0))),
        out_shape=(sds(T, 512), sds(T, 512), sds(T, 512), sds(4, 8, T)),
        compiler_params=_cp(("parallel", "arbitrary")),
    )(proj, proj, proj, cum_c, dcat)


BIAS_W = 640


def _rel_onehot():
    j = _iota((REL_PAD, BIAS_W), 1)
    rel = jnp.clip(CA_PAD + CHUNK - 1 - j, -128, 128) + 128
    return (_iota((REL_PAD, BIAS_W), 0) == rel).astype(F32)


def _bias_build(rbp):
    def body(rb_ref, o_ref):
        f = _hdot_raw(rb_ref[...], _rel_onehot(), "nn")
        for q in range(CHUNK):
            o_ref[q] = pltpu.roll(f, (BIAS_W - (CHUNK - 1 - q)) % BIAS_W, 1)[:, :CA_BAND]

    return pl.pallas_call(body, name="ca_bias_build", out_shape=jax.ShapeDtypeStruct((CHUNK, 8, CA_BAND), F32))(rbp)


def _bias_grad(dbias_q):
    def body(db_ref, o_ref):
        acc = jnp.zeros((8, BIAS_W), F32)
        for q in range(CHUNK):
            acc = acc + pltpu.roll(db_ref[q], CHUNK - 1 - q, 1)
        o_ref[...] = _hdot_raw(acc, _rel_onehot(), "nt")

    return pl.pallas_call(body, name="ca_bias_grad", out_shape=jax.ShapeDtypeStruct((8, REL_PAD), F32))(dbias_q)


def _ca_block(c, masked, q, kb, vb, bias2):
    lane = _iota((CHUNK, 128), 1)
    qs = q * ATT_SCALE
    q2 = jnp.concatenate([jnp.where(lane < 64, qs, 0.0), jnp.where(lane >= 64, qs, 0.0)], axis=0)
    s = bdot(q2, kb, "nt") + bias2.reshape(2 * CHUNK, CA_BAND)
    if masked:
        s = jnp.where((c * CHUNK - CA_PAD + _iota((2 * CHUNK, CA_BAND), 1)) >= 0, s, NEG)
    o2 = _attend(s, vb)
    return jnp.where(lane < 64, o2[:CHUNK], o2[CHUNK:])


CA_PER_STEP = 16
CA_ROWS = CA_PER_STEP * CHUNK
CA_MASKED_CHUNKS = CA_PAD // CHUNK
assert CA_PER_STEP >= CA_MASKED_CHUNKS
CA_MASKED_STEPS = 1


def _ca_fwd(proj, kvpad, bias):
    def body(q_ref, k_ref, v_ref, b_ref, o_ref):
        def run(masked):
            outs = []
            for i in range(CA_PER_STEP):
                c = pl.program_id(1) * CA_PER_STEP + i
                band = pl.ds(pl.multiple_of(c * CHUNK, CHUNK), CA_BAND)
                rows = slice(i * CHUNK, (i + 1) * CHUNK)
                outs.append(_ca_block(c, masked and i < CA_MASKED_CHUNKS, q_ref[rows, :], k_ref[band, :], v_ref[band, :],
                                      b_ref[...]))
            for i in range(CA_PER_STEP):
                o_ref[i * CHUNK:(i + 1) * CHUNK, :] = outs[i]

        pl.when(pl.program_id(1) < CA_MASKED_STEPS)(lambda: run(True))
        pl.when(pl.program_id(1) >= CA_MASKED_STEPS)(lambda: run(False))

    return pl.pallas_call(
        body, name="ca_fwd", grid=(4, NCHUNK // CA_PER_STEP),
        in_specs=[pl.BlockSpec((CA_ROWS, 128), lambda hp, c: (c, hp)),
                  pl.BlockSpec((T + CA_PAD, 128), lambda hp, c: (0, hp)),
                  pl.BlockSpec((T + CA_PAD, 128), lambda hp, c: (0, 4 + hp)),
                  pl.BlockSpec((2, CHUNK, CA_BAND), lambda hp, c: (hp, 0, 0))],
        out_specs=pl.BlockSpec((CA_ROWS, 128), lambda hp, c: (c, hp)),
        out_shape=jax.ShapeDtypeStruct((T, D), F32),
        compiler_params=_cp(("parallel", "parallel")),
    )(proj, kvpad, kvpad, bias)


def _ca_bwd(proj, kvpad, bias, dcat):
    def body(q_ref, k_ref, v_ref, b_ref, do_ref, dq_ref, dk_ref, dv_ref, db_ref):
        c = pl.program_id(1)

        @pl.when(c == 0)
        def _():
            dk_ref[...] = jnp.zeros_like(dk_ref)
            dv_ref[...] = jnp.zeros_like(dv_ref)
            db_ref[...] = jnp.zeros_like(db_ref)

        def run(masked):
            grads, bands = [], []
            for i in range(CA_PER_STEP):
                ci = c * CA_PER_STEP + i
                band = pl.ds(pl.multiple_of(ci * CHUNK, CHUNK), CA_BAND)
                rows = slice(i * CHUNK, (i + 1) * CHUNK)
                fn = functools.partial(_ca_block, ci, masked and i < CA_MASKED_CHUNKS)
                _, vjp = jax.vjp(fn, q_ref[rows, :], k_ref[band, :], v_ref[band, :], b_ref[...])
                grads.append(vjp(do_ref[rows, :]))
                bands.append(band)
            for i, (dq, _, _, _) in enumerate(grads):
                dq_ref[i * CHUNK:(i + 1) * CHUNK, :] = dq
            for band, (_, dkb, dvb, _) in zip(bands, grads):
                dk_ref[band, :] += dkb
                dv_ref[band, :] += dvb
            db_ref[...] += functools.reduce(lambda a, b: a + b, [g[3] for g in grads])

        pl.when(c < CA_MASKED_STEPS)(lambda: run(True))
        pl.when(c >= CA_MASKED_STEPS)(lambda: run(False))

    sds = lambda *s: jax.ShapeDtypeStruct(s, F32)
    padded = lambda: pl.BlockSpec((T + CA_PAD, 128), lambda hp, c: (0, hp))
    return pl.pallas_call(
        body, name="ca_bwd", grid=(4, NCHUNK // CA_PER_STEP),
        in_specs=[pl.BlockSpec((CA_ROWS, 128), lambda hp, c: (c, hp)),
                  pl.BlockSpec((T + CA_PAD, 128), lambda hp, c: (0, hp)),
                  pl.BlockSpec((T + CA_PAD, 128), lambda hp, c: (0, 4 + hp)),
                  pl.BlockSpec((2, CHUNK, CA_BAND), lambda hp, c: (hp, 0, 0)),
                  pl.BlockSpec((CA_ROWS, 128), lambda hp, c: (c, hp))],
        out_specs=(pl.BlockSpec((CA_ROWS, 128), lambda hp, c: (c, hp)), padded(), padded(),
                   pl.BlockSpec((2, CHUNK, CA_BAND), lambda hp, c: (hp, 0, 0))),
        out_shape=(sds(T, 512), sds(T + CA_PAD, 512), sds(T + CA_PAD, 512), sds(8, CHUNK, CA_BAND)),
        compiler_params=_cp(("parallel", "arbitrary")),
    )(proj, kvpad, kvpad, bias, dcat)


def _block_diag_dot(x, w):
    return jnp.concatenate([bdot(x[:, :256], w[:256, :256], "nn"), bdot(x[:, 256:], w[256:, 256:], "nn")], axis=1)


def _lru_pre(xs, cw, cb, wa, ba, wx, bx, lam):
    xc = cb + xs[0] * cw[0:1, :] + xs[1] * cw[1:2, :] + xs[2] * cw[2:3, :] + xs[3] * cw[3:4, :]
    ra = _sigmoid(_block_diag_dot(xc, wa) + ba)
    ii = _sigmoid(_block_diag_dot(xc, wx) + bx)
    la = 8.0 * ra * _log_sigmoid(lam)
    return jnp.exp(la), jnp.sqrt(-_expm1(2.0 * la)) * (ii * xc)


def _lru_pre_specs():
    full = lambda shape: pl.BlockSpec(shape, lambda i: (0,) * len(shape))
    return [pl.BlockSpec((4, ROWS, 512), lambda i: (0, i, 0)), full((4, 512)), full((1, 512)),
            full((512, 512)), full((1, 512)), full((512, 512)), full((1, 512)), full((1, 512))]


def _lru_pre_fwd(xs, cw, cb, wa, ba, wx, bx, lam):
    def body(xs_ref, cw_ref, cb_ref, wa_ref, ba_ref, wx_ref, bx_ref, lam_ref, a_ref, b_ref):
        a, b = _lru_pre(xs_ref[...], cw_ref[...], cb_ref[...], wa_ref[...], ba_ref[...], wx_ref[...], bx_ref[...],
                        lam_ref[...])
        a_ref[...] = a
        b_ref[...] = b

    row = pl.BlockSpec((ROWS, 512), lambda i: (i, 0))
    sds = jax.ShapeDtypeStruct((T, 512), F32)
    return pl.pallas_call(body, name="lru_pre_fwd", grid=(T // ROWS,), in_specs=_lru_pre_specs(),
                          out_specs=(row, row), out_shape=(sds, sds), compiler_params=_cp(("parallel",)),
                          )(xs, cw, cb, wa, ba, wx, bx, lam)


def _lru_pre_bwd(xs, cw, cb, wa, ba, wx, bx, lam, da, db):
    def body(xs_ref, cw_ref, cb_ref, wa_ref, ba_ref, wx_ref, bx_ref, lam_ref, da_ref, db_ref,
             dxs_ref, dcw_ref, dcb_ref, dwa_ref, dba_ref, dwx_ref, dbx_ref, dlam_ref):
        acc = (dcw_ref, dcb_ref, dwa_ref, dba_ref, dwx_ref, dbx_ref, dlam_ref)

        @pl.when(pl.program_id(0) == 0)
        def _():
            for r in acc:
                r[...] = jnp.zeros_like(r)

        _, vjp = jax.vjp(_lru_pre, xs_ref[...], cw_ref[...], cb_ref[...], wa_ref[...], ba_ref[...], wx_ref[...],
                         bx_ref[...], lam_ref[...])
        grads = vjp((da_ref[...], db_ref[...]))
        dxs_ref[...] = grads[0]
        for r, g in zip(acc, grads[1:]):
            r[...] += g

    row = pl.BlockSpec((ROWS, 512), lambda i: (i, 0))
    specs = _lru_pre_specs()
    sds = lambda *s: jax.ShapeDtypeStruct(s, F32)
    return pl.pallas_call(
        body, name="lru_pre_bwd", grid=(T // ROWS,), in_specs=specs + [row, row], out_specs=tuple(specs),
        out_shape=(sds(4, T, 512), sds(4, 512), sds(1, 512), sds(512, 512), sds(1, 512), sds(512, 512), sds(1, 512),
                   sds(1, 512)),
        compiler_params=_cp(("arbitrary",)),
    )(xs, cw, cb, wa, ba, wx, bx, lam, da, db)


SCAN_ROWS = 8


def _scan8(a, b, towards_later):
    row = _iota((SCAN_ROWS, 512), 0)
    for s in (1, 2, 4):
        if towards_later:
            keep, shift = row >= s, s
        else:
            keep, shift = row < SCAN_ROWS - s, SCAN_ROWS - s
        a_s = jnp.where(keep, pltpu.roll(a, shift, 0), 1.0)
        b_s = jnp.where(keep, pltpu.roll(b, shift, 0), 0.0)
        b = a * b_s + b
        a = a * a_s
    return a, b


def _lru_scan_fwd(a, b):
    def body(a_ref, b_ref, h_ref):
        def step(i, carry):
            rows = pl.ds(pl.multiple_of(i * SCAN_ROWS, SCAN_ROWS), SCAN_ROWS)
            a8, b8 = _scan8(a_ref[rows, :], b_ref[rows, :], True)
            h = a8 * carry + b8
            h_ref[rows, :] = h
            return jnp.broadcast_to(h[SCAN_ROWS - 1:, :], (SCAN_ROWS, 512))

        lax.fori_loop(0, T // SCAN_ROWS, step, jnp.zeros((SCAN_ROWS, 512), F32), unroll=2)

    return pl.pallas_call(body, name="lru_scan_fwd", out_shape=jax.ShapeDtypeStruct((T, 512), F32),
                          compiler_params=pltpu.CompilerParams(vmem_limit_bytes=VMEM_LIMIT))(a, b)


def _lru_scan_bwd(a_next, h_prev, dh):
    def body(a_ref, h_ref, dh_ref, da_ref, db_ref):
        def step(i, carry):
            start = T - SCAN_ROWS * (i + 1)
            rows = pl.ds(pl.multiple_of(start, SCAN_ROWS), SCAN_ROWS)
            a8, b8 = _scan8(a_ref[rows, :], dh_ref[rows, :], False)
            g = a8 * carry + b8
            db_ref[rows, :] = g
            da_ref[rows, :] = g * h_ref[rows, :]
            return jnp.broadcast_to(g[:1, :], (SCAN_ROWS, 512))

        lax.fori_loop(0, T // SCAN_ROWS, step, jnp.zeros((SCAN_ROWS, 512), F32), unroll=2)

    sds = jax.ShapeDtypeStruct((T, 512), F32)
    return pl.pallas_call(body, name="lru_scan_bwd", out_shape=(sds, sds),
                          compiler_params=pltpu.CompilerParams(vmem_limit_bytes=VMEM_LIMIT))(a_next, h_prev, dh)


def _lru_post(h, gate):
    return h * _gelu_tanh(gate)


def _lru_post_fwd(h, proj, cat):
    def body(h_ref, g_ref, cat_ref, o_ref):
        o_ref[...] = _lru_post(h_ref[...], g_ref[...])

    row = pl.BlockSpec((ROWS, 512), lambda i: (i, 0))
    return pl.pallas_call(body, name="lru_post_fwd", grid=(T // ROWS,),
                          in_specs=[row, pl.BlockSpec((ROWS, 512), lambda i: (i, 3)), pl.BlockSpec(memory_space=pl.ANY)],
                          out_specs=pl.BlockSpec((ROWS, 512), lambda i: (i, 1)),
                          out_shape=jax.ShapeDtypeStruct((T, D), F32), input_output_aliases={2: 0},
                          compiler_params=_cp(("parallel",)))(h, proj, cat)


def _lru_post_bwd(h, proj, dcat):
    def body(h_ref, g_ref, do_ref, dh_ref, dg_ref):
        _, vjp = jax.vjp(_lru_post, h_ref[...], g_ref[...])
        dh, dg = vjp(do_ref[...])
        dh_ref[...] = dh
        dg_ref[...] = dg

    row = pl.BlockSpec((ROWS, 512), lambda i: (i, 0))
    sds = jax.ShapeDtypeStruct((T, 512), F32)
    return pl.pallas_call(body, name="lru_post_bwd", grid=(T // ROWS,),
                          in_specs=[row, pl.BlockSpec((ROWS, 512), lambda i: (i, 3)),
                                    pl.BlockSpec((ROWS, 512), lambda i: (i, 1))],
                          out_specs=(row, row), out_shape=(sds, sds), compiler_params=_cp(("parallel",)))(h, proj, dcat)


def _conv_dx(dxs_shift):
    def body(d_ref, o_ref):
        o_ref[...] = d_ref[0] + d_ref[1] + d_ref[2] + d_ref[3]

    row = pl.BlockSpec((ROWS, 512), lambda i: (i, 0))
    return pl.pallas_call(body, name="lru_conv_dx", grid=(T // ROWS,),
                          in_specs=[pl.BlockSpec((4, ROWS, 512), lambda i: (0, i, 0))], out_specs=row,
                          out_shape=jax.ShapeDtypeStruct((T, 512), F32), compiler_params=_cp(("parallel",)))(dxs_shift)


def _position():
    return lax.axis_index("x"), lax.axis_index("y"), lax.axis_index("c")


def _other_chips(x, y):
    return [(1 - x, y), (x, 1 - y), (1 - x, 1 - y)]


def _al(v, n):
    return v * n if isinstance(v, int) else pl.multiple_of(v * n, n)


_AG_ITEMS = [
    ((4, 32, 128), lambda o, s, h: o.at[s, pl.ds(_al(h, 16), 16), :], lambda r, h: r.at[pl.ds(_al(h, 16), 16), :]),
    ((4, 774, 1024), lambda o, s, h: o.at[s, :, pl.ds(_al(h, 512), 512)], lambda r, h: r.at[:, pl.ds(_al(h, 512), 512)]),
    ((1024, 1024), lambda o, s, h: o.at[pl.ds(_al(2 * s + h, 128), 128), :], lambda r, h: r.at[pl.ds(_al(h, 128), 128), :]),
    ((2, 1024, 4096), lambda o, s, h: o.at[h, :, pl.ds(_al(s, 1024), 1024)], lambda r, h: r.at[h]),
    ((2, 4096, 1024), lambda o, s, h: o.at[h, pl.ds(_al(s, 1024), 1024), :], lambda r, h: r.at[h]),
    ((1024, 2560), lambda o, s, h: o.at[pl.ds(_al(h, 512), 512), pl.ds(_al(s, 640), 640)],
     lambda r, h: r.at[pl.ds(_al(h, 512), 512), :]),
    ((1024, 1024), lambda o, s, h: o.at[pl.ds(_al(2 * s + h, 128), 128), :], lambda r, h: r.at[pl.ds(_al(h, 128), 128), :]),
]


_AG_GROUPS = [(0, 1, 2), (3,), (4,), (5, 6)]

_HBM = pl.BlockSpec(memory_space=pltpu.HBM)
_SEM = pl.BlockSpec(memory_space=pltpu.SEMAPHORE)
_SPLIT = dict(has_side_effects=pltpu.SideEffectType.DATAFLOW_SIDE_EFFECTING)


def _hbm(a):
    return pltpu.with_memory_space_constraint(a, pltpu.HBM)


def _ag_ici_copy(i, j, chip, c, slot, src_ref, land_ref, send_sems, recv_sems, k):
    _, dst, half = _AG_ITEMS[i]
    return pltpu.make_async_remote_copy(src_ref=half(src_ref, c), dst_ref=dst(land_ref, slot, c), send_sem=send_sems.at[k],
                                        recv_sem=recv_sems.at[k], device_id=(*chip, c), device_id_type=MESH)


def _ag_start(groups, shards, name):
    items_all = [i for g in groups for i in _AG_GROUPS[g]]
    n = len(items_all)
    ng = len(groups)
    lands = [lax.empty(_AG_ITEMS[i][0], shards[i].dtype) for i in items_all]

    def body(*refs):
        srcs, land_refs = dict(zip(items_all, refs[:n])), dict(zip(items_all, refs[n:2 * n]))
        sems = refs[2 * n:2 * n + 2 * ng]
        token = refs[-1]
        x, y, c = _position()
        me = 2 * x + y
        for gi, g in enumerate(groups):
            for t, i in enumerate(_AG_GROUPS[g]):
                for j, chip in enumerate(_other_chips(x, y)):
                    _ag_ici_copy(i, j, chip, c, me, srcs[i], land_refs[i], sems[2 * gi], sems[2 * gi + 1], 3 * t + j).start()
        token[...] = jnp.zeros_like(token)

    sem_shapes = []
    for g in groups:
        sem_shapes += [pltpu.SemaphoreType.DMA((3 * len(_AG_GROUPS[g]),))] * 2
    ops = [shards[i] for i in items_all] + lands
    out = pl.pallas_call(
        body, name=name,
        out_shape=tuple(sem_shapes) + tuple(pltpu.HBM(a.shape, a.dtype) for a in ops) + (jax.ShapeDtypeStruct((8, 128), F32),),
        in_specs=(_HBM,) * (2 * n),
        out_specs=(_SEM,) * (2 * ng) + (_HBM,) * (2 * n) + (pl.BlockSpec(memory_space=pltpu.VMEM),),
        input_output_aliases={i: 2 * ng + i for i in range(2 * n)},
        compiler_params=pltpu.CompilerParams(**_SPLIT),
    )(*[_hbm(a) for a in ops])
    sems, thru, token = out[:2 * ng], out[2 * ng:-1], out[-1]
    return ({g: (sems[2 * gi], sems[2 * gi + 1]) for gi, g in enumerate(groups)},
            dict(zip(items_all, thru[:n])), dict(zip(items_all, thru[n:])), token)


def _ag_wait(g, sems, srcs, lands, after):
    items = _AG_GROUPS[g]
    m = len(items)

    def body(*refs):
        src_refs, land_refs = refs[:m], refs[m:2 * m]
        send_sems, recv_sems = refs[2 * m], refs[2 * m + 1]
        x, y, c = _position()
        for t, i in enumerate(items):
            for j, chip in enumerate(_other_chips(x, y)):
                cp = _ag_ici_copy(i, j, chip, c, 2 * chip[0] + chip[1], src_refs[t], land_refs[t], send_sems, recv_sems,
                                  3 * t + j)
                cp.wait_send()
                cp.wait_recv()

    ops = [srcs[i] for i in items] + [lands[i] for i in items]
    out = pl.pallas_call(
        body, name=f"allgather_wait_{g}",
        out_shape=tuple(pltpu.HBM(a.shape, a.dtype) for a in ops),
        in_specs=(_HBM,) * (2 * m) + (_SEM, _SEM, pl.BlockSpec(memory_space=pl.ANY)),
        out_specs=(_HBM,) * (2 * m),
        input_output_aliases={i: i for i in range(2 * m)},
        compiler_params=pltpu.CompilerParams(**_SPLIT),
    )(*ops, sems[0], sems[1], after)
    return list(out[:m]), list(out[m:])


def _ag_forward(g, srcs, lands):
    return _ag_sibling(_AG_GROUPS[g], srcs, lands, False, f"allgather_forward_{g}")


def _ag_push_own(srcs, lands):
    items = tuple(sorted(lands))
    out = _ag_sibling(items, [srcs[i] for i in items], [lands[i] for i in items], True, "allgather_push_own")
    return dict(zip(items, out))


def _ag_sibling(items, srcs, lands, own, name):
    m = len(items)
    per = 2 if own else 3

    def body(*refs):
        src_refs, in_refs, out_refs = refs[:m], refs[m:2 * m], refs[2 * m:3 * m]
        send_sems, recv_sems = refs[3 * m:]
        x, y, c = _position()
        sibling = (x, y, 1 - c)
        me = 2 * x + y
        if own:
            mine = theirs = [(me, 0), (me, 1)]
        else:
            slots = [2 * chip[0] + chip[1] for chip in _other_chips(x, y)]
            mine, theirs = [(s, c) for s in slots], [(s, 1 - c) for s in slots]
        sends = []
        for t, i in enumerate(items):
            _, dst, half = _AG_ITEMS[i]
            for k, (slot, hc) in enumerate(mine):
                src = half(src_refs[t], hc) if own else dst(in_refs[t], slot, hc)
                sends.append(pltpu.make_async_remote_copy(
                    src_ref=src, dst_ref=dst(out_refs[t], slot, hc), send_sem=send_sems.at[per * t + k],
                    recv_sem=recv_sems.at[per * t + k], device_id=sibling, device_id_type=MESH))
        for cp in sends:
            cp.start()
        for t, i in enumerate(items):
            dst = _AG_ITEMS[i][1]
            for k, (slot, hc) in enumerate(theirs):
                there = dst(out_refs[t], slot, hc)
                pltpu.make_async_remote_copy(src_ref=there, dst_ref=there, send_sem=send_sems.at[per * t + k],
                                             recv_sem=recv_sems.at[per * t + k], device_id=sibling,
                                             device_id_type=MESH).wait_recv()
        for cp in sends:
            cp.wait_send()

    any_spec = pl.BlockSpec(memory_space=pl.ANY)
    return pl.pallas_call(
        body, name=name,
        in_specs=[any_spec] * (2 * m), out_specs=(any_spec,) * m,
        out_shape=tuple(jax.ShapeDtypeStruct(a.shape, a.dtype) for a in lands),
        input_output_aliases={m + t: t for t in range(m)},
        scratch_shapes=[pltpu.SemaphoreType.DMA((per * m,)), pltpu.SemaphoreType.DMA((per * m,))],
    )(*srcs, *lands)


def _pair_swap_copy(g_ref, r_ref, send_sem, recv_sem):
    x, y, c = _position()
    hc = g_ref.shape[2] // 2
    return pltpu.make_async_remote_copy(src_ref=g_ref.at[:, :, pl.ds(_al(1 - c, hc), hc)], dst_ref=r_ref,
                                        send_sem=send_sem, recv_sem=recv_sem, device_id=(x, y, 1 - c),
                                        device_id_type=MESH)


def _pair_swap_start(gb, tag):
    _, rows, cols = gb.shape
    recv = lax.empty((4, rows, cols // 2), gb.dtype)

    def body(g_ref, r_ref, send_sem, recv_sem, g_thru, r_thru, token):
        _pair_swap_copy(g_ref, r_ref, send_sem, recv_sem).start()
        token[...] = jnp.zeros_like(token)

    return pl.pallas_call(
        body, name="grad_pair_swap_start_" + tag,
        out_shape=(pltpu.SemaphoreType.DMA(()), pltpu.SemaphoreType.DMA(()), pltpu.HBM(gb.shape, gb.dtype),
                   pltpu.HBM(recv.shape, recv.dtype), jax.ShapeDtypeStruct((8, 128), F32)),
        in_specs=(_HBM, _HBM), out_specs=(_SEM, _SEM, _HBM, _HBM, pl.BlockSpec(memory_space=pltpu.VMEM)),
        input_output_aliases={0: 2, 1: 3},
        compiler_params=pltpu.CompilerParams(**_SPLIT),
    )(_hbm(gb), _hbm(recv))


def _pair_swap_wait(started, after, tag):
    send_sem, recv_sem, gb, recv, _ = started

    def body(g_ref, r_ref, send_sem, recv_sem, after_ref, g_out, r_out):
        cp = _pair_swap_copy(g_ref, r_ref, send_sem, recv_sem)
        cp.wait_send()
        cp.wait_recv()

    return pl.pallas_call(
        body, name="grad_pair_swap_wait_" + tag,
        out_shape=(pltpu.HBM(gb.shape, gb.dtype), pltpu.HBM(recv.shape, recv.dtype)),
        in_specs=(_HBM, _HBM, _SEM, _SEM, pl.BlockSpec(memory_space=pl.ANY)), out_specs=(_HBM, _HBM),
        input_output_aliases={0: 0, 1: 1},
        compiler_params=pltpu.CompilerParams(**_SPLIT),
    )(gb, recv, send_sem, recv_sem, after)


def _handover_copy(r_ref, send_sem, recv_sem, core):
    x, y, c = _position()
    hc = r_ref.shape[1] // 2
    cols = r_ref.at[:, pl.ds(_al(core, hc), hc)]
    return pltpu.make_async_remote_copy(src_ref=cols, dst_ref=cols, send_sem=send_sem, recv_sem=recv_sem,
                                        device_id=(x, y, 1 - c), device_id_type=MESH)


def _handover_start(red, tag):
    def body(r_ref, send_sem, recv_sem, r_thru, token):
        _handover_copy(r_ref, send_sem, recv_sem, lax.axis_index("c")).start()
        token[...] = jnp.zeros_like(token)

    return pl.pallas_call(
        body, name="grad_handover_start_" + tag,
        out_shape=(pltpu.SemaphoreType.DMA(()), pltpu.SemaphoreType.DMA(()), pltpu.HBM(red.shape, red.dtype),
                   jax.ShapeDtypeStruct((8, 128), F32)),
        in_specs=(_HBM,), out_specs=(_SEM, _SEM, _HBM, pl.BlockSpec(memory_space=pltpu.VMEM)),
        input_output_aliases={0: 2},
        compiler_params=pltpu.CompilerParams(**_SPLIT),
    )(_hbm(red))


def _handover_wait(started, after, tag):
    send_sem, recv_sem, red, _ = started

    def body(r_ref, send_sem, recv_sem, after_ref, r_out):
        c = lax.axis_index("c")
        _handover_copy(r_ref, send_sem, recv_sem, c).wait_send()
        _handover_copy(r_ref, send_sem, recv_sem, 1 - c).wait_recv()

    return pl.pallas_call(
        body, name="grad_handover_wait_" + tag,
        out_shape=pltpu.HBM(red.shape, red.dtype),
        in_specs=(_HBM, _SEM, _SEM, pl.BlockSpec(memory_space=pl.ANY)), out_specs=_HBM,
        input_output_aliases={0: 0},
        compiler_params=pltpu.CompilerParams(**_SPLIT),
    )(red, send_sem, recv_sem, after)


def _handover(red, tag):
    started = _handover_start(red, tag)
    return _handover_wait(started, started[3], tag)


def _a2a_copy(j, chip, c, p_ref, q_ref, q_slot, send_sems, recv_sems):
    return pltpu.make_async_remote_copy(src_ref=p_ref.at[2 * chip[0] + chip[1]], dst_ref=q_ref.at[q_slot],
                                        send_sem=send_sems.at[j], recv_sem=recv_sems.at[j], device_id=(*chip, c),
                                        device_id_type=MESH)


def _a2a_start(p, tag):
    def body(p_ref, q_ref, send_sems, recv_sems, p_thru, q_thru, token):
        x, y, c = _position()
        for j, chip in enumerate(_other_chips(x, y)):
            _a2a_copy(j, chip, c, p_ref, q_ref, 2 * x + y, send_sems, recv_sems).start()
        token[...] = jnp.zeros_like(token)

    return pl.pallas_call(
        body, name="grad_alltoall_start_" + tag,
        out_shape=(pltpu.SemaphoreType.DMA((3,)), pltpu.SemaphoreType.DMA((3,)), pltpu.HBM(p.shape, p.dtype),
                   pltpu.HBM(p.shape, p.dtype), jax.ShapeDtypeStruct((8, 128), F32)),
        in_specs=(_HBM, _HBM), out_specs=(_SEM, _SEM, _HBM, _HBM, pl.BlockSpec(memory_space=pltpu.VMEM)),
        input_output_aliases={0: 2, 1: 3},
        compiler_params=pltpu.CompilerParams(**_SPLIT),
    )(_hbm(p), _hbm(lax.empty(p.shape, p.dtype)))


def _a2a_wait(send_sems, recv_sems, p, q, after, tag):
    def body(p_ref, q_ref, send_sems, recv_sems, after_ref, p_out, q_out):
        x, y, c = _position()
        for j, chip in enumerate(_other_chips(x, y)):
            cp = _a2a_copy(j, chip, c, p_ref, q_ref, 2 * chip[0] + chip[1], send_sems, recv_sems)
            cp.wait_send()
            cp.wait_recv()

    return pl.pallas_call(
        body, name="grad_alltoall_wait_" + tag,
        out_shape=(pltpu.HBM(p.shape, p.dtype), pltpu.HBM(q.shape, q.dtype)),
        in_specs=(_HBM, _HBM, _SEM, _SEM, pl.BlockSpec(memory_space=pl.ANY)), out_specs=(_HBM, _HBM),
        input_output_aliases={0: 0, 1: 1},
        compiler_params=pltpu.CompilerParams(**_SPLIT),
    )(p, q, send_sems, recv_sems, after)


def _comm_rows(rows):
    return next(t for t in (1536, 1152, 1024, 512, 384, 256, 128) if rows % t == 0)


def _pair_add(gb, recv, where, tag):
    _, rows, cols = gb.shape
    hc = cols // 2
    tr = _comm_rows(rows)

    def body(w_ref, g_ref, r_ref, o_ref):
        o_ref[...] = (g_ref[...].astype(F32) + r_ref[...].astype(F32)).astype(o_ref.dtype)

    return pl.pallas_call(
        body, name="grad_pair_add_" + tag,
        grid_spec=pltpu.PrefetchScalarGridSpec(
            num_scalar_prefetch=1, grid=(4, rows // tr),
            in_specs=[pl.BlockSpec((None, tr, hc), lambda s, j, w_ref: (s, j, w_ref[0])),
                      pl.BlockSpec((None, tr, hc), lambda s, j, w_ref: (s, j, 0))],
            out_specs=pl.BlockSpec((None, tr, hc), lambda s, j, w_ref: (s, j, 0))),
        out_shape=jax.ShapeDtypeStruct((4, rows, hc), gb.dtype),
        compiler_params=_cp(("parallel", "parallel")),
    )(where, gb, recv)


def _sum_chips(p, q, where, tag):
    _, rows, hc = q.shape
    tr = _comm_rows(rows)

    def body(w_ref, p_ref, qa_ref, qb_ref, qc_ref, o_ref):
        me = w_ref[1]
        own, qa, qb, qc = (r[...].astype(F32) for r in (p_ref, qa_ref, qb_ref, qc_ref))
        v0 = jnp.where(me == 0, own, qa)
        v1 = jnp.where(me == 1, own, jnp.where(me == 0, qa, qb))
        v2 = jnp.where(me == 2, own, jnp.where(me < 2, qb, qc))
        v3 = jnp.where(me == 3, own, qc)
        o_ref[...] = ((v0 + v1) + v2) + v3

    slot = lambda k: pl.BlockSpec((None, tr, hc), lambda j, w_ref: (w_ref[k], j, 0))
    return pl.pallas_call(
        body, name="grad_sum_chips_" + tag,
        grid_spec=pltpu.PrefetchScalarGridSpec(
            num_scalar_prefetch=1, grid=(rows // tr,),
            in_specs=[slot(1), slot(2), slot(3), slot(4)],
            out_specs=pl.BlockSpec((tr, hc), lambda j, w_ref: (j, w_ref[0]))),
        out_shape=jax.ShapeDtypeStruct((rows, 2 * hc), F32),
        compiler_params=_cp(("parallel",)),
    )(where, p, q, q, q)


def _shard_major(g, axis):
    shape = g.shape
    g = g.reshape(shape[:axis] + (4, shape[axis] // 4) + shape[axis + 1:])
    return jnp.moveaxis(g, axis, 0).reshape(4, -1)


def _unshard(g4, shape, axis):
    n = shape[axis] // 4
    g = g4.reshape((4,) + shape[:axis] + (n,) + shape[axis + 1:])
    return jnp.moveaxis(g, 0, axis).reshape(shape)


def _split(flat, shapes):
    out, off = [], 0
    for shp in shapes:
        n = 1
        for d in shp:
            n *= d
        out.append(flat[..., off:off + n].reshape(flat.shape[:-1] + tuple(shp)))
        off += n
    return out


def _even_rows_to_kernel(wt):
    return jnp.concatenate([wt[:1536], wt[1552:3088], wt[1536:1552], wt[3088:3096],
                            jnp.zeros((PE - 3096, wt.shape[1]), wt.dtype)], axis=0)


def _block_diag(w):
    eye = jnp.eye(8, dtype=w.dtype)
    return (w[:, :, None, :] * eye[:, None, :, None]).reshape(512, 512)


def _diag_blocks(g):
    eye = jnp.eye(8, dtype=g.dtype)
    return (g.reshape(8, 64, 8, 64) * eye[:, None, :, None]).sum(axis=2)


def _shift_down(a, s):
    return a if s == 0 else jnp.pad(a, ((s, 0), (0, 0)))[:a.shape[0]]


def _shift_up(a, s):
    return a if s == 0 else jnp.pad(a, ((0, s), (0, 0)))[s:]


SMALL_SHARDED_SHAPES = [(2, 4, 256), (16, 64), (4, 128), (128,), (128,), (128,), (128,)]
REPL_SHAPES = [(256,), (512,), (8,), (8, 257), (8, 64, 64), (8, 64, 64)]


def kernel(x, norm_w, w_in_even, gla_w_a_up, gla_b_a, gla_norm_w, fox_b_f, w_out_even, w_in_odd, rel_bias, conv_w, conv_b, lru_w_a, lru_b_a, lru_w_x, lru_b_x, lru_lambda, w_out_odd, w_mlp_up, w_mlp_down, loss_target, m_norm_w, m_w_in_even, m_gla_w_a_up, m_gla_b_a, m_gla_norm_w, m_fox_b_f, m_w_out_even, m_w_in_odd, m_rel_bias, m_conv_w, m_conv_b, m_lru_w_a, m_lru_b_a, m_lru_w_x, m_lru_b_x, m_lru_lambda, m_w_out_odd, m_w_mlp_up, m_w_mlp_down, v_norm_w, v_w_in_even, v_gla_w_a_up, v_gla_b_a, v_gla_norm_w, v_fox_b_f, v_w_out_even, v_w_in_odd, v_rel_bias, v_conv_w, v_conv_b, v_lru_w_a, v_lru_b_a, v_lru_w_x, v_lru_b_x, v_lru_lambda, v_w_out_odd, v_w_mlp_up, v_w_mlp_down):
    c_idx = lax.axis_index("c")

    small_local = [norm_w, gla_w_a_up[0], conv_w[0], conv_b[0], lru_b_a[0], lru_b_x[0], lru_lambda[0]]
    small_src = jnp.concatenate([a.reshape(-1) for a in small_local]).reshape(32, 128)
    first = {0: small_src, 1: w_in_even[0].T.astype(BF16), 2: w_out_even[0].astype(BF16)}
    sems0, srcs0, lands0, ag_token = _ag_start([0], first, "allgather_start_0")
    zero = ag_token[0, 0]
    later = {3: (w_mlp_up + zero).astype(BF16), 4: (w_mlp_down + zero).astype(BF16),
             5: (w_in_odd[0] + zero).astype(BF16), 6: (w_out_odd[0] + zero).astype(BF16)}
    sems1, srcs1, lands1, ag_token = _ag_start([1, 2, 3], later, "allgather_start_1")
    ag_sems, ag_srcs = {**sems0, **sems1}, {**srcs0, **srcs1}
    ag_lands = _ag_push_own(ag_srcs, {**lands0, **lands1})

    def gathered(g, after):
        srcs_g, lands_g = _ag_wait(g, ag_sems[g], ag_srcs, ag_lands, after)
        return _ag_forward(g, srcs_g, lands_g)

    small4, w_in_e4, w_out_e = gathered(0, ag_token)
    me = 2 * lax.axis_index("x") + lax.axis_index("y")
    others = [k + (k >= me).astype(jnp.int32) for k in range(3)]
    where = jnp.stack([c_idx, me] + others).astype(jnp.int32)

    w_in_e_t = _even_rows_to_kernel(w_in_e4.reshape(3096, D))
    g_small = _split(small4.reshape(4, 32 * 128), SMALL_SHARDED_SHAPES)
    nw_full = _unshard(g_small[0], (2, 4, 1024), 2)
    wa_up = _unshard(g_small[1], (16, 256), 1)
    cw = _unshard(g_small[2], (4, 512), 1)
    cb, lba, lbx, lam = [_unshard(g, (512,), 0).reshape(1, 512) for g in g_small[3:]]
    nw = lambda layer, i: nw_full[layer, i].reshape(1, D)

    wa_pad = jnp.pad(wa_up, ((0, 128 - 16), (0, 0)))
    gla_ba = gla_b_a.reshape(1, 256)
    gla_nw = gla_norm_w.reshape(1, 512)
    fox_bpad = jnp.pad(fox_b_f.reshape(1, 8), ((0, 0), (FOX_LANE0, 128 - FOX_LANE0 - 8)))
    rbp = jnp.pad(rel_bias[0], ((0, 0), (0, REL_PAD - 257)))
    wa_bd = _block_diag(lru_w_a[0])
    wx_bd = _block_diag(lru_w_x[0])

    x0 = x[0]
    tgt = loss_target[0]

    h0 = _prenorm(x0, nw(0, 0), "prenorm_l0_mix")
    proj_e = _mm(h0, w_in_e_t, "nt", tm=2048, tn=640, name="mm_in_even")
    cat0, s_prev = _gla_fwd(proj_e, wa_pad, gla_ba, gla_nw)
    cum_r = _fox_gate_fwd(proj_e, fox_bpad)
    cum_c = cum_r[:, FOX_LANE0:FOX_LANE0 + 8].T
    cat0 = _fox_fwd(proj_e, cum_c, cat0)
    mix0 = _mm(cat0, w_out_e, "nn", tm=2048, tn=512, name="mm_out_even")
    x1, h1 = _post_pre_fwd(x0, mix0, nw(0, 1), nw(0, 2), "post_pre_l0_mix")
    w_up, = gathered(1, x1)
    a0, r0 = _mm(h1, w_up, "nn", tm=2048, tn=1024, b_layer=0, relu_pair=True, name="mm_up_l0")
    w_dn, = gathered(2, a0)
    d0 = _mm(a0, w_dn, "nn", tm=1024, tn=512, b_layer=0, name="mm_down_l0")
    x2, h2 = _post_pre_fwd(x1, d0, nw(0, 3), nw(1, 0), "post_pre_l0_mlp")

    w_in_o, w_out_o = gathered(3, x2)
    proj_o = _mm(h2, w_in_o, "nn", tm=2048, tn=640, name="mm_in_odd")
    bias_q = _bias_build(rbp)
    bias = bias_q.transpose(1, 0, 2)
    kvpad = jnp.pad(proj_o[:, 512:1536], ((CA_PAD, 0), (0, 0)))
    cat1 = _ca_fwd(proj_o, kvpad, bias)
    x_in = proj_o[:, 2048:2560]
    xs = jnp.stack([_shift_down(x_in, 3 - j) for j in range(4)])
    lru_a, lru_b = _lru_pre_fwd(xs, cw, cb, wa_bd, lba, wx_bd, lbx, lam)
    hh = _lru_scan_fwd(lru_a, lru_b)
    cat1 = _lru_post_fwd(hh, proj_o, cat1)
    mix1 = _mm(cat1, w_out_o, "nn", tm=2048, tn=512, name="mm_out_odd")
    x3, h3 = _post_pre_fwd(x2, mix1, nw(1, 1), nw(1, 2), "post_pre_l1_mix")
    a1, r1 = _mm(h3, w_up, "nn", tm=2048, tn=1024, b_layer=1, relu_pair=True, name="mm_up_l1")
    d1 = _mm(a1, w_dn, "nn", tm=1024, tn=512, b_layer=1, name="mm_down_l1")
    g4, loss_part, dd1, dnw13 = _post_loss(x3, d1, nw(1, 3), tgt)
    loss = lax.psum(loss_part[0, 0], ("x", "y", "c"))

    def rs_begin(swap, after, tag):
        gb, recv = _pair_swap_wait(swap, after, tag)
        return _a2a_start(_pair_add(gb, recv, where, tag), tag)

    def rs_end(started, after, tag):
        send_sems, recv_sems, p, q, _ = started
        p, q = _a2a_wait(send_sems, recv_sems, p, q, after, tag)
        return _handover(_sum_chips(p, q, where, tag), tag)

    gba = lax.dynamic_update_slice(lax.empty((4, GA_ROWS, D), BF16), jnp.zeros((4, GA_UP - GA_GAP, D), BF16),
                                   (0, GA_GAP, 0))
    gba = _mm(a1, dd1, "tn", tm=512, tn=1024, into=(gba, 1024, GA_DN), name="mm_down_l1_dw")
    du1 = _mm(dd1, w_dn, "nt", tm=2048, tn=1024, b_layer=1, times2=r1, out_dtype=BF16, name="mm_down_l1_dx")
    gba = _mm(du1, h3, "tn", tm=512, tn=1024, into=(gba, 1024, GA_UP), name="mm_up_l1_dw")
    dh3 = _mm(du1, w_up, "nt", tm=1024, tn=512, b_layer=1, name="mm_up_l1_dx")
    g3, dmix1, dnw12, dnw11 = _pre_post_bwd(x3, nw(1, 2), dh3, g4, mix1, nw(1, 1), "pre_post_bwd_l1_mlp")
    gba = _mm(cat1, dmix1, "tn", tm=128, tn=1024, into=(gba, 256, GA_OUT_O), name="mm_out_odd_dw")
    dcat1 = _mm(dmix1, w_out_o, "nt", tm=2048, tn=512, name="mm_out_odd_dx")

    dq_c, dkpad, dvpad, dbias = _ca_bwd(proj_o, kvpad, bias, dcat1)
    g_rel = _bias_grad(jnp.pad(dbias.transpose(1, 0, 2), ((0, 0), (0, 0), (0, BIAS_W - CA_BAND))))[:, :257]
    dhh, dgate = _lru_post_bwd(hh, proj_o, dcat1)
    da_l, db_l = _lru_scan_bwd(_shift_up(lru_a, 1), _shift_down(hh, 1), dhh)
    dxs, g_cw, g_cb, g_wa_bd, g_lba, g_wx_bd, g_lbx, g_lam = _lru_pre_bwd(xs, cw, cb, wa_bd, lba, wx_bd, lbx, lam, da_l, db_l)
    dx_in = _conv_dx(jnp.stack([_shift_up(dxs[j], 3 - j) for j in range(4)]))
    dproj_o = jnp.concatenate([dq_c, dkpad[CA_PAD:], dvpad[CA_PAD:], dgate, dx_in], axis=1).astype(BF16)
    gba = _mm(dproj_o, h2, "tn", tm=128, tn=1024, into=(gba, 640, GA_IN_O), name="mm_in_odd_dw")
    swap_a = _pair_swap_start(gba, "a")
    dh2 = _mm(dproj_o, w_in_o, "nt", tm=1024, tn=512, name="mm_in_odd_dx")
    g2, dd0, dnw10, dnw03 = _pre_post_bwd(x2, nw(1, 0) + swap_a[4][0, 0], dh2, g3, d0, nw(0, 3), "pre_post_bwd_l1_mix")
    rs_a = rs_begin(swap_a, g2, "a")

    gbb = lax.empty((4, GB_ROWS, D), BF16)
    gbb = _mm(a0, dd0, "tn", tm=512, tn=1024, into=(gbb, 1024, GB_DN), name="mm_down_l0_dw")
    du0 = _mm(dd0, w_dn, "nt", tm=2048, tn=1024, b_layer=0, times2=r0, out_dtype=BF16, name="mm_down_l0_dx")
    gbb = _mm(du0, h1, "tn", tm=512, tn=1024, into=(gbb, 1024, GB_UP), name="mm_up_l0_dw")
    swap_b = _pair_swap_start(gbb, "b")
    dh1 = _mm(du0, w_up, "nt", tm=1024, tn=512, b_layer=0, name="mm_up_l0_dx")
    g1, dmix0, dnw02, dnw01 = _pre_post_bwd(x1, nw(0, 2) + (swap_b[4][0, 0] + rs_a[4][0, 0]), dh1, g2, mix0, nw(0, 1),
                                            "pre_post_bwd_l0_mlp")
    rs_b = rs_begin(swap_b, g1, "b")
    gbc = lax.empty((4, GC_ROWS, D), BF16)
    gbc = _mm(cat0, dmix0, "tn", tm=128, tn=1024, into=(gbc, 256, GC_OUT_E), name="mm_out_even_dw")
    dcat0 = _mm(dmix0, w_out_e, "nt", tm=2048, tn=512, name="mm_out_even_dx")

    dq_g, dk_g, dv_g, dr_g, daux_g, g_wa_pad, g_gla_ba, g_gla_nw = _gla_bwd(
        proj_e, s_prev, wa_pad, gla_ba, gla_nw + rs_b[4][0, 0], dcat0)
    dq_f, dk_f, dv_f, dccol = _fox_bwd(proj_e, cum_c, dcat0)
    dccol_t = jnp.pad(dccol.sum(axis=0).T, ((0, 0), (FOX_LANE0, 128 - FOX_LANE0 - 8)))
    daux, g_fox_bpad = _fox_gate_bwd(proj_e, fox_bpad, dccol_t, daux_g)
    dproj_e = jnp.concatenate([dq_g, dk_g, dv_g, dr_g, dq_f, dk_f, dv_f, daux], axis=1).astype(BF16)
    gt_in_e = _mm(dproj_e, h0, "tn", tm=640, tn=1024, out_dtype=BF16, name="mm_in_even_dw")
    dh0 = _mm(dproj_e, w_in_e_t, "nn", tm=1024, tn=512, name="mm_in_even_dx")
    grad_x, dnw00 = _norm_bwd(x0, nw(0, 0), dh0, g1, "prenorm_l0_mix_bwd")

    def rs_reduce(started, after, tag):
        send_sems, recv_sems, p, q, _ = started
        p, q = _a2a_wait(send_sems, recv_sems, p, q, after, tag)
        return _handover_start(_sum_chips(p, q, where, tag), tag)

    ho_a = rs_reduce(rs_a, grad_x, "a")
    ho_b = rs_reduce(rs_b, ho_a[3], "b")

    g_norm = jnp.stack([jnp.concatenate([dnw00, dnw01, dnw02, dnw03]), jnp.concatenate([dnw10, dnw11, dnw12, dnw13])])
    sharded = [(g_norm, 2), (g_wa_pad[:16], 1), (g_cw, 1), (g_cb[0], 0), (g_lba[0], 0), (g_lbx[0], 0), (g_lam[0], 0)]
    replicated = [g_gla_ba[0], g_gla_nw[0], g_fox_bpad[0, FOX_LANE0:FOX_LANE0 + 8], g_rel, _diag_blocks(g_wa_bd),
                  _diag_blocks(g_wx_bd)]
    small4 = jnp.concatenate([_shard_major(g, ax) for g, ax in sharded]
                             + [jnp.broadcast_to(g.reshape(1, -1), (4, g.size)) for g in replicated], axis=1)
    n_small = small4.shape[1]
    small_rows = GC_ROWS - GC_TAIL - 774
    small4 = jnp.pad(small4, ((0, 0), (0, small_rows * D - n_small))).reshape(4, small_rows, D)
    gt_rows = jnp.concatenate([gt_in_e[:1536], gt_in_e[3072:3088], gt_in_e[1536:3072], gt_in_e[3088:3096]], axis=0)
    tail = jnp.concatenate([gt_rows.reshape(4, 774, D), small4.astype(BF16)], axis=1)
    gbc = lax.dynamic_update_slice(gbc, tail, (0, GC_TAIL, 0))
    swap_c = _pair_swap_start(gbc, "c")
    rs_c = rs_begin(swap_c, swap_c[4], "c")

    red_a = _handover_wait(ho_a, rs_c[4], "a")
    red_b = _handover_wait(ho_b, red_a, "b")
    early = dict(
        w_mlp_up=_adamw_from(w_mlp_up, m_w_mlp_up, v_w_mlp_up, [(red_b, GB_UP, True), (red_a, GA_UP, True)], 512,
                             "adamw_w_mlp_up"),
        w_mlp_down=_adamw_from(w_mlp_down, m_w_mlp_down, v_w_mlp_down, [(red_b, GB_DN, False), (red_a, GA_DN, False)],
                               512, "adamw_w_mlp_down"),
        w_in_odd=_adamw_from(w_in_odd, m_w_in_odd, v_w_in_odd, [(red_a, GA_IN_O, True)], 256, "adamw_w_in_odd"),
        w_out_odd=_adamw_from(w_out_odd, m_w_out_odd, v_w_out_odd, [(red_a, GA_OUT_O, False)], 128, "adamw_w_out_odd"))
    red_c = rs_end(rs_c, early["w_out_odd"][3], "c")

    g_small = _split(red_c[GC_TAIL + 774:].reshape(-1)[:n_small], SMALL_SHARDED_SHAPES + REPL_SHAPES)
    g_of = dict(zip(["norm_w", "gla_w_a_up", "conv_w", "conv_b", "lru_b_a", "lru_b_x", "lru_lambda", "gla_b_a",
                     "gla_norm_w", "fox_b_f", "rel_bias", "lru_w_a", "lru_w_x"], g_small))
    g_of.update(w_in_even=red_c[GC_TAIL:GC_TAIL + 774])
    early["w_out_even"] = _adamw_from(w_out_even, m_w_out_even, v_w_out_even, [(red_c, GC_OUT_E, False)], 256,
                                      "adamw_w_out_even")

    names = ["norm_w", "w_in_even", "gla_w_a_up", "gla_b_a", "gla_norm_w", "fox_b_f", "w_out_even", "w_in_odd", "rel_bias",
             "conv_w", "conv_b", "lru_w_a", "lru_b_a", "lru_w_x", "lru_b_x", "lru_lambda", "w_out_odd", "w_mlp_up",
             "w_mlp_down"]
    w_of = dict(norm_w=norm_w, w_in_even=w_in_even, gla_w_a_up=gla_w_a_up, gla_b_a=gla_b_a, gla_norm_w=gla_norm_w,
                fox_b_f=fox_b_f, w_out_even=w_out_even, w_in_odd=w_in_odd, rel_bias=rel_bias, conv_w=conv_w, conv_b=conv_b,
                lru_w_a=lru_w_a, lru_b_a=lru_b_a, lru_w_x=lru_w_x, lru_b_x=lru_b_x, lru_lambda=lru_lambda,
                w_out_odd=w_out_odd, w_mlp_up=w_mlp_up, w_mlp_down=w_mlp_down)
    m_of = dict(norm_w=m_norm_w, w_in_even=m_w_in_even, gla_w_a_up=m_gla_w_a_up, gla_b_a=m_gla_b_a,
                gla_norm_w=m_gla_norm_w, fox_b_f=m_fox_b_f, w_out_even=m_w_out_even, w_in_odd=m_w_in_odd,
                rel_bias=m_rel_bias, conv_w=m_conv_w, conv_b=m_conv_b, lru_w_a=m_lru_w_a, lru_b_a=m_lru_b_a,
                lru_w_x=m_lru_w_x, lru_b_x=m_lru_b_x, lru_lambda=m_lru_lambda, w_out_odd=m_w_out_odd,
                w_mlp_up=m_w_mlp_up, w_mlp_down=m_w_mlp_down)
    v_of = dict(norm_w=v_norm_w, w_in_even=v_w_in_even, gla_w_a_up=v_gla_w_a_up, gla_b_a=v_gla_b_a,
                gla_norm_w=v_gla_norm_w, fox_b_f=v_fox_b_f, w_out_even=v_w_out_even, w_in_odd=v_w_in_odd,
                rel_bias=v_rel_bias, conv_w=v_conv_w, conv_b=v_conv_b, lru_w_a=v_lru_w_a, lru_b_a=v_lru_b_a,
                lru_w_x=v_lru_w_x, lru_b_x=v_lru_b_x, lru_lambda=v_lru_lambda, w_out_odd=v_w_out_odd,
                w_mlp_up=v_w_mlp_up, w_mlp_down=v_w_mlp_down)
    grads, deltas, new_ms, new_vs = [], [], [], []
    for n in names:
        w = w_of[n]
        if n in early:
            g, d, mn, vn = early[n]
            grads.append(g)
            deltas.append(d)
            new_ms.append(mn)
            new_vs.append(vn)
            continue
        if n == "w_in_even":
            to_view = lambda a: a[0].T
            from_view = lambda a: a.T[None]
        else:
            view = w.shape if w.ndim <= 3 else w.shape[-3:]
            to_view = lambda a, view=view: a.reshape(view)
            from_view = lambda a, w=w: a.reshape(w.shape)
        g = g_of[n] if n == "w_in_even" else to_view(g_of[n])
        d, mn, vn = _adamw(to_view(w), g, to_view(m_of[n]), to_view(v_of[n]), "adamw_" + n)
        grads.append(from_view(g))
        deltas.append(from_view(d))
        new_ms.append(from_view(mn))
        new_vs.append(from_view(vn))

    return (loss, grad_x.reshape(1, T, D), *grads, *deltas, *new_ms, *new_vs)
```

```python
import functools

import jax
import jax.numpy as jnp
from jax import lax
from jax.experimental import pallas as pl
from jax.experimental.pallas import tpu as pltpu

F32 = jnp.float32
BF16 = jnp.bfloat16
MESH = pl.DeviceIdType.MESH

T = 2048
D = 1024
DFF = 4096
EPS = 1e-6
CHUNK = 64
NCHUNK = T // CHUNK
PE = 3200
PO = 2560
AUX_BLK = 3072 // 128
FOX_LANE0 = 16
GLA_SCALE = 64 ** -0.5
ATT_SCALE = 64 ** -0.5
NEG = float(jnp.finfo(jnp.float32).min)
CA_BAND = 576
CA_PAD = 512
REL_PAD = 384

VMEM_LIMIT = 48 * 1024 * 1024

ADAM_LR, ADAM_B1, ADAM_B2, ADAM_EPS, ADAM_WD, ADAM_STEP = 0.001, 0.9, 0.999, 1e-08, 0.01, 10

GA_ROWS, GA_IN_O, GA_OUT_O, GA_GAP, GA_UP, GA_DN = 3072, 0, 640, 896, 1024, 2048
GB_ROWS, GB_UP, GB_DN = 2048, 0, 1024
GC_ROWS, GC_OUT_E, GC_TAIL = 1152, 0, 256

_DIMS = {"nn": (((1,), (0,)), ((), ())), "nt": (((1,), (1,)), ((), ())), "tn": (((0,), (0,)), ((), ()))}


def _cp(sem, **kw):
    return pltpu.CompilerParams(dimension_semantics=sem, vmem_limit_bytes=VMEM_LIMIT, **kw)


def _dot(a, b, mode):
    return lax.dot_general(a.astype(BF16), b.astype(BF16), _DIMS[mode], preferred_element_type=F32)


@functools.partial(jax.custom_vjp, nondiff_argnums=(2,))
def bdot(a, b, mode):
    return _dot(a, b, mode)


def _bdot_fwd(a, b, mode):
    return _dot(a, b, mode), (a, b)


def _bdot_bwd(mode, res, g):
    a, b = res
    if mode == "nn":
        da, db = _dot(g, b, "nt"), _dot(a, g, "tn")
    elif mode == "nt":
        da, db = _dot(g, b, "nn"), _dot(g, a, "tn")
    else:
        da, db = _dot(b, g, "nt"), _dot(a, g, "nn")
    return da.astype(a.dtype), db.astype(b.dtype)


bdot.defvjp(_bdot_fwd, _bdot_bwd)


def _hdot_raw(a, b, mode):
    return lax.dot_general(a, b, _DIMS[mode], precision=lax.Precision.HIGHEST, preferred_element_type=F32)


def _log_sigmoid(x):
    return jnp.minimum(x, 0.0) - jnp.log(1.0 + jnp.exp(-jnp.abs(x)))


def _sigmoid(x):
    return 1.0 / (1.0 + jnp.exp(-x))


def _expm1(x):
    series = x * (1.0 + x * 0.5 * (1.0 + x * (1.0 / 3.0) * (1.0 + x * 0.25)))
    return jnp.where(jnp.abs(x) < 0.03, series, jnp.exp(x) - 1.0)


def _gelu_tanh(x):
    return 0.5 * x * (1.0 + jnp.tanh(0.7978845608028654 * (x + 0.044715 * x * x * x)))


def _iota(shape, dim):
    return lax.broadcasted_iota(jnp.int32, shape, dim)


def _mm(a, b, mode, *, tm, tn, tk=None, out_dtype=F32, name, b_layer=None, into=None, relu_pair=False, times2=None):
    b2 = b.shape[-2:]
    if mode == "nn":
        (m, k), n = a.shape, b2[1]
    elif mode == "nt":
        (m, k), n = a.shape, b2[0]
    else:
        (k, m), n = a.shape, b2[1]
    tk = k if tk is None else tk
    assert m % tm == 0 and n % tn == 0 and k % tk == 0, (name, a.shape, b.shape)
    nk = k // tk
    if mode == "tn":
        a_spec = pl.BlockSpec((tk, tm), lambda i, j, kk: (kk, i))
    elif m == tm and nk == 1:
        a_spec = pl.BlockSpec((tm, tk), lambda i, j, kk: (i, kk), pipeline_mode=pl.Buffered(1))
    else:
        a_spec = pl.BlockSpec((tm, tk), lambda i, j, kk: (i, kk))
    b_blk = {"nn": (tk, tn), "nt": (tn, tk), "tn": (tk, tn)}[mode]
    b_idx = {"nn": lambda i, j, kk: (kk, j), "nt": lambda i, j, kk: (j, kk), "tn": lambda i, j, kk: (kk, j)}[mode]
    if b_layer is None:
        b_spec = pl.BlockSpec(b_blk, b_idx)
    else:
        b_spec = pl.BlockSpec((None,) + b_blk, lambda i, j, kk: (b_layer,) + b_idx(i, j, kk))

    tile = pl.BlockSpec((tm, tn), lambda i, j, kk: (i, j))
    if into is not None:
        buf, per_slot, row_off = into
        assert m == 4 * per_slot and per_slot % tm == 0 and row_off % tm == 0 and buf.shape[2] == n, (name, buf.shape)
        bps = per_slot // tm
        out_specs = pl.BlockSpec((None, tm, tn), lambda i, j, kk: (i // bps, row_off // tm + i % bps, j))
        out_shape = jax.ShapeDtypeStruct(buf.shape, buf.dtype)
        extra_in, extra_specs, aliases = [buf], [pl.BlockSpec(memory_space=pl.ANY)], {2: 0}
        finish = lambda acc, extra: [acc.astype(buf.dtype)]
    elif relu_pair:
        out_specs = (tile, tile)
        out_shape = (jax.ShapeDtypeStruct((m, n), BF16),) * 2
        extra_in, extra_specs, aliases = [], [], {}

        def finish(acc, extra):
            r = jnp.maximum(acc, 0.0)
            return [(r * r).astype(BF16), r.astype(BF16)]
    elif times2 is not None:
        out_specs = tile
        out_shape = jax.ShapeDtypeStruct((m, n), out_dtype)
        extra_in, extra_specs, aliases = [times2], [tile], {}
        finish = lambda acc, extra: [(acc * (2.0 * extra[...].astype(F32))).astype(out_dtype)]
    else:
        out_specs = tile
        out_shape = jax.ShapeDtypeStruct((m, n), out_dtype)
        extra_in, extra_specs, aliases = [], [], {}
        finish = lambda acc, extra: [acc.astype(out_dtype)]
    n_out = 2 if relu_pair else 1

    def body(*refs):
        a_ref, b_ref = refs[0], refs[1]
        extra = refs[2] if extra_in else None
        o_refs = refs[2 + len(extra_in):2 + len(extra_in) + n_out]

        def store(acc):
            for o_ref, val in zip(o_refs, finish(acc, extra)):
                o_ref[...] = val

        if nk == 1:
            store(_dot(a_ref[...], b_ref[...], mode))
            return
        acc_ref = refs[-1]
        kk = pl.program_id(2)

        @pl.when(kk == 0)
        def _():
            acc_ref[...] = jnp.zeros_like(acc_ref)

        acc_ref[...] += _dot(a_ref[...], b_ref[...], mode)

        @pl.when(kk == nk - 1)
        def _():
            store(acc_ref[...])

    return pl.pallas_call(
        body, name=name, grid=(m // tm, n // tn, nk),
        in_specs=[a_spec, b_spec] + extra_specs,
        out_specs=out_specs, out_shape=out_shape,
        scratch_shapes=[pltpu.VMEM((tm, tn), F32)] if nk > 1 else [],
        input_output_aliases=aliases,
        compiler_params=_cp(("parallel", "parallel", "arbitrary")),
    )(a, b, *extra_in)


ROWS = 512


def _prenorm(x, w, name):
    def body(x_ref, w_ref, o_ref):
        xv = x_ref[...]
        r = lax.rsqrt(jnp.mean(xv * xv, axis=-1, keepdims=True) + EPS)
        o_ref[...] = (xv * r * w_ref[...]).astype(BF16)

    return pl.pallas_call(
        body, name=name, grid=(T // ROWS,),
        in_specs=[pl.BlockSpec((ROWS, D), lambda i: (i, 0)), pl.BlockSpec((1, D), lambda i: (0, 0))],
        out_specs=pl.BlockSpec((ROWS, D), lambda i: (i, 0)),
        out_shape=jax.ShapeDtypeStruct((T, D), BF16),
        compiler_params=_cp(("parallel",)),
    )(x, w)


def _rms(z):
    return lax.rsqrt(jnp.mean(z * z, axis=-1, keepdims=True) + EPS)


def _rms_bwd(z, w, dy):
    r = _rms(z)
    wdy = dy * w
    dz = r * wdy - z * (r * r * r) * jnp.mean(z * wdy, axis=-1, keepdims=True)
    return dz, jnp.sum(dy * z * r, axis=0, keepdims=True)


_ROW = pl.BlockSpec((ROWS, D), lambda i: (i, 0))
_VEC = pl.BlockSpec((1, D), lambda i: (0, 0))


def _post_pre_fwd(x, z, w_post, w_pre, name):
    def body(x_ref, z_ref, wp_ref, wn_ref, x_out, h_out):
        zv = z_ref[...]
        xn = x_ref[...] + zv * _rms(zv) * wp_ref[...]
        x_out[...] = xn
        h_out[...] = (xn * _rms(xn) * wn_ref[...]).astype(BF16)

    return pl.pallas_call(
        body, name=name, grid=(T // ROWS,), in_specs=[_ROW, _ROW, _VEC, _VEC], out_specs=(_ROW, _ROW),
        out_shape=(jax.ShapeDtypeStruct((T, D), F32), jax.ShapeDtypeStruct((T, D), BF16)),
        compiler_params=_cp(("parallel",)),
    )(x, z, w_post, w_pre)


def _post_loss(x, z, w_post, tgt):
    def body(x_ref, z_ref, w_ref, t_ref, g_ref, l_ref, dz_ref, dw_ref):
        @pl.when(pl.program_id(0) == 0)
        def _():
            l_ref[...] = jnp.zeros_like(l_ref)
            dw_ref[...] = jnp.zeros_like(dw_ref)

        zv = z_ref[...]
        e = x_ref[...] + zv * _rms(zv) * w_ref[...] - t_ref[...]
        g = e * (1.0 / D)
        g_ref[...] = g
        l_ref[...] += jnp.sum(e * e) * (0.5 / D)
        dz, dw = _rms_bwd(zv, w_ref[...], g)
        dz_ref[...] = dz.astype(BF16)
        dw_ref[...] += dw

    return pl.pallas_call(
        body, name="postnorm_loss", grid=(T // ROWS,), in_specs=[_ROW, _ROW, _VEC, _ROW],
        out_specs=(_ROW, pl.BlockSpec((1, 128), lambda i: (0, 0)), _ROW, _VEC),
        out_shape=(jax.ShapeDtypeStruct((T, D), F32), jax.ShapeDtypeStruct((1, 128), F32),
                   jax.ShapeDtypeStruct((T, D), BF16), jax.ShapeDtypeStruct((1, D), F32)),
        compiler_params=_cp(("arbitrary",)),
    )(x, z, w_post, tgt)


def _pre_post_bwd(x, w_pre, dh, add, z, w_post, name):
    def body(x_ref, wn_ref, dh_ref, add_ref, z_ref, wp_ref, g_ref, dz_ref, dwn_ref, dwp_ref):
        @pl.when(pl.program_id(0) == 0)
        def _():
            dwn_ref[...] = jnp.zeros_like(dwn_ref)
            dwp_ref[...] = jnp.zeros_like(dwp_ref)

        dx, dwn = _rms_bwd(x_ref[...], wn_ref[...], dh_ref[...])
        g = dx + add_ref[...]
        g_ref[...] = g
        dz, dwp = _rms_bwd(z_ref[...], wp_ref[...], g)
        dz_ref[...] = dz.astype(BF16)
        dwn_ref[...] += dwn
        dwp_ref[...] += dwp

    return pl.pallas_call(
        body, name=name, grid=(T // ROWS,), in_specs=[_ROW, _VEC, _ROW, _ROW, _ROW, _VEC],
        out_specs=(_ROW, _ROW, _VEC, _VEC),
        out_shape=(jax.ShapeDtypeStruct((T, D), F32), jax.ShapeDtypeStruct((T, D), BF16),
                   jax.ShapeDtypeStruct((1, D), F32), jax.ShapeDtypeStruct((1, D), F32)),
        compiler_params=_cp(("arbitrary",)),
    )(x, w_pre, dh, add, z, w_post)


def _norm_bwd(z, w, dy, add, name):
    has_add = add is not None

    def body(*refs):
        if has_add:
            z_ref, w_ref, dy_ref, add_ref, dz_ref, dw_ref = refs
        else:
            z_ref, w_ref, dy_ref, dz_ref, dw_ref = refs
        i = pl.program_id(0)

        @pl.when(i == 0)
        def _():
            dw_ref[...] = jnp.zeros_like(dw_ref)

        zv = z_ref[...].astype(F32)
        dyv = dy_ref[...]
        r = lax.rsqrt(jnp.mean(zv * zv, axis=-1, keepdims=True) + EPS)
        wdy = dyv * w_ref[...]
        dz = r * wdy - zv * (r * r * r) * jnp.mean(zv * wdy, axis=-1, keepdims=True)
        if has_add:
            dz = dz + add_ref[...]
        dz_ref[...] = dz.astype(dz_ref.dtype)
        dw_ref[...] += jnp.sum(dyv * zv * r, axis=0, keepdims=True)

    row = pl.BlockSpec((ROWS, D), lambda i: (i, 0))
    vec = pl.BlockSpec((1, D), lambda i: (0, 0))
    ins = [z, w, dy] + ([add] if has_add else [])
    dz_dtype = F32 if has_add else BF16
    return pl.pallas_call(
        body, name=name, grid=(T // ROWS,),
        in_specs=[row, vec, row] + ([row] if has_add else []),
        out_specs=(row, vec),
        out_shape=(jax.ShapeDtypeStruct((T, D), dz_dtype), jax.ShapeDtypeStruct((1, D), F32)),
        compiler_params=_cp(("arbitrary",)),
    )(*ins)


def _adamw_math(w, g, m, v):
    c1 = 1.0 - ADAM_B1 ** ADAM_STEP
    c2 = 1.0 - ADAM_B2 ** ADAM_STEP
    mn = ADAM_B1 * m + (1.0 - ADAM_B1) * g
    vn = ADAM_B2 * v + (1.0 - ADAM_B2) * (g * g)
    return -ADAM_LR * ((mn / c1) / (jnp.sqrt(vn / c2) + ADAM_EPS) + ADAM_WD * w), mn, vn


def _adamw_from(w, m, v, sources, tr, name):
    layers, rows, cols = w.shape
    assert len(sources) == layers and rows % tr == 0, (name, w.shape)
    g_specs = []
    for layer, (buf, row0, transposed) in enumerate(sources):
        step = lambda l, i, layer=layer: jnp.where(l == layer, i, 0)
        if transposed:
            assert row0 % cols == 0 and buf.shape[1] == rows, (name, row0)
            g_specs.append(pl.BlockSpec((cols, tr), lambda l, i, b=row0 // cols, step=step: (b, step(l, i))))
        else:
            assert row0 % tr == 0 and buf.shape[1] == cols, (name, row0)
            g_specs.append(pl.BlockSpec((tr, cols), lambda l, i, b=row0 // tr, step=step: (b + step(l, i), 0)))

    def body(*refs):
        w_ref, m_ref, v_ref = refs[:3]
        g_refs = refs[3:3 + layers]
        g_out, d_ref, mo_ref, vo_ref = refs[3 + layers:]
        gs = [r[...].T if src[2] else r[...] for r, src in zip(g_refs, sources)]
        g = gs[0] if layers == 1 else jnp.where(pl.program_id(0) == 0, gs[0], gs[1])
        g_out[...] = g
        d_ref[...], mo_ref[...], vo_ref[...] = _adamw_math(w_ref[...], g, m_ref[...], v_ref[...])

    blk = pl.BlockSpec((None, tr, cols), lambda l, i: (l, i, 0))
    sds = jax.ShapeDtypeStruct(w.shape, F32)
    return pl.pallas_call(body, name=name, grid=(layers, rows // tr), in_specs=[blk] * 3 + g_specs,
                          out_specs=(blk,) * 4, out_shape=(sds,) * 4,
                          compiler_params=_cp(("parallel", "parallel")))(w, m, v, *[s[0] for s in sources])


def _adamw(w, g, m, v, name):
    lead = w.shape[:-2]
    assert len(lead) <= 1 and g.shape == w.shape, (name, w.shape, g.shape)
    rows, cols = w.shape[-2:]
    if rows <= 512:
        tr, tc = rows, cols
    elif rows % 256 == 0:
        tr, tc = 256, cols
    else:
        tr, tc = rows, 256
    assert rows % tr == 0 and cols % tc == 0, (name, w.shape)
    c1 = 1.0 - ADAM_B1 ** ADAM_STEP
    c2 = 1.0 - ADAM_B2 ** ADAM_STEP

    def body(w_ref, g_ref, m_ref, v_ref, d_ref, mo_ref, vo_ref):
        gv = g_ref[...]
        mn = ADAM_B1 * m_ref[...] + (1.0 - ADAM_B1) * gv
        vn = ADAM_B2 * v_ref[...] + (1.0 - ADAM_B2) * (gv * gv)
        m_hat = mn / c1
        v_hat = vn / c2
        d_ref[...] = -ADAM_LR * (m_hat / (jnp.sqrt(v_hat) + ADAM_EPS) + ADAM_WD * w_ref[...])
        mo_ref[...] = mn
        vo_ref[...] = vn

    if lead:
        grid = (lead[0], rows // tr, cols // tc)
        blk = pl.BlockSpec((None, tr, tc), lambda l, i, j: (l, i, j))
    else:
        grid = (rows // tr, cols // tc)
        blk = pl.BlockSpec((tr, tc), lambda i, j: (i, j))
    sds = jax.ShapeDtypeStruct(w.shape, F32)
    return pl.pallas_call(body, name=name, grid=grid, in_specs=[blk] * 4, out_specs=(blk,) * 3,
                          out_shape=(sds,) * 3, compiler_params=_cp(("parallel",) * len(grid)))(w, g, m, v)


def _running_sum(x, towards_later):
    n = x.shape[0]
    row = _iota(x.shape, 0)
    s = 1
    while s < n:
        if towards_later:
            x = x + jnp.where(row >= s, pltpu.roll(x, s, 0), 0.0)
        else:
            x = x + jnp.where(row < n - s, pltpu.roll(x, n - s, 0), 0.0)
        s *= 2
    return x


@jax.custom_vjp
def _cumsum_rows(x):
    return _running_sum(x, True)


_cumsum_rows.defvjp(lambda x: (_running_sum(x, True), None), lambda _, g: (_running_sum(g, False),))


def _gla_consts():
    return (_iota((256, 512), 0) // 64 == _iota((256, 512), 1) // 128).astype(F32)


def _gla_chunk(mask, q, k, v, r, aux, s_prev, wa, ba, nw):
    la = _log_sigmoid(bdot(aux, wa, "nn") + ba) * (1.0 / 16.0)
    cum = _cumsum_rows(la)
    total = jnp.sum(la, axis=0, keepdims=True)
    k_dec = k * jnp.exp(total - cum)
    inc = bdot(k_dec, v, "tn") * mask
    dec = jnp.exp(jnp.broadcast_to(total, (128, 256)).T)
    dec = jnp.concatenate([dec, dec, dec, dec], axis=1)
    s_new = dec * s_prev + inc
    o = bdot(q * GLA_SCALE, s_new, "nn")
    parts = []
    for h in range(4):
        oh = o[:, h * 128:(h + 1) * 128]
        parts.append(oh * lax.rsqrt(jnp.mean(oh * oh, axis=-1, keepdims=True) + EPS))
    on = jnp.concatenate(parts, axis=1)
    return s_new, on * nw * (r * _sigmoid(r))


GLA_PER_STEP = 4
GLA_ROWS = GLA_PER_STEP * CHUNK
GLA_STEPS = NCHUNK // GLA_PER_STEP


def _gla_specs(cmap):
    return [pl.BlockSpec((GLA_ROWS, 256), lambda c: (cmap(c), 0)),
            pl.BlockSpec((GLA_ROWS, 256), lambda c: (cmap(c), 1)),
            pl.BlockSpec((GLA_ROWS, 512), lambda c: (cmap(c), 1)),
            pl.BlockSpec((GLA_ROWS, 512), lambda c: (cmap(c), 2)),
            pl.BlockSpec((GLA_ROWS, 128), lambda c: (cmap(c), AUX_BLK))]


def _gla_fwd(proj, wa, ba, nw):
    def body(q_ref, k_ref, v_ref, r_ref, aux_ref, wa_ref, ba_ref, nw_ref, o_ref, sp_ref, s_ref):
        @pl.when(pl.program_id(0) == 0)
        def _():
            s_ref[...] = jnp.zeros_like(s_ref)

        s = s_ref[...]
        consts = _gla_consts()
        outs, states = [], []
        for i in range(GLA_PER_STEP):
            rows = slice(i * CHUNK, (i + 1) * CHUNK)
            states.append(s)
            s, out = _gla_chunk(consts, q_ref[rows, :], k_ref[rows, :], v_ref[rows, :], r_ref[rows, :], aux_ref[rows, :],
                                s, wa_ref[...], ba_ref[...], nw_ref[...])
            outs.append(out)
        s_ref[...] = s
        for i in range(GLA_PER_STEP):
            o_ref[i * CHUNK:(i + 1) * CHUNK, :] = outs[i]
            sp_ref[i] = states[i]

    full = lambda shape: pl.BlockSpec(shape, lambda c: (0,) * len(shape))
    return pl.pallas_call(
        body, name="gla_fwd", grid=(GLA_STEPS,),
        in_specs=_gla_specs(lambda c: c) + [full((128, 256)), full((1, 256)), full((1, 512))],
        out_specs=(pl.BlockSpec((GLA_ROWS, 512), lambda c: (c, 0)),
                   pl.BlockSpec((GLA_PER_STEP, 256, 512), lambda c: (c, 0, 0))),
        out_shape=(jax.ShapeDtypeStruct((T, D), F32), jax.ShapeDtypeStruct((NCHUNK, 256, 512), F32)),
        scratch_shapes=[pltpu.VMEM((256, 512), F32)],
        compiler_params=_cp(("arbitrary",)),
    )(proj, proj, proj, proj, proj, wa, ba, nw)


def _gla_bwd(proj, s_prev_all, wa, ba, nw, dcat):
    rev = lambda c: GLA_STEPS - 1 - c

    def body(q_ref, k_ref, v_ref, r_ref, aux_ref, sp_ref, wa_ref, ba_ref, nw_ref, do_ref,
             dq_ref, dk_ref, dv_ref, dr_ref, daux_ref, dwa_ref, dba_ref, dnw_ref, ds_ref):
        @pl.when(pl.program_id(0) == 0)
        def _():
            ds_ref[...] = jnp.zeros_like(ds_ref)
            dwa_ref[...] = jnp.zeros_like(dwa_ref)
            dba_ref[...] = jnp.zeros_like(dba_ref)
            dnw_ref[...] = jnp.zeros_like(dnw_ref)

        fn = functools.partial(_gla_chunk, _gla_consts())
        ds = ds_ref[...]
        dwa, dba, dnw = dwa_ref[...], dba_ref[...], dnw_ref[...]
        grads = {}
        for i in reversed(range(GLA_PER_STEP)):
            rows = slice(i * CHUNK, (i + 1) * CHUNK)
            _, vjp = jax.vjp(fn, q_ref[rows, :], k_ref[rows, :], v_ref[rows, :], r_ref[rows, :], aux_ref[rows, :],
                             sp_ref[i], wa_ref[...], ba_ref[...], nw_ref[...])
            *grads[i], ds, dwa_i, dba_i, dnw_i = vjp((ds, do_ref[rows, :]))
            dwa, dba, dnw = dwa + dwa_i, dba + dba_i, dnw + dnw_i
        ds_ref[...] = ds
        dwa_ref[...] = dwa
        dba_ref[...] = dba
        dnw_ref[...] = dnw
        for i in range(GLA_PER_STEP):
            rows = slice(i * CHUNK, (i + 1) * CHUNK)
            for ref, g in zip((dq_ref, dk_ref, dv_ref, dr_ref, daux_ref), grads[i]):
                ref[rows, :] = g

    full = lambda shape: pl.BlockSpec(shape, lambda c: (0,) * len(shape))
    blk = lambda w: pl.BlockSpec((GLA_ROWS, w), lambda c: (rev(c), 0))
    sds = lambda *s: jax.ShapeDtypeStruct(s, F32)
    return pl.pallas_call(
        body, name="gla_bwd", grid=(GLA_STEPS,),
        in_specs=_gla_specs(rev) + [pl.BlockSpec((GLA_PER_STEP, 256, 512), lambda c: (rev(c), 0, 0)),
                                    full((128, 256)), full((1, 256)), full((1, 512)), blk(512)],
        out_specs=(blk(256), blk(256), blk(512), blk(512), blk(128), full((128, 256)), full((1, 256)), full((1, 512))),
        out_shape=(sds(T, 256), sds(T, 256), sds(T, 512), sds(T, 512), sds(T, 128),
                   sds(128, 256), sds(1, 256), sds(1, 512)),
        scratch_shapes=[pltpu.VMEM((256, 512), F32)],
        compiler_params=_cp(("arbitrary",)),
    )(proj, proj, proj, proj, proj, s_prev_all, wa, ba, nw, dcat)


def _prefix8(x, towards_later):
    row = _iota(x.shape, 0)
    for s in (1, 2, 4):
        if towards_later:
            keep, shift = row >= s, s
        else:
            keep, shift = row < 8 - s, 8 - s
        x = x + jnp.where(keep, pltpu.roll(x, shift, 0), 0.0)
    return x


def _fox_gate_fwd(proj, bpad):
    def body(aux_ref, b_ref, cum_ref):
        cum_ref[...] = _log_sigmoid(aux_ref[...] + b_ref[...])

        def step(i, carry):
            rows = pl.ds(pl.multiple_of(i * 8, 8), 8)
            cum = _prefix8(cum_ref[rows, :], True) + carry
            cum_ref[rows, :] = cum
            return jnp.broadcast_to(cum[7:, :], (8, 128))

        lax.fori_loop(0, T // 8, step, jnp.zeros((8, 128), F32), unroll=4)

    return pl.pallas_call(
        body, name="fox_gate_fwd", grid=(1,),
        in_specs=[pl.BlockSpec((T, 128), lambda i: (0, AUX_BLK)), pl.BlockSpec((1, 128), lambda i: (0, 0))],
        out_specs=pl.BlockSpec((T, 128), lambda i: (0, 0)),
        out_shape=jax.ShapeDtypeStruct((T, 128), F32),
        compiler_params=_cp(("arbitrary",)),
    )(proj, bpad)


def _fox_gate_bwd(proj, bpad, dccol_t, daux_gla):
    def body(aux_ref, b_ref, dc_ref, dg_ref, daux_ref, db_ref):
        def step(i, carry):
            rows = pl.ds(pl.multiple_of(T - 8 * (i + 1), 8), 8)
            dlf = _prefix8(dc_ref[rows, :], False) + carry
            daux_ref[rows, :] = dlf
            return jnp.broadcast_to(dlf[:1, :], (8, 128))

        lax.fori_loop(0, T // 8, step, jnp.zeros((8, 128), F32), unroll=4)
        dz = daux_ref[...] * _sigmoid(-(aux_ref[...] + b_ref[...]))
        daux_ref[...] = dz + dg_ref[...]
        db_ref[...] = jnp.sum(dz, axis=0, keepdims=True)

    whole = pl.BlockSpec((T, 128), lambda i: (0, 0))
    vec = pl.BlockSpec((1, 128), lambda i: (0, 0))
    return pl.pallas_call(
        body, name="fox_gate_bwd", grid=(1,),
        in_specs=[pl.BlockSpec((T, 128), lambda i: (0, AUX_BLK)), vec, whole, whole],
        out_specs=(whole, vec),
        out_shape=(jax.ShapeDtypeStruct((T, 128), F32), jax.ShapeDtypeStruct((1, 128), F32)),
        compiler_params=_cp(("arbitrary",)),
    )(proj, bpad, dccol_t, daux_gla)


FOX_Q = 256
FOX_QB = T // FOX_Q
FOX_QF = 512


@jax.custom_vjp
def _attend(s, v):
    return _attend_fwd(s, v)[0]


def _attend_fwd(s, v):
    e = jnp.exp(s - jnp.max(s, axis=-1, keepdims=True))
    r = 1.0 / jnp.sum(e, axis=-1, keepdims=True)
    return _dot(e, v, "nn") * r, (e, r, v)


def _attend_bwd(res, do):
    e, r, v = res
    do_r = do * r
    dpr = _dot(do_r, v, "nt")
    ds = e * (dpr - r * jnp.sum(e * dpr, axis=-1, keepdims=True))
    return ds, _dot(e, do_r, "tn").astype(v.dtype)


_attend.defvjp(_attend_fwd, _attend_bwd)


def _fox_block(hp, q, k, v, ccol):
    fq, kl = q.shape[0], k.shape[0]
    lane = _iota((fq, 128), 1)
    tri = jnp.bitwise_and(_iota((2 * fq, fq), 0), fq - 1) >= _iota((2 * fq, fq), 1)
    sub = _iota((8, kl), 0)
    qs = q * ATT_SCALE
    q2 = jnp.concatenate([jnp.where(lane < 64, qs, 0.0), jnp.where(lane >= 64, qs, 0.0)], axis=0)
    s = bdot(q2, k, "nt")
    cs = [jnp.sum(jnp.where(sub == 2 * hp + e, ccol, 0.0), axis=0, keepdims=True) for e in range(2)]
    s = jnp.concatenate([s[:fq] - cs[0], s[fq:] - cs[1]], axis=0)
    diag = jnp.where(tri, s[:, kl - fq:], NEG)
    s = diag if kl == fq else jnp.concatenate([s[:, :kl - fq], diag], axis=1)
    o2 = _attend(s, v)
    return jnp.where(lane < 64, o2[:fq], o2[fq:])


def _fox_in_specs(fq):
    return [pl.BlockSpec((fq, 128), lambda hp, qb: (qb, 12 + hp)),
            pl.BlockSpec((T, 128), lambda hp, qb: (0, 16 + hp)),
            pl.BlockSpec((T, 128), lambda hp, qb: (0, 20 + hp)),
            pl.BlockSpec((8, T), lambda hp, qb: (0, 0))]


def _fox_fwd(proj, cum_c, cat):
    def body(q_ref, k_ref, v_ref, cc_ref, cat_ref, o_ref):
        qb = pl.program_id(1)
        for g in range(T // FOX_QF):
            kl = FOX_QF * (g + 1)

            @pl.when(qb == g)
            def _(kl=kl):
                o_ref[...] = _fox_block(pl.program_id(0), q_ref[...], k_ref[0:kl, :], v_ref[0:kl, :], cc_ref[:, 0:kl])

    return pl.pallas_call(
        body, name="fox_fwd", grid=(4, T // FOX_QF),
        in_specs=_fox_in_specs(FOX_QF) + [pl.BlockSpec(memory_space=pl.ANY)],
        out_specs=pl.BlockSpec((FOX_QF, 128), lambda hp, qb: (qb, 4 + hp)),
        out_shape=jax.ShapeDtypeStruct((T, D), F32), input_output_aliases={4: 0},
        compiler_params=_cp(("parallel", "parallel")),
    )(proj, proj, proj, cum_c, cat)


def _fox_bwd(proj, cum_c, dcat):
    def body(q_ref, k_ref, v_ref, cc_ref, do_ref, dq_ref, dk_ref, dv_ref, dcc_ref):
        qb = pl.program_id(1)

        @pl.when(qb == 0)
        def _():
            dk_ref[...] = jnp.zeros_like(dk_ref)
            dv_ref[...] = jnp.zeros_like(dv_ref)
            dcc_ref[...] = jnp.zeros_like(dcc_ref)

        fn = functools.partial(_fox_block, pl.program_id(0))
        for g in range(FOX_QB):
            kl = FOX_Q * (g + 1)

            @pl.when(qb == g)
            def _(kl=kl):
                _, vjp = jax.vjp(fn, q_ref[...], k_ref[0:kl, :], v_ref[0:kl, :], cc_ref[:, 0:kl])
                dq, dk, dv, dcc = vjp(do_ref[...])
                dq_ref[...] = dq
                dk_ref[0:kl, :] += dk
                dv_ref[0:kl, :] += dv
                dcc_ref[:, 0:kl] += dcc

    sds = lambda *s: jax.ShapeDtypeStruct(s, F32)
    return pl.pallas_call(
        body, name="fox_bwd", grid=(4, FOX_QB),
        in_specs=_fox_in_specs(FOX_Q) + [pl.BlockSpec((FOX_Q, 128), lambda hp, qb: (qb, 4 + hp))],
        out_specs=(pl.BlockSpec((FOX_Q, 128), lambda hp, qb: (qb, hp)),
                   pl.BlockSpec((T, 128), lambda hp, qb: (0, hp)),
                   pl.BlockSpec((T, 128), lambda hp, qb: (0, hp)),
                   pl.BlockSpec((None, 8, T), lambda hp, qb: (hp, 0, 0))),
        out_shape=(sds(T, 512), sds(T, 512), sds(T, 512), sds(4, 8, T)),
        compiler_params=_cp(("parallel", "arbitrary")),
    )(proj, proj, proj, cum_c, dcat)


BIAS_W = 640


def _rel_onehot():
    j = _iota((REL_PAD, BIAS_W), 1)
    rel = jnp.clip(CA_PAD + CHUNK - 1 - j, -128, 128) + 128
    return (_iota((REL_PAD, BIAS_W), 0) == rel).astype(F32)


def _bias_build(rbp):
    def body(rb_ref, o_ref):
        f = _hdot_raw(rb_ref[...], _rel_onehot(), "nn")
        for q in range(CHUNK):
            o_ref[q] = pltpu.roll(f, (BIAS_W - (CHUNK - 1 - q)) % BIAS_W, 1)[:, :CA_BAND]

    return pl.pallas_call(body, name="ca_bias_build", out_shape=jax.ShapeDtypeStruct((CHUNK, 8, CA_BAND), F32))(rbp)


def _bias_grad(dbias_q):
    def body(db_ref, o_ref):
        acc = jnp.zeros((8, BIAS_W), F32)
        for q in range(CHUNK):
            acc = acc + pltpu.roll(db_ref[q], CHUNK - 1 - q, 1)
        o_ref[...] = _hdot_raw(acc, _rel_onehot(), "nt")

    return pl.pallas_call(body, name="ca_bias_grad", out_shape=jax.ShapeDtypeStruct((8, REL_PAD), F32))(dbias_q)


def _ca_block(c, masked, q, kb, vb, bias2):
    lane = _iota((CHUNK, 128), 1)
    qs = q * ATT_SCALE
    q2 = jnp.concatenate([jnp.where(lane < 64, qs, 0.0), jnp.where(lane >= 64, qs, 0.0)], axis=0)
    s = bdot(q2, kb, "nt") + bias2.reshape(2 * CHUNK, CA_BAND)
    if masked:
        s = jnp.where((c * CHUNK - CA_PAD + _iota((2 * CHUNK, CA_BAND), 1)) >= 0, s, NEG)
    o2 = _attend(s, vb)
    return jnp.where(lane < 64, o2[:CHUNK], o2[CHUNK:])


CA_PER_STEP = 16
CA_ROWS = CA_PER_STEP * CHUNK
CA_MASKED_CHUNKS = CA_PAD // CHUNK
assert CA_PER_STEP >= CA_MASKED_CHUNKS
CA_MASKED_STEPS = 1


def _ca_fwd(proj, kvpad, bias):
    def body(q_ref, k_ref, v_ref, b_ref, o_ref):
        def run(masked):
            outs = []
            for i in range(CA_PER_STEP):
                c = pl.program_id(1) * CA_PER_STEP + i
                band = pl.ds(pl.multiple_of(c * CHUNK, CHUNK), CA_BAND)
                rows = slice(i * CHUNK, (i + 1) * CHUNK)
                outs.append(_ca_block(c, masked and i < CA_MASKED_CHUNKS, q_ref[rows, :], k_ref[band, :], v_ref[band, :],
                                      b_ref[...]))
            for i in range(CA_PER_STEP):
                o_ref[i * CHUNK:(i + 1) * CHUNK, :] = outs[i]

        pl.when(pl.program_id(1) < CA_MASKED_STEPS)(lambda: run(True))
        pl.when(pl.program_id(1) >= CA_MASKED_STEPS)(lambda: run(False))

    return pl.pallas_call(
        body, name="ca_fwd", grid=(4, NCHUNK // CA_PER_STEP),
        in_specs=[pl.BlockSpec((CA_ROWS, 128), lambda hp, c: (c, hp)),
                  pl.BlockSpec((T + CA_PAD, 128), lambda hp, c: (0, hp)),
                  pl.BlockSpec((T + CA_PAD, 128), lambda hp, c: (0, 4 + hp)),
                  pl.BlockSpec((2, CHUNK, CA_BAND), lambda hp, c: (hp, 0, 0))],
        out_specs=pl.BlockSpec((CA_ROWS, 128), lambda hp, c: (c, hp)),
        out_shape=jax.ShapeDtypeStruct((T, D), F32),
        compiler_params=_cp(("parallel", "parallel")),
    )(proj, kvpad, kvpad, bias)


def _ca_bwd(proj, kvpad, bias, dcat):
    def body(q_ref, k_ref, v_ref, b_ref, do_ref, dq_ref, dk_ref, dv_ref, db_ref):
        c = pl.program_id(1)

        @pl.when(c == 0)
        def _():
            dk_ref[...] = jnp.zeros_like(dk_ref)
            dv_ref[...] = jnp.zeros_like(dv_ref)
            db_ref[...] = jnp.zeros_like(db_ref)

        def run(masked):
            grads, bands = [], []
            for i in range(CA_PER_STEP):
                ci = c * CA_PER_STEP + i
                band = pl.ds(pl.multiple_of(ci * CHUNK, CHUNK), CA_BAND)
                rows = slice(i * CHUNK, (i + 1) * CHUNK)
                fn = functools.partial(_ca_block, ci, masked and i < CA_MASKED_CHUNKS)
                _, vjp = jax.vjp(fn, q_ref[rows, :], k_ref[band, :], v_ref[band, :], b_ref[...])
                grads.append(vjp(do_ref[rows, :]))
                bands.append(band)
            for i, (dq, _, _, _) in enumerate(grads):
                dq_ref[i * CHUNK:(i + 1) * CHUNK, :] = dq
            for band, (_, dkb, dvb, _) in zip(bands, grads):
                dk_ref[band, :] += dkb
                dv_ref[band, :] += dvb
            db_ref[...] += functools.reduce(lambda a, b: a + b, [g[3] for g in grads])

        pl.when(c < CA_MASKED_STEPS)(lambda: run(True))
        pl.when(c >= CA_MASKED_STEPS)(lambda: run(False))

    sds = lambda *s: jax.ShapeDtypeStruct(s, F32)
    padded = lambda: pl.BlockSpec((T + CA_PAD, 128), lambda hp, c: (0, hp))
    return pl.pallas_call(
        body, name="ca_bwd", grid=(4, NCHUNK // CA_PER_STEP),
        in_specs=[pl.BlockSpec((CA_ROWS, 128), lambda hp, c: (c, hp)),
                  pl.BlockSpec((T + CA_PAD, 128), lambda hp, c: (0, hp)),
                  pl.BlockSpec((T + CA_PAD, 128), lambda hp, c: (0, 4 + hp)),
                  pl.BlockSpec((2, CHUNK, CA_BAND), lambda hp, c: (hp, 0, 0)),
                  pl.BlockSpec((CA_ROWS, 128), lambda hp, c: (c, hp))],
        out_specs=(pl.BlockSpec((CA_ROWS, 128), lambda hp, c: (c, hp)), padded(), padded(),
                   pl.BlockSpec((2, CHUNK, CA_BAND), lambda hp, c: (hp, 0, 0))),
        out_shape=(sds(T, 512), sds(T + CA_PAD, 512), sds(T + CA_PAD, 512), sds(8, CHUNK, CA_BAND)),
        compiler_params=_cp(("parallel", "arbitrary")),
    )(proj, kvpad, kvpad, bias, dcat)


def _block_diag_dot(x, w):
    return jnp.concatenate([bdot(x[:, :256], w[:256, :256], "nn"), bdot(x[:, 256:], w[256:, 256:], "nn")], axis=1)


def _lru_pre(xs, cw, cb, wa, ba, wx, bx, lam):
    xc = cb + xs[0] * cw[0:1, :] + xs[1] * cw[1:2, :] + xs[2] * cw[2:3, :] + xs[3] * cw[3:4, :]
    ra = _sigmoid(_block_diag_dot(xc, wa) + ba)
    ii = _sigmoid(_block_diag_dot(xc, wx) + bx)
    la = 8.0 * ra * _log_sigmoid(lam)
    return jnp.exp(la), jnp.sqrt(-_expm1(2.0 * la)) * (ii * xc)


def _lru_pre_specs():
    full = lambda shape: pl.BlockSpec(shape, lambda i: (0,) * len(shape))
    return [pl.BlockSpec((4, ROWS, 512), lambda i: (0, i, 0)), full((4, 512)), full((1, 512)),
            full((512, 512)), full((1, 512)), full((512, 512)), full((1, 512)), full((1, 512))]


def _lru_pre_fwd(xs, cw, cb, wa, ba, wx, bx, lam):
    def body(xs_ref, cw_ref, cb_ref, wa_ref, ba_ref, wx_ref, bx_ref, lam_ref, a_ref, b_ref):
        a, b = _lru_pre(xs_ref[...], cw_ref[...], cb_ref[...], wa_ref[...], ba_ref[...], wx_ref[...], bx_ref[...],
                        lam_ref[...])
        a_ref[...] = a
        b_ref[...] = b

    row = pl.BlockSpec((ROWS, 512), lambda i: (i, 0))
    sds = jax.ShapeDtypeStruct((T, 512), F32)
    return pl.pallas_call(body, name="lru_pre_fwd", grid=(T // ROWS,), in_specs=_lru_pre_specs(),
                          out_specs=(row, row), out_shape=(sds, sds), compiler_params=_cp(("parallel",)),
                          )(xs, cw, cb, wa, ba, wx, bx, lam)


def _lru_pre_bwd(xs, cw, cb, wa, ba, wx, bx, lam, da, db):
    def body(xs_ref, cw_ref, cb_ref, wa_ref, ba_ref, wx_ref, bx_ref, lam_ref, da_ref, db_ref,
             dxs_ref, dcw_ref, dcb_ref, dwa_ref, dba_ref, dwx_ref, dbx_ref, dlam_ref):
        acc = (dcw_ref, dcb_ref, dwa_ref, dba_ref, dwx_ref, dbx_ref, dlam_ref)

        @pl.when(pl.program_id(0) == 0)
        def _():
            for r in acc:
                r[...] = jnp.zeros_like(r)

        _, vjp = jax.vjp(_lru_pre, xs_ref[...], cw_ref[...], cb_ref[...], wa_ref[...], ba_ref[...], wx_ref[...],
                         bx_ref[...], lam_ref[...])
        grads = vjp((da_ref[...], db_ref[...]))
        dxs_ref[...] = grads[0]
        for r, g in zip(acc, grads[1:]):
            r[...] += g

    row = pl.BlockSpec((ROWS, 512), lambda i: (i, 0))
    specs = _lru_pre_specs()
    sds = lambda *s: jax.ShapeDtypeStruct(s, F32)
    return pl.pallas_call(
        body, name="lru_pre_bwd", grid=(T // ROWS,), in_specs=specs + [row, row], out_specs=tuple(specs),
        out_shape=(sds(4, T, 512), sds(4, 512), sds(1, 512), sds(512, 512), sds(1, 512), sds(512, 512), sds(1, 512),
                   sds(1, 512)),
        compiler_params=_cp(("arbitrary",)),
    )(xs, cw, cb, wa, ba, wx, bx, lam, da, db)


SCAN_ROWS = 8


def _scan8(a, b, towards_later):
    row = _iota((SCAN_ROWS, 512), 0)
    for s in (1, 2, 4):
        if towards_later:
            keep, shift = row >= s, s
        else:
            keep, shift = row < SCAN_ROWS - s, SCAN_ROWS - s
        a_s = jnp.where(keep, pltpu.roll(a, shift, 0), 1.0)
        b_s = jnp.where(keep, pltpu.roll(b, shift, 0), 0.0)
        b = a * b_s + b
        a = a * a_s
    return a, b


def _lru_scan_fwd(a, b):
    def body(a_ref, b_ref, h_ref):
        def step(i, carry):
            rows = pl.ds(pl.multiple_of(i * SCAN_ROWS, SCAN_ROWS), SCAN_ROWS)
            a8, b8 = _scan8(a_ref[rows, :], b_ref[rows, :], True)
            h = a8 * carry + b8
            h_ref[rows, :] = h
            return jnp.broadcast_to(h[SCAN_ROWS - 1:, :], (SCAN_ROWS, 512))

        lax.fori_loop(0, T // SCAN_ROWS, step, jnp.zeros((SCAN_ROWS, 512), F32), unroll=2)

    return pl.pallas_call(body, name="lru_scan_fwd", out_shape=jax.ShapeDtypeStruct((T, 512), F32),
                          compiler_params=pltpu.CompilerParams(vmem_limit_bytes=VMEM_LIMIT))(a, b)


def _lru_scan_bwd(a_next, h_prev, dh):
    def body(a_ref, h_ref, dh_ref, da_ref, db_ref):
        def step(i, carry):
            start = T - SCAN_ROWS * (i + 1)
            rows = pl.ds(pl.multiple_of(start, SCAN_ROWS), SCAN_ROWS)
            a8, b8 = _scan8(a_ref[rows, :], dh_ref[rows, :], False)
            g = a8 * carry + b8
            db_ref[rows, :] = g
            da_ref[rows, :] = g * h_ref[rows, :]
            return jnp.broadcast_to(g[:1, :], (SCAN_ROWS, 512))

        lax.fori_loop(0, T // SCAN_ROWS, step, jnp.zeros((SCAN_ROWS, 512), F32), unroll=2)

    sds = jax.ShapeDtypeStruct((T, 512), F32)
    return pl.pallas_call(body, name="lru_scan_bwd", out_shape=(sds, sds),
                          compiler_params=pltpu.CompilerParams(vmem_limit_bytes=VMEM_LIMIT))(a_next, h_prev, dh)


def _lru_post(h, gate):
    return h * _gelu_tanh(gate)


def _lru_post_fwd(h, proj, cat):
    def body(h_ref, g_ref, cat_ref, o_ref):
        o_ref[...] = _lru_post(h_ref[...], g_ref[...])

    row = pl.BlockSpec((ROWS, 512), lambda i: (i, 0))
    return pl.pallas_call(body, name="lru_post_fwd", grid=(T // ROWS,),
                          in_specs=[row, pl.BlockSpec((ROWS, 512), lambda i: (i, 3)), pl.BlockSpec(memory_space=pl.ANY)],
                          out_specs=pl.BlockSpec((ROWS, 512), lambda i: (i, 1)),
                          out_shape=jax.ShapeDtypeStruct((T, D), F32), input_output_aliases={2: 0},
                          compiler_params=_cp(("parallel",)))(h, proj, cat)


def _lru_post_bwd(h, proj, dcat):
    def body(h_ref, g_ref, do_ref, dh_ref, dg_ref):
        _, vjp = jax.vjp(_lru_post, h_ref[...], g_ref[...])
        dh, dg = vjp(do_ref[...])
        dh_ref[...] = dh
        dg_ref[...] = dg

    row = pl.BlockSpec((ROWS, 512), lambda i: (i, 0))
    sds = jax.ShapeDtypeStruct((T, 512), F32)
    return pl.pallas_call(body, name="lru_post_bwd", grid=(T // ROWS,),
                          in_specs=[row, pl.BlockSpec((ROWS, 512), lambda i: (i, 3)),
                                    pl.BlockSpec((ROWS, 512), lambda i: (i, 1))],
                          out_specs=(row, row), out_shape=(sds, sds), compiler_params=_cp(("parallel",)))(h, proj, dcat)


def _conv_dx(dxs_shift):
    def body(d_ref, o_ref):
        o_ref[...] = d_ref[0] + d_ref[1] + d_ref[2] + d_ref[3]

    row = pl.BlockSpec((ROWS, 512), lambda i: (i, 0))
    return pl.pallas_call(body, name="lru_conv_dx", grid=(T // ROWS,),
                          in_specs=[pl.BlockSpec((4, ROWS, 512), lambda i: (0, i, 0))], out_specs=row,
                          out_shape=jax.ShapeDtypeStruct((T, 512), F32), compiler_params=_cp(("parallel",)))(dxs_shift)


def _position():
    return lax.axis_index("x"), lax.axis_index("y"), lax.axis_index("c")


def _other_chips(x, y):
    return [(1 - x, y), (x, 1 - y), (1 - x, 1 - y)]


def _al(v, n):
    return v * n if isinstance(v, int) else pl.multiple_of(v * n, n)


_AG_ITEMS = [
    ((4, 32, 128), lambda o, s, h: o.at[s, pl.ds(_al(h, 16), 16), :], lambda r, h: r.at[pl.ds(_al(h, 16), 16), :]),
    ((4, 774, 1024), lambda o, s, h: o.at[s, :, pl.ds(_al(h, 512), 512)], lambda r, h: r.at[:, pl.ds(_al(h, 512), 512)]),
    ((1024, 1024), lambda o, s, h: o.at[pl.ds(_al(2 * s + h, 128), 128), :], lambda r, h: r.at[pl.ds(_al(h, 128), 128), :]),
    ((2, 1024, 4096), lambda o, s, h: o.at[h, :, pl.ds(_al(s, 1024), 1024)], lambda r, h: r.at[h]),
    ((2, 4096, 1024), lambda o, s, h: o.at[h, pl.ds(_al(s, 1024), 1024), :], lambda r, h: r.at[h]),
    ((1024, 2560), lambda o, s, h: o.at[pl.ds(_al(h, 512), 512), pl.ds(_al(s, 640), 640)],
     lambda r, h: r.at[pl.ds(_al(h, 512), 512), :]),
    ((1024, 1024), lambda o, s, h: o.at[pl.ds(_al(2 * s + h, 128), 128), :], lambda r, h: r.at[pl.ds(_al(h, 128), 128), :]),
]


_AG_GROUPS = [(0, 1, 2), (3,), (4,), (5, 6)]

_HBM = pl.BlockSpec(memory_space=pltpu.HBM)
_SEM = pl.BlockSpec(memory_space=pltpu.SEMAPHORE)
_SPLIT = dict(has_side_effects=pltpu.SideEffectType.DATAFLOW_SIDE_EFFECTING)


def _hbm(a):
    return pltpu.with_memory_space_constraint(a, pltpu.HBM)


def _ag_ici_copy(i, j, chip, c, slot, src_ref, land_ref, send_sems, recv_sems, k):
    _, dst, half = _AG_ITEMS[i]
    return pltpu.make_async_remote_copy(src_ref=half(src_ref, c), dst_ref=dst(land_ref, slot, c), send_sem=send_sems.at[k],
                                        recv_sem=recv_sems.at[k], device_id=(*chip, c), device_id_type=MESH)


def _ag_start(groups, shards, name):
    items_all = [i for g in groups for i in _AG_GROUPS[g]]
    n = len(items_all)
    ng = len(groups)
    lands = [lax.empty(_AG_ITEMS[i][0], shards[i].dtype) for i in items_all]

    def body(*refs):
        srcs, land_refs = dict(zip(items_all, refs[:n])), dict(zip(items_all, refs[n:2 * n]))
        sems = refs[2 * n:2 * n + 2 * ng]
        token = refs[-1]
        x, y, c = _position()
        me = 2 * x + y
        for gi, g in enumerate(groups):
            for t, i in enumerate(_AG_GROUPS[g]):
                for j, chip in enumerate(_other_chips(x, y)):
                    _ag_ici_copy(i, j, chip, c, me, srcs[i], land_refs[i], sems[2 * gi], sems[2 * gi + 1], 3 * t + j).start()
        token[...] = jnp.zeros_like(token)

    sem_shapes = []
    for g in groups:
        sem_shapes += [pltpu.SemaphoreType.DMA((3 * len(_AG_GROUPS[g]),))] * 2
    ops = [shards[i] for i in items_all] + lands
    out = pl.pallas_call(
        body, name=name,
        out_shape=tuple(sem_shapes) + tuple(pltpu.HBM(a.shape, a.dtype) for a in ops) + (jax.ShapeDtypeStruct((8, 128), F32),),
        in_specs=(_HBM,) * (2 * n),
        out_specs=(_SEM,) * (2 * ng) + (_HBM,) * (2 * n) + (pl.BlockSpec(memory_space=pltpu.VMEM),),
        input_output_aliases={i: 2 * ng + i for i in range(2 * n)},
        compiler_params=pltpu.CompilerParams(**_SPLIT),
    )(*[_hbm(a) for a in ops])
    sems, thru, token = out[:2 * ng], out[2 * ng:-1], out[-1]
    return ({g: (sems[2 * gi], sems[2 * gi + 1]) for gi, g in enumerate(groups)},
            dict(zip(items_all, thru[:n])), dict(zip(items_all, thru[n:])), token)


def _ag_wait(g, sems, srcs, lands, after):
    items = _AG_GROUPS[g]
    m = len(items)

    def body(*refs):
        src_refs, land_refs = refs[:m], refs[m:2 * m]
        send_sems, recv_sems = refs[2 * m], refs[2 * m + 1]
        x, y, c = _position()
        for t, i in enumerate(items):
            for j, chip in enumerate(_other_chips(x, y)):
                cp = _ag_ici_copy(i, j, chip, c, 2 * chip[0] + chip[1], src_refs[t], land_refs[t], send_sems, recv_sems,
                                  3 * t + j)
                cp.wait_send()
                cp.wait_recv()

    ops = [srcs[i] for i in items] + [lands[i] for i in items]
    out = pl.pallas_call(
        body, name=f"allgather_wait_{g}",
        out_shape=tuple(pltpu.HBM(a.shape, a.dtype) for a in ops),
        in_specs=(_HBM,) * (2 * m) + (_SEM, _SEM, pl.BlockSpec(memory_space=pl.ANY)),
        out_specs=(_HBM,) * (2 * m),
        input_output_aliases={i: i for i in range(2 * m)},
        compiler_params=pltpu.CompilerParams(**_SPLIT),
    )(*ops, sems[0], sems[1], after)
    return list(out[:m]), list(out[m:])


def _ag_forward(g, srcs, lands):
    return _ag_sibling(_AG_GROUPS[g], srcs, lands, False, f"allgather_forward_{g}")


def _ag_push_own(srcs, lands):
    items = tuple(sorted(lands))
    out = _ag_sibling(items, [srcs[i] for i in items], [lands[i] for i in items], True, "allgather_push_own")
    return dict(zip(items, out))


def _ag_sibling(items, srcs, lands, own, name):
    m = len(items)
    per = 2 if own else 3

    def body(*refs):
        src_refs, in_refs, out_refs = refs[:m], refs[m:2 * m], refs[2 * m:3 * m]
        send_sems, recv_sems = refs[3 * m:]
        x, y, c = _position()
        sibling = (x, y, 1 - c)
        me = 2 * x + y
        if own:
            mine = theirs = [(me, 0), (me, 1)]
        else:
            slots = [2 * chip[0] + chip[1] for chip in _other_chips(x, y)]
            mine, theirs = [(s, c) for s in slots], [(s, 1 - c) for s in slots]
        sends = []
        for t, i in enumerate(items):
            _, dst, half = _AG_ITEMS[i]
            for k, (slot, hc) in enumerate(mine):
                src = half(src_refs[t], hc) if own else dst(in_refs[t], slot, hc)
                sends.append(pltpu.make_async_remote_copy(
                    src_ref=src, dst_ref=dst(out_refs[t], slot, hc), send_sem=send_sems.at[per * t + k],
                    recv_sem=recv_sems.at[per * t + k], device_id=sibling, device_id_type=MESH))
        for cp in sends:
            cp.start()
        for t, i in enumerate(items):
            dst = _AG_ITEMS[i][1]
            for k, (slot, hc) in enumerate(theirs):
                there = dst(out_refs[t], slot, hc)
                pltpu.make_async_remote_copy(src_ref=there, dst_ref=there, send_sem=send_sems.at[per * t + k],
                                             recv_sem=recv_sems.at[per * t + k], device_id=sibling,
                                             device_id_type=MESH).wait_recv()
        for cp in sends:
            cp.wait_send()

    any_spec = pl.BlockSpec(memory_space=pl.ANY)
    return pl.pallas_call(
        body, name=name,
        in_specs=[any_spec] * (2 * m), out_specs=(any_spec,) * m,
        out_shape=tuple(jax.ShapeDtypeStruct(a.shape, a.dtype) for a in lands),
        input_output_aliases={m + t: t for t in range(m)},
        scratch_shapes=[pltpu.SemaphoreType.DMA((per * m,)), pltpu.SemaphoreType.DMA((per * m,))],
    )(*srcs, *lands)


def _ag_d2d_copy(i, land_ref, slot, half, send_sems, recv_sems, k):
    x, y, c = _position()
    part = _AG_ITEMS[i][1](land_ref, slot, half)
    return pltpu.make_async_remote_copy(src_ref=part, dst_ref=part, send_sem=send_sems.at[k], recv_sem=recv_sems.at[k],
                                        device_id=(x, y, 1 - c), device_id_type=MESH)


def _ag_forward_start(g, lands, carry):
    items = _AG_GROUPS[g]
    m = len(items)

    def body(*refs):
        land_refs, send_sems, recv_sems, token = refs[:m], refs[m + 1], refs[m + 2], refs[-1]
        x, y, c = _position()
        for t, i in enumerate(items):
            for k, chip in enumerate(_other_chips(x, y)):
                _ag_d2d_copy(i, land_refs[t], 2 * chip[0] + chip[1], c, send_sems, recv_sems, 3 * t + k).start()
        token[...] = jnp.zeros_like(token)

    ops = list(lands) + [carry]
    out = pl.pallas_call(
        body, name=f"allgather_forward_start_{g}",
        out_shape=(pltpu.SemaphoreType.DMA((3 * m,)), pltpu.SemaphoreType.DMA((3 * m,)))
        + tuple(pltpu.HBM(a.shape, a.dtype) for a in ops) + (jax.ShapeDtypeStruct((8, 128), F32),),
        in_specs=(_HBM,) * (m + 1),
        out_specs=(_SEM, _SEM) + (_HBM,) * (m + 1) + (pl.BlockSpec(memory_space=pltpu.VMEM),),
        input_output_aliases={t: 2 + t for t in range(m + 1)},
        compiler_params=pltpu.CompilerParams(**_SPLIT),
    )(*[_hbm(a) for a in ops])
    return (out[0], out[1], list(out[2:2 + m])), out[2 + m]


def _ag_forward_wait(g, started, after):
    items = _AG_GROUPS[g]
    m = len(items)
    send_sems, recv_sems, lands = started

    def body(*refs):
        land_refs, send_sems, recv_sems = refs[:m], refs[m], refs[m + 1]
        x, y, c = _position()
        for t, i in enumerate(items):
            for k, chip in enumerate(_other_chips(x, y)):
                slot = 2 * chip[0] + chip[1]
                _ag_d2d_copy(i, land_refs[t], slot, 1 - c, send_sems, recv_sems, 3 * t + k).wait_recv()
                _ag_d2d_copy(i, land_refs[t], slot, c, send_sems, recv_sems, 3 * t + k).wait_send()

    out = pl.pallas_call(
        body, name=f"allgather_forward_wait_{g}",
        out_shape=tuple(pltpu.HBM(a.shape, a.dtype) for a in lands),
        in_specs=(_HBM,) * m + (_SEM, _SEM, pl.BlockSpec(memory_space=pl.ANY)), out_specs=(_HBM,) * m,
        input_output_aliases={t: t for t in range(m)},
        compiler_params=pltpu.CompilerParams(**_SPLIT),
    )(*lands, send_sems, recv_sems, after)
    return list(out)


def _pair_swap_copy(g_ref, r_ref, send_sem, recv_sem):
    x, y, c = _position()
    hc = g_ref.shape[2] // 2
    return pltpu.make_async_remote_copy(src_ref=g_ref.at[:, :, pl.ds(_al(1 - c, hc), hc)], dst_ref=r_ref,
                                        send_sem=send_sem, recv_sem=recv_sem, device_id=(x, y, 1 - c),
                                        device_id_type=MESH)


def _pair_swap_start(gb, tag):
    _, rows, cols = gb.shape
    recv = lax.empty((4, rows, cols // 2), gb.dtype)

    def body(g_ref, r_ref, send_sem, recv_sem, g_thru, r_thru, token):
        _pair_swap_copy(g_ref, r_ref, send_sem, recv_sem).start()
        token[...] = jnp.zeros_like(token)

    return pl.pallas_call(
        body, name="grad_pair_swap_start_" + tag,
        out_shape=(pltpu.SemaphoreType.DMA(()), pltpu.SemaphoreType.DMA(()), pltpu.HBM(gb.shape, gb.dtype),
                   pltpu.HBM(recv.shape, recv.dtype), jax.ShapeDtypeStruct((8, 128), F32)),
        in_specs=(_HBM, _HBM), out_specs=(_SEM, _SEM, _HBM, _HBM, pl.BlockSpec(memory_space=pltpu.VMEM)),
        input_output_aliases={0: 2, 1: 3},
        compiler_params=pltpu.CompilerParams(**_SPLIT),
    )(_hbm(gb), _hbm(recv))


def _pair_swap_wait(started, after, tag):
    send_sem, recv_sem, gb, recv, _ = started

    def body(g_ref, r_ref, send_sem, recv_sem, after_ref, g_out, r_out):
        cp = _pair_swap_copy(g_ref, r_ref, send_sem, recv_sem)
        cp.wait_send()
        cp.wait_recv()

    return pl.pallas_call(
        body, name="grad_pair_swap_wait_" + tag,
        out_shape=(pltpu.HBM(gb.shape, gb.dtype), pltpu.HBM(recv.shape, recv.dtype)),
        in_specs=(_HBM, _HBM, _SEM, _SEM, pl.BlockSpec(memory_space=pl.ANY)), out_specs=(_HBM, _HBM),
        input_output_aliases={0: 0, 1: 1},
        compiler_params=pltpu.CompilerParams(**_SPLIT),
    )(gb, recv, send_sem, recv_sem, after)


def _handover_copy(r_ref, send_sem, recv_sem, core):
    x, y, c = _position()
    hc = r_ref.shape[1] // 2
    cols = r_ref.at[:, pl.ds(_al(core, hc), hc)]
    return pltpu.make_async_remote_copy(src_ref=cols, dst_ref=cols, send_sem=send_sem, recv_sem=recv_sem,
                                        device_id=(x, y, 1 - c), device_id_type=MESH)


def _handover_start(red, tag):
    def body(r_ref, send_sem, recv_sem, r_thru, token):
        _handover_copy(r_ref, send_sem, recv_sem, lax.axis_index("c")).start()
        token[...] = jnp.zeros_like(token)

    return pl.pallas_call(
        body, name="grad_handover_start_" + tag,
        out_shape=(pltpu.SemaphoreType.DMA(()), pltpu.SemaphoreType.DMA(()), pltpu.HBM(red.shape, red.dtype),
                   jax.ShapeDtypeStruct((8, 128), F32)),
        in_specs=(_HBM,), out_specs=(_SEM, _SEM, _HBM, pl.BlockSpec(memory_space=pltpu.VMEM)),
        input_output_aliases={0: 2},
        compiler_params=pltpu.CompilerParams(**_SPLIT),
    )(_hbm(red))


def _handover_wait(started, after, tag):
    send_sem, recv_sem, red, _ = started

    def body(r_ref, send_sem, recv_sem, after_ref, r_out):
        c = lax.axis_index("c")
        _handover_copy(r_ref, send_sem, recv_sem, c).wait_send()
        _handover_copy(r_ref, send_sem, recv_sem, 1 - c).wait_recv()

    return pl.pallas_call(
        body, name="grad_handover_wait_" + tag,
        out_shape=pltpu.HBM(red.shape, red.dtype),
        in_specs=(_HBM, _SEM, _SEM, pl.BlockSpec(memory_space=pl.ANY)), out_specs=_HBM,
        input_output_aliases={0: 0},
        compiler_params=pltpu.CompilerParams(**_SPLIT),
    )(red, send_sem, recv_sem, after)


def _handover(red, tag):
    started = _handover_start(red, tag)
    return _handover_wait(started, started[3], tag)


def _a2a_copy(j, chip, c, p_ref, q_ref, q_slot, send_sems, recv_sems):
    return pltpu.make_async_remote_copy(src_ref=p_ref.at[2 * chip[0] + chip[1]], dst_ref=q_ref.at[q_slot],
                                        send_sem=send_sems.at[j], recv_sem=recv_sems.at[j], device_id=(*chip, c),
                                        device_id_type=MESH)


def _a2a_start(p, tag):
    def body(p_ref, q_ref, send_sems, recv_sems, p_thru, q_thru, token):
        x, y, c = _position()
        for j, chip in enumerate(_other_chips(x, y)):
            _a2a_copy(j, chip, c, p_ref, q_ref, 2 * x + y, send_sems, recv_sems).start()
        token[...] = jnp.zeros_like(token)

    return pl.pallas_call(
        body, name="grad_alltoall_start_" + tag,
        out_shape=(pltpu.SemaphoreType.DMA((3,)), pltpu.SemaphoreType.DMA((3,)), pltpu.HBM(p.shape, p.dtype),
                   pltpu.HBM(p.shape, p.dtype), jax.ShapeDtypeStruct((8, 128), F32)),
        in_specs=(_HBM, _HBM), out_specs=(_SEM, _SEM, _HBM, _HBM, pl.BlockSpec(memory_space=pltpu.VMEM)),
        input_output_aliases={0: 2, 1: 3},
        compiler_params=pltpu.CompilerParams(**_SPLIT),
    )(_hbm(p), _hbm(lax.empty(p.shape, p.dtype)))


def _a2a_wait(send_sems, recv_sems, p, q, after, tag):
    def body(p_ref, q_ref, send_sems, recv_sems, after_ref, p_out, q_out):
        x, y, c = _position()
        for j, chip in enumerate(_other_chips(x, y)):
            cp = _a2a_copy(j, chip, c, p_ref, q_ref, 2 * chip[0] + chip[1], send_sems, recv_sems)
            cp.wait_send()
            cp.wait_recv()

    return pl.pallas_call(
        body, name="grad_alltoall_wait_" + tag,
        out_shape=(pltpu.HBM(p.shape, p.dtype), pltpu.HBM(q.shape, q.dtype)),
        in_specs=(_HBM, _HBM, _SEM, _SEM, pl.BlockSpec(memory_space=pl.ANY)), out_specs=(_HBM, _HBM),
        input_output_aliases={0: 0, 1: 1},
        compiler_params=pltpu.CompilerParams(**_SPLIT),
    )(p, q, send_sems, recv_sems, after)


def _comm_rows(rows):
    return next(t for t in (1536, 1152, 1024, 512, 384, 256, 128) if rows % t == 0)


def _pair_add(gb, recv, where, tag):
    _, rows, cols = gb.shape
    hc = cols // 2
    tr = _comm_rows(rows)

    def body(w_ref, g_ref, r_ref, o_ref):
        o_ref[...] = (g_ref[...].astype(F32) + r_ref[...].astype(F32)).astype(o_ref.dtype)

    return pl.pallas_call(
        body, name="grad_pair_add_" + tag,
        grid_spec=pltpu.PrefetchScalarGridSpec(
            num_scalar_prefetch=1, grid=(4, rows // tr),
            in_specs=[pl.BlockSpec((None, tr, hc), lambda s, j, w_ref: (s, j, w_ref[0])),
                      pl.BlockSpec((None, tr, hc), lambda s, j, w_ref: (s, j, 0))],
            out_specs=pl.BlockSpec((None, tr, hc), lambda s, j, w_ref: (s, j, 0))),
        out_shape=jax.ShapeDtypeStruct((4, rows, hc), gb.dtype),
        compiler_params=_cp(("parallel", "parallel")),
    )(where, gb, recv)


def _sum_chips(p, q, where, tag):
    _, rows, hc = q.shape
    tr = _comm_rows(rows)

    def body(w_ref, p_ref, qa_ref, qb_ref, qc_ref, o_ref):
        me = w_ref[1]
        own, qa, qb, qc = (r[...].astype(F32) for r in (p_ref, qa_ref, qb_ref, qc_ref))
        v0 = jnp.where(me == 0, own, qa)
        v1 = jnp.where(me == 1, own, jnp.where(me == 0, qa, qb))
        v2 = jnp.where(me == 2, own, jnp.where(me < 2, qb, qc))
        v3 = jnp.where(me == 3, own, qc)
        o_ref[...] = ((v0 + v1) + v2) + v3

    slot = lambda k: pl.BlockSpec((None, tr, hc), lambda j, w_ref: (w_ref[k], j, 0))
    return pl.pallas_call(
        body, name="grad_sum_chips_" + tag,
        grid_spec=pltpu.PrefetchScalarGridSpec(
            num_scalar_prefetch=1, grid=(rows // tr,),
            in_specs=[slot(1), slot(2), slot(3), slot(4)],
            out_specs=pl.BlockSpec((tr, hc), lambda j, w_ref: (j, w_ref[0]))),
        out_shape=jax.ShapeDtypeStruct((rows, 2 * hc), F32),
        compiler_params=_cp(("parallel",)),
    )(where, p, q, q, q)


def _shard_major(g, axis):
    shape = g.shape
    g = g.reshape(shape[:axis] + (4, shape[axis] // 4) + shape[axis + 1:])
    return jnp.moveaxis(g, axis, 0).reshape(4, -1)


def _unshard(g4, shape, axis):
    n = shape[axis] // 4
    g = g4.reshape((4,) + shape[:axis] + (n,) + shape[axis + 1:])
    return jnp.moveaxis(g, 0, axis).reshape(shape)


def _split(flat, shapes):
    out, off = [], 0
    for shp in shapes:
        n = 1
        for d in shp:
            n *= d
        out.append(flat[..., off:off + n].reshape(flat.shape[:-1] + tuple(shp)))
        off += n
    return out


def _even_rows_to_kernel(wt):
    return jnp.concatenate([wt[:1536], wt[1552:3088], wt[1536:1552], wt[3088:3096],
                            jnp.zeros((PE - 3096, wt.shape[1]), wt.dtype)], axis=0)


def _block_diag(w):
    eye = jnp.eye(8, dtype=w.dtype)
    return (w[:, :, None, :] * eye[:, None, :, None]).reshape(512, 512)


def _diag_blocks(g):
    eye = jnp.eye(8, dtype=g.dtype)
    return (g.reshape(8, 64, 8, 64) * eye[:, None, :, None]).sum(axis=2)


def _shift_down(a, s):
    return a if s == 0 else jnp.pad(a, ((s, 0), (0, 0)))[:a.shape[0]]


def _shift_up(a, s):
    return a if s == 0 else jnp.pad(a, ((0, s), (0, 0)))[s:]


SMALL_SHARDED_SHAPES = [(2, 4, 256), (16, 64), (4, 128), (128,), (128,), (128,), (128,)]
REPL_SHAPES = [(256,), (512,), (8,), (8, 257), (8, 64, 64), (8, 64, 64)]


def kernel(x, norm_w, w_in_even, gla_w_a_up, gla_b_a, gla_norm_w, fox_b_f, w_out_even, w_in_odd, rel_bias, conv_w, conv_b, lru_w_a, lru_b_a, lru_w_x, lru_b_x, lru_lambda, w_out_odd, w_mlp_up, w_mlp_down, loss_target, m_norm_w, m_w_in_even, m_gla_w_a_up, m_gla_b_a, m_gla_norm_w, m_fox_b_f, m_w_out_even, m_w_in_odd, m_rel_bias, m_conv_w, m_conv_b, m_lru_w_a, m_lru_b_a, m_lru_w_x, m_lru_b_x, m_lru_lambda, m_w_out_odd, m_w_mlp_up, m_w_mlp_down, v_norm_w, v_w_in_even, v_gla_w_a_up, v_gla_b_a, v_gla_norm_w, v_fox_b_f, v_w_out_even, v_w_in_odd, v_rel_bias, v_conv_w, v_conv_b, v_lru_w_a, v_lru_b_a, v_lru_w_x, v_lru_b_x, v_lru_lambda, v_w_out_odd, v_w_mlp_up, v_w_mlp_down):
    c_idx = lax.axis_index("c")

    small_local = [norm_w, gla_w_a_up[0], conv_w[0], conv_b[0], lru_b_a[0], lru_b_x[0], lru_lambda[0]]
    small_src = jnp.concatenate([a.reshape(-1) for a in small_local]).reshape(32, 128)
    first = {0: small_src, 1: w_in_even[0].T.astype(BF16), 2: w_out_even[0].astype(BF16)}
    sems0, srcs0, lands0, ag_token = _ag_start([0], first, "allgather_start_0")
    zero = ag_token[0, 0]
    later = {3: (w_mlp_up + zero).astype(BF16), 4: (w_mlp_down + zero).astype(BF16),
             5: (w_in_odd[0] + zero).astype(BF16), 6: (w_out_odd[0] + zero).astype(BF16)}
    sems1, srcs1, lands1, ag_token = _ag_start([1, 2, 3], later, "allgather_start_1")
    ag_sems, ag_srcs = {**sems0, **sems1}, {**srcs0, **srcs1}
    ag_lands = _ag_push_own(ag_srcs, {**lands0, **lands1})

    def gathered(g, after):
        srcs_g, lands_g = _ag_wait(g, ag_sems[g], ag_srcs, ag_lands, after)
        return _ag_forward(g, srcs_g, lands_g)

    small4, w_in_e4, w_out_e = gathered(0, ag_token)
    me = 2 * lax.axis_index("x") + lax.axis_index("y")
    others = [k + (k >= me).astype(jnp.int32) for k in range(3)]
    where = jnp.stack([c_idx, me] + others).astype(jnp.int32)

    w_in_e_t = _even_rows_to_kernel(w_in_e4.reshape(3096, D))
    g_small = _split(small4.reshape(4, 32 * 128), SMALL_SHARDED_SHAPES)
    nw_full = _unshard(g_small[0], (2, 4, 1024), 2)
    wa_up = _unshard(g_small[1], (16, 256), 1)
    cw = _unshard(g_small[2], (4, 512), 1)
    cb, lba, lbx, lam = [_unshard(g, (512,), 0).reshape(1, 512) for g in g_small[3:]]
    nw = lambda layer, i: nw_full[layer, i].reshape(1, D)

    wa_pad = jnp.pad(wa_up, ((0, 128 - 16), (0, 0)))
    gla_ba = gla_b_a.reshape(1, 256)
    gla_nw = gla_norm_w.reshape(1, 512)
    fox_bpad = jnp.pad(fox_b_f.reshape(1, 8), ((0, 0), (FOX_LANE0, 128 - FOX_LANE0 - 8)))
    rbp = jnp.pad(rel_bias[0], ((0, 0), (0, REL_PAD - 257)))
    wa_bd = _block_diag(lru_w_a[0])
    wx_bd = _block_diag(lru_w_x[0])

    x0 = x[0]
    tgt = loss_target[0]

    h0 = _prenorm(x0, nw(0, 0), "prenorm_l0_mix")
    proj_e = _mm(h0, w_in_e_t, "nt", tm=2048, tn=640, name="mm_in_even")
    cat0, s_prev = _gla_fwd(proj_e, wa_pad, gla_ba, gla_nw)
    cum_r = _fox_gate_fwd(proj_e, fox_bpad)
    cum_c = cum_r[:, FOX_LANE0:FOX_LANE0 + 8].T
    cat0 = _fox_fwd(proj_e, cum_c, cat0)
    fwd_up, cat0 = _ag_forward_start(1, _ag_wait(1, ag_sems[1], ag_srcs, ag_lands, cat0)[1], cat0)
    mix0 = _mm(cat0, w_out_e, "nn", tm=2048, tn=512, name="mm_out_even")
    x1, h1 = _post_pre_fwd(x0, mix0, nw(0, 1), nw(0, 2), "post_pre_l0_mix")
    w_up, = _ag_forward_wait(1, fwd_up, x1)
    fwd_dn, h1 = _ag_forward_start(2, _ag_wait(2, ag_sems[2], ag_srcs, ag_lands, w_up)[1], h1)
    a0, r0 = _mm(h1, w_up, "nn", tm=2048, tn=1024, b_layer=0, relu_pair=True, name="mm_up_l0")
    w_dn, = _ag_forward_wait(2, fwd_dn, a0)
    d0 = _mm(a0, w_dn, "nn", tm=1024, tn=512, b_layer=0, name="mm_down_l0")
    x2, h2 = _post_pre_fwd(x1, d0, nw(0, 3), nw(1, 0), "post_pre_l0_mlp")

    w_in_o, w_out_o = gathered(3, x2)
    proj_o = _mm(h2, w_in_o, "nn", tm=2048, tn=640, name="mm_in_odd")
    bias_q = _bias_build(rbp)
    bias = bias_q.transpose(1, 0, 2)
    kvpad = jnp.pad(proj_o[:, 512:1536], ((CA_PAD, 0), (0, 0)))
    cat1 = _ca_fwd(proj_o, kvpad, bias)
    x_in = proj_o[:, 2048:2560]
    xs = jnp.stack([_shift_down(x_in, 3 - j) for j in range(4)])
    lru_a, lru_b = _lru_pre_fwd(xs, cw, cb, wa_bd, lba, wx_bd, lbx, lam)
    hh = _lru_scan_fwd(lru_a, lru_b)
    cat1 = _lru_post_fwd(hh, proj_o, cat1)
    mix1 = _mm(cat1, w_out_o, "nn", tm=2048, tn=512, name="mm_out_odd")
    x3, h3 = _post_pre_fwd(x2, mix1, nw(1, 1), nw(1, 2), "post_pre_l1_mix")
    a1, r1 = _mm(h3, w_up, "nn", tm=2048, tn=1024, b_layer=1, relu_pair=True, name="mm_up_l1")
    d1 = _mm(a1, w_dn, "nn", tm=1024, tn=512, b_layer=1, name="mm_down_l1")
    g4, loss_part, dd1, dnw13 = _post_loss(x3, d1, nw(1, 3), tgt)
    loss = lax.psum(loss_part[0, 0], ("x", "y", "c"))

    def rs_begin(swap, after, tag):
        gb, recv = _pair_swap_wait(swap, after, tag)
        return _a2a_start(_pair_add(gb, recv, where, tag), tag)

    def rs_end(started, after, tag):
        send_sems, recv_sems, p, q, _ = started
        p, q = _a2a_wait(send_sems, recv_sems, p, q, after, tag)
        return _handover(_sum_chips(p, q, where, tag), tag)

    gba = lax.dynamic_update_slice(lax.empty((4, GA_ROWS, D), BF16), jnp.zeros((4, GA_UP - GA_GAP, D), BF16),
                                   (0, GA_GAP, 0))
    gba = _mm(a1, dd1, "tn", tm=512, tn=1024, into=(gba, 1024, GA_DN), name="mm_down_l1_dw")
    du1 = _mm(dd1, w_dn, "nt", tm=2048, tn=1024, b_layer=1, times2=r1, out_dtype=BF16, name="mm_down_l1_dx")
    gba = _mm(du1, h3, "tn", tm=512, tn=1024, into=(gba, 1024, GA_UP), name="mm_up_l1_dw")
    dh3 = _mm(du1, w_up, "nt", tm=1024, tn=512, b_layer=1, name="mm_up_l1_dx")
    g3, dmix1, dnw12, dnw11 = _pre_post_bwd(x3, nw(1, 2), dh3, g4, mix1, nw(1, 1), "pre_post_bwd_l1_mlp")
    gba = _mm(cat1, dmix1, "tn", tm=128, tn=1024, into=(gba, 256, GA_OUT_O), name="mm_out_odd_dw")
    dcat1 = _mm(dmix1, w_out_o, "nt", tm=2048, tn=512, name="mm_out_odd_dx")

    dq_c, dkpad, dvpad, dbias = _ca_bwd(proj_o, kvpad, bias, dcat1)
    g_rel = _bias_grad(jnp.pad(dbias.transpose(1, 0, 2), ((0, 0), (0, 0), (0, BIAS_W - CA_BAND))))[:, :257]
    dhh, dgate = _lru_post_bwd(hh, proj_o, dcat1)
    da_l, db_l = _lru_scan_bwd(_shift_up(lru_a, 1), _shift_down(hh, 1), dhh)
    dxs, g_cw, g_cb, g_wa_bd, g_lba, g_wx_bd, g_lbx, g_lam = _lru_pre_bwd(xs, cw, cb, wa_bd, lba, wx_bd, lbx, lam, da_l, db_l)
    dx_in = _conv_dx(jnp.stack([_shift_up(dxs[j], 3 - j) for j in range(4)]))
    dproj_o = jnp.concatenate([dq_c, dkpad[CA_PAD:], dvpad[CA_PAD:], dgate, dx_in], axis=1).astype(BF16)
    gba = _mm(dproj_o, h2, "tn", tm=128, tn=1024, into=(gba, 640, GA_IN_O), name="mm_in_odd_dw")
    swap_a = _pair_swap_start(gba, "a")
    dh2 = _mm(dproj_o, w_in_o, "nt", tm=1024, tn=512, name="mm_in_odd_dx")
    g2, dd0, dnw10, dnw03 = _pre_post_bwd(x2, nw(1, 0) + swap_a[4][0, 0], dh2, g3, d0, nw(0, 3), "pre_post_bwd_l1_mix")
    rs_a = rs_begin(swap_a, g2, "a")

    gbb = lax.empty((4, GB_ROWS, D), BF16)
    gbb = _mm(a0, dd0, "tn", tm=512, tn=1024, into=(gbb, 1024, GB_DN), name="mm_down_l0_dw")
    du0 = _mm(dd0, w_dn, "nt", tm=2048, tn=1024, b_layer=0, times2=r0, out_dtype=BF16, name="mm_down_l0_dx")
    gbb = _mm(du0, h1, "tn", tm=512, tn=1024, into=(gbb, 1024, GB_UP), name="mm_up_l0_dw")
    swap_b = _pair_swap_start(gbb, "b")
    dh1 = _mm(du0, w_up, "nt", tm=1024, tn=512, b_layer=0, name="mm_up_l0_dx")
    g1, dmix0, dnw02, dnw01 = _pre_post_bwd(x1, nw(0, 2) + (swap_b[4][0, 0] + rs_a[4][0, 0]), dh1, g2, mix0, nw(0, 1),
                                            "pre_post_bwd_l0_mlp")
    rs_b = rs_begin(swap_b, g1, "b")
    gbc = lax.empty((4, GC_ROWS, D), BF16)
    gbc = _mm(cat0, dmix0, "tn", tm=128, tn=1024, into=(gbc, 256, GC_OUT_E), name="mm_out_even_dw")
    dcat0 = _mm(dmix0, w_out_e, "nt", tm=2048, tn=512, name="mm_out_even_dx")

    dq_g, dk_g, dv_g, dr_g, daux_g, g_wa_pad, g_gla_ba, g_gla_nw = _gla_bwd(
        proj_e, s_prev, wa_pad, gla_ba, gla_nw + rs_b[4][0, 0], dcat0)
    dq_f, dk_f, dv_f, dccol = _fox_bwd(proj_e, cum_c, dcat0)
    dccol_t = jnp.pad(dccol.sum(axis=0).T, ((0, 0), (FOX_LANE0, 128 - FOX_LANE0 - 8)))
    daux, g_fox_bpad = _fox_gate_bwd(proj_e, fox_bpad, dccol_t, daux_g)
    dproj_e = jnp.concatenate([dq_g, dk_g, dv_g, dr_g, dq_f, dk_f, dv_f, daux], axis=1).astype(BF16)
    gt_in_e = _mm(dproj_e, h0, "tn", tm=640, tn=1024, out_dtype=BF16, name="mm_in_even_dw")
    dh0 = _mm(dproj_e, w_in_e_t, "nn", tm=1024, tn=512, name="mm_in_even_dx")
    grad_x, dnw00 = _norm_bwd(x0, nw(0, 0), dh0, g1, "prenorm_l0_mix_bwd")

    def rs_reduce(started, after, tag):
        send_sems, recv_sems, p, q, _ = started
        p, q = _a2a_wait(send_sems, recv_sems, p, q, after, tag)
        return _handover_start(_sum_chips(p, q, where, tag), tag)

    ho_a = rs_reduce(rs_a, grad_x, "a")
    ho_b = rs_reduce(rs_b, ho_a[3], "b")

    g_norm = jnp.stack([jnp.concatenate([dnw00, dnw01, dnw02, dnw03]), jnp.concatenate([dnw10, dnw11, dnw12, dnw13])])
    sharded = [(g_norm, 2), (g_wa_pad[:16], 1), (g_cw, 1), (g_cb[0], 0), (g_lba[0], 0), (g_lbx[0], 0), (g_lam[0], 0)]
    replicated = [g_gla_ba[0], g_gla_nw[0], g_fox_bpad[0, FOX_LANE0:FOX_LANE0 + 8], g_rel, _diag_blocks(g_wa_bd),
                  _diag_blocks(g_wx_bd)]
    small4 = jnp.concatenate([_shard_major(g, ax) for g, ax in sharded]
                             + [jnp.broadcast_to(g.reshape(1, -1), (4, g.size)) for g in replicated], axis=1)
    n_small = small4.shape[1]
    small_rows = GC_ROWS - GC_TAIL - 774
    small4 = jnp.pad(small4, ((0, 0), (0, small_rows * D - n_small))).reshape(4, small_rows, D)
    gt_rows = jnp.concatenate([gt_in_e[:1536], gt_in_e[3072:3088], gt_in_e[1536:3072], gt_in_e[3088:3096]], axis=0)
    tail = jnp.concatenate([gt_rows.reshape(4, 774, D), small4.astype(BF16)], axis=1)
    gbc = lax.dynamic_update_slice(gbc, tail, (0, GC_TAIL, 0))
    swap_c = _pair_swap_start(gbc, "c")
    rs_c = rs_begin(swap_c, swap_c[4], "c")

    red_a = _handover_wait(ho_a, rs_c[4], "a")
    red_b = _handover_wait(ho_b, red_a, "b")
    early = dict(
        w_mlp_up=_adamw_from(w_mlp_up, m_w_mlp_up, v_w_mlp_up, [(red_b, GB_UP, True), (red_a, GA_UP, True)], 512,
                             "adamw_w_mlp_up"),
        w_mlp_down=_adamw_from(w_mlp_down, m_w_mlp_down, v_w_mlp_down, [(red_b, GB_DN, False), (red_a, GA_DN, False)],
                               512, "adamw_w_mlp_down"),
        w_in_odd=_adamw_from(w_in_odd, m_w_in_odd, v_w_in_odd, [(red_a, GA_IN_O, True)], 256, "adamw_w_in_odd"),
        w_out_odd=_adamw_from(w_out_odd, m_w_out_odd, v_w_out_odd, [(red_a, GA_OUT_O, False)], 128, "adamw_w_out_odd"))
    red_c = rs_end(rs_c, early["w_out_odd"][3], "c")

    g_small = _split(red_c[GC_TAIL + 774:].reshape(-1)[:n_small], SMALL_SHARDED_SHAPES + REPL_SHAPES)
    g_of = dict(zip(["norm_w", "gla_w_a_up", "conv_w", "conv_b", "lru_b_a", "lru_b_x", "lru_lambda", "gla_b_a",
                     "gla_norm_w", "fox_b_f", "rel_bias", "lru_w_a", "lru_w_x"], g_small))
    g_of.update(w_in_even=red_c[GC_TAIL:GC_TAIL + 774])
    early["w_out_even"] = _adamw_from(w_out_even, m_w_out_even, v_w_out_even, [(red_c, GC_OUT_E, False)], 256,
                                      "adamw_w_out_even")

    names = ["norm_w", "w_in_even", "gla_w_a_up", "gla_b_a", "gla_norm_w", "fox_b_f", "w_out_even", "w_in_odd", "rel_bias",
             "conv_w", "conv_b", "lru_w_a", "lru_b_a", "lru_w_x", "lru_b_x", "lru_lambda", "w_out_odd", "w_mlp_up",
             "w_mlp_down"]
    w_of = dict(norm_w=norm_w, w_in_even=w_in_even, gla_w_a_up=gla_w_a_up, gla_b_a=gla_b_a, gla_norm_w=gla_norm_w,
                fox_b_f=fox_b_f, w_out_even=w_out_even, w_in_odd=w_in_odd, rel_bias=rel_bias, conv_w=conv_w, conv_b=conv_b,
                lru_w_a=lru_w_a, lru_b_a=lru_b_a, lru_w_x=lru_w_x, lru_b_x=lru_b_x, lru_lambda=lru_lambda,
                w_out_odd=w_out_odd, w_mlp_up=w_mlp_up, w_mlp_down=w_mlp_down)
    m_of = dict(norm_w=m_norm_w, w_in_even=m_w_in_even, gla_w_a_up=m_gla_w_a_up, gla_b_a=m_gla_b_a,
                gla_norm_w=m_gla_norm_w, fox_b_f=m_fox_b_f, w_out_even=m_w_out_even, w_in_odd=m_w_in_odd,
                rel_bias=m_rel_bias, conv_w=m_conv_w, conv_b=m_conv_b, lru_w_a=m_lru_w_a, lru_b_a=m_lru_b_a,
                lru_w_x=m_lru_w_x, lru_b_x=m_lru_b_x, lru_lambda=m_lru_lambda, w_out_odd=m_w_out_odd,
                w_mlp_up=m_w_mlp_up, w_mlp_down=m_w_mlp_down)
    v_of = dict(norm_w=v_norm_w, w_in_even=v_w_in_even, gla_w_a_up=v_gla_w_a_up, gla_b_a=v_gla_b_a,
                gla_norm_w=v_gla_norm_w, fox_b_f=v_fox_b_f, w_out_even=v_w_out_even, w_in_odd=v_w_in_odd,
                rel_bias=v_rel_bias, conv_w=v_conv_w, conv_b=v_conv_b, lru_w_a=v_lru_w_a, lru_b_a=v_lru_b_a,
                lru_w_x=v_lru_w_x, lru_b_x=v_lru_b_x, lru_lambda=v_lru_lambda, w_out_odd=v_w_out_odd,
                w_mlp_up=v_w_mlp_up, w_mlp_down=v_w_mlp_down)
    grads, deltas, new_ms, new_vs = [], [], [], []
    for n in names:
        w = w_of[n]
        if n in early:
            g, d, mn, vn = early[n]
            grads.append(g)
            deltas.append(d)
            new_ms.append(mn)
            new_vs.append(vn)
            continue
        if n == "w_in_even":
            to_view = lambda a: a[0].T
            from_view = lambda a: a.T[None]
        else:
            view = w.shape if w.ndim <= 3 else w.shape[-3:]
            to_view = lambda a, view=view: a.reshape(view)
            from_view = lambda a, w=w: a.reshape(w.shape)
        g = g_of[n] if n == "w_in_even" else to_view(g_of[n])
        d, mn, vn = _adamw(to_view(w), g, to_view(m_of[n]), to_view(v_of[n]), "adamw_" + n)
        grads.append(from_view(g))
        deltas.append(from_view(d))
        new_ms.append(from_view(mn))
        new_vs.append(from_view(vn))

    return (loss, grad_x.reshape(1, T, D), *grads, *deltas, *new_ms, *new_vs)
```

```python
import functools

import jax
import jax.numpy as jnp
from jax import lax
from jax.experimental import pallas as pl
from jax.experimental.pallas import tpu as pltpu

F32 = jnp.float32
BF16 = jnp.bfloat16
MESH = pl.DeviceIdType.MESH

T = 2048
D = 1024
DFF = 4096
EPS = 1e-6
CHUNK = 64
NCHUNK = T // CHUNK
PE = 3200
PO = 2560
AUX_BLK = 3072 // 128
FOX_LANE0 = 16
GLA_SCALE = 64 ** -0.5
ATT_SCALE = 64 ** -0.5
NEG = float(jnp.finfo(jnp.float32).min)
CA_BAND = 576
CA_PAD = 512
REL_PAD = 384

VMEM_LIMIT = 48 * 1024 * 1024

ADAM_LR, ADAM_B1, ADAM_B2, ADAM_EPS, ADAM_WD, ADAM_STEP = 0.001, 0.9, 0.999, 1e-08, 0.01, 10

GA_ROWS, GA_IN_O, GA_OUT_O, GA_GAP, GA_UP, GA_DN = 3072, 0, 640, 896, 1024, 2048
GB_ROWS, GB_UP, GB_DN = 2048, 0, 1024
GC_ROWS, GC_OUT_E, GC_TAIL = 1152, 0, 256

_DIMS = {"nn": (((1,), (0,)), ((), ())), "nt": (((1,), (1,)), ((), ())), "tn": (((0,), (0,)), ((), ()))}


def _cp(sem, **kw):
    return pltpu.CompilerParams(dimension_semantics=sem, vmem_limit_bytes=VMEM_LIMIT, **kw)


def _dot(a, b, mode):
    return lax.dot_general(a.astype(BF16), b.astype(BF16), _DIMS[mode], preferred_element_type=F32)


@functools.partial(jax.custom_vjp, nondiff_argnums=(2,))
def bdot(a, b, mode):
    return _dot(a, b, mode)


def _bdot_fwd(a, b, mode):
    return _dot(a, b, mode), (a, b)


def _bdot_bwd(mode, res, g):
    a, b = res
    if mode == "nn":
        da, db = _dot(g, b, "nt"), _dot(a, g, "tn")
    elif mode == "nt":
        da, db = _dot(g, b, "nn"), _dot(g, a, "tn")
    else:
        da, db = _dot(b, g, "nt"), _dot(a, g, "nn")
    return da.astype(a.dtype), db.astype(b.dtype)


bdot.defvjp(_bdot_fwd, _bdot_bwd)


def _hdot_raw(a, b, mode):
    return lax.dot_general(a, b, _DIMS[mode], precision=lax.Precision.HIGHEST, preferred_element_type=F32)


def _log_sigmoid(x):
    return jnp.minimum(x, 0.0) - jnp.log(1.0 + jnp.exp(-jnp.abs(x)))


def _sigmoid(x):
    return 1.0 / (1.0 + jnp.exp(-x))


def _expm1(x):
    series = x * (1.0 + x * 0.5 * (1.0 + x * (1.0 / 3.0) * (1.0 + x * 0.25)))
    return jnp.where(jnp.abs(x) < 0.03, series, jnp.exp(x) - 1.0)


def _gelu_tanh(x):
    return 0.5 * x * (1.0 + jnp.tanh(0.7978845608028654 * (x + 0.044715 * x * x * x)))


def _iota(shape, dim):
    return lax.broadcasted_iota(jnp.int32, shape, dim)


def _mm(a, b, mode, *, tm, tn, tk=None, out_dtype=F32, name, b_layer=None, into=None, relu_pair=False, times2=None):
    b2 = b.shape[-2:]
    if mode == "nn":
        (m, k), n = a.shape, b2[1]
    elif mode == "nt":
        (m, k), n = a.shape, b2[0]
    else:
        (k, m), n = a.shape, b2[1]
    tk = k if tk is None else tk
    assert m % tm == 0 and n % tn == 0 and k % tk == 0, (name, a.shape, b.shape)
    nk = k // tk
    if mode == "tn":
        a_spec = pl.BlockSpec((tk, tm), lambda i, j, kk: (kk, i))
    elif m == tm and nk == 1:
        a_spec = pl.BlockSpec((tm, tk), lambda i, j, kk: (i, kk), pipeline_mode=pl.Buffered(1))
    else:
        a_spec = pl.BlockSpec((tm, tk), lambda i, j, kk: (i, kk))
    b_blk = {"nn": (tk, tn), "nt": (tn, tk), "tn": (tk, tn)}[mode]
    b_idx = {"nn": lambda i, j, kk: (kk, j), "nt": lambda i, j, kk: (j, kk), "tn": lambda i, j, kk: (kk, j)}[mode]
    if b_layer is None:
        b_spec = pl.BlockSpec(b_blk, b_idx)
    else:
        b_spec = pl.BlockSpec((None,) + b_blk, lambda i, j, kk: (b_layer,) + b_idx(i, j, kk))

    tile = pl.BlockSpec((tm, tn), lambda i, j, kk: (i, j))
    if into is not None:
        buf, per_slot, row_off = into
        assert m == 4 * per_slot and per_slot % tm == 0 and row_off % tm == 0 and buf.shape[2] == n, (name, buf.shape)
        bps = per_slot // tm
        out_specs = pl.BlockSpec((None, tm, tn), lambda i, j, kk: (i // bps, row_off // tm + i % bps, j))
        out_shape = jax.ShapeDtypeStruct(buf.shape, buf.dtype)
        extra_in, extra_specs, aliases = [buf], [pl.BlockSpec(memory_space=pl.ANY)], {2: 0}
        finish = lambda acc, extra: [acc.astype(buf.dtype)]
    elif relu_pair:
        out_specs = (tile, tile)
        out_shape = (jax.ShapeDtypeStruct((m, n), BF16),) * 2
        extra_in, extra_specs, aliases = [], [], {}

        def finish(acc, extra):
            r = jnp.maximum(acc, 0.0)
            return [(r * r).astype(BF16), r.astype(BF16)]
    elif times2 is not None:
        out_specs = tile
        out_shape = jax.ShapeDtypeStruct((m, n), out_dtype)
        extra_in, extra_specs, aliases = [times2], [tile], {}
        finish = lambda acc, extra: [(acc * (2.0 * extra[...].astype(F32))).astype(out_dtype)]
    else:
        out_specs = tile
        out_shape = jax.ShapeDtypeStruct((m, n), out_dtype)
        extra_in, extra_specs, aliases = [], [], {}
        finish = lambda acc, extra: [acc.astype(out_dtype)]
    n_out = 2 if relu_pair else 1

    def body(*refs):
        a_ref, b_ref = refs[0], refs[1]
        extra = refs[2] if extra_in else None
        o_refs = refs[2 + len(extra_in):2 + len(extra_in) + n_out]

        def store(acc):
            for o_ref, val in zip(o_refs, finish(acc, extra)):
                o_ref[...] = val

        if nk == 1:
            store(_dot(a_ref[...], b_ref[...], mode))
            return
        acc_ref = refs[-1]
        kk = pl.program_id(2)

        @pl.when(kk == 0)
        def _():
            acc_ref[...] = jnp.zeros_like(acc_ref)

        acc_ref[...] += _dot(a_ref[...], b_ref[...], mode)

        @pl.when(kk == nk - 1)
        def _():
            store(acc_ref[...])

    return pl.pallas_call(
        body, name=name, grid=(m // tm, n // tn, nk),
        in_specs=[a_spec, b_spec] + extra_specs,
        out_specs=out_specs, out_shape=out_shape,
        scratch_shapes=[pltpu.VMEM((tm, tn), F32)] if nk > 1 else [],
        input_output_aliases=aliases,
        compiler_params=_cp(("parallel", "parallel", "arbitrary")),
    )(a, b, *extra_in)


ROWS = 512


def _prenorm(x, w, name):
    def body(x_ref, w_ref, o_ref):
        xv = x_ref[...]
        r = lax.rsqrt(jnp.mean(xv * xv, axis=-1, keepdims=True) + EPS)
        o_ref[...] = (xv * r * w_ref[...]).astype(BF16)

    return pl.pallas_call(
        body, name=name, grid=(T // ROWS,),
        in_specs=[pl.BlockSpec((ROWS, D), lambda i: (i, 0)), pl.BlockSpec((1, D), lambda i: (0, 0))],
        out_specs=pl.BlockSpec((ROWS, D), lambda i: (i, 0)),
        out_shape=jax.ShapeDtypeStruct((T, D), BF16),
        compiler_params=_cp(("parallel",)),
    )(x, w)


def _rms(z):
    return lax.rsqrt(jnp.mean(z * z, axis=-1, keepdims=True) + EPS)


def _rms_bwd(z, w, dy):
    r = _rms(z)
    wdy = dy * w
    dz = r * wdy - z * (r * r * r) * jnp.mean(z * wdy, axis=-1, keepdims=True)
    return dz, jnp.sum(dy * z * r, axis=0, keepdims=True)


_ROW = pl.BlockSpec((ROWS, D), lambda i: (i, 0))
_VEC = pl.BlockSpec((1, D), lambda i: (0, 0))


def _post_pre_fwd(x, z, w_post, w_pre, name):
    def body(x_ref, z_ref, wp_ref, wn_ref, x_out, h_out):
        zv = z_ref[...]
        xn = x_ref[...] + zv * _rms(zv) * wp_ref[...]
        x_out[...] = xn
        h_out[...] = (xn * _rms(xn) * wn_ref[...]).astype(BF16)

    return pl.pallas_call(
        body, name=name, grid=(T // ROWS,), in_specs=[_ROW, _ROW, _VEC, _VEC], out_specs=(_ROW, _ROW),
        out_shape=(jax.ShapeDtypeStruct((T, D), F32), jax.ShapeDtypeStruct((T, D), BF16)),
        compiler_params=_cp(("parallel",)),
    )(x, z, w_post, w_pre)


def _post_loss(x, z, w_post, tgt):
    def body(x_ref, z_ref, w_ref, t_ref, g_ref, l_ref, dz_ref, dw_ref):
        @pl.when(pl.program_id(0) == 0)
        def _():
            l_ref[...] = jnp.zeros_like(l_ref)
            dw_ref[...] = jnp.zeros_like(dw_ref)

        zv = z_ref[...]
        e = x_ref[...] + zv * _rms(zv) * w_ref[...] - t_ref[...]
        g = e * (1.0 / D)
        g_ref[...] = g
        l_ref[...] += jnp.sum(e * e) * (0.5 / D)
        dz, dw = _rms_bwd(zv, w_ref[...], g)
        dz_ref[...] = dz.astype(BF16)
        dw_ref[...] += dw

    return pl.pallas_call(
        body, name="postnorm_loss", grid=(T // ROWS,), in_specs=[_ROW, _ROW, _VEC, _ROW],
        out_specs=(_ROW, pl.BlockSpec((1, 128), lambda i: (0, 0)), _ROW, _VEC),
        out_shape=(jax.ShapeDtypeStruct((T, D), F32), jax.ShapeDtypeStruct((1, 128), F32),
                   jax.ShapeDtypeStruct((T, D), BF16), jax.ShapeDtypeStruct((1, D), F32)),
        compiler_params=_cp(("arbitrary",)),
    )(x, z, w_post, tgt)


def _pre_post_bwd(x, w_pre, dh, add, z, w_post, name):
    def body(x_ref, wn_ref, dh_ref, add_ref, z_ref, wp_ref, g_ref, dz_ref, dwn_ref, dwp_ref):
        @pl.when(pl.program_id(0) == 0)
        def _():
            dwn_ref[...] = jnp.zeros_like(dwn_ref)
            dwp_ref[...] = jnp.zeros_like(dwp_ref)

        dx, dwn = _rms_bwd(x_ref[...], wn_ref[...], dh_ref[...])
        g = dx + add_ref[...]
        g_ref[...] = g
        dz, dwp = _rms_bwd(z_ref[...], wp_ref[...], g)
        dz_ref[...] = dz.astype(BF16)
        dwn_ref[...] += dwn
        dwp_ref[...] += dwp

    return pl.pallas_call(
        body, name=name, grid=(T // ROWS,), in_specs=[_ROW, _VEC, _ROW, _ROW, _ROW, _VEC],
        out_specs=(_ROW, _ROW, _VEC, _VEC),
        out_shape=(jax.ShapeDtypeStruct((T, D), F32), jax.ShapeDtypeStruct((T, D), BF16),
                   jax.ShapeDtypeStruct((1, D), F32), jax.ShapeDtypeStruct((1, D), F32)),
        compiler_params=_cp(("arbitrary",)),
    )(x, w_pre, dh, add, z, w_post)


def _norm_bwd(z, w, dy, add, name):
    has_add = add is not None

    def body(*refs):
        if has_add:
            z_ref, w_ref, dy_ref, add_ref, dz_ref, dw_ref = refs
        else:
            z_ref, w_ref, dy_ref, dz_ref, dw_ref = refs
        i = pl.program_id(0)

        @pl.when(i == 0)
        def _():
            dw_ref[...] = jnp.zeros_like(dw_ref)

        zv = z_ref[...].astype(F32)
        dyv = dy_ref[...]
        r = lax.rsqrt(jnp.mean(zv * zv, axis=-1, keepdims=True) + EPS)
        wdy = dyv * w_ref[...]
        dz = r * wdy - zv * (r * r * r) * jnp.mean(zv * wdy, axis=-1, keepdims=True)
        if has_add:
            dz = dz + add_ref[...]
        dz_ref[...] = dz.astype(dz_ref.dtype)
        dw_ref[...] += jnp.sum(dyv * zv * r, axis=0, keepdims=True)

    row = pl.BlockSpec((ROWS, D), lambda i: (i, 0))
    vec = pl.BlockSpec((1, D), lambda i: (0, 0))
    ins = [z, w, dy] + ([add] if has_add else [])
    dz_dtype = F32 if has_add else BF16
    return pl.pallas_call(
        body, name=name, grid=(T // ROWS,),
        in_specs=[row, vec, row] + ([row] if has_add else []),
        out_specs=(row, vec),
        out_shape=(jax.ShapeDtypeStruct((T, D), dz_dtype), jax.ShapeDtypeStruct((1, D), F32)),
        compiler_params=_cp(("arbitrary",)),
    )(*ins)


def _adamw_math(w, g, m, v):
    c1 = 1.0 - ADAM_B1 ** ADAM_STEP
    c2 = 1.0 - ADAM_B2 ** ADAM_STEP
    mn = ADAM_B1 * m + (1.0 - ADAM_B1) * g
    vn = ADAM_B2 * v + (1.0 - ADAM_B2) * (g * g)
    return -ADAM_LR * ((mn / c1) / (jnp.sqrt(vn / c2) + ADAM_EPS) + ADAM_WD * w), mn, vn


def _adamw_from(w, m, v, sources, tr, name):
    layers, rows, cols = w.shape
    assert len(sources) == layers and rows % tr == 0, (name, w.shape)
    g_specs = []
    for layer, (buf, row0, transposed) in enumerate(sources):
        step = lambda l, i, layer=layer: jnp.where(l == layer, i, 0)
        if transposed:
            assert row0 % cols == 0 and buf.shape[1] == rows, (name, row0)
            g_specs.append(pl.BlockSpec((cols, tr), lambda l, i, b=row0 // cols, step=step: (b, step(l, i))))
        else:
            assert row0 % tr == 0 and buf.shape[1] == cols, (name, row0)
            g_specs.append(pl.BlockSpec((tr, cols), lambda l, i, b=row0 // tr, step=step: (b + step(l, i), 0)))

    def body(*refs):
        w_ref, m_ref, v_ref = refs[:3]
        g_refs = refs[3:3 + layers]
        g_out, d_ref, mo_ref, vo_ref = refs[3 + layers:]
        gs = [r[...].T if src[2] else r[...] for r, src in zip(g_refs, sources)]
        g = gs[0] if layers == 1 else jnp.where(pl.program_id(0) == 0, gs[0], gs[1])
        g_out[...] = g
        d_ref[...], mo_ref[...], vo_ref[...] = _adamw_math(w_ref[...], g, m_ref[...], v_ref[...])

    blk = pl.BlockSpec((None, tr, cols), lambda l, i: (l, i, 0))
    sds = jax.ShapeDtypeStruct(w.shape, F32)
    return pl.pallas_call(body, name=name, grid=(layers, rows // tr), in_specs=[blk] * 3 + g_specs,
                          out_specs=(blk,) * 4, out_shape=(sds,) * 4,
                          compiler_params=_cp(("parallel", "parallel")))(w, m, v, *[s[0] for s in sources])


def _adamw(w, g, m, v, name):
    lead = w.shape[:-2]
    assert len(lead) <= 1 and g.shape == w.shape, (name, w.shape, g.shape)
    rows, cols = w.shape[-2:]
    if rows <= 512:
        tr, tc = rows, cols
    elif rows % 256 == 0:
        tr, tc = 256, cols
    else:
        tr, tc = rows, 256
    assert rows % tr == 0 and cols % tc == 0, (name, w.shape)
    c1 = 1.0 - ADAM_B1 ** ADAM_STEP
    c2 = 1.0 - ADAM_B2 ** ADAM_STEP

    def body(w_ref, g_ref, m_ref, v_ref, d_ref, mo_ref, vo_ref):
        gv = g_ref[...]
        mn = ADAM_B1 * m_ref[...] + (1.0 - ADAM_B1) * gv
        vn = ADAM_B2 * v_ref[...] + (1.0 - ADAM_B2) * (gv * gv)
        m_hat = mn / c1
        v_hat = vn / c2
        d_ref[...] = -ADAM_LR * (m_hat / (jnp.sqrt(v_hat) + ADAM_EPS) + ADAM_WD * w_ref[...])
        mo_ref[...] = mn
        vo_ref[...] = vn

    if lead:
        grid = (lead[0], rows // tr, cols // tc)
        blk = pl.BlockSpec((None, tr, tc), lambda l, i, j: (l, i, j))
    else:
        grid = (rows // tr, cols // tc)
        blk = pl.BlockSpec((tr, tc), lambda i, j: (i, j))
    sds = jax.ShapeDtypeStruct(w.shape, F32)
    return pl.pallas_call(body, name=name, grid=grid, in_specs=[blk] * 4, out_specs=(blk,) * 3,
                          out_shape=(sds,) * 3, compiler_params=_cp(("parallel",) * len(grid)))(w, g, m, v)


def _running_sum(x, towards_later):
    n = x.shape[0]
    row = _iota(x.shape, 0)
    s = 1
    while s < n:
        if towards_later:
            x = x + jnp.where(row >= s, pltpu.roll(x, s, 0), 0.0)
        else:
            x = x + jnp.where(row < n - s, pltpu.roll(x, n - s, 0), 0.0)
        s *= 2
    return x


@jax.custom_vjp
def _cumsum_rows(x):
    return _running_sum(x, True)


_cumsum_rows.defvjp(lambda x: (_running_sum(x, True), None), lambda _, g: (_running_sum(g, False),))


def _gla_consts():
    return (_iota((256, 512), 0) // 64 == _iota((256, 512), 1) // 128).astype(F32)


def _gla_chunk(mask, q, k, v, r, aux, s_prev, wa, ba, nw):
    la = _log_sigmoid(bdot(aux, wa, "nn") + ba) * (1.0 / 16.0)
    cum = _cumsum_rows(la)
    total = jnp.sum(la, axis=0, keepdims=True)
    k_dec = k * jnp.exp(total - cum)
    inc = bdot(k_dec, v, "tn") * mask
    dec = jnp.exp(jnp.broadcast_to(total, (128, 256)).T)
    dec = jnp.concatenate([dec, dec, dec, dec], axis=1)
    s_new = dec * s_prev + inc
    o = bdot(q * GLA_SCALE, s_new, "nn")
    parts = []
    for h in range(4):
        oh = o[:, h * 128:(h + 1) * 128]
        parts.append(oh * lax.rsqrt(jnp.mean(oh * oh, axis=-1, keepdims=True) + EPS))
    on = jnp.concatenate(parts, axis=1)
    return s_new, on * nw * (r * _sigmoid(r))


GLA_PER_STEP = 4
GLA_ROWS = GLA_PER_STEP * CHUNK
GLA_STEPS = NCHUNK // GLA_PER_STEP


def _gla_specs(cmap):
    return [pl.BlockSpec((GLA_ROWS, 256), lambda c: (cmap(c), 0)),
            pl.BlockSpec((GLA_ROWS, 256), lambda c: (cmap(c), 1)),
            pl.BlockSpec((GLA_ROWS, 512), lambda c: (cmap(c), 1)),
            pl.BlockSpec((GLA_ROWS, 512), lambda c: (cmap(c), 2)),
            pl.BlockSpec((GLA_ROWS, 128), lambda c: (cmap(c), AUX_BLK))]


def _gla_fwd(proj, wa, ba, nw):
    def body(q_ref, k_ref, v_ref, r_ref, aux_ref, wa_ref, ba_ref, nw_ref, o_ref, sp_ref, s_ref):
        @pl.when(pl.program_id(0) == 0)
        def _():
            s_ref[...] = jnp.zeros_like(s_ref)

        s = s_ref[...]
        consts = _gla_consts()
        outs, states = [], []
        for i in range(GLA_PER_STEP):
            rows = slice(i * CHUNK, (i + 1) * CHUNK)
            states.append(s)
            s, out = _gla_chunk(consts, q_ref[rows, :], k_ref[rows, :], v_ref[rows, :], r_ref[rows, :], aux_ref[rows, :],
                                s, wa_ref[...], ba_ref[...], nw_ref[...])
            outs.append(out)
        s_ref[...] = s
        for i in range(GLA_PER_STEP):
            o_ref[i * CHUNK:(i + 1) * CHUNK, :] = outs[i]
            sp_ref[i] = states[i]

    full = lambda shape: pl.BlockSpec(shape, lambda c: (0,) * len(shape))
    return pl.pallas_call(
        body, name="gla_fwd", grid=(GLA_STEPS,),
        in_specs=_gla_specs(lambda c: c) + [full((128, 256)), full((1, 256)), full((1, 512))],
        out_specs=(pl.BlockSpec((GLA_ROWS, 512), lambda c: (c, 0)),
                   pl.BlockSpec((GLA_PER_STEP, 256, 512), lambda c: (c, 0, 0))),
        out_shape=(jax.ShapeDtypeStruct((T, D), F32), jax.ShapeDtypeStruct((NCHUNK, 256, 512), F32)),
        scratch_shapes=[pltpu.VMEM((256, 512), F32)],
        compiler_params=_cp(("arbitrary",)),
    )(proj, proj, proj, proj, proj, wa, ba, nw)


def _gla_bwd(proj, s_prev_all, wa, ba, nw, dcat):
    rev = lambda c: GLA_STEPS - 1 - c

    def body(q_ref, k_ref, v_ref, r_ref, aux_ref, sp_ref, wa_ref, ba_ref, nw_ref, do_ref,
             dq_ref, dk_ref, dv_ref, dr_ref, daux_ref, dwa_ref, dba_ref, dnw_ref, ds_ref):
        @pl.when(pl.program_id(0) == 0)
        def _():
            ds_ref[...] = jnp.zeros_like(ds_ref)
            dwa_ref[...] = jnp.zeros_like(dwa_ref)
            dba_ref[...] = jnp.zeros_like(dba_ref)
            dnw_ref[...] = jnp.zeros_like(dnw_ref)

        fn = functools.partial(_gla_chunk, _gla_consts())
        ds = ds_ref[...]
        dwa, dba, dnw = dwa_ref[...], dba_ref[...], dnw_ref[...]
        grads = {}
        for i in reversed(range(GLA_PER_STEP)):
            rows = slice(i * CHUNK, (i + 1) * CHUNK)
            _, vjp = jax.vjp(fn, q_ref[rows, :], k_ref[rows, :], v_ref[rows, :], r_ref[rows, :], aux_ref[rows, :],
                             sp_ref[i], wa_ref[...], ba_ref[...], nw_ref[...])
            *grads[i], ds, dwa_i, dba_i, dnw_i = vjp((ds, do_ref[rows, :]))
            dwa, dba, dnw = dwa + dwa_i, dba + dba_i, dnw + dnw_i
        ds_ref[...] = ds
        dwa_ref[...] = dwa
        dba_ref[...] = dba
        dnw_ref[...] = dnw
        for i in range(GLA_PER_STEP):
            rows = slice(i * CHUNK, (i + 1) * CHUNK)
            for ref, g in zip((dq_ref, dk_ref, dv_ref, dr_ref, daux_ref), grads[i]):
                ref[rows, :] = g

    full = lambda shape: pl.BlockSpec(shape, lambda c: (0,) * len(shape))
    blk = lambda w: pl.BlockSpec((GLA_ROWS, w), lambda c: (rev(c), 0))
    sds = lambda *s: jax.ShapeDtypeStruct(s, F32)
    return pl.pallas_call(
        body, name="gla_bwd", grid=(GLA_STEPS,),
        in_specs=_gla_specs(rev) + [pl.BlockSpec((GLA_PER_STEP, 256, 512), lambda c: (rev(c), 0, 0)),
                                    full((128, 256)), full((1, 256)), full((1, 512)), blk(512)],
        out_specs=(blk(256), blk(256), blk(512), blk(512), blk(128), full((128, 256)), full((1, 256)), full((1, 512))),
        out_shape=(sds(T, 256), sds(T, 256), sds(T, 512), sds(T, 512), sds(T, 128),
                   sds(128, 256), sds(1, 256), sds(1, 512)),
        scratch_shapes=[pltpu.VMEM((256, 512), F32)],
        compiler_params=_cp(("arbitrary",)),
    )(proj, proj, proj, proj, proj, s_prev_all, wa, ba, nw, dcat)


def _prefix8(x, towards_later):
    row = _iota(x.shape, 0)
    for s in (1, 2, 4):
        if towards_later:
            keep, shift = row >= s, s
        else:
            keep, shift = row < 8 - s, 8 - s
        x = x + jnp.where(keep, pltpu.roll(x, shift, 0), 0.0)
    return x


def _fox_gate_fwd(proj, bpad):
    def body(aux_ref, b_ref, cum_ref):
        cum_ref[...] = _log_sigmoid(aux_ref[...] + b_ref[...])

        def step(i, carry):
            rows = pl.ds(pl.multiple_of(i * 8, 8), 8)
            cum = _prefix8(cum_ref[rows, :], True) + carry
            cum_ref[rows, :] = cum
            return jnp.broadcast_to(cum[7:, :], (8, 128))

        lax.fori_loop(0, T // 8, step, jnp.zeros((8, 128), F32), unroll=4)

    return pl.pallas_call(
        body, name="fox_gate_fwd", grid=(1,),
        in_specs=[pl.BlockSpec((T, 128), lambda i: (0, AUX_BLK)), pl.BlockSpec((1, 128), lambda i: (0, 0))],
        out_specs=pl.BlockSpec((T, 128), lambda i: (0, 0)),
        out_shape=jax.ShapeDtypeStruct((T, 128), F32),
        compiler_params=_cp(("arbitrary",)),
    )(proj, bpad)


def _fox_gate_bwd(proj, bpad, dccol_t, daux_gla):
    def body(aux_ref, b_ref, dc_ref, dg_ref, daux_ref, db_ref):
        def step(i, carry):
            rows = pl.ds(pl.multiple_of(T - 8 * (i + 1), 8), 8)
            dlf = _prefix8(dc_ref[rows, :], False) + carry
            daux_ref[rows, :] = dlf
            return jnp.broadcast_to(dlf[:1, :], (8, 128))

        lax.fori_loop(0, T // 8, step, jnp.zeros((8, 128), F32), unroll=4)
        dz = daux_ref[...] * _sigmoid(-(aux_ref[...] + b_ref[...]))
        daux_ref[...] = dz + dg_ref[...]
        db_ref[...] = jnp.sum(dz, axis=0, keepdims=True)

    whole = pl.BlockSpec((T, 128), lambda i: (0, 0))
    vec = pl.BlockSpec((1, 128), lambda i: (0, 0))
    return pl.pallas_call(
        body, name="fox_gate_bwd", grid=(1,),
        in_specs=[pl.BlockSpec((T, 128), lambda i: (0, AUX_BLK)), vec, whole, whole],
        out_specs=(whole, vec),
        out_shape=(jax.ShapeDtypeStruct((T, 128), F32), jax.ShapeDtypeStruct((1, 128), F32)),
        compiler_params=_cp(("arbitrary",)),
    )(proj, bpad, dccol_t, daux_gla)


FOX_Q = 256
FOX_QB = T // FOX_Q
FOX_QF = 512


@jax.custom_vjp
def _attend(s, v):
    return _attend_fwd(s, v)[0]


def _attend_fwd(s, v):
    e = jnp.exp(s - jnp.max(s, axis=-1, keepdims=True))
    r = 1.0 / jnp.sum(e, axis=-1, keepdims=True)
    return _dot(e, v, "nn") * r, (e, r, v)


def _attend_bwd(res, do):
    e, r, v = res
    do_r = do * r
    dpr = _dot(do_r, v, "nt")
    ds = e * (dpr - r * jnp.sum(e * dpr, axis=-1, keepdims=True))
    return ds, _dot(e, do_r, "tn").astype(v.dtype)


_attend.defvjp(_attend_fwd, _attend_bwd)


def _fox_block(hp, q, k, v, ccol):
    fq, kl = q.shape[0], k.shape[0]
    lane = _iota((fq, 128), 1)
    tri = jnp.bitwise_and(_iota((2 * fq, fq), 0), fq - 1) >= _iota((2 * fq, fq), 1)
    sub = _iota((8, kl), 0)
    qs = q * ATT_SCALE
    q2 = jnp.concatenate([jnp.where(lane < 64, qs, 0.0), jnp.where(lane >= 64, qs, 0.0)], axis=0)
    s = bdot(q2, k, "nt")
    cs = [jnp.sum(jnp.where(sub == 2 * hp + e, ccol, 0.0), axis=0, keepdims=True) for e in range(2)]
    s = jnp.concatenate([s[:fq] - cs[0], s[fq:] - cs[1]], axis=0)
    diag = jnp.where(tri, s[:, kl - fq:], NEG)
    s = diag if kl == fq else jnp.concatenate([s[:, :kl - fq], diag], axis=1)
    o2 = _attend(s, v)
    return jnp.where(lane < 64, o2[:fq], o2[fq:])


def _fox_in_specs(fq):
    return [pl.BlockSpec((fq, 128), lambda hp, qb: (qb, 12 + hp)),
            pl.BlockSpec((T, 128), lambda hp, qb: (0, 16 + hp)),
            pl.BlockSpec((T, 128), lambda hp, qb: (0, 20 + hp)),
            pl.BlockSpec((8, T), lambda hp, qb: (0, 0))]


def _fox_fwd(proj, cum_c, cat):
    def body(q_ref, k_ref, v_ref, cc_ref, cat_ref, o_ref):
        qb = pl.program_id(1)
        for g in range(T // FOX_QF):
            kl = FOX_QF * (g + 1)

            @pl.when(qb == g)
            def _(kl=kl):
                o_ref[...] = _fox_block(pl.program_id(0), q_ref[...], k_ref[0:kl, :], v_ref[0:kl, :], cc_ref[:, 0:kl])

    return pl.pallas_call(
        body, name="fox_fwd", grid=(4, T // FOX_QF),
        in_specs=_fox_in_specs(FOX_QF) + [pl.BlockSpec(memory_space=pl.ANY)],
        out_specs=pl.BlockSpec((FOX_QF, 128), lambda hp, qb: (qb, 4 + hp)),
        out_shape=jax.ShapeDtypeStruct((T, D), F32), input_output_aliases={4: 0},
        compiler_params=_cp(("parallel", "parallel")),
    )(proj, proj, proj, cum_c, cat)


def _fox_bwd(proj, cum_c, dcat):
    def body(q_ref, k_ref, v_ref, cc_ref, do_ref, dq_ref, dk_ref, dv_ref, dcc_ref):
        qb = pl.program_id(1)

        @pl.when(qb == 0)
        def _():
            dk_ref[...] = jnp.zeros_like(dk_ref)
            dv_ref[...] = jnp.zeros_like(dv_ref)
            dcc_ref[...] = jnp.zeros_like(dcc_ref)

        fn = functools.partial(_fox_block, pl.program_id(0))
        for g in range(FOX_QB):
            kl = FOX_Q * (g + 1)

            @pl.when(qb == g)
            def _(kl=kl):
                _, vjp = jax.vjp(fn, q_ref[...], k_ref[0:kl, :], v_ref[0:kl, :], cc_ref[:, 0:kl])
                dq, dk, dv, dcc = vjp(do_ref[...])
                dq_ref[...] = dq
                dk_ref[0:kl, :] += dk
                dv_ref[0:kl, :] += dv
                dcc_ref[:, 0:kl] += dcc

    sds = lambda *s: jax.ShapeDtypeStruct(s, F32)
    return pl.pallas_call(
        body, name="fox_bwd", grid=(4, FOX_QB),
        in_specs=_fox_in_specs(FOX_Q) + [pl.BlockSpec((FOX_Q, 128), lambda hp, qb: (qb, 4 + hp))],
        out_specs=(pl.BlockSpec((FOX_Q, 128), lambda hp, qb: (qb, hp)),
                   pl.BlockSpec((T, 128), lambda hp, qb: (0, hp)),
                   pl.BlockSpec((T, 128), lambda hp, qb: (0, hp)),
                   pl.BlockSpec((None, 8, T), lambda hp, qb: (hp, 0, 0))),
        out_shape=(sds(T, 512), sds(T, 512), sds(T, 512), sds(4, 8, T)),
        compiler_params=_cp(("parallel", "arbitrary")),
    )(proj, proj, proj, cum_c, dcat)


BIAS_W = 640


def _rel_onehot():
    j = _iota((REL_PAD, BIAS_W), 1)
    rel = jnp.clip(CA_PAD + CHUNK - 1 - j, -128, 128) + 128
    return (_iota((REL_PAD, BIAS_W), 0) == rel).astype(F32)


def _bias_build(rbp):
    def body(rb_ref, o_ref):
        f = _hdot_raw(rb_ref[...], _rel_onehot(), "nn")
        for q in range(CHUNK):
            o_ref[q] = pltpu.roll(f, (BIAS_W - (CHUNK - 1 - q)) % BIAS_W, 1)[:, :CA_BAND]

    return pl.pallas_call(body, name="ca_bias_build", out_shape=jax.ShapeDtypeStruct((CHUNK, 8, CA_BAND), F32))(rbp)


def _bias_grad(dbias_q):
    def body(db_ref, o_ref):
        acc = jnp.zeros((8, BIAS_W), F32)
        for q in range(CHUNK):
            acc = acc + pltpu.roll(db_ref[q], CHUNK - 1 - q, 1)
        o_ref[...] = _hdot_raw(acc, _rel_onehot(), "nt")

    return pl.pallas_call(body, name="ca_bias_grad", out_shape=jax.ShapeDtypeStruct((8, REL_PAD), F32))(dbias_q)


def _ca_block(c, masked, q, kb, vb, bias2):
    lane = _iota((CHUNK, 128), 1)
    qs = q * ATT_SCALE
    q2 = jnp.concatenate([jnp.where(lane < 64, qs, 0.0), jnp.where(lane >= 64, qs, 0.0)], axis=0)
    s = bdot(q2, kb, "nt") + bias2.reshape(2 * CHUNK, CA_BAND)
    if masked:
        s = jnp.where((c * CHUNK - CA_PAD + _iota((2 * CHUNK, CA_BAND), 1)) >= 0, s, NEG)
    o2 = _attend(s, vb)
    return jnp.where(lane < 64, o2[:CHUNK], o2[CHUNK:])


CA_PER_STEP = 16
CA_ROWS = CA_PER_STEP * CHUNK
CA_MASKED_CHUNKS = CA_PAD // CHUNK
assert CA_PER_STEP >= CA_MASKED_CHUNKS
CA_MASKED_STEPS = 1


def _ca_fwd(proj, kvpad, bias):
    def body(q_ref, k_ref, v_ref, b_ref, o_ref):
        def run(masked):
            outs = []
            for i in range(CA_PER_STEP):
                c = pl.program_id(1) * CA_PER_STEP + i
                band = pl.ds(pl.multiple_of(c * CHUNK, CHUNK), CA_BAND)
                rows = slice(i * CHUNK, (i + 1) * CHUNK)
                outs.append(_ca_block(c, masked and i < CA_MASKED_CHUNKS, q_ref[rows, :], k_ref[band, :], v_ref[band, :],
                                      b_ref[...]))
            for i in range(CA_PER_STEP):
                o_ref[i * CHUNK:(i + 1) * CHUNK, :] = outs[i]

        pl.when(pl.program_id(1) < CA_MASKED_STEPS)(lambda: run(True))
        pl.when(pl.program_id(1) >= CA_MASKED_STEPS)(lambda: run(False))

    return pl.pallas_call(
        body, name="ca_fwd", grid=(4, NCHUNK // CA_PER_STEP),
        in_specs=[pl.BlockSpec((CA_ROWS, 128), lambda hp, c: (c, hp)),
                  pl.BlockSpec((T + CA_PAD, 128), lambda hp, c: (0, hp)),
                  pl.BlockSpec((T + CA_PAD, 128), lambda hp, c: (0, 4 + hp)),
                  pl.BlockSpec((2, CHUNK, CA_BAND), lambda hp, c: (hp, 0, 0))],
        out_specs=pl.BlockSpec((CA_ROWS, 128), lambda hp, c: (c, hp)),
        out_shape=jax.ShapeDtypeStruct((T, D), F32),
        compiler_params=_cp(("parallel", "parallel")),
    )(proj, kvpad, kvpad, bias)


def _ca_bwd(proj, kvpad, bias, dcat):
    def body(q_ref, k_ref, v_ref, b_ref, do_ref, dq_ref, dk_ref, dv_ref, db_ref):
        c = pl.program_id(1)

        @pl.when(c == 0)
        def _():
            dk_ref[...] = jnp.zeros_like(dk_ref)
            dv_ref[...] = jnp.zeros_like(dv_ref)
            db_ref[...] = jnp.zeros_like(db_ref)

        def run(masked):
            grads, bands = [], []
            for i in range(CA_PER_STEP):
                ci = c * CA_PER_STEP + i
                band = pl.ds(pl.multiple_of(ci * CHUNK, CHUNK), CA_BAND)
                rows = slice(i * CHUNK, (i + 1) * CHUNK)
                fn = functools.partial(_ca_block, ci, masked and i < CA_MASKED_CHUNKS)
                _, vjp = jax.vjp(fn, q_ref[rows, :], k_ref[band, :], v_ref[band, :], b_ref[...])
                grads.append(vjp(do_ref[rows, :]))
                bands.append(band)
            for i, (dq, _, _, _) in enumerate(grads):
                dq_ref[i * CHUNK:(i + 1) * CHUNK, :] = dq
            for band, (_, dkb, dvb, _) in zip(bands, grads):
                dk_ref[band, :] += dkb
                dv_ref[band, :] += dvb
            db_ref[...] += functools.reduce(lambda a, b: a + b, [g[3] for g in grads])

        pl.when(c < CA_MASKED_STEPS)(lambda: run(True))
        pl.when(c >= CA_MASKED_STEPS)(lambda: run(False))

    sds = lambda *s: jax.ShapeDtypeStruct(s, F32)
    padded = lambda: pl.BlockSpec((T + CA_PAD, 128), lambda hp, c: (0, hp))
    return pl.pallas_call(
        body, name="ca_bwd", grid=(4, NCHUNK // CA_PER_STEP),
        in_specs=[pl.BlockSpec((CA_ROWS, 128), lambda hp, c: (c, hp)),
                  pl.BlockSpec((T + CA_PAD, 128), lambda hp, c: (0, hp)),
                  pl.BlockSpec((T + CA_PAD, 128), lambda hp, c: (0, 4 + hp)),
                  pl.BlockSpec((2, CHUNK, CA_BAND), lambda hp, c: (hp, 0, 0)),
                  pl.BlockSpec((CA_ROWS, 128), lambda hp, c: (c, hp))],
        out_specs=(pl.BlockSpec((CA_ROWS, 128), lambda hp, c: (c, hp)), padded(), padded(),
                   pl.BlockSpec((2, CHUNK, CA_BAND), lambda hp, c: (hp, 0, 0))),
        out_shape=(sds(T, 512), sds(T + CA_PAD, 512), sds(T + CA_PAD, 512), sds(8, CHUNK, CA_BAND)),
        compiler_params=_cp(("parallel", "arbitrary")),
    )(proj, kvpad, kvpad, bias, dcat)


def _block_diag_dot(x, w):
    return jnp.concatenate([bdot(x[:, :256], w[:256, :256], "nn"), bdot(x[:, 256:], w[256:, 256:], "nn")], axis=1)


def _lru_pre(xs, cw, cb, wa, ba, wx, bx, lam):
    xc = cb + xs[0] * cw[0:1, :] + xs[1] * cw[1:2, :] + xs[2] * cw[2:3, :] + xs[3] * cw[3:4, :]
    ra = _sigmoid(_block_diag_dot(xc, wa) + ba)
    ii = _sigmoid(_block_diag_dot(xc, wx) + bx)
    la = 8.0 * ra * _log_sigmoid(lam)
    return jnp.exp(la), jnp.sqrt(-_expm1(2.0 * la)) * (ii * xc)


def _lru_pre_specs():
    full = lambda shape: pl.BlockSpec(shape, lambda i: (0,) * len(shape))
    return [pl.BlockSpec((4, ROWS, 512), lambda i: (0, i, 0)), full((4, 512)), full((1, 512)),
            full((512, 512)), full((1, 512)), full((512, 512)), full((1, 512)), full((1, 512))]


def _lru_pre_fwd(xs, cw, cb, wa, ba, wx, bx, lam):
    def body(xs_ref, cw_ref, cb_ref, wa_ref, ba_ref, wx_ref, bx_ref, lam_ref, a_ref, b_ref):
        a, b = _lru_pre(xs_ref[...], cw_ref[...], cb_ref[...], wa_ref[...], ba_ref[...], wx_ref[...], bx_ref[...],
                        lam_ref[...])
        a_ref[...] = a
        b_ref[...] = b

    row = pl.BlockSpec((ROWS, 512), lambda i: (i, 0))
    sds = jax.ShapeDtypeStruct((T, 512), F32)
    return pl.pallas_call(body, name="lru_pre_fwd", grid=(T // ROWS,), in_specs=_lru_pre_specs(),
                          out_specs=(row, row), out_shape=(sds, sds), compiler_params=_cp(("parallel",)),
                          )(xs, cw, cb, wa, ba, wx, bx, lam)


def _lru_pre_bwd(xs, cw, cb, wa, ba, wx, bx, lam, da, db):
    def body(xs_ref, cw_ref, cb_ref, wa_ref, ba_ref, wx_ref, bx_ref, lam_ref, da_ref, db_ref,
             dxs_ref, dcw_ref, dcb_ref, dwa_ref, dba_ref, dwx_ref, dbx_ref, dlam_ref):
        acc = (dcw_ref, dcb_ref, dwa_ref, dba_ref, dwx_ref, dbx_ref, dlam_ref)

        @pl.when(pl.program_id(0) == 0)
        def _():
            for r in acc:
                r[...] = jnp.zeros_like(r)

        _, vjp = jax.vjp(_lru_pre, xs_ref[...], cw_ref[...], cb_ref[...], wa_ref[...], ba_ref[...], wx_ref[...],
                         bx_ref[...], lam_ref[...])
        grads = vjp((da_ref[...], db_ref[...]))
        dxs_ref[...] = grads[0]
        for r, g in zip(acc, grads[1:]):
            r[...] += g

    row = pl.BlockSpec((ROWS, 512), lambda i: (i, 0))
    specs = _lru_pre_specs()
    sds = lambda *s: jax.ShapeDtypeStruct(s, F32)
    return pl.pallas_call(
        body, name="lru_pre_bwd", grid=(T // ROWS,), in_specs=specs + [row, row], out_specs=tuple(specs),
        out_shape=(sds(4, T, 512), sds(4, 512), sds(1, 512), sds(512, 512), sds(1, 512), sds(512, 512), sds(1, 512),
                   sds(1, 512)),
        compiler_params=_cp(("arbitrary",)),
    )(xs, cw, cb, wa, ba, wx, bx, lam, da, db)


SCAN_ROWS = 8


def _scan8(a, b, towards_later):
    row = _iota((SCAN_ROWS, 512), 0)
    for s in (1, 2, 4):
        if towards_later:
            keep, shift = row >= s, s
        else:
            keep, shift = row < SCAN_ROWS - s, SCAN_ROWS - s
        a_s = jnp.where(keep, pltpu.roll(a, shift, 0), 1.0)
        b_s = jnp.where(keep, pltpu.roll(b, shift, 0), 0.0)
        b = a * b_s + b
        a = a * a_s
    return a, b


def _lru_scan_fwd(a, b):
    def body(a_ref, b_ref, h_ref):
        def step(i, carry):
            rows = pl.ds(pl.multiple_of(i * SCAN_ROWS, SCAN_ROWS), SCAN_ROWS)
            a8, b8 = _scan8(a_ref[rows, :], b_ref[rows, :], True)
            h = a8 * carry + b8
            h_ref[rows, :] = h
            return jnp.broadcast_to(h[SCAN_ROWS - 1:, :], (SCAN_ROWS, 512))

        lax.fori_loop(0, T // SCAN_ROWS, step, jnp.zeros((SCAN_ROWS, 512), F32), unroll=2)

    return pl.pallas_call(body, name="lru_scan_fwd", out_shape=jax.ShapeDtypeStruct((T, 512), F32),
                          compiler_params=pltpu.CompilerParams(vmem_limit_bytes=VMEM_LIMIT))(a, b)


def _lru_scan_bwd(a_next, h_prev, dh):
    def body(a_ref, h_ref, dh_ref, da_ref, db_ref):
        def step(i, carry):
            start = T - SCAN_ROWS * (i + 1)
            rows = pl.ds(pl.multiple_of(start, SCAN_ROWS), SCAN_ROWS)
            a8, b8 = _scan8(a_ref[rows, :], dh_ref[rows, :], False)
            g = a8 * carry + b8
            db_ref[rows, :] = g
            da_ref[rows, :] = g * h_ref[rows, :]
            return jnp.broadcast_to(g[:1, :], (SCAN_ROWS, 512))

        lax.fori_loop(0, T // SCAN_ROWS, step, jnp.zeros((SCAN_ROWS, 512), F32), unroll=2)

    sds = jax.ShapeDtypeStruct((T, 512), F32)
    return pl.pallas_call(body, name="lru_scan_bwd", out_shape=(sds, sds),
                          compiler_params=pltpu.CompilerParams(vmem_limit_bytes=VMEM_LIMIT))(a_next, h_prev, dh)


def _lru_post(h, gate):
    return h * _gelu_tanh(gate)


def _lru_post_fwd(h, proj, cat):
    def body(h_ref, g_ref, cat_ref, o_ref):
        o_ref[...] = _lru_post(h_ref[...], g_ref[...])

    row = pl.BlockSpec((ROWS, 512), lambda i: (i, 0))
    return pl.pallas_call(body, name="lru_post_fwd", grid=(T // ROWS,),
                          in_specs=[row, pl.BlockSpec((ROWS, 512), lambda i: (i, 3)), pl.BlockSpec(memory_space=pl.ANY)],
                          out_specs=pl.BlockSpec((ROWS, 512), lambda i: (i, 1)),
                          out_shape=jax.ShapeDtypeStruct((T, D), F32), input_output_aliases={2: 0},
                          compiler_params=_cp(("parallel",)))(h, proj, cat)


def _lru_post_bwd(h, proj, dcat):
    def body(h_ref, g_ref, do_ref, dh_ref, dg_ref):
        _, vjp = jax.vjp(_lru_post, h_ref[...], g_ref[...])
        dh, dg = vjp(do_ref[...])
        dh_ref[...] = dh
        dg_ref[...] = dg

    row = pl.BlockSpec((ROWS, 512), lambda i: (i, 0))
    sds = jax.ShapeDtypeStruct((T, 512), F32)
    return pl.pallas_call(body, name="lru_post_bwd", grid=(T // ROWS,),
                          in_specs=[row, pl.BlockSpec((ROWS, 512), lambda i: (i, 3)),
                                    pl.BlockSpec((ROWS, 512), lambda i: (i, 1))],
                          out_specs=(row, row), out_shape=(sds, sds), compiler_params=_cp(("parallel",)))(h, proj, dcat)


def _conv_dx(dxs_shift):
    def body(d_ref, o_ref):
        o_ref[...] = d_ref[0] + d_ref[1] + d_ref[2] + d_ref[3]

    row = pl.BlockSpec((ROWS, 512), lambda i: (i, 0))
    return pl.pallas_call(body, name="lru_conv_dx", grid=(T // ROWS,),
                          in_specs=[pl.BlockSpec((4, ROWS, 512), lambda i: (0, i, 0))], out_specs=row,
                          out_shape=jax.ShapeDtypeStruct((T, 512), F32), compiler_params=_cp(("parallel",)))(dxs_shift)


def _position():
    return lax.axis_index("x"), lax.axis_index("y"), lax.axis_index("c")


def _other_chips(x, y):
    return [(1 - x, y), (x, 1 - y), (1 - x, 1 - y)]


def _al(v, n):
    return v * n if isinstance(v, int) else pl.multiple_of(v * n, n)


_AG_ITEMS = [
    ((4, 32, 128), lambda o, s, h: o.at[s, pl.ds(_al(h, 16), 16), :], lambda r, h: r.at[pl.ds(_al(h, 16), 16), :]),
    ((4, 774, 1024), lambda o, s, h: o.at[s, :, pl.ds(_al(h, 512), 512)], lambda r, h: r.at[:, pl.ds(_al(h, 512), 512)]),
    ((1024, 1024), lambda o, s, h: o.at[pl.ds(_al(2 * s + h, 128), 128), :], lambda r, h: r.at[pl.ds(_al(h, 128), 128), :]),
    ((2, 1024, 4096), lambda o, s, h: o.at[h, :, pl.ds(_al(s, 1024), 1024)], lambda r, h: r.at[h]),
    ((2, 4096, 1024), lambda o, s, h: o.at[h, pl.ds(_al(s, 1024), 1024), :], lambda r, h: r.at[h]),
    ((1024, 2560), lambda o, s, h: o.at[pl.ds(_al(h, 512), 512), pl.ds(_al(s, 640), 640)],
     lambda r, h: r.at[pl.ds(_al(h, 512), 512), :]),
    ((1024, 1024), lambda o, s, h: o.at[pl.ds(_al(2 * s + h, 128), 128), :], lambda r, h: r.at[pl.ds(_al(h, 128), 128), :]),
]


_AG_GROUPS = [(0, 1, 2), (3,), (4,), (5, 6)]

_HBM = pl.BlockSpec(memory_space=pltpu.HBM)
_SEM = pl.BlockSpec(memory_space=pltpu.SEMAPHORE)
_SPLIT = dict(has_side_effects=pltpu.SideEffectType.DATAFLOW_SIDE_EFFECTING)


def _hbm(a):
    return pltpu.with_memory_space_constraint(a, pltpu.HBM)


def _ag_ici_copy(i, j, chip, c, slot, src_ref, land_ref, send_sems, recv_sems, k):
    _, dst, half = _AG_ITEMS[i]
    return pltpu.make_async_remote_copy(src_ref=half(src_ref, c), dst_ref=dst(land_ref, slot, c), send_sem=send_sems.at[k],
                                        recv_sem=recv_sems.at[k], device_id=(*chip, c), device_id_type=MESH)


def _ag_start(groups, shards, name):
    items_all = [i for g in groups for i in _AG_GROUPS[g]]
    n = len(items_all)
    ng = len(groups)
    lands = [lax.empty(_AG_ITEMS[i][0], shards[i].dtype) for i in items_all]

    def body(*refs):
        srcs, land_refs = dict(zip(items_all, refs[:n])), dict(zip(items_all, refs[n:2 * n]))
        sems = refs[2 * n:2 * n + 2 * ng]
        token = refs[-1]
        x, y, c = _position()
        me = 2 * x + y
        for gi, g in enumerate(groups):
            for t, i in enumerate(_AG_GROUPS[g]):
                for j, chip in enumerate(_other_chips(x, y)):
                    _ag_ici_copy(i, j, chip, c, me, srcs[i], land_refs[i], sems[2 * gi], sems[2 * gi + 1], 3 * t + j).start()
        token[...] = jnp.zeros_like(token)

    sem_shapes = []
    for g in groups:
        sem_shapes += [pltpu.SemaphoreType.DMA((3 * len(_AG_GROUPS[g]),))] * 2
    ops = [shards[i] for i in items_all] + lands
    out = pl.pallas_call(
        body, name=name,
        out_shape=tuple(sem_shapes) + tuple(pltpu.HBM(a.shape, a.dtype) for a in ops) + (jax.ShapeDtypeStruct((8, 128), F32),),
        in_specs=(_HBM,) * (2 * n),
        out_specs=(_SEM,) * (2 * ng) + (_HBM,) * (2 * n) + (pl.BlockSpec(memory_space=pltpu.VMEM),),
        input_output_aliases={i: 2 * ng + i for i in range(2 * n)},
        compiler_params=pltpu.CompilerParams(**_SPLIT),
    )(*[_hbm(a) for a in ops])
    sems, thru, token = out[:2 * ng], out[2 * ng:-1], out[-1]
    return ({g: (sems[2 * gi], sems[2 * gi + 1]) for gi, g in enumerate(groups)},
            dict(zip(items_all, thru[:n])), dict(zip(items_all, thru[n:])), token)


def _ag_wait(g, sems, srcs, lands, after):
    items = _AG_GROUPS[g]
    m = len(items)

    def body(*refs):
        src_refs, land_refs = refs[:m], refs[m:2 * m]
        send_sems, recv_sems = refs[2 * m], refs[2 * m + 1]
        x, y, c = _position()
        for t, i in enumerate(items):
            for j, chip in enumerate(_other_chips(x, y)):
                cp = _ag_ici_copy(i, j, chip, c, 2 * chip[0] + chip[1], src_refs[t], land_refs[t], send_sems, recv_sems,
                                  3 * t + j)
                cp.wait_send()
                cp.wait_recv()

    ops = [srcs[i] for i in items] + [lands[i] for i in items]
    out = pl.pallas_call(
        body, name=f"allgather_wait_{g}",
        out_shape=tuple(pltpu.HBM(a.shape, a.dtype) for a in ops),
        in_specs=(_HBM,) * (2 * m) + (_SEM, _SEM, pl.BlockSpec(memory_space=pl.ANY)),
        out_specs=(_HBM,) * (2 * m),
        input_output_aliases={i: i for i in range(2 * m)},
        compiler_params=pltpu.CompilerParams(**_SPLIT),
    )(*ops, sems[0], sems[1], after)
    return list(out[:m]), list(out[m:])


def _ag_forward(g, srcs, lands):
    return _ag_sibling(_AG_GROUPS[g], srcs, lands, False, f"allgather_forward_{g}")


def _ag_push_own(srcs, lands):
    items = tuple(sorted(lands))
    out = _ag_sibling(items, [srcs[i] for i in items], [lands[i] for i in items], True, "allgather_push_own")
    return dict(zip(items, out))


def _ag_sibling(items, srcs, lands, own, name):
    m = len(items)
    per = 2 if own else 3

    def body(*refs):
        src_refs, in_refs, out_refs = refs[:m], refs[m:2 * m], refs[2 * m:3 * m]
        send_sems, recv_sems = refs[3 * m:]
        x, y, c = _position()
        sibling = (x, y, 1 - c)
        me = 2 * x + y
        if own:
            mine = theirs = [(me, 0), (me, 1)]
        else:
            slots = [2 * chip[0] + chip[1] for chip in _other_chips(x, y)]
            mine, theirs = [(s, c) for s in slots], [(s, 1 - c) for s in slots]
        sends = []
        for t, i in enumerate(items):
            _, dst, half = _AG_ITEMS[i]
            for k, (slot, hc) in enumerate(mine):
                src = half(src_refs[t], hc) if own else dst(in_refs[t], slot, hc)
                sends.append(pltpu.make_async_remote_copy(
                    src_ref=src, dst_ref=dst(out_refs[t], slot, hc), send_sem=send_sems.at[per * t + k],
                    recv_sem=recv_sems.at[per * t + k], device_id=sibling, device_id_type=MESH))
        for cp in sends:
            cp.start()
        for t, i in enumerate(items):
            dst = _AG_ITEMS[i][1]
            for k, (slot, hc) in enumerate(theirs):
                there = dst(out_refs[t], slot, hc)
                pltpu.make_async_remote_copy(src_ref=there, dst_ref=there, send_sem=send_sems.at[per * t + k],
                                             recv_sem=recv_sems.at[per * t + k], device_id=sibling,
                                             device_id_type=MESH).wait_recv()
        for cp in sends:
            cp.wait_send()

    any_spec = pl.BlockSpec(memory_space=pl.ANY)
    return pl.pallas_call(
        body, name=name,
        in_specs=[any_spec] * (2 * m), out_specs=(any_spec,) * m,
        out_shape=tuple(jax.ShapeDtypeStruct(a.shape, a.dtype) for a in lands),
        input_output_aliases={m + t: t for t in range(m)},
        scratch_shapes=[pltpu.SemaphoreType.DMA((per * m,)), pltpu.SemaphoreType.DMA((per * m,))],
    )(*srcs, *lands)


def _ag_d2d_copy(i, land_ref, slot, half, send_sems, recv_sems, k):
    x, y, c = _position()
    part = _AG_ITEMS[i][1](land_ref, slot, half)
    return pltpu.make_async_remote_copy(src_ref=part, dst_ref=part, send_sem=send_sems.at[k], recv_sem=recv_sems.at[k],
                                        device_id=(x, y, 1 - c), device_id_type=MESH)


def _ag_forward_start(g, lands, carry):
    items = _AG_GROUPS[g]
    m = len(items)

    def body(*refs):
        land_refs, send_sems, recv_sems, token = refs[:m], refs[m + 1], refs[m + 2], refs[-1]
        x, y, c = _position()
        for t, i in enumerate(items):
            for k, chip in enumerate(_other_chips(x, y)):
                _ag_d2d_copy(i, land_refs[t], 2 * chip[0] + chip[1], c, send_sems, recv_sems, 3 * t + k).start()
        token[...] = jnp.zeros_like(token)

    ops = list(lands) + [carry]
    out = pl.pallas_call(
        body, name=f"allgather_forward_start_{g}",
        out_shape=(pltpu.SemaphoreType.DMA((3 * m,)), pltpu.SemaphoreType.DMA((3 * m,)))
        + tuple(pltpu.HBM(a.shape, a.dtype) for a in ops) + (jax.ShapeDtypeStruct((8, 128), F32),),
        in_specs=(_HBM,) * (m + 1),
        out_specs=(_SEM, _SEM) + (_HBM,) * (m + 1) + (pl.BlockSpec(memory_space=pltpu.VMEM),),
        input_output_aliases={t: 2 + t for t in range(m + 1)},
        compiler_params=pltpu.CompilerParams(**_SPLIT),
    )(*[_hbm(a) for a in ops])
    return (out[0], out[1], list(out[2:2 + m])), out[2 + m]


def _ag_forward_wait(g, started, after):
    items = _AG_GROUPS[g]
    m = len(items)
    send_sems, recv_sems, lands = started

    def body(*refs):
        land_refs, send_sems, recv_sems = refs[:m], refs[m], refs[m + 1]
        x, y, c = _position()
        for t, i in enumerate(items):
            for k, chip in enumerate(_other_chips(x, y)):
                slot = 2 * chip[0] + chip[1]
                _ag_d2d_copy(i, land_refs[t], slot, 1 - c, send_sems, recv_sems, 3 * t + k).wait_recv()
                _ag_d2d_copy(i, land_refs[t], slot, c, send_sems, recv_sems, 3 * t + k).wait_send()

    out = pl.pallas_call(
        body, name=f"allgather_forward_wait_{g}",
        out_shape=tuple(pltpu.HBM(a.shape, a.dtype) for a in lands),
        in_specs=(_HBM,) * m + (_SEM, _SEM, pl.BlockSpec(memory_space=pl.ANY)), out_specs=(_HBM,) * m,
        input_output_aliases={t: t for t in range(m)},
        compiler_params=pltpu.CompilerParams(**_SPLIT),
    )(*lands, send_sems, recv_sems, after)
    return list(out)


def _pair_swap_copy(g_ref, r_ref, send_sem, recv_sem):
    x, y, c = _position()
    hc = g_ref.shape[2] // 2
    return pltpu.make_async_remote_copy(src_ref=g_ref.at[:, :, pl.ds(_al(1 - c, hc), hc)], dst_ref=r_ref,
                                        send_sem=send_sem, recv_sem=recv_sem, device_id=(x, y, 1 - c),
                                        device_id_type=MESH)


def _pair_swap_start(gb, tag):
    _, rows, cols = gb.shape
    recv = lax.empty((4, rows, cols // 2), gb.dtype)

    def body(g_ref, r_ref, send_sem, recv_sem, g_thru, r_thru, token):
        _pair_swap_copy(g_ref, r_ref, send_sem, recv_sem).start()
        token[...] = jnp.zeros_like(token)

    return pl.pallas_call(
        body, name="grad_pair_swap_start_" + tag,
        out_shape=(pltpu.SemaphoreType.DMA(()), pltpu.SemaphoreType.DMA(()), pltpu.HBM(gb.shape, gb.dtype),
                   pltpu.HBM(recv.shape, recv.dtype), jax.ShapeDtypeStruct((8, 128), F32)),
        in_specs=(_HBM, _HBM), out_specs=(_SEM, _SEM, _HBM, _HBM, pl.BlockSpec(memory_space=pltpu.VMEM)),
        input_output_aliases={0: 2, 1: 3},
        compiler_params=pltpu.CompilerParams(**_SPLIT),
    )(_hbm(gb), _hbm(recv))


def _pair_swap_wait(started, after, tag):
    send_sem, recv_sem, gb, recv, _ = started

    def body(g_ref, r_ref, send_sem, recv_sem, after_ref, g_out, r_out):
        cp = _pair_swap_copy(g_ref, r_ref, send_sem, recv_sem)
        cp.wait_send()
        cp.wait_recv()

    return pl.pallas_call(
        body, name="grad_pair_swap_wait_" + tag,
        out_shape=(pltpu.HBM(gb.shape, gb.dtype), pltpu.HBM(recv.shape, recv.dtype)),
        in_specs=(_HBM, _HBM, _SEM, _SEM, pl.BlockSpec(memory_space=pl.ANY)), out_specs=(_HBM, _HBM),
        input_output_aliases={0: 0, 1: 1},
        compiler_params=pltpu.CompilerParams(**_SPLIT),
    )(gb, recv, send_sem, recv_sem, after)


def _handover_copy(r_ref, send_sem, recv_sem, core):
    x, y, c = _position()
    hc = r_ref.shape[1] // 2
    cols = r_ref.at[:, pl.ds(_al(core, hc), hc)]
    return pltpu.make_async_remote_copy(src_ref=cols, dst_ref=cols, send_sem=send_sem, recv_sem=recv_sem,
                                        device_id=(x, y, 1 - c), device_id_type=MESH)


def _handover_start(red, tag):
    def body(r_ref, send_sem, recv_sem, r_thru, token):
        _handover_copy(r_ref, send_sem, recv_sem, lax.axis_index("c")).start()
        token[...] = jnp.zeros_like(token)

    return pl.pallas_call(
        body, name="grad_handover_start_" + tag,
        out_shape=(pltpu.SemaphoreType.DMA(()), pltpu.SemaphoreType.DMA(()), pltpu.HBM(red.shape, red.dtype),
                   jax.ShapeDtypeStruct((8, 128), F32)),
        in_specs=(_HBM,), out_specs=(_SEM, _SEM, _HBM, pl.BlockSpec(memory_space=pltpu.VMEM)),
        input_output_aliases={0: 2},
        compiler_params=pltpu.CompilerParams(**_SPLIT),
    )(_hbm(red))


def _handover_wait(started, after, tag):
    send_sem, recv_sem, red, _ = started

    def body(r_ref, send_sem, recv_sem, after_ref, r_out):
        c = lax.axis_index("c")
        _handover_copy(r_ref, send_sem, recv_sem, c).wait_send()
        _handover_copy(r_ref, send_sem, recv_sem, 1 - c).wait_recv()

    return pl.pallas_call(
        body, name="grad_handover_wait_" + tag,
        out_shape=pltpu.HBM(red.shape, red.dtype),
        in_specs=(_HBM, _SEM, _SEM, pl.BlockSpec(memory_space=pl.ANY)), out_specs=_HBM,
        input_output_aliases={0: 0},
        compiler_params=pltpu.CompilerParams(**_SPLIT),
    )(red, send_sem, recv_sem, after)


def _handover(red, tag):
    started = _handover_start(red, tag)
    return _handover_wait(started, started[3], tag)


def _a2a_copy(j, chip, c, p_ref, q_ref, q_slot, send_sems, recv_sems):
    return pltpu.make_async_remote_copy(src_ref=p_ref.at[2 * chip[0] + chip[1]], dst_ref=q_ref.at[q_slot],
                                        send_sem=send_sems.at[j], recv_sem=recv_sems.at[j], device_id=(*chip, c),
                                        device_id_type=MESH)


def _a2a_start(p, tag):
    def body(p_ref, q_ref, send_sems, recv_sems, p_thru, q_thru, token):
        x, y, c = _position()
        for j, chip in enumerate(_other_chips(x, y)):
            _a2a_copy(j, chip, c, p_ref, q_ref, 2 * x + y, send_sems, recv_sems).start()
        token[...] = jnp.zeros_like(token)

    return pl.pallas_call(
        body, name="grad_alltoall_start_" + tag,
        out_shape=(pltpu.SemaphoreType.DMA((3,)), pltpu.SemaphoreType.DMA((3,)), pltpu.HBM(p.shape, p.dtype),
                   pltpu.HBM(p.shape, p.dtype), jax.ShapeDtypeStruct((8, 128), F32)),
        in_specs=(_HBM, _HBM), out_specs=(_SEM, _SEM, _HBM, _HBM, pl.BlockSpec(memory_space=pltpu.VMEM)),
        input_output_aliases={0: 2, 1: 3},
        compiler_params=pltpu.CompilerParams(**_SPLIT),
    )(_hbm(p), _hbm(lax.empty(p.shape, p.dtype)))


def _a2a_wait(send_sems, recv_sems, p, q, after, tag):
    def body(p_ref, q_ref, send_sems, recv_sems, after_ref, p_out, q_out):
        x, y, c = _position()
        for j, chip in enumerate(_other_chips(x, y)):
            cp = _a2a_copy(j, chip, c, p_ref, q_ref, 2 * chip[0] + chip[1], send_sems, recv_sems)
            cp.wait_send()
            cp.wait_recv()

    return pl.pallas_call(
        body, name="grad_alltoall_wait_" + tag,
        out_shape=(pltpu.HBM(p.shape, p.dtype), pltpu.HBM(q.shape, q.dtype)),
        in_specs=(_HBM, _HBM, _SEM, _SEM, pl.BlockSpec(memory_space=pl.ANY)), out_specs=(_HBM, _HBM),
        input_output_aliases={0: 0, 1: 1},
        compiler_params=pltpu.CompilerParams(**_SPLIT),
    )(p, q, send_sems, recv_sems, after)


def _comm_rows(rows):
    return next(t for t in (1536, 1152, 1024, 512, 384, 256, 128) if rows % t == 0)


def _pair_add(gb, recv, where, tag):
    _, rows, cols = gb.shape
    hc = cols // 2
    tr = _comm_rows(rows)

    def body(w_ref, g_ref, r_ref, o_ref):
        o_ref[...] = (g_ref[...].astype(F32) + r_ref[...].astype(F32)).astype(o_ref.dtype)

    return pl.pallas_call(
        body, name="grad_pair_add_" + tag,
        grid_spec=pltpu.PrefetchScalarGridSpec(
            num_scalar_prefetch=1, grid=(4, rows // tr),
            in_specs=[pl.BlockSpec((None, tr, hc), lambda s, j, w_ref: (s, j, w_ref[0])),
                      pl.BlockSpec((None, tr, hc), lambda s, j, w_ref: (s, j, 0))],
            out_specs=pl.BlockSpec((None, tr, hc), lambda s, j, w_ref: (s, j, 0))),
        out_shape=jax.ShapeDtypeStruct((4, rows, hc), gb.dtype),
        compiler_params=_cp(("parallel", "parallel")),
    )(where, gb, recv)


def _sum_chips(p, q, where, tag):
    _, rows, hc = q.shape
    tr = _comm_rows(rows)

    def body(w_ref, p_ref, qa_ref, qb_ref, qc_ref, o_ref):
        me = w_ref[1]
        own, qa, qb, qc = (r[...].astype(F32) for r in (p_ref, qa_ref, qb_ref, qc_ref))
        v0 = jnp.where(me == 0, own, qa)
        v1 = jnp.where(me == 1, own, jnp.where(me == 0, qa, qb))
        v2 = jnp.where(me == 2, own, jnp.where(me < 2, qb, qc))
        v3 = jnp.where(me == 3, own, qc)
        o_ref[...] = ((v0 + v1) + v2) + v3

    slot = lambda k: pl.BlockSpec((None, tr, hc), lambda j, w_ref: (w_ref[k], j, 0))
    return pl.pallas_call(
        body, name="grad_sum_chips_" + tag,
        grid_spec=pltpu.PrefetchScalarGridSpec(
            num_scalar_prefetch=1, grid=(rows // tr,),
            in_specs=[slot(1), slot(2), slot(3), slot(4)],
            out_specs=pl.BlockSpec((tr, hc), lambda j, w_ref: (j, w_ref[0]))),
        out_shape=jax.ShapeDtypeStruct((rows, 2 * hc), F32),
        compiler_params=_cp(("parallel",)),
    )(where, p, q, q, q)


def _shard_major(g, axis):
    shape = g.shape
    g = g.reshape(shape[:axis] + (4, shape[axis] // 4) + shape[axis + 1:])
    return jnp.moveaxis(g, axis, 0).reshape(4, -1)


def _unshard(g4, shape, axis):
    n = shape[axis] // 4
    g = g4.reshape((4,) + shape[:axis] + (n,) + shape[axis + 1:])
    return jnp.moveaxis(g, 0, axis).reshape(shape)


def _split(flat, shapes):
    out, off = [], 0
    for shp in shapes:
        n = 1
        for d in shp:
            n *= d
        out.append(flat[..., off:off + n].reshape(flat.shape[:-1] + tuple(shp)))
        off += n
    return out


def _even_rows_to_kernel(wt):
    return jnp.concatenate([wt[:1536], wt[1552:3088], wt[1536:1552], wt[3088:3096],
                            jnp.zeros((PE - 3096, wt.shape[1]), wt.dtype)], axis=0)


def _block_diag(w):
    eye = jnp.eye(8, dtype=w.dtype)
    return (w[:, :, None, :] * eye[:, None, :, None]).reshape(512, 512)


def _diag_blocks(g):
    eye = jnp.eye(8, dtype=g.dtype)
    return (g.reshape(8, 64, 8, 64) * eye[:, None, :, None]).sum(axis=2)


def _shift_down(a, s):
    return a if s == 0 else jnp.pad(a, ((s, 0), (0, 0)))[:a.shape[0]]


def _shift_up(a, s):
    return a if s == 0 else jnp.pad(a, ((0, s), (0, 0)))[s:]


SMALL_SHARDED_SHAPES = [(2, 4, 256), (16, 64), (4, 128), (128,), (128,), (128,), (128,)]
REPL_SHAPES = [(256,), (512,), (8,), (8, 257), (8, 64, 64), (8, 64, 64)]


def kernel(x, norm_w, w_in_even, gla_w_a_up, gla_b_a, gla_norm_w, fox_b_f, w_out_even, w_in_odd, rel_bias, conv_w, conv_b, lru_w_a, lru_b_a, lru_w_x, lru_b_x, lru_lambda, w_out_odd, w_mlp_up, w_mlp_down, loss_target, m_norm_w, m_w_in_even, m_gla_w_a_up, m_gla_b_a, m_gla_norm_w, m_fox_b_f, m_w_out_even, m_w_in_odd, m_rel_bias, m_conv_w, m_conv_b, m_lru_w_a, m_lru_b_a, m_lru_w_x, m_lru_b_x, m_lru_lambda, m_w_out_odd, m_w_mlp_up, m_w_mlp_down, v_norm_w, v_w_in_even, v_gla_w_a_up, v_gla_b_a, v_gla_norm_w, v_fox_b_f, v_w_out_even, v_w_in_odd, v_rel_bias, v_conv_w, v_conv_b, v_lru_w_a, v_lru_b_a, v_lru_w_x, v_lru_b_x, v_lru_lambda, v_w_out_odd, v_w_mlp_up, v_w_mlp_down):
    c_idx = lax.axis_index("c")

    small_local = [norm_w, gla_w_a_up[0], conv_w[0], conv_b[0], lru_b_a[0], lru_b_x[0], lru_lambda[0]]
    small_src = jnp.concatenate([a.reshape(-1) for a in small_local]).reshape(32, 128)
    first = {0: small_src, 1: w_in_even[0].T.astype(BF16), 2: w_out_even[0].astype(BF16)}
    sems0, srcs0, lands0, ag_token = _ag_start([0], first, "allgather_start_0")
    zero = ag_token[0, 0]
    later = {3: (w_mlp_up + zero).astype(BF16), 4: (w_mlp_down + zero).astype(BF16),
             5: (w_in_odd[0] + zero).astype(BF16), 6: (w_out_odd[0] + zero).astype(BF16)}
    sems1, srcs1, lands1, ag_token = _ag_start([1, 2, 3], later, "allgather_start_1")
    ag_sems, ag_srcs = {**sems0, **sems1}, {**srcs0, **srcs1}
    ag_lands = _ag_push_own(ag_srcs, {**lands0, **lands1})

    def gathered(g, after):
        srcs_g, lands_g = _ag_wait(g, ag_sems[g], ag_srcs, ag_lands, after)
        return _ag_forward(g, srcs_g, lands_g)

    small4, w_in_e4, w_out_e = gathered(0, ag_token)
    me = 2 * lax.axis_index("x") + lax.axis_index("y")
    others = [k + (k >= me).astype(jnp.int32) for k in range(3)]
    where = jnp.stack([c_idx, me] + others).astype(jnp.int32)

    w_in_e_t = _even_rows_to_kernel(w_in_e4.reshape(3096, D))
    g_small = _split(small4.reshape(4, 32 * 128), SMALL_SHARDED_SHAPES)
    nw_full = _unshard(g_small[0], (2, 4, 1024), 2)
    wa_up = _unshard(g_small[1], (16, 256), 1)
    cw = _unshard(g_small[2], (4, 512), 1)
    cb, lba, lbx, lam = [_unshard(g, (512,), 0).reshape(1, 512) for g in g_small[3:]]
    nw = lambda layer, i: nw_full[layer, i].reshape(1, D)

    wa_pad = jnp.pad(wa_up, ((0, 128 - 16), (0, 0)))
    gla_ba = gla_b_a.reshape(1, 256)
    gla_nw = gla_norm_w.reshape(1, 512)
    fox_bpad = jnp.pad(fox_b_f.reshape(1, 8), ((0, 0), (FOX_LANE0, 128 - FOX_LANE0 - 8)))
    rbp = jnp.pad(rel_bias[0], ((0, 0), (0, REL_PAD - 257)))
    wa_bd = _block_diag(lru_w_a[0])
    wx_bd = _block_diag(lru_w_x[0])

    x0 = x[0]
    tgt = loss_target[0]

    h0 = _prenorm(x0, nw(0, 0), "prenorm_l0_mix")
    proj_e = _mm(h0, w_in_e_t, "nt", tm=2048, tn=640, name="mm_in_even")
    cat0, s_prev = _gla_fwd(proj_e, wa_pad, gla_ba, gla_nw)
    cum_r = _fox_gate_fwd(proj_e, fox_bpad)
    cum_c = cum_r[:, FOX_LANE0:FOX_LANE0 + 8].T
    cat0 = _fox_fwd(proj_e, cum_c, cat0)
    fwd_up, cat0 = _ag_forward_start(1, _ag_wait(1, ag_sems[1], ag_srcs, ag_lands, cat0)[1], cat0)
    mix0 = _mm(cat0, w_out_e, "nn", tm=2048, tn=512, name="mm_out_even")
    x1, h1 = _post_pre_fwd(x0, mix0, nw(0, 1), nw(0, 2), "post_pre_l0_mix")
    w_up, = _ag_forward_wait(1, fwd_up, x1)
    fwd_dn, h1 = _ag_forward_start(2, _ag_wait(2, ag_sems[2], ag_srcs, ag_lands, w_up)[1], h1)
    a0, r0 = _mm(h1, w_up, "nn", tm=2048, tn=1024, b_layer=0, relu_pair=True, name="mm_up_l0")
    w_dn, = _ag_forward_wait(2, fwd_dn, a0)
    d0 = _mm(a0, w_dn, "nn", tm=1024, tn=512, b_layer=0, name="mm_down_l0")
    fwd_odd, d0 = _ag_forward_start(3, _ag_wait(3, ag_sems[3], ag_srcs, ag_lands, d0)[1], d0)
    x2, h2 = _post_pre_fwd(x1, d0, nw(0, 3), nw(1, 0), "post_pre_l0_mlp")

    w_in_o, w_out_o = _ag_forward_wait(3, fwd_odd, x2)
    proj_o = _mm(h2, w_in_o, "nn", tm=2048, tn=640, name="mm_in_odd")
    bias_q = _bias_build(rbp)
    bias = bias_q.transpose(1, 0, 2)
    kvpad = jnp.pad(proj_o[:, 512:1536], ((CA_PAD, 0), (0, 0)))
    cat1 = _ca_fwd(proj_o, kvpad, bias)
    x_in = proj_o[:, 2048:2560]
    xs = jnp.stack([_shift_down(x_in, 3 - j) for j in range(4)])
    lru_a, lru_b = _lru_pre_fwd(xs, cw, cb, wa_bd, lba, wx_bd, lbx, lam)
    hh = _lru_scan_fwd(lru_a, lru_b)
    cat1 = _lru_post_fwd(hh, proj_o, cat1)
    mix1 = _mm(cat1, w_out_o, "nn", tm=2048, tn=512, name="mm_out_odd")
    x3, h3 = _post_pre_fwd(x2, mix1, nw(1, 1), nw(1, 2), "post_pre_l1_mix")
    a1, r1 = _mm(h3, w_up, "nn", tm=2048, tn=1024, b_layer=1, relu_pair=True, name="mm_up_l1")
    d1 = _mm(a1, w_dn, "nn", tm=1024, tn=512, b_layer=1, name="mm_down_l1")
    g4, loss_part, dd1, dnw13 = _post_loss(x3, d1, nw(1, 3), tgt)
    loss = lax.psum(loss_part[0, 0], ("x", "y", "c"))

    def rs_begin(swap, after, tag):
        gb, recv = _pair_swap_wait(swap, after, tag)
        return _a2a_start(_pair_add(gb, recv, where, tag), tag)

    def rs_end(started, after, tag):
        send_sems, recv_sems, p, q, _ = started
        p, q = _a2a_wait(send_sems, recv_sems, p, q, after, tag)
        return _handover(_sum_chips(p, q, where, tag), tag)

    gba = lax.dynamic_update_slice(lax.empty((4, GA_ROWS, D), BF16), jnp.zeros((4, GA_UP - GA_GAP, D), BF16),
                                   (0, GA_GAP, 0))
    gba = _mm(a1, dd1, "tn", tm=512, tn=1024, into=(gba, 1024, GA_DN), name="mm_down_l1_dw")
    du1 = _mm(dd1, w_dn, "nt", tm=2048, tn=1024, b_layer=1, times2=r1, out_dtype=BF16, name="mm_down_l1_dx")
    gba = _mm(du1, h3, "tn", tm=512, tn=1024, into=(gba, 1024, GA_UP), name="mm_up_l1_dw")
    dh3 = _mm(du1, w_up, "nt", tm=1024, tn=512, b_layer=1, name="mm_up_l1_dx")
    g3, dmix1, dnw12, dnw11 = _pre_post_bwd(x3, nw(1, 2), dh3, g4, mix1, nw(1, 1), "pre_post_bwd_l1_mlp")
    gba = _mm(cat1, dmix1, "tn", tm=128, tn=1024, into=(gba, 256, GA_OUT_O), name="mm_out_odd_dw")
    dcat1 = _mm(dmix1, w_out_o, "nt", tm=2048, tn=512, name="mm_out_odd_dx")

    dq_c, dkpad, dvpad, dbias = _ca_bwd(proj_o, kvpad, bias, dcat1)
    g_rel = _bias_grad(jnp.pad(dbias.transpose(1, 0, 2), ((0, 0), (0, 0), (0, BIAS_W - CA_BAND))))[:, :257]
    dhh, dgate = _lru_post_bwd(hh, proj_o, dcat1)
    da_l, db_l = _lru_scan_bwd(_shift_up(lru_a, 1), _shift_down(hh, 1), dhh)
    dxs, g_cw, g_cb, g_wa_bd, g_lba, g_wx_bd, g_lbx, g_lam = _lru_pre_bwd(xs, cw, cb, wa_bd, lba, wx_bd, lbx, lam, da_l, db_l)
    dx_in = _conv_dx(jnp.stack([_shift_up(dxs[j], 3 - j) for j in range(4)]))
    dproj_o = jnp.concatenate([dq_c, dkpad[CA_PAD:], dvpad[CA_PAD:], dgate, dx_in], axis=1).astype(BF16)
    gba = _mm(dproj_o, h2, "tn", tm=128, tn=1024, into=(gba, 640, GA_IN_O), name="mm_in_odd_dw")
    swap_a = _pair_swap_start(gba, "a")
    dh2 = _mm(dproj_o, w_in_o, "nt", tm=1024, tn=512, name="mm_in_odd_dx")
    g2, dd0, dnw10, dnw03 = _pre_post_bwd(x2, nw(1, 0) + swap_a[4][0, 0], dh2, g3, d0, nw(0, 3), "pre_post_bwd_l1_mix")
    rs_a = rs_begin(swap_a, g2, "a")

    gbb = lax.empty((4, GB_ROWS, D), BF16)
    gbb = _mm(a0, dd0, "tn", tm=512, tn=1024, into=(gbb, 1024, GB_DN), name="mm_down_l0_dw")
    du0 = _mm(dd0, w_dn, "nt", tm=2048, tn=1024, b_layer=0, times2=r0, out_dtype=BF16, name="mm_down_l0_dx")
    gbb = _mm(du0, h1, "tn", tm=512, tn=1024, into=(gbb, 1024, GB_UP), name="mm_up_l0_dw")
    swap_b = _pair_swap_start(gbb, "b")
    dh1 = _mm(du0, w_up, "nt", tm=1024, tn=512, b_layer=0, name="mm_up_l0_dx")
    g1, dmix0, dnw02, dnw01 = _pre_post_bwd(x1, nw(0, 2) + (swap_b[4][0, 0] + rs_a[4][0, 0]), dh1, g2, mix0, nw(0, 1),
                                            "pre_post_bwd_l0_mlp")
    rs_b = rs_begin(swap_b, g1, "b")
    gbc = lax.empty((4, GC_ROWS, D), BF16)
    gbc = _mm(cat0, dmix0, "tn", tm=128, tn=1024, into=(gbc, 256, GC_OUT_E), name="mm_out_even_dw")
    dcat0 = _mm(dmix0, w_out_e, "nt", tm=2048, tn=512, name="mm_out_even_dx")

    dq_g, dk_g, dv_g, dr_g, daux_g, g_wa_pad, g_gla_ba, g_gla_nw = _gla_bwd(
        proj_e, s_prev, wa_pad, gla_ba, gla_nw + rs_b[4][0, 0], dcat0)
    dq_f, dk_f, dv_f, dccol = _fox_bwd(proj_e, cum_c, dcat0)
    dccol_t = jnp.pad(dccol.sum(axis=0).T, ((0, 0), (FOX_LANE0, 128 - FOX_LANE0 - 8)))
    daux, g_fox_bpad = _fox_gate_bwd(proj_e, fox_bpad, dccol_t, daux_g)
    dproj_e = jnp.concatenate([dq_g, dk_g, dv_g, dr_g, dq_f, dk_f, dv_f, daux], axis=1).astype(BF16)
    gt_in_e = _mm(dproj_e, h0, "tn", tm=640, tn=1024, out_dtype=BF16, name="mm_in_even_dw")
    dh0 = _mm(dproj_e, w_in_e_t, "nn", tm=1024, tn=512, name="mm_in_even_dx")
    grad_x, dnw00 = _norm_bwd(x0, nw(0, 0), dh0, g1, "prenorm_l0_mix_bwd")

    def rs_reduce(started, after, tag):
        send_sems, recv_sems, p, q, _ = started
        p, q = _a2a_wait(send_sems, recv_sems, p, q, after, tag)
        return _handover_start(_sum_chips(p, q, where, tag), tag)

    ho_a = rs_reduce(rs_a, grad_x, "a")
    ho_b = rs_reduce(rs_b, ho_a[3], "b")

    g_norm = jnp.stack([jnp.concatenate([dnw00, dnw01, dnw02, dnw03]), jnp.concatenate([dnw10, dnw11, dnw12, dnw13])])
    sharded = [(g_norm, 2), (g_wa_pad[:16], 1), (g_cw, 1), (g_cb[0], 0), (g_lba[0], 0), (g_lbx[0], 0), (g_lam[0], 0)]
    replicated = [g_gla_ba[0], g_gla_nw[0], g_fox_bpad[0, FOX_LANE0:FOX_LANE0 + 8], g_rel, _diag_blocks(g_wa_bd),
                  _diag_blocks(g_wx_bd)]
    small4 = jnp.concatenate([_shard_major(g, ax) for g, ax in sharded]
                             + [jnp.broadcast_to(g.reshape(1, -1), (4, g.size)) for g in replicated], axis=1)
    n_small = small4.shape[1]
    small_rows = GC_ROWS - GC_TAIL - 774
    small4 = jnp.pad(small4, ((0, 0), (0, small_rows * D - n_small))).reshape(4, small_rows, D)
    gt_rows = jnp.concatenate([gt_in_e[:1536], gt_in_e[3072:3088], gt_in_e[1536:3072], gt_in_e[3088:3096]], axis=0)
    tail = jnp.concatenate([gt_rows.reshape(4, 774, D), small4.astype(BF16)], axis=1)
    gbc = lax.dynamic_update_slice(gbc, tail, (0, GC_TAIL, 0))
    swap_c = _pair_swap_start(gbc, "c")
    rs_c = rs_begin(swap_c, swap_c[4], "c")

    red_a = _handover_wait(ho_a, rs_c[4], "a")
    red_b = _handover_wait(ho_b, red_a, "b")
    early = dict(
        w_mlp_up=_adamw_from(w_mlp_up, m_w_mlp_up, v_w_mlp_up, [(red_b, GB_UP, True), (red_a, GA_UP, True)], 512,
                             "adamw_w_mlp_up"),
        w_mlp_down=_adamw_from(w_mlp_down, m_w_mlp_down, v_w_mlp_down, [(red_b, GB_DN, False), (red_a, GA_DN, False)],
                               512, "adamw_w_mlp_down"),
        w_in_odd=_adamw_from(w_in_odd, m_w_in_odd, v_w_in_odd, [(red_a, GA_IN_O, True)], 256, "adamw_w_in_odd"),
        w_out_odd=_adamw_from(w_out_odd, m_w_out_odd, v_w_out_odd, [(red_a, GA_OUT_O, False)], 128, "adamw_w_out_odd"))
    red_c = rs_end(rs_c, early["w_out_odd"][3], "c")

    g_small = _split(red_c[GC_TAIL + 774:].reshape(-1)[:n_small], SMALL_SHARDED_SHAPES + REPL_SHAPES)
    g_of = dict(zip(["norm_w", "gla_w_a_up", "conv_w", "conv_b", "lru_b_a", "lru_b_x", "lru_lambda", "gla_b_a",
                     "gla_norm_w", "fox_b_f", "rel_bias", "lru_w_a", "lru_w_x"], g_small))
    g_of.update(w_in_even=red_c[GC_TAIL:GC_TAIL + 774])
    early["w_out_even"] = _adamw_from(w_out_even, m_w_out_even, v_w_out_even, [(red_c, GC_OUT_E, False)], 256,
                                      "adamw_w_out_even")

    names = ["norm_w", "w_in_even", "gla_w_a_up", "gla_b_a", "gla_norm_w", "fox_b_f", "w_out_even", "w_in_odd", "rel_bias",
             "conv_w", "conv_b", "lru_w_a", "lru_b_a", "lru_w_x", "lru_b_x", "lru_lambda", "w_out_odd", "w_mlp_up",
             "w_mlp_down"]
    w_of = dict(norm_w=norm_w, w_in_even=w_in_even, gla_w_a_up=gla_w_a_up, gla_b_a=gla_b_a, gla_norm_w=gla_norm_w,
                fox_b_f=fox_b_f, w_out_even=w_out_even, w_in_odd=w_in_odd, rel_bias=rel_bias, conv_w=conv_w, conv_b=conv_b,
                lru_w_a=lru_w_a, lru_b_a=lru_b_a, lru_w_x=lru_w_x, lru_b_x=lru_b_x, lru_lambda=lru_lambda,
                w_out_odd=w_out_odd, w_mlp_up=w_mlp_up, w_mlp_down=w_mlp_down)
    m_of = dict(norm_w=m_norm_w, w_in_even=m_w_in_even, gla_w_a_up=m_gla_w_a_up, gla_b_a=m_gla_b_a,
                gla_norm_w=m_gla_norm_w, fox_b_f=m_fox_b_f, w_out_even=m_w_out_even, w_in_odd=m_w_in_odd,
                rel_bias=m_rel_bias, conv_w=m_conv_w, conv_b=m_conv_b, lru_w_a=m_lru_w_a, lru_b_a=m_lru_b_a,
                lru_w_x=m_lru_w_x, lru_b_x=m_lru_b_x, lru_lambda=m_lru_lambda, w_out_odd=m_w_out_odd,
                w_mlp_up=m_w_mlp_up, w_mlp_down=m_w_mlp_down)
    v_of = dict(norm_w=v_norm_w, w_in_even=v_w_in_even, gla_w_a_up=v_gla_w_a_up, gla_b_a=v_gla_b_a,
                gla_norm_w=v_gla_norm_w, fox_b_f=v_fox_b_f, w_out_even=v_w_out_even, w_in_odd=v_w_in_odd,
                rel_bias=v_rel_bias, conv_w=v_conv_w, conv_b=v_conv_b, lru_w_a=v_lru_w_a, lru_b_a=v_lru_b_a,
                lru_w_x=v_lru_w_x, lru_b_x=v_lru_b_x, lru_lambda=v_lru_lambda, w_out_odd=v_w_out_odd,
                w_mlp_up=v_w_mlp_up, w_mlp_down=v_w_mlp_down)
    grads, deltas, new_ms, new_vs = [], [], [], []
    for n in names:
        w = w_of[n]
        if n in early:
            g, d, mn, vn = early[n]
            grads.append(g)
            deltas.append(d)
            new_ms.append(mn)
            new_vs.append(vn)
            continue
        if n == "w_in_even":
            to_view = lambda a: a[0].T
            from_view = lambda a: a.T[None]
        else:
            view = w.shape if w.ndim <= 3 else w.shape[-3:]
            to_view = lambda a, view=view: a.reshape(view)
            from_view = lambda a, w=w: a.reshape(w.shape)
        g = g_of[n] if n == "w_in_even" else to_view(g_of[n])
        d, mn, vn = _adamw(to_view(w), g, to_view(m_of[n]), to_view(v_of[n]), "adamw_" + n)
        grads.append(from_view(g))
        deltas.append(from_view(d))
        new_ms.append(from_view(mn))
        new_vs.append(from_view(vn))

    return (loss, grad_x.reshape(1, T, D), *grads, *deltas, *new_ms, *new_vs)
```
